```python
import math
import jax, jax.numpy as jnp
from jax import lax
import numpy as np

D_MODEL = 1024
BATCH = 8
SEQ = 4096
DEPTH = 1

GRID_W = 64
CTX_LEN = 256
N_HEADS = 8
QK_NOPE_DIM = 64
QK_ROPE_DIM = 32
V_HEAD_DIM = 64
Q_LORA_RANK = 256
KV_LORA_RANK = 128
MLA_WIDTH = N_HEADS * V_HEAD_DIM
CONV_WIDTH = D_MODEL - MLA_WIDTH
CONV_K = 3
D_FF = 4 * D_MODEL
ROPE_THETA = 10000.0
ROPE_AXIS_DIM = QK_ROPE_DIM // 2
Q_BLOCK = 128
EPS = 1e-6
MLA_IN = Q_LORA_RANK + KV_LORA_RANK + QK_ROPE_DIM
IN_COLS = MLA_IN + 3 * CONV_WIDTH
QK_DIM = QK_NOPE_DIM + QK_ROPE_DIM
ATTN_SCALE = 1.0 / math.sqrt(QK_DIM)

kernel_name = 'hybrid_mla_shortconv_dit_layer'


def rmsnorm(x):
    xf = x.astype(jnp.float32)
    y = xf * lax.rsqrt(jnp.mean(xf * xf, axis=-1, keepdims=True) + EPS)
    return y.astype(x.dtype)


def modulate(x, shift, scale):
    return rmsnorm(x) * (1 + scale) + shift


def adaln(cvec, w_mod, b_mod):
    m = jax.nn.silu(cvec) @ w_mod + b_mod
    return jnp.split(m, 6, axis=-1)


def rope_tables(rows):
    row = jnp.broadcast_to(jnp.arange(rows)[:, None], (rows, GRID_W)).reshape(-1)
    col = jnp.broadcast_to(jnp.arange(GRID_W)[None, :], (rows, GRID_W)).reshape(-1)
    freqs = ROPE_THETA ** (-jnp.arange(0, ROPE_AXIS_DIM, 2, dtype=jnp.float32) / ROPE_AXIS_DIM)
    ang = jnp.stack([row.astype(jnp.float32)[:, None] * freqs,
                     col.astype(jnp.float32)[:, None] * freqs], axis=1)
    ang = ang[:, None]
    return jnp.cos(ang), jnp.sin(ang)


def apply_rope(x, cos, sin):
    xs = x.reshape(x.shape[:-1] + (2, 2, ROPE_AXIS_DIM // 2))
    x1, x2 = xs[..., 0, :], xs[..., 1, :]
    cos = cos.astype(x.dtype)
    sin = sin.astype(x.dtype)
    out = jnp.stack([x1 * cos - x2 * sin, x2 * cos + x1 * sin], axis=-2)
    return out.reshape(x.shape)


def mla_q(z, q_g, w_uq, cos, sin):
    cq = rmsnorm(z[..., :Q_LORA_RANK]) * q_g
    q = (cq @ w_uq).reshape(z.shape[:-1] + (N_HEADS, QK_DIM))
    q_nope, q_rope = q[..., :QK_NOPE_DIM], q[..., QK_NOPE_DIM:]
    if cos is not None:
        q_rope = apply_rope(q_rope, cos, sin)
    return jnp.concatenate([q_nope, q_rope], axis=-1)


def mla_kv(z, kv_g, w_ukv, cos, sin):
    ckv = rmsnorm(z[..., Q_LORA_RANK:Q_LORA_RANK + KV_LORA_RANK]) * kv_g
    k_rope = z[..., Q_LORA_RANK + KV_LORA_RANK:MLA_IN][..., None, :]
    kv = (ckv @ w_ukv).reshape(z.shape[:-1] + (N_HEADS, QK_NOPE_DIM + V_HEAD_DIM))
    k_nope, v = kv[..., :QK_NOPE_DIM], kv[..., QK_NOPE_DIM:]
    if cos is not None:
        k_rope = apply_rope(k_rope, cos, sin)
    k_rope = jnp.broadcast_to(k_rope, k_nope.shape[:-1] + (QK_ROPE_DIM,))
    return jnp.concatenate([k_nope, k_rope], axis=-1), v


def attention_dense(q, k, v):
    s = jnp.einsum('bqhd,bkhd->bhqk', q, k).astype(jnp.float32) * ATTN_SCALE
    p = jax.nn.softmax(s, axis=-1).astype(v.dtype)
    o = jnp.einsum('bhqk,bkhd->bqhd', p, v)
    return o.reshape(o.shape[:2] + (N_HEADS * V_HEAD_DIM,))


def attention_blocked(q, k, v):
    b, s = q.shape[0], q.shape[1]
    nblk = s // Q_BLOCK
    qb = q.reshape(b, nblk, Q_BLOCK, N_HEADS, QK_DIM).swapaxes(0, 1)
    o = lax.map(lambda qq: attention_dense(qq, k, v), qb)
    return o.swapaxes(0, 1).reshape(b, s, N_HEADS * V_HEAD_DIM)


def short_conv(z, conv_w):
    gb, gc, xin = jnp.split(z[..., MLA_IN:], 3, axis=-1)
    u = gc * xin
    n = u.shape[1]
    up = jnp.pad(u, ((0, 0), (1, 1), (0, 0)))
    y = conv_w[0] * up[:, :n] + conv_w[1] * up[:, 1:n + 1] + conv_w[2] * up[:, 2:n + 2]
    return gb * y


def sq_relu_mlp(h, w1, w2):
    return jnp.square(jax.nn.relu(h @ w1)) @ w2


def _fwd_setup_inputs(seed: int = 0) -> dict:
    key = jax.random.key(seed)
    ks = jax.random.split(key, 16)
    f32 = jnp.float32
    n = lambda k, shape, s: jax.random.normal(k, shape, f32) * s
    return {
        'x': n(ks[0], (BATCH, SEQ, D_MODEL), 1.0),
        'c': n(ks[1], (BATCH, D_MODEL), 1.0),
        'ctx': n(ks[2], (BATCH, CTX_LEN, D_MODEL), 1.0),
        'c_ctx': n(ks[3], (D_MODEL,), 1.0),
        'w_mod': n(ks[4], (DEPTH, D_MODEL, 6 * D_MODEL), D_MODEL ** -0.5),
        'b_mod': n(ks[5], (DEPTH, 6 * D_MODEL), 0.02),
        'w_in': n(ks[6], (DEPTH, D_MODEL, IN_COLS), D_MODEL ** -0.5),
        'q_norm_g': 1.0 + n(ks[7], (DEPTH, Q_LORA_RANK), 0.1),
        'w_uq': n(ks[8], (DEPTH, Q_LORA_RANK, N_HEADS * QK_DIM), Q_LORA_RANK ** -0.5),
        'kv_norm_g': 1.0 + n(ks[9], (DEPTH, KV_LORA_RANK), 0.1),
        'w_ukv': n(ks[10], (DEPTH, KV_LORA_RANK, N_HEADS * (QK_NOPE_DIM + V_HEAD_DIM)), KV_LORA_RANK ** -0.5),
        'conv_w': n(ks[11], (DEPTH, CONV_K, CONV_WIDTH), CONV_K ** -0.5),
        'w_out': n(ks[12], (DEPTH, D_MODEL, D_MODEL), D_MODEL ** -0.5),
        'w_mlp1': n(ks[13], (DEPTH, D_MODEL, D_FF), D_MODEL ** -0.5),
        'w_mlp2': n(ks[14], (DEPTH, D_FF, D_MODEL), D_FF ** -0.5),
        'final_norm_g': 1.0 + n(ks[15], (D_MODEL,), 0.1),
    }


def _fwd_reference(x, c, ctx, c_ctx, w_mod, b_mod, w_in, q_norm_g, w_uq, kv_norm_g, w_ukv,
              conv_w, w_out, w_mlp1, w_mlp2, final_norm_g):
    rows = x.shape[1] // GRID_W
    cos, sin = rope_tables(rows)
    ctx_s = ctx
    for i in range(DEPTH):
        sh1, sc1, g1, sh2, sc2, g2 = [m[:, None, :] for m in adaln(c, w_mod[i], b_mod[i])]
        sh1c, sc1c, g1c, sh2c, sc2c, g2c = adaln(c_ctx, w_mod[i], b_mod[i])

        z = modulate(x, sh1, sc1) @ w_in[i]
        zc = modulate(ctx_s, sh1c, sc1c) @ w_in[i]

        q = mla_q(z, q_norm_g[i], w_uq[i], cos, sin)
        k, v = mla_kv(z, kv_norm_g[i], w_ukv[i], cos, sin)
        kc, vc = mla_kv(zc, kv_norm_g[i], w_ukv[i], None, None)
        k_all = jnp.concatenate([k, kc], axis=1)
        v_all = jnp.concatenate([v, vc], axis=1)
        attn = attention_blocked(q, k_all, v_all)
        conv = short_conv(z, conv_w[i])
        x = x + g1 * (jnp.concatenate([attn, conv], axis=-1) @ w_out[i])

        x = x + g2 * sq_relu_mlp(modulate(x, sh2, sc2), w_mlp1[i], w_mlp2[i])

        if i + 1 < DEPTH:
            qc = mla_q(zc, q_norm_g[i], w_uq[i], None, None)
            attn_c = attention_dense(qc, kc, vc)
            conv_c = short_conv(zc, conv_w[i])
            ctx_s = ctx_s + g1c * (jnp.concatenate([attn_c, conv_c], axis=-1) @ w_out[i])
            ctx_s = ctx_s + g2c * sq_relu_mlp(modulate(ctx_s, sh2c, sc2c), w_mlp1[i], w_mlp2[i])

    return rmsnorm(x) * final_norm_g


import jax as _jax
import jax.numpy as _jnp

TWIN_FORMAT = 'train_step'
FWD_PARAMS = ['x', 'c', 'ctx', 'c_ctx', 'w_mod', 'b_mod', 'w_in', 'q_norm_g', 'w_uq', 'kv_norm_g', 'w_ukv', 'conv_w', 'w_out', 'w_mlp1', 'w_mlp2', 'final_norm_g']
TWIN_WEIGHTS = ['c_ctx', 'w_mod', 'b_mod', 'w_in', 'q_norm_g', 'w_uq', 'kv_norm_g', 'w_ukv', 'conv_w', 'w_out', 'w_mlp1', 'w_mlp2', 'final_norm_g']
TWIN_DIFF_INPUT = 'x'
TWIN_INPUTS = ['x', 'c', 'ctx', 'c_ctx', 'w_mod', 'b_mod', 'w_in', 'q_norm_g', 'w_uq', 'kv_norm_g', 'w_ukv', 'conv_w', 'w_out', 'w_mlp1', 'w_mlp2', 'final_norm_g', 'loss_target', 'm_c_ctx', 'm_w_mod', 'm_b_mod', 'm_w_in', 'm_q_norm_g', 'm_w_uq', 'm_kv_norm_g', 'm_w_ukv', 'm_conv_w', 'm_w_out', 'm_w_mlp1', 'm_w_mlp2', 'm_final_norm_g', 'v_c_ctx', 'v_w_mod', 'v_b_mod', 'v_w_in', 'v_q_norm_g', 'v_w_uq', 'v_kv_norm_g', 'v_w_ukv', 'v_conv_w', 'v_w_out', 'v_w_mlp1', 'v_w_mlp2', 'v_final_norm_g']
TWIN_OUTPUTS = ['loss', 'grad_x', 'grad_c_ctx', 'grad_w_mod', 'grad_b_mod', 'grad_w_in', 'grad_q_norm_g', 'grad_w_uq', 'grad_kv_norm_g', 'grad_w_ukv', 'grad_conv_w', 'grad_w_out', 'grad_w_mlp1', 'grad_w_mlp2', 'grad_final_norm_g', 'delta_c_ctx', 'delta_w_mod', 'delta_b_mod', 'delta_w_in', 'delta_q_norm_g', 'delta_w_uq', 'delta_kv_norm_g', 'delta_w_ukv', 'delta_conv_w', 'delta_w_out', 'delta_w_mlp1', 'delta_w_mlp2', 'delta_final_norm_g', 'new_m_c_ctx', 'new_m_w_mod', 'new_m_b_mod', 'new_m_w_in', 'new_m_q_norm_g', 'new_m_w_uq', 'new_m_kv_norm_g', 'new_m_w_ukv', 'new_m_conv_w', 'new_m_w_out', 'new_m_w_mlp1', 'new_m_w_mlp2', 'new_m_final_norm_g', 'new_v_c_ctx', 'new_v_w_mod', 'new_v_b_mod', 'new_v_w_in', 'new_v_q_norm_g', 'new_v_w_uq', 'new_v_kv_norm_g', 'new_v_w_ukv', 'new_v_conv_w', 'new_v_w_out', 'new_v_w_mlp1', 'new_v_w_mlp2', 'new_v_final_norm_g']
TWIN_LEAF_KINDS = {'loss': 'loss', 'grad_x': 'grad_x', 'grad_c_ctx': 'grad_w', 'grad_w_mod': 'grad_w', 'grad_b_mod': 'grad_w', 'grad_w_in': 'grad_w', 'grad_q_norm_g': 'grad_w', 'grad_w_uq': 'grad_w', 'grad_kv_norm_g': 'grad_w', 'grad_w_ukv': 'grad_w', 'grad_conv_w': 'grad_w', 'grad_w_out': 'grad_w', 'grad_w_mlp1': 'grad_w', 'grad_w_mlp2': 'grad_w', 'grad_final_norm_g': 'grad_w', 'delta_c_ctx': 'delta_w', 'delta_w_mod': 'delta_w', 'delta_b_mod': 'delta_w', 'delta_w_in': 'delta_w', 'delta_q_norm_g': 'delta_w', 'delta_w_uq': 'delta_w', 'delta_kv_norm_g': 'delta_w', 'delta_w_ukv': 'delta_w', 'delta_conv_w': 'delta_w', 'delta_w_out': 'delta_w', 'delta_w_mlp1': 'delta_w', 'delta_w_mlp2': 'delta_w', 'delta_final_norm_g': 'delta_w', 'new_m_c_ctx': 'new_m', 'new_m_w_mod': 'new_m', 'new_m_b_mod': 'new_m', 'new_m_w_in': 'new_m', 'new_m_q_norm_g': 'new_m', 'new_m_w_uq': 'new_m', 'new_m_kv_norm_g': 'new_m', 'new_m_w_ukv': 'new_m', 'new_m_conv_w': 'new_m', 'new_m_w_out': 'new_m', 'new_m_w_mlp1': 'new_m', 'new_m_w_mlp2': 'new_m', 'new_m_final_norm_g': 'new_m', 'new_v_c_ctx': 'new_v', 'new_v_w_mod': 'new_v', 'new_v_b_mod': 'new_v', 'new_v_w_in': 'new_v', 'new_v_q_norm_g': 'new_v', 'new_v_w_uq': 'new_v', 'new_v_kv_norm_g': 'new_v', 'new_v_w_ukv': 'new_v', 'new_v_conv_w': 'new_v', 'new_v_w_out': 'new_v', 'new_v_w_mlp1': 'new_v', 'new_v_w_mlp2': 'new_v', 'new_v_final_norm_g': 'new_v'}


def _forward(args):
    return _fwd_reference(*[args[k] for k in FWD_PARAMS])


def _output_shape():
    out = _jax.eval_shape(lambda: _forward(_fwd_setup_inputs(0)))
    return out.shape, out.dtype

N_MICROBATCH = 1
ADAM_LR = 0.001
ADAM_B1 = 0.9
ADAM_B2 = 0.999
ADAM_EPS = 1e-08
ADAM_WD = 0.01
ADAM_STEP = 10
PER_EXAMPLE_BATCH_AXIS = {'x': 0, 'c': 0, 'ctx': 0, 'loss_target': 0}
SHARED_INPUTS = []
_WEIGHT_DTYPES = {'c_ctx': _jnp.float32, 'w_mod': _jnp.float32, 'b_mod': _jnp.float32, 'w_in': _jnp.float32, 'q_norm_g': _jnp.float32, 'w_uq': _jnp.float32, 'kv_norm_g': _jnp.float32, 'w_ukv': _jnp.float32, 'conv_w': _jnp.float32, 'w_out': _jnp.float32, 'w_mlp1': _jnp.float32, 'w_mlp2': _jnp.float32, 'final_norm_g': _jnp.float32}
MOMENT_SCALE = {'c_ctx': 2.381705e-02, 'w_mod': 7.099430e-01, 'b_mod': 1.315391e+00, 'w_in': 1.960220e-01, 'q_norm_g': 2.273152e-02, 'w_uq': 1.255901e-02, 'kv_norm_g': 5.075789e-01, 'w_ukv': 1.732253e-01, 'conv_w': 2.397746e-01, 'w_out': 2.007645e-01, 'w_mlp1': 2.225979e-01, 'w_mlp2': 8.147952e-01, 'final_norm_g': 3.512743e+01}


def _to_microbatches(a, axis):
    t = _jnp.moveaxis(a, axis, 0)
    t = t.reshape((N_MICROBATCH, t.shape[0] // N_MICROBATCH) + t.shape[1:])
    return _jnp.moveaxis(t, 1, axis + 1)


def setup_inputs(seed: int = 0) -> dict:
    inp = _fwd_setup_inputs(seed)
    key = _jax.random.fold_in(_jax.random.key(seed), 7919)
    shape, _ = _output_shape()
    out = dict(inp)
    out["loss_target"] = _jax.random.normal(_jax.random.fold_in(key, 0), shape, _jnp.float32)
    for i, name in enumerate(TWIN_WEIGHTS):
        w = inp[name].astype(_jnp.float32)
        if MOMENT_SCALE is None:
            s = _jnp.sqrt(_jnp.mean(_jnp.square(w)) + 1e-30)
        else:
            s = MOMENT_SCALE[name]
        km, kv = _jax.random.split(_jax.random.fold_in(key, i + 1))
        out[name] = w
        out["m_" + name] = s * _jax.random.normal(km, w.shape, _jnp.float32)
        out["v_" + name] = (s * s) * _jax.random.uniform(kv, w.shape, _jnp.float32, 0.5, 1.5)
    if N_MICROBATCH > 1:
        for name, axis in PER_EXAMPLE_BATCH_AXIS.items():
            out[name] = _to_microbatches(out[name], axis)
    return {'x': out['x'], 'c': out['c'], 'ctx': out['ctx'], 'c_ctx': out['c_ctx'], 'w_mod': out['w_mod'], 'b_mod': out['b_mod'], 'w_in': out['w_in'], 'q_norm_g': out['q_norm_g'], 'w_uq': out['w_uq'], 'kv_norm_g': out['kv_norm_g'], 'w_ukv': out['w_ukv'], 'conv_w': out['conv_w'], 'w_out': out['w_out'], 'w_mlp1': out['w_mlp1'], 'w_mlp2': out['w_mlp2'], 'final_norm_g': out['final_norm_g'], 'loss_target': out['loss_target'], 'm_c_ctx': out['m_c_ctx'], 'm_w_mod': out['m_w_mod'], 'm_b_mod': out['m_b_mod'], 'm_w_in': out['m_w_in'], 'm_q_norm_g': out['m_q_norm_g'], 'm_w_uq': out['m_w_uq'], 'm_kv_norm_g': out['m_kv_norm_g'], 'm_w_ukv': out['m_w_ukv'], 'm_conv_w': out['m_conv_w'], 'm_w_out': out['m_w_out'], 'm_w_mlp1': out['m_w_mlp1'], 'm_w_mlp2': out['m_w_mlp2'], 'm_final_norm_g': out['m_final_norm_g'], 'v_c_ctx': out['v_c_ctx'], 'v_w_mod': out['v_w_mod'], 'v_b_mod': out['v_b_mod'], 'v_w_in': out['v_w_in'], 'v_q_norm_g': out['v_q_norm_g'], 'v_w_uq': out['v_w_uq'], 'v_kv_norm_g': out['v_kv_norm_g'], 'v_w_ukv': out['v_w_ukv'], 'v_conv_w': out['v_conv_w'], 'v_w_out': out['v_w_out'], 'v_w_mlp1': out['v_w_mlp1'], 'v_w_mlp2': out['v_w_mlp2'], 'v_final_norm_g': out['v_final_norm_g']}


def _loss(weights, diff, rest, loss_target):
    with _jax.named_scope("forward"):
        args = {**rest, TWIN_DIFF_INPUT: diff, **{k: w.astype(_WEIGHT_DTYPES[k]) for k, w in weights.items()}}
        y = _forward(args)
    with _jax.named_scope("loss_head"):
        err = _jnp.square(y.astype(_jnp.float32) - loss_target)
        return 0.5 * _jnp.sum(_jnp.mean(err, axis=-1)) if err.ndim else 0.5 * err


def _adamw(w, g, m, v):
    m = ADAM_B1 * m + (1.0 - ADAM_B1) * g
    v = ADAM_B2 * v + (1.0 - ADAM_B2) * _jnp.square(g)
    m_hat = m / (1.0 - ADAM_B1 ** ADAM_STEP)
    v_hat = v / (1.0 - ADAM_B2 ** ADAM_STEP)
    delta = -ADAM_LR * (m_hat / (_jnp.sqrt(v_hat) + ADAM_EPS) + ADAM_WD * w)
    return delta, m, v


def reference(x, c, ctx, c_ctx, w_mod, b_mod, w_in, q_norm_g, w_uq, kv_norm_g, w_ukv, conv_w, w_out, w_mlp1, w_mlp2, final_norm_g, loss_target, m_c_ctx, m_w_mod, m_b_mod, m_w_in, m_q_norm_g, m_w_uq, m_kv_norm_g, m_w_ukv, m_conv_w, m_w_out, m_w_mlp1, m_w_mlp2, m_final_norm_g, v_c_ctx, v_w_mod, v_b_mod, v_w_in, v_q_norm_g, v_w_uq, v_kv_norm_g, v_w_ukv, v_conv_w, v_w_out, v_w_mlp1, v_w_mlp2, v_final_norm_g):
    given = dict(x=x, c=c, ctx=ctx, c_ctx=c_ctx, w_mod=w_mod, b_mod=b_mod, w_in=w_in, q_norm_g=q_norm_g, w_uq=w_uq, kv_norm_g=kv_norm_g, w_ukv=w_ukv, conv_w=conv_w, w_out=w_out, w_mlp1=w_mlp1, w_mlp2=w_mlp2, final_norm_g=final_norm_g, loss_target=loss_target, m_c_ctx=m_c_ctx, m_w_mod=m_w_mod, m_b_mod=m_b_mod, m_w_in=m_w_in, m_q_norm_g=m_q_norm_g, m_w_uq=m_w_uq, m_kv_norm_g=m_kv_norm_g, m_w_ukv=m_w_ukv, m_conv_w=m_conv_w, m_w_out=m_w_out, m_w_mlp1=m_w_mlp1, m_w_mlp2=m_w_mlp2, m_final_norm_g=m_final_norm_g, v_c_ctx=v_c_ctx, v_w_mod=v_w_mod, v_b_mod=v_b_mod, v_w_in=v_w_in, v_q_norm_g=v_q_norm_g, v_w_uq=v_w_uq, v_kv_norm_g=v_kv_norm_g, v_w_ukv=v_w_ukv, v_conv_w=v_conv_w, v_w_out=v_w_out, v_w_mlp1=v_w_mlp1, v_w_mlp2=v_w_mlp2, v_final_norm_g=v_final_norm_g)
    weights = {n: given[n] for n in TWIN_WEIGHTS}
    shared = {n: given[n] for n in SHARED_INPUTS}
    per_example = {n: given[n] for n in ['x', 'c', 'ctx']}
    grad_fn = _jax.value_and_grad(_loss, argnums=(0, 1))

    def one_microbatch(ex, loss_target):
        ex = dict(ex)
        diff = ex.pop(TWIN_DIFF_INPUT)
        return grad_fn(weights, diff, {**shared, **ex}, loss_target)

    if N_MICROBATCH == 1:
        loss, (grad_w, grad_x) = one_microbatch(per_example, given["loss_target"])
    else:
        def body(carry, xs):
            loss_sum, grad_sum = carry
            l_k, (gw_k, gx_k) = one_microbatch(xs[0], xs[1])
            with _jax.named_scope("update"):
                return (loss_sum + l_k, _jax.tree.map(_jnp.add, grad_sum, gw_k)), gx_k

        init = (_jnp.zeros((), _jnp.float32), _jax.tree.map(_jnp.zeros_like, weights))
        (loss, grad_w), grad_x = _jax.lax.scan(body, init, (per_example, given["loss_target"]))
    with _jax.named_scope("update"):
        delta_w, new_m, new_v = {}, {}, {}
        for n in TWIN_WEIGHTS:
            delta_w[n], new_m[n], new_v[n] = _adamw(weights[n], grad_w[n], given["m_" + n], given["v_" + n])
    return (loss, grad_x, *[grad_w[n] for n in TWIN_WEIGHTS], *[delta_w[n] for n in TWIN_WEIGHTS],
            *[new_m[n] for n in TWIN_WEIGHTS], *[new_v[n] for n in TWIN_WEIGHTS])
```

```python
import functools
import math

import jax
import jax.numpy as jnp
from jax import lax
from jax.experimental import pallas as pl
from jax.experimental.pallas import tpu as pltpu

F32 = jnp.float32
BF16 = jnp.bfloat16

D_MODEL = 1024
GRID_W = 64
N_HEADS = 8
QK_NOPE = 64
QK_ROPE = 32
V_DIM = 64
Q_RANK = 256
KV_RANK = 128
MLA_IN = Q_RANK + KV_RANK + QK_ROPE
CONV_W = 512
IN_COLS = MLA_IN + 3 * CONV_W
IN_PAD = 2048
D_FF = 4096
ROPE_THETA = 10000.0
EPS = 1e-6
ATTN_SCALE = 1.0 / math.sqrt(QK_NOPE + QK_ROPE)
N_DEV = 8
LANES = 128

ADAM_LR, ADAM_B1, ADAM_B2, ADAM_EPS, ADAM_WD, ADAM_STEP = 0.001, 0.9, 0.999, 1e-08, 0.01, 10

ROW_TILE = 256
VMEM_BIG = 60 * 1024 * 1024

PACK_ROWS = (1952, 192, 128, 1024, 4096, 4096)
PACK_OFF = tuple(sum(PACK_ROWS[:i]) for i in range(len(PACK_ROWS) + 1))
PACK_PAD = 11520
SUM_TILE = 768


def _params(sem=None, vmem=None):
    return pltpu.CompilerParams(dimension_semantics=sem, vmem_limit_bytes=vmem)


def _pick(n, prefs):
    for p in prefs:
        if n % p == 0:
            return p
    return n


def _my_index():
    return 4 * lax.axis_index("x") + 2 * lax.axis_index("y") + lax.axis_index("c")


def _all_gather(x, name, in_vmem):
    space = pltpu.VMEM if in_vmem else pl.ANY

    def body(x_ref, out_ref, send_sems, recv_sems, local_sem):
        x, y, c = lax.axis_index("x"), lax.axis_index("y"), lax.axis_index("c")
        me, sibling = (x, y, c), (x, y, 1 - c)
        chips = [(1 - x, y), (x, 1 - y), (1 - x, 1 - y)]

        def slot(px, py, pc):
            return out_ref.at[4 * px + 2 * py + pc]

        def copy(k, block, to, src=None):
            return pltpu.make_async_remote_copy(
                src_ref=slot(*block) if src is None else src, dst_ref=slot(*block),
                send_sem=send_sems.at[k], recv_sem=recv_sems.at[k],
                device_id=to, device_id_type=pl.DeviceIdType.MESH)

        mine = pltpu.make_async_copy(x_ref, slot(*me), local_sem)
        mine.start()
        first = [copy(0, me, sibling, src=x_ref)]
        first += [copy(1 + j, me, (*chip, c), src=x_ref) for j, chip in enumerate(chips)]
        for cp in first:
            cp.start()
        passed = [copy(4 + j, (*chip, c), sibling) for j, chip in enumerate(chips)]
        for j, chip in enumerate(chips):
            copy(1 + j, (*chip, c), me).wait_recv()
            passed[j].start()
        copy(0, sibling, me).wait_recv()
        for j, chip in enumerate(chips):
            copy(4 + j, (*chip, 1 - c), me).wait_recv()
        for cp in first + passed:
            cp.wait_send()
        mine.wait()

    return pl.pallas_call(
        body, name=name,
        out_shape=jax.ShapeDtypeStruct((N_DEV,) + x.shape, x.dtype),
        in_specs=[pl.BlockSpec(memory_space=space)],
        out_specs=pl.BlockSpec(memory_space=space),
        scratch_shapes=[pltpu.SemaphoreType.DMA((7,)), pltpu.SemaphoreType.DMA((7,)), pltpu.SemaphoreType.DMA],
    )(x)


def _all_to_all(x, name):
    def body(x_ref, y_ref, send_sems, recv_sems, local_sem):
        x, y, c = lax.axis_index("x"), lax.axis_index("y"), lax.axis_index("c")
        me = 4 * x + 2 * y + c
        peers = []
        for k in range(1, N_DEV):
            px = 1 - x if k & 4 else x
            py = 1 - y if k & 2 else y
            pc = 1 - c if k & 1 else c
            peers.append((px, py, pc))

        def copy(k, peer):
            pid = 4 * peer[0] + 2 * peer[1] + peer[2]
            return pltpu.make_async_remote_copy(
                src_ref=x_ref.at[pid], dst_ref=y_ref.at[me],
                send_sem=send_sems.at[k], recv_sem=recv_sems.at[k],
                device_id=peer, device_id_type=pl.DeviceIdType.MESH)

        def landing(k, peer):
            pid = 4 * peer[0] + 2 * peer[1] + peer[2]
            return pltpu.make_async_remote_copy(
                src_ref=x_ref.at[pid], dst_ref=y_ref.at[pid],
                send_sem=send_sems.at[k], recv_sem=recv_sems.at[k],
                device_id=peer, device_id_type=pl.DeviceIdType.MESH)

        mine = pltpu.make_async_copy(x_ref.at[me], y_ref.at[me], local_sem)
        mine.start()
        sends = [copy(k, peer) for k, peer in enumerate(peers)]
        for cp in sends:
            cp.start()
        for k, peer in enumerate(peers):
            landing(k, peer).wait_recv()
        for cp in sends:
            cp.wait_send()
        mine.wait()

    return pl.pallas_call(
        body, name=name,
        out_shape=jax.ShapeDtypeStruct(x.shape, x.dtype),
        in_specs=[pl.BlockSpec(memory_space=pl.ANY)],
        out_specs=pl.BlockSpec(memory_space=pl.ANY),
        scratch_shapes=[pltpu.SemaphoreType.DMA((7,)), pltpu.SemaphoreType.DMA((7,)), pltpu.SemaphoreType.DMA],
    )(x)


def _sum_slots(y, name):
    n, rows, cols = y.shape

    def body(y_ref, o_ref):
        acc = y_ref[0].astype(F32)
        for j in range(1, n):
            acc = acc + y_ref[j].astype(F32)
        o_ref[...] = acc

    return pl.pallas_call(
        body, name=name, grid=(rows // SUM_TILE,),
        out_shape=jax.ShapeDtypeStruct((rows, cols), F32),
        in_specs=[pl.BlockSpec((n, SUM_TILE, cols), lambda i: (0, i, 0))],
        out_specs=pl.BlockSpec((SUM_TILE, cols), lambda i: (i, 0)),
        compiler_params=_params(("parallel",)),
    )(y)


_DIMS = {"nn": (((1,), (0,)), ((), ())), "nt": (((1,), (1,)), ((), ())), "tn": (((0,), (0,)), ((), ()))}


def _matmul(a, b, *, mode, name, out_dtype=F32, tm=512, tn=512, tk=512, epilogue=None, extra=None):
    if mode == "nn":
        (m, k), (k2, n) = a.shape, b.shape
    elif mode == "nt":
        (m, k), (n, k2) = a.shape, b.shape
    else:
        (k, m), (k2, n) = a.shape, b.shape
    assert k == k2, (a.shape, b.shape, mode)
    tm, tn, tk = min(tm, m), min(tn, n), min(tk, k)
    assert m % tm == 0 and n % tn == 0 and k % tk == 0, (name, m, n, k, tm, tn, tk)
    nk = k // tk
    dims = _DIMS[mode]
    a_spec = (pl.BlockSpec((tk, tm), lambda i, j, kk: (kk, i)) if mode == "tn"
              else pl.BlockSpec((tm, tk), lambda i, j, kk: (i, kk)))
    b_spec = (pl.BlockSpec((tn, tk), lambda i, j, kk: (j, kk)) if mode == "nt"
              else pl.BlockSpec((tk, tn), lambda i, j, kk: (kk, j)))
    o_spec = pl.BlockSpec((tm, tn), lambda i, j, kk: (i, j))
    in_specs, args = [a_spec, b_spec], [a, b]
    if epilogue == "drelu2":
        in_specs.append(o_spec)
        args.append(extra)
    if epilogue == "relu2":
        out_shape = (jax.ShapeDtypeStruct((m, n), BF16), jax.ShapeDtypeStruct((m, n), BF16))
        out_specs = (o_spec, o_spec)
    else:
        out_shape = jax.ShapeDtypeStruct((m, n), out_dtype)
        out_specs = o_spec
    n_in = len(args)
    n_out = 2 if epilogue == "relu2" else 1

    def body(*refs):
        a_ref, b_ref = refs[0], refs[1]
        outs = refs[n_in:n_in + n_out]
        part = lax.dot_general(a_ref[...], b_ref[...], dims, preferred_element_type=F32)

        def finish(acc):
            if epilogue == "relu2":
                outs[0][...] = acc.astype(BF16)
                r = jnp.maximum(acc, 0.0)
                outs[1][...] = (r * r).astype(BF16)
            elif epilogue == "drelu2":
                u = refs[2][...].astype(F32)
                outs[0][...] = (acc * (2.0 * jnp.maximum(u, 0.0))).astype(out_dtype)
            else:
                outs[0][...] = acc.astype(out_dtype)

        if nk == 1:
            finish(part)
        else:
            acc_ref = refs[n_in + n_out]
            kk = pl.program_id(2)

            @pl.when(kk == 0)
            def _():
                acc_ref[...] = part

            @pl.when(kk > 0)
            def _():
                acc_ref[...] += part

            @pl.when(kk == nk - 1)
            def _():
                finish(acc_ref[...])

    return pl.pallas_call(
        body, name=name, grid=(m // tm, n // tn, nk),
        out_shape=out_shape, in_specs=in_specs, out_specs=out_specs,
        scratch_shapes=[pltpu.VMEM((tm, tn), F32)] if nk > 1 else [],
        compiler_params=_params(("parallel", "parallel", "arbitrary"), VMEM_BIG),
    )(*args)


def _rstd(x):
    return lax.rsqrt(jnp.mean(x * x, axis=1, keepdims=True) + EPS)


def _norm_bwd(dxn, xn, r):
    return r * (dxn - xn * jnp.mean(dxn * xn, axis=1, keepdims=True))


def _swap8(x):
    lane = lax.broadcasted_iota(jnp.int32, x.shape, 1)
    return jnp.where((lane & 15) < 8, pltpu.roll(x, LANES - 8, 1), pltpu.roll(x, 8, 1))


def _rope(x, cos, sgn, bwd):
    return x * cos + (_swap8(x * sgn) if bwd else _swap8(x) * sgn)


def _modulate_all(x, ctx, shift, scale, name):
    s, d = x.shape
    t = s + ctx.shape[0]
    ns = s // ROW_TILE
    nc = ctx.shape[0] // ROW_TILE

    def body(x_ref, c_ref, sh_ref, sc_ref, h_ref):
        def emit(v):
            h_ref[...] = (v * _rstd(v) * (1.0 + sc_ref[0]) + sh_ref[0]).astype(BF16)

        i = pl.program_id(0)

        @pl.when(i < ns)
        def _():
            emit(x_ref[...])

        @pl.when(i >= ns)
        def _():
            emit(c_ref[...])

    vec = pl.BlockSpec((1, 1, d), lambda i: (jnp.where(i < ns, 0, 1), 0, 0))
    return pl.pallas_call(
        body, name=name, grid=(ns + nc,),
        out_shape=jax.ShapeDtypeStruct((t, d), BF16),
        in_specs=[pl.BlockSpec((ROW_TILE, d), lambda i: (jnp.minimum(i, ns - 1), 0)),
                  pl.BlockSpec((ROW_TILE, d), lambda i: (jnp.maximum(i - ns, 0), 0)), vec, vec],
        out_specs=pl.BlockSpec((ROW_TILE, d), lambda i: (i, 0)),
        compiler_params=_params(("arbitrary",)),
    )(x, ctx, shift, scale)


def _resid_modulate(x, o, gate, shift, scale, name):
    s, d = x.shape

    def body(x_ref, o_ref, g_ref, sh_ref, sc_ref, x1_ref, h_ref):
        x1 = x_ref[...] + g_ref[...] * o_ref[...]
        x1_ref[...] = x1
        h_ref[...] = (x1 * _rstd(x1) * (1.0 + sc_ref[...]) + sh_ref[...]).astype(BF16)

    row = pl.BlockSpec((ROW_TILE, d), lambda i: (i, 0))
    vec = pl.BlockSpec((1, d), lambda i: (0, 0))
    return pl.pallas_call(
        body, name=name, grid=(s // ROW_TILE,),
        out_shape=(jax.ShapeDtypeStruct((s, d), F32), jax.ShapeDtypeStruct((s, d), BF16)),
        in_specs=[row, row, vec, vec, vec], out_specs=(row, row),
        compiler_params=_params(("parallel",)),
    )(x, o, gate, shift, scale)


def _final(x1, m, gate, gain, target, name):
    s, d = x1.shape
    n = s // ROW_TILE

    def body(x1_ref, m_ref, g_ref, gf_ref, t_ref, dx2_ref, dm_ref, sums_ref):
        i = pl.program_id(0)
        mm = m_ref[...]
        x2 = x1_ref[...] + g_ref[...] * mm
        r = _rstd(x2)
        xn = x2 * r
        err = xn * gf_ref[...] - t_ref[...]
        dy = err * (1.0 / d)
        dx2 = _norm_bwd(dy * gf_ref[...], xn, r)
        dx2_ref[...] = dx2
        dm_ref[...] = (dx2 * g_ref[...]).astype(BF16)

        @pl.when(i == 0)
        def _():
            sums_ref[...] = jnp.zeros_like(sums_ref)

        sums_ref[0:1, :] += jnp.sum(dy * xn, axis=0, keepdims=True)
        sums_ref[1:2, :] += jnp.sum(dx2 * mm, axis=0, keepdims=True)
        sums_ref[2:3, :] += jnp.sum(err * err, axis=0, keepdims=True)

        @pl.when(i == n - 1)
        def _():
            tot = jnp.sum(sums_ref[2:3, :], axis=1, keepdims=True) * (0.5 / d)
            sums_ref[3:4, :] = jnp.broadcast_to(tot, (1, d))

    row = pl.BlockSpec((ROW_TILE, d), lambda i: (i, 0))
    vec = pl.BlockSpec((1, d), lambda i: (0, 0))
    return pl.pallas_call(
        body, name=name, grid=(n,),
        out_shape=(jax.ShapeDtypeStruct((s, d), F32), jax.ShapeDtypeStruct((s, d), BF16),
                   jax.ShapeDtypeStruct((8, d), F32)),
        in_specs=[row, row, vec, vec, row],
        out_specs=(row, row, pl.BlockSpec((8, d), lambda i: (0, 0))),
        compiler_params=_params(("arbitrary",)),
    )(x1, m, gate, gain, target)


def _modulate_bwd(dh, row_off, xsrc, scale, name, dres=None, o=None, gate=None):
    s, d = xsrc.shape
    n = s // ROW_TILE
    has_dx, has_o = dres is not None, o is not None
    assert has_dx or not has_o

    def body(*refs):
        it = iter(refs)
        dh_ref, x_ref, sc_ref = next(it), next(it), next(it)
        dres_ref = next(it) if has_dx else None
        o_ref, g_ref = (next(it), next(it)) if has_o else (None, None)
        dx_ref = next(it) if has_dx else None
        do_ref = next(it) if has_o else None
        sums_ref = next(it)
        i = pl.program_id(0)
        x = x_ref[...]
        r = _rstd(x)
        xn = x * r
        dhv = dh_ref[...]

        @pl.when(i == 0)
        def _():
            sums_ref[...] = jnp.zeros_like(sums_ref)

        sums_ref[0:1, :] += jnp.sum(dhv * xn, axis=0, keepdims=True)
        sums_ref[1:2, :] += jnp.sum(dhv, axis=0, keepdims=True)
        if has_dx:
            dx = dres_ref[...] + _norm_bwd(dhv * (1.0 + sc_ref[...]), xn, r)
            dx_ref[...] = dx
            if has_o:
                do_ref[...] = (dx * g_ref[...]).astype(BF16)
                sums_ref[2:3, :] += jnp.sum(dx * o_ref[...], axis=0, keepdims=True)

    row = pl.BlockSpec((ROW_TILE, d), lambda i: (i, 0))
    vec = pl.BlockSpec((1, d), lambda i: (0, 0))
    in_specs = [pl.BlockSpec((ROW_TILE, d), lambda i: (i + row_off, 0)), row, vec]
    args = [dh, xsrc, scale]
    out_shape, out_specs = [], []
    if has_dx:
        in_specs.append(row)
        args.append(dres)
        out_shape.append(jax.ShapeDtypeStruct((s, d), F32))
        out_specs.append(row)
    if has_o:
        in_specs += [row, vec]
        args += [o, gate]
        out_shape.append(jax.ShapeDtypeStruct((s, d), BF16))
        out_specs.append(row)
    out_shape.append(jax.ShapeDtypeStruct((8, d), F32))
    out_specs.append(pl.BlockSpec((8, d), lambda i: (0, 0)))
    return pl.pallas_call(
        body, name=name, grid=(n,),
        out_shape=tuple(out_shape), in_specs=in_specs, out_specs=tuple(out_specs),
        compiler_params=_params(("arbitrary",)),
    )(*args)


def _qkv_prep(z, q_gain, kv_gain, kcos, ksgn, name):
    t = z.shape[0]

    def body(z_ref, qg_ref, kg_ref, c_ref, s_ref, cq_ref, kv_ref):
        zq = z_ref[:, 0:Q_RANK]
        cq_ref[...] = (zq * _rstd(zq) * qg_ref[...]).astype(BF16)
        zk = z_ref[:, Q_RANK:Q_RANK + KV_RANK]
        kv_ref[:, 0:KV_RANK] = (zk * _rstd(zk) * kg_ref[...]).astype(BF16)
        kr = z_ref[:, Q_RANK + KV_RANK:Q_RANK + KV_RANK + LANES]
        kv_ref[:, KV_RANK:KV_RANK + LANES] = _rope(kr, c_ref[...], s_ref[...], False).astype(BF16)

    tab = pl.BlockSpec((ROW_TILE, LANES), lambda i: (i, 0))
    return pl.pallas_call(
        body, name=name, grid=(t // ROW_TILE,),
        out_shape=(jax.ShapeDtypeStruct((t, Q_RANK), BF16), jax.ShapeDtypeStruct((t, KV_RANK + LANES), BF16)),
        in_specs=[pl.BlockSpec((ROW_TILE, 512), lambda i: (i, 0)),
                  pl.BlockSpec((1, Q_RANK), lambda i: (0, 0)), pl.BlockSpec((1, KV_RANK), lambda i: (0, 0)), tab, tab],
        out_specs=(pl.BlockSpec((ROW_TILE, Q_RANK), lambda i: (i, 0)),
                   pl.BlockSpec((ROW_TILE, KV_RANK + LANES), lambda i: (i, 0))),
        compiler_params=_params(("parallel",)),
    )(z, q_gain, kv_gain, kcos, ksgn)


def _qkv_prep_bwd(z, dcq, dkv, q_gain, kv_gain, kcos, ksgn, name):
    t = z.shape[0]

    def body(z_ref, dcq_ref, dkv_ref, qg_ref, kg_ref, c_ref, s_ref, dz_ref, sums_ref):
        i = pl.program_id(0)

        @pl.when(i == 0)
        def _():
            sums_ref[...] = jnp.zeros_like(sums_ref)

        zq = z_ref[:, 0:Q_RANK]
        r = _rstd(zq)
        zn = zq * r
        dc = dcq_ref[...]
        sums_ref[0:1, :] += jnp.sum(dc * zn, axis=0, keepdims=True)
        dz_ref[:, 0:Q_RANK] = _norm_bwd(dc * qg_ref[...], zn, r).astype(BF16)

        zk = z_ref[:, Q_RANK:Q_RANK + KV_RANK]
        r = _rstd(zk)
        zn = zk * r
        dc = dkv_ref[:, 0:KV_RANK]
        sums_ref[1:2, 0:KV_RANK] += jnp.sum(dc * zn, axis=0, keepdims=True)
        dz_ref[:, Q_RANK:Q_RANK + KV_RANK] = _norm_bwd(dc * kg_ref[...], zn, r).astype(BF16)

        dkr = dkv_ref[:, KV_RANK:KV_RANK + LANES]
        dz_ref[:, Q_RANK + KV_RANK:Q_RANK + KV_RANK + LANES] = _rope(dkr, c_ref[...], s_ref[...], True).astype(BF16)

    tab = pl.BlockSpec((ROW_TILE, LANES), lambda i: (i, 0))
    return pl.pallas_call(
        body, name=name, grid=(t // ROW_TILE,),
        out_shape=(jax.ShapeDtypeStruct((t, 512), BF16), jax.ShapeDtypeStruct((8, Q_RANK), F32)),
        in_specs=[pl.BlockSpec((ROW_TILE, 512), lambda i: (i, 0)),
                  pl.BlockSpec((ROW_TILE, Q_RANK), lambda i: (i, 0)),
                  pl.BlockSpec((ROW_TILE, KV_RANK + LANES), lambda i: (i, 0)),
                  pl.BlockSpec((1, Q_RANK), lambda i: (0, 0)), pl.BlockSpec((1, KV_RANK), lambda i: (0, 0)), tab, tab],
        out_specs=(pl.BlockSpec((ROW_TILE, 512), lambda i: (i, 0)), pl.BlockSpec((8, Q_RANK), lambda i: (0, 0))),
        compiler_params=_params(("arbitrary",)),
    )(z, dcq, dkv, q_gain, kv_gain, kcos, ksgn)


def _rope_heads(x, qcos, qsgn, bwd, name):
    s, w = x.shape

    def body(x_ref, c_ref, s_ref, o_ref):
        cos, sgn = c_ref[...], s_ref[...]
        for h in range(w // LANES):
            sl = slice(h * LANES, (h + 1) * LANES)
            o_ref[:, sl] = _rope(x_ref[:, sl], cos, sgn, bwd).astype(BF16)

    row = pl.BlockSpec((ROW_TILE, w), lambda i: (i, 0))
    tab = pl.BlockSpec((ROW_TILE, LANES), lambda i: (i, 0))
    return pl.pallas_call(
        body, name=name, grid=(s // ROW_TILE,),
        out_shape=jax.ShapeDtypeStruct((s, w), BF16),
        in_specs=[row, tab, tab], out_specs=row,
        compiler_params=_params(("parallel",)),
    )(x, qcos, qsgn)


def _shift_rows(u, s):
    rowi = lax.broadcasted_iota(jnp.int32, u.shape, 0)
    prev = jnp.where(rowi == 0, 0.0, pltpu.roll(u, 1, 0))
    nxt = jnp.where(rowi == s - 1, 0.0, pltpu.roll(u, s - 1, 0))
    return prev, nxt


def _conv_specs(s):
    gb = pl.BlockSpec((s, LANES), lambda j: (0, 4 + j))
    gc = pl.BlockSpec((s, LANES), lambda j: (0, 8 + j))
    xin = pl.BlockSpec((s, LANES), lambda j: (0, 12 + j))
    cw = pl.BlockSpec((3, LANES), lambda j: (0, j))
    col = pl.BlockSpec((s, LANES), lambda j: (0, j))
    return gb, gc, xin, cw, col


def _conv_fwd(z, cw, s, name):
    def body(gb_ref, gc_ref, x_ref, w_ref, o_ref):
        u = gc_ref[...] * x_ref[...]
        prev, nxt = _shift_rows(u, s)
        y = w_ref[0:1, :] * prev + w_ref[1:2, :] * u + w_ref[2:3, :] * nxt
        o_ref[...] = (gb_ref[...] * y).astype(BF16)

    gb, gc, xin, cws, col = _conv_specs(s)
    return pl.pallas_call(
        body, name=name, grid=(CONV_W // LANES,),
        out_shape=jax.ShapeDtypeStruct((s, CONV_W), BF16),
        in_specs=[gb, gc, xin, cws], out_specs=col,
        compiler_params=_params(("parallel",), VMEM_BIG),
    )(z, z, z, cw)


def _conv_bwd(z, cw, da, s, name):
    def body(gb_ref, gc_ref, x_ref, w_ref, da_ref, dgb_ref, dgc_ref, dx_ref, dw_ref):
        gc, xv = gc_ref[...], x_ref[...]
        u = gc * xv
        prev, nxt = _shift_rows(u, s)
        dcv = da_ref[...]
        dgb_ref[...] = (dcv * (w_ref[0:1, :] * prev + w_ref[1:2, :] * u + w_ref[2:3, :] * nxt)).astype(BF16)
        dy = dcv * gb_ref[...]
        dw_ref[0:1, :] = jnp.sum(dy * prev, axis=0, keepdims=True)
        dw_ref[1:2, :] = jnp.sum(dy * u, axis=0, keepdims=True)
        dw_ref[2:3, :] = jnp.sum(dy * nxt, axis=0, keepdims=True)
        dyp, dyn = _shift_rows(dy, s)
        du = w_ref[0:1, :] * dyn + w_ref[1:2, :] * dy + w_ref[2:3, :] * dyp
        dgc_ref[...] = (du * xv).astype(BF16)
        dx_ref[...] = (du * gc).astype(BF16)

    gb, gc, xin, cws, col = _conv_specs(s)
    act = jax.ShapeDtypeStruct((s, CONV_W), BF16)
    return pl.pallas_call(
        body, name=name, grid=(CONV_W // LANES,),
        out_shape=(act, act, act, jax.ShapeDtypeStruct((3, CONV_W), F32)),
        in_specs=[gb, gc, xin, cws, gb], out_specs=(col, col, col, cws),
        compiler_params=_params(("parallel",), VMEM_BIG),
    )(z, z, z, cw, da)


ATT_TQ = 256
NT_DIMS = _DIMS["nt"]
TN_DIMS = _DIMS["tn"]


def _head_mask(shape, hh):
    lane = lax.broadcasted_iota(jnp.int32, shape, 1)
    return (lane >= hh * V_DIM) & (lane < (hh + 1) * V_DIM)


def _attn_fwd(qf, kv, s, name):
    t = kv.shape[0]

    def body(q_ref, k_ref, v_ref, o_ref):
        v = v_ref[...]
        acc = jnp.zeros((ATT_TQ, LANES), F32)
        for hh in range(2):
            sl = slice(hh * LANES, (hh + 1) * LANES)
            sc = lax.dot_general(q_ref[:, sl], k_ref[:, sl], NT_DIMS, preferred_element_type=F32) * ATTN_SCALE
            e = jnp.exp(sc - jnp.max(sc, axis=1, keepdims=True))
            p = (e * (1.0 / jnp.sum(e, axis=1, keepdims=True))).astype(BF16)
            vm = jnp.where(_head_mask(v.shape, hh), v, jnp.zeros_like(v))
            acc = acc + jnp.dot(p, vm, preferred_element_type=F32)
        o_ref[...] = acc

    return pl.pallas_call(
        body, name=name, grid=(N_HEADS // 2, s // ATT_TQ),
        out_shape=jax.ShapeDtypeStruct((s, N_HEADS * V_DIM), F32),
        in_specs=[pl.BlockSpec((ATT_TQ, 2 * LANES), lambda p, i: (i, p)),
                  pl.BlockSpec((t, 2 * LANES), lambda p, i: (0, p)),
                  pl.BlockSpec((t, LANES), lambda p, i: (0, N_HEADS + p))],
        out_specs=pl.BlockSpec((ATT_TQ, LANES), lambda p, i: (i, p)),
        compiler_params=_params(("parallel", "parallel"), VMEM_BIG),
    )(qf, kv, kv)


def _attn_bwd(qf, kv, o, da, s, name):
    t = kv.shape[0]
    nq = s // ATT_TQ

    def body(q_ref, k_ref, v_ref, o_ref, do_ref, dq_ref, dk_ref, dv_ref):
        i = pl.program_id(1)

        @pl.when(i == 0)
        def _():
            dk_ref[...] = jnp.zeros_like(dk_ref)
            dv_ref[...] = jnp.zeros_like(dv_ref)

        v = v_ref[...]
        do = do_ref[...]
        od = do * o_ref[...]
        ones = jnp.ones((8, LANES), F32)
        for hh in range(2):
            sl = slice(hh * LANES, (hh + 1) * LANES)
            q, k = q_ref[:, sl], k_ref[:, sl]
            mask = _head_mask(do.shape, hh)
            st = lax.dot_general(k, q, NT_DIMS, preferred_element_type=F32) * ATTN_SCALE
            e = jnp.exp(st - jnp.max(st, axis=0, keepdims=True))
            pt = e * (1.0 / jnp.sum(e, axis=0, keepdims=True))
            dom = jnp.where(mask, do, 0.0).astype(BF16)
            dpt = lax.dot_general(v, dom, NT_DIMS, preferred_element_type=F32)
            delta = lax.dot_general(ones, jnp.where(mask, od, 0.0), NT_DIMS, preferred_element_type=F32,
                                    precision=lax.Precision.HIGHEST)[0:1, :]
            dst = (pt * (dpt - delta) * ATTN_SCALE).astype(BF16)
            dv_ref[...] += jnp.dot(pt.astype(BF16), dom, preferred_element_type=F32)
            dk_ref[:, sl] += jnp.dot(dst, q, preferred_element_type=F32)
            dq_ref[:, sl] = lax.dot_general(dst, k, TN_DIMS, preferred_element_type=F32)

    return pl.pallas_call(
        body, name=name, grid=(N_HEADS // 2, nq),
        out_shape=(jax.ShapeDtypeStruct((s, N_HEADS * LANES), F32),
                   jax.ShapeDtypeStruct((t, N_HEADS * LANES), F32),
                   jax.ShapeDtypeStruct((t, N_HEADS * V_DIM), F32)),
        in_specs=[pl.BlockSpec((ATT_TQ, 2 * LANES), lambda p, i: (i, p)),
                  pl.BlockSpec((t, 2 * LANES), lambda p, i: (0, p)),
                  pl.BlockSpec((t, LANES), lambda p, i: (0, N_HEADS + p)),
                  pl.BlockSpec((ATT_TQ, LANES), lambda p, i: (i, p)),
                  pl.BlockSpec((ATT_TQ, LANES), lambda p, i: (i, p))],
        out_specs=(pl.BlockSpec((ATT_TQ, 2 * LANES), lambda p, i: (i, p)),
                   pl.BlockSpec((t, 2 * LANES), lambda p, i: (0, p)),
                   pl.BlockSpec((t, LANES), lambda p, i: (0, p))),
        compiler_params=_params(("parallel", "arbitrary"), VMEM_BIG),
    )(qf, kv, kv, o, da)


def _silu(x):
    return x * (1.0 / (1.0 + jnp.exp(-x)))


def _adaln_fwd(a, w, b, name):
    def body(a_ref, w_ref, b_ref, o_ref):
        o_ref[...] = jnp.dot(_silu(a_ref[...]), w_ref[...], preferred_element_type=F32,
                             precision=lax.Precision.HIGHEST) + b_ref[...]

    return pl.pallas_call(
        body, name=name, out_shape=jax.ShapeDtypeStruct((a.shape[0], w.shape[1]), F32),
        compiler_params=_params(None, VMEM_BIG),
    )(a, w, b)


def _adaln_bwd(a_t, w, d_ex, d_ctx, d_all, name):
    def body(at_ref, w_ref, dex_ref, dctx_ref, dall_ref, gw_ref, dsil_ref, dsum_ref):
        sil_t = _silu(at_ref[...])
        dctx = dctx_ref[...]
        row = dctx[0:1, :]
        for j in range(1, N_DEV):
            row = row + dctx[j:j + 1, :]
        rowi = lax.broadcasted_iota(jnp.int32, dctx.shape, 0)
        ctx_rows = jnp.where(rowi == 0, jnp.broadcast_to(row, dctx.shape), 0.0)
        hi = lax.Precision.HIGHEST
        d_rows = jnp.concatenate([dex_ref[...], ctx_rows], axis=0)
        gw_ref[...] = jnp.dot(sil_t, d_rows, preferred_element_type=F32, precision=hi)
        dsil_ref[...] = lax.dot_general(ctx_rows, w_ref[...], NT_DIMS, preferred_element_type=F32, precision=hi)
        tot = dall_ref[0]
        for j in range(1, N_DEV):
            tot = tot + dall_ref[j]
        dsum_ref[...] = tot

    return pl.pallas_call(
        body, name=name,
        out_shape=(jax.ShapeDtypeStruct(w.shape, F32), jax.ShapeDtypeStruct((8, w.shape[0]), F32),
                   jax.ShapeDtypeStruct(d_all.shape[1:], F32)),
        compiler_params=_params(None, VMEM_BIG),
    )(a_t, w, d_ex, d_ctx, d_all)


def _cctx_grad(parts, c_ctx, name):
    def body(p_ref, c_ref, o_ref):
        tot = p_ref[0]
        for j in range(1, N_DEV):
            tot = tot + p_ref[j]
        cv = c_ref[...]
        sg = 1.0 / (1.0 + jnp.exp(-cv))
        o_ref[...] = tot[0:1, :] * (sg * (1.0 + cv * (1.0 - sg)))

    return pl.pallas_call(body, name=name, out_shape=jax.ShapeDtypeStruct(c_ctx.shape, F32))(parts, c_ctx)


def _adamw(w, g, m, v, name):
    rows, cols = w.shape
    tr = _pick(rows, (256, 128, 64, 32, 16, 8))

    def body(w_ref, g_ref, m_ref, v_ref, d_ref, nm_ref, nv_ref):
        gv = g_ref[...]
        nm = ADAM_B1 * m_ref[...] + (1.0 - ADAM_B1) * gv
        nv = ADAM_B2 * v_ref[...] + (1.0 - ADAM_B2) * (gv * gv)
        m_hat = nm / (1.0 - ADAM_B1 ** ADAM_STEP)
        v_hat = nv / (1.0 - ADAM_B2 ** ADAM_STEP)
        d_ref[...] = -ADAM_LR * (m_hat / (jnp.sqrt(v_hat) + ADAM_EPS) + ADAM_WD * w_ref[...])
        nm_ref[...] = nm
        nv_ref[...] = nv

    blk = pl.BlockSpec((tr, cols), lambda i: (i, 0))
    sh = jax.ShapeDtypeStruct((rows, cols), F32)
    return pl.pallas_call(
        body, name=name, grid=(rows // tr,), out_shape=(sh, sh, sh),
        in_specs=[blk, blk, blk, blk], out_specs=(blk, blk, blk),
        compiler_params=_params(("parallel",)),
    )(w, g, m, v)


def _rope_tables(s, l):
    tok = jnp.arange(s)
    row = (tok // GRID_W).astype(F32)
    col = (tok % GRID_W).astype(F32)
    freqs = ROPE_THETA ** (-jnp.arange(0, QK_ROPE // 2, 2, dtype=F32) / (QK_ROPE // 2))
    dd = jnp.arange(QK_ROPE)
    pos = jnp.where((dd // 16)[None, :] == 0, row[:, None], col[:, None])
    ang = pos * freqs[dd % 8][None, :]
    cos = jnp.cos(ang)
    sin = jnp.sin(ang)
    sgn = jnp.where(((dd % 16) // 8)[None, :] == 0, -sin, sin)
    kcos = jnp.ones((s + l, LANES), F32).at[:s, 0:QK_ROPE].set(cos)
    ksgn = jnp.zeros((s + l, LANES), F32).at[:s, 0:QK_ROPE].set(sgn)
    qcos = jnp.ones((s, LANES), F32).at[:, QK_NOPE:QK_NOPE + QK_ROPE].set(cos)
    qsgn = jnp.zeros((s, LANES), F32).at[:, QK_NOPE:QK_NOPE + QK_ROPE].set(sgn)
    return kcos, ksgn, qcos, qsgn


def _cols_to_slots(w):
    r, c8 = w.shape
    c = c8 // N_DEV
    return w.reshape(r, N_DEV, c).transpose(1, 0, 2).reshape(N_DEV, r * c // LANES, LANES)


def _slots_to_cols(g, r):
    c = g.shape[1] * LANES // r
    return g.reshape(N_DEV, r, c).transpose(1, 0, 2).reshape(r, N_DEV * c)


def _pack_shard(w_in, w_uq, w_ukv, w_out, w_mlp1, w_mlp2):
    parts = [w.astype(BF16).reshape(-1, LANES) for w in (w_in, w_uq, w_ukv, w_out, w_mlp1, w_mlp2)]
    parts.append(jnp.zeros((PACK_PAD - PACK_OFF[-1], LANES), BF16))
    return jnp.concatenate(parts, axis=0)


def _unpack_weights(g):
    seg = [g[:, PACK_OFF[i]:PACK_OFF[i + 1]] for i in range(6)]
    w_in = _slots_to_cols(seg[0], D_MODEL)
    w_in = jnp.concatenate([w_in[:, :MLA_IN], jnp.zeros((D_MODEL, IN_PAD - IN_COLS), BF16), w_in[:, MLA_IN:]], axis=1)
    w_uq = _slots_to_cols(seg[1], Q_RANK).reshape(Q_RANK, N_HEADS, QK_NOPE + QK_ROPE)
    wq = jnp.pad(w_uq, ((0, 0), (0, 0), (0, LANES - QK_NOPE - QK_ROPE))).reshape(Q_RANK, N_HEADS * LANES)
    w_ukv = _slots_to_cols(seg[2], KV_RANK).reshape(KV_RANK, N_HEADS, QK_NOPE + V_DIM)
    k_top = jnp.pad(w_ukv[:, :, :QK_NOPE], ((0, 0), (0, 0), (0, LANES - QK_NOPE))).reshape(KV_RANK, N_HEADS * LANES)
    v_top = w_ukv[:, :, QK_NOPE:].reshape(KV_RANK, N_HEADS * V_DIM)
    eye = jnp.pad(jnp.eye(QK_ROPE, dtype=BF16), ((0, LANES - QK_ROPE), (QK_NOPE, LANES - QK_NOPE - QK_ROPE)))
    wk = jnp.concatenate([
        jnp.concatenate([k_top, v_top], axis=1),
        jnp.concatenate([jnp.tile(eye, (1, N_HEADS)), jnp.zeros((LANES, N_HEADS * V_DIM), BF16)], axis=1)], axis=0)
    w_out = seg[3].reshape(D_MODEL, D_MODEL)
    w1 = _slots_to_cols(seg[4], D_MODEL)
    w2 = seg[5].reshape(D_FF, D_MODEL)
    return w_in, wq, wk, w_out, w1, w2


def _pack_grads(d_in, d_wq, d_wk, d_out, d_w1, d_w2):
    g_in = jnp.concatenate([d_in[:, :MLA_IN], d_in[:, MLA_IN + IN_PAD - IN_COLS:]], axis=1)
    g_uq = d_wq.reshape(Q_RANK, N_HEADS, LANES)[:, :, :QK_NOPE + QK_ROPE].reshape(Q_RANK, -1)
    g_kn = d_wk[:KV_RANK, :N_HEADS * LANES].reshape(KV_RANK, N_HEADS, LANES)[:, :, :QK_NOPE]
    g_v = d_wk[:KV_RANK, N_HEADS * LANES:].reshape(KV_RANK, N_HEADS, V_DIM)
    g_ukv = jnp.concatenate([g_kn, g_v], axis=2).reshape(KV_RANK, -1)
    parts = [_cols_to_slots(g_in), _cols_to_slots(g_uq), _cols_to_slots(g_ukv),
             d_out.reshape(N_DEV, -1, LANES), _cols_to_slots(d_w1), d_w2.reshape(N_DEV, -1, LANES)]
    parts.append(jnp.zeros((N_DEV, PACK_PAD - PACK_OFF[-1], LANES), F32))
    return jnp.concatenate(parts, axis=1).astype(BF16)


def kernel(x, c, ctx, c_ctx, w_mod, b_mod, w_in, q_norm_g, w_uq, kv_norm_g, w_ukv, conv_w, w_out, w_mlp1, w_mlp2, final_norm_g, loss_target, m_c_ctx, m_w_mod, m_b_mod, m_w_in, m_q_norm_g, m_w_uq, m_kv_norm_g, m_w_ukv, m_conv_w, m_w_out, m_w_mlp1, m_w_mlp2, m_final_norm_g, v_c_ctx, v_w_mod, v_b_mod, v_w_in, v_q_norm_g, v_w_uq, v_kv_norm_g, v_w_ukv, v_conv_w, v_w_out, v_w_mlp1, v_w_mlp2, v_final_norm_g):
    me = _my_index()
    x2d, ctx2d, tgt = x[0], ctx[0], loss_target[0]
    s, l = x2d.shape[0], ctx2d.shape[0]
    t = s + l
    d = D_MODEL
    mod_cols = w_mod.shape[2]

    c_all = _all_gather(jnp.pad(c, ((0, 7), (0, 0))), "gather_c", True)[:, 0, :]
    a_rows = jnp.concatenate([c_all, c_ctx[None, :], jnp.zeros((7, d), F32)], axis=0)
    b_cols = lax.dynamic_slice(b_mod, (0, me * mod_cols), (1, mod_cols))
    mod_cols_all = _adaln_fwd(a_rows, w_mod[0], b_cols, "adaln_fwd")
    cw_blk = jnp.pad(conv_w[0], ((0, 5), (0, mod_cols - conv_w.shape[2])))
    gathered = _all_gather(jnp.concatenate([mod_cols_all, cw_blk], axis=0), "gather_mod", True)
    mod_all = gathered[:, :16, :]
    mod_mine = lax.dynamic_index_in_dim(mod_all, me, axis=1, keepdims=False).reshape(1, 6 * d)
    mod_ctx = mod_all[:, 8, :].reshape(1, 6 * d)
    cw_full = gathered[:, 16:19, :conv_w.shape[2]].transpose(1, 0, 2).reshape(3, CONV_W)
    sh1, sc1, g1, sh2, sc2, g2 = [mod_mine[:, i * d:(i + 1) * d] for i in range(6)]
    sh1c, sc1c = mod_ctx[:, 0:d], mod_ctx[:, d:2 * d]

    packed = _pack_shard(w_in[0], w_uq[0], w_ukv[0], w_out[0], w_mlp1[0], w_mlp2[0])
    win, wq, wk, wo, w1, w2 = _unpack_weights(_all_gather(packed, "gather_weights", False))
    kcos, ksgn, qcos, qsgn = _rope_tables(s, l)

    h_all = _modulate_all(x2d, ctx2d, jnp.stack([sh1, sh1c]), jnp.stack([sc1, sc1c]), "modulate1")
    tk_t = _pick(t, (2176, 768, 512, 256))
    z = _matmul(h_all, win, mode="nn", name="in_proj", tm=256, tn=2048, tk=1024)
    cq, kv_in = _qkv_prep(z, q_norm_g, kv_norm_g, kcos, ksgn, "qkv_prep")
    cq = cq[:s]
    q_raw = _matmul(cq, wq, mode="nn", name="q_up", tm=512, tn=1024, tk=256)
    qf = _rope_heads(q_raw, qcos, qsgn, False, "rope_q")
    kv = _matmul(kv_in, wk, mode="nn", name="kv_up", out_dtype=BF16, tm=256, tn=1536, tk=256)
    attn = _attn_fwd(qf, kv, s, "attn_fwd")
    conv = _conv_fwd(z, cw_full, s, "conv_fwd")
    a_cat = jnp.concatenate([attn.astype(BF16), conv], axis=1)
    o = _matmul(a_cat, wo, mode="nn", name="out_proj", tm=512, tn=1024, tk=1024)
    x1, h2 = _resid_modulate(x2d, o, g1, sh2, sc2, "resid_modulate2")
    u1, act = _matmul(h2, w1, mode="nn", name="mlp_up", tm=512, tn=1024, tk=1024, epilogue="relu2")
    mlp = _matmul(act, w2, mode="nn", name="mlp_down", tm=512, tn=1024, tk=2048)
    dx2, dm, fsums = _final(x1, mlp, g2, final_norm_g[None, :], tgt, "final_loss")
    loss = lax.psum(fsums[3, 0], ("x", "y", "c"))

    d_w2 = _matmul(act, dm, mode="tn", name="d_w_mlp2", tm=1024, tn=1024, tk=1024)
    du1 = _matmul(dm, w2, mode="nt", name="d_act", out_dtype=BF16, tm=512, tn=1024, tk=1024,
                  epilogue="drelu2", extra=u1)
    d_w1 = _matmul(h2, du1, mode="tn", name="d_w_mlp1", tm=1024, tn=1024, tk=1024)
    dh2 = _matmul(du1, w1, mode="nt", name="d_h2", tm=512, tn=1024, tk=2048)
    dx1, do, sums2 = _modulate_bwd(dh2, 0, x1, sc2, "modulate2_bwd", dres=dx2, o=o, gate=g1)
    d_wo = _matmul(a_cat, do, mode="tn", name="d_w_out", tm=1024, tn=1024, tk=1024)
    da = _matmul(do, wo, mode="nt", name="d_a", tm=512, tn=1024, tk=1024)
    dgb, dgc, dxc, d_cw = _conv_bwd(z, cw_full, da, s, "conv_bwd")
    dqf, dk, dv = _attn_bwd(qf, kv, attn, da, s, "attn_bwd")
    dq = _rope_heads(dqf, qcos, qsgn, True, "rope_q_bwd")
    d_wq = _matmul(cq, dq, mode="tn", name="d_w_uq", tm=256, tn=1024, tk=2048)
    dcq = _matmul(dq, wq, mode="nt", name="d_cq", tm=512, tn=256, tk=1024)
    dcq = jnp.pad(dcq, ((0, l), (0, 0)))
    dkv = jnp.concatenate([dk.astype(BF16), dv.astype(BF16)], axis=1)
    d_wk = _matmul(kv_in, dkv, mode="tn", name="d_w_ukv", tm=256, tn=1536, tk=tk_t)
    dkv_in = _matmul(dkv, wk, mode="nt", name="d_kv_in", tm=256, tn=256, tk=1536)
    dz_head, psums = _qkv_prep_bwd(z, dcq, dkv_in, q_norm_g, kv_norm_g, kcos, ksgn, "qkv_prep_bwd")
    dz_conv = jnp.pad(jnp.concatenate([dgb, dgc, dxc], axis=1), ((0, l), (0, 0)))
    dz = jnp.concatenate([dz_head, dz_conv], axis=1)
    d_win = _matmul(h_all, dz, mode="tn", name="d_w_in", tm=1024, tn=1024, tk=tk_t)
    dh = _matmul(dz, win, mode="nt", name="d_h1", tm=256, tn=1024, tk=2048)
    grad_x, sums1 = _modulate_bwd(dh, 0, x2d, sc1, "modulate1_bwd", dres=dx1)
    (sums1c,) = _modulate_bwd(dh, s // ROW_TILE, ctx2d, sc1c, "modulate1_ctx_bwd")

    zero = jnp.zeros((1, d), F32)
    d_mod_mine = jnp.concatenate([sums1[1:2], sums1[0:1], sums2[2:3], sums2[1:2], sums2[0:1], fsums[1:2]], axis=1)
    d_mod_ctx = jnp.concatenate([sums1c[1:2], sums1c[0:1], zero, zero, zero, zero], axis=1)
    misc = jnp.concatenate([psums[0:1, :], psums[1:2, :KV_RANK], fsums[0:1]], axis=1)
    misc = jnp.pad(misc, ((0, 0), (0, 6 * d - misc.shape[1])))
    cw_rows = jnp.pad(d_cw, ((0, 0), (0, 6 * d - CONV_W)))
    small = jnp.concatenate([d_mod_mine, d_mod_ctx, misc, cw_rows, jnp.zeros((2, 6 * d), F32)], axis=0)
    d_all = _all_gather(small, "gather_small_grads", True)
    d_cols = lax.dynamic_slice_in_dim(d_all, me * mod_cols, mod_cols, axis=2)
    g_w_mod, dsil, dsum = _adaln_bwd(a_rows.T, w_mod[0], d_cols[:, 0, :], d_cols[:, 1, :], d_all, "adaln_bwd")
    g_c_ctx = _cctx_grad(_all_gather(dsil, "gather_d_cctx", True), c_ctx[None, :], "cctx_grad")
    g_b_mod = dsum[0:1] + dsum[1:2]
    g_qg = dsum[2:3, 0:Q_RANK]
    g_kvg = dsum[2:3, Q_RANK:Q_RANK + KV_RANK]
    g_fg = dsum[2:3, Q_RANK + KV_RANK:Q_RANK + KV_RANK + d]
    g_cw = lax.dynamic_slice_in_dim(dsum[3:6, :CONV_W], me * conv_w.shape[2], conv_w.shape[2], axis=1)

    slots = _all_to_all(_pack_grads(d_win, d_wq, d_wk, d_wo, d_w1, d_w2), "exchange_grads")
    gsum = _sum_slots(slots, "sum_grads")
    shapes = [w_in.shape[1:], w_uq.shape[1:], w_ukv.shape[1:], w_out.shape[1:], w_mlp1.shape[1:], w_mlp2.shape[1:]]
    g_in, g_uq, g_ukv, g_out, g_m1, g_m2 = [gsum[PACK_OFF[i]:PACK_OFF[i + 1]].reshape(shapes[i]) for i in range(6)]

    grads = {"c_ctx": g_c_ctx[0], "w_mod": g_w_mod[None], "b_mod": g_b_mod, "w_in": g_in[None],
             "q_norm_g": g_qg, "w_uq": g_uq[None], "kv_norm_g": g_kvg, "w_ukv": g_ukv[None],
             "conv_w": g_cw[None], "w_out": g_out[None], "w_mlp1": g_m1[None], "w_mlp2": g_m2[None],
             "final_norm_g": g_fg[0]}
    weights = {"c_ctx": c_ctx, "w_mod": w_mod, "b_mod": b_mod, "w_in": w_in, "q_norm_g": q_norm_g, "w_uq": w_uq,
               "kv_norm_g": kv_norm_g, "w_ukv": w_ukv, "conv_w": conv_w, "w_out": w_out, "w_mlp1": w_mlp1,
               "w_mlp2": w_mlp2, "final_norm_g": final_norm_g}
    m_in = {"c_ctx": m_c_ctx, "w_mod": m_w_mod, "b_mod": m_b_mod, "w_in": m_w_in, "q_norm_g": m_q_norm_g,
            "w_uq": m_w_uq, "kv_norm_g": m_kv_norm_g, "w_ukv": m_w_ukv, "conv_w": m_conv_w, "w_out": m_w_out,
            "w_mlp1": m_w_mlp1, "w_mlp2": m_w_mlp2, "final_norm_g": m_final_norm_g}
    v_in = {"c_ctx": v_c_ctx, "w_mod": v_w_mod, "b_mod": v_b_mod, "w_in": v_w_in, "q_norm_g": v_q_norm_g,
            "w_uq": v_w_uq, "kv_norm_g": v_kv_norm_g, "w_ukv": v_w_ukv, "conv_w": v_conv_w, "w_out": v_w_out,
            "w_mlp1": v_w_mlp1, "w_mlp2": v_w_mlp2, "final_norm_g": v_final_norm_g}
    names = list(weights)
    small_names = ["c_ctx", "b_mod", "q_norm_g", "kv_norm_g", "final_norm_g"]
    delta, new_m, new_v = {}, {}, {}

    def flat(a):
        return a.reshape(1, -1)

    packs = [jnp.concatenate([flat(src[n]) for n in small_names], axis=1) for src in (weights, grads, m_in, v_in)]
    small_out = _adamw(*packs, "adamw_small")
    off = 0
    for n in small_names:
        size = weights[n].size
        for dst, arr in zip((delta, new_m, new_v), small_out):
            dst[n] = arr[:, off:off + size].reshape(weights[n].shape)
        off += size
    for n in names:
        if n in small_names:
            continue
        shp = weights[n].shape
        two_d = (shp[0] * shp[1], shp[2])
        outs = _adamw(weights[n].reshape(two_d), grads[n].reshape(two_d), m_in[n].reshape(two_d),
                      v_in[n].reshape(two_d), "adamw_" + n)
        delta[n], new_m[n], new_v[n] = [a.reshape(shp) for a in outs]

    return (loss, grad_x[None], *[grads[n] for n in names], *[delta[n] for n in names],
            *[new_m[n] for n in names], *[new_v[n] for n in names])
```

```python
import math

import jax
import jax.numpy as jnp
import numpy as np
from jax import lax
from jax.experimental import pallas as pl
from jax.experimental.pallas import tpu as pltpu

F32 = jnp.float32
BF16 = jnp.bfloat16

D_MODEL = 1024
GRID_W = 64
N_HEADS = 8
QK_NOPE = 64
QK_ROPE = 32
V_DIM = 64
Q_RANK = 256
KV_RANK = 128
MLA_IN = Q_RANK + KV_RANK + QK_ROPE
CONV_W = 512
HEAD_COLS = 512
D_FF = 4096
ROPE_THETA = 10000.0
EPS = 1e-6
ATTN_SCALE = 1.0 / math.sqrt(QK_NOPE + QK_ROPE)
LOG2_E = 1.0 / math.log(2.0)
EXP2_SCALE = ATTN_SCALE * LOG2_E
N_DEV = 8
LANES = 128

ADAM_LR, ADAM_B1, ADAM_B2, ADAM_EPS, ADAM_WD, ADAM_STEP = 0.001, 0.9, 0.999, 1e-08, 0.01, 10

ROW_TILE = 256
VMEM_BIG = 60 * 1024 * 1024


def _params(sem=None, vmem=None):
    return pltpu.CompilerParams(dimension_semantics=sem, vmem_limit_bytes=vmem)


def _pick(n, prefs):
    for p in prefs:
        if n % p == 0:
            return p
    return n


def _my_index():
    return 4 * lax.axis_index("x") + 2 * lax.axis_index("y") + lax.axis_index("c")


def _all_gather(arrays, name, in_vmem):
    space = pltpu.VMEM if in_vmem else pl.ANY
    n = len(arrays)

    def body(*refs):
        x_refs, out_refs = refs[:n], refs[n:2 * n]
        send_sems, recv_sems, local_sems = refs[2 * n:]
        x, y, c = lax.axis_index("x"), lax.axis_index("y"), lax.axis_index("c")
        me, sibling = (x, y, c), (x, y, 1 - c)
        chips = [(1 - x, y), (x, 1 - y), (1 - x, 1 - y)]

        def slot(a, px, py, pc):
            return out_refs[a].at[4 * px + 2 * py + pc]

        def copy(a, k, block, to, src=None):
            return pltpu.make_async_remote_copy(
                src_ref=slot(a, *block) if src is None else src, dst_ref=slot(a, *block),
                send_sem=send_sems.at[7 * a + k], recv_sem=recv_sems.at[7 * a + k],
                device_id=to, device_id_type=pl.DeviceIdType.MESH)

        mine = [pltpu.make_async_copy(x_refs[a], slot(a, *me), local_sems.at[a]) for a in range(n)]
        for cp in mine:
            cp.start()
        started = []
        for a in range(n):
            first = [copy(a, 0, me, sibling, src=x_refs[a])]
            first += [copy(a, 1 + j, me, (*chip, c), src=x_refs[a]) for j, chip in enumerate(chips)]
            for cp in first:
                cp.start()
            started += first
        for a in range(n):
            for j, chip in enumerate(chips):
                copy(a, 1 + j, (*chip, c), me).wait_recv()
                passed = copy(a, 4 + j, (*chip, c), sibling)
                passed.start()
                started.append(passed)
        for a in range(n):
            copy(a, 0, sibling, me).wait_recv()
            for j, chip in enumerate(chips):
                copy(a, 4 + j, (*chip, 1 - c), me).wait_recv()
        for cp in started:
            cp.wait_send()
        for cp in mine:
            cp.wait()

    outs = pl.pallas_call(
        body, name=name,
        out_shape=tuple(jax.ShapeDtypeStruct((N_DEV,) + a.shape, a.dtype) for a in arrays),
        in_specs=[pl.BlockSpec(memory_space=space)] * n,
        out_specs=tuple(pl.BlockSpec(memory_space=space) for _ in arrays),
        scratch_shapes=[pltpu.SemaphoreType.DMA((7 * n,)), pltpu.SemaphoreType.DMA((7 * n,)),
                        pltpu.SemaphoreType.DMA((n,))],
    )(*arrays)
    return list(outs)


def _all_to_all(arrays, name):
    n = len(arrays)

    def body(*refs):
        x_refs, y_refs = refs[:n], refs[n:2 * n]
        send_sems, recv_sems, local_sems = refs[2 * n:]
        x, y, c = lax.axis_index("x"), lax.axis_index("y"), lax.axis_index("c")
        me = 4 * x + 2 * y + c
        peers = []
        for k in range(1, N_DEV):
            peers.append((1 - x if k & 4 else x, 1 - y if k & 2 else y, 1 - c if k & 1 else c))

        def copy(a, k, peer, landing):
            pid = 4 * peer[0] + 2 * peer[1] + peer[2]
            return pltpu.make_async_remote_copy(
                src_ref=x_refs[a].at[pid], dst_ref=y_refs[a].at[pid if landing else me],
                send_sem=send_sems.at[7 * a + k], recv_sem=recv_sems.at[7 * a + k],
                device_id=peer, device_id_type=pl.DeviceIdType.MESH)

        mine = [pltpu.make_async_copy(x_refs[a].at[me], y_refs[a].at[me], local_sems.at[a]) for a in range(n)]
        for cp in mine:
            cp.start()
        sends = [copy(a, k, peer, False) for a in range(n) for k, peer in enumerate(peers)]
        for cp in sends:
            cp.start()
        for a in range(n):
            for k, peer in enumerate(peers):
                copy(a, k, peer, True).wait_recv()
        for cp in sends:
            cp.wait_send()
        for cp in mine:
            cp.wait()

    outs = pl.pallas_call(
        body, name=name,
        out_shape=tuple(jax.ShapeDtypeStruct(a.shape, a.dtype) for a in arrays),
        in_specs=[pl.BlockSpec(memory_space=pl.ANY)] * n,
        out_specs=tuple(pl.BlockSpec(memory_space=pl.ANY) for _ in arrays),
        scratch_shapes=[pltpu.SemaphoreType.DMA((7 * n,)), pltpu.SemaphoreType.DMA((7 * n,)),
                        pltpu.SemaphoreType.DMA((n,))],
    )(*arrays)
    return list(outs)


_DIMS = {"nn": (((1,), (0,)), ((), ())), "nt": (((1,), (1,)), ((), ())), "tn": (((0,), (0,)), ((), ()))}
NT_DIMS = _DIMS["nt"]
TN_DIMS = _DIMS["tn"]


def _swap8(x):
    lane = lax.broadcasted_iota(jnp.int32, x.shape, 1)
    return jnp.where((lane & 15) < 8, pltpu.roll(x, LANES - 8, 1), pltpu.roll(x, 8, 1))


def _rope(x, cos, sgn, bwd):
    return x * cos + (_swap8(x * sgn) if bwd else _swap8(x) * sgn)


def _matmul(a, b, *, mode, name, out_dtype=F32, tm=512, tn=512, tk=512, m=None, k=None,
            epilogue=None, extra=(), addend=None, slots=None):
    if mode == "nn":
        m = a.shape[0] if m is None else m
        k = a.shape[1]
        n = N_DEV * b.shape[2] if slots == "b_cols" else b.shape[1]
    elif mode == "nt":
        m = a.shape[0] if m is None else m
        k = a.shape[1]
        n = b.shape[1] if slots == "b_contract" else b.shape[0]
    else:
        k = a.shape[0] if k is None else k
        m, n = a.shape[1], b.shape[1]
    tm, tn, tk = min(tm, m), min(tn, n), min(tk, k)
    if slots == "b_cols":
        tn = b.shape[2]
    if slots == "b_contract":
        tk = b.shape[2]
    if slots == "out":
        tn = n // N_DEV
    assert m % tm == 0 and n % tn == 0 and k % tk == 0, (name, m, n, k, tm, tn, tk)
    nk = k // tk
    dims = _DIMS[mode]
    a_spec = (pl.BlockSpec((tk, tm), lambda i, j, kk: (kk, i)) if mode == "tn"
              else pl.BlockSpec((tm, tk), lambda i, j, kk: (i, kk)))
    if slots == "b_cols":
        b_spec = pl.BlockSpec((None, tk, tn), lambda i, j, kk: (j, kk, 0))
    elif slots == "b_contract":
        b_spec = pl.BlockSpec((None, tn, tk), lambda i, j, kk: (kk, j, 0))
    elif mode == "nt":
        b_spec = pl.BlockSpec((tn, tk), lambda i, j, kk: (j, kk))
    else:
        b_spec = pl.BlockSpec((tk, tn), lambda i, j, kk: (kk, j))
    tile = pl.BlockSpec((tm, tn), lambda i, j, kk: (i, j))
    if slots == "out":
        o_spec = pl.BlockSpec((None, tm, tn), lambda i, j, kk: (j, i, 0))
        o_shape = (N_DEV, m, tn)
    else:
        o_spec, o_shape = tile, (m, n)
    in_specs, args = [a_spec, b_spec], [a, b]
    if epilogue == "drelu2":
        in_specs.append(tile)
    elif epilogue == "rope":
        in_specs += [pl.BlockSpec((tm, LANES), lambda i, j, kk: (i, 0))] * 2
    args += list(extra)
    if addend is not None:
        in_specs.append(tile)
        args.append(addend)
    if epilogue == "relu2":
        out_shape = (jax.ShapeDtypeStruct(o_shape, BF16), jax.ShapeDtypeStruct(o_shape, BF16))
        out_specs = (o_spec, o_spec)
    else:
        out_shape = jax.ShapeDtypeStruct(o_shape, out_dtype)
        out_specs = o_spec
    n_in = len(args)
    n_out = 2 if epilogue == "relu2" else 1

    def body(*refs):
        a_ref, b_ref = refs[0], refs[1]
        outs = refs[n_in:n_in + n_out]
        part = lax.dot_general(a_ref[...], b_ref[...], dims, preferred_element_type=F32)

        def finish(acc):
            if addend is not None:
                acc = acc + refs[n_in - 1][...]
            if epilogue == "relu2":
                outs[0][...] = acc.astype(BF16)
                r = jnp.maximum(acc, 0.0)
                outs[1][...] = (r * r).astype(BF16)
            elif epilogue == "drelu2":
                u = refs[2][...].astype(F32)
                outs[0][...] = (acc * (2.0 * jnp.maximum(u, 0.0))).astype(out_dtype)
            elif epilogue == "rope":
                cos, sgn = refs[2][...], refs[3][...]
                for h in range(tn // LANES):
                    sl = slice(h * LANES, (h + 1) * LANES)
                    outs[0][:, sl] = _rope(acc[:, sl], cos, sgn, False).astype(out_dtype)
            else:
                outs[0][...] = acc.astype(out_dtype)

        if nk == 1:
            finish(part)
        else:
            acc_ref = refs[n_in + n_out]
            kk = pl.program_id(2)

            @pl.when(kk == 0)
            def _():
                acc_ref[...] = part

            @pl.when(kk > 0)
            def _():
                acc_ref[...] += part

            @pl.when(kk == nk - 1)
            def _():
                finish(acc_ref[...])

    return pl.pallas_call(
        body, name=name, grid=(m // tm, n // tn, nk),
        out_shape=out_shape, in_specs=in_specs, out_specs=out_specs,
        scratch_shapes=[pltpu.VMEM((tm, tn), F32)] if nk > 1 else [],
        compiler_params=_params(("parallel", "parallel", "arbitrary"), VMEM_BIG),
    )(*args)


def _rstd(x):
    return lax.rsqrt(jnp.mean(x * x, axis=1, keepdims=True) + EPS)


def _norm_bwd(dxn, xn, r):
    return r * (dxn - xn * jnp.mean(dxn * xn, axis=1, keepdims=True))


def _modulate_all(x, ctx, shift, scale, name):
    s, d = x.shape
    t = s + ctx.shape[0]
    ns = s // ROW_TILE
    nc = ctx.shape[0] // ROW_TILE

    def body(x_ref, c_ref, sh_ref, sc_ref, h_ref):
        def emit(v):
            h_ref[...] = (v * _rstd(v) * (1.0 + sc_ref[0]) + sh_ref[0]).astype(BF16)

        i = pl.program_id(0)

        @pl.when(i < ns)
        def _():
            emit(x_ref[...])

        @pl.when(i >= ns)
        def _():
            emit(c_ref[...])

    vec = pl.BlockSpec((1, 1, d), lambda i: (jnp.where(i < ns, 0, 1), 0, 0))
    return pl.pallas_call(
        body, name=name, grid=(ns + nc,),
        out_shape=jax.ShapeDtypeStruct((t, d), BF16),
        in_specs=[pl.BlockSpec((ROW_TILE, d), lambda i: (jnp.minimum(i, ns - 1), 0)),
                  pl.BlockSpec((ROW_TILE, d), lambda i: (jnp.maximum(i - ns, 0), 0)), vec, vec],
        out_specs=pl.BlockSpec((ROW_TILE, d), lambda i: (i, 0)),
        compiler_params=_params(("arbitrary",)),
    )(x, ctx, shift, scale)


def _resid_modulate(x, o, gate, shift, scale, name):
    s, d = x.shape

    def body(x_ref, o_ref, g_ref, sh_ref, sc_ref, x1_ref, h_ref):
        x1 = x_ref[...] + g_ref[...] * o_ref[...]
        x1_ref[...] = x1
        h_ref[...] = (x1 * _rstd(x1) * (1.0 + sc_ref[...]) + sh_ref[...]).astype(BF16)

    row = pl.BlockSpec((ROW_TILE, d), lambda i: (i, 0))
    vec = pl.BlockSpec((1, d), lambda i: (0, 0))
    return pl.pallas_call(
        body, name=name, grid=(s // ROW_TILE,),
        out_shape=(jax.ShapeDtypeStruct((s, d), F32), jax.ShapeDtypeStruct((s, d), BF16)),
        in_specs=[row, row, vec, vec, vec], out_specs=(row, row),
        compiler_params=_params(("parallel",)),
    )(x, o, gate, shift, scale)


def _final(x1, m, gate, gain, target, name):
    s, d = x1.shape
    n = s // ROW_TILE

    def body(x1_ref, m_ref, g_ref, gf_ref, t_ref, dx2_ref, dm_ref, sums_ref):
        i = pl.program_id(0)
        mm = m_ref[...]
        x2 = x1_ref[...] + g_ref[...] * mm
        r = _rstd(x2)
        xn = x2 * r
        err = xn * gf_ref[...] - t_ref[...]
        dy = err * (1.0 / d)
        dx2 = _norm_bwd(dy * gf_ref[...], xn, r)
        dx2_ref[...] = dx2
        dm_ref[...] = (dx2 * g_ref[...]).astype(BF16)

        @pl.when(i == 0)
        def _():
            sums_ref[...] = jnp.zeros_like(sums_ref)

        sums_ref[0:1, :] += jnp.sum(dy * xn, axis=0, keepdims=True)
        sums_ref[1:2, :] += jnp.sum(dx2 * mm, axis=0, keepdims=True)
        sums_ref[2:3, :] += jnp.sum(err * err, axis=0, keepdims=True)

        @pl.when(i == n - 1)
        def _():
            tot = jnp.sum(sums_ref[2:3, :], axis=1, keepdims=True) * (0.5 / d)
            sums_ref[3:4, :] = jnp.broadcast_to(tot, (1, d))

    row = pl.BlockSpec((ROW_TILE, d), lambda i: (i, 0))
    vec = pl.BlockSpec((1, d), lambda i: (0, 0))
    return pl.pallas_call(
        body, name=name, grid=(n,),
        out_shape=(jax.ShapeDtypeStruct((s, d), F32), jax.ShapeDtypeStruct((s, d), BF16),
                   jax.ShapeDtypeStruct((8, d), F32)),
        in_specs=[row, row, vec, vec, row],
        out_specs=(row, row, pl.BlockSpec((8, d), lambda i: (0, 0))),
        compiler_params=_params(("arbitrary",)),
    )(x1, m, gate, gain, target)


def _modulate_bwd(dh, row_off, xsrc, scale, name, dres=None, o=None, gate=None):
    s, d = xsrc.shape
    n = s // ROW_TILE
    has_dx, has_o = dres is not None, o is not None
    assert has_dx or not has_o

    def body(*refs):
        it = iter(refs)
        dh_ref, x_ref, sc_ref = next(it), next(it), next(it)
        dres_ref = next(it) if has_dx else None
        o_ref, g_ref = (next(it), next(it)) if has_o else (None, None)
        dx_ref = next(it) if has_dx else None
        do_ref = next(it) if has_o else None
        sums_ref = next(it)
        i = pl.program_id(0)
        x = x_ref[...]
        r = _rstd(x)
        xn = x * r
        dhv = dh_ref[...]

        @pl.when(i == 0)
        def _():
            sums_ref[...] = jnp.zeros_like(sums_ref)

        sums_ref[0:1, :] += jnp.sum(dhv * xn, axis=0, keepdims=True)
        sums_ref[1:2, :] += jnp.sum(dhv, axis=0, keepdims=True)
        if has_dx:
            dx = dres_ref[...] + _norm_bwd(dhv * (1.0 + sc_ref[...]), xn, r)
            dx_ref[...] = dx
            if has_o:
                do_ref[...] = (dx * g_ref[...]).astype(BF16)
                sums_ref[2:3, :] += jnp.sum(dx * o_ref[...], axis=0, keepdims=True)

    row = pl.BlockSpec((ROW_TILE, d), lambda i: (i, 0))
    vec = pl.BlockSpec((1, d), lambda i: (0, 0))
    in_specs = [pl.BlockSpec((ROW_TILE, d), lambda i: (i + row_off, 0)), row, vec]
    args = [dh, xsrc, scale]
    out_shape, out_specs = [], []
    if has_dx:
        in_specs.append(row)
        args.append(dres)
        out_shape.append(jax.ShapeDtypeStruct((s, d), F32))
        out_specs.append(row)
    if has_o:
        in_specs += [row, vec]
        args += [o, gate]
        out_shape.append(jax.ShapeDtypeStruct((s, d), BF16))
        out_specs.append(row)
    out_shape.append(jax.ShapeDtypeStruct((8, d), F32))
    out_specs.append(pl.BlockSpec((8, d), lambda i: (0, 0)))
    return pl.pallas_call(
        body, name=name, grid=(n,),
        out_shape=tuple(out_shape), in_specs=in_specs, out_specs=tuple(out_specs),
        compiler_params=_params(("arbitrary",)),
    )(*args)


def _qkv_prep(z, q_gain, kv_gain, cos, sgn, name):
    t = z.shape[0]

    def body(z_ref, qg_ref, kg_ref, c_ref, s_ref, cq_ref, kv_ref):
        zq = z_ref[:, 0:Q_RANK]
        cq_ref[...] = (zq * _rstd(zq) * qg_ref[...]).astype(BF16)
        zk = z_ref[:, Q_RANK:Q_RANK + KV_RANK]
        kv_ref[:, 0:KV_RANK] = (zk * _rstd(zk) * kg_ref[...]).astype(BF16)
        kr = z_ref[:, Q_RANK + KV_RANK:HEAD_COLS]
        kv_ref[:, KV_RANK:KV_RANK + LANES] = _rope(kr, c_ref[...], s_ref[...], False).astype(BF16)

    tab = pl.BlockSpec((ROW_TILE, LANES), lambda i: (i, 0))
    return pl.pallas_call(
        body, name=name, grid=(t // ROW_TILE,),
        out_shape=(jax.ShapeDtypeStruct((t, Q_RANK), BF16), jax.ShapeDtypeStruct((t, KV_RANK + LANES), BF16)),
        in_specs=[pl.BlockSpec((ROW_TILE, HEAD_COLS), lambda i: (i, 0)),
                  pl.BlockSpec((1, Q_RANK), lambda i: (0, 0)), pl.BlockSpec((1, KV_RANK), lambda i: (0, 0)), tab, tab],
        out_specs=(pl.BlockSpec((ROW_TILE, Q_RANK), lambda i: (i, 0)),
                   pl.BlockSpec((ROW_TILE, KV_RANK + LANES), lambda i: (i, 0))),
        compiler_params=_params(("parallel",)),
    )(z, q_gain, kv_gain, cos, sgn)


def _qkv_prep_bwd(z, dcq, dkv, q_gain, kv_gain, cos, sgn, s, name):
    t = z.shape[0]
    ns = s // ROW_TILE

    def body(z_ref, dcq_ref, dkv_ref, qg_ref, kg_ref, c_ref, s_ref, dz_ref, sums_ref):
        i = pl.program_id(0)

        @pl.when(i == 0)
        def _():
            sums_ref[...] = jnp.zeros_like(sums_ref)

        @pl.when(i < ns)
        def _():
            zq = z_ref[:, 0:Q_RANK]
            r = _rstd(zq)
            zn = zq * r
            dc = dcq_ref[...]
            sums_ref[0:1, :] += jnp.sum(dc * zn, axis=0, keepdims=True)
            dz_ref[:, 0:Q_RANK] = _norm_bwd(dc * qg_ref[...], zn, r).astype(BF16)

        @pl.when(i >= ns)
        def _():
            dz_ref[:, 0:Q_RANK] = jnp.zeros((ROW_TILE, Q_RANK), BF16)

        zk = z_ref[:, Q_RANK:Q_RANK + KV_RANK]
        r = _rstd(zk)
        zn = zk * r
        dc = dkv_ref[:, 0:KV_RANK]
        sums_ref[1:2, 0:KV_RANK] += jnp.sum(dc * zn, axis=0, keepdims=True)
        dz_ref[:, Q_RANK:Q_RANK + KV_RANK] = _norm_bwd(dc * kg_ref[...], zn, r).astype(BF16)
        dkr = dkv_ref[:, KV_RANK:KV_RANK + LANES]
        dz_ref[:, Q_RANK + KV_RANK:HEAD_COLS] = _rope(dkr, c_ref[...], s_ref[...], True).astype(BF16)

    tab = pl.BlockSpec((ROW_TILE, LANES), lambda i: (i, 0))
    return pl.pallas_call(
        body, name=name, grid=(t // ROW_TILE,),
        out_shape=(jax.ShapeDtypeStruct((t, HEAD_COLS), BF16), jax.ShapeDtypeStruct((8, Q_RANK), F32)),
        in_specs=[pl.BlockSpec((ROW_TILE, HEAD_COLS), lambda i: (i, 0)),
                  pl.BlockSpec((ROW_TILE, Q_RANK), lambda i: (jnp.minimum(i, ns - 1), 0)),
                  pl.BlockSpec((ROW_TILE, KV_RANK + LANES), lambda i: (i, 0)),
                  pl.BlockSpec((1, Q_RANK), lambda i: (0, 0)), pl.BlockSpec((1, KV_RANK), lambda i: (0, 0)), tab, tab],
        out_specs=(pl.BlockSpec((ROW_TILE, HEAD_COLS), lambda i: (i, 0)), pl.BlockSpec((8, Q_RANK), lambda i: (0, 0))),
        compiler_params=_params(("arbitrary",)),
    )(z, dcq, dkv, q_gain, kv_gain, cos, sgn)


def _shift_rows(u, s):
    rowi = lax.broadcasted_iota(jnp.int32, u.shape, 0)
    prev = jnp.where(rowi == 0, 0.0, pltpu.roll(u, 1, 0))
    nxt = jnp.where(rowi == s - 1, 0.0, pltpu.roll(u, s - 1, 0))
    return prev, nxt


def _conv_fwd(z_conv, cw, a_cat, name):
    s = z_conv.shape[0]

    def body(z_ref, w_ref, a_in_ref, o_ref):
        del a_in_ref
        gb, gc, xv = z_ref[:, 0:LANES], z_ref[:, LANES:2 * LANES], z_ref[:, 2 * LANES:3 * LANES]
        u = gc * xv
        prev, nxt = _shift_rows(u, s)
        y = w_ref[0:1, :] * prev + w_ref[1:2, :] * u + w_ref[2:3, :] * nxt
        o_ref[...] = (gb * y).astype(BF16)

    return pl.pallas_call(
        body, name=name, grid=(CONV_W // LANES,),
        out_shape=jax.ShapeDtypeStruct(a_cat.shape, a_cat.dtype),
        in_specs=[pl.BlockSpec((s, 3 * LANES), lambda j: (0, j)), pl.BlockSpec((3, LANES), lambda j: (0, j)),
                  pl.BlockSpec(memory_space=pl.ANY)],
        out_specs=pl.BlockSpec((s, LANES), lambda j: (0, 4 + j)),
        input_output_aliases={2: 0},
        compiler_params=_params(("parallel",), VMEM_BIG),
    )(z_conv, cw, a_cat)


def _conv_bwd(z_conv, cw, da, name):
    s = z_conv.shape[0]

    def body(z_ref, w_ref, da_ref, dz_ref, dw_ref):
        gb, gc, xv = z_ref[:, 0:LANES], z_ref[:, LANES:2 * LANES], z_ref[:, 2 * LANES:3 * LANES]
        u = gc * xv
        prev, nxt = _shift_rows(u, s)
        dcv = da_ref[...]
        dz_ref[:, 0:LANES] = (dcv * (w_ref[0:1, :] * prev + w_ref[1:2, :] * u + w_ref[2:3, :] * nxt)).astype(BF16)
        dy = dcv * gb
        dw_ref[0:1, :] = jnp.sum(dy * prev, axis=0, keepdims=True)
        dw_ref[1:2, :] = jnp.sum(dy * u, axis=0, keepdims=True)
        dw_ref[2:3, :] = jnp.sum(dy * nxt, axis=0, keepdims=True)
        dyp, dyn = _shift_rows(dy, s)
        du = w_ref[0:1, :] * dyn + w_ref[1:2, :] * dy + w_ref[2:3, :] * dyp
        dz_ref[:, LANES:2 * LANES] = (du * xv).astype(BF16)
        dz_ref[:, 2 * LANES:3 * LANES] = (du * gc).astype(BF16)

    blk = pl.BlockSpec((s, 3 * LANES), lambda j: (0, j))
    cws = pl.BlockSpec((3, LANES), lambda j: (0, j))
    return pl.pallas_call(
        body, name=name, grid=(CONV_W // LANES,),
        out_shape=(jax.ShapeDtypeStruct(z_conv.shape, BF16), jax.ShapeDtypeStruct((3, CONV_W), F32)),
        in_specs=[blk, cws, pl.BlockSpec((s, LANES), lambda j: (0, 4 + j))], out_specs=(blk, cws),
        compiler_params=_params(("parallel",), VMEM_BIG),
    )(z_conv, cw, da)


ATT_TQ = 256


def _head_mask(shape, hh):
    lane = lax.broadcasted_iota(jnp.int32, shape, 1)
    return (lane >= hh * V_DIM) & (lane < (hh + 1) * V_DIM)


def _attn_fwd(qf, kv, name):
    s, t = qf.shape[0], kv.shape[0]

    def body(q_ref, k_ref, v_ref, o_ref, ob_ref, st_ref):
        v = v_ref[...]
        vlane = lax.broadcasted_iota(jnp.int32, v.shape, 1)
        olane = lax.broadcasted_iota(jnp.int32, (ATT_TQ, LANES), 1)
        acc = jnp.zeros((ATT_TQ, LANES), F32)
        stat = jnp.zeros((ATT_TQ, LANES), F32)
        for hh in range(2):
            sl = slice(hh * LANES, (hh + 1) * LANES)
            sc = lax.dot_general(q_ref[:, sl], k_ref[:, sl], NT_DIMS, preferred_element_type=F32)
            mx = jnp.max(sc, axis=1, keepdims=True)
            e = jnp.exp2((sc - mx) * EXP2_SCALE).astype(BF16)
            one_lane = (1 - hh) * V_DIM
            vm = jnp.where(_head_mask(v.shape, hh), v, jnp.where(vlane == one_lane, 1.0, 0.0).astype(BF16))
            r = jnp.dot(e, vm, preferred_element_type=F32)
            den = jnp.sum(jnp.where(olane == one_lane, r, 0.0), axis=1, keepdims=True)
            acc = acc + jnp.where(_head_mask(r.shape, hh), r * (1.0 / den), 0.0)
            stat = stat + jnp.where(olane == hh, mx * EXP2_SCALE + jnp.log(den) * LOG2_E, 0.0)
        o_ref[...] = acc
        ob_ref[...] = acc.astype(BF16)
        st_ref[...] = stat.T[0:8, :]

    o_spec = pl.BlockSpec((ATT_TQ, LANES), lambda p, i: (i, p))
    return pl.pallas_call(
        body, name=name, grid=(N_HEADS // 2, s // ATT_TQ),
        out_shape=(jax.ShapeDtypeStruct((s, N_HEADS * V_DIM), F32),
                   jax.ShapeDtypeStruct((s, D_MODEL), BF16),
                   jax.ShapeDtypeStruct((N_HEADS // 2 * 8, s), F32)),
        in_specs=[pl.BlockSpec((ATT_TQ, 2 * LANES), lambda p, i: (i, p)),
                  pl.BlockSpec((t, 2 * LANES), lambda p, i: (0, p)),
                  pl.BlockSpec((t, LANES), lambda p, i: (0, N_HEADS + p))],
        out_specs=(o_spec, o_spec, pl.BlockSpec((8, ATT_TQ), lambda p, i: (p, i))),
        compiler_params=_params(("parallel", "parallel"), VMEM_BIG),
    )(qf, kv, kv)


def _attn_bwd(qf, kv, o, da, stats, cos, sgn, name):
    s, t = qf.shape[0], kv.shape[0]
    nq = s // ATT_TQ

    def body(q_ref, k_ref, v_ref, o_ref, do_ref, st_ref, c_ref, s_ref, dq_ref, dk_ref, dv_ref, dk_acc, dv_acc):
        i = pl.program_id(1)

        @pl.when(i == 0)
        def _():
            dk_acc[...] = jnp.zeros_like(dk_acc)
            dv_acc[...] = jnp.zeros_like(dv_acc)

        v = v_ref[...]
        do = do_ref[...]
        od = do * o_ref[...]
        ones = jnp.ones((8, LANES), F32)
        for hh in range(2):
            sl = slice(hh * LANES, (hh + 1) * LANES)
            q, k = q_ref[:, sl], k_ref[:, sl]
            mask = _head_mask(do.shape, hh)
            st = lax.dot_general(k, q, NT_DIMS, preferred_element_type=F32)
            pt = jnp.exp2(st * EXP2_SCALE - st_ref[hh:hh + 1, :])
            dom = jnp.where(mask, do, 0.0).astype(BF16)
            dpt = lax.dot_general(v, dom, NT_DIMS, preferred_element_type=F32)
            delta = lax.dot_general(ones, jnp.where(mask, od, 0.0), NT_DIMS, preferred_element_type=F32,
                                    precision=lax.Precision.HIGHEST)[0:1, :]
            dst = (pt * (dpt - delta)).astype(BF16)
            dv_acc[...] += jnp.dot(pt.astype(BF16), dom, preferred_element_type=F32)
            dk_acc[:, sl] += jnp.dot(dst, q, preferred_element_type=F32)
            dq = lax.dot_general(dst, k, TN_DIMS, preferred_element_type=F32) * ATTN_SCALE
            dq_ref[:, sl] = _rope(dq, c_ref[...], s_ref[...], True).astype(BF16)

        @pl.when(i == nq - 1)
        def _():
            dk_ref[...] = (dk_acc[...] * ATTN_SCALE).astype(BF16)
            dv_ref[...] = dv_acc[...].astype(BF16)

    o_spec = pl.BlockSpec((ATT_TQ, LANES), lambda p, i: (i, p))
    tab = pl.BlockSpec((ATT_TQ, LANES), lambda p, i: (i, 0))
    return pl.pallas_call(
        body, name=name, grid=(N_HEADS // 2, nq),
        out_shape=(jax.ShapeDtypeStruct((s, N_HEADS * LANES), BF16),
                   jax.ShapeDtypeStruct((t, N_HEADS * LANES), BF16),
                   jax.ShapeDtypeStruct((t, N_HEADS * V_DIM), BF16)),
        in_specs=[pl.BlockSpec((ATT_TQ, 2 * LANES), lambda p, i: (i, p)),
                  pl.BlockSpec((t, 2 * LANES), lambda p, i: (0, p)),
                  pl.BlockSpec((t, LANES), lambda p, i: (0, N_HEADS + p)),
                  o_spec, o_spec,
                  pl.BlockSpec((8, ATT_TQ), lambda p, i: (p, i)), tab, tab],
        out_specs=(pl.BlockSpec((ATT_TQ, 2 * LANES), lambda p, i: (i, p)),
                   pl.BlockSpec((t, 2 * LANES), lambda p, i: (0, p)),
                   pl.BlockSpec((t, LANES), lambda p, i: (0, p))),
        scratch_shapes=[pltpu.VMEM((t, 2 * LANES), F32), pltpu.VMEM((t, LANES), F32)],
        compiler_params=_params(("parallel", "arbitrary"), VMEM_BIG),
    )(qf, kv, kv, o, da, stats, cos, sgn)


def _silu(x):
    return x * (1.0 / (1.0 + jnp.exp(-x)))


def _adaln_fwd(a, w, b, name):
    def body(a_ref, w_ref, b_ref, o_ref):
        o_ref[...] = jnp.dot(_silu(a_ref[...]), w_ref[...], preferred_element_type=F32,
                             precision=lax.Precision.HIGHEST) + b_ref[...]

    return pl.pallas_call(
        body, name=name, out_shape=jax.ShapeDtypeStruct((a.shape[0], w.shape[1]), F32),
        compiler_params=_params(None, VMEM_BIG),
    )(a, w, b)


def _adaln_bwd(a_t, w, d_ex, d_ctx, d_all, name):
    def body(at_ref, w_ref, dex_ref, dctx_ref, dall_ref, gw_ref, dsil_ref, dsum_ref):
        sil_t = _silu(at_ref[...])
        dctx = dctx_ref[...]
        row = dctx[0:1, :]
        for j in range(1, N_DEV):
            row = row + dctx[j:j + 1, :]
        rowi = lax.broadcasted_iota(jnp.int32, dctx.shape, 0)
        ctx_rows = jnp.where(rowi == 0, jnp.broadcast_to(row, dctx.shape), 0.0)
        hi = lax.Precision.HIGHEST
        d_rows = jnp.concatenate([dex_ref[...], ctx_rows], axis=0)
        gw_ref[...] = jnp.dot(sil_t, d_rows, preferred_element_type=F32, precision=hi)
        dsil_ref[...] = lax.dot_general(ctx_rows, w_ref[...], NT_DIMS, preferred_element_type=F32, precision=hi)
        tot = dall_ref[0]
        for j in range(1, N_DEV):
            tot = tot + dall_ref[j]
        dsum_ref[...] = tot

    return pl.pallas_call(
        body, name=name,
        out_shape=(jax.ShapeDtypeStruct(w.shape, F32), jax.ShapeDtypeStruct((8, w.shape[0]), F32),
                   jax.ShapeDtypeStruct(d_all.shape[1:], F32)),
        compiler_params=_params(None, VMEM_BIG),
    )(a_t, w, d_ex, d_ctx, d_all)


def _cctx_grad(parts, c_ctx, name):
    def body(p_ref, c_ref, o_ref):
        tot = p_ref[0]
        for j in range(1, N_DEV):
            tot = tot + p_ref[j]
        cv = c_ref[...]
        sg = 1.0 / (1.0 + jnp.exp(-cv))
        o_ref[...] = tot[0:1, :] * (sg * (1.0 + cv * (1.0 - sg)))

    return pl.pallas_call(body, name=name, out_shape=jax.ShapeDtypeStruct(c_ctx.shape, F32))(parts, c_ctx)


def _adamw(w, g, m, v, name, slots=False):
    rows, cols = w.shape
    tr = _pick(rows, (256, 128, 64, 32, 16, 8))

    def body(w_ref, g_ref, m_ref, v_ref, *outs):
        if slots:
            gv = g_ref[0].astype(F32)
            for j in range(1, N_DEV):
                gv = gv + g_ref[j].astype(F32)
            outs[0][...] = gv
        else:
            gv = g_ref[...]
        d_ref, nm_ref, nv_ref = outs[-3:]
        nm = ADAM_B1 * m_ref[...] + (1.0 - ADAM_B1) * gv
        nv = ADAM_B2 * v_ref[...] + (1.0 - ADAM_B2) * (gv * gv)
        m_hat = nm / (1.0 - ADAM_B1 ** ADAM_STEP)
        v_hat = nv / (1.0 - ADAM_B2 ** ADAM_STEP)
        d_ref[...] = -ADAM_LR * (m_hat / (jnp.sqrt(v_hat) + ADAM_EPS) + ADAM_WD * w_ref[...])
        nm_ref[...] = nm
        nv_ref[...] = nv

    blk = pl.BlockSpec((tr, cols), lambda i: (i, 0))
    g_spec = pl.BlockSpec((N_DEV, tr, cols), lambda i: (0, i, 0)) if slots else blk
    sh = jax.ShapeDtypeStruct((rows, cols), F32)
    n_out = 4 if slots else 3
    return pl.pallas_call(
        body, name=name, grid=(rows // tr,), out_shape=(sh,) * n_out,
        in_specs=[blk, g_spec, blk, blk], out_specs=(blk,) * n_out,
        compiler_params=_params(("parallel",)),
    )(w, g, m, v)


def _rope_tables(s, l):
    tok = np.arange(s)
    row = (tok // GRID_W).astype(np.float32)
    col = (tok % GRID_W).astype(np.float32)
    half = QK_ROPE // 2
    freqs = np.float32(ROPE_THETA) ** (-np.arange(0, half, 2, dtype=np.float32) / np.float32(half))
    dd = np.arange(QK_ROPE)
    pos = np.where((dd // half)[None, :] == 0, row[:, None], col[:, None]).astype(np.float32)
    ang = (pos * freqs[dd % (half // 2)][None, :]).astype(np.float32)
    sin = np.sin(ang).astype(np.float32)
    cos_t = np.ones((s + l, LANES), np.float32)
    sgn_t = np.zeros((s + l, LANES), np.float32)
    cos_t[:s, QK_NOPE:QK_NOPE + QK_ROPE] = np.cos(ang)
    sgn_t[:s, QK_NOPE:QK_NOPE + QK_ROPE] = np.where(((dd % half) // (half // 2))[None, :] == 0, -sin, sin)
    return jnp.asarray(cos_t), jnp.asarray(sgn_t)


def _slots_to_cols(g):
    return g.transpose(1, 0, 2).reshape(g.shape[1], N_DEV * g.shape[2])


def _cols_to_slots(w):
    return w.reshape(w.shape[0], N_DEV, w.shape[1] // N_DEV).transpose(1, 0, 2)


def _unpack_small_weights(g_in, g_uq, g_ukv):
    w_in = _slots_to_cols(g_in)
    zeros = jnp.zeros((D_MODEL, QK_NOPE), BF16)
    win_head = jnp.concatenate([w_in[:, :Q_RANK + KV_RANK], zeros, w_in[:, Q_RANK + KV_RANK:MLA_IN],
                                zeros[:, :LANES - QK_NOPE - QK_ROPE]], axis=1)
    win_conv = w_in[:, MLA_IN:].reshape(D_MODEL, 3, CONV_W // LANES, LANES).transpose(0, 2, 1, 3)
    win_conv = win_conv.reshape(D_MODEL, 3 * CONV_W)
    w_uq = _slots_to_cols(g_uq).reshape(Q_RANK, N_HEADS, QK_NOPE + QK_ROPE)
    wq = jnp.pad(w_uq, ((0, 0), (0, 0), (0, LANES - QK_NOPE - QK_ROPE))).reshape(Q_RANK, N_HEADS * LANES)
    w_ukv = _slots_to_cols(g_ukv).reshape(KV_RANK, N_HEADS, QK_NOPE + V_DIM)
    k_top = jnp.pad(w_ukv[:, :, :QK_NOPE], ((0, 0), (0, 0), (0, LANES - QK_NOPE))).reshape(KV_RANK, N_HEADS * LANES)
    v_top = w_ukv[:, :, QK_NOPE:].reshape(KV_RANK, N_HEADS * V_DIM)
    eye = jnp.pad(jnp.eye(QK_ROPE, dtype=BF16), ((QK_NOPE, LANES - QK_NOPE - QK_ROPE),) * 2)
    wk = jnp.concatenate([
        jnp.concatenate([k_top, v_top], axis=1),
        jnp.concatenate([jnp.tile(eye, (1, N_HEADS)), jnp.zeros((LANES, N_HEADS * V_DIM), BF16)], axis=1)], axis=0)
    return win_head, win_conv, wq, wk


def _pack_small_grads(d_head, d_conv, d_wq, d_wkk, d_wkv):
    d_conv = d_conv.reshape(D_MODEL, CONV_W // LANES, 3, LANES).transpose(0, 2, 1, 3).reshape(D_MODEL, 3 * CONV_W)
    g_in = jnp.concatenate([d_head[:, :Q_RANK + KV_RANK],
                            d_head[:, Q_RANK + KV_RANK + QK_NOPE:Q_RANK + KV_RANK + QK_NOPE + QK_ROPE], d_conv], axis=1)
    g_uq = d_wq.reshape(Q_RANK, N_HEADS, LANES)[:, :, :QK_NOPE + QK_ROPE].reshape(Q_RANK, -1)
    g_kn = d_wkk[:KV_RANK].reshape(KV_RANK, N_HEADS, LANES)[:, :, :QK_NOPE]
    g_v = d_wkv[:KV_RANK].reshape(KV_RANK, N_HEADS, V_DIM)
    g_ukv = jnp.concatenate([g_kn, g_v], axis=2).reshape(KV_RANK, -1)
    return [_cols_to_slots(g).astype(BF16) for g in (g_in, g_uq, g_ukv)]


def kernel(x, c, ctx, c_ctx, w_mod, b_mod, w_in, q_norm_g, w_uq, kv_norm_g, w_ukv, conv_w, w_out, w_mlp1, w_mlp2, final_norm_g, loss_target, m_c_ctx, m_w_mod, m_b_mod, m_w_in, m_q_norm_g, m_w_uq, m_kv_norm_g, m_w_ukv, m_conv_w, m_w_out, m_w_mlp1, m_w_mlp2, m_final_norm_g, v_c_ctx, v_w_mod, v_b_mod, v_w_in, v_q_norm_g, v_w_uq, v_kv_norm_g, v_w_ukv, v_conv_w, v_w_out, v_w_mlp1, v_w_mlp2, v_final_norm_g):
    me = _my_index()
    x2d, ctx2d, tgt = x[0], ctx[0], loss_target[0]
    s, l = x2d.shape[0], ctx2d.shape[0]
    t = s + l
    d = D_MODEL
    mod_cols = w_mod.shape[2]
    cw_cols = conv_w.shape[2]

    (c_all,) = _all_gather([jnp.pad(c, ((0, 7), (0, 0)))], "gather_c", True)
    a_rows = jnp.concatenate([c_all[:, 0, :], c_ctx[None, :], jnp.zeros((7, d), F32)], axis=0)
    b_cols = lax.dynamic_slice(b_mod, (0, me * mod_cols), (1, mod_cols))
    mod_cols_all = _adaln_fwd(a_rows, w_mod[0], b_cols, "adaln_fwd")
    cw_blk = jnp.pad(conv_w[0], ((0, 5), (0, mod_cols - cw_cols)))
    (gathered,) = _all_gather([jnp.concatenate([mod_cols_all, cw_blk], axis=0)], "gather_mod", True)
    mod_mine = lax.dynamic_index_in_dim(gathered, me, axis=1, keepdims=False).reshape(1, 6 * d)
    mod_ctx = gathered[:, 8, :].reshape(1, 6 * d)
    cw_full = gathered[:, 16:19, :cw_cols].transpose(1, 0, 2).reshape(3, CONV_W)
    sh1, sc1, g1, sh2, sc2, g2 = [mod_mine[:, i * d:(i + 1) * d] for i in range(6)]
    sh1c, sc1c = mod_ctx[:, 0:d], mod_ctx[:, d:2 * d]

    shards = [w.astype(BF16) for w in (w_in[0], w_uq[0], w_ukv[0], w_out[0], w_mlp1[0], w_mlp2[0])]
    g_in, g_uq, g_ukv, g_out, w1, g_w2 = _all_gather(shards, "gather_weights", False)
    win_head, win_conv, wq, wk = _unpack_small_weights(g_in, g_uq, g_ukv)
    wk_k, wk_v = wk[:, :N_HEADS * LANES], wk[:, N_HEADS * LANES:]
    wo = g_out.reshape(d, d)
    w2 = g_w2.reshape(D_FF, d)
    cos, sgn = _rope_tables(s, l)

    h_all = _modulate_all(x2d, ctx2d, jnp.stack([sh1, sh1c]), jnp.stack([sc1, sc1c]), "modulate1")
    tk_t = _pick(t, (2176, 768, 512, 256))
    z_head = _matmul(h_all, win_head, mode="nn", name="in_proj_head", tm=256, tn=512, tk=1024)
    z_conv = _matmul(h_all, win_conv, mode="nn", name="in_proj_conv", m=s, tm=512, tn=1536, tk=1024)
    cq, kv_in = _qkv_prep(z_head, q_norm_g, kv_norm_g, cos, sgn, "qkv_prep")
    qf = _matmul(cq, wq, mode="nn", name="q_up", out_dtype=BF16, m=s, tm=512, tn=1024, tk=256,
                 epilogue="rope", extra=(cos, sgn))
    kv = _matmul(kv_in, wk, mode="nn", name="kv_up", out_dtype=BF16, tm=256, tn=1536, tk=256)
    attn, a_cat, stats = _attn_fwd(qf, kv, "attn_fwd")
    a_cat = _conv_fwd(z_conv, cw_full, a_cat, "conv_fwd")
    o = _matmul(a_cat, wo, mode="nn", name="out_proj", tm=512, tn=1024, tk=1024)
    x1, h2 = _resid_modulate(x2d, o, g1, sh2, sc2, "resid_modulate2")
    u1, act = _matmul(h2, w1, mode="nn", name="mlp_up", tm=512, tk=1024, epilogue="relu2", slots="b_cols")
    mlp = _matmul(act, w2, mode="nn", name="mlp_down", tm=512, tn=1024, tk=2048)
    dx2, dm, fsums = _final(x1, mlp, g2, final_norm_g[None, :], tgt, "final_loss")

    d_w2 = _matmul(act, dm, mode="tn", name="d_w_mlp2", out_dtype=BF16, tm=1024, tn=1024, tk=1024)
    du1 = _matmul(dm, w2, mode="nt", name="d_act", out_dtype=BF16, tm=512, tn=1024, tk=1024,
                  epilogue="drelu2", extra=(u1,))
    d_w1 = _matmul(h2, du1, mode="tn", name="d_w_mlp1", out_dtype=BF16, tm=1024, tk=1024, slots="out")
    dh2 = _matmul(du1, w1, mode="nt", name="d_h2", tm=512, tn=1024, slots="b_contract")
    dx1, do, sums2 = _modulate_bwd(dh2, 0, x1, sc2, "modulate2_bwd", dres=dx2, o=o, gate=g1)
    d_wo = _matmul(a_cat, do, mode="tn", name="d_w_out", out_dtype=BF16, tm=1024, tn=1024, tk=1024)
    da = _matmul(do, wo, mode="nt", name="d_a", tm=512, tn=1024, tk=1024)
    dz_conv, d_cw = _conv_bwd(z_conv, cw_full, da, "conv_bwd")
    dq, dk, dv = _attn_bwd(qf, kv, attn, da, stats, cos, sgn, "attn_bwd")
    d_wq = _matmul(cq, dq, mode="tn", name="d_w_uq", k=s, tm=256, tn=1024, tk=2048)
    dcq = _matmul(dq, wq, mode="nt", name="d_cq", tm=512, tn=256, tk=1024)
    d_wkk = _matmul(kv_in, dk, mode="tn", name="d_w_ukv_k", tm=256, tn=1024, tk=tk_t)
    d_wkv = _matmul(kv_in, dv, mode="tn", name="d_w_ukv_v", tm=256, tn=512, tk=tk_t)
    dkv_in = _matmul(dk, wk_k, mode="nt", name="d_kv_in_k", tm=256, tn=256, tk=1024)
    dkv_in = _matmul(dv, wk_v, mode="nt", name="d_kv_in_v", tm=256, tn=256, tk=512, addend=dkv_in)
    dz_head, psums = _qkv_prep_bwd(z_head, dcq, dkv_in, q_norm_g, kv_norm_g, cos, sgn, s, "qkv_prep_bwd")
    d_head = _matmul(h_all, dz_head, mode="tn", name="d_w_in_head", tm=1024, tn=512, tk=tk_t)
    d_conv = _matmul(h_all, dz_conv, mode="tn", name="d_w_in_conv", k=s, tm=1024, tn=1536, tk=1024)
    dh_head = _matmul(dz_head, win_head, mode="nt", name="d_h1_head", tm=256, tn=1024, tk=512)
    dh = _matmul(dz_conv, win_conv, mode="nt", name="d_h1", tm=512, tn=1024, tk=1536, addend=dh_head)
    grad_x, sums1 = _modulate_bwd(dh, 0, x2d, sc1, "modulate1_bwd", dres=dx1)
    (sums1c,) = _modulate_bwd(dh_head, s // ROW_TILE, ctx2d, sc1c, "modulate1_ctx_bwd")

    zero = jnp.zeros((1, d), F32)
    wide = 6 * d
    d_mod_mine = jnp.concatenate([sums1[1:2], sums1[0:1], sums2[2:3], sums2[1:2], sums2[0:1], fsums[1:2]], axis=1)
    d_mod_ctx = jnp.concatenate([sums1c[1:2], sums1c[0:1], zero, zero, zero, zero], axis=1)
    misc = jnp.concatenate([psums[0:1, :], psums[1:2, :KV_RANK], fsums[0:1]], axis=1)
    misc = jnp.pad(misc, ((0, 0), (0, wide - misc.shape[1])))
    cw_rows = jnp.pad(d_cw, ((0, 0), (0, wide - CONV_W)))
    loss_row = jnp.pad(fsums[3:4], ((0, 0), (0, wide - d)))
    small = jnp.concatenate([d_mod_mine, d_mod_ctx, misc, cw_rows, loss_row, jnp.zeros((1, wide), F32)], axis=0)
    (d_all,) = _all_gather([small], "gather_small_grads", True)
    d_cols = lax.dynamic_slice_in_dim(d_all, me * mod_cols, mod_cols, axis=2)
    g_w_mod, dsil, dsum = _adaln_bwd(a_rows.T, w_mod[0], d_cols[:, 0, :], d_cols[:, 1, :], d_all, "adaln_bwd")
    (dsil_all,) = _all_gather([dsil], "gather_d_cctx", True)
    g_c_ctx = _cctx_grad(dsil_all, c_ctx[None, :], "cctx_grad")
    loss = dsum[6, 0]
    g_b_mod = dsum[0:1] + dsum[1:2]
    g_qg = dsum[2:3, 0:Q_RANK]
    g_kvg = dsum[2:3, Q_RANK:Q_RANK + KV_RANK]
    g_fg = dsum[2:3, Q_RANK + KV_RANK:Q_RANK + KV_RANK + d]
    g_cw = lax.dynamic_slice_in_dim(dsum[3:6, :CONV_W], me * cw_cols, cw_cols, axis=1)

    send = _pack_small_grads(d_head, d_conv, d_wq, d_wkk, d_wkv)
    send += [d_wo.reshape(N_DEV, d // N_DEV, d), d_w1, d_w2.reshape(N_DEV, D_FF // N_DEV, d)]
    slots = dict(zip(["w_in", "w_uq", "w_ukv", "w_out", "w_mlp1", "w_mlp2"], _all_to_all(send, "exchange_grads")))

    grads = {"c_ctx": g_c_ctx[0], "w_mod": g_w_mod[None], "b_mod": g_b_mod, "q_norm_g": g_qg, "kv_norm_g": g_kvg,
             "conv_w": g_cw[None], "final_norm_g": g_fg[0]}
    weights = {"c_ctx": c_ctx, "w_mod": w_mod, "b_mod": b_mod, "w_in": w_in, "q_norm_g": q_norm_g, "w_uq": w_uq,
               "kv_norm_g": kv_norm_g, "w_ukv": w_ukv, "conv_w": conv_w, "w_out": w_out, "w_mlp1": w_mlp1,
               "w_mlp2": w_mlp2, "final_norm_g": final_norm_g}
    m_in = {"c_ctx": m_c_ctx, "w_mod": m_w_mod, "b_mod": m_b_mod, "w_in": m_w_in, "q_norm_g": m_q_norm_g,
            "w_uq": m_w_uq, "kv_norm_g": m_kv_norm_g, "w_ukv": m_w_ukv, "conv_w": m_conv_w, "w_out": m_w_out,
            "w_mlp1": m_w_mlp1, "w_mlp2": m_w_mlp2, "final_norm_g": m_final_norm_g}
    v_in = {"c_ctx": v_c_ctx, "w_mod": v_w_mod, "b_mod": v_b_mod, "w_in": v_w_in, "q_norm_g": v_q_norm_g,
            "w_uq": v_w_uq, "kv_norm_g": v_kv_norm_g, "w_ukv": v_w_ukv, "conv_w": v_conv_w, "w_out": v_w_out,
            "w_mlp1": v_w_mlp1, "w_mlp2": v_w_mlp2, "final_norm_g": v_final_norm_g}
    names = list(weights)
    small_names = ["c_ctx", "b_mod", "q_norm_g", "kv_norm_g", "final_norm_g"]
    delta, new_m, new_v = {}, {}, {}

    def flat(a):
        return a.reshape(1, -1)

    packs = [jnp.concatenate([flat(src[n]) for n in small_names], axis=1) for src in (weights, grads, m_in, v_in)]
    small_out = _adamw(*packs, "adamw_small")
    off = 0
    for n in small_names:
        size = weights[n].size
        for dst, arr in zip((delta, new_m, new_v), small_out):
            dst[n] = arr[:, off:off + size].reshape(weights[n].shape)
        off += size
    for n in names:
        if n in small_names:
            continue
        shp = weights[n].shape
        two_d = (shp[0] * shp[1], shp[2])
        wmv = [a.reshape(two_d) for a in (weights[n], m_in[n], v_in[n])]
        if n in slots:
            outs = _adamw(wmv[0], slots[n], wmv[1], wmv[2], "adamw_" + n, slots=True)
            grads[n] = outs[0].reshape(shp)
            outs = outs[1:]
        else:
            outs = _adamw(wmv[0], grads[n].reshape(two_d), wmv[1], wmv[2], "adamw_" + n)
        delta[n], new_m[n], new_v[n] = [a.reshape(shp) for a in outs]

    return (loss, grad_x[None], *[grads[n] for n in names], *[delta[n] for n in names],
            *[new_m[n] for n in names], *[new_v[n] for n in names])
```

```python
import math

import jax
import jax.numpy as jnp
import numpy as np
from jax import lax
from jax.experimental import pallas as pl
from jax.experimental.pallas import tpu as pltpu

F32 = jnp.float32
BF16 = jnp.bfloat16

D_MODEL = 1024
GRID_W = 64
N_HEADS = 8
QK_NOPE = 64
QK_ROPE = 32
V_DIM = 64
Q_RANK = 256
KV_RANK = 128
MLA_IN = Q_RANK + KV_RANK + QK_ROPE
CONV_W = 512
HEAD_COLS = 512
D_FF = 4096
ROPE_THETA = 10000.0
EPS = 1e-6
ATTN_SCALE = 1.0 / math.sqrt(QK_NOPE + QK_ROPE)
LOG2_E = 1.0 / math.log(2.0)
EXP2_SCALE = ATTN_SCALE * LOG2_E
N_DEV = 8
LANES = 128

ADAM_LR, ADAM_B1, ADAM_B2, ADAM_EPS, ADAM_WD, ADAM_STEP = 0.001, 0.9, 0.999, 1e-08, 0.01, 10

ROW_TILE = 256
VMEM_BIG = 60 * 1024 * 1024


def _params(sem=None, vmem=None):
    return pltpu.CompilerParams(dimension_semantics=sem, vmem_limit_bytes=vmem)


def _pick(n, prefs):
    for p in prefs:
        if n % p == 0:
            return p
    return n


def _my_index():
    return 4 * lax.axis_index("x") + 2 * lax.axis_index("y") + lax.axis_index("c")


def _all_gather(arrays, name, in_vmem):
    space = pltpu.VMEM if in_vmem else pl.ANY
    n = len(arrays)

    def body(*refs):
        x_refs, out_refs = refs[:n], refs[n:2 * n]
        send_sems, recv_sems, local_sems = refs[2 * n:]
        x, y, c = lax.axis_index("x"), lax.axis_index("y"), lax.axis_index("c")
        me, sibling = (x, y, c), (x, y, 1 - c)
        chips = [(1 - x, y), (x, 1 - y), (1 - x, 1 - y)]

        def slot(a, px, py, pc):
            return out_refs[a].at[4 * px + 2 * py + pc]

        def copy(a, k, block, to, src=None):
            return pltpu.make_async_remote_copy(
                src_ref=slot(a, *block) if src is None else src, dst_ref=slot(a, *block),
                send_sem=send_sems.at[7 * a + k], recv_sem=recv_sems.at[7 * a + k],
                device_id=to, device_id_type=pl.DeviceIdType.MESH)

        mine = [pltpu.make_async_copy(x_refs[a], slot(a, *me), local_sems.at[a]) for a in range(n)]
        for cp in mine:
            cp.start()
        started = []
        for a in range(n):
            first = [copy(a, 0, me, sibling, src=x_refs[a])]
            first += [copy(a, 1 + j, me, (*chip, c), src=x_refs[a]) for j, chip in enumerate(chips)]
            for cp in first:
                cp.start()
            started += first
        for a in range(n):
            for j, chip in enumerate(chips):
                copy(a, 1 + j, (*chip, c), me).wait_recv()
                passed = copy(a, 4 + j, (*chip, c), sibling)
                passed.start()
                started.append(passed)
        for a in range(n):
            copy(a, 0, sibling, me).wait_recv()
            for j, chip in enumerate(chips):
                copy(a, 4 + j, (*chip, 1 - c), me).wait_recv()
        for cp in started:
            cp.wait_send()
        for cp in mine:
            cp.wait()

    outs = pl.pallas_call(
        body, name=name,
        out_shape=tuple(jax.ShapeDtypeStruct((N_DEV,) + a.shape, a.dtype) for a in arrays),
        in_specs=[pl.BlockSpec(memory_space=space)] * n,
        out_specs=tuple(pl.BlockSpec(memory_space=space) for _ in arrays),
        scratch_shapes=[pltpu.SemaphoreType.DMA((7 * n,)), pltpu.SemaphoreType.DMA((7 * n,)),
                        pltpu.SemaphoreType.DMA((n,))],
    )(*arrays)
    return list(outs)


def _all_to_all(arrays, name):
    n = len(arrays)

    def body(*refs):
        x_refs, y_refs = refs[:n], refs[n:2 * n]
        send_sems, recv_sems, local_sems = refs[2 * n:]
        x, y, c = lax.axis_index("x"), lax.axis_index("y"), lax.axis_index("c")
        me = 4 * x + 2 * y + c
        peers = []
        for k in range(1, N_DEV):
            peers.append((1 - x if k & 4 else x, 1 - y if k & 2 else y, 1 - c if k & 1 else c))

        def copy(a, k, peer, landing):
            pid = 4 * peer[0] + 2 * peer[1] + peer[2]
            return pltpu.make_async_remote_copy(
                src_ref=x_refs[a].at[pid], dst_ref=y_refs[a].at[pid if landing else me],
                send_sem=send_sems.at[7 * a + k], recv_sem=recv_sems.at[7 * a + k],
                device_id=peer, device_id_type=pl.DeviceIdType.MESH)

        mine = [pltpu.make_async_copy(x_refs[a].at[me], y_refs[a].at[me], local_sems.at[a]) for a in range(n)]
        for cp in mine:
            cp.start()
        sends = [copy(a, k, peer, False) for a in range(n) for k, peer in enumerate(peers)]
        for cp in sends:
            cp.start()
        for a in range(n):
            for k, peer in enumerate(peers):
                copy(a, k, peer, True).wait_recv()
        for cp in sends:
            cp.wait_send()
        for cp in mine:
            cp.wait()

    outs = pl.pallas_call(
        body, name=name,
        out_shape=tuple(jax.ShapeDtypeStruct(a.shape, a.dtype) for a in arrays),
        in_specs=[pl.BlockSpec(memory_space=pl.ANY)] * n,
        out_specs=tuple(pl.BlockSpec(memory_space=pl.ANY) for _ in arrays),
        scratch_shapes=[pltpu.SemaphoreType.DMA((7 * n,)), pltpu.SemaphoreType.DMA((7 * n,)),
                        pltpu.SemaphoreType.DMA((n,))],
    )(*arrays)
    return list(outs)


class _Riding:
    def __init__(self, kind, arrays):
        self.kind, self.arrays, self.n = kind, list(arrays), len(arrays)
        lead = (N_DEV,) if kind == "gather" else ()
        self.out_shape = [jax.ShapeDtypeStruct(lead + a.shape, a.dtype) for a in self.arrays]
        self.specs = [pl.BlockSpec(memory_space=pl.ANY)] * self.n
        self.scratch = [pltpu.SemaphoreType.DMA((7 * self.n,)), pltpu.SemaphoreType.DMA((7 * self.n,)),
                        pltpu.SemaphoreType.DMA((self.n,))]

    def copies(self, x_refs, y_refs, send_sems, recv_sems, local_sems):
        x, y, c = lax.axis_index("x"), lax.axis_index("y"), lax.axis_index("c")
        me = 4 * x + 2 * y + c
        local, sends, landings = [], [], []
        for a in range(self.n):
            src_mine = x_refs[a] if self.kind == "gather" else x_refs[a].at[me]
            local.append(pltpu.make_async_copy(src_mine, y_refs[a].at[me], local_sems.at[a]))
            for k in range(1, N_DEV):
                peer = (1 - x if k & 4 else x, 1 - y if k & 2 else y, 1 - c if k & 1 else c)
                pid = 4 * peer[0] + 2 * peer[1] + peer[2]
                src = x_refs[a] if self.kind == "gather" else x_refs[a].at[pid]
                for dst, out in ((me, sends), (pid, landings)):
                    out.append(pltpu.make_async_remote_copy(
                        src_ref=src, dst_ref=y_refs[a].at[dst],
                        send_sem=send_sems.at[7 * a + k - 1], recv_sem=recv_sems.at[7 * a + k - 1],
                        device_id=peer, device_id_type=pl.DeviceIdType.MESH))
        return local, sends, landings

    def run(self, first, last, x_refs, y_refs, sems):
        local, sends, landings = self.copies(x_refs, y_refs, *sems)

        @pl.when(first)
        def _():
            for cp in local + sends:
                cp.start()

        return local, sends, landings, last

    @staticmethod
    def finish(state):
        local, sends, landings, last = state

        @pl.when(last)
        def _():
            for cp in landings:
                cp.wait_recv()
            for cp in sends:
                cp.wait_send()
            for cp in local:
                cp.wait()


_DIMS = {"nn": (((1,), (0,)), ((), ())), "nt": (((1,), (1,)), ((), ())), "tn": (((0,), (0,)), ((), ()))}
NT_DIMS = _DIMS["nt"]
TN_DIMS = _DIMS["tn"]


def _swap8(x):
    lane = lax.broadcasted_iota(jnp.int32, x.shape, 1)
    return jnp.where((lane & 15) < 8, pltpu.roll(x, LANES - 8, 1), pltpu.roll(x, 8, 1))


def _rope(x, cos, sgn, bwd):
    return x * cos + (_swap8(x * sgn) if bwd else _swap8(x) * sgn)


def _matmul(a, b, *, mode, name, out_dtype=F32, tm=512, tn=512, tk=512, m=None, k=None,
            epilogue=None, extra=(), addend=None, slots=None):
    if mode == "nn":
        m = a.shape[0] if m is None else m
        k = a.shape[1]
        n = N_DEV * b.shape[2] if slots == "b_cols" else b.shape[1]
    elif mode == "nt":
        m = a.shape[0] if m is None else m
        k = a.shape[1]
        n = b.shape[1] if slots == "b_contract" else b.shape[0]
    else:
        k = a.shape[0] if k is None else k
        m, n = a.shape[1], b.shape[1]
    tm, tn, tk = min(tm, m), min(tn, n), min(tk, k)
    if slots == "b_cols":
        tn = b.shape[2]
    if slots == "b_contract":
        tk = b.shape[2]
    if slots == "out":
        tn = n // N_DEV
    assert m % tm == 0 and n % tn == 0 and k % tk == 0, (name, m, n, k, tm, tn, tk)
    nk = k // tk
    dims = _DIMS[mode]
    a_spec = (pl.BlockSpec((tk, tm), lambda i, j, kk: (kk, i)) if mode == "tn"
              else pl.BlockSpec((tm, tk), lambda i, j, kk: (i, kk)))
    if slots == "b_cols":
        b_spec = pl.BlockSpec((None, tk, tn), lambda i, j, kk: (j, kk, 0))
    elif slots == "b_contract":
        b_spec = pl.BlockSpec((None, tn, tk), lambda i, j, kk: (kk, j, 0))
    elif mode == "nt":
        b_spec = pl.BlockSpec((tn, tk), lambda i, j, kk: (j, kk))
    else:
        b_spec = pl.BlockSpec((tk, tn), lambda i, j, kk: (kk, j))
    tile = pl.BlockSpec((tm, tn), lambda i, j, kk: (i, j))
    if slots == "out":
        o_spec = pl.BlockSpec((None, tm, tn), lambda i, j, kk: (j, i, 0))
        o_shape = (N_DEV, m, tn)
    else:
        o_spec, o_shape = tile, (m, n)
    in_specs, args = [a_spec, b_spec], [a, b]
    if epilogue == "drelu2":
        in_specs.append(tile)
    elif epilogue == "rope":
        in_specs += [pl.BlockSpec((tm, LANES), lambda i, j, kk: (i, 0))] * 2
    args += list(extra)
    if addend is not None:
        in_specs.append(tile)
        args.append(addend)
    if epilogue == "relu2":
        out_shape = (jax.ShapeDtypeStruct(o_shape, BF16), jax.ShapeDtypeStruct(o_shape, BF16))
        out_specs = (o_spec, o_spec)
    else:
        out_shape = jax.ShapeDtypeStruct(o_shape, out_dtype)
        out_specs = o_spec
    n_in = len(args)
    n_out = 2 if epilogue == "relu2" else 1

    def body(*refs):
        a_ref, b_ref = refs[0], refs[1]
        outs = refs[n_in:n_in + n_out]
        part = lax.dot_general(a_ref[...], b_ref[...], dims, preferred_element_type=F32)

        def finish(acc):
            if addend is not None:
                acc = acc + refs[n_in - 1][...]
            if epilogue == "relu2":
                outs[0][...] = acc.astype(BF16)
                r = jnp.maximum(acc, 0.0)
                outs[1][...] = (r * r).astype(BF16)
            elif epilogue == "drelu2":
                u = refs[2][...].astype(F32)
                outs[0][...] = (acc * (2.0 * jnp.maximum(u, 0.0))).astype(out_dtype)
            elif epilogue == "rope":
                cos, sgn = refs[2][...], refs[3][...]
                for h in range(tn // LANES):
                    sl = slice(h * LANES, (h + 1) * LANES)
                    outs[0][:, sl] = _rope(acc[:, sl], cos, sgn, False).astype(out_dtype)
            else:
                outs[0][...] = acc.astype(out_dtype)

        if nk == 1:
            finish(part)
        else:
            acc_ref = refs[n_in + n_out]
            kk = pl.program_id(2)

            @pl.when(kk == 0)
            def _():
                acc_ref[...] = part

            @pl.when(kk > 0)
            def _():
                acc_ref[...] += part

            @pl.when(kk == nk - 1)
            def _():
                finish(acc_ref[...])

    return pl.pallas_call(
        body, name=name, grid=(m // tm, n // tn, nk),
        out_shape=out_shape, in_specs=in_specs, out_specs=out_specs,
        scratch_shapes=[pltpu.VMEM((tm, tn), F32)] if nk > 1 else [],
        compiler_params=_params(("parallel", "parallel", "arbitrary"), VMEM_BIG),
    )(*args)


def _rstd(x):
    return lax.rsqrt(jnp.mean(x * x, axis=1, keepdims=True) + EPS)


def _norm_bwd(dxn, xn, r):
    return r * (dxn - xn * jnp.mean(dxn * xn, axis=1, keepdims=True))


def _modulate_all(x, ctx, shift, scale, name):
    s, d = x.shape
    t = s + ctx.shape[0]
    ns = s // ROW_TILE
    nc = ctx.shape[0] // ROW_TILE

    def body(x_ref, c_ref, sh_ref, sc_ref, h_ref):
        def emit(v):
            h_ref[...] = (v * _rstd(v) * (1.0 + sc_ref[0]) + sh_ref[0]).astype(BF16)

        i = pl.program_id(0)

        @pl.when(i < ns)
        def _():
            emit(x_ref[...])

        @pl.when(i >= ns)
        def _():
            emit(c_ref[...])

    vec = pl.BlockSpec((1, 1, d), lambda i: (jnp.where(i < ns, 0, 1), 0, 0))
    return pl.pallas_call(
        body, name=name, grid=(ns + nc,),
        out_shape=jax.ShapeDtypeStruct((t, d), BF16),
        in_specs=[pl.BlockSpec((ROW_TILE, d), lambda i: (jnp.minimum(i, ns - 1), 0)),
                  pl.BlockSpec((ROW_TILE, d), lambda i: (jnp.maximum(i - ns, 0), 0)), vec, vec],
        out_specs=pl.BlockSpec((ROW_TILE, d), lambda i: (i, 0)),
        compiler_params=_params(("arbitrary",)),
    )(x, ctx, shift, scale)


def _resid_modulate(x, o, gate, shift, scale, name):
    s, d = x.shape

    def body(x_ref, o_ref, g_ref, sh_ref, sc_ref, x1_ref, h_ref):
        x1 = x_ref[...] + g_ref[...] * o_ref[...]
        x1_ref[...] = x1
        h_ref[...] = (x1 * _rstd(x1) * (1.0 + sc_ref[...]) + sh_ref[...]).astype(BF16)

    row = pl.BlockSpec((ROW_TILE, d), lambda i: (i, 0))
    vec = pl.BlockSpec((1, d), lambda i: (0, 0))
    return pl.pallas_call(
        body, name=name, grid=(s // ROW_TILE,),
        out_shape=(jax.ShapeDtypeStruct((s, d), F32), jax.ShapeDtypeStruct((s, d), BF16)),
        in_specs=[row, row, vec, vec, vec], out_specs=(row, row),
        compiler_params=_params(("parallel",)),
    )(x, o, gate, shift, scale)


def _final(x1, m, gate, gain, target, name):
    s, d = x1.shape
    n = s // ROW_TILE

    def body(x1_ref, m_ref, g_ref, gf_ref, t_ref, dx2_ref, dm_ref, sums_ref):
        i = pl.program_id(0)
        mm = m_ref[...]
        x2 = x1_ref[...] + g_ref[...] * mm
        r = _rstd(x2)
        xn = x2 * r
        err = xn * gf_ref[...] - t_ref[...]
        dy = err * (1.0 / d)
        dx2 = _norm_bwd(dy * gf_ref[...], xn, r)
        dx2_ref[...] = dx2
        dm_ref[...] = (dx2 * g_ref[...]).astype(BF16)

        @pl.when(i == 0)
        def _():
            sums_ref[...] = jnp.zeros_like(sums_ref)

        sums_ref[0:1, :] += jnp.sum(dy * xn, axis=0, keepdims=True)
        sums_ref[1:2, :] += jnp.sum(dx2 * mm, axis=0, keepdims=True)
        sums_ref[2:3, :] += jnp.sum(err * err, axis=0, keepdims=True)

        @pl.when(i == n - 1)
        def _():
            tot = jnp.sum(sums_ref[2:3, :], axis=1, keepdims=True) * (0.5 / d)
            sums_ref[3:4, :] = jnp.broadcast_to(tot, (1, d))

    row = pl.BlockSpec((ROW_TILE, d), lambda i: (i, 0))
    vec = pl.BlockSpec((1, d), lambda i: (0, 0))
    return pl.pallas_call(
        body, name=name, grid=(n,),
        out_shape=(jax.ShapeDtypeStruct((s, d), F32), jax.ShapeDtypeStruct((s, d), BF16),
                   jax.ShapeDtypeStruct((8, d), F32)),
        in_specs=[row, row, vec, vec, row],
        out_specs=(row, row, pl.BlockSpec((8, d), lambda i: (0, 0))),
        compiler_params=_params(("arbitrary",)),
    )(x1, m, gate, gain, target)


def _modulate_bwd(dh, row_off, xsrc, scale, name, dres=None, o=None, gate=None):
    s, d = xsrc.shape
    n = s // ROW_TILE
    has_dx, has_o = dres is not None, o is not None
    assert has_dx or not has_o

    def body(*refs):
        it = iter(refs)
        dh_ref, x_ref, sc_ref = next(it), next(it), next(it)
        dres_ref = next(it) if has_dx else None
        o_ref, g_ref = (next(it), next(it)) if has_o else (None, None)
        dx_ref = next(it) if has_dx else None
        do_ref = next(it) if has_o else None
        sums_ref = next(it)
        i = pl.program_id(0)
        x = x_ref[...]
        r = _rstd(x)
        xn = x * r
        dhv = dh_ref[...]

        @pl.when(i == 0)
        def _():
            sums_ref[...] = jnp.zeros_like(sums_ref)

        sums_ref[0:1, :] += jnp.sum(dhv * xn, axis=0, keepdims=True)
        sums_ref[1:2, :] += jnp.sum(dhv, axis=0, keepdims=True)
        if has_dx:
            dx = dres_ref[...] + _norm_bwd(dhv * (1.0 + sc_ref[...]), xn, r)
            dx_ref[...] = dx
            if has_o:
                do_ref[...] = (dx * g_ref[...]).astype(BF16)
                sums_ref[2:3, :] += jnp.sum(dx * o_ref[...], axis=0, keepdims=True)

    row = pl.BlockSpec((ROW_TILE, d), lambda i: (i, 0))
    vec = pl.BlockSpec((1, d), lambda i: (0, 0))
    in_specs = [pl.BlockSpec((ROW_TILE, d), lambda i: (i + row_off, 0)), row, vec]
    args = [dh, xsrc, scale]
    out_shape, out_specs = [], []
    if has_dx:
        in_specs.append(row)
        args.append(dres)
        out_shape.append(jax.ShapeDtypeStruct((s, d), F32))
        out_specs.append(row)
    if has_o:
        in_specs += [row, vec]
        args += [o, gate]
        out_shape.append(jax.ShapeDtypeStruct((s, d), BF16))
        out_specs.append(row)
    out_shape.append(jax.ShapeDtypeStruct((8, d), F32))
    out_specs.append(pl.BlockSpec((8, d), lambda i: (0, 0)))
    return pl.pallas_call(
        body, name=name, grid=(n,),
        out_shape=tuple(out_shape), in_specs=in_specs, out_specs=tuple(out_specs),
        compiler_params=_params(("arbitrary",)),
    )(*args)


def _qkv_prep(z, q_gain, kv_gain, cos, sgn, name):
    t = z.shape[0]

    def body(z_ref, qg_ref, kg_ref, c_ref, s_ref, cq_ref, kv_ref):
        zq = z_ref[:, 0:Q_RANK]
        cq_ref[...] = (zq * _rstd(zq) * qg_ref[...]).astype(BF16)
        zk = z_ref[:, Q_RANK:Q_RANK + KV_RANK]
        kv_ref[:, 0:KV_RANK] = (zk * _rstd(zk) * kg_ref[...]).astype(BF16)
        kr = z_ref[:, Q_RANK + KV_RANK:HEAD_COLS]
        kv_ref[:, KV_RANK:KV_RANK + LANES] = _rope(kr, c_ref[...], s_ref[...], False).astype(BF16)

    tab = pl.BlockSpec((ROW_TILE, LANES), lambda i: (i, 0))
    return pl.pallas_call(
        body, name=name, grid=(t // ROW_TILE,),
        out_shape=(jax.ShapeDtypeStruct((t, Q_RANK), BF16), jax.ShapeDtypeStruct((t, KV_RANK + LANES), BF16)),
        in_specs=[pl.BlockSpec((ROW_TILE, HEAD_COLS), lambda i: (i, 0)),
                  pl.BlockSpec((1, Q_RANK), lambda i: (0, 0)), pl.BlockSpec((1, KV_RANK), lambda i: (0, 0)), tab, tab],
        out_specs=(pl.BlockSpec((ROW_TILE, Q_RANK), lambda i: (i, 0)),
                   pl.BlockSpec((ROW_TILE, KV_RANK + LANES), lambda i: (i, 0))),
        compiler_params=_params(("parallel",)),
    )(z, q_gain, kv_gain, cos, sgn)


def _qkv_prep_bwd(z, dcq, dkv, q_gain, kv_gain, cos, sgn, s, name):
    t = z.shape[0]
    ns = s // ROW_TILE

    def body(z_ref, dcq_ref, dkv_ref, qg_ref, kg_ref, c_ref, s_ref, dz_ref, sums_ref):
        i = pl.program_id(0)

        @pl.when(i == 0)
        def _():
            sums_ref[...] = jnp.zeros_like(sums_ref)

        @pl.when(i < ns)
        def _():
            zq = z_ref[:, 0:Q_RANK]
            r = _rstd(zq)
            zn = zq * r
            dc = dcq_ref[...]
            sums_ref[0:1, :] += jnp.sum(dc * zn, axis=0, keepdims=True)
            dz_ref[:, 0:Q_RANK] = _norm_bwd(dc * qg_ref[...], zn, r).astype(BF16)

        @pl.when(i >= ns)
        def _():
            dz_ref[:, 0:Q_RANK] = jnp.zeros((ROW_TILE, Q_RANK), BF16)

        zk = z_ref[:, Q_RANK:Q_RANK + KV_RANK]
        r = _rstd(zk)
        zn = zk * r
        dc = dkv_ref[:, 0:KV_RANK]
        sums_ref[1:2, 0:KV_RANK] += jnp.sum(dc * zn, axis=0, keepdims=True)
        dz_ref[:, Q_RANK:Q_RANK + KV_RANK] = _norm_bwd(dc * kg_ref[...], zn, r).astype(BF16)
        dkr = dkv_ref[:, KV_RANK:KV_RANK + LANES]
        dz_ref[:, Q_RANK + KV_RANK:HEAD_COLS] = _rope(dkr, c_ref[...], s_ref[...], True).astype(BF16)

    tab = pl.BlockSpec((ROW_TILE, LANES), lambda i: (i, 0))
    return pl.pallas_call(
        body, name=name, grid=(t // ROW_TILE,),
        out_shape=(jax.ShapeDtypeStruct((t, HEAD_COLS), BF16), jax.ShapeDtypeStruct((8, Q_RANK), F32)),
        in_specs=[pl.BlockSpec((ROW_TILE, HEAD_COLS), lambda i: (i, 0)),
                  pl.BlockSpec((ROW_TILE, Q_RANK), lambda i: (jnp.minimum(i, ns - 1), 0)),
                  pl.BlockSpec((ROW_TILE, KV_RANK + LANES), lambda i: (i, 0)),
                  pl.BlockSpec((1, Q_RANK), lambda i: (0, 0)), pl.BlockSpec((1, KV_RANK), lambda i: (0, 0)), tab, tab],
        out_specs=(pl.BlockSpec((ROW_TILE, HEAD_COLS), lambda i: (i, 0)), pl.BlockSpec((8, Q_RANK), lambda i: (0, 0))),
        compiler_params=_params(("arbitrary",)),
    )(z, dcq, dkv, q_gain, kv_gain, cos, sgn)


def _shift_rows(u, s):
    rowi = lax.broadcasted_iota(jnp.int32, u.shape, 0)
    prev = jnp.where(rowi == 0, 0.0, pltpu.roll(u, 1, 0))
    nxt = jnp.where(rowi == s - 1, 0.0, pltpu.roll(u, s - 1, 0))
    return prev, nxt


def _conv_fwd(z_conv, cw, a_cat, name):
    s = z_conv.shape[0]

    def body(z_ref, w_ref, a_in_ref, o_ref):
        del a_in_ref
        gb, gc, xv = z_ref[:, 0:LANES], z_ref[:, LANES:2 * LANES], z_ref[:, 2 * LANES:3 * LANES]
        u = gc * xv
        prev, nxt = _shift_rows(u, s)
        y = w_ref[0:1, :] * prev + w_ref[1:2, :] * u + w_ref[2:3, :] * nxt
        o_ref[...] = (gb * y).astype(BF16)

    return pl.pallas_call(
        body, name=name, grid=(CONV_W // LANES,),
        out_shape=jax.ShapeDtypeStruct(a_cat.shape, a_cat.dtype),
        in_specs=[pl.BlockSpec((s, 3 * LANES), lambda j: (0, j)), pl.BlockSpec((3, LANES), lambda j: (0, j)),
                  pl.BlockSpec(memory_space=pl.ANY)],
        out_specs=pl.BlockSpec((s, LANES), lambda j: (0, 4 + j)),
        input_output_aliases={2: 0},
        compiler_params=_params(("parallel",), VMEM_BIG),
    )(z_conv, cw, a_cat)


def _conv_bwd(z_conv, cw, da, name):
    s = z_conv.shape[0]

    def body(z_ref, w_ref, da_ref, dz_ref, dw_ref):
        gb, gc, xv = z_ref[:, 0:LANES], z_ref[:, LANES:2 * LANES], z_ref[:, 2 * LANES:3 * LANES]
        u = gc * xv
        prev, nxt = _shift_rows(u, s)
        dcv = da_ref[...]
        dz_ref[:, 0:LANES] = (dcv * (w_ref[0:1, :] * prev + w_ref[1:2, :] * u + w_ref[2:3, :] * nxt)).astype(BF16)
        dy = dcv * gb
        dw_ref[0:1, :] = jnp.sum(dy * prev, axis=0, keepdims=True)
        dw_ref[1:2, :] = jnp.sum(dy * u, axis=0, keepdims=True)
        dw_ref[2:3, :] = jnp.sum(dy * nxt, axis=0, keepdims=True)
        dyp, dyn = _shift_rows(dy, s)
        du = w_ref[0:1, :] * dyn + w_ref[1:2, :] * dy + w_ref[2:3, :] * dyp
        dz_ref[:, LANES:2 * LANES] = (du * xv).astype(BF16)
        dz_ref[:, 2 * LANES:3 * LANES] = (du * gc).astype(BF16)

    blk = pl.BlockSpec((s, 3 * LANES), lambda j: (0, j))
    cws = pl.BlockSpec((3, LANES), lambda j: (0, j))
    return pl.pallas_call(
        body, name=name, grid=(CONV_W // LANES,),
        out_shape=(jax.ShapeDtypeStruct(z_conv.shape, BF16), jax.ShapeDtypeStruct((3, CONV_W), F32)),
        in_specs=[blk, cws, pl.BlockSpec((s, LANES), lambda j: (0, 4 + j))], out_specs=(blk, cws),
        compiler_params=_params(("parallel",), VMEM_BIG),
    )(z_conv, cw, da)


ATT_TQ = 256


def _head_mask(shape, hh):
    lane = lax.broadcasted_iota(jnp.int32, shape, 1)
    return (lane >= hh * V_DIM) & (lane < (hh + 1) * V_DIM)


def _attn_fwd(qf, kv, riding, name):
    s, t = qf.shape[0], kv.shape[0]
    nq = s // ATT_TQ
    nr = riding.n

    def body(*refs):
        q_ref, k_ref, v_ref = refs[:3]
        o_ref, ob_ref, st_ref = refs[3 + nr:6 + nr]
        p, i = pl.program_id(0), pl.program_id(1)
        state = riding.run((p == 0) & (i == 0), (p == N_HEADS // 2 - 1) & (i == nq - 1),
                           refs[3:3 + nr], refs[6 + nr:6 + 2 * nr], refs[6 + 2 * nr:])
        v = v_ref[...]
        vlane = lax.broadcasted_iota(jnp.int32, v.shape, 1)
        olane = lax.broadcasted_iota(jnp.int32, (ATT_TQ, LANES), 1)
        acc = jnp.zeros((ATT_TQ, LANES), F32)
        stat = jnp.zeros((ATT_TQ, LANES), F32)
        for hh in range(2):
            sl = slice(hh * LANES, (hh + 1) * LANES)
            sc = lax.dot_general(q_ref[:, sl], k_ref[:, sl], NT_DIMS, preferred_element_type=F32)
            mx = jnp.max(sc, axis=1, keepdims=True)
            e = jnp.exp2((sc - mx) * EXP2_SCALE).astype(BF16)
            one_lane = (1 - hh) * V_DIM
            vm = jnp.where(_head_mask(v.shape, hh), v, jnp.where(vlane == one_lane, 1.0, 0.0).astype(BF16))
            r = jnp.dot(e, vm, preferred_element_type=F32)
            den = jnp.sum(jnp.where(olane == one_lane, r, 0.0), axis=1, keepdims=True)
            acc = acc + jnp.where(_head_mask(r.shape, hh), r * (1.0 / den), 0.0)
            stat = stat + jnp.where(olane == hh, mx * EXP2_SCALE + jnp.log(den) * LOG2_E, 0.0)
        o_ref[...] = acc
        ob_ref[...] = acc.astype(BF16)
        st_ref[...] = stat.T[0:8, :]
        riding.finish(state)

    o_spec = pl.BlockSpec((ATT_TQ, LANES), lambda p, i: (i, p))
    outs = pl.pallas_call(
        body, name=name, grid=(N_HEADS // 2, nq),
        out_shape=(jax.ShapeDtypeStruct((s, N_HEADS * V_DIM), F32),
                   jax.ShapeDtypeStruct((s, D_MODEL), BF16),
                   jax.ShapeDtypeStruct((N_HEADS // 2 * 8, s), F32), *riding.out_shape),
        in_specs=[pl.BlockSpec((ATT_TQ, 2 * LANES), lambda p, i: (i, p)),
                  pl.BlockSpec((t, 2 * LANES), lambda p, i: (0, p)),
                  pl.BlockSpec((t, LANES), lambda p, i: (0, N_HEADS + p)), *riding.specs],
        out_specs=(o_spec, o_spec, pl.BlockSpec((8, ATT_TQ), lambda p, i: (p, i)), *riding.specs),
        scratch_shapes=riding.scratch,
        compiler_params=_params(("arbitrary", "arbitrary"), VMEM_BIG),
    )(qf, kv, kv, *riding.arrays)
    return outs[0], outs[1], outs[2], list(outs[3:])


def _attn_bwd(qf, kv, o, da, stats, cos, sgn, riding, name):
    s, t = qf.shape[0], kv.shape[0]
    nq = s // ATT_TQ
    nr = riding.n

    def body(*refs):
        q_ref, k_ref, v_ref, o_ref, do_ref, st_ref, c_ref, s_ref = refs[:8]
        dq_ref, dk_ref, dv_ref = refs[8 + nr:11 + nr]
        dk_acc, dv_acc = refs[11 + 2 * nr:13 + 2 * nr]
        p, i = pl.program_id(0), pl.program_id(1)
        state = riding.run((p == 0) & (i == 0), (p == N_HEADS // 2 - 1) & (i == nq - 1),
                           refs[8:8 + nr], refs[11 + nr:11 + 2 * nr], refs[13 + 2 * nr:])

        @pl.when(i == 0)
        def _():
            dk_acc[...] = jnp.zeros_like(dk_acc)
            dv_acc[...] = jnp.zeros_like(dv_acc)

        v = v_ref[...]
        do = do_ref[...]
        od = do * o_ref[...]
        ones = jnp.ones((8, LANES), F32)
        for hh in range(2):
            sl = slice(hh * LANES, (hh + 1) * LANES)
            q, k = q_ref[:, sl], k_ref[:, sl]
            mask = _head_mask(do.shape, hh)
            st = lax.dot_general(k, q, NT_DIMS, preferred_element_type=F32)
            pt = jnp.exp2(st * EXP2_SCALE - st_ref[hh:hh + 1, :])
            dom = jnp.where(mask, do, 0.0).astype(BF16)
            dpt = lax.dot_general(v, dom, NT_DIMS, preferred_element_type=F32)
            delta = lax.dot_general(ones, jnp.where(mask, od, 0.0), NT_DIMS, preferred_element_type=F32,
                                    precision=lax.Precision.HIGHEST)[0:1, :]
            dst = (pt * (dpt - delta)).astype(BF16)
            dv_acc[...] += jnp.dot(pt.astype(BF16), dom, preferred_element_type=F32)
            dk_acc[:, sl] += jnp.dot(dst, q, preferred_element_type=F32)
            dq = lax.dot_general(dst, k, TN_DIMS, preferred_element_type=F32) * ATTN_SCALE
            dq_ref[:, sl] = _rope(dq, c_ref[...], s_ref[...], True).astype(BF16)

        @pl.when(i == nq - 1)
        def _():
            dk_ref[...] = (dk_acc[...] * ATTN_SCALE).astype(BF16)
            dv_ref[...] = dv_acc[...].astype(BF16)

        riding.finish(state)

    o_spec = pl.BlockSpec((ATT_TQ, LANES), lambda p, i: (i, p))
    tab = pl.BlockSpec((ATT_TQ, LANES), lambda p, i: (i, 0))
    outs = pl.pallas_call(
        body, name=name, grid=(N_HEADS // 2, nq),
        out_shape=(jax.ShapeDtypeStruct((s, N_HEADS * LANES), BF16),
                   jax.ShapeDtypeStruct((t, N_HEADS * LANES), BF16),
                   jax.ShapeDtypeStruct((t, N_HEADS * V_DIM), BF16), *riding.out_shape),
        in_specs=[pl.BlockSpec((ATT_TQ, 2 * LANES), lambda p, i: (i, p)),
                  pl.BlockSpec((t, 2 * LANES), lambda p, i: (0, p)),
                  pl.BlockSpec((t, LANES), lambda p, i: (0, N_HEADS + p)),
                  o_spec, o_spec,
                  pl.BlockSpec((8, ATT_TQ), lambda p, i: (p, i)), tab, tab, *riding.specs],
        out_specs=(pl.BlockSpec((ATT_TQ, 2 * LANES), lambda p, i: (i, p)),
                   pl.BlockSpec((t, 2 * LANES), lambda p, i: (0, p)),
                   pl.BlockSpec((t, LANES), lambda p, i: (0, p)), *riding.specs),
        scratch_shapes=[pltpu.VMEM((t, 2 * LANES), F32), pltpu.VMEM((t, LANES), F32), *riding.scratch],
        compiler_params=_params(("arbitrary", "arbitrary"), VMEM_BIG),
    )(qf, kv, kv, o, da, stats, cos, sgn, *riding.arrays)
    return outs[0], outs[1], outs[2], list(outs[3:])


def _silu(x):
    return x * (1.0 / (1.0 + jnp.exp(-x)))


def _adaln_fwd(a, w, b, name):
    def body(a_ref, w_ref, b_ref, o_ref):
        o_ref[...] = jnp.dot(_silu(a_ref[...]), w_ref[...], preferred_element_type=F32,
                             precision=lax.Precision.HIGHEST) + b_ref[...]

    return pl.pallas_call(
        body, name=name, out_shape=jax.ShapeDtypeStruct((a.shape[0], w.shape[1]), F32),
        compiler_params=_params(None, VMEM_BIG),
    )(a, w, b)


def _adaln_bwd(a_t, w, d_ex, d_ctx, d_all, name):
    def body(at_ref, w_ref, dex_ref, dctx_ref, dall_ref, gw_ref, dsil_ref, dsum_ref):
        sil_t = _silu(at_ref[...])
        dctx = dctx_ref[...]
        row = dctx[0:1, :]
        for j in range(1, N_DEV):
            row = row + dctx[j:j + 1, :]
        rowi = lax.broadcasted_iota(jnp.int32, dctx.shape, 0)
        ctx_rows = jnp.where(rowi == 0, jnp.broadcast_to(row, dctx.shape), 0.0)
        hi = lax.Precision.HIGHEST
        d_rows = jnp.concatenate([dex_ref[...], ctx_rows], axis=0)
        gw_ref[...] = jnp.dot(sil_t, d_rows, preferred_element_type=F32, precision=hi)
        dsil_ref[...] = lax.dot_general(ctx_rows, w_ref[...], NT_DIMS, preferred_element_type=F32, precision=hi)
        tot = dall_ref[0]
        for j in range(1, N_DEV):
            tot = tot + dall_ref[j]
        dsum_ref[...] = tot

    return pl.pallas_call(
        body, name=name,
        out_shape=(jax.ShapeDtypeStruct(w.shape, F32), jax.ShapeDtypeStruct((8, w.shape[0]), F32),
                   jax.ShapeDtypeStruct(d_all.shape[1:], F32)),
        compiler_params=_params(None, VMEM_BIG),
    )(a_t, w, d_ex, d_ctx, d_all)


def _cctx_grad(parts, c_ctx, name):
    def body(p_ref, c_ref, o_ref):
        tot = p_ref[0]
        for j in range(1, N_DEV):
            tot = tot + p_ref[j]
        cv = c_ref[...]
        sg = 1.0 / (1.0 + jnp.exp(-cv))
        o_ref[...] = tot[0:1, :] * (sg * (1.0 + cv * (1.0 - sg)))

    return pl.pallas_call(body, name=name, out_shape=jax.ShapeDtypeStruct(c_ctx.shape, F32))(parts, c_ctx)


def _adamw(w, g, m, v, name, slots=False):
    rows, cols = w.shape
    tr = _pick(rows, (256, 128, 64, 32, 16, 8))

    def body(w_ref, g_ref, m_ref, v_ref, *outs):
        if slots:
            gv = g_ref[0].astype(F32)
            for j in range(1, N_DEV):
                gv = gv + g_ref[j].astype(F32)
            outs[0][...] = gv
        else:
            gv = g_ref[...]
        d_ref, nm_ref, nv_ref = outs[-3:]
        nm = ADAM_B1 * m_ref[...] + (1.0 - ADAM_B1) * gv
        nv = ADAM_B2 * v_ref[...] + (1.0 - ADAM_B2) * (gv * gv)
        m_hat = nm / (1.0 - ADAM_B1 ** ADAM_STEP)
        v_hat = nv / (1.0 - ADAM_B2 ** ADAM_STEP)
        d_ref[...] = -ADAM_LR * (m_hat / (jnp.sqrt(v_hat) + ADAM_EPS) + ADAM_WD * w_ref[...])
        nm_ref[...] = nm
        nv_ref[...] = nv

    blk = pl.BlockSpec((tr, cols), lambda i: (i, 0))
    g_spec = pl.BlockSpec((N_DEV, tr, cols), lambda i: (0, i, 0)) if slots else blk
    sh = jax.ShapeDtypeStruct((rows, cols), F32)
    n_out = 4 if slots else 3
    return pl.pallas_call(
        body, name=name, grid=(rows // tr,), out_shape=(sh,) * n_out,
        in_specs=[blk, g_spec, blk, blk], out_specs=(blk,) * n_out,
        compiler_params=_params(("parallel",)),
    )(w, g, m, v)


def _rope_tables(s, l):
    tok = np.arange(s)
    row = (tok // GRID_W).astype(np.float32)
    col = (tok % GRID_W).astype(np.float32)
    half = QK_ROPE // 2
    freqs = np.float32(ROPE_THETA) ** (-np.arange(0, half, 2, dtype=np.float32) / np.float32(half))
    dd = np.arange(QK_ROPE)
    pos = np.where((dd // half)[None, :] == 0, row[:, None], col[:, None]).astype(np.float32)
    ang = (pos * freqs[dd % (half // 2)][None, :]).astype(np.float32)
    sin = np.sin(ang).astype(np.float32)
    cos_t = np.ones((s + l, LANES), np.float32)
    sgn_t = np.zeros((s + l, LANES), np.float32)
    cos_t[:s, QK_NOPE:QK_NOPE + QK_ROPE] = np.cos(ang)
    sgn_t[:s, QK_NOPE:QK_NOPE + QK_ROPE] = np.where(((dd % half) // (half // 2))[None, :] == 0, -sin, sin)
    return jnp.asarray(cos_t), jnp.asarray(sgn_t)


def _slots_to_cols(g):
    return g.transpose(1, 0, 2).reshape(g.shape[1], N_DEV * g.shape[2])


def _cols_to_slots(w):
    return w.reshape(w.shape[0], N_DEV, w.shape[1] // N_DEV).transpose(1, 0, 2)


def _unpack_small_weights(g_in, g_uq, g_ukv):
    w_in = _slots_to_cols(g_in)
    zeros = jnp.zeros((D_MODEL, QK_NOPE), BF16)
    win_head = jnp.concatenate([w_in[:, :Q_RANK + KV_RANK], zeros, w_in[:, Q_RANK + KV_RANK:MLA_IN],
                                zeros[:, :LANES - QK_NOPE - QK_ROPE]], axis=1)
    win_conv = w_in[:, MLA_IN:].reshape(D_MODEL, 3, CONV_W // LANES, LANES).transpose(0, 2, 1, 3)
    win_conv = win_conv.reshape(D_MODEL, 3 * CONV_W)
    w_uq = _slots_to_cols(g_uq).reshape(Q_RANK, N_HEADS, QK_NOPE + QK_ROPE)
    wq = jnp.pad(w_uq, ((0, 0), (0, 0), (0, LANES - QK_NOPE - QK_ROPE))).reshape(Q_RANK, N_HEADS * LANES)
    w_ukv = _slots_to_cols(g_ukv).reshape(KV_RANK, N_HEADS, QK_NOPE + V_DIM)
    k_top = jnp.pad(w_ukv[:, :, :QK_NOPE], ((0, 0), (0, 0), (0, LANES - QK_NOPE))).reshape(KV_RANK, N_HEADS * LANES)
    v_top = w_ukv[:, :, QK_NOPE:].reshape(KV_RANK, N_HEADS * V_DIM)
    eye = jnp.pad(jnp.eye(QK_ROPE, dtype=BF16), ((QK_NOPE, LANES - QK_NOPE - QK_ROPE),) * 2)
    wk = jnp.concatenate([
        jnp.concatenate([k_top, v_top], axis=1),
        jnp.concatenate([jnp.tile(eye, (1, N_HEADS)), jnp.zeros((LANES, N_HEADS * V_DIM), BF16)], axis=1)], axis=0)
    return win_head, win_conv, wq, wk


def _pack_small_grads(d_head, d_conv, d_wq, d_wkk, d_wkv):
    d_conv = d_conv.reshape(D_MODEL, CONV_W // LANES, 3, LANES).transpose(0, 2, 1, 3).reshape(D_MODEL, 3 * CONV_W)
    g_in = jnp.concatenate([d_head[:, :Q_RANK + KV_RANK],
                            d_head[:, Q_RANK + KV_RANK + QK_NOPE:Q_RANK + KV_RANK + QK_NOPE + QK_ROPE], d_conv], axis=1)
    g_uq = d_wq.reshape(Q_RANK, N_HEADS, LANES)[:, :, :QK_NOPE + QK_ROPE].reshape(Q_RANK, -1)
    g_kn = d_wkk[:KV_RANK].reshape(KV_RANK, N_HEADS, LANES)[:, :, :QK_NOPE]
    g_v = d_wkv[:KV_RANK].reshape(KV_RANK, N_HEADS, V_DIM)
    g_ukv = jnp.concatenate([g_kn, g_v], axis=2).reshape(KV_RANK, -1)
    return [_cols_to_slots(g).astype(BF16) for g in (g_in, g_uq, g_ukv)]


def kernel(x, c, ctx, c_ctx, w_mod, b_mod, w_in, q_norm_g, w_uq, kv_norm_g, w_ukv, conv_w, w_out, w_mlp1, w_mlp2, final_norm_g, loss_target, m_c_ctx, m_w_mod, m_b_mod, m_w_in, m_q_norm_g, m_w_uq, m_kv_norm_g, m_w_ukv, m_conv_w, m_w_out, m_w_mlp1, m_w_mlp2, m_final_norm_g, v_c_ctx, v_w_mod, v_b_mod, v_w_in, v_q_norm_g, v_w_uq, v_kv_norm_g, v_w_ukv, v_conv_w, v_w_out, v_w_mlp1, v_w_mlp2, v_final_norm_g):
    me = _my_index()
    x2d, ctx2d, tgt = x[0], ctx[0], loss_target[0]
    s, l = x2d.shape[0], ctx2d.shape[0]
    t = s + l
    d = D_MODEL
    mod_cols = w_mod.shape[2]
    cw_cols = conv_w.shape[2]

    (c_all,) = _all_gather([jnp.pad(c, ((0, 7), (0, 0)))], "gather_c", True)
    a_rows = jnp.concatenate([c_all[:, 0, :], c_ctx[None, :], jnp.zeros((7, d), F32)], axis=0)
    b_cols = lax.dynamic_slice(b_mod, (0, me * mod_cols), (1, mod_cols))
    mod_cols_all = _adaln_fwd(a_rows, w_mod[0], b_cols, "adaln_fwd")
    cw_blk = jnp.pad(conv_w[0], ((0, 5), (0, mod_cols - cw_cols)))
    (gathered,) = _all_gather([jnp.concatenate([mod_cols_all, cw_blk], axis=0)], "gather_mod", True)
    mod_mine = lax.dynamic_index_in_dim(gathered, me, axis=1, keepdims=False).reshape(1, 6 * d)
    mod_ctx = gathered[:, 8, :].reshape(1, 6 * d)
    cw_full = gathered[:, 16:19, :cw_cols].transpose(1, 0, 2).reshape(3, CONV_W)
    sh1, sc1, g1, sh2, sc2, g2 = [mod_mine[:, i * d:(i + 1) * d] for i in range(6)]
    sh1c, sc1c = mod_ctx[:, 0:d], mod_ctx[:, d:2 * d]

    early = [w.astype(BF16) for w in (w_in[0], w_uq[0], w_ukv[0])]
    late = [w.astype(BF16) for w in (w_out[0], w_mlp1[0], w_mlp2[0])]
    g_in, g_uq, g_ukv = _all_gather(early, "gather_weights", False)
    win_head, win_conv, wq, wk = _unpack_small_weights(g_in, g_uq, g_ukv)
    wk_k, wk_v = wk[:, :N_HEADS * LANES], wk[:, N_HEADS * LANES:]
    cos, sgn = _rope_tables(s, l)

    h_all = _modulate_all(x2d, ctx2d, jnp.stack([sh1, sh1c]), jnp.stack([sc1, sc1c]), "modulate1")
    tm_t = _pick(t, (1088, 768, 256))
    tk_t = _pick(t, (2176, 768, 256))
    z_head = _matmul(h_all, win_head, mode="nn", name="in_proj_head", tm=tm_t, tn=512, tk=1024)
    z_conv = _matmul(h_all, win_conv, mode="nn", name="in_proj_conv", m=s, tm=1024, tn=1536, tk=1024)
    cq, kv_in = _qkv_prep(z_head, q_norm_g, kv_norm_g, cos, sgn, "qkv_prep")
    qf = _matmul(cq, wq, mode="nn", name="q_up", out_dtype=BF16, m=s, tm=1024, tn=1024, tk=256,
                 epilogue="rope", extra=(cos, sgn))
    kv = _matmul(kv_in, wk, mode="nn", name="kv_up", out_dtype=BF16, tm=tm_t, tn=1536, tk=256)
    attn, a_cat, stats, (g_out, w1, g_w2) = _attn_fwd(qf, kv, _Riding("gather", late), "attn_fwd")
    wo = g_out.reshape(d, d)
    w2 = g_w2.reshape(D_FF, d)
    a_cat = _conv_fwd(z_conv, cw_full, a_cat, "conv_fwd")
    o = _matmul(a_cat, wo, mode="nn", name="out_proj", tm=1024, tn=1024, tk=1024)
    x1, h2 = _resid_modulate(x2d, o, g1, sh2, sc2, "resid_modulate2")
    u1, act = _matmul(h2, w1, mode="nn", name="mlp_up", tm=2048, tk=1024, epilogue="relu2", slots="b_cols")
    mlp = _matmul(act, w2, mode="nn", name="mlp_down", tm=1024, tn=1024, tk=4096)
    dx2, dm, fsums = _final(x1, mlp, g2, final_norm_g[None, :], tgt, "final_loss")

    d_w2 = _matmul(act, dm, mode="tn", name="d_w_mlp2", out_dtype=BF16, tm=2048, tn=1024, tk=1024)
    du1 = _matmul(dm, w2, mode="nt", name="d_act", out_dtype=BF16, tm=2048, tn=1024, tk=1024,
                  epilogue="drelu2", extra=(u1,))
    d_w1 = _matmul(h2, du1, mode="tn", name="d_w_mlp1", out_dtype=BF16, tm=1024, tk=4096, slots="out")
    dh2 = _matmul(du1, w1, mode="nt", name="d_h2", tm=2048, tn=1024, slots="b_contract")
    dx1, do, sums2 = _modulate_bwd(dh2, 0, x1, sc2, "modulate2_bwd", dres=dx2, o=o, gate=g1)
    d_wo = _matmul(a_cat, do, mode="tn", name="d_w_out", out_dtype=BF16, tm=1024, tn=1024, tk=2048)
    da = _matmul(do, wo, mode="nt", name="d_a", tm=1024, tn=1024, tk=1024)
    dz_conv, d_cw = _conv_bwd(z_conv, cw_full, da, "conv_bwd")
    ready = [d_wo.reshape(N_DEV, d // N_DEV, d), d_w1, d_w2.reshape(N_DEV, D_FF // N_DEV, d)]
    dq, dk, dv, rode = _attn_bwd(qf, kv, attn, da, stats, cos, sgn, _Riding("exchange", ready), "attn_bwd")
    d_wq = _matmul(cq, dq, mode="tn", name="d_w_uq", k=s, tm=256, tn=1024, tk=4096)
    dcq = _matmul(dq, wq, mode="nt", name="d_cq", tm=1024, tn=256, tk=1024)
    d_wkk = _matmul(kv_in, dk, mode="tn", name="d_w_ukv_k", tm=256, tn=1024, tk=tk_t)
    d_wkv = _matmul(kv_in, dv, mode="tn", name="d_w_ukv_v", tm=256, tn=512, tk=tk_t)
    dkv_in = _matmul(dk, wk_k, mode="nt", name="d_kv_in_k", tm=tm_t, tn=256, tk=1024)
    dkv_in = _matmul(dv, wk_v, mode="nt", name="d_kv_in_v", tm=tm_t, tn=256, tk=512, addend=dkv_in)
    dz_head, psums = _qkv_prep_bwd(z_head, dcq, dkv_in, q_norm_g, kv_norm_g, cos, sgn, s, "qkv_prep_bwd")
    d_head = _matmul(h_all, dz_head, mode="tn", name="d_w_in_head", tm=1024, tn=512, tk=tk_t)
    d_conv = _matmul(h_all, dz_conv, mode="tn", name="d_w_in_conv", k=s, tm=1024, tn=1536, tk=2048)
    dh_head = _matmul(dz_head, win_head, mode="nt", name="d_h1_head", tm=tm_t, tn=1024, tk=512)
    dh = _matmul(dz_conv, win_conv, mode="nt", name="d_h1", tm=1024, tn=1024, tk=1536, addend=dh_head)
    grad_x, sums1 = _modulate_bwd(dh, 0, x2d, sc1, "modulate1_bwd", dres=dx1)
    (sums1c,) = _modulate_bwd(dh_head, s // ROW_TILE, ctx2d, sc1c, "modulate1_ctx_bwd")

    zero = jnp.zeros((1, d), F32)
    wide = 6 * d
    d_mod_mine = jnp.concatenate([sums1[1:2], sums1[0:1], sums2[2:3], sums2[1:2], sums2[0:1], fsums[1:2]], axis=1)
    d_mod_ctx = jnp.concatenate([sums1c[1:2], sums1c[0:1], zero, zero, zero, zero], axis=1)
    misc = jnp.concatenate([psums[0:1, :], psums[1:2, :KV_RANK], fsums[0:1]], axis=1)
    misc = jnp.pad(misc, ((0, 0), (0, wide - misc.shape[1])))
    cw_rows = jnp.pad(d_cw, ((0, 0), (0, wide - CONV_W)))
    loss_row = jnp.pad(fsums[3:4], ((0, 0), (0, wide - d)))
    small = jnp.concatenate([d_mod_mine, d_mod_ctx, misc, cw_rows, loss_row, jnp.zeros((1, wide), F32)], axis=0)
    (d_all,) = _all_gather([small], "gather_small_grads", True)
    d_cols = lax.dynamic_slice_in_dim(d_all, me * mod_cols, mod_cols, axis=2)
    g_w_mod, dsil, dsum = _adaln_bwd(a_rows.T, w_mod[0], d_cols[:, 0, :], d_cols[:, 1, :], d_all, "adaln_bwd")
    (dsil_all,) = _all_gather([dsil], "gather_d_cctx", True)
    g_c_ctx = _cctx_grad(dsil_all, c_ctx[None, :], "cctx_grad")
    loss = dsum[6, 0]
    g_b_mod = dsum[0:1] + dsum[1:2]
    g_qg = dsum[2:3, 0:Q_RANK]
    g_kvg = dsum[2:3, Q_RANK:Q_RANK + KV_RANK]
    g_fg = dsum[2:3, Q_RANK + KV_RANK:Q_RANK + KV_RANK + d]
    g_cw = lax.dynamic_slice_in_dim(dsum[3:6, :CONV_W], me * cw_cols, cw_cols, axis=1)

    send = _pack_small_grads(d_head, d_conv, d_wq, d_wkk, d_wkv)
    slots = dict(zip(["w_in", "w_uq", "w_ukv"], _all_to_all(send, "exchange_grads")))
    slots.update(zip(["w_out", "w_mlp1", "w_mlp2"], rode))

    grads = {"c_ctx": g_c_ctx[0], "w_mod": g_w_mod[None], "b_mod": g_b_mod, "q_norm_g": g_qg, "kv_norm_g": g_kvg,
             "conv_w": g_cw[None], "final_norm_g": g_fg[0]}
    weights = {"c_ctx": c_ctx, "w_mod": w_mod, "b_mod": b_mod, "w_in": w_in, "q_norm_g": q_norm_g, "w_uq": w_uq,
               "kv_norm_g": kv_norm_g, "w_ukv": w_ukv, "conv_w": conv_w, "w_out": w_out, "w_mlp1": w_mlp1,
               "w_mlp2": w_mlp2, "final_norm_g": final_norm_g}
    m_in = {"c_ctx": m_c_ctx, "w_mod": m_w_mod, "b_mod": m_b_mod, "w_in": m_w_in, "q_norm_g": m_q_norm_g,
            "w_uq": m_w_uq, "kv_norm_g": m_kv_norm_g, "w_ukv": m_w_ukv, "conv_w": m_conv_w, "w_out": m_w_out,
            "w_mlp1": m_w_mlp1, "w_mlp2": m_w_mlp2, "final_norm_g": m_final_norm_g}
    v_in = {"c_ctx": v_c_ctx, "w_mod": v_w_mod, "b_mod": v_b_mod, "w_in": v_w_in, "q_norm_g": v_q_norm_g,
            "w_uq": v_w_uq, "kv_norm_g": v_kv_norm_g, "w_ukv": v_w_ukv, "conv_w": v_conv_w, "w_out": v_w_out,
            "w_mlp1": v_w_mlp1, "w_mlp2": v_w_mlp2, "final_norm_g": v_final_norm_g}
    names = list(weights)
    small_names = ["c_ctx", "b_mod", "q_norm_g", "kv_norm_g", "final_norm_g"]
    delta, new_m, new_v = {}, {}, {}

    def flat(a):
        return a.reshape(1, -1)

    packs = [jnp.concatenate([flat(src[n]) for n in small_names], axis=1) for src in (weights, grads, m_in, v_in)]
    small_out = _adamw(*packs, "adamw_small")
    off = 0
    for n in small_names:
        size = weights[n].size
        for dst, arr in zip((delta, new_m, new_v), small_out):
            dst[n] = arr[:, off:off + size].reshape(weights[n].shape)
        off += size
    for n in names:
        if n in small_names:
            continue
        shp = weights[n].shape
        two_d = (shp[0] * shp[1], shp[2])
        wmv = [a.reshape(two_d) for a in (weights[n], m_in[n], v_in[n])]
        if n in slots:
            outs = _adamw(wmv[0], slots[n], wmv[1], wmv[2], "adamw_" + n, slots=True)
            grads[n] = outs[0].reshape(shp)
            outs = outs[1:]
        else:
            outs = _adamw(wmv[0], grads[n].reshape(two_d), wmv[1], wmv[2], "adamw_" + n)
        delta[n], new_m[n], new_v[n] = [a.reshape(shp) for a in outs]

    return (loss, grad_x[None], *[grads[n] for n in names], *[delta[n] for n in names],
            *[new_m[n] for n in names], *[new_v[n] for n in names])
```

```python
import math

import jax
import jax.numpy as jnp
import numpy as np
from jax import lax
from jax.experimental import pallas as pl
from jax.experimental.pallas import tpu as pltpu

F32 = jnp.float32
BF16 = jnp.bfloat16

D_MODEL = 1024
GRID_W = 64
N_HEADS = 8
QK_NOPE = 64
QK_ROPE = 32
V_DIM = 64
Q_RANK = 256
KV_RANK = 128
MLA_IN = Q_RANK + KV_RANK + QK_ROPE
CONV_W = 512
HEAD_COLS = 512
D_FF = 4096
ROPE_THETA = 10000.0
EPS = 1e-6
ATTN_SCALE = 1.0 / math.sqrt(QK_NOPE + QK_ROPE)
LOG2_E = 1.0 / math.log(2.0)
EXP2_SCALE = ATTN_SCALE * LOG2_E
N_DEV = 8
LANES = 128

ADAM_LR, ADAM_B1, ADAM_B2, ADAM_EPS, ADAM_WD, ADAM_STEP = 0.001, 0.9, 0.999, 1e-08, 0.01, 10

ROW_TILE = 256
VMEM_BIG = 60 * 1024 * 1024


def _params(sem=None, vmem=None):
    return pltpu.CompilerParams(dimension_semantics=sem, vmem_limit_bytes=vmem)


def _pick(n, prefs):
    for p in prefs:
        if n % p == 0:
            return p
    return n


def _my_index():
    return 4 * lax.axis_index("x") + 2 * lax.axis_index("y") + lax.axis_index("c")


def _all_gather(arrays, name, in_vmem):
    space = pltpu.VMEM if in_vmem else pl.ANY
    n = len(arrays)

    def body(*refs):
        x_refs, out_refs = refs[:n], refs[n:2 * n]
        send_sems, recv_sems, local_sems = refs[2 * n:]
        x, y, c = lax.axis_index("x"), lax.axis_index("y"), lax.axis_index("c")
        me, sibling = (x, y, c), (x, y, 1 - c)
        chips = [(1 - x, y), (x, 1 - y), (1 - x, 1 - y)]

        def slot(a, px, py, pc):
            return out_refs[a].at[4 * px + 2 * py + pc]

        def copy(a, k, block, to, src=None):
            return pltpu.make_async_remote_copy(
                src_ref=slot(a, *block) if src is None else src, dst_ref=slot(a, *block),
                send_sem=send_sems.at[7 * a + k], recv_sem=recv_sems.at[7 * a + k],
                device_id=to, device_id_type=pl.DeviceIdType.MESH)

        mine = [pltpu.make_async_copy(x_refs[a], slot(a, *me), local_sems.at[a]) for a in range(n)]
        for cp in mine:
            cp.start()
        started = []
        for a in range(n):
            first = [copy(a, 0, me, sibling, src=x_refs[a])]
            first += [copy(a, 1 + j, me, (*chip, c), src=x_refs[a]) for j, chip in enumerate(chips)]
            for cp in first:
                cp.start()
            started += first
        for a in range(n):
            for j, chip in enumerate(chips):
                copy(a, 1 + j, (*chip, c), me).wait_recv()
                passed = copy(a, 4 + j, (*chip, c), sibling)
                passed.start()
                started.append(passed)
        for a in range(n):
            copy(a, 0, sibling, me).wait_recv()
            for j, chip in enumerate(chips):
                copy(a, 4 + j, (*chip, 1 - c), me).wait_recv()
        for cp in started:
            cp.wait_send()
        for cp in mine:
            cp.wait()

    outs = pl.pallas_call(
        body, name=name,
        out_shape=tuple(jax.ShapeDtypeStruct((N_DEV,) + a.shape, a.dtype) for a in arrays),
        in_specs=[pl.BlockSpec(memory_space=space)] * n,
        out_specs=tuple(pl.BlockSpec(memory_space=space) for _ in arrays),
        scratch_shapes=[pltpu.SemaphoreType.DMA((7 * n,)), pltpu.SemaphoreType.DMA((7 * n,)),
                        pltpu.SemaphoreType.DMA((n,))],
    )(*arrays)
    return list(outs)


def _all_to_all(arrays, name):
    n = len(arrays)

    def body(*refs):
        x_refs, y_refs = refs[:n], refs[n:2 * n]
        send_sems, recv_sems, local_sems = refs[2 * n:]
        x, y, c = lax.axis_index("x"), lax.axis_index("y"), lax.axis_index("c")
        me = 4 * x + 2 * y + c
        peers = []
        for k in range(1, N_DEV):
            peers.append((1 - x if k & 4 else x, 1 - y if k & 2 else y, 1 - c if k & 1 else c))

        def copy(a, k, peer, landing):
            pid = 4 * peer[0] + 2 * peer[1] + peer[2]
            return pltpu.make_async_remote_copy(
                src_ref=x_refs[a].at[pid], dst_ref=y_refs[a].at[pid if landing else me],
                send_sem=send_sems.at[7 * a + k], recv_sem=recv_sems.at[7 * a + k],
                device_id=peer, device_id_type=pl.DeviceIdType.MESH)

        mine = [pltpu.make_async_copy(x_refs[a].at[me], y_refs[a].at[me], local_sems.at[a]) for a in range(n)]
        for cp in mine:
            cp.start()
        sends = [copy(a, k, peer, False) for a in range(n) for k, peer in enumerate(peers)]
        for cp in sends:
            cp.start()
        for a in range(n):
            for k, peer in enumerate(peers):
                copy(a, k, peer, True).wait_recv()
        for cp in sends:
            cp.wait_send()
        for cp in mine:
            cp.wait()

    outs = pl.pallas_call(
        body, name=name,
        out_shape=tuple(jax.ShapeDtypeStruct(a.shape, a.dtype) for a in arrays),
        in_specs=[pl.BlockSpec(memory_space=pl.ANY)] * n,
        out_specs=tuple(pl.BlockSpec(memory_space=pl.ANY) for _ in arrays),
        scratch_shapes=[pltpu.SemaphoreType.DMA((7 * n,)), pltpu.SemaphoreType.DMA((7 * n,)),
                        pltpu.SemaphoreType.DMA((n,))],
    )(*arrays)
    return list(outs)


class _Riding:
    def __init__(self, kind, arrays):
        self.kind, self.arrays, self.n = kind, list(arrays), len(arrays)
        lead = (N_DEV,) if kind == "gather" else ()
        self.out_shape = [jax.ShapeDtypeStruct(lead + a.shape, a.dtype) for a in self.arrays]
        self.specs = [pl.BlockSpec(memory_space=pl.ANY)] * self.n
        self.scratch = [pltpu.SemaphoreType.DMA((7 * self.n,)), pltpu.SemaphoreType.DMA((7 * self.n,)),
                        pltpu.SemaphoreType.DMA((self.n,))]

    def copies(self, x_refs, y_refs, send_sems, recv_sems, local_sems):
        x, y, c = lax.axis_index("x"), lax.axis_index("y"), lax.axis_index("c")
        me = 4 * x + 2 * y + c
        local, sends, landings = [], [], []
        for a in range(self.n):
            src_mine = x_refs[a] if self.kind == "gather" else x_refs[a].at[me]
            local.append(pltpu.make_async_copy(src_mine, y_refs[a].at[me], local_sems.at[a]))
            for k in range(1, N_DEV):
                peer = (1 - x if k & 4 else x, 1 - y if k & 2 else y, 1 - c if k & 1 else c)
                pid = 4 * peer[0] + 2 * peer[1] + peer[2]
                src = x_refs[a] if self.kind == "gather" else x_refs[a].at[pid]
                for dst, out in ((me, sends), (pid, landings)):
                    out.append(pltpu.make_async_remote_copy(
                        src_ref=src, dst_ref=y_refs[a].at[dst],
                        send_sem=send_sems.at[7 * a + k - 1], recv_sem=recv_sems.at[7 * a + k - 1],
                        device_id=peer, device_id_type=pl.DeviceIdType.MESH))
        return local, sends, landings

    def run(self, first, last, x_refs, y_refs, sems):
        local, sends, landings = self.copies(x_refs, y_refs, *sems)

        @pl.when(first)
        def _():
            for cp in local + sends:
                cp.start()

        return local, sends, landings, last

    @staticmethod
    def finish(state):
        local, sends, landings, last = state

        @pl.when(last)
        def _():
            for cp in landings:
                cp.wait_recv()
            for cp in sends:
                cp.wait_send()
            for cp in local:
                cp.wait()


_DIMS = {"nn": (((1,), (0,)), ((), ())), "nt": (((1,), (1,)), ((), ())), "tn": (((0,), (0,)), ((), ()))}
NT_DIMS = _DIMS["nt"]
TN_DIMS = _DIMS["tn"]


def _swap8(x):
    lane = lax.broadcasted_iota(jnp.int32, x.shape, 1)
    return jnp.where((lane & 15) < 8, pltpu.roll(x, LANES - 8, 1), pltpu.roll(x, 8, 1))


def _rope(x, cos, sgn, bwd):
    return x * cos + (_swap8(x * sgn) if bwd else _swap8(x) * sgn)


def _matmul(a, b, *, mode, name, out_dtype=F32, tm=512, tn=512, tk=512, m=None, k=None,
            epilogue=None, extra=(), addend=None, slots=None):
    if mode == "nn":
        m = a.shape[0] if m is None else m
        k = a.shape[1]
        n = N_DEV * b.shape[2] if slots == "b_cols" else b.shape[1]
    elif mode == "nt":
        m = a.shape[0] if m is None else m
        k = a.shape[1]
        n = b.shape[1] if slots == "b_contract" else b.shape[0]
    else:
        k = a.shape[0] if k is None else k
        m, n = a.shape[1], b.shape[1]
    tm, tn, tk = min(tm, m), min(tn, n), min(tk, k)
    if slots == "b_cols":
        tn = b.shape[2]
    if slots == "b_contract":
        tk = b.shape[2]
    if slots == "out":
        tn = n // N_DEV
    assert m % tm == 0 and n % tn == 0 and k % tk == 0, (name, m, n, k, tm, tn, tk)
    nk = k // tk
    dims = _DIMS[mode]
    a_spec = (pl.BlockSpec((tk, tm), lambda i, j, kk: (kk, i)) if mode == "tn"
              else pl.BlockSpec((tm, tk), lambda i, j, kk: (i, kk)))
    if slots == "b_cols":
        b_spec = pl.BlockSpec((None, tk, tn), lambda i, j, kk: (j, kk, 0))
    elif slots == "b_contract":
        b_spec = pl.BlockSpec((None, tn, tk), lambda i, j, kk: (kk, j, 0))
    elif mode == "nt":
        b_spec = pl.BlockSpec((tn, tk), lambda i, j, kk: (j, kk))
    else:
        b_spec = pl.BlockSpec((tk, tn), lambda i, j, kk: (kk, j))
    tile = pl.BlockSpec((tm, tn), lambda i, j, kk: (i, j))
    if slots == "out":
        o_spec = pl.BlockSpec((None, tm, tn), lambda i, j, kk: (j, i, 0))
        o_shape = (N_DEV, m, tn)
    else:
        o_spec, o_shape = tile, (m, n)
    in_specs, args = [a_spec, b_spec], [a, b]
    if epilogue == "drelu2":
        in_specs.append(tile)
    elif epilogue == "rope":
        in_specs += [pl.BlockSpec((tm, LANES), lambda i, j, kk: (i, 0))] * 2
    args += list(extra)
    if addend is not None:
        in_specs.append(tile)
        args.append(addend)
    if epilogue == "relu2":
        out_shape = (jax.ShapeDtypeStruct(o_shape, BF16), jax.ShapeDtypeStruct(o_shape, BF16))
        out_specs = (o_spec, o_spec)
    else:
        out_shape = jax.ShapeDtypeStruct(o_shape, out_dtype)
        out_specs = o_spec
    n_in = len(args)
    n_out = 2 if epilogue == "relu2" else 1

    def body(*refs):
        a_ref, b_ref = refs[0], refs[1]
        outs = refs[n_in:n_in + n_out]
        part = lax.dot_general(a_ref[...], b_ref[...], dims, preferred_element_type=F32)

        def finish(acc):
            if addend is not None:
                acc = acc + refs[n_in - 1][...]
            if epilogue == "relu2":
                outs[0][...] = acc.astype(BF16)
                r = jnp.maximum(acc, 0.0)
                outs[1][...] = (r * r).astype(BF16)
            elif epilogue == "drelu2":
                u = refs[2][...].astype(F32)
                outs[0][...] = (acc * (2.0 * jnp.maximum(u, 0.0))).astype(out_dtype)
            elif epilogue == "rope":
                cos, sgn = refs[2][...], refs[3][...]
                for h in range(tn // LANES):
                    sl = slice(h * LANES, (h + 1) * LANES)
                    outs[0][:, sl] = _rope(acc[:, sl], cos, sgn, False).astype(out_dtype)
            else:
                outs[0][...] = acc.astype(out_dtype)

        if nk == 1:
            finish(part)
        else:
            acc_ref = refs[n_in + n_out]
            kk = pl.program_id(2)

            @pl.when(kk == 0)
            def _():
                acc_ref[...] = part

            @pl.when(kk > 0)
            def _():
                acc_ref[...] += part

            @pl.when(kk == nk - 1)
            def _():
                finish(acc_ref[...])

    return pl.pallas_call(
        body, name=name, grid=(m // tm, n // tn, nk),
        out_shape=out_shape, in_specs=in_specs, out_specs=out_specs,
        scratch_shapes=[pltpu.VMEM((tm, tn), F32)] if nk > 1 else [],
        compiler_params=_params(("parallel", "parallel", "arbitrary"), VMEM_BIG),
    )(*args)


def _rstd(x):
    return lax.rsqrt(jnp.mean(x * x, axis=1, keepdims=True) + EPS)


def _norm_bwd(dxn, xn, r):
    return r * (dxn - xn * jnp.mean(dxn * xn, axis=1, keepdims=True))


def _modulate_all(x, ctx, shift, scale, name):
    s, d = x.shape
    t = s + ctx.shape[0]
    ns = s // ROW_TILE
    nc = ctx.shape[0] // ROW_TILE

    def body(x_ref, c_ref, sh_ref, sc_ref, h_ref):
        def emit(v):
            h_ref[...] = (v * _rstd(v) * (1.0 + sc_ref[0]) + sh_ref[0]).astype(BF16)

        i = pl.program_id(0)

        @pl.when(i < ns)
        def _():
            emit(x_ref[...])

        @pl.when(i >= ns)
        def _():
            emit(c_ref[...])

    vec = pl.BlockSpec((1, 1, d), lambda i: (jnp.where(i < ns, 0, 1), 0, 0))
    return pl.pallas_call(
        body, name=name, grid=(ns + nc,),
        out_shape=jax.ShapeDtypeStruct((t, d), BF16),
        in_specs=[pl.BlockSpec((ROW_TILE, d), lambda i: (jnp.minimum(i, ns - 1), 0)),
                  pl.BlockSpec((ROW_TILE, d), lambda i: (jnp.maximum(i - ns, 0), 0)), vec, vec],
        out_specs=pl.BlockSpec((ROW_TILE, d), lambda i: (i, 0)),
        compiler_params=_params(("arbitrary",)),
    )(x, ctx, shift, scale)


def _resid_modulate(x, o, gate, shift, scale, name):
    s, d = x.shape

    def body(x_ref, o_ref, g_ref, sh_ref, sc_ref, x1_ref, h_ref):
        x1 = x_ref[...] + g_ref[...] * o_ref[...]
        x1_ref[...] = x1
        h_ref[...] = (x1 * _rstd(x1) * (1.0 + sc_ref[...]) + sh_ref[...]).astype(BF16)

    row = pl.BlockSpec((ROW_TILE, d), lambda i: (i, 0))
    vec = pl.BlockSpec((1, d), lambda i: (0, 0))
    return pl.pallas_call(
        body, name=name, grid=(s // ROW_TILE,),
        out_shape=(jax.ShapeDtypeStruct((s, d), F32), jax.ShapeDtypeStruct((s, d), BF16)),
        in_specs=[row, row, vec, vec, vec], out_specs=(row, row),
        compiler_params=_params(("parallel",)),
    )(x, o, gate, shift, scale)


def _final(x1, m, gate, gain, target, name):
    s, d = x1.shape
    n = s // ROW_TILE

    def body(x1_ref, m_ref, g_ref, gf_ref, t_ref, dx2_ref, dm_ref, sums_ref):
        i = pl.program_id(0)
        mm = m_ref[...]
        x2 = x1_ref[...] + g_ref[...] * mm
        r = _rstd(x2)
        xn = x2 * r
        err = xn * gf_ref[...] - t_ref[...]
        dy = err * (1.0 / d)
        dx2 = _norm_bwd(dy * gf_ref[...], xn, r)
        dx2_ref[...] = dx2
        dm_ref[...] = (dx2 * g_ref[...]).astype(BF16)

        @pl.when(i == 0)
        def _():
            sums_ref[...] = jnp.zeros_like(sums_ref)

        sums_ref[0:1, :] += jnp.sum(dy * xn, axis=0, keepdims=True)
        sums_ref[1:2, :] += jnp.sum(dx2 * mm, axis=0, keepdims=True)
        sums_ref[2:3, :] += jnp.sum(err * err, axis=0, keepdims=True)

        @pl.when(i == n - 1)
        def _():
            tot = jnp.sum(sums_ref[2:3, :], axis=1, keepdims=True) * (0.5 / d)
            sums_ref[3:4, :] = jnp.broadcast_to(tot, (1, d))

    row = pl.BlockSpec((ROW_TILE, d), lambda i: (i, 0))
    vec = pl.BlockSpec((1, d), lambda i: (0, 0))
    return pl.pallas_call(
        body, name=name, grid=(n,),
        out_shape=(jax.ShapeDtypeStruct((s, d), F32), jax.ShapeDtypeStruct((s, d), BF16),
                   jax.ShapeDtypeStruct((8, d), F32)),
        in_specs=[row, row, vec, vec, row],
        out_specs=(row, row, pl.BlockSpec((8, d), lambda i: (0, 0))),
        compiler_params=_params(("arbitrary",)),
    )(x1, m, gate, gain, target)


def _modulate_bwd(dh, row_off, xsrc, scale, name, dres=None, o=None, gate=None):
    s, d = xsrc.shape
    n = s // ROW_TILE
    has_dx, has_o = dres is not None, o is not None
    assert has_dx or not has_o

    def body(*refs):
        it = iter(refs)
        dh_ref, x_ref, sc_ref = next(it), next(it), next(it)
        dres_ref = next(it) if has_dx else None
        o_ref, g_ref = (next(it), next(it)) if has_o else (None, None)
        dx_ref = next(it) if has_dx else None
        do_ref = next(it) if has_o else None
        sums_ref = next(it)
        i = pl.program_id(0)
        x = x_ref[...]
        r = _rstd(x)
        xn = x * r
        dhv = dh_ref[...]

        @pl.when(i == 0)
        def _():
            sums_ref[...] = jnp.zeros_like(sums_ref)

        sums_ref[0:1, :] += jnp.sum(dhv * xn, axis=0, keepdims=True)
        sums_ref[1:2, :] += jnp.sum(dhv, axis=0, keepdims=True)
        if has_dx:
            dx = dres_ref[...] + _norm_bwd(dhv * (1.0 + sc_ref[...]), xn, r)
            dx_ref[...] = dx
            if has_o:
                do_ref[...] = (dx * g_ref[...]).astype(BF16)
                sums_ref[2:3, :] += jnp.sum(dx * o_ref[...], axis=0, keepdims=True)

    row = pl.BlockSpec((ROW_TILE, d), lambda i: (i, 0))
    vec = pl.BlockSpec((1, d), lambda i: (0, 0))
    in_specs = [pl.BlockSpec((ROW_TILE, d), lambda i: (i + row_off, 0)), row, vec]
    args = [dh, xsrc, scale]
    out_shape, out_specs = [], []
    if has_dx:
        in_specs.append(row)
        args.append(dres)
        out_shape.append(jax.ShapeDtypeStruct((s, d), F32))
        out_specs.append(row)
    if has_o:
        in_specs += [row, vec]
        args += [o, gate]
        out_shape.append(jax.ShapeDtypeStruct((s, d), BF16))
        out_specs.append(row)
    out_shape.append(jax.ShapeDtypeStruct((8, d), F32))
    out_specs.append(pl.BlockSpec((8, d), lambda i: (0, 0)))
    return pl.pallas_call(
        body, name=name, grid=(n,),
        out_shape=tuple(out_shape), in_specs=in_specs, out_specs=tuple(out_specs),
        compiler_params=_params(("arbitrary",)),
    )(*args)


def _qkv_prep(z, q_gain, kv_gain, cos, sgn, name):
    t = z.shape[0]

    def body(z_ref, qg_ref, kg_ref, c_ref, s_ref, cq_ref, kv_ref):
        zq = z_ref[:, 0:Q_RANK]
        cq_ref[...] = (zq * _rstd(zq) * qg_ref[...]).astype(BF16)
        zk = z_ref[:, Q_RANK:Q_RANK + KV_RANK]
        kv_ref[:, 0:KV_RANK] = (zk * _rstd(zk) * kg_ref[...]).astype(BF16)
        kr = z_ref[:, Q_RANK + KV_RANK:HEAD_COLS]
        kv_ref[:, KV_RANK:KV_RANK + LANES] = _rope(kr, c_ref[...], s_ref[...], False).astype(BF16)

    tab = pl.BlockSpec((ROW_TILE, LANES), lambda i: (i, 0))
    return pl.pallas_call(
        body, name=name, grid=(t // ROW_TILE,),
        out_shape=(jax.ShapeDtypeStruct((t, Q_RANK), BF16), jax.ShapeDtypeStruct((t, KV_RANK + LANES), BF16)),
        in_specs=[pl.BlockSpec((ROW_TILE, HEAD_COLS), lambda i: (i, 0)),
                  pl.BlockSpec((1, Q_RANK), lambda i: (0, 0)), pl.BlockSpec((1, KV_RANK), lambda i: (0, 0)), tab, tab],
        out_specs=(pl.BlockSpec((ROW_TILE, Q_RANK), lambda i: (i, 0)),
                   pl.BlockSpec((ROW_TILE, KV_RANK + LANES), lambda i: (i, 0))),
        compiler_params=_params(("parallel",)),
    )(z, q_gain, kv_gain, cos, sgn)


def _qkv_prep_bwd(z, dcq, dkv, q_gain, kv_gain, cos, sgn, s, name):
    t = z.shape[0]
    ns = s // ROW_TILE

    def body(z_ref, dcq_ref, dkv_ref, qg_ref, kg_ref, c_ref, s_ref, dz_ref, sums_ref):
        i = pl.program_id(0)

        @pl.when(i == 0)
        def _():
            sums_ref[...] = jnp.zeros_like(sums_ref)

        @pl.when(i < ns)
        def _():
            zq = z_ref[:, 0:Q_RANK]
            r = _rstd(zq)
            zn = zq * r
            dc = dcq_ref[...]
            sums_ref[0:1, :] += jnp.sum(dc * zn, axis=0, keepdims=True)
            dz_ref[:, 0:Q_RANK] = _norm_bwd(dc * qg_ref[...], zn, r).astype(BF16)

        @pl.when(i >= ns)
        def _():
            dz_ref[:, 0:Q_RANK] = jnp.zeros((ROW_TILE, Q_RANK), BF16)

        zk = z_ref[:, Q_RANK:Q_RANK + KV_RANK]
        r = _rstd(zk)
        zn = zk * r
        dc = dkv_ref[:, 0:KV_RANK]
        sums_ref[1:2, 0:KV_RANK] += jnp.sum(dc * zn, axis=0, keepdims=True)
        dz_ref[:, Q_RANK:Q_RANK + KV_RANK] = _norm_bwd(dc * kg_ref[...], zn, r).astype(BF16)
        dkr = dkv_ref[:, KV_RANK:KV_RANK + LANES]
        dz_ref[:, Q_RANK + KV_RANK:HEAD_COLS] = _rope(dkr, c_ref[...], s_ref[...], True).astype(BF16)

    tab = pl.BlockSpec((ROW_TILE, LANES), lambda i: (i, 0))
    return pl.pallas_call(
        body, name=name, grid=(t // ROW_TILE,),
        out_shape=(jax.ShapeDtypeStruct((t, HEAD_COLS), BF16), jax.ShapeDtypeStruct((8, Q_RANK), F32)),
        in_specs=[pl.BlockSpec((ROW_TILE, HEAD_COLS), lambda i: (i, 0)),
                  pl.BlockSpec((ROW_TILE, Q_RANK), lambda i: (jnp.minimum(i, ns - 1), 0)),
                  pl.BlockSpec((ROW_TILE, KV_RANK + LANES), lambda i: (i, 0)),
                  pl.BlockSpec((1, Q_RANK), lambda i: (0, 0)), pl.BlockSpec((1, KV_RANK), lambda i: (0, 0)), tab, tab],
        out_specs=(pl.BlockSpec((ROW_TILE, HEAD_COLS), lambda i: (i, 0)), pl.BlockSpec((8, Q_RANK), lambda i: (0, 0))),
        compiler_params=_params(("arbitrary",)),
    )(z, dcq, dkv, q_gain, kv_gain, cos, sgn)


def _shift_rows(u, s):
    rowi = lax.broadcasted_iota(jnp.int32, u.shape, 0)
    prev = jnp.where(rowi == 0, 0.0, pltpu.roll(u, 1, 0))
    nxt = jnp.where(rowi == s - 1, 0.0, pltpu.roll(u, s - 1, 0))
    return prev, nxt


def _conv_fwd(z_conv, cw, a_cat, name):
    s = z_conv.shape[0]

    def body(z_ref, w_ref, a_in_ref, o_ref):
        del a_in_ref
        gb, gc, xv = z_ref[:, 0:LANES], z_ref[:, LANES:2 * LANES], z_ref[:, 2 * LANES:3 * LANES]
        u = gc * xv
        prev, nxt = _shift_rows(u, s)
        y = w_ref[0:1, :] * prev + w_ref[1:2, :] * u + w_ref[2:3, :] * nxt
        o_ref[...] = (gb * y).astype(BF16)

    return pl.pallas_call(
        body, name=name, grid=(CONV_W // LANES,),
        out_shape=jax.ShapeDtypeStruct(a_cat.shape, a_cat.dtype),
        in_specs=[pl.BlockSpec((s, 3 * LANES), lambda j: (0, j)), pl.BlockSpec((3, LANES), lambda j: (0, j)),
                  pl.BlockSpec(memory_space=pl.ANY)],
        out_specs=pl.BlockSpec((s, LANES), lambda j: (0, 4 + j)),
        input_output_aliases={2: 0},
        compiler_params=_params(("parallel",), VMEM_BIG),
    )(z_conv, cw, a_cat)


def _conv_bwd(z_conv, cw, da, name):
    s = z_conv.shape[0]

    def body(z_ref, w_ref, da_ref, dz_ref, dw_ref):
        gb, gc, xv = z_ref[:, 0:LANES], z_ref[:, LANES:2 * LANES], z_ref[:, 2 * LANES:3 * LANES]
        u = gc * xv
        prev, nxt = _shift_rows(u, s)
        dcv = da_ref[...]
        dz_ref[:, 0:LANES] = (dcv * (w_ref[0:1, :] * prev + w_ref[1:2, :] * u + w_ref[2:3, :] * nxt)).astype(BF16)
        dy = dcv * gb
        dw_ref[0:1, :] = jnp.sum(dy * prev, axis=0, keepdims=True)
        dw_ref[1:2, :] = jnp.sum(dy * u, axis=0, keepdims=True)
        dw_ref[2:3, :] = jnp.sum(dy * nxt, axis=0, keepdims=True)
        dyp, dyn = _shift_rows(dy, s)
        du = w_ref[0:1, :] * dyn + w_ref[1:2, :] * dy + w_ref[2:3, :] * dyp
        dz_ref[:, LANES:2 * LANES] = (du * xv).astype(BF16)
        dz_ref[:, 2 * LANES:3 * LANES] = (du * gc).astype(BF16)

    blk = pl.BlockSpec((s, 3 * LANES), lambda j: (0, j))
    cws = pl.BlockSpec((3, LANES), lambda j: (0, j))
    return pl.pallas_call(
        body, name=name, grid=(CONV_W // LANES,),
        out_shape=(jax.ShapeDtypeStruct(z_conv.shape, BF16), jax.ShapeDtypeStruct((3, CONV_W), F32)),
        in_specs=[blk, cws, pl.BlockSpec((s, LANES), lambda j: (0, 4 + j))], out_specs=(blk, cws),
        compiler_params=_params(("parallel",), VMEM_BIG),
    )(z_conv, cw, da)


ATT_TQ = 256


def _head_mask(shape, hh):
    lane = lax.broadcasted_iota(jnp.int32, shape, 1)
    return (lane >= hh * V_DIM) & (lane < (hh + 1) * V_DIM)


def _attn_fwd(qf, kv, riding, name):
    s, t = qf.shape[0], kv.shape[0]
    nq = s // ATT_TQ
    nr = riding.n

    def body(*refs):
        q_ref, k_ref, v_ref = refs[:3]
        o_ref, ob_ref, st_ref = refs[3 + nr:6 + nr]
        p, i = pl.program_id(0), pl.program_id(1)
        state = riding.run((p == 0) & (i == 0), (p == N_HEADS // 2 - 1) & (i == nq - 1),
                           refs[3:3 + nr], refs[6 + nr:6 + 2 * nr], refs[6 + 2 * nr:])
        v = v_ref[...]
        vlane = lax.broadcasted_iota(jnp.int32, v.shape, 1)
        olane = lax.broadcasted_iota(jnp.int32, (ATT_TQ, LANES), 1)
        acc = jnp.zeros((ATT_TQ, LANES), F32)
        stat = jnp.zeros((ATT_TQ, LANES), F32)
        for hh in range(2):
            sl = slice(hh * LANES, (hh + 1) * LANES)
            sc = lax.dot_general(q_ref[:, sl], k_ref[:, sl], NT_DIMS, preferred_element_type=F32)
            mx = jnp.max(sc, axis=1, keepdims=True)
            e = jnp.exp2((sc - mx) * EXP2_SCALE).astype(BF16)
            one_lane = (1 - hh) * V_DIM
            vm = jnp.where(_head_mask(v.shape, hh), v, jnp.where(vlane == one_lane, 1.0, 0.0).astype(BF16))
            r = jnp.dot(e, vm, preferred_element_type=F32)
            den = jnp.sum(jnp.where(olane == one_lane, r, 0.0), axis=1, keepdims=True)
            acc = acc + jnp.where(_head_mask(r.shape, hh), r * (1.0 / den), 0.0)
            stat = stat + jnp.where(olane == hh, mx * EXP2_SCALE + jnp.log(den) * LOG2_E, 0.0)
        o_ref[...] = acc
        ob_ref[...] = acc.astype(BF16)
        st_ref[...] = stat.T[0:8, :]
        riding.finish(state)

    o_spec = pl.BlockSpec((ATT_TQ, LANES), lambda p, i: (i, p))
    outs = pl.pallas_call(
        body, name=name, grid=(N_HEADS // 2, nq),
        out_shape=(jax.ShapeDtypeStruct((s, N_HEADS * V_DIM), F32),
                   jax.ShapeDtypeStruct((s, D_MODEL), BF16),
                   jax.ShapeDtypeStruct((N_HEADS // 2 * 8, s), F32), *riding.out_shape),
        in_specs=[pl.BlockSpec((ATT_TQ, 2 * LANES), lambda p, i: (i, p)),
                  pl.BlockSpec((t, 2 * LANES), lambda p, i: (0, p)),
                  pl.BlockSpec((t, LANES), lambda p, i: (0, N_HEADS + p)), *riding.specs],
        out_specs=(o_spec, o_spec, pl.BlockSpec((8, ATT_TQ), lambda p, i: (p, i)), *riding.specs),
        scratch_shapes=riding.scratch,
        compiler_params=_params(("arbitrary", "arbitrary"), VMEM_BIG),
    )(qf, kv, kv, *riding.arrays)
    return outs[0], outs[1], outs[2], list(outs[3:])


def _attn_bwd(qf, kv, o, da, stats, cos, sgn, riding, name):
    s, t = qf.shape[0], kv.shape[0]
    nq = s // ATT_TQ
    nr = riding.n

    def body(*refs):
        q_ref, k_ref, v_ref, o_ref, do_ref, st_ref, c_ref, s_ref = refs[:8]
        dq_ref, dk_ref, dv_ref = refs[8 + nr:11 + nr]
        dk_acc, dv_acc = refs[11 + 2 * nr:13 + 2 * nr]
        p, i = pl.program_id(0), pl.program_id(1)
        state = riding.run((p == 0) & (i == 0), (p == N_HEADS // 2 - 1) & (i == nq - 1),
                           refs[8:8 + nr], refs[11 + nr:11 + 2 * nr], refs[13 + 2 * nr:])

        @pl.when(i == 0)
        def _():
            dk_acc[...] = jnp.zeros_like(dk_acc)
            dv_acc[...] = jnp.zeros_like(dv_acc)

        v = v_ref[...]
        do = do_ref[...]
        od = do * o_ref[...]
        ones = jnp.ones((8, LANES), F32)
        for hh in range(2):
            sl = slice(hh * LANES, (hh + 1) * LANES)
            q, k = q_ref[:, sl], k_ref[:, sl]
            mask = _head_mask(do.shape, hh)
            dom = jnp.where(mask, do, 0.0).astype(BF16)
            delta = lax.dot_general(ones, jnp.where(mask, od, 0.0), NT_DIMS, preferred_element_type=F32,
                                    precision=lax.Precision.HIGHEST)[0:1, :]
            st = lax.dot_general(k, q, NT_DIMS, preferred_element_type=F32)
            pt = jnp.exp2(st * EXP2_SCALE - st_ref[hh:hh + 1, :]).astype(BF16)
            dpt = lax.dot_general(v, dom, NT_DIMS, preferred_element_type=F32)
            dst = (pt.astype(F32) * (dpt - delta)).astype(BF16)
            dv_acc[...] += jnp.dot(pt, dom, preferred_element_type=F32)
            dk_acc[:, sl] += jnp.dot(dst, q, preferred_element_type=F32)
            dq = lax.dot_general(dst, k, TN_DIMS, preferred_element_type=F32) * ATTN_SCALE
            dq_ref[:, sl] = _rope(dq, c_ref[...], s_ref[...], True).astype(BF16)

        @pl.when(i == nq - 1)
        def _():
            dk_ref[...] = (dk_acc[...] * ATTN_SCALE).astype(BF16)
            dv_ref[...] = dv_acc[...].astype(BF16)

        riding.finish(state)

    o_spec = pl.BlockSpec((ATT_TQ, LANES), lambda p, i: (i, p))
    tab = pl.BlockSpec((ATT_TQ, LANES), lambda p, i: (i, 0))
    outs = pl.pallas_call(
        body, name=name, grid=(N_HEADS // 2, nq),
        out_shape=(jax.ShapeDtypeStruct((s, N_HEADS * LANES), BF16),
                   jax.ShapeDtypeStruct((t, N_HEADS * LANES), BF16),
                   jax.ShapeDtypeStruct((t, N_HEADS * V_DIM), BF16), *riding.out_shape),
        in_specs=[pl.BlockSpec((ATT_TQ, 2 * LANES), lambda p, i: (i, p)),
                  pl.BlockSpec((t, 2 * LANES), lambda p, i: (0, p)),
                  pl.BlockSpec((t, LANES), lambda p, i: (0, N_HEADS + p)),
                  o_spec, o_spec,
                  pl.BlockSpec((8, ATT_TQ), lambda p, i: (p, i)), tab, tab, *riding.specs],
        out_specs=(pl.BlockSpec((ATT_TQ, 2 * LANES), lambda p, i: (i, p)),
                   pl.BlockSpec((t, 2 * LANES), lambda p, i: (0, p)),
                   pl.BlockSpec((t, LANES), lambda p, i: (0, p)), *riding.specs),
        scratch_shapes=[pltpu.VMEM((t, 2 * LANES), F32), pltpu.VMEM((t, LANES), F32), *riding.scratch],
        compiler_params=_params(("arbitrary", "arbitrary"), VMEM_BIG),
    )(qf, kv, kv, o, da, stats, cos, sgn, *riding.arrays)
    return outs[0], outs[1], outs[2], list(outs[3:])


def _silu(x):
    return x * (1.0 / (1.0 + jnp.exp(-x)))


def _adaln_fwd(a, w, b, name):
    def body(a_ref, w_ref, b_ref, o_ref):
        o_ref[...] = jnp.dot(_silu(a_ref[...]), w_ref[...], preferred_element_type=F32,
                             precision=lax.Precision.HIGHEST) + b_ref[...]

    return pl.pallas_call(
        body, name=name, out_shape=jax.ShapeDtypeStruct((a.shape[0], w.shape[1]), F32),
        compiler_params=_params(None, VMEM_BIG),
    )(a, w, b)


def _adaln_bwd(a_t, w, d_ex, d_ctx, d_all, name):
    def body(at_ref, w_ref, dex_ref, dctx_ref, dall_ref, gw_ref, dsil_ref, dsum_ref):
        sil_t = _silu(at_ref[...])
        dctx = dctx_ref[...]
        row = dctx[0:1, :]
        for j in range(1, N_DEV):
            row = row + dctx[j:j + 1, :]
        rowi = lax.broadcasted_iota(jnp.int32, dctx.shape, 0)
        ctx_rows = jnp.where(rowi == 0, jnp.broadcast_to(row, dctx.shape), 0.0)
        hi = lax.Precision.HIGHEST
        d_rows = jnp.concatenate([dex_ref[...], ctx_rows], axis=0)
        gw_ref[...] = jnp.dot(sil_t, d_rows, preferred_element_type=F32, precision=hi)
        dsil_ref[...] = lax.dot_general(ctx_rows, w_ref[...], NT_DIMS, preferred_element_type=F32, precision=hi)
        tot = dall_ref[0]
        for j in range(1, N_DEV):
            tot = tot + dall_ref[j]
        dsum_ref[...] = tot

    return pl.pallas_call(
        body, name=name,
        out_shape=(jax.ShapeDtypeStruct(w.shape, F32), jax.ShapeDtypeStruct((8, w.shape[0]), F32),
                   jax.ShapeDtypeStruct(d_all.shape[1:], F32)),
        compiler_params=_params(None, VMEM_BIG),
    )(a_t, w, d_ex, d_ctx, d_all)


def _cctx_grad(parts, c_ctx, name):
    def body(p_ref, c_ref, o_ref):
        tot = p_ref[0]
        for j in range(1, N_DEV):
            tot = tot + p_ref[j]
        cv = c_ref[...]
        sg = 1.0 / (1.0 + jnp.exp(-cv))
        o_ref[...] = tot[0:1, :] * (sg * (1.0 + cv * (1.0 - sg)))

    return pl.pallas_call(body, name=name, out_shape=jax.ShapeDtypeStruct(c_ctx.shape, F32))(parts, c_ctx)


def _adamw(w, g, m, v, name, slots=False):
    rows, cols = w.shape
    tr = _pick(rows, (256, 128, 64, 32, 16, 8))

    def body(w_ref, g_ref, m_ref, v_ref, *outs):
        if slots:
            gv = g_ref[0].astype(F32)
            for j in range(1, N_DEV):
                gv = gv + g_ref[j].astype(F32)
            outs[0][...] = gv
        else:
            gv = g_ref[...]
        d_ref, nm_ref, nv_ref = outs[-3:]
        nm = ADAM_B1 * m_ref[...] + (1.0 - ADAM_B1) * gv
        nv = ADAM_B2 * v_ref[...] + (1.0 - ADAM_B2) * (gv * gv)
        m_hat = nm / (1.0 - ADAM_B1 ** ADAM_STEP)
        v_hat = nv / (1.0 - ADAM_B2 ** ADAM_STEP)
        d_ref[...] = -ADAM_LR * (m_hat / (jnp.sqrt(v_hat) + ADAM_EPS) + ADAM_WD * w_ref[...])
        nm_ref[...] = nm
        nv_ref[...] = nv

    blk = pl.BlockSpec((tr, cols), lambda i: (i, 0))
    g_spec = pl.BlockSpec((N_DEV, tr, cols), lambda i: (0, i, 0)) if slots else blk
    sh = jax.ShapeDtypeStruct((rows, cols), F32)
    n_out = 4 if slots else 3
    return pl.pallas_call(
        body, name=name, grid=(rows // tr,), out_shape=(sh,) * n_out,
        in_specs=[blk, g_spec, blk, blk], out_specs=(blk,) * n_out,
        compiler_params=_params(("parallel",)),
    )(w, g, m, v)


def _rope_tables(s, l):
    tok = np.arange(s)
    row = (tok // GRID_W).astype(np.float32)
    col = (tok % GRID_W).astype(np.float32)
    half = QK_ROPE // 2
    freqs = np.float32(ROPE_THETA) ** (-np.arange(0, half, 2, dtype=np.float32) / np.float32(half))
    dd = np.arange(QK_ROPE)
    pos = np.where((dd // half)[None, :] == 0, row[:, None], col[:, None]).astype(np.float32)
    ang = (pos * freqs[dd % (half // 2)][None, :]).astype(np.float32)
    sin = np.sin(ang).astype(np.float32)
    cos_t = np.ones((s + l, LANES), np.float32)
    sgn_t = np.zeros((s + l, LANES), np.float32)
    cos_t[:s, QK_NOPE:QK_NOPE + QK_ROPE] = np.cos(ang)
    sgn_t[:s, QK_NOPE:QK_NOPE + QK_ROPE] = np.where(((dd % half) // (half // 2))[None, :] == 0, -sin, sin)
    return jnp.asarray(cos_t), jnp.asarray(sgn_t)


def _slots_to_cols(g):
    return g.transpose(1, 0, 2).reshape(g.shape[1], N_DEV * g.shape[2])


def _cols_to_slots(w):
    return w.reshape(w.shape[0], N_DEV, w.shape[1] // N_DEV).transpose(1, 0, 2)


def _unpack_small_weights(g_in, g_uq, g_ukv):
    w_in = _slots_to_cols(g_in)
    zeros = jnp.zeros((D_MODEL, QK_NOPE), BF16)
    win_head = jnp.concatenate([w_in[:, :Q_RANK + KV_RANK], zeros, w_in[:, Q_RANK + KV_RANK:MLA_IN],
                                zeros[:, :LANES - QK_NOPE - QK_ROPE]], axis=1)
    win_conv = w_in[:, MLA_IN:].reshape(D_MODEL, 3, CONV_W // LANES, LANES).transpose(0, 2, 1, 3)
    win_conv = win_conv.reshape(D_MODEL, 3 * CONV_W)
    w_uq = _slots_to_cols(g_uq).reshape(Q_RANK, N_HEADS, QK_NOPE + QK_ROPE)
    wq = jnp.pad(w_uq, ((0, 0), (0, 0), (0, LANES - QK_NOPE - QK_ROPE))).reshape(Q_RANK, N_HEADS * LANES)
    w_ukv = _slots_to_cols(g_ukv).reshape(KV_RANK, N_HEADS, QK_NOPE + V_DIM)
    k_top = jnp.pad(w_ukv[:, :, :QK_NOPE], ((0, 0), (0, 0), (0, LANES - QK_NOPE))).reshape(KV_RANK, N_HEADS * LANES)
    v_top = w_ukv[:, :, QK_NOPE:].reshape(KV_RANK, N_HEADS * V_DIM)
    eye = jnp.pad(jnp.eye(QK_ROPE, dtype=BF16), ((QK_NOPE, LANES - QK_NOPE - QK_ROPE),) * 2)
    wk = jnp.concatenate([
        jnp.concatenate([k_top, v_top], axis=1),
        jnp.concatenate([jnp.tile(eye, (1, N_HEADS)), jnp.zeros((LANES, N_HEADS * V_DIM), BF16)], axis=1)], axis=0)
    return win_head, win_conv, wq, wk


def _pack_small_grads(d_head, d_conv, d_wq, d_wkk, d_wkv):
    d_conv = d_conv.reshape(D_MODEL, CONV_W // LANES, 3, LANES).transpose(0, 2, 1, 3).reshape(D_MODEL, 3 * CONV_W)
    g_in = jnp.concatenate([d_head[:, :Q_RANK + KV_RANK],
                            d_head[:, Q_RANK + KV_RANK + QK_NOPE:Q_RANK + KV_RANK + QK_NOPE + QK_ROPE], d_conv], axis=1)
    g_uq = d_wq.reshape(Q_RANK, N_HEADS, LANES)[:, :, :QK_NOPE + QK_ROPE].reshape(Q_RANK, -1)
    g_kn = d_wkk[:KV_RANK].reshape(KV_RANK, N_HEADS, LANES)[:, :, :QK_NOPE]
    g_v = d_wkv[:KV_RANK].reshape(KV_RANK, N_HEADS, V_DIM)
    g_ukv = jnp.concatenate([g_kn, g_v], axis=2).reshape(KV_RANK, -1)
    return [_cols_to_slots(g).astype(BF16) for g in (g_in, g_uq, g_ukv)]


def kernel(x, c, ctx, c_ctx, w_mod, b_mod, w_in, q_norm_g, w_uq, kv_norm_g, w_ukv, conv_w, w_out, w_mlp1, w_mlp2, final_norm_g, loss_target, m_c_ctx, m_w_mod, m_b_mod, m_w_in, m_q_norm_g, m_w_uq, m_kv_norm_g, m_w_ukv, m_conv_w, m_w_out, m_w_mlp1, m_w_mlp2, m_final_norm_g, v_c_ctx, v_w_mod, v_b_mod, v_w_in, v_q_norm_g, v_w_uq, v_kv_norm_g, v_w_ukv, v_conv_w, v_w_out, v_w_mlp1, v_w_mlp2, v_final_norm_g):
    me = _my_index()
    x2d, ctx2d, tgt = x[0], ctx[0], loss_target[0]
    s, l = x2d.shape[0], ctx2d.shape[0]
    t = s + l
    d = D_MODEL
    mod_cols = w_mod.shape[2]
    cw_cols = conv_w.shape[2]

    (c_all,) = _all_gather([jnp.pad(c, ((0, 7), (0, 0)))], "gather_c", True)
    a_rows = jnp.concatenate([c_all[:, 0, :], c_ctx[None, :], jnp.zeros((7, d), F32)], axis=0)
    b_cols = lax.dynamic_slice(b_mod, (0, me * mod_cols), (1, mod_cols))
    mod_cols_all = _adaln_fwd(a_rows, w_mod[0], b_cols, "adaln_fwd")
    cw_blk = jnp.pad(conv_w[0], ((0, 5), (0, mod_cols - cw_cols)))
    (gathered,) = _all_gather([jnp.concatenate([mod_cols_all, cw_blk], axis=0)], "gather_mod", True)
    mod_mine = lax.dynamic_index_in_dim(gathered, me, axis=1, keepdims=False).reshape(1, 6 * d)
    mod_ctx = gathered[:, 8, :].reshape(1, 6 * d)
    cw_full = gathered[:, 16:19, :cw_cols].transpose(1, 0, 2).reshape(3, CONV_W)
    sh1, sc1, g1, sh2, sc2, g2 = [mod_mine[:, i * d:(i + 1) * d] for i in range(6)]
    sh1c, sc1c = mod_ctx[:, 0:d], mod_ctx[:, d:2 * d]

    early = [w.astype(BF16) for w in (w_in[0], w_uq[0], w_ukv[0])]
    late = [w.astype(BF16) for w in (w_out[0], w_mlp1[0], w_mlp2[0])]
    g_in, g_uq, g_ukv = _all_gather(early, "gather_weights", False)
    win_head, win_conv, wq, wk = _unpack_small_weights(g_in, g_uq, g_ukv)
    wk_k, wk_v = wk[:, :N_HEADS * LANES], wk[:, N_HEADS * LANES:]
    cos, sgn = _rope_tables(s, l)

    h_all = _modulate_all(x2d, ctx2d, jnp.stack([sh1, sh1c]), jnp.stack([sc1, sc1c]), "modulate1")
    tm_t = _pick(t, (1088, 768, 256))
    tk_t = _pick(t, (2176, 768, 256))
    z_head = _matmul(h_all, win_head, mode="nn", name="in_proj_head", tm=tm_t, tn=512, tk=1024)
    z_conv = _matmul(h_all, win_conv, mode="nn", name="in_proj_conv", m=s, tm=1024, tn=1536, tk=1024)
    cq, kv_in = _qkv_prep(z_head, q_norm_g, kv_norm_g, cos, sgn, "qkv_prep")
    qf = _matmul(cq, wq, mode="nn", name="q_up", out_dtype=BF16, m=s, tm=1024, tn=1024, tk=256,
                 epilogue="rope", extra=(cos, sgn))
    kv = _matmul(kv_in, wk, mode="nn", name="kv_up", out_dtype=BF16, tm=tm_t, tn=1536, tk=256)
    attn, a_cat, stats, (g_out, w1, g_w2) = _attn_fwd(qf, kv, _Riding("gather", late), "attn_fwd")
    wo = g_out.reshape(d, d)
    w2 = g_w2.reshape(D_FF, d)
    a_cat = _conv_fwd(z_conv, cw_full, a_cat, "conv_fwd")
    o = _matmul(a_cat, wo, mode="nn", name="out_proj", tm=1024, tn=1024, tk=1024)
    x1, h2 = _resid_modulate(x2d, o, g1, sh2, sc2, "resid_modulate2")
    u1, act = _matmul(h2, w1, mode="nn", name="mlp_up", tm=2048, tk=1024, epilogue="relu2", slots="b_cols")
    mlp = _matmul(act, w2, mode="nn", name="mlp_down", tm=1024, tn=1024, tk=4096)
    dx2, dm, fsums = _final(x1, mlp, g2, final_norm_g[None, :], tgt, "final_loss")

    d_w2 = _matmul(act, dm, mode="tn", name="d_w_mlp2", out_dtype=BF16, tm=2048, tn=1024, tk=1024)
    du1 = _matmul(dm, w2, mode="nt", name="d_act", out_dtype=BF16, tm=2048, tn=1024, tk=1024,
                  epilogue="drelu2", extra=(u1,))
    d_w1 = _matmul(h2, du1, mode="tn", name="d_w_mlp1", out_dtype=BF16, tm=1024, tk=4096, slots="out")
    dh2 = _matmul(du1, w1, mode="nt", name="d_h2", tm=2048, tn=1024, slots="b_contract")
    dx1, do, sums2 = _modulate_bwd(dh2, 0, x1, sc2, "modulate2_bwd", dres=dx2, o=o, gate=g1)
    d_wo = _matmul(a_cat, do, mode="tn", name="d_w_out", out_dtype=BF16, tm=1024, tn=1024, tk=2048)
    da = _matmul(do, wo, mode="nt", name="d_a", tm=1024, tn=1024, tk=1024)
    dz_conv, d_cw = _conv_bwd(z_conv, cw_full, da, "conv_bwd")
    ready = [d_wo.reshape(N_DEV, d // N_DEV, d), d_w1, d_w2.reshape(N_DEV, D_FF // N_DEV, d)]
    dq, dk, dv, rode = _attn_bwd(qf, kv, attn, da, stats, cos, sgn, _Riding("exchange", ready), "attn_bwd")
    d_wq = _matmul(cq, dq, mode="tn", name="d_w_uq", k=s, tm=256, tn=1024, tk=4096)
    dcq = _matmul(dq, wq, mode="nt", name="d_cq", tm=1024, tn=256, tk=1024)
    d_wkk = _matmul(kv_in, dk, mode="tn", name="d_w_ukv_k", tm=256, tn=1024, tk=tk_t)
    d_wkv = _matmul(kv_in, dv, mode="tn", name="d_w_ukv_v", tm=256, tn=512, tk=tk_t)
    dkv_in = _matmul(dk, wk_k, mode="nt", name="d_kv_in_k", tm=tm_t, tn=256, tk=1024)
    dkv_in = _matmul(dv, wk_v, mode="nt", name="d_kv_in_v", tm=tm_t, tn=256, tk=512, addend=dkv_in)
    dz_head, psums = _qkv_prep_bwd(z_head, dcq, dkv_in, q_norm_g, kv_norm_g, cos, sgn, s, "qkv_prep_bwd")
    d_head = _matmul(h_all, dz_head, mode="tn", name="d_w_in_head", tm=1024, tn=512, tk=tk_t)
    d_conv = _matmul(h_all, dz_conv, mode="tn", name="d_w_in_conv", k=s, tm=1024, tn=1536, tk=2048)
    dh_head = _matmul(dz_head, win_head, mode="nt", name="d_h1_head", tm=tm_t, tn=1024, tk=512)
    dh = _matmul(dz_conv, win_conv, mode="nt", name="d_h1", tm=1024, tn=1024, tk=1536, addend=dh_head)
    grad_x, sums1 = _modulate_bwd(dh, 0, x2d, sc1, "modulate1_bwd", dres=dx1)
    (sums1c,) = _modulate_bwd(dh_head, s // ROW_TILE, ctx2d, sc1c, "modulate1_ctx_bwd")

    zero = jnp.zeros((1, d), F32)
    wide = 6 * d
    d_mod_mine = jnp.concatenate([sums1[1:2], sums1[0:1], sums2[2:3], sums2[1:2], sums2[0:1], fsums[1:2]], axis=1)
    d_mod_ctx = jnp.concatenate([sums1c[1:2], sums1c[0:1], zero, zero, zero, zero], axis=1)
    misc = jnp.concatenate([psums[0:1, :], psums[1:2, :KV_RANK], fsums[0:1]], axis=1)
    misc = jnp.pad(misc, ((0, 0), (0, wide - misc.shape[1])))
    cw_rows = jnp.pad(d_cw, ((0, 0), (0, wide - CONV_W)))
    loss_row = jnp.pad(fsums[3:4], ((0, 0), (0, wide - d)))
    small = jnp.concatenate([d_mod_mine, d_mod_ctx, misc, cw_rows, loss_row, jnp.zeros((1, wide), F32)], axis=0)
    (d_all,) = _all_gather([small], "gather_small_grads", True)
    d_cols = lax.dynamic_slice_in_dim(d_all, me * mod_cols, mod_cols, axis=2)
    g_w_mod, dsil, dsum = _adaln_bwd(a_rows.T, w_mod[0], d_cols[:, 0, :], d_cols[:, 1, :], d_all, "adaln_bwd")
    (dsil_all,) = _all_gather([dsil], "gather_d_cctx", True)
    g_c_ctx = _cctx_grad(dsil_all, c_ctx[None, :], "cctx_grad")
    loss = dsum[6, 0]
    g_b_mod = dsum[0:1] + dsum[1:2]
    g_qg = dsum[2:3, 0:Q_RANK]
    g_kvg = dsum[2:3, Q_RANK:Q_RANK + KV_RANK]
    g_fg = dsum[2:3, Q_RANK + KV_RANK:Q_RANK + KV_RANK + d]
    g_cw = lax.dynamic_slice_in_dim(dsum[3:6, :CONV_W], me * cw_cols, cw_cols, axis=1)

    send = _pack_small_grads(d_head, d_conv, d_wq, d_wkk, d_wkv)
    slots = dict(zip(["w_in", "w_uq", "w_ukv"], _all_to_all(send, "exchange_grads")))
    slots.update(zip(["w_out", "w_mlp1", "w_mlp2"], rode))

    grads = {"c_ctx": g_c_ctx[0], "w_mod": g_w_mod[None], "b_mod": g_b_mod, "q_norm_g": g_qg, "kv_norm_g": g_kvg,
             "conv_w": g_cw[None], "final_norm_g": g_fg[0]}
    weights = {"c_ctx": c_ctx, "w_mod": w_mod, "b_mod": b_mod, "w_in": w_in, "q_norm_g": q_norm_g, "w_uq": w_uq,
               "kv_norm_g": kv_norm_g, "w_ukv": w_ukv, "conv_w": conv_w, "w_out": w_out, "w_mlp1": w_mlp1,
               "w_mlp2": w_mlp2, "final_norm_g": final_norm_g}
    m_in = {"c_ctx": m_c_ctx, "w_mod": m_w_mod, "b_mod": m_b_mod, "w_in": m_w_in, "q_norm_g": m_q_norm_g,
            "w_uq": m_w_uq, "kv_norm_g": m_kv_norm_g, "w_ukv": m_w_ukv, "conv_w": m_conv_w, "w_out": m_w_out,
            "w_mlp1": m_w_mlp1, "w_mlp2": m_w_mlp2, "final_norm_g": m_final_norm_g}
    v_in = {"c_ctx": v_c_ctx, "w_mod": v_w_mod, "b_mod": v_b_mod, "w_in": v_w_in, "q_norm_g": v_q_norm_g,
            "w_uq": v_w_uq, "kv_norm_g": v_kv_norm_g, "w_ukv": v_w_ukv, "conv_w": v_conv_w, "w_out": v_w_out,
            "w_mlp1": v_w_mlp1, "w_mlp2": v_w_mlp2, "final_norm_g": v_final_norm_g}
    names = list(weights)
    small_names = ["c_ctx", "b_mod", "q_norm_g", "kv_norm_g", "final_norm_g"]
    delta, new_m, new_v = {}, {}, {}

    def flat(a):
        return a.reshape(1, -1)

    packs = [jnp.concatenate([flat(src[n]) for n in small_names], axis=1) for src in (weights, grads, m_in, v_in)]
    small_out = _adamw(*packs, "adamw_small")
    off = 0
    for n in small_names:
        size = weights[n].size
        for dst, arr in zip((delta, new_m, new_v), small_out):
            dst[n] = arr[:, off:off + size].reshape(weights[n].shape)
        off += size
    for n in names:
        if n in small_names:
            continue
        shp = weights[n].shape
        two_d = (shp[0] * shp[1], shp[2])
        wmv = [a.reshape(two_d) for a in (weights[n], m_in[n], v_in[n])]
        if n in slots:
            outs = _adamw(wmv[0], slots[n], wmv[1], wmv[2], "adamw_" + n, slots=True)
            grads[n] = outs[0].reshape(shp)
            outs = outs[1:]
        else:
            outs = _adamw(wmv[0], grads[n].reshape(two_d), wmv[1], wmv[2], "adamw_" + n)
        delta[n], new_m[n], new_v[n] = [a.reshape(shp) for a in outs]

    return (loss, grad_x[None], *[grads[n] for n in names], *[delta[n] for n in names],
            *[new_m[n] for n in names], *[new_v[n] for n in names])
```

```python
import math

import jax
import jax.numpy as jnp
import numpy as np
from jax import lax
from jax.experimental import pallas as pl
from jax.experimental.pallas import tpu as pltpu

F32 = jnp.float32
BF16 = jnp.bfloat16

D_MODEL = 1024
GRID_W = 64
N_HEADS = 8
QK_NOPE = 64
QK_ROPE = 32
V_DIM = 64
Q_RANK = 256
KV_RANK = 128
MLA_IN = Q_RANK + KV_RANK + QK_ROPE
CONV_W = 512
HEAD_COLS = 512
D_FF = 4096
ROPE_THETA = 10000.0
EPS = 1e-6
ATTN_SCALE = 1.0 / math.sqrt(QK_NOPE + QK_ROPE)
LOG2_E = 1.0 / math.log(2.0)
EXP2_SCALE = ATTN_SCALE * LOG2_E
N_DEV = 8
LANES = 128

ADAM_LR, ADAM_B1, ADAM_B2, ADAM_EPS, ADAM_WD, ADAM_STEP = 0.001, 0.9, 0.999, 1e-08, 0.01, 10

ROW_TILE = 256
VMEM_BIG = 60 * 1024 * 1024


def _params(sem=None, vmem=None):
    return pltpu.CompilerParams(dimension_semantics=sem, vmem_limit_bytes=vmem)


def _pick(n, prefs):
    for p in prefs:
        if n % p == 0:
            return p
    return n


def _my_index():
    return 4 * lax.axis_index("x") + 2 * lax.axis_index("y") + lax.axis_index("c")


def _all_gather(arrays, name, in_vmem):
    space = pltpu.VMEM if in_vmem else pl.ANY
    n = len(arrays)

    def body(*refs):
        x_refs, out_refs = refs[:n], refs[n:2 * n]
        send_sems, recv_sems, local_sems = refs[2 * n:]
        x, y, c = lax.axis_index("x"), lax.axis_index("y"), lax.axis_index("c")
        me, sibling = (x, y, c), (x, y, 1 - c)
        chips = [(1 - x, y), (x, 1 - y), (1 - x, 1 - y)]

        def slot(a, px, py, pc):
            return out_refs[a].at[4 * px + 2 * py + pc]

        def copy(a, k, block, to, src=None):
            return pltpu.make_async_remote_copy(
                src_ref=slot(a, *block) if src is None else src, dst_ref=slot(a, *block),
                send_sem=send_sems.at[7 * a + k], recv_sem=recv_sems.at[7 * a + k],
                device_id=to, device_id_type=pl.DeviceIdType.MESH)

        mine = [pltpu.make_async_copy(x_refs[a], slot(a, *me), local_sems.at[a]) for a in range(n)]
        for cp in mine:
            cp.start()
        started = []
        for a in range(n):
            first = [copy(a, 0, me, sibling, src=x_refs[a])]
            first += [copy(a, 1 + j, me, (*chip, c), src=x_refs[a]) for j, chip in enumerate(chips)]
            for cp in first:
                cp.start()
            started += first
        for a in range(n):
            for j, chip in enumerate(chips):
                copy(a, 1 + j, (*chip, c), me).wait_recv()
                passed = copy(a, 4 + j, (*chip, c), sibling)
                passed.start()
                started.append(passed)
        for a in range(n):
            copy(a, 0, sibling, me).wait_recv()
            for j, chip in enumerate(chips):
                copy(a, 4 + j, (*chip, 1 - c), me).wait_recv()
        for cp in started:
            cp.wait_send()
        for cp in mine:
            cp.wait()

    outs = pl.pallas_call(
        body, name=name,
        out_shape=tuple(jax.ShapeDtypeStruct((N_DEV,) + a.shape, a.dtype) for a in arrays),
        in_specs=[pl.BlockSpec(memory_space=space)] * n,
        out_specs=tuple(pl.BlockSpec(memory_space=space) for _ in arrays),
        scratch_shapes=[pltpu.SemaphoreType.DMA((7 * n,)), pltpu.SemaphoreType.DMA((7 * n,)),
                        pltpu.SemaphoreType.DMA((n,))],
    )(*arrays)
    return list(outs)


def _all_to_all(arrays, name):
    n = len(arrays)

    def body(*refs):
        x_refs, y_refs = refs[:n], refs[n:2 * n]
        send_sems, recv_sems, local_sems = refs[2 * n:]
        x, y, c = lax.axis_index("x"), lax.axis_index("y"), lax.axis_index("c")
        me = 4 * x + 2 * y + c
        peers = []
        for k in range(1, N_DEV):
            peers.append((1 - x if k & 4 else x, 1 - y if k & 2 else y, 1 - c if k & 1 else c))

        def copy(a, k, peer, landing):
            pid = 4 * peer[0] + 2 * peer[1] + peer[2]
            return pltpu.make_async_remote_copy(
                src_ref=x_refs[a].at[pid], dst_ref=y_refs[a].at[pid if landing else me],
                send_sem=send_sems.at[7 * a + k], recv_sem=recv_sems.at[7 * a + k],
                device_id=peer, device_id_type=pl.DeviceIdType.MESH)

        mine = [pltpu.make_async_copy(x_refs[a].at[me], y_refs[a].at[me], local_sems.at[a]) for a in range(n)]
        for cp in mine:
            cp.start()
        sends = [copy(a, k, peer, False) for a in range(n) for k, peer in enumerate(peers)]
        for cp in sends:
            cp.start()
        for a in range(n):
            for k, peer in enumerate(peers):
                copy(a, k, peer, True).wait_recv()
        for cp in sends:
            cp.wait_send()
        for cp in mine:
            cp.wait()

    outs = pl.pallas_call(
        body, name=name,
        out_shape=tuple(jax.ShapeDtypeStruct(a.shape, a.dtype) for a in arrays),
        in_specs=[pl.BlockSpec(memory_space=pl.ANY)] * n,
        out_specs=tuple(pl.BlockSpec(memory_space=pl.ANY) for _ in arrays),
        scratch_shapes=[pltpu.SemaphoreType.DMA((7 * n,)), pltpu.SemaphoreType.DMA((7 * n,)),
                        pltpu.SemaphoreType.DMA((n,))],
    )(*arrays)
    return list(outs)


class _Riding:
    def __init__(self, kind, arrays):
        self.kind, self.arrays, self.n = kind, list(arrays), len(arrays)
        lead = (N_DEV,) if kind == "gather" else ()
        self.out_shape = [jax.ShapeDtypeStruct(lead + a.shape, a.dtype) for a in self.arrays]
        self.specs = [pl.BlockSpec(memory_space=pl.ANY)] * self.n
        self.scratch = [pltpu.SemaphoreType.DMA((7 * self.n,)), pltpu.SemaphoreType.DMA((7 * self.n,)),
                        pltpu.SemaphoreType.DMA((self.n,))]

    def copies(self, x_refs, y_refs, send_sems, recv_sems, local_sems):
        x, y, c = lax.axis_index("x"), lax.axis_index("y"), lax.axis_index("c")
        me = 4 * x + 2 * y + c
        local, sends, landings = [], [], []
        for a in range(self.n):
            src_mine = x_refs[a] if self.kind == "gather" else x_refs[a].at[me]
            local.append(pltpu.make_async_copy(src_mine, y_refs[a].at[me], local_sems.at[a]))
            for k in range(1, N_DEV):
                peer = (1 - x if k & 4 else x, 1 - y if k & 2 else y, 1 - c if k & 1 else c)
                pid = 4 * peer[0] + 2 * peer[1] + peer[2]
                src = x_refs[a] if self.kind == "gather" else x_refs[a].at[pid]
                for dst, out in ((me, sends), (pid, landings)):
                    out.append(pltpu.make_async_remote_copy(
                        src_ref=src, dst_ref=y_refs[a].at[dst],
                        send_sem=send_sems.at[7 * a + k - 1], recv_sem=recv_sems.at[7 * a + k - 1],
                        device_id=peer, device_id_type=pl.DeviceIdType.MESH))
        return local, sends, landings

    def run(self, first, last, x_refs, y_refs, sems):
        local, sends, landings = self.copies(x_refs, y_refs, *sems)

        @pl.when(first)
        def _():
            for cp in local + sends:
                cp.start()

        return local, sends, landings, last

    @staticmethod
    def finish(state):
        local, sends, landings, last = state

        @pl.when(last)
        def _():
            for cp in landings:
                cp.wait_recv()
            for cp in sends:
                cp.wait_send()
            for cp in local:
                cp.wait()


_DIMS = {"nn": (((1,), (0,)), ((), ())), "nt": (((1,), (1,)), ((), ())), "tn": (((0,), (0,)), ((), ()))}
NT_DIMS = _DIMS["nt"]
TN_DIMS = _DIMS["tn"]


def _swap8(x):
    lane = lax.broadcasted_iota(jnp.int32, x.shape, 1)
    return jnp.where((lane & 15) < 8, pltpu.roll(x, LANES - 8, 1), pltpu.roll(x, 8, 1))


def _rope(x, cos, sgn, bwd):
    return x * cos + (_swap8(x * sgn) if bwd else _swap8(x) * sgn)


def _matmul(a, b, *, mode, name, out_dtype=F32, tm=512, tn=512, tk=512, m=None, k=None,
            epilogue=None, extra=(), addend=None, slots=None):
    if mode == "nn":
        m = a.shape[0] if m is None else m
        k = a.shape[1]
        n = N_DEV * b.shape[2] if slots == "b_cols" else b.shape[1]
    elif mode == "nt":
        m = a.shape[0] if m is None else m
        k = a.shape[1]
        n = b.shape[1] if slots == "b_contract" else b.shape[0]
    else:
        k = a.shape[0] if k is None else k
        m, n = a.shape[1], b.shape[1]
    tm, tn, tk = min(tm, m), min(tn, n), min(tk, k)
    group = 1
    if slots == "b_cols":
        tn = b.shape[2]
    if slots == "b_contract":
        group = max(1, tk // b.shape[2])
        tk = group * b.shape[2]
    if slots == "out":
        tn = n // N_DEV
    assert m % tm == 0 and n % tn == 0 and k % tk == 0, (name, m, n, k, tm, tn, tk)
    nk = k // tk
    dims = _DIMS[mode]
    a_spec = (pl.BlockSpec((tk, tm), lambda i, j, kk: (kk, i)) if mode == "tn"
              else pl.BlockSpec((tm, tk), lambda i, j, kk: (i, kk)))
    if slots == "b_cols":
        b_spec = pl.BlockSpec((None, tk, tn), lambda i, j, kk: (j, kk, 0))
    elif slots == "b_contract":
        b_spec = pl.BlockSpec((group, tn, tk // group), lambda i, j, kk: (kk, j, 0))
    elif mode == "nt":
        b_spec = pl.BlockSpec((tn, tk), lambda i, j, kk: (j, kk))
    else:
        b_spec = pl.BlockSpec((tk, tn), lambda i, j, kk: (kk, j))
    tile = pl.BlockSpec((tm, tn), lambda i, j, kk: (i, j))
    if slots == "out":
        o_spec = pl.BlockSpec((None, tm, tn), lambda i, j, kk: (j, i, 0))
        o_shape = (N_DEV, m, tn)
    else:
        o_spec, o_shape = tile, (m, n)
    in_specs, args = [a_spec, b_spec], [a, b]
    if epilogue == "drelu2":
        in_specs.append(tile)
    elif epilogue == "rope":
        in_specs += [pl.BlockSpec((tm, LANES), lambda i, j, kk: (i, 0))] * 2
    args += list(extra)
    if addend is not None:
        in_specs.append(tile)
        args.append(addend)
    if epilogue == "relu2":
        out_shape = (jax.ShapeDtypeStruct(o_shape, BF16), jax.ShapeDtypeStruct(o_shape, BF16))
        out_specs = (o_spec, o_spec)
    else:
        out_shape = jax.ShapeDtypeStruct(o_shape, out_dtype)
        out_specs = o_spec
    n_in = len(args)
    n_out = 2 if epilogue == "relu2" else 1

    def body(*refs):
        a_ref, b_ref = refs[0], refs[1]
        outs = refs[n_in:n_in + n_out]
        if slots == "b_contract":
            c = tk // group
            part = lax.dot_general(a_ref[:, 0:c], b_ref[0], dims, preferred_element_type=F32)
            for u in range(1, group):
                part = part + lax.dot_general(a_ref[:, u * c:(u + 1) * c], b_ref[u], dims, preferred_element_type=F32)
        else:
            part = lax.dot_general(a_ref[...], b_ref[...], dims, preferred_element_type=F32)

        def finish(acc):
            if addend is not None:
                acc = acc + refs[n_in - 1][...]
            if epilogue == "relu2":
                outs[0][...] = acc.astype(BF16)
                r = jnp.maximum(acc, 0.0)
                outs[1][...] = (r * r).astype(BF16)
            elif epilogue == "drelu2":
                u = refs[2][...].astype(F32)
                outs[0][...] = (acc * (2.0 * jnp.maximum(u, 0.0))).astype(out_dtype)
            elif epilogue == "rope":
                cos, sgn = refs[2][...], refs[3][...]
                for h in range(tn // LANES):
                    sl = slice(h * LANES, (h + 1) * LANES)
                    outs[0][:, sl] = _rope(acc[:, sl], cos, sgn, False).astype(out_dtype)
            else:
                outs[0][...] = acc.astype(out_dtype)

        if nk == 1:
            finish(part)
        else:
            acc_ref = refs[n_in + n_out]
            kk = pl.program_id(2)

            @pl.when(kk == 0)
            def _():
                acc_ref[...] = part

            @pl.when(kk > 0)
            def _():
                acc_ref[...] += part

            @pl.when(kk == nk - 1)
            def _():
                finish(acc_ref[...])

    return pl.pallas_call(
        body, name=name, grid=(m // tm, n // tn, nk),
        out_shape=out_shape, in_specs=in_specs, out_specs=out_specs,
        scratch_shapes=[pltpu.VMEM((tm, tn), F32)] if nk > 1 else [],
        compiler_params=_params(("parallel", "parallel", "arbitrary"), VMEM_BIG),
    )(*args)


def _rstd(x):
    return lax.rsqrt(jnp.mean(x * x, axis=1, keepdims=True) + EPS)


def _norm_bwd(dxn, xn, r):
    return r * (dxn - xn * jnp.mean(dxn * xn, axis=1, keepdims=True))


def _vec(col):
    return pl.BlockSpec((1, D_MODEL), lambda i: (0, col))


def _modulate_all(x, ctx, mod, mod_ctx, name):
    s, d = x.shape
    t = s + ctx.shape[0]
    ns = s // ROW_TILE
    nc = ctx.shape[0] // ROW_TILE

    def body(x_ref, c_ref, sh_ref, sc_ref, shc_ref, scc_ref, h_ref):
        i = pl.program_id(0)

        @pl.when(i < ns)
        def _():
            v = x_ref[...]
            h_ref[...] = (v * _rstd(v) * (1.0 + sc_ref[...]) + sh_ref[...]).astype(BF16)

        @pl.when(i >= ns)
        def _():
            v = c_ref[...]
            h_ref[...] = (v * _rstd(v) * (1.0 + scc_ref[...]) + shc_ref[...]).astype(BF16)

    return pl.pallas_call(
        body, name=name, grid=(ns + nc,),
        out_shape=jax.ShapeDtypeStruct((t, d), BF16),
        in_specs=[pl.BlockSpec((ROW_TILE, d), lambda i: (jnp.minimum(i, ns - 1), 0)),
                  pl.BlockSpec((ROW_TILE, d), lambda i: (jnp.maximum(i - ns, 0), 0)),
                  _vec(0), _vec(1), _vec(0), _vec(1)],
        out_specs=pl.BlockSpec((ROW_TILE, d), lambda i: (i, 0)),
        compiler_params=_params(("arbitrary",)),
    )(x, ctx, mod, mod, mod_ctx, mod_ctx)


def _resid_modulate(x, o, mod, name):
    s, d = x.shape

    def body(x_ref, o_ref, g_ref, sh_ref, sc_ref, x1_ref, h_ref):
        x1 = x_ref[...] + g_ref[...] * o_ref[...]
        x1_ref[...] = x1
        h_ref[...] = (x1 * _rstd(x1) * (1.0 + sc_ref[...]) + sh_ref[...]).astype(BF16)

    row = pl.BlockSpec((ROW_TILE, d), lambda i: (i, 0))
    return pl.pallas_call(
        body, name=name, grid=(s // ROW_TILE,),
        out_shape=(jax.ShapeDtypeStruct((s, d), F32), jax.ShapeDtypeStruct((s, d), BF16)),
        in_specs=[row, row, _vec(2), _vec(3), _vec(4)], out_specs=(row, row),
        compiler_params=_params(("parallel",)),
    )(x, o, mod, mod, mod)


def _final(x1, m, mod, gain, target, name):
    s, d = x1.shape
    n = s // ROW_TILE

    def body(x1_ref, m_ref, g_ref, gf_ref, t_ref, dx2_ref, dm_ref, sums_ref):
        i = pl.program_id(0)
        mm = m_ref[...]
        x2 = x1_ref[...] + g_ref[...] * mm
        r = _rstd(x2)
        xn = x2 * r
        err = xn * gf_ref[...] - t_ref[...]
        dy = err * (1.0 / d)
        dx2 = _norm_bwd(dy * gf_ref[...], xn, r)
        dx2_ref[...] = dx2
        dm_ref[...] = (dx2 * g_ref[...]).astype(BF16)

        @pl.when(i == 0)
        def _():
            sums_ref[...] = jnp.zeros_like(sums_ref)

        sums_ref[0:1, :] += jnp.sum(dy * xn, axis=0, keepdims=True)
        sums_ref[1:2, :] += jnp.sum(dx2 * mm, axis=0, keepdims=True)
        sums_ref[2:3, :] += jnp.sum(err * err, axis=0, keepdims=True)

        @pl.when(i == n - 1)
        def _():
            tot = jnp.sum(sums_ref[2:3, :], axis=1, keepdims=True) * (0.5 / d)
            sums_ref[3:4, :] = jnp.broadcast_to(tot, (1, d))

    row = pl.BlockSpec((ROW_TILE, d), lambda i: (i, 0))
    vec = pl.BlockSpec((1, d), lambda i: (0, 0))
    return pl.pallas_call(
        body, name=name, grid=(n,),
        out_shape=(jax.ShapeDtypeStruct((s, d), F32), jax.ShapeDtypeStruct((s, d), BF16),
                   jax.ShapeDtypeStruct((8, d), F32)),
        in_specs=[row, row, _vec(5), vec, row],
        out_specs=(row, row, pl.BlockSpec((8, d), lambda i: (0, 0))),
        compiler_params=_params(("arbitrary",)),
    )(x1, m, mod, gain, target)


def _modulate_bwd(dh, row_off, xsrc, mod, scale_col, name, dres=None, o=None):
    s, d = xsrc.shape
    n = s // ROW_TILE
    has_dx, has_o = dres is not None, o is not None
    assert has_dx or not has_o

    def body(*refs):
        it = iter(refs)
        dh_ref, x_ref, sc_ref = next(it), next(it), next(it)
        dres_ref = next(it) if has_dx else None
        o_ref, g_ref = (next(it), next(it)) if has_o else (None, None)
        dx_ref = next(it) if has_dx else None
        do_ref = next(it) if has_o else None
        sums_ref = next(it)
        i = pl.program_id(0)
        x = x_ref[...]
        r = _rstd(x)
        xn = x * r
        dhv = dh_ref[...]

        @pl.when(i == 0)
        def _():
            sums_ref[...] = jnp.zeros_like(sums_ref)

        sums_ref[0:1, :] += jnp.sum(dhv * xn, axis=0, keepdims=True)
        sums_ref[1:2, :] += jnp.sum(dhv, axis=0, keepdims=True)
        if has_dx:
            dx = dres_ref[...] + _norm_bwd(dhv * (1.0 + sc_ref[...]), xn, r)
            dx_ref[...] = dx
            if has_o:
                do_ref[...] = (dx * g_ref[...]).astype(BF16)
                sums_ref[2:3, :] += jnp.sum(dx * o_ref[...], axis=0, keepdims=True)

    row = pl.BlockSpec((ROW_TILE, d), lambda i: (i, 0))
    in_specs = [pl.BlockSpec((ROW_TILE, d), lambda i: (i + row_off, 0)), row, _vec(scale_col)]
    args = [dh, xsrc, mod]
    out_shape, out_specs = [], []
    if has_dx:
        in_specs.append(row)
        args.append(dres)
        out_shape.append(jax.ShapeDtypeStruct((s, d), F32))
        out_specs.append(row)
    if has_o:
        in_specs += [row, _vec(2)]
        args += [o, mod]
        out_shape.append(jax.ShapeDtypeStruct((s, d), BF16))
        out_specs.append(row)
    out_shape.append(jax.ShapeDtypeStruct((8, d), F32))
    out_specs.append(pl.BlockSpec((8, d), lambda i: (0, 0)))
    return pl.pallas_call(
        body, name=name, grid=(n,),
        out_shape=tuple(out_shape), in_specs=in_specs, out_specs=tuple(out_specs),
        compiler_params=_params(("arbitrary",)),
    )(*args)


def _qkv_prep(z, q_gain, kv_gain, cos, sgn, name):
    t = z.shape[0]

    def body(z_ref, qg_ref, kg_ref, c_ref, s_ref, cq_ref, kv_ref):
        zq = z_ref[:, 0:Q_RANK]
        cq_ref[...] = (zq * _rstd(zq) * qg_ref[...]).astype(BF16)
        zk = z_ref[:, Q_RANK:Q_RANK + KV_RANK]
        kv_ref[:, 0:KV_RANK] = (zk * _rstd(zk) * kg_ref[...]).astype(BF16)
        kr = z_ref[:, Q_RANK + KV_RANK:HEAD_COLS]
        kv_ref[:, KV_RANK:KV_RANK + LANES] = _rope(kr, c_ref[...], s_ref[...], False).astype(BF16)

    tab = pl.BlockSpec((ROW_TILE, LANES), lambda i: (i, 0))
    return pl.pallas_call(
        body, name=name, grid=(t // ROW_TILE,),
        out_shape=(jax.ShapeDtypeStruct((t, Q_RANK), BF16), jax.ShapeDtypeStruct((t, KV_RANK + LANES), BF16)),
        in_specs=[pl.BlockSpec((ROW_TILE, HEAD_COLS), lambda i: (i, 0)),
                  pl.BlockSpec((1, Q_RANK), lambda i: (0, 0)), pl.BlockSpec((1, KV_RANK), lambda i: (0, 0)), tab, tab],
        out_specs=(pl.BlockSpec((ROW_TILE, Q_RANK), lambda i: (i, 0)),
                   pl.BlockSpec((ROW_TILE, KV_RANK + LANES), lambda i: (i, 0))),
        compiler_params=_params(("parallel",)),
    )(z, q_gain, kv_gain, cos, sgn)


def _qkv_prep_bwd(z, dcq, dkv, q_gain, kv_gain, cos, sgn, s, name):
    t = z.shape[0]
    ns = s // ROW_TILE

    def body(z_ref, dcq_ref, dkv_ref, qg_ref, kg_ref, c_ref, s_ref, dz_ref, sums_ref):
        i = pl.program_id(0)

        @pl.when(i == 0)
        def _():
            sums_ref[...] = jnp.zeros_like(sums_ref)

        @pl.when(i < ns)
        def _():
            zq = z_ref[:, 0:Q_RANK]
            r = _rstd(zq)
            zn = zq * r
            dc = dcq_ref[...]
            sums_ref[0:1, :] += jnp.sum(dc * zn, axis=0, keepdims=True)
            dz_ref[:, 0:Q_RANK] = _norm_bwd(dc * qg_ref[...], zn, r).astype(BF16)

        @pl.when(i >= ns)
        def _():
            dz_ref[:, 0:Q_RANK] = jnp.zeros((ROW_TILE, Q_RANK), BF16)

        zk = z_ref[:, Q_RANK:Q_RANK + KV_RANK]
        r = _rstd(zk)
        zn = zk * r
        dc = dkv_ref[:, 0:KV_RANK]
        sums_ref[1:2, 0:KV_RANK] += jnp.sum(dc * zn, axis=0, keepdims=True)
        dz_ref[:, Q_RANK:Q_RANK + KV_RANK] = _norm_bwd(dc * kg_ref[...], zn, r).astype(BF16)
        dkr = dkv_ref[:, KV_RANK:KV_RANK + LANES]
        dz_ref[:, Q_RANK + KV_RANK:HEAD_COLS] = _rope(dkr, c_ref[...], s_ref[...], True).astype(BF16)

    tab = pl.BlockSpec((ROW_TILE, LANES), lambda i: (i, 0))
    return pl.pallas_call(
        body, name=name, grid=(t // ROW_TILE,),
        out_shape=(jax.ShapeDtypeStruct((t, HEAD_COLS), BF16), jax.ShapeDtypeStruct((8, Q_RANK), F32)),
        in_specs=[pl.BlockSpec((ROW_TILE, HEAD_COLS), lambda i: (i, 0)),
                  pl.BlockSpec((ROW_TILE, Q_RANK), lambda i: (jnp.minimum(i, ns - 1), 0)),
                  pl.BlockSpec((ROW_TILE, KV_RANK + LANES), lambda i: (i, 0)),
                  pl.BlockSpec((1, Q_RANK), lambda i: (0, 0)), pl.BlockSpec((1, KV_RANK), lambda i: (0, 0)), tab, tab],
        out_specs=(pl.BlockSpec((ROW_TILE, HEAD_COLS), lambda i: (i, 0)), pl.BlockSpec((8, Q_RANK), lambda i: (0, 0))),
        compiler_params=_params(("arbitrary",)),
    )(z, dcq, dkv, q_gain, kv_gain, cos, sgn)


def _shift_rows(u, s):
    rowi = lax.broadcasted_iota(jnp.int32, u.shape, 0)
    prev = jnp.where(rowi == 0, 0.0, pltpu.roll(u, 1, 0))
    nxt = jnp.where(rowi == s - 1, 0.0, pltpu.roll(u, s - 1, 0))
    return prev, nxt


def _conv_fwd(z_conv, cw, a_cat, name):
    s = z_conv.shape[0]

    def body(z_ref, w_ref, a_in_ref, o_ref):
        del a_in_ref
        gb, gc, xv = z_ref[:, 0:LANES], z_ref[:, LANES:2 * LANES], z_ref[:, 2 * LANES:3 * LANES]
        u = gc * xv
        prev, nxt = _shift_rows(u, s)
        y = w_ref[0:1, :] * prev + w_ref[1:2, :] * u + w_ref[2:3, :] * nxt
        o_ref[...] = (gb * y).astype(BF16)

    return pl.pallas_call(
        body, name=name, grid=(CONV_W // LANES,),
        out_shape=jax.ShapeDtypeStruct(a_cat.shape, a_cat.dtype),
        in_specs=[pl.BlockSpec((s, 3 * LANES), lambda j: (0, j)), pl.BlockSpec((3, LANES), lambda j: (0, j)),
                  pl.BlockSpec(memory_space=pl.ANY)],
        out_specs=pl.BlockSpec((s, LANES), lambda j: (0, 4 + j)),
        input_output_aliases={2: 0},
        compiler_params=_params(("parallel",), VMEM_BIG),
    )(z_conv, cw, a_cat)


def _conv_bwd(z_conv, cw, da, name):
    s = z_conv.shape[0]

    def body(z_ref, w_ref, da_ref, dz_ref, dw_ref):
        gb, gc, xv = z_ref[:, 0:LANES], z_ref[:, LANES:2 * LANES], z_ref[:, 2 * LANES:3 * LANES]
        u = gc * xv
        prev, nxt = _shift_rows(u, s)
        dcv = da_ref[...]
        dz_ref[:, 0:LANES] = (dcv * (w_ref[0:1, :] * prev + w_ref[1:2, :] * u + w_ref[2:3, :] * nxt)).astype(BF16)
        dy = dcv * gb
        dw_ref[0:1, :] = jnp.sum(dy * prev, axis=0, keepdims=True)
        dw_ref[1:2, :] = jnp.sum(dy * u, axis=0, keepdims=True)
        dw_ref[2:3, :] = jnp.sum(dy * nxt, axis=0, keepdims=True)
        dyp, dyn = _shift_rows(dy, s)
        du = w_ref[0:1, :] * dyn + w_ref[1:2, :] * dy + w_ref[2:3, :] * dyp
        dz_ref[:, LANES:2 * LANES] = (du * xv).astype(BF16)
        dz_ref[:, 2 * LANES:3 * LANES] = (du * gc).astype(BF16)

    blk = pl.BlockSpec((s, 3 * LANES), lambda j: (0, j))
    cws = pl.BlockSpec((3, LANES), lambda j: (0, j))
    return pl.pallas_call(
        body, name=name, grid=(CONV_W // LANES,),
        out_shape=(jax.ShapeDtypeStruct(z_conv.shape, BF16), jax.ShapeDtypeStruct((3, CONV_W), F32)),
        in_specs=[blk, cws, pl.BlockSpec((s, LANES), lambda j: (0, 4 + j))], out_specs=(blk, cws),
        compiler_params=_params(("parallel",), VMEM_BIG),
    )(z_conv, cw, da)


ATT_TQ = 256
ATT_TQ_BWD = 512


def _head_mask(shape, hh):
    lane = lax.broadcasted_iota(jnp.int32, shape, 1)
    return (lane >= hh * V_DIM) & (lane < (hh + 1) * V_DIM)


def _attn_fwd(qf, kv, riding, name):
    s, t = qf.shape[0], kv.shape[0]
    nq = s // ATT_TQ
    nr = riding.n

    def body(*refs):
        q_ref, k_ref, v_ref = refs[:3]
        o_ref, ob_ref, st_ref = refs[3 + nr:6 + nr]
        p, i = pl.program_id(0), pl.program_id(1)
        state = riding.run((p == 0) & (i == 0), (p == N_HEADS // 2 - 1) & (i == nq - 1),
                           refs[3:3 + nr], refs[6 + nr:6 + 2 * nr], refs[6 + 2 * nr:])
        v = v_ref[...]
        vlane = lax.broadcasted_iota(jnp.int32, v.shape, 1)
        olane = lax.broadcasted_iota(jnp.int32, (ATT_TQ, LANES), 1)
        acc = jnp.zeros((ATT_TQ, LANES), F32)
        stat = jnp.zeros((ATT_TQ, LANES), F32)
        for hh in range(2):
            sl = slice(hh * LANES, (hh + 1) * LANES)
            sc = lax.dot_general(q_ref[:, sl], k_ref[:, sl], NT_DIMS, preferred_element_type=F32)
            mx = jnp.max(sc, axis=1, keepdims=True)
            e = jnp.exp2((sc - mx) * EXP2_SCALE).astype(BF16)
            one_lane = (1 - hh) * V_DIM
            vm = jnp.where(_head_mask(v.shape, hh), v, jnp.where(vlane == one_lane, 1.0, 0.0).astype(BF16))
            r = jnp.dot(e, vm, preferred_element_type=F32)
            den = jnp.sum(jnp.where(olane == one_lane, r, 0.0), axis=1, keepdims=True)
            acc = acc + jnp.where(_head_mask(r.shape, hh), r * (1.0 / den), 0.0)
            stat = stat + jnp.where(olane == hh, mx * EXP2_SCALE + jnp.log(den) * LOG2_E, 0.0)
        o_ref[...] = acc
        ob_ref[...] = acc.astype(BF16)
        st_ref[...] = stat.T[0:8, :]
        riding.finish(state)

    o_spec = pl.BlockSpec((ATT_TQ, LANES), lambda p, i: (i, p))
    outs = pl.pallas_call(
        body, name=name, grid=(N_HEADS // 2, nq),
        out_shape=(jax.ShapeDtypeStruct((s, N_HEADS * V_DIM), F32),
                   jax.ShapeDtypeStruct((s, D_MODEL), BF16),
                   jax.ShapeDtypeStruct((N_HEADS // 2 * 8, s), F32), *riding.out_shape),
        in_specs=[pl.BlockSpec((ATT_TQ, 2 * LANES), lambda p, i: (i, p)),
                  pl.BlockSpec((t, 2 * LANES), lambda p, i: (0, p)),
                  pl.BlockSpec((t, LANES), lambda p, i: (0, N_HEADS + p)), *riding.specs],
        out_specs=(o_spec, o_spec, pl.BlockSpec((8, ATT_TQ), lambda p, i: (p, i)), *riding.specs),
        scratch_shapes=riding.scratch,
        compiler_params=_params(("arbitrary", "arbitrary"), VMEM_BIG),
    )(qf, kv, kv, *riding.arrays)
    return outs[0], outs[1], outs[2], list(outs[3:])


def _attn_bwd(qf, kv, o, da, stats, cos, sgn, riding, name):
    s, t = qf.shape[0], kv.shape[0]
    ATT_TQ = ATT_TQ_BWD
    nq = s // ATT_TQ
    nr = riding.n

    def body(*refs):
        q_ref, k_ref, v_ref, o_ref, do_ref, st_ref, c_ref, s_ref = refs[:8]
        dq_ref, dk_ref, dv_ref = refs[8 + nr:11 + nr]
        dk_acc, dv_acc = refs[11 + 2 * nr:13 + 2 * nr]
        p, i = pl.program_id(0), pl.program_id(1)
        state = riding.run((p == 0) & (i == 0), (p == N_HEADS // 2 - 1) & (i == nq - 1),
                           refs[8:8 + nr], refs[11 + nr:11 + 2 * nr], refs[13 + 2 * nr:])

        @pl.when(i == 0)
        def _():
            dk_acc[...] = jnp.zeros_like(dk_acc)
            dv_acc[...] = jnp.zeros_like(dv_acc)

        v = v_ref[...]
        do = do_ref[...]
        od = do * o_ref[...]
        ones = jnp.ones((8, LANES), F32)
        for hh in range(2):
            sl = slice(hh * LANES, (hh + 1) * LANES)
            q, k = q_ref[:, sl], k_ref[:, sl]
            mask = _head_mask(do.shape, hh)
            dom = jnp.where(mask, do, 0.0).astype(BF16)
            delta = lax.dot_general(ones, jnp.where(mask, od, 0.0), NT_DIMS, preferred_element_type=F32,
                                    precision=lax.Precision.HIGHEST)[0:1, :]
            st = lax.dot_general(k, q, NT_DIMS, preferred_element_type=F32)
            pt = jnp.exp2(st * EXP2_SCALE - st_ref[hh:hh + 1, :]).astype(BF16)
            dpt = lax.dot_general(v, dom, NT_DIMS, preferred_element_type=F32)
            dst = (pt.astype(F32) * (dpt - delta)).astype(BF16)
            dv_acc[...] += jnp.dot(pt, dom, preferred_element_type=F32)
            dk_acc[:, sl] += jnp.dot(dst, q, preferred_element_type=F32)
            dq = lax.dot_general(dst, k, TN_DIMS, preferred_element_type=F32) * ATTN_SCALE
            dq_ref[:, sl] = _rope(dq, c_ref[...], s_ref[...], True).astype(BF16)

        @pl.when(i == nq - 1)
        def _():
            dk_ref[...] = (dk_acc[...] * ATTN_SCALE).astype(BF16)
            dv_ref[...] = dv_acc[...].astype(BF16)

        riding.finish(state)

    o_spec = pl.BlockSpec((ATT_TQ, LANES), lambda p, i: (i, p))
    tab = pl.BlockSpec((ATT_TQ, LANES), lambda p, i: (i, 0))
    outs = pl.pallas_call(
        body, name=name, grid=(N_HEADS // 2, nq),
        out_shape=(jax.ShapeDtypeStruct((s, N_HEADS * LANES), BF16),
                   jax.ShapeDtypeStruct((t, N_HEADS * LANES), BF16),
                   jax.ShapeDtypeStruct((t, N_HEADS * V_DIM), BF16), *riding.out_shape),
        in_specs=[pl.BlockSpec((ATT_TQ, 2 * LANES), lambda p, i: (i, p)),
                  pl.BlockSpec((t, 2 * LANES), lambda p, i: (0, p)),
                  pl.BlockSpec((t, LANES), lambda p, i: (0, N_HEADS + p)),
                  o_spec, o_spec,
                  pl.BlockSpec((8, ATT_TQ), lambda p, i: (p, i)), tab, tab, *riding.specs],
        out_specs=(pl.BlockSpec((ATT_TQ, 2 * LANES), lambda p, i: (i, p)),
                   pl.BlockSpec((t, 2 * LANES), lambda p, i: (0, p)),
                   pl.BlockSpec((t, LANES), lambda p, i: (0, p)), *riding.specs),
        scratch_shapes=[pltpu.VMEM((t, 2 * LANES), F32), pltpu.VMEM((t, LANES), F32), *riding.scratch],
        compiler_params=_params(("arbitrary", "arbitrary"), VMEM_BIG),
    )(qf, kv, kv, o, da, stats, cos, sgn, *riding.arrays)
    return outs[0], outs[1], outs[2], list(outs[3:])


def _silu(x):
    return x * (1.0 / (1.0 + jnp.exp(-x)))


def _adaln_fwd(a, w, b, name):
    def body(a_ref, w_ref, b_ref, o_ref):
        o_ref[...] = jnp.dot(_silu(a_ref[...]), w_ref[...], preferred_element_type=F32,
                             precision=lax.Precision.HIGHEST) + b_ref[...]

    return pl.pallas_call(
        body, name=name, out_shape=jax.ShapeDtypeStruct((a.shape[0], w.shape[1]), F32),
        compiler_params=_params(None, VMEM_BIG),
    )(a, w, b)


def _adaln_bwd(a_t, w, d_ex, d_ctx, d_all, name):
    def body(at_ref, w_ref, dex_ref, dctx_ref, dall_ref, gw_ref, dsil_ref, dsum_ref):
        sil_t = _silu(at_ref[...])
        dctx = dctx_ref[...]
        row = dctx[0:1, :]
        for j in range(1, N_DEV):
            row = row + dctx[j:j + 1, :]
        rowi = lax.broadcasted_iota(jnp.int32, dctx.shape, 0)
        ctx_rows = jnp.where(rowi == 0, jnp.broadcast_to(row, dctx.shape), 0.0)
        hi = lax.Precision.HIGHEST
        d_rows = jnp.concatenate([dex_ref[...], ctx_rows], axis=0)
        gw_ref[...] = jnp.dot(sil_t, d_rows, preferred_element_type=F32, precision=hi)
        dsil_ref[...] = lax.dot_general(ctx_rows, w_ref[...], NT_DIMS, preferred_element_type=F32, precision=hi)
        tot = dall_ref[0]
        for j in range(1, N_DEV):
            tot = tot + dall_ref[j]
        dsum_ref[...] = tot

    return pl.pallas_call(
        body, name=name,
        out_shape=(jax.ShapeDtypeStruct(w.shape, F32), jax.ShapeDtypeStruct((8, w.shape[0]), F32),
                   jax.ShapeDtypeStruct(d_all.shape[1:], F32)),
        compiler_params=_params(None, VMEM_BIG),
    )(a_t, w, d_ex, d_ctx, d_all)


def _pack_small(sums1, sums2, fsums, sums1c, psums, d_cw, name):
    d = D_MODEL

    def body(s1_ref, s2_ref, f_ref, s1c_ref, p_ref, cw_ref, o_ref):
        o_ref[...] = jnp.zeros_like(o_ref)
        for col, (ref, r) in enumerate([(s1_ref, 1), (s1_ref, 0), (s2_ref, 2), (s2_ref, 1), (s2_ref, 0), (f_ref, 1)]):
            o_ref[0:1, col * d:(col + 1) * d] = ref[r:r + 1, :]
        o_ref[1:2, 0:d] = s1c_ref[1:2, :]
        o_ref[1:2, d:2 * d] = s1c_ref[0:1, :]
        o_ref[2:3, 0:Q_RANK] = p_ref[0:1, :]
        o_ref[2:3, Q_RANK:Q_RANK + KV_RANK] = p_ref[1:2, 0:KV_RANK]
        o_ref[2:3, Q_RANK + KV_RANK:Q_RANK + KV_RANK + d] = f_ref[0:1, :]
        for r in range(3):
            o_ref[3 + r:4 + r, 0:CONV_W] = cw_ref[r:r + 1, :]
        o_ref[6:7, 0:d] = f_ref[3:4, :]

    return pl.pallas_call(body, name=name, out_shape=jax.ShapeDtypeStruct((8, 6 * d), F32))(
        sums1, sums2, fsums, sums1c, psums, d_cw)


def _adam_math(w, g, m, v):
    nm = ADAM_B1 * m + (1.0 - ADAM_B1) * g
    nv = ADAM_B2 * v + (1.0 - ADAM_B2) * (g * g)
    m_hat = nm / (1.0 - ADAM_B1 ** ADAM_STEP)
    v_hat = nv / (1.0 - ADAM_B2 ** ADAM_STEP)
    return -ADAM_LR * (m_hat / (jnp.sqrt(v_hat) + ADAM_EPS) + ADAM_WD * w), nm, nv


def _small_update(dsum, dsil_all, g_cw, params, name):
    d = D_MODEL
    n = len(params)

    def body(*refs):
        dsum_ref, dsil_ref, gcw_ref = refs[:3]
        wmv = refs[3:3 + 3 * n]
        outs = refs[3 + 3 * n:]
        tot = dsil_ref[0]
        for j in range(1, N_DEV):
            tot = tot + dsil_ref[j]
        cv = wmv[0][...]
        sg = 1.0 / (1.0 + jnp.exp(-cv))
        off = Q_RANK + KV_RANK
        grads = [tot[0:1, :] * (sg * (1.0 + cv * (1.0 - sg))),
                 dsum_ref[0:1, :] + dsum_ref[1:2, :],
                 dsum_ref[2:3, 0:Q_RANK], dsum_ref[2:3, Q_RANK:off], dsum_ref[2:3, off:off + d],
                 gcw_ref[...]]
        for p, g in enumerate(grads):
            w_ref, m_ref, v_ref = wmv[3 * p:3 * p + 3]
            delta, nm, nv = _adam_math(w_ref[...], g, m_ref[...], v_ref[...])
            outs[4 * p][...] = g
            outs[4 * p + 1][...] = delta
            outs[4 * p + 2][...] = nm
            outs[4 * p + 3][...] = nv

    flat = [a for wmv in params for a in wmv]
    out_shape = tuple(jax.ShapeDtypeStruct(wmv[0].shape, F32) for wmv in params for _ in range(4))
    outs = pl.pallas_call(body, name=name, out_shape=out_shape)(dsum, dsil_all, g_cw, *flat)
    return [outs[4 * p:4 * p + 4] for p in range(n)]


def _adamw(w, g, m, v, name, slots=False):
    rows, cols = w.shape
    tr = _pick(rows, (256, 128, 64, 32, 16, 8))

    def body(w_ref, g_ref, m_ref, v_ref, *outs):
        if slots:
            gv = g_ref[0].astype(F32)
            for j in range(1, N_DEV):
                gv = gv + g_ref[j].astype(F32)
            outs[0][...] = gv
        else:
            gv = g_ref[...]
        d_ref, nm_ref, nv_ref = outs[-3:]
        d_ref[...], nm_ref[...], nv_ref[...] = _adam_math(w_ref[...], gv, m_ref[...], v_ref[...])

    blk = pl.BlockSpec((tr, cols), lambda i: (i, 0))
    g_spec = pl.BlockSpec((N_DEV, tr, cols), lambda i: (0, i, 0)) if slots else blk
    sh = jax.ShapeDtypeStruct((rows, cols), F32)
    n_out = 4 if slots else 3
    return pl.pallas_call(
        body, name=name, grid=(rows // tr,), out_shape=(sh,) * n_out,
        in_specs=[blk, g_spec, blk, blk], out_specs=(blk,) * n_out,
        compiler_params=_params(("parallel",)),
    )(w, g, m, v)


def _rope_tables(s, l):
    tok = np.arange(s)
    row = (tok // GRID_W).astype(np.float32)
    col = (tok % GRID_W).astype(np.float32)
    half = QK_ROPE // 2
    freqs = np.float32(ROPE_THETA) ** (-np.arange(0, half, 2, dtype=np.float32) / np.float32(half))
    dd = np.arange(QK_ROPE)
    pos = np.where((dd // half)[None, :] == 0, row[:, None], col[:, None]).astype(np.float32)
    ang = (pos * freqs[dd % (half // 2)][None, :]).astype(np.float32)
    sin = np.sin(ang).astype(np.float32)
    cos_t = np.ones((s + l, LANES), np.float32)
    sgn_t = np.zeros((s + l, LANES), np.float32)
    cos_t[:s, QK_NOPE:QK_NOPE + QK_ROPE] = np.cos(ang)
    sgn_t[:s, QK_NOPE:QK_NOPE + QK_ROPE] = np.where(((dd % half) // (half // 2))[None, :] == 0, -sin, sin)
    return jnp.asarray(cos_t), jnp.asarray(sgn_t)


def _slots_to_cols(g):
    return g.transpose(1, 0, 2).reshape(g.shape[1], N_DEV * g.shape[2])


def _cols_to_slots(w):
    return w.reshape(w.shape[0], N_DEV, w.shape[1] // N_DEV).transpose(1, 0, 2)


def _unpack_small_weights(g_in, g_uq, g_ukv):
    w_in = _slots_to_cols(g_in)
    zeros = jnp.zeros((D_MODEL, QK_NOPE), BF16)
    win_head = jnp.concatenate([w_in[:, :Q_RANK + KV_RANK], zeros, w_in[:, Q_RANK + KV_RANK:MLA_IN],
                                zeros[:, :LANES - QK_NOPE - QK_ROPE]], axis=1)
    win_conv = w_in[:, MLA_IN:].reshape(D_MODEL, 3, CONV_W // LANES, LANES).transpose(0, 2, 1, 3)
    win_conv = win_conv.reshape(D_MODEL, 3 * CONV_W)
    w_uq = _slots_to_cols(g_uq).reshape(Q_RANK, N_HEADS, QK_NOPE + QK_ROPE)
    wq = jnp.pad(w_uq, ((0, 0), (0, 0), (0, LANES - QK_NOPE - QK_ROPE))).reshape(Q_RANK, N_HEADS * LANES)
    w_ukv = _slots_to_cols(g_ukv).reshape(KV_RANK, N_HEADS, QK_NOPE + V_DIM)
    k_top = jnp.pad(w_ukv[:, :, :QK_NOPE], ((0, 0), (0, 0), (0, LANES - QK_NOPE))).reshape(KV_RANK, N_HEADS * LANES)
    v_top = w_ukv[:, :, QK_NOPE:].reshape(KV_RANK, N_HEADS * V_DIM)
    eye = jnp.pad(jnp.eye(QK_ROPE, dtype=BF16), ((QK_NOPE, LANES - QK_NOPE - QK_ROPE),) * 2)
    wk = jnp.concatenate([
        jnp.concatenate([k_top, v_top], axis=1),
        jnp.concatenate([jnp.tile(eye, (1, N_HEADS)), jnp.zeros((LANES, N_HEADS * V_DIM), BF16)], axis=1)], axis=0)
    return win_head, win_conv, wq, wk


def _pack_small_grads(d_head, d_conv, d_wq, d_wkk, d_wkv):
    d_conv = d_conv.reshape(D_MODEL, CONV_W // LANES, 3, LANES).transpose(0, 2, 1, 3).reshape(D_MODEL, 3 * CONV_W)
    g_in = jnp.concatenate([d_head[:, :Q_RANK + KV_RANK],
                            d_head[:, Q_RANK + KV_RANK + QK_NOPE:Q_RANK + KV_RANK + QK_NOPE + QK_ROPE], d_conv], axis=1)
    g_uq = d_wq.reshape(Q_RANK, N_HEADS, LANES)[:, :, :QK_NOPE + QK_ROPE].reshape(Q_RANK, -1)
    g_kn = d_wkk[:KV_RANK].reshape(KV_RANK, N_HEADS, LANES)[:, :, :QK_NOPE]
    g_v = d_wkv[:KV_RANK].reshape(KV_RANK, N_HEADS, V_DIM)
    g_ukv = jnp.concatenate([g_kn, g_v], axis=2).reshape(KV_RANK, -1)
    return [_cols_to_slots(g).astype(BF16) for g in (g_in, g_uq, g_ukv)]


def kernel(x, c, ctx, c_ctx, w_mod, b_mod, w_in, q_norm_g, w_uq, kv_norm_g, w_ukv, conv_w, w_out, w_mlp1, w_mlp2, final_norm_g, loss_target, m_c_ctx, m_w_mod, m_b_mod, m_w_in, m_q_norm_g, m_w_uq, m_kv_norm_g, m_w_ukv, m_conv_w, m_w_out, m_w_mlp1, m_w_mlp2, m_final_norm_g, v_c_ctx, v_w_mod, v_b_mod, v_w_in, v_q_norm_g, v_w_uq, v_kv_norm_g, v_w_ukv, v_conv_w, v_w_out, v_w_mlp1, v_w_mlp2, v_final_norm_g):
    me = _my_index()
    x2d, ctx2d, tgt = x[0], ctx[0], loss_target[0]
    s, l = x2d.shape[0], ctx2d.shape[0]
    t = s + l
    d = D_MODEL
    mod_cols = w_mod.shape[2]
    cw_cols = conv_w.shape[2]

    (c_all,) = _all_gather([jnp.pad(c, ((0, 7), (0, 0)))], "gather_c", True)
    a_rows = jnp.concatenate([c_all[:, 0, :], c_ctx[None, :], jnp.zeros((7, d), F32)], axis=0)
    b_cols = lax.dynamic_slice(b_mod, (0, me * mod_cols), (1, mod_cols))
    mod_cols_all = _adaln_fwd(a_rows, w_mod[0], b_cols, "adaln_fwd")
    cw_blk = jnp.pad(conv_w[0], ((0, 5), (0, mod_cols - cw_cols)))
    (gathered,) = _all_gather([jnp.concatenate([mod_cols_all, cw_blk], axis=0)], "gather_mod", True)
    mod_mine = lax.dynamic_index_in_dim(gathered, me, axis=1, keepdims=False).reshape(1, 6 * d)
    mod_ctx = gathered[:, 8, :].reshape(1, 6 * d)
    cw_full = gathered[:, 16:19, :cw_cols].transpose(1, 0, 2).reshape(3, CONV_W)

    early = [w.astype(BF16) for w in (w_in[0], w_uq[0], w_ukv[0])]
    late = [w.astype(BF16) for w in (w_out[0], w_mlp1[0], w_mlp2[0])]
    g_in, g_uq, g_ukv = _all_gather(early, "gather_weights", False)
    win_head, win_conv, wq, wk = _unpack_small_weights(g_in, g_uq, g_ukv)
    wk_k, wk_v = wk[:, :N_HEADS * LANES], wk[:, N_HEADS * LANES:]
    cos, sgn = _rope_tables(s, l)

    h_all = _modulate_all(x2d, ctx2d, mod_mine, mod_ctx, "modulate1")
    tm_t = _pick(t, (1088, 768, 256))
    tk_t = _pick(t, (2176, 768, 256))
    z_head = _matmul(h_all, win_head, mode="nn", name="in_proj_head", tm=tm_t, tn=512, tk=1024)
    z_conv = _matmul(h_all, win_conv, mode="nn", name="in_proj_conv", m=s, tm=1024, tn=1536, tk=1024)
    cq, kv_in = _qkv_prep(z_head, q_norm_g, kv_norm_g, cos, sgn, "qkv_prep")
    qf = _matmul(cq, wq, mode="nn", name="q_up", out_dtype=BF16, m=s, tm=1024, tn=1024, tk=256,
                 epilogue="rope", extra=(cos, sgn))
    kv = _matmul(kv_in, wk, mode="nn", name="kv_up", out_dtype=BF16, tm=tm_t, tn=1536, tk=256)
    attn, a_cat, stats, (g_out, w1, g_w2) = _attn_fwd(qf, kv, _Riding("gather", late), "attn_fwd")
    wo = g_out.reshape(d, d)
    w2 = g_w2.reshape(D_FF, d)
    a_cat = _conv_fwd(z_conv, cw_full, a_cat, "conv_fwd")
    o = _matmul(a_cat, wo, mode="nn", name="out_proj", tm=1024, tn=1024, tk=1024)
    x1, h2 = _resid_modulate(x2d, o, mod_mine, "resid_modulate2")
    u1, act = _matmul(h2, w1, mode="nn", name="mlp_up", tm=2048, tk=1024, epilogue="relu2", slots="b_cols")
    mlp = _matmul(act, w2, mode="nn", name="mlp_down", tm=1024, tn=1024, tk=4096)
    dx2, dm, fsums = _final(x1, mlp, mod_mine, final_norm_g[None, :], tgt, "final_loss")

    d_w2 = _matmul(act, dm, mode="tn", name="d_w_mlp2", out_dtype=BF16, tm=2048, tn=1024, tk=1024)
    du1 = _matmul(dm, w2, mode="nt", name="d_act", out_dtype=BF16, tm=2048, tn=1024, tk=1024,
                  epilogue="drelu2", extra=(u1,))
    d_w1 = _matmul(h2, du1, mode="tn", name="d_w_mlp1", out_dtype=BF16, tm=1024, tk=4096, slots="out")
    dh2 = _matmul(du1, w1, mode="nt", name="d_h2", tm=1024, tn=1024, tk=2048, slots="b_contract")
    dx1, do, sums2 = _modulate_bwd(dh2, 0, x1, mod_mine, 4, "modulate2_bwd", dres=dx2, o=o)
    d_wo = _matmul(a_cat, do, mode="tn", name="d_w_out", out_dtype=BF16, tm=1024, tn=1024, tk=2048)
    da = _matmul(do, wo, mode="nt", name="d_a", tm=1024, tn=1024, tk=1024)
    dz_conv, d_cw = _conv_bwd(z_conv, cw_full, da, "conv_bwd")
    ready = [d_wo.reshape(N_DEV, d // N_DEV, d), d_w1, d_w2.reshape(N_DEV, D_FF // N_DEV, d)]
    dq, dk, dv, rode = _attn_bwd(qf, kv, attn, da, stats, cos, sgn, _Riding("exchange", ready), "attn_bwd")
    d_wq = _matmul(cq, dq, mode="tn", name="d_w_uq", k=s, tm=256, tn=1024, tk=4096)
    dcq = _matmul(dq, wq, mode="nt", name="d_cq", tm=1024, tn=256, tk=1024)
    d_wkk = _matmul(kv_in, dk, mode="tn", name="d_w_ukv_k", tm=256, tn=1024, tk=tk_t)
    d_wkv = _matmul(kv_in, dv, mode="tn", name="d_w_ukv_v", tm=256, tn=512, tk=tk_t)
    dkv_in = _matmul(dk, wk_k, mode="nt", name="d_kv_in_k", tm=tm_t, tn=256, tk=1024)
    dkv_in = _matmul(dv, wk_v, mode="nt", name="d_kv_in_v", tm=tm_t, tn=256, tk=512, addend=dkv_in)
    dz_head, psums = _qkv_prep_bwd(z_head, dcq, dkv_in, q_norm_g, kv_norm_g, cos, sgn, s, "qkv_prep_bwd")
    d_head = _matmul(h_all, dz_head, mode="tn", name="d_w_in_head", tm=1024, tn=512, tk=tk_t)
    d_conv = _matmul(h_all, dz_conv, mode="tn", name="d_w_in_conv", k=s, tm=1024, tn=1536, tk=2048)
    dh_head = _matmul(dz_head, win_head, mode="nt", name="d_h1_head", tm=tm_t, tn=1024, tk=512)
    dh = _matmul(dz_conv, win_conv, mode="nt", name="d_h1", tm=1024, tn=1024, tk=1536, addend=dh_head)
    grad_x, sums1 = _modulate_bwd(dh, 0, x2d, mod_mine, 1, "modulate1_bwd", dres=dx1)
    (sums1c,) = _modulate_bwd(dh_head, s // ROW_TILE, ctx2d, mod_ctx, 1, "modulate1_ctx_bwd")

    small = _pack_small(sums1, sums2, fsums, sums1c, psums, d_cw, "pack_small")
    (d_all,) = _all_gather([small], "gather_small_grads", True)
    d_cols = lax.dynamic_slice_in_dim(d_all, me * mod_cols, mod_cols, axis=2)
    g_w_mod, dsil, dsum = _adaln_bwd(a_rows.T, w_mod[0], d_cols[:, 0, :], d_cols[:, 1, :], d_all, "adaln_bwd")
    (dsil_all,) = _all_gather([dsil], "gather_d_cctx", True)
    loss = dsum[6, 0]
    g_cw = lax.dynamic_slice(dsum, (3, me * cw_cols), (3, cw_cols))

    send = _pack_small_grads(d_head, d_conv, d_wq, d_wkk, d_wkv)
    slots = dict(zip(["w_in", "w_uq", "w_ukv"], _all_to_all(send, "exchange_grads")))
    slots.update(zip(["w_out", "w_mlp1", "w_mlp2"], rode))

    grads = {"w_mod": g_w_mod[None]}
    weights = {"c_ctx": c_ctx, "w_mod": w_mod, "b_mod": b_mod, "w_in": w_in, "q_norm_g": q_norm_g, "w_uq": w_uq,
               "kv_norm_g": kv_norm_g, "w_ukv": w_ukv, "conv_w": conv_w, "w_out": w_out, "w_mlp1": w_mlp1,
               "w_mlp2": w_mlp2, "final_norm_g": final_norm_g}
    m_in = {"c_ctx": m_c_ctx, "w_mod": m_w_mod, "b_mod": m_b_mod, "w_in": m_w_in, "q_norm_g": m_q_norm_g,
            "w_uq": m_w_uq, "kv_norm_g": m_kv_norm_g, "w_ukv": m_w_ukv, "conv_w": m_conv_w, "w_out": m_w_out,
            "w_mlp1": m_w_mlp1, "w_mlp2": m_w_mlp2, "final_norm_g": m_final_norm_g}
    v_in = {"c_ctx": v_c_ctx, "w_mod": v_w_mod, "b_mod": v_b_mod, "w_in": v_w_in, "q_norm_g": v_q_norm_g,
            "w_uq": v_w_uq, "kv_norm_g": v_kv_norm_g, "w_ukv": v_w_ukv, "conv_w": v_conv_w, "w_out": v_w_out,
            "w_mlp1": v_w_mlp1, "w_mlp2": v_w_mlp2, "final_norm_g": v_final_norm_g}
    names = list(weights)
    small_names = ["c_ctx", "b_mod", "q_norm_g", "kv_norm_g", "final_norm_g", "conv_w"]
    delta, new_m, new_v = {}, {}, {}

    def two_dims(a):
        return a.reshape(-1, a.shape[-1])

    small_out = _small_update(dsum, dsil_all, g_cw, [[two_dims(src[n]) for src in (weights, m_in, v_in)]
                                                      for n in small_names], "small_update")
    for n, outs in zip(small_names, small_out):
        grads[n], delta[n], new_m[n], new_v[n] = [a.reshape(weights[n].shape) for a in outs]
    for n in names:
        if n in small_names:
            continue
        shp = weights[n].shape
        two_d = (shp[0] * shp[1], shp[2])
        wmv = [a.reshape(two_d) for a in (weights[n], m_in[n], v_in[n])]
        if n in slots:
            outs = _adamw(wmv[0], slots[n], wmv[1], wmv[2], "adamw_" + n, slots=True)
            grads[n] = outs[0].reshape(shp)
            outs = outs[1:]
        else:
            outs = _adamw(wmv[0], grads[n].reshape(two_d), wmv[1], wmv[2], "adamw_" + n)
        delta[n], new_m[n], new_v[n] = [a.reshape(shp) for a in outs]

    return (loss, grad_x[None], *[grads[n] for n in names], *[delta[n] for n in names],
            *[new_m[n] for n in names], *[new_v[n] for n in names])
```

```python
import math

import jax
import jax.numpy as jnp
import numpy as np
from jax import lax
from jax.experimental import pallas as pl
from jax.experimental.pallas import tpu as pltpu

F32 = jnp.float32
BF16 = jnp.bfloat16

D_MODEL = 1024
GRID_W = 64
N_HEADS = 8
QK_NOPE = 64
QK_ROPE = 32
V_DIM = 64
Q_RANK = 256
KV_RANK = 128
MLA_IN = Q_RANK + KV_RANK + QK_ROPE
CONV_W = 512
HEAD_COLS = 512
D_FF = 4096
ROPE_THETA = 10000.0
EPS = 1e-6
ATTN_SCALE = 1.0 / math.sqrt(QK_NOPE + QK_ROPE)
LOG2_E = 1.0 / math.log(2.0)
EXP2_SCALE = ATTN_SCALE * LOG2_E
N_DEV = 8
LANES = 128

ADAM_LR, ADAM_B1, ADAM_B2, ADAM_EPS, ADAM_WD, ADAM_STEP = 0.001, 0.9, 0.999, 1e-08, 0.01, 10

ROW_TILE = 256
VMEM_BIG = 60 * 1024 * 1024


def _params(sem=None, vmem=None):
    return pltpu.CompilerParams(dimension_semantics=sem, vmem_limit_bytes=vmem)


def _pick(n, prefs):
    for p in prefs:
        if n % p == 0:
            return p
    return n


def _my_index():
    return 4 * lax.axis_index("x") + 2 * lax.axis_index("y") + lax.axis_index("c")


def _all_gather(arrays, name, in_vmem):
    space = pltpu.VMEM if in_vmem else pl.ANY
    n = len(arrays)

    def body(*refs):
        x_refs, out_refs = refs[:n], refs[n:2 * n]
        send_sems, recv_sems, local_sems = refs[2 * n:]
        x, y, c = lax.axis_index("x"), lax.axis_index("y"), lax.axis_index("c")
        me, sibling = (x, y, c), (x, y, 1 - c)
        chips = [(1 - x, y), (x, 1 - y), (1 - x, 1 - y)]

        def slot(a, px, py, pc):
            return out_refs[a].at[4 * px + 2 * py + pc]

        def copy(a, k, block, to, src=None):
            return pltpu.make_async_remote_copy(
                src_ref=slot(a, *block) if src is None else src, dst_ref=slot(a, *block),
                send_sem=send_sems.at[7 * a + k], recv_sem=recv_sems.at[7 * a + k],
                device_id=to, device_id_type=pl.DeviceIdType.MESH)

        mine = [pltpu.make_async_copy(x_refs[a], slot(a, *me), local_sems.at[a]) for a in range(n)]
        for cp in mine:
            cp.start()
        started = []
        for a in range(n):
            first = [copy(a, 0, me, sibling, src=x_refs[a])]
            first += [copy(a, 1 + j, me, (*chip, c), src=x_refs[a]) for j, chip in enumerate(chips)]
            for cp in first:
                cp.start()
            started += first
        for a in range(n):
            for j, chip in enumerate(chips):
                copy(a, 1 + j, (*chip, c), me).wait_recv()
                passed = copy(a, 4 + j, (*chip, c), sibling)
                passed.start()
                started.append(passed)
        for a in range(n):
            copy(a, 0, sibling, me).wait_recv()
            for j, chip in enumerate(chips):
                copy(a, 4 + j, (*chip, 1 - c), me).wait_recv()
        for cp in started:
            cp.wait_send()
        for cp in mine:
            cp.wait()

    outs = pl.pallas_call(
        body, name=name,
        out_shape=tuple(jax.ShapeDtypeStruct((N_DEV,) + a.shape, a.dtype) for a in arrays),
        in_specs=[pl.BlockSpec(memory_space=space)] * n,
        out_specs=tuple(pl.BlockSpec(memory_space=space) for _ in arrays),
        scratch_shapes=[pltpu.SemaphoreType.DMA((7 * n,)), pltpu.SemaphoreType.DMA((7 * n,)),
                        pltpu.SemaphoreType.DMA((n,))],
    )(*arrays)
    return list(outs)


class _Riding:
    def __init__(self, kind, arrays):
        self.kind, self.arrays, self.n = kind, list(arrays), len(arrays)
        lead = (N_DEV,) if kind == "gather" else ()
        self.out_shape = [jax.ShapeDtypeStruct(lead + a.shape, a.dtype) for a in self.arrays]
        self.specs = [pl.BlockSpec(memory_space=pl.ANY)] * self.n
        self.scratch = [pltpu.SemaphoreType.DMA((7 * self.n,)), pltpu.SemaphoreType.DMA((7 * self.n,)),
                        pltpu.SemaphoreType.DMA((self.n,))]

    def copies(self, x_refs, y_refs, send_sems, recv_sems, local_sems):
        x, y, c = lax.axis_index("x"), lax.axis_index("y"), lax.axis_index("c")
        me = 4 * x + 2 * y + c
        local, sends, landings = [], [], []
        for a in range(self.n):
            src_mine = x_refs[a] if self.kind == "gather" else x_refs[a].at[me]
            local.append(pltpu.make_async_copy(src_mine, y_refs[a].at[me], local_sems.at[a]))
            for k in range(1, N_DEV):
                peer = (1 - x if k & 4 else x, 1 - y if k & 2 else y, 1 - c if k & 1 else c)
                pid = 4 * peer[0] + 2 * peer[1] + peer[2]
                src = x_refs[a] if self.kind == "gather" else x_refs[a].at[pid]
                for dst, out in ((me, sends), (pid, landings)):
                    out.append(pltpu.make_async_remote_copy(
                        src_ref=src, dst_ref=y_refs[a].at[dst],
                        send_sem=send_sems.at[7 * a + k - 1], recv_sem=recv_sems.at[7 * a + k - 1],
                        device_id=peer, device_id_type=pl.DeviceIdType.MESH))
        return local, sends, landings

    def run(self, first, last, x_refs, y_refs, sems):
        if self.n == 0:
            return None
        local, sends, landings = self.copies(x_refs, y_refs, *sems)

        @pl.when(first)
        def _():
            for cp in local + sends:
                cp.start()

        return local, sends, landings, last

    @staticmethod
    def finish(state):
        if state is None:
            return
        local, sends, landings, last = state

        @pl.when(last)
        def _():
            for cp in landings:
                cp.wait_recv()
            for cp in sends:
                cp.wait_send()
            for cp in local:
                cp.wait()


_DIMS = {"nn": (((1,), (0,)), ((), ())), "nt": (((1,), (1,)), ((), ())), "tn": (((0,), (0,)), ((), ()))}
NT_DIMS = _DIMS["nt"]
TN_DIMS = _DIMS["tn"]


def _swap8(x):
    lane = lax.broadcasted_iota(jnp.int32, x.shape, 1)
    return jnp.where((lane & 15) < 8, pltpu.roll(x, LANES - 8, 1), pltpu.roll(x, 8, 1))


def _rope(x, cos, sgn, bwd):
    return x * cos + (_swap8(x * sgn) if bwd else _swap8(x) * sgn)


def _matmul(a, b, *, mode, name, out_dtype=F32, tm=512, tn=512, tk=512, m=None, k=None,
            epilogue=None, extra=(), addend=None, slots=None):
    if mode == "nn":
        m = a.shape[0] if m is None else m
        k = a.shape[1]
        n = N_DEV * b.shape[2] if slots == "b_cols" else b.shape[1]
    elif mode == "nt":
        m = a.shape[0] if m is None else m
        k = a.shape[1]
        n = b.shape[1] if slots == "b_contract" else b.shape[0]
    else:
        k = a.shape[0] if k is None else k
        m, n = a.shape[1], b.shape[1]
    tm, tn, tk = min(tm, m), min(tn, n), min(tk, k)
    group = 1
    if slots == "b_cols":
        tn = b.shape[2]
    if slots == "b_contract":
        group = max(1, tk // b.shape[2])
        tk = group * b.shape[2]
    if slots == "out":
        tn = n // N_DEV
    assert m % tm == 0 and n % tn == 0 and k % tk == 0, (name, m, n, k, tm, tn, tk)
    nk = k // tk
    dims = _DIMS[mode]
    a_spec = (pl.BlockSpec((tk, tm), lambda i, j, kk: (kk, i)) if mode == "tn"
              else pl.BlockSpec((tm, tk), lambda i, j, kk: (i, kk)))
    if slots == "b_cols":
        b_spec = pl.BlockSpec((None, tk, tn), lambda i, j, kk: (j, kk, 0))
    elif slots == "b_contract":
        b_spec = pl.BlockSpec((group, tn, tk // group), lambda i, j, kk: (kk, j, 0))
    elif mode == "nt":
        b_spec = pl.BlockSpec((tn, tk), lambda i, j, kk: (j, kk))
    else:
        b_spec = pl.BlockSpec((tk, tn), lambda i, j, kk: (kk, j))
    tile = pl.BlockSpec((tm, tn), lambda i, j, kk: (i, j))
    if slots == "out":
        o_spec = pl.BlockSpec((None, tm, tn), lambda i, j, kk: (j, i, 0))
        o_shape = (N_DEV, m, tn)
    else:
        o_spec, o_shape = tile, (m, n)
    in_specs, args = [a_spec, b_spec], [a, b]
    if epilogue == "drelu2":
        in_specs.append(tile)
    elif epilogue == "rope":
        in_specs += [pl.BlockSpec((tm, LANES), lambda i, j, kk: (i, 0))] * 2
    args += list(extra)
    if addend is not None:
        in_specs.append(tile)
        args.append(addend)
    if epilogue == "relu2":
        out_shape = (jax.ShapeDtypeStruct(o_shape, BF16), jax.ShapeDtypeStruct(o_shape, BF16))
        out_specs = (o_spec, o_spec)
    else:
        out_shape = jax.ShapeDtypeStruct(o_shape, out_dtype)
        out_specs = o_spec
    n_in = len(args)
    n_out = 2 if epilogue == "relu2" else 1

    def body(*refs):
        a_ref, b_ref = refs[0], refs[1]
        outs = refs[n_in:n_in + n_out]
        if slots == "b_contract":
            c = tk // group
            part = lax.dot_general(a_ref[:, 0:c], b_ref[0], dims, preferred_element_type=F32)
            for u in range(1, group):
                part = part + lax.dot_general(a_ref[:, u * c:(u + 1) * c], b_ref[u], dims, preferred_element_type=F32)
        else:
            part = lax.dot_general(a_ref[...], b_ref[...], dims, preferred_element_type=F32)

        def finish(acc):
            if addend is not None:
                acc = acc + refs[n_in - 1][...]
            if epilogue == "relu2":
                outs[0][...] = acc.astype(BF16)
                r = jnp.maximum(acc, 0.0)
                outs[1][...] = (r * r).astype(BF16)
            elif epilogue == "drelu2":
                u = refs[2][...].astype(F32)
                outs[0][...] = (acc * (2.0 * jnp.maximum(u, 0.0))).astype(out_dtype)
            elif epilogue == "rope":
                cos, sgn = refs[2][...], refs[3][...]
                for h in range(tn // LANES):
                    sl = slice(h * LANES, (h + 1) * LANES)
                    outs[0][:, sl] = _rope(acc[:, sl], cos, sgn, False).astype(out_dtype)
            else:
                outs[0][...] = acc.astype(out_dtype)

        if nk == 1:
            finish(part)
        else:
            acc_ref = refs[n_in + n_out]
            kk = pl.program_id(2)

            @pl.when(kk == 0)
            def _():
                acc_ref[...] = part

            @pl.when(kk > 0)
            def _():
                acc_ref[...] += part

            @pl.when(kk == nk - 1)
            def _():
                finish(acc_ref[...])

    return pl.pallas_call(
        body, name=name, grid=(m // tm, n // tn, nk),
        out_shape=out_shape, in_specs=in_specs, out_specs=out_specs,
        scratch_shapes=[pltpu.VMEM((tm, tn), F32)] if nk > 1 else [],
        compiler_params=_params(("parallel", "parallel", "arbitrary"), VMEM_BIG),
    )(*args)


def _rstd(x):
    return lax.rsqrt(jnp.mean(x * x, axis=1, keepdims=True) + EPS)


def _norm_bwd(dxn, xn, r):
    return r * (dxn - xn * jnp.mean(dxn * xn, axis=1, keepdims=True))


def _vec(col):
    return pl.BlockSpec((1, D_MODEL), lambda i: (0, col))


def _matmul_rows(a, b, epi, *, mode, name, tm, tk, rows=(), vecs=(), out_dtypes=(), sums=False, slots=None,
                 riding=None):
    m, k = a.shape
    n = D_MODEL
    riding = riding or _Riding("gather", [])
    group = 1
    if slots == "b_contract":
        group = max(1, tk // b.shape[2])
        tk = group * b.shape[2]
        b_spec = pl.BlockSpec((group, n, tk // group), lambda i, kk: (kk, 0, 0))
    elif mode == "nt":
        b_spec = pl.BlockSpec((n, tk), lambda i, kk: (0, kk))
    else:
        b_spec = pl.BlockSpec((tk, n), lambda i, kk: (kk, 0))
    assert m % tm == 0 and k % tk == 0, (name, m, k, tm, tk)
    ni, nk = m // tm, k // tk
    dims = _DIMS[mode]
    tile = pl.BlockSpec((tm, n), lambda i, kk: (i, 0))
    in_specs = [pl.BlockSpec((tm, tk), lambda i, kk: (i, kk)), b_spec] + [tile] * len(rows)
    in_specs += [pl.BlockSpec((1, n), lambda i, kk, col=col: (0, col)) for _, col in vecs]
    args = [a, b, *rows, *[v for v, _ in vecs]]
    out_shape = [jax.ShapeDtypeStruct((m, n), dt) for dt in out_dtypes]
    out_specs = [tile] * len(out_dtypes)
    if sums:
        out_shape.append(jax.ShapeDtypeStruct((8, n), F32))
        out_specs.append(pl.BlockSpec((8, n), lambda i, kk: (0, 0)))
    n_rows, n_vecs, n_outs, nr = len(rows), len(vecs), len(out_dtypes), riding.n
    n_in = 2 + n_rows + n_vecs

    def body(*refs):
        a_ref, b_ref = refs[0], refs[1]
        row_refs = refs[2:2 + n_rows]
        vec_refs = refs[2 + n_rows:n_in]
        x_refs = refs[n_in:n_in + nr]
        out_refs = refs[n_in + nr:n_in + nr + n_outs]
        pos = n_in + nr + n_outs
        sums_ref = refs[pos] if sums else None
        pos += 1 if sums else 0
        y_refs = refs[pos:pos + nr]
        pos += nr
        acc_ref = refs[pos] if nk > 1 else None
        sem_refs = refs[pos + (1 if nk > 1 else 0):]
        i, kk = pl.program_id(0), pl.program_id(1)
        state = riding.run((i == 0) & (kk == 0), (i == ni - 1) & (kk == nk - 1), x_refs, y_refs, sem_refs)
        if slots == "b_contract":
            c = tk // group
            part = lax.dot_general(a_ref[:, 0:c], b_ref[0], dims, preferred_element_type=F32)
            for u in range(1, group):
                part = part + lax.dot_general(a_ref[:, u * c:(u + 1) * c], b_ref[u], dims, preferred_element_type=F32)
        else:
            part = lax.dot_general(a_ref[...], b_ref[...], dims, preferred_element_type=F32)

        def finish(acc):
            epi(acc, row_refs, vec_refs, out_refs, sums_ref, i, ni)

        if nk == 1:
            finish(part)
        else:
            @pl.when(kk == 0)
            def _():
                acc_ref[...] = part

            @pl.when(kk > 0)
            def _():
                acc_ref[...] += part

            @pl.when(kk == nk - 1)
            def _():
                finish(acc_ref[...])

        riding.finish(state)

    outs = pl.pallas_call(
        body, name=name, grid=(ni, nk),
        out_shape=(*out_shape, *riding.out_shape),
        in_specs=[*in_specs, *riding.specs], out_specs=(*out_specs, *riding.specs),
        scratch_shapes=([pltpu.VMEM((tm, n), F32)] if nk > 1 else []) + (riding.scratch if nr else []),
        compiler_params=_params(("arbitrary", "arbitrary"), VMEM_BIG),
    )(*args, *riding.arrays)
    n_own = len(out_shape)
    return list(outs[:n_own]), list(outs[n_own:])


def _zero_sums_at_start(sums_ref, i):
    @pl.when(i == 0)
    def _():
        sums_ref[...] = jnp.zeros_like(sums_ref)


def _epi_resid_modulate(acc, rows, vecs, outs, sums_ref, i, ni):
    (x_ref,), (g_ref, sh_ref, sc_ref) = rows, vecs
    x1 = x_ref[...] + g_ref[...] * acc
    outs[0][...] = acc
    outs[1][...] = x1
    outs[2][...] = (x1 * _rstd(x1) * (1.0 + sc_ref[...]) + sh_ref[...]).astype(BF16)


def _epi_final(acc, rows, vecs, outs, sums_ref, i, ni):
    (x1_ref, t_ref), (g_ref, gf_ref) = rows, vecs
    d = acc.shape[1]
    x2 = x1_ref[...] + g_ref[...] * acc
    r = _rstd(x2)
    xn = x2 * r
    err = xn * gf_ref[...] - t_ref[...]
    dy = err * (1.0 / d)
    dx2 = _norm_bwd(dy * gf_ref[...], xn, r)
    outs[0][...] = dx2
    outs[1][...] = (dx2 * g_ref[...]).astype(BF16)
    _zero_sums_at_start(sums_ref, i)
    sums_ref[0:1, :] += jnp.sum(dy * xn, axis=0, keepdims=True)
    sums_ref[1:2, :] += jnp.sum(dx2 * acc, axis=0, keepdims=True)
    sums_ref[2:3, :] += jnp.sum(err * err, axis=0, keepdims=True)

    @pl.when(i == ni - 1)
    def _():
        tot = jnp.sum(sums_ref[2:3, :], axis=1, keepdims=True) * (0.5 / d)
        sums_ref[3:4, :] = jnp.broadcast_to(tot, (1, d))


def _epi_modulate2_bwd(acc, rows, vecs, outs, sums_ref, i, ni):
    (x_ref, dres_ref, o_ref), (sc_ref, g_ref) = rows, vecs
    x = x_ref[...]
    r = _rstd(x)
    xn = x * r
    dx = dres_ref[...] + _norm_bwd(acc * (1.0 + sc_ref[...]), xn, r)
    outs[0][...] = dx
    outs[1][...] = (dx * g_ref[...]).astype(BF16)
    _zero_sums_at_start(sums_ref, i)
    sums_ref[0:1, :] += jnp.sum(acc * xn, axis=0, keepdims=True)
    sums_ref[1:2, :] += jnp.sum(acc, axis=0, keepdims=True)
    sums_ref[2:3, :] += jnp.sum(dx * o_ref[...], axis=0, keepdims=True)


def _epi_modulate1_bwd(acc, rows, vecs, outs, sums_ref, i, ni):
    (add_ref, x_ref, dres_ref), (sc_ref,) = rows, vecs
    dh = acc + add_ref[...]
    x = x_ref[...]
    r = _rstd(x)
    xn = x * r
    outs[0][...] = dres_ref[...] + _norm_bwd(dh * (1.0 + sc_ref[...]), xn, r)
    _zero_sums_at_start(sums_ref, i)
    sums_ref[0:1, :] += jnp.sum(dh * xn, axis=0, keepdims=True)
    sums_ref[1:2, :] += jnp.sum(dh, axis=0, keepdims=True)


def _modulate_all(x, ctx, mod, mod_ctx, name):
    s, d = x.shape
    t = s + ctx.shape[0]
    ns = s // ROW_TILE
    nc = ctx.shape[0] // ROW_TILE

    def body(x_ref, c_ref, sh_ref, sc_ref, shc_ref, scc_ref, h_ref):
        i = pl.program_id(0)

        @pl.when(i < ns)
        def _():
            v = x_ref[...]
            h_ref[...] = (v * _rstd(v) * (1.0 + sc_ref[...]) + sh_ref[...]).astype(BF16)

        @pl.when(i >= ns)
        def _():
            v = c_ref[...]
            h_ref[...] = (v * _rstd(v) * (1.0 + scc_ref[...]) + shc_ref[...]).astype(BF16)

    return pl.pallas_call(
        body, name=name, grid=(ns + nc,),
        out_shape=jax.ShapeDtypeStruct((t, d), BF16),
        in_specs=[pl.BlockSpec((ROW_TILE, d), lambda i: (jnp.minimum(i, ns - 1), 0)),
                  pl.BlockSpec((ROW_TILE, d), lambda i: (jnp.maximum(i - ns, 0), 0)),
                  _vec(0), _vec(1), _vec(0), _vec(1)],
        out_specs=pl.BlockSpec((ROW_TILE, d), lambda i: (i, 0)),
        compiler_params=_params(("arbitrary",)),
    )(x, ctx, mod, mod, mod_ctx, mod_ctx)


def _modulate_bwd(dh, row_off, xsrc, mod, scale_col, name, dres=None, o=None):
    s, d = xsrc.shape
    n = s // ROW_TILE
    has_dx, has_o = dres is not None, o is not None
    assert has_dx or not has_o

    def body(*refs):
        it = iter(refs)
        dh_ref, x_ref, sc_ref = next(it), next(it), next(it)
        dres_ref = next(it) if has_dx else None
        o_ref, g_ref = (next(it), next(it)) if has_o else (None, None)
        dx_ref = next(it) if has_dx else None
        do_ref = next(it) if has_o else None
        sums_ref = next(it)
        i = pl.program_id(0)
        x = x_ref[...]
        r = _rstd(x)
        xn = x * r
        dhv = dh_ref[...]

        @pl.when(i == 0)
        def _():
            sums_ref[...] = jnp.zeros_like(sums_ref)

        sums_ref[0:1, :] += jnp.sum(dhv * xn, axis=0, keepdims=True)
        sums_ref[1:2, :] += jnp.sum(dhv, axis=0, keepdims=True)
        if has_dx:
            dx = dres_ref[...] + _norm_bwd(dhv * (1.0 + sc_ref[...]), xn, r)
            dx_ref[...] = dx
            if has_o:
                do_ref[...] = (dx * g_ref[...]).astype(BF16)
                sums_ref[2:3, :] += jnp.sum(dx * o_ref[...], axis=0, keepdims=True)

    row = pl.BlockSpec((ROW_TILE, d), lambda i: (i, 0))
    in_specs = [pl.BlockSpec((ROW_TILE, d), lambda i: (i + row_off, 0)), row, _vec(scale_col)]
    args = [dh, xsrc, mod]
    out_shape, out_specs = [], []
    if has_dx:
        in_specs.append(row)
        args.append(dres)
        out_shape.append(jax.ShapeDtypeStruct((s, d), F32))
        out_specs.append(row)
    if has_o:
        in_specs += [row, _vec(2)]
        args += [o, mod]
        out_shape.append(jax.ShapeDtypeStruct((s, d), BF16))
        out_specs.append(row)
    out_shape.append(jax.ShapeDtypeStruct((8, d), F32))
    out_specs.append(pl.BlockSpec((8, d), lambda i: (0, 0)))
    return pl.pallas_call(
        body, name=name, grid=(n,),
        out_shape=tuple(out_shape), in_specs=in_specs, out_specs=tuple(out_specs),
        compiler_params=_params(("arbitrary",)),
    )(*args)


def _qkv_prep(z, q_gain, kv_gain, cos, sgn, name):
    t = z.shape[0]

    def body(z_ref, qg_ref, kg_ref, c_ref, s_ref, cq_ref, kv_ref):
        zq = z_ref[:, 0:Q_RANK]
        cq_ref[...] = (zq * _rstd(zq) * qg_ref[...]).astype(BF16)
        zk = z_ref[:, Q_RANK:Q_RANK + KV_RANK]
        kv_ref[:, 0:KV_RANK] = (zk * _rstd(zk) * kg_ref[...]).astype(BF16)
        kr = z_ref[:, Q_RANK + KV_RANK:HEAD_COLS]
        kv_ref[:, KV_RANK:KV_RANK + LANES] = _rope(kr, c_ref[...], s_ref[...], False).astype(BF16)

    tab = pl.BlockSpec((ROW_TILE, LANES), lambda i: (i, 0))
    return pl.pallas_call(
        body, name=name, grid=(t // ROW_TILE,),
        out_shape=(jax.ShapeDtypeStruct((t, Q_RANK), BF16), jax.ShapeDtypeStruct((t, KV_RANK + LANES), BF16)),
        in_specs=[pl.BlockSpec((ROW_TILE, HEAD_COLS), lambda i: (i, 0)),
                  pl.BlockSpec((1, Q_RANK), lambda i: (0, 0)), pl.BlockSpec((1, KV_RANK), lambda i: (0, 0)), tab, tab],
        out_specs=(pl.BlockSpec((ROW_TILE, Q_RANK), lambda i: (i, 0)),
                   pl.BlockSpec((ROW_TILE, KV_RANK + LANES), lambda i: (i, 0))),
        compiler_params=_params(("parallel",)),
    )(z, q_gain, kv_gain, cos, sgn)


def _qkv_prep_bwd(z, dcq, dkv, q_gain, kv_gain, cos, sgn, s, name):
    t = z.shape[0]
    ns = s // ROW_TILE

    def body(z_ref, dcq_ref, dkv_ref, qg_ref, kg_ref, c_ref, s_ref, dz_ref, sums_ref):
        i = pl.program_id(0)

        @pl.when(i == 0)
        def _():
            sums_ref[...] = jnp.zeros_like(sums_ref)

        @pl.when(i < ns)
        def _():
            zq = z_ref[:, 0:Q_RANK]
            r = _rstd(zq)
            zn = zq * r
            dc = dcq_ref[...]
            sums_ref[0:1, :] += jnp.sum(dc * zn, axis=0, keepdims=True)
            dz_ref[:, 0:Q_RANK] = _norm_bwd(dc * qg_ref[...], zn, r).astype(BF16)

        @pl.when(i >= ns)
        def _():
            dz_ref[:, 0:Q_RANK] = jnp.zeros((ROW_TILE, Q_RANK), BF16)

        zk = z_ref[:, Q_RANK:Q_RANK + KV_RANK]
        r = _rstd(zk)
        zn = zk * r
        dc = dkv_ref[:, 0:KV_RANK]
        sums_ref[1:2, 0:KV_RANK] += jnp.sum(dc * zn, axis=0, keepdims=True)
        dz_ref[:, Q_RANK:Q_RANK + KV_RANK] = _norm_bwd(dc * kg_ref[...], zn, r).astype(BF16)
        dkr = dkv_ref[:, KV_RANK:KV_RANK + LANES]
        dz_ref[:, Q_RANK + KV_RANK:HEAD_COLS] = _rope(dkr, c_ref[...], s_ref[...], True).astype(BF16)

    tab = pl.BlockSpec((ROW_TILE, LANES), lambda i: (i, 0))
    return pl.pallas_call(
        body, name=name, grid=(t // ROW_TILE,),
        out_shape=(jax.ShapeDtypeStruct((t, HEAD_COLS), BF16), jax.ShapeDtypeStruct((8, Q_RANK), F32)),
        in_specs=[pl.BlockSpec((ROW_TILE, HEAD_COLS), lambda i: (i, 0)),
                  pl.BlockSpec((ROW_TILE, Q_RANK), lambda i: (jnp.minimum(i, ns - 1), 0)),
                  pl.BlockSpec((ROW_TILE, KV_RANK + LANES), lambda i: (i, 0)),
                  pl.BlockSpec((1, Q_RANK), lambda i: (0, 0)), pl.BlockSpec((1, KV_RANK), lambda i: (0, 0)), tab, tab],
        out_specs=(pl.BlockSpec((ROW_TILE, HEAD_COLS), lambda i: (i, 0)), pl.BlockSpec((8, Q_RANK), lambda i: (0, 0))),
        compiler_params=_params(("arbitrary",)),
    )(z, dcq, dkv, q_gain, kv_gain, cos, sgn)


def _shift_rows(u, s):
    rowi = lax.broadcasted_iota(jnp.int32, u.shape, 0)
    prev = jnp.where(rowi == 0, 0.0, pltpu.roll(u, 1, 0))
    nxt = jnp.where(rowi == s - 1, 0.0, pltpu.roll(u, s - 1, 0))
    return prev, nxt


def _conv_fwd(z_conv, cw, a_cat, name):
    s = z_conv.shape[0]

    def body(z_ref, w_ref, a_in_ref, o_ref):
        del a_in_ref
        gb, gc, xv = z_ref[:, 0:LANES], z_ref[:, LANES:2 * LANES], z_ref[:, 2 * LANES:3 * LANES]
        u = gc * xv
        prev, nxt = _shift_rows(u, s)
        y = w_ref[0:1, :] * prev + w_ref[1:2, :] * u + w_ref[2:3, :] * nxt
        o_ref[...] = (gb * y).astype(BF16)

    return pl.pallas_call(
        body, name=name, grid=(CONV_W // LANES,),
        out_shape=jax.ShapeDtypeStruct(a_cat.shape, a_cat.dtype),
        in_specs=[pl.BlockSpec((s, 3 * LANES), lambda j: (0, j)), pl.BlockSpec((3, LANES), lambda j: (0, j)),
                  pl.BlockSpec(memory_space=pl.ANY)],
        out_specs=pl.BlockSpec((s, LANES), lambda j: (0, 4 + j)),
        input_output_aliases={2: 0},
        compiler_params=_params(("parallel",), VMEM_BIG),
    )(z_conv, cw, a_cat)


def _conv_bwd(z_conv, cw, da, name):
    s = z_conv.shape[0]

    def body(z_ref, w_ref, da_ref, dz_ref, dw_ref):
        gb, gc, xv = z_ref[:, 0:LANES], z_ref[:, LANES:2 * LANES], z_ref[:, 2 * LANES:3 * LANES]
        u = gc * xv
        prev, nxt = _shift_rows(u, s)
        dcv = da_ref[...]
        dz_ref[:, 0:LANES] = (dcv * (w_ref[0:1, :] * prev + w_ref[1:2, :] * u + w_ref[2:3, :] * nxt)).astype(BF16)
        dy = dcv * gb
        dw_ref[0:1, :] = jnp.sum(dy * prev, axis=0, keepdims=True)
        dw_ref[1:2, :] = jnp.sum(dy * u, axis=0, keepdims=True)
        dw_ref[2:3, :] = jnp.sum(dy * nxt, axis=0, keepdims=True)
        dyp, dyn = _shift_rows(dy, s)
        du = w_ref[0:1, :] * dyn + w_ref[1:2, :] * dy + w_ref[2:3, :] * dyp
        dz_ref[:, LANES:2 * LANES] = (du * xv).astype(BF16)
        dz_ref[:, 2 * LANES:3 * LANES] = (du * gc).astype(BF16)

    blk = pl.BlockSpec((s, 3 * LANES), lambda j: (0, j))
    cws = pl.BlockSpec((3, LANES), lambda j: (0, j))
    return pl.pallas_call(
        body, name=name, grid=(CONV_W // LANES,),
        out_shape=(jax.ShapeDtypeStruct(z_conv.shape, BF16), jax.ShapeDtypeStruct((3, CONV_W), F32)),
        in_specs=[blk, cws, pl.BlockSpec((s, LANES), lambda j: (0, 4 + j))], out_specs=(blk, cws),
        compiler_params=_params(("parallel",), VMEM_BIG),
    )(z_conv, cw, da)


ATT_TQ = 256
ATT_TQ_BWD = 512


def _head_mask(shape, hh):
    lane = lax.broadcasted_iota(jnp.int32, shape, 1)
    return (lane >= hh * V_DIM) & (lane < (hh + 1) * V_DIM)


def _attn_fwd(qf, kv, riding, name):
    s, t = qf.shape[0], kv.shape[0]
    nq = s // ATT_TQ
    nr = riding.n

    def body(*refs):
        q_ref, k_ref, v_ref = refs[:3]
        o_ref, ob_ref, st_ref = refs[3 + nr:6 + nr]
        p, i = pl.program_id(0), pl.program_id(1)
        state = riding.run((p == 0) & (i == 0), (p == N_HEADS // 2 - 1) & (i == nq - 1),
                           refs[3:3 + nr], refs[6 + nr:6 + 2 * nr], refs[6 + 2 * nr:])
        v = v_ref[...]
        vlane = lax.broadcasted_iota(jnp.int32, v.shape, 1)
        olane = lax.broadcasted_iota(jnp.int32, (ATT_TQ, LANES), 1)
        acc = jnp.zeros((ATT_TQ, LANES), F32)
        stat = jnp.zeros((ATT_TQ, LANES), F32)
        for hh in range(2):
            sl = slice(hh * LANES, (hh + 1) * LANES)
            sc = lax.dot_general(q_ref[:, sl], k_ref[:, sl], NT_DIMS, preferred_element_type=F32)
            mx = jnp.max(sc, axis=1, keepdims=True)
            e = jnp.exp2((sc - mx) * EXP2_SCALE).astype(BF16)
            one_lane = (1 - hh) * V_DIM
            vm = jnp.where(_head_mask(v.shape, hh), v, jnp.where(vlane == one_lane, 1.0, 0.0).astype(BF16))
            r = jnp.dot(e, vm, preferred_element_type=F32)
            den = jnp.sum(jnp.where(olane == one_lane, r, 0.0), axis=1, keepdims=True)
            acc = acc + jnp.where(_head_mask(r.shape, hh), r * (1.0 / den), 0.0)
            stat = stat + jnp.where(olane == hh, mx * EXP2_SCALE + jnp.log(den) * LOG2_E, 0.0)
        o_ref[...] = acc
        ob_ref[...] = acc.astype(BF16)
        st_ref[...] = stat.T[0:8, :]
        riding.finish(state)

    o_spec = pl.BlockSpec((ATT_TQ, LANES), lambda p, i: (i, p))
    outs = pl.pallas_call(
        body, name=name, grid=(N_HEADS // 2, nq),
        out_shape=(jax.ShapeDtypeStruct((s, N_HEADS * V_DIM), F32),
                   jax.ShapeDtypeStruct((s, D_MODEL), BF16),
                   jax.ShapeDtypeStruct((N_HEADS // 2 * 8, s), F32), *riding.out_shape),
        in_specs=[pl.BlockSpec((ATT_TQ, 2 * LANES), lambda p, i: (i, p)),
                  pl.BlockSpec((t, 2 * LANES), lambda p, i: (0, p)),
                  pl.BlockSpec((t, LANES), lambda p, i: (0, N_HEADS + p)), *riding.specs],
        out_specs=(o_spec, o_spec, pl.BlockSpec((8, ATT_TQ), lambda p, i: (p, i)), *riding.specs),
        scratch_shapes=riding.scratch,
        compiler_params=_params(("arbitrary", "arbitrary"), VMEM_BIG),
    )(qf, kv, kv, *riding.arrays)
    return outs[0], outs[1], outs[2], list(outs[3:])


def _attn_bwd(qf, kv, o, da, stats, cos, sgn, riding, name):
    s, t = qf.shape[0], kv.shape[0]
    ATT_TQ = ATT_TQ_BWD
    nq = s // ATT_TQ
    nr = riding.n

    def body(*refs):
        q_ref, k_ref, v_ref, o_ref, do_ref, st_ref, c_ref, s_ref = refs[:8]
        dq_ref, dk_ref, dv_ref = refs[8 + nr:11 + nr]
        dk_acc, dv_acc = refs[11 + 2 * nr:13 + 2 * nr]
        p, i = pl.program_id(0), pl.program_id(1)
        state = riding.run((p == 0) & (i == 0), (p == N_HEADS // 2 - 1) & (i == nq - 1),
                           refs[8:8 + nr], refs[11 + nr:11 + 2 * nr], refs[13 + 2 * nr:])

        @pl.when(i == 0)
        def _():
            dk_acc[...] = jnp.zeros_like(dk_acc)
            dv_acc[...] = jnp.zeros_like(dv_acc)

        v = v_ref[...]
        do = do_ref[...]
        od = do * o_ref[...]
        ones = jnp.ones((8, LANES), F32)
        for hh in range(2):
            sl = slice(hh * LANES, (hh + 1) * LANES)
            q, k = q_ref[:, sl], k_ref[:, sl]
            mask = _head_mask(do.shape, hh)
            dom = jnp.where(mask, do, 0.0).astype(BF16)
            delta = lax.dot_general(ones, jnp.where(mask, od, 0.0), NT_DIMS, preferred_element_type=F32,
                                    precision=lax.Precision.HIGHEST)[0:1, :]
            st = lax.dot_general(k, q, NT_DIMS, preferred_element_type=F32)
            pt = jnp.exp2(st * EXP2_SCALE - st_ref[hh:hh + 1, :]).astype(BF16)
            dpt = lax.dot_general(v, dom, NT_DIMS, preferred_element_type=F32)
            dst = (pt.astype(F32) * (dpt - delta)).astype(BF16)
            dv_acc[...] += jnp.dot(pt, dom, preferred_element_type=F32)
            dk_acc[:, sl] += jnp.dot(dst, q, preferred_element_type=F32)
            dq = lax.dot_general(dst, k, TN_DIMS, preferred_element_type=F32) * ATTN_SCALE
            dq_ref[:, sl] = _rope(dq, c_ref[...], s_ref[...], True).astype(BF16)

        @pl.when(i == nq - 1)
        def _():
            dk_ref[...] = (dk_acc[...] * ATTN_SCALE).astype(BF16)
            dv_ref[...] = dv_acc[...].astype(BF16)

        riding.finish(state)

    o_spec = pl.BlockSpec((ATT_TQ, LANES), lambda p, i: (i, p))
    tab = pl.BlockSpec((ATT_TQ, LANES), lambda p, i: (i, 0))
    outs = pl.pallas_call(
        body, name=name, grid=(N_HEADS // 2, nq),
        out_shape=(jax.ShapeDtypeStruct((s, N_HEADS * LANES), BF16),
                   jax.ShapeDtypeStruct((t, N_HEADS * LANES), BF16),
                   jax.ShapeDtypeStruct((t, N_HEADS * V_DIM), BF16), *riding.out_shape),
        in_specs=[pl.BlockSpec((ATT_TQ, 2 * LANES), lambda p, i: (i, p)),
                  pl.BlockSpec((t, 2 * LANES), lambda p, i: (0, p)),
                  pl.BlockSpec((t, LANES), lambda p, i: (0, N_HEADS + p)),
                  o_spec, o_spec,
                  pl.BlockSpec((8, ATT_TQ), lambda p, i: (p, i)), tab, tab, *riding.specs],
        out_specs=(pl.BlockSpec((ATT_TQ, 2 * LANES), lambda p, i: (i, p)),
                   pl.BlockSpec((t, 2 * LANES), lambda p, i: (0, p)),
                   pl.BlockSpec((t, LANES), lambda p, i: (0, p)), *riding.specs),
        scratch_shapes=[pltpu.VMEM((t, 2 * LANES), F32), pltpu.VMEM((t, LANES), F32), *riding.scratch],
        compiler_params=_params(("arbitrary", "arbitrary"), VMEM_BIG),
    )(qf, kv, kv, o, da, stats, cos, sgn, *riding.arrays)
    return outs[0], outs[1], outs[2], list(outs[3:])


def _silu(x):
    return x * (1.0 / (1.0 + jnp.exp(-x)))


def _adaln_fwd(a, w, b, name):
    def body(a_ref, w_ref, b_ref, o_ref):
        o_ref[...] = jnp.dot(_silu(a_ref[...]), w_ref[...], preferred_element_type=F32,
                             precision=lax.Precision.HIGHEST) + b_ref[...]

    return pl.pallas_call(
        body, name=name, out_shape=jax.ShapeDtypeStruct((a.shape[0], w.shape[1]), F32),
        compiler_params=_params(None, VMEM_BIG),
    )(a, w, b)


def _adaln_bwd(a_t, w, d_ex, d_ctx, d_all, name):
    def body(at_ref, w_ref, dex_ref, dctx_ref, dall_ref, gw_ref, dsil_ref, dsum_ref):
        sil_t = _silu(at_ref[...])
        dctx = dctx_ref[...]
        row = dctx[0:1, :]
        for j in range(1, N_DEV):
            row = row + dctx[j:j + 1, :]
        rowi = lax.broadcasted_iota(jnp.int32, dctx.shape, 0)
        ctx_rows = jnp.where(rowi == 0, jnp.broadcast_to(row, dctx.shape), 0.0)
        hi = lax.Precision.HIGHEST
        d_rows = jnp.concatenate([dex_ref[...], ctx_rows], axis=0)
        gw_ref[...] = jnp.dot(sil_t, d_rows, preferred_element_type=F32, precision=hi)
        dsil_ref[...] = lax.dot_general(ctx_rows, w_ref[...], NT_DIMS, preferred_element_type=F32, precision=hi)
        tot = dall_ref[0]
        for j in range(1, N_DEV):
            tot = tot + dall_ref[j]
        dsum_ref[...] = tot

    return pl.pallas_call(
        body, name=name,
        out_shape=(jax.ShapeDtypeStruct(w.shape, F32), jax.ShapeDtypeStruct((8, w.shape[0]), F32),
                   jax.ShapeDtypeStruct(d_all.shape[1:], F32)),
        compiler_params=_params(None, VMEM_BIG),
    )(a_t, w, d_ex, d_ctx, d_all)


def _pack_small(sums1, sums2, fsums, sums1c, psums, d_cw, name):
    d = D_MODEL

    def body(s1_ref, s2_ref, f_ref, s1c_ref, p_ref, cw_ref, o_ref):
        o_ref[...] = jnp.zeros_like(o_ref)
        for col, (ref, r) in enumerate([(s1_ref, 1), (s1_ref, 0), (s2_ref, 2), (s2_ref, 1), (s2_ref, 0), (f_ref, 1)]):
            o_ref[0:1, col * d:(col + 1) * d] = ref[r:r + 1, :]
        o_ref[1:2, 0:d] = s1c_ref[1:2, :]
        o_ref[1:2, d:2 * d] = s1c_ref[0:1, :]
        o_ref[2:3, 0:Q_RANK] = p_ref[0:1, :]
        o_ref[2:3, Q_RANK:Q_RANK + KV_RANK] = p_ref[1:2, 0:KV_RANK]
        o_ref[2:3, Q_RANK + KV_RANK:Q_RANK + KV_RANK + d] = f_ref[0:1, :]
        for r in range(3):
            o_ref[3 + r:4 + r, 0:CONV_W] = cw_ref[r:r + 1, :]
        o_ref[6:7, 0:d] = f_ref[3:4, :]

    return pl.pallas_call(body, name=name, out_shape=jax.ShapeDtypeStruct((8, 6 * d), F32))(
        sums1, sums2, fsums, sums1c, psums, d_cw)


def _adam_math(w, g, m, v):
    nm = ADAM_B1 * m + (1.0 - ADAM_B1) * g
    nv = ADAM_B2 * v + (1.0 - ADAM_B2) * (g * g)
    m_hat = nm / (1.0 - ADAM_B1 ** ADAM_STEP)
    v_hat = nv / (1.0 - ADAM_B2 ** ADAM_STEP)
    return -ADAM_LR * (m_hat / (jnp.sqrt(v_hat) + ADAM_EPS) + ADAM_WD * w), nm, nv


def _small_update(dsum, dsil_all, g_cw, params, name):
    d = D_MODEL
    n = len(params)

    def body(*refs):
        dsum_ref, dsil_ref, gcw_ref = refs[:3]
        wmv = refs[3:3 + 3 * n]
        outs = refs[3 + 3 * n:]
        tot = dsil_ref[0]
        for j in range(1, N_DEV):
            tot = tot + dsil_ref[j]
        cv = wmv[0][...]
        sg = 1.0 / (1.0 + jnp.exp(-cv))
        off = Q_RANK + KV_RANK
        grads = [tot[0:1, :] * (sg * (1.0 + cv * (1.0 - sg))),
                 dsum_ref[0:1, :] + dsum_ref[1:2, :],
                 dsum_ref[2:3, 0:Q_RANK], dsum_ref[2:3, Q_RANK:off], dsum_ref[2:3, off:off + d],
                 gcw_ref[...]]
        for p, g in enumerate(grads):
            w_ref, m_ref, v_ref = wmv[3 * p:3 * p + 3]
            delta, nm, nv = _adam_math(w_ref[...], g, m_ref[...], v_ref[...])
            outs[4 * p][...] = g
            outs[4 * p + 1][...] = delta
            outs[4 * p + 2][...] = nm
            outs[4 * p + 3][...] = nv

    flat = [a for wmv in params for a in wmv]
    out_shape = tuple(jax.ShapeDtypeStruct(wmv[0].shape, F32) for wmv in params for _ in range(4))
    outs = pl.pallas_call(body, name=name, out_shape=out_shape)(dsum, dsil_all, g_cw, *flat)
    return [outs[4 * p:4 * p + 4] for p in range(n)]


def _adamw(w, g, m, v, name, slots=False):
    rows, cols = w.shape
    tr = _pick(rows, (256, 128, 64, 32, 16, 8))

    def body(w_ref, g_ref, m_ref, v_ref, *outs):
        if slots:
            gv = g_ref[0].astype(F32)
            for j in range(1, N_DEV):
                gv = gv + g_ref[j].astype(F32)
            outs[0][...] = gv
        else:
            gv = g_ref[...]
        d_ref, nm_ref, nv_ref = outs[-3:]
        d_ref[...], nm_ref[...], nv_ref[...] = _adam_math(w_ref[...], gv, m_ref[...], v_ref[...])

    blk = pl.BlockSpec((tr, cols), lambda i: (i, 0))
    g_spec = pl.BlockSpec((N_DEV, tr, cols), lambda i: (0, i, 0)) if slots else blk
    sh = jax.ShapeDtypeStruct((rows, cols), F32)
    n_out = 4 if slots else 3
    return pl.pallas_call(
        body, name=name, grid=(rows // tr,), out_shape=(sh,) * n_out,
        in_specs=[blk, g_spec, blk, blk], out_specs=(blk,) * n_out,
        compiler_params=_params(("parallel",)),
    )(w, g, m, v)


def _rope_tables(s, l):
    tok = np.arange(s)
    row = (tok // GRID_W).astype(np.float32)
    col = (tok % GRID_W).astype(np.float32)
    half = QK_ROPE // 2
    freqs = np.float32(ROPE_THETA) ** (-np.arange(0, half, 2, dtype=np.float32) / np.float32(half))
    dd = np.arange(QK_ROPE)
    pos = np.where((dd // half)[None, :] == 0, row[:, None], col[:, None]).astype(np.float32)
    ang = (pos * freqs[dd % (half // 2)][None, :]).astype(np.float32)
    sin = np.sin(ang).astype(np.float32)
    cos_t = np.ones((s + l, LANES), np.float32)
    sgn_t = np.zeros((s + l, LANES), np.float32)
    cos_t[:s, QK_NOPE:QK_NOPE + QK_ROPE] = np.cos(ang)
    sgn_t[:s, QK_NOPE:QK_NOPE + QK_ROPE] = np.where(((dd % half) // (half // 2))[None, :] == 0, -sin, sin)
    return jnp.asarray(cos_t), jnp.asarray(sgn_t)


def _slots_to_cols(g):
    return g.transpose(1, 0, 2).reshape(g.shape[1], N_DEV * g.shape[2])


def _cols_to_slots(w):
    return w.reshape(w.shape[0], N_DEV, w.shape[1] // N_DEV).transpose(1, 0, 2)


def _unpack_small_weights(g_in, g_uq, g_ukv):
    w_in = _slots_to_cols(g_in)
    zeros = jnp.zeros((D_MODEL, QK_NOPE), BF16)
    win_head = jnp.concatenate([w_in[:, :Q_RANK + KV_RANK], zeros, w_in[:, Q_RANK + KV_RANK:MLA_IN],
                                zeros[:, :LANES - QK_NOPE - QK_ROPE]], axis=1)
    win_conv = w_in[:, MLA_IN:].reshape(D_MODEL, 3, CONV_W // LANES, LANES).transpose(0, 2, 1, 3)
    win_conv = win_conv.reshape(D_MODEL, 3 * CONV_W)
    w_uq = _slots_to_cols(g_uq).reshape(Q_RANK, N_HEADS, QK_NOPE + QK_ROPE)
    wq = jnp.pad(w_uq, ((0, 0), (0, 0), (0, LANES - QK_NOPE - QK_ROPE))).reshape(Q_RANK, N_HEADS * LANES)
    w_ukv = _slots_to_cols(g_ukv).reshape(KV_RANK, N_HEADS, QK_NOPE + V_DIM)
    k_top = jnp.pad(w_ukv[:, :, :QK_NOPE], ((0, 0), (0, 0), (0, LANES - QK_NOPE))).reshape(KV_RANK, N_HEADS * LANES)
    v_top = w_ukv[:, :, QK_NOPE:].reshape(KV_RANK, N_HEADS * V_DIM)
    eye = jnp.pad(jnp.eye(QK_ROPE, dtype=BF16), ((QK_NOPE, LANES - QK_NOPE - QK_ROPE),) * 2)
    wk = jnp.concatenate([
        jnp.concatenate([k_top, v_top], axis=1),
        jnp.concatenate([jnp.tile(eye, (1, N_HEADS)), jnp.zeros((LANES, N_HEADS * V_DIM), BF16)], axis=1)], axis=0)
    return win_head, win_conv, wq, wk


def _pack_small_grads(d_head, d_conv, d_wq, d_wkk, d_wkv):
    d_conv = d_conv.reshape(D_MODEL, CONV_W // LANES, 3, LANES).transpose(0, 2, 1, 3).reshape(D_MODEL, 3 * CONV_W)
    g_in = jnp.concatenate([d_head[:, :Q_RANK + KV_RANK],
                            d_head[:, Q_RANK + KV_RANK + QK_NOPE:Q_RANK + KV_RANK + QK_NOPE + QK_ROPE], d_conv], axis=1)
    g_uq = d_wq.reshape(Q_RANK, N_HEADS, LANES)[:, :, :QK_NOPE + QK_ROPE].reshape(Q_RANK, -1)
    g_kn = d_wkk[:KV_RANK].reshape(KV_RANK, N_HEADS, LANES)[:, :, :QK_NOPE]
    g_v = d_wkv[:KV_RANK].reshape(KV_RANK, N_HEADS, V_DIM)
    g_ukv = jnp.concatenate([g_kn, g_v], axis=2).reshape(KV_RANK, -1)
    return [_cols_to_slots(g).astype(BF16) for g in (g_in, g_uq, g_ukv)]


def kernel(x, c, ctx, c_ctx, w_mod, b_mod, w_in, q_norm_g, w_uq, kv_norm_g, w_ukv, conv_w, w_out, w_mlp1, w_mlp2, final_norm_g, loss_target, m_c_ctx, m_w_mod, m_b_mod, m_w_in, m_q_norm_g, m_w_uq, m_kv_norm_g, m_w_ukv, m_conv_w, m_w_out, m_w_mlp1, m_w_mlp2, m_final_norm_g, v_c_ctx, v_w_mod, v_b_mod, v_w_in, v_q_norm_g, v_w_uq, v_kv_norm_g, v_w_ukv, v_conv_w, v_w_out, v_w_mlp1, v_w_mlp2, v_final_norm_g):
    me = _my_index()
    x2d, ctx2d, tgt = x[0], ctx[0], loss_target[0]
    s, l = x2d.shape[0], ctx2d.shape[0]
    t = s + l
    d = D_MODEL
    mod_cols = w_mod.shape[2]
    cw_cols = conv_w.shape[2]

    (c_all,) = _all_gather([jnp.pad(c, ((0, 7), (0, 0)))], "gather_c", True)
    a_rows = jnp.concatenate([c_all[:, 0, :], c_ctx[None, :], jnp.zeros((7, d), F32)], axis=0)
    b_cols = lax.dynamic_slice(b_mod, (0, me * mod_cols), (1, mod_cols))
    mod_cols_all = _adaln_fwd(a_rows, w_mod[0], b_cols, "adaln_fwd")
    cw_blk = jnp.pad(conv_w[0], ((0, 5), (0, mod_cols - cw_cols)))
    (gathered,) = _all_gather([jnp.concatenate([mod_cols_all, cw_blk], axis=0)], "gather_mod", True)
    mod_mine = lax.dynamic_index_in_dim(gathered, me, axis=1, keepdims=False).reshape(1, 6 * d)
    mod_ctx = gathered[:, 8, :].reshape(1, 6 * d)
    cw_full = gathered[:, 16:19, :cw_cols].transpose(1, 0, 2).reshape(3, CONV_W)

    early = [w.astype(BF16) for w in (w_in[0], w_uq[0], w_ukv[0])]
    late = [w.astype(BF16) for w in (w_out[0], w_mlp1[0], w_mlp2[0])]
    g_in, g_uq, g_ukv = _all_gather(early, "gather_weights", False)
    win_head, win_conv, wq, wk = _unpack_small_weights(g_in, g_uq, g_ukv)
    wk_k, wk_v = wk[:, :N_HEADS * LANES], wk[:, N_HEADS * LANES:]
    cos, sgn = _rope_tables(s, l)

    h_all = _modulate_all(x2d, ctx2d, mod_mine, mod_ctx, "modulate1")
    tm_t = _pick(t, (1088, 768, 256))
    tk_t = _pick(t, (2176, 768, 256))
    z_head = _matmul(h_all, win_head, mode="nn", name="in_proj_head", tm=tm_t, tn=512, tk=1024)
    z_conv = _matmul(h_all, win_conv, mode="nn", name="in_proj_conv", m=s, tm=1024, tn=1536, tk=1024)
    cq, kv_in = _qkv_prep(z_head, q_norm_g, kv_norm_g, cos, sgn, "qkv_prep")
    qf = _matmul(cq, wq, mode="nn", name="q_up", out_dtype=BF16, m=s, tm=1024, tn=1024, tk=256,
                 epilogue="rope", extra=(cos, sgn))
    kv = _matmul(kv_in, wk, mode="nn", name="kv_up", out_dtype=BF16, tm=tm_t, tn=1536, tk=256)
    attn, a_cat, stats, (g_out, w1, g_w2) = _attn_fwd(qf, kv, _Riding("gather", late), "attn_fwd")
    wo = g_out.reshape(d, d)
    w2 = g_w2.reshape(D_FF, d)
    a_cat = _conv_fwd(z_conv, cw_full, a_cat, "conv_fwd")
    (o, x1, h2), _ = _matmul_rows(a_cat, wo, _epi_resid_modulate, mode="nn", name="out_proj", tm=512, tk=1024,
                                  rows=[x2d], vecs=[(mod_mine, 2), (mod_mine, 3), (mod_mine, 4)],
                                  out_dtypes=[F32, F32, BF16])
    u1, act = _matmul(h2, w1, mode="nn", name="mlp_up", tm=2048, tk=1024, epilogue="relu2", slots="b_cols")
    (dx2, dm, fsums), _ = _matmul_rows(act, w2, _epi_final, mode="nn", name="mlp_down", tm=512, tk=2048,
                                       rows=[x1, tgt], vecs=[(mod_mine, 5), (final_norm_g[None, :], 0)],
                                       out_dtypes=[F32, BF16], sums=True)

    d_w2 = _matmul(act, dm, mode="tn", name="d_w_mlp2", out_dtype=BF16, tm=2048, tn=1024, tk=1024)
    du1 = _matmul(dm, w2, mode="nt", name="d_act", out_dtype=BF16, tm=2048, tn=1024, tk=1024,
                  epilogue="drelu2", extra=(u1,))
    d_w1 = _matmul(h2, du1, mode="tn", name="d_w_mlp1", out_dtype=BF16, tm=1024, tk=4096, slots="out")
    (dx1, do, sums2), _ = _matmul_rows(du1, w1, _epi_modulate2_bwd, mode="nt", name="d_h2", tm=512, tk=2048,
                                       slots="b_contract", rows=[x1, dx2, o], vecs=[(mod_mine, 4), (mod_mine, 2)],
                                       out_dtypes=[F32, BF16], sums=True)
    d_wo = _matmul(a_cat, do, mode="tn", name="d_w_out", out_dtype=BF16, tm=1024, tn=1024, tk=2048)
    da = _matmul(do, wo, mode="nt", name="d_a", tm=1024, tn=1024, tk=1024)
    dz_conv, d_cw = _conv_bwd(z_conv, cw_full, da, "conv_bwd")
    ready = [d_wo.reshape(N_DEV, d // N_DEV, d), d_w1, d_w2.reshape(N_DEV, D_FF // N_DEV, d)]
    dq, dk, dv, rode = _attn_bwd(qf, kv, attn, da, stats, cos, sgn, _Riding("exchange", ready), "attn_bwd")
    d_wq = _matmul(cq, dq, mode="tn", name="d_w_uq", k=s, tm=256, tn=1024, tk=4096)
    dcq = _matmul(dq, wq, mode="nt", name="d_cq", tm=1024, tn=256, tk=1024)
    d_wkk = _matmul(kv_in, dk, mode="tn", name="d_w_ukv_k", tm=256, tn=1024, tk=tk_t)
    d_wkv = _matmul(kv_in, dv, mode="tn", name="d_w_ukv_v", tm=256, tn=512, tk=tk_t)
    dkv_in = _matmul(dk, wk_k, mode="nt", name="d_kv_in_k", tm=tm_t, tn=256, tk=1024)
    dkv_in = _matmul(dv, wk_v, mode="nt", name="d_kv_in_v", tm=tm_t, tn=256, tk=512, addend=dkv_in)
    dz_head, psums = _qkv_prep_bwd(z_head, dcq, dkv_in, q_norm_g, kv_norm_g, cos, sgn, s, "qkv_prep_bwd")
    d_head = _matmul(h_all, dz_head, mode="tn", name="d_w_in_head", tm=1024, tn=512, tk=tk_t)
    d_conv = _matmul(h_all, dz_conv, mode="tn", name="d_w_in_conv", k=s, tm=1024, tn=1536, tk=2048)
    dh_head = _matmul(dz_head, win_head, mode="nt", name="d_h1_head", tm=tm_t, tn=1024, tk=512)
    send = _pack_small_grads(d_head, d_conv, d_wq, d_wkk, d_wkv)
    (grad_x, sums1), got = _matmul_rows(dz_conv, win_conv, _epi_modulate1_bwd, mode="nt", name="d_h1", tm=512,
                                        tk=win_conv.shape[1], rows=[dh_head, x2d, dx1], vecs=[(mod_mine, 1)],
                                        out_dtypes=[F32], sums=True, riding=_Riding("exchange", send))
    (sums1c,) = _modulate_bwd(dh_head, s // ROW_TILE, ctx2d, mod_ctx, 1, "modulate1_ctx_bwd")

    small = _pack_small(sums1, sums2, fsums, sums1c, psums, d_cw, "pack_small")
    (d_all,) = _all_gather([small], "gather_small_grads", True)
    d_cols = lax.dynamic_slice_in_dim(d_all, me * mod_cols, mod_cols, axis=2)
    g_w_mod, dsil, dsum = _adaln_bwd(a_rows.T, w_mod[0], d_cols[:, 0, :], d_cols[:, 1, :], d_all, "adaln_bwd")
    (dsil_all,) = _all_gather([dsil], "gather_d_cctx", True)
    loss = dsum[6, 0]
    g_cw = lax.dynamic_slice(dsum, (3, me * cw_cols), (3, cw_cols))

    slots = dict(zip(["w_in", "w_uq", "w_ukv"], got))
    slots.update(zip(["w_out", "w_mlp1", "w_mlp2"], rode))

    grads = {"w_mod": g_w_mod[None]}
    weights = {"c_ctx": c_ctx, "w_mod": w_mod, "b_mod": b_mod, "w_in": w_in, "q_norm_g": q_norm_g, "w_uq": w_uq,
               "kv_norm_g": kv_norm_g, "w_ukv": w_ukv, "conv_w": conv_w, "w_out": w_out, "w_mlp1": w_mlp1,
               "w_mlp2": w_mlp2, "final_norm_g": final_norm_g}
    m_in = {"c_ctx": m_c_ctx, "w_mod": m_w_mod, "b_mod": m_b_mod, "w_in": m_w_in, "q_norm_g": m_q_norm_g,
            "w_uq": m_w_uq, "kv_norm_g": m_kv_norm_g, "w_ukv": m_w_ukv, "conv_w": m_conv_w, "w_out": m_w_out,
            "w_mlp1": m_w_mlp1, "w_mlp2": m_w_mlp2, "final_norm_g": m_final_norm_g}
    v_in = {"c_ctx": v_c_ctx, "w_mod": v_w_mod, "b_mod": v_b_mod, "w_in": v_w_in, "q_norm_g": v_q_norm_g,
            "w_uq": v_w_uq, "kv_norm_g": v_kv_norm_g, "w_ukv": v_w_ukv, "conv_w": v_conv_w, "w_out": v_w_out,
            "w_mlp1": v_w_mlp1, "w_mlp2": v_w_mlp2, "final_norm_g": v_final_norm_g}
    names = list(weights)
    small_names = ["c_ctx", "b_mod", "q_norm_g", "kv_norm_g", "final_norm_g", "conv_w"]
    delta, new_m, new_v = {}, {}, {}

    def two_dims(a):
        return a.reshape(-1, a.shape[-1])

    small_out = _small_update(dsum, dsil_all, g_cw, [[two_dims(src[n]) for src in (weights, m_in, v_in)]
                                                      for n in small_names], "small_update")
    for n, outs in zip(small_names, small_out):
        grads[n], delta[n], new_m[n], new_v[n] = [a.reshape(weights[n].shape) for a in outs]
    for n in names:
        if n in small_names:
            continue
        shp = weights[n].shape
        two_d = (shp[0] * shp[1], shp[2])
        wmv = [a.reshape(two_d) for a in (weights[n], m_in[n], v_in[n])]
        if n in slots:
            outs = _adamw(wmv[0], slots[n], wmv[1], wmv[2], "adamw_" + n, slots=True)
            grads[n] = outs[0].reshape(shp)
            outs = outs[1:]
        else:
            outs = _adamw(wmv[0], grads[n].reshape(two_d), wmv[1], wmv[2], "adamw_" + n)
        delta[n], new_m[n], new_v[n] = [a.reshape(shp) for a in outs]

    return (loss, grad_x[None], *[grads[n] for n in names], *[delta[n] for n in names],
            *[new_m[n] for n in names], *[new_v[n] for n in names])
```

```python
import math

import jax
import jax.numpy as jnp
import numpy as np
from jax import lax
from jax.experimental import pallas as pl
from jax.experimental.pallas import tpu as pltpu

F32 = jnp.float32
BF16 = jnp.bfloat16

D_MODEL = 1024
GRID_W = 64
N_HEADS = 8
QK_NOPE = 64
QK_ROPE = 32
V_DIM = 64
Q_RANK = 256
KV_RANK = 128
MLA_IN = Q_RANK + KV_RANK + QK_ROPE
CONV_W = 512
HEAD_COLS = 512
D_FF = 4096
ROPE_THETA = 10000.0
EPS = 1e-6
ATTN_SCALE = 1.0 / math.sqrt(QK_NOPE + QK_ROPE)
LOG2_E = 1.0 / math.log(2.0)
EXP2_SCALE = ATTN_SCALE * LOG2_E
N_DEV = 8
LANES = 128

ADAM_LR, ADAM_B1, ADAM_B2, ADAM_EPS, ADAM_WD, ADAM_STEP = 0.001, 0.9, 0.999, 1e-08, 0.01, 10

ROW_TILE = 256
VMEM_BIG = 60 * 1024 * 1024


def _params(sem=None, vmem=None):
    return pltpu.CompilerParams(dimension_semantics=sem, vmem_limit_bytes=vmem)


def _pick(n, prefs):
    for p in prefs:
        if n % p == 0:
            return p
    return n


def _my_index():
    return 4 * lax.axis_index("x") + 2 * lax.axis_index("y") + lax.axis_index("c")


def _all_gather(arrays, name, in_vmem):
    space = pltpu.VMEM if in_vmem else pl.ANY
    n = len(arrays)

    def body(*refs):
        x_refs, out_refs = refs[:n], refs[n:2 * n]
        send_sems, recv_sems, local_sems = refs[2 * n:]
        x, y, c = lax.axis_index("x"), lax.axis_index("y"), lax.axis_index("c")
        me, sibling = (x, y, c), (x, y, 1 - c)
        chips = [(1 - x, y), (x, 1 - y), (1 - x, 1 - y)]

        def slot(a, px, py, pc):
            return out_refs[a].at[4 * px + 2 * py + pc]

        def copy(a, k, block, to, src=None):
            return pltpu.make_async_remote_copy(
                src_ref=slot(a, *block) if src is None else src, dst_ref=slot(a, *block),
                send_sem=send_sems.at[7 * a + k], recv_sem=recv_sems.at[7 * a + k],
                device_id=to, device_id_type=pl.DeviceIdType.MESH)

        mine = [pltpu.make_async_copy(x_refs[a], slot(a, *me), local_sems.at[a]) for a in range(n)]
        for cp in mine:
            cp.start()
        started = []
        for a in range(n):
            first = [copy(a, 0, me, sibling, src=x_refs[a])]
            first += [copy(a, 1 + j, me, (*chip, c), src=x_refs[a]) for j, chip in enumerate(chips)]
            for cp in first:
                cp.start()
            started += first
        for a in range(n):
            for j, chip in enumerate(chips):
                copy(a, 1 + j, (*chip, c), me).wait_recv()
                passed = copy(a, 4 + j, (*chip, c), sibling)
                passed.start()
                started.append(passed)
        for a in range(n):
            copy(a, 0, sibling, me).wait_recv()
            for j, chip in enumerate(chips):
                copy(a, 4 + j, (*chip, 1 - c), me).wait_recv()
        for cp in started:
            cp.wait_send()
        for cp in mine:
            cp.wait()

    outs = pl.pallas_call(
        body, name=name,
        out_shape=tuple(jax.ShapeDtypeStruct((N_DEV,) + a.shape, a.dtype) for a in arrays),
        in_specs=[pl.BlockSpec(memory_space=space)] * n,
        out_specs=tuple(pl.BlockSpec(memory_space=space) for _ in arrays),
        scratch_shapes=[pltpu.SemaphoreType.DMA((7 * n,)), pltpu.SemaphoreType.DMA((7 * n,)),
                        pltpu.SemaphoreType.DMA((n,))],
    )(*arrays)
    return list(outs)


class _Riding:
    def __init__(self, kind, arrays):
        self.kind, self.arrays, self.n = kind, list(arrays), len(arrays)
        lead = (N_DEV,) if kind == "gather" else ()
        self.out_shape = [jax.ShapeDtypeStruct(lead + a.shape, a.dtype) for a in self.arrays]
        self.specs = [pl.BlockSpec(memory_space=pl.ANY)] * self.n
        self.scratch = [pltpu.SemaphoreType.DMA((7 * self.n,)), pltpu.SemaphoreType.DMA((7 * self.n,)),
                        pltpu.SemaphoreType.DMA((self.n,))]

    def copies(self, x_refs, y_refs, send_sems, recv_sems, local_sems):
        x, y, c = lax.axis_index("x"), lax.axis_index("y"), lax.axis_index("c")
        me = 4 * x + 2 * y + c
        local, sends, landings = [], [], []
        for a in range(self.n):
            src_mine = x_refs[a] if self.kind == "gather" else x_refs[a].at[me]
            local.append(pltpu.make_async_copy(src_mine, y_refs[a].at[me], local_sems.at[a]))
            for k in range(1, N_DEV):
                peer = (1 - x if k & 4 else x, 1 - y if k & 2 else y, 1 - c if k & 1 else c)
                pid = 4 * peer[0] + 2 * peer[1] + peer[2]
                src = x_refs[a] if self.kind == "gather" else x_refs[a].at[pid]
                for dst, out in ((me, sends), (pid, landings)):
                    out.append(pltpu.make_async_remote_copy(
                        src_ref=src, dst_ref=y_refs[a].at[dst],
                        send_sem=send_sems.at[7 * a + k - 1], recv_sem=recv_sems.at[7 * a + k - 1],
                        device_id=peer, device_id_type=pl.DeviceIdType.MESH))
        return local, sends, landings

    def run(self, first, last, x_refs, y_refs, sems):
        if self.n == 0:
            return None
        local, sends, landings = self.copies(x_refs, y_refs, *sems)

        @pl.when(first)
        def _():
            for cp in local + sends:
                cp.start()

        return local, sends, landings, last

    @staticmethod
    def finish(state):
        if state is None:
            return
        local, sends, landings, last = state

        @pl.when(last)
        def _():
            for cp in landings:
                cp.wait_recv()
            for cp in sends:
                cp.wait_send()
            for cp in local:
                cp.wait()


_DIMS = {"nn": (((1,), (0,)), ((), ())), "nt": (((1,), (1,)), ((), ())), "tn": (((0,), (0,)), ((), ()))}
NT_DIMS = _DIMS["nt"]
TN_DIMS = _DIMS["tn"]


def _swap8(x):
    lane = lax.broadcasted_iota(jnp.int32, x.shape, 1)
    return jnp.where((lane & 15) < 8, pltpu.roll(x, LANES - 8, 1), pltpu.roll(x, 8, 1))


def _rope(x, cos, sgn, bwd):
    return x * cos + (_swap8(x * sgn) if bwd else _swap8(x) * sgn)


def _matmul(a, b, *, mode, name, out_dtype=F32, tm=512, tn=512, tk=512, m=None, k=None,
            epilogue=None, extra=(), addend=None, slots=None):
    if mode == "nn":
        m = a.shape[0] if m is None else m
        k = a.shape[1]
        n = N_DEV * b.shape[2] if slots == "b_cols" else b.shape[1]
    elif mode == "nt":
        m = a.shape[0] if m is None else m
        k = a.shape[1]
        n = b.shape[1] if slots == "b_contract" else b.shape[0]
    else:
        k = a.shape[0] if k is None else k
        m, n = a.shape[1], b.shape[1]
    tm, tn, tk = min(tm, m), min(tn, n), min(tk, k)
    group = 1
    if slots == "b_cols":
        tn = b.shape[2]
    if slots == "b_contract":
        group = max(1, tk // b.shape[2])
        tk = group * b.shape[2]
    if slots == "out":
        tn = n // N_DEV
    assert m % tm == 0 and n % tn == 0 and k % tk == 0, (name, m, n, k, tm, tn, tk)
    nk = k // tk
    dims = _DIMS[mode]
    a_spec = (pl.BlockSpec((tk, tm), lambda i, j, kk: (kk, i)) if mode == "tn"
              else pl.BlockSpec((tm, tk), lambda i, j, kk: (i, kk)))
    if slots == "b_cols":
        b_spec = pl.BlockSpec((None, tk, tn), lambda i, j, kk: (j, kk, 0))
    elif slots == "b_contract":
        b_spec = pl.BlockSpec((group, tn, tk // group), lambda i, j, kk: (kk, j, 0))
    elif mode == "nt":
        b_spec = pl.BlockSpec((tn, tk), lambda i, j, kk: (j, kk))
    else:
        b_spec = pl.BlockSpec((tk, tn), lambda i, j, kk: (kk, j))
    tile = pl.BlockSpec((tm, tn), lambda i, j, kk: (i, j))
    if slots == "out":
        o_spec = pl.BlockSpec((None, tm, tn), lambda i, j, kk: (j, i, 0))
        o_shape = (N_DEV, m, tn)
    else:
        o_spec, o_shape = tile, (m, n)
    in_specs, args = [a_spec, b_spec], [a, b]
    if epilogue == "drelu2":
        in_specs.append(tile)
    elif epilogue == "rope":
        in_specs += [pl.BlockSpec((tm, LANES), lambda i, j, kk: (i, 0))] * 2
    args += list(extra)
    if addend is not None:
        in_specs.append(tile)
        args.append(addend)
    if epilogue == "relu2":
        out_shape = (jax.ShapeDtypeStruct(o_shape, BF16), jax.ShapeDtypeStruct(o_shape, BF16))
        out_specs = (o_spec, o_spec)
    else:
        out_shape = jax.ShapeDtypeStruct(o_shape, out_dtype)
        out_specs = o_spec
    n_in = len(args)
    n_out = 2 if epilogue == "relu2" else 1

    def body(*refs):
        a_ref, b_ref = refs[0], refs[1]
        outs = refs[n_in:n_in + n_out]
        if slots == "b_contract":
            c = tk // group
            part = lax.dot_general(a_ref[:, 0:c], b_ref[0], dims, preferred_element_type=F32)
            for u in range(1, group):
                part = part + lax.dot_general(a_ref[:, u * c:(u + 1) * c], b_ref[u], dims, preferred_element_type=F32)
        else:
            part = lax.dot_general(a_ref[...], b_ref[...], dims, preferred_element_type=F32)

        def finish(acc):
            if addend is not None:
                acc = acc + refs[n_in - 1][...]
            if epilogue == "relu2":
                outs[0][...] = acc.astype(BF16)
                r = jnp.maximum(acc, 0.0)
                outs[1][...] = (r * r).astype(BF16)
            elif epilogue == "drelu2":
                u = refs[2][...].astype(F32)
                outs[0][...] = (acc * (2.0 * jnp.maximum(u, 0.0))).astype(out_dtype)
            elif epilogue == "rope":
                cos, sgn = refs[2][...], refs[3][...]
                for h in range(tn // LANES):
                    sl = slice(h * LANES, (h + 1) * LANES)
                    outs[0][:, sl] = _rope(acc[:, sl], cos, sgn, False).astype(out_dtype)
            else:
                outs[0][...] = acc.astype(out_dtype)

        if nk == 1:
            finish(part)
        else:
            acc_ref = refs[n_in + n_out]
            kk = pl.program_id(2)

            @pl.when(kk == 0)
            def _():
                acc_ref[...] = part

            @pl.when(kk > 0)
            def _():
                acc_ref[...] += part

            @pl.when(kk == nk - 1)
            def _():
                finish(acc_ref[...])

    return pl.pallas_call(
        body, name=name, grid=(m // tm, n // tn, nk),
        out_shape=out_shape, in_specs=in_specs, out_specs=out_specs,
        scratch_shapes=[pltpu.VMEM((tm, tn), F32)] if nk > 1 else [],
        compiler_params=_params(("parallel", "parallel", "arbitrary"), VMEM_BIG),
    )(*args)


def _rstd(x):
    return lax.rsqrt(jnp.mean(x * x, axis=1, keepdims=True) + EPS)


def _norm_bwd(dxn, xn, r):
    return r * (dxn - xn * jnp.mean(dxn * xn, axis=1, keepdims=True))


def _vec(col):
    return pl.BlockSpec((1, D_MODEL), lambda i: (0, col))


def _matmul_rows(a, b, epi, *, mode, name, tm, tk, rows=(), vecs=(), out_dtypes=(), sums=False, slots=None,
                 riding=None):
    m, k = a.shape
    n = D_MODEL
    tm, tk = min(tm, m), min(tk, k)
    riding = riding or _Riding("gather", [])
    group = 1
    if slots == "b_contract":
        group = max(1, tk // b.shape[2])
        tk = group * b.shape[2]
        b_spec = pl.BlockSpec((group, n, tk // group), lambda i, kk: (kk, 0, 0))
    elif mode == "nt":
        b_spec = pl.BlockSpec((n, tk), lambda i, kk: (0, kk))
    else:
        b_spec = pl.BlockSpec((tk, n), lambda i, kk: (kk, 0))
    assert m % tm == 0 and k % tk == 0, (name, m, k, tm, tk)
    ni, nk = m // tm, k // tk
    dims = _DIMS[mode]
    tile = pl.BlockSpec((tm, n), lambda i, kk: (i, 0))
    in_specs = [pl.BlockSpec((tm, tk), lambda i, kk: (i, kk)), b_spec] + [tile] * len(rows)
    in_specs += [pl.BlockSpec((1, n), lambda i, kk, col=col: (0, col)) for _, col in vecs]
    args = [a, b, *rows, *[v for v, _ in vecs]]
    out_shape = [jax.ShapeDtypeStruct((m, n), dt) for dt in out_dtypes]
    out_specs = [tile] * len(out_dtypes)
    if sums:
        out_shape.append(jax.ShapeDtypeStruct((8, n), F32))
        out_specs.append(pl.BlockSpec((8, n), lambda i, kk: (0, 0)))
    n_rows, n_vecs, n_outs, nr = len(rows), len(vecs), len(out_dtypes), riding.n
    n_in = 2 + n_rows + n_vecs

    def body(*refs):
        a_ref, b_ref = refs[0], refs[1]
        row_refs = refs[2:2 + n_rows]
        vec_refs = refs[2 + n_rows:n_in]
        x_refs = refs[n_in:n_in + nr]
        out_refs = refs[n_in + nr:n_in + nr + n_outs]
        pos = n_in + nr + n_outs
        sums_ref = refs[pos] if sums else None
        pos += 1 if sums else 0
        y_refs = refs[pos:pos + nr]
        pos += nr
        acc_ref = refs[pos] if nk > 1 else None
        sem_refs = refs[pos + (1 if nk > 1 else 0):]
        i, kk = pl.program_id(0), pl.program_id(1)
        state = riding.run((i == 0) & (kk == 0), (i == ni - 1) & (kk == nk - 1), x_refs, y_refs, sem_refs)
        if slots == "b_contract":
            c = tk // group
            part = lax.dot_general(a_ref[:, 0:c], b_ref[0], dims, preferred_element_type=F32)
            for u in range(1, group):
                part = part + lax.dot_general(a_ref[:, u * c:(u + 1) * c], b_ref[u], dims, preferred_element_type=F32)
        else:
            part = lax.dot_general(a_ref[...], b_ref[...], dims, preferred_element_type=F32)

        def finish(acc):
            nsub = tm // ROW_TILE
            for r in range(nsub):
                blk = pl.ds(r * ROW_TILE, ROW_TILE)
                epi(acc[r * ROW_TILE:(r + 1) * ROW_TILE], [ref.at[blk] for ref in row_refs], vec_refs,
                    [ref.at[blk] for ref in out_refs], sums_ref,
                    (i == 0) if r == 0 else None, (i == ni - 1) if r == nsub - 1 else None)

        if nk == 1:
            finish(part)
        else:
            @pl.when(kk == 0)
            def _():
                acc_ref[...] = part

            @pl.when(kk > 0)
            def _():
                acc_ref[...] += part

            @pl.when(kk == nk - 1)
            def _():
                finish(acc_ref)

        riding.finish(state)

    outs = pl.pallas_call(
        body, name=name, grid=(ni, nk),
        out_shape=(*out_shape, *riding.out_shape),
        in_specs=[*in_specs, *riding.specs], out_specs=(*out_specs, *riding.specs),
        scratch_shapes=([pltpu.VMEM((tm, n), F32)] if nk > 1 else []) + (riding.scratch if nr else []),
        compiler_params=_params(("arbitrary", "arbitrary"), VMEM_BIG),
    )(*args, *riding.arrays)
    n_own = len(out_shape)
    return list(outs[:n_own]), list(outs[n_own:])


def _zero_sums_at_start(sums_ref, first):
    if first is not None:
        @pl.when(first)
        def _():
            sums_ref[...] = jnp.zeros_like(sums_ref)


def _epi_resid_modulate(acc, rows, vecs, outs, sums_ref, first, last):
    (x_ref,), (g_ref, sh_ref, sc_ref) = rows, vecs
    x1 = x_ref[...] + g_ref[...] * acc
    outs[0][...] = acc
    outs[1][...] = x1
    outs[2][...] = (x1 * _rstd(x1) * (1.0 + sc_ref[...]) + sh_ref[...]).astype(BF16)


def _epi_final(acc, rows, vecs, outs, sums_ref, first, last):
    (x1_ref, t_ref), (g_ref, gf_ref) = rows, vecs
    d = acc.shape[1]
    x2 = x1_ref[...] + g_ref[...] * acc
    r = _rstd(x2)
    xn = x2 * r
    err = xn * gf_ref[...] - t_ref[...]
    dy = err * (1.0 / d)
    dx2 = _norm_bwd(dy * gf_ref[...], xn, r)
    outs[0][...] = dx2
    outs[1][...] = (dx2 * g_ref[...]).astype(BF16)
    _zero_sums_at_start(sums_ref, first)
    sums_ref[0:1, :] += jnp.sum(dy * xn, axis=0, keepdims=True)
    sums_ref[1:2, :] += jnp.sum(dx2 * acc, axis=0, keepdims=True)
    sums_ref[2:3, :] += jnp.sum(err * err, axis=0, keepdims=True)

    if last is not None:
        @pl.when(last)
        def _():
            tot = jnp.sum(sums_ref[2:3, :], axis=1, keepdims=True) * (0.5 / d)
            sums_ref[3:4, :] = jnp.broadcast_to(tot, (1, d))


def _epi_modulate2_bwd(acc, rows, vecs, outs, sums_ref, first, last):
    (x_ref, dres_ref, o_ref), (sc_ref, g_ref) = rows, vecs
    x = x_ref[...]
    r = _rstd(x)
    xn = x * r
    dx = dres_ref[...] + _norm_bwd(acc * (1.0 + sc_ref[...]), xn, r)
    outs[0][...] = dx
    outs[1][...] = (dx * g_ref[...]).astype(BF16)
    _zero_sums_at_start(sums_ref, first)
    sums_ref[0:1, :] += jnp.sum(acc * xn, axis=0, keepdims=True)
    sums_ref[1:2, :] += jnp.sum(acc, axis=0, keepdims=True)
    sums_ref[2:3, :] += jnp.sum(dx * o_ref[...], axis=0, keepdims=True)


def _epi_modulate1_bwd(acc, rows, vecs, outs, sums_ref, first, last):
    (add_ref, x_ref, dres_ref), (sc_ref,) = rows, vecs
    dh = acc + add_ref[...]
    x = x_ref[...]
    r = _rstd(x)
    xn = x * r
    outs[0][...] = dres_ref[...] + _norm_bwd(dh * (1.0 + sc_ref[...]), xn, r)
    _zero_sums_at_start(sums_ref, first)
    sums_ref[0:1, :] += jnp.sum(dh * xn, axis=0, keepdims=True)
    sums_ref[1:2, :] += jnp.sum(dh, axis=0, keepdims=True)


def _modulate_all(x, ctx, mod, mod_ctx, name):
    s, d = x.shape
    t = s + ctx.shape[0]
    ns = s // ROW_TILE
    nc = ctx.shape[0] // ROW_TILE

    def body(x_ref, c_ref, sh_ref, sc_ref, shc_ref, scc_ref, h_ref):
        i = pl.program_id(0)

        @pl.when(i < ns)
        def _():
            v = x_ref[...]
            h_ref[...] = (v * _rstd(v) * (1.0 + sc_ref[...]) + sh_ref[...]).astype(BF16)

        @pl.when(i >= ns)
        def _():
            v = c_ref[...]
            h_ref[...] = (v * _rstd(v) * (1.0 + scc_ref[...]) + shc_ref[...]).astype(BF16)

    return pl.pallas_call(
        body, name=name, grid=(ns + nc,),
        out_shape=jax.ShapeDtypeStruct((t, d), BF16),
        in_specs=[pl.BlockSpec((ROW_TILE, d), lambda i: (jnp.minimum(i, ns - 1), 0)),
                  pl.BlockSpec((ROW_TILE, d), lambda i: (jnp.maximum(i - ns, 0), 0)),
                  _vec(0), _vec(1), _vec(0), _vec(1)],
        out_specs=pl.BlockSpec((ROW_TILE, d), lambda i: (i, 0)),
        compiler_params=_params(("arbitrary",)),
    )(x, ctx, mod, mod, mod_ctx, mod_ctx)


def _modulate_bwd(dh, row_off, xsrc, mod, scale_col, name, dres=None, o=None):
    s, d = xsrc.shape
    n = s // ROW_TILE
    has_dx, has_o = dres is not None, o is not None
    assert has_dx or not has_o

    def body(*refs):
        it = iter(refs)
        dh_ref, x_ref, sc_ref = next(it), next(it), next(it)
        dres_ref = next(it) if has_dx else None
        o_ref, g_ref = (next(it), next(it)) if has_o else (None, None)
        dx_ref = next(it) if has_dx else None
        do_ref = next(it) if has_o else None
        sums_ref = next(it)
        i = pl.program_id(0)
        x = x_ref[...]
        r = _rstd(x)
        xn = x * r
        dhv = dh_ref[...]

        @pl.when(i == 0)
        def _():
            sums_ref[...] = jnp.zeros_like(sums_ref)

        sums_ref[0:1, :] += jnp.sum(dhv * xn, axis=0, keepdims=True)
        sums_ref[1:2, :] += jnp.sum(dhv, axis=0, keepdims=True)
        if has_dx:
            dx = dres_ref[...] + _norm_bwd(dhv * (1.0 + sc_ref[...]), xn, r)
            dx_ref[...] = dx
            if has_o:
                do_ref[...] = (dx * g_ref[...]).astype(BF16)
                sums_ref[2:3, :] += jnp.sum(dx * o_ref[...], axis=0, keepdims=True)

    row = pl.BlockSpec((ROW_TILE, d), lambda i: (i, 0))
    in_specs = [pl.BlockSpec((ROW_TILE, d), lambda i: (i + row_off, 0)), row, _vec(scale_col)]
    args = [dh, xsrc, mod]
    out_shape, out_specs = [], []
    if has_dx:
        in_specs.append(row)
        args.append(dres)
        out_shape.append(jax.ShapeDtypeStruct((s, d), F32))
        out_specs.append(row)
    if has_o:
        in_specs += [row, _vec(2)]
        args += [o, mod]
        out_shape.append(jax.ShapeDtypeStruct((s, d), BF16))
        out_specs.append(row)
    out_shape.append(jax.ShapeDtypeStruct((8, d), F32))
    out_specs.append(pl.BlockSpec((8, d), lambda i: (0, 0)))
    return pl.pallas_call(
        body, name=name, grid=(n,),
        out_shape=tuple(out_shape), in_specs=in_specs, out_specs=tuple(out_specs),
        compiler_params=_params(("arbitrary",)),
    )(*args)


def _qkv_prep(z, q_gain, kv_gain, cos, sgn, name):
    t = z.shape[0]

    def body(z_ref, qg_ref, kg_ref, c_ref, s_ref, cq_ref, kv_ref):
        zq = z_ref[:, 0:Q_RANK]
        cq_ref[...] = (zq * _rstd(zq) * qg_ref[...]).astype(BF16)
        zk = z_ref[:, Q_RANK:Q_RANK + KV_RANK]
        kv_ref[:, 0:KV_RANK] = (zk * _rstd(zk) * kg_ref[...]).astype(BF16)
        kr = z_ref[:, Q_RANK + KV_RANK:HEAD_COLS]
        kv_ref[:, KV_RANK:KV_RANK + LANES] = _rope(kr, c_ref[...], s_ref[...], False).astype(BF16)

    tab = pl.BlockSpec((ROW_TILE, LANES), lambda i: (i, 0))
    return pl.pallas_call(
        body, name=name, grid=(t // ROW_TILE,),
        out_shape=(jax.ShapeDtypeStruct((t, Q_RANK), BF16), jax.ShapeDtypeStruct((t, KV_RANK + LANES), BF16)),
        in_specs=[pl.BlockSpec((ROW_TILE, HEAD_COLS), lambda i: (i, 0)),
                  pl.BlockSpec((1, Q_RANK), lambda i: (0, 0)), pl.BlockSpec((1, KV_RANK), lambda i: (0, 0)), tab, tab],
        out_specs=(pl.BlockSpec((ROW_TILE, Q_RANK), lambda i: (i, 0)),
                   pl.BlockSpec((ROW_TILE, KV_RANK + LANES), lambda i: (i, 0))),
        compiler_params=_params(("parallel",)),
    )(z, q_gain, kv_gain, cos, sgn)


def _qkv_prep_bwd(z, dcq, dkv, q_gain, kv_gain, cos, sgn, s, name):
    t = z.shape[0]
    ns = s // ROW_TILE

    def body(z_ref, dcq_ref, dkv_ref, qg_ref, kg_ref, c_ref, s_ref, dz_ref, sums_ref):
        i = pl.program_id(0)

        @pl.when(i == 0)
        def _():
            sums_ref[...] = jnp.zeros_like(sums_ref)

        @pl.when(i < ns)
        def _():
            zq = z_ref[:, 0:Q_RANK]
            r = _rstd(zq)
            zn = zq * r
            dc = dcq_ref[...]
            sums_ref[0:1, :] += jnp.sum(dc * zn, axis=0, keepdims=True)
            dz_ref[:, 0:Q_RANK] = _norm_bwd(dc * qg_ref[...], zn, r).astype(BF16)

        @pl.when(i >= ns)
        def _():
            dz_ref[:, 0:Q_RANK] = jnp.zeros((ROW_TILE, Q_RANK), BF16)

        zk = z_ref[:, Q_RANK:Q_RANK + KV_RANK]
        r = _rstd(zk)
        zn = zk * r
        dc = dkv_ref[:, 0:KV_RANK]
        sums_ref[1:2, 0:KV_RANK] += jnp.sum(dc * zn, axis=0, keepdims=True)
        dz_ref[:, Q_RANK:Q_RANK + KV_RANK] = _norm_bwd(dc * kg_ref[...], zn, r).astype(BF16)
        dkr = dkv_ref[:, KV_RANK:KV_RANK + LANES]
        dz_ref[:, Q_RANK + KV_RANK:HEAD_COLS] = _rope(dkr, c_ref[...], s_ref[...], True).astype(BF16)

    tab = pl.BlockSpec((ROW_TILE, LANES), lambda i: (i, 0))
    return pl.pallas_call(
        body, name=name, grid=(t // ROW_TILE,),
        out_shape=(jax.ShapeDtypeStruct((t, HEAD_COLS), BF16), jax.ShapeDtypeStruct((8, Q_RANK), F32)),
        in_specs=[pl.BlockSpec((ROW_TILE, HEAD_COLS), lambda i: (i, 0)),
                  pl.BlockSpec((ROW_TILE, Q_RANK), lambda i: (jnp.minimum(i, ns - 1), 0)),
                  pl.BlockSpec((ROW_TILE, KV_RANK + LANES), lambda i: (i, 0)),
                  pl.BlockSpec((1, Q_RANK), lambda i: (0, 0)), pl.BlockSpec((1, KV_RANK), lambda i: (0, 0)), tab, tab],
        out_specs=(pl.BlockSpec((ROW_TILE, HEAD_COLS), lambda i: (i, 0)), pl.BlockSpec((8, Q_RANK), lambda i: (0, 0))),
        compiler_params=_params(("arbitrary",)),
    )(z, dcq, dkv, q_gain, kv_gain, cos, sgn)


def _shift_rows(u, s):
    rowi = lax.broadcasted_iota(jnp.int32, u.shape, 0)
    prev = jnp.where(rowi == 0, 0.0, pltpu.roll(u, 1, 0))
    nxt = jnp.where(rowi == s - 1, 0.0, pltpu.roll(u, s - 1, 0))
    return prev, nxt


def _conv_fwd(z_conv, cw, a_cat, name):
    s = z_conv.shape[0]

    def body(z_ref, w_ref, a_in_ref, o_ref):
        del a_in_ref
        gb, gc, xv = z_ref[:, 0:LANES], z_ref[:, LANES:2 * LANES], z_ref[:, 2 * LANES:3 * LANES]
        u = gc * xv
        prev, nxt = _shift_rows(u, s)
        y = w_ref[0:1, :] * prev + w_ref[1:2, :] * u + w_ref[2:3, :] * nxt
        o_ref[...] = (gb * y).astype(BF16)

    return pl.pallas_call(
        body, name=name, grid=(CONV_W // LANES,),
        out_shape=jax.ShapeDtypeStruct(a_cat.shape, a_cat.dtype),
        in_specs=[pl.BlockSpec((s, 3 * LANES), lambda j: (0, j)), pl.BlockSpec((3, LANES), lambda j: (0, j)),
                  pl.BlockSpec(memory_space=pl.ANY)],
        out_specs=pl.BlockSpec((s, LANES), lambda j: (0, 4 + j)),
        input_output_aliases={2: 0},
        compiler_params=_params(("parallel",), VMEM_BIG),
    )(z_conv, cw, a_cat)


def _conv_bwd(z_conv, cw, da, name):
    s = z_conv.shape[0]

    def body(z_ref, w_ref, da_ref, dz_ref, dw_ref):
        gb, gc, xv = z_ref[:, 0:LANES], z_ref[:, LANES:2 * LANES], z_ref[:, 2 * LANES:3 * LANES]
        u = gc * xv
        prev, nxt = _shift_rows(u, s)
        dcv = da_ref[...]
        dz_ref[:, 0:LANES] = (dcv * (w_ref[0:1, :] * prev + w_ref[1:2, :] * u + w_ref[2:3, :] * nxt)).astype(BF16)
        dy = dcv * gb
        dw_ref[0:1, :] = jnp.sum(dy * prev, axis=0, keepdims=True)
        dw_ref[1:2, :] = jnp.sum(dy * u, axis=0, keepdims=True)
        dw_ref[2:3, :] = jnp.sum(dy * nxt, axis=0, keepdims=True)
        dyp, dyn = _shift_rows(dy, s)
        du = w_ref[0:1, :] * dyn + w_ref[1:2, :] * dy + w_ref[2:3, :] * dyp
        dz_ref[:, LANES:2 * LANES] = (du * xv).astype(BF16)
        dz_ref[:, 2 * LANES:3 * LANES] = (du * gc).astype(BF16)

    blk = pl.BlockSpec((s, 3 * LANES), lambda j: (0, j))
    cws = pl.BlockSpec((3, LANES), lambda j: (0, j))
    return pl.pallas_call(
        body, name=name, grid=(CONV_W // LANES,),
        out_shape=(jax.ShapeDtypeStruct(z_conv.shape, BF16), jax.ShapeDtypeStruct((3, CONV_W), F32)),
        in_specs=[blk, cws, pl.BlockSpec((s, LANES), lambda j: (0, 4 + j))], out_specs=(blk, cws),
        compiler_params=_params(("parallel",), VMEM_BIG),
    )(z_conv, cw, da)


ATT_TQ = 256
ATT_TQ_BWD = 512


def _head_mask(shape, hh):
    lane = lax.broadcasted_iota(jnp.int32, shape, 1)
    return (lane >= hh * V_DIM) & (lane < (hh + 1) * V_DIM)


def _attn_fwd(qf, kv, riding, name):
    s, t = qf.shape[0], kv.shape[0]
    nq = s // ATT_TQ
    nr = riding.n

    def body(*refs):
        q_ref, k_ref, v_ref = refs[:3]
        o_ref, ob_ref, st_ref = refs[3 + nr:6 + nr]
        p, i = pl.program_id(0), pl.program_id(1)
        state = riding.run((p == 0) & (i == 0), (p == N_HEADS // 2 - 1) & (i == nq - 1),
                           refs[3:3 + nr], refs[6 + nr:6 + 2 * nr], refs[6 + 2 * nr:])
        v = v_ref[...]
        vlane = lax.broadcasted_iota(jnp.int32, v.shape, 1)
        olane = lax.broadcasted_iota(jnp.int32, (ATT_TQ, LANES), 1)
        acc = jnp.zeros((ATT_TQ, LANES), F32)
        stat = jnp.zeros((ATT_TQ, LANES), F32)
        for hh in range(2):
            sl = slice(hh * LANES, (hh + 1) * LANES)
            sc = lax.dot_general(q_ref[:, sl], k_ref[:, sl], NT_DIMS, preferred_element_type=F32)
            mx = jnp.max(sc, axis=1, keepdims=True)
            e = jnp.exp2((sc - mx) * EXP2_SCALE).astype(BF16)
            one_lane = (1 - hh) * V_DIM
            vm = jnp.where(_head_mask(v.shape, hh), v, jnp.where(vlane == one_lane, 1.0, 0.0).astype(BF16))
            r = jnp.dot(e, vm, preferred_element_type=F32)
            den = jnp.sum(jnp.where(olane == one_lane, r, 0.0), axis=1, keepdims=True)
            acc = acc + jnp.where(_head_mask(r.shape, hh), r * (1.0 / den), 0.0)
            stat = stat + jnp.where(olane == hh, mx * EXP2_SCALE + jnp.log(den) * LOG2_E, 0.0)
        o_ref[...] = acc
        ob_ref[...] = acc.astype(BF16)
        st_ref[...] = stat.T[0:8, :]
        riding.finish(state)

    o_spec = pl.BlockSpec((ATT_TQ, LANES), lambda p, i: (i, p))
    outs = pl.pallas_call(
        body, name=name, grid=(N_HEADS // 2, nq),
        out_shape=(jax.ShapeDtypeStruct((s, N_HEADS * V_DIM), F32),
                   jax.ShapeDtypeStruct((s, D_MODEL), BF16),
                   jax.ShapeDtypeStruct((N_HEADS // 2 * 8, s), F32), *riding.out_shape),
        in_specs=[pl.BlockSpec((ATT_TQ, 2 * LANES), lambda p, i: (i, p)),
                  pl.BlockSpec((t, 2 * LANES), lambda p, i: (0, p)),
                  pl.BlockSpec((t, LANES), lambda p, i: (0, N_HEADS + p)), *riding.specs],
        out_specs=(o_spec, o_spec, pl.BlockSpec((8, ATT_TQ), lambda p, i: (p, i)), *riding.specs),
        scratch_shapes=riding.scratch,
        compiler_params=_params(("arbitrary", "arbitrary"), VMEM_BIG),
    )(qf, kv, kv, *riding.arrays)
    return outs[0], outs[1], outs[2], list(outs[3:])


def _attn_bwd(qf, kv, o, da, stats, cos, sgn, riding, name):
    s, t = qf.shape[0], kv.shape[0]
    ATT_TQ = ATT_TQ_BWD
    nq = s // ATT_TQ
    nr = riding.n

    def body(*refs):
        q_ref, k_ref, v_ref, o_ref, do_ref, st_ref, c_ref, s_ref = refs[:8]
        dq_ref, dk_ref, dv_ref = refs[8 + nr:11 + nr]
        dk_acc, dv_acc = refs[11 + 2 * nr:13 + 2 * nr]
        p, i = pl.program_id(0), pl.program_id(1)
        state = riding.run((p == 0) & (i == 0), (p == N_HEADS // 2 - 1) & (i == nq - 1),
                           refs[8:8 + nr], refs[11 + nr:11 + 2 * nr], refs[13 + 2 * nr:])

        @pl.when(i == 0)
        def _():
            dk_acc[...] = jnp.zeros_like(dk_acc)
            dv_acc[...] = jnp.zeros_like(dv_acc)

        v = v_ref[...]
        do = do_ref[...]
        od = do * o_ref[...]
        ones = jnp.ones((8, LANES), F32)
        for hh in range(2):
            sl = slice(hh * LANES, (hh + 1) * LANES)
            q, k = q_ref[:, sl], k_ref[:, sl]
            mask = _head_mask(do.shape, hh)
            dom = jnp.where(mask, do, 0.0).astype(BF16)
            delta = lax.dot_general(ones, jnp.where(mask, od, 0.0), NT_DIMS, preferred_element_type=F32,
                                    precision=lax.Precision.HIGHEST)[0:1, :]
            st = lax.dot_general(k, q, NT_DIMS, preferred_element_type=F32)
            pt = jnp.exp2(st * EXP2_SCALE - st_ref[hh:hh + 1, :]).astype(BF16)
            dpt = lax.dot_general(v, dom, NT_DIMS, preferred_element_type=F32)
            dst = (pt.astype(F32) * (dpt - delta)).astype(BF16)
            dv_acc[...] += jnp.dot(pt, dom, preferred_element_type=F32)
            dk_acc[:, sl] += jnp.dot(dst, q, preferred_element_type=F32)
            dq = lax.dot_general(dst, k, TN_DIMS, preferred_element_type=F32) * ATTN_SCALE
            dq_ref[:, sl] = _rope(dq, c_ref[...], s_ref[...], True).astype(BF16)

        @pl.when(i == nq - 1)
        def _():
            dk_ref[...] = (dk_acc[...] * ATTN_SCALE).astype(BF16)
            dv_ref[...] = dv_acc[...].astype(BF16)

        riding.finish(state)

    o_spec = pl.BlockSpec((ATT_TQ, LANES), lambda p, i: (i, p))
    tab = pl.BlockSpec((ATT_TQ, LANES), lambda p, i: (i, 0))
    outs = pl.pallas_call(
        body, name=name, grid=(N_HEADS // 2, nq),
        out_shape=(jax.ShapeDtypeStruct((s, N_HEADS * LANES), BF16),
                   jax.ShapeDtypeStruct((t, N_HEADS * LANES), BF16),
                   jax.ShapeDtypeStruct((t, N_HEADS * V_DIM), BF16), *riding.out_shape),
        in_specs=[pl.BlockSpec((ATT_TQ, 2 * LANES), lambda p, i: (i, p)),
                  pl.BlockSpec((t, 2 * LANES), lambda p, i: (0, p)),
                  pl.BlockSpec((t, LANES), lambda p, i: (0, N_HEADS + p)),
                  o_spec, o_spec,
                  pl.BlockSpec((8, ATT_TQ), lambda p, i: (p, i)), tab, tab, *riding.specs],
        out_specs=(pl.BlockSpec((ATT_TQ, 2 * LANES), lambda p, i: (i, p)),
                   pl.BlockSpec((t, 2 * LANES), lambda p, i: (0, p)),
                   pl.BlockSpec((t, LANES), lambda p, i: (0, p)), *riding.specs),
        scratch_shapes=[pltpu.VMEM((t, 2 * LANES), F32), pltpu.VMEM((t, LANES), F32), *riding.scratch],
        compiler_params=_params(("arbitrary", "arbitrary"), VMEM_BIG),
    )(qf, kv, kv, o, da, stats, cos, sgn, *riding.arrays)
    return outs[0], outs[1], outs[2], list(outs[3:])


def _silu(x):
    return x * (1.0 / (1.0 + jnp.exp(-x)))


def _adaln_fwd(a, w, b, name):
    def body(a_ref, w_ref, b_ref, o_ref):
        o_ref[...] = jnp.dot(_silu(a_ref[...]), w_ref[...], preferred_element_type=F32,
                             precision=lax.Precision.HIGHEST) + b_ref[...]

    return pl.pallas_call(
        body, name=name, out_shape=jax.ShapeDtypeStruct((a.shape[0], w.shape[1]), F32),
        compiler_params=_params(None, VMEM_BIG),
    )(a, w, b)


def _adaln_bwd(a_t, w, d_ex, d_ctx, d_all, name):
    def body(at_ref, w_ref, dex_ref, dctx_ref, dall_ref, gw_ref, dsil_ref, dsum_ref):
        sil_t = _silu(at_ref[...])
        dctx = dctx_ref[...]
        row = dctx[0:1, :]
        for j in range(1, N_DEV):
            row = row + dctx[j:j + 1, :]
        rowi = lax.broadcasted_iota(jnp.int32, dctx.shape, 0)
        ctx_rows = jnp.where(rowi == 0, jnp.broadcast_to(row, dctx.shape), 0.0)
        hi = lax.Precision.HIGHEST
        d_rows = jnp.concatenate([dex_ref[...], ctx_rows], axis=0)
        gw_ref[...] = jnp.dot(sil_t, d_rows, preferred_element_type=F32, precision=hi)
        dsil_ref[...] = lax.dot_general(ctx_rows, w_ref[...], NT_DIMS, preferred_element_type=F32, precision=hi)
        tot = dall_ref[0]
        for j in range(1, N_DEV):
            tot = tot + dall_ref[j]
        dsum_ref[...] = tot

    return pl.pallas_call(
        body, name=name,
        out_shape=(jax.ShapeDtypeStruct(w.shape, F32), jax.ShapeDtypeStruct((8, w.shape[0]), F32),
                   jax.ShapeDtypeStruct(d_all.shape[1:], F32)),
        compiler_params=_params(None, VMEM_BIG),
    )(a_t, w, d_ex, d_ctx, d_all)


def _pack_small(sums1, sums2, fsums, sums1c, psums, d_cw, name):
    d = D_MODEL

    def body(s1_ref, s2_ref, f_ref, s1c_ref, p_ref, cw_ref, o_ref):
        o_ref[...] = jnp.zeros_like(o_ref)
        for col, (ref, r) in enumerate([(s1_ref, 1), (s1_ref, 0), (s2_ref, 2), (s2_ref, 1), (s2_ref, 0), (f_ref, 1)]):
            o_ref[0:1, col * d:(col + 1) * d] = ref[r:r + 1, :]
        o_ref[1:2, 0:d] = s1c_ref[1:2, :]
        o_ref[1:2, d:2 * d] = s1c_ref[0:1, :]
        o_ref[2:3, 0:Q_RANK] = p_ref[0:1, :]
        o_ref[2:3, Q_RANK:Q_RANK + KV_RANK] = p_ref[1:2, 0:KV_RANK]
        o_ref[2:3, Q_RANK + KV_RANK:Q_RANK + KV_RANK + d] = f_ref[0:1, :]
        for r in range(3):
            o_ref[3 + r:4 + r, 0:CONV_W] = cw_ref[r:r + 1, :]
        o_ref[6:7, 0:d] = f_ref[3:4, :]

    return pl.pallas_call(body, name=name, out_shape=jax.ShapeDtypeStruct((8, 6 * d), F32))(
        sums1, sums2, fsums, sums1c, psums, d_cw)


def _adam_math(w, g, m, v):
    nm = ADAM_B1 * m + (1.0 - ADAM_B1) * g
    nv = ADAM_B2 * v + (1.0 - ADAM_B2) * (g * g)
    m_hat = nm / (1.0 - ADAM_B1 ** ADAM_STEP)
    v_hat = nv / (1.0 - ADAM_B2 ** ADAM_STEP)
    return -ADAM_LR * (m_hat / (jnp.sqrt(v_hat) + ADAM_EPS) + ADAM_WD * w), nm, nv


def _small_update(dsum, dsil_all, g_cw, params, name):
    d = D_MODEL
    n = len(params)

    def body(*refs):
        dsum_ref, dsil_ref, gcw_ref = refs[:3]
        wmv = refs[3:3 + 3 * n]
        outs = refs[3 + 3 * n:]
        tot = dsil_ref[0]
        for j in range(1, N_DEV):
            tot = tot + dsil_ref[j]
        cv = wmv[0][...]
        sg = 1.0 / (1.0 + jnp.exp(-cv))
        off = Q_RANK + KV_RANK
        grads = [tot[0:1, :] * (sg * (1.0 + cv * (1.0 - sg))),
                 dsum_ref[0:1, :] + dsum_ref[1:2, :],
                 dsum_ref[2:3, 0:Q_RANK], dsum_ref[2:3, Q_RANK:off], dsum_ref[2:3, off:off + d],
                 gcw_ref[...]]
        for p, g in enumerate(grads):
            w_ref, m_ref, v_ref = wmv[3 * p:3 * p + 3]
            at = 0 if len(w_ref.shape) == 3 else Ellipsis
            res = (g,) + _adam_math(w_ref[at], g, m_ref[at], v_ref[at])
            for q, val in enumerate(res):
                outs[4 * p + q][at] = val

    flat = [a for wmv in params for a in wmv]
    out_shape = tuple(jax.ShapeDtypeStruct(wmv[0].shape, F32) for wmv in params for _ in range(4))
    outs = pl.pallas_call(body, name=name, out_shape=out_shape)(dsum, dsil_all, g_cw, *flat)
    return [outs[4 * p:4 * p + 4] for p in range(n)]


def _adamw(w, g, m, v, name, slots=False):
    _, rows, cols = w.shape
    tr = _pick(rows, (256, 128, 64, 32, 16, 8))

    def body(w_ref, g_ref, m_ref, v_ref, *outs):
        if slots:
            gv = g_ref[0].astype(F32)
            for j in range(1, N_DEV):
                gv = gv + g_ref[j].astype(F32)
            outs[0][...] = gv
        else:
            gv = g_ref[...]
        d_ref, nm_ref, nv_ref = outs[-3:]
        d_ref[...], nm_ref[...], nv_ref[...] = _adam_math(w_ref[...], gv, m_ref[...], v_ref[...])

    blk = pl.BlockSpec((None, tr, cols), lambda i: (0, i, 0))
    g_spec = (pl.BlockSpec((N_DEV, tr, cols), lambda i: (0, i, 0)) if slots
              else pl.BlockSpec((tr, cols), lambda i: (i, 0)))
    sh = jax.ShapeDtypeStruct((1, rows, cols), F32)
    n_out = 4 if slots else 3
    return pl.pallas_call(
        body, name=name, grid=(rows // tr,), out_shape=(sh,) * n_out,
        in_specs=[blk, g_spec, blk, blk], out_specs=(blk,) * n_out,
        compiler_params=_params(("parallel",)),
    )(w, g, m, v)


def _rope_tables(s, l):
    tok = np.arange(s)
    row = (tok // GRID_W).astype(np.float32)
    col = (tok % GRID_W).astype(np.float32)
    half = QK_ROPE // 2
    freqs = np.float32(ROPE_THETA) ** (-np.arange(0, half, 2, dtype=np.float32) / np.float32(half))
    dd = np.arange(QK_ROPE)
    pos = np.where((dd // half)[None, :] == 0, row[:, None], col[:, None]).astype(np.float32)
    ang = (pos * freqs[dd % (half // 2)][None, :]).astype(np.float32)
    sin = np.sin(ang).astype(np.float32)
    cos_t = np.ones((s + l, LANES), np.float32)
    sgn_t = np.zeros((s + l, LANES), np.float32)
    cos_t[:s, QK_NOPE:QK_NOPE + QK_ROPE] = np.cos(ang)
    sgn_t[:s, QK_NOPE:QK_NOPE + QK_ROPE] = np.where(((dd % half) // (half // 2))[None, :] == 0, -sin, sin)
    return jnp.asarray(cos_t), jnp.asarray(sgn_t)


def _slots_to_cols(g):
    return g.transpose(1, 0, 2).reshape(g.shape[1], N_DEV * g.shape[2])


def _cols_to_slots(w):
    return w.reshape(w.shape[0], N_DEV, w.shape[1] // N_DEV).transpose(1, 0, 2)


def _unpack_small_weights(g_in, g_uq, g_ukv):
    w_in = _slots_to_cols(g_in)
    zeros = jnp.zeros((D_MODEL, QK_NOPE), BF16)
    win_head = jnp.concatenate([w_in[:, :Q_RANK + KV_RANK], zeros, w_in[:, Q_RANK + KV_RANK:MLA_IN],
                                zeros[:, :LANES - QK_NOPE - QK_ROPE]], axis=1)
    win_conv = w_in[:, MLA_IN:].reshape(D_MODEL, 3, CONV_W // LANES, LANES).transpose(0, 2, 1, 3)
    win_conv = win_conv.reshape(D_MODEL, 3 * CONV_W)
    w_uq = _slots_to_cols(g_uq).reshape(Q_RANK, N_HEADS, QK_NOPE + QK_ROPE)
    wq = jnp.pad(w_uq, ((0, 0), (0, 0), (0, LANES - QK_NOPE - QK_ROPE))).reshape(Q_RANK, N_HEADS * LANES)
    w_ukv = _slots_to_cols(g_ukv).reshape(KV_RANK, N_HEADS, QK_NOPE + V_DIM)
    k_top = jnp.pad(w_ukv[:, :, :QK_NOPE], ((0, 0), (0, 0), (0, LANES - QK_NOPE))).reshape(KV_RANK, N_HEADS * LANES)
    v_top = w_ukv[:, :, QK_NOPE:].reshape(KV_RANK, N_HEADS * V_DIM)
    eye = jnp.pad(jnp.eye(QK_ROPE, dtype=BF16), ((QK_NOPE, LANES - QK_NOPE - QK_ROPE),) * 2)
    wk = jnp.concatenate([
        jnp.concatenate([k_top, v_top], axis=1),
        jnp.concatenate([jnp.tile(eye, (1, N_HEADS)), jnp.zeros((LANES, N_HEADS * V_DIM), BF16)], axis=1)], axis=0)
    return win_head, win_conv, wq, wk


def _pack_small_grads(d_head, d_conv, d_wq, d_wkk, d_wkv):
    d_conv = d_conv.reshape(D_MODEL, CONV_W // LANES, 3, LANES).transpose(0, 2, 1, 3).reshape(D_MODEL, 3 * CONV_W)
    g_in = jnp.concatenate([d_head[:, :Q_RANK + KV_RANK],
                            d_head[:, Q_RANK + KV_RANK + QK_NOPE:Q_RANK + KV_RANK + QK_NOPE + QK_ROPE], d_conv], axis=1)
    g_uq = d_wq.reshape(Q_RANK, N_HEADS, LANES)[:, :, :QK_NOPE + QK_ROPE].reshape(Q_RANK, -1)
    g_kn = d_wkk[:KV_RANK].reshape(KV_RANK, N_HEADS, LANES)[:, :, :QK_NOPE]
    g_v = d_wkv[:KV_RANK].reshape(KV_RANK, N_HEADS, V_DIM)
    g_ukv = jnp.concatenate([g_kn, g_v], axis=2).reshape(KV_RANK, -1)
    return [_cols_to_slots(g).astype(BF16) for g in (g_in, g_uq, g_ukv)]


def kernel(x, c, ctx, c_ctx, w_mod, b_mod, w_in, q_norm_g, w_uq, kv_norm_g, w_ukv, conv_w, w_out, w_mlp1, w_mlp2, final_norm_g, loss_target, m_c_ctx, m_w_mod, m_b_mod, m_w_in, m_q_norm_g, m_w_uq, m_kv_norm_g, m_w_ukv, m_conv_w, m_w_out, m_w_mlp1, m_w_mlp2, m_final_norm_g, v_c_ctx, v_w_mod, v_b_mod, v_w_in, v_q_norm_g, v_w_uq, v_kv_norm_g, v_w_ukv, v_conv_w, v_w_out, v_w_mlp1, v_w_mlp2, v_final_norm_g):
    me = _my_index()
    x2d, ctx2d, tgt = x[0], ctx[0], loss_target[0]
    s, l = x2d.shape[0], ctx2d.shape[0]
    t = s + l
    d = D_MODEL
    mod_cols = w_mod.shape[2]
    cw_cols = conv_w.shape[2]

    early = [w.astype(BF16) for w in (w_in[0], w_uq[0], w_ukv[0])]
    late = [w.astype(BF16) for w in (w_out[0], w_mlp1[0], w_mlp2[0])]
    c_all, g_in, g_uq, g_ukv = _all_gather([jnp.pad(c, ((0, 7), (0, 0))), *early], "gather_c_weights", False)
    a_rows = jnp.concatenate([c_all[:, 0, :], c_ctx[None, :], jnp.zeros((7, d), F32)], axis=0)
    b_cols = lax.dynamic_slice(b_mod, (0, me * mod_cols), (1, mod_cols))
    mod_cols_all = _adaln_fwd(a_rows, w_mod[0], b_cols, "adaln_fwd")
    cw_blk = jnp.pad(conv_w[0], ((0, 5), (0, mod_cols - cw_cols)))
    (gathered,) = _all_gather([jnp.concatenate([mod_cols_all, cw_blk], axis=0)], "gather_mod", True)
    mod_mine = lax.dynamic_index_in_dim(gathered, me, axis=1, keepdims=False).reshape(1, 6 * d)
    mod_ctx = gathered[:, 8, :].reshape(1, 6 * d)
    cw_full = gathered[:, 16:19, :cw_cols].transpose(1, 0, 2).reshape(3, CONV_W)

    win_head, win_conv, wq, wk = _unpack_small_weights(g_in, g_uq, g_ukv)
    wk_k, wk_v = wk[:, :N_HEADS * LANES], wk[:, N_HEADS * LANES:]
    cos, sgn = _rope_tables(s, l)

    h_all = _modulate_all(x2d, ctx2d, mod_mine, mod_ctx, "modulate1")
    tm_t = _pick(t, (1088, 768, 256))
    tk_t = _pick(t, (2176, 768, 256))
    z_head = _matmul(h_all, win_head, mode="nn", name="in_proj_head", tm=tm_t, tn=512, tk=1024)
    z_conv = _matmul(h_all, win_conv, mode="nn", name="in_proj_conv", m=s, tm=1024, tn=1536, tk=1024)
    cq, kv_in = _qkv_prep(z_head, q_norm_g, kv_norm_g, cos, sgn, "qkv_prep")
    qf = _matmul(cq, wq, mode="nn", name="q_up", out_dtype=BF16, m=s, tm=1024, tn=1024, tk=256,
                 epilogue="rope", extra=(cos, sgn))
    kv = _matmul(kv_in, wk, mode="nn", name="kv_up", out_dtype=BF16, tm=tm_t, tn=1536, tk=256)
    attn, a_cat, stats, (g_out, w1, g_w2) = _attn_fwd(qf, kv, _Riding("gather", late), "attn_fwd")
    wo = g_out.reshape(d, d)
    w2 = g_w2.reshape(D_FF, d)
    a_cat = _conv_fwd(z_conv, cw_full, a_cat, "conv_fwd")
    (o, x1, h2), _ = _matmul_rows(a_cat, wo, _epi_resid_modulate, mode="nn", name="out_proj", tm=1024, tk=1024,
                                  rows=[x2d], vecs=[(mod_mine, 2), (mod_mine, 3), (mod_mine, 4)],
                                  out_dtypes=[F32, F32, BF16])
    u1, act = _matmul(h2, w1, mode="nn", name="mlp_up", tm=2048, tk=1024, epilogue="relu2", slots="b_cols")
    (dx2, dm, fsums), _ = _matmul_rows(act, w2, _epi_final, mode="nn", name="mlp_down", tm=1024, tk=1024,
                                       rows=[x1, tgt], vecs=[(mod_mine, 5), (final_norm_g[None, :], 0)],
                                       out_dtypes=[F32, BF16], sums=True)

    d_w2 = _matmul(act, dm, mode="tn", name="d_w_mlp2", out_dtype=BF16, tm=2048, tn=1024, tk=1024)
    du1 = _matmul(dm, w2, mode="nt", name="d_act", out_dtype=BF16, tm=2048, tn=1024, tk=1024,
                  epilogue="drelu2", extra=(u1,))
    d_w1 = _matmul(h2, du1, mode="tn", name="d_w_mlp1", out_dtype=BF16, tm=1024, tk=4096, slots="out")
    (dx1, do, sums2), _ = _matmul_rows(du1, w1, _epi_modulate2_bwd, mode="nt", name="d_h2", tm=1024, tk=1024,
                                       slots="b_contract", rows=[x1, dx2, o], vecs=[(mod_mine, 4), (mod_mine, 2)],
                                       out_dtypes=[F32, BF16], sums=True)
    d_wo = _matmul(a_cat, do, mode="tn", name="d_w_out", out_dtype=BF16, tm=1024, tn=1024, tk=2048)
    da = _matmul(do, wo, mode="nt", name="d_a", tm=1024, tn=1024, tk=1024)
    dz_conv, d_cw = _conv_bwd(z_conv, cw_full, da, "conv_bwd")
    ready = [d_wo.reshape(N_DEV, d // N_DEV, d), d_w1, d_w2.reshape(N_DEV, D_FF // N_DEV, d)]
    dq, dk, dv, rode = _attn_bwd(qf, kv, attn, da, stats, cos, sgn, _Riding("exchange", ready), "attn_bwd")
    d_wq = _matmul(cq, dq, mode="tn", name="d_w_uq", k=s, tm=256, tn=1024, tk=4096)
    dcq = _matmul(dq, wq, mode="nt", name="d_cq", tm=1024, tn=256, tk=1024)
    d_wkk = _matmul(kv_in, dk, mode="tn", name="d_w_ukv_k", tm=256, tn=1024, tk=tk_t)
    d_wkv = _matmul(kv_in, dv, mode="tn", name="d_w_ukv_v", tm=256, tn=512, tk=tk_t)
    dkv_in = _matmul(dk, wk_k, mode="nt", name="d_kv_in_k", tm=tm_t, tn=256, tk=1024)
    dkv_in = _matmul(dv, wk_v, mode="nt", name="d_kv_in_v", tm=tm_t, tn=256, tk=512, addend=dkv_in)
    dz_head, psums = _qkv_prep_bwd(z_head, dcq, dkv_in, q_norm_g, kv_norm_g, cos, sgn, s, "qkv_prep_bwd")
    d_head = _matmul(h_all, dz_head, mode="tn", name="d_w_in_head", tm=1024, tn=512, tk=tk_t)
    d_conv = _matmul(h_all, dz_conv, mode="tn", name="d_w_in_conv", k=s, tm=1024, tn=1536, tk=2048)
    dh_head = _matmul(dz_head, win_head, mode="nt", name="d_h1_head", tm=tm_t, tn=1024, tk=512)
    send = _pack_small_grads(d_head, d_conv, d_wq, d_wkk, d_wkv)
    (grad_x, sums1), got = _matmul_rows(dz_conv, win_conv, _epi_modulate1_bwd, mode="nt", name="d_h1", tm=1024,
                                        tk=win_conv.shape[1], rows=[dh_head, x2d, dx1], vecs=[(mod_mine, 1)],
                                        out_dtypes=[F32], sums=True, riding=_Riding("exchange", send))
    (sums1c,) = _modulate_bwd(dh_head, s // ROW_TILE, ctx2d, mod_ctx, 1, "modulate1_ctx_bwd")

    small = _pack_small(sums1, sums2, fsums, sums1c, psums, d_cw, "pack_small")
    (d_all,) = _all_gather([small], "gather_small_grads", True)
    d_cols = lax.dynamic_slice_in_dim(d_all, me * mod_cols, mod_cols, axis=2)
    g_w_mod, dsil, dsum = _adaln_bwd(a_rows.T, w_mod[0], d_cols[:, 0, :], d_cols[:, 1, :], d_all, "adaln_bwd")
    (dsil_all,) = _all_gather([dsil], "gather_d_cctx", True)
    loss = dsum[6, 0]
    g_cw = lax.dynamic_slice(dsum, (3, me * cw_cols), (3, cw_cols))

    slots = dict(zip(["w_in", "w_uq", "w_ukv"], got))
    slots.update(zip(["w_out", "w_mlp1", "w_mlp2"], rode))

    grads = {}
    weights = {"c_ctx": c_ctx, "w_mod": w_mod, "b_mod": b_mod, "w_in": w_in, "q_norm_g": q_norm_g, "w_uq": w_uq,
               "kv_norm_g": kv_norm_g, "w_ukv": w_ukv, "conv_w": conv_w, "w_out": w_out, "w_mlp1": w_mlp1,
               "w_mlp2": w_mlp2, "final_norm_g": final_norm_g}
    m_in = {"c_ctx": m_c_ctx, "w_mod": m_w_mod, "b_mod": m_b_mod, "w_in": m_w_in, "q_norm_g": m_q_norm_g,
            "w_uq": m_w_uq, "kv_norm_g": m_kv_norm_g, "w_ukv": m_w_ukv, "conv_w": m_conv_w, "w_out": m_w_out,
            "w_mlp1": m_w_mlp1, "w_mlp2": m_w_mlp2, "final_norm_g": m_final_norm_g}
    v_in = {"c_ctx": v_c_ctx, "w_mod": v_w_mod, "b_mod": v_b_mod, "w_in": v_w_in, "q_norm_g": v_q_norm_g,
            "w_uq": v_w_uq, "kv_norm_g": v_kv_norm_g, "w_ukv": v_w_ukv, "conv_w": v_conv_w, "w_out": v_w_out,
            "w_mlp1": v_w_mlp1, "w_mlp2": v_w_mlp2, "final_norm_g": v_final_norm_g}
    names = list(weights)
    small_names = ["c_ctx", "b_mod", "q_norm_g", "kv_norm_g", "final_norm_g", "conv_w"]
    delta, new_m, new_v = {}, {}, {}

    def as_rows(a):
        return a[None, :] if a.ndim == 1 else a

    small_out = _small_update(dsum, dsil_all, g_cw, [[as_rows(src[n]) for src in (weights, m_in, v_in)]
                                                      for n in small_names], "small_update")
    for n, outs in zip(small_names, small_out):
        grads[n], delta[n], new_m[n], new_v[n] = [a.reshape(weights[n].shape) for a in outs]
    for n in names:
        if n in small_names:
            continue
        if n in slots:
            grads[n], delta[n], new_m[n], new_v[n] = _adamw(weights[n], slots[n], m_in[n], v_in[n], "adamw_" + n,
                                                            slots=True)
        else:
            delta[n], new_m[n], new_v[n] = _adamw(weights[n], g_w_mod, m_in[n], v_in[n], "adamw_" + n)
            grads[n] = g_w_mod[None]

    return (loss, grad_x[None], *[grads[n] for n in names], *[delta[n] for n in names],
            *[new_m[n] for n in names], *[new_v[n] for n in names])
```

```python
import math

import jax
import jax.numpy as jnp
import numpy as np
from jax import lax
from jax.experimental import pallas as pl
from jax.experimental.pallas import tpu as pltpu

F32 = jnp.float32
BF16 = jnp.bfloat16

D_MODEL = 1024
GRID_W = 64
N_HEADS = 8
QK_NOPE = 64
QK_ROPE = 32
V_DIM = 64
Q_RANK = 256
KV_RANK = 128
MLA_IN = Q_RANK + KV_RANK + QK_ROPE
CONV_W = 512
HEAD_COLS = 512
D_FF = 4096
ROPE_THETA = 10000.0
EPS = 1e-6
ATTN_SCALE = 1.0 / math.sqrt(QK_NOPE + QK_ROPE)
LOG2_E = 1.0 / math.log(2.0)
EXP2_SCALE = ATTN_SCALE * LOG2_E
N_DEV = 8
LANES = 128

ADAM_LR, ADAM_B1, ADAM_B2, ADAM_EPS, ADAM_WD, ADAM_STEP = 0.001, 0.9, 0.999, 1e-08, 0.01, 10

ROW_TILE = 256
VMEM_BIG = 60 * 1024 * 1024


def _params(sem=None, vmem=None):
    return pltpu.CompilerParams(dimension_semantics=sem, vmem_limit_bytes=vmem)


def _pick(n, prefs):
    for p in prefs:
        if n % p == 0:
            return p
    return n


def _my_index():
    return 4 * lax.axis_index("x") + 2 * lax.axis_index("y") + lax.axis_index("c")


def _all_gather(arrays, name, in_vmem):
    space = pltpu.VMEM if in_vmem else pl.ANY
    n = len(arrays)

    def body(*refs):
        x_refs, out_refs = refs[:n], refs[n:2 * n]
        send_sems, recv_sems, local_sems = refs[2 * n:]
        x, y, c = lax.axis_index("x"), lax.axis_index("y"), lax.axis_index("c")
        me, sibling = (x, y, c), (x, y, 1 - c)
        chips = [(1 - x, y), (x, 1 - y), (1 - x, 1 - y)]

        def slot(a, px, py, pc):
            return out_refs[a].at[4 * px + 2 * py + pc]

        def copy(a, k, block, to, src=None):
            return pltpu.make_async_remote_copy(
                src_ref=slot(a, *block) if src is None else src, dst_ref=slot(a, *block),
                send_sem=send_sems.at[7 * a + k], recv_sem=recv_sems.at[7 * a + k],
                device_id=to, device_id_type=pl.DeviceIdType.MESH)

        mine = [pltpu.make_async_copy(x_refs[a], slot(a, *me), local_sems.at[a]) for a in range(n)]
        for cp in mine:
            cp.start()
        started = []
        for a in range(n):
            first = [copy(a, 0, me, sibling, src=x_refs[a])]
            first += [copy(a, 1 + j, me, (*chip, c), src=x_refs[a]) for j, chip in enumerate(chips)]
            for cp in first:
                cp.start()
            started += first
        for a in range(n):
            for j, chip in enumerate(chips):
                copy(a, 1 + j, (*chip, c), me).wait_recv()
                passed = copy(a, 4 + j, (*chip, c), sibling)
                passed.start()
                started.append(passed)
        for a in range(n):
            copy(a, 0, sibling, me).wait_recv()
            for j, chip in enumerate(chips):
                copy(a, 4 + j, (*chip, 1 - c), me).wait_recv()
        for cp in started:
            cp.wait_send()
        for cp in mine:
            cp.wait()

    outs = pl.pallas_call(
        body, name=name,
        out_shape=tuple(jax.ShapeDtypeStruct((N_DEV,) + a.shape, a.dtype) for a in arrays),
        in_specs=[pl.BlockSpec(memory_space=space)] * n,
        out_specs=tuple(pl.BlockSpec(memory_space=space) for _ in arrays),
        scratch_shapes=[pltpu.SemaphoreType.DMA((7 * n,)), pltpu.SemaphoreType.DMA((7 * n,)),
                        pltpu.SemaphoreType.DMA((n,))],
    )(*arrays)
    return list(outs)


class _Riding:
    def __init__(self, kind, arrays):
        self.kind, self.arrays, self.n = kind, list(arrays), len(arrays)
        lead = (N_DEV,) if kind == "gather" else ()
        self.out_shape = [jax.ShapeDtypeStruct(lead + a.shape, a.dtype) for a in self.arrays]
        self.specs = [pl.BlockSpec(memory_space=pl.ANY)] * self.n
        self.scratch = [pltpu.SemaphoreType.DMA((7 * self.n,)), pltpu.SemaphoreType.DMA((7 * self.n,)),
                        pltpu.SemaphoreType.DMA((self.n,))]

    def copies(self, x_refs, y_refs, send_sems, recv_sems, local_sems):
        x, y, c = lax.axis_index("x"), lax.axis_index("y"), lax.axis_index("c")
        me = 4 * x + 2 * y + c
        local, sends, landings = [], [], []
        for a in range(self.n):
            src_mine = x_refs[a] if self.kind == "gather" else x_refs[a].at[me]
            local.append(pltpu.make_async_copy(src_mine, y_refs[a].at[me], local_sems.at[a]))
            for k in range(1, N_DEV):
                peer = (1 - x if k & 4 else x, 1 - y if k & 2 else y, 1 - c if k & 1 else c)
                pid = 4 * peer[0] + 2 * peer[1] + peer[2]
                src = x_refs[a] if self.kind == "gather" else x_refs[a].at[pid]
                for dst, out in ((me, sends), (pid, landings)):
                    out.append(pltpu.make_async_remote_copy(
                        src_ref=src, dst_ref=y_refs[a].at[dst],
                        send_sem=send_sems.at[7 * a + k - 1], recv_sem=recv_sems.at[7 * a + k - 1],
                        device_id=peer, device_id_type=pl.DeviceIdType.MESH))
        return local, sends, landings

    def run(self, first, last, x_refs, y_refs, sems):
        if self.n == 0:
            return None
        local, sends, landings = self.copies(x_refs, y_refs, *sems)

        @pl.when(first)
        def _():
            for cp in local + sends:
                cp.start()

        return local, sends, landings, last

    @staticmethod
    def finish(state):
        if state is None:
            return
        local, sends, landings, last = state

        @pl.when(last)
        def _():
            for cp in landings:
                cp.wait_recv()
            for cp in sends:
                cp.wait_send()
            for cp in local:
                cp.wait()


_DIMS = {"nn": (((1,), (0,)), ((), ())), "nt": (((1,), (1,)), ((), ())), "tn": (((0,), (0,)), ((), ()))}
NT_DIMS = _DIMS["nt"]
TN_DIMS = _DIMS["tn"]


def _swap8(x):
    lane = lax.broadcasted_iota(jnp.int32, x.shape, 1)
    return jnp.where((lane & 15) < 8, pltpu.roll(x, LANES - 8, 1), pltpu.roll(x, 8, 1))


def _rope(x, cos, sgn, bwd):
    return x * cos + (_swap8(x * sgn) if bwd else _swap8(x) * sgn)


def _matmul(a, b, *, mode, name, out_dtype=F32, tm=512, tn=512, tk=512, m=None, k=None,
            epilogue=None, extra=(), addend=None, slots=None):
    if mode == "nn":
        m = a.shape[0] if m is None else m
        k = a.shape[1]
        n = N_DEV * b.shape[2] if slots == "b_cols" else b.shape[1]
    elif mode == "nt":
        m = a.shape[0] if m is None else m
        k = a.shape[1]
        n = b.shape[1] if slots == "b_contract" else b.shape[0]
    else:
        k = a.shape[0] if k is None else k
        m, n = a.shape[1], b.shape[1]
    tm, tn, tk = min(tm, m), min(tn, n), min(tk, k)
    group = 1
    if slots == "b_cols":
        tn = b.shape[2]
    if slots == "b_contract":
        group = max(1, tk // b.shape[2])
        tk = group * b.shape[2]
    if slots == "out":
        tn = n // N_DEV
    assert m % tm == 0 and n % tn == 0 and k % tk == 0, (name, m, n, k, tm, tn, tk)
    nk = k // tk
    dims = _DIMS[mode]
    a_spec = (pl.BlockSpec((tk, tm), lambda i, j, kk: (kk, i)) if mode == "tn"
              else pl.BlockSpec((tm, tk), lambda i, j, kk: (i, kk)))
    if slots == "b_cols":
        b_spec = pl.BlockSpec((None, tk, tn), lambda i, j, kk: (j, kk, 0))
    elif slots == "b_contract":
        b_spec = pl.BlockSpec((group, tn, tk // group), lambda i, j, kk: (kk, j, 0))
    elif mode == "nt":
        b_spec = pl.BlockSpec((tn, tk), lambda i, j, kk: (j, kk))
    else:
        b_spec = pl.BlockSpec((tk, tn), lambda i, j, kk: (kk, j))
    tile = pl.BlockSpec((tm, tn), lambda i, j, kk: (i, j))
    if slots == "out":
        o_spec = pl.BlockSpec((None, tm, tn), lambda i, j, kk: (j, i, 0))
        o_shape = (N_DEV, m, tn)
    else:
        o_spec, o_shape = tile, (m, n)
    in_specs, args = [a_spec, b_spec], [a, b]
    if epilogue == "drelu2":
        in_specs.append(tile)
    elif epilogue == "rope":
        in_specs += [pl.BlockSpec((tm, LANES), lambda i, j, kk: (i, 0))] * 2
    args += list(extra)
    if addend is not None:
        in_specs.append(tile)
        args.append(addend)
    if epilogue == "relu2":
        out_shape = (jax.ShapeDtypeStruct(o_shape, BF16), jax.ShapeDtypeStruct(o_shape, BF16))
        out_specs = (o_spec, o_spec)
    else:
        out_shape = jax.ShapeDtypeStruct(o_shape, out_dtype)
        out_specs = o_spec
    n_in = len(args)
    n_out = 2 if epilogue == "relu2" else 1

    def body(*refs):
        a_ref, b_ref = refs[0], refs[1]
        outs = refs[n_in:n_in + n_out]
        if slots == "b_contract":
            c = tk // group
            part = lax.dot_general(a_ref[:, 0:c], b_ref[0], dims, preferred_element_type=F32)
            for u in range(1, group):
                part = part + lax.dot_general(a_ref[:, u * c:(u + 1) * c], b_ref[u], dims, preferred_element_type=F32)
        else:
            part = lax.dot_general(a_ref[...], b_ref[...], dims, preferred_element_type=F32)

        def finish(acc):
            if addend is not None:
                acc = acc + refs[n_in - 1][...]
            if epilogue == "relu2":
                outs[0][...] = acc.astype(BF16)
                r = jnp.maximum(acc, 0.0)
                outs[1][...] = (r * r).astype(BF16)
            elif epilogue == "drelu2":
                u = refs[2][...].astype(F32)
                outs[0][...] = (acc * (2.0 * jnp.maximum(u, 0.0))).astype(out_dtype)
            elif epilogue == "rope":
                cos, sgn = refs[2][...], refs[3][...]
                for h in range(tn // LANES):
                    sl = slice(h * LANES, (h + 1) * LANES)
                    outs[0][:, sl] = _rope(acc[:, sl], cos, sgn, False).astype(out_dtype)
            else:
                outs[0][...] = acc.astype(out_dtype)

        if nk == 1:
            finish(part)
        else:
            acc_ref = refs[n_in + n_out]
            kk = pl.program_id(2)

            @pl.when(kk == 0)
            def _():
                acc_ref[...] = part

            @pl.when(kk > 0)
            def _():
                acc_ref[...] += part

            @pl.when(kk == nk - 1)
            def _():
                finish(acc_ref[...])

    return pl.pallas_call(
        body, name=name, grid=(m // tm, n // tn, nk),
        out_shape=out_shape, in_specs=in_specs, out_specs=out_specs,
        scratch_shapes=[pltpu.VMEM((tm, tn), F32)] if nk > 1 else [],
        compiler_params=_params(("parallel", "parallel", "arbitrary"), VMEM_BIG),
    )(*args)


def _rstd(x):
    return lax.rsqrt(jnp.mean(x * x, axis=1, keepdims=True) + EPS)


def _norm_bwd(dxn, xn, r):
    return r * (dxn - xn * jnp.mean(dxn * xn, axis=1, keepdims=True))


def _vec(col):
    return pl.BlockSpec((1, D_MODEL), lambda i: (0, col))


def _matmul_rows(a, b, epi, *, mode, name, tm, tk, rows=(), vecs=(), out_dtypes=(), sums=False, slots=None,
                 riding=None):
    m, k = a.shape
    n = D_MODEL
    tm, tk = min(tm, m), min(tk, k)
    riding = riding or _Riding("gather", [])
    group = 1
    if slots == "b_contract":
        group = max(1, tk // b.shape[2])
        tk = group * b.shape[2]
        b_spec = pl.BlockSpec((group, n, tk // group), lambda i, kk: (kk, 0, 0))
    elif mode == "nt":
        b_spec = pl.BlockSpec((n, tk), lambda i, kk: (0, kk))
    else:
        b_spec = pl.BlockSpec((tk, n), lambda i, kk: (kk, 0))
    assert m % tm == 0 and k % tk == 0, (name, m, k, tm, tk)
    ni, nk = m // tm, k // tk
    dims = _DIMS[mode]
    tile = pl.BlockSpec((tm, n), lambda i, kk: (i, 0))
    in_specs = [pl.BlockSpec((tm, tk), lambda i, kk: (i, kk)), b_spec] + [tile] * len(rows)
    in_specs += [pl.BlockSpec((1, n), lambda i, kk, col=col: (0, col)) for _, col in vecs]
    args = [a, b, *rows, *[v for v, _ in vecs]]
    out_shape = [jax.ShapeDtypeStruct((m, n), dt) for dt in out_dtypes]
    out_specs = [tile] * len(out_dtypes)
    if sums:
        out_shape.append(jax.ShapeDtypeStruct((8, n), F32))
        out_specs.append(pl.BlockSpec((8, n), lambda i, kk: (0, 0)))
    n_rows, n_vecs, n_outs, nr = len(rows), len(vecs), len(out_dtypes), riding.n
    n_in = 2 + n_rows + n_vecs

    def body(*refs):
        a_ref, b_ref = refs[0], refs[1]
        row_refs = refs[2:2 + n_rows]
        vec_refs = refs[2 + n_rows:n_in]
        x_refs = refs[n_in:n_in + nr]
        out_refs = refs[n_in + nr:n_in + nr + n_outs]
        pos = n_in + nr + n_outs
        sums_ref = refs[pos] if sums else None
        pos += 1 if sums else 0
        y_refs = refs[pos:pos + nr]
        pos += nr
        acc_ref = refs[pos] if nk > 1 else None
        sem_refs = refs[pos + (1 if nk > 1 else 0):]
        i, kk = pl.program_id(0), pl.program_id(1)
        state = riding.run((i == 0) & (kk == 0), (i == ni - 1) & (kk == nk - 1), x_refs, y_refs, sem_refs)
        if slots == "b_contract":
            c = tk // group
            part = lax.dot_general(a_ref[:, 0:c], b_ref[0], dims, preferred_element_type=F32)
            for u in range(1, group):
                part = part + lax.dot_general(a_ref[:, u * c:(u + 1) * c], b_ref[u], dims, preferred_element_type=F32)
        else:
            part = lax.dot_general(a_ref[...], b_ref[...], dims, preferred_element_type=F32)

        def finish(acc):
            nsub = tm // ROW_TILE
            for r in range(nsub):
                blk = pl.ds(r * ROW_TILE, ROW_TILE)
                epi(acc[r * ROW_TILE:(r + 1) * ROW_TILE], [ref.at[blk] for ref in row_refs], vec_refs,
                    [ref.at[blk] for ref in out_refs], sums_ref,
                    (i == 0) if r == 0 else None, (i == ni - 1) if r == nsub - 1 else None)

        if nk == 1:
            finish(part)
        else:
            @pl.when(kk == 0)
            def _():
                acc_ref[...] = part

            @pl.when(kk > 0)
            def _():
                acc_ref[...] += part

            @pl.when(kk == nk - 1)
            def _():
                finish(acc_ref)

        riding.finish(state)

    outs = pl.pallas_call(
        body, name=name, grid=(ni, nk),
        out_shape=(*out_shape, *riding.out_shape),
        in_specs=[*in_specs, *riding.specs], out_specs=(*out_specs, *riding.specs),
        scratch_shapes=([pltpu.VMEM((tm, n), F32)] if nk > 1 else []) + (riding.scratch if nr else []),
        compiler_params=_params(("arbitrary", "arbitrary"), VMEM_BIG),
    )(*args, *riding.arrays)
    n_own = len(out_shape)
    return list(outs[:n_own]), list(outs[n_own:])


def _zero_sums_at_start(sums_ref, first):
    if first is not None:
        @pl.when(first)
        def _():
            sums_ref[...] = jnp.zeros_like(sums_ref)


def _epi_resid_modulate(acc, rows, vecs, outs, sums_ref, first, last):
    (x_ref,), (g_ref, sh_ref, sc_ref) = rows, vecs
    x1 = x_ref[...] + g_ref[...] * acc
    outs[0][...] = acc
    outs[1][...] = x1
    outs[2][...] = (x1 * _rstd(x1) * (1.0 + sc_ref[...]) + sh_ref[...]).astype(BF16)


def _epi_final(acc, rows, vecs, outs, sums_ref, first, last):
    (x1_ref, t_ref), (g_ref, gf_ref) = rows, vecs
    d = acc.shape[1]
    x2 = x1_ref[...] + g_ref[...] * acc
    r = _rstd(x2)
    xn = x2 * r
    err = xn * gf_ref[...] - t_ref[...]
    dy = err * (1.0 / d)
    dx2 = _norm_bwd(dy * gf_ref[...], xn, r)
    outs[0][...] = dx2
    outs[1][...] = (dx2 * g_ref[...]).astype(BF16)
    _zero_sums_at_start(sums_ref, first)
    sums_ref[0:1, :] += jnp.sum(dy * xn, axis=0, keepdims=True)
    sums_ref[1:2, :] += jnp.sum(dx2 * acc, axis=0, keepdims=True)
    sums_ref[2:3, :] += jnp.sum(err * err, axis=0, keepdims=True)

    if last is not None:
        @pl.when(last)
        def _():
            tot = jnp.sum(sums_ref[2:3, :], axis=1, keepdims=True) * (0.5 / d)
            sums_ref[3:4, :] = jnp.broadcast_to(tot, (1, d))


def _epi_modulate2_bwd(acc, rows, vecs, outs, sums_ref, first, last):
    (x_ref, dres_ref, o_ref), (sc_ref, g_ref) = rows, vecs
    x = x_ref[...]
    r = _rstd(x)
    xn = x * r
    dx = dres_ref[...] + _norm_bwd(acc * (1.0 + sc_ref[...]), xn, r)
    outs[0][...] = dx
    outs[1][...] = (dx * g_ref[...]).astype(BF16)
    _zero_sums_at_start(sums_ref, first)
    sums_ref[0:1, :] += jnp.sum(acc * xn, axis=0, keepdims=True)
    sums_ref[1:2, :] += jnp.sum(acc, axis=0, keepdims=True)
    sums_ref[2:3, :] += jnp.sum(dx * o_ref[...], axis=0, keepdims=True)


def _epi_modulate1_bwd(acc, rows, vecs, outs, sums_ref, first, last):
    (add_ref, x_ref, dres_ref), (sc_ref,) = rows, vecs
    dh = acc + add_ref[...]
    x = x_ref[...]
    r = _rstd(x)
    xn = x * r
    outs[0][...] = dres_ref[...] + _norm_bwd(dh * (1.0 + sc_ref[...]), xn, r)
    _zero_sums_at_start(sums_ref, first)
    sums_ref[0:1, :] += jnp.sum(dh * xn, axis=0, keepdims=True)
    sums_ref[1:2, :] += jnp.sum(dh, axis=0, keepdims=True)


def _modulate_all(x, ctx, mod, mod_ctx, name):
    s, d = x.shape
    t = s + ctx.shape[0]
    ns = s // ROW_TILE
    nc = ctx.shape[0] // ROW_TILE

    def body(x_ref, c_ref, sh_ref, sc_ref, shc_ref, scc_ref, h_ref):
        i = pl.program_id(0)

        @pl.when(i < ns)
        def _():
            v = x_ref[...]
            h_ref[...] = (v * _rstd(v) * (1.0 + sc_ref[...]) + sh_ref[...]).astype(BF16)

        @pl.when(i >= ns)
        def _():
            v = c_ref[...]
            h_ref[...] = (v * _rstd(v) * (1.0 + scc_ref[...]) + shc_ref[...]).astype(BF16)

    return pl.pallas_call(
        body, name=name, grid=(ns + nc,),
        out_shape=jax.ShapeDtypeStruct((t, d), BF16),
        in_specs=[pl.BlockSpec((ROW_TILE, d), lambda i: (jnp.minimum(i, ns - 1), 0)),
                  pl.BlockSpec((ROW_TILE, d), lambda i: (jnp.maximum(i - ns, 0), 0)),
                  _vec(0), _vec(1), _vec(0), _vec(1)],
        out_specs=pl.BlockSpec((ROW_TILE, d), lambda i: (i, 0)),
        compiler_params=_params(("arbitrary",)),
    )(x, ctx, mod, mod, mod_ctx, mod_ctx)


def _modulate_bwd(dh, row_off, xsrc, mod, scale_col, name, dres=None, o=None):
    s, d = xsrc.shape
    n = s // ROW_TILE
    has_dx, has_o = dres is not None, o is not None
    assert has_dx or not has_o

    def body(*refs):
        it = iter(refs)
        dh_ref, x_ref, sc_ref = next(it), next(it), next(it)
        dres_ref = next(it) if has_dx else None
        o_ref, g_ref = (next(it), next(it)) if has_o else (None, None)
        dx_ref = next(it) if has_dx else None
        do_ref = next(it) if has_o else None
        sums_ref = next(it)
        i = pl.program_id(0)
        x = x_ref[...]
        r = _rstd(x)
        xn = x * r
        dhv = dh_ref[...]

        @pl.when(i == 0)
        def _():
            sums_ref[...] = jnp.zeros_like(sums_ref)

        sums_ref[0:1, :] += jnp.sum(dhv * xn, axis=0, keepdims=True)
        sums_ref[1:2, :] += jnp.sum(dhv, axis=0, keepdims=True)
        if has_dx:
            dx = dres_ref[...] + _norm_bwd(dhv * (1.0 + sc_ref[...]), xn, r)
            dx_ref[...] = dx
            if has_o:
                do_ref[...] = (dx * g_ref[...]).astype(BF16)
                sums_ref[2:3, :] += jnp.sum(dx * o_ref[...], axis=0, keepdims=True)

    row = pl.BlockSpec((ROW_TILE, d), lambda i: (i, 0))
    in_specs = [pl.BlockSpec((ROW_TILE, d), lambda i: (i + row_off, 0)), row, _vec(scale_col)]
    args = [dh, xsrc, mod]
    out_shape, out_specs = [], []
    if has_dx:
        in_specs.append(row)
        args.append(dres)
        out_shape.append(jax.ShapeDtypeStruct((s, d), F32))
        out_specs.append(row)
    if has_o:
        in_specs += [row, _vec(2)]
        args += [o, mod]
        out_shape.append(jax.ShapeDtypeStruct((s, d), BF16))
        out_specs.append(row)
    out_shape.append(jax.ShapeDtypeStruct((8, d), F32))
    out_specs.append(pl.BlockSpec((8, d), lambda i: (0, 0)))
    return pl.pallas_call(
        body, name=name, grid=(n,),
        out_shape=tuple(out_shape), in_specs=in_specs, out_specs=tuple(out_specs),
        compiler_params=_params(("arbitrary",)),
    )(*args)


def _qkv_prep(z, q_gain, kv_gain, cos, sgn, name):
    t = z.shape[0]

    def body(z_ref, qg_ref, kg_ref, c_ref, s_ref, cq_ref, kv_ref):
        zq = z_ref[:, 0:Q_RANK]
        cq_ref[...] = (zq * _rstd(zq) * qg_ref[...]).astype(BF16)
        zk = z_ref[:, Q_RANK:Q_RANK + KV_RANK]
        kv_ref[:, 0:KV_RANK] = (zk * _rstd(zk) * kg_ref[...]).astype(BF16)
        kr = z_ref[:, Q_RANK + KV_RANK:HEAD_COLS]
        kv_ref[:, KV_RANK:KV_RANK + LANES] = _rope(kr, c_ref[...], s_ref[...], False).astype(BF16)

    tab = pl.BlockSpec((ROW_TILE, LANES), lambda i: (i, 0))
    return pl.pallas_call(
        body, name=name, grid=(t // ROW_TILE,),
        out_shape=(jax.ShapeDtypeStruct((t, Q_RANK), BF16), jax.ShapeDtypeStruct((t, KV_RANK + LANES), BF16)),
        in_specs=[pl.BlockSpec((ROW_TILE, HEAD_COLS), lambda i: (i, 0)),
                  pl.BlockSpec((1, Q_RANK), lambda i: (0, 0)), pl.BlockSpec((1, KV_RANK), lambda i: (0, 0)), tab, tab],
        out_specs=(pl.BlockSpec((ROW_TILE, Q_RANK), lambda i: (i, 0)),
                   pl.BlockSpec((ROW_TILE, KV_RANK + LANES), lambda i: (i, 0))),
        compiler_params=_params(("parallel",)),
    )(z, q_gain, kv_gain, cos, sgn)


def _qkv_prep_bwd(z, dcq, dkv, q_gain, kv_gain, cos, sgn, s, name):
    t = z.shape[0]
    ns = s // ROW_TILE

    def body(z_ref, dcq_ref, dkv_ref, qg_ref, kg_ref, c_ref, s_ref, dz_ref, sums_ref):
        i = pl.program_id(0)

        @pl.when(i == 0)
        def _():
            sums_ref[...] = jnp.zeros_like(sums_ref)

        @pl.when(i < ns)
        def _():
            zq = z_ref[:, 0:Q_RANK]
            r = _rstd(zq)
            zn = zq * r
            dc = dcq_ref[...]
            sums_ref[0:1, :] += jnp.sum(dc * zn, axis=0, keepdims=True)
            dz_ref[:, 0:Q_RANK] = _norm_bwd(dc * qg_ref[...], zn, r).astype(BF16)

        @pl.when(i >= ns)
        def _():
            dz_ref[:, 0:Q_RANK] = jnp.zeros((ROW_TILE, Q_RANK), BF16)

        zk = z_ref[:, Q_RANK:Q_RANK + KV_RANK]
        r = _rstd(zk)
        zn = zk * r
        dc = dkv_ref[:, 0:KV_RANK]
        sums_ref[1:2, 0:KV_RANK] += jnp.sum(dc * zn, axis=0, keepdims=True)
        dz_ref[:, Q_RANK:Q_RANK + KV_RANK] = _norm_bwd(dc * kg_ref[...], zn, r).astype(BF16)
        dkr = dkv_ref[:, KV_RANK:KV_RANK + LANES]
        dz_ref[:, Q_RANK + KV_RANK:HEAD_COLS] = _rope(dkr, c_ref[...], s_ref[...], True).astype(BF16)

    tab = pl.BlockSpec((ROW_TILE, LANES), lambda i: (i, 0))
    return pl.pallas_call(
        body, name=name, grid=(t // ROW_TILE,),
        out_shape=(jax.ShapeDtypeStruct((t, HEAD_COLS), BF16), jax.ShapeDtypeStruct((8, Q_RANK), F32)),
        in_specs=[pl.BlockSpec((ROW_TILE, HEAD_COLS), lambda i: (i, 0)),
                  pl.BlockSpec((ROW_TILE, Q_RANK), lambda i: (jnp.minimum(i, ns - 1), 0)),
                  pl.BlockSpec((ROW_TILE, KV_RANK + LANES), lambda i: (i, 0)),
                  pl.BlockSpec((1, Q_RANK), lambda i: (0, 0)), pl.BlockSpec((1, KV_RANK), lambda i: (0, 0)), tab, tab],
        out_specs=(pl.BlockSpec((ROW_TILE, HEAD_COLS), lambda i: (i, 0)), pl.BlockSpec((8, Q_RANK), lambda i: (0, 0))),
        compiler_params=_params(("arbitrary",)),
    )(z, dcq, dkv, q_gain, kv_gain, cos, sgn)


def _shift_rows(u, s):
    rowi = lax.broadcasted_iota(jnp.int32, u.shape, 0)
    prev = jnp.where(rowi == 0, 0.0, pltpu.roll(u, 1, 0))
    nxt = jnp.where(rowi == s - 1, 0.0, pltpu.roll(u, s - 1, 0))
    return prev, nxt


def _conv_fwd(z_conv, cw, a_cat, name):
    s = z_conv.shape[0]

    def body(z_ref, w_ref, a_in_ref, o_ref):
        del a_in_ref
        gb, gc, xv = z_ref[:, 0:LANES], z_ref[:, LANES:2 * LANES], z_ref[:, 2 * LANES:3 * LANES]
        u = gc * xv
        prev, nxt = _shift_rows(u, s)
        y = w_ref[0:1, :] * prev + w_ref[1:2, :] * u + w_ref[2:3, :] * nxt
        o_ref[...] = (gb * y).astype(BF16)

    return pl.pallas_call(
        body, name=name, grid=(CONV_W // LANES,),
        out_shape=jax.ShapeDtypeStruct(a_cat.shape, a_cat.dtype),
        in_specs=[pl.BlockSpec((s, 3 * LANES), lambda j: (0, j)), pl.BlockSpec((3, LANES), lambda j: (0, j)),
                  pl.BlockSpec(memory_space=pl.ANY)],
        out_specs=pl.BlockSpec((s, LANES), lambda j: (0, 4 + j)),
        input_output_aliases={2: 0},
        compiler_params=_params(("parallel",), VMEM_BIG),
    )(z_conv, cw, a_cat)


def _conv_bwd(z_conv, cw, da, name):
    s = z_conv.shape[0]

    def body(z_ref, w_ref, da_ref, dz_ref, dw_ref):
        gb, gc, xv = z_ref[:, 0:LANES], z_ref[:, LANES:2 * LANES], z_ref[:, 2 * LANES:3 * LANES]
        u = gc * xv
        prev, nxt = _shift_rows(u, s)
        dcv = da_ref[...]
        dz_ref[:, 0:LANES] = (dcv * (w_ref[0:1, :] * prev + w_ref[1:2, :] * u + w_ref[2:3, :] * nxt)).astype(BF16)
        dy = dcv * gb
        dw_ref[0:1, :] = jnp.sum(dy * prev, axis=0, keepdims=True)
        dw_ref[1:2, :] = jnp.sum(dy * u, axis=0, keepdims=True)
        dw_ref[2:3, :] = jnp.sum(dy * nxt, axis=0, keepdims=True)
        dyp, dyn = _shift_rows(dy, s)
        du = w_ref[0:1, :] * dyn + w_ref[1:2, :] * dy + w_ref[2:3, :] * dyp
        dz_ref[:, LANES:2 * LANES] = (du * xv).astype(BF16)
        dz_ref[:, 2 * LANES:3 * LANES] = (du * gc).astype(BF16)

    blk = pl.BlockSpec((s, 3 * LANES), lambda j: (0, j))
    cws = pl.BlockSpec((3, LANES), lambda j: (0, j))
    return pl.pallas_call(
        body, name=name, grid=(CONV_W // LANES,),
        out_shape=(jax.ShapeDtypeStruct(z_conv.shape, BF16), jax.ShapeDtypeStruct((3, CONV_W), F32)),
        in_specs=[blk, cws, pl.BlockSpec((s, LANES), lambda j: (0, 4 + j))], out_specs=(blk, cws),
        compiler_params=_params(("parallel",), VMEM_BIG),
    )(z_conv, cw, da)


ATT_TQ = 256
ATT_Q_STEP = 1024
ATT_TQ_BWD = 512


def _head_mask(shape, hh):
    lane = lax.broadcasted_iota(jnp.int32, shape, 1)
    return (lane >= hh * V_DIM) & (lane < (hh + 1) * V_DIM)


def _attn_fwd(qf, kv, riding, name):
    s, t = qf.shape[0], kv.shape[0]
    step = min(ATT_Q_STEP, s)
    nq = s // step
    nr = riding.n

    def body(*refs):
        q_ref, k_ref, v_ref = refs[:3]
        o_ref, ob_ref, st_ref = refs[3 + nr:6 + nr]
        p, i = pl.program_id(0), pl.program_id(1)
        state = riding.run((p == 0) & (i == 0), (p == N_HEADS // 2 - 1) & (i == nq - 1),
                           refs[3:3 + nr], refs[6 + nr:6 + 2 * nr], refs[6 + 2 * nr:])
        v = v_ref[...]
        vlane = lax.broadcasted_iota(jnp.int32, v.shape, 1)
        one_lane = [(1 - hh) * V_DIM for hh in range(2)]
        vm = [jnp.where(_head_mask(v.shape, hh), v, jnp.where(vlane == one_lane[hh], 1.0, 0.0).astype(BF16))
              for hh in range(2)]

        def block(r, carry):
            rows = pl.ds(pl.multiple_of(r * ATT_TQ, ATT_TQ), ATT_TQ)
            olane = lax.broadcasted_iota(jnp.int32, (ATT_TQ, LANES), 1)
            acc = jnp.zeros((ATT_TQ, LANES), F32)
            stat = jnp.zeros((ATT_TQ, LANES), F32)
            for hh in range(2):
                sl = slice(hh * LANES, (hh + 1) * LANES)
                sc = lax.dot_general(q_ref[rows, sl], k_ref[:, sl], NT_DIMS, preferred_element_type=F32)
                mx = jnp.max(sc, axis=1, keepdims=True)
                e = jnp.exp2((sc - mx) * EXP2_SCALE).astype(BF16)
                res = jnp.dot(e, vm[hh], preferred_element_type=F32)
                den = jnp.sum(jnp.where(olane == one_lane[hh], res, 0.0), axis=1, keepdims=True)
                acc = acc + jnp.where(_head_mask(res.shape, hh), res * (1.0 / den), 0.0)
                stat = stat + jnp.where(olane == hh, mx * EXP2_SCALE + jnp.log(den) * LOG2_E, 0.0)
            o_ref[rows, :] = acc
            ob_ref[rows, :] = acc.astype(BF16)
            st_ref[:, rows] = stat.T[0:8, :]
            return carry

        lax.fori_loop(0, step // ATT_TQ, block, 0)
        riding.finish(state)

    o_spec = pl.BlockSpec((step, LANES), lambda p, i: (i, p))
    outs = pl.pallas_call(
        body, name=name, grid=(N_HEADS // 2, nq),
        out_shape=(jax.ShapeDtypeStruct((s, N_HEADS * V_DIM), F32),
                   jax.ShapeDtypeStruct((s, D_MODEL), BF16),
                   jax.ShapeDtypeStruct((N_HEADS // 2 * 8, s), F32), *riding.out_shape),
        in_specs=[pl.BlockSpec((step, 2 * LANES), lambda p, i: (i, p)),
                  pl.BlockSpec((t, 2 * LANES), lambda p, i: (0, p)),
                  pl.BlockSpec((t, LANES), lambda p, i: (0, N_HEADS + p)), *riding.specs],
        out_specs=(o_spec, o_spec, pl.BlockSpec((8, step), lambda p, i: (p, i)), *riding.specs),
        scratch_shapes=riding.scratch,
        compiler_params=_params(("arbitrary", "arbitrary"), VMEM_BIG),
    )(qf, kv, kv, *riding.arrays)
    return outs[0], outs[1], outs[2], list(outs[3:])


def _attn_bwd(qf, kv, o, da, stats, cos, sgn, riding, name):
    s, t = qf.shape[0], kv.shape[0]
    ATT_TQ = ATT_TQ_BWD
    nq = s // ATT_TQ
    nr = riding.n

    def body(*refs):
        q_ref, k_ref, v_ref, o_ref, do_ref, st_ref, c_ref, s_ref = refs[:8]
        dq_ref, dk_ref, dv_ref = refs[8 + nr:11 + nr]
        dk_acc, dv_acc = refs[11 + 2 * nr:13 + 2 * nr]
        p, i = pl.program_id(0), pl.program_id(1)
        state = riding.run((p == 0) & (i == 0), (p == N_HEADS // 2 - 1) & (i == nq - 1),
                           refs[8:8 + nr], refs[11 + nr:11 + 2 * nr], refs[13 + 2 * nr:])

        @pl.when(i == 0)
        def _():
            dk_acc[...] = jnp.zeros_like(dk_acc)
            dv_acc[...] = jnp.zeros_like(dv_acc)

        v = v_ref[...]
        do = do_ref[...]
        od = do * o_ref[...]
        ones = jnp.ones((8, LANES), F32)
        for hh in range(2):
            sl = slice(hh * LANES, (hh + 1) * LANES)
            q, k = q_ref[:, sl], k_ref[:, sl]
            mask = _head_mask(do.shape, hh)
            dom = jnp.where(mask, do, 0.0).astype(BF16)
            delta = lax.dot_general(ones, jnp.where(mask, od, 0.0), NT_DIMS, preferred_element_type=F32,
                                    precision=lax.Precision.HIGHEST)[0:1, :]
            st = lax.dot_general(k, q, NT_DIMS, preferred_element_type=F32)
            pt = jnp.exp2(st * EXP2_SCALE - st_ref[hh:hh + 1, :]).astype(BF16)
            dpt = lax.dot_general(v, dom, NT_DIMS, preferred_element_type=F32)
            dst = (pt.astype(F32) * (dpt - delta)).astype(BF16)
            dv_acc[...] += jnp.dot(pt, dom, preferred_element_type=F32)
            dk_acc[:, sl] += jnp.dot(dst, q, preferred_element_type=F32)
            dq = lax.dot_general(dst, k, TN_DIMS, preferred_element_type=F32) * ATTN_SCALE
            dq_ref[:, sl] = _rope(dq, c_ref[...], s_ref[...], True).astype(BF16)

        @pl.when(i == nq - 1)
        def _():
            dk_ref[...] = (dk_acc[...] * ATTN_SCALE).astype(BF16)
            dv_ref[...] = dv_acc[...].astype(BF16)

        riding.finish(state)

    o_spec = pl.BlockSpec((ATT_TQ, LANES), lambda p, i: (i, p))
    tab = pl.BlockSpec((ATT_TQ, LANES), lambda p, i: (i, 0))
    outs = pl.pallas_call(
        body, name=name, grid=(N_HEADS // 2, nq),
        out_shape=(jax.ShapeDtypeStruct((s, N_HEADS * LANES), BF16),
                   jax.ShapeDtypeStruct((t, N_HEADS * LANES), BF16),
                   jax.ShapeDtypeStruct((t, N_HEADS * V_DIM), BF16), *riding.out_shape),
        in_specs=[pl.BlockSpec((ATT_TQ, 2 * LANES), lambda p, i: (i, p)),
                  pl.BlockSpec((t, 2 * LANES), lambda p, i: (0, p)),
                  pl.BlockSpec((t, LANES), lambda p, i: (0, N_HEADS + p)),
                  o_spec, o_spec,
                  pl.BlockSpec((8, ATT_TQ), lambda p, i: (p, i)), tab, tab, *riding.specs],
        out_specs=(pl.BlockSpec((ATT_TQ, 2 * LANES), lambda p, i: (i, p)),
                   pl.BlockSpec((t, 2 * LANES), lambda p, i: (0, p)),
                   pl.BlockSpec((t, LANES), lambda p, i: (0, p)), *riding.specs),
        scratch_shapes=[pltpu.VMEM((t, 2 * LANES), F32), pltpu.VMEM((t, LANES), F32), *riding.scratch],
        compiler_params=_params(("arbitrary", "arbitrary"), VMEM_BIG),
    )(qf, kv, kv, o, da, stats, cos, sgn, *riding.arrays)
    return outs[0], outs[1], outs[2], list(outs[3:])


def _silu(x):
    return x * (1.0 / (1.0 + jnp.exp(-x)))


def _adaln_fwd(a, w, b, name):
    def body(a_ref, w_ref, b_ref, o_ref):
        o_ref[...] = jnp.dot(_silu(a_ref[...]), w_ref[...], preferred_element_type=F32,
                             precision=lax.Precision.HIGHEST) + b_ref[...]

    return pl.pallas_call(
        body, name=name, out_shape=jax.ShapeDtypeStruct((a.shape[0], w.shape[1]), F32),
        compiler_params=_params(None, VMEM_BIG),
    )(a, w, b)


def _adaln_bwd(a_t, w, d_ex, d_ctx, d_all, name):
    def body(at_ref, w_ref, dex_ref, dctx_ref, dall_ref, gw_ref, dsil_ref, dsum_ref):
        sil_t = _silu(at_ref[...])
        dctx = dctx_ref[...]
        row = dctx[0:1, :]
        for j in range(1, N_DEV):
            row = row + dctx[j:j + 1, :]
        rowi = lax.broadcasted_iota(jnp.int32, dctx.shape, 0)
        ctx_rows = jnp.where(rowi == 0, jnp.broadcast_to(row, dctx.shape), 0.0)
        hi = lax.Precision.HIGHEST
        d_rows = jnp.concatenate([dex_ref[...], ctx_rows], axis=0)
        gw_ref[...] = jnp.dot(sil_t, d_rows, preferred_element_type=F32, precision=hi)
        dsil_ref[...] = lax.dot_general(ctx_rows, w_ref[...], NT_DIMS, preferred_element_type=F32, precision=hi)
        tot = dall_ref[0]
        for j in range(1, N_DEV):
            tot = tot + dall_ref[j]
        dsum_ref[...] = tot

    return pl.pallas_call(
        body, name=name,
        out_shape=(jax.ShapeDtypeStruct(w.shape, F32), jax.ShapeDtypeStruct((8, w.shape[0]), F32),
                   jax.ShapeDtypeStruct(d_all.shape[1:], F32)),
        compiler_params=_params(None, VMEM_BIG),
    )(a_t, w, d_ex, d_ctx, d_all)


def _pack_small(sums1, sums2, fsums, sums1c, psums, d_cw, name):
    d = D_MODEL

    def body(s1_ref, s2_ref, f_ref, s1c_ref, p_ref, cw_ref, o_ref):
        o_ref[...] = jnp.zeros_like(o_ref)
        for col, (ref, r) in enumerate([(s1_ref, 1), (s1_ref, 0), (s2_ref, 2), (s2_ref, 1), (s2_ref, 0), (f_ref, 1)]):
            o_ref[0:1, col * d:(col + 1) * d] = ref[r:r + 1, :]
        o_ref[1:2, 0:d] = s1c_ref[1:2, :]
        o_ref[1:2, d:2 * d] = s1c_ref[0:1, :]
        o_ref[2:3, 0:Q_RANK] = p_ref[0:1, :]
        o_ref[2:3, Q_RANK:Q_RANK + KV_RANK] = p_ref[1:2, 0:KV_RANK]
        o_ref[2:3, Q_RANK + KV_RANK:Q_RANK + KV_RANK + d] = f_ref[0:1, :]
        for r in range(3):
            o_ref[3 + r:4 + r, 0:CONV_W] = cw_ref[r:r + 1, :]
        o_ref[6:7, 0:d] = f_ref[3:4, :]

    return pl.pallas_call(body, name=name, out_shape=jax.ShapeDtypeStruct((8, 6 * d), F32))(
        sums1, sums2, fsums, sums1c, psums, d_cw)


def _adam_math(w, g, m, v):
    nm = ADAM_B1 * m + (1.0 - ADAM_B1) * g
    nv = ADAM_B2 * v + (1.0 - ADAM_B2) * (g * g)
    m_hat = nm / (1.0 - ADAM_B1 ** ADAM_STEP)
    v_hat = nv / (1.0 - ADAM_B2 ** ADAM_STEP)
    return -ADAM_LR * (m_hat / (jnp.sqrt(v_hat) + ADAM_EPS) + ADAM_WD * w), nm, nv


def _small_update(dsum, dsil_all, g_cw, params, name):
    d = D_MODEL
    n = len(params)

    def body(*refs):
        dsum_ref, dsil_ref, gcw_ref = refs[:3]
        wmv = refs[3:3 + 3 * n]
        outs = refs[3 + 3 * n:]
        tot = dsil_ref[0]
        for j in range(1, N_DEV):
            tot = tot + dsil_ref[j]
        cv = wmv[0][...]
        sg = 1.0 / (1.0 + jnp.exp(-cv))
        off = Q_RANK + KV_RANK
        grads = [tot[0:1, :] * (sg * (1.0 + cv * (1.0 - sg))),
                 dsum_ref[0:1, :] + dsum_ref[1:2, :],
                 dsum_ref[2:3, 0:Q_RANK], dsum_ref[2:3, Q_RANK:off], dsum_ref[2:3, off:off + d],
                 gcw_ref[...]]
        for p, g in enumerate(grads):
            w_ref, m_ref, v_ref = wmv[3 * p:3 * p + 3]
            at = 0 if len(w_ref.shape) == 3 else Ellipsis
            res = (g,) + _adam_math(w_ref[at], g, m_ref[at], v_ref[at])
            for q, val in enumerate(res):
                outs[4 * p + q][at] = val

    flat = [a for wmv in params for a in wmv]
    out_shape = tuple(jax.ShapeDtypeStruct(wmv[0].shape, F32) for wmv in params for _ in range(4))
    outs = pl.pallas_call(body, name=name, out_shape=out_shape)(dsum, dsil_all, g_cw, *flat)
    return [outs[4 * p:4 * p + 4] for p in range(n)]


def _adamw(w, g, m, v, name, slots=False):
    _, rows, cols = w.shape
    tr = _pick(rows, (256, 128, 64, 32, 16, 8))

    def body(w_ref, g_ref, m_ref, v_ref, *outs):
        if slots:
            gv = g_ref[0].astype(F32)
            for j in range(1, N_DEV):
                gv = gv + g_ref[j].astype(F32)
            outs[0][...] = gv
        else:
            gv = g_ref[...]
        d_ref, nm_ref, nv_ref = outs[-3:]
        d_ref[...], nm_ref[...], nv_ref[...] = _adam_math(w_ref[...], gv, m_ref[...], v_ref[...])

    blk = pl.BlockSpec((None, tr, cols), lambda i: (0, i, 0))
    g_spec = (pl.BlockSpec((N_DEV, tr, cols), lambda i: (0, i, 0)) if slots
              else pl.BlockSpec((tr, cols), lambda i: (i, 0)))
    sh = jax.ShapeDtypeStruct((1, rows, cols), F32)
    n_out = 4 if slots else 3
    return pl.pallas_call(
        body, name=name, grid=(rows // tr,), out_shape=(sh,) * n_out,
        in_specs=[blk, g_spec, blk, blk], out_specs=(blk,) * n_out,
        compiler_params=_params(("parallel",)),
    )(w, g, m, v)


def _rope_tables(s, l):
    tok = np.arange(s)
    row = (tok // GRID_W).astype(np.float32)
    col = (tok % GRID_W).astype(np.float32)
    half = QK_ROPE // 2
    freqs = np.float32(ROPE_THETA) ** (-np.arange(0, half, 2, dtype=np.float32) / np.float32(half))
    dd = np.arange(QK_ROPE)
    pos = np.where((dd // half)[None, :] == 0, row[:, None], col[:, None]).astype(np.float32)
    ang = (pos * freqs[dd % (half // 2)][None, :]).astype(np.float32)
    sin = np.sin(ang).astype(np.float32)
    cos_t = np.ones((s + l, LANES), np.float32)
    sgn_t = np.zeros((s + l, LANES), np.float32)
    cos_t[:s, QK_NOPE:QK_NOPE + QK_ROPE] = np.cos(ang)
    sgn_t[:s, QK_NOPE:QK_NOPE + QK_ROPE] = np.where(((dd % half) // (half // 2))[None, :] == 0, -sin, sin)
    return jnp.asarray(cos_t), jnp.asarray(sgn_t)


def _slots_to_cols(g):
    return g.transpose(1, 0, 2).reshape(g.shape[1], N_DEV * g.shape[2])


def _cols_to_slots(w):
    return w.reshape(w.shape[0], N_DEV, w.shape[1] // N_DEV).transpose(1, 0, 2)


def _unpack_small_weights(g_in, g_uq, g_ukv):
    w_in = _slots_to_cols(g_in)
    zeros = jnp.zeros((D_MODEL, QK_NOPE), BF16)
    win_head = jnp.concatenate([w_in[:, :Q_RANK + KV_RANK], zeros, w_in[:, Q_RANK + KV_RANK:MLA_IN],
                                zeros[:, :LANES - QK_NOPE - QK_ROPE]], axis=1)
    win_conv = w_in[:, MLA_IN:].reshape(D_MODEL, 3, CONV_W // LANES, LANES).transpose(0, 2, 1, 3)
    win_conv = win_conv.reshape(D_MODEL, 3 * CONV_W)
    w_uq = _slots_to_cols(g_uq).reshape(Q_RANK, N_HEADS, QK_NOPE + QK_ROPE)
    wq = jnp.pad(w_uq, ((0, 0), (0, 0), (0, LANES - QK_NOPE - QK_ROPE))).reshape(Q_RANK, N_HEADS * LANES)
    w_ukv = _slots_to_cols(g_ukv).reshape(KV_RANK, N_HEADS, QK_NOPE + V_DIM)
    k_top = jnp.pad(w_ukv[:, :, :QK_NOPE], ((0, 0), (0, 0), (0, LANES - QK_NOPE))).reshape(KV_RANK, N_HEADS * LANES)
    v_top = w_ukv[:, :, QK_NOPE:].reshape(KV_RANK, N_HEADS * V_DIM)
    eye = jnp.pad(jnp.eye(QK_ROPE, dtype=BF16), ((QK_NOPE, LANES - QK_NOPE - QK_ROPE),) * 2)
    wk = jnp.concatenate([
        jnp.concatenate([k_top, v_top], axis=1),
        jnp.concatenate([jnp.tile(eye, (1, N_HEADS)), jnp.zeros((LANES, N_HEADS * V_DIM), BF16)], axis=1)], axis=0)
    return win_head, win_conv, wq, wk


def _pack_small_grads(d_head, d_conv, d_wq, d_wkk, d_wkv):
    d_conv = d_conv.reshape(D_MODEL, CONV_W // LANES, 3, LANES).transpose(0, 2, 1, 3).reshape(D_MODEL, 3 * CONV_W)
    g_in = jnp.concatenate([d_head[:, :Q_RANK + KV_RANK],
                            d_head[:, Q_RANK + KV_RANK + QK_NOPE:Q_RANK + KV_RANK + QK_NOPE + QK_ROPE], d_conv], axis=1)
    g_uq = d_wq.reshape(Q_RANK, N_HEADS, LANES)[:, :, :QK_NOPE + QK_ROPE].reshape(Q_RANK, -1)
    g_kn = d_wkk[:KV_RANK].reshape(KV_RANK, N_HEADS, LANES)[:, :, :QK_NOPE]
    g_v = d_wkv[:KV_RANK].reshape(KV_RANK, N_HEADS, V_DIM)
    g_ukv = jnp.concatenate([g_kn, g_v], axis=2).reshape(KV_RANK, -1)
    return [_cols_to_slots(g).astype(BF16) for g in (g_in, g_uq, g_ukv)]


def kernel(x, c, ctx, c_ctx, w_mod, b_mod, w_in, q_norm_g, w_uq, kv_norm_g, w_ukv, conv_w, w_out, w_mlp1, w_mlp2, final_norm_g, loss_target, m_c_ctx, m_w_mod, m_b_mod, m_w_in, m_q_norm_g, m_w_uq, m_kv_norm_g, m_w_ukv, m_conv_w, m_w_out, m_w_mlp1, m_w_mlp2, m_final_norm_g, v_c_ctx, v_w_mod, v_b_mod, v_w_in, v_q_norm_g, v_w_uq, v_kv_norm_g, v_w_ukv, v_conv_w, v_w_out, v_w_mlp1, v_w_mlp2, v_final_norm_g):
    me = _my_index()
    x2d, ctx2d, tgt = x[0], ctx[0], loss_target[0]
    s, l = x2d.shape[0], ctx2d.shape[0]
    t = s + l
    d = D_MODEL
    mod_cols = w_mod.shape[2]
    cw_cols = conv_w.shape[2]

    early = [w.astype(BF16) for w in (w_in[0], w_uq[0], w_ukv[0])]
    late = [w.astype(BF16) for w in (w_out[0], w_mlp1[0], w_mlp2[0])]
    c_all, g_in, g_uq, g_ukv = _all_gather([jnp.pad(c, ((0, 7), (0, 0))), *early], "gather_c_weights", False)
    a_rows = jnp.concatenate([c_all[:, 0, :], c_ctx[None, :], jnp.zeros((7, d), F32)], axis=0)
    b_cols = lax.dynamic_slice(b_mod, (0, me * mod_cols), (1, mod_cols))
    mod_cols_all = _adaln_fwd(a_rows, w_mod[0], b_cols, "adaln_fwd")
    cw_blk = jnp.pad(conv_w[0], ((0, 5), (0, mod_cols - cw_cols)))
    (gathered,) = _all_gather([jnp.concatenate([mod_cols_all, cw_blk], axis=0)], "gather_mod", True)
    mod_mine = lax.dynamic_index_in_dim(gathered, me, axis=1, keepdims=False).reshape(1, 6 * d)
    mod_ctx = gathered[:, 8, :].reshape(1, 6 * d)
    cw_full = gathered[:, 16:19, :cw_cols].transpose(1, 0, 2).reshape(3, CONV_W)

    win_head, win_conv, wq, wk = _unpack_small_weights(g_in, g_uq, g_ukv)
    wk_k, wk_v = wk[:, :N_HEADS * LANES], wk[:, N_HEADS * LANES:]
    cos, sgn = _rope_tables(s, l)

    h_all = _modulate_all(x2d, ctx2d, mod_mine, mod_ctx, "modulate1")
    tm_t = _pick(t, (1088, 768, 256))
    tk_t = _pick(t, (2176, 768, 256))
    z_head = _matmul(h_all, win_head, mode="nn", name="in_proj_head", tm=tm_t, tn=512, tk=1024)
    z_conv = _matmul(h_all, win_conv, mode="nn", name="in_proj_conv", m=s, tm=1024, tn=1536, tk=1024)
    cq, kv_in = _qkv_prep(z_head, q_norm_g, kv_norm_g, cos, sgn, "qkv_prep")
    qf = _matmul(cq, wq, mode="nn", name="q_up", out_dtype=BF16, m=s, tm=1024, tn=1024, tk=256,
                 epilogue="rope", extra=(cos, sgn))
    kv = _matmul(kv_in, wk, mode="nn", name="kv_up", out_dtype=BF16, tm=tm_t, tn=1536, tk=256)
    attn, a_cat, stats, (g_out, w1, g_w2) = _attn_fwd(qf, kv, _Riding("gather", late), "attn_fwd")
    wo = g_out.reshape(d, d)
    w2 = g_w2.reshape(D_FF, d)
    a_cat = _conv_fwd(z_conv, cw_full, a_cat, "conv_fwd")
    (o, x1, h2), _ = _matmul_rows(a_cat, wo, _epi_resid_modulate, mode="nn", name="out_proj", tm=1024, tk=1024,
                                  rows=[x2d], vecs=[(mod_mine, 2), (mod_mine, 3), (mod_mine, 4)],
                                  out_dtypes=[F32, F32, BF16])
    u1, act = _matmul(h2, w1, mode="nn", name="mlp_up", tm=2048, tk=1024, epilogue="relu2", slots="b_cols")
    (dx2, dm, fsums), _ = _matmul_rows(act, w2, _epi_final, mode="nn", name="mlp_down", tm=512, tk=4096,
                                       rows=[x1, tgt], vecs=[(mod_mine, 5), (final_norm_g[None, :], 0)],
                                       out_dtypes=[F32, BF16], sums=True)

    d_w2 = _matmul(act, dm, mode="tn", name="d_w_mlp2", out_dtype=BF16, tm=2048, tn=1024, tk=1024)
    du1 = _matmul(dm, w2, mode="nt", name="d_act", out_dtype=BF16, tm=2048, tn=1024, tk=1024,
                  epilogue="drelu2", extra=(u1,))
    d_w1 = _matmul(h2, du1, mode="tn", name="d_w_mlp1", out_dtype=BF16, tm=1024, tk=4096, slots="out")
    (dx1, do, sums2), _ = _matmul_rows(du1, w1, _epi_modulate2_bwd, mode="nt", name="d_h2", tm=512, tk=4096,
                                       slots="b_contract", rows=[x1, dx2, o], vecs=[(mod_mine, 4), (mod_mine, 2)],
                                       out_dtypes=[F32, BF16], sums=True)
    d_wo = _matmul(a_cat, do, mode="tn", name="d_w_out", out_dtype=BF16, tm=1024, tn=1024, tk=2048)
    da = _matmul(do, wo, mode="nt", name="d_a", tm=1024, tn=1024, tk=1024)
    dz_conv, d_cw = _conv_bwd(z_conv, cw_full, da, "conv_bwd")
    ready = [d_wo.reshape(N_DEV, d // N_DEV, d), d_w1, d_w2.reshape(N_DEV, D_FF // N_DEV, d)]
    dq, dk, dv, rode = _attn_bwd(qf, kv, attn, da, stats, cos, sgn, _Riding("exchange", ready), "attn_bwd")
    d_wq = _matmul(cq, dq, mode="tn", name="d_w_uq", k=s, tm=256, tn=1024, tk=4096)
    dcq = _matmul(dq, wq, mode="nt", name="d_cq", tm=1024, tn=256, tk=1024)
    d_wkk = _matmul(kv_in, dk, mode="tn", name="d_w_ukv_k", tm=256, tn=1024, tk=tk_t)
    d_wkv = _matmul(kv_in, dv, mode="tn", name="d_w_ukv_v", tm=256, tn=512, tk=tk_t)
    dkv_in = _matmul(dk, wk_k, mode="nt", name="d_kv_in_k", tm=tm_t, tn=256, tk=1024)
    dkv_in = _matmul(dv, wk_v, mode="nt", name="d_kv_in_v", tm=tm_t, tn=256, tk=512, addend=dkv_in)
    dz_head, psums = _qkv_prep_bwd(z_head, dcq, dkv_in, q_norm_g, kv_norm_g, cos, sgn, s, "qkv_prep_bwd")
    d_head = _matmul(h_all, dz_head, mode="tn", name="d_w_in_head", tm=1024, tn=512, tk=tk_t)
    d_conv = _matmul(h_all, dz_conv, mode="tn", name="d_w_in_conv", k=s, tm=1024, tn=1536, tk=2048)
    dh_head = _matmul(dz_head, win_head, mode="nt", name="d_h1_head", tm=tm_t, tn=1024, tk=512)
    send = _pack_small_grads(d_head, d_conv, d_wq, d_wkk, d_wkv)
    (grad_x, sums1), got = _matmul_rows(dz_conv, win_conv, _epi_modulate1_bwd, mode="nt", name="d_h1", tm=1024,
                                        tk=win_conv.shape[1], rows=[dh_head, x2d, dx1], vecs=[(mod_mine, 1)],
                                        out_dtypes=[F32], sums=True, riding=_Riding("exchange", send))
    (sums1c,) = _modulate_bwd(dh_head, s // ROW_TILE, ctx2d, mod_ctx, 1, "modulate1_ctx_bwd")

    small = _pack_small(sums1, sums2, fsums, sums1c, psums, d_cw, "pack_small")
    (d_all,) = _all_gather([small], "gather_small_grads", True)
    d_cols = lax.dynamic_slice_in_dim(d_all, me * mod_cols, mod_cols, axis=2)
    g_w_mod, dsil, dsum = _adaln_bwd(a_rows.T, w_mod[0], d_cols[:, 0, :], d_cols[:, 1, :], d_all, "adaln_bwd")
    (dsil_all,) = _all_gather([dsil], "gather_d_cctx", True)
    loss = dsum[6, 0]
    g_cw = lax.dynamic_slice(dsum, (3, me * cw_cols), (3, cw_cols))

    slots = dict(zip(["w_in", "w_uq", "w_ukv"], got))
    slots.update(zip(["w_out", "w_mlp1", "w_mlp2"], rode))

    grads = {}
    weights = {"c_ctx": c_ctx, "w_mod": w_mod, "b_mod": b_mod, "w_in": w_in, "q_norm_g": q_norm_g, "w_uq": w_uq,
               "kv_norm_g": kv_norm_g, "w_ukv": w_ukv, "conv_w": conv_w, "w_out": w_out, "w_mlp1": w_mlp1,
               "w_mlp2": w_mlp2, "final_norm_g": final_norm_g}
    m_in = {"c_ctx": m_c_ctx, "w_mod": m_w_mod, "b_mod": m_b_mod, "w_in": m_w_in, "q_norm_g": m_q_norm_g,
            "w_uq": m_w_uq, "kv_norm_g": m_kv_norm_g, "w_ukv": m_w_ukv, "conv_w": m_conv_w, "w_out": m_w_out,
            "w_mlp1": m_w_mlp1, "w_mlp2": m_w_mlp2, "final_norm_g": m_final_norm_g}
    v_in = {"c_ctx": v_c_ctx, "w_mod": v_w_mod, "b_mod": v_b_mod, "w_in": v_w_in, "q_norm_g": v_q_norm_g,
            "w_uq": v_w_uq, "kv_norm_g": v_kv_norm_g, "w_ukv": v_w_ukv, "conv_w": v_conv_w, "w_out": v_w_out,
            "w_mlp1": v_w_mlp1, "w_mlp2": v_w_mlp2, "final_norm_g": v_final_norm_g}
    names = list(weights)
    small_names = ["c_ctx", "b_mod", "q_norm_g", "kv_norm_g", "final_norm_g", "conv_w"]
    delta, new_m, new_v = {}, {}, {}

    def as_rows(a):
        return a[None, :] if a.ndim == 1 else a

    small_out = _small_update(dsum, dsil_all, g_cw, [[as_rows(src[n]) for src in (weights, m_in, v_in)]
                                                      for n in small_names], "small_update")
    for n, outs in zip(small_names, small_out):
        grads[n], delta[n], new_m[n], new_v[n] = [a.reshape(weights[n].shape) for a in outs]
    for n in names:
        if n in small_names:
            continue
        if n in slots:
            grads[n], delta[n], new_m[n], new_v[n] = _adamw(weights[n], slots[n], m_in[n], v_in[n], "adamw_" + n,
                                                            slots=True)
        else:
            delta[n], new_m[n], new_v[n] = _adamw(weights[n], g_w_mod, m_in[n], v_in[n], "adamw_" + n)
            grads[n] = g_w_mod[None]

    return (loss, grad_x[None], *[grads[n] for n in names], *[delta[n] for n in names],
            *[new_m[n] for n in names], *[new_v[n] for n in names])
```

```python
import math

import jax
import jax.numpy as jnp
import numpy as np
from jax import lax
from jax.experimental import pallas as pl
from jax.experimental.pallas import tpu as pltpu

F32 = jnp.float32
BF16 = jnp.bfloat16

D_MODEL = 1024
GRID_W = 64
N_HEADS = 8
QK_NOPE = 64
QK_ROPE = 32
V_DIM = 64
Q_RANK = 256
KV_RANK = 128
MLA_IN = Q_RANK + KV_RANK + QK_ROPE
CONV_W = 512
HEAD_COLS = 512
D_FF = 4096
ROPE_THETA = 10000.0
EPS = 1e-6
ATTN_SCALE = 1.0 / math.sqrt(QK_NOPE + QK_ROPE)
LOG2_E = 1.0 / math.log(2.0)
EXP2_SCALE = ATTN_SCALE * LOG2_E
N_DEV = 8
LANES = 128

ADAM_LR, ADAM_B1, ADAM_B2, ADAM_EPS, ADAM_WD, ADAM_STEP = 0.001, 0.9, 0.999, 1e-08, 0.01, 10

ROW_TILE = 256
VMEM_BIG = 60 * 1024 * 1024


def _params(sem=None, vmem=None):
    return pltpu.CompilerParams(dimension_semantics=sem, vmem_limit_bytes=vmem)


def _pick(n, prefs):
    for p in prefs:
        if n % p == 0:
            return p
    return n


def _my_index():
    return 4 * lax.axis_index("x") + 2 * lax.axis_index("y") + lax.axis_index("c")


def _all_gather(arrays, name, in_vmem):
    space = pltpu.VMEM if in_vmem else pl.ANY
    n = len(arrays)

    def body(*refs):
        x_refs, out_refs = refs[:n], refs[n:2 * n]
        send_sems, recv_sems, local_sems = refs[2 * n:]
        x, y, c = lax.axis_index("x"), lax.axis_index("y"), lax.axis_index("c")
        me, sibling = (x, y, c), (x, y, 1 - c)
        chips = [(1 - x, y), (x, 1 - y), (1 - x, 1 - y)]

        def slot(a, px, py, pc):
            return out_refs[a].at[4 * px + 2 * py + pc]

        def copy(a, k, block, to, src=None):
            return pltpu.make_async_remote_copy(
                src_ref=slot(a, *block) if src is None else src, dst_ref=slot(a, *block),
                send_sem=send_sems.at[7 * a + k], recv_sem=recv_sems.at[7 * a + k],
                device_id=to, device_id_type=pl.DeviceIdType.MESH)

        mine = [pltpu.make_async_copy(x_refs[a], slot(a, *me), local_sems.at[a]) for a in range(n)]
        for cp in mine:
            cp.start()
        started = []
        for a in range(n):
            first = [copy(a, 0, me, sibling, src=x_refs[a])]
            first += [copy(a, 1 + j, me, (*chip, c), src=x_refs[a]) for j, chip in enumerate(chips)]
            for cp in first:
                cp.start()
            started += first
        for a in range(n):
            for j, chip in enumerate(chips):
                copy(a, 1 + j, (*chip, c), me).wait_recv()
                passed = copy(a, 4 + j, (*chip, c), sibling)
                passed.start()
                started.append(passed)
        for a in range(n):
            copy(a, 0, sibling, me).wait_recv()
            for j, chip in enumerate(chips):
                copy(a, 4 + j, (*chip, 1 - c), me).wait_recv()
        for cp in started:
            cp.wait_send()
        for cp in mine:
            cp.wait()

    outs = pl.pallas_call(
        body, name=name,
        out_shape=tuple(jax.ShapeDtypeStruct((N_DEV,) + a.shape, a.dtype) for a in arrays),
        in_specs=[pl.BlockSpec(memory_space=space)] * n,
        out_specs=tuple(pl.BlockSpec(memory_space=space) for _ in arrays),
        scratch_shapes=[pltpu.SemaphoreType.DMA((7 * n,)), pltpu.SemaphoreType.DMA((7 * n,)),
                        pltpu.SemaphoreType.DMA((n,))],
    )(*arrays)
    return list(outs)


class _Riding:
    def __init__(self, kind, arrays):
        self.kind, self.arrays, self.n = kind, list(arrays), len(arrays)
        lead = (N_DEV,) if kind == "gather" else ()
        self.out_shape = [jax.ShapeDtypeStruct(lead + a.shape, a.dtype) for a in self.arrays]
        self.specs = [pl.BlockSpec(memory_space=pl.ANY)] * self.n
        self.scratch = [pltpu.SemaphoreType.DMA((7 * self.n,)), pltpu.SemaphoreType.DMA((7 * self.n,)),
                        pltpu.SemaphoreType.DMA((self.n,))]

    def copies(self, x_refs, y_refs, send_sems, recv_sems, local_sems):
        x, y, c = lax.axis_index("x"), lax.axis_index("y"), lax.axis_index("c")
        me = 4 * x + 2 * y + c
        local, sends, landings = [], [], []
        for a in range(self.n):
            src_mine = x_refs[a] if self.kind == "gather" else x_refs[a].at[me]
            local.append(pltpu.make_async_copy(src_mine, y_refs[a].at[me], local_sems.at[a]))
            for k in range(1, N_DEV):
                peer = (1 - x if k & 4 else x, 1 - y if k & 2 else y, 1 - c if k & 1 else c)
                pid = 4 * peer[0] + 2 * peer[1] + peer[2]
                src = x_refs[a] if self.kind == "gather" else x_refs[a].at[pid]
                for dst, out in ((me, sends), (pid, landings)):
                    out.append(pltpu.make_async_remote_copy(
                        src_ref=src, dst_ref=y_refs[a].at[dst],
                        send_sem=send_sems.at[7 * a + k - 1], recv_sem=recv_sems.at[7 * a + k - 1],
                        device_id=peer, device_id_type=pl.DeviceIdType.MESH))
        return local, sends, landings

    def run(self, first, last, x_refs, y_refs, sems):
        if self.n == 0:
            return None
        local, sends, landings = self.copies(x_refs, y_refs, *sems)

        @pl.when(first)
        def _():
            for cp in local + sends:
                cp.start()

        return local, sends, landings, last

    @staticmethod
    def finish(state):
        if state is None:
            return
        local, sends, landings, last = state

        @pl.when(last)
        def _():
            for cp in landings:
                cp.wait_recv()
            for cp in sends:
                cp.wait_send()
            for cp in local:
                cp.wait()


_DIMS = {"nn": (((1,), (0,)), ((), ())), "nt": (((1,), (1,)), ((), ())), "tn": (((0,), (0,)), ((), ()))}
NT_DIMS = _DIMS["nt"]
TN_DIMS = _DIMS["tn"]


def _swap8(x):
    lane = lax.broadcasted_iota(jnp.int32, x.shape, 1)
    return jnp.where((lane & 15) < 8, pltpu.roll(x, LANES - 8, 1), pltpu.roll(x, 8, 1))


def _rope(x, cos, sgn, bwd):
    return x * cos + (_swap8(x * sgn) if bwd else _swap8(x) * sgn)


def _matmul(a, b, *, mode, name, out_dtype=F32, tm=512, tn=512, tk=512, m=None, k=None,
            epilogue=None, extra=(), addend=None, slots=None):
    if mode == "nn":
        m = a.shape[0] if m is None else m
        k = a.shape[1]
        n = N_DEV * b.shape[2] if slots == "b_cols" else b.shape[1]
    elif mode == "nt":
        m = a.shape[0] if m is None else m
        k = a.shape[1]
        n = b.shape[1] if slots == "b_contract" else b.shape[0]
    else:
        k = a.shape[0] if k is None else k
        m, n = a.shape[1], b.shape[1]
    tm, tn, tk = min(tm, m), min(tn, n), min(tk, k)
    group = 1
    if slots == "b_cols":
        tn = b.shape[2]
    if slots == "b_contract":
        group = max(1, tk // b.shape[2])
        tk = group * b.shape[2]
    if slots == "out":
        tn = n // N_DEV
    assert m % tm == 0 and n % tn == 0 and k % tk == 0, (name, m, n, k, tm, tn, tk)
    nk = k // tk
    dims = _DIMS[mode]
    a_spec = (pl.BlockSpec((tk, tm), lambda i, j, kk: (kk, i)) if mode == "tn"
              else pl.BlockSpec((tm, tk), lambda i, j, kk: (i, kk)))
    if slots == "b_cols":
        b_spec = pl.BlockSpec((None, tk, tn), lambda i, j, kk: (j, kk, 0))
    elif slots == "b_contract":
        b_spec = pl.BlockSpec((group, tn, tk // group), lambda i, j, kk: (kk, j, 0))
    elif mode == "nt":
        b_spec = pl.BlockSpec((tn, tk), lambda i, j, kk: (j, kk))
    else:
        b_spec = pl.BlockSpec((tk, tn), lambda i, j, kk: (kk, j))
    tile = pl.BlockSpec((tm, tn), lambda i, j, kk: (i, j))
    if slots == "out":
        o_spec = pl.BlockSpec((None, tm, tn), lambda i, j, kk: (j, i, 0))
        o_shape = (N_DEV, m, tn)
    else:
        o_spec, o_shape = tile, (m, n)
    in_specs, args = [a_spec, b_spec], [a, b]
    if epilogue == "drelu2":
        in_specs.append(tile)
    elif epilogue == "rope":
        in_specs += [pl.BlockSpec((tm, LANES), lambda i, j, kk: (i, 0))] * 2
    args += list(extra)
    if addend is not None:
        in_specs.append(tile)
        args.append(addend)
    if epilogue == "relu2":
        out_shape = (jax.ShapeDtypeStruct(o_shape, BF16), jax.ShapeDtypeStruct(o_shape, BF16))
        out_specs = (o_spec, o_spec)
    else:
        out_shape = jax.ShapeDtypeStruct(o_shape, out_dtype)
        out_specs = o_spec
    n_in = len(args)
    n_out = 2 if epilogue == "relu2" else 1

    def body(*refs):
        a_ref, b_ref = refs[0], refs[1]
        outs = refs[n_in:n_in + n_out]
        if slots == "b_contract":
            c = tk // group
            part = lax.dot_general(a_ref[:, 0:c], b_ref[0], dims, preferred_element_type=F32)
            for u in range(1, group):
                part = part + lax.dot_general(a_ref[:, u * c:(u + 1) * c], b_ref[u], dims, preferred_element_type=F32)
        else:
            part = lax.dot_general(a_ref[...], b_ref[...], dims, preferred_element_type=F32)

        def finish(acc):
            if addend is not None:
                acc = acc + refs[n_in - 1][...]
            if epilogue == "relu2":
                outs[0][...] = acc.astype(BF16)
                r = jnp.maximum(acc, 0.0)
                outs[1][...] = (r * r).astype(BF16)
            elif epilogue == "drelu2":
                u = refs[2][...].astype(F32)
                outs[0][...] = (acc * (2.0 * jnp.maximum(u, 0.0))).astype(out_dtype)
            elif epilogue == "rope":
                cos, sgn = refs[2][...], refs[3][...]
                for h in range(tn // LANES):
                    sl = slice(h * LANES, (h + 1) * LANES)
                    outs[0][:, sl] = _rope(acc[:, sl], cos, sgn, False).astype(out_dtype)
            else:
                outs[0][...] = acc.astype(out_dtype)

        if nk == 1:
            finish(part)
        else:
            acc_ref = refs[n_in + n_out]
            kk = pl.program_id(2)

            @pl.when(kk == 0)
            def _():
                acc_ref[...] = part

            @pl.when(kk > 0)
            def _():
                acc_ref[...] += part

            @pl.when(kk == nk - 1)
            def _():
                finish(acc_ref[...])

    return pl.pallas_call(
        body, name=name, grid=(m // tm, n // tn, nk),
        out_shape=out_shape, in_specs=in_specs, out_specs=out_specs,
        scratch_shapes=[pltpu.VMEM((tm, tn), F32)] if nk > 1 else [],
        compiler_params=_params(("parallel", "parallel", "arbitrary"), VMEM_BIG),
    )(*args)


def _rstd(x):
    return lax.rsqrt(jnp.mean(x * x, axis=1, keepdims=True) + EPS)


def _norm_bwd(dxn, xn, r):
    return r * (dxn - xn * jnp.mean(dxn * xn, axis=1, keepdims=True))


def _vec(col):
    return pl.BlockSpec((1, D_MODEL), lambda i: (0, col))


def _matmul_rows(a, b, epi, *, mode, name, tm, tk, rows=(), vecs=(), out_dtypes=(), sums=False, slots=None,
                 riding=None):
    m, k = a.shape
    n = D_MODEL
    tm, tk = min(tm, m), min(tk, k)
    riding = riding or _Riding("gather", [])
    group = 1
    if slots == "b_contract":
        group = max(1, tk // b.shape[2])
        tk = group * b.shape[2]
        b_spec = pl.BlockSpec((group, n, tk // group), lambda i, kk: (kk, 0, 0))
    elif mode == "nt":
        b_spec = pl.BlockSpec((n, tk), lambda i, kk: (0, kk))
    else:
        b_spec = pl.BlockSpec((tk, n), lambda i, kk: (kk, 0))
    assert m % tm == 0 and k % tk == 0, (name, m, k, tm, tk)
    ni, nk = m // tm, k // tk
    dims = _DIMS[mode]
    tile = pl.BlockSpec((tm, n), lambda i, kk: (i, 0))
    in_specs = [pl.BlockSpec((tm, tk), lambda i, kk: (i, kk)), b_spec] + [tile] * len(rows)
    in_specs += [pl.BlockSpec((1, n), lambda i, kk, col=col: (0, col)) for _, col in vecs]
    args = [a, b, *rows, *[v for v, _ in vecs]]
    out_shape = [jax.ShapeDtypeStruct((m, n), dt) for dt in out_dtypes]
    out_specs = [tile] * len(out_dtypes)
    if sums:
        out_shape.append(jax.ShapeDtypeStruct((8, n), F32))
        out_specs.append(pl.BlockSpec((8, n), lambda i, kk: (0, 0)))
    n_rows, n_vecs, n_outs, nr = len(rows), len(vecs), len(out_dtypes), riding.n
    n_in = 2 + n_rows + n_vecs

    def body(*refs):
        a_ref, b_ref = refs[0], refs[1]
        row_refs = refs[2:2 + n_rows]
        vec_refs = refs[2 + n_rows:n_in]
        x_refs = refs[n_in:n_in + nr]
        out_refs = refs[n_in + nr:n_in + nr + n_outs]
        pos = n_in + nr + n_outs
        sums_ref = refs[pos] if sums else None
        pos += 1 if sums else 0
        y_refs = refs[pos:pos + nr]
        pos += nr
        acc_ref = refs[pos] if nk > 1 else None
        sem_refs = refs[pos + (1 if nk > 1 else 0):]
        i, kk = pl.program_id(0), pl.program_id(1)
        state = riding.run((i == 0) & (kk == 0), (i == ni - 1) & (kk == nk - 1), x_refs, y_refs, sem_refs)
        if slots == "b_contract":
            c = tk // group
            part = lax.dot_general(a_ref[:, 0:c], b_ref[0], dims, preferred_element_type=F32)
            for u in range(1, group):
                part = part + lax.dot_general(a_ref[:, u * c:(u + 1) * c], b_ref[u], dims, preferred_element_type=F32)
        else:
            part = lax.dot_general(a_ref[...], b_ref[...], dims, preferred_element_type=F32)

        def finish(acc):
            nsub = tm // ROW_TILE
            for r in range(nsub):
                blk = pl.ds(r * ROW_TILE, ROW_TILE)
                epi(acc[r * ROW_TILE:(r + 1) * ROW_TILE], [ref.at[blk] for ref in row_refs], vec_refs,
                    [ref.at[blk] for ref in out_refs], sums_ref,
                    (i == 0) if r == 0 else None, (i == ni - 1) if r == nsub - 1 else None)

        if nk == 1:
            finish(part)
        else:
            @pl.when(kk == 0)
            def _():
                acc_ref[...] = part

            @pl.when(kk > 0)
            def _():
                acc_ref[...] += part

            @pl.when(kk == nk - 1)
            def _():
                finish(acc_ref)

        riding.finish(state)

    outs = pl.pallas_call(
        body, name=name, grid=(ni, nk),
        out_shape=(*out_shape, *riding.out_shape),
        in_specs=[*in_specs, *riding.specs], out_specs=(*out_specs, *riding.specs),
        scratch_shapes=([pltpu.VMEM((tm, n), F32)] if nk > 1 else []) + (riding.scratch if nr else []),
        compiler_params=_params(("arbitrary", "arbitrary"), VMEM_BIG),
    )(*args, *riding.arrays)
    n_own = len(out_shape)
    return list(outs[:n_own]), list(outs[n_own:])


def _zero_sums_at_start(sums_ref, first):
    if first is not None:
        @pl.when(first)
        def _():
            sums_ref[...] = jnp.zeros_like(sums_ref)


def _epi_resid_modulate(acc, rows, vecs, outs, sums_ref, first, last):
    (x_ref,), (g_ref, sh_ref, sc_ref) = rows, vecs
    x1 = x_ref[...] + g_ref[...] * acc
    outs[0][...] = acc
    outs[1][...] = x1
    outs[2][...] = (x1 * _rstd(x1) * (1.0 + sc_ref[...]) + sh_ref[...]).astype(BF16)


def _epi_final(acc, rows, vecs, outs, sums_ref, first, last):
    (x1_ref, t_ref), (g_ref, gf_ref) = rows, vecs
    d = acc.shape[1]
    x2 = x1_ref[...] + g_ref[...] * acc
    r = _rstd(x2)
    xn = x2 * r
    err = xn * gf_ref[...] - t_ref[...]
    dy = err * (1.0 / d)
    dx2 = _norm_bwd(dy * gf_ref[...], xn, r)
    outs[0][...] = dx2
    outs[1][...] = (dx2 * g_ref[...]).astype(BF16)
    _zero_sums_at_start(sums_ref, first)
    sums_ref[0:1, :] += jnp.sum(dy * xn, axis=0, keepdims=True)
    sums_ref[1:2, :] += jnp.sum(dx2 * acc, axis=0, keepdims=True)
    sums_ref[2:3, :] += jnp.sum(err * err, axis=0, keepdims=True)

    if last is not None:
        @pl.when(last)
        def _():
            tot = jnp.sum(sums_ref[2:3, :], axis=1, keepdims=True) * (0.5 / d)
            sums_ref[3:4, :] = jnp.broadcast_to(tot, (1, d))


def _epi_modulate2_bwd(acc, rows, vecs, outs, sums_ref, first, last):
    (x_ref, dres_ref, o_ref), (sc_ref, g_ref) = rows, vecs
    x = x_ref[...]
    r = _rstd(x)
    xn = x * r
    dx = dres_ref[...] + _norm_bwd(acc * (1.0 + sc_ref[...]), xn, r)
    outs[0][...] = dx
    outs[1][...] = (dx * g_ref[...]).astype(BF16)
    _zero_sums_at_start(sums_ref, first)
    sums_ref[0:1, :] += jnp.sum(acc * xn, axis=0, keepdims=True)
    sums_ref[1:2, :] += jnp.sum(acc, axis=0, keepdims=True)
    sums_ref[2:3, :] += jnp.sum(dx * o_ref[...], axis=0, keepdims=True)


def _epi_modulate1_bwd(acc, rows, vecs, outs, sums_ref, first, last):
    (add_ref, x_ref, dres_ref), (sc_ref,) = rows, vecs
    dh = acc + add_ref[...]
    x = x_ref[...]
    r = _rstd(x)
    xn = x * r
    outs[0][...] = dres_ref[...] + _norm_bwd(dh * (1.0 + sc_ref[...]), xn, r)
    _zero_sums_at_start(sums_ref, first)
    sums_ref[0:1, :] += jnp.sum(dh * xn, axis=0, keepdims=True)
    sums_ref[1:2, :] += jnp.sum(dh, axis=0, keepdims=True)


def _modulate_all(x, ctx, mod, mod_ctx, name):
    s, d = x.shape
    t = s + ctx.shape[0]
    ns = s // ROW_TILE
    nc = ctx.shape[0] // ROW_TILE

    def body(x_ref, c_ref, sh_ref, sc_ref, shc_ref, scc_ref, h_ref):
        i = pl.program_id(0)

        @pl.when(i < ns)
        def _():
            v = x_ref[...]
            h_ref[...] = (v * _rstd(v) * (1.0 + sc_ref[...]) + sh_ref[...]).astype(BF16)

        @pl.when(i >= ns)
        def _():
            v = c_ref[...]
            h_ref[...] = (v * _rstd(v) * (1.0 + scc_ref[...]) + shc_ref[...]).astype(BF16)

    return pl.pallas_call(
        body, name=name, grid=(ns + nc,),
        out_shape=jax.ShapeDtypeStruct((t, d), BF16),
        in_specs=[pl.BlockSpec((ROW_TILE, d), lambda i: (jnp.minimum(i, ns - 1), 0)),
                  pl.BlockSpec((ROW_TILE, d), lambda i: (jnp.maximum(i - ns, 0), 0)),
                  _vec(0), _vec(1), _vec(0), _vec(1)],
        out_specs=pl.BlockSpec((ROW_TILE, d), lambda i: (i, 0)),
        compiler_params=_params(("arbitrary",)),
    )(x, ctx, mod, mod, mod_ctx, mod_ctx)


def _modulate_bwd(dh, row_off, xsrc, mod, scale_col, name, dres=None, o=None):
    s, d = xsrc.shape
    n = s // ROW_TILE
    has_dx, has_o = dres is not None, o is not None
    assert has_dx or not has_o

    def body(*refs):
        it = iter(refs)
        dh_ref, x_ref, sc_ref = next(it), next(it), next(it)
        dres_ref = next(it) if has_dx else None
        o_ref, g_ref = (next(it), next(it)) if has_o else (None, None)
        dx_ref = next(it) if has_dx else None
        do_ref = next(it) if has_o else None
        sums_ref = next(it)
        i = pl.program_id(0)
        x = x_ref[...]
        r = _rstd(x)
        xn = x * r
        dhv = dh_ref[...]

        @pl.when(i == 0)
        def _():
            sums_ref[...] = jnp.zeros_like(sums_ref)

        sums_ref[0:1, :] += jnp.sum(dhv * xn, axis=0, keepdims=True)
        sums_ref[1:2, :] += jnp.sum(dhv, axis=0, keepdims=True)
        if has_dx:
            dx = dres_ref[...] + _norm_bwd(dhv * (1.0 + sc_ref[...]), xn, r)
            dx_ref[...] = dx
            if has_o:
                do_ref[...] = (dx * g_ref[...]).astype(BF16)
                sums_ref[2:3, :] += jnp.sum(dx * o_ref[...], axis=0, keepdims=True)

    row = pl.BlockSpec((ROW_TILE, d), lambda i: (i, 0))
    in_specs = [pl.BlockSpec((ROW_TILE, d), lambda i: (i + row_off, 0)), row, _vec(scale_col)]
    args = [dh, xsrc, mod]
    out_shape, out_specs = [], []
    if has_dx:
        in_specs.append(row)
        args.append(dres)
        out_shape.append(jax.ShapeDtypeStruct((s, d), F32))
        out_specs.append(row)
    if has_o:
        in_specs += [row, _vec(2)]
        args += [o, mod]
        out_shape.append(jax.ShapeDtypeStruct((s, d), BF16))
        out_specs.append(row)
    out_shape.append(jax.ShapeDtypeStruct((8, d), F32))
    out_specs.append(pl.BlockSpec((8, d), lambda i: (0, 0)))
    return pl.pallas_call(
        body, name=name, grid=(n,),
        out_shape=tuple(out_shape), in_specs=in_specs, out_specs=tuple(out_specs),
        compiler_params=_params(("arbitrary",)),
    )(*args)


def _head_fwd(h_all, win_head, wq, wk, q_gain, kv_gain, cos, sgn, tm, name):
    t, d = h_all.shape
    nq, nkv = wq.shape[1], wk.shape[1]

    def body(h_ref, wi_ref, wq_ref, wk_ref, qg_ref, kg_ref, c_ref, s_ref, z_ref, cq_ref, kvin_ref, qf_ref, kv_ref):
        z = jnp.dot(h_ref[...], wi_ref[...], preferred_element_type=F32)
        z_ref[...] = z
        cos, sgn = c_ref[...], s_ref[...]
        zq = z[:, 0:Q_RANK]
        cq = (zq * _rstd(zq) * qg_ref[...]).astype(BF16)
        cq_ref[...] = cq
        zk = z[:, Q_RANK:Q_RANK + KV_RANK]
        kv_in = jnp.concatenate([(zk * _rstd(zk) * kg_ref[...]).astype(BF16),
                                 _rope(z[:, Q_RANK + KV_RANK:HEAD_COLS], cos, sgn, False).astype(BF16)], axis=1)
        kvin_ref[...] = kv_in
        q = jnp.dot(cq, wq_ref[...], preferred_element_type=F32)
        for h in range(nq // LANES):
            sl = slice(h * LANES, (h + 1) * LANES)
            qf_ref[:, sl] = _rope(q[:, sl], cos, sgn, False).astype(BF16)
        kv_ref[...] = jnp.dot(kv_in, wk_ref[...], preferred_element_type=F32).astype(BF16)

    def row(w):
        return pl.BlockSpec((tm, w), lambda i: (i, 0))

    def whole(a):
        return pl.BlockSpec(a.shape, lambda i: (0, 0))

    return pl.pallas_call(
        body, name=name, grid=(t // tm,),
        out_shape=(jax.ShapeDtypeStruct((t, HEAD_COLS), F32), jax.ShapeDtypeStruct((t, Q_RANK), BF16),
                   jax.ShapeDtypeStruct((t, KV_RANK + LANES), BF16), jax.ShapeDtypeStruct((t, nq), BF16),
                   jax.ShapeDtypeStruct((t, nkv), BF16)),
        in_specs=[row(d), whole(win_head), whole(wq), whole(wk), whole(q_gain), whole(kv_gain), row(LANES), row(LANES)],
        out_specs=(row(HEAD_COLS), row(Q_RANK), row(KV_RANK + LANES), row(nq), row(nkv)),
        compiler_params=_params(("parallel",), VMEM_BIG),
    )(h_all, win_head, wq, wk, q_gain, kv_gain, cos, sgn)


def _head_bwd(dq, dk, dv, z, wq, wk_k, wk_v, win_head, q_gain, kv_gain, cos, sgn, s, name):
    t = z.shape[0]
    ns = s // ROW_TILE

    def body(dq_ref, dk_ref, dv_ref, z_ref, wq_ref, wkk_ref, wkv_ref, wi_ref, qg_ref, kg_ref, c_ref, s_ref,
             dz_ref, dh_ref, sums_ref):
        i = pl.program_id(0)

        @pl.when(i == 0)
        def _():
            sums_ref[...] = jnp.zeros_like(sums_ref)

        @pl.when(i < ns)
        def _():
            dc = lax.dot_general(dq_ref[...], wq_ref[...], NT_DIMS, preferred_element_type=F32)
            zq = z_ref[:, 0:Q_RANK]
            r = _rstd(zq)
            zn = zq * r
            sums_ref[0:1, :] += jnp.sum(dc * zn, axis=0, keepdims=True)
            dz_ref[:, 0:Q_RANK] = _norm_bwd(dc * qg_ref[...], zn, r).astype(BF16)

        @pl.when(i >= ns)
        def _():
            dz_ref[:, 0:Q_RANK] = jnp.zeros((ROW_TILE, Q_RANK), BF16)

        dkv = (lax.dot_general(dk_ref[...], wkk_ref[...], NT_DIMS, preferred_element_type=F32)
               + lax.dot_general(dv_ref[...], wkv_ref[...], NT_DIMS, preferred_element_type=F32))
        zk = z_ref[:, Q_RANK:Q_RANK + KV_RANK]
        r = _rstd(zk)
        zn = zk * r
        dc = dkv[:, 0:KV_RANK]
        sums_ref[1:2, 0:KV_RANK] += jnp.sum(dc * zn, axis=0, keepdims=True)
        dz_ref[:, Q_RANK:Q_RANK + KV_RANK] = _norm_bwd(dc * kg_ref[...], zn, r).astype(BF16)
        dz_ref[:, Q_RANK + KV_RANK:HEAD_COLS] = _rope(dkv[:, KV_RANK:KV_RANK + LANES], c_ref[...], s_ref[...],
                                                       True).astype(BF16)
        dh_ref[...] = lax.dot_general(dz_ref[...], wi_ref[...], NT_DIMS, preferred_element_type=F32)

    def row(w):
        return pl.BlockSpec((ROW_TILE, w), lambda i: (i, 0))

    def whole(a):
        return pl.BlockSpec(a.shape, lambda i: (0, 0))

    return pl.pallas_call(
        body, name=name, grid=(t // ROW_TILE,),
        out_shape=(jax.ShapeDtypeStruct((t, HEAD_COLS), BF16), jax.ShapeDtypeStruct((t, D_MODEL), F32),
                   jax.ShapeDtypeStruct((8, Q_RANK), F32)),
        in_specs=[pl.BlockSpec((ROW_TILE, dq.shape[1]), lambda i: (jnp.minimum(i, ns - 1), 0)),
                  row(dk.shape[1]), row(dv.shape[1]), row(HEAD_COLS), whole(wq), whole(wk_k), whole(wk_v),
                  whole(win_head), whole(q_gain), whole(kv_gain), row(LANES), row(LANES)],
        out_specs=(row(HEAD_COLS), row(D_MODEL), pl.BlockSpec((8, Q_RANK), lambda i: (0, 0))),
        compiler_params=_params(("arbitrary",), VMEM_BIG),
    )(dq, dk, dv, z, wq, wk_k, wk_v, win_head, q_gain, kv_gain, cos, sgn)


def _shift_rows(u, s):
    rowi = lax.broadcasted_iota(jnp.int32, u.shape, 0)
    prev = jnp.where(rowi == 0, 0.0, pltpu.roll(u, 1, 0))
    nxt = jnp.where(rowi == s - 1, 0.0, pltpu.roll(u, s - 1, 0))
    return prev, nxt


def _conv_fwd(z_conv, cw, a_cat, name):
    s = z_conv.shape[0]

    def body(z_ref, w_ref, a_in_ref, o_ref):
        del a_in_ref
        gb, gc, xv = z_ref[:, 0:LANES], z_ref[:, LANES:2 * LANES], z_ref[:, 2 * LANES:3 * LANES]
        u = gc * xv
        prev, nxt = _shift_rows(u, s)
        y = w_ref[0:1, :] * prev + w_ref[1:2, :] * u + w_ref[2:3, :] * nxt
        o_ref[...] = (gb * y).astype(BF16)

    return pl.pallas_call(
        body, name=name, grid=(CONV_W // LANES,),
        out_shape=jax.ShapeDtypeStruct(a_cat.shape, a_cat.dtype),
        in_specs=[pl.BlockSpec((s, 3 * LANES), lambda j: (0, j)), pl.BlockSpec((3, LANES), lambda j: (0, j)),
                  pl.BlockSpec(memory_space=pl.ANY)],
        out_specs=pl.BlockSpec((s, LANES), lambda j: (0, 4 + j)),
        input_output_aliases={2: 0},
        compiler_params=_params(("parallel",), VMEM_BIG),
    )(z_conv, cw, a_cat)


def _conv_bwd(z_conv, cw, da, name):
    s = z_conv.shape[0]

    def body(z_ref, w_ref, da_ref, dz_ref, dw_ref):
        gb, gc, xv = z_ref[:, 0:LANES], z_ref[:, LANES:2 * LANES], z_ref[:, 2 * LANES:3 * LANES]
        u = gc * xv
        prev, nxt = _shift_rows(u, s)
        dcv = da_ref[...]
        dz_ref[:, 0:LANES] = (dcv * (w_ref[0:1, :] * prev + w_ref[1:2, :] * u + w_ref[2:3, :] * nxt)).astype(BF16)
        dy = dcv * gb
        dw_ref[0:1, :] = jnp.sum(dy * prev, axis=0, keepdims=True)
        dw_ref[1:2, :] = jnp.sum(dy * u, axis=0, keepdims=True)
        dw_ref[2:3, :] = jnp.sum(dy * nxt, axis=0, keepdims=True)
        dyp, dyn = _shift_rows(dy, s)
        du = w_ref[0:1, :] * dyn + w_ref[1:2, :] * dy + w_ref[2:3, :] * dyp
        dz_ref[:, LANES:2 * LANES] = (du * xv).astype(BF16)
        dz_ref[:, 2 * LANES:3 * LANES] = (du * gc).astype(BF16)

    blk = pl.BlockSpec((s, 3 * LANES), lambda j: (0, j))
    cws = pl.BlockSpec((3, LANES), lambda j: (0, j))
    return pl.pallas_call(
        body, name=name, grid=(CONV_W // LANES,),
        out_shape=(jax.ShapeDtypeStruct(z_conv.shape, BF16), jax.ShapeDtypeStruct((3, CONV_W), F32)),
        in_specs=[blk, cws, pl.BlockSpec((s, LANES), lambda j: (0, 4 + j))], out_specs=(blk, cws),
        compiler_params=_params(("parallel",), VMEM_BIG),
    )(z_conv, cw, da)


ATT_TQ = 256
ATT_Q_STEP = 1024
ATT_TQ_BWD = 512


def _head_mask(shape, hh):
    lane = lax.broadcasted_iota(jnp.int32, shape, 1)
    return (lane >= hh * V_DIM) & (lane < (hh + 1) * V_DIM)


def _attn_fwd(qf, kv, s, riding, name):
    t = kv.shape[0]
    step = min(ATT_Q_STEP, s)
    nq = s // step
    nr = riding.n

    def body(*refs):
        q_ref, k_ref, v_ref = refs[:3]
        o_ref, ob_ref, st_ref = refs[3 + nr:6 + nr]
        p, i = pl.program_id(0), pl.program_id(1)
        state = riding.run((p == 0) & (i == 0), (p == N_HEADS // 2 - 1) & (i == nq - 1),
                           refs[3:3 + nr], refs[6 + nr:6 + 2 * nr], refs[6 + 2 * nr:])
        v = v_ref[...]
        vlane = lax.broadcasted_iota(jnp.int32, v.shape, 1)
        one_lane = [(1 - hh) * V_DIM for hh in range(2)]
        vm = [jnp.where(_head_mask(v.shape, hh), v, jnp.where(vlane == one_lane[hh], 1.0, 0.0).astype(BF16))
              for hh in range(2)]

        def block(r, carry):
            rows = pl.ds(pl.multiple_of(r * ATT_TQ, ATT_TQ), ATT_TQ)
            olane = lax.broadcasted_iota(jnp.int32, (ATT_TQ, LANES), 1)
            acc = jnp.zeros((ATT_TQ, LANES), F32)
            stat = jnp.zeros((ATT_TQ, LANES), F32)
            for hh in range(2):
                sl = slice(hh * LANES, (hh + 1) * LANES)
                sc = lax.dot_general(q_ref[rows, sl], k_ref[:, sl], NT_DIMS, preferred_element_type=F32)
                mx = jnp.max(sc, axis=1, keepdims=True)
                e = jnp.exp2((sc - mx) * EXP2_SCALE).astype(BF16)
                res = jnp.dot(e, vm[hh], preferred_element_type=F32)
                den = jnp.sum(jnp.where(olane == one_lane[hh], res, 0.0), axis=1, keepdims=True)
                acc = acc + jnp.where(_head_mask(res.shape, hh), res * (1.0 / den), 0.0)
                stat = stat + jnp.where(olane == hh, mx * EXP2_SCALE + jnp.log(den) * LOG2_E, 0.0)
            o_ref[rows, :] = acc
            ob_ref[rows, :] = acc.astype(BF16)
            st_ref[:, rows] = stat.T[0:8, :]
            return carry

        lax.fori_loop(0, step // ATT_TQ, block, 0)
        riding.finish(state)

    o_spec = pl.BlockSpec((step, LANES), lambda p, i: (i, p))
    outs = pl.pallas_call(
        body, name=name, grid=(N_HEADS // 2, nq),
        out_shape=(jax.ShapeDtypeStruct((s, N_HEADS * V_DIM), F32),
                   jax.ShapeDtypeStruct((s, D_MODEL), BF16),
                   jax.ShapeDtypeStruct((N_HEADS // 2 * 8, s), F32), *riding.out_shape),
        in_specs=[pl.BlockSpec((step, 2 * LANES), lambda p, i: (i, p)),
                  pl.BlockSpec((t, 2 * LANES), lambda p, i: (0, p)),
                  pl.BlockSpec((t, LANES), lambda p, i: (0, N_HEADS + p)), *riding.specs],
        out_specs=(o_spec, o_spec, pl.BlockSpec((8, step), lambda p, i: (p, i)), *riding.specs),
        scratch_shapes=riding.scratch,
        compiler_params=_params(("arbitrary", "arbitrary"), VMEM_BIG),
    )(qf, kv, kv, *riding.arrays)
    return outs[0], outs[1], outs[2], list(outs[3:])


def _attn_bwd(qf, kv, o, da, stats, cos, sgn, riding, name):
    s, t = o.shape[0], kv.shape[0]
    ATT_TQ = ATT_TQ_BWD
    nq = s // ATT_TQ
    nr = riding.n

    def body(*refs):
        q_ref, k_ref, v_ref, o_ref, do_ref, st_ref, c_ref, s_ref = refs[:8]
        dq_ref, dk_ref, dv_ref = refs[8 + nr:11 + nr]
        dk_acc, dv_acc = refs[11 + 2 * nr:13 + 2 * nr]
        p, i = pl.program_id(0), pl.program_id(1)
        state = riding.run((p == 0) & (i == 0), (p == N_HEADS // 2 - 1) & (i == nq - 1),
                           refs[8:8 + nr], refs[11 + nr:11 + 2 * nr], refs[13 + 2 * nr:])

        @pl.when(i == 0)
        def _():
            dk_acc[...] = jnp.zeros_like(dk_acc)
            dv_acc[...] = jnp.zeros_like(dv_acc)

        v = v_ref[...]
        do = do_ref[...]
        od = do * o_ref[...]
        ones = jnp.ones((8, LANES), F32)
        for hh in range(2):
            sl = slice(hh * LANES, (hh + 1) * LANES)
            q, k = q_ref[:, sl], k_ref[:, sl]
            mask = _head_mask(do.shape, hh)
            dom = jnp.where(mask, do, 0.0).astype(BF16)
            delta = lax.dot_general(ones, jnp.where(mask, od, 0.0), NT_DIMS, preferred_element_type=F32,
                                    precision=lax.Precision.HIGHEST)[0:1, :]
            st = lax.dot_general(k, q, NT_DIMS, preferred_element_type=F32)
            pt = jnp.exp2(st * EXP2_SCALE - st_ref[hh:hh + 1, :]).astype(BF16)
            dpt = lax.dot_general(v, dom, NT_DIMS, preferred_element_type=F32)
            dst = (pt.astype(F32) * (dpt - delta)).astype(BF16)
            dv_acc[...] += jnp.dot(pt, dom, preferred_element_type=F32)
            dk_acc[:, sl] += jnp.dot(dst, q, preferred_element_type=F32)
            dq = lax.dot_general(dst, k, TN_DIMS, preferred_element_type=F32) * ATTN_SCALE
            dq_ref[:, sl] = _rope(dq, c_ref[...], s_ref[...], True).astype(BF16)

        @pl.when(i == nq - 1)
        def _():
            dk_ref[...] = (dk_acc[...] * ATTN_SCALE).astype(BF16)
            dv_ref[...] = dv_acc[...].astype(BF16)

        riding.finish(state)

    o_spec = pl.BlockSpec((ATT_TQ, LANES), lambda p, i: (i, p))
    tab = pl.BlockSpec((ATT_TQ, LANES), lambda p, i: (i, 0))
    outs = pl.pallas_call(
        body, name=name, grid=(N_HEADS // 2, nq),
        out_shape=(jax.ShapeDtypeStruct((s, N_HEADS * LANES), BF16),
                   jax.ShapeDtypeStruct((t, N_HEADS * LANES), BF16),
                   jax.ShapeDtypeStruct((t, N_HEADS * V_DIM), BF16), *riding.out_shape),
        in_specs=[pl.BlockSpec((ATT_TQ, 2 * LANES), lambda p, i: (i, p)),
                  pl.BlockSpec((t, 2 * LANES), lambda p, i: (0, p)),
                  pl.BlockSpec((t, LANES), lambda p, i: (0, N_HEADS + p)),
                  o_spec, o_spec,
                  pl.BlockSpec((8, ATT_TQ), lambda p, i: (p, i)), tab, tab, *riding.specs],
        out_specs=(pl.BlockSpec((ATT_TQ, 2 * LANES), lambda p, i: (i, p)),
                   pl.BlockSpec((t, 2 * LANES), lambda p, i: (0, p)),
                   pl.BlockSpec((t, LANES), lambda p, i: (0, p)), *riding.specs),
        scratch_shapes=[pltpu.VMEM((t, 2 * LANES), F32), pltpu.VMEM((t, LANES), F32), *riding.scratch],
        compiler_params=_params(("arbitrary", "arbitrary"), VMEM_BIG),
    )(qf, kv, kv, o, da, stats, cos, sgn, *riding.arrays)
    return outs[0], outs[1], outs[2], list(outs[3:])


def _silu(x):
    return x * (1.0 / (1.0 + jnp.exp(-x)))


def _adaln_fwd(a, w, b, name):
    def body(a_ref, w_ref, b_ref, o_ref):
        o_ref[...] = jnp.dot(_silu(a_ref[...]), w_ref[...], preferred_element_type=F32,
                             precision=lax.Precision.HIGHEST) + b_ref[...]

    return pl.pallas_call(
        body, name=name, out_shape=jax.ShapeDtypeStruct((a.shape[0], w.shape[1]), F32),
        compiler_params=_params(None, VMEM_BIG),
    )(a, w, b)


def _adaln_bwd(a_t, w, d_ex, d_ctx, d_all, name):
    def body(at_ref, w_ref, dex_ref, dctx_ref, dall_ref, gw_ref, dsil_ref, dsum_ref):
        sil_t = _silu(at_ref[...])
        dctx = dctx_ref[...]
        row = dctx[0:1, :]
        for j in range(1, N_DEV):
            row = row + dctx[j:j + 1, :]
        rowi = lax.broadcasted_iota(jnp.int32, dctx.shape, 0)
        ctx_rows = jnp.where(rowi == 0, jnp.broadcast_to(row, dctx.shape), 0.0)
        hi = lax.Precision.HIGHEST
        d_rows = jnp.concatenate([dex_ref[...], ctx_rows], axis=0)
        gw_ref[...] = jnp.dot(sil_t, d_rows, preferred_element_type=F32, precision=hi)
        dsil_ref[...] = lax.dot_general(ctx_rows, w_ref[...], NT_DIMS, preferred_element_type=F32, precision=hi)
        tot = dall_ref[0]
        for j in range(1, N_DEV):
            tot = tot + dall_ref[j]
        dsum_ref[...] = tot

    return pl.pallas_call(
        body, name=name,
        out_shape=(jax.ShapeDtypeStruct(w.shape, F32), jax.ShapeDtypeStruct((8, w.shape[0]), F32),
                   jax.ShapeDtypeStruct(d_all.shape[1:], F32)),
        compiler_params=_params(None, VMEM_BIG),
    )(a_t, w, d_ex, d_ctx, d_all)


def _pack_small(sums1, sums2, fsums, sums1c, psums, d_cw, name):
    d = D_MODEL

    def body(s1_ref, s2_ref, f_ref, s1c_ref, p_ref, cw_ref, o_ref):
        o_ref[...] = jnp.zeros_like(o_ref)
        for col, (ref, r) in enumerate([(s1_ref, 1), (s1_ref, 0), (s2_ref, 2), (s2_ref, 1), (s2_ref, 0), (f_ref, 1)]):
            o_ref[0:1, col * d:(col + 1) * d] = ref[r:r + 1, :]
        o_ref[1:2, 0:d] = s1c_ref[1:2, :]
        o_ref[1:2, d:2 * d] = s1c_ref[0:1, :]
        o_ref[2:3, 0:Q_RANK] = p_ref[0:1, :]
        o_ref[2:3, Q_RANK:Q_RANK + KV_RANK] = p_ref[1:2, 0:KV_RANK]
        o_ref[2:3, Q_RANK + KV_RANK:Q_RANK + KV_RANK + d] = f_ref[0:1, :]
        for r in range(3):
            o_ref[3 + r:4 + r, 0:CONV_W] = cw_ref[r:r + 1, :]
        o_ref[6:7, 0:d] = f_ref[3:4, :]

    return pl.pallas_call(body, name=name, out_shape=jax.ShapeDtypeStruct((8, 6 * d), F32))(
        sums1, sums2, fsums, sums1c, psums, d_cw)


def _adam_math(w, g, m, v):
    nm = ADAM_B1 * m + (1.0 - ADAM_B1) * g
    nv = ADAM_B2 * v + (1.0 - ADAM_B2) * (g * g)
    m_hat = nm / (1.0 - ADAM_B1 ** ADAM_STEP)
    v_hat = nv / (1.0 - ADAM_B2 ** ADAM_STEP)
    return -ADAM_LR * (m_hat / (jnp.sqrt(v_hat) + ADAM_EPS) + ADAM_WD * w), nm, nv


def _small_update(dsum, dsil_all, g_cw, params, name):
    d = D_MODEL
    n = len(params)

    def body(*refs):
        dsum_ref, dsil_ref, gcw_ref = refs[:3]
        wmv = refs[3:3 + 3 * n]
        outs = refs[3 + 3 * n:]
        tot = dsil_ref[0]
        for j in range(1, N_DEV):
            tot = tot + dsil_ref[j]
        cv = wmv[0][...]
        sg = 1.0 / (1.0 + jnp.exp(-cv))
        off = Q_RANK + KV_RANK
        grads = [tot[0:1, :] * (sg * (1.0 + cv * (1.0 - sg))),
                 dsum_ref[0:1, :] + dsum_ref[1:2, :],
                 dsum_ref[2:3, 0:Q_RANK], dsum_ref[2:3, Q_RANK:off], dsum_ref[2:3, off:off + d],
                 gcw_ref[...]]
        for p, g in enumerate(grads):
            w_ref, m_ref, v_ref = wmv[3 * p:3 * p + 3]
            at = 0 if len(w_ref.shape) == 3 else Ellipsis
            res = (g,) + _adam_math(w_ref[at], g, m_ref[at], v_ref[at])
            for q, val in enumerate(res):
                outs[4 * p + q][at] = val

    flat = [a for wmv in params for a in wmv]
    out_shape = tuple(jax.ShapeDtypeStruct(wmv[0].shape, F32) for wmv in params for _ in range(4))
    outs = pl.pallas_call(body, name=name, out_shape=out_shape)(dsum, dsil_all, g_cw, *flat)
    return [outs[4 * p:4 * p + 4] for p in range(n)]


def _adamw(w, g, m, v, name, slots=False):
    _, rows, cols = w.shape
    tr = _pick(rows, (256, 128, 64, 32, 16, 8))

    def body(w_ref, g_ref, m_ref, v_ref, *outs):
        if slots:
            gv = g_ref[0].astype(F32)
            for j in range(1, N_DEV):
                gv = gv + g_ref[j].astype(F32)
            outs[0][...] = gv
        else:
            gv = g_ref[...]
        d_ref, nm_ref, nv_ref = outs[-3:]
        d_ref[...], nm_ref[...], nv_ref[...] = _adam_math(w_ref[...], gv, m_ref[...], v_ref[...])

    blk = pl.BlockSpec((None, tr, cols), lambda i: (0, i, 0))
    g_spec = (pl.BlockSpec((N_DEV, tr, cols), lambda i: (0, i, 0)) if slots
              else pl.BlockSpec((tr, cols), lambda i: (i, 0)))
    sh = jax.ShapeDtypeStruct((1, rows, cols), F32)
    n_out = 4 if slots else 3
    return pl.pallas_call(
        body, name=name, grid=(rows // tr,), out_shape=(sh,) * n_out,
        in_specs=[blk, g_spec, blk, blk], out_specs=(blk,) * n_out,
        compiler_params=_params(("parallel",)),
    )(w, g, m, v)


def _rope_tables(s, l):
    tok = np.arange(s)
    row = (tok // GRID_W).astype(np.float32)
    col = (tok % GRID_W).astype(np.float32)
    half = QK_ROPE // 2
    freqs = np.float32(ROPE_THETA) ** (-np.arange(0, half, 2, dtype=np.float32) / np.float32(half))
    dd = np.arange(QK_ROPE)
    pos = np.where((dd // half)[None, :] == 0, row[:, None], col[:, None]).astype(np.float32)
    ang = (pos * freqs[dd % (half // 2)][None, :]).astype(np.float32)
    sin = np.sin(ang).astype(np.float32)
    cos_t = np.ones((s + l, LANES), np.float32)
    sgn_t = np.zeros((s + l, LANES), np.float32)
    cos_t[:s, QK_NOPE:QK_NOPE + QK_ROPE] = np.cos(ang)
    sgn_t[:s, QK_NOPE:QK_NOPE + QK_ROPE] = np.where(((dd % half) // (half // 2))[None, :] == 0, -sin, sin)
    return jnp.asarray(cos_t), jnp.asarray(sgn_t)


def _slots_to_cols(g):
    return g.transpose(1, 0, 2).reshape(g.shape[1], N_DEV * g.shape[2])


def _cols_to_slots(w):
    return w.reshape(w.shape[0], N_DEV, w.shape[1] // N_DEV).transpose(1, 0, 2)


def _unpack_small_weights(g_in, g_uq, g_ukv):
    w_in = _slots_to_cols(g_in)
    zeros = jnp.zeros((D_MODEL, QK_NOPE), BF16)
    win_head = jnp.concatenate([w_in[:, :Q_RANK + KV_RANK], zeros, w_in[:, Q_RANK + KV_RANK:MLA_IN],
                                zeros[:, :LANES - QK_NOPE - QK_ROPE]], axis=1)
    win_conv = w_in[:, MLA_IN:].reshape(D_MODEL, 3, CONV_W // LANES, LANES).transpose(0, 2, 1, 3)
    win_conv = win_conv.reshape(D_MODEL, 3 * CONV_W)
    w_uq = _slots_to_cols(g_uq).reshape(Q_RANK, N_HEADS, QK_NOPE + QK_ROPE)
    wq = jnp.pad(w_uq, ((0, 0), (0, 0), (0, LANES - QK_NOPE - QK_ROPE))).reshape(Q_RANK, N_HEADS * LANES)
    w_ukv = _slots_to_cols(g_ukv).reshape(KV_RANK, N_HEADS, QK_NOPE + V_DIM)
    k_top = jnp.pad(w_ukv[:, :, :QK_NOPE], ((0, 0), (0, 0), (0, LANES - QK_NOPE))).reshape(KV_RANK, N_HEADS * LANES)
    v_top = w_ukv[:, :, QK_NOPE:].reshape(KV_RANK, N_HEADS * V_DIM)
    eye = jnp.pad(jnp.eye(QK_ROPE, dtype=BF16), ((QK_NOPE, LANES - QK_NOPE - QK_ROPE),) * 2)
    wk = jnp.concatenate([
        jnp.concatenate([k_top, v_top], axis=1),
        jnp.concatenate([jnp.tile(eye, (1, N_HEADS)), jnp.zeros((LANES, N_HEADS * V_DIM), BF16)], axis=1)], axis=0)
    return win_head, win_conv, wq, wk


def _pack_small_grads(d_head, d_conv, d_wq, d_wkk, d_wkv):
    d_conv = d_conv.reshape(D_MODEL, CONV_W // LANES, 3, LANES).transpose(0, 2, 1, 3).reshape(D_MODEL, 3 * CONV_W)
    g_in = jnp.concatenate([d_head[:, :Q_RANK + KV_RANK],
                            d_head[:, Q_RANK + KV_RANK + QK_NOPE:Q_RANK + KV_RANK + QK_NOPE + QK_ROPE], d_conv], axis=1)
    g_uq = d_wq.reshape(Q_RANK, N_HEADS, LANES)[:, :, :QK_NOPE + QK_ROPE].reshape(Q_RANK, -1)
    g_kn = d_wkk[:KV_RANK].reshape(KV_RANK, N_HEADS, LANES)[:, :, :QK_NOPE]
    g_v = d_wkv[:KV_RANK].reshape(KV_RANK, N_HEADS, V_DIM)
    g_ukv = jnp.concatenate([g_kn, g_v], axis=2).reshape(KV_RANK, -1)
    return [_cols_to_slots(g).astype(BF16) for g in (g_in, g_uq, g_ukv)]


def kernel(x, c, ctx, c_ctx, w_mod, b_mod, w_in, q_norm_g, w_uq, kv_norm_g, w_ukv, conv_w, w_out, w_mlp1, w_mlp2, final_norm_g, loss_target, m_c_ctx, m_w_mod, m_b_mod, m_w_in, m_q_norm_g, m_w_uq, m_kv_norm_g, m_w_ukv, m_conv_w, m_w_out, m_w_mlp1, m_w_mlp2, m_final_norm_g, v_c_ctx, v_w_mod, v_b_mod, v_w_in, v_q_norm_g, v_w_uq, v_kv_norm_g, v_w_ukv, v_conv_w, v_w_out, v_w_mlp1, v_w_mlp2, v_final_norm_g):
    me = _my_index()
    x2d, ctx2d, tgt = x[0], ctx[0], loss_target[0]
    s, l = x2d.shape[0], ctx2d.shape[0]
    t = s + l
    d = D_MODEL
    mod_cols = w_mod.shape[2]
    cw_cols = conv_w.shape[2]

    early = [w.astype(BF16) for w in (w_in[0], w_uq[0], w_ukv[0])]
    late = [w.astype(BF16) for w in (w_out[0], w_mlp1[0], w_mlp2[0])]
    c_all, g_in, g_uq, g_ukv = _all_gather([jnp.pad(c, ((0, 7), (0, 0))), *early], "gather_c_weights", False)
    a_rows = jnp.concatenate([c_all[:, 0, :], c_ctx[None, :], jnp.zeros((7, d), F32)], axis=0)
    b_cols = lax.dynamic_slice(b_mod, (0, me * mod_cols), (1, mod_cols))
    mod_cols_all = _adaln_fwd(a_rows, w_mod[0], b_cols, "adaln_fwd")
    cw_blk = jnp.pad(conv_w[0], ((0, 5), (0, mod_cols - cw_cols)))
    (gathered,) = _all_gather([jnp.concatenate([mod_cols_all, cw_blk], axis=0)], "gather_mod", True)
    mod_mine = lax.dynamic_index_in_dim(gathered, me, axis=1, keepdims=False).reshape(1, 6 * d)
    mod_ctx = gathered[:, 8, :].reshape(1, 6 * d)
    cw_full = gathered[:, 16:19, :cw_cols].transpose(1, 0, 2).reshape(3, CONV_W)

    win_head, win_conv, wq, wk = _unpack_small_weights(g_in, g_uq, g_ukv)
    wk_k, wk_v = wk[:, :N_HEADS * LANES], wk[:, N_HEADS * LANES:]
    cos, sgn = _rope_tables(s, l)

    h_all = _modulate_all(x2d, ctx2d, mod_mine, mod_ctx, "modulate1")
    tm_t = _pick(t, (1088, 768, 256))
    tk_t = _pick(t, (2176, 768, 256))
    z_head, cq, kv_in, qf, kv = _head_fwd(h_all, win_head, wq, wk, q_norm_g, kv_norm_g, cos, sgn, tm_t, "head_fwd")
    z_conv = _matmul(h_all, win_conv, mode="nn", name="in_proj_conv", m=s, tm=1024, tn=1536, tk=1024)
    attn, a_cat, stats, (g_out, w1, g_w2) = _attn_fwd(qf, kv, s, _Riding("gather", late), "attn_fwd")
    wo = g_out.reshape(d, d)
    w2 = g_w2.reshape(D_FF, d)
    a_cat = _conv_fwd(z_conv, cw_full, a_cat, "conv_fwd")
    (o, x1, h2), _ = _matmul_rows(a_cat, wo, _epi_resid_modulate, mode="nn", name="out_proj", tm=1024, tk=1024,
                                  rows=[x2d], vecs=[(mod_mine, 2), (mod_mine, 3), (mod_mine, 4)],
                                  out_dtypes=[F32, F32, BF16])
    u1, act = _matmul(h2, w1, mode="nn", name="mlp_up", tm=2048, tk=1024, epilogue="relu2", slots="b_cols")
    (dx2, dm, fsums), _ = _matmul_rows(act, w2, _epi_final, mode="nn", name="mlp_down", tm=512, tk=4096,
                                       rows=[x1, tgt], vecs=[(mod_mine, 5), (final_norm_g[None, :], 0)],
                                       out_dtypes=[F32, BF16], sums=True)

    d_w2 = _matmul(act, dm, mode="tn", name="d_w_mlp2", out_dtype=BF16, tm=2048, tn=1024, tk=1024)
    du1 = _matmul(dm, w2, mode="nt", name="d_act", out_dtype=BF16, tm=2048, tn=1024, tk=1024,
                  epilogue="drelu2", extra=(u1,))
    d_w1 = _matmul(h2, du1, mode="tn", name="d_w_mlp1", out_dtype=BF16, tm=1024, tk=4096, slots="out")
    (dx1, do, sums2), _ = _matmul_rows(du1, w1, _epi_modulate2_bwd, mode="nt", name="d_h2", tm=512, tk=4096,
                                       slots="b_contract", rows=[x1, dx2, o], vecs=[(mod_mine, 4), (mod_mine, 2)],
                                       out_dtypes=[F32, BF16], sums=True)
    d_wo = _matmul(a_cat, do, mode="tn", name="d_w_out", out_dtype=BF16, tm=1024, tn=1024, tk=2048)
    da = _matmul(do, wo, mode="nt", name="d_a", tm=1024, tn=1024, tk=1024)
    dz_conv, d_cw = _conv_bwd(z_conv, cw_full, da, "conv_bwd")
    ready = [d_wo.reshape(N_DEV, d // N_DEV, d), d_w1, d_w2.reshape(N_DEV, D_FF // N_DEV, d)]
    dq, dk, dv, rode = _attn_bwd(qf, kv, attn, da, stats, cos, sgn, _Riding("exchange", ready), "attn_bwd")
    d_wq = _matmul(cq, dq, mode="tn", name="d_w_uq", k=s, tm=256, tn=1024, tk=4096)
    d_wkk = _matmul(kv_in, dk, mode="tn", name="d_w_ukv_k", tm=256, tn=1024, tk=tk_t)
    d_wkv = _matmul(kv_in, dv, mode="tn", name="d_w_ukv_v", tm=256, tn=512, tk=tk_t)
    dz_head, dh_head, psums = _head_bwd(dq, dk, dv, z_head, wq, wk_k, wk_v, win_head, q_norm_g, kv_norm_g, cos, sgn, s,
                                        "head_bwd")
    d_head = _matmul(h_all, dz_head, mode="tn", name="d_w_in_head", tm=1024, tn=512, tk=tk_t)
    d_conv = _matmul(h_all, dz_conv, mode="tn", name="d_w_in_conv", k=s, tm=1024, tn=1536, tk=2048)
    send = _pack_small_grads(d_head, d_conv, d_wq, d_wkk, d_wkv)
    (grad_x, sums1), got = _matmul_rows(dz_conv, win_conv, _epi_modulate1_bwd, mode="nt", name="d_h1", tm=1024,
                                        tk=win_conv.shape[1], rows=[dh_head, x2d, dx1], vecs=[(mod_mine, 1)],
                                        out_dtypes=[F32], sums=True, riding=_Riding("exchange", send))
    (sums1c,) = _modulate_bwd(dh_head, s // ROW_TILE, ctx2d, mod_ctx, 1, "modulate1_ctx_bwd")

    small = _pack_small(sums1, sums2, fsums, sums1c, psums, d_cw, "pack_small")
    (d_all,) = _all_gather([small], "gather_small_grads", True)
    d_cols = lax.dynamic_slice_in_dim(d_all, me * mod_cols, mod_cols, axis=2)
    g_w_mod, dsil, dsum = _adaln_bwd(a_rows.T, w_mod[0], d_cols[:, 0, :], d_cols[:, 1, :], d_all, "adaln_bwd")
    (dsil_all,) = _all_gather([dsil], "gather_d_cctx", True)
    loss = dsum[6, 0]
    g_cw = lax.dynamic_slice(dsum, (3, me * cw_cols), (3, cw_cols))

    slots = dict(zip(["w_in", "w_uq", "w_ukv"], got))
    slots.update(zip(["w_out", "w_mlp1", "w_mlp2"], rode))

    grads = {}
    weights = {"c_ctx": c_ctx, "w_mod": w_mod, "b_mod": b_mod, "w_in": w_in, "q_norm_g": q_norm_g, "w_uq": w_uq,
               "kv_norm_g": kv_norm_g, "w_ukv": w_ukv, "conv_w": conv_w, "w_out": w_out, "w_mlp1": w_mlp1,
               "w_mlp2": w_mlp2, "final_norm_g": final_norm_g}
    m_in = {"c_ctx": m_c_ctx, "w_mod": m_w_mod, "b_mod": m_b_mod, "w_in": m_w_in, "q_norm_g": m_q_norm_g,
            "w_uq": m_w_uq, "kv_norm_g": m_kv_norm_g, "w_ukv": m_w_ukv, "conv_w": m_conv_w, "w_out": m_w_out,
            "w_mlp1": m_w_mlp1, "w_mlp2": m_w_mlp2, "final_norm_g": m_final_norm_g}
    v_in = {"c_ctx": v_c_ctx, "w_mod": v_w_mod, "b_mod": v_b_mod, "w_in": v_w_in, "q_norm_g": v_q_norm_g,
            "w_uq": v_w_uq, "kv_norm_g": v_kv_norm_g, "w_ukv": v_w_ukv, "conv_w": v_conv_w, "w_out": v_w_out,
            "w_mlp1": v_w_mlp1, "w_mlp2": v_w_mlp2, "final_norm_g": v_final_norm_g}
    names = list(weights)
    small_names = ["c_ctx", "b_mod", "q_norm_g", "kv_norm_g", "final_norm_g", "conv_w"]
    delta, new_m, new_v = {}, {}, {}

    def as_rows(a):
        return a[None, :] if a.ndim == 1 else a

    small_out = _small_update(dsum, dsil_all, g_cw, [[as_rows(src[n]) for src in (weights, m_in, v_in)]
                                                      for n in small_names], "small_update")
    for n, outs in zip(small_names, small_out):
        grads[n], delta[n], new_m[n], new_v[n] = [a.reshape(weights[n].shape) for a in outs]
    for n in names:
        if n in small_names:
            continue
        if n in slots:
            grads[n], delta[n], new_m[n], new_v[n] = _adamw(weights[n], slots[n], m_in[n], v_in[n], "adamw_" + n,
                                                            slots=True)
        else:
            delta[n], new_m[n], new_v[n] = _adamw(weights[n], g_w_mod, m_in[n], v_in[n], "adamw_" + n)
            grads[n] = g_w_mod[None]

    return (loss, grad_x[None], *[grads[n] for n in names], *[delta[n] for n in names],
            *[new_m[n] for n in names], *[new_v[n] for n in names])
```

```python
import math

import jax
import jax.numpy as jnp
import numpy as np
from jax import lax
from jax.experimental import pallas as pl
from jax.experimental.pallas import tpu as pltpu

F32 = jnp.float32
BF16 = jnp.bfloat16

D_MODEL = 1024
GRID_W = 64
N_HEADS = 8
QK_NOPE = 64
QK_ROPE = 32
V_DIM = 64
Q_RANK = 256
KV_RANK = 128
MLA_IN = Q_RANK + KV_RANK + QK_ROPE
CONV_W = 512
HEAD_COLS = 512
D_FF = 4096
ROPE_THETA = 10000.0
EPS = 1e-6
ATTN_SCALE = 1.0 / math.sqrt(QK_NOPE + QK_ROPE)
LOG2_E = 1.0 / math.log(2.0)
EXP2_SCALE = ATTN_SCALE * LOG2_E
N_DEV = 8
LANES = 128

ADAM_LR, ADAM_B1, ADAM_B2, ADAM_EPS, ADAM_WD, ADAM_STEP = 0.001, 0.9, 0.999, 1e-08, 0.01, 10

ROW_TILE = 256
VMEM_BIG = 60 * 1024 * 1024


def _params(sem=None, vmem=None):
    return pltpu.CompilerParams(dimension_semantics=sem, vmem_limit_bytes=vmem)


def _pick(n, prefs):
    for p in prefs:
        if n % p == 0:
            return p
    return n


def _my_index():
    return 4 * lax.axis_index("x") + 2 * lax.axis_index("y") + lax.axis_index("c")


def _all_gather(arrays, name, in_vmem):
    space = pltpu.VMEM if in_vmem else pl.ANY
    n = len(arrays)

    def body(*refs):
        x_refs, out_refs = refs[:n], refs[n:2 * n]
        send_sems, recv_sems, local_sems = refs[2 * n:]
        x, y, c = lax.axis_index("x"), lax.axis_index("y"), lax.axis_index("c")
        me, sibling = (x, y, c), (x, y, 1 - c)
        chips = [(1 - x, y), (x, 1 - y), (1 - x, 1 - y)]

        def slot(a, px, py, pc):
            return out_refs[a].at[4 * px + 2 * py + pc]

        def copy(a, k, block, to, src=None):
            return pltpu.make_async_remote_copy(
                src_ref=slot(a, *block) if src is None else src, dst_ref=slot(a, *block),
                send_sem=send_sems.at[7 * a + k], recv_sem=recv_sems.at[7 * a + k],
                device_id=to, device_id_type=pl.DeviceIdType.MESH)

        mine = [pltpu.make_async_copy(x_refs[a], slot(a, *me), local_sems.at[a]) for a in range(n)]
        for cp in mine:
            cp.start()
        started = []
        for a in range(n):
            first = [copy(a, 0, me, sibling, src=x_refs[a])]
            first += [copy(a, 1 + j, me, (*chip, c), src=x_refs[a]) for j, chip in enumerate(chips)]
            for cp in first:
                cp.start()
            started += first
        for a in range(n):
            for j, chip in enumerate(chips):
                copy(a, 1 + j, (*chip, c), me).wait_recv()
                passed = copy(a, 4 + j, (*chip, c), sibling)
                passed.start()
                started.append(passed)
        for a in range(n):
            copy(a, 0, sibling, me).wait_recv()
            for j, chip in enumerate(chips):
                copy(a, 4 + j, (*chip, 1 - c), me).wait_recv()
        for cp in started:
            cp.wait_send()
        for cp in mine:
            cp.wait()

    outs = pl.pallas_call(
        body, name=name,
        out_shape=tuple(jax.ShapeDtypeStruct((N_DEV,) + a.shape, a.dtype) for a in arrays),
        in_specs=[pl.BlockSpec(memory_space=space)] * n,
        out_specs=tuple(pl.BlockSpec(memory_space=space) for _ in arrays),
        scratch_shapes=[pltpu.SemaphoreType.DMA((7 * n,)), pltpu.SemaphoreType.DMA((7 * n,)),
                        pltpu.SemaphoreType.DMA((n,))],
    )(*arrays)
    return list(outs)


class _Riding:
    def __init__(self, kind, arrays):
        self.kind, self.arrays, self.n = kind, list(arrays), len(arrays)
        lead = (N_DEV,) if kind == "gather" else ()
        self.out_shape = [jax.ShapeDtypeStruct(lead + a.shape, a.dtype) for a in self.arrays]
        self.specs = [pl.BlockSpec(memory_space=pl.ANY)] * self.n
        self.scratch = [pltpu.SemaphoreType.DMA((7 * self.n,)), pltpu.SemaphoreType.DMA((7 * self.n,)),
                        pltpu.SemaphoreType.DMA((self.n,))]

    def copies(self, x_refs, y_refs, send_sems, recv_sems, local_sems):
        x, y, c = lax.axis_index("x"), lax.axis_index("y"), lax.axis_index("c")
        me = 4 * x + 2 * y + c
        local, sends, landings = [], [], []
        for a in range(self.n):
            src_mine = x_refs[a] if self.kind == "gather" else x_refs[a].at[me]
            local.append(pltpu.make_async_copy(src_mine, y_refs[a].at[me], local_sems.at[a]))
            for k in range(1, N_DEV):
                peer = (1 - x if k & 4 else x, 1 - y if k & 2 else y, 1 - c if k & 1 else c)
                pid = 4 * peer[0] + 2 * peer[1] + peer[2]
                src = x_refs[a] if self.kind == "gather" else x_refs[a].at[pid]
                for dst, out in ((me, sends), (pid, landings)):
                    out.append(pltpu.make_async_remote_copy(
                        src_ref=src, dst_ref=y_refs[a].at[dst],
                        send_sem=send_sems.at[7 * a + k - 1], recv_sem=recv_sems.at[7 * a + k - 1],
                        device_id=peer, device_id_type=pl.DeviceIdType.MESH))
        return local, sends, landings

    def run(self, first, last, x_refs, y_refs, sems):
        if self.n == 0:
            return None
        local, sends, landings = self.copies(x_refs, y_refs, *sems)

        @pl.when(first)
        def _():
            for cp in local + sends:
                cp.start()

        return local, sends, landings, last

    @staticmethod
    def finish(state):
        if state is None:
            return
        local, sends, landings, last = state

        @pl.when(last)
        def _():
            for cp in landings:
                cp.wait_recv()
            for cp in sends:
                cp.wait_send()
            for cp in local:
                cp.wait()


_DIMS = {"nn": (((1,), (0,)), ((), ())), "nt": (((1,), (1,)), ((), ())), "tn": (((0,), (0,)), ((), ()))}
NT_DIMS = _DIMS["nt"]
TN_DIMS = _DIMS["tn"]


def _swap8(x):
    lane = lax.broadcasted_iota(jnp.int32, x.shape, 1)
    return jnp.where((lane & 15) < 8, pltpu.roll(x, LANES - 8, 1), pltpu.roll(x, 8, 1))


def _rope(x, cos, sgn, bwd):
    return x * cos + (_swap8(x * sgn) if bwd else _swap8(x) * sgn)


def _matmul(a, b, *, mode, name, out_dtype=F32, tm=512, tn=512, tk=512, m=None, k=None,
            epilogue=None, extra=(), slots=None):
    if mode == "nn":
        m = a.shape[0] if m is None else m
        k = a.shape[1]
        n = N_DEV * b.shape[2] if slots == "b_cols" else b.shape[1]
    elif mode == "nt":
        m = a.shape[0] if m is None else m
        k = a.shape[1]
        n = b.shape[0]
    else:
        k = a.shape[0] if k is None else k
        m, n = a.shape[1], b.shape[1]
    tm, tn, tk = min(tm, m), min(tn, n), min(tk, k)
    if slots == "b_cols":
        tn = b.shape[2]
    if slots == "out":
        tn = n // N_DEV
    assert m % tm == 0 and n % tn == 0 and k % tk == 0, (name, m, n, k, tm, tn, tk)
    nk = k // tk
    dims = _DIMS[mode]
    a_spec = (pl.BlockSpec((tk, tm), lambda i, j, kk: (kk, i)) if mode == "tn"
              else pl.BlockSpec((tm, tk), lambda i, j, kk: (i, kk)))
    if slots == "b_cols":
        b_spec = pl.BlockSpec((None, tk, tn), lambda i, j, kk: (j, kk, 0))
    elif mode == "nt":
        b_spec = pl.BlockSpec((tn, tk), lambda i, j, kk: (j, kk))
    else:
        b_spec = pl.BlockSpec((tk, tn), lambda i, j, kk: (kk, j))
    tile = pl.BlockSpec((tm, tn), lambda i, j, kk: (i, j))
    if slots == "out":
        o_spec = pl.BlockSpec((None, tm, tn), lambda i, j, kk: (j, i, 0))
        o_shape = (N_DEV, m, tn)
    else:
        o_spec, o_shape = tile, (m, n)
    in_specs, args = [a_spec, b_spec], [a, b]
    if epilogue == "drelu2":
        in_specs.append(tile)
    args += list(extra)
    if epilogue == "relu2":
        out_shape = (jax.ShapeDtypeStruct(o_shape, BF16), jax.ShapeDtypeStruct(o_shape, BF16))
        out_specs = (o_spec, o_spec)
    else:
        out_shape = jax.ShapeDtypeStruct(o_shape, out_dtype)
        out_specs = o_spec
    n_in = len(args)
    n_out = 2 if epilogue == "relu2" else 1

    def body(*refs):
        a_ref, b_ref = refs[0], refs[1]
        outs = refs[n_in:n_in + n_out]
        part = lax.dot_general(a_ref[...], b_ref[...], dims, preferred_element_type=F32)

        def finish(acc):
            if epilogue == "relu2":
                outs[0][...] = acc.astype(BF16)
                r = jnp.maximum(acc, 0.0)
                outs[1][...] = (r * r).astype(BF16)
            elif epilogue == "drelu2":
                u = refs[2][...].astype(F32)
                outs[0][...] = (acc * (2.0 * jnp.maximum(u, 0.0))).astype(out_dtype)
            else:
                outs[0][...] = acc.astype(out_dtype)

        if nk == 1:
            finish(part)
        else:
            acc_ref = refs[n_in + n_out]
            kk = pl.program_id(2)

            @pl.when(kk == 0)
            def _():
                acc_ref[...] = part

            @pl.when(kk > 0)
            def _():
                acc_ref[...] += part

            @pl.when(kk == nk - 1)
            def _():
                finish(acc_ref[...])

    return pl.pallas_call(
        body, name=name, grid=(m // tm, n // tn, nk),
        out_shape=out_shape, in_specs=in_specs, out_specs=out_specs,
        scratch_shapes=[pltpu.VMEM((tm, tn), F32)] if nk > 1 else [],
        compiler_params=_params(("parallel", "parallel", "arbitrary"), VMEM_BIG),
    )(*args)


def _rstd(x):
    return lax.rsqrt(jnp.mean(x * x, axis=1, keepdims=True) + EPS)


def _norm_bwd(dxn, xn, r):
    return r * (dxn - xn * jnp.mean(dxn * xn, axis=1, keepdims=True))


def _vec(col):
    return pl.BlockSpec((1, D_MODEL), lambda i: (0, col))


def _matmul_rows(a, b, epi, *, mode, name, tm, tk, rows=(), vecs=(), out_dtypes=(), sums=False, slots=None,
                 riding=None):
    m, k = a.shape
    n = D_MODEL
    tm, tk = min(tm, m), min(tk, k)
    riding = riding or _Riding("gather", [])
    group = 1
    if slots == "b_contract":
        group = max(1, tk // b.shape[2])
        tk = group * b.shape[2]
        b_spec = pl.BlockSpec((group, n, tk // group), lambda i, kk: (kk, 0, 0))
    elif mode == "nt":
        b_spec = pl.BlockSpec((n, tk), lambda i, kk: (0, kk))
    else:
        b_spec = pl.BlockSpec((tk, n), lambda i, kk: (kk, 0))
    assert m % tm == 0 and k % tk == 0, (name, m, k, tm, tk)
    ni, nk = m // tm, k // tk
    dims = _DIMS[mode]
    tile = pl.BlockSpec((tm, n), lambda i, kk: (i, 0))
    in_specs = [pl.BlockSpec((tm, tk), lambda i, kk: (i, kk)), b_spec] + [tile] * len(rows)
    in_specs += [pl.BlockSpec((1, n), lambda i, kk, col=col: (0, col)) for _, col in vecs]
    args = [a, b, *rows, *[v for v, _ in vecs]]
    out_shape = [jax.ShapeDtypeStruct((m, n), dt) for dt in out_dtypes]
    out_specs = [tile] * len(out_dtypes)
    if sums:
        out_shape.append(jax.ShapeDtypeStruct((8, n), F32))
        out_specs.append(pl.BlockSpec((8, n), lambda i, kk: (0, 0)))
    n_rows, n_vecs, n_outs, nr = len(rows), len(vecs), len(out_dtypes), riding.n
    n_in = 2 + n_rows + n_vecs

    def body(*refs):
        a_ref, b_ref = refs[0], refs[1]
        row_refs = refs[2:2 + n_rows]
        vec_refs = refs[2 + n_rows:n_in]
        x_refs = refs[n_in:n_in + nr]
        out_refs = refs[n_in + nr:n_in + nr + n_outs]
        pos = n_in + nr + n_outs
        sums_ref = refs[pos] if sums else None
        pos += 1 if sums else 0
        y_refs = refs[pos:pos + nr]
        pos += nr
        acc_ref = refs[pos] if nk > 1 else None
        sem_refs = refs[pos + (1 if nk > 1 else 0):]
        i, kk = pl.program_id(0), pl.program_id(1)
        state = riding.run((i == 0) & (kk == 0), (i == ni - 1) & (kk == nk - 1), x_refs, y_refs, sem_refs)
        if slots == "b_contract":
            c = tk // group
            part = lax.dot_general(a_ref[:, 0:c], b_ref[0], dims, preferred_element_type=F32)
            for u in range(1, group):
                part = part + lax.dot_general(a_ref[:, u * c:(u + 1) * c], b_ref[u], dims, preferred_element_type=F32)
        else:
            part = lax.dot_general(a_ref[...], b_ref[...], dims, preferred_element_type=F32)

        def finish(acc):
            nsub = tm // ROW_TILE
            for r in range(nsub):
                blk = pl.ds(r * ROW_TILE, ROW_TILE)
                epi(acc[r * ROW_TILE:(r + 1) * ROW_TILE], [ref.at[blk] for ref in row_refs], vec_refs,
                    [ref.at[blk] for ref in out_refs], sums_ref,
                    (i == 0) if r == 0 else None, (i == ni - 1) if r == nsub - 1 else None)

        if nk == 1:
            finish(part)
        else:
            @pl.when(kk == 0)
            def _():
                acc_ref[...] = part

            @pl.when(kk > 0)
            def _():
                acc_ref[...] += part

            @pl.when(kk == nk - 1)
            def _():
                finish(acc_ref)

        riding.finish(state)

    outs = pl.pallas_call(
        body, name=name, grid=(ni, nk),
        out_shape=(*out_shape, *riding.out_shape),
        in_specs=[*in_specs, *riding.specs], out_specs=(*out_specs, *riding.specs),
        scratch_shapes=([pltpu.VMEM((tm, n), F32)] if nk > 1 else []) + (riding.scratch if nr else []),
        compiler_params=_params(("arbitrary", "arbitrary"), VMEM_BIG),
    )(*args, *riding.arrays)
    n_own = len(out_shape)
    return list(outs[:n_own]), list(outs[n_own:])


def _zero_sums_at_start(sums_ref, first):
    if first is not None:
        @pl.when(first)
        def _():
            sums_ref[...] = jnp.zeros_like(sums_ref)


def _epi_resid_modulate(acc, rows, vecs, outs, sums_ref, first, last):
    (x_ref,), (g_ref, sh_ref, sc_ref) = rows, vecs
    x1 = x_ref[...] + g_ref[...] * acc
    outs[0][...] = acc
    outs[1][...] = x1
    outs[2][...] = (x1 * _rstd(x1) * (1.0 + sc_ref[...]) + sh_ref[...]).astype(BF16)


def _epi_final(acc, rows, vecs, outs, sums_ref, first, last):
    (x1_ref, t_ref), (g_ref, gf_ref) = rows, vecs
    d = acc.shape[1]
    x2 = x1_ref[...] + g_ref[...] * acc
    r = _rstd(x2)
    xn = x2 * r
    err = xn * gf_ref[...] - t_ref[...]
    dy = err * (1.0 / d)
    dx2 = _norm_bwd(dy * gf_ref[...], xn, r)
    outs[0][...] = dx2
    outs[1][...] = (dx2 * g_ref[...]).astype(BF16)
    _zero_sums_at_start(sums_ref, first)
    sums_ref[0:1, :] += jnp.sum(dy * xn, axis=0, keepdims=True)
    sums_ref[1:2, :] += jnp.sum(dx2 * acc, axis=0, keepdims=True)
    sums_ref[2:3, :] += jnp.sum(err * err, axis=0, keepdims=True)

    if last is not None:
        @pl.when(last)
        def _():
            tot = jnp.sum(sums_ref[2:3, :], axis=1, keepdims=True) * (0.5 / d)
            sums_ref[3:4, :] = jnp.broadcast_to(tot, (1, d))


def _epi_modulate2_bwd(acc, rows, vecs, outs, sums_ref, first, last):
    (x_ref, dres_ref, o_ref), (sc_ref, g_ref) = rows, vecs
    x = x_ref[...]
    r = _rstd(x)
    xn = x * r
    dx = dres_ref[...] + _norm_bwd(acc * (1.0 + sc_ref[...]), xn, r)
    outs[0][...] = dx
    outs[1][...] = (dx * g_ref[...]).astype(BF16)
    _zero_sums_at_start(sums_ref, first)
    sums_ref[0:1, :] += jnp.sum(acc * xn, axis=0, keepdims=True)
    sums_ref[1:2, :] += jnp.sum(acc, axis=0, keepdims=True)
    sums_ref[2:3, :] += jnp.sum(dx * o_ref[...], axis=0, keepdims=True)


def _epi_modulate1_bwd(acc, rows, vecs, outs, sums_ref, first, last):
    (add_ref, x_ref, dres_ref), (sc_ref,) = rows, vecs
    dh = acc + add_ref[...]
    x = x_ref[...]
    r = _rstd(x)
    xn = x * r
    outs[0][...] = dres_ref[...] + _norm_bwd(dh * (1.0 + sc_ref[...]), xn, r)
    _zero_sums_at_start(sums_ref, first)
    sums_ref[0:1, :] += jnp.sum(dh * xn, axis=0, keepdims=True)
    sums_ref[1:2, :] += jnp.sum(dh, axis=0, keepdims=True)


def _modulate_all(x, ctx, mod, mod_ctx, name):
    s, d = x.shape
    t = s + ctx.shape[0]
    ns = s // ROW_TILE
    nc = ctx.shape[0] // ROW_TILE

    def body(x_ref, c_ref, sh_ref, sc_ref, shc_ref, scc_ref, h_ref):
        i = pl.program_id(0)

        @pl.when(i < ns)
        def _():
            v = x_ref[...]
            h_ref[...] = (v * _rstd(v) * (1.0 + sc_ref[...]) + sh_ref[...]).astype(BF16)

        @pl.when(i >= ns)
        def _():
            v = c_ref[...]
            h_ref[...] = (v * _rstd(v) * (1.0 + scc_ref[...]) + shc_ref[...]).astype(BF16)

    return pl.pallas_call(
        body, name=name, grid=(ns + nc,),
        out_shape=jax.ShapeDtypeStruct((t, d), BF16),
        in_specs=[pl.BlockSpec((ROW_TILE, d), lambda i: (jnp.minimum(i, ns - 1), 0)),
                  pl.BlockSpec((ROW_TILE, d), lambda i: (jnp.maximum(i - ns, 0), 0)),
                  _vec(0), _vec(1), _vec(0), _vec(1)],
        out_specs=pl.BlockSpec((ROW_TILE, d), lambda i: (i, 0)),
        compiler_params=_params(("arbitrary",)),
    )(x, ctx, mod, mod, mod_ctx, mod_ctx)


def _modulate_sums(dh, row_off, xsrc):
    s, d = xsrc.shape

    def body(dh_ref, x_ref, sums_ref):
        i = pl.program_id(0)
        x = x_ref[...]
        dhv = dh_ref[...]

        @pl.when(i == 0)
        def _():
            sums_ref[...] = jnp.zeros_like(sums_ref)

        sums_ref[0:1, :] += jnp.sum(dhv * (x * _rstd(x)), axis=0, keepdims=True)
        sums_ref[1:2, :] += jnp.sum(dhv, axis=0, keepdims=True)

    return pl.pallas_call(
        body, name="modulate1_ctx_bwd", grid=(s // ROW_TILE,),
        out_shape=jax.ShapeDtypeStruct((8, d), F32),
        in_specs=[pl.BlockSpec((ROW_TILE, d), lambda i: (i + row_off, 0)), pl.BlockSpec((ROW_TILE, d), lambda i: (i, 0))],
        out_specs=pl.BlockSpec((8, d), lambda i: (0, 0)),
        compiler_params=_params(("arbitrary",)),
    )(dh, xsrc)


def _head_fwd(h_all, win_head, wq, wk, q_gain, kv_gain, cos, sgn, tm, name):
    t, d = h_all.shape
    nq, nkv = wq.shape[1], wk.shape[1]

    def body(h_ref, wi_ref, wq_ref, wk_ref, qg_ref, kg_ref, c_ref, s_ref, z_ref, cq_ref, kvin_ref, qf_ref, kv_ref):
        z = lax.dot_general(h_ref[...], wi_ref[...], NT_DIMS, preferred_element_type=F32)
        z_ref[...] = z
        cos, sgn = c_ref[...], s_ref[...]
        zq = z[:, 0:Q_RANK]
        cq = (zq * _rstd(zq) * qg_ref[...]).astype(BF16)
        cq_ref[...] = cq
        zk = z[:, Q_RANK:Q_RANK + KV_RANK]
        kv_in = jnp.concatenate([(zk * _rstd(zk) * kg_ref[...]).astype(BF16),
                                 _rope(z[:, Q_RANK + KV_RANK:HEAD_COLS], cos, sgn, False).astype(BF16)], axis=1)
        kvin_ref[...] = kv_in
        q = jnp.dot(cq, wq_ref[...], preferred_element_type=F32)
        for h in range(nq // LANES):
            sl = slice(h * LANES, (h + 1) * LANES)
            qf_ref[:, sl] = _rope(q[:, sl], cos, sgn, False).astype(BF16)
        kv_ref[...] = jnp.dot(kv_in, wk_ref[...], preferred_element_type=F32).astype(BF16)

    def row(w):
        return pl.BlockSpec((tm, w), lambda i: (i, 0))

    def whole(a):
        return pl.BlockSpec(a.shape, lambda i: (0, 0))

    return pl.pallas_call(
        body, name=name, grid=(t // tm,),
        out_shape=(jax.ShapeDtypeStruct((t, HEAD_COLS), F32), jax.ShapeDtypeStruct((t, Q_RANK), BF16),
                   jax.ShapeDtypeStruct((t, KV_RANK + LANES), BF16), jax.ShapeDtypeStruct((t, nq), BF16),
                   jax.ShapeDtypeStruct((t, nkv), BF16)),
        in_specs=[row(d), whole(win_head), whole(wq), whole(wk), whole(q_gain), whole(kv_gain), row(LANES), row(LANES)],
        out_specs=(row(HEAD_COLS), row(Q_RANK), row(KV_RANK + LANES), row(nq), row(nkv)),
        compiler_params=_params(("parallel",), VMEM_BIG),
    )(h_all, win_head, wq, wk, q_gain, kv_gain, cos, sgn)


def _head_bwd(dq, dk, dv, z, wq, wk_k, wk_v, win_head, q_gain, kv_gain, cos, sgn, s, name):
    t = z.shape[0]
    ns = s // ROW_TILE

    def body(dq_ref, dk_ref, dv_ref, z_ref, wq_ref, wkk_ref, wkv_ref, wi_ref, qg_ref, kg_ref, c_ref, s_ref,
             dz_ref, dh_ref, sums_ref):
        i = pl.program_id(0)

        @pl.when(i == 0)
        def _():
            sums_ref[...] = jnp.zeros_like(sums_ref)

        @pl.when(i < ns)
        def _():
            dc = lax.dot_general(dq_ref[...], wq_ref[...], NT_DIMS, preferred_element_type=F32)
            zq = z_ref[:, 0:Q_RANK]
            r = _rstd(zq)
            zn = zq * r
            sums_ref[0:1, :] += jnp.sum(dc * zn, axis=0, keepdims=True)
            dz_ref[:, 0:Q_RANK] = _norm_bwd(dc * qg_ref[...], zn, r).astype(BF16)

        @pl.when(i >= ns)
        def _():
            dz_ref[:, 0:Q_RANK] = jnp.zeros((ROW_TILE, Q_RANK), BF16)

        dkv = (lax.dot_general(dk_ref[...], wkk_ref[...], NT_DIMS, preferred_element_type=F32)
               + lax.dot_general(dv_ref[...], wkv_ref[...], NT_DIMS, preferred_element_type=F32))
        zk = z_ref[:, Q_RANK:Q_RANK + KV_RANK]
        r = _rstd(zk)
        zn = zk * r
        dc = dkv[:, 0:KV_RANK]
        sums_ref[1:2, 0:KV_RANK] += jnp.sum(dc * zn, axis=0, keepdims=True)
        dz_ref[:, Q_RANK:Q_RANK + KV_RANK] = _norm_bwd(dc * kg_ref[...], zn, r).astype(BF16)
        dz_ref[:, Q_RANK + KV_RANK:HEAD_COLS] = _rope(dkv[:, KV_RANK:KV_RANK + LANES], c_ref[...], s_ref[...],
                                                       True).astype(BF16)
        dh_ref[...] = jnp.dot(dz_ref[...], wi_ref[...], preferred_element_type=F32)

    def row(w):
        return pl.BlockSpec((ROW_TILE, w), lambda i: (i, 0))

    def whole(a):
        return pl.BlockSpec(a.shape, lambda i: (0, 0))

    return pl.pallas_call(
        body, name=name, grid=(t // ROW_TILE,),
        out_shape=(jax.ShapeDtypeStruct((t, HEAD_COLS), BF16), jax.ShapeDtypeStruct((t, D_MODEL), F32),
                   jax.ShapeDtypeStruct((8, Q_RANK), F32)),
        in_specs=[pl.BlockSpec((ROW_TILE, dq.shape[1]), lambda i: (jnp.minimum(i, ns - 1), 0)),
                  row(dk.shape[1]), row(dv.shape[1]), row(HEAD_COLS), whole(wq), whole(wk_k), whole(wk_v),
                  whole(win_head), whole(q_gain), whole(kv_gain), row(LANES), row(LANES)],
        out_specs=(row(HEAD_COLS), row(D_MODEL), pl.BlockSpec((8, Q_RANK), lambda i: (0, 0))),
        compiler_params=_params(("arbitrary",), VMEM_BIG),
    )(dq, dk, dv, z, wq, wk_k, wk_v, win_head, q_gain, kv_gain, cos, sgn)


def _shift_rows(u, s):
    rowi = lax.broadcasted_iota(jnp.int32, u.shape, 0)
    prev = jnp.where(rowi == 0, 0.0, pltpu.roll(u, 1, 0))
    nxt = jnp.where(rowi == s - 1, 0.0, pltpu.roll(u, s - 1, 0))
    return prev, nxt


def _conv_fwd(z_conv, cw, a_cat, name):
    s = z_conv.shape[0]

    def body(z_ref, w_ref, a_in_ref, o_ref):
        del a_in_ref
        gb, gc, xv = z_ref[:, 0:LANES], z_ref[:, LANES:2 * LANES], z_ref[:, 2 * LANES:3 * LANES]
        u = gc * xv
        prev, nxt = _shift_rows(u, s)
        y = w_ref[0:1, :] * prev + w_ref[1:2, :] * u + w_ref[2:3, :] * nxt
        o_ref[...] = (gb * y).astype(BF16)

    return pl.pallas_call(
        body, name=name, grid=(CONV_W // LANES,),
        out_shape=jax.ShapeDtypeStruct(a_cat.shape, a_cat.dtype),
        in_specs=[pl.BlockSpec((s, 3 * LANES), lambda j: (0, j)), pl.BlockSpec((3, LANES), lambda j: (0, j)),
                  pl.BlockSpec(memory_space=pl.ANY)],
        out_specs=pl.BlockSpec((s, LANES), lambda j: (0, 4 + j)),
        input_output_aliases={2: 0},
        compiler_params=_params(("parallel",), VMEM_BIG),
    )(z_conv, cw, a_cat)


def _conv_bwd(z_conv, cw, da, name):
    s = z_conv.shape[0]

    def body(z_ref, w_ref, da_ref, dz_ref, dw_ref):
        gb, gc, xv = z_ref[:, 0:LANES], z_ref[:, LANES:2 * LANES], z_ref[:, 2 * LANES:3 * LANES]
        u = gc * xv
        prev, nxt = _shift_rows(u, s)
        dcv = da_ref[...]
        dz_ref[:, 0:LANES] = (dcv * (w_ref[0:1, :] * prev + w_ref[1:2, :] * u + w_ref[2:3, :] * nxt)).astype(BF16)
        dy = dcv * gb
        dw_ref[0:1, :] = jnp.sum(dy * prev, axis=0, keepdims=True)
        dw_ref[1:2, :] = jnp.sum(dy * u, axis=0, keepdims=True)
        dw_ref[2:3, :] = jnp.sum(dy * nxt, axis=0, keepdims=True)
        dyp, dyn = _shift_rows(dy, s)
        du = w_ref[0:1, :] * dyn + w_ref[1:2, :] * dy + w_ref[2:3, :] * dyp
        dz_ref[:, LANES:2 * LANES] = (du * xv).astype(BF16)
        dz_ref[:, 2 * LANES:3 * LANES] = (du * gc).astype(BF16)

    blk = pl.BlockSpec((s, 3 * LANES), lambda j: (0, j))
    cws = pl.BlockSpec((3, LANES), lambda j: (0, j))
    return pl.pallas_call(
        body, name=name, grid=(CONV_W // LANES,),
        out_shape=(jax.ShapeDtypeStruct(z_conv.shape, BF16), jax.ShapeDtypeStruct((3, CONV_W), F32)),
        in_specs=[blk, cws, pl.BlockSpec((s, LANES), lambda j: (0, 4 + j))], out_specs=(blk, cws),
        compiler_params=_params(("parallel",), VMEM_BIG),
    )(z_conv, cw, da)


ATT_TQ = 256
ATT_Q_STEP = 1024
ATT_TQ_BWD = 512


def _head_mask(shape, hh):
    lane = lax.broadcasted_iota(jnp.int32, shape, 1)
    return (lane >= hh * V_DIM) & (lane < (hh + 1) * V_DIM)


def _attn_fwd(qf, kv, s, riding, name):
    t = kv.shape[0]
    step = min(ATT_Q_STEP, s)
    nq = s // step
    nr = riding.n

    def body(*refs):
        q_ref, k_ref, v_ref = refs[:3]
        o_ref, ob_ref, st_ref = refs[3 + nr:6 + nr]
        p, i = pl.program_id(0), pl.program_id(1)
        state = riding.run((p == 0) & (i == 0), (p == N_HEADS // 2 - 1) & (i == nq - 1),
                           refs[3:3 + nr], refs[6 + nr:6 + 2 * nr], refs[6 + 2 * nr:])
        v = v_ref[...]
        vlane = lax.broadcasted_iota(jnp.int32, v.shape, 1)
        one_lane = [(1 - hh) * V_DIM for hh in range(2)]
        vm = [jnp.where(_head_mask(v.shape, hh), v, jnp.where(vlane == one_lane[hh], 1.0, 0.0).astype(BF16))
              for hh in range(2)]

        def block(r, carry):
            rows = pl.ds(pl.multiple_of(r * ATT_TQ, ATT_TQ), ATT_TQ)
            olane = lax.broadcasted_iota(jnp.int32, (ATT_TQ, LANES), 1)
            acc = jnp.zeros((ATT_TQ, LANES), F32)
            stat = jnp.zeros((ATT_TQ, LANES), F32)
            for hh in range(2):
                sl = slice(hh * LANES, (hh + 1) * LANES)
                sc = lax.dot_general(q_ref[rows, sl], k_ref[:, sl], NT_DIMS, preferred_element_type=F32)
                mx = jnp.max(sc, axis=1, keepdims=True)
                e = jnp.exp2((sc - mx) * EXP2_SCALE).astype(BF16)
                res = jnp.dot(e, vm[hh], preferred_element_type=F32)
                den = jnp.sum(jnp.where(olane == one_lane[hh], res, 0.0), axis=1, keepdims=True)
                acc = acc + jnp.where(_head_mask(res.shape, hh), res * (1.0 / den), 0.0)
                stat = stat + jnp.where(olane == hh, mx * EXP2_SCALE + jnp.log(den) * LOG2_E, 0.0)
            o_ref[rows, :] = acc
            ob_ref[rows, :] = acc.astype(BF16)
            st_ref[:, rows] = stat.T[0:8, :]
            return carry

        lax.fori_loop(0, step // ATT_TQ, block, 0)
        riding.finish(state)

    o_spec = pl.BlockSpec((step, LANES), lambda p, i: (i, p))
    outs = pl.pallas_call(
        body, name=name, grid=(N_HEADS // 2, nq),
        out_shape=(jax.ShapeDtypeStruct((s, N_HEADS * V_DIM), F32),
                   jax.ShapeDtypeStruct((s, D_MODEL), BF16),
                   jax.ShapeDtypeStruct((N_HEADS // 2 * 8, s), F32), *riding.out_shape),
        in_specs=[pl.BlockSpec((step, 2 * LANES), lambda p, i: (i, p)),
                  pl.BlockSpec((t, 2 * LANES), lambda p, i: (0, p)),
                  pl.BlockSpec((t, LANES), lambda p, i: (0, N_HEADS + p)), *riding.specs],
        out_specs=(o_spec, o_spec, pl.BlockSpec((8, step), lambda p, i: (p, i)), *riding.specs),
        scratch_shapes=riding.scratch,
        compiler_params=_params(("arbitrary", "arbitrary"), VMEM_BIG),
    )(qf, kv, kv, *riding.arrays)
    return outs[0], outs[1], outs[2], list(outs[3:])


def _attn_bwd(qf, kv, o, da, stats, cos, sgn, riding, name):
    s, t = o.shape[0], kv.shape[0]
    ATT_TQ = ATT_TQ_BWD
    nq = s // ATT_TQ
    nr = riding.n

    def body(*refs):
        q_ref, k_ref, v_ref, o_ref, do_ref, st_ref, c_ref, s_ref = refs[:8]
        dq_ref, dk_ref, dv_ref = refs[8 + nr:11 + nr]
        dk_acc, dv_acc = refs[11 + 2 * nr:13 + 2 * nr]
        p, i = pl.program_id(0), pl.program_id(1)
        state = riding.run((p == 0) & (i == 0), (p == N_HEADS // 2 - 1) & (i == nq - 1),
                           refs[8:8 + nr], refs[11 + nr:11 + 2 * nr], refs[13 + 2 * nr:])

        @pl.when(i == 0)
        def _():
            dk_acc[...] = jnp.zeros_like(dk_acc)
            dv_acc[...] = jnp.zeros_like(dv_acc)

        v = v_ref[...]
        do = do_ref[...]
        od = do * o_ref[...]
        ones = jnp.ones((8, LANES), F32)
        for hh in range(2):
            sl = slice(hh * LANES, (hh + 1) * LANES)
            q, k = q_ref[:, sl], k_ref[:, sl]
            mask = _head_mask(do.shape, hh)
            dom = jnp.where(mask, do, 0.0).astype(BF16)
            delta = lax.dot_general(ones, jnp.where(mask, od, 0.0), NT_DIMS, preferred_element_type=F32,
                                    precision=lax.Precision.HIGHEST)[0:1, :]
            st = lax.dot_general(k, q, NT_DIMS, preferred_element_type=F32)
            pt = jnp.exp2(st * EXP2_SCALE - st_ref[hh:hh + 1, :]).astype(BF16)
            dpt = lax.dot_general(v, dom, NT_DIMS, preferred_element_type=F32)
            dst = (pt.astype(F32) * (dpt - delta)).astype(BF16)
            dv_acc[...] += jnp.dot(pt, dom, preferred_element_type=F32)
            dk_acc[:, sl] += jnp.dot(dst, q, preferred_element_type=F32)
            dq = lax.dot_general(dst, k, TN_DIMS, preferred_element_type=F32) * ATTN_SCALE
            dq_ref[:, sl] = _rope(dq, c_ref[...], s_ref[...], True).astype(BF16)

        @pl.when(i == nq - 1)
        def _():
            dk_ref[...] = (dk_acc[...] * ATTN_SCALE).astype(BF16)
            dv_ref[...] = dv_acc[...].astype(BF16)

        riding.finish(state)

    o_spec = pl.BlockSpec((ATT_TQ, LANES), lambda p, i: (i, p))
    tab = pl.BlockSpec((ATT_TQ, LANES), lambda p, i: (i, 0))
    outs = pl.pallas_call(
        body, name=name, grid=(N_HEADS // 2, nq),
        out_shape=(jax.ShapeDtypeStruct((s, N_HEADS * LANES), BF16),
                   jax.ShapeDtypeStruct((t, N_HEADS * LANES), BF16),
                   jax.ShapeDtypeStruct((t, N_HEADS * V_DIM), BF16), *riding.out_shape),
        in_specs=[pl.BlockSpec((ATT_TQ, 2 * LANES), lambda p, i: (i, p)),
                  pl.BlockSpec((t, 2 * LANES), lambda p, i: (0, p)),
                  pl.BlockSpec((t, LANES), lambda p, i: (0, N_HEADS + p)),
                  o_spec, o_spec,
                  pl.BlockSpec((8, ATT_TQ), lambda p, i: (p, i)), tab, tab, *riding.specs],
        out_specs=(pl.BlockSpec((ATT_TQ, 2 * LANES), lambda p, i: (i, p)),
                   pl.BlockSpec((t, 2 * LANES), lambda p, i: (0, p)),
                   pl.BlockSpec((t, LANES), lambda p, i: (0, p)), *riding.specs),
        scratch_shapes=[pltpu.VMEM((t, 2 * LANES), F32), pltpu.VMEM((t, LANES), F32), *riding.scratch],
        compiler_params=_params(("arbitrary", "arbitrary"), VMEM_BIG),
    )(qf, kv, kv, o, da, stats, cos, sgn, *riding.arrays)
    return outs[0], outs[1], outs[2], list(outs[3:])


def _silu(x):
    return x * (1.0 / (1.0 + jnp.exp(-x)))


def _adaln_fwd(a, w, b, name):
    def body(a_ref, w_ref, b_ref, o_ref):
        o_ref[...] = jnp.dot(_silu(a_ref[...]), w_ref[...], preferred_element_type=F32,
                             precision=lax.Precision.HIGHEST) + b_ref[...]

    return pl.pallas_call(
        body, name=name, out_shape=jax.ShapeDtypeStruct((a.shape[0], w.shape[1]), F32),
        compiler_params=_params(None, VMEM_BIG),
    )(a, w, b)


def _adaln_bwd(a_t, w, d_ex, d_ctx, d_all, name):
    def body(at_ref, w_ref, dex_ref, dctx_ref, dall_ref, gw_ref, dsil_ref, dsum_ref):
        sil_t = _silu(at_ref[...])
        dctx = dctx_ref[...]
        row = dctx[0:1, :]
        for j in range(1, N_DEV):
            row = row + dctx[j:j + 1, :]
        rowi = lax.broadcasted_iota(jnp.int32, dctx.shape, 0)
        ctx_rows = jnp.where(rowi == 0, jnp.broadcast_to(row, dctx.shape), 0.0)
        hi = lax.Precision.HIGHEST
        d_rows = jnp.concatenate([dex_ref[...], ctx_rows], axis=0)
        gw_ref[...] = jnp.dot(sil_t, d_rows, preferred_element_type=F32, precision=hi)
        dsil_ref[...] = lax.dot_general(ctx_rows, w_ref[...], NT_DIMS, preferred_element_type=F32, precision=hi)
        tot = dall_ref[0]
        for j in range(1, N_DEV):
            tot = tot + dall_ref[j]
        dsum_ref[...] = tot

    return pl.pallas_call(
        body, name=name,
        out_shape=(jax.ShapeDtypeStruct(w.shape, F32), jax.ShapeDtypeStruct((8, w.shape[0]), F32),
                   jax.ShapeDtypeStruct(d_all.shape[1:], F32)),
        compiler_params=_params(None, VMEM_BIG),
    )(a_t, w, d_ex, d_ctx, d_all)


def _pack_small(sums1, sums2, fsums, sums1c, psums, d_cw, name):
    d = D_MODEL

    def body(s1_ref, s2_ref, f_ref, s1c_ref, p_ref, cw_ref, o_ref):
        o_ref[...] = jnp.zeros_like(o_ref)
        for col, (ref, r) in enumerate([(s1_ref, 1), (s1_ref, 0), (s2_ref, 2), (s2_ref, 1), (s2_ref, 0), (f_ref, 1)]):
            o_ref[0:1, col * d:(col + 1) * d] = ref[r:r + 1, :]
        o_ref[1:2, 0:d] = s1c_ref[1:2, :]
        o_ref[1:2, d:2 * d] = s1c_ref[0:1, :]
        o_ref[2:3, 0:Q_RANK] = p_ref[0:1, :]
        o_ref[2:3, Q_RANK:Q_RANK + KV_RANK] = p_ref[1:2, 0:KV_RANK]
        o_ref[2:3, Q_RANK + KV_RANK:Q_RANK + KV_RANK + d] = f_ref[0:1, :]
        for r in range(3):
            o_ref[3 + r:4 + r, 0:CONV_W] = cw_ref[r:r + 1, :]
        o_ref[6:7, 0:d] = f_ref[3:4, :]

    return pl.pallas_call(body, name=name, out_shape=jax.ShapeDtypeStruct((8, 6 * d), F32))(
        sums1, sums2, fsums, sums1c, psums, d_cw)


def _adam_math(w, g, m, v):
    nm = ADAM_B1 * m + (1.0 - ADAM_B1) * g
    nv = ADAM_B2 * v + (1.0 - ADAM_B2) * (g * g)
    m_hat = nm / (1.0 - ADAM_B1 ** ADAM_STEP)
    v_hat = nv / (1.0 - ADAM_B2 ** ADAM_STEP)
    return -ADAM_LR * (m_hat / (jnp.sqrt(v_hat) + ADAM_EPS) + ADAM_WD * w), nm, nv


def _small_update(dsum, dsil_all, g_cw, params, name):
    d = D_MODEL
    n = len(params)

    def body(*refs):
        dsum_ref, dsil_ref, gcw_ref = refs[:3]
        wmv = refs[3:3 + 3 * n]
        outs = refs[3 + 3 * n:]
        tot = dsil_ref[0]
        for j in range(1, N_DEV):
            tot = tot + dsil_ref[j]
        cv = wmv[0][...]
        sg = 1.0 / (1.0 + jnp.exp(-cv))
        off = Q_RANK + KV_RANK
        grads = [tot[0:1, :] * (sg * (1.0 + cv * (1.0 - sg))),
                 dsum_ref[0:1, :] + dsum_ref[1:2, :],
                 dsum_ref[2:3, 0:Q_RANK], dsum_ref[2:3, Q_RANK:off], dsum_ref[2:3, off:off + d],
                 gcw_ref[...]]
        for p, g in enumerate(grads):
            w_ref, m_ref, v_ref = wmv[3 * p:3 * p + 3]
            at = 0 if len(w_ref.shape) == 3 else Ellipsis
            res = (g,) + _adam_math(w_ref[at], g, m_ref[at], v_ref[at])
            for q, val in enumerate(res):
                outs[4 * p + q][at] = val

    flat = [a for wmv in params for a in wmv]
    out_shape = tuple(jax.ShapeDtypeStruct(wmv[0].shape, F32) for wmv in params for _ in range(4))
    outs = pl.pallas_call(body, name=name, out_shape=out_shape)(dsum, dsil_all, g_cw, *flat)
    return [outs[4 * p:4 * p + 4] for p in range(n)]


def _adamw(w, g, m, v, name, slots=False):
    _, rows, cols = w.shape
    tr = _pick(rows, (256, 128, 64, 32, 16, 8))

    def body(w_ref, g_ref, m_ref, v_ref, *outs):
        if slots:
            gv = g_ref[0].astype(F32)
            for j in range(1, N_DEV):
                gv = gv + g_ref[j].astype(F32)
            outs[0][...] = gv
        else:
            gv = g_ref[...]
        d_ref, nm_ref, nv_ref = outs[-3:]
        d_ref[...], nm_ref[...], nv_ref[...] = _adam_math(w_ref[...], gv, m_ref[...], v_ref[...])

    blk = pl.BlockSpec((None, tr, cols), lambda i: (0, i, 0))
    g_spec = (pl.BlockSpec((N_DEV, tr, cols), lambda i: (0, i, 0)) if slots
              else pl.BlockSpec((tr, cols), lambda i: (i, 0)))
    sh = jax.ShapeDtypeStruct((1, rows, cols), F32)
    n_out = 4 if slots else 3
    return pl.pallas_call(
        body, name=name, grid=(rows // tr,), out_shape=(sh,) * n_out,
        in_specs=[blk, g_spec, blk, blk], out_specs=(blk,) * n_out,
        compiler_params=_params(("parallel",), VMEM_BIG),
    )(w, g, m, v)


def _rope_tables(s, l):
    tok = np.arange(s)
    row = (tok // GRID_W).astype(np.float32)
    col = (tok % GRID_W).astype(np.float32)
    half = QK_ROPE // 2
    freqs = np.float32(ROPE_THETA) ** (-np.arange(0, half, 2, dtype=np.float32) / np.float32(half))
    dd = np.arange(QK_ROPE)
    pos = np.where((dd // half)[None, :] == 0, row[:, None], col[:, None]).astype(np.float32)
    ang = (pos * freqs[dd % (half // 2)][None, :]).astype(np.float32)
    sin = np.sin(ang).astype(np.float32)
    cos_t = np.ones((s + l, LANES), np.float32)
    sgn_t = np.zeros((s + l, LANES), np.float32)
    cos_t[:s, QK_NOPE:QK_NOPE + QK_ROPE] = np.cos(ang)
    sgn_t[:s, QK_NOPE:QK_NOPE + QK_ROPE] = np.where(((dd % half) // (half // 2))[None, :] == 0, -sin, sin)
    return jnp.asarray(cos_t), jnp.asarray(sgn_t)


def _slots_to_cols(g):
    return g.transpose(1, 0, 2).reshape(g.shape[1], N_DEV * g.shape[2])


def _cols_to_slots(w):
    return w.reshape(w.shape[0], N_DEV, w.shape[1] // N_DEV).transpose(1, 0, 2)


def _unpack_small_weights(g_in_t, g_uq, g_ukv):
    w_t = g_in_t.reshape(N_DEV * g_in_t.shape[1], D_MODEL)
    zeros = jnp.zeros((QK_NOPE, D_MODEL), BF16)
    win_head_t = jnp.concatenate([w_t[:Q_RANK + KV_RANK], zeros, w_t[Q_RANK + KV_RANK:MLA_IN],
                                  zeros[:LANES - QK_NOPE - QK_ROPE]], axis=0)
    win_conv_t = w_t[MLA_IN:].reshape(3, CONV_W // LANES, LANES, D_MODEL).transpose(1, 0, 2, 3)
    win_conv_t = win_conv_t.reshape(3 * CONV_W, D_MODEL)
    w_uq = _slots_to_cols(g_uq).reshape(Q_RANK, N_HEADS, QK_NOPE + QK_ROPE)
    wq = jnp.pad(w_uq, ((0, 0), (0, 0), (0, LANES - QK_NOPE - QK_ROPE))).reshape(Q_RANK, N_HEADS * LANES)
    w_ukv = _slots_to_cols(g_ukv).reshape(KV_RANK, N_HEADS, QK_NOPE + V_DIM)
    k_top = jnp.pad(w_ukv[:, :, :QK_NOPE], ((0, 0), (0, 0), (0, LANES - QK_NOPE))).reshape(KV_RANK, N_HEADS * LANES)
    v_top = w_ukv[:, :, QK_NOPE:].reshape(KV_RANK, N_HEADS * V_DIM)
    eye = jnp.pad(jnp.eye(QK_ROPE, dtype=BF16), ((QK_NOPE, LANES - QK_NOPE - QK_ROPE),) * 2)
    wk = jnp.concatenate([
        jnp.concatenate([k_top, v_top], axis=1),
        jnp.concatenate([jnp.tile(eye, (1, N_HEADS)), jnp.zeros((LANES, N_HEADS * V_DIM), BF16)], axis=1)], axis=0)
    return win_head_t, win_conv_t, wq, wk


def _pack_small_grads(d_head_t, d_conv_t, d_wq, d_wkk, d_wkv):
    d_conv_t = d_conv_t.reshape(CONV_W // LANES, 3, LANES, D_MODEL).transpose(1, 0, 2, 3).reshape(3 * CONV_W, D_MODEL)
    rope0 = Q_RANK + KV_RANK + QK_NOPE
    g_in_t = jnp.concatenate([d_head_t[:Q_RANK + KV_RANK], d_head_t[rope0:rope0 + QK_ROPE], d_conv_t], axis=0)
    g_in_t = g_in_t.reshape(N_DEV, -1, D_MODEL).astype(BF16)
    g_uq = d_wq.reshape(Q_RANK, N_HEADS, LANES)[:, :, :QK_NOPE + QK_ROPE].reshape(Q_RANK, -1)
    g_kn = d_wkk[:KV_RANK].reshape(KV_RANK, N_HEADS, LANES)[:, :, :QK_NOPE]
    g_v = d_wkv[:KV_RANK].reshape(KV_RANK, N_HEADS, V_DIM)
    g_ukv = jnp.concatenate([g_kn, g_v], axis=2).reshape(KV_RANK, -1)
    return [g_in_t] + [_cols_to_slots(g).astype(BF16) for g in (g_uq, g_ukv)]


def kernel(x, c, ctx, c_ctx, w_mod, b_mod, w_in, q_norm_g, w_uq, kv_norm_g, w_ukv, conv_w, w_out, w_mlp1, w_mlp2, final_norm_g, loss_target, m_c_ctx, m_w_mod, m_b_mod, m_w_in, m_q_norm_g, m_w_uq, m_kv_norm_g, m_w_ukv, m_conv_w, m_w_out, m_w_mlp1, m_w_mlp2, m_final_norm_g, v_c_ctx, v_w_mod, v_b_mod, v_w_in, v_q_norm_g, v_w_uq, v_kv_norm_g, v_w_ukv, v_conv_w, v_w_out, v_w_mlp1, v_w_mlp2, v_final_norm_g):
    me = _my_index()
    x2d, ctx2d, tgt = x[0], ctx[0], loss_target[0]
    s, l = x2d.shape[0], ctx2d.shape[0]
    t = s + l
    d = D_MODEL
    mod_cols = w_mod.shape[2]
    cw_cols = conv_w.shape[2]

    early = [w.astype(BF16) for w in (w_in[0].T, w_uq[0], w_ukv[0])]
    late = [w.astype(BF16) for w in (w_out[0], w_mlp1[0], w_mlp2[0])]
    c_all, g_in, g_uq, g_ukv = _all_gather([jnp.pad(c, ((0, 7), (0, 0))), *early], "gather_c_weights", False)
    a_rows = jnp.concatenate([c_all[:, 0, :], c_ctx[None, :], jnp.zeros((7, d), F32)], axis=0)
    b_cols = lax.dynamic_slice(b_mod, (0, me * mod_cols), (1, mod_cols))
    mod_cols_all = _adaln_fwd(a_rows, w_mod[0], b_cols, "adaln_fwd")
    cw_blk = jnp.pad(conv_w[0], ((0, 5), (0, mod_cols - cw_cols)))
    (gathered,) = _all_gather([jnp.concatenate([mod_cols_all, cw_blk], axis=0)], "gather_mod", True)
    mod_mine = lax.dynamic_index_in_dim(gathered, me, axis=1, keepdims=False).reshape(1, 6 * d)
    mod_ctx = gathered[:, 8, :].reshape(1, 6 * d)
    cw_full = gathered[:, 16:19, :cw_cols].transpose(1, 0, 2).reshape(3, CONV_W)

    win_head, win_conv, wq, wk = _unpack_small_weights(g_in, g_uq, g_ukv)
    wk_k, wk_v = wk[:, :N_HEADS * LANES], wk[:, N_HEADS * LANES:]
    cos, sgn = _rope_tables(s, l)

    h_all = _modulate_all(x2d, ctx2d, mod_mine, mod_ctx, "modulate1")
    tm_t = _pick(t, (1088, 768, 256))
    tk_t = _pick(t, (2176, 768, 256))
    z_head, cq, kv_in, qf, kv = _head_fwd(h_all, win_head, wq, wk, q_norm_g, kv_norm_g, cos, sgn, tm_t, "head_fwd")
    z_conv = _matmul(h_all, win_conv, mode="nt", name="in_proj_conv", m=s, tm=1024, tn=1536, tk=1024)
    attn, a_cat, stats, (g_out, w1, g_w2) = _attn_fwd(qf, kv, s, _Riding("gather", late), "attn_fwd")
    wo = g_out.reshape(d, d)
    w2 = g_w2.reshape(D_FF, d)
    a_cat = _conv_fwd(z_conv, cw_full, a_cat, "conv_fwd")
    (o, x1, h2), _ = _matmul_rows(a_cat, wo, _epi_resid_modulate, mode="nn", name="out_proj", tm=1024, tk=1024,
                                  rows=[x2d], vecs=[(mod_mine, 2), (mod_mine, 3), (mod_mine, 4)],
                                  out_dtypes=[F32, F32, BF16])
    u1, act = _matmul(h2, w1, mode="nn", name="mlp_up", tm=2048, tk=1024, epilogue="relu2", slots="b_cols")
    (dx2, dm, fsums), _ = _matmul_rows(act, w2, _epi_final, mode="nn", name="mlp_down", tm=512, tk=4096,
                                       rows=[x1, tgt], vecs=[(mod_mine, 5), (final_norm_g[None, :], 0)],
                                       out_dtypes=[F32, BF16], sums=True)

    d_w2 = _matmul(act, dm, mode="tn", name="d_w_mlp2", out_dtype=BF16, tm=2048, tn=1024, tk=1024)
    du1 = _matmul(dm, w2, mode="nt", name="d_act", out_dtype=BF16, tm=2048, tn=1024, tk=1024,
                  epilogue="drelu2", extra=(u1,))
    d_w1 = _matmul(h2, du1, mode="tn", name="d_w_mlp1", out_dtype=BF16, tm=1024, tk=4096, slots="out")
    (dx1, do, sums2), _ = _matmul_rows(du1, w1, _epi_modulate2_bwd, mode="nt", name="d_h2", tm=512, tk=4096,
                                       slots="b_contract", rows=[x1, dx2, o], vecs=[(mod_mine, 4), (mod_mine, 2)],
                                       out_dtypes=[F32, BF16], sums=True)
    d_wo = _matmul(a_cat, do, mode="tn", name="d_w_out", out_dtype=BF16, tm=1024, tn=1024, tk=2048)
    da = _matmul(do, wo, mode="nt", name="d_a", tm=1024, tn=1024, tk=1024)
    dz_conv, d_cw = _conv_bwd(z_conv, cw_full, da, "conv_bwd")
    ready = [d_wo.reshape(N_DEV, d // N_DEV, d), d_w1, d_w2.reshape(N_DEV, D_FF // N_DEV, d)]
    dq, dk, dv, rode = _attn_bwd(qf, kv, attn, da, stats, cos, sgn, _Riding("exchange", ready), "attn_bwd")
    d_wq = _matmul(cq, dq, mode="tn", name="d_w_uq", k=s, tm=256, tn=1024, tk=4096)
    d_wkk = _matmul(kv_in, dk, mode="tn", name="d_w_ukv_k", tm=256, tn=1024, tk=tk_t)
    d_wkv = _matmul(kv_in, dv, mode="tn", name="d_w_ukv_v", tm=256, tn=512, tk=tk_t)
    dz_head, dh_head, psums = _head_bwd(dq, dk, dv, z_head, wq, wk_k, wk_v, win_head, q_norm_g, kv_norm_g, cos, sgn, s,
                                        "head_bwd")
    d_head = _matmul(dz_head, h_all, mode="tn", name="d_w_in_head", tm=512, tn=1024, tk=tk_t)
    d_conv = _matmul(dz_conv, h_all, mode="tn", name="d_w_in_conv", k=s, tm=1536, tn=1024, tk=2048)
    send = _pack_small_grads(d_head, d_conv, d_wq, d_wkk, d_wkv)
    (grad_x, sums1), got = _matmul_rows(dz_conv, win_conv, _epi_modulate1_bwd, mode="nn", name="d_h1", tm=1024,
                                        tk=win_conv.shape[0], rows=[dh_head, x2d, dx1], vecs=[(mod_mine, 1)],
                                        out_dtypes=[F32], sums=True, riding=_Riding("exchange", send))
    sums1c = _modulate_sums(dh_head, s // ROW_TILE, ctx2d)

    small = _pack_small(sums1, sums2, fsums, sums1c, psums, d_cw, "pack_small")
    (d_all,) = _all_gather([small], "gather_small_grads", True)
    d_cols = lax.dynamic_slice_in_dim(d_all, me * mod_cols, mod_cols, axis=2)
    g_w_mod, dsil, dsum = _adaln_bwd(a_rows.T, w_mod[0], d_cols[:, 0, :], d_cols[:, 1, :], d_all, "adaln_bwd")
    (dsil_all,) = _all_gather([dsil], "gather_d_cctx", True)
    loss = dsum[6, 0]
    g_cw = lax.dynamic_slice(dsum, (3, me * cw_cols), (3, cw_cols))

    slots = dict(zip(["w_in", "w_uq", "w_ukv"], got))
    slots.update(zip(["w_out", "w_mlp1", "w_mlp2"], rode))

    grads = {}
    weights = {"c_ctx": c_ctx, "w_mod": w_mod, "b_mod": b_mod, "w_in": w_in, "q_norm_g": q_norm_g, "w_uq": w_uq,
               "kv_norm_g": kv_norm_g, "w_ukv": w_ukv, "conv_w": conv_w, "w_out": w_out, "w_mlp1": w_mlp1,
               "w_mlp2": w_mlp2, "final_norm_g": final_norm_g}
    m_in = {"c_ctx": m_c_ctx, "w_mod": m_w_mod, "b_mod": m_b_mod, "w_in": m_w_in, "q_norm_g": m_q_norm_g,
            "w_uq": m_w_uq, "kv_norm_g": m_kv_norm_g, "w_ukv": m_w_ukv, "conv_w": m_conv_w, "w_out": m_w_out,
            "w_mlp1": m_w_mlp1, "w_mlp2": m_w_mlp2, "final_norm_g": m_final_norm_g}
    v_in = {"c_ctx": v_c_ctx, "w_mod": v_w_mod, "b_mod": v_b_mod, "w_in": v_w_in, "q_norm_g": v_q_norm_g,
            "w_uq": v_w_uq, "kv_norm_g": v_kv_norm_g, "w_ukv": v_w_ukv, "conv_w": v_conv_w, "w_out": v_w_out,
            "w_mlp1": v_w_mlp1, "w_mlp2": v_w_mlp2, "final_norm_g": v_final_norm_g}
    names = list(weights)
    small_names = ["c_ctx", "b_mod", "q_norm_g", "kv_norm_g", "final_norm_g", "conv_w"]
    delta, new_m, new_v = {}, {}, {}

    def as_rows(a):
        return a[None, :] if a.ndim == 1 else a

    small_out = _small_update(dsum, dsil_all, g_cw, [[as_rows(src[n]) for src in (weights, m_in, v_in)]
                                                      for n in small_names], "small_update")
    for n, outs in zip(small_names, small_out):
        grads[n], delta[n], new_m[n], new_v[n] = [a.reshape(weights[n].shape) for a in outs]
    for n in names:
        if n in small_names:
            continue
        if n == "w_in":
            wmv = [jnp.swapaxes(src[n], 1, 2) for src in (weights, m_in, v_in)]
            outs = _adamw(wmv[0], slots[n], wmv[1], wmv[2], "adamw_" + n, slots=True)
            grads[n], delta[n], new_m[n], new_v[n] = [jnp.swapaxes(a, 1, 2) for a in outs]
        elif n in slots:
            grads[n], delta[n], new_m[n], new_v[n] = _adamw(weights[n], slots[n], m_in[n], v_in[n], "adamw_" + n,
                                                            slots=True)
        else:
            delta[n], new_m[n], new_v[n] = _adamw(weights[n], g_w_mod, m_in[n], v_in[n], "adamw_" + n)
            grads[n] = g_w_mod[None]

    return (loss, grad_x[None], *[grads[n] for n in names], *[delta[n] for n in names],
            *[new_m[n] for n in names], *[new_v[n] for n in names])
```

```python
import math

import jax
import jax.numpy as jnp
import numpy as np
from jax import lax
from jax.experimental import pallas as pl
from jax.experimental.pallas import tpu as pltpu

F32 = jnp.float32
BF16 = jnp.bfloat16

D_MODEL = 1024
GRID_W = 64
N_HEADS = 8
QK_NOPE = 64
QK_ROPE = 32
V_DIM = 64
Q_RANK = 256
KV_RANK = 128
MLA_IN = Q_RANK + KV_RANK + QK_ROPE
CONV_W = 512
HEAD_COLS = 512
D_FF = 4096
ROPE_THETA = 10000.0
EPS = 1e-6
ATTN_SCALE = 1.0 / math.sqrt(QK_NOPE + QK_ROPE)
LOG2_E = 1.0 / math.log(2.0)
EXP2_SCALE = ATTN_SCALE * LOG2_E
N_DEV = 8
LANES = 128

ADAM_LR, ADAM_B1, ADAM_B2, ADAM_EPS, ADAM_WD, ADAM_STEP = 0.001, 0.9, 0.999, 1e-08, 0.01, 10

ROW_TILE = 256
VMEM_BIG = 60 * 1024 * 1024


def _params(sem=None, vmem=None):
    return pltpu.CompilerParams(dimension_semantics=sem, vmem_limit_bytes=vmem)


def _pick(n, prefs):
    for p in prefs:
        if n % p == 0:
            return p
    return n


def _my_index():
    return 4 * lax.axis_index("x") + 2 * lax.axis_index("y") + lax.axis_index("c")


def _two_level_gather(x_refs, out_refs, send_sems, recv_sems, local_sems):
    n = len(x_refs)
    x, y, c = lax.axis_index("x"), lax.axis_index("y"), lax.axis_index("c")
    me, sibling = (x, y, c), (x, y, 1 - c)
    chips = [(1 - x, y), (x, 1 - y), (1 - x, 1 - y)]

    def slot(a, px, py, pc):
        return out_refs[a].at[4 * px + 2 * py + pc]

    def copy(a, k, block, to, src=None):
        return pltpu.make_async_remote_copy(
            src_ref=slot(a, *block) if src is None else src, dst_ref=slot(a, *block),
            send_sem=send_sems.at[7 * a + k], recv_sem=recv_sems.at[7 * a + k],
            device_id=to, device_id_type=pl.DeviceIdType.MESH)

    mine = [pltpu.make_async_copy(x_refs[a], slot(a, *me), local_sems.at[a]) for a in range(n)]
    for cp in mine:
        cp.start()
    started = []
    for a in range(n):
        first = [copy(a, 0, me, sibling, src=x_refs[a])]
        first += [copy(a, 1 + j, me, (*chip, c), src=x_refs[a]) for j, chip in enumerate(chips)]
        for cp in first:
            cp.start()
        started += first

    def finish():
        for a in range(n):
            for j, chip in enumerate(chips):
                copy(a, 1 + j, (*chip, c), me).wait_recv()
                passed = copy(a, 4 + j, (*chip, c), sibling)
                passed.start()
                started.append(passed)
        for a in range(n):
            copy(a, 0, sibling, me).wait_recv()
            for j, chip in enumerate(chips):
                copy(a, 4 + j, (*chip, 1 - c), me).wait_recv()
        for cp in started:
            cp.wait_send()
        for cp in mine:
            cp.wait()

    return finish


def _direct_gather(src_ref, dst_ref, send_sems, recv_sems):
    x, y, c = lax.axis_index("x"), lax.axis_index("y"), lax.axis_index("c")
    me = 4 * x + 2 * y + c
    dst_ref[me] = src_ref[...]
    sends, landings = [], []
    for k in range(1, N_DEV):
        peer = (1 - x if k & 4 else x, 1 - y if k & 2 else y, 1 - c if k & 1 else c)
        pid = 4 * peer[0] + 2 * peer[1] + peer[2]
        for dst, out in ((me, sends), (pid, landings)):
            out.append(pltpu.make_async_remote_copy(
                src_ref=src_ref, dst_ref=dst_ref.at[dst], send_sem=send_sems.at[k - 1], recv_sem=recv_sems.at[k - 1],
                device_id=peer, device_id_type=pl.DeviceIdType.MESH))
    for cp in sends:
        cp.start()

    def finish():
        for cp in landings:
            cp.wait_recv()
        for cp in sends:
            cp.wait_send()

    return finish


def _all_gather(arrays, name, in_vmem):
    space = pltpu.VMEM if in_vmem else pl.ANY
    n = len(arrays)

    def body(*refs):
        _two_level_gather(refs[:n], refs[n:2 * n], *refs[2 * n:])()

    outs = pl.pallas_call(
        body, name=name,
        out_shape=tuple(jax.ShapeDtypeStruct((N_DEV,) + a.shape, a.dtype) for a in arrays),
        in_specs=[pl.BlockSpec(memory_space=space)] * n,
        out_specs=tuple(pl.BlockSpec(memory_space=space) for _ in arrays),
        scratch_shapes=[pltpu.SemaphoreType.DMA((7 * n,)), pltpu.SemaphoreType.DMA((7 * n,)),
                        pltpu.SemaphoreType.DMA((n,))],
    )(*arrays)
    return list(outs)


class _Riding:
    def __init__(self, kind, arrays):
        self.kind, self.arrays, self.n = kind, list(arrays), len(arrays)
        lead = (N_DEV,) if kind == "gather" else ()
        self.out_shape = [jax.ShapeDtypeStruct(lead + a.shape, a.dtype) for a in self.arrays]
        self.specs = [pl.BlockSpec(memory_space=pl.ANY)] * self.n
        self.scratch = [pltpu.SemaphoreType.DMA((7 * self.n,)), pltpu.SemaphoreType.DMA((7 * self.n,)),
                        pltpu.SemaphoreType.DMA((self.n,))]

    def copies(self, x_refs, y_refs, send_sems, recv_sems, local_sems):
        x, y, c = lax.axis_index("x"), lax.axis_index("y"), lax.axis_index("c")
        me = 4 * x + 2 * y + c
        local, sends, landings = [], [], []
        for a in range(self.n):
            src_mine = x_refs[a] if self.kind == "gather" else x_refs[a].at[me]
            local.append(pltpu.make_async_copy(src_mine, y_refs[a].at[me], local_sems.at[a]))
            for k in range(1, N_DEV):
                peer = (1 - x if k & 4 else x, 1 - y if k & 2 else y, 1 - c if k & 1 else c)
                pid = 4 * peer[0] + 2 * peer[1] + peer[2]
                src = x_refs[a] if self.kind == "gather" else x_refs[a].at[pid]
                for dst, out in ((me, sends), (pid, landings)):
                    out.append(pltpu.make_async_remote_copy(
                        src_ref=src, dst_ref=y_refs[a].at[dst],
                        send_sem=send_sems.at[7 * a + k - 1], recv_sem=recv_sems.at[7 * a + k - 1],
                        device_id=peer, device_id_type=pl.DeviceIdType.MESH))
        return local, sends, landings

    def run(self, first, last, x_refs, y_refs, sems):
        if self.n == 0:
            return None
        local, sends, landings = self.copies(x_refs, y_refs, *sems)

        @pl.when(first)
        def _():
            for cp in local + sends:
                cp.start()

        return local, sends, landings, last

    @staticmethod
    def finish(state):
        if state is None:
            return
        local, sends, landings, last = state

        @pl.when(last)
        def _():
            for cp in landings:
                cp.wait_recv()
            for cp in sends:
                cp.wait_send()
            for cp in local:
                cp.wait()


_DIMS = {"nn": (((1,), (0,)), ((), ())), "nt": (((1,), (1,)), ((), ())), "tn": (((0,), (0,)), ((), ()))}
NT_DIMS = _DIMS["nt"]
TN_DIMS = _DIMS["tn"]


def _swap8(x):
    lane = lax.broadcasted_iota(jnp.int32, x.shape, 1)
    return jnp.where((lane & 15) < 8, pltpu.roll(x, LANES - 8, 1), pltpu.roll(x, 8, 1))


def _rope(x, cos, sgn, bwd):
    return x * cos + (_swap8(x * sgn) if bwd else _swap8(x) * sgn)


def _matmul(a, b, *, mode, name, out_dtype=F32, tm=512, tn=512, tk=512, m=None, k=None,
            epilogue=None, extra=(), slots=None):
    if mode == "nn":
        m = a.shape[0] if m is None else m
        k = a.shape[1]
        n = N_DEV * b.shape[2] if slots == "b_cols" else b.shape[1]
    elif mode == "nt":
        m = a.shape[0] if m is None else m
        k = a.shape[1]
        n = b.shape[0]
    else:
        k = a.shape[0] if k is None else k
        m, n = a.shape[1], b.shape[1]
    tm, tn, tk = min(tm, m), min(tn, n), min(tk, k)
    if slots == "b_cols":
        tn = b.shape[2]
    if slots == "out":
        tn = n // N_DEV
    assert m % tm == 0 and n % tn == 0 and k % tk == 0, (name, m, n, k, tm, tn, tk)
    nk = k // tk
    dims = _DIMS[mode]
    a_spec = (pl.BlockSpec((tk, tm), lambda i, j, kk: (kk, i)) if mode == "tn"
              else pl.BlockSpec((tm, tk), lambda i, j, kk: (i, kk)))
    if slots == "b_cols":
        b_spec = pl.BlockSpec((None, tk, tn), lambda i, j, kk: (j, kk, 0))
    elif mode == "nt":
        b_spec = pl.BlockSpec((tn, tk), lambda i, j, kk: (j, kk))
    else:
        b_spec = pl.BlockSpec((tk, tn), lambda i, j, kk: (kk, j))
    tile = pl.BlockSpec((tm, tn), lambda i, j, kk: (i, j))
    if slots == "out":
        o_spec = pl.BlockSpec((None, tm, tn), lambda i, j, kk: (j, i, 0))
        o_shape = (N_DEV, m, tn)
    else:
        o_spec, o_shape = tile, (m, n)
    in_specs, args = [a_spec, b_spec], [a, b]
    if epilogue == "drelu2":
        in_specs.append(tile)
    args += list(extra)
    if epilogue == "relu2":
        out_shape = (jax.ShapeDtypeStruct(o_shape, BF16), jax.ShapeDtypeStruct(o_shape, BF16))
        out_specs = (o_spec, o_spec)
    else:
        out_shape = jax.ShapeDtypeStruct(o_shape, out_dtype)
        out_specs = o_spec
    n_in = len(args)
    n_out = 2 if epilogue == "relu2" else 1

    def body(*refs):
        a_ref, b_ref = refs[0], refs[1]
        outs = refs[n_in:n_in + n_out]
        part = lax.dot_general(a_ref[...], b_ref[...], dims, preferred_element_type=F32)

        def finish(acc):
            if epilogue == "relu2":
                outs[0][...] = acc.astype(BF16)
                r = jnp.maximum(acc, 0.0)
                outs[1][...] = (r * r).astype(BF16)
            elif epilogue == "drelu2":
                u = refs[2][...].astype(F32)
                outs[0][...] = (acc * (2.0 * jnp.maximum(u, 0.0))).astype(out_dtype)
            else:
                outs[0][...] = acc.astype(out_dtype)

        if nk == 1:
            finish(part)
        else:
            acc_ref = refs[n_in + n_out]
            kk = pl.program_id(2)

            @pl.when(kk == 0)
            def _():
                acc_ref[...] = part

            @pl.when(kk > 0)
            def _():
                acc_ref[...] += part

            @pl.when(kk == nk - 1)
            def _():
                finish(acc_ref[...])

    return pl.pallas_call(
        body, name=name, grid=(m // tm, n // tn, nk),
        out_shape=out_shape, in_specs=in_specs, out_specs=out_specs,
        scratch_shapes=[pltpu.VMEM((tm, tn), F32)] if nk > 1 else [],
        compiler_params=_params(("parallel", "parallel", "arbitrary"), VMEM_BIG),
    )(*args)


def _rstd(x):
    return lax.rsqrt(jnp.mean(x * x, axis=1, keepdims=True) + EPS)


def _norm_bwd(dxn, xn, r):
    return r * (dxn - xn * jnp.mean(dxn * xn, axis=1, keepdims=True))


def _vec(col):
    return pl.BlockSpec((1, D_MODEL), lambda i: (0, col))


def _matmul_rows(a, b, epi, *, mode, name, tm, tk, rows=(), vecs=(), out_dtypes=(), sums=False, slots=None,
                 riding=None):
    m, k = a.shape
    n = D_MODEL
    tm, tk = min(tm, m), min(tk, k)
    riding = riding or _Riding("gather", [])
    group = 1
    if slots == "b_contract":
        group = max(1, tk // b.shape[2])
        tk = group * b.shape[2]
        b_spec = pl.BlockSpec((group, n, tk // group), lambda i, kk: (kk, 0, 0))
    elif mode == "nt":
        b_spec = pl.BlockSpec((n, tk), lambda i, kk: (0, kk))
    else:
        b_spec = pl.BlockSpec((tk, n), lambda i, kk: (kk, 0))
    assert m % tm == 0 and k % tk == 0, (name, m, k, tm, tk)
    ni, nk = m // tm, k // tk
    dims = _DIMS[mode]
    tile = pl.BlockSpec((tm, n), lambda i, kk: (i, 0))
    in_specs = [pl.BlockSpec((tm, tk), lambda i, kk: (i, kk)), b_spec] + [tile] * len(rows)
    in_specs += [pl.BlockSpec((1, n), lambda i, kk, col=col: (0, col)) for _, col in vecs]
    args = [a, b, *rows, *[v for v, _ in vecs]]
    out_shape = [jax.ShapeDtypeStruct((m, n), dt) for dt in out_dtypes]
    out_specs = [tile] * len(out_dtypes)
    if sums:
        out_shape.append(jax.ShapeDtypeStruct((8, n), F32))
        out_specs.append(pl.BlockSpec((8, n), lambda i, kk: (0, 0)))
    n_rows, n_vecs, n_outs, nr = len(rows), len(vecs), len(out_dtypes), riding.n
    n_in = 2 + n_rows + n_vecs

    def body(*refs):
        a_ref, b_ref = refs[0], refs[1]
        row_refs = refs[2:2 + n_rows]
        vec_refs = refs[2 + n_rows:n_in]
        x_refs = refs[n_in:n_in + nr]
        out_refs = refs[n_in + nr:n_in + nr + n_outs]
        pos = n_in + nr + n_outs
        sums_ref = refs[pos] if sums else None
        pos += 1 if sums else 0
        y_refs = refs[pos:pos + nr]
        pos += nr
        acc_ref = refs[pos] if nk > 1 else None
        sem_refs = refs[pos + (1 if nk > 1 else 0):]
        i, kk = pl.program_id(0), pl.program_id(1)
        state = riding.run((i == 0) & (kk == 0), (i == ni - 1) & (kk == nk - 1), x_refs, y_refs, sem_refs)
        if slots == "b_contract":
            c = tk // group
            part = lax.dot_general(a_ref[:, 0:c], b_ref[0], dims, preferred_element_type=F32)
            for u in range(1, group):
                part = part + lax.dot_general(a_ref[:, u * c:(u + 1) * c], b_ref[u], dims, preferred_element_type=F32)
        else:
            part = lax.dot_general(a_ref[...], b_ref[...], dims, preferred_element_type=F32)

        def finish(acc):
            nsub = tm // ROW_TILE
            for r in range(nsub):
                blk = pl.ds(r * ROW_TILE, ROW_TILE)
                epi(acc[r * ROW_TILE:(r + 1) * ROW_TILE], [ref.at[blk] for ref in row_refs], vec_refs,
                    [ref.at[blk] for ref in out_refs], sums_ref,
                    (i == 0) if r == 0 else None, (i == ni - 1) if r == nsub - 1 else None)

        if nk == 1:
            finish(part)
        else:
            @pl.when(kk == 0)
            def _():
                acc_ref[...] = part

            @pl.when(kk > 0)
            def _():
                acc_ref[...] += part

            @pl.when(kk == nk - 1)
            def _():
                finish(acc_ref)

        riding.finish(state)

    outs = pl.pallas_call(
        body, name=name, grid=(ni, nk),
        out_shape=(*out_shape, *riding.out_shape),
        in_specs=[*in_specs, *riding.specs], out_specs=(*out_specs, *riding.specs),
        scratch_shapes=([pltpu.VMEM((tm, n), F32)] if nk > 1 else []) + (riding.scratch if nr else []),
        compiler_params=_params(("arbitrary", "arbitrary"), VMEM_BIG),
    )(*args, *riding.arrays)
    n_own = len(out_shape)
    return list(outs[:n_own]), list(outs[n_own:])


def _zero_sums_at_start(sums_ref, first):
    if first is not None:
        @pl.when(first)
        def _():
            sums_ref[...] = jnp.zeros_like(sums_ref)


def _epi_resid_modulate(acc, rows, vecs, outs, sums_ref, first, last):
    (x_ref,), (g_ref, sh_ref, sc_ref) = rows, vecs
    x1 = x_ref[...] + g_ref[...] * acc
    outs[0][...] = acc
    outs[1][...] = x1
    outs[2][...] = (x1 * _rstd(x1) * (1.0 + sc_ref[...]) + sh_ref[...]).astype(BF16)


def _epi_final(acc, rows, vecs, outs, sums_ref, first, last):
    (x1_ref, t_ref), (g_ref, gf_ref) = rows, vecs
    d = acc.shape[1]
    x2 = x1_ref[...] + g_ref[...] * acc
    r = _rstd(x2)
    xn = x2 * r
    err = xn * gf_ref[...] - t_ref[...]
    dy = err * (1.0 / d)
    dx2 = _norm_bwd(dy * gf_ref[...], xn, r)
    outs[0][...] = dx2
    outs[1][...] = (dx2 * g_ref[...]).astype(BF16)
    _zero_sums_at_start(sums_ref, first)
    sums_ref[0:1, :] += jnp.sum(dy * xn, axis=0, keepdims=True)
    sums_ref[1:2, :] += jnp.sum(dx2 * acc, axis=0, keepdims=True)
    sums_ref[2:3, :] += jnp.sum(err * err, axis=0, keepdims=True)

    if last is not None:
        @pl.when(last)
        def _():
            tot = jnp.sum(sums_ref[2:3, :], axis=1, keepdims=True) * (0.5 / d)
            sums_ref[3:4, :] = jnp.broadcast_to(tot, (1, d))


def _epi_modulate2_bwd(acc, rows, vecs, outs, sums_ref, first, last):
    (x_ref, dres_ref, o_ref), (sc_ref, g_ref) = rows, vecs
    x = x_ref[...]
    r = _rstd(x)
    xn = x * r
    dx = dres_ref[...] + _norm_bwd(acc * (1.0 + sc_ref[...]), xn, r)
    outs[0][...] = dx
    outs[1][...] = (dx * g_ref[...]).astype(BF16)
    _zero_sums_at_start(sums_ref, first)
    sums_ref[0:1, :] += jnp.sum(acc * xn, axis=0, keepdims=True)
    sums_ref[1:2, :] += jnp.sum(acc, axis=0, keepdims=True)
    sums_ref[2:3, :] += jnp.sum(dx * o_ref[...], axis=0, keepdims=True)


def _epi_modulate1_bwd(acc, rows, vecs, outs, sums_ref, first, last):
    (add_ref, x_ref, dres_ref), (sc_ref,) = rows, vecs
    dh = acc + add_ref[...]
    x = x_ref[...]
    r = _rstd(x)
    xn = x * r
    outs[0][...] = dres_ref[...] + _norm_bwd(dh * (1.0 + sc_ref[...]), xn, r)
    _zero_sums_at_start(sums_ref, first)
    sums_ref[0:1, :] += jnp.sum(dh * xn, axis=0, keepdims=True)
    sums_ref[1:2, :] += jnp.sum(dh, axis=0, keepdims=True)


def _modulate_all(x, ctx, mod, mod_ctx, name):
    s, d = x.shape
    t = s + ctx.shape[0]
    ns = s // ROW_TILE
    nc = ctx.shape[0] // ROW_TILE

    def body(x_ref, c_ref, sh_ref, sc_ref, shc_ref, scc_ref, h_ref):
        i = pl.program_id(0)

        @pl.when(i < ns)
        def _():
            v = x_ref[...]
            h_ref[...] = (v * _rstd(v) * (1.0 + sc_ref[...]) + sh_ref[...]).astype(BF16)

        @pl.when(i >= ns)
        def _():
            v = c_ref[...]
            h_ref[...] = (v * _rstd(v) * (1.0 + scc_ref[...]) + shc_ref[...]).astype(BF16)

    return pl.pallas_call(
        body, name=name, grid=(ns + nc,),
        out_shape=jax.ShapeDtypeStruct((t, d), BF16),
        in_specs=[pl.BlockSpec((ROW_TILE, d), lambda i: (jnp.minimum(i, ns - 1), 0)),
                  pl.BlockSpec((ROW_TILE, d), lambda i: (jnp.maximum(i - ns, 0), 0)),
                  _vec(0), _vec(1), _vec(0), _vec(1)],
        out_specs=pl.BlockSpec((ROW_TILE, d), lambda i: (i, 0)),
        compiler_params=_params(("arbitrary",)),
    )(x, ctx, mod, mod, mod_ctx, mod_ctx)


def _modulate_sums(dh, row_off, xsrc):
    s, d = xsrc.shape

    def body(dh_ref, x_ref, sums_ref):
        i = pl.program_id(0)
        x = x_ref[...]
        dhv = dh_ref[...]

        @pl.when(i == 0)
        def _():
            sums_ref[...] = jnp.zeros_like(sums_ref)

        sums_ref[0:1, :] += jnp.sum(dhv * (x * _rstd(x)), axis=0, keepdims=True)
        sums_ref[1:2, :] += jnp.sum(dhv, axis=0, keepdims=True)

    return pl.pallas_call(
        body, name="modulate1_ctx_bwd", grid=(s // ROW_TILE,),
        out_shape=jax.ShapeDtypeStruct((8, d), F32),
        in_specs=[pl.BlockSpec((ROW_TILE, d), lambda i: (i + row_off, 0)), pl.BlockSpec((ROW_TILE, d), lambda i: (i, 0))],
        out_specs=pl.BlockSpec((8, d), lambda i: (0, 0)),
        compiler_params=_params(("arbitrary",)),
    )(dh, xsrc)


def _head_fwd(h_all, win_head, wq, wk, q_gain, kv_gain, cos, sgn, tm, name):
    t, d = h_all.shape
    nq, nkv = wq.shape[1], wk.shape[1]

    def body(h_ref, wi_ref, wq_ref, wk_ref, qg_ref, kg_ref, c_ref, s_ref, z_ref, cq_ref, kvin_ref, qf_ref, kv_ref):
        z = lax.dot_general(h_ref[...], wi_ref[...], NT_DIMS, preferred_element_type=F32)
        z_ref[...] = z
        cos, sgn = c_ref[...], s_ref[...]
        zq = z[:, 0:Q_RANK]
        cq = (zq * _rstd(zq) * qg_ref[...]).astype(BF16)
        cq_ref[...] = cq
        zk = z[:, Q_RANK:Q_RANK + KV_RANK]
        kv_in = jnp.concatenate([(zk * _rstd(zk) * kg_ref[...]).astype(BF16),
                                 _rope(z[:, Q_RANK + KV_RANK:HEAD_COLS], cos, sgn, False).astype(BF16)], axis=1)
        kvin_ref[...] = kv_in
        q = jnp.dot(cq, wq_ref[...], preferred_element_type=F32)
        for h in range(nq // LANES):
            sl = slice(h * LANES, (h + 1) * LANES)
            qf_ref[:, sl] = _rope(q[:, sl], cos, sgn, False).astype(BF16)
        kv_ref[...] = jnp.dot(kv_in, wk_ref[...], preferred_element_type=F32).astype(BF16)

    def row(w):
        return pl.BlockSpec((tm, w), lambda i: (i, 0))

    def whole(a):
        return pl.BlockSpec(a.shape, lambda i: (0, 0))

    return pl.pallas_call(
        body, name=name, grid=(t // tm,),
        out_shape=(jax.ShapeDtypeStruct((t, HEAD_COLS), F32), jax.ShapeDtypeStruct((t, Q_RANK), BF16),
                   jax.ShapeDtypeStruct((t, KV_RANK + LANES), BF16), jax.ShapeDtypeStruct((t, nq), BF16),
                   jax.ShapeDtypeStruct((t, nkv), BF16)),
        in_specs=[row(d), whole(win_head), whole(wq), whole(wk), whole(q_gain), whole(kv_gain), row(LANES), row(LANES)],
        out_specs=(row(HEAD_COLS), row(Q_RANK), row(KV_RANK + LANES), row(nq), row(nkv)),
        compiler_params=_params(("parallel",), VMEM_BIG),
    )(h_all, win_head, wq, wk, q_gain, kv_gain, cos, sgn)


def _head_bwd(dq, dk, dv, z, wq, wk_k, wk_v, win_head, q_gain, kv_gain, cos, sgn, s, name):
    t = z.shape[0]
    ns = s // ROW_TILE

    def body(dq_ref, dk_ref, dv_ref, z_ref, wq_ref, wkk_ref, wkv_ref, wi_ref, qg_ref, kg_ref, c_ref, s_ref,
             dz_ref, dh_ref, sums_ref):
        i = pl.program_id(0)

        @pl.when(i == 0)
        def _():
            sums_ref[...] = jnp.zeros_like(sums_ref)

        @pl.when(i < ns)
        def _():
            dc = lax.dot_general(dq_ref[...], wq_ref[...], NT_DIMS, preferred_element_type=F32)
            zq = z_ref[:, 0:Q_RANK]
            r = _rstd(zq)
            zn = zq * r
            sums_ref[0:1, :] += jnp.sum(dc * zn, axis=0, keepdims=True)
            dz_ref[:, 0:Q_RANK] = _norm_bwd(dc * qg_ref[...], zn, r).astype(BF16)

        @pl.when(i >= ns)
        def _():
            dz_ref[:, 0:Q_RANK] = jnp.zeros((ROW_TILE, Q_RANK), BF16)

        dkv = (lax.dot_general(dk_ref[...], wkk_ref[...], NT_DIMS, preferred_element_type=F32)
               + lax.dot_general(dv_ref[...], wkv_ref[...], NT_DIMS, preferred_element_type=F32))
        zk = z_ref[:, Q_RANK:Q_RANK + KV_RANK]
        r = _rstd(zk)
        zn = zk * r
        dc = dkv[:, 0:KV_RANK]
        sums_ref[1:2, 0:KV_RANK] += jnp.sum(dc * zn, axis=0, keepdims=True)
        dz_ref[:, Q_RANK:Q_RANK + KV_RANK] = _norm_bwd(dc * kg_ref[...], zn, r).astype(BF16)
        dz_ref[:, Q_RANK + KV_RANK:HEAD_COLS] = _rope(dkv[:, KV_RANK:KV_RANK + LANES], c_ref[...], s_ref[...],
                                                       True).astype(BF16)
        dh_ref[...] = jnp.dot(dz_ref[...], wi_ref[...], preferred_element_type=F32)

    def row(w):
        return pl.BlockSpec((ROW_TILE, w), lambda i: (i, 0))

    def whole(a):
        return pl.BlockSpec(a.shape, lambda i: (0, 0))

    return pl.pallas_call(
        body, name=name, grid=(t // ROW_TILE,),
        out_shape=(jax.ShapeDtypeStruct((t, HEAD_COLS), BF16), jax.ShapeDtypeStruct((t, D_MODEL), F32),
                   jax.ShapeDtypeStruct((8, Q_RANK), F32)),
        in_specs=[pl.BlockSpec((ROW_TILE, dq.shape[1]), lambda i: (jnp.minimum(i, ns - 1), 0)),
                  row(dk.shape[1]), row(dv.shape[1]), row(HEAD_COLS), whole(wq), whole(wk_k), whole(wk_v),
                  whole(win_head), whole(q_gain), whole(kv_gain), row(LANES), row(LANES)],
        out_specs=(row(HEAD_COLS), row(D_MODEL), pl.BlockSpec((8, Q_RANK), lambda i: (0, 0))),
        compiler_params=_params(("arbitrary",), VMEM_BIG),
    )(dq, dk, dv, z, wq, wk_k, wk_v, win_head, q_gain, kv_gain, cos, sgn)


def _shift_rows(u, s):
    rowi = lax.broadcasted_iota(jnp.int32, u.shape, 0)
    prev = jnp.where(rowi == 0, 0.0, pltpu.roll(u, 1, 0))
    nxt = jnp.where(rowi == s - 1, 0.0, pltpu.roll(u, s - 1, 0))
    return prev, nxt


def _conv_fwd(z_conv, cw, a_cat, name):
    s = z_conv.shape[0]

    def body(z_ref, w_ref, a_in_ref, o_ref):
        del a_in_ref
        gb, gc, xv = z_ref[:, 0:LANES], z_ref[:, LANES:2 * LANES], z_ref[:, 2 * LANES:3 * LANES]
        u = gc * xv
        prev, nxt = _shift_rows(u, s)
        y = w_ref[0:1, :] * prev + w_ref[1:2, :] * u + w_ref[2:3, :] * nxt
        o_ref[...] = (gb * y).astype(BF16)

    return pl.pallas_call(
        body, name=name, grid=(CONV_W // LANES,),
        out_shape=jax.ShapeDtypeStruct(a_cat.shape, a_cat.dtype),
        in_specs=[pl.BlockSpec((s, 3 * LANES), lambda j: (0, j)), pl.BlockSpec((3, LANES), lambda j: (0, j)),
                  pl.BlockSpec(memory_space=pl.ANY)],
        out_specs=pl.BlockSpec((s, LANES), lambda j: (0, 4 + j)),
        input_output_aliases={2: 0},
        compiler_params=_params(("parallel",), VMEM_BIG),
    )(z_conv, cw, a_cat)


def _conv_bwd(z_conv, cw, da, name):
    s = z_conv.shape[0]

    def body(z_ref, w_ref, da_ref, dz_ref, dw_ref):
        gb, gc, xv = z_ref[:, 0:LANES], z_ref[:, LANES:2 * LANES], z_ref[:, 2 * LANES:3 * LANES]
        u = gc * xv
        prev, nxt = _shift_rows(u, s)
        dcv = da_ref[...]
        dz_ref[:, 0:LANES] = (dcv * (w_ref[0:1, :] * prev + w_ref[1:2, :] * u + w_ref[2:3, :] * nxt)).astype(BF16)
        dy = dcv * gb
        dw_ref[0:1, :] = jnp.sum(dy * prev, axis=0, keepdims=True)
        dw_ref[1:2, :] = jnp.sum(dy * u, axis=0, keepdims=True)
        dw_ref[2:3, :] = jnp.sum(dy * nxt, axis=0, keepdims=True)
        dyp, dyn = _shift_rows(dy, s)
        du = w_ref[0:1, :] * dyn + w_ref[1:2, :] * dy + w_ref[2:3, :] * dyp
        dz_ref[:, LANES:2 * LANES] = (du * xv).astype(BF16)
        dz_ref[:, 2 * LANES:3 * LANES] = (du * gc).astype(BF16)

    blk = pl.BlockSpec((s, 3 * LANES), lambda j: (0, j))
    cws = pl.BlockSpec((3, LANES), lambda j: (0, j))
    return pl.pallas_call(
        body, name=name, grid=(CONV_W // LANES,),
        out_shape=(jax.ShapeDtypeStruct(z_conv.shape, BF16), jax.ShapeDtypeStruct((3, CONV_W), F32)),
        in_specs=[blk, cws, pl.BlockSpec((s, LANES), lambda j: (0, 4 + j))], out_specs=(blk, cws),
        compiler_params=_params(("parallel",), VMEM_BIG),
    )(z_conv, cw, da)


ATT_TQ = 256
ATT_Q_STEP = 1024
ATT_TQ_BWD = 512


def _head_mask(shape, hh):
    lane = lax.broadcasted_iota(jnp.int32, shape, 1)
    return (lane >= hh * V_DIM) & (lane < (hh + 1) * V_DIM)


def _attn_fwd(qf, kv, s, riding, name):
    t = kv.shape[0]
    step = min(ATT_Q_STEP, s)
    nq = s // step
    nr = riding.n

    def body(*refs):
        q_ref, k_ref, v_ref = refs[:3]
        o_ref, ob_ref, st_ref = refs[3 + nr:6 + nr]
        p, i = pl.program_id(0), pl.program_id(1)
        state = riding.run((p == 0) & (i == 0), (p == N_HEADS // 2 - 1) & (i == nq - 1),
                           refs[3:3 + nr], refs[6 + nr:6 + 2 * nr], refs[6 + 2 * nr:])
        v = v_ref[...]
        vlane = lax.broadcasted_iota(jnp.int32, v.shape, 1)
        one_lane = [(1 - hh) * V_DIM for hh in range(2)]
        vm = [jnp.where(_head_mask(v.shape, hh), v, jnp.where(vlane == one_lane[hh], 1.0, 0.0).astype(BF16))
              for hh in range(2)]

        def block(r, carry):
            rows = pl.ds(pl.multiple_of(r * ATT_TQ, ATT_TQ), ATT_TQ)
            olane = lax.broadcasted_iota(jnp.int32, (ATT_TQ, LANES), 1)
            acc = jnp.zeros((ATT_TQ, LANES), F32)
            stat = jnp.zeros((ATT_TQ, LANES), F32)
            for hh in range(2):
                sl = slice(hh * LANES, (hh + 1) * LANES)
                sc = lax.dot_general(q_ref[rows, sl], k_ref[:, sl], NT_DIMS, preferred_element_type=F32)
                mx = jnp.max(sc, axis=1, keepdims=True)
                e = jnp.exp2((sc - mx) * EXP2_SCALE).astype(BF16)
                res = jnp.dot(e, vm[hh], preferred_element_type=F32)
                den = jnp.sum(jnp.where(olane == one_lane[hh], res, 0.0), axis=1, keepdims=True)
                acc = acc + jnp.where(_head_mask(res.shape, hh), res * (1.0 / den), 0.0)
                stat = stat + jnp.where(olane == hh, mx * EXP2_SCALE + jnp.log(den) * LOG2_E, 0.0)
            o_ref[rows, :] = acc
            ob_ref[rows, :] = acc.astype(BF16)
            st_ref[:, rows] = stat.T[0:8, :]
            return carry

        lax.fori_loop(0, step // ATT_TQ, block, 0)
        riding.finish(state)

    o_spec = pl.BlockSpec((step, LANES), lambda p, i: (i, p))
    outs = pl.pallas_call(
        body, name=name, grid=(N_HEADS // 2, nq),
        out_shape=(jax.ShapeDtypeStruct((s, N_HEADS * V_DIM), F32),
                   jax.ShapeDtypeStruct((s, D_MODEL), BF16),
                   jax.ShapeDtypeStruct((N_HEADS // 2 * 8, s), F32), *riding.out_shape),
        in_specs=[pl.BlockSpec((step, 2 * LANES), lambda p, i: (i, p)),
                  pl.BlockSpec((t, 2 * LANES), lambda p, i: (0, p)),
                  pl.BlockSpec((t, LANES), lambda p, i: (0, N_HEADS + p)), *riding.specs],
        out_specs=(o_spec, o_spec, pl.BlockSpec((8, step), lambda p, i: (p, i)), *riding.specs),
        scratch_shapes=riding.scratch,
        compiler_params=_params(("arbitrary", "arbitrary"), VMEM_BIG),
    )(qf, kv, kv, *riding.arrays)
    return outs[0], outs[1], outs[2], list(outs[3:])


def _attn_bwd(qf, kv, o, da, stats, cos, sgn, riding, name):
    s, t = o.shape[0], kv.shape[0]
    ATT_TQ = ATT_TQ_BWD
    nq = s // ATT_TQ
    nr = riding.n

    def body(*refs):
        q_ref, k_ref, v_ref, o_ref, do_ref, st_ref, c_ref, s_ref = refs[:8]
        dq_ref, dk_ref, dv_ref = refs[8 + nr:11 + nr]
        dk_acc, dv_acc = refs[11 + 2 * nr:13 + 2 * nr]
        p, i = pl.program_id(0), pl.program_id(1)
        state = riding.run((p == 0) & (i == 0), (p == N_HEADS // 2 - 1) & (i == nq - 1),
                           refs[8:8 + nr], refs[11 + nr:11 + 2 * nr], refs[13 + 2 * nr:])

        @pl.when(i == 0)
        def _():
            dk_acc[...] = jnp.zeros_like(dk_acc)
            dv_acc[...] = jnp.zeros_like(dv_acc)

        v = v_ref[...]
        do = do_ref[...]
        od = do * o_ref[...]
        ones = jnp.ones((8, LANES), F32)
        for hh in range(2):
            sl = slice(hh * LANES, (hh + 1) * LANES)
            q, k = q_ref[:, sl], k_ref[:, sl]
            mask = _head_mask(do.shape, hh)
            dom = jnp.where(mask, do, 0.0).astype(BF16)
            delta = lax.dot_general(ones, jnp.where(mask, od, 0.0), NT_DIMS, preferred_element_type=F32,
                                    precision=lax.Precision.HIGHEST)[0:1, :]
            st = lax.dot_general(k, q, NT_DIMS, preferred_element_type=F32)
            pt = jnp.exp2(st * EXP2_SCALE - st_ref[hh:hh + 1, :]).astype(BF16)
            dpt = lax.dot_general(v, dom, NT_DIMS, preferred_element_type=F32)
            dst = (pt.astype(F32) * (dpt - delta)).astype(BF16)
            dv_acc[...] += jnp.dot(pt, dom, preferred_element_type=F32)
            dk_acc[:, sl] += jnp.dot(dst, q, preferred_element_type=F32)
            dq = lax.dot_general(dst, k, TN_DIMS, preferred_element_type=F32) * ATTN_SCALE
            dq_ref[:, sl] = _rope(dq, c_ref[...], s_ref[...], True).astype(BF16)

        @pl.when(i == nq - 1)
        def _():
            dk_ref[...] = (dk_acc[...] * ATTN_SCALE).astype(BF16)
            dv_ref[...] = dv_acc[...].astype(BF16)

        riding.finish(state)

    o_spec = pl.BlockSpec((ATT_TQ, LANES), lambda p, i: (i, p))
    tab = pl.BlockSpec((ATT_TQ, LANES), lambda p, i: (i, 0))
    outs = pl.pallas_call(
        body, name=name, grid=(N_HEADS // 2, nq),
        out_shape=(jax.ShapeDtypeStruct((s, N_HEADS * LANES), BF16),
                   jax.ShapeDtypeStruct((t, N_HEADS * LANES), BF16),
                   jax.ShapeDtypeStruct((t, N_HEADS * V_DIM), BF16), *riding.out_shape),
        in_specs=[pl.BlockSpec((ATT_TQ, 2 * LANES), lambda p, i: (i, p)),
                  pl.BlockSpec((t, 2 * LANES), lambda p, i: (0, p)),
                  pl.BlockSpec((t, LANES), lambda p, i: (0, N_HEADS + p)),
                  o_spec, o_spec,
                  pl.BlockSpec((8, ATT_TQ), lambda p, i: (p, i)), tab, tab, *riding.specs],
        out_specs=(pl.BlockSpec((ATT_TQ, 2 * LANES), lambda p, i: (i, p)),
                   pl.BlockSpec((t, 2 * LANES), lambda p, i: (0, p)),
                   pl.BlockSpec((t, LANES), lambda p, i: (0, p)), *riding.specs),
        scratch_shapes=[pltpu.VMEM((t, 2 * LANES), F32), pltpu.VMEM((t, LANES), F32), *riding.scratch],
        compiler_params=_params(("arbitrary", "arbitrary"), VMEM_BIG),
    )(qf, kv, kv, o, da, stats, cos, sgn, *riding.arrays)
    return outs[0], outs[1], outs[2], list(outs[3:])


def _silu(x):
    return x * (1.0 / (1.0 + jnp.exp(-x)))


def _prologue(c_rows, c_ctx, w_mod, b_cols, extra_rows, weights, name):
    nw = len(weights)
    d, cols = c_rows.shape[1], w_mod.shape[1]

    def body(c_ref, cctx_ref, wmod_ref, b_ref, x_ref, *rest):
        w_refs, (a_ref, modg_ref), wg_refs = rest[:nw], rest[nw:nw + 2], rest[nw + 2:2 * nw + 2]
        c_all, blk = rest[2 * nw + 2:2 * nw + 4]
        w_send, w_recv, w_local, c_send, c_recv, m_send, m_recv = rest[2 * nw + 4:]
        finish_weights = _two_level_gather(w_refs, wg_refs, w_send, w_recv, w_local)
        _direct_gather(c_ref, c_all, c_send, c_recv)()
        a_ref[...] = jnp.zeros_like(a_ref)
        for j in range(N_DEV):
            a_ref[j:j + 1, :] = c_all[j, 0:1, :]
        a_ref[N_DEV:N_DEV + 1, :] = cctx_ref[...]
        blk[0:16, :] = jnp.dot(_silu(a_ref[...]), wmod_ref[...], preferred_element_type=F32,
                               precision=lax.Precision.HIGHEST) + b_ref[...]
        blk[16:24, :] = x_ref[...]
        finish_mod = _direct_gather(blk, modg_ref, m_send, m_recv)
        finish_weights()
        finish_mod()

    vmem, hbm = pl.BlockSpec(memory_space=pltpu.VMEM), pl.BlockSpec(memory_space=pl.ANY)
    outs = pl.pallas_call(
        body, name=name,
        out_shape=(jax.ShapeDtypeStruct((16, d), F32), jax.ShapeDtypeStruct((N_DEV, 24, cols), F32),
                   *[jax.ShapeDtypeStruct((N_DEV,) + w.shape, w.dtype) for w in weights]),
        in_specs=[vmem] * 5 + [hbm] * nw, out_specs=(vmem, vmem, *[hbm] * nw),
        scratch_shapes=[pltpu.VMEM((N_DEV, 8, d), F32), pltpu.VMEM((24, cols), F32),
                        pltpu.SemaphoreType.DMA((7 * nw,)), pltpu.SemaphoreType.DMA((7 * nw,)),
                        pltpu.SemaphoreType.DMA((nw,)), pltpu.SemaphoreType.DMA((7,)), pltpu.SemaphoreType.DMA((7,)),
                        pltpu.SemaphoreType.DMA((7,)), pltpu.SemaphoreType.DMA((7,))],
        compiler_params=_params(None, VMEM_BIG),
    )(c_rows, c_ctx, w_mod, b_cols, extra_rows, *weights)
    return outs[0], outs[1], list(outs[2:])


def _adaln_bwd(a_t, w, d_ex, d_ctx, d_all, name):
    def body(at_ref, w_ref, dex_ref, dctx_ref, dall_ref, gw_ref, dsil_ref, dsum_ref):
        sil_t = _silu(at_ref[...])
        dctx = dctx_ref[...]
        row = dctx[0:1, :]
        for j in range(1, N_DEV):
            row = row + dctx[j:j + 1, :]
        rowi = lax.broadcasted_iota(jnp.int32, dctx.shape, 0)
        ctx_rows = jnp.where(rowi == 0, jnp.broadcast_to(row, dctx.shape), 0.0)
        hi = lax.Precision.HIGHEST
        d_rows = jnp.concatenate([dex_ref[...], ctx_rows], axis=0)
        gw_ref[...] = jnp.dot(sil_t, d_rows, preferred_element_type=F32, precision=hi)
        dsil_ref[...] = lax.dot_general(ctx_rows, w_ref[...], NT_DIMS, preferred_element_type=F32, precision=hi)
        tot = dall_ref[0]
        for j in range(1, N_DEV):
            tot = tot + dall_ref[j]
        dsum_ref[...] = tot

    return pl.pallas_call(
        body, name=name,
        out_shape=(jax.ShapeDtypeStruct(w.shape, F32), jax.ShapeDtypeStruct((8, w.shape[0]), F32),
                   jax.ShapeDtypeStruct(d_all.shape[1:], F32)),
        compiler_params=_params(None, VMEM_BIG),
    )(a_t, w, d_ex, d_ctx, d_all)


def _pack_small(sums1, sums2, fsums, sums1c, psums, d_cw, name):
    d = D_MODEL

    def body(s1_ref, s2_ref, f_ref, s1c_ref, p_ref, cw_ref, o_ref):
        o_ref[...] = jnp.zeros_like(o_ref)
        for col, (ref, r) in enumerate([(s1_ref, 1), (s1_ref, 0), (s2_ref, 2), (s2_ref, 1), (s2_ref, 0), (f_ref, 1)]):
            o_ref[0:1, col * d:(col + 1) * d] = ref[r:r + 1, :]
        o_ref[1:2, 0:d] = s1c_ref[1:2, :]
        o_ref[1:2, d:2 * d] = s1c_ref[0:1, :]
        o_ref[2:3, 0:Q_RANK] = p_ref[0:1, :]
        o_ref[2:3, Q_RANK:Q_RANK + KV_RANK] = p_ref[1:2, 0:KV_RANK]
        o_ref[2:3, Q_RANK + KV_RANK:Q_RANK + KV_RANK + d] = f_ref[0:1, :]
        for r in range(3):
            o_ref[3 + r:4 + r, 0:CONV_W] = cw_ref[r:r + 1, :]
        o_ref[6:7, 0:d] = f_ref[3:4, :]

    return pl.pallas_call(body, name=name, out_shape=jax.ShapeDtypeStruct((8, 6 * d), F32))(
        sums1, sums2, fsums, sums1c, psums, d_cw)


def _adam_math(w, g, m, v):
    nm = ADAM_B1 * m + (1.0 - ADAM_B1) * g
    nv = ADAM_B2 * v + (1.0 - ADAM_B2) * (g * g)
    m_hat = nm / (1.0 - ADAM_B1 ** ADAM_STEP)
    v_hat = nv / (1.0 - ADAM_B2 ** ADAM_STEP)
    return -ADAM_LR * (m_hat / (jnp.sqrt(v_hat) + ADAM_EPS) + ADAM_WD * w), nm, nv


def _small_update(dsum, dsil_all, g_cw, params, name):
    d = D_MODEL
    n = len(params)

    def body(*refs):
        dsum_ref, dsil_ref, gcw_ref = refs[:3]
        wmv = refs[3:3 + 3 * n]
        outs = refs[3 + 3 * n:]
        tot = dsil_ref[0]
        for j in range(1, N_DEV):
            tot = tot + dsil_ref[j]
        cv = wmv[0][...]
        sg = 1.0 / (1.0 + jnp.exp(-cv))
        off = Q_RANK + KV_RANK
        grads = [tot[0:1, :] * (sg * (1.0 + cv * (1.0 - sg))),
                 dsum_ref[0:1, :] + dsum_ref[1:2, :],
                 dsum_ref[2:3, 0:Q_RANK], dsum_ref[2:3, Q_RANK:off], dsum_ref[2:3, off:off + d],
                 gcw_ref[...]]
        for p, g in enumerate(grads):
            w_ref, m_ref, v_ref = wmv[3 * p:3 * p + 3]
            at = 0 if len(w_ref.shape) == 3 else Ellipsis
            res = (g,) + _adam_math(w_ref[at], g, m_ref[at], v_ref[at])
            for q, val in enumerate(res):
                outs[4 * p + q][at] = val

    flat = [a for wmv in params for a in wmv]
    out_shape = tuple(jax.ShapeDtypeStruct(wmv[0].shape, F32) for wmv in params for _ in range(4))
    outs = pl.pallas_call(body, name=name, out_shape=out_shape)(dsum, dsil_all, g_cw, *flat)
    return [outs[4 * p:4 * p + 4] for p in range(n)]


def _adamw(w, g, m, v, name, slots=False):
    _, rows, cols = w.shape
    tr = _pick(rows, (256, 128, 64, 32, 16, 8))

    def body(w_ref, g_ref, m_ref, v_ref, *outs):
        if slots:
            gv = g_ref[0].astype(F32)
            for j in range(1, N_DEV):
                gv = gv + g_ref[j].astype(F32)
            outs[0][...] = gv
        else:
            gv = g_ref[...]
        d_ref, nm_ref, nv_ref = outs[-3:]
        d_ref[...], nm_ref[...], nv_ref[...] = _adam_math(w_ref[...], gv, m_ref[...], v_ref[...])

    blk = pl.BlockSpec((None, tr, cols), lambda i: (0, i, 0))
    g_spec = (pl.BlockSpec((N_DEV, tr, cols), lambda i: (0, i, 0)) if slots
              else pl.BlockSpec((tr, cols), lambda i: (i, 0)))
    sh = jax.ShapeDtypeStruct((1, rows, cols), F32)
    n_out = 4 if slots else 3
    return pl.pallas_call(
        body, name=name, grid=(rows // tr,), out_shape=(sh,) * n_out,
        in_specs=[blk, g_spec, blk, blk], out_specs=(blk,) * n_out,
        compiler_params=_params(("parallel",), VMEM_BIG),
    )(w, g, m, v)


def _rope_tables(s, l):
    tok = np.arange(s)
    row = (tok // GRID_W).astype(np.float32)
    col = (tok % GRID_W).astype(np.float32)
    half = QK_ROPE // 2
    freqs = np.float32(ROPE_THETA) ** (-np.arange(0, half, 2, dtype=np.float32) / np.float32(half))
    dd = np.arange(QK_ROPE)
    pos = np.where((dd // half)[None, :] == 0, row[:, None], col[:, None]).astype(np.float32)
    ang = (pos * freqs[dd % (half // 2)][None, :]).astype(np.float32)
    sin = np.sin(ang).astype(np.float32)
    cos_t = np.ones((s + l, LANES), np.float32)
    sgn_t = np.zeros((s + l, LANES), np.float32)
    cos_t[:s, QK_NOPE:QK_NOPE + QK_ROPE] = np.cos(ang)
    sgn_t[:s, QK_NOPE:QK_NOPE + QK_ROPE] = np.where(((dd % half) // (half // 2))[None, :] == 0, -sin, sin)
    return jnp.asarray(cos_t), jnp.asarray(sgn_t)


def _slots_to_cols(g):
    return g.transpose(1, 0, 2).reshape(g.shape[1], N_DEV * g.shape[2])


def _cols_to_slots(w):
    return w.reshape(w.shape[0], N_DEV, w.shape[1] // N_DEV).transpose(1, 0, 2)


def _unpack_small_weights(g_in_t, g_uq, g_ukv):
    w_t = g_in_t.reshape(N_DEV * g_in_t.shape[1], D_MODEL)
    zeros = jnp.zeros((QK_NOPE, D_MODEL), BF16)
    win_head_t = jnp.concatenate([w_t[:Q_RANK + KV_RANK], zeros, w_t[Q_RANK + KV_RANK:MLA_IN],
                                  zeros[:LANES - QK_NOPE - QK_ROPE]], axis=0)
    win_conv_t = w_t[MLA_IN:].reshape(3, CONV_W // LANES, LANES, D_MODEL).transpose(1, 0, 2, 3)
    win_conv_t = win_conv_t.reshape(3 * CONV_W, D_MODEL)
    w_uq = _slots_to_cols(g_uq).reshape(Q_RANK, N_HEADS, QK_NOPE + QK_ROPE)
    wq = jnp.pad(w_uq, ((0, 0), (0, 0), (0, LANES - QK_NOPE - QK_ROPE))).reshape(Q_RANK, N_HEADS * LANES)
    w_ukv = _slots_to_cols(g_ukv).reshape(KV_RANK, N_HEADS, QK_NOPE + V_DIM)
    k_top = jnp.pad(w_ukv[:, :, :QK_NOPE], ((0, 0), (0, 0), (0, LANES - QK_NOPE))).reshape(KV_RANK, N_HEADS * LANES)
    v_top = w_ukv[:, :, QK_NOPE:].reshape(KV_RANK, N_HEADS * V_DIM)
    eye = jnp.pad(jnp.eye(QK_ROPE, dtype=BF16), ((QK_NOPE, LANES - QK_NOPE - QK_ROPE),) * 2)
    wk = jnp.concatenate([
        jnp.concatenate([k_top, v_top], axis=1),
        jnp.concatenate([jnp.tile(eye, (1, N_HEADS)), jnp.zeros((LANES, N_HEADS * V_DIM), BF16)], axis=1)], axis=0)
    return win_head_t, win_conv_t, wq, wk


def _pack_small_grads(d_head_t, d_conv_t, d_wq, d_wkk, d_wkv):
    d_conv_t = d_conv_t.reshape(CONV_W // LANES, 3, LANES, D_MODEL).transpose(1, 0, 2, 3).reshape(3 * CONV_W, D_MODEL)
    rope0 = Q_RANK + KV_RANK + QK_NOPE
    g_in_t = jnp.concatenate([d_head_t[:Q_RANK + KV_RANK], d_head_t[rope0:rope0 + QK_ROPE], d_conv_t], axis=0)
    g_in_t = g_in_t.reshape(N_DEV, -1, D_MODEL).astype(BF16)
    g_uq = d_wq.reshape(Q_RANK, N_HEADS, LANES)[:, :, :QK_NOPE + QK_ROPE].reshape(Q_RANK, -1)
    g_kn = d_wkk[:KV_RANK].reshape(KV_RANK, N_HEADS, LANES)[:, :, :QK_NOPE]
    g_v = d_wkv[:KV_RANK].reshape(KV_RANK, N_HEADS, V_DIM)
    g_ukv = jnp.concatenate([g_kn, g_v], axis=2).reshape(KV_RANK, -1)
    return [g_in_t] + [_cols_to_slots(g).astype(BF16) for g in (g_uq, g_ukv)]


def kernel(x, c, ctx, c_ctx, w_mod, b_mod, w_in, q_norm_g, w_uq, kv_norm_g, w_ukv, conv_w, w_out, w_mlp1, w_mlp2, final_norm_g, loss_target, m_c_ctx, m_w_mod, m_b_mod, m_w_in, m_q_norm_g, m_w_uq, m_kv_norm_g, m_w_ukv, m_conv_w, m_w_out, m_w_mlp1, m_w_mlp2, m_final_norm_g, v_c_ctx, v_w_mod, v_b_mod, v_w_in, v_q_norm_g, v_w_uq, v_kv_norm_g, v_w_ukv, v_conv_w, v_w_out, v_w_mlp1, v_w_mlp2, v_final_norm_g):
    me = _my_index()
    x2d, ctx2d, tgt = x[0], ctx[0], loss_target[0]
    s, l = x2d.shape[0], ctx2d.shape[0]
    t = s + l
    d = D_MODEL
    mod_cols = w_mod.shape[2]
    cw_cols = conv_w.shape[2]

    early = [w.astype(BF16) for w in (w_in[0].T, w_uq[0], w_ukv[0])]
    late = [w.astype(BF16) for w in (w_out[0], w_mlp1[0], w_mlp2[0])]
    b_cols = lax.dynamic_slice(b_mod, (0, me * mod_cols), (1, mod_cols))
    cw_blk = jnp.pad(conv_w[0], ((0, 5), (0, mod_cols - cw_cols)))
    a_rows, gathered, (g_in, g_uq, g_ukv) = _prologue(jnp.pad(c, ((0, 7), (0, 0))), c_ctx[None, :], w_mod[0], b_cols,
                                                      cw_blk, early, "prologue")
    mod_mine = lax.dynamic_index_in_dim(gathered, me, axis=1, keepdims=False).reshape(1, 6 * d)
    mod_ctx = gathered[:, 8, :].reshape(1, 6 * d)
    cw_full = gathered[:, 16:19, :cw_cols].transpose(1, 0, 2).reshape(3, CONV_W)

    win_head, win_conv, wq, wk = _unpack_small_weights(g_in, g_uq, g_ukv)
    wk_k, wk_v = wk[:, :N_HEADS * LANES], wk[:, N_HEADS * LANES:]
    cos, sgn = _rope_tables(s, l)

    h_all = _modulate_all(x2d, ctx2d, mod_mine, mod_ctx, "modulate1")
    tm_t = _pick(t, (1088, 768, 256))
    tk_t = _pick(t, (2176, 768, 256))
    z_head, cq, kv_in, qf, kv = _head_fwd(h_all, win_head, wq, wk, q_norm_g, kv_norm_g, cos, sgn, tm_t, "head_fwd")
    z_conv = _matmul(h_all, win_conv, mode="nt", name="in_proj_conv", m=s, tm=1024, tn=1536, tk=1024)
    attn, a_cat, stats, (g_out, w1, g_w2) = _attn_fwd(qf, kv, s, _Riding("gather", late), "attn_fwd")
    wo = g_out.reshape(d, d)
    w2 = g_w2.reshape(D_FF, d)
    a_cat = _conv_fwd(z_conv, cw_full, a_cat, "conv_fwd")
    (o, x1, h2), _ = _matmul_rows(a_cat, wo, _epi_resid_modulate, mode="nn", name="out_proj", tm=1024, tk=1024,
                                  rows=[x2d], vecs=[(mod_mine, 2), (mod_mine, 3), (mod_mine, 4)],
                                  out_dtypes=[F32, F32, BF16])
    u1, act = _matmul(h2, w1, mode="nn", name="mlp_up", tm=2048, tk=1024, epilogue="relu2", slots="b_cols")
    (dx2, dm, fsums), _ = _matmul_rows(act, w2, _epi_final, mode="nn", name="mlp_down", tm=512, tk=4096,
                                       rows=[x1, tgt], vecs=[(mod_mine, 5), (final_norm_g[None, :], 0)],
                                       out_dtypes=[F32, BF16], sums=True)

    d_w2 = _matmul(act, dm, mode="tn", name="d_w_mlp2", out_dtype=BF16, tm=2048, tn=1024, tk=1024)
    du1 = _matmul(dm, w2, mode="nt", name="d_act", out_dtype=BF16, tm=2048, tn=1024, tk=1024,
                  epilogue="drelu2", extra=(u1,))
    d_w1 = _matmul(h2, du1, mode="tn", name="d_w_mlp1", out_dtype=BF16, tm=1024, tk=4096, slots="out")
    (dx1, do, sums2), _ = _matmul_rows(du1, w1, _epi_modulate2_bwd, mode="nt", name="d_h2", tm=512, tk=4096,
                                       slots="b_contract", rows=[x1, dx2, o], vecs=[(mod_mine, 4), (mod_mine, 2)],
                                       out_dtypes=[F32, BF16], sums=True)
    d_wo = _matmul(a_cat, do, mode="tn", name="d_w_out", out_dtype=BF16, tm=1024, tn=1024, tk=2048)
    da = _matmul(do, wo, mode="nt", name="d_a", tm=1024, tn=1024, tk=1024)
    dz_conv, d_cw = _conv_bwd(z_conv, cw_full, da, "conv_bwd")
    ready = [d_wo.reshape(N_DEV, d // N_DEV, d), d_w1, d_w2.reshape(N_DEV, D_FF // N_DEV, d)]
    dq, dk, dv, rode = _attn_bwd(qf, kv, attn, da, stats, cos, sgn, _Riding("exchange", ready), "attn_bwd")
    d_wq = _matmul(cq, dq, mode="tn", name="d_w_uq", k=s, tm=256, tn=1024, tk=4096)
    d_wkk = _matmul(kv_in, dk, mode="tn", name="d_w_ukv_k", tm=256, tn=1024, tk=tk_t)
    d_wkv = _matmul(kv_in, dv, mode="tn", name="d_w_ukv_v", tm=256, tn=512, tk=tk_t)
    dz_head, dh_head, psums = _head_bwd(dq, dk, dv, z_head, wq, wk_k, wk_v, win_head, q_norm_g, kv_norm_g, cos, sgn, s,
                                        "head_bwd")
    d_head = _matmul(dz_head, h_all, mode="tn", name="d_w_in_head", tm=512, tn=1024, tk=tk_t)
    d_conv = _matmul(dz_conv, h_all, mode="tn", name="d_w_in_conv", k=s, tm=1536, tn=1024, tk=2048)
    send = _pack_small_grads(d_head, d_conv, d_wq, d_wkk, d_wkv)
    (grad_x, sums1), got = _matmul_rows(dz_conv, win_conv, _epi_modulate1_bwd, mode="nn", name="d_h1", tm=1024,
                                        tk=win_conv.shape[0], rows=[dh_head, x2d, dx1], vecs=[(mod_mine, 1)],
                                        out_dtypes=[F32], sums=True, riding=_Riding("exchange", send))
    sums1c = _modulate_sums(dh_head, s // ROW_TILE, ctx2d)

    small = _pack_small(sums1, sums2, fsums, sums1c, psums, d_cw, "pack_small")
    (d_all,) = _all_gather([small], "gather_small_grads", True)
    d_cols = lax.dynamic_slice_in_dim(d_all, me * mod_cols, mod_cols, axis=2)
    g_w_mod, dsil, dsum = _adaln_bwd(a_rows.T, w_mod[0], d_cols[:, 0, :], d_cols[:, 1, :], d_all, "adaln_bwd")
    (dsil_all,) = _all_gather([dsil], "gather_d_cctx", True)
    loss = dsum[6, 0]
    g_cw = lax.dynamic_slice(dsum, (3, me * cw_cols), (3, cw_cols))

    slots = dict(zip(["w_in", "w_uq", "w_ukv"], got))
    slots.update(zip(["w_out", "w_mlp1", "w_mlp2"], rode))

    grads = {}
    weights = {"c_ctx": c_ctx, "w_mod": w_mod, "b_mod": b_mod, "w_in": w_in, "q_norm_g": q_norm_g, "w_uq": w_uq,
               "kv_norm_g": kv_norm_g, "w_ukv": w_ukv, "conv_w": conv_w, "w_out": w_out, "w_mlp1": w_mlp1,
               "w_mlp2": w_mlp2, "final_norm_g": final_norm_g}
    m_in = {"c_ctx": m_c_ctx, "w_mod": m_w_mod, "b_mod": m_b_mod, "w_in": m_w_in, "q_norm_g": m_q_norm_g,
            "w_uq": m_w_uq, "kv_norm_g": m_kv_norm_g, "w_ukv": m_w_ukv, "conv_w": m_conv_w, "w_out": m_w_out,
            "w_mlp1": m_w_mlp1, "w_mlp2": m_w_mlp2, "final_norm_g": m_final_norm_g}
    v_in = {"c_ctx": v_c_ctx, "w_mod": v_w_mod, "b_mod": v_b_mod, "w_in": v_w_in, "q_norm_g": v_q_norm_g,
            "w_uq": v_w_uq, "kv_norm_g": v_kv_norm_g, "w_ukv": v_w_ukv, "conv_w": v_conv_w, "w_out": v_w_out,
            "w_mlp1": v_w_mlp1, "w_mlp2": v_w_mlp2, "final_norm_g": v_final_norm_g}
    names = list(weights)
    small_names = ["c_ctx", "b_mod", "q_norm_g", "kv_norm_g", "final_norm_g", "conv_w"]
    delta, new_m, new_v = {}, {}, {}

    def as_rows(a):
        return a[None, :] if a.ndim == 1 else a

    small_out = _small_update(dsum, dsil_all, g_cw, [[as_rows(src[n]) for src in (weights, m_in, v_in)]
                                                      for n in small_names], "small_update")
    for n, outs in zip(small_names, small_out):
        grads[n], delta[n], new_m[n], new_v[n] = [a.reshape(weights[n].shape) for a in outs]
    for n in names:
        if n in small_names:
            continue
        if n == "w_in":
            wmv = [jnp.swapaxes(src[n], 1, 2) for src in (weights, m_in, v_in)]
            outs = _adamw(wmv[0], slots[n], wmv[1], wmv[2], "adamw_" + n, slots=True)
            grads[n], delta[n], new_m[n], new_v[n] = [jnp.swapaxes(a, 1, 2) for a in outs]
        elif n in slots:
            grads[n], delta[n], new_m[n], new_v[n] = _adamw(weights[n], slots[n], m_in[n], v_in[n], "adamw_" + n,
                                                            slots=True)
        else:
            delta[n], new_m[n], new_v[n] = _adamw(weights[n], g_w_mod, m_in[n], v_in[n], "adamw_" + n)
            grads[n] = g_w_mod[None]

    return (loss, grad_x[None], *[grads[n] for n in names], *[delta[n] for n in names],
            *[new_m[n] for n in names], *[new_v[n] for n in names])
```

```python
import math

import jax
import jax.numpy as jnp
import numpy as np
from jax import lax
from jax.experimental import pallas as pl
from jax.experimental.pallas import tpu as pltpu

F32 = jnp.float32
BF16 = jnp.bfloat16

D_MODEL = 1024
GRID_W = 64
N_HEADS = 8
QK_NOPE = 64
QK_ROPE = 32
V_DIM = 64
Q_RANK = 256
KV_RANK = 128
MLA_IN = Q_RANK + KV_RANK + QK_ROPE
CONV_W = 512
HEAD_COLS = 512
D_FF = 4096
ROPE_THETA = 10000.0
EPS = 1e-6
ATTN_SCALE = 1.0 / math.sqrt(QK_NOPE + QK_ROPE)
LOG2_E = 1.0 / math.log(2.0)
EXP2_SCALE = ATTN_SCALE * LOG2_E
N_DEV = 8
LANES = 128

ADAM_LR, ADAM_B1, ADAM_B2, ADAM_EPS, ADAM_WD, ADAM_STEP = 0.001, 0.9, 0.999, 1e-08, 0.01, 10

ROW_TILE = 256
VMEM_BIG = 60 * 1024 * 1024


def _params(sem=None, vmem=None):
    return pltpu.CompilerParams(dimension_semantics=sem, vmem_limit_bytes=vmem)


def _pick(n, prefs):
    for p in prefs:
        if n % p == 0:
            return p
    return n


def _my_index():
    return 4 * lax.axis_index("x") + 2 * lax.axis_index("y") + lax.axis_index("c")


def _two_level_gather(x_refs, out_refs, send_sems, recv_sems, local_sems):
    n = len(x_refs)
    x, y, c = lax.axis_index("x"), lax.axis_index("y"), lax.axis_index("c")
    me, sibling = (x, y, c), (x, y, 1 - c)
    chips = [(1 - x, y), (x, 1 - y), (1 - x, 1 - y)]

    def slot(a, px, py, pc):
        return out_refs[a].at[4 * px + 2 * py + pc]

    def copy(a, k, block, to, src=None):
        return pltpu.make_async_remote_copy(
            src_ref=slot(a, *block) if src is None else src, dst_ref=slot(a, *block),
            send_sem=send_sems.at[7 * a + k], recv_sem=recv_sems.at[7 * a + k],
            device_id=to, device_id_type=pl.DeviceIdType.MESH)

    mine = [pltpu.make_async_copy(x_refs[a], slot(a, *me), local_sems.at[a]) for a in range(n)]
    for cp in mine:
        cp.start()
    started = []
    for a in range(n):
        first = [copy(a, 0, me, sibling, src=x_refs[a])]
        first += [copy(a, 1 + j, me, (*chip, c), src=x_refs[a]) for j, chip in enumerate(chips)]
        for cp in first:
            cp.start()
        started += first

    def finish():
        for a in range(n):
            for j, chip in enumerate(chips):
                copy(a, 1 + j, (*chip, c), me).wait_recv()
                passed = copy(a, 4 + j, (*chip, c), sibling)
                passed.start()
                started.append(passed)
        for a in range(n):
            copy(a, 0, sibling, me).wait_recv()
            for j, chip in enumerate(chips):
                copy(a, 4 + j, (*chip, 1 - c), me).wait_recv()
        for cp in started:
            cp.wait_send()
        for cp in mine:
            cp.wait()

    return finish


def _direct_gather(src_ref, dst_ref, send_sems, recv_sems):
    x, y, c = lax.axis_index("x"), lax.axis_index("y"), lax.axis_index("c")
    me = 4 * x + 2 * y + c
    dst_ref[me] = src_ref[...]
    sends, landings = [], []
    for k in range(1, N_DEV):
        peer = (1 - x if k & 4 else x, 1 - y if k & 2 else y, 1 - c if k & 1 else c)
        pid = 4 * peer[0] + 2 * peer[1] + peer[2]
        for dst, out in ((me, sends), (pid, landings)):
            out.append(pltpu.make_async_remote_copy(
                src_ref=src_ref, dst_ref=dst_ref.at[dst], send_sem=send_sems.at[k - 1], recv_sem=recv_sems.at[k - 1],
                device_id=peer, device_id_type=pl.DeviceIdType.MESH))
    for cp in sends:
        cp.start()

    def finish():
        for cp in landings:
            cp.wait_recv()
        for cp in sends:
            cp.wait_send()

    return finish


def _all_gather(arrays, name, in_vmem):
    space = pltpu.VMEM if in_vmem else pl.ANY
    n = len(arrays)

    def body(*refs):
        _two_level_gather(refs[:n], refs[n:2 * n], *refs[2 * n:])()

    outs = pl.pallas_call(
        body, name=name,
        out_shape=tuple(jax.ShapeDtypeStruct((N_DEV,) + a.shape, a.dtype) for a in arrays),
        in_specs=[pl.BlockSpec(memory_space=space)] * n,
        out_specs=tuple(pl.BlockSpec(memory_space=space) for _ in arrays),
        scratch_shapes=[pltpu.SemaphoreType.DMA((7 * n,)), pltpu.SemaphoreType.DMA((7 * n,)),
                        pltpu.SemaphoreType.DMA((n,))],
    )(*arrays)
    return list(outs)


class _Riding:
    def __init__(self, kind, arrays):
        self.kind, self.arrays, self.n = kind, list(arrays), len(arrays)
        lead = (N_DEV,) if kind == "gather" else ()
        self.out_shape = [jax.ShapeDtypeStruct(lead + a.shape, a.dtype) for a in self.arrays]
        self.specs = [pl.BlockSpec(memory_space=pl.ANY)] * self.n
        self.scratch = [pltpu.SemaphoreType.DMA((7 * self.n,)), pltpu.SemaphoreType.DMA((7 * self.n,)),
                        pltpu.SemaphoreType.DMA((self.n,))]

    def copies(self, x_refs, y_refs, send_sems, recv_sems, local_sems):
        x, y, c = lax.axis_index("x"), lax.axis_index("y"), lax.axis_index("c")
        me = 4 * x + 2 * y + c
        local, sends, landings = [], [], []
        for a in range(self.n):
            src_mine = x_refs[a] if self.kind == "gather" else x_refs[a].at[me]
            local.append(pltpu.make_async_copy(src_mine, y_refs[a].at[me], local_sems.at[a]))
            for k in range(1, N_DEV):
                peer = (1 - x if k & 4 else x, 1 - y if k & 2 else y, 1 - c if k & 1 else c)
                pid = 4 * peer[0] + 2 * peer[1] + peer[2]
                src = x_refs[a] if self.kind == "gather" else x_refs[a].at[pid]
                for dst, out in ((me, sends), (pid, landings)):
                    out.append(pltpu.make_async_remote_copy(
                        src_ref=src, dst_ref=y_refs[a].at[dst],
                        send_sem=send_sems.at[7 * a + k - 1], recv_sem=recv_sems.at[7 * a + k - 1],
                        device_id=peer, device_id_type=pl.DeviceIdType.MESH))
        return local, sends, landings

    def run(self, first, last, x_refs, y_refs, sems):
        if self.n == 0:
            return None
        local, sends, landings = self.copies(x_refs, y_refs, *sems)

        @pl.when(first)
        def _():
            for cp in local + sends:
                cp.start()

        return local, sends, landings, last

    @staticmethod
    def finish(state):
        if state is None:
            return
        local, sends, landings, last = state

        @pl.when(last)
        def _():
            for cp in landings:
                cp.wait_recv()
            for cp in sends:
                cp.wait_send()
            for cp in local:
                cp.wait()


_DIMS = {"nn": (((1,), (0,)), ((), ())), "nt": (((1,), (1,)), ((), ())), "tn": (((0,), (0,)), ((), ()))}
NT_DIMS = _DIMS["nt"]
TN_DIMS = _DIMS["tn"]


def _swap8(x):
    lane = lax.broadcasted_iota(jnp.int32, x.shape, 1)
    return jnp.where((lane & 15) < 8, pltpu.roll(x, LANES - 8, 1), pltpu.roll(x, 8, 1))


def _rope(x, cos, sgn, bwd):
    return x * cos + (_swap8(x * sgn) if bwd else _swap8(x) * sgn)


def _matmul(a, b, *, mode, name, out_dtype=F32, tm=512, tn=512, tk=512, m=None, k=None,
            epilogue=None, extra=(), slots=None):
    if mode == "nn":
        m = a.shape[0] if m is None else m
        k = a.shape[1]
        n = N_DEV * b.shape[2] if slots == "b_cols" else b.shape[1]
    elif mode == "nt":
        m = a.shape[0] if m is None else m
        k = a.shape[1]
        n = b.shape[0]
    else:
        k = a.shape[0] if k is None else k
        m, n = a.shape[1], b.shape[1]
    tm, tn, tk = min(tm, m), min(tn, n), min(tk, k)
    if slots == "b_cols":
        tn = b.shape[2]
    if slots == "out":
        tn = n // N_DEV
    assert m % tm == 0 and n % tn == 0 and k % tk == 0, (name, m, n, k, tm, tn, tk)
    nk = k // tk
    dims = _DIMS[mode]
    a_spec = (pl.BlockSpec((tk, tm), lambda i, j, kk: (kk, i)) if mode == "tn"
              else pl.BlockSpec((tm, tk), lambda i, j, kk: (i, kk)))
    if slots == "b_cols":
        b_spec = pl.BlockSpec((None, tk, tn), lambda i, j, kk: (j, kk, 0))
    elif mode == "nt":
        b_spec = pl.BlockSpec((tn, tk), lambda i, j, kk: (j, kk))
    else:
        b_spec = pl.BlockSpec((tk, tn), lambda i, j, kk: (kk, j))
    tile = pl.BlockSpec((tm, tn), lambda i, j, kk: (i, j))
    if slots == "out":
        o_spec = pl.BlockSpec((None, tm, tn), lambda i, j, kk: (j, i, 0))
        o_shape = (N_DEV, m, tn)
    else:
        o_spec, o_shape = tile, (m, n)
    in_specs, args = [a_spec, b_spec], [a, b]
    if epilogue == "drelu2":
        in_specs.append(tile)
    args += list(extra)
    if epilogue == "relu2":
        out_shape = (jax.ShapeDtypeStruct(o_shape, BF16), jax.ShapeDtypeStruct(o_shape, BF16))
        out_specs = (o_spec, o_spec)
    else:
        out_shape = jax.ShapeDtypeStruct(o_shape, out_dtype)
        out_specs = o_spec
    n_in = len(args)
    n_out = 2 if epilogue == "relu2" else 1

    def body(*refs):
        a_ref, b_ref = refs[0], refs[1]
        outs = refs[n_in:n_in + n_out]
        part = lax.dot_general(a_ref[...], b_ref[...], dims, preferred_element_type=F32)

        def finish(acc):
            if epilogue == "relu2":
                outs[0][...] = acc.astype(BF16)
                r = jnp.maximum(acc, 0.0)
                outs[1][...] = (r * r).astype(BF16)
            elif epilogue == "drelu2":
                u = refs[2][...].astype(F32)
                outs[0][...] = (acc * (2.0 * jnp.maximum(u, 0.0))).astype(out_dtype)
            else:
                outs[0][...] = acc.astype(out_dtype)

        if nk == 1:
            finish(part)
        else:
            acc_ref = refs[n_in + n_out]
            kk = pl.program_id(2)

            @pl.when(kk == 0)
            def _():
                acc_ref[...] = part

            @pl.when(kk > 0)
            def _():
                acc_ref[...] += part

            @pl.when(kk == nk - 1)
            def _():
                finish(acc_ref[...])

    return pl.pallas_call(
        body, name=name, grid=(m // tm, n // tn, nk),
        out_shape=out_shape, in_specs=in_specs, out_specs=out_specs,
        scratch_shapes=[pltpu.VMEM((tm, tn), F32)] if nk > 1 else [],
        compiler_params=_params(("parallel", "parallel", "arbitrary"), VMEM_BIG),
    )(*args)


def _rstd(x):
    return lax.rsqrt(jnp.mean(x * x, axis=1, keepdims=True) + EPS)


def _norm_bwd(dxn, xn, r):
    return r * (dxn - xn * jnp.mean(dxn * xn, axis=1, keepdims=True))


def _vec(col):
    return pl.BlockSpec((1, D_MODEL), lambda i: (0, col))


def _matmul_rows(a, b, epi, *, mode, name, tm, tk, rows=(), vecs=(), out_dtypes=(), sums=False, slots=None,
                 riding=None):
    m, k = a.shape
    n = D_MODEL
    tm, tk = min(tm, m), min(tk, k)
    riding = riding or _Riding("gather", [])
    group = 1
    if slots == "b_contract":
        group = max(1, tk // b.shape[2])
        tk = group * b.shape[2]
        b_spec = pl.BlockSpec((group, n, tk // group), lambda i, kk: (kk, 0, 0))
    elif mode == "nt":
        b_spec = pl.BlockSpec((n, tk), lambda i, kk: (0, kk))
    else:
        b_spec = pl.BlockSpec((tk, n), lambda i, kk: (kk, 0))
    assert m % tm == 0 and k % tk == 0, (name, m, k, tm, tk)
    ni, nk = m // tm, k // tk
    dims = _DIMS[mode]
    tile = pl.BlockSpec((tm, n), lambda i, kk: (i, 0))
    in_specs = [pl.BlockSpec((tm, tk), lambda i, kk: (i, kk)), b_spec] + [tile] * len(rows)
    in_specs += [pl.BlockSpec((1, n), lambda i, kk, col=col: (0, col)) for _, col in vecs]
    args = [a, b, *rows, *[v for v, _ in vecs]]
    out_shape = [jax.ShapeDtypeStruct((m, n), dt) for dt in out_dtypes]
    out_specs = [tile] * len(out_dtypes)
    if sums:
        out_shape.append(jax.ShapeDtypeStruct((8, n), F32))
        out_specs.append(pl.BlockSpec((8, n), lambda i, kk: (0, 0)))
    n_rows, n_vecs, n_outs, nr = len(rows), len(vecs), len(out_dtypes), riding.n
    n_in = 2 + n_rows + n_vecs

    def body(*refs):
        a_ref, b_ref = refs[0], refs[1]
        row_refs = refs[2:2 + n_rows]
        vec_refs = refs[2 + n_rows:n_in]
        x_refs = refs[n_in:n_in + nr]
        out_refs = refs[n_in + nr:n_in + nr + n_outs]
        pos = n_in + nr + n_outs
        sums_ref = refs[pos] if sums else None
        pos += 1 if sums else 0
        y_refs = refs[pos:pos + nr]
        pos += nr
        acc_ref = refs[pos] if nk > 1 else None
        sem_refs = refs[pos + (1 if nk > 1 else 0):]
        i, kk = pl.program_id(0), pl.program_id(1)
        state = riding.run((i == 0) & (kk == 0), (i == ni - 1) & (kk == nk - 1), x_refs, y_refs, sem_refs)
        if slots == "b_contract":
            c = tk // group
            part = lax.dot_general(a_ref[:, 0:c], b_ref[0], dims, preferred_element_type=F32)
            for u in range(1, group):
                part = part + lax.dot_general(a_ref[:, u * c:(u + 1) * c], b_ref[u], dims, preferred_element_type=F32)
        else:
            part = lax.dot_general(a_ref[...], b_ref[...], dims, preferred_element_type=F32)

        def finish(acc):
            nsub = tm // ROW_TILE
            for r in range(nsub):
                blk = pl.ds(r * ROW_TILE, ROW_TILE)
                epi(acc[r * ROW_TILE:(r + 1) * ROW_TILE], [ref.at[blk] for ref in row_refs], vec_refs,
                    [ref.at[blk] for ref in out_refs], sums_ref,
                    (i == 0) if r == 0 else None, (i == ni - 1) if r == nsub - 1 else None)

        if nk == 1:
            finish(part)
        else:
            @pl.when(kk == 0)
            def _():
                acc_ref[...] = part

            @pl.when(kk > 0)
            def _():
                acc_ref[...] += part

            @pl.when(kk == nk - 1)
            def _():
                finish(acc_ref)

        riding.finish(state)

    outs = pl.pallas_call(
        body, name=name, grid=(ni, nk),
        out_shape=(*out_shape, *riding.out_shape),
        in_specs=[*in_specs, *riding.specs], out_specs=(*out_specs, *riding.specs),
        scratch_shapes=([pltpu.VMEM((tm, n), F32)] if nk > 1 else []) + (riding.scratch if nr else []),
        compiler_params=_params(("arbitrary", "arbitrary"), VMEM_BIG),
    )(*args, *riding.arrays)
    n_own = len(out_shape)
    return list(outs[:n_own]), list(outs[n_own:])


def _zero_sums_at_start(sums_ref, first):
    if first is not None:
        @pl.when(first)
        def _():
            sums_ref[...] = jnp.zeros_like(sums_ref)


def _epi_resid_modulate(acc, rows, vecs, outs, sums_ref, first, last):
    (x_ref,), (g_ref, sh_ref, sc_ref) = rows, vecs
    x1 = x_ref[...] + g_ref[...] * acc
    outs[0][...] = acc
    outs[1][...] = x1
    outs[2][...] = (x1 * _rstd(x1) * (1.0 + sc_ref[...]) + sh_ref[...]).astype(BF16)


def _epi_final(acc, rows, vecs, outs, sums_ref, first, last):
    (x1_ref, t_ref), (g_ref, gf_ref) = rows, vecs
    d = acc.shape[1]
    x2 = x1_ref[...] + g_ref[...] * acc
    r = _rstd(x2)
    xn = x2 * r
    err = xn * gf_ref[...] - t_ref[...]
    dy = err * (1.0 / d)
    dx2 = _norm_bwd(dy * gf_ref[...], xn, r)
    outs[0][...] = dx2
    outs[1][...] = (dx2 * g_ref[...]).astype(BF16)
    _zero_sums_at_start(sums_ref, first)
    sums_ref[0:1, :] += jnp.sum(dy * xn, axis=0, keepdims=True)
    sums_ref[1:2, :] += jnp.sum(dx2 * acc, axis=0, keepdims=True)
    sums_ref[2:3, :] += jnp.sum(err * err, axis=0, keepdims=True)

    if last is not None:
        @pl.when(last)
        def _():
            tot = jnp.sum(sums_ref[2:3, :], axis=1, keepdims=True) * (0.5 / d)
            sums_ref[3:4, :] = jnp.broadcast_to(tot, (1, d))


def _epi_modulate2_bwd(acc, rows, vecs, outs, sums_ref, first, last):
    (x_ref, dres_ref, o_ref), (sc_ref, g_ref) = rows, vecs
    x = x_ref[...]
    r = _rstd(x)
    xn = x * r
    dx = dres_ref[...] + _norm_bwd(acc * (1.0 + sc_ref[...]), xn, r)
    outs[0][...] = dx
    outs[1][...] = (dx * g_ref[...]).astype(BF16)
    _zero_sums_at_start(sums_ref, first)
    sums_ref[0:1, :] += jnp.sum(acc * xn, axis=0, keepdims=True)
    sums_ref[1:2, :] += jnp.sum(acc, axis=0, keepdims=True)
    sums_ref[2:3, :] += jnp.sum(dx * o_ref[...], axis=0, keepdims=True)


def _epi_modulate1_bwd(acc, rows, vecs, outs, sums_ref, first, last):
    (add_ref, x_ref, dres_ref), (sc_ref,) = rows, vecs
    dh = acc + add_ref[...]
    x = x_ref[...]
    r = _rstd(x)
    xn = x * r
    outs[0][...] = dres_ref[...] + _norm_bwd(dh * (1.0 + sc_ref[...]), xn, r)
    _zero_sums_at_start(sums_ref, first)
    sums_ref[0:1, :] += jnp.sum(dh * xn, axis=0, keepdims=True)
    sums_ref[1:2, :] += jnp.sum(dh, axis=0, keepdims=True)


def _modulate_all(x, ctx, mod, mod_ctx, name):
    s, d = x.shape
    t = s + ctx.shape[0]
    ns = s // ROW_TILE
    nc = ctx.shape[0] // ROW_TILE

    def body(x_ref, c_ref, sh_ref, sc_ref, shc_ref, scc_ref, h_ref):
        i = pl.program_id(0)

        @pl.when(i < ns)
        def _():
            v = x_ref[...]
            h_ref[...] = (v * _rstd(v) * (1.0 + sc_ref[...]) + sh_ref[...]).astype(BF16)

        @pl.when(i >= ns)
        def _():
            v = c_ref[...]
            h_ref[...] = (v * _rstd(v) * (1.0 + scc_ref[...]) + shc_ref[...]).astype(BF16)

    return pl.pallas_call(
        body, name=name, grid=(ns + nc,),
        out_shape=jax.ShapeDtypeStruct((t, d), BF16),
        in_specs=[pl.BlockSpec((ROW_TILE, d), lambda i: (jnp.minimum(i, ns - 1), 0)),
                  pl.BlockSpec((ROW_TILE, d), lambda i: (jnp.maximum(i - ns, 0), 0)),
                  _vec(0), _vec(1), _vec(0), _vec(1)],
        out_specs=pl.BlockSpec((ROW_TILE, d), lambda i: (i, 0)),
        compiler_params=_params(("arbitrary",)),
    )(x, ctx, mod, mod, mod_ctx, mod_ctx)


def _modulate_sums(dh, row_off, xsrc):
    s, d = xsrc.shape

    def body(dh_ref, x_ref, sums_ref):
        i = pl.program_id(0)
        x = x_ref[...]
        dhv = dh_ref[...]

        @pl.when(i == 0)
        def _():
            sums_ref[...] = jnp.zeros_like(sums_ref)

        sums_ref[0:1, :] += jnp.sum(dhv * (x * _rstd(x)), axis=0, keepdims=True)
        sums_ref[1:2, :] += jnp.sum(dhv, axis=0, keepdims=True)

    return pl.pallas_call(
        body, name="modulate1_ctx_bwd", grid=(s // ROW_TILE,),
        out_shape=jax.ShapeDtypeStruct((8, d), F32),
        in_specs=[pl.BlockSpec((ROW_TILE, d), lambda i: (i + row_off, 0)), pl.BlockSpec((ROW_TILE, d), lambda i: (i, 0))],
        out_specs=pl.BlockSpec((8, d), lambda i: (0, 0)),
        compiler_params=_params(("arbitrary",)),
    )(dh, xsrc)


def _head_fwd(h_all, win_head, wq, wk, q_gain, kv_gain, cos, sgn, tm, name):
    t, d = h_all.shape
    nq, nkv = wq.shape[1], wk.shape[1]

    def body(h_ref, wi_ref, wq_ref, wk_ref, qg_ref, kg_ref, c_ref, s_ref, z_ref, cq_ref, kvin_ref, qf_ref, kv_ref):
        z = lax.dot_general(h_ref[...], wi_ref[...], NT_DIMS, preferred_element_type=F32)
        z_ref[...] = z
        cos, sgn = c_ref[...], s_ref[...]
        zq = z[:, 0:Q_RANK]
        cq = (zq * _rstd(zq) * qg_ref[...]).astype(BF16)
        cq_ref[...] = cq
        zk = z[:, Q_RANK:Q_RANK + KV_RANK]
        kv_in = jnp.concatenate([(zk * _rstd(zk) * kg_ref[...]).astype(BF16),
                                 _rope(z[:, Q_RANK + KV_RANK:HEAD_COLS], cos, sgn, False).astype(BF16)], axis=1)
        kvin_ref[...] = kv_in
        q = jnp.dot(cq, wq_ref[...], preferred_element_type=F32)
        for h in range(nq // LANES):
            sl = slice(h * LANES, (h + 1) * LANES)
            qf_ref[:, sl] = _rope(q[:, sl], cos, sgn, False).astype(BF16)
        kv_ref[...] = jnp.dot(kv_in, wk_ref[...], preferred_element_type=F32).astype(BF16)

    def row(w):
        return pl.BlockSpec((tm, w), lambda i: (i, 0))

    def whole(a):
        return pl.BlockSpec(a.shape, lambda i: (0, 0))

    return pl.pallas_call(
        body, name=name, grid=(t // tm,),
        out_shape=(jax.ShapeDtypeStruct((t, HEAD_COLS), F32), jax.ShapeDtypeStruct((t, Q_RANK), BF16),
                   jax.ShapeDtypeStruct((t, KV_RANK + LANES), BF16), jax.ShapeDtypeStruct((t, nq), BF16),
                   jax.ShapeDtypeStruct((t, nkv), BF16)),
        in_specs=[row(d), whole(win_head), whole(wq), whole(wk), whole(q_gain), whole(kv_gain), row(LANES), row(LANES)],
        out_specs=(row(HEAD_COLS), row(Q_RANK), row(KV_RANK + LANES), row(nq), row(nkv)),
        compiler_params=_params(("parallel",), VMEM_BIG),
    )(h_all, win_head, wq, wk, q_gain, kv_gain, cos, sgn)


def _head_bwd(dq, dk, dv, z, wq, wk_k, wk_v, win_head, q_gain, kv_gain, cos, sgn, s, name):
    t = z.shape[0]
    ns = s // ROW_TILE

    def body(dq_ref, dk_ref, dv_ref, z_ref, wq_ref, wkk_ref, wkv_ref, wi_ref, qg_ref, kg_ref, c_ref, s_ref,
             dz_ref, dh_ref, sums_ref):
        i = pl.program_id(0)

        @pl.when(i == 0)
        def _():
            sums_ref[...] = jnp.zeros_like(sums_ref)

        @pl.when(i < ns)
        def _():
            dc = lax.dot_general(dq_ref[...], wq_ref[...], NT_DIMS, preferred_element_type=F32)
            zq = z_ref[:, 0:Q_RANK]
            r = _rstd(zq)
            zn = zq * r
            sums_ref[0:1, :] += jnp.sum(dc * zn, axis=0, keepdims=True)
            dz_ref[:, 0:Q_RANK] = _norm_bwd(dc * qg_ref[...], zn, r).astype(BF16)

        @pl.when(i >= ns)
        def _():
            dz_ref[:, 0:Q_RANK] = jnp.zeros((ROW_TILE, Q_RANK), BF16)

        dkv = (lax.dot_general(dk_ref[...], wkk_ref[...], NT_DIMS, preferred_element_type=F32)
               + lax.dot_general(dv_ref[...], wkv_ref[...], NT_DIMS, preferred_element_type=F32))
        zk = z_ref[:, Q_RANK:Q_RANK + KV_RANK]
        r = _rstd(zk)
        zn = zk * r
        dc = dkv[:, 0:KV_RANK]
        sums_ref[1:2, 0:KV_RANK] += jnp.sum(dc * zn, axis=0, keepdims=True)
        dz_ref[:, Q_RANK:Q_RANK + KV_RANK] = _norm_bwd(dc * kg_ref[...], zn, r).astype(BF16)
        dz_ref[:, Q_RANK + KV_RANK:HEAD_COLS] = _rope(dkv[:, KV_RANK:KV_RANK + LANES], c_ref[...], s_ref[...],
                                                       True).astype(BF16)
        dh_ref[...] = jnp.dot(dz_ref[...], wi_ref[...], preferred_element_type=F32)

    def row(w):
        return pl.BlockSpec((ROW_TILE, w), lambda i: (i, 0))

    def whole(a):
        return pl.BlockSpec(a.shape, lambda i: (0, 0))

    return pl.pallas_call(
        body, name=name, grid=(t // ROW_TILE,),
        out_shape=(jax.ShapeDtypeStruct((t, HEAD_COLS), BF16), jax.ShapeDtypeStruct((t, D_MODEL), F32),
                   jax.ShapeDtypeStruct((8, Q_RANK), F32)),
        in_specs=[pl.BlockSpec((ROW_TILE, dq.shape[1]), lambda i: (jnp.minimum(i, ns - 1), 0)),
                  row(dk.shape[1]), row(dv.shape[1]), row(HEAD_COLS), whole(wq), whole(wk_k), whole(wk_v),
                  whole(win_head), whole(q_gain), whole(kv_gain), row(LANES), row(LANES)],
        out_specs=(row(HEAD_COLS), row(D_MODEL), pl.BlockSpec((8, Q_RANK), lambda i: (0, 0))),
        compiler_params=_params(("arbitrary",), VMEM_BIG),
    )(dq, dk, dv, z, wq, wk_k, wk_v, win_head, q_gain, kv_gain, cos, sgn)


def _shift_rows(u, s):
    rowi = lax.broadcasted_iota(jnp.int32, u.shape, 0)
    prev = jnp.where(rowi == 0, 0.0, pltpu.roll(u, 1, 0))
    nxt = jnp.where(rowi == s - 1, 0.0, pltpu.roll(u, s - 1, 0))
    return prev, nxt


def _conv_fwd(z_conv, cw, a_cat, name):
    s = z_conv.shape[0]

    def body(z_ref, w_ref, a_in_ref, o_ref):
        del a_in_ref
        gb, gc, xv = z_ref[:, 0:LANES], z_ref[:, LANES:2 * LANES], z_ref[:, 2 * LANES:3 * LANES]
        u = gc * xv
        prev, nxt = _shift_rows(u, s)
        y = w_ref[0:1, :] * prev + w_ref[1:2, :] * u + w_ref[2:3, :] * nxt
        o_ref[...] = (gb * y).astype(BF16)

    return pl.pallas_call(
        body, name=name, grid=(CONV_W // LANES,),
        out_shape=jax.ShapeDtypeStruct(a_cat.shape, a_cat.dtype),
        in_specs=[pl.BlockSpec((s, 3 * LANES), lambda j: (0, j)), pl.BlockSpec((3, LANES), lambda j: (0, j)),
                  pl.BlockSpec(memory_space=pl.ANY)],
        out_specs=pl.BlockSpec((s, LANES), lambda j: (0, 4 + j)),
        input_output_aliases={2: 0},
        compiler_params=_params(("parallel",), VMEM_BIG),
    )(z_conv, cw, a_cat)


def _conv_bwd(z_conv, cw, da, name):
    s = z_conv.shape[0]

    def body(z_ref, w_ref, da_ref, dz_ref, dw_ref):
        gb, gc, xv = z_ref[:, 0:LANES], z_ref[:, LANES:2 * LANES], z_ref[:, 2 * LANES:3 * LANES]
        u = gc * xv
        prev, nxt = _shift_rows(u, s)
        dcv = da_ref[...]
        dz_ref[:, 0:LANES] = (dcv * (w_ref[0:1, :] * prev + w_ref[1:2, :] * u + w_ref[2:3, :] * nxt)).astype(BF16)
        dy = dcv * gb
        dw_ref[0:1, :] = jnp.sum(dy * prev, axis=0, keepdims=True)
        dw_ref[1:2, :] = jnp.sum(dy * u, axis=0, keepdims=True)
        dw_ref[2:3, :] = jnp.sum(dy * nxt, axis=0, keepdims=True)
        dyp, dyn = _shift_rows(dy, s)
        du = w_ref[0:1, :] * dyn + w_ref[1:2, :] * dy + w_ref[2:3, :] * dyp
        dz_ref[:, LANES:2 * LANES] = (du * xv).astype(BF16)
        dz_ref[:, 2 * LANES:3 * LANES] = (du * gc).astype(BF16)

    blk = pl.BlockSpec((s, 3 * LANES), lambda j: (0, j))
    cws = pl.BlockSpec((3, LANES), lambda j: (0, j))
    return pl.pallas_call(
        body, name=name, grid=(CONV_W // LANES,),
        out_shape=(jax.ShapeDtypeStruct(z_conv.shape, BF16), jax.ShapeDtypeStruct((3, CONV_W), F32)),
        in_specs=[blk, cws, pl.BlockSpec((s, LANES), lambda j: (0, 4 + j))], out_specs=(blk, cws),
        compiler_params=_params(("parallel",), VMEM_BIG),
    )(z_conv, cw, da)


ATT_TQ = 256
ATT_Q_STEP = 1024
ATT_TQ_BWD = 512


def _head_mask(shape, hh):
    lane = lax.broadcasted_iota(jnp.int32, shape, 1)
    return (lane >= hh * V_DIM) & (lane < (hh + 1) * V_DIM)


def _attn_fwd(qf, kv, s, riding, name):
    t = kv.shape[0]
    step = min(ATT_Q_STEP, s)
    nq = s // step
    nr = riding.n

    def body(*refs):
        q_ref, k_ref, v_ref = refs[:3]
        o_ref, ob_ref, st_ref = refs[3 + nr:6 + nr]
        p, i = pl.program_id(0), pl.program_id(1)
        state = riding.run((p == 0) & (i == 0), (p == N_HEADS // 2 - 1) & (i == nq - 1),
                           refs[3:3 + nr], refs[6 + nr:6 + 2 * nr], refs[6 + 2 * nr:])
        v = v_ref[...]
        vlane = lax.broadcasted_iota(jnp.int32, v.shape, 1)
        one_lane = [(1 - hh) * V_DIM for hh in range(2)]
        vm = [jnp.where(_head_mask(v.shape, hh), v, jnp.where(vlane == one_lane[hh], 1.0, 0.0).astype(BF16))
              for hh in range(2)]

        def block(r, carry):
            rows = pl.ds(pl.multiple_of(r * ATT_TQ, ATT_TQ), ATT_TQ)
            olane = lax.broadcasted_iota(jnp.int32, (ATT_TQ, LANES), 1)
            acc = jnp.zeros((ATT_TQ, LANES), F32)
            stat = jnp.zeros((ATT_TQ, LANES), F32)
            for hh in range(2):
                sl = slice(hh * LANES, (hh + 1) * LANES)
                sc = lax.dot_general(q_ref[rows, sl], k_ref[:, sl], NT_DIMS, preferred_element_type=F32)
                mx = jnp.max(sc, axis=1, keepdims=True)
                e = jnp.exp2((sc - mx) * EXP2_SCALE).astype(BF16)
                res = jnp.dot(e, vm[hh], preferred_element_type=F32)
                den = jnp.sum(jnp.where(olane == one_lane[hh], res, 0.0), axis=1, keepdims=True)
                acc = acc + jnp.where(_head_mask(res.shape, hh), res * (1.0 / den), 0.0)
                stat = stat + jnp.where(olane == hh, mx * EXP2_SCALE + jnp.log(den) * LOG2_E, 0.0)
            o_ref[rows, :] = acc
            ob_ref[rows, :] = acc.astype(BF16)
            st_ref[:, rows] = stat.T[0:8, :]
            return carry

        lax.fori_loop(0, step // ATT_TQ, block, 0)
        riding.finish(state)

    o_spec = pl.BlockSpec((step, LANES), lambda p, i: (i, p))
    outs = pl.pallas_call(
        body, name=name, grid=(N_HEADS // 2, nq),
        out_shape=(jax.ShapeDtypeStruct((s, N_HEADS * V_DIM), F32),
                   jax.ShapeDtypeStruct((s, D_MODEL), BF16),
                   jax.ShapeDtypeStruct((N_HEADS // 2 * 8, s), F32), *riding.out_shape),
        in_specs=[pl.BlockSpec((step, 2 * LANES), lambda p, i: (i, p)),
                  pl.BlockSpec((t, 2 * LANES), lambda p, i: (0, p)),
                  pl.BlockSpec((t, LANES), lambda p, i: (0, N_HEADS + p)), *riding.specs],
        out_specs=(o_spec, o_spec, pl.BlockSpec((8, step), lambda p, i: (p, i)), *riding.specs),
        scratch_shapes=riding.scratch,
        compiler_params=_params(("arbitrary", "arbitrary"), VMEM_BIG),
    )(qf, kv, kv, *riding.arrays)
    return outs[0], outs[1], outs[2], list(outs[3:])


def _attn_bwd(qf, kv, o, da, stats, cos, sgn, riding, name):
    s, t = o.shape[0], kv.shape[0]
    ATT_TQ = ATT_TQ_BWD
    nq = s // ATT_TQ
    nr = riding.n

    def body(*refs):
        q_ref, k_ref, v_ref, o_ref, do_ref, st_ref, c_ref, s_ref = refs[:8]
        dq_ref, dk_ref, dv_ref = refs[8 + nr:11 + nr]
        dk_acc, dv_acc = refs[11 + 2 * nr:13 + 2 * nr]
        p, i = pl.program_id(0), pl.program_id(1)
        state = riding.run((p == 0) & (i == 0), (p == N_HEADS // 2 - 1) & (i == nq - 1),
                           refs[8:8 + nr], refs[11 + nr:11 + 2 * nr], refs[13 + 2 * nr:])

        @pl.when(i == 0)
        def _():
            dk_acc[...] = jnp.zeros_like(dk_acc)
            dv_acc[...] = jnp.zeros_like(dv_acc)

        v = v_ref[...]
        do = do_ref[...]
        od = do * o_ref[...]
        ones = jnp.ones((8, LANES), F32)
        for hh in range(2):
            sl = slice(hh * LANES, (hh + 1) * LANES)
            q, k = q_ref[:, sl], k_ref[:, sl]
            mask = _head_mask(do.shape, hh)
            dom = jnp.where(mask, do, 0.0).astype(BF16)
            delta = lax.dot_general(ones, jnp.where(mask, od, 0.0), NT_DIMS, preferred_element_type=F32,
                                    precision=lax.Precision.HIGHEST)[0:1, :]
            st = lax.dot_general(k, q, NT_DIMS, preferred_element_type=F32)
            pt = jnp.exp2(st * EXP2_SCALE - st_ref[hh:hh + 1, :]).astype(BF16)
            dpt = lax.dot_general(v, dom, NT_DIMS, preferred_element_type=F32)
            dst = (pt.astype(F32) * (dpt - delta)).astype(BF16)
            dv_acc[...] += jnp.dot(pt, dom, preferred_element_type=F32)
            dk_acc[:, sl] += jnp.dot(dst, q, preferred_element_type=F32)
            dq = lax.dot_general(dst, k, TN_DIMS, preferred_element_type=F32) * ATTN_SCALE
            dq_ref[:, sl] = _rope(dq, c_ref[...], s_ref[...], True).astype(BF16)

        @pl.when(i == nq - 1)
        def _():
            dk_ref[...] = (dk_acc[...] * ATTN_SCALE).astype(BF16)
            dv_ref[...] = dv_acc[...].astype(BF16)

        riding.finish(state)

    o_spec = pl.BlockSpec((ATT_TQ, LANES), lambda p, i: (i, p))
    tab = pl.BlockSpec((ATT_TQ, LANES), lambda p, i: (i, 0))
    outs = pl.pallas_call(
        body, name=name, grid=(N_HEADS // 2, nq),
        out_shape=(jax.ShapeDtypeStruct((s, N_HEADS * LANES), BF16),
                   jax.ShapeDtypeStruct((t, N_HEADS * LANES), BF16),
                   jax.ShapeDtypeStruct((t, N_HEADS * V_DIM), BF16), *riding.out_shape),
        in_specs=[pl.BlockSpec((ATT_TQ, 2 * LANES), lambda p, i: (i, p)),
                  pl.BlockSpec((t, 2 * LANES), lambda p, i: (0, p)),
                  pl.BlockSpec((t, LANES), lambda p, i: (0, N_HEADS + p)),
                  o_spec, o_spec,
                  pl.BlockSpec((8, ATT_TQ), lambda p, i: (p, i)), tab, tab, *riding.specs],
        out_specs=(pl.BlockSpec((ATT_TQ, 2 * LANES), lambda p, i: (i, p)),
                   pl.BlockSpec((t, 2 * LANES), lambda p, i: (0, p)),
                   pl.BlockSpec((t, LANES), lambda p, i: (0, p)), *riding.specs),
        scratch_shapes=[pltpu.VMEM((t, 2 * LANES), F32), pltpu.VMEM((t, LANES), F32), *riding.scratch],
        compiler_params=_params(("arbitrary", "arbitrary"), VMEM_BIG),
    )(qf, kv, kv, o, da, stats, cos, sgn, *riding.arrays)
    return outs[0], outs[1], outs[2], list(outs[3:])


def _silu(x):
    return x * (1.0 / (1.0 + jnp.exp(-x)))


def _prologue(c_rows, c_ctx, w_mod, b_cols, extra_rows, weights, name):
    nw = len(weights)
    d, cols = c_rows.shape[1], w_mod.shape[1]

    def body(c_ref, cctx_ref, wmod_ref, b_ref, x_ref, *rest):
        w_refs, (a_ref, modg_ref), wg_refs = rest[:nw], rest[nw:nw + 2], rest[nw + 2:2 * nw + 2]
        c_all, blk = rest[2 * nw + 2:2 * nw + 4]
        w_send, w_recv, w_local, c_send, c_recv, m_send, m_recv = rest[2 * nw + 4:]
        finish_c = _direct_gather(c_ref, c_all, c_send, c_recv)
        finish_weights = _two_level_gather(w_refs, wg_refs, w_send, w_recv, w_local)
        finish_c()
        a_ref[...] = jnp.zeros_like(a_ref)
        for j in range(N_DEV):
            a_ref[j:j + 1, :] = c_all[j, 0:1, :]
        a_ref[N_DEV:N_DEV + 1, :] = cctx_ref[...]
        blk[0:16, :] = jnp.dot(_silu(a_ref[...]), wmod_ref[...], preferred_element_type=F32,
                               precision=lax.Precision.HIGHEST) + b_ref[...]
        blk[16:24, :] = x_ref[...]
        finish_mod = _direct_gather(blk, modg_ref, m_send, m_recv)
        finish_weights()
        finish_mod()

    vmem, hbm = pl.BlockSpec(memory_space=pltpu.VMEM), pl.BlockSpec(memory_space=pl.ANY)
    outs = pl.pallas_call(
        body, name=name,
        out_shape=(jax.ShapeDtypeStruct((16, d), F32), jax.ShapeDtypeStruct((N_DEV, 24, cols), F32),
                   *[jax.ShapeDtypeStruct((N_DEV,) + w.shape, w.dtype) for w in weights]),
        in_specs=[vmem] * 5 + [hbm] * nw, out_specs=(vmem, vmem, *[hbm] * nw),
        scratch_shapes=[pltpu.VMEM((N_DEV, 8, d), F32), pltpu.VMEM((24, cols), F32),
                        pltpu.SemaphoreType.DMA((7 * nw,)), pltpu.SemaphoreType.DMA((7 * nw,)),
                        pltpu.SemaphoreType.DMA((nw,)), pltpu.SemaphoreType.DMA((7,)), pltpu.SemaphoreType.DMA((7,)),
                        pltpu.SemaphoreType.DMA((7,)), pltpu.SemaphoreType.DMA((7,))],
        compiler_params=_params(None, VMEM_BIG),
    )(c_rows, c_ctx, w_mod, b_cols, extra_rows, *weights)
    return outs[0], outs[1], list(outs[2:])


def _adaln_bwd(a_t, w, d_ex, d_ctx, d_all, name):
    def body(at_ref, w_ref, dex_ref, dctx_ref, dall_ref, gw_ref, dsil_ref, dsum_ref):
        sil_t = _silu(at_ref[...])
        dctx = dctx_ref[...]
        row = dctx[0:1, :]
        for j in range(1, N_DEV):
            row = row + dctx[j:j + 1, :]
        rowi = lax.broadcasted_iota(jnp.int32, dctx.shape, 0)
        ctx_rows = jnp.where(rowi == 0, jnp.broadcast_to(row, dctx.shape), 0.0)
        hi = lax.Precision.HIGHEST
        d_rows = jnp.concatenate([dex_ref[...], ctx_rows], axis=0)
        gw_ref[...] = jnp.dot(sil_t, d_rows, preferred_element_type=F32, precision=hi)
        dsil_ref[...] = lax.dot_general(ctx_rows, w_ref[...], NT_DIMS, preferred_element_type=F32, precision=hi)
        tot = dall_ref[0]
        for j in range(1, N_DEV):
            tot = tot + dall_ref[j]
        dsum_ref[...] = tot

    return pl.pallas_call(
        body, name=name,
        out_shape=(jax.ShapeDtypeStruct(w.shape, F32), jax.ShapeDtypeStruct((8, w.shape[0]), F32),
                   jax.ShapeDtypeStruct(d_all.shape[1:], F32)),
        compiler_params=_params(None, VMEM_BIG),
    )(a_t, w, d_ex, d_ctx, d_all)


def _pack_small(sums1, sums2, fsums, sums1c, psums, d_cw, name):
    d = D_MODEL

    def body(s1_ref, s2_ref, f_ref, s1c_ref, p_ref, cw_ref, o_ref):
        o_ref[...] = jnp.zeros_like(o_ref)
        for col, (ref, r) in enumerate([(s1_ref, 1), (s1_ref, 0), (s2_ref, 2), (s2_ref, 1), (s2_ref, 0), (f_ref, 1)]):
            o_ref[0:1, col * d:(col + 1) * d] = ref[r:r + 1, :]
        o_ref[1:2, 0:d] = s1c_ref[1:2, :]
        o_ref[1:2, d:2 * d] = s1c_ref[0:1, :]
        o_ref[2:3, 0:Q_RANK] = p_ref[0:1, :]
        o_ref[2:3, Q_RANK:Q_RANK + KV_RANK] = p_ref[1:2, 0:KV_RANK]
        o_ref[2:3, Q_RANK + KV_RANK:Q_RANK + KV_RANK + d] = f_ref[0:1, :]
        for r in range(3):
            o_ref[3 + r:4 + r, 0:CONV_W] = cw_ref[r:r + 1, :]
        o_ref[6:7, 0:d] = f_ref[3:4, :]

    return pl.pallas_call(body, name=name, out_shape=jax.ShapeDtypeStruct((8, 6 * d), F32))(
        sums1, sums2, fsums, sums1c, psums, d_cw)


def _adam_math(w, g, m, v):
    nm = ADAM_B1 * m + (1.0 - ADAM_B1) * g
    nv = ADAM_B2 * v + (1.0 - ADAM_B2) * (g * g)
    m_hat = nm / (1.0 - ADAM_B1 ** ADAM_STEP)
    v_hat = nv / (1.0 - ADAM_B2 ** ADAM_STEP)
    return -ADAM_LR * (m_hat / (jnp.sqrt(v_hat) + ADAM_EPS) + ADAM_WD * w), nm, nv


def _small_update(dsum, dsil_all, g_cw, params, name):
    d = D_MODEL
    n = len(params)

    def body(*refs):
        dsum_ref, dsil_ref, gcw_ref = refs[:3]
        wmv = refs[3:3 + 3 * n]
        outs = refs[3 + 3 * n:]
        tot = dsil_ref[0]
        for j in range(1, N_DEV):
            tot = tot + dsil_ref[j]
        cv = wmv[0][...]
        sg = 1.0 / (1.0 + jnp.exp(-cv))
        off = Q_RANK + KV_RANK
        grads = [tot[0:1, :] * (sg * (1.0 + cv * (1.0 - sg))),
                 dsum_ref[0:1, :] + dsum_ref[1:2, :],
                 dsum_ref[2:3, 0:Q_RANK], dsum_ref[2:3, Q_RANK:off], dsum_ref[2:3, off:off + d],
                 gcw_ref[...]]
        for p, g in enumerate(grads):
            w_ref, m_ref, v_ref = wmv[3 * p:3 * p + 3]
            at = 0 if len(w_ref.shape) == 3 else Ellipsis
            res = (g,) + _adam_math(w_ref[at], g, m_ref[at], v_ref[at])
            for q, val in enumerate(res):
                outs[4 * p + q][at] = val

    flat = [a for wmv in params for a in wmv]
    out_shape = tuple(jax.ShapeDtypeStruct(wmv[0].shape, F32) for wmv in params for _ in range(4))
    outs = pl.pallas_call(body, name=name, out_shape=out_shape)(dsum, dsil_all, g_cw, *flat)
    return [outs[4 * p:4 * p + 4] for p in range(n)]


def _adamw(w, g, m, v, name, slots=False):
    _, rows, cols = w.shape
    tr = _pick(rows, (256, 128, 64, 32, 16, 8))

    def body(w_ref, g_ref, m_ref, v_ref, *outs):
        if slots:
            gv = g_ref[0].astype(F32)
            for j in range(1, N_DEV):
                gv = gv + g_ref[j].astype(F32)
            outs[0][...] = gv
        else:
            gv = g_ref[...]
        d_ref, nm_ref, nv_ref = outs[-3:]
        d_ref[...], nm_ref[...], nv_ref[...] = _adam_math(w_ref[...], gv, m_ref[...], v_ref[...])

    blk = pl.BlockSpec((None, tr, cols), lambda i: (0, i, 0))
    g_spec = (pl.BlockSpec((N_DEV, tr, cols), lambda i: (0, i, 0)) if slots
              else pl.BlockSpec((tr, cols), lambda i: (i, 0)))
    sh = jax.ShapeDtypeStruct((1, rows, cols), F32)
    n_out = 4 if slots else 3
    return pl.pallas_call(
        body, name=name, grid=(rows // tr,), out_shape=(sh,) * n_out,
        in_specs=[blk, g_spec, blk, blk], out_specs=(blk,) * n_out,
        compiler_params=_params(("parallel",), VMEM_BIG),
    )(w, g, m, v)


def _rope_tables(s, l):
    tok = np.arange(s)
    row = (tok // GRID_W).astype(np.float32)
    col = (tok % GRID_W).astype(np.float32)
    half = QK_ROPE // 2
    freqs = np.float32(ROPE_THETA) ** (-np.arange(0, half, 2, dtype=np.float32) / np.float32(half))
    dd = np.arange(QK_ROPE)
    pos = np.where((dd // half)[None, :] == 0, row[:, None], col[:, None]).astype(np.float32)
    ang = (pos * freqs[dd % (half // 2)][None, :]).astype(np.float32)
    sin = np.sin(ang).astype(np.float32)
    cos_t = np.ones((s + l, LANES), np.float32)
    sgn_t = np.zeros((s + l, LANES), np.float32)
    cos_t[:s, QK_NOPE:QK_NOPE + QK_ROPE] = np.cos(ang)
    sgn_t[:s, QK_NOPE:QK_NOPE + QK_ROPE] = np.where(((dd % half) // (half // 2))[None, :] == 0, -sin, sin)
    return jnp.asarray(cos_t), jnp.asarray(sgn_t)


def _slots_to_cols(g):
    return g.transpose(1, 0, 2).reshape(g.shape[1], N_DEV * g.shape[2])


def _cols_to_slots(w):
    return w.reshape(w.shape[0], N_DEV, w.shape[1] // N_DEV).transpose(1, 0, 2)


def _unpack_small_weights(g_in_t, g_uq, g_ukv):
    w_t = g_in_t.reshape(N_DEV * g_in_t.shape[1], D_MODEL)
    zeros = jnp.zeros((QK_NOPE, D_MODEL), BF16)
    win_head_t = jnp.concatenate([w_t[:Q_RANK + KV_RANK], zeros, w_t[Q_RANK + KV_RANK:MLA_IN],
                                  zeros[:LANES - QK_NOPE - QK_ROPE]], axis=0)
    win_conv_t = w_t[MLA_IN:].reshape(3, CONV_W // LANES, LANES, D_MODEL).transpose(1, 0, 2, 3)
    win_conv_t = win_conv_t.reshape(3 * CONV_W, D_MODEL)
    w_uq = _slots_to_cols(g_uq).reshape(Q_RANK, N_HEADS, QK_NOPE + QK_ROPE)
    wq = jnp.pad(w_uq, ((0, 0), (0, 0), (0, LANES - QK_NOPE - QK_ROPE))).reshape(Q_RANK, N_HEADS * LANES)
    w_ukv = _slots_to_cols(g_ukv).reshape(KV_RANK, N_HEADS, QK_NOPE + V_DIM)
    k_top = jnp.pad(w_ukv[:, :, :QK_NOPE], ((0, 0), (0, 0), (0, LANES - QK_NOPE))).reshape(KV_RANK, N_HEADS * LANES)
    v_top = w_ukv[:, :, QK_NOPE:].reshape(KV_RANK, N_HEADS * V_DIM)
    eye = jnp.pad(jnp.eye(QK_ROPE, dtype=BF16), ((QK_NOPE, LANES - QK_NOPE - QK_ROPE),) * 2)
    wk = jnp.concatenate([
        jnp.concatenate([k_top, v_top], axis=1),
        jnp.concatenate([jnp.tile(eye, (1, N_HEADS)), jnp.zeros((LANES, N_HEADS * V_DIM), BF16)], axis=1)], axis=0)
    return win_head_t, win_conv_t, wq, wk


def _pack_small_grads(d_head_t, d_conv_t, d_wq, d_wkk, d_wkv):
    d_conv_t = d_conv_t.reshape(CONV_W // LANES, 3, LANES, D_MODEL).transpose(1, 0, 2, 3).reshape(3 * CONV_W, D_MODEL)
    rope0 = Q_RANK + KV_RANK + QK_NOPE
    g_in_t = jnp.concatenate([d_head_t[:Q_RANK + KV_RANK], d_head_t[rope0:rope0 + QK_ROPE], d_conv_t], axis=0)
    g_in_t = g_in_t.reshape(N_DEV, -1, D_MODEL).astype(BF16)
    g_uq = d_wq.reshape(Q_RANK, N_HEADS, LANES)[:, :, :QK_NOPE + QK_ROPE].reshape(Q_RANK, -1)
    g_kn = d_wkk[:KV_RANK].reshape(KV_RANK, N_HEADS, LANES)[:, :, :QK_NOPE]
    g_v = d_wkv[:KV_RANK].reshape(KV_RANK, N_HEADS, V_DIM)
    g_ukv = jnp.concatenate([g_kn, g_v], axis=2).reshape(KV_RANK, -1)
    return [g_in_t] + [_cols_to_slots(g).astype(BF16) for g in (g_uq, g_ukv)]


def kernel(x, c, ctx, c_ctx, w_mod, b_mod, w_in, q_norm_g, w_uq, kv_norm_g, w_ukv, conv_w, w_out, w_mlp1, w_mlp2, final_norm_g, loss_target, m_c_ctx, m_w_mod, m_b_mod, m_w_in, m_q_norm_g, m_w_uq, m_kv_norm_g, m_w_ukv, m_conv_w, m_w_out, m_w_mlp1, m_w_mlp2, m_final_norm_g, v_c_ctx, v_w_mod, v_b_mod, v_w_in, v_q_norm_g, v_w_uq, v_kv_norm_g, v_w_ukv, v_conv_w, v_w_out, v_w_mlp1, v_w_mlp2, v_final_norm_g):
    me = _my_index()
    x2d, ctx2d, tgt = x[0], ctx[0], loss_target[0]
    s, l = x2d.shape[0], ctx2d.shape[0]
    t = s + l
    d = D_MODEL
    mod_cols = w_mod.shape[2]
    cw_cols = conv_w.shape[2]

    early = [w.astype(BF16) for w in (w_in[0].T, w_uq[0], w_ukv[0])]
    late = [w.astype(BF16) for w in (w_out[0], w_mlp1[0], w_mlp2[0])]
    b_cols = lax.dynamic_slice(b_mod, (0, me * mod_cols), (1, mod_cols))
    cw_blk = jnp.pad(conv_w[0], ((0, 5), (0, mod_cols - cw_cols)))
    a_rows, gathered, (g_in, g_uq, g_ukv) = _prologue(jnp.pad(c, ((0, 7), (0, 0))), c_ctx[None, :], w_mod[0], b_cols,
                                                      cw_blk, early, "prologue")
    mod_mine = lax.dynamic_index_in_dim(gathered, me, axis=1, keepdims=False).reshape(1, 6 * d)
    mod_ctx = gathered[:, 8, :].reshape(1, 6 * d)
    cw_full = gathered[:, 16:19, :cw_cols].transpose(1, 0, 2).reshape(3, CONV_W)

    win_head, win_conv, wq, wk = _unpack_small_weights(g_in, g_uq, g_ukv)
    wk_k, wk_v = wk[:, :N_HEADS * LANES], wk[:, N_HEADS * LANES:]
    cos, sgn = _rope_tables(s, l)

    h_all = _modulate_all(x2d, ctx2d, mod_mine, mod_ctx, "modulate1")
    tm_t = _pick(t, (1088, 768, 256))
    tk_t = _pick(t, (2176, 768, 256))
    z_head, cq, kv_in, qf, kv = _head_fwd(h_all, win_head, wq, wk, q_norm_g, kv_norm_g, cos, sgn, tm_t, "head_fwd")
    z_conv = _matmul(h_all, win_conv, mode="nt", name="in_proj_conv", m=s, tm=1024, tn=1536, tk=1024)
    attn, a_cat, stats, (g_out, w1, g_w2) = _attn_fwd(qf, kv, s, _Riding("gather", late), "attn_fwd")
    wo = g_out.reshape(d, d)
    w2 = g_w2.reshape(D_FF, d)
    a_cat = _conv_fwd(z_conv, cw_full, a_cat, "conv_fwd")
    (o, x1, h2), _ = _matmul_rows(a_cat, wo, _epi_resid_modulate, mode="nn", name="out_proj", tm=1024, tk=1024,
                                  rows=[x2d], vecs=[(mod_mine, 2), (mod_mine, 3), (mod_mine, 4)],
                                  out_dtypes=[F32, F32, BF16])
    u1, act = _matmul(h2, w1, mode="nn", name="mlp_up", tm=4096, tk=1024, epilogue="relu2", slots="b_cols")
    (dx2, dm, fsums), _ = _matmul_rows(act, w2, _epi_final, mode="nn", name="mlp_down", tm=512, tk=4096,
                                       rows=[x1, tgt], vecs=[(mod_mine, 5), (final_norm_g[None, :], 0)],
                                       out_dtypes=[F32, BF16], sums=True)

    d_w2 = _matmul(act, dm, mode="tn", name="d_w_mlp2", out_dtype=BF16, tm=1024, tn=1024, tk=4096)
    du1 = _matmul(dm, w2, mode="nt", name="d_act", out_dtype=BF16, tm=2048, tn=1024, tk=1024,
                  epilogue="drelu2", extra=(u1,))
    d_w1 = _matmul(h2, du1, mode="tn", name="d_w_mlp1", out_dtype=BF16, tm=1024, tk=4096, slots="out")
    (dx1, do, sums2), _ = _matmul_rows(du1, w1, _epi_modulate2_bwd, mode="nt", name="d_h2", tm=512, tk=4096,
                                       slots="b_contract", rows=[x1, dx2, o], vecs=[(mod_mine, 4), (mod_mine, 2)],
                                       out_dtypes=[F32, BF16], sums=True)
    d_wo = _matmul(a_cat, do, mode="tn", name="d_w_out", out_dtype=BF16, tm=1024, tn=1024, tk=2048)
    da = _matmul(do, wo, mode="nt", name="d_a", tm=1024, tn=1024, tk=1024)
    dz_conv, d_cw = _conv_bwd(z_conv, cw_full, da, "conv_bwd")
    ready = [d_wo.reshape(N_DEV, d // N_DEV, d), d_w1, d_w2.reshape(N_DEV, D_FF // N_DEV, d)]
    dq, dk, dv, rode = _attn_bwd(qf, kv, attn, da, stats, cos, sgn, _Riding("exchange", ready), "attn_bwd")
    d_wq = _matmul(cq, dq, mode="tn", name="d_w_uq", k=s, tm=256, tn=1024, tk=4096)
    d_wkk = _matmul(kv_in, dk, mode="tn", name="d_w_ukv_k", tm=256, tn=1024, tk=tk_t)
    d_wkv = _matmul(kv_in, dv, mode="tn", name="d_w_ukv_v", tm=256, tn=512, tk=tk_t)
    dz_head, dh_head, psums = _head_bwd(dq, dk, dv, z_head, wq, wk_k, wk_v, win_head, q_norm_g, kv_norm_g, cos, sgn, s,
                                        "head_bwd")
    d_head = _matmul(dz_head, h_all, mode="tn", name="d_w_in_head", tm=512, tn=1024, tk=tk_t)
    d_conv = _matmul(dz_conv, h_all, mode="tn", name="d_w_in_conv", k=s, tm=1536, tn=1024, tk=2048)
    send = _pack_small_grads(d_head, d_conv, d_wq, d_wkk, d_wkv)
    (grad_x, sums1), got = _matmul_rows(dz_conv, win_conv, _epi_modulate1_bwd, mode="nn", name="d_h1", tm=1024,
                                        tk=win_conv.shape[0], rows=[dh_head, x2d, dx1], vecs=[(mod_mine, 1)],
                                        out_dtypes=[F32], sums=True, riding=_Riding("exchange", send))
    sums1c = _modulate_sums(dh_head, s // ROW_TILE, ctx2d)

    small = _pack_small(sums1, sums2, fsums, sums1c, psums, d_cw, "pack_small")
    (d_all,) = _all_gather([small], "gather_small_grads", True)
    d_cols = lax.dynamic_slice_in_dim(d_all, me * mod_cols, mod_cols, axis=2)
    g_w_mod, dsil, dsum = _adaln_bwd(a_rows.T, w_mod[0], d_cols[:, 0, :], d_cols[:, 1, :], d_all, "adaln_bwd")
    (dsil_all,) = _all_gather([dsil], "gather_d_cctx", True)
    loss = dsum[6, 0]
    g_cw = lax.dynamic_slice(dsum, (3, me * cw_cols), (3, cw_cols))

    slots = dict(zip(["w_in", "w_uq", "w_ukv"], got))
    slots.update(zip(["w_out", "w_mlp1", "w_mlp2"], rode))

    grads = {}
    weights = {"c_ctx": c_ctx, "w_mod": w_mod, "b_mod": b_mod, "w_in": w_in, "q_norm_g": q_norm_g, "w_uq": w_uq,
               "kv_norm_g": kv_norm_g, "w_ukv": w_ukv, "conv_w": conv_w, "w_out": w_out, "w_mlp1": w_mlp1,
               "w_mlp2": w_mlp2, "final_norm_g": final_norm_g}
    m_in = {"c_ctx": m_c_ctx, "w_mod": m_w_mod, "b_mod": m_b_mod, "w_in": m_w_in, "q_norm_g": m_q_norm_g,
            "w_uq": m_w_uq, "kv_norm_g": m_kv_norm_g, "w_ukv": m_w_ukv, "conv_w": m_conv_w, "w_out": m_w_out,
            "w_mlp1": m_w_mlp1, "w_mlp2": m_w_mlp2, "final_norm_g": m_final_norm_g}
    v_in = {"c_ctx": v_c_ctx, "w_mod": v_w_mod, "b_mod": v_b_mod, "w_in": v_w_in, "q_norm_g": v_q_norm_g,
            "w_uq": v_w_uq, "kv_norm_g": v_kv_norm_g, "w_ukv": v_w_ukv, "conv_w": v_conv_w, "w_out": v_w_out,
            "w_mlp1": v_w_mlp1, "w_mlp2": v_w_mlp2, "final_norm_g": v_final_norm_g}
    names = list(weights)
    small_names = ["c_ctx", "b_mod", "q_norm_g", "kv_norm_g", "final_norm_g", "conv_w"]
    delta, new_m, new_v = {}, {}, {}

    def as_rows(a):
        return a[None, :] if a.ndim == 1 else a

    small_out = _small_update(dsum, dsil_all, g_cw, [[as_rows(src[n]) for src in (weights, m_in, v_in)]
                                                      for n in small_names], "small_update")
    for n, outs in zip(small_names, small_out):
        grads[n], delta[n], new_m[n], new_v[n] = [a.reshape(weights[n].shape) for a in outs]
    for n in names:
        if n in small_names:
            continue
        if n == "w_in":
            wmv = [jnp.swapaxes(src[n], 1, 2) for src in (weights, m_in, v_in)]
            outs = _adamw(wmv[0], slots[n], wmv[1], wmv[2], "adamw_" + n, slots=True)
            grads[n], delta[n], new_m[n], new_v[n] = [jnp.swapaxes(a, 1, 2) for a in outs]
        elif n in slots:
            grads[n], delta[n], new_m[n], new_v[n] = _adamw(weights[n], slots[n], m_in[n], v_in[n], "adamw_" + n,
                                                            slots=True)
        else:
            delta[n], new_m[n], new_v[n] = _adamw(weights[n], g_w_mod, m_in[n], v_in[n], "adamw_" + n)
            grads[n] = g_w_mod[None]

    return (loss, grad_x[None], *[grads[n] for n in names], *[delta[n] for n in names],
            *[new_m[n] for n in names], *[new_v[n] for n in names])
```

```python
import math

import jax
import jax.numpy as jnp
import numpy as np
from jax import lax
from jax.experimental import pallas as pl
from jax.experimental.pallas import tpu as pltpu

F32 = jnp.float32
BF16 = jnp.bfloat16

D_MODEL = 1024
GRID_W = 64
N_HEADS = 8
QK_NOPE = 64
QK_ROPE = 32
V_DIM = 64
Q_RANK = 256
KV_RANK = 128
MLA_IN = Q_RANK + KV_RANK + QK_ROPE
CONV_W = 512
HEAD_COLS = 512
D_FF = 4096
ROPE_THETA = 10000.0
EPS = 1e-6
ATTN_SCALE = 1.0 / math.sqrt(QK_NOPE + QK_ROPE)
LOG2_E = 1.0 / math.log(2.0)
EXP2_SCALE = ATTN_SCALE * LOG2_E
N_DEV = 8
LANES = 128

ADAM_LR, ADAM_B1, ADAM_B2, ADAM_EPS, ADAM_WD, ADAM_STEP = 0.001, 0.9, 0.999, 1e-08, 0.01, 10

ROW_TILE = 256
VMEM_BIG = 60 * 1024 * 1024


def _params(sem=None, vmem=None):
    return pltpu.CompilerParams(dimension_semantics=sem, vmem_limit_bytes=vmem)


def _pick(n, prefs):
    for p in prefs:
        if n % p == 0:
            return p
    return n


def _my_index():
    return 4 * lax.axis_index("x") + 2 * lax.axis_index("y") + lax.axis_index("c")


def _two_level_gather(x_refs, out_refs, send_sems, recv_sems, local_sems):
    n = len(x_refs)
    x, y, c = lax.axis_index("x"), lax.axis_index("y"), lax.axis_index("c")
    me, sibling = (x, y, c), (x, y, 1 - c)
    chips = [(1 - x, y), (x, 1 - y), (1 - x, 1 - y)]

    def slot(a, px, py, pc):
        return out_refs[a].at[4 * px + 2 * py + pc]

    def copy(a, k, block, to, src=None):
        return pltpu.make_async_remote_copy(
            src_ref=slot(a, *block) if src is None else src, dst_ref=slot(a, *block),
            send_sem=send_sems.at[7 * a + k], recv_sem=recv_sems.at[7 * a + k],
            device_id=to, device_id_type=pl.DeviceIdType.MESH)

    mine = [pltpu.make_async_copy(x_refs[a], slot(a, *me), local_sems.at[a]) for a in range(n)]
    for cp in mine:
        cp.start()
    started = []
    for a in range(n):
        first = [copy(a, 0, me, sibling, src=x_refs[a])]
        first += [copy(a, 1 + j, me, (*chip, c), src=x_refs[a]) for j, chip in enumerate(chips)]
        for cp in first:
            cp.start()
        started += first

    def finish():
        for a in range(n):
            for j, chip in enumerate(chips):
                copy(a, 1 + j, (*chip, c), me).wait_recv()
                passed = copy(a, 4 + j, (*chip, c), sibling)
                passed.start()
                started.append(passed)
        for a in range(n):
            copy(a, 0, sibling, me).wait_recv()
            for j, chip in enumerate(chips):
                copy(a, 4 + j, (*chip, 1 - c), me).wait_recv()
        for cp in started:
            cp.wait_send()
        for cp in mine:
            cp.wait()

    return finish


def _direct_gather(src_ref, dst_ref, send_sems, recv_sems):
    x, y, c = lax.axis_index("x"), lax.axis_index("y"), lax.axis_index("c")
    me = 4 * x + 2 * y + c
    dst_ref[me] = src_ref[...]
    sends, landings = [], []
    for k in range(1, N_DEV):
        peer = (1 - x if k & 4 else x, 1 - y if k & 2 else y, 1 - c if k & 1 else c)
        pid = 4 * peer[0] + 2 * peer[1] + peer[2]
        for dst, out in ((me, sends), (pid, landings)):
            out.append(pltpu.make_async_remote_copy(
                src_ref=src_ref, dst_ref=dst_ref.at[dst], send_sem=send_sems.at[k - 1], recv_sem=recv_sems.at[k - 1],
                device_id=peer, device_id_type=pl.DeviceIdType.MESH))
    for cp in sends:
        cp.start()

    def finish():
        for cp in landings:
            cp.wait_recv()
        for cp in sends:
            cp.wait_send()

    return finish


def _all_gather(arrays, name, in_vmem):
    space = pltpu.VMEM if in_vmem else pl.ANY
    n = len(arrays)

    def body(*refs):
        _two_level_gather(refs[:n], refs[n:2 * n], *refs[2 * n:])()

    outs = pl.pallas_call(
        body, name=name,
        out_shape=tuple(jax.ShapeDtypeStruct((N_DEV,) + a.shape, a.dtype) for a in arrays),
        in_specs=[pl.BlockSpec(memory_space=space)] * n,
        out_specs=tuple(pl.BlockSpec(memory_space=space) for _ in arrays),
        scratch_shapes=[pltpu.SemaphoreType.DMA((7 * n,)), pltpu.SemaphoreType.DMA((7 * n,)),
                        pltpu.SemaphoreType.DMA((n,))],
    )(*arrays)
    return list(outs)


class _Riding:
    def __init__(self, kind, arrays):
        self.kind, self.arrays, self.n = kind, list(arrays), len(arrays)
        lead = (N_DEV,) if kind == "gather" else ()
        self.out_shape = [jax.ShapeDtypeStruct(lead + a.shape, a.dtype) for a in self.arrays]
        self.specs = [pl.BlockSpec(memory_space=pl.ANY)] * self.n
        self.scratch = [pltpu.SemaphoreType.DMA((7 * self.n,)), pltpu.SemaphoreType.DMA((7 * self.n,)),
                        pltpu.SemaphoreType.DMA((self.n,))]

    def copies(self, x_refs, y_refs, send_sems, recv_sems, local_sems):
        x, y, c = lax.axis_index("x"), lax.axis_index("y"), lax.axis_index("c")
        me = 4 * x + 2 * y + c
        local, sends, landings = [], [], []
        for a in range(self.n):
            src_mine = x_refs[a] if self.kind == "gather" else x_refs[a].at[me]
            local.append(pltpu.make_async_copy(src_mine, y_refs[a].at[me], local_sems.at[a]))
            for k in range(1, N_DEV):
                peer = (1 - x if k & 4 else x, 1 - y if k & 2 else y, 1 - c if k & 1 else c)
                pid = 4 * peer[0] + 2 * peer[1] + peer[2]
                src = x_refs[a] if self.kind == "gather" else x_refs[a].at[pid]
                for dst, out in ((me, sends), (pid, landings)):
                    out.append(pltpu.make_async_remote_copy(
                        src_ref=src, dst_ref=y_refs[a].at[dst],
                        send_sem=send_sems.at[7 * a + k - 1], recv_sem=recv_sems.at[7 * a + k - 1],
                        device_id=peer, device_id_type=pl.DeviceIdType.MESH))
        return local, sends, landings

    def run(self, first, last, x_refs, y_refs, sems, middle=None):
        if self.n == 0:
            return None
        local, sends, landings = self.copies(x_refs, y_refs, *sems)

        @pl.when(first)
        def _():
            for cp in local + sends:
                cp.start()

        return local, sends, landings, last

    @staticmethod
    def finish(state):
        if state is None:
            return
        local, sends, landings, last = state

        @pl.when(last)
        def _():
            for cp in landings:
                cp.wait_recv()
            for cp in sends:
                cp.wait_send()
            for cp in local:
                cp.wait()


class _RidingReduce:
    def __init__(self, arrays):
        self.arrays, self.n = list(arrays), len(arrays)
        self.out_shape = [jax.ShapeDtypeStruct((4,) + a.shape[1:], a.dtype) for a in self.arrays]
        self.specs = [pl.BlockSpec(memory_space=pl.ANY)] * self.n
        self.scratch = [pltpu.VMEM((4,) + a.shape[1:], a.dtype) for a in self.arrays for _ in range(3)]
        self.scratch += [pltpu.SemaphoreType.DMA((self.n,)) for _ in range(6)]

    def run(self, first, last, x_refs, y_refs, scratch, middle):
        n = self.n
        own, sib, tot = scratch[0:3 * n:3], scratch[1:3 * n:3], scratch[2:3 * n:3]
        d2d_send, d2d_recv, local_in, ici_send, ici_recv, local_out = scratch[3 * n:]
        x, y, c = lax.axis_index("x"), lax.axis_index("y"), lax.axis_index("c")
        my_chip = 2 * x + y
        sibling = (x, y, 1 - c)
        others = [(1 - x, y), (x, 1 - y), (1 - x, 1 - y)]

        def to_sibling(a, j=None):
            src = x_refs[a].at[pl.ds(0, 4)] if j is None else x_refs[a].at[2 * j + 1 - c]
            dst = sib[a] if j is None else sib[a].at[j]
            return pltpu.make_async_remote_copy(src_ref=src, dst_ref=dst, send_sem=d2d_send.at[a],
                                                recv_sem=d2d_recv.at[a], device_id=sibling,
                                                device_id_type=pl.DeviceIdType.MESH)

        def mine_in(a, j=None):
            src = x_refs[a].at[pl.ds(0, 4)] if j is None else x_refs[a].at[2 * j + c]
            return pltpu.make_async_copy(src, own[a] if j is None else own[a].at[j], local_in.at[a])

        def to_chip(a, chip=None):
            if chip is None:
                src, dst, peer = tot[a].at[pl.ds(0, 3)], y_refs[a].at[pl.ds(0, 3)], sibling
            else:
                src, dst, peer = tot[a].at[2 * chip[0] + chip[1]], y_refs[a].at[my_chip], (*chip, c)
            return pltpu.make_async_remote_copy(src_ref=src, dst_ref=dst, send_sem=ici_send.at[a],
                                                recv_sem=ici_recv.at[a], device_id=peer,
                                                device_id_type=pl.DeviceIdType.MESH)

        def mine_out(a):
            return pltpu.make_async_copy(tot[a].at[my_chip], y_refs[a].at[my_chip], local_out.at[a])

        @pl.when(first)
        def _():
            for a in range(n):
                for j in range(4):
                    to_sibling(a, j).start()
                    mine_in(a, j).start()

        @pl.when(middle)
        def _():
            for a in range(n):
                to_sibling(a).wait_recv()
                to_sibling(a).wait_send()
                mine_in(a).wait()
                tot[a][...] = (own[a][...].astype(F32) + sib[a][...].astype(F32)).astype(tot[a].dtype)
                for chip in others:
                    to_chip(a, chip).start()
                mine_out(a).start()

        def finish():
            @pl.when(last)
            def _():
                for a in range(n):
                    to_chip(a).wait_recv()
                    to_chip(a).wait_send()
                    mine_out(a).wait()

        return finish

    @staticmethod
    def finish(state):
        state()


_DIMS ={"nn": (((1,), (0,)), ((), ())), "nt": (((1,), (1,)), ((), ())), "tn": (((0,), (0,)), ((), ()))}
NT_DIMS = _DIMS["nt"]
TN_DIMS = _DIMS["tn"]


def _swap8(x):
    lane = lax.broadcasted_iota(jnp.int32, x.shape, 1)
    return jnp.where((lane & 15) < 8, pltpu.roll(x, LANES - 8, 1), pltpu.roll(x, 8, 1))


def _rope(x, cos, sgn, bwd):
    return x * cos + (_swap8(x * sgn) if bwd else _swap8(x) * sgn)


def _matmul(a, b, *, mode, name, out_dtype=F32, tm=512, tn=512, tk=512, m=None, k=None,
            epilogue=None, extra=(), slots=None):
    if mode == "nn":
        m = a.shape[0] if m is None else m
        k = a.shape[1]
        n = N_DEV * b.shape[2] if slots == "b_cols" else b.shape[1]
    elif mode == "nt":
        m = a.shape[0] if m is None else m
        k = a.shape[1]
        n = b.shape[0]
    else:
        k = a.shape[0] if k is None else k
        m, n = a.shape[1], b.shape[1]
    tm, tn, tk = min(tm, m), min(tn, n), min(tk, k)
    if slots == "b_cols":
        tn = b.shape[2]
    if slots == "out":
        tn = n // N_DEV
    assert m % tm == 0 and n % tn == 0 and k % tk == 0, (name, m, n, k, tm, tn, tk)
    nk = k // tk
    dims = _DIMS[mode]
    a_spec = (pl.BlockSpec((tk, tm), lambda i, j, kk: (kk, i)) if mode == "tn"
              else pl.BlockSpec((tm, tk), lambda i, j, kk: (i, kk)))
    if slots == "b_cols":
        b_spec = pl.BlockSpec((None, tk, tn), lambda i, j, kk: (j, kk, 0))
    elif mode == "nt":
        b_spec = pl.BlockSpec((tn, tk), lambda i, j, kk: (j, kk))
    else:
        b_spec = pl.BlockSpec((tk, tn), lambda i, j, kk: (kk, j))
    tile = pl.BlockSpec((tm, tn), lambda i, j, kk: (i, j))
    if slots == "out":
        o_spec = pl.BlockSpec((None, tm, tn), lambda i, j, kk: (j, i, 0))
        o_shape = (N_DEV, m, tn)
    else:
        o_spec, o_shape = tile, (m, n)
    in_specs, args = [a_spec, b_spec], [a, b]
    if epilogue == "drelu2":
        in_specs.append(tile)
    args += list(extra)
    if epilogue == "relu2":
        out_shape = (jax.ShapeDtypeStruct(o_shape, BF16), jax.ShapeDtypeStruct(o_shape, BF16))
        out_specs = (o_spec, o_spec)
    else:
        out_shape = jax.ShapeDtypeStruct(o_shape, out_dtype)
        out_specs = o_spec
    n_in = len(args)
    n_out = 2 if epilogue == "relu2" else 1

    def body(*refs):
        a_ref, b_ref = refs[0], refs[1]
        outs = refs[n_in:n_in + n_out]
        part = lax.dot_general(a_ref[...], b_ref[...], dims, preferred_element_type=F32)

        def finish(acc):
            if epilogue == "relu2":
                outs[0][...] = acc.astype(BF16)
                r = jnp.maximum(acc, 0.0)
                outs[1][...] = (r * r).astype(BF16)
            elif epilogue == "drelu2":
                u = refs[2][...].astype(F32)
                outs[0][...] = (acc * (2.0 * jnp.maximum(u, 0.0))).astype(out_dtype)
            else:
                outs[0][...] = acc.astype(out_dtype)

        if nk == 1:
            finish(part)
        else:
            acc_ref = refs[n_in + n_out]
            kk = pl.program_id(2)

            @pl.when(kk == 0)
            def _():
                acc_ref[...] = part

            @pl.when(kk > 0)
            def _():
                acc_ref[...] += part

            @pl.when(kk == nk - 1)
            def _():
                finish(acc_ref[...])

    return pl.pallas_call(
        body, name=name, grid=(m // tm, n // tn, nk),
        out_shape=out_shape, in_specs=in_specs, out_specs=out_specs,
        scratch_shapes=[pltpu.VMEM((tm, tn), F32)] if nk > 1 else [],
        compiler_params=_params(("parallel", "parallel", "arbitrary"), VMEM_BIG),
    )(*args)


def _rstd(x):
    return lax.rsqrt(jnp.mean(x * x, axis=1, keepdims=True) + EPS)


def _norm_bwd(dxn, xn, r):
    return r * (dxn - xn * jnp.mean(dxn * xn, axis=1, keepdims=True))


def _vec(col):
    return pl.BlockSpec((1, D_MODEL), lambda i: (0, col))


def _matmul_rows(a, b, epi, *, mode, name, tm, tk, rows=(), vecs=(), out_dtypes=(), sums=False, slots=None,
                 riding=None):
    m, k = a.shape
    n = D_MODEL
    tm, tk = min(tm, m), min(tk, k)
    riding = riding or _Riding("gather", [])
    group = 1
    if slots == "b_contract":
        group = max(1, tk // b.shape[2])
        tk = group * b.shape[2]
        b_spec = pl.BlockSpec((group, n, tk // group), lambda i, kk: (kk, 0, 0))
    elif mode == "nt":
        b_spec = pl.BlockSpec((n, tk), lambda i, kk: (0, kk))
    else:
        b_spec = pl.BlockSpec((tk, n), lambda i, kk: (kk, 0))
    assert m % tm == 0 and k % tk == 0, (name, m, k, tm, tk)
    ni, nk = m // tm, k // tk
    assert ni >= 2 or not isinstance(riding, _RidingReduce), "the two-level exchange needs a middle grid step"
    dims = _DIMS[mode]
    tile = pl.BlockSpec((tm, n), lambda i, kk: (i, 0))
    in_specs = [pl.BlockSpec((tm, tk), lambda i, kk: (i, kk)), b_spec] + [tile] * len(rows)
    in_specs += [pl.BlockSpec((1, n), lambda i, kk, col=col: (0, col)) for _, col in vecs]
    args = [a, b, *rows, *[v for v, _ in vecs]]
    out_shape = [jax.ShapeDtypeStruct((m, n), dt) for dt in out_dtypes]
    out_specs = [tile] * len(out_dtypes)
    if sums:
        out_shape.append(jax.ShapeDtypeStruct((8, n), F32))
        out_specs.append(pl.BlockSpec((8, n), lambda i, kk: (0, 0)))
    n_rows, n_vecs, n_outs, nr = len(rows), len(vecs), len(out_dtypes), riding.n
    n_in = 2 + n_rows + n_vecs

    def body(*refs):
        a_ref, b_ref = refs[0], refs[1]
        row_refs = refs[2:2 + n_rows]
        vec_refs = refs[2 + n_rows:n_in]
        x_refs = refs[n_in:n_in + nr]
        out_refs = refs[n_in + nr:n_in + nr + n_outs]
        pos = n_in + nr + n_outs
        sums_ref = refs[pos] if sums else None
        pos += 1 if sums else 0
        y_refs = refs[pos:pos + nr]
        pos += nr
        acc_ref = refs[pos] if nk > 1 else None
        sem_refs = refs[pos + (1 if nk > 1 else 0):]
        i, kk = pl.program_id(0), pl.program_id(1)
        state = riding.run((i == 0) & (kk == 0), (i == ni - 1) & (kk == nk - 1), x_refs, y_refs, sem_refs,
                           middle=(i == 1) & (kk == 0))
        if slots == "b_contract":
            c = tk // group
            part = lax.dot_general(a_ref[:, 0:c], b_ref[0], dims, preferred_element_type=F32)
            for u in range(1, group):
                part = part + lax.dot_general(a_ref[:, u * c:(u + 1) * c], b_ref[u], dims, preferred_element_type=F32)
        else:
            part = lax.dot_general(a_ref[...], b_ref[...], dims, preferred_element_type=F32)

        def finish(acc):
            nsub = tm // ROW_TILE
            for r in range(nsub):
                blk = pl.ds(r * ROW_TILE, ROW_TILE)
                epi(acc[r * ROW_TILE:(r + 1) * ROW_TILE], [ref.at[blk] for ref in row_refs], vec_refs,
                    [ref.at[blk] for ref in out_refs], sums_ref,
                    (i == 0) if r == 0 else None, (i == ni - 1) if r == nsub - 1 else None)

        if nk == 1:
            finish(part)
        else:
            @pl.when(kk == 0)
            def _():
                acc_ref[...] = part

            @pl.when(kk > 0)
            def _():
                acc_ref[...] += part

            @pl.when(kk == nk - 1)
            def _():
                finish(acc_ref)

        riding.finish(state)

    outs = pl.pallas_call(
        body, name=name, grid=(ni, nk),
        out_shape=(*out_shape, *riding.out_shape),
        in_specs=[*in_specs, *riding.specs], out_specs=(*out_specs, *riding.specs),
        scratch_shapes=([pltpu.VMEM((tm, n), F32)] if nk > 1 else []) + (riding.scratch if nr else []),
        compiler_params=_params(("arbitrary", "arbitrary"), VMEM_BIG),
    )(*args, *riding.arrays)
    n_own = len(out_shape)
    return list(outs[:n_own]), list(outs[n_own:])


def _zero_sums_at_start(sums_ref, first):
    if first is not None:
        @pl.when(first)
        def _():
            sums_ref[...] = jnp.zeros_like(sums_ref)


def _epi_resid_modulate(acc, rows, vecs, outs, sums_ref, first, last):
    (x_ref,), (g_ref, sh_ref, sc_ref) = rows, vecs
    x1 = x_ref[...] + g_ref[...] * acc
    outs[0][...] = acc
    outs[1][...] = x1
    outs[2][...] = (x1 * _rstd(x1) * (1.0 + sc_ref[...]) + sh_ref[...]).astype(BF16)


def _epi_final(acc, rows, vecs, outs, sums_ref, first, last):
    (x1_ref, t_ref), (g_ref, gf_ref) = rows, vecs
    d = acc.shape[1]
    x2 = x1_ref[...] + g_ref[...] * acc
    r = _rstd(x2)
    xn = x2 * r
    err = xn * gf_ref[...] - t_ref[...]
    dy = err * (1.0 / d)
    dx2 = _norm_bwd(dy * gf_ref[...], xn, r)
    outs[0][...] = dx2
    outs[1][...] = (dx2 * g_ref[...]).astype(BF16)
    _zero_sums_at_start(sums_ref, first)
    sums_ref[0:1, :] += jnp.sum(dy * xn, axis=0, keepdims=True)
    sums_ref[1:2, :] += jnp.sum(dx2 * acc, axis=0, keepdims=True)
    sums_ref[2:3, :] += jnp.sum(err * err, axis=0, keepdims=True)

    if last is not None:
        @pl.when(last)
        def _():
            tot = jnp.sum(sums_ref[2:3, :], axis=1, keepdims=True) * (0.5 / d)
            sums_ref[3:4, :] = jnp.broadcast_to(tot, (1, d))


def _epi_modulate2_bwd(acc, rows, vecs, outs, sums_ref, first, last):
    (x_ref, dres_ref, o_ref), (sc_ref, g_ref) = rows, vecs
    x = x_ref[...]
    r = _rstd(x)
    xn = x * r
    dx = dres_ref[...] + _norm_bwd(acc * (1.0 + sc_ref[...]), xn, r)
    outs[0][...] = dx
    outs[1][...] = (dx * g_ref[...]).astype(BF16)
    _zero_sums_at_start(sums_ref, first)
    sums_ref[0:1, :] += jnp.sum(acc * xn, axis=0, keepdims=True)
    sums_ref[1:2, :] += jnp.sum(acc, axis=0, keepdims=True)
    sums_ref[2:3, :] += jnp.sum(dx * o_ref[...], axis=0, keepdims=True)


def _epi_modulate1_bwd(acc, rows, vecs, outs, sums_ref, first, last):
    (add_ref, x_ref, dres_ref), (sc_ref,) = rows, vecs
    dh = acc + add_ref[...]
    x = x_ref[...]
    r = _rstd(x)
    xn = x * r
    outs[0][...] = dres_ref[...] + _norm_bwd(dh * (1.0 + sc_ref[...]), xn, r)
    _zero_sums_at_start(sums_ref, first)
    sums_ref[0:1, :] += jnp.sum(dh * xn, axis=0, keepdims=True)
    sums_ref[1:2, :] += jnp.sum(dh, axis=0, keepdims=True)


def _modulate_all(x, ctx, mod, mod_ctx, name):
    s, d = x.shape
    t = s + ctx.shape[0]
    ns = s // ROW_TILE
    nc = ctx.shape[0] // ROW_TILE

    def body(x_ref, c_ref, sh_ref, sc_ref, shc_ref, scc_ref, h_ref):
        i = pl.program_id(0)

        @pl.when(i < ns)
        def _():
            v = x_ref[...]
            h_ref[...] = (v * _rstd(v) * (1.0 + sc_ref[...]) + sh_ref[...]).astype(BF16)

        @pl.when(i >= ns)
        def _():
            v = c_ref[...]
            h_ref[...] = (v * _rstd(v) * (1.0 + scc_ref[...]) + shc_ref[...]).astype(BF16)

    return pl.pallas_call(
        body, name=name, grid=(ns + nc,),
        out_shape=jax.ShapeDtypeStruct((t, d), BF16),
        in_specs=[pl.BlockSpec((ROW_TILE, d), lambda i: (jnp.minimum(i, ns - 1), 0)),
                  pl.BlockSpec((ROW_TILE, d), lambda i: (jnp.maximum(i - ns, 0), 0)),
                  _vec(0), _vec(1), _vec(0), _vec(1)],
        out_specs=pl.BlockSpec((ROW_TILE, d), lambda i: (i, 0)),
        compiler_params=_params(("arbitrary",)),
    )(x, ctx, mod, mod, mod_ctx, mod_ctx)


def _modulate_sums(dh, row_off, xsrc):
    s, d = xsrc.shape

    def body(dh_ref, x_ref, sums_ref):
        i = pl.program_id(0)
        x = x_ref[...]
        dhv = dh_ref[...]

        @pl.when(i == 0)
        def _():
            sums_ref[...] = jnp.zeros_like(sums_ref)

        sums_ref[0:1, :] += jnp.sum(dhv * (x * _rstd(x)), axis=0, keepdims=True)
        sums_ref[1:2, :] += jnp.sum(dhv, axis=0, keepdims=True)

    return pl.pallas_call(
        body, name="modulate1_ctx_bwd", grid=(s // ROW_TILE,),
        out_shape=jax.ShapeDtypeStruct((8, d), F32),
        in_specs=[pl.BlockSpec((ROW_TILE, d), lambda i: (i + row_off, 0)), pl.BlockSpec((ROW_TILE, d), lambda i: (i, 0))],
        out_specs=pl.BlockSpec((8, d), lambda i: (0, 0)),
        compiler_params=_params(("arbitrary",)),
    )(dh, xsrc)


def _head_fwd(h_all, win_head, wq, wk, q_gain, kv_gain, cos, sgn, tm, name):
    t, d = h_all.shape
    nq, nkv = wq.shape[1], wk.shape[1]

    def body(h_ref, wi_ref, wq_ref, wk_ref, qg_ref, kg_ref, c_ref, s_ref, z_ref, cq_ref, kvin_ref, qf_ref, kv_ref):
        z = lax.dot_general(h_ref[...], wi_ref[...], NT_DIMS, preferred_element_type=F32)
        z_ref[...] = z
        cos, sgn = c_ref[...], s_ref[...]
        zq = z[:, 0:Q_RANK]
        cq = (zq * _rstd(zq) * qg_ref[...]).astype(BF16)
        cq_ref[...] = cq
        zk = z[:, Q_RANK:Q_RANK + KV_RANK]
        kv_in = jnp.concatenate([(zk * _rstd(zk) * kg_ref[...]).astype(BF16),
                                 _rope(z[:, Q_RANK + KV_RANK:HEAD_COLS], cos, sgn, False).astype(BF16)], axis=1)
        kvin_ref[...] = kv_in
        q = jnp.dot(cq, wq_ref[...], preferred_element_type=F32)
        for h in range(nq // LANES):
            sl = slice(h * LANES, (h + 1) * LANES)
            qf_ref[:, sl] = _rope(q[:, sl], cos, sgn, False).astype(BF16)
        kv_ref[...] = jnp.dot(kv_in, wk_ref[...], preferred_element_type=F32).astype(BF16)

    def row(w):
        return pl.BlockSpec((tm, w), lambda i: (i, 0))

    def whole(a):
        return pl.BlockSpec(a.shape, lambda i: (0, 0))

    return pl.pallas_call(
        body, name=name, grid=(t // tm,),
        out_shape=(jax.ShapeDtypeStruct((t, HEAD_COLS), F32), jax.ShapeDtypeStruct((t, Q_RANK), BF16),
                   jax.ShapeDtypeStruct((t, KV_RANK + LANES), BF16), jax.ShapeDtypeStruct((t, nq), BF16),
                   jax.ShapeDtypeStruct((t, nkv), BF16)),
        in_specs=[row(d), whole(win_head), whole(wq), whole(wk), whole(q_gain), whole(kv_gain), row(LANES), row(LANES)],
        out_specs=(row(HEAD_COLS), row(Q_RANK), row(KV_RANK + LANES), row(nq), row(nkv)),
        compiler_params=_params(("parallel",), VMEM_BIG),
    )(h_all, win_head, wq, wk, q_gain, kv_gain, cos, sgn)


def _head_bwd(dq, dk, dv, z, wq, wk_k, wk_v, win_head, q_gain, kv_gain, cos, sgn, s, name):
    t = z.shape[0]
    ns = s // ROW_TILE

    def body(dq_ref, dk_ref, dv_ref, z_ref, wq_ref, wkk_ref, wkv_ref, wi_ref, qg_ref, kg_ref, c_ref, s_ref,
             dz_ref, dh_ref, sums_ref):
        i = pl.program_id(0)

        @pl.when(i == 0)
        def _():
            sums_ref[...] = jnp.zeros_like(sums_ref)

        @pl.when(i < ns)
        def _():
            dc = lax.dot_general(dq_ref[...], wq_ref[...], NT_DIMS, preferred_element_type=F32)
            zq = z_ref[:, 0:Q_RANK]
            r = _rstd(zq)
            zn = zq * r
            sums_ref[0:1, :] += jnp.sum(dc * zn, axis=0, keepdims=True)
            dz_ref[:, 0:Q_RANK] = _norm_bwd(dc * qg_ref[...], zn, r).astype(BF16)

        @pl.when(i >= ns)
        def _():
            dz_ref[:, 0:Q_RANK] = jnp.zeros((ROW_TILE, Q_RANK), BF16)

        dkv = (lax.dot_general(dk_ref[...], wkk_ref[...], NT_DIMS, preferred_element_type=F32)
               + lax.dot_general(dv_ref[...], wkv_ref[...], NT_DIMS, preferred_element_type=F32))
        zk = z_ref[:, Q_RANK:Q_RANK + KV_RANK]
        r = _rstd(zk)
        zn = zk * r
        dc = dkv[:, 0:KV_RANK]
        sums_ref[1:2, 0:KV_RANK] += jnp.sum(dc * zn, axis=0, keepdims=True)
        dz_ref[:, Q_RANK:Q_RANK + KV_RANK] = _norm_bwd(dc * kg_ref[...], zn, r).astype(BF16)
        dz_ref[:, Q_RANK + KV_RANK:HEAD_COLS] = _rope(dkv[:, KV_RANK:KV_RANK + LANES], c_ref[...], s_ref[...],
                                                       True).astype(BF16)
        dh_ref[...] = jnp.dot(dz_ref[...], wi_ref[...], preferred_element_type=F32)

    def row(w):
        return pl.BlockSpec((ROW_TILE, w), lambda i: (i, 0))

    def whole(a):
        return pl.BlockSpec(a.shape, lambda i: (0, 0))

    return pl.pallas_call(
        body, name=name, grid=(t // ROW_TILE,),
        out_shape=(jax.ShapeDtypeStruct((t, HEAD_COLS), BF16), jax.ShapeDtypeStruct((t, D_MODEL), F32),
                   jax.ShapeDtypeStruct((8, Q_RANK), F32)),
        in_specs=[pl.BlockSpec((ROW_TILE, dq.shape[1]), lambda i: (jnp.minimum(i, ns - 1), 0)),
                  row(dk.shape[1]), row(dv.shape[1]), row(HEAD_COLS), whole(wq), whole(wk_k), whole(wk_v),
                  whole(win_head), whole(q_gain), whole(kv_gain), row(LANES), row(LANES)],
        out_specs=(row(HEAD_COLS), row(D_MODEL), pl.BlockSpec((8, Q_RANK), lambda i: (0, 0))),
        compiler_params=_params(("arbitrary",), VMEM_BIG),
    )(dq, dk, dv, z, wq, wk_k, wk_v, win_head, q_gain, kv_gain, cos, sgn)


def _shift_rows(u, s):
    rowi = lax.broadcasted_iota(jnp.int32, u.shape, 0)
    prev = jnp.where(rowi == 0, 0.0, pltpu.roll(u, 1, 0))
    nxt = jnp.where(rowi == s - 1, 0.0, pltpu.roll(u, s - 1, 0))
    return prev, nxt


def _conv_fwd(z_conv, cw, a_cat, name):
    s = z_conv.shape[0]

    def body(z_ref, w_ref, a_in_ref, o_ref):
        del a_in_ref
        gb, gc, xv = z_ref[:, 0:LANES], z_ref[:, LANES:2 * LANES], z_ref[:, 2 * LANES:3 * LANES]
        u = gc * xv
        prev, nxt = _shift_rows(u, s)
        y = w_ref[0:1, :] * prev + w_ref[1:2, :] * u + w_ref[2:3, :] * nxt
        o_ref[...] = (gb * y).astype(BF16)

    return pl.pallas_call(
        body, name=name, grid=(CONV_W // LANES,),
        out_shape=jax.ShapeDtypeStruct(a_cat.shape, a_cat.dtype),
        in_specs=[pl.BlockSpec((s, 3 * LANES), lambda j: (0, j)), pl.BlockSpec((3, LANES), lambda j: (0, j)),
                  pl.BlockSpec(memory_space=pl.ANY)],
        out_specs=pl.BlockSpec((s, LANES), lambda j: (0, 4 + j)),
        input_output_aliases={2: 0},
        compiler_params=_params(("parallel",), VMEM_BIG),
    )(z_conv, cw, a_cat)


def _conv_bwd(z_conv, cw, da, name):
    s = z_conv.shape[0]

    def body(z_ref, w_ref, da_ref, dz_ref, dw_ref):
        gb, gc, xv = z_ref[:, 0:LANES], z_ref[:, LANES:2 * LANES], z_ref[:, 2 * LANES:3 * LANES]
        u = gc * xv
        prev, nxt = _shift_rows(u, s)
        dcv = da_ref[...]
        dz_ref[:, 0:LANES] = (dcv * (w_ref[0:1, :] * prev + w_ref[1:2, :] * u + w_ref[2:3, :] * nxt)).astype(BF16)
        dy = dcv * gb
        dw_ref[0:1, :] = jnp.sum(dy * prev, axis=0, keepdims=True)
        dw_ref[1:2, :] = jnp.sum(dy * u, axis=0, keepdims=True)
        dw_ref[2:3, :] = jnp.sum(dy * nxt, axis=0, keepdims=True)
        dyp, dyn = _shift_rows(dy, s)
        du = w_ref[0:1, :] * dyn + w_ref[1:2, :] * dy + w_ref[2:3, :] * dyp
        dz_ref[:, LANES:2 * LANES] = (du * xv).astype(BF16)
        dz_ref[:, 2 * LANES:3 * LANES] = (du * gc).astype(BF16)

    blk = pl.BlockSpec((s, 3 * LANES), lambda j: (0, j))
    cws = pl.BlockSpec((3, LANES), lambda j: (0, j))
    return pl.pallas_call(
        body, name=name, grid=(CONV_W // LANES,),
        out_shape=(jax.ShapeDtypeStruct(z_conv.shape, BF16), jax.ShapeDtypeStruct((3, CONV_W), F32)),
        in_specs=[blk, cws, pl.BlockSpec((s, LANES), lambda j: (0, 4 + j))], out_specs=(blk, cws),
        compiler_params=_params(("parallel",), VMEM_BIG),
    )(z_conv, cw, da)


ATT_TQ = 256
ATT_Q_STEP = 1024
ATT_TQ_BWD = 512


def _head_mask(shape, hh):
    lane = lax.broadcasted_iota(jnp.int32, shape, 1)
    return (lane >= hh * V_DIM) & (lane < (hh + 1) * V_DIM)


def _attn_fwd(qf, kv, s, riding, name):
    t = kv.shape[0]
    step = min(ATT_Q_STEP, s)
    nq = s // step
    nr = riding.n

    def body(*refs):
        q_ref, k_ref, v_ref = refs[:3]
        o_ref, ob_ref, st_ref = refs[3 + nr:6 + nr]
        p, i = pl.program_id(0), pl.program_id(1)
        state = riding.run((p == 0) & (i == 0), (p == N_HEADS // 2 - 1) & (i == nq - 1),
                           refs[3:3 + nr], refs[6 + nr:6 + 2 * nr], refs[6 + 2 * nr:])
        v = v_ref[...]
        vlane = lax.broadcasted_iota(jnp.int32, v.shape, 1)
        one_lane = [(1 - hh) * V_DIM for hh in range(2)]
        vm = [jnp.where(_head_mask(v.shape, hh), v, jnp.where(vlane == one_lane[hh], 1.0, 0.0).astype(BF16))
              for hh in range(2)]

        def block(r, carry):
            rows = pl.ds(pl.multiple_of(r * ATT_TQ, ATT_TQ), ATT_TQ)
            olane = lax.broadcasted_iota(jnp.int32, (ATT_TQ, LANES), 1)
            acc = jnp.zeros((ATT_TQ, LANES), F32)
            stat = jnp.zeros((ATT_TQ, LANES), F32)
            for hh in range(2):
                sl = slice(hh * LANES, (hh + 1) * LANES)
                sc = lax.dot_general(q_ref[rows, sl], k_ref[:, sl], NT_DIMS, preferred_element_type=F32)
                mx = jnp.max(sc, axis=1, keepdims=True)
                e = jnp.exp2((sc - mx) * EXP2_SCALE).astype(BF16)
                res = jnp.dot(e, vm[hh], preferred_element_type=F32)
                den = jnp.sum(jnp.where(olane == one_lane[hh], res, 0.0), axis=1, keepdims=True)
                acc = acc + jnp.where(_head_mask(res.shape, hh), res * (1.0 / den), 0.0)
                stat = stat + jnp.where(olane == hh, mx * EXP2_SCALE + jnp.log(den) * LOG2_E, 0.0)
            o_ref[rows, :] = acc
            ob_ref[rows, :] = acc.astype(BF16)
            st_ref[:, rows] = stat.T[0:8, :]
            return carry

        lax.fori_loop(0, step // ATT_TQ, block, 0)
        riding.finish(state)

    o_spec = pl.BlockSpec((step, LANES), lambda p, i: (i, p))
    outs = pl.pallas_call(
        body, name=name, grid=(N_HEADS // 2, nq),
        out_shape=(jax.ShapeDtypeStruct((s, N_HEADS * V_DIM), F32),
                   jax.ShapeDtypeStruct((s, D_MODEL), BF16),
                   jax.ShapeDtypeStruct((N_HEADS // 2 * 8, s), F32), *riding.out_shape),
        in_specs=[pl.BlockSpec((step, 2 * LANES), lambda p, i: (i, p)),
                  pl.BlockSpec((t, 2 * LANES), lambda p, i: (0, p)),
                  pl.BlockSpec((t, LANES), lambda p, i: (0, N_HEADS + p)), *riding.specs],
        out_specs=(o_spec, o_spec, pl.BlockSpec((8, step), lambda p, i: (p, i)), *riding.specs),
        scratch_shapes=riding.scratch,
        compiler_params=_params(("arbitrary", "arbitrary"), VMEM_BIG),
    )(qf, kv, kv, *riding.arrays)
    return outs[0], outs[1], outs[2], list(outs[3:])


def _attn_bwd(qf, kv, o, da, stats, cos, sgn, riding, name):
    s, t = o.shape[0], kv.shape[0]
    ATT_TQ = ATT_TQ_BWD
    nq = s // ATT_TQ
    nr = riding.n

    def body(*refs):
        q_ref, k_ref, v_ref, o_ref, do_ref, st_ref, c_ref, s_ref = refs[:8]
        dq_ref, dk_ref, dv_ref = refs[8 + nr:11 + nr]
        dk_acc, dv_acc = refs[11 + 2 * nr:13 + 2 * nr]
        p, i = pl.program_id(0), pl.program_id(1)
        state = riding.run((p == 0) & (i == 0), (p == N_HEADS // 2 - 1) & (i == nq - 1),
                           refs[8:8 + nr], refs[11 + nr:11 + 2 * nr], refs[13 + 2 * nr:])

        @pl.when(i == 0)
        def _():
            dk_acc[...] = jnp.zeros_like(dk_acc)
            dv_acc[...] = jnp.zeros_like(dv_acc)

        v = v_ref[...]
        do = do_ref[...]
        od = do * o_ref[...]
        ones = jnp.ones((8, LANES), F32)
        for hh in range(2):
            sl = slice(hh * LANES, (hh + 1) * LANES)
            q, k = q_ref[:, sl], k_ref[:, sl]
            mask = _head_mask(do.shape, hh)
            dom = jnp.where(mask, do, 0.0).astype(BF16)
            delta = lax.dot_general(ones, jnp.where(mask, od, 0.0), NT_DIMS, preferred_element_type=F32,
                                    precision=lax.Precision.HIGHEST)[0:1, :]
            st = lax.dot_general(k, q, NT_DIMS, preferred_element_type=F32)
            pt = jnp.exp2(st * EXP2_SCALE - st_ref[hh:hh + 1, :]).astype(BF16)
            dpt = lax.dot_general(v, dom, NT_DIMS, preferred_element_type=F32)
            dst = (pt.astype(F32) * (dpt - delta)).astype(BF16)
            dv_acc[...] += jnp.dot(pt, dom, preferred_element_type=F32)
            dk_acc[:, sl] += jnp.dot(dst, q, preferred_element_type=F32)
            dq = lax.dot_general(dst, k, TN_DIMS, preferred_element_type=F32) * ATTN_SCALE
            dq_ref[:, sl] = _rope(dq, c_ref[...], s_ref[...], True).astype(BF16)

        @pl.when(i == nq - 1)
        def _():
            dk_ref[...] = (dk_acc[...] * ATTN_SCALE).astype(BF16)
            dv_ref[...] = dv_acc[...].astype(BF16)

        riding.finish(state)

    o_spec = pl.BlockSpec((ATT_TQ, LANES), lambda p, i: (i, p))
    tab = pl.BlockSpec((ATT_TQ, LANES), lambda p, i: (i, 0))
    outs = pl.pallas_call(
        body, name=name, grid=(N_HEADS // 2, nq),
        out_shape=(jax.ShapeDtypeStruct((s, N_HEADS * LANES), BF16),
                   jax.ShapeDtypeStruct((t, N_HEADS * LANES), BF16),
                   jax.ShapeDtypeStruct((t, N_HEADS * V_DIM), BF16), *riding.out_shape),
        in_specs=[pl.BlockSpec((ATT_TQ, 2 * LANES), lambda p, i: (i, p)),
                  pl.BlockSpec((t, 2 * LANES), lambda p, i: (0, p)),
                  pl.BlockSpec((t, LANES), lambda p, i: (0, N_HEADS + p)),
                  o_spec, o_spec,
                  pl.BlockSpec((8, ATT_TQ), lambda p, i: (p, i)), tab, tab, *riding.specs],
        out_specs=(pl.BlockSpec((ATT_TQ, 2 * LANES), lambda p, i: (i, p)),
                   pl.BlockSpec((t, 2 * LANES), lambda p, i: (0, p)),
                   pl.BlockSpec((t, LANES), lambda p, i: (0, p)), *riding.specs),
        scratch_shapes=[pltpu.VMEM((t, 2 * LANES), F32), pltpu.VMEM((t, LANES), F32), *riding.scratch],
        compiler_params=_params(("arbitrary", "arbitrary"), VMEM_BIG),
    )(qf, kv, kv, o, da, stats, cos, sgn, *riding.arrays)
    return outs[0], outs[1], outs[2], list(outs[3:])


def _silu(x):
    return x * (1.0 / (1.0 + jnp.exp(-x)))


def _prologue(c_rows, c_ctx, w_mod, b_cols, extra_rows, weights, name):
    nw = len(weights)
    d, cols = c_rows.shape[1], w_mod.shape[1]

    def body(c_ref, cctx_ref, wmod_ref, b_ref, x_ref, *rest):
        w_refs, (a_ref, modg_ref), wg_refs = rest[:nw], rest[nw:nw + 2], rest[nw + 2:2 * nw + 2]
        c_all, blk = rest[2 * nw + 2:2 * nw + 4]
        w_send, w_recv, w_local, c_send, c_recv, m_send, m_recv = rest[2 * nw + 4:]
        finish_c = _direct_gather(c_ref, c_all, c_send, c_recv)
        finish_weights = _two_level_gather(w_refs, wg_refs, w_send, w_recv, w_local)
        finish_c()
        a_ref[...] = jnp.zeros_like(a_ref)
        for j in range(N_DEV):
            a_ref[j:j + 1, :] = c_all[j, 0:1, :]
        a_ref[N_DEV:N_DEV + 1, :] = cctx_ref[...]
        blk[0:16, :] = jnp.dot(_silu(a_ref[...]), wmod_ref[...], preferred_element_type=F32,
                               precision=lax.Precision.HIGHEST) + b_ref[...]
        blk[16:24, :] = x_ref[...]
        finish_mod = _direct_gather(blk, modg_ref, m_send, m_recv)
        finish_weights()
        finish_mod()

    vmem, hbm = pl.BlockSpec(memory_space=pltpu.VMEM), pl.BlockSpec(memory_space=pl.ANY)
    outs = pl.pallas_call(
        body, name=name,
        out_shape=(jax.ShapeDtypeStruct((16, d), F32), jax.ShapeDtypeStruct((N_DEV, 24, cols), F32),
                   *[jax.ShapeDtypeStruct((N_DEV,) + w.shape, w.dtype) for w in weights]),
        in_specs=[vmem] * 5 + [hbm] * nw, out_specs=(vmem, vmem, *[hbm] * nw),
        scratch_shapes=[pltpu.VMEM((N_DEV, 8, d), F32), pltpu.VMEM((24, cols), F32),
                        pltpu.SemaphoreType.DMA((7 * nw,)), pltpu.SemaphoreType.DMA((7 * nw,)),
                        pltpu.SemaphoreType.DMA((nw,)), pltpu.SemaphoreType.DMA((7,)), pltpu.SemaphoreType.DMA((7,)),
                        pltpu.SemaphoreType.DMA((7,)), pltpu.SemaphoreType.DMA((7,))],
        compiler_params=_params(None, VMEM_BIG),
    )(c_rows, c_ctx, w_mod, b_cols, extra_rows, *weights)
    return outs[0], outs[1], list(outs[2:])


def _adaln_bwd(a_t, w, d_ex, d_ctx, d_all, name):
    def body(at_ref, w_ref, dex_ref, dctx_ref, dall_ref, gw_ref, dsil_ref, dsum_ref):
        sil_t = _silu(at_ref[...])
        dctx = dctx_ref[...]
        row = dctx[0:1, :]
        for j in range(1, N_DEV):
            row = row + dctx[j:j + 1, :]
        rowi = lax.broadcasted_iota(jnp.int32, dctx.shape, 0)
        ctx_rows = jnp.where(rowi == 0, jnp.broadcast_to(row, dctx.shape), 0.0)
        hi = lax.Precision.HIGHEST
        d_rows = jnp.concatenate([dex_ref[...], ctx_rows], axis=0)
        gw_ref[...] = jnp.dot(sil_t, d_rows, preferred_element_type=F32, precision=hi)
        dsil_ref[...] = lax.dot_general(ctx_rows, w_ref[...], NT_DIMS, preferred_element_type=F32, precision=hi)
        tot = dall_ref[0]
        for j in range(1, N_DEV):
            tot = tot + dall_ref[j]
        dsum_ref[...] = tot

    return pl.pallas_call(
        body, name=name,
        out_shape=(jax.ShapeDtypeStruct(w.shape, F32), jax.ShapeDtypeStruct((8, w.shape[0]), F32),
                   jax.ShapeDtypeStruct(d_all.shape[1:], F32)),
        compiler_params=_params(None, VMEM_BIG),
    )(a_t, w, d_ex, d_ctx, d_all)


def _pack_small(sums1, sums2, fsums, sums1c, psums, d_cw, name):
    d = D_MODEL

    def body(s1_ref, s2_ref, f_ref, s1c_ref, p_ref, cw_ref, o_ref):
        o_ref[...] = jnp.zeros_like(o_ref)
        for col, (ref, r) in enumerate([(s1_ref, 1), (s1_ref, 0), (s2_ref, 2), (s2_ref, 1), (s2_ref, 0), (f_ref, 1)]):
            o_ref[0:1, col * d:(col + 1) * d] = ref[r:r + 1, :]
        o_ref[1:2, 0:d] = s1c_ref[1:2, :]
        o_ref[1:2, d:2 * d] = s1c_ref[0:1, :]
        o_ref[2:3, 0:Q_RANK] = p_ref[0:1, :]
        o_ref[2:3, Q_RANK:Q_RANK + KV_RANK] = p_ref[1:2, 0:KV_RANK]
        o_ref[2:3, Q_RANK + KV_RANK:Q_RANK + KV_RANK + d] = f_ref[0:1, :]
        for r in range(3):
            o_ref[3 + r:4 + r, 0:CONV_W] = cw_ref[r:r + 1, :]
        o_ref[6:7, 0:d] = f_ref[3:4, :]

    return pl.pallas_call(body, name=name, out_shape=jax.ShapeDtypeStruct((8, 6 * d), F32))(
        sums1, sums2, fsums, sums1c, psums, d_cw)


def _adam_math(w, g, m, v):
    nm = ADAM_B1 * m + (1.0 - ADAM_B1) * g
    nv = ADAM_B2 * v + (1.0 - ADAM_B2) * (g * g)
    m_hat = nm / (1.0 - ADAM_B1 ** ADAM_STEP)
    v_hat = nv / (1.0 - ADAM_B2 ** ADAM_STEP)
    return -ADAM_LR * (m_hat / (jnp.sqrt(v_hat) + ADAM_EPS) + ADAM_WD * w), nm, nv


def _small_update(dsum, dsil_all, g_cw, params, name):
    d = D_MODEL
    n = len(params)

    def body(*refs):
        dsum_ref, dsil_ref, gcw_ref = refs[:3]
        wmv = refs[3:3 + 3 * n]
        outs = refs[3 + 3 * n:]
        tot = dsil_ref[0]
        for j in range(1, N_DEV):
            tot = tot + dsil_ref[j]
        cv = wmv[0][...]
        sg = 1.0 / (1.0 + jnp.exp(-cv))
        off = Q_RANK + KV_RANK
        grads = [tot[0:1, :] * (sg * (1.0 + cv * (1.0 - sg))),
                 dsum_ref[0:1, :] + dsum_ref[1:2, :],
                 dsum_ref[2:3, 0:Q_RANK], dsum_ref[2:3, Q_RANK:off], dsum_ref[2:3, off:off + d],
                 gcw_ref[...]]
        for p, g in enumerate(grads):
            w_ref, m_ref, v_ref = wmv[3 * p:3 * p + 3]
            at = 0 if len(w_ref.shape) == 3 else Ellipsis
            res = (g,) + _adam_math(w_ref[at], g, m_ref[at], v_ref[at])
            for q, val in enumerate(res):
                outs[4 * p + q][at] = val

    flat = [a for wmv in params for a in wmv]
    out_shape = tuple(jax.ShapeDtypeStruct(wmv[0].shape, F32) for wmv in params for _ in range(4))
    outs = pl.pallas_call(body, name=name, out_shape=out_shape)(dsum, dsil_all, g_cw, *flat)
    return [outs[4 * p:4 * p + 4] for p in range(n)]


def _adamw(w, g, m, v, name, slots=False):
    _, rows, cols = w.shape
    tr = _pick(rows, (256, 128, 64, 32, 16, 8))

    def body(w_ref, g_ref, m_ref, v_ref, *outs):
        if slots:
            gv = g_ref[0].astype(F32)
            for j in range(1, g.shape[0]):
                gv = gv + g_ref[j].astype(F32)
            outs[0][...] = gv
        else:
            gv = g_ref[...]
        d_ref, nm_ref, nv_ref = outs[-3:]
        d_ref[...], nm_ref[...], nv_ref[...] = _adam_math(w_ref[...], gv, m_ref[...], v_ref[...])

    blk = pl.BlockSpec((None, tr, cols), lambda i: (0, i, 0))
    g_spec = (pl.BlockSpec((g.shape[0], tr, cols), lambda i: (0, i, 0)) if slots
              else pl.BlockSpec((tr, cols), lambda i: (i, 0)))
    sh = jax.ShapeDtypeStruct((1, rows, cols), F32)
    n_out = 4 if slots else 3
    return pl.pallas_call(
        body, name=name, grid=(rows // tr,), out_shape=(sh,) * n_out,
        in_specs=[blk, g_spec, blk, blk], out_specs=(blk,) * n_out,
        compiler_params=_params(("parallel",), VMEM_BIG),
    )(w, g, m, v)


def _rope_tables(s, l):
    tok = np.arange(s)
    row = (tok // GRID_W).astype(np.float32)
    col = (tok % GRID_W).astype(np.float32)
    half = QK_ROPE // 2
    freqs = np.float32(ROPE_THETA) ** (-np.arange(0, half, 2, dtype=np.float32) / np.float32(half))
    dd = np.arange(QK_ROPE)
    pos = np.where((dd // half)[None, :] == 0, row[:, None], col[:, None]).astype(np.float32)
    ang = (pos * freqs[dd % (half // 2)][None, :]).astype(np.float32)
    sin = np.sin(ang).astype(np.float32)
    cos_t = np.ones((s + l, LANES), np.float32)
    sgn_t = np.zeros((s + l, LANES), np.float32)
    cos_t[:s, QK_NOPE:QK_NOPE + QK_ROPE] = np.cos(ang)
    sgn_t[:s, QK_NOPE:QK_NOPE + QK_ROPE] = np.where(((dd % half) // (half // 2))[None, :] == 0, -sin, sin)
    return jnp.asarray(cos_t), jnp.asarray(sgn_t)


def _slots_to_cols(g):
    return g.transpose(1, 0, 2).reshape(g.shape[1], N_DEV * g.shape[2])


def _cols_to_slots(w):
    return w.reshape(w.shape[0], N_DEV, w.shape[1] // N_DEV).transpose(1, 0, 2)


def _unpack_small_weights(g_in_t, g_uq, g_ukv):
    w_t = g_in_t.reshape(N_DEV * g_in_t.shape[1], D_MODEL)
    zeros = jnp.zeros((QK_NOPE, D_MODEL), BF16)
    win_head_t = jnp.concatenate([w_t[:Q_RANK + KV_RANK], zeros, w_t[Q_RANK + KV_RANK:MLA_IN],
                                  zeros[:LANES - QK_NOPE - QK_ROPE]], axis=0)
    win_conv_t = w_t[MLA_IN:].reshape(3, CONV_W // LANES, LANES, D_MODEL).transpose(1, 0, 2, 3)
    win_conv_t = win_conv_t.reshape(3 * CONV_W, D_MODEL)
    w_uq = _slots_to_cols(g_uq).reshape(Q_RANK, N_HEADS, QK_NOPE + QK_ROPE)
    wq = jnp.pad(w_uq, ((0, 0), (0, 0), (0, LANES - QK_NOPE - QK_ROPE))).reshape(Q_RANK, N_HEADS * LANES)
    w_ukv = _slots_to_cols(g_ukv).reshape(KV_RANK, N_HEADS, QK_NOPE + V_DIM)
    k_top = jnp.pad(w_ukv[:, :, :QK_NOPE], ((0, 0), (0, 0), (0, LANES - QK_NOPE))).reshape(KV_RANK, N_HEADS * LANES)
    v_top = w_ukv[:, :, QK_NOPE:].reshape(KV_RANK, N_HEADS * V_DIM)
    eye = jnp.pad(jnp.eye(QK_ROPE, dtype=BF16), ((QK_NOPE, LANES - QK_NOPE - QK_ROPE),) * 2)
    wk = jnp.concatenate([
        jnp.concatenate([k_top, v_top], axis=1),
        jnp.concatenate([jnp.tile(eye, (1, N_HEADS)), jnp.zeros((LANES, N_HEADS * V_DIM), BF16)], axis=1)], axis=0)
    return win_head_t, win_conv_t, wq, wk


def _pack_small_grads(d_head_t, d_conv_t, d_wq, d_wkk, d_wkv):
    d_conv_t = d_conv_t.reshape(CONV_W // LANES, 3, LANES, D_MODEL).transpose(1, 0, 2, 3).reshape(3 * CONV_W, D_MODEL)
    rope0 = Q_RANK + KV_RANK + QK_NOPE
    g_in_t = jnp.concatenate([d_head_t[:Q_RANK + KV_RANK], d_head_t[rope0:rope0 + QK_ROPE], d_conv_t], axis=0)
    g_in_t = g_in_t.reshape(N_DEV, -1, D_MODEL).astype(BF16)
    g_uq = d_wq.reshape(Q_RANK, N_HEADS, LANES)[:, :, :QK_NOPE + QK_ROPE].reshape(Q_RANK, -1)
    g_kn = d_wkk[:KV_RANK].reshape(KV_RANK, N_HEADS, LANES)[:, :, :QK_NOPE]
    g_v = d_wkv[:KV_RANK].reshape(KV_RANK, N_HEADS, V_DIM)
    g_ukv = jnp.concatenate([g_kn, g_v], axis=2).reshape(KV_RANK, -1)
    return [g_in_t] + [_cols_to_slots(g).astype(BF16) for g in (g_uq, g_ukv)]


def kernel(x, c, ctx, c_ctx, w_mod, b_mod, w_in, q_norm_g, w_uq, kv_norm_g, w_ukv, conv_w, w_out, w_mlp1, w_mlp2, final_norm_g, loss_target, m_c_ctx, m_w_mod, m_b_mod, m_w_in, m_q_norm_g, m_w_uq, m_kv_norm_g, m_w_ukv, m_conv_w, m_w_out, m_w_mlp1, m_w_mlp2, m_final_norm_g, v_c_ctx, v_w_mod, v_b_mod, v_w_in, v_q_norm_g, v_w_uq, v_kv_norm_g, v_w_ukv, v_conv_w, v_w_out, v_w_mlp1, v_w_mlp2, v_final_norm_g):
    me = _my_index()
    x2d, ctx2d, tgt = x[0], ctx[0], loss_target[0]
    s, l = x2d.shape[0], ctx2d.shape[0]
    t = s + l
    d = D_MODEL
    mod_cols = w_mod.shape[2]
    cw_cols = conv_w.shape[2]

    early = [w.astype(BF16) for w in (w_in[0].T, w_uq[0], w_ukv[0])]
    late = [w.astype(BF16) for w in (w_out[0], w_mlp1[0], w_mlp2[0])]
    b_cols = lax.dynamic_slice(b_mod, (0, me * mod_cols), (1, mod_cols))
    cw_blk = jnp.pad(conv_w[0], ((0, 5), (0, mod_cols - cw_cols)))
    a_rows, gathered, (g_in, g_uq, g_ukv) = _prologue(jnp.pad(c, ((0, 7), (0, 0))), c_ctx[None, :], w_mod[0], b_cols,
                                                      cw_blk, early, "prologue")
    mod_mine = lax.dynamic_index_in_dim(gathered, me, axis=1, keepdims=False).reshape(1, 6 * d)
    mod_ctx = gathered[:, 8, :].reshape(1, 6 * d)
    cw_full = gathered[:, 16:19, :cw_cols].transpose(1, 0, 2).reshape(3, CONV_W)

    win_head, win_conv, wq, wk = _unpack_small_weights(g_in, g_uq, g_ukv)
    wk_k, wk_v = wk[:, :N_HEADS * LANES], wk[:, N_HEADS * LANES:]
    cos, sgn = _rope_tables(s, l)

    h_all = _modulate_all(x2d, ctx2d, mod_mine, mod_ctx, "modulate1")
    tm_t = _pick(t, (1088, 768, 256))
    tk_t = _pick(t, (2176, 768, 256))
    z_head, cq, kv_in, qf, kv = _head_fwd(h_all, win_head, wq, wk, q_norm_g, kv_norm_g, cos, sgn, tm_t, "head_fwd")
    z_conv = _matmul(h_all, win_conv, mode="nt", name="in_proj_conv", m=s, tm=1024, tn=1536, tk=1024)
    attn, a_cat, stats, (g_out, w1, g_w2) = _attn_fwd(qf, kv, s, _Riding("gather", late), "attn_fwd")
    wo = g_out.reshape(d, d)
    w2 = g_w2.reshape(D_FF, d)
    a_cat = _conv_fwd(z_conv, cw_full, a_cat, "conv_fwd")
    (o, x1, h2), _ = _matmul_rows(a_cat, wo, _epi_resid_modulate, mode="nn", name="out_proj", tm=1024, tk=1024,
                                  rows=[x2d], vecs=[(mod_mine, 2), (mod_mine, 3), (mod_mine, 4)],
                                  out_dtypes=[F32, F32, BF16])
    u1, act = _matmul(h2, w1, mode="nn", name="mlp_up", tm=4096, tk=1024, epilogue="relu2", slots="b_cols")
    (dx2, dm, fsums), _ = _matmul_rows(act, w2, _epi_final, mode="nn", name="mlp_down", tm=512, tk=4096,
                                       rows=[x1, tgt], vecs=[(mod_mine, 5), (final_norm_g[None, :], 0)],
                                       out_dtypes=[F32, BF16], sums=True)

    d_w2 = _matmul(act, dm, mode="tn", name="d_w_mlp2", out_dtype=BF16, tm=1024, tn=1024, tk=4096)
    du1 = _matmul(dm, w2, mode="nt", name="d_act", out_dtype=BF16, tm=2048, tn=1024, tk=1024,
                  epilogue="drelu2", extra=(u1,))
    d_w1 = _matmul(h2, du1, mode="tn", name="d_w_mlp1", out_dtype=BF16, tm=1024, tk=4096, slots="out")
    (dx1, do, sums2), _ = _matmul_rows(du1, w1, _epi_modulate2_bwd, mode="nt", name="d_h2", tm=512, tk=4096,
                                       slots="b_contract", rows=[x1, dx2, o], vecs=[(mod_mine, 4), (mod_mine, 2)],
                                       out_dtypes=[F32, BF16], sums=True)
    d_wo = _matmul(a_cat, do, mode="tn", name="d_w_out", out_dtype=BF16, tm=1024, tn=1024, tk=2048)
    da = _matmul(do, wo, mode="nt", name="d_a", tm=1024, tn=1024, tk=1024)
    dz_conv, d_cw = _conv_bwd(z_conv, cw_full, da, "conv_bwd")
    ready = [d_wo.reshape(N_DEV, d // N_DEV, d), d_w1, d_w2.reshape(N_DEV, D_FF // N_DEV, d)]
    dq, dk, dv, rode = _attn_bwd(qf, kv, attn, da, stats, cos, sgn, _Riding("exchange", ready), "attn_bwd")
    d_wq = _matmul(cq, dq, mode="tn", name="d_w_uq", k=s, tm=256, tn=1024, tk=4096)
    d_wkk = _matmul(kv_in, dk, mode="tn", name="d_w_ukv_k", tm=256, tn=1024, tk=tk_t)
    d_wkv = _matmul(kv_in, dv, mode="tn", name="d_w_ukv_v", tm=256, tn=512, tk=tk_t)
    dz_head, dh_head, psums = _head_bwd(dq, dk, dv, z_head, wq, wk_k, wk_v, win_head, q_norm_g, kv_norm_g, cos, sgn, s,
                                        "head_bwd")
    d_head = _matmul(dz_head, h_all, mode="tn", name="d_w_in_head", tm=512, tn=1024, tk=tk_t)
    d_conv = _matmul(dz_conv, h_all, mode="tn", name="d_w_in_conv", k=s, tm=1536, tn=1024, tk=2048)
    send = _pack_small_grads(d_head, d_conv, d_wq, d_wkk, d_wkv)
    (grad_x, sums1), got = _matmul_rows(dz_conv, win_conv, _epi_modulate1_bwd, mode="nn", name="d_h1", tm=s // 4,
                                        tk=win_conv.shape[0], rows=[dh_head, x2d, dx1], vecs=[(mod_mine, 1)],
                                        out_dtypes=[F32], sums=True, riding=_RidingReduce(send))
    sums1c = _modulate_sums(dh_head, s // ROW_TILE, ctx2d)

    small = _pack_small(sums1, sums2, fsums, sums1c, psums, d_cw, "pack_small")
    (d_all,) = _all_gather([small], "gather_small_grads", True)
    d_cols = lax.dynamic_slice_in_dim(d_all, me * mod_cols, mod_cols, axis=2)
    g_w_mod, dsil, dsum = _adaln_bwd(a_rows.T, w_mod[0], d_cols[:, 0, :], d_cols[:, 1, :], d_all, "adaln_bwd")
    (dsil_all,) = _all_gather([dsil], "gather_d_cctx", True)
    loss = dsum[6, 0]
    g_cw = lax.dynamic_slice(dsum, (3, me * cw_cols), (3, cw_cols))

    slots = dict(zip(["w_in", "w_uq", "w_ukv"], got))
    slots.update(zip(["w_out", "w_mlp1", "w_mlp2"], rode))

    grads = {}
    weights = {"c_ctx": c_ctx, "w_mod": w_mod, "b_mod": b_mod, "w_in": w_in, "q_norm_g": q_norm_g, "w_uq": w_uq,
               "kv_norm_g": kv_norm_g, "w_ukv": w_ukv, "conv_w": conv_w, "w_out": w_out, "w_mlp1": w_mlp1,
               "w_mlp2": w_mlp2, "final_norm_g": final_norm_g}
    m_in = {"c_ctx": m_c_ctx, "w_mod": m_w_mod, "b_mod": m_b_mod, "w_in": m_w_in, "q_norm_g": m_q_norm_g,
            "w_uq": m_w_uq, "kv_norm_g": m_kv_norm_g, "w_ukv": m_w_ukv, "conv_w": m_conv_w, "w_out": m_w_out,
            "w_mlp1": m_w_mlp1, "w_mlp2": m_w_mlp2, "final_norm_g": m_final_norm_g}
    v_in = {"c_ctx": v_c_ctx, "w_mod": v_w_mod, "b_mod": v_b_mod, "w_in": v_w_in, "q_norm_g": v_q_norm_g,
            "w_uq": v_w_uq, "kv_norm_g": v_kv_norm_g, "w_ukv": v_w_ukv, "conv_w": v_conv_w, "w_out": v_w_out,
            "w_mlp1": v_w_mlp1, "w_mlp2": v_w_mlp2, "final_norm_g": v_final_norm_g}
    names = list(weights)
    small_names = ["c_ctx", "b_mod", "q_norm_g", "kv_norm_g", "final_norm_g", "conv_w"]
    delta, new_m, new_v = {}, {}, {}

    def as_rows(a):
        return a[None, :] if a.ndim == 1 else a

    small_out = _small_update(dsum, dsil_all, g_cw, [[as_rows(src[n]) for src in (weights, m_in, v_in)]
                                                      for n in small_names], "small_update")
    for n, outs in zip(small_names, small_out):
        grads[n], delta[n], new_m[n], new_v[n] = [a.reshape(weights[n].shape) for a in outs]
    for n in names:
        if n in small_names:
            continue
        if n == "w_in":
            wmv = [jnp.swapaxes(src[n], 1, 2) for src in (weights, m_in, v_in)]
            outs = _adamw(wmv[0], slots[n], wmv[1], wmv[2], "adamw_" + n, slots=True)
            grads[n], delta[n], new_m[n], new_v[n] = [jnp.swapaxes(a, 1, 2) for a in outs]
        elif n in slots:
            grads[n], delta[n], new_m[n], new_v[n] = _adamw(weights[n], slots[n], m_in[n], v_in[n], "adamw_" + n,
                                                            slots=True)
        else:
            delta[n], new_m[n], new_v[n] = _adamw(weights[n], g_w_mod, m_in[n], v_in[n], "adamw_" + n)
            grads[n] = g_w_mod[None]

    return (loss, grad_x[None], *[grads[n] for n in names], *[delta[n] for n in names],
            *[new_m[n] for n in names], *[new_v[n] for n in names])
```

```python
import math

import jax
import jax.numpy as jnp
import numpy as np
from jax import lax
from jax.experimental import pallas as pl
from jax.experimental.pallas import tpu as pltpu

F32 = jnp.float32
BF16 = jnp.bfloat16

D_MODEL = 1024
GRID_W = 64
N_HEADS = 8
QK_NOPE = 64
QK_ROPE = 32
V_DIM = 64
Q_RANK = 256
KV_RANK = 128
MLA_IN = Q_RANK + KV_RANK + QK_ROPE
CONV_W = 512
HEAD_COLS = 512
D_FF = 4096
ROPE_THETA = 10000.0
EPS = 1e-6
ATTN_SCALE = 1.0 / math.sqrt(QK_NOPE + QK_ROPE)
LOG2_E = 1.0 / math.log(2.0)
EXP2_SCALE = ATTN_SCALE * LOG2_E
N_DEV = 8
LANES = 128

ADAM_LR, ADAM_B1, ADAM_B2, ADAM_EPS, ADAM_WD, ADAM_STEP = 0.001, 0.9, 0.999, 1e-08, 0.01, 10

ROW_TILE = 256
VMEM_BIG = 60 * 1024 * 1024


def _params(sem=None, vmem=None):
    return pltpu.CompilerParams(dimension_semantics=sem, vmem_limit_bytes=vmem)


def _pick(n, prefs):
    for p in prefs:
        if n % p == 0:
            return p
    return n


def _my_index():
    return 4 * lax.axis_index("x") + 2 * lax.axis_index("y") + lax.axis_index("c")


def _two_level_gather(x_refs, out_refs, send_sems, recv_sems, local_sems):
    n = len(x_refs)
    x, y, c = lax.axis_index("x"), lax.axis_index("y"), lax.axis_index("c")
    me, sibling = (x, y, c), (x, y, 1 - c)
    chips = [(1 - x, y), (x, 1 - y), (1 - x, 1 - y)]

    def slot(a, px, py, pc):
        return out_refs[a].at[4 * px + 2 * py + pc]

    def copy(a, k, block, to, src=None):
        return pltpu.make_async_remote_copy(
            src_ref=slot(a, *block) if src is None else src, dst_ref=slot(a, *block),
            send_sem=send_sems.at[7 * a + k], recv_sem=recv_sems.at[7 * a + k],
            device_id=to, device_id_type=pl.DeviceIdType.MESH)

    mine = [pltpu.make_async_copy(x_refs[a], slot(a, *me), local_sems.at[a]) for a in range(n)]
    first = [cp for a in range(n) for cp in
             [copy(a, 0, me, sibling, src=x_refs[a])]
             + [copy(a, 1 + j, me, (*chip, c), src=x_refs[a]) for j, chip in enumerate(chips)]]
    passed = [[copy(a, 4 + j, (*chip, c), sibling) for j, chip in enumerate(chips)] for a in range(n)]

    def start():
        for cp in mine + first:
            cp.start()

    def forward():
        for a in range(n):
            for j, chip in enumerate(chips):
                copy(a, 1 + j, (*chip, c), me).wait_recv()
                passed[a][j].start()

    def finish():
        for a in range(n):
            copy(a, 0, sibling, me).wait_recv()
            for j, chip in enumerate(chips):
                copy(a, 4 + j, (*chip, 1 - c), me).wait_recv()
        for cp in first + [cp for per_array in passed for cp in per_array]:
            cp.wait_send()
        for cp in mine:
            cp.wait()

    return start, forward, finish


def _direct_gather(src_ref, dst_ref, send_sems, recv_sems):
    x, y, c = lax.axis_index("x"), lax.axis_index("y"), lax.axis_index("c")
    me = 4 * x + 2 * y + c
    dst_ref[me] = src_ref[...]
    sends, landings = [], []
    for k in range(1, N_DEV):
        peer = (1 - x if k & 4 else x, 1 - y if k & 2 else y, 1 - c if k & 1 else c)
        pid = 4 * peer[0] + 2 * peer[1] + peer[2]
        for dst, out in ((me, sends), (pid, landings)):
            out.append(pltpu.make_async_remote_copy(
                src_ref=src_ref, dst_ref=dst_ref.at[dst], send_sem=send_sems.at[k - 1], recv_sem=recv_sems.at[k - 1],
                device_id=peer, device_id_type=pl.DeviceIdType.MESH))
    for cp in sends:
        cp.start()

    def finish():
        for cp in landings:
            cp.wait_recv()
        for cp in sends:
            cp.wait_send()

    return finish


def _all_gather(arrays, name, in_vmem):
    space = pltpu.VMEM if in_vmem else pl.ANY
    n = len(arrays)

    def body(*refs):
        for phase in _two_level_gather(refs[:n], refs[n:2 * n], *refs[2 * n:]):
            phase()

    outs = pl.pallas_call(
        body, name=name,
        out_shape=tuple(jax.ShapeDtypeStruct((N_DEV,) + a.shape, a.dtype) for a in arrays),
        in_specs=[pl.BlockSpec(memory_space=space)] * n,
        out_specs=tuple(pl.BlockSpec(memory_space=space) for _ in arrays),
        scratch_shapes=[pltpu.SemaphoreType.DMA((7 * n,)), pltpu.SemaphoreType.DMA((7 * n,)),
                        pltpu.SemaphoreType.DMA((n,))],
    )(*arrays)
    return list(outs)


class _Riding:
    def __init__(self, kind, arrays):
        self.kind, self.arrays, self.n = kind, list(arrays), len(arrays)
        lead = (N_DEV,) if kind == "gather" else ()
        self.out_shape = [jax.ShapeDtypeStruct(lead + a.shape, a.dtype) for a in self.arrays]
        self.specs = [pl.BlockSpec(memory_space=pl.ANY)] * self.n
        self.scratch = [pltpu.SemaphoreType.DMA((7 * self.n,)), pltpu.SemaphoreType.DMA((7 * self.n,)),
                        pltpu.SemaphoreType.DMA((self.n,))]

    def copies(self, x_refs, y_refs, send_sems, recv_sems, local_sems):
        x, y, c = lax.axis_index("x"), lax.axis_index("y"), lax.axis_index("c")
        me = 4 * x + 2 * y + c
        local, sends, landings = [], [], []
        for a in range(self.n):
            src_mine = x_refs[a] if self.kind == "gather" else x_refs[a].at[me]
            local.append(pltpu.make_async_copy(src_mine, y_refs[a].at[me], local_sems.at[a]))
            for k in range(1, N_DEV):
                peer = (1 - x if k & 4 else x, 1 - y if k & 2 else y, 1 - c if k & 1 else c)
                pid = 4 * peer[0] + 2 * peer[1] + peer[2]
                src = x_refs[a] if self.kind == "gather" else x_refs[a].at[pid]
                for dst, out in ((me, sends), (pid, landings)):
                    out.append(pltpu.make_async_remote_copy(
                        src_ref=src, dst_ref=y_refs[a].at[dst],
                        send_sem=send_sems.at[7 * a + k - 1], recv_sem=recv_sems.at[7 * a + k - 1],
                        device_id=peer, device_id_type=pl.DeviceIdType.MESH))
        return local, sends, landings

    def run(self, first, last, x_refs, y_refs, sems, middle=None):
        if self.n == 0:
            return None
        local, sends, landings = self.copies(x_refs, y_refs, *sems)

        @pl.when(first)
        def _():
            for cp in local + sends:
                cp.start()

        return local, sends, landings, last

    @staticmethod
    def finish(state):
        if state is None:
            return
        local, sends, landings, last = state

        @pl.when(last)
        def _():
            for cp in landings:
                cp.wait_recv()
            for cp in sends:
                cp.wait_send()
            for cp in local:
                cp.wait()


class _RidingGather:
    def __init__(self, arrays):
        self.arrays, self.n = list(arrays), len(arrays)
        self.out_shape = [jax.ShapeDtypeStruct((N_DEV,) + a.shape, a.dtype) for a in self.arrays]
        self.specs = [pl.BlockSpec(memory_space=pl.ANY)] * self.n
        self.scratch = [pltpu.SemaphoreType.DMA((7 * self.n,)), pltpu.SemaphoreType.DMA((7 * self.n,)),
                        pltpu.SemaphoreType.DMA((self.n,))]

    def run(self, first, last, x_refs, y_refs, sems, middle):
        start, forward, finish = _two_level_gather(x_refs, y_refs, *sems)
        pl.when(first)(start)
        pl.when(middle)(forward)
        return finish, last

    @staticmethod
    def finish(state):
        finish, last = state
        pl.when(last)(finish)


class _RidingReduce:
    def __init__(self, arrays):
        self.arrays, self.n = list(arrays), len(arrays)
        self.out_shape = [jax.ShapeDtypeStruct((4,) + a.shape[1:], a.dtype) for a in self.arrays]
        self.specs = [pl.BlockSpec(memory_space=pl.ANY)] * self.n
        self.scratch = [pltpu.VMEM((4,) + a.shape[1:], a.dtype) for a in self.arrays for _ in range(3)]
        self.scratch += [pltpu.SemaphoreType.DMA((self.n,)) for _ in range(6)]

    def run(self, first, last, x_refs, y_refs, scratch, middle):
        n = self.n
        own, sib, tot = scratch[0:3 * n:3], scratch[1:3 * n:3], scratch[2:3 * n:3]
        d2d_send, d2d_recv, local_in, ici_send, ici_recv, local_out = scratch[3 * n:]
        x, y, c = lax.axis_index("x"), lax.axis_index("y"), lax.axis_index("c")
        my_chip = 2 * x + y
        sibling = (x, y, 1 - c)
        others = [(1 - x, y), (x, 1 - y), (1 - x, 1 - y)]

        def to_sibling(a, j=None):
            src = x_refs[a].at[pl.ds(0, 4)] if j is None else x_refs[a].at[2 * j + 1 - c]
            dst = sib[a] if j is None else sib[a].at[j]
            return pltpu.make_async_remote_copy(src_ref=src, dst_ref=dst, send_sem=d2d_send.at[a],
                                                recv_sem=d2d_recv.at[a], device_id=sibling,
                                                device_id_type=pl.DeviceIdType.MESH)

        def mine_in(a, j=None):
            src = x_refs[a].at[pl.ds(0, 4)] if j is None else x_refs[a].at[2 * j + c]
            return pltpu.make_async_copy(src, own[a] if j is None else own[a].at[j], local_in.at[a])

        def to_chip(a, chip=None):
            if chip is None:
                src, dst, peer = tot[a].at[pl.ds(0, 3)], y_refs[a].at[pl.ds(0, 3)], sibling
            else:
                src, dst, peer = tot[a].at[2 * chip[0] + chip[1]], y_refs[a].at[my_chip], (*chip, c)
            return pltpu.make_async_remote_copy(src_ref=src, dst_ref=dst, send_sem=ici_send.at[a],
                                                recv_sem=ici_recv.at[a], device_id=peer,
                                                device_id_type=pl.DeviceIdType.MESH)

        def mine_out(a):
            return pltpu.make_async_copy(tot[a].at[my_chip], y_refs[a].at[my_chip], local_out.at[a])

        @pl.when(first)
        def _():
            for a in range(n):
                for j in range(4):
                    to_sibling(a, j).start()
                    mine_in(a, j).start()

        @pl.when(middle)
        def _():
            for a in range(n):
                to_sibling(a).wait_recv()
                to_sibling(a).wait_send()
                mine_in(a).wait()
                tot[a][...] = (own[a][...].astype(F32) + sib[a][...].astype(F32)).astype(tot[a].dtype)
                for chip in others:
                    to_chip(a, chip).start()
                mine_out(a).start()

        def finish():
            @pl.when(last)
            def _():
                for a in range(n):
                    to_chip(a).wait_recv()
                    to_chip(a).wait_send()
                    mine_out(a).wait()

        return finish

    @staticmethod
    def finish(state):
        state()


_DIMS ={"nn": (((1,), (0,)), ((), ())), "nt": (((1,), (1,)), ((), ())), "tn": (((0,), (0,)), ((), ()))}
NT_DIMS = _DIMS["nt"]
TN_DIMS = _DIMS["tn"]


def _swap8(x):
    lane = lax.broadcasted_iota(jnp.int32, x.shape, 1)
    return jnp.where((lane & 15) < 8, pltpu.roll(x, LANES - 8, 1), pltpu.roll(x, 8, 1))


def _rope(x, cos, sgn, bwd):
    return x * cos + (_swap8(x * sgn) if bwd else _swap8(x) * sgn)


def _matmul(a, b, *, mode, name, out_dtype=F32, tm=512, tn=512, tk=512, m=None, k=None,
            epilogue=None, extra=(), slots=None):
    if mode == "nn":
        m = a.shape[0] if m is None else m
        k = a.shape[1]
        n = N_DEV * b.shape[2] if slots == "b_cols" else b.shape[1]
    elif mode == "nt":
        m = a.shape[0] if m is None else m
        k = a.shape[1]
        n = b.shape[0]
    else:
        k = a.shape[0] if k is None else k
        m, n = a.shape[1], b.shape[1]
    tm, tn, tk = min(tm, m), min(tn, n), min(tk, k)
    if slots == "b_cols":
        tn = b.shape[2]
    if slots == "out":
        tn = n // N_DEV
    assert m % tm == 0 and n % tn == 0 and k % tk == 0, (name, m, n, k, tm, tn, tk)
    nk = k // tk
    dims = _DIMS[mode]
    a_spec = (pl.BlockSpec((tk, tm), lambda i, j, kk: (kk, i)) if mode == "tn"
              else pl.BlockSpec((tm, tk), lambda i, j, kk: (i, kk)))
    if slots == "b_cols":
        b_spec = pl.BlockSpec((None, tk, tn), lambda i, j, kk: (j, kk, 0))
    elif mode == "nt":
        b_spec = pl.BlockSpec((tn, tk), lambda i, j, kk: (j, kk))
    else:
        b_spec = pl.BlockSpec((tk, tn), lambda i, j, kk: (kk, j))
    tile = pl.BlockSpec((tm, tn), lambda i, j, kk: (i, j))
    if slots == "out":
        o_spec = pl.BlockSpec((None, tm, tn), lambda i, j, kk: (j, i, 0))
        o_shape = (N_DEV, m, tn)
    else:
        o_spec, o_shape = tile, (m, n)
    in_specs, args = [a_spec, b_spec], [a, b]
    if epilogue == "drelu2":
        in_specs.append(tile)
    args += list(extra)
    if epilogue == "relu2":
        out_shape = (jax.ShapeDtypeStruct(o_shape, BF16), jax.ShapeDtypeStruct(o_shape, BF16))
        out_specs = (o_spec, o_spec)
    else:
        out_shape = jax.ShapeDtypeStruct(o_shape, out_dtype)
        out_specs = o_spec
    n_in = len(args)
    n_out = 2 if epilogue == "relu2" else 1

    def body(*refs):
        a_ref, b_ref = refs[0], refs[1]
        outs = refs[n_in:n_in + n_out]
        part = lax.dot_general(a_ref[...], b_ref[...], dims, preferred_element_type=F32)

        def finish(acc):
            if epilogue == "relu2":
                outs[0][...] = acc.astype(BF16)
                r = jnp.maximum(acc, 0.0)
                outs[1][...] = (r * r).astype(BF16)
            elif epilogue == "drelu2":
                u = refs[2][...].astype(F32)
                outs[0][...] = (acc * (2.0 * jnp.maximum(u, 0.0))).astype(out_dtype)
            else:
                outs[0][...] = acc.astype(out_dtype)

        if nk == 1:
            finish(part)
        else:
            acc_ref = refs[n_in + n_out]
            kk = pl.program_id(2)

            @pl.when(kk == 0)
            def _():
                acc_ref[...] = part

            @pl.when(kk > 0)
            def _():
                acc_ref[...] += part

            @pl.when(kk == nk - 1)
            def _():
                finish(acc_ref[...])

    return pl.pallas_call(
        body, name=name, grid=(m // tm, n // tn, nk),
        out_shape=out_shape, in_specs=in_specs, out_specs=out_specs,
        scratch_shapes=[pltpu.VMEM((tm, tn), F32)] if nk > 1 else [],
        compiler_params=_params(("parallel", "parallel", "arbitrary"), VMEM_BIG),
    )(*args)


def _rstd(x):
    return lax.rsqrt(jnp.mean(x * x, axis=1, keepdims=True) + EPS)


def _norm_bwd(dxn, xn, r):
    return r * (dxn - xn * jnp.mean(dxn * xn, axis=1, keepdims=True))


def _vec(col):
    return pl.BlockSpec((1, D_MODEL), lambda i: (0, col))


def _matmul_rows(a, b, epi, *, mode, name, tm, tk, rows=(), vecs=(), out_dtypes=(), sums=False, slots=None,
                 riding=None):
    m, k = a.shape
    n = D_MODEL
    tm, tk = min(tm, m), min(tk, k)
    riding = riding or _Riding("gather", [])
    group = 1
    if slots == "b_contract":
        group = max(1, tk // b.shape[2])
        tk = group * b.shape[2]
        b_spec = pl.BlockSpec((group, n, tk // group), lambda i, kk: (kk, 0, 0))
    elif mode == "nt":
        b_spec = pl.BlockSpec((n, tk), lambda i, kk: (0, kk))
    else:
        b_spec = pl.BlockSpec((tk, n), lambda i, kk: (kk, 0))
    assert m % tm == 0 and k % tk == 0, (name, m, k, tm, tk)
    ni, nk = m // tm, k // tk
    assert ni >= 2 or not isinstance(riding, _RidingReduce), "the two-level exchange needs a middle grid step"
    dims = _DIMS[mode]
    tile = pl.BlockSpec((tm, n), lambda i, kk: (i, 0))
    in_specs = [pl.BlockSpec((tm, tk), lambda i, kk: (i, kk)), b_spec] + [tile] * len(rows)
    in_specs += [pl.BlockSpec((1, n), lambda i, kk, col=col: (0, col)) for _, col in vecs]
    args = [a, b, *rows, *[v for v, _ in vecs]]
    out_shape = [jax.ShapeDtypeStruct((m, n), dt) for dt in out_dtypes]
    out_specs = [tile] * len(out_dtypes)
    if sums:
        out_shape.append(jax.ShapeDtypeStruct((8, n), F32))
        out_specs.append(pl.BlockSpec((8, n), lambda i, kk: (0, 0)))
    n_rows, n_vecs, n_outs, nr = len(rows), len(vecs), len(out_dtypes), riding.n
    n_in = 2 + n_rows + n_vecs

    def body(*refs):
        a_ref, b_ref = refs[0], refs[1]
        row_refs = refs[2:2 + n_rows]
        vec_refs = refs[2 + n_rows:n_in]
        x_refs = refs[n_in:n_in + nr]
        out_refs = refs[n_in + nr:n_in + nr + n_outs]
        pos = n_in + nr + n_outs
        sums_ref = refs[pos] if sums else None
        pos += 1 if sums else 0
        y_refs = refs[pos:pos + nr]
        pos += nr
        acc_ref = refs[pos] if nk > 1 else None
        sem_refs = refs[pos + (1 if nk > 1 else 0):]
        i, kk = pl.program_id(0), pl.program_id(1)
        state = riding.run((i == 0) & (kk == 0), (i == ni - 1) & (kk == nk - 1), x_refs, y_refs, sem_refs,
                           middle=(i == 1) & (kk == 0))
        if slots == "b_contract":
            c = tk // group
            part = lax.dot_general(a_ref[:, 0:c], b_ref[0], dims, preferred_element_type=F32)
            for u in range(1, group):
                part = part + lax.dot_general(a_ref[:, u * c:(u + 1) * c], b_ref[u], dims, preferred_element_type=F32)
        else:
            part = lax.dot_general(a_ref[...], b_ref[...], dims, preferred_element_type=F32)

        def finish(acc):
            nsub = tm // ROW_TILE
            for r in range(nsub):
                blk = pl.ds(r * ROW_TILE, ROW_TILE)
                epi(acc[r * ROW_TILE:(r + 1) * ROW_TILE], [ref.at[blk] for ref in row_refs], vec_refs,
                    [ref.at[blk] for ref in out_refs], sums_ref,
                    (i == 0) if r == 0 else None, (i == ni - 1) if r == nsub - 1 else None)

        if nk == 1:
            finish(part)
        else:
            @pl.when(kk == 0)
            def _():
                acc_ref[...] = part

            @pl.when(kk > 0)
            def _():
                acc_ref[...] += part

            @pl.when(kk == nk - 1)
            def _():
                finish(acc_ref)

        riding.finish(state)

    outs = pl.pallas_call(
        body, name=name, grid=(ni, nk),
        out_shape=(*out_shape, *riding.out_shape),
        in_specs=[*in_specs, *riding.specs], out_specs=(*out_specs, *riding.specs),
        scratch_shapes=([pltpu.VMEM((tm, n), F32)] if nk > 1 else []) + (riding.scratch if nr else []),
        compiler_params=_params(("arbitrary", "arbitrary"), VMEM_BIG),
    )(*args, *riding.arrays)
    n_own = len(out_shape)
    return list(outs[:n_own]), list(outs[n_own:])


def _zero_sums_at_start(sums_ref, first):
    if first is not None:
        @pl.when(first)
        def _():
            sums_ref[...] = jnp.zeros_like(sums_ref)


def _epi_resid_modulate(acc, rows, vecs, outs, sums_ref, first, last):
    (x_ref,), (g_ref, sh_ref, sc_ref) = rows, vecs
    x1 = x_ref[...] + g_ref[...] * acc
    outs[0][...] = acc
    outs[1][...] = x1
    outs[2][...] = (x1 * _rstd(x1) * (1.0 + sc_ref[...]) + sh_ref[...]).astype(BF16)


def _epi_final(acc, rows, vecs, outs, sums_ref, first, last):
    (x1_ref, t_ref), (g_ref, gf_ref) = rows, vecs
    d = acc.shape[1]
    x2 = x1_ref[...] + g_ref[...] * acc
    r = _rstd(x2)
    xn = x2 * r
    err = xn * gf_ref[...] - t_ref[...]
    dy = err * (1.0 / d)
    dx2 = _norm_bwd(dy * gf_ref[...], xn, r)
    outs[0][...] = dx2
    outs[1][...] = (dx2 * g_ref[...]).astype(BF16)
    _zero_sums_at_start(sums_ref, first)
    sums_ref[0:1, :] += jnp.sum(dy * xn, axis=0, keepdims=True)
    sums_ref[1:2, :] += jnp.sum(dx2 * acc, axis=0, keepdims=True)
    sums_ref[2:3, :] += jnp.sum(err * err, axis=0, keepdims=True)

    if last is not None:
        @pl.when(last)
        def _():
            tot = jnp.sum(sums_ref[2:3, :], axis=1, keepdims=True) * (0.5 / d)
            sums_ref[3:4, :] = jnp.broadcast_to(tot, (1, d))


def _epi_modulate2_bwd(acc, rows, vecs, outs, sums_ref, first, last):
    (x_ref, dres_ref, o_ref), (sc_ref, g_ref) = rows, vecs
    x = x_ref[...]
    r = _rstd(x)
    xn = x * r
    dx = dres_ref[...] + _norm_bwd(acc * (1.0 + sc_ref[...]), xn, r)
    outs[0][...] = dx
    outs[1][...] = (dx * g_ref[...]).astype(BF16)
    _zero_sums_at_start(sums_ref, first)
    sums_ref[0:1, :] += jnp.sum(acc * xn, axis=0, keepdims=True)
    sums_ref[1:2, :] += jnp.sum(acc, axis=0, keepdims=True)
    sums_ref[2:3, :] += jnp.sum(dx * o_ref[...], axis=0, keepdims=True)


def _epi_modulate1_bwd(acc, rows, vecs, outs, sums_ref, first, last):
    (add_ref, x_ref, dres_ref), (sc_ref,) = rows, vecs
    dh = acc + add_ref[...]
    x = x_ref[...]
    r = _rstd(x)
    xn = x * r
    outs[0][...] = dres_ref[...] + _norm_bwd(dh * (1.0 + sc_ref[...]), xn, r)
    _zero_sums_at_start(sums_ref, first)
    sums_ref[0:1, :] += jnp.sum(dh * xn, axis=0, keepdims=True)
    sums_ref[1:2, :] += jnp.sum(dh, axis=0, keepdims=True)


def _modulate_all(x, ctx, mod, mod_ctx, riding, name):
    s, d = x.shape
    t = s + ctx.shape[0]
    ns = s // ROW_TILE
    nc = ctx.shape[0] // ROW_TILE
    nr = riding.n

    def body(*refs):
        x_ref, c_ref, sh_ref, sc_ref, shc_ref, scc_ref = refs[:6]
        h_ref = refs[6 + nr]
        i = pl.program_id(0)
        state = riding.run(i == 0, i == ns + nc - 1, refs[6:6 + nr], refs[7 + nr:7 + 2 * nr], refs[7 + 2 * nr:],
                           middle=i == ns + nc - 3)

        @pl.when(i < ns)
        def _():
            v = x_ref[...]
            h_ref[...] = (v * _rstd(v) * (1.0 + sc_ref[...]) + sh_ref[...]).astype(BF16)

        @pl.when(i >= ns)
        def _():
            v = c_ref[...]
            h_ref[...] = (v * _rstd(v) * (1.0 + scc_ref[...]) + shc_ref[...]).astype(BF16)

        riding.finish(state)

    outs = pl.pallas_call(
        body, name=name, grid=(ns + nc,),
        out_shape=(jax.ShapeDtypeStruct((t, d), BF16), *riding.out_shape),
        in_specs=[pl.BlockSpec((ROW_TILE, d), lambda i: (jnp.minimum(i, ns - 1), 0)),
                  pl.BlockSpec((ROW_TILE, d), lambda i: (jnp.maximum(i - ns, 0), 0)),
                  _vec(0), _vec(1), _vec(0), _vec(1), *riding.specs],
        out_specs=(pl.BlockSpec((ROW_TILE, d), lambda i: (i, 0)), *riding.specs),
        scratch_shapes=riding.scratch,
        compiler_params=_params(("arbitrary",)),
    )(x, ctx, mod, mod, mod_ctx, mod_ctx, *riding.arrays)
    return outs[0], list(outs[1:])


def _modulate_sums(dh, row_off, xsrc):
    s, d = xsrc.shape

    def body(dh_ref, x_ref, sums_ref):
        i = pl.program_id(0)
        x = x_ref[...]
        dhv = dh_ref[...]

        @pl.when(i == 0)
        def _():
            sums_ref[...] = jnp.zeros_like(sums_ref)

        sums_ref[0:1, :] += jnp.sum(dhv * (x * _rstd(x)), axis=0, keepdims=True)
        sums_ref[1:2, :] += jnp.sum(dhv, axis=0, keepdims=True)

    return pl.pallas_call(
        body, name="modulate1_ctx_bwd", grid=(s // ROW_TILE,),
        out_shape=jax.ShapeDtypeStruct((8, d), F32),
        in_specs=[pl.BlockSpec((ROW_TILE, d), lambda i: (i + row_off, 0)), pl.BlockSpec((ROW_TILE, d), lambda i: (i, 0))],
        out_specs=pl.BlockSpec((8, d), lambda i: (0, 0)),
        compiler_params=_params(("arbitrary",)),
    )(dh, xsrc)


def _head_fwd(h_all, win_head, wq, wk, q_gain, kv_gain, cos, sgn, tm, name):
    t, d = h_all.shape
    nq, nkv = wq.shape[1], wk.shape[1]

    def body(h_ref, wi_ref, wq_ref, wk_ref, qg_ref, kg_ref, c_ref, s_ref, z_ref, cq_ref, kvin_ref, qf_ref, kv_ref):
        z = lax.dot_general(h_ref[...], wi_ref[...], NT_DIMS, preferred_element_type=F32)
        z_ref[...] = z
        cos, sgn = c_ref[...], s_ref[...]
        zq = z[:, 0:Q_RANK]
        cq = (zq * _rstd(zq) * qg_ref[...]).astype(BF16)
        cq_ref[...] = cq
        zk = z[:, Q_RANK:Q_RANK + KV_RANK]
        kv_in = jnp.concatenate([(zk * _rstd(zk) * kg_ref[...]).astype(BF16),
                                 _rope(z[:, Q_RANK + KV_RANK:HEAD_COLS], cos, sgn, False).astype(BF16)], axis=1)
        kvin_ref[...] = kv_in
        q = jnp.dot(cq, wq_ref[...], preferred_element_type=F32)
        for h in range(nq // LANES):
            sl = slice(h * LANES, (h + 1) * LANES)
            qf_ref[:, sl] = _rope(q[:, sl], cos, sgn, False).astype(BF16)
        kv_ref[...] = jnp.dot(kv_in, wk_ref[...], preferred_element_type=F32).astype(BF16)

    def row(w):
        return pl.BlockSpec((tm, w), lambda i: (i, 0))

    def whole(a):
        return pl.BlockSpec(a.shape, lambda i: (0, 0))

    return pl.pallas_call(
        body, name=name, grid=(t // tm,),
        out_shape=(jax.ShapeDtypeStruct((t, HEAD_COLS), F32), jax.ShapeDtypeStruct((t, Q_RANK), BF16),
                   jax.ShapeDtypeStruct((t, KV_RANK + LANES), BF16), jax.ShapeDtypeStruct((t, nq), BF16),
                   jax.ShapeDtypeStruct((t, nkv), BF16)),
        in_specs=[row(d), whole(win_head), whole(wq), whole(wk), whole(q_gain), whole(kv_gain), row(LANES), row(LANES)],
        out_specs=(row(HEAD_COLS), row(Q_RANK), row(KV_RANK + LANES), row(nq), row(nkv)),
        compiler_params=_params(("parallel",), VMEM_BIG),
    )(h_all, win_head, wq, wk, q_gain, kv_gain, cos, sgn)


def _head_bwd(dq, dk, dv, z, wq, wk_k, wk_v, win_head, q_gain, kv_gain, cos, sgn, s, name):
    t = z.shape[0]
    ns = s // ROW_TILE

    def body(dq_ref, dk_ref, dv_ref, z_ref, wq_ref, wkk_ref, wkv_ref, wi_ref, qg_ref, kg_ref, c_ref, s_ref,
             dz_ref, dh_ref, sums_ref):
        i = pl.program_id(0)

        @pl.when(i == 0)
        def _():
            sums_ref[...] = jnp.zeros_like(sums_ref)

        @pl.when(i < ns)
        def _():
            dc = lax.dot_general(dq_ref[...], wq_ref[...], NT_DIMS, preferred_element_type=F32)
            zq = z_ref[:, 0:Q_RANK]
            r = _rstd(zq)
            zn = zq * r
            sums_ref[0:1, :] += jnp.sum(dc * zn, axis=0, keepdims=True)
            dz_ref[:, 0:Q_RANK] = _norm_bwd(dc * qg_ref[...], zn, r).astype(BF16)

        @pl.when(i >= ns)
        def _():
            dz_ref[:, 0:Q_RANK] = jnp.zeros((ROW_TILE, Q_RANK), BF16)

        dkv = (lax.dot_general(dk_ref[...], wkk_ref[...], NT_DIMS, preferred_element_type=F32)
               + lax.dot_general(dv_ref[...], wkv_ref[...], NT_DIMS, preferred_element_type=F32))
        zk = z_ref[:, Q_RANK:Q_RANK + KV_RANK]
        r = _rstd(zk)
        zn = zk * r
        dc = dkv[:, 0:KV_RANK]
        sums_ref[1:2, 0:KV_RANK] += jnp.sum(dc * zn, axis=0, keepdims=True)
        dz_ref[:, Q_RANK:Q_RANK + KV_RANK] = _norm_bwd(dc * kg_ref[...], zn, r).astype(BF16)
        dz_ref[:, Q_RANK + KV_RANK:HEAD_COLS] = _rope(dkv[:, KV_RANK:KV_RANK + LANES], c_ref[...], s_ref[...],
                                                       True).astype(BF16)
        dh_ref[...] = jnp.dot(dz_ref[...], wi_ref[...], preferred_element_type=F32)

    def row(w):
        return pl.BlockSpec((ROW_TILE, w), lambda i: (i, 0))

    def whole(a):
        return pl.BlockSpec(a.shape, lambda i: (0, 0))

    return pl.pallas_call(
        body, name=name, grid=(t // ROW_TILE,),
        out_shape=(jax.ShapeDtypeStruct((t, HEAD_COLS), BF16), jax.ShapeDtypeStruct((t, D_MODEL), F32),
                   jax.ShapeDtypeStruct((8, Q_RANK), F32)),
        in_specs=[pl.BlockSpec((ROW_TILE, dq.shape[1]), lambda i: (jnp.minimum(i, ns - 1), 0)),
                  row(dk.shape[1]), row(dv.shape[1]), row(HEAD_COLS), whole(wq), whole(wk_k), whole(wk_v),
                  whole(win_head), whole(q_gain), whole(kv_gain), row(LANES), row(LANES)],
        out_specs=(row(HEAD_COLS), row(D_MODEL), pl.BlockSpec((8, Q_RANK), lambda i: (0, 0))),
        compiler_params=_params(("arbitrary",), VMEM_BIG),
    )(dq, dk, dv, z, wq, wk_k, wk_v, win_head, q_gain, kv_gain, cos, sgn)


def _shift_rows(u, s):
    rowi = lax.broadcasted_iota(jnp.int32, u.shape, 0)
    prev = jnp.where(rowi == 0, 0.0, pltpu.roll(u, 1, 0))
    nxt = jnp.where(rowi == s - 1, 0.0, pltpu.roll(u, s - 1, 0))
    return prev, nxt


def _conv_fwd(z_conv, cw, a_cat, name):
    s = z_conv.shape[0]

    def body(z_ref, w_ref, a_in_ref, o_ref):
        del a_in_ref
        gb, gc, xv = z_ref[:, 0:LANES], z_ref[:, LANES:2 * LANES], z_ref[:, 2 * LANES:3 * LANES]
        u = gc * xv
        prev, nxt = _shift_rows(u, s)
        y = w_ref[0:1, :] * prev + w_ref[1:2, :] * u + w_ref[2:3, :] * nxt
        o_ref[...] = (gb * y).astype(BF16)

    return pl.pallas_call(
        body, name=name, grid=(CONV_W // LANES,),
        out_shape=jax.ShapeDtypeStruct(a_cat.shape, a_cat.dtype),
        in_specs=[pl.BlockSpec((s, 3 * LANES), lambda j: (0, j)), pl.BlockSpec((3, LANES), lambda j: (0, j)),
                  pl.BlockSpec(memory_space=pl.ANY)],
        out_specs=pl.BlockSpec((s, LANES), lambda j: (0, 4 + j)),
        input_output_aliases={2: 0},
        compiler_params=_params(("parallel",), VMEM_BIG),
    )(z_conv, cw, a_cat)


def _conv_bwd(z_conv, cw, da, name):
    s = z_conv.shape[0]

    def body(z_ref, w_ref, da_ref, dz_ref, dw_ref):
        gb, gc, xv = z_ref[:, 0:LANES], z_ref[:, LANES:2 * LANES], z_ref[:, 2 * LANES:3 * LANES]
        u = gc * xv
        prev, nxt = _shift_rows(u, s)
        dcv = da_ref[...]
        dz_ref[:, 0:LANES] = (dcv * (w_ref[0:1, :] * prev + w_ref[1:2, :] * u + w_ref[2:3, :] * nxt)).astype(BF16)
        dy = dcv * gb
        dw_ref[0:1, :] = jnp.sum(dy * prev, axis=0, keepdims=True)
        dw_ref[1:2, :] = jnp.sum(dy * u, axis=0, keepdims=True)
        dw_ref[2:3, :] = jnp.sum(dy * nxt, axis=0, keepdims=True)
        dyp, dyn = _shift_rows(dy, s)
        du = w_ref[0:1, :] * dyn + w_ref[1:2, :] * dy + w_ref[2:3, :] * dyp
        dz_ref[:, LANES:2 * LANES] = (du * xv).astype(BF16)
        dz_ref[:, 2 * LANES:3 * LANES] = (du * gc).astype(BF16)

    blk = pl.BlockSpec((s, 3 * LANES), lambda j: (0, j))
    cws = pl.BlockSpec((3, LANES), lambda j: (0, j))
    return pl.pallas_call(
        body, name=name, grid=(CONV_W // LANES,),
        out_shape=(jax.ShapeDtypeStruct(z_conv.shape, BF16), jax.ShapeDtypeStruct((3, CONV_W), F32)),
        in_specs=[blk, cws, pl.BlockSpec((s, LANES), lambda j: (0, 4 + j))], out_specs=(blk, cws),
        compiler_params=_params(("parallel",), VMEM_BIG),
    )(z_conv, cw, da)


ATT_TQ = 256
ATT_Q_STEP = 1024
ATT_TQ_BWD = 512


def _head_mask(shape, hh):
    lane = lax.broadcasted_iota(jnp.int32, shape, 1)
    return (lane >= hh * V_DIM) & (lane < (hh + 1) * V_DIM)


def _attn_fwd(qf, kv, s, riding, name):
    t = kv.shape[0]
    step = min(ATT_Q_STEP, s)
    nq = s // step
    nr = riding.n

    def body(*refs):
        q_ref, k_ref, v_ref = refs[:3]
        o_ref, ob_ref, st_ref = refs[3 + nr:6 + nr]
        p, i = pl.program_id(0), pl.program_id(1)
        state = riding.run((p == 0) & (i == 0), (p == N_HEADS // 2 - 1) & (i == nq - 1),
                           refs[3:3 + nr], refs[6 + nr:6 + 2 * nr], refs[6 + 2 * nr:])
        v = v_ref[...]
        vlane = lax.broadcasted_iota(jnp.int32, v.shape, 1)
        one_lane = [(1 - hh) * V_DIM for hh in range(2)]
        vm = [jnp.where(_head_mask(v.shape, hh), v, jnp.where(vlane == one_lane[hh], 1.0, 0.0).astype(BF16))
              for hh in range(2)]

        def block(r, carry):
            rows = pl.ds(pl.multiple_of(r * ATT_TQ, ATT_TQ), ATT_TQ)
            olane = lax.broadcasted_iota(jnp.int32, (ATT_TQ, LANES), 1)
            acc = jnp.zeros((ATT_TQ, LANES), F32)
            stat = jnp.zeros((ATT_TQ, LANES), F32)
            for hh in range(2):
                sl = slice(hh * LANES, (hh + 1) * LANES)
                sc = lax.dot_general(q_ref[rows, sl], k_ref[:, sl], NT_DIMS, preferred_element_type=F32)
                mx = jnp.max(sc, axis=1, keepdims=True)
                e = jnp.exp2((sc - mx) * EXP2_SCALE).astype(BF16)
                res = jnp.dot(e, vm[hh], preferred_element_type=F32)
                den = jnp.sum(jnp.where(olane == one_lane[hh], res, 0.0), axis=1, keepdims=True)
                acc = acc + jnp.where(_head_mask(res.shape, hh), res * (1.0 / den), 0.0)
                stat = stat + jnp.where(olane == hh, mx * EXP2_SCALE + jnp.log(den) * LOG2_E, 0.0)
            o_ref[rows, :] = acc
            ob_ref[rows, :] = acc.astype(BF16)
            st_ref[:, rows] = stat.T[0:8, :]
            return carry

        lax.fori_loop(0, step // ATT_TQ, block, 0)
        riding.finish(state)

    o_spec = pl.BlockSpec((step, LANES), lambda p, i: (i, p))
    outs = pl.pallas_call(
        body, name=name, grid=(N_HEADS // 2, nq),
        out_shape=(jax.ShapeDtypeStruct((s, N_HEADS * V_DIM), F32),
                   jax.ShapeDtypeStruct((s, D_MODEL), BF16),
                   jax.ShapeDtypeStruct((N_HEADS // 2 * 8, s), F32), *riding.out_shape),
        in_specs=[pl.BlockSpec((step, 2 * LANES), lambda p, i: (i, p)),
                  pl.BlockSpec((t, 2 * LANES), lambda p, i: (0, p)),
                  pl.BlockSpec((t, LANES), lambda p, i: (0, N_HEADS + p)), *riding.specs],
        out_specs=(o_spec, o_spec, pl.BlockSpec((8, step), lambda p, i: (p, i)), *riding.specs),
        scratch_shapes=riding.scratch,
        compiler_params=_params(("arbitrary", "arbitrary"), VMEM_BIG),
    )(qf, kv, kv, *riding.arrays)
    return outs[0], outs[1], outs[2], list(outs[3:])


def _attn_bwd(qf, kv, o, da, stats, cos, sgn, riding, name):
    s, t = o.shape[0], kv.shape[0]
    ATT_TQ = ATT_TQ_BWD
    nq = s // ATT_TQ
    nr = riding.n

    def body(*refs):
        q_ref, k_ref, v_ref, o_ref, do_ref, st_ref, c_ref, s_ref = refs[:8]
        dq_ref, dk_ref, dv_ref = refs[8 + nr:11 + nr]
        dk_acc, dv_acc = refs[11 + 2 * nr:13 + 2 * nr]
        p, i = pl.program_id(0), pl.program_id(1)
        state = riding.run((p == 0) & (i == 0), (p == N_HEADS // 2 - 1) & (i == nq - 1),
                           refs[8:8 + nr], refs[11 + nr:11 + 2 * nr], refs[13 + 2 * nr:])

        @pl.when(i == 0)
        def _():
            dk_acc[...] = jnp.zeros_like(dk_acc)
            dv_acc[...] = jnp.zeros_like(dv_acc)

        v = v_ref[...]
        do = do_ref[...]
        od = do * o_ref[...]
        ones = jnp.ones((8, LANES), F32)
        for hh in range(2):
            sl = slice(hh * LANES, (hh + 1) * LANES)
            q, k = q_ref[:, sl], k_ref[:, sl]
            mask = _head_mask(do.shape, hh)
            dom = jnp.where(mask, do, 0.0).astype(BF16)
            delta = lax.dot_general(ones, jnp.where(mask, od, 0.0), NT_DIMS, preferred_element_type=F32,
                                    precision=lax.Precision.HIGHEST)[0:1, :]
            st = lax.dot_general(k, q, NT_DIMS, preferred_element_type=F32)
            pt = jnp.exp2(st * EXP2_SCALE - st_ref[hh:hh + 1, :]).astype(BF16)
            dpt = lax.dot_general(v, dom, NT_DIMS, preferred_element_type=F32)
            dst = (pt.astype(F32) * (dpt - delta)).astype(BF16)
            dv_acc[...] += jnp.dot(pt, dom, preferred_element_type=F32)
            dk_acc[:, sl] += jnp.dot(dst, q, preferred_element_type=F32)
            dq = lax.dot_general(dst, k, TN_DIMS, preferred_element_type=F32) * ATTN_SCALE
            dq_ref[:, sl] = _rope(dq, c_ref[...], s_ref[...], True).astype(BF16)

        @pl.when(i == nq - 1)
        def _():
            dk_ref[...] = (dk_acc[...] * ATTN_SCALE).astype(BF16)
            dv_ref[...] = dv_acc[...].astype(BF16)

        riding.finish(state)

    o_spec = pl.BlockSpec((ATT_TQ, LANES), lambda p, i: (i, p))
    tab = pl.BlockSpec((ATT_TQ, LANES), lambda p, i: (i, 0))
    outs = pl.pallas_call(
        body, name=name, grid=(N_HEADS // 2, nq),
        out_shape=(jax.ShapeDtypeStruct((s, N_HEADS * LANES), BF16),
                   jax.ShapeDtypeStruct((t, N_HEADS * LANES), BF16),
                   jax.ShapeDtypeStruct((t, N_HEADS * V_DIM), BF16), *riding.out_shape),
        in_specs=[pl.BlockSpec((ATT_TQ, 2 * LANES), lambda p, i: (i, p)),
                  pl.BlockSpec((t, 2 * LANES), lambda p, i: (0, p)),
                  pl.BlockSpec((t, LANES), lambda p, i: (0, N_HEADS + p)),
                  o_spec, o_spec,
                  pl.BlockSpec((8, ATT_TQ), lambda p, i: (p, i)), tab, tab, *riding.specs],
        out_specs=(pl.BlockSpec((ATT_TQ, 2 * LANES), lambda p, i: (i, p)),
                   pl.BlockSpec((t, 2 * LANES), lambda p, i: (0, p)),
                   pl.BlockSpec((t, LANES), lambda p, i: (0, p)), *riding.specs),
        scratch_shapes=[pltpu.VMEM((t, 2 * LANES), F32), pltpu.VMEM((t, LANES), F32), *riding.scratch],
        compiler_params=_params(("arbitrary", "arbitrary"), VMEM_BIG),
    )(qf, kv, kv, o, da, stats, cos, sgn, *riding.arrays)
    return outs[0], outs[1], outs[2], list(outs[3:])


def _silu(x):
    return x * (1.0 / (1.0 + jnp.exp(-x)))


def _prologue(c_rows, c_ctx, w_mod, b_cols, extra_rows, name):
    d, cols = c_rows.shape[1], w_mod.shape[1]

    def body(c_ref, cctx_ref, wmod_ref, b_ref, x_ref, a_ref, modg_ref, c_all, blk, c_send, c_recv, m_send, m_recv):
        _direct_gather(c_ref, c_all, c_send, c_recv)()
        a_ref[...] = jnp.zeros_like(a_ref)
        for j in range(N_DEV):
            a_ref[j:j + 1, :] = c_all[j, 0:1, :]
        a_ref[N_DEV:N_DEV + 1, :] = cctx_ref[...]
        blk[0:16, :] = jnp.dot(_silu(a_ref[...]), wmod_ref[...], preferred_element_type=F32,
                               precision=lax.Precision.HIGHEST) + b_ref[...]
        blk[16:24, :] = x_ref[...]
        _direct_gather(blk, modg_ref, m_send, m_recv)()

    vmem = pl.BlockSpec(memory_space=pltpu.VMEM)
    return pl.pallas_call(
        body, name=name,
        out_shape=(jax.ShapeDtypeStruct((16, d), F32), jax.ShapeDtypeStruct((N_DEV, 24, cols), F32)),
        in_specs=[vmem] * 5, out_specs=(vmem, vmem),
        scratch_shapes=[pltpu.VMEM((N_DEV, 8, d), F32), pltpu.VMEM((24, cols), F32)]
        + [pltpu.SemaphoreType.DMA((7,)) for _ in range(4)],
        compiler_params=_params(None, VMEM_BIG),
    )(c_rows, c_ctx, w_mod, b_cols, extra_rows)


def _adaln_bwd(a_t, w, d_ex, d_ctx, d_all, name):
    def body(at_ref, w_ref, dex_ref, dctx_ref, dall_ref, gw_ref, dsil_ref, dsum_ref):
        sil_t = _silu(at_ref[...])
        dctx = dctx_ref[...]
        row = dctx[0:1, :]
        for j in range(1, N_DEV):
            row = row + dctx[j:j + 1, :]
        rowi = lax.broadcasted_iota(jnp.int32, dctx.shape, 0)
        ctx_rows = jnp.where(rowi == 0, jnp.broadcast_to(row, dctx.shape), 0.0)
        hi = lax.Precision.HIGHEST
        d_rows = jnp.concatenate([dex_ref[...], ctx_rows], axis=0)
        gw_ref[...] = jnp.dot(sil_t, d_rows, preferred_element_type=F32, precision=hi)
        dsil_ref[...] = lax.dot_general(ctx_rows, w_ref[...], NT_DIMS, preferred_element_type=F32, precision=hi)
        tot = dall_ref[0]
        for j in range(1, N_DEV):
            tot = tot + dall_ref[j]
        dsum_ref[...] = tot

    return pl.pallas_call(
        body, name=name,
        out_shape=(jax.ShapeDtypeStruct(w.shape, F32), jax.ShapeDtypeStruct((8, w.shape[0]), F32),
                   jax.ShapeDtypeStruct(d_all.shape[1:], F32)),
        compiler_params=_params(None, VMEM_BIG),
    )(a_t, w, d_ex, d_ctx, d_all)


def _pack_small(sums1, sums2, fsums, sums1c, psums, d_cw, name):
    d = D_MODEL

    def body(s1_ref, s2_ref, f_ref, s1c_ref, p_ref, cw_ref, o_ref):
        o_ref[...] = jnp.zeros_like(o_ref)
        for col, (ref, r) in enumerate([(s1_ref, 1), (s1_ref, 0), (s2_ref, 2), (s2_ref, 1), (s2_ref, 0), (f_ref, 1)]):
            o_ref[0:1, col * d:(col + 1) * d] = ref[r:r + 1, :]
        o_ref[1:2, 0:d] = s1c_ref[1:2, :]
        o_ref[1:2, d:2 * d] = s1c_ref[0:1, :]
        o_ref[2:3, 0:Q_RANK] = p_ref[0:1, :]
        o_ref[2:3, Q_RANK:Q_RANK + KV_RANK] = p_ref[1:2, 0:KV_RANK]
        o_ref[2:3, Q_RANK + KV_RANK:Q_RANK + KV_RANK + d] = f_ref[0:1, :]
        for r in range(3):
            o_ref[3 + r:4 + r, 0:CONV_W] = cw_ref[r:r + 1, :]
        o_ref[6:7, 0:d] = f_ref[3:4, :]

    return pl.pallas_call(body, name=name, out_shape=jax.ShapeDtypeStruct((8, 6 * d), F32))(
        sums1, sums2, fsums, sums1c, psums, d_cw)


def _adam_math(w, g, m, v):
    nm = ADAM_B1 * m + (1.0 - ADAM_B1) * g
    nv = ADAM_B2 * v + (1.0 - ADAM_B2) * (g * g)
    m_hat = nm / (1.0 - ADAM_B1 ** ADAM_STEP)
    v_hat = nv / (1.0 - ADAM_B2 ** ADAM_STEP)
    return -ADAM_LR * (m_hat / (jnp.sqrt(v_hat) + ADAM_EPS) + ADAM_WD * w), nm, nv


def _small_update(dsum, dsil_all, g_cw, params, name):
    d = D_MODEL
    n = len(params)

    def body(*refs):
        dsum_ref, dsil_ref, gcw_ref = refs[:3]
        wmv = refs[3:3 + 3 * n]
        outs = refs[3 + 3 * n:]
        tot = dsil_ref[0]
        for j in range(1, N_DEV):
            tot = tot + dsil_ref[j]
        cv = wmv[0][...]
        sg = 1.0 / (1.0 + jnp.exp(-cv))
        off = Q_RANK + KV_RANK
        grads = [tot[0:1, :] * (sg * (1.0 + cv * (1.0 - sg))),
                 dsum_ref[0:1, :] + dsum_ref[1:2, :],
                 dsum_ref[2:3, 0:Q_RANK], dsum_ref[2:3, Q_RANK:off], dsum_ref[2:3, off:off + d],
                 gcw_ref[...]]
        for p, g in enumerate(grads):
            w_ref, m_ref, v_ref = wmv[3 * p:3 * p + 3]
            at = 0 if len(w_ref.shape) == 3 else Ellipsis
            res = (g,) + _adam_math(w_ref[at], g, m_ref[at], v_ref[at])
            for q, val in enumerate(res):
                outs[4 * p + q][at] = val

    flat = [a for wmv in params for a in wmv]
    out_shape = tuple(jax.ShapeDtypeStruct(wmv[0].shape, F32) for wmv in params for _ in range(4))
    outs = pl.pallas_call(body, name=name, out_shape=out_shape)(dsum, dsil_all, g_cw, *flat)
    return [outs[4 * p:4 * p + 4] for p in range(n)]


def _adamw(w, g, m, v, name, slots=False):
    _, rows, cols = w.shape
    tr = _pick(rows, (256, 128, 64, 32, 16, 8))

    def body(w_ref, g_ref, m_ref, v_ref, *outs):
        if slots:
            gv = g_ref[0].astype(F32)
            for j in range(1, g.shape[0]):
                gv = gv + g_ref[j].astype(F32)
            outs[0][...] = gv
        else:
            gv = g_ref[...]
        d_ref, nm_ref, nv_ref = outs[-3:]
        d_ref[...], nm_ref[...], nv_ref[...] = _adam_math(w_ref[...], gv, m_ref[...], v_ref[...])

    blk = pl.BlockSpec((None, tr, cols), lambda i: (0, i, 0))
    g_spec = (pl.BlockSpec((g.shape[0], tr, cols), lambda i: (0, i, 0)) if slots
              else pl.BlockSpec((tr, cols), lambda i: (i, 0)))
    sh = jax.ShapeDtypeStruct((1, rows, cols), F32)
    n_out = 4 if slots else 3
    return pl.pallas_call(
        body, name=name, grid=(rows // tr,), out_shape=(sh,) * n_out,
        in_specs=[blk, g_spec, blk, blk], out_specs=(blk,) * n_out,
        compiler_params=_params(("parallel",), VMEM_BIG),
    )(w, g, m, v)


def _rope_tables(s, l):
    tok = np.arange(s)
    row = (tok // GRID_W).astype(np.float32)
    col = (tok % GRID_W).astype(np.float32)
    half = QK_ROPE // 2
    freqs = np.float32(ROPE_THETA) ** (-np.arange(0, half, 2, dtype=np.float32) / np.float32(half))
    dd = np.arange(QK_ROPE)
    pos = np.where((dd // half)[None, :] == 0, row[:, None], col[:, None]).astype(np.float32)
    ang = (pos * freqs[dd % (half // 2)][None, :]).astype(np.float32)
    sin = np.sin(ang).astype(np.float32)
    cos_t = np.ones((s + l, LANES), np.float32)
    sgn_t = np.zeros((s + l, LANES), np.float32)
    cos_t[:s, QK_NOPE:QK_NOPE + QK_ROPE] = np.cos(ang)
    sgn_t[:s, QK_NOPE:QK_NOPE + QK_ROPE] = np.where(((dd % half) // (half // 2))[None, :] == 0, -sin, sin)
    return jnp.asarray(cos_t), jnp.asarray(sgn_t)


def _slots_to_cols(g):
    return g.transpose(1, 0, 2).reshape(g.shape[1], N_DEV * g.shape[2])


def _cols_to_slots(w):
    return w.reshape(w.shape[0], N_DEV, w.shape[1] // N_DEV).transpose(1, 0, 2)


def _unpack_small_weights(g_in_t, g_uq, g_ukv):
    w_t = g_in_t.reshape(N_DEV * g_in_t.shape[1], D_MODEL)
    zeros = jnp.zeros((QK_NOPE, D_MODEL), BF16)
    win_head_t = jnp.concatenate([w_t[:Q_RANK + KV_RANK], zeros, w_t[Q_RANK + KV_RANK:MLA_IN],
                                  zeros[:LANES - QK_NOPE - QK_ROPE]], axis=0)
    win_conv_t = w_t[MLA_IN:].reshape(3, CONV_W // LANES, LANES, D_MODEL).transpose(1, 0, 2, 3)
    win_conv_t = win_conv_t.reshape(3 * CONV_W, D_MODEL)
    w_uq = _slots_to_cols(g_uq).reshape(Q_RANK, N_HEADS, QK_NOPE + QK_ROPE)
    wq = jnp.pad(w_uq, ((0, 0), (0, 0), (0, LANES - QK_NOPE - QK_ROPE))).reshape(Q_RANK, N_HEADS * LANES)
    w_ukv = _slots_to_cols(g_ukv).reshape(KV_RANK, N_HEADS, QK_NOPE + V_DIM)
    k_top = jnp.pad(w_ukv[:, :, :QK_NOPE], ((0, 0), (0, 0), (0, LANES - QK_NOPE))).reshape(KV_RANK, N_HEADS * LANES)
    v_top = w_ukv[:, :, QK_NOPE:].reshape(KV_RANK, N_HEADS * V_DIM)
    eye = jnp.pad(jnp.eye(QK_ROPE, dtype=BF16), ((QK_NOPE, LANES - QK_NOPE - QK_ROPE),) * 2)
    wk = jnp.concatenate([
        jnp.concatenate([k_top, v_top], axis=1),
        jnp.concatenate([jnp.tile(eye, (1, N_HEADS)), jnp.zeros((LANES, N_HEADS * V_DIM), BF16)], axis=1)], axis=0)
    return win_head_t, win_conv_t, wq, wk


def _pack_small_grads(d_head_t, d_conv_t, d_wq, d_wkk, d_wkv):
    d_conv_t = d_conv_t.reshape(CONV_W // LANES, 3, LANES, D_MODEL).transpose(1, 0, 2, 3).reshape(3 * CONV_W, D_MODEL)
    rope0 = Q_RANK + KV_RANK + QK_NOPE
    g_in_t = jnp.concatenate([d_head_t[:Q_RANK + KV_RANK], d_head_t[rope0:rope0 + QK_ROPE], d_conv_t], axis=0)
    g_in_t = g_in_t.reshape(N_DEV, -1, D_MODEL).astype(BF16)
    g_uq = d_wq.reshape(Q_RANK, N_HEADS, LANES)[:, :, :QK_NOPE + QK_ROPE].reshape(Q_RANK, -1)
    g_kn = d_wkk[:KV_RANK].reshape(KV_RANK, N_HEADS, LANES)[:, :, :QK_NOPE]
    g_v = d_wkv[:KV_RANK].reshape(KV_RANK, N_HEADS, V_DIM)
    g_ukv = jnp.concatenate([g_kn, g_v], axis=2).reshape(KV_RANK, -1)
    return [g_in_t] + [_cols_to_slots(g).astype(BF16) for g in (g_uq, g_ukv)]


def kernel(x, c, ctx, c_ctx, w_mod, b_mod, w_in, q_norm_g, w_uq, kv_norm_g, w_ukv, conv_w, w_out, w_mlp1, w_mlp2, final_norm_g, loss_target, m_c_ctx, m_w_mod, m_b_mod, m_w_in, m_q_norm_g, m_w_uq, m_kv_norm_g, m_w_ukv, m_conv_w, m_w_out, m_w_mlp1, m_w_mlp2, m_final_norm_g, v_c_ctx, v_w_mod, v_b_mod, v_w_in, v_q_norm_g, v_w_uq, v_kv_norm_g, v_w_ukv, v_conv_w, v_w_out, v_w_mlp1, v_w_mlp2, v_final_norm_g):
    me = _my_index()
    x2d, ctx2d, tgt = x[0], ctx[0], loss_target[0]
    s, l = x2d.shape[0], ctx2d.shape[0]
    t = s + l
    d = D_MODEL
    mod_cols = w_mod.shape[2]
    cw_cols = conv_w.shape[2]

    b_cols = lax.dynamic_slice(b_mod, (0, me * mod_cols), (1, mod_cols))
    cw_blk = jnp.pad(conv_w[0], ((0, 5), (0, mod_cols - cw_cols)))
    a_rows, gathered = _prologue(jnp.pad(c, ((0, 7), (0, 0))), c_ctx[None, :], w_mod[0], b_cols, cw_blk,
                                 "prologue")
    mod_mine = lax.dynamic_index_in_dim(gathered, me, axis=1, keepdims=False).reshape(1, 6 * d)
    mod_ctx = gathered[:, 8, :].reshape(1, 6 * d)
    cw_full = gathered[:, 16:19, :cw_cols].transpose(1, 0, 2).reshape(3, CONV_W)

    early = [w.astype(BF16) for w in (w_in[0].T, w_uq[0], w_ukv[0])]
    late = [w.astype(BF16) for w in (w_out[0], w_mlp1[0], w_mlp2[0])]
    h_all, (g_in, g_uq, g_ukv) = _modulate_all(x2d, ctx2d, mod_mine, mod_ctx, _RidingGather(early),
                                               "modulate1")
    win_head, win_conv, wq, wk = _unpack_small_weights(g_in, g_uq, g_ukv)
    wk_k, wk_v = wk[:, :N_HEADS * LANES], wk[:, N_HEADS * LANES:]
    cos, sgn = _rope_tables(s, l)

    tm_t = _pick(t, (1088, 768, 256))
    tk_t = _pick(t, (2176, 768, 256))
    z_head, cq, kv_in, qf, kv = _head_fwd(h_all, win_head, wq, wk, q_norm_g, kv_norm_g, cos, sgn, tm_t, "head_fwd")
    z_conv = _matmul(h_all, win_conv, mode="nt", name="in_proj_conv", m=s, tm=1024, tn=1536, tk=1024)
    attn, a_cat, stats, (g_out, w1, g_w2) = _attn_fwd(qf, kv, s, _Riding("gather", late), "attn_fwd")
    wo = g_out.reshape(d, d)
    w2 = g_w2.reshape(D_FF, d)
    a_cat = _conv_fwd(z_conv, cw_full, a_cat, "conv_fwd")
    (o, x1, h2), _ = _matmul_rows(a_cat, wo, _epi_resid_modulate, mode="nn", name="out_proj", tm=1024, tk=1024,
                                  rows=[x2d], vecs=[(mod_mine, 2), (mod_mine, 3), (mod_mine, 4)],
                                  out_dtypes=[F32, F32, BF16])
    u1, act = _matmul(h2, w1, mode="nn", name="mlp_up", tm=4096, tk=1024, epilogue="relu2", slots="b_cols")
    (dx2, dm, fsums), _ = _matmul_rows(act, w2, _epi_final, mode="nn", name="mlp_down", tm=512, tk=4096,
                                       rows=[x1, tgt], vecs=[(mod_mine, 5), (final_norm_g[None, :], 0)],
                                       out_dtypes=[F32, BF16], sums=True)

    d_w2 = _matmul(act, dm, mode="tn", name="d_w_mlp2", out_dtype=BF16, tm=1024, tn=1024, tk=4096)
    du1 = _matmul(dm, w2, mode="nt", name="d_act", out_dtype=BF16, tm=2048, tn=1024, tk=1024,
                  epilogue="drelu2", extra=(u1,))
    d_w1 = _matmul(h2, du1, mode="tn", name="d_w_mlp1", out_dtype=BF16, tm=1024, tk=4096, slots="out")
    (dx1, do, sums2), _ = _matmul_rows(du1, w1, _epi_modulate2_bwd, mode="nt", name="d_h2", tm=512, tk=4096,
                                       slots="b_contract", rows=[x1, dx2, o], vecs=[(mod_mine, 4), (mod_mine, 2)],
                                       out_dtypes=[F32, BF16], sums=True)
    d_wo = _matmul(a_cat, do, mode="tn", name="d_w_out", out_dtype=BF16, tm=1024, tn=1024, tk=2048)
    da = _matmul(do, wo, mode="nt", name="d_a", tm=1024, tn=1024, tk=1024)
    dz_conv, d_cw = _conv_bwd(z_conv, cw_full, da, "conv_bwd")
    ready = [d_wo.reshape(N_DEV, d // N_DEV, d), d_w1, d_w2.reshape(N_DEV, D_FF // N_DEV, d)]
    dq, dk, dv, rode = _attn_bwd(qf, kv, attn, da, stats, cos, sgn, _Riding("exchange", ready), "attn_bwd")
    d_wq = _matmul(cq, dq, mode="tn", name="d_w_uq", k=s, tm=256, tn=1024, tk=4096)
    d_wkk = _matmul(kv_in, dk, mode="tn", name="d_w_ukv_k", tm=256, tn=1024, tk=tk_t)
    d_wkv = _matmul(kv_in, dv, mode="tn", name="d_w_ukv_v", tm=256, tn=512, tk=tk_t)
    dz_head, dh_head, psums = _head_bwd(dq, dk, dv, z_head, wq, wk_k, wk_v, win_head, q_norm_g, kv_norm_g, cos, sgn, s,
                                        "head_bwd")
    d_head = _matmul(dz_head, h_all, mode="tn", name="d_w_in_head", tm=512, tn=1024, tk=tk_t)
    d_conv = _matmul(dz_conv, h_all, mode="tn", name="d_w_in_conv", k=s, tm=1536, tn=1024, tk=2048)
    send = _pack_small_grads(d_head, d_conv, d_wq, d_wkk, d_wkv)
    (grad_x, sums1), got = _matmul_rows(dz_conv, win_conv, _epi_modulate1_bwd, mode="nn", name="d_h1", tm=s // 4,
                                        tk=win_conv.shape[0], rows=[dh_head, x2d, dx1], vecs=[(mod_mine, 1)],
                                        out_dtypes=[F32], sums=True, riding=_RidingReduce(send))
    sums1c = _modulate_sums(dh_head, s // ROW_TILE, ctx2d)

    small = _pack_small(sums1, sums2, fsums, sums1c, psums, d_cw, "pack_small")
    (d_all,) = _all_gather([small], "gather_small_grads", True)
    d_cols = lax.dynamic_slice_in_dim(d_all, me * mod_cols, mod_cols, axis=2)
    g_w_mod, dsil, dsum = _adaln_bwd(a_rows.T, w_mod[0], d_cols[:, 0, :], d_cols[:, 1, :], d_all, "adaln_bwd")
    (dsil_all,) = _all_gather([dsil], "gather_d_cctx", True)
    loss = dsum[6, 0]
    g_cw = lax.dynamic_slice(dsum, (3, me * cw_cols), (3, cw_cols))

    slots = dict(zip(["w_in", "w_uq", "w_ukv"], got))
    slots.update(zip(["w_out", "w_mlp1", "w_mlp2"], rode))

    grads = {}
    weights = {"c_ctx": c_ctx, "w_mod": w_mod, "b_mod": b_mod, "w_in": w_in, "q_norm_g": q_norm_g, "w_uq": w_uq,
               "kv_norm_g": kv_norm_g, "w_ukv": w_ukv, "conv_w": conv_w, "w_out": w_out, "w_mlp1": w_mlp1,
               "w_mlp2": w_mlp2, "final_norm_g": final_norm_g}
    m_in = {"c_ctx": m_c_ctx, "w_mod": m_w_mod, "b_mod": m_b_mod, "w_in": m_w_in, "q_norm_g": m_q_norm_g,
            "w_uq": m_w_uq, "kv_norm_g": m_kv_norm_g, "w_ukv": m_w_ukv, "conv_w": m_conv_w, "w_out": m_w_out,
            "w_mlp1": m_w_mlp1, "w_mlp2": m_w_mlp2, "final_norm_g": m_final_norm_g}
    v_in = {"c_ctx": v_c_ctx, "w_mod": v_w_mod, "b_mod": v_b_mod, "w_in": v_w_in, "q_norm_g": v_q_norm_g,
            "w_uq": v_w_uq, "kv_norm_g": v_kv_norm_g, "w_ukv": v_w_ukv, "conv_w": v_conv_w, "w_out": v_w_out,
            "w_mlp1": v_w_mlp1, "w_mlp2": v_w_mlp2, "final_norm_g": v_final_norm_g}
    names = list(weights)
    small_names = ["c_ctx", "b_mod", "q_norm_g", "kv_norm_g", "final_norm_g", "conv_w"]
    delta, new_m, new_v = {}, {}, {}

    def as_rows(a):
        return a[None, :] if a.ndim == 1 else a

    small_out = _small_update(dsum, dsil_all, g_cw, [[as_rows(src[n]) for src in (weights, m_in, v_in)]
                                                      for n in small_names], "small_update")
    for n, outs in zip(small_names, small_out):
        grads[n], delta[n], new_m[n], new_v[n] = [a.reshape(weights[n].shape) for a in outs]
    for n in names:
        if n in small_names:
            continue
        if n == "w_in":
            wmv = [jnp.swapaxes(src[n], 1, 2) for src in (weights, m_in, v_in)]
            outs = _adamw(wmv[0], slots[n], wmv[1], wmv[2], "adamw_" + n, slots=True)
            grads[n], delta[n], new_m[n], new_v[n] = [jnp.swapaxes(a, 1, 2) for a in outs]
        elif n in slots:
            grads[n], delta[n], new_m[n], new_v[n] = _adamw(weights[n], slots[n], m_in[n], v_in[n], "adamw_" + n,
                                                            slots=True)
        else:
            delta[n], new_m[n], new_v[n] = _adamw(weights[n], g_w_mod, m_in[n], v_in[n], "adamw_" + n)
            grads[n] = g_w_mod[None]

    return (loss, grad_x[None], *[grads[n] for n in names], *[delta[n] for n in names],
            *[new_m[n] for n in names], *[new_v[n] for n in names])
```

```python
import math

import jax
import jax.numpy as jnp
import numpy as np
from jax import lax
from jax.experimental import pallas as pl
from jax.experimental.pallas import tpu as pltpu

F32 = jnp.float32
BF16 = jnp.bfloat16

D_MODEL = 1024
GRID_W = 64
N_HEADS = 8
QK_NOPE = 64
QK_ROPE = 32
V_DIM = 64
Q_RANK = 256
KV_RANK = 128
MLA_IN = Q_RANK + KV_RANK + QK_ROPE
CONV_W = 512
HEAD_COLS = 512
D_FF = 4096
ROPE_THETA = 10000.0
EPS = 1e-6
ATTN_SCALE = 1.0 / math.sqrt(QK_NOPE + QK_ROPE)
LOG2_E = 1.0 / math.log(2.0)
EXP2_SCALE = ATTN_SCALE * LOG2_E
N_DEV = 8
LANES = 128

ADAM_LR, ADAM_B1, ADAM_B2, ADAM_EPS, ADAM_WD, ADAM_STEP = 0.001, 0.9, 0.999, 1e-08, 0.01, 10

ROW_TILE = 256
VMEM_BIG = 60 * 1024 * 1024


def _params(sem=None, vmem=None):
    return pltpu.CompilerParams(dimension_semantics=sem, vmem_limit_bytes=vmem)


def _pick(n, prefs):
    for p in prefs:
        if n % p == 0:
            return p
    return n


def _my_index():
    return 4 * lax.axis_index("x") + 2 * lax.axis_index("y") + lax.axis_index("c")


def _two_level_gather(x_refs, out_refs, send_sems, recv_sems, local_sems):
    n = len(x_refs)
    x, y, c = lax.axis_index("x"), lax.axis_index("y"), lax.axis_index("c")
    me, sibling = (x, y, c), (x, y, 1 - c)
    chips = [(1 - x, y), (x, 1 - y), (1 - x, 1 - y)]

    def slot(a, px, py, pc):
        return out_refs[a].at[4 * px + 2 * py + pc]

    def copy(a, k, block, to, src=None):
        return pltpu.make_async_remote_copy(
            src_ref=slot(a, *block) if src is None else src, dst_ref=slot(a, *block),
            send_sem=send_sems.at[7 * a + k], recv_sem=recv_sems.at[7 * a + k],
            device_id=to, device_id_type=pl.DeviceIdType.MESH)

    mine = [pltpu.make_async_copy(x_refs[a], slot(a, *me), local_sems.at[a]) for a in range(n)]
    first = [cp for a in range(n) for cp in
             [copy(a, 0, me, sibling, src=x_refs[a])]
             + [copy(a, 1 + j, me, (*chip, c), src=x_refs[a]) for j, chip in enumerate(chips)]]
    passed = [[copy(a, 4 + j, (*chip, c), sibling) for j, chip in enumerate(chips)] for a in range(n)]

    def start():
        for cp in mine + first:
            cp.start()

    def forward():
        for a in range(n):
            for j, chip in enumerate(chips):
                copy(a, 1 + j, (*chip, c), me).wait_recv()
                passed[a][j].start()

    def finish():
        for a in range(n):
            copy(a, 0, sibling, me).wait_recv()
            for j, chip in enumerate(chips):
                copy(a, 4 + j, (*chip, 1 - c), me).wait_recv()
        for cp in first + [cp for per_array in passed for cp in per_array]:
            cp.wait_send()
        for cp in mine:
            cp.wait()

    return start, forward, finish


def _direct_gather(src_ref, dst_ref, send_sems, recv_sems):
    x, y, c = lax.axis_index("x"), lax.axis_index("y"), lax.axis_index("c")
    me = 4 * x + 2 * y + c
    dst_ref[me] = src_ref[...]
    sends, landings = [], []
    for k in range(1, N_DEV):
        peer = (1 - x if k & 4 else x, 1 - y if k & 2 else y, 1 - c if k & 1 else c)
        pid = 4 * peer[0] + 2 * peer[1] + peer[2]
        for dst, out in ((me, sends), (pid, landings)):
            out.append(pltpu.make_async_remote_copy(
                src_ref=src_ref, dst_ref=dst_ref.at[dst], send_sem=send_sems.at[k - 1], recv_sem=recv_sems.at[k - 1],
                device_id=peer, device_id_type=pl.DeviceIdType.MESH))
    for cp in sends:
        cp.start()

    def finish():
        for cp in landings:
            cp.wait_recv()
        for cp in sends:
            cp.wait_send()

    return finish


def _all_gather(arrays, name, in_vmem):
    space = pltpu.VMEM if in_vmem else pl.ANY
    n = len(arrays)

    def body(*refs):
        for phase in _two_level_gather(refs[:n], refs[n:2 * n], *refs[2 * n:]):
            phase()

    outs = pl.pallas_call(
        body, name=name,
        out_shape=tuple(jax.ShapeDtypeStruct((N_DEV,) + a.shape, a.dtype) for a in arrays),
        in_specs=[pl.BlockSpec(memory_space=space)] * n,
        out_specs=tuple(pl.BlockSpec(memory_space=space) for _ in arrays),
        scratch_shapes=[pltpu.SemaphoreType.DMA((7 * n,)), pltpu.SemaphoreType.DMA((7 * n,)),
                        pltpu.SemaphoreType.DMA((n,))],
    )(*arrays)
    return list(outs)


class _Riding:
    def __init__(self, kind, arrays):
        self.kind, self.arrays, self.n = kind, list(arrays), len(arrays)
        lead = (N_DEV,) if kind == "gather" else ()
        self.out_shape = [jax.ShapeDtypeStruct(lead + a.shape, a.dtype) for a in self.arrays]
        self.specs = [pl.BlockSpec(memory_space=pl.ANY)] * self.n
        self.scratch = [pltpu.SemaphoreType.DMA((7 * self.n,)), pltpu.SemaphoreType.DMA((7 * self.n,)),
                        pltpu.SemaphoreType.DMA((self.n,))]

    def copies(self, x_refs, y_refs, send_sems, recv_sems, local_sems):
        x, y, c = lax.axis_index("x"), lax.axis_index("y"), lax.axis_index("c")
        me = 4 * x + 2 * y + c
        local, sends, landings = [], [], []
        for a in range(self.n):
            src_mine = x_refs[a] if self.kind == "gather" else x_refs[a].at[me]
            local.append(pltpu.make_async_copy(src_mine, y_refs[a].at[me], local_sems.at[a]))
            for k in range(1, N_DEV):
                peer = (1 - x if k & 4 else x, 1 - y if k & 2 else y, 1 - c if k & 1 else c)
                pid = 4 * peer[0] + 2 * peer[1] + peer[2]
                src = x_refs[a] if self.kind == "gather" else x_refs[a].at[pid]
                for dst, out in ((me, sends), (pid, landings)):
                    out.append(pltpu.make_async_remote_copy(
                        src_ref=src, dst_ref=y_refs[a].at[dst],
                        send_sem=send_sems.at[7 * a + k - 1], recv_sem=recv_sems.at[7 * a + k - 1],
                        device_id=peer, device_id_type=pl.DeviceIdType.MESH))
        return local, sends, landings

    def run(self, first, last, x_refs, y_refs, sems, middle=None):
        if self.n == 0:
            return None
        local, sends, landings = self.copies(x_refs, y_refs, *sems)

        @pl.when(first)
        def _():
            for cp in local + sends:
                cp.start()

        return local, sends, landings, last

    @staticmethod
    def finish(state):
        if state is None:
            return
        local, sends, landings, last = state

        @pl.when(last)
        def _():
            for cp in landings:
                cp.wait_recv()
            for cp in sends:
                cp.wait_send()
            for cp in local:
                cp.wait()


class _RidingGather:
    def __init__(self, arrays):
        self.arrays, self.n = list(arrays), len(arrays)
        self.out_shape = [jax.ShapeDtypeStruct((N_DEV,) + a.shape, a.dtype) for a in self.arrays]
        self.specs = [pl.BlockSpec(memory_space=pl.ANY)] * self.n
        self.scratch = [pltpu.SemaphoreType.DMA((7 * self.n,)), pltpu.SemaphoreType.DMA((7 * self.n,)),
                        pltpu.SemaphoreType.DMA((self.n,))]

    def run(self, first, last, x_refs, y_refs, sems, middle):
        start, forward, finish = _two_level_gather(x_refs, y_refs, *sems)
        pl.when(first)(start)
        pl.when(middle)(forward)
        return finish, last

    @staticmethod
    def finish(state):
        finish, last = state
        pl.when(last)(finish)


class _RidingReduce:
    def __init__(self, arrays):
        self.arrays, self.n = list(arrays), len(arrays)
        self.out_shape = [jax.ShapeDtypeStruct((4,) + a.shape[1:], a.dtype) for a in self.arrays]
        self.specs = [pl.BlockSpec(memory_space=pl.ANY)] * self.n
        self.scratch = [pltpu.VMEM((4,) + a.shape[1:], a.dtype) for a in self.arrays for _ in range(3)]
        self.scratch += [pltpu.SemaphoreType.DMA((self.n,)) for _ in range(6)]

    def run(self, first, last, x_refs, y_refs, scratch, middle):
        n = self.n
        own, sib, tot = scratch[0:3 * n:3], scratch[1:3 * n:3], scratch[2:3 * n:3]
        d2d_send, d2d_recv, local_in, ici_send, ici_recv, local_out = scratch[3 * n:]
        x, y, c = lax.axis_index("x"), lax.axis_index("y"), lax.axis_index("c")
        my_chip = 2 * x + y
        sibling = (x, y, 1 - c)
        others = [(1 - x, y), (x, 1 - y), (1 - x, 1 - y)]

        def to_sibling(a, j=None):
            src = x_refs[a].at[pl.ds(0, 4)] if j is None else x_refs[a].at[2 * j + 1 - c]
            dst = sib[a] if j is None else sib[a].at[j]
            return pltpu.make_async_remote_copy(src_ref=src, dst_ref=dst, send_sem=d2d_send.at[a],
                                                recv_sem=d2d_recv.at[a], device_id=sibling,
                                                device_id_type=pl.DeviceIdType.MESH)

        def mine_in(a, j=None):
            src = x_refs[a].at[pl.ds(0, 4)] if j is None else x_refs[a].at[2 * j + c]
            return pltpu.make_async_copy(src, own[a] if j is None else own[a].at[j], local_in.at[a])

        def to_chip(a, chip=None):
            if chip is None:
                src, dst, peer = tot[a].at[pl.ds(0, 3)], y_refs[a].at[pl.ds(0, 3)], sibling
            else:
                src, dst, peer = tot[a].at[2 * chip[0] + chip[1]], y_refs[a].at[my_chip], (*chip, c)
            return pltpu.make_async_remote_copy(src_ref=src, dst_ref=dst, send_sem=ici_send.at[a],
                                                recv_sem=ici_recv.at[a], device_id=peer,
                                                device_id_type=pl.DeviceIdType.MESH)

        def mine_out(a):
            return pltpu.make_async_copy(tot[a].at[my_chip], y_refs[a].at[my_chip], local_out.at[a])

        @pl.when(first)
        def _():
            for a in range(n):
                for j in range(4):
                    to_sibling(a, j).start()
                    mine_in(a, j).start()

        @pl.when(middle)
        def _():
            for a in range(n):
                to_sibling(a).wait_recv()
                to_sibling(a).wait_send()
                mine_in(a).wait()
                tot[a][...] = (own[a][...].astype(F32) + sib[a][...].astype(F32)).astype(tot[a].dtype)
                for chip in others:
                    to_chip(a, chip).start()
                mine_out(a).start()

        def finish():
            @pl.when(last)
            def _():
                for a in range(n):
                    to_chip(a).wait_recv()
                    to_chip(a).wait_send()
                    mine_out(a).wait()

        return finish

    @staticmethod
    def finish(state):
        state()


_DIMS ={"nn": (((1,), (0,)), ((), ())), "nt": (((1,), (1,)), ((), ())), "tn": (((0,), (0,)), ((), ()))}
NT_DIMS = _DIMS["nt"]
TN_DIMS = _DIMS["tn"]


def _swap8(x):
    lane = lax.broadcasted_iota(jnp.int32, x.shape, 1)
    return jnp.where((lane & 15) < 8, pltpu.roll(x, LANES - 8, 1), pltpu.roll(x, 8, 1))


def _rope(x, cos, sgn, bwd):
    return x * cos + (_swap8(x * sgn) if bwd else _swap8(x) * sgn)


def _matmul(a, b, *, mode, name, out_dtype=F32, tm=512, tn=512, tk=512, m=None, k=None,
            epilogue=None, extra=(), slots=None):
    if mode == "nn":
        m = a.shape[0] if m is None else m
        k = a.shape[1]
        n = N_DEV * b.shape[2] if slots == "b_cols" else b.shape[1]
    elif mode == "nt":
        m = a.shape[0] if m is None else m
        k = a.shape[1]
        n = b.shape[0]
    else:
        k = a.shape[0] if k is None else k
        m, n = a.shape[1], b.shape[1]
    tm, tn, tk = min(tm, m), min(tn, n), min(tk, k)
    if slots == "b_cols":
        tn = b.shape[2]
    if slots == "out":
        tn = n // N_DEV
    assert m % tm == 0 and n % tn == 0 and k % tk == 0, (name, m, n, k, tm, tn, tk)
    nk = k // tk
    dims = _DIMS[mode]
    a_spec = (pl.BlockSpec((tk, tm), lambda i, j, kk: (kk, i)) if mode == "tn"
              else pl.BlockSpec((tm, tk), lambda i, j, kk: (i, kk)))
    if slots == "b_cols":
        b_spec = pl.BlockSpec((None, tk, tn), lambda i, j, kk: (j, kk, 0))
    elif mode == "nt":
        b_spec = pl.BlockSpec((tn, tk), lambda i, j, kk: (j, kk))
    else:
        b_spec = pl.BlockSpec((tk, tn), lambda i, j, kk: (kk, j))
    tile = pl.BlockSpec((tm, tn), lambda i, j, kk: (i, j))
    if slots == "out":
        o_spec = pl.BlockSpec((None, tm, tn), lambda i, j, kk: (j, i, 0))
        o_shape = (N_DEV, m, tn)
    else:
        o_spec, o_shape = tile, (m, n)
    in_specs, args = [a_spec, b_spec], [a, b]
    if epilogue == "drelu2":
        in_specs.append(tile)
    args += list(extra)
    if epilogue == "relu2":
        out_shape = (jax.ShapeDtypeStruct(o_shape, BF16), jax.ShapeDtypeStruct(o_shape, BF16))
        out_specs = (o_spec, o_spec)
    else:
        out_shape = jax.ShapeDtypeStruct(o_shape, out_dtype)
        out_specs = o_spec
    n_in = len(args)
    n_out = 2 if epilogue == "relu2" else 1

    def body(*refs):
        a_ref, b_ref = refs[0], refs[1]
        outs = refs[n_in:n_in + n_out]
        part = lax.dot_general(a_ref[...], b_ref[...], dims, preferred_element_type=F32)

        def finish(acc):
            if epilogue == "relu2":
                outs[0][...] = acc.astype(BF16)
                r = jnp.maximum(acc, 0.0)
                outs[1][...] = (r * r).astype(BF16)
            elif epilogue == "drelu2":
                u = refs[2][...].astype(F32)
                outs[0][...] = (acc * (2.0 * jnp.maximum(u, 0.0))).astype(out_dtype)
            else:
                outs[0][...] = acc.astype(out_dtype)

        if nk == 1:
            finish(part)
        else:
            acc_ref = refs[n_in + n_out]
            kk = pl.program_id(2)

            @pl.when(kk == 0)
            def _():
                acc_ref[...] = part

            @pl.when(kk > 0)
            def _():
                acc_ref[...] += part

            @pl.when(kk == nk - 1)
            def _():
                finish(acc_ref[...])

    return pl.pallas_call(
        body, name=name, grid=(m // tm, n // tn, nk),
        out_shape=out_shape, in_specs=in_specs, out_specs=out_specs,
        scratch_shapes=[pltpu.VMEM((tm, tn), F32)] if nk > 1 else [],
        compiler_params=_params(("parallel", "parallel", "arbitrary"), VMEM_BIG),
    )(*args)


def _rstd(x):
    return lax.rsqrt(jnp.mean(x * x, axis=1, keepdims=True) + EPS)


def _norm_bwd(dxn, xn, r):
    return r * (dxn - xn * jnp.mean(dxn * xn, axis=1, keepdims=True))


def _vec(col):
    return pl.BlockSpec((1, D_MODEL), lambda i: (0, col))


def _matmul_rows(a, b, epi, *, mode, name, tm, tk, rows=(), vecs=(), out_dtypes=(), sums=False, slots=None,
                 riding=None):
    m, k = a.shape
    n = D_MODEL
    tm, tk = min(tm, m), min(tk, k)
    riding = riding or _Riding("gather", [])
    group = 1
    if slots == "b_contract":
        group = max(1, tk // b.shape[2])
        tk = group * b.shape[2]
        b_spec = pl.BlockSpec((group, n, tk // group), lambda i, kk: (kk, 0, 0))
    elif mode == "nt":
        b_spec = pl.BlockSpec((n, tk), lambda i, kk: (0, kk))
    else:
        b_spec = pl.BlockSpec((tk, n), lambda i, kk: (kk, 0))
    assert m % tm == 0 and k % tk == 0, (name, m, k, tm, tk)
    ni, nk = m // tm, k // tk
    assert ni >= 2 or not isinstance(riding, _RidingReduce), "the two-level exchange needs a middle grid step"
    dims = _DIMS[mode]
    tile = pl.BlockSpec((tm, n), lambda i, kk: (i, 0))
    in_specs = [pl.BlockSpec((tm, tk), lambda i, kk: (i, kk)), b_spec] + [tile] * len(rows)
    in_specs += [pl.BlockSpec((1, n), lambda i, kk, col=col: (0, col)) for _, col in vecs]
    args = [a, b, *rows, *[v for v, _ in vecs]]
    out_shape = [jax.ShapeDtypeStruct((m, n), dt) for dt in out_dtypes]
    out_specs = [tile] * len(out_dtypes)
    if sums:
        out_shape.append(jax.ShapeDtypeStruct((8, n), F32))
        out_specs.append(pl.BlockSpec((8, n), lambda i, kk: (0, 0)))
    n_rows, n_vecs, n_outs, nr = len(rows), len(vecs), len(out_dtypes), riding.n
    n_in = 2 + n_rows + n_vecs

    def body(*refs):
        a_ref, b_ref = refs[0], refs[1]
        row_refs = refs[2:2 + n_rows]
        vec_refs = refs[2 + n_rows:n_in]
        x_refs = refs[n_in:n_in + nr]
        out_refs = refs[n_in + nr:n_in + nr + n_outs]
        pos = n_in + nr + n_outs
        sums_ref = refs[pos] if sums else None
        pos += 1 if sums else 0
        y_refs = refs[pos:pos + nr]
        pos += nr
        acc_ref = refs[pos] if nk > 1 else None
        sem_refs = refs[pos + (1 if nk > 1 else 0):]
        i, kk = pl.program_id(0), pl.program_id(1)
        state = riding.run((i == 0) & (kk == 0), (i == ni - 1) & (kk == nk - 1), x_refs, y_refs, sem_refs,
                           middle=(i == 1) & (kk == 0))
        if slots == "b_contract":
            c = tk // group
            part = lax.dot_general(a_ref[:, 0:c], b_ref[0], dims, preferred_element_type=F32)
            for u in range(1, group):
                part = part + lax.dot_general(a_ref[:, u * c:(u + 1) * c], b_ref[u], dims, preferred_element_type=F32)
        else:
            part = lax.dot_general(a_ref[...], b_ref[...], dims, preferred_element_type=F32)

        def finish(acc):
            nsub = tm // ROW_TILE
            for r in range(nsub):
                blk = pl.ds(r * ROW_TILE, ROW_TILE)
                epi(acc[r * ROW_TILE:(r + 1) * ROW_TILE], [ref.at[blk] for ref in row_refs], vec_refs,
                    [ref.at[blk] for ref in out_refs], sums_ref,
                    (i == 0) if r == 0 else None, (i == ni - 1) if r == nsub - 1 else None)

        if nk == 1:
            finish(part)
        else:
            @pl.when(kk == 0)
            def _():
                acc_ref[...] = part

            @pl.when(kk > 0)
            def _():
                acc_ref[...] += part

            @pl.when(kk == nk - 1)
            def _():
                finish(acc_ref)

        riding.finish(state)

    outs = pl.pallas_call(
        body, name=name, grid=(ni, nk),
        out_shape=(*out_shape, *riding.out_shape),
        in_specs=[*in_specs, *riding.specs], out_specs=(*out_specs, *riding.specs),
        scratch_shapes=([pltpu.VMEM((tm, n), F32)] if nk > 1 else []) + (riding.scratch if nr else []),
        compiler_params=_params(("arbitrary", "arbitrary"), VMEM_BIG),
    )(*args, *riding.arrays)
    n_own = len(out_shape)
    return list(outs[:n_own]), list(outs[n_own:])


def _zero_sums_at_start(sums_ref, first):
    if first is not None:
        @pl.when(first)
        def _():
            sums_ref[...] = jnp.zeros_like(sums_ref)


def _epi_resid_modulate(acc, rows, vecs, outs, sums_ref, first, last):
    (x_ref,), (g_ref, sh_ref, sc_ref) = rows, vecs
    x1 = x_ref[...] + g_ref[...] * acc
    outs[0][...] = acc
    outs[1][...] = x1
    outs[2][...] = (x1 * _rstd(x1) * (1.0 + sc_ref[...]) + sh_ref[...]).astype(BF16)


def _epi_final(acc, rows, vecs, outs, sums_ref, first, last):
    (x1_ref, t_ref), (g_ref, gf_ref) = rows, vecs
    d = acc.shape[1]
    x2 = x1_ref[...] + g_ref[...] * acc
    r = _rstd(x2)
    xn = x2 * r
    err = xn * gf_ref[...] - t_ref[...]
    dy = err * (1.0 / d)
    dx2 = _norm_bwd(dy * gf_ref[...], xn, r)
    outs[0][...] = dx2
    outs[1][...] = (dx2 * g_ref[...]).astype(BF16)
    _zero_sums_at_start(sums_ref, first)
    sums_ref[0:1, :] += jnp.sum(dy * xn, axis=0, keepdims=True)
    sums_ref[1:2, :] += jnp.sum(dx2 * acc, axis=0, keepdims=True)
    sums_ref[2:3, :] += jnp.sum(err * err, axis=0, keepdims=True)

    if last is not None:
        @pl.when(last)
        def _():
            tot = jnp.sum(sums_ref[2:3, :], axis=1, keepdims=True) * (0.5 / d)
            sums_ref[3:4, :] = jnp.broadcast_to(tot, (1, d))


def _epi_modulate2_bwd(acc, rows, vecs, outs, sums_ref, first, last):
    (x_ref, dres_ref, o_ref), (sc_ref, g_ref) = rows, vecs
    x = x_ref[...]
    r = _rstd(x)
    xn = x * r
    dx = dres_ref[...] + _norm_bwd(acc * (1.0 + sc_ref[...]), xn, r)
    outs[0][...] = dx
    outs[1][...] = (dx * g_ref[...]).astype(BF16)
    _zero_sums_at_start(sums_ref, first)
    sums_ref[0:1, :] += jnp.sum(acc * xn, axis=0, keepdims=True)
    sums_ref[1:2, :] += jnp.sum(acc, axis=0, keepdims=True)
    sums_ref[2:3, :] += jnp.sum(dx * o_ref[...], axis=0, keepdims=True)


def _epi_modulate1_bwd(acc, rows, vecs, outs, sums_ref, first, last):
    (add_ref, x_ref, dres_ref), (sc_ref,) = rows, vecs
    dh = acc + add_ref[...]
    x = x_ref[...]
    r = _rstd(x)
    xn = x * r
    outs[0][...] = dres_ref[...] + _norm_bwd(dh * (1.0 + sc_ref[...]), xn, r)
    _zero_sums_at_start(sums_ref, first)
    sums_ref[0:1, :] += jnp.sum(dh * xn, axis=0, keepdims=True)
    sums_ref[1:2, :] += jnp.sum(dh, axis=0, keepdims=True)


def _modulate_all(x, ctx, mod, mod_ctx, riding, name):
    s, d = x.shape
    t = s + ctx.shape[0]
    ns = s // ROW_TILE
    nc = ctx.shape[0] // ROW_TILE
    nr = riding.n

    def body(*refs):
        x_ref, c_ref, sh_ref, sc_ref, shc_ref, scc_ref = refs[:6]
        h_ref = refs[6 + nr]
        i = pl.program_id(0)
        state = riding.run(i == 0, i == ns + nc - 1, refs[6:6 + nr], refs[7 + nr:7 + 2 * nr], refs[7 + 2 * nr:],
                           middle=i == ns + nc - 3)

        @pl.when(i < ns)
        def _():
            v = x_ref[...]
            h_ref[...] = (v * _rstd(v) * (1.0 + sc_ref[...]) + sh_ref[...]).astype(BF16)

        @pl.when(i >= ns)
        def _():
            v = c_ref[...]
            h_ref[...] = (v * _rstd(v) * (1.0 + scc_ref[...]) + shc_ref[...]).astype(BF16)

        riding.finish(state)

    outs = pl.pallas_call(
        body, name=name, grid=(ns + nc,),
        out_shape=(jax.ShapeDtypeStruct((t, d), BF16), *riding.out_shape),
        in_specs=[pl.BlockSpec((ROW_TILE, d), lambda i: (jnp.minimum(i, ns - 1), 0)),
                  pl.BlockSpec((ROW_TILE, d), lambda i: (jnp.maximum(i - ns, 0), 0)),
                  _vec(0), _vec(1), _vec(0), _vec(1), *riding.specs],
        out_specs=(pl.BlockSpec((ROW_TILE, d), lambda i: (i, 0)), *riding.specs),
        scratch_shapes=riding.scratch,
        compiler_params=_params(("arbitrary",)),
    )(x, ctx, mod, mod, mod_ctx, mod_ctx, *riding.arrays)
    return outs[0], list(outs[1:])


def _modulate_sums(dh, row_off, xsrc):
    s, d = xsrc.shape

    def body(dh_ref, x_ref, sums_ref):
        i = pl.program_id(0)
        x = x_ref[...]
        dhv = dh_ref[...]

        @pl.when(i == 0)
        def _():
            sums_ref[...] = jnp.zeros_like(sums_ref)

        sums_ref[0:1, :] += jnp.sum(dhv * (x * _rstd(x)), axis=0, keepdims=True)
        sums_ref[1:2, :] += jnp.sum(dhv, axis=0, keepdims=True)

    return pl.pallas_call(
        body, name="modulate1_ctx_bwd", grid=(s // ROW_TILE,),
        out_shape=jax.ShapeDtypeStruct((8, d), F32),
        in_specs=[pl.BlockSpec((ROW_TILE, d), lambda i: (i + row_off, 0)), pl.BlockSpec((ROW_TILE, d), lambda i: (i, 0))],
        out_specs=pl.BlockSpec((8, d), lambda i: (0, 0)),
        compiler_params=_params(("arbitrary",)),
    )(dh, xsrc)


def _head_fwd(h_all, win_head, wq, wk, q_gain, kv_gain, cos, sgn, tm, name):
    t, d = h_all.shape
    nq, nkv = wq.shape[1], wk.shape[1]

    def body(h_ref, wi_ref, wq_ref, wk_ref, qg_ref, kg_ref, c_ref, s_ref, z_ref, cq_ref, kvin_ref, qf_ref, kv_ref):
        z = lax.dot_general(h_ref[...], wi_ref[...], NT_DIMS, preferred_element_type=F32)
        z_ref[...] = z
        cos, sgn = c_ref[...], s_ref[...]
        zq = z[:, 0:Q_RANK]
        cq = (zq * _rstd(zq) * qg_ref[...]).astype(BF16)
        cq_ref[...] = cq
        zk = z[:, Q_RANK:Q_RANK + KV_RANK]
        kv_in = jnp.concatenate([(zk * _rstd(zk) * kg_ref[...]).astype(BF16),
                                 _rope(z[:, Q_RANK + KV_RANK:HEAD_COLS], cos, sgn, False).astype(BF16)], axis=1)
        kvin_ref[...] = kv_in
        q = jnp.dot(cq, wq_ref[...], preferred_element_type=F32)
        for h in range(nq // LANES):
            sl = slice(h * LANES, (h + 1) * LANES)
            qf_ref[:, sl] = _rope(q[:, sl], cos, sgn, False).astype(BF16)
        kv_ref[...] = jnp.dot(kv_in, wk_ref[...], preferred_element_type=F32).astype(BF16)

    def row(w):
        return pl.BlockSpec((tm, w), lambda i: (i, 0))

    def whole(a):
        return pl.BlockSpec(a.shape, lambda i: (0, 0))

    return pl.pallas_call(
        body, name=name, grid=(t // tm,),
        out_shape=(jax.ShapeDtypeStruct((t, HEAD_COLS), F32), jax.ShapeDtypeStruct((t, Q_RANK), BF16),
                   jax.ShapeDtypeStruct((t, KV_RANK + LANES), BF16), jax.ShapeDtypeStruct((t, nq), BF16),
                   jax.ShapeDtypeStruct((t, nkv), BF16)),
        in_specs=[row(d), whole(win_head), whole(wq), whole(wk), whole(q_gain), whole(kv_gain), row(LANES), row(LANES)],
        out_specs=(row(HEAD_COLS), row(Q_RANK), row(KV_RANK + LANES), row(nq), row(nkv)),
        compiler_params=_params(("parallel",), VMEM_BIG),
    )(h_all, win_head, wq, wk, q_gain, kv_gain, cos, sgn)


def _head_bwd(dq, dk, dv, z, wq, wk_k, wk_v, win_head, q_gain, kv_gain, cos, sgn, s, name):
    t = z.shape[0]
    ns = s // ROW_TILE

    def body(dq_ref, dk_ref, dv_ref, z_ref, wq_ref, wkk_ref, wkv_ref, wi_ref, qg_ref, kg_ref, c_ref, s_ref,
             dz_ref, dh_ref, sums_ref):
        i = pl.program_id(0)

        @pl.when(i == 0)
        def _():
            sums_ref[...] = jnp.zeros_like(sums_ref)

        @pl.when(i < ns)
        def _():
            dc = lax.dot_general(dq_ref[...], wq_ref[...], NT_DIMS, preferred_element_type=F32)
            zq = z_ref[:, 0:Q_RANK]
            r = _rstd(zq)
            zn = zq * r
            sums_ref[0:1, :] += jnp.sum(dc * zn, axis=0, keepdims=True)
            dz_ref[:, 0:Q_RANK] = _norm_bwd(dc * qg_ref[...], zn, r).astype(BF16)

        @pl.when(i >= ns)
        def _():
            dz_ref[:, 0:Q_RANK] = jnp.zeros((ROW_TILE, Q_RANK), BF16)

        dkv = (lax.dot_general(dk_ref[...], wkk_ref[...], NT_DIMS, preferred_element_type=F32)
               + lax.dot_general(dv_ref[...], wkv_ref[...], NT_DIMS, preferred_element_type=F32))
        zk = z_ref[:, Q_RANK:Q_RANK + KV_RANK]
        r = _rstd(zk)
        zn = zk * r
        dc = dkv[:, 0:KV_RANK]
        sums_ref[1:2, 0:KV_RANK] += jnp.sum(dc * zn, axis=0, keepdims=True)
        dz_ref[:, Q_RANK:Q_RANK + KV_RANK] = _norm_bwd(dc * kg_ref[...], zn, r).astype(BF16)
        dz_ref[:, Q_RANK + KV_RANK:HEAD_COLS] = _rope(dkv[:, KV_RANK:KV_RANK + LANES], c_ref[...], s_ref[...],
                                                       True).astype(BF16)
        dh_ref[...] = jnp.dot(dz_ref[...], wi_ref[...], preferred_element_type=F32)

    def row(w):
        return pl.BlockSpec((ROW_TILE, w), lambda i: (i, 0))

    def whole(a):
        return pl.BlockSpec(a.shape, lambda i: (0, 0))

    return pl.pallas_call(
        body, name=name, grid=(t // ROW_TILE,),
        out_shape=(jax.ShapeDtypeStruct((t, HEAD_COLS), BF16), jax.ShapeDtypeStruct((t, D_MODEL), F32),
                   jax.ShapeDtypeStruct((8, Q_RANK), F32)),
        in_specs=[pl.BlockSpec((ROW_TILE, dq.shape[1]), lambda i: (jnp.minimum(i, ns - 1), 0)),
                  row(dk.shape[1]), row(dv.shape[1]), row(HEAD_COLS), whole(wq), whole(wk_k), whole(wk_v),
                  whole(win_head), whole(q_gain), whole(kv_gain), row(LANES), row(LANES)],
        out_specs=(row(HEAD_COLS), row(D_MODEL), pl.BlockSpec((8, Q_RANK), lambda i: (0, 0))),
        compiler_params=_params(("arbitrary",), VMEM_BIG),
    )(dq, dk, dv, z, wq, wk_k, wk_v, win_head, q_gain, kv_gain, cos, sgn)


def _shift_rows(u, s):
    rowi = lax.broadcasted_iota(jnp.int32, u.shape, 0)
    prev = jnp.where(rowi == 0, 0.0, pltpu.roll(u, 1, 0))
    nxt = jnp.where(rowi == s - 1, 0.0, pltpu.roll(u, s - 1, 0))
    return prev, nxt


def _conv_fwd(z_conv, cw, a_cat, name):
    s = z_conv.shape[0]

    def body(z_ref, w_ref, a_in_ref, o_ref):
        del a_in_ref
        gb, gc, xv = z_ref[:, 0:LANES], z_ref[:, LANES:2 * LANES], z_ref[:, 2 * LANES:3 * LANES]
        u = gc * xv
        prev, nxt = _shift_rows(u, s)
        y = w_ref[0:1, :] * prev + w_ref[1:2, :] * u + w_ref[2:3, :] * nxt
        o_ref[...] = (gb * y).astype(BF16)

    return pl.pallas_call(
        body, name=name, grid=(CONV_W // LANES,),
        out_shape=jax.ShapeDtypeStruct(a_cat.shape, a_cat.dtype),
        in_specs=[pl.BlockSpec((s, 3 * LANES), lambda j: (0, j)), pl.BlockSpec((3, LANES), lambda j: (0, j)),
                  pl.BlockSpec(memory_space=pl.ANY)],
        out_specs=pl.BlockSpec((s, LANES), lambda j: (0, 4 + j)),
        input_output_aliases={2: 0},
        compiler_params=_params(("parallel",), VMEM_BIG),
    )(z_conv, cw, a_cat)


def _conv_bwd(z_conv, cw, da, name):
    s = z_conv.shape[0]

    def body(z_ref, w_ref, da_ref, dz_ref, dw_ref):
        gb, gc, xv = z_ref[:, 0:LANES], z_ref[:, LANES:2 * LANES], z_ref[:, 2 * LANES:3 * LANES]
        u = gc * xv
        prev, nxt = _shift_rows(u, s)
        dcv = da_ref[...]
        dz_ref[:, 0:LANES] = (dcv * (w_ref[0:1, :] * prev + w_ref[1:2, :] * u + w_ref[2:3, :] * nxt)).astype(BF16)
        dy = dcv * gb
        dw_ref[0:1, :] = jnp.sum(dy * prev, axis=0, keepdims=True)
        dw_ref[1:2, :] = jnp.sum(dy * u, axis=0, keepdims=True)
        dw_ref[2:3, :] = jnp.sum(dy * nxt, axis=0, keepdims=True)
        dyp, dyn = _shift_rows(dy, s)
        du = w_ref[0:1, :] * dyn + w_ref[1:2, :] * dy + w_ref[2:3, :] * dyp
        dz_ref[:, LANES:2 * LANES] = (du * xv).astype(BF16)
        dz_ref[:, 2 * LANES:3 * LANES] = (du * gc).astype(BF16)

    blk = pl.BlockSpec((s, 3 * LANES), lambda j: (0, j))
    cws = pl.BlockSpec((3, LANES), lambda j: (0, j))
    return pl.pallas_call(
        body, name=name, grid=(CONV_W // LANES,),
        out_shape=(jax.ShapeDtypeStruct(z_conv.shape, BF16), jax.ShapeDtypeStruct((3, CONV_W), F32)),
        in_specs=[blk, cws, pl.BlockSpec((s, LANES), lambda j: (0, 4 + j))], out_specs=(blk, cws),
        compiler_params=_params(("parallel",), VMEM_BIG),
    )(z_conv, cw, da)


ATT_TQ = 512
ATT_Q_STEP = 1024
ATT_TQ_BWD = 512


def _head_mask(shape, hh):
    lane = lax.broadcasted_iota(jnp.int32, shape, 1)
    return (lane >= hh * V_DIM) & (lane < (hh + 1) * V_DIM)


def _attn_fwd(qf, kv, s, riding, name):
    t = kv.shape[0]
    step = min(ATT_Q_STEP, s)
    nq = s // step
    nr = riding.n

    def body(*refs):
        q_ref, k_ref, v_ref = refs[:3]
        o_ref, ob_ref, st_ref = refs[3 + nr:6 + nr]
        p, i = pl.program_id(0), pl.program_id(1)
        state = riding.run((p == 0) & (i == 0), (p == N_HEADS // 2 - 1) & (i == nq - 1),
                           refs[3:3 + nr], refs[6 + nr:6 + 2 * nr], refs[6 + 2 * nr:])
        v = v_ref[...]
        vlane = lax.broadcasted_iota(jnp.int32, v.shape, 1)
        one_lane = [(1 - hh) * V_DIM for hh in range(2)]
        vm = [jnp.where(_head_mask(v.shape, hh), v, jnp.where(vlane == one_lane[hh], 1.0, 0.0).astype(BF16))
              for hh in range(2)]

        def block(r, carry):
            rows = pl.ds(pl.multiple_of(r * ATT_TQ, ATT_TQ), ATT_TQ)
            olane = lax.broadcasted_iota(jnp.int32, (ATT_TQ, LANES), 1)
            acc = jnp.zeros((ATT_TQ, LANES), F32)
            stat = jnp.zeros((ATT_TQ, LANES), F32)
            scores = [lax.dot_general(q_ref[rows, hh * LANES:(hh + 1) * LANES], k_ref[:, hh * LANES:(hh + 1) * LANES],
                                      NT_DIMS, preferred_element_type=F32) for hh in range(2)]
            maxes = [jnp.max(sc, axis=1, keepdims=True) for sc in scores]
            exps = [jnp.exp2((sc - mx) * EXP2_SCALE).astype(BF16) for sc, mx in zip(scores, maxes)]
            for hh in range(2):
                mx = maxes[hh]
                res = jnp.dot(exps[hh], vm[hh], preferred_element_type=F32)
                den = jnp.sum(jnp.where(olane == one_lane[hh], res, 0.0), axis=1, keepdims=True)
                acc = acc + jnp.where(_head_mask(res.shape, hh), res * (1.0 / den), 0.0)
                stat = stat + jnp.where(olane == hh, mx * EXP2_SCALE + jnp.log(den) * LOG2_E, 0.0)
            o_ref[rows, :] = acc
            ob_ref[rows, :] = acc.astype(BF16)
            st_ref[:, rows] = stat.T[0:8, :]
            return carry

        lax.fori_loop(0, step // ATT_TQ, block, 0)
        riding.finish(state)

    o_spec = pl.BlockSpec((step, LANES), lambda p, i: (i, p))
    outs = pl.pallas_call(
        body, name=name, grid=(N_HEADS // 2, nq),
        out_shape=(jax.ShapeDtypeStruct((s, N_HEADS * V_DIM), F32),
                   jax.ShapeDtypeStruct((s, D_MODEL), BF16),
                   jax.ShapeDtypeStruct((N_HEADS // 2 * 8, s), F32), *riding.out_shape),
        in_specs=[pl.BlockSpec((step, 2 * LANES), lambda p, i: (i, p)),
                  pl.BlockSpec((t, 2 * LANES), lambda p, i: (0, p)),
                  pl.BlockSpec((t, LANES), lambda p, i: (0, N_HEADS + p)), *riding.specs],
        out_specs=(o_spec, o_spec, pl.BlockSpec((8, step), lambda p, i: (p, i)), *riding.specs),
        scratch_shapes=riding.scratch,
        compiler_params=_params(("arbitrary", "arbitrary"), VMEM_BIG),
    )(qf, kv, kv, *riding.arrays)
    return outs[0], outs[1], outs[2], list(outs[3:])


def _attn_bwd(qf, kv, o, da, stats, cos, sgn, riding, name):
    s, t = o.shape[0], kv.shape[0]
    ATT_TQ = ATT_TQ_BWD
    nq = s // ATT_TQ
    nr = riding.n

    def body(*refs):
        q_ref, k_ref, v_ref, o_ref, do_ref, st_ref, c_ref, s_ref = refs[:8]
        dq_ref, dk_ref, dv_ref = refs[8 + nr:11 + nr]
        dk_acc, dv_acc = refs[11 + 2 * nr:13 + 2 * nr]
        p, i = pl.program_id(0), pl.program_id(1)
        state = riding.run((p == 0) & (i == 0), (p == N_HEADS // 2 - 1) & (i == nq - 1),
                           refs[8:8 + nr], refs[11 + nr:11 + 2 * nr], refs[13 + 2 * nr:])

        @pl.when(i == 0)
        def _():
            dk_acc[...] = jnp.zeros_like(dk_acc)
            dv_acc[...] = jnp.zeros_like(dv_acc)

        v = v_ref[...]
        do = do_ref[...]
        od = do * o_ref[...]
        ones = jnp.ones((8, LANES), F32)
        for hh in range(2):
            sl = slice(hh * LANES, (hh + 1) * LANES)
            q, k = q_ref[:, sl], k_ref[:, sl]
            mask = _head_mask(do.shape, hh)
            dom = jnp.where(mask, do, 0.0).astype(BF16)
            delta = lax.dot_general(ones, jnp.where(mask, od, 0.0), NT_DIMS, preferred_element_type=F32,
                                    precision=lax.Precision.HIGHEST)[0:1, :]
            st = lax.dot_general(k, q, NT_DIMS, preferred_element_type=F32)
            pt = jnp.exp2(st * EXP2_SCALE - st_ref[hh:hh + 1, :]).astype(BF16)
            dpt = lax.dot_general(v, dom, NT_DIMS, preferred_element_type=F32)
            dst = (pt.astype(F32) * (dpt - delta)).astype(BF16)
            dv_acc[...] += jnp.dot(pt, dom, preferred_element_type=F32)
            dk_acc[:, sl] += jnp.dot(dst, q, preferred_element_type=F32)
            dq = lax.dot_general(dst, k, TN_DIMS, preferred_element_type=F32) * ATTN_SCALE
            dq_ref[:, sl] = _rope(dq, c_ref[...], s_ref[...], True).astype(BF16)

        @pl.when(i == nq - 1)
        def _():
            dk_ref[...] = (dk_acc[...] * ATTN_SCALE).astype(BF16)
            dv_ref[...] = dv_acc[...].astype(BF16)

        riding.finish(state)

    o_spec = pl.BlockSpec((ATT_TQ, LANES), lambda p, i: (i, p))
    tab = pl.BlockSpec((ATT_TQ, LANES), lambda p, i: (i, 0))
    outs = pl.pallas_call(
        body, name=name, grid=(N_HEADS // 2, nq),
        out_shape=(jax.ShapeDtypeStruct((s, N_HEADS * LANES), BF16),
                   jax.ShapeDtypeStruct((t, N_HEADS * LANES), BF16),
                   jax.ShapeDtypeStruct((t, N_HEADS * V_DIM), BF16), *riding.out_shape),
        in_specs=[pl.BlockSpec((ATT_TQ, 2 * LANES), lambda p, i: (i, p)),
                  pl.BlockSpec((t, 2 * LANES), lambda p, i: (0, p)),
                  pl.BlockSpec((t, LANES), lambda p, i: (0, N_HEADS + p)),
                  o_spec, o_spec,
                  pl.BlockSpec((8, ATT_TQ), lambda p, i: (p, i)), tab, tab, *riding.specs],
        out_specs=(pl.BlockSpec((ATT_TQ, 2 * LANES), lambda p, i: (i, p)),
                   pl.BlockSpec((t, 2 * LANES), lambda p, i: (0, p)),
                   pl.BlockSpec((t, LANES), lambda p, i: (0, p)), *riding.specs),
        scratch_shapes=[pltpu.VMEM((t, 2 * LANES), F32), pltpu.VMEM((t, LANES), F32), *riding.scratch],
        compiler_params=_params(("arbitrary", "arbitrary"), VMEM_BIG),
    )(qf, kv, kv, o, da, stats, cos, sgn, *riding.arrays)
    return outs[0], outs[1], outs[2], list(outs[3:])


def _silu(x):
    return x * (1.0 / (1.0 + jnp.exp(-x)))


def _prologue(c_rows, c_ctx, w_mod, b_cols, extra_rows, name):
    d, cols = c_rows.shape[1], w_mod.shape[1]

    def body(c_ref, cctx_ref, wmod_ref, b_ref, x_ref, a_ref, modg_ref, c_all, blk, c_send, c_recv, m_send, m_recv):
        _direct_gather(c_ref, c_all, c_send, c_recv)()
        a_ref[...] = jnp.zeros_like(a_ref)
        for j in range(N_DEV):
            a_ref[j:j + 1, :] = c_all[j, 0:1, :]
        a_ref[N_DEV:N_DEV + 1, :] = cctx_ref[...]
        blk[0:16, :] = jnp.dot(_silu(a_ref[...]), wmod_ref[...], preferred_element_type=F32,
                               precision=lax.Precision.HIGHEST) + b_ref[...]
        blk[16:24, :] = x_ref[...]
        _direct_gather(blk, modg_ref, m_send, m_recv)()

    vmem = pl.BlockSpec(memory_space=pltpu.VMEM)
    return pl.pallas_call(
        body, name=name,
        out_shape=(jax.ShapeDtypeStruct((16, d), F32), jax.ShapeDtypeStruct((N_DEV, 24, cols), F32)),
        in_specs=[vmem] * 5, out_specs=(vmem, vmem),
        scratch_shapes=[pltpu.VMEM((N_DEV, 8, d), F32), pltpu.VMEM((24, cols), F32)]
        + [pltpu.SemaphoreType.DMA((7,)) for _ in range(4)],
        compiler_params=_params(None, VMEM_BIG),
    )(c_rows, c_ctx, w_mod, b_cols, extra_rows)


def _adaln_bwd(a_t, w, d_ex, d_ctx, d_all, name):
    def body(at_ref, w_ref, dex_ref, dctx_ref, dall_ref, gw_ref, dsil_ref, dsum_ref):
        sil_t = _silu(at_ref[...])
        dctx = dctx_ref[...]
        row = dctx[0:1, :]
        for j in range(1, N_DEV):
            row = row + dctx[j:j + 1, :]
        rowi = lax.broadcasted_iota(jnp.int32, dctx.shape, 0)
        ctx_rows = jnp.where(rowi == 0, jnp.broadcast_to(row, dctx.shape), 0.0)
        hi = lax.Precision.HIGHEST
        d_rows = jnp.concatenate([dex_ref[...], ctx_rows], axis=0)
        gw_ref[...] = jnp.dot(sil_t, d_rows, preferred_element_type=F32, precision=hi)
        dsil_ref[...] = lax.dot_general(ctx_rows, w_ref[...], NT_DIMS, preferred_element_type=F32, precision=hi)
        tot = dall_ref[0]
        for j in range(1, N_DEV):
            tot = tot + dall_ref[j]
        dsum_ref[...] = tot

    return pl.pallas_call(
        body, name=name,
        out_shape=(jax.ShapeDtypeStruct(w.shape, F32), jax.ShapeDtypeStruct((8, w.shape[0]), F32),
                   jax.ShapeDtypeStruct(d_all.shape[1:], F32)),
        compiler_params=_params(None, VMEM_BIG),
    )(a_t, w, d_ex, d_ctx, d_all)


def _pack_small(sums1, sums2, fsums, sums1c, psums, d_cw, name):
    d = D_MODEL

    def body(s1_ref, s2_ref, f_ref, s1c_ref, p_ref, cw_ref, o_ref):
        o_ref[...] = jnp.zeros_like(o_ref)
        for col, (ref, r) in enumerate([(s1_ref, 1), (s1_ref, 0), (s2_ref, 2), (s2_ref, 1), (s2_ref, 0), (f_ref, 1)]):
            o_ref[0:1, col * d:(col + 1) * d] = ref[r:r + 1, :]
        o_ref[1:2, 0:d] = s1c_ref[1:2, :]
        o_ref[1:2, d:2 * d] = s1c_ref[0:1, :]
        o_ref[2:3, 0:Q_RANK] = p_ref[0:1, :]
        o_ref[2:3, Q_RANK:Q_RANK + KV_RANK] = p_ref[1:2, 0:KV_RANK]
        o_ref[2:3, Q_RANK + KV_RANK:Q_RANK + KV_RANK + d] = f_ref[0:1, :]
        for r in range(3):
            o_ref[3 + r:4 + r, 0:CONV_W] = cw_ref[r:r + 1, :]
        o_ref[6:7, 0:d] = f_ref[3:4, :]

    return pl.pallas_call(body, name=name, out_shape=jax.ShapeDtypeStruct((8, 6 * d), F32))(
        sums1, sums2, fsums, sums1c, psums, d_cw)


def _adam_math(w, g, m, v):
    nm = ADAM_B1 * m + (1.0 - ADAM_B1) * g
    nv = ADAM_B2 * v + (1.0 - ADAM_B2) * (g * g)
    m_hat = nm / (1.0 - ADAM_B1 ** ADAM_STEP)
    v_hat = nv / (1.0 - ADAM_B2 ** ADAM_STEP)
    return -ADAM_LR * (m_hat / (jnp.sqrt(v_hat) + ADAM_EPS) + ADAM_WD * w), nm, nv


def _small_update(dsum, dsil_all, g_cw, params, name):
    d = D_MODEL
    n = len(params)

    def body(*refs):
        dsum_ref, dsil_ref, gcw_ref = refs[:3]
        wmv = refs[3:3 + 3 * n]
        outs = refs[3 + 3 * n:]
        tot = dsil_ref[0]
        for j in range(1, N_DEV):
            tot = tot + dsil_ref[j]
        cv = wmv[0][...]
        sg = 1.0 / (1.0 + jnp.exp(-cv))
        off = Q_RANK + KV_RANK
        grads = [tot[0:1, :] * (sg * (1.0 + cv * (1.0 - sg))),
                 dsum_ref[0:1, :] + dsum_ref[1:2, :],
                 dsum_ref[2:3, 0:Q_RANK], dsum_ref[2:3, Q_RANK:off], dsum_ref[2:3, off:off + d],
                 gcw_ref[...]]
        for p, g in enumerate(grads):
            w_ref, m_ref, v_ref = wmv[3 * p:3 * p + 3]
            at = 0 if len(w_ref.shape) == 3 else Ellipsis
            res = (g,) + _adam_math(w_ref[at], g, m_ref[at], v_ref[at])
            for q, val in enumerate(res):
                outs[4 * p + q][at] = val

    flat = [a for wmv in params for a in wmv]
    out_shape = tuple(jax.ShapeDtypeStruct(wmv[0].shape, F32) for wmv in params for _ in range(4))
    outs = pl.pallas_call(body, name=name, out_shape=out_shape)(dsum, dsil_all, g_cw, *flat)
    return [outs[4 * p:4 * p + 4] for p in range(n)]


def _adamw(w, g, m, v, name, slots=False):
    _, rows, cols = w.shape
    tr = _pick(rows, (256, 128, 64, 32, 16, 8))

    def body(w_ref, g_ref, m_ref, v_ref, *outs):
        if slots:
            gv = g_ref[0].astype(F32)
            for j in range(1, g.shape[0]):
                gv = gv + g_ref[j].astype(F32)
            outs[0][...] = gv
        else:
            gv = g_ref[...]
        d_ref, nm_ref, nv_ref = outs[-3:]
        d_ref[...], nm_ref[...], nv_ref[...] = _adam_math(w_ref[...], gv, m_ref[...], v_ref[...])

    blk = pl.BlockSpec((None, tr, cols), lambda i: (0, i, 0))
    g_spec = (pl.BlockSpec((g.shape[0], tr, cols), lambda i: (0, i, 0)) if slots
              else pl.BlockSpec((tr, cols), lambda i: (i, 0)))
    sh = jax.ShapeDtypeStruct((1, rows, cols), F32)
    n_out = 4 if slots else 3
    return pl.pallas_call(
        body, name=name, grid=(rows // tr,), out_shape=(sh,) * n_out,
        in_specs=[blk, g_spec, blk, blk], out_specs=(blk,) * n_out,
        compiler_params=_params(("parallel",), VMEM_BIG),
    )(w, g, m, v)


def _rope_tables(s, l):
    tok = np.arange(s)
    row = (tok // GRID_W).astype(np.float32)
    col = (tok % GRID_W).astype(np.float32)
    half = QK_ROPE // 2
    freqs = np.float32(ROPE_THETA) ** (-np.arange(0, half, 2, dtype=np.float32) / np.float32(half))
    dd = np.arange(QK_ROPE)
    pos = np.where((dd // half)[None, :] == 0, row[:, None], col[:, None]).astype(np.float32)
    ang = (pos * freqs[dd % (half // 2)][None, :]).astype(np.float32)
    sin = np.sin(ang).astype(np.float32)
    cos_t = np.ones((s + l, LANES), np.float32)
    sgn_t = np.zeros((s + l, LANES), np.float32)
    cos_t[:s, QK_NOPE:QK_NOPE + QK_ROPE] = np.cos(ang)
    sgn_t[:s, QK_NOPE:QK_NOPE + QK_ROPE] = np.where(((dd % half) // (half // 2))[None, :] == 0, -sin, sin)
    return jnp.asarray(cos_t), jnp.asarray(sgn_t)


def _slots_to_cols(g):
    return g.transpose(1, 0, 2).reshape(g.shape[1], N_DEV * g.shape[2])


def _cols_to_slots(w):
    return w.reshape(w.shape[0], N_DEV, w.shape[1] // N_DEV).transpose(1, 0, 2)


def _unpack_small_weights(g_in_t, g_uq, g_ukv):
    w_t = g_in_t.reshape(N_DEV * g_in_t.shape[1], D_MODEL)
    zeros = jnp.zeros((QK_NOPE, D_MODEL), BF16)
    win_head_t = jnp.concatenate([w_t[:Q_RANK + KV_RANK], zeros, w_t[Q_RANK + KV_RANK:MLA_IN],
                                  zeros[:LANES - QK_NOPE - QK_ROPE]], axis=0)
    win_conv_t = w_t[MLA_IN:].reshape(3, CONV_W // LANES, LANES, D_MODEL).transpose(1, 0, 2, 3)
    win_conv_t = win_conv_t.reshape(3 * CONV_W, D_MODEL)
    w_uq = _slots_to_cols(g_uq).reshape(Q_RANK, N_HEADS, QK_NOPE + QK_ROPE)
    wq = jnp.pad(w_uq, ((0, 0), (0, 0), (0, LANES - QK_NOPE - QK_ROPE))).reshape(Q_RANK, N_HEADS * LANES)
    w_ukv = _slots_to_cols(g_ukv).reshape(KV_RANK, N_HEADS, QK_NOPE + V_DIM)
    k_top = jnp.pad(w_ukv[:, :, :QK_NOPE], ((0, 0), (0, 0), (0, LANES - QK_NOPE))).reshape(KV_RANK, N_HEADS * LANES)
    v_top = w_ukv[:, :, QK_NOPE:].reshape(KV_RANK, N_HEADS * V_DIM)
    eye = jnp.pad(jnp.eye(QK_ROPE, dtype=BF16), ((QK_NOPE, LANES - QK_NOPE - QK_ROPE),) * 2)
    wk = jnp.concatenate([
        jnp.concatenate([k_top, v_top], axis=1),
        jnp.concatenate([jnp.tile(eye, (1, N_HEADS)), jnp.zeros((LANES, N_HEADS * V_DIM), BF16)], axis=1)], axis=0)
    return win_head_t, win_conv_t, wq, wk


def _pack_small_grads(d_head_t, d_conv_t, d_wq, d_wkk, d_wkv):
    d_conv_t = d_conv_t.reshape(CONV_W // LANES, 3, LANES, D_MODEL).transpose(1, 0, 2, 3).reshape(3 * CONV_W, D_MODEL)
    rope0 = Q_RANK + KV_RANK + QK_NOPE
    g_in_t = jnp.concatenate([d_head_t[:Q_RANK + KV_RANK], d_head_t[rope0:rope0 + QK_ROPE], d_conv_t], axis=0)
    g_in_t = g_in_t.reshape(N_DEV, -1, D_MODEL).astype(BF16)
    g_uq = d_wq.reshape(Q_RANK, N_HEADS, LANES)[:, :, :QK_NOPE + QK_ROPE].reshape(Q_RANK, -1)
    g_kn = d_wkk[:KV_RANK].reshape(KV_RANK, N_HEADS, LANES)[:, :, :QK_NOPE]
    g_v = d_wkv[:KV_RANK].reshape(KV_RANK, N_HEADS, V_DIM)
    g_ukv = jnp.concatenate([g_kn, g_v], axis=2).reshape(KV_RANK, -1)
    return [g_in_t] + [_cols_to_slots(g).astype(BF16) for g in (g_uq, g_ukv)]


def kernel(x, c, ctx, c_ctx, w_mod, b_mod, w_in, q_norm_g, w_uq, kv_norm_g, w_ukv, conv_w, w_out, w_mlp1, w_mlp2, final_norm_g, loss_target, m_c_ctx, m_w_mod, m_b_mod, m_w_in, m_q_norm_g, m_w_uq, m_kv_norm_g, m_w_ukv, m_conv_w, m_w_out, m_w_mlp1, m_w_mlp2, m_final_norm_g, v_c_ctx, v_w_mod, v_b_mod, v_w_in, v_q_norm_g, v_w_uq, v_kv_norm_g, v_w_ukv, v_conv_w, v_w_out, v_w_mlp1, v_w_mlp2, v_final_norm_g):
    me = _my_index()
    x2d, ctx2d, tgt = x[0], ctx[0], loss_target[0]
    s, l = x2d.shape[0], ctx2d.shape[0]
    t = s + l
    d = D_MODEL
    mod_cols = w_mod.shape[2]
    cw_cols = conv_w.shape[2]

    b_cols = lax.dynamic_slice(b_mod, (0, me * mod_cols), (1, mod_cols))
    cw_blk = jnp.pad(conv_w[0], ((0, 5), (0, mod_cols - cw_cols)))
    a_rows, gathered = _prologue(jnp.pad(c, ((0, 7), (0, 0))), c_ctx[None, :], w_mod[0], b_cols, cw_blk,
                                 "prologue")
    mod_mine = lax.dynamic_index_in_dim(gathered, me, axis=1, keepdims=False).reshape(1, 6 * d)
    mod_ctx = gathered[:, 8, :].reshape(1, 6 * d)
    cw_full = gathered[:, 16:19, :cw_cols].transpose(1, 0, 2).reshape(3, CONV_W)

    early = [w.astype(BF16) for w in (w_in[0].T, w_uq[0], w_ukv[0])]
    late = [w.astype(BF16) for w in (w_out[0], w_mlp1[0], w_mlp2[0])]
    h_all, (g_in, g_uq, g_ukv) = _modulate_all(x2d, ctx2d, mod_mine, mod_ctx, _RidingGather(early),
                                               "modulate1")
    win_head, win_conv, wq, wk = _unpack_small_weights(g_in, g_uq, g_ukv)
    wk_k, wk_v = wk[:, :N_HEADS * LANES], wk[:, N_HEADS * LANES:]
    cos, sgn = _rope_tables(s, l)

    tm_t = _pick(t, (1088, 768, 256))
    tk_t = _pick(t, (2176, 768, 256))
    z_head, cq, kv_in, qf, kv = _head_fwd(h_all, win_head, wq, wk, q_norm_g, kv_norm_g, cos, sgn, tm_t, "head_fwd")
    z_conv = _matmul(h_all, win_conv, mode="nt", name="in_proj_conv", m=s, tm=1024, tn=1536, tk=1024)
    attn, a_cat, stats, (g_out, w1, g_w2) = _attn_fwd(qf, kv, s, _Riding("gather", late), "attn_fwd")
    wo = g_out.reshape(d, d)
    w2 = g_w2.reshape(D_FF, d)
    a_cat = _conv_fwd(z_conv, cw_full, a_cat, "conv_fwd")
    (o, x1, h2), _ = _matmul_rows(a_cat, wo, _epi_resid_modulate, mode="nn", name="out_proj", tm=1024, tk=1024,
                                  rows=[x2d], vecs=[(mod_mine, 2), (mod_mine, 3), (mod_mine, 4)],
                                  out_dtypes=[F32, F32, BF16])
    u1, act = _matmul(h2, w1, mode="nn", name="mlp_up", tm=4096, tk=1024, epilogue="relu2", slots="b_cols")
    (dx2, dm, fsums), _ = _matmul_rows(act, w2, _epi_final, mode="nn", name="mlp_down", tm=512, tk=4096,
                                       rows=[x1, tgt], vecs=[(mod_mine, 5), (final_norm_g[None, :], 0)],
                                       out_dtypes=[F32, BF16], sums=True)

    d_w2 = _matmul(act, dm, mode="tn", name="d_w_mlp2", out_dtype=BF16, tm=1024, tn=1024, tk=4096)
    du1 = _matmul(dm, w2, mode="nt", name="d_act", out_dtype=BF16, tm=2048, tn=1024, tk=1024,
                  epilogue="drelu2", extra=(u1,))
    d_w1 = _matmul(h2, du1, mode="tn", name="d_w_mlp1", out_dtype=BF16, tm=1024, tk=4096, slots="out")
    (dx1, do, sums2), _ = _matmul_rows(du1, w1, _epi_modulate2_bwd, mode="nt", name="d_h2", tm=512, tk=4096,
                                       slots="b_contract", rows=[x1, dx2, o], vecs=[(mod_mine, 4), (mod_mine, 2)],
                                       out_dtypes=[F32, BF16], sums=True)
    d_wo = _matmul(a_cat, do, mode="tn", name="d_w_out", out_dtype=BF16, tm=1024, tn=1024, tk=2048)
    da = _matmul(do, wo, mode="nt", name="d_a", tm=1024, tn=1024, tk=1024)
    dz_conv, d_cw = _conv_bwd(z_conv, cw_full, da, "conv_bwd")
    ready = [d_wo.reshape(N_DEV, d // N_DEV, d), d_w1, d_w2.reshape(N_DEV, D_FF // N_DEV, d)]
    dq, dk, dv, rode = _attn_bwd(qf, kv, attn, da, stats, cos, sgn, _Riding("exchange", ready), "attn_bwd")
    d_wq = _matmul(cq, dq, mode="tn", name="d_w_uq", k=s, tm=256, tn=1024, tk=4096)
    d_wkk = _matmul(kv_in, dk, mode="tn", name="d_w_ukv_k", tm=256, tn=1024, tk=tk_t)
    d_wkv = _matmul(kv_in, dv, mode="tn", name="d_w_ukv_v", tm=256, tn=512, tk=tk_t)
    dz_head, dh_head, psums = _head_bwd(dq, dk, dv, z_head, wq, wk_k, wk_v, win_head, q_norm_g, kv_norm_g, cos, sgn, s,
                                        "head_bwd")
    d_head = _matmul(dz_head, h_all, mode="tn", name="d_w_in_head", tm=512, tn=1024, tk=tk_t)
    d_conv = _matmul(dz_conv, h_all, mode="tn", name="d_w_in_conv", k=s, tm=1536, tn=1024, tk=2048)
    send = _pack_small_grads(d_head, d_conv, d_wq, d_wkk, d_wkv)
    (grad_x, sums1), got = _matmul_rows(dz_conv, win_conv, _epi_modulate1_bwd, mode="nn", name="d_h1", tm=s // 4,
                                        tk=win_conv.shape[0], rows=[dh_head, x2d, dx1], vecs=[(mod_mine, 1)],
                                        out_dtypes=[F32], sums=True, riding=_RidingReduce(send))
    sums1c = _modulate_sums(dh_head, s // ROW_TILE, ctx2d)

    small = _pack_small(sums1, sums2, fsums, sums1c, psums, d_cw, "pack_small")
    (d_all,) = _all_gather([small], "gather_small_grads", True)
    d_cols = lax.dynamic_slice_in_dim(d_all, me * mod_cols, mod_cols, axis=2)
    g_w_mod, dsil, dsum = _adaln_bwd(a_rows.T, w_mod[0], d_cols[:, 0, :], d_cols[:, 1, :], d_all, "adaln_bwd")
    (dsil_all,) = _all_gather([dsil], "gather_d_cctx", True)
    loss = dsum[6, 0]
    g_cw = lax.dynamic_slice(dsum, (3, me * cw_cols), (3, cw_cols))

    slots = dict(zip(["w_in", "w_uq", "w_ukv"], got))
    slots.update(zip(["w_out", "w_mlp1", "w_mlp2"], rode))

    grads = {}
    weights = {"c_ctx": c_ctx, "w_mod": w_mod, "b_mod": b_mod, "w_in": w_in, "q_norm_g": q_norm_g, "w_uq": w_uq,
               "kv_norm_g": kv_norm_g, "w_ukv": w_ukv, "conv_w": conv_w, "w_out": w_out, "w_mlp1": w_mlp1,
               "w_mlp2": w_mlp2, "final_norm_g": final_norm_g}
    m_in = {"c_ctx": m_c_ctx, "w_mod": m_w_mod, "b_mod": m_b_mod, "w_in": m_w_in, "q_norm_g": m_q_norm_g,
            "w_uq": m_w_uq, "kv_norm_g": m_kv_norm_g, "w_ukv": m_w_ukv, "conv_w": m_conv_w, "w_out": m_w_out,
            "w_mlp1": m_w_mlp1, "w_mlp2": m_w_mlp2, "final_norm_g": m_final_norm_g}
    v_in = {"c_ctx": v_c_ctx, "w_mod": v_w_mod, "b_mod": v_b_mod, "w_in": v_w_in, "q_norm_g": v_q_norm_g,
            "w_uq": v_w_uq, "kv_norm_g": v_kv_norm_g, "w_ukv": v_w_ukv, "conv_w": v_conv_w, "w_out": v_w_out,
            "w_mlp1": v_w_mlp1, "w_mlp2": v_w_mlp2, "final_norm_g": v_final_norm_g}
    names = list(weights)
    small_names = ["c_ctx", "b_mod", "q_norm_g", "kv_norm_g", "final_norm_g", "conv_w"]
    delta, new_m, new_v = {}, {}, {}

    def as_rows(a):
        return a[None, :] if a.ndim == 1 else a

    small_out = _small_update(dsum, dsil_all, g_cw, [[as_rows(src[n]) for src in (weights, m_in, v_in)]
                                                      for n in small_names], "small_update")
    for n, outs in zip(small_names, small_out):
        grads[n], delta[n], new_m[n], new_v[n] = [a.reshape(weights[n].shape) for a in outs]
    for n in names:
        if n in small_names:
            continue
        if n == "w_in":
            wmv = [jnp.swapaxes(src[n], 1, 2) for src in (weights, m_in, v_in)]
            outs = _adamw(wmv[0], slots[n], wmv[1], wmv[2], "adamw_" + n, slots=True)
            grads[n], delta[n], new_m[n], new_v[n] = [jnp.swapaxes(a, 1, 2) for a in outs]
        elif n in slots:
            grads[n], delta[n], new_m[n], new_v[n] = _adamw(weights[n], slots[n], m_in[n], v_in[n], "adamw_" + n,
                                                            slots=True)
        else:
            delta[n], new_m[n], new_v[n] = _adamw(weights[n], g_w_mod, m_in[n], v_in[n], "adamw_" + n)
            grads[n] = g_w_mod[None]

    return (loss, grad_x[None], *[grads[n] for n in names], *[delta[n] for n in names],
            *[new_m[n] for n in names], *[new_v[n] for n in names])
```

```python
import math

import jax
import jax.numpy as jnp
import numpy as np
from jax import lax
from jax.experimental import pallas as pl
from jax.experimental.pallas import tpu as pltpu

F32 = jnp.float32
BF16 = jnp.bfloat16

D_MODEL = 1024
GRID_W = 64
N_HEADS = 8
QK_NOPE = 64
QK_ROPE = 32
V_DIM = 64
Q_RANK = 256
KV_RANK = 128
MLA_IN = Q_RANK + KV_RANK + QK_ROPE
CONV_W = 512
HEAD_COLS = 512
D_FF = 4096
ROPE_THETA = 10000.0
EPS = 1e-6
ATTN_SCALE = 1.0 / math.sqrt(QK_NOPE + QK_ROPE)
LOG2_E = 1.0 / math.log(2.0)
EXP2_SCALE = ATTN_SCALE * LOG2_E
N_DEV = 8
LANES = 128

ADAM_LR, ADAM_B1, ADAM_B2, ADAM_EPS, ADAM_WD, ADAM_STEP = 0.001, 0.9, 0.999, 1e-08, 0.01, 10

ROW_TILE = 256
VMEM_BIG = 60 * 1024 * 1024


def _params(sem=None, vmem=None):
    return pltpu.CompilerParams(dimension_semantics=sem, vmem_limit_bytes=vmem)


def _pick(n, prefs):
    for p in prefs:
        if n % p == 0:
            return p
    return n


def _my_index():
    return 4 * lax.axis_index("x") + 2 * lax.axis_index("y") + lax.axis_index("c")


def _two_level_gather(x_refs, out_refs, send_sems, recv_sems, local_sems):
    n = len(x_refs)
    x, y, c = lax.axis_index("x"), lax.axis_index("y"), lax.axis_index("c")
    me, sibling = (x, y, c), (x, y, 1 - c)
    chips = [(1 - x, y), (x, 1 - y), (1 - x, 1 - y)]

    def slot(a, px, py, pc):
        return out_refs[a].at[4 * px + 2 * py + pc]

    def copy(a, k, block, to, src=None):
        return pltpu.make_async_remote_copy(
            src_ref=slot(a, *block) if src is None else src, dst_ref=slot(a, *block),
            send_sem=send_sems.at[7 * a + k], recv_sem=recv_sems.at[7 * a + k],
            device_id=to, device_id_type=pl.DeviceIdType.MESH)

    mine = [pltpu.make_async_copy(x_refs[a], slot(a, *me), local_sems.at[a]) for a in range(n)]
    first = [cp for a in range(n) for cp in
             [copy(a, 0, me, sibling, src=x_refs[a])]
             + [copy(a, 1 + j, me, (*chip, c), src=x_refs[a]) for j, chip in enumerate(chips)]]
    passed = [[copy(a, 4 + j, (*chip, c), sibling) for j, chip in enumerate(chips)] for a in range(n)]

    def start():
        for cp in mine + first:
            cp.start()

    def forward():
        for a in range(n):
            for j, chip in enumerate(chips):
                copy(a, 1 + j, (*chip, c), me).wait_recv()
                passed[a][j].start()

    def finish():
        for a in range(n):
            copy(a, 0, sibling, me).wait_recv()
            for j, chip in enumerate(chips):
                copy(a, 4 + j, (*chip, 1 - c), me).wait_recv()
        for cp in first + [cp for per_array in passed for cp in per_array]:
            cp.wait_send()
        for cp in mine:
            cp.wait()

    return start, forward, finish


def _direct_gather(src_ref, dst_ref, send_sems, recv_sems):
    x, y, c = lax.axis_index("x"), lax.axis_index("y"), lax.axis_index("c")
    me = 4 * x + 2 * y + c
    dst_ref[me] = src_ref[...]
    sends, landings = [], []
    for k in range(1, N_DEV):
        peer = (1 - x if k & 4 else x, 1 - y if k & 2 else y, 1 - c if k & 1 else c)
        pid = 4 * peer[0] + 2 * peer[1] + peer[2]
        for dst, out in ((me, sends), (pid, landings)):
            out.append(pltpu.make_async_remote_copy(
                src_ref=src_ref, dst_ref=dst_ref.at[dst], send_sem=send_sems.at[k - 1], recv_sem=recv_sems.at[k - 1],
                device_id=peer, device_id_type=pl.DeviceIdType.MESH))
    for cp in sends:
        cp.start()

    def finish():
        for cp in landings:
            cp.wait_recv()
        for cp in sends:
            cp.wait_send()

    return finish


def _all_gather(arrays, name, in_vmem):
    space = pltpu.VMEM if in_vmem else pl.ANY
    n = len(arrays)

    def body(*refs):
        for phase in _two_level_gather(refs[:n], refs[n:2 * n], *refs[2 * n:]):
            phase()

    outs = pl.pallas_call(
        body, name=name,
        out_shape=tuple(jax.ShapeDtypeStruct((N_DEV,) + a.shape, a.dtype) for a in arrays),
        in_specs=[pl.BlockSpec(memory_space=space)] * n,
        out_specs=tuple(pl.BlockSpec(memory_space=space) for _ in arrays),
        scratch_shapes=[pltpu.SemaphoreType.DMA((7 * n,)), pltpu.SemaphoreType.DMA((7 * n,)),
                        pltpu.SemaphoreType.DMA((n,))],
    )(*arrays)
    return list(outs)


class _Riding:
    def __init__(self, kind, arrays):
        self.kind, self.arrays, self.n = kind, list(arrays), len(arrays)
        lead = (N_DEV,) if kind == "gather" else ()
        self.out_shape = [jax.ShapeDtypeStruct(lead + a.shape, a.dtype) for a in self.arrays]
        self.specs = [pl.BlockSpec(memory_space=pl.ANY)] * self.n
        self.scratch = [pltpu.SemaphoreType.DMA((7 * self.n,)), pltpu.SemaphoreType.DMA((7 * self.n,)),
                        pltpu.SemaphoreType.DMA((self.n,))]

    def copies(self, x_refs, y_refs, send_sems, recv_sems, local_sems):
        x, y, c = lax.axis_index("x"), lax.axis_index("y"), lax.axis_index("c")
        me = 4 * x + 2 * y + c
        local, sends, landings = [], [], []
        for a in range(self.n):
            src_mine = x_refs[a] if self.kind == "gather" else x_refs[a].at[me]
            local.append(pltpu.make_async_copy(src_mine, y_refs[a].at[me], local_sems.at[a]))
            for k in range(1, N_DEV):
                peer = (1 - x if k & 4 else x, 1 - y if k & 2 else y, 1 - c if k & 1 else c)
                pid = 4 * peer[0] + 2 * peer[1] + peer[2]
                src = x_refs[a] if self.kind == "gather" else x_refs[a].at[pid]
                for dst, out in ((me, sends), (pid, landings)):
                    out.append(pltpu.make_async_remote_copy(
                        src_ref=src, dst_ref=y_refs[a].at[dst],
                        send_sem=send_sems.at[7 * a + k - 1], recv_sem=recv_sems.at[7 * a + k - 1],
                        device_id=peer, device_id_type=pl.DeviceIdType.MESH))
        return local, sends, landings

    def run(self, first, last, x_refs, y_refs, sems, middle=None):
        if self.n == 0:
            return None
        local, sends, landings = self.copies(x_refs, y_refs, *sems)

        @pl.when(first)
        def _():
            for cp in local + sends:
                cp.start()

        return local, sends, landings, last

    @staticmethod
    def finish(state):
        if state is None:
            return
        local, sends, landings, last = state

        @pl.when(last)
        def _():
            for cp in landings:
                cp.wait_recv()
            for cp in sends:
                cp.wait_send()
            for cp in local:
                cp.wait()


class _RidingGather:
    def __init__(self, arrays):
        self.arrays, self.n = list(arrays), len(arrays)
        self.out_shape = [jax.ShapeDtypeStruct((N_DEV,) + a.shape, a.dtype) for a in self.arrays]
        self.specs = [pl.BlockSpec(memory_space=pl.ANY)] * self.n
        self.scratch = [pltpu.SemaphoreType.DMA((7 * self.n,)), pltpu.SemaphoreType.DMA((7 * self.n,)),
                        pltpu.SemaphoreType.DMA((self.n,))]

    def run(self, first, last, x_refs, y_refs, sems, middle):
        start, forward, finish = _two_level_gather(x_refs, y_refs, *sems)
        pl.when(first)(start)
        pl.when(middle)(forward)
        return finish, last

    @staticmethod
    def finish(state):
        finish, last = state
        pl.when(last)(finish)


class _RidingReduce:
    def __init__(self, arrays):
        self.arrays, self.n = list(arrays), len(arrays)
        self.out_shape = [jax.ShapeDtypeStruct((4,) + a.shape[1:], a.dtype) for a in self.arrays]
        self.specs = [pl.BlockSpec(memory_space=pl.ANY)] * self.n
        self.scratch = [pltpu.VMEM((4,) + a.shape[1:], a.dtype) for a in self.arrays for _ in range(3)]
        self.scratch += [pltpu.SemaphoreType.DMA((self.n,)) for _ in range(6)]

    def run(self, first, last, x_refs, y_refs, scratch, middle):
        n = self.n
        own, sib, tot = scratch[0:3 * n:3], scratch[1:3 * n:3], scratch[2:3 * n:3]
        d2d_send, d2d_recv, local_in, ici_send, ici_recv, local_out = scratch[3 * n:]
        x, y, c = lax.axis_index("x"), lax.axis_index("y"), lax.axis_index("c")
        my_chip = 2 * x + y
        sibling = (x, y, 1 - c)
        others = [(1 - x, y), (x, 1 - y), (1 - x, 1 - y)]

        def to_sibling(a, j=None):
            src = x_refs[a].at[pl.ds(0, 4)] if j is None else x_refs[a].at[2 * j + 1 - c]
            dst = sib[a] if j is None else sib[a].at[j]
            return pltpu.make_async_remote_copy(src_ref=src, dst_ref=dst, send_sem=d2d_send.at[a],
                                                recv_sem=d2d_recv.at[a], device_id=sibling,
                                                device_id_type=pl.DeviceIdType.MESH)

        def mine_in(a, j=None):
            src = x_refs[a].at[pl.ds(0, 4)] if j is None else x_refs[a].at[2 * j + c]
            return pltpu.make_async_copy(src, own[a] if j is None else own[a].at[j], local_in.at[a])

        def to_chip(a, chip=None):
            if chip is None:
                src, dst, peer = tot[a].at[pl.ds(0, 3)], y_refs[a].at[pl.ds(0, 3)], sibling
            else:
                src, dst, peer = tot[a].at[2 * chip[0] + chip[1]], y_refs[a].at[my_chip], (*chip, c)
            return pltpu.make_async_remote_copy(src_ref=src, dst_ref=dst, send_sem=ici_send.at[a],
                                                recv_sem=ici_recv.at[a], device_id=peer,
                                                device_id_type=pl.DeviceIdType.MESH)

        def mine_out(a):
            return pltpu.make_async_copy(tot[a].at[my_chip], y_refs[a].at[my_chip], local_out.at[a])

        @pl.when(first)
        def _():
            for a in range(n):
                for j in range(4):
                    to_sibling(a, j).start()
                    mine_in(a, j).start()

        @pl.when(middle)
        def _():
            for a in range(n):
                to_sibling(a).wait_recv()
                to_sibling(a).wait_send()
                mine_in(a).wait()
                tot[a][...] = (own[a][...].astype(F32) + sib[a][...].astype(F32)).astype(tot[a].dtype)
                for chip in others:
                    to_chip(a, chip).start()
                mine_out(a).start()

        def finish():
            @pl.when(last)
            def _():
                for a in range(n):
                    to_chip(a).wait_recv()
                    to_chip(a).wait_send()
                    mine_out(a).wait()

        return finish

    @staticmethod
    def finish(state):
        state()


_DIMS ={"nn": (((1,), (0,)), ((), ())), "nt": (((1,), (1,)), ((), ())), "tn": (((0,), (0,)), ((), ()))}
NT_DIMS = _DIMS["nt"]
TN_DIMS = _DIMS["tn"]


def _swap8(x):
    lane = lax.broadcasted_iota(jnp.int32, x.shape, 1)
    return jnp.where((lane & 15) < 8, pltpu.roll(x, LANES - 8, 1), pltpu.roll(x, 8, 1))


def _rope(x, cos, sgn, bwd):
    return x * cos + (_swap8(x * sgn) if bwd else _swap8(x) * sgn)


def _matmul(a, b, *, mode, name, out_dtype=F32, tm=512, tn=512, tk=512, m=None, k=None,
            epilogue=None, extra=(), slots=None):
    if mode == "nn":
        m = a.shape[0] if m is None else m
        k = a.shape[1]
        n = N_DEV * b.shape[2] if slots == "b_cols" else b.shape[1]
    elif mode == "nt":
        m = a.shape[0] if m is None else m
        k = a.shape[1]
        n = b.shape[0]
    else:
        k = a.shape[0] if k is None else k
        m, n = a.shape[1], b.shape[1]
    tm, tn, tk = min(tm, m), min(tn, n), min(tk, k)
    if slots == "b_cols":
        tn = b.shape[2]
    if slots == "out":
        tn = n // N_DEV
    assert m % tm == 0 and n % tn == 0 and k % tk == 0, (name, m, n, k, tm, tn, tk)
    nk = k // tk
    dims = _DIMS[mode]
    a_spec = (pl.BlockSpec((tk, tm), lambda i, j, kk: (kk, i)) if mode == "tn"
              else pl.BlockSpec((tm, tk), lambda i, j, kk: (i, kk)))
    if slots == "b_cols":
        b_spec = pl.BlockSpec((None, tk, tn), lambda i, j, kk: (j, kk, 0))
    elif mode == "nt":
        b_spec = pl.BlockSpec((tn, tk), lambda i, j, kk: (j, kk))
    else:
        b_spec = pl.BlockSpec((tk, tn), lambda i, j, kk: (kk, j))
    tile = pl.BlockSpec((tm, tn), lambda i, j, kk: (i, j))
    if slots == "out":
        o_spec = pl.BlockSpec((None, tm, tn), lambda i, j, kk: (j, i, 0))
        o_shape = (N_DEV, m, tn)
    else:
        o_spec, o_shape = tile, (m, n)
    in_specs, args = [a_spec, b_spec], [a, b]
    if epilogue == "drelu2":
        in_specs.append(tile)
    args += list(extra)
    if epilogue == "relu2":
        out_shape = (jax.ShapeDtypeStruct(o_shape, BF16), jax.ShapeDtypeStruct(o_shape, BF16))
        out_specs = (o_spec, o_spec)
    else:
        out_shape = jax.ShapeDtypeStruct(o_shape, out_dtype)
        out_specs = o_spec
    n_in = len(args)
    n_out = 2 if epilogue == "relu2" else 1

    def body(*refs):
        a_ref, b_ref = refs[0], refs[1]
        outs = refs[n_in:n_in + n_out]
        part = lax.dot_general(a_ref[...], b_ref[...], dims, preferred_element_type=F32)

        def finish(acc):
            if epilogue == "relu2":
                outs[0][...] = acc.astype(BF16)
                r = jnp.maximum(acc, 0.0)
                outs[1][...] = (r * r).astype(BF16)
            elif epilogue == "drelu2":
                u = refs[2][...].astype(F32)
                outs[0][...] = (acc * (2.0 * jnp.maximum(u, 0.0))).astype(out_dtype)
            else:
                outs[0][...] = acc.astype(out_dtype)

        if nk == 1:
            finish(part)
        else:
            acc_ref = refs[n_in + n_out]
            kk = pl.program_id(2)

            @pl.when(kk == 0)
            def _():
                acc_ref[...] = part

            @pl.when(kk > 0)
            def _():
                acc_ref[...] += part

            @pl.when(kk == nk - 1)
            def _():
                finish(acc_ref[...])

    return pl.pallas_call(
        body, name=name, grid=(m // tm, n // tn, nk),
        out_shape=out_shape, in_specs=in_specs, out_specs=out_specs,
        scratch_shapes=[pltpu.VMEM((tm, tn), F32)] if nk > 1 else [],
        compiler_params=_params(("parallel", "parallel", "arbitrary"), VMEM_BIG),
    )(*args)


def _rstd(x):
    return lax.rsqrt(jnp.mean(x * x, axis=1, keepdims=True) + EPS)


def _norm_bwd(dxn, xn, r):
    return r * (dxn - xn * jnp.mean(dxn * xn, axis=1, keepdims=True))


def _vec(col):
    return pl.BlockSpec((1, D_MODEL), lambda i: (0, col))


def _matmul_rows(a, b, epi, *, mode, name, tm, tk, rows=(), vecs=(), out_dtypes=(), sums=False, slots=None,
                 riding=None):
    m, k = a.shape
    n = D_MODEL
    tm, tk = min(tm, m), min(tk, k)
    riding = riding or _Riding("gather", [])
    group = 1
    if slots == "b_contract":
        group = max(1, tk // b.shape[2])
        tk = group * b.shape[2]
        b_spec = pl.BlockSpec((group, n, tk // group), lambda i, kk: (kk, 0, 0))
    elif mode == "nt":
        b_spec = pl.BlockSpec((n, tk), lambda i, kk: (0, kk))
    else:
        b_spec = pl.BlockSpec((tk, n), lambda i, kk: (kk, 0))
    assert m % tm == 0 and k % tk == 0, (name, m, k, tm, tk)
    ni, nk = m // tm, k // tk
    assert ni >= 2 or not isinstance(riding, _RidingReduce), "the two-level exchange needs a middle grid step"
    dims = _DIMS[mode]
    tile = pl.BlockSpec((tm, n), lambda i, kk: (i, 0))
    in_specs = [pl.BlockSpec((tm, tk), lambda i, kk: (i, kk)), b_spec] + [tile] * len(rows)
    in_specs += [pl.BlockSpec((1, n), lambda i, kk, col=col: (0, col)) for _, col in vecs]
    args = [a, b, *rows, *[v for v, _ in vecs]]
    out_shape = [jax.ShapeDtypeStruct((m, n), dt) for dt in out_dtypes]
    out_specs = [tile] * len(out_dtypes)
    if sums:
        out_shape.append(jax.ShapeDtypeStruct((8, n), F32))
        out_specs.append(pl.BlockSpec((8, n), lambda i, kk: (0, 0)))
    n_rows, n_vecs, n_outs, nr = len(rows), len(vecs), len(out_dtypes), riding.n
    n_in = 2 + n_rows + n_vecs

    def body(*refs):
        a_ref, b_ref = refs[0], refs[1]
        row_refs = refs[2:2 + n_rows]
        vec_refs = refs[2 + n_rows:n_in]
        x_refs = refs[n_in:n_in + nr]
        out_refs = refs[n_in + nr:n_in + nr + n_outs]
        pos = n_in + nr + n_outs
        sums_ref = refs[pos] if sums else None
        pos += 1 if sums else 0
        y_refs = refs[pos:pos + nr]
        pos += nr
        acc_ref = refs[pos] if nk > 1 else None
        sem_refs = refs[pos + (1 if nk > 1 else 0):]
        i, kk = pl.program_id(0), pl.program_id(1)
        state = riding.run((i == 0) & (kk == 0), (i == ni - 1) & (kk == nk - 1), x_refs, y_refs, sem_refs,
                           middle=(i == 1) & (kk == 0))
        if slots == "b_contract":
            c = tk // group
            part = lax.dot_general(a_ref[:, 0:c], b_ref[0], dims, preferred_element_type=F32)
            for u in range(1, group):
                part = part + lax.dot_general(a_ref[:, u * c:(u + 1) * c], b_ref[u], dims, preferred_element_type=F32)
        else:
            part = lax.dot_general(a_ref[...], b_ref[...], dims, preferred_element_type=F32)

        def finish(acc):
            nsub = tm // ROW_TILE
            for r in range(nsub):
                blk = pl.ds(r * ROW_TILE, ROW_TILE)
                epi(acc[r * ROW_TILE:(r + 1) * ROW_TILE], [ref.at[blk] for ref in row_refs], vec_refs,
                    [ref.at[blk] for ref in out_refs], sums_ref,
                    (i == 0) if r == 0 else None, (i == ni - 1) if r == nsub - 1 else None)

        if nk == 1:
            finish(part)
        else:
            @pl.when(kk == 0)
            def _():
                acc_ref[...] = part

            @pl.when(kk > 0)
            def _():
                acc_ref[...] += part

            @pl.when(kk == nk - 1)
            def _():
                finish(acc_ref)

        riding.finish(state)

    outs = pl.pallas_call(
        body, name=name, grid=(ni, nk),
        out_shape=(*out_shape, *riding.out_shape),
        in_specs=[*in_specs, *riding.specs], out_specs=(*out_specs, *riding.specs),
        scratch_shapes=([pltpu.VMEM((tm, n), F32)] if nk > 1 else []) + (riding.scratch if nr else []),
        compiler_params=_params(("arbitrary", "arbitrary"), VMEM_BIG),
    )(*args, *riding.arrays)
    n_own = len(out_shape)
    return list(outs[:n_own]), list(outs[n_own:])


def _zero_sums_at_start(sums_ref, first):
    if first is not None:
        @pl.when(first)
        def _():
            sums_ref[...] = jnp.zeros_like(sums_ref)


def _epi_resid_modulate(acc, rows, vecs, outs, sums_ref, first, last):
    (x_ref,), (g_ref, sh_ref, sc_ref) = rows, vecs
    x1 = x_ref[...] + g_ref[...] * acc
    outs[0][...] = acc
    outs[1][...] = x1
    outs[2][...] = (x1 * _rstd(x1) * (1.0 + sc_ref[...]) + sh_ref[...]).astype(BF16)


def _epi_final(acc, rows, vecs, outs, sums_ref, first, last):
    (x1_ref, t_ref), (g_ref, gf_ref) = rows, vecs
    d = acc.shape[1]
    x2 = x1_ref[...] + g_ref[...] * acc
    r = _rstd(x2)
    xn = x2 * r
    err = xn * gf_ref[...] - t_ref[...]
    dy = err * (1.0 / d)
    dx2 = _norm_bwd(dy * gf_ref[...], xn, r)
    outs[0][...] = dx2
    outs[1][...] = (dx2 * g_ref[...]).astype(BF16)
    _zero_sums_at_start(sums_ref, first)
    sums_ref[0:1, :] += jnp.sum(dy * xn, axis=0, keepdims=True)
    sums_ref[1:2, :] += jnp.sum(dx2 * acc, axis=0, keepdims=True)
    sums_ref[2:3, :] += jnp.sum(err * err, axis=0, keepdims=True)

    if last is not None:
        @pl.when(last)
        def _():
            tot = jnp.sum(sums_ref[2:3, :], axis=1, keepdims=True) * (0.5 / d)
            sums_ref[3:4, :] = jnp.broadcast_to(tot, (1, d))


def _epi_modulate2_bwd(acc, rows, vecs, outs, sums_ref, first, last):
    (x_ref, dres_ref, o_ref), (sc_ref, g_ref) = rows, vecs
    x = x_ref[...]
    r = _rstd(x)
    xn = x * r
    dx = dres_ref[...] + _norm_bwd(acc * (1.0 + sc_ref[...]), xn, r)
    outs[0][...] = dx
    outs[1][...] = (dx * g_ref[...]).astype(BF16)
    _zero_sums_at_start(sums_ref, first)
    sums_ref[0:1, :] += jnp.sum(acc * xn, axis=0, keepdims=True)
    sums_ref[1:2, :] += jnp.sum(acc, axis=0, keepdims=True)
    sums_ref[2:3, :] += jnp.sum(dx * o_ref[...], axis=0, keepdims=True)


def _epi_modulate1_bwd(acc, rows, vecs, outs, sums_ref, first, last):
    (add_ref, x_ref, dres_ref), (sc_ref,) = rows, vecs
    dh = acc + add_ref[...]
    x = x_ref[...]
    r = _rstd(x)
    xn = x * r
    outs[0][...] = dres_ref[...] + _norm_bwd(dh * (1.0 + sc_ref[...]), xn, r)
    _zero_sums_at_start(sums_ref, first)
    sums_ref[0:1, :] += jnp.sum(dh * xn, axis=0, keepdims=True)
    sums_ref[1:2, :] += jnp.sum(dh, axis=0, keepdims=True)


def _modulate_all(x, ctx, mod, mod_ctx, riding, name):
    s, d = x.shape
    t = s + ctx.shape[0]
    ns = s // ROW_TILE
    nc = ctx.shape[0] // ROW_TILE
    nr = riding.n

    def body(*refs):
        x_ref, c_ref, sh_ref, sc_ref, shc_ref, scc_ref = refs[:6]
        h_ref = refs[6 + nr]
        i = pl.program_id(0)
        state = riding.run(i == 0, i == ns + nc - 1, refs[6:6 + nr], refs[7 + nr:7 + 2 * nr], refs[7 + 2 * nr:],
                           middle=i == ns + nc - 3)

        @pl.when(i < ns)
        def _():
            v = x_ref[...]
            h_ref[...] = (v * _rstd(v) * (1.0 + sc_ref[...]) + sh_ref[...]).astype(BF16)

        @pl.when(i >= ns)
        def _():
            v = c_ref[...]
            h_ref[...] = (v * _rstd(v) * (1.0 + scc_ref[...]) + shc_ref[...]).astype(BF16)

        riding.finish(state)

    outs = pl.pallas_call(
        body, name=name, grid=(ns + nc,),
        out_shape=(jax.ShapeDtypeStruct((t, d), BF16), *riding.out_shape),
        in_specs=[pl.BlockSpec((ROW_TILE, d), lambda i: (jnp.minimum(i, ns - 1), 0)),
                  pl.BlockSpec((ROW_TILE, d), lambda i: (jnp.maximum(i - ns, 0), 0)),
                  _vec(0), _vec(1), _vec(0), _vec(1), *riding.specs],
        out_specs=(pl.BlockSpec((ROW_TILE, d), lambda i: (i, 0)), *riding.specs),
        scratch_shapes=riding.scratch,
        compiler_params=_params(("arbitrary",)),
    )(x, ctx, mod, mod, mod_ctx, mod_ctx, *riding.arrays)
    return outs[0], list(outs[1:])


def _modulate_sums(dh, row_off, xsrc):
    s, d = xsrc.shape

    def body(dh_ref, x_ref, sums_ref):
        i = pl.program_id(0)
        x = x_ref[...]
        dhv = dh_ref[...]

        @pl.when(i == 0)
        def _():
            sums_ref[...] = jnp.zeros_like(sums_ref)

        sums_ref[0:1, :] += jnp.sum(dhv * (x * _rstd(x)), axis=0, keepdims=True)
        sums_ref[1:2, :] += jnp.sum(dhv, axis=0, keepdims=True)

    return pl.pallas_call(
        body, name="modulate1_ctx_bwd", grid=(s // ROW_TILE,),
        out_shape=jax.ShapeDtypeStruct((8, d), F32),
        in_specs=[pl.BlockSpec((ROW_TILE, d), lambda i: (i + row_off, 0)), pl.BlockSpec((ROW_TILE, d), lambda i: (i, 0))],
        out_specs=pl.BlockSpec((8, d), lambda i: (0, 0)),
        compiler_params=_params(("arbitrary",)),
    )(dh, xsrc)


def _head_fwd(h_all, win_head, wq, wk, q_gain, kv_gain, cos, sgn, tm, name):
    t, d = h_all.shape
    nq, nkv = wq.shape[1], wk.shape[1]

    def body(h_ref, wi_ref, wq_ref, wk_ref, qg_ref, kg_ref, c_ref, s_ref, z_ref, cq_ref, kvin_ref, qf_ref, kv_ref):
        z = lax.dot_general(h_ref[...], wi_ref[...], NT_DIMS, preferred_element_type=F32)
        z_ref[...] = z
        cos, sgn = c_ref[...], s_ref[...]
        zq = z[:, 0:Q_RANK]
        cq = (zq * _rstd(zq) * qg_ref[...]).astype(BF16)
        cq_ref[...] = cq
        zk = z[:, Q_RANK:Q_RANK + KV_RANK]
        kv_in = jnp.concatenate([(zk * _rstd(zk) * kg_ref[...]).astype(BF16),
                                 _rope(z[:, Q_RANK + KV_RANK:HEAD_COLS], cos, sgn, False).astype(BF16)], axis=1)
        kvin_ref[...] = kv_in
        q = jnp.dot(cq, wq_ref[...], preferred_element_type=F32)
        for h in range(nq // LANES):
            sl = slice(h * LANES, (h + 1) * LANES)
            qf_ref[:, sl] = _rope(q[:, sl], cos, sgn, False).astype(BF16)
        kv_ref[...] = jnp.dot(kv_in, wk_ref[...], preferred_element_type=F32).astype(BF16)

    def row(w):
        return pl.BlockSpec((tm, w), lambda i: (i, 0))

    def whole(a):
        return pl.BlockSpec(a.shape, lambda i: (0, 0))

    return pl.pallas_call(
        body, name=name, grid=(t // tm,),
        out_shape=(jax.ShapeDtypeStruct((t, HEAD_COLS), F32), jax.ShapeDtypeStruct((t, Q_RANK), BF16),
                   jax.ShapeDtypeStruct((t, KV_RANK + LANES), BF16), jax.ShapeDtypeStruct((t, nq), BF16),
                   jax.ShapeDtypeStruct((t, nkv), BF16)),
        in_specs=[row(d), whole(win_head), whole(wq), whole(wk), whole(q_gain), whole(kv_gain), row(LANES), row(LANES)],
        out_specs=(row(HEAD_COLS), row(Q_RANK), row(KV_RANK + LANES), row(nq), row(nkv)),
        compiler_params=_params(("parallel",), VMEM_BIG),
    )(h_all, win_head, wq, wk, q_gain, kv_gain, cos, sgn)


def _head_bwd(dq, dk, dv, z, wq, wk_k, wk_v, win_head, q_gain, kv_gain, cos, sgn, s, name):
    t = z.shape[0]
    ns = s // ROW_TILE

    def body(dq_ref, dk_ref, dv_ref, z_ref, wq_ref, wkk_ref, wkv_ref, wi_ref, qg_ref, kg_ref, c_ref, s_ref,
             dz_ref, dh_ref, sums_ref):
        i = pl.program_id(0)

        @pl.when(i == 0)
        def _():
            sums_ref[...] = jnp.zeros_like(sums_ref)

        @pl.when(i < ns)
        def _():
            dc = lax.dot_general(dq_ref[...], wq_ref[...], NT_DIMS, preferred_element_type=F32)
            zq = z_ref[:, 0:Q_RANK]
            r = _rstd(zq)
            zn = zq * r
            sums_ref[0:1, :] += jnp.sum(dc * zn, axis=0, keepdims=True)
            dz_ref[:, 0:Q_RANK] = _norm_bwd(dc * qg_ref[...], zn, r).astype(BF16)

        @pl.when(i >= ns)
        def _():
            dz_ref[:, 0:Q_RANK] = jnp.zeros((ROW_TILE, Q_RANK), BF16)

        dkv = (lax.dot_general(dk_ref[...], wkk_ref[...], NT_DIMS, preferred_element_type=F32)
               + lax.dot_general(dv_ref[...], wkv_ref[...], NT_DIMS, preferred_element_type=F32))
        zk = z_ref[:, Q_RANK:Q_RANK + KV_RANK]
        r = _rstd(zk)
        zn = zk * r
        dc = dkv[:, 0:KV_RANK]
        sums_ref[1:2, 0:KV_RANK] += jnp.sum(dc * zn, axis=0, keepdims=True)
        dz_ref[:, Q_RANK:Q_RANK + KV_RANK] = _norm_bwd(dc * kg_ref[...], zn, r).astype(BF16)
        dz_ref[:, Q_RANK + KV_RANK:HEAD_COLS] = _rope(dkv[:, KV_RANK:KV_RANK + LANES], c_ref[...], s_ref[...],
                                                       True).astype(BF16)
        dh_ref[...] = jnp.dot(dz_ref[...], wi_ref[...], preferred_element_type=F32)

    def row(w):
        return pl.BlockSpec((ROW_TILE, w), lambda i: (i, 0))

    def whole(a):
        return pl.BlockSpec(a.shape, lambda i: (0, 0))

    return pl.pallas_call(
        body, name=name, grid=(t // ROW_TILE,),
        out_shape=(jax.ShapeDtypeStruct((t, HEAD_COLS), BF16), jax.ShapeDtypeStruct((t, D_MODEL), F32),
                   jax.ShapeDtypeStruct((8, Q_RANK), F32)),
        in_specs=[pl.BlockSpec((ROW_TILE, dq.shape[1]), lambda i: (jnp.minimum(i, ns - 1), 0)),
                  row(dk.shape[1]), row(dv.shape[1]), row(HEAD_COLS), whole(wq), whole(wk_k), whole(wk_v),
                  whole(win_head), whole(q_gain), whole(kv_gain), row(LANES), row(LANES)],
        out_specs=(row(HEAD_COLS), row(D_MODEL), pl.BlockSpec((8, Q_RANK), lambda i: (0, 0))),
        compiler_params=_params(("arbitrary",), VMEM_BIG),
    )(dq, dk, dv, z, wq, wk_k, wk_v, win_head, q_gain, kv_gain, cos, sgn)


def _shift_rows(u, s):
    rowi = lax.broadcasted_iota(jnp.int32, u.shape, 0)
    prev = jnp.where(rowi == 0, 0.0, pltpu.roll(u, 1, 0))
    nxt = jnp.where(rowi == s - 1, 0.0, pltpu.roll(u, s - 1, 0))
    return prev, nxt


def _conv_fwd(z_conv, cw, a_cat, name):
    s = z_conv.shape[0]

    def body(z_ref, w_ref, a_in_ref, o_ref):
        del a_in_ref
        gb, gc, xv = z_ref[:, 0:LANES], z_ref[:, LANES:2 * LANES], z_ref[:, 2 * LANES:3 * LANES]
        u = gc * xv
        prev, nxt = _shift_rows(u, s)
        y = w_ref[0:1, :] * prev + w_ref[1:2, :] * u + w_ref[2:3, :] * nxt
        o_ref[...] = (gb * y).astype(BF16)

    return pl.pallas_call(
        body, name=name, grid=(CONV_W // LANES,),
        out_shape=jax.ShapeDtypeStruct(a_cat.shape, a_cat.dtype),
        in_specs=[pl.BlockSpec((s, 3 * LANES), lambda j: (0, j)), pl.BlockSpec((3, LANES), lambda j: (0, j)),
                  pl.BlockSpec(memory_space=pl.ANY)],
        out_specs=pl.BlockSpec((s, LANES), lambda j: (0, 4 + j)),
        input_output_aliases={2: 0},
        compiler_params=_params(("parallel",), VMEM_BIG),
    )(z_conv, cw, a_cat)


def _conv_bwd(z_conv, cw, da, name):
    s = z_conv.shape[0]

    def body(z_ref, w_ref, da_ref, dz_ref, dw_ref):
        gb, gc, xv = z_ref[:, 0:LANES], z_ref[:, LANES:2 * LANES], z_ref[:, 2 * LANES:3 * LANES]
        u = gc * xv
        prev, nxt = _shift_rows(u, s)
        dcv = da_ref[...]
        dz_ref[:, 0:LANES] = (dcv * (w_ref[0:1, :] * prev + w_ref[1:2, :] * u + w_ref[2:3, :] * nxt)).astype(BF16)
        dy = dcv * gb
        dw_ref[0:1, :] = jnp.sum(dy * prev, axis=0, keepdims=True)
        dw_ref[1:2, :] = jnp.sum(dy * u, axis=0, keepdims=True)
        dw_ref[2:3, :] = jnp.sum(dy * nxt, axis=0, keepdims=True)
        dyp, dyn = _shift_rows(dy, s)
        du = w_ref[0:1, :] * dyn + w_ref[1:2, :] * dy + w_ref[2:3, :] * dyp
        dz_ref[:, LANES:2 * LANES] = (du * xv).astype(BF16)
        dz_ref[:, 2 * LANES:3 * LANES] = (du * gc).astype(BF16)

    blk = pl.BlockSpec((s, 3 * LANES), lambda j: (0, j))
    cws = pl.BlockSpec((3, LANES), lambda j: (0, j))
    return pl.pallas_call(
        body, name=name, grid=(CONV_W // LANES,),
        out_shape=(jax.ShapeDtypeStruct(z_conv.shape, BF16), jax.ShapeDtypeStruct((3, CONV_W), F32)),
        in_specs=[blk, cws, pl.BlockSpec((s, LANES), lambda j: (0, 4 + j))], out_specs=(blk, cws),
        compiler_params=_params(("parallel",), VMEM_BIG),
    )(z_conv, cw, da)


ATT_TQ = 512
ATT_Q_STEP = 1024
ATT_TQ_BWD = 512


def _head_mask(shape, hh):
    lane = lax.broadcasted_iota(jnp.int32, shape, 1)
    return (lane >= hh * V_DIM) & (lane < (hh + 1) * V_DIM)


def _attn_fwd(qf, kv, s, riding, name):
    t = kv.shape[0]
    step = min(ATT_Q_STEP, s)
    nq = s // step
    nr = riding.n

    def body(*refs):
        q_ref, k_ref, v_ref = refs[:3]
        o_ref, ob_ref, st_ref = refs[3 + nr:6 + nr]
        p, i = pl.program_id(0), pl.program_id(1)
        state = riding.run((p == 0) & (i == 0), (p == N_HEADS // 2 - 1) & (i == nq - 1),
                           refs[3:3 + nr], refs[6 + nr:6 + 2 * nr], refs[6 + 2 * nr:],
                           middle=(p == N_HEADS // 2 - 2) & (i == nq // 2))
        v = v_ref[...]
        vlane = lax.broadcasted_iota(jnp.int32, v.shape, 1)
        one_lane = [(1 - hh) * V_DIM for hh in range(2)]
        vm = [jnp.where(_head_mask(v.shape, hh), v, jnp.where(vlane == one_lane[hh], 1.0, 0.0).astype(BF16))
              for hh in range(2)]

        def block(r, carry):
            rows = pl.ds(pl.multiple_of(r * ATT_TQ, ATT_TQ), ATT_TQ)
            olane = lax.broadcasted_iota(jnp.int32, (ATT_TQ, LANES), 1)
            acc = jnp.zeros((ATT_TQ, LANES), F32)
            stat = jnp.zeros((ATT_TQ, LANES), F32)
            scores = [lax.dot_general(q_ref[rows, hh * LANES:(hh + 1) * LANES], k_ref[:, hh * LANES:(hh + 1) * LANES],
                                      NT_DIMS, preferred_element_type=F32) for hh in range(2)]
            maxes = [jnp.max(sc, axis=1, keepdims=True) for sc in scores]
            exps = [jnp.exp2((sc - mx) * EXP2_SCALE).astype(BF16) for sc, mx in zip(scores, maxes)]
            for hh in range(2):
                mx = maxes[hh]
                res = jnp.dot(exps[hh], vm[hh], preferred_element_type=F32)
                den = jnp.sum(jnp.where(olane == one_lane[hh], res, 0.0), axis=1, keepdims=True)
                acc = acc + jnp.where(_head_mask(res.shape, hh), res * (1.0 / den), 0.0)
                stat = stat + jnp.where(olane == hh, mx * EXP2_SCALE + jnp.log(den) * LOG2_E, 0.0)
            o_ref[rows, :] = acc
            ob_ref[rows, :] = acc.astype(BF16)
            st_ref[:, rows] = stat.T[0:8, :]
            return carry

        lax.fori_loop(0, step // ATT_TQ, block, 0)
        riding.finish(state)

    o_spec = pl.BlockSpec((step, LANES), lambda p, i: (i, p))
    outs = pl.pallas_call(
        body, name=name, grid=(N_HEADS // 2, nq),
        out_shape=(jax.ShapeDtypeStruct((s, N_HEADS * V_DIM), F32),
                   jax.ShapeDtypeStruct((s, D_MODEL), BF16),
                   jax.ShapeDtypeStruct((N_HEADS // 2 * 8, s), F32), *riding.out_shape),
        in_specs=[pl.BlockSpec((step, 2 * LANES), lambda p, i: (i, p)),
                  pl.BlockSpec((t, 2 * LANES), lambda p, i: (0, p)),
                  pl.BlockSpec((t, LANES), lambda p, i: (0, N_HEADS + p)), *riding.specs],
        out_specs=(o_spec, o_spec, pl.BlockSpec((8, step), lambda p, i: (p, i)), *riding.specs),
        scratch_shapes=riding.scratch,
        compiler_params=_params(("arbitrary", "arbitrary"), VMEM_BIG),
    )(qf, kv, kv, *riding.arrays)
    return outs[0], outs[1], outs[2], list(outs[3:])


def _attn_bwd(qf, kv, o, da, stats, cos, sgn, riding, name):
    s, t = o.shape[0], kv.shape[0]
    ATT_TQ = ATT_TQ_BWD
    nq = s // ATT_TQ
    nr = riding.n

    def body(*refs):
        q_ref, k_ref, v_ref, o_ref, do_ref, st_ref, c_ref, s_ref = refs[:8]
        dq_ref, dk_ref, dv_ref = refs[8 + nr:11 + nr]
        dk_acc, dv_acc = refs[11 + 2 * nr:13 + 2 * nr]
        p, i = pl.program_id(0), pl.program_id(1)
        state = riding.run((p == 0) & (i == 0), (p == N_HEADS // 2 - 1) & (i == nq - 1),
                           refs[8:8 + nr], refs[11 + nr:11 + 2 * nr], refs[13 + 2 * nr:])

        @pl.when(i == 0)
        def _():
            dk_acc[...] = jnp.zeros_like(dk_acc)
            dv_acc[...] = jnp.zeros_like(dv_acc)

        v = v_ref[...]
        do = do_ref[...]
        od = do * o_ref[...]
        ones = jnp.ones((8, LANES), F32)
        for hh in range(2):
            sl = slice(hh * LANES, (hh + 1) * LANES)
            q, k = q_ref[:, sl], k_ref[:, sl]
            mask = _head_mask(do.shape, hh)
            dom = jnp.where(mask, do, 0.0).astype(BF16)
            delta = lax.dot_general(ones, jnp.where(mask, od, 0.0), NT_DIMS, preferred_element_type=F32,
                                    precision=lax.Precision.HIGHEST)[0:1, :]
            st = lax.dot_general(k, q, NT_DIMS, preferred_element_type=F32)
            pt = jnp.exp2(st * EXP2_SCALE - st_ref[hh:hh + 1, :]).astype(BF16)
            dpt = lax.dot_general(v, dom, NT_DIMS, preferred_element_type=F32)
            dst = (pt.astype(F32) * (dpt - delta)).astype(BF16)
            dv_acc[...] += jnp.dot(pt, dom, preferred_element_type=F32)
            dk_acc[:, sl] += jnp.dot(dst, q, preferred_element_type=F32)
            dq = lax.dot_general(dst, k, TN_DIMS, preferred_element_type=F32) * ATTN_SCALE
            dq_ref[:, sl] = _rope(dq, c_ref[...], s_ref[...], True).astype(BF16)

        @pl.when(i == nq - 1)
        def _():
            dk_ref[...] = (dk_acc[...] * ATTN_SCALE).astype(BF16)
            dv_ref[...] = dv_acc[...].astype(BF16)

        riding.finish(state)

    o_spec = pl.BlockSpec((ATT_TQ, LANES), lambda p, i: (i, p))
    tab = pl.BlockSpec((ATT_TQ, LANES), lambda p, i: (i, 0))
    outs = pl.pallas_call(
        body, name=name, grid=(N_HEADS // 2, nq),
        out_shape=(jax.ShapeDtypeStruct((s, N_HEADS * LANES), BF16),
                   jax.ShapeDtypeStruct((t, N_HEADS * LANES), BF16),
                   jax.ShapeDtypeStruct((t, N_HEADS * V_DIM), BF16), *riding.out_shape),
        in_specs=[pl.BlockSpec((ATT_TQ, 2 * LANES), lambda p, i: (i, p)),
                  pl.BlockSpec((t, 2 * LANES), lambda p, i: (0, p)),
                  pl.BlockSpec((t, LANES), lambda p, i: (0, N_HEADS + p)),
                  o_spec, o_spec,
                  pl.BlockSpec((8, ATT_TQ), lambda p, i: (p, i)), tab, tab, *riding.specs],
        out_specs=(pl.BlockSpec((ATT_TQ, 2 * LANES), lambda p, i: (i, p)),
                   pl.BlockSpec((t, 2 * LANES), lambda p, i: (0, p)),
                   pl.BlockSpec((t, LANES), lambda p, i: (0, p)), *riding.specs),
        scratch_shapes=[pltpu.VMEM((t, 2 * LANES), F32), pltpu.VMEM((t, LANES), F32), *riding.scratch],
        compiler_params=_params(("arbitrary", "arbitrary"), VMEM_BIG),
    )(qf, kv, kv, o, da, stats, cos, sgn, *riding.arrays)
    return outs[0], outs[1], outs[2], list(outs[3:])


def _silu(x):
    return x * (1.0 / (1.0 + jnp.exp(-x)))


def _prologue(c_rows, c_ctx, w_mod, b_cols, extra_rows, name):
    d, cols = c_rows.shape[1], w_mod.shape[1]

    def body(c_ref, cctx_ref, wmod_ref, b_ref, x_ref, a_ref, modg_ref, c_all, blk, c_send, c_recv, m_send, m_recv):
        _direct_gather(c_ref, c_all, c_send, c_recv)()
        a_ref[...] = jnp.zeros_like(a_ref)
        for j in range(N_DEV):
            a_ref[j:j + 1, :] = c_all[j, 0:1, :]
        a_ref[N_DEV:N_DEV + 1, :] = cctx_ref[...]
        blk[0:16, :] = jnp.dot(_silu(a_ref[...]), wmod_ref[...], preferred_element_type=F32,
                               precision=lax.Precision.HIGHEST) + b_ref[...]
        blk[16:24, :] = x_ref[...]
        _direct_gather(blk, modg_ref, m_send, m_recv)()

    vmem = pl.BlockSpec(memory_space=pltpu.VMEM)
    return pl.pallas_call(
        body, name=name,
        out_shape=(jax.ShapeDtypeStruct((16, d), F32), jax.ShapeDtypeStruct((N_DEV, 24, cols), F32)),
        in_specs=[vmem] * 5, out_specs=(vmem, vmem),
        scratch_shapes=[pltpu.VMEM((N_DEV, 8, d), F32), pltpu.VMEM((24, cols), F32)]
        + [pltpu.SemaphoreType.DMA((7,)) for _ in range(4)],
        compiler_params=_params(None, VMEM_BIG),
    )(c_rows, c_ctx, w_mod, b_cols, extra_rows)


def _adaln_bwd(a_t, w, d_ex, d_ctx, d_all, name):
    def body(at_ref, w_ref, dex_ref, dctx_ref, dall_ref, gw_ref, dsil_ref, dsum_ref):
        sil_t = _silu(at_ref[...])
        dctx = dctx_ref[...]
        row = dctx[0:1, :]
        for j in range(1, N_DEV):
            row = row + dctx[j:j + 1, :]
        rowi = lax.broadcasted_iota(jnp.int32, dctx.shape, 0)
        ctx_rows = jnp.where(rowi == 0, jnp.broadcast_to(row, dctx.shape), 0.0)
        hi = lax.Precision.HIGHEST
        d_rows = jnp.concatenate([dex_ref[...], ctx_rows], axis=0)
        gw_ref[...] = jnp.dot(sil_t, d_rows, preferred_element_type=F32, precision=hi)
        dsil_ref[...] = lax.dot_general(ctx_rows, w_ref[...], NT_DIMS, preferred_element_type=F32, precision=hi)
        tot = dall_ref[0]
        for j in range(1, N_DEV):
            tot = tot + dall_ref[j]
        dsum_ref[...] = tot

    return pl.pallas_call(
        body, name=name,
        out_shape=(jax.ShapeDtypeStruct(w.shape, F32), jax.ShapeDtypeStruct((8, w.shape[0]), F32),
                   jax.ShapeDtypeStruct(d_all.shape[1:], F32)),
        compiler_params=_params(None, VMEM_BIG),
    )(a_t, w, d_ex, d_ctx, d_all)


def _pack_small(sums1, sums2, fsums, sums1c, psums, d_cw, name):
    d = D_MODEL

    def body(s1_ref, s2_ref, f_ref, s1c_ref, p_ref, cw_ref, o_ref):
        o_ref[...] = jnp.zeros_like(o_ref)
        for col, (ref, r) in enumerate([(s1_ref, 1), (s1_ref, 0), (s2_ref, 2), (s2_ref, 1), (s2_ref, 0), (f_ref, 1)]):
            o_ref[0:1, col * d:(col + 1) * d] = ref[r:r + 1, :]
        o_ref[1:2, 0:d] = s1c_ref[1:2, :]
        o_ref[1:2, d:2 * d] = s1c_ref[0:1, :]
        o_ref[2:3, 0:Q_RANK] = p_ref[0:1, :]
        o_ref[2:3, Q_RANK:Q_RANK + KV_RANK] = p_ref[1:2, 0:KV_RANK]
        o_ref[2:3, Q_RANK + KV_RANK:Q_RANK + KV_RANK + d] = f_ref[0:1, :]
        for r in range(3):
            o_ref[3 + r:4 + r, 0:CONV_W] = cw_ref[r:r + 1, :]
        o_ref[6:7, 0:d] = f_ref[3:4, :]

    return pl.pallas_call(body, name=name, out_shape=jax.ShapeDtypeStruct((8, 6 * d), F32))(
        sums1, sums2, fsums, sums1c, psums, d_cw)


def _adam_math(w, g, m, v):
    nm = ADAM_B1 * m + (1.0 - ADAM_B1) * g
    nv = ADAM_B2 * v + (1.0 - ADAM_B2) * (g * g)
    m_hat = nm / (1.0 - ADAM_B1 ** ADAM_STEP)
    v_hat = nv / (1.0 - ADAM_B2 ** ADAM_STEP)
    return -ADAM_LR * (m_hat / (jnp.sqrt(v_hat) + ADAM_EPS) + ADAM_WD * w), nm, nv


def _small_update(dsum, dsil_all, g_cw, params, name):
    d = D_MODEL
    n = len(params)

    def body(*refs):
        dsum_ref, dsil_ref, gcw_ref = refs[:3]
        wmv = refs[3:3 + 3 * n]
        outs = refs[3 + 3 * n:]
        tot = dsil_ref[0]
        for j in range(1, N_DEV):
            tot = tot + dsil_ref[j]
        cv = wmv[0][...]
        sg = 1.0 / (1.0 + jnp.exp(-cv))
        off = Q_RANK + KV_RANK
        grads = [tot[0:1, :] * (sg * (1.0 + cv * (1.0 - sg))),
                 dsum_ref[0:1, :] + dsum_ref[1:2, :],
                 dsum_ref[2:3, 0:Q_RANK], dsum_ref[2:3, Q_RANK:off], dsum_ref[2:3, off:off + d],
                 gcw_ref[...]]
        for p, g in enumerate(grads):
            w_ref, m_ref, v_ref = wmv[3 * p:3 * p + 3]
            at = 0 if len(w_ref.shape) == 3 else Ellipsis
            res = (g,) + _adam_math(w_ref[at], g, m_ref[at], v_ref[at])
            for q, val in enumerate(res):
                outs[4 * p + q][at] = val

    flat = [a for wmv in params for a in wmv]
    out_shape = tuple(jax.ShapeDtypeStruct(wmv[0].shape, F32) for wmv in params for _ in range(4))
    outs = pl.pallas_call(body, name=name, out_shape=out_shape)(dsum, dsil_all, g_cw, *flat)
    return [outs[4 * p:4 * p + 4] for p in range(n)]


def _adamw(w, g, m, v, name, slots=False):
    _, rows, cols = w.shape
    tr = _pick(rows, (256, 128, 64, 32, 16, 8))

    def body(w_ref, g_ref, m_ref, v_ref, *outs):
        if slots:
            gv = g_ref[0].astype(F32)
            for j in range(1, g.shape[0]):
                gv = gv + g_ref[j].astype(F32)
            outs[0][...] = gv
        else:
            gv = g_ref[...]
        d_ref, nm_ref, nv_ref = outs[-3:]
        d_ref[...], nm_ref[...], nv_ref[...] = _adam_math(w_ref[...], gv, m_ref[...], v_ref[...])

    blk = pl.BlockSpec((None, tr, cols), lambda i: (0, i, 0))
    g_spec = (pl.BlockSpec((g.shape[0], tr, cols), lambda i: (0, i, 0)) if slots
              else pl.BlockSpec((tr, cols), lambda i: (i, 0)))
    sh = jax.ShapeDtypeStruct((1, rows, cols), F32)
    n_out = 4 if slots else 3
    return pl.pallas_call(
        body, name=name, grid=(rows // tr,), out_shape=(sh,) * n_out,
        in_specs=[blk, g_spec, blk, blk], out_specs=(blk,) * n_out,
        compiler_params=_params(("parallel",), VMEM_BIG),
    )(w, g, m, v)


def _rope_tables(s, l):
    tok = np.arange(s)
    row = (tok // GRID_W).astype(np.float32)
    col = (tok % GRID_W).astype(np.float32)
    half = QK_ROPE // 2
    freqs = np.float32(ROPE_THETA) ** (-np.arange(0, half, 2, dtype=np.float32) / np.float32(half))
    dd = np.arange(QK_ROPE)
    pos = np.where((dd // half)[None, :] == 0, row[:, None], col[:, None]).astype(np.float32)
    ang = (pos * freqs[dd % (half // 2)][None, :]).astype(np.float32)
    sin = np.sin(ang).astype(np.float32)
    cos_t = np.ones((s + l, LANES), np.float32)
    sgn_t = np.zeros((s + l, LANES), np.float32)
    cos_t[:s, QK_NOPE:QK_NOPE + QK_ROPE] = np.cos(ang)
    sgn_t[:s, QK_NOPE:QK_NOPE + QK_ROPE] = np.where(((dd % half) // (half // 2))[None, :] == 0, -sin, sin)
    return jnp.asarray(cos_t), jnp.asarray(sgn_t)


def _slots_to_cols(g):
    return g.transpose(1, 0, 2).reshape(g.shape[1], N_DEV * g.shape[2])


def _cols_to_slots(w):
    return w.reshape(w.shape[0], N_DEV, w.shape[1] // N_DEV).transpose(1, 0, 2)


def _unpack_small_weights(g_in_t, g_uq, g_ukv):
    w_t = g_in_t.reshape(N_DEV * g_in_t.shape[1], D_MODEL)
    zeros = jnp.zeros((QK_NOPE, D_MODEL), BF16)
    win_head_t = jnp.concatenate([w_t[:Q_RANK + KV_RANK], zeros, w_t[Q_RANK + KV_RANK:MLA_IN],
                                  zeros[:LANES - QK_NOPE - QK_ROPE]], axis=0)
    win_conv_t = w_t[MLA_IN:].reshape(3, CONV_W // LANES, LANES, D_MODEL).transpose(1, 0, 2, 3)
    win_conv_t = win_conv_t.reshape(3 * CONV_W, D_MODEL)
    w_uq = _slots_to_cols(g_uq).reshape(Q_RANK, N_HEADS, QK_NOPE + QK_ROPE)
    wq = jnp.pad(w_uq, ((0, 0), (0, 0), (0, LANES - QK_NOPE - QK_ROPE))).reshape(Q_RANK, N_HEADS * LANES)
    w_ukv = _slots_to_cols(g_ukv).reshape(KV_RANK, N_HEADS, QK_NOPE + V_DIM)
    k_top = jnp.pad(w_ukv[:, :, :QK_NOPE], ((0, 0), (0, 0), (0, LANES - QK_NOPE))).reshape(KV_RANK, N_HEADS * LANES)
    v_top = w_ukv[:, :, QK_NOPE:].reshape(KV_RANK, N_HEADS * V_DIM)
    eye = jnp.pad(jnp.eye(QK_ROPE, dtype=BF16), ((QK_NOPE, LANES - QK_NOPE - QK_ROPE),) * 2)
    wk = jnp.concatenate([
        jnp.concatenate([k_top, v_top], axis=1),
        jnp.concatenate([jnp.tile(eye, (1, N_HEADS)), jnp.zeros((LANES, N_HEADS * V_DIM), BF16)], axis=1)], axis=0)
    return win_head_t, win_conv_t, wq, wk


def _pack_small_grads(d_head_t, d_conv_t, d_wq, d_wkk, d_wkv):
    d_conv_t = d_conv_t.reshape(CONV_W // LANES, 3, LANES, D_MODEL).transpose(1, 0, 2, 3).reshape(3 * CONV_W, D_MODEL)
    rope0 = Q_RANK + KV_RANK + QK_NOPE
    g_in_t = jnp.concatenate([d_head_t[:Q_RANK + KV_RANK], d_head_t[rope0:rope0 + QK_ROPE], d_conv_t], axis=0)
    g_in_t = g_in_t.reshape(N_DEV, -1, D_MODEL).astype(BF16)
    g_uq = d_wq.reshape(Q_RANK, N_HEADS, LANES)[:, :, :QK_NOPE + QK_ROPE].reshape(Q_RANK, -1)
    g_kn = d_wkk[:KV_RANK].reshape(KV_RANK, N_HEADS, LANES)[:, :, :QK_NOPE]
    g_v = d_wkv[:KV_RANK].reshape(KV_RANK, N_HEADS, V_DIM)
    g_ukv = jnp.concatenate([g_kn, g_v], axis=2).reshape(KV_RANK, -1)
    return [g_in_t] + [_cols_to_slots(g).astype(BF16) for g in (g_uq, g_ukv)]


def kernel(x, c, ctx, c_ctx, w_mod, b_mod, w_in, q_norm_g, w_uq, kv_norm_g, w_ukv, conv_w, w_out, w_mlp1, w_mlp2, final_norm_g, loss_target, m_c_ctx, m_w_mod, m_b_mod, m_w_in, m_q_norm_g, m_w_uq, m_kv_norm_g, m_w_ukv, m_conv_w, m_w_out, m_w_mlp1, m_w_mlp2, m_final_norm_g, v_c_ctx, v_w_mod, v_b_mod, v_w_in, v_q_norm_g, v_w_uq, v_kv_norm_g, v_w_ukv, v_conv_w, v_w_out, v_w_mlp1, v_w_mlp2, v_final_norm_g):
    me = _my_index()
    x2d, ctx2d, tgt = x[0], ctx[0], loss_target[0]
    s, l = x2d.shape[0], ctx2d.shape[0]
    t = s + l
    d = D_MODEL
    mod_cols = w_mod.shape[2]
    cw_cols = conv_w.shape[2]

    b_cols = lax.dynamic_slice(b_mod, (0, me * mod_cols), (1, mod_cols))
    cw_blk = jnp.pad(conv_w[0], ((0, 5), (0, mod_cols - cw_cols)))
    a_rows, gathered = _prologue(jnp.pad(c, ((0, 7), (0, 0))), c_ctx[None, :], w_mod[0], b_cols, cw_blk,
                                 "prologue")
    mod_mine = lax.dynamic_index_in_dim(gathered, me, axis=1, keepdims=False).reshape(1, 6 * d)
    mod_ctx = gathered[:, 8, :].reshape(1, 6 * d)
    cw_full = gathered[:, 16:19, :cw_cols].transpose(1, 0, 2).reshape(3, CONV_W)

    early = [w.astype(BF16) for w in (w_in[0].T, w_uq[0], w_ukv[0])]
    late = [w.astype(BF16) for w in (w_out[0], w_mlp1[0], w_mlp2[0])]
    h_all, (g_in, g_uq, g_ukv) = _modulate_all(x2d, ctx2d, mod_mine, mod_ctx, _RidingGather(early),
                                               "modulate1")
    win_head, win_conv, wq, wk = _unpack_small_weights(g_in, g_uq, g_ukv)
    wk_k, wk_v = wk[:, :N_HEADS * LANES], wk[:, N_HEADS * LANES:]
    cos, sgn = _rope_tables(s, l)

    tm_t = _pick(t, (1088, 768, 256))
    tk_t = _pick(t, (2176, 768, 256))
    z_head, cq, kv_in, qf, kv = _head_fwd(h_all, win_head, wq, wk, q_norm_g, kv_norm_g, cos, sgn, tm_t, "head_fwd")
    z_conv = _matmul(h_all, win_conv, mode="nt", name="in_proj_conv", m=s, tm=1024, tn=1536, tk=1024)
    attn, a_cat, stats, (g_out, w1, g_w2) = _attn_fwd(qf, kv, s, _RidingGather(late), "attn_fwd")
    wo = g_out.reshape(d, d)
    w2 = g_w2.reshape(D_FF, d)
    a_cat = _conv_fwd(z_conv, cw_full, a_cat, "conv_fwd")
    (o, x1, h2), _ = _matmul_rows(a_cat, wo, _epi_resid_modulate, mode="nn", name="out_proj", tm=1024, tk=1024,
                                  rows=[x2d], vecs=[(mod_mine, 2), (mod_mine, 3), (mod_mine, 4)],
                                  out_dtypes=[F32, F32, BF16])
    u1, act = _matmul(h2, w1, mode="nn", name="mlp_up", tm=4096, tk=1024, epilogue="relu2", slots="b_cols")
    (dx2, dm, fsums), _ = _matmul_rows(act, w2, _epi_final, mode="nn", name="mlp_down", tm=512, tk=4096,
                                       rows=[x1, tgt], vecs=[(mod_mine, 5), (final_norm_g[None, :], 0)],
                                       out_dtypes=[F32, BF16], sums=True)

    d_w2 = _matmul(act, dm, mode="tn", name="d_w_mlp2", out_dtype=BF16, tm=1024, tn=1024, tk=4096)
    du1 = _matmul(dm, w2, mode="nt", name="d_act", out_dtype=BF16, tm=2048, tn=1024, tk=1024,
                  epilogue="drelu2", extra=(u1,))
    d_w1 = _matmul(h2, du1, mode="tn", name="d_w_mlp1", out_dtype=BF16, tm=1024, tk=4096, slots="out")
    (dx1, do, sums2), _ = _matmul_rows(du1, w1, _epi_modulate2_bwd, mode="nt", name="d_h2", tm=512, tk=4096,
                                       slots="b_contract", rows=[x1, dx2, o], vecs=[(mod_mine, 4), (mod_mine, 2)],
                                       out_dtypes=[F32, BF16], sums=True)
    d_wo = _matmul(a_cat, do, mode="tn", name="d_w_out", out_dtype=BF16, tm=1024, tn=1024, tk=2048)
    da = _matmul(do, wo, mode="nt", name="d_a", tm=1024, tn=1024, tk=1024)
    dz_conv, d_cw = _conv_bwd(z_conv, cw_full, da, "conv_bwd")
    ready = [d_wo.reshape(N_DEV, d // N_DEV, d), d_w1, d_w2.reshape(N_DEV, D_FF // N_DEV, d)]
    dq, dk, dv, rode = _attn_bwd(qf, kv, attn, da, stats, cos, sgn, _Riding("exchange", ready), "attn_bwd")
    d_wq = _matmul(cq, dq, mode="tn", name="d_w_uq", k=s, tm=256, tn=1024, tk=4096)
    d_wkk = _matmul(kv_in, dk, mode="tn", name="d_w_ukv_k", tm=256, tn=1024, tk=tk_t)
    d_wkv = _matmul(kv_in, dv, mode="tn", name="d_w_ukv_v", tm=256, tn=512, tk=tk_t)
    dz_head, dh_head, psums = _head_bwd(dq, dk, dv, z_head, wq, wk_k, wk_v, win_head, q_norm_g, kv_norm_g, cos, sgn, s,
                                        "head_bwd")
    d_head = _matmul(dz_head, h_all, mode="tn", name="d_w_in_head", tm=512, tn=1024, tk=tk_t)
    d_conv = _matmul(dz_conv, h_all, mode="tn", name="d_w_in_conv", k=s, tm=1536, tn=1024, tk=2048)
    send = _pack_small_grads(d_head, d_conv, d_wq, d_wkk, d_wkv)
    (grad_x, sums1), got = _matmul_rows(dz_conv, win_conv, _epi_modulate1_bwd, mode="nn", name="d_h1", tm=s // 4,
                                        tk=win_conv.shape[0], rows=[dh_head, x2d, dx1], vecs=[(mod_mine, 1)],
                                        out_dtypes=[F32], sums=True, riding=_RidingReduce(send))
    sums1c = _modulate_sums(dh_head, s // ROW_TILE, ctx2d)

    small = _pack_small(sums1, sums2, fsums, sums1c, psums, d_cw, "pack_small")
    (d_all,) = _all_gather([small], "gather_small_grads", True)
    d_cols = lax.dynamic_slice_in_dim(d_all, me * mod_cols, mod_cols, axis=2)
    g_w_mod, dsil, dsum = _adaln_bwd(a_rows.T, w_mod[0], d_cols[:, 0, :], d_cols[:, 1, :], d_all, "adaln_bwd")
    (dsil_all,) = _all_gather([dsil], "gather_d_cctx", True)
    loss = dsum[6, 0]
    g_cw = lax.dynamic_slice(dsum, (3, me * cw_cols), (3, cw_cols))

    slots = dict(zip(["w_in", "w_uq", "w_ukv"], got))
    slots.update(zip(["w_out", "w_mlp1", "w_mlp2"], rode))

    grads = {}
    weights = {"c_ctx": c_ctx, "w_mod": w_mod, "b_mod": b_mod, "w_in": w_in, "q_norm_g": q_norm_g, "w_uq": w_uq,
               "kv_norm_g": kv_norm_g, "w_ukv": w_ukv, "conv_w": conv_w, "w_out": w_out, "w_mlp1": w_mlp1,
               "w_mlp2": w_mlp2, "final_norm_g": final_norm_g}
    m_in = {"c_ctx": m_c_ctx, "w_mod": m_w_mod, "b_mod": m_b_mod, "w_in": m_w_in, "q_norm_g": m_q_norm_g,
            "w_uq": m_w_uq, "kv_norm_g": m_kv_norm_g, "w_ukv": m_w_ukv, "conv_w": m_conv_w, "w_out": m_w_out,
            "w_mlp1": m_w_mlp1, "w_mlp2": m_w_mlp2, "final_norm_g": m_final_norm_g}
    v_in = {"c_ctx": v_c_ctx, "w_mod": v_w_mod, "b_mod": v_b_mod, "w_in": v_w_in, "q_norm_g": v_q_norm_g,
            "w_uq": v_w_uq, "kv_norm_g": v_kv_norm_g, "w_ukv": v_w_ukv, "conv_w": v_conv_w, "w_out": v_w_out,
            "w_mlp1": v_w_mlp1, "w_mlp2": v_w_mlp2, "final_norm_g": v_final_norm_g}
    names = list(weights)
    small_names = ["c_ctx", "b_mod", "q_norm_g", "kv_norm_g", "final_norm_g", "conv_w"]
    delta, new_m, new_v = {}, {}, {}

    def as_rows(a):
        return a[None, :] if a.ndim == 1 else a

    small_out = _small_update(dsum, dsil_all, g_cw, [[as_rows(src[n]) for src in (weights, m_in, v_in)]
                                                      for n in small_names], "small_update")
    for n, outs in zip(small_names, small_out):
        grads[n], delta[n], new_m[n], new_v[n] = [a.reshape(weights[n].shape) for a in outs]
    for n in names:
        if n in small_names:
            continue
        if n == "w_in":
            wmv = [jnp.swapaxes(src[n], 1, 2) for src in (weights, m_in, v_in)]
            outs = _adamw(wmv[0], slots[n], wmv[1], wmv[2], "adamw_" + n, slots=True)
            grads[n], delta[n], new_m[n], new_v[n] = [jnp.swapaxes(a, 1, 2) for a in outs]
        elif n in slots:
            grads[n], delta[n], new_m[n], new_v[n] = _adamw(weights[n], slots[n], m_in[n], v_in[n], "adamw_" + n,
                                                            slots=True)
        else:
            delta[n], new_m[n], new_v[n] = _adamw(weights[n], g_w_mod, m_in[n], v_in[n], "adamw_" + n)
            grads[n] = g_w_mod[None]

    return (loss, grad_x[None], *[grads[n] for n in names], *[delta[n] for n in names],
            *[new_m[n] for n in names], *[new_v[n] for n in names])
```

```python
import math

import jax
import jax.numpy as jnp
import numpy as np
from jax import lax
from jax.experimental import pallas as pl
from jax.experimental.pallas import tpu as pltpu

F32 = jnp.float32
BF16 = jnp.bfloat16

D_MODEL = 1024
GRID_W = 64
N_HEADS = 8
QK_NOPE = 64
QK_ROPE = 32
V_DIM = 64
Q_RANK = 256
KV_RANK = 128
MLA_IN = Q_RANK + KV_RANK + QK_ROPE
CONV_W = 512
HEAD_COLS = 512
D_FF = 4096
ROPE_THETA = 10000.0
EPS = 1e-6
ATTN_SCALE = 1.0 / math.sqrt(QK_NOPE + QK_ROPE)
LOG2_E = 1.0 / math.log(2.0)
EXP2_SCALE = ATTN_SCALE * LOG2_E
N_DEV = 8
LANES = 128

ADAM_LR, ADAM_B1, ADAM_B2, ADAM_EPS, ADAM_WD, ADAM_STEP = 0.001, 0.9, 0.999, 1e-08, 0.01, 10

ROW_TILE = 256
VMEM_BIG = 60 * 1024 * 1024


def _params(sem=None, vmem=None):
    return pltpu.CompilerParams(dimension_semantics=sem, vmem_limit_bytes=vmem)


def _pick(n, prefs):
    for p in prefs:
        if n % p == 0:
            return p
    return n


def _my_index():
    return 4 * lax.axis_index("x") + 2 * lax.axis_index("y") + lax.axis_index("c")


def _two_level_gather(x_refs, out_refs, send_sems, recv_sems, local_sems):
    n = len(x_refs)
    x, y, c = lax.axis_index("x"), lax.axis_index("y"), lax.axis_index("c")
    me, sibling = (x, y, c), (x, y, 1 - c)
    chips = [(1 - x, y), (x, 1 - y), (1 - x, 1 - y)]

    def slot(a, px, py, pc):
        return out_refs[a].at[4 * px + 2 * py + pc]

    def copy(a, k, block, to, src=None):
        return pltpu.make_async_remote_copy(
            src_ref=slot(a, *block) if src is None else src, dst_ref=slot(a, *block),
            send_sem=send_sems.at[7 * a + k], recv_sem=recv_sems.at[7 * a + k],
            device_id=to, device_id_type=pl.DeviceIdType.MESH)

    mine = [pltpu.make_async_copy(x_refs[a], slot(a, *me), local_sems.at[a]) for a in range(n)]
    first = [cp for a in range(n) for cp in
             [copy(a, 0, me, sibling, src=x_refs[a])]
             + [copy(a, 1 + j, me, (*chip, c), src=x_refs[a]) for j, chip in enumerate(chips)]]
    passed = [[copy(a, 4 + j, (*chip, c), sibling) for j, chip in enumerate(chips)] for a in range(n)]

    def start():
        for cp in mine + first:
            cp.start()

    def forward():
        for a in range(n):
            for j, chip in enumerate(chips):
                copy(a, 1 + j, (*chip, c), me).wait_recv()
                passed[a][j].start()

    def finish():
        for a in range(n):
            copy(a, 0, sibling, me).wait_recv()
            for j, chip in enumerate(chips):
                copy(a, 4 + j, (*chip, 1 - c), me).wait_recv()
        for cp in first + [cp for per_array in passed for cp in per_array]:
            cp.wait_send()
        for cp in mine:
            cp.wait()

    return start, forward, finish


def _direct_gather(src_ref, dst_ref, send_sems, recv_sems):
    x, y, c = lax.axis_index("x"), lax.axis_index("y"), lax.axis_index("c")
    me = 4 * x + 2 * y + c
    dst_ref[me] = src_ref[...]
    sends, landings = [], []
    for k in range(1, N_DEV):
        peer = (1 - x if k & 4 else x, 1 - y if k & 2 else y, 1 - c if k & 1 else c)
        pid = 4 * peer[0] + 2 * peer[1] + peer[2]
        for dst, out in ((me, sends), (pid, landings)):
            out.append(pltpu.make_async_remote_copy(
                src_ref=src_ref, dst_ref=dst_ref.at[dst], send_sem=send_sems.at[k - 1], recv_sem=recv_sems.at[k - 1],
                device_id=peer, device_id_type=pl.DeviceIdType.MESH))
    for cp in sends:
        cp.start()

    def finish():
        for cp in landings:
            cp.wait_recv()
        for cp in sends:
            cp.wait_send()

    return finish


def _all_gather(arrays, name, in_vmem):
    space = pltpu.VMEM if in_vmem else pl.ANY
    n = len(arrays)

    def body(*refs):
        for phase in _two_level_gather(refs[:n], refs[n:2 * n], *refs[2 * n:]):
            phase()

    outs = pl.pallas_call(
        body, name=name,
        out_shape=tuple(jax.ShapeDtypeStruct((N_DEV,) + a.shape, a.dtype) for a in arrays),
        in_specs=[pl.BlockSpec(memory_space=space)] * n,
        out_specs=tuple(pl.BlockSpec(memory_space=space) for _ in arrays),
        scratch_shapes=[pltpu.SemaphoreType.DMA((7 * n,)), pltpu.SemaphoreType.DMA((7 * n,)),
                        pltpu.SemaphoreType.DMA((n,))],
    )(*arrays)
    return list(outs)


class _Riding:
    def __init__(self, kind, arrays):
        self.kind, self.arrays, self.n = kind, list(arrays), len(arrays)
        lead = (N_DEV,) if kind == "gather" else ()
        self.out_shape = [jax.ShapeDtypeStruct(lead + a.shape, a.dtype) for a in self.arrays]
        self.specs = [pl.BlockSpec(memory_space=pl.ANY)] * self.n
        self.scratch = [pltpu.SemaphoreType.DMA((7 * self.n,)), pltpu.SemaphoreType.DMA((7 * self.n,)),
                        pltpu.SemaphoreType.DMA((self.n,))]

    def copies(self, x_refs, y_refs, send_sems, recv_sems, local_sems):
        x, y, c = lax.axis_index("x"), lax.axis_index("y"), lax.axis_index("c")
        me = 4 * x + 2 * y + c
        local, sends, landings = [], [], []
        for a in range(self.n):
            src_mine = x_refs[a] if self.kind == "gather" else x_refs[a].at[me]
            local.append(pltpu.make_async_copy(src_mine, y_refs[a].at[me], local_sems.at[a]))
            for k in range(1, N_DEV):
                peer = (1 - x if k & 4 else x, 1 - y if k & 2 else y, 1 - c if k & 1 else c)
                pid = 4 * peer[0] + 2 * peer[1] + peer[2]
                src = x_refs[a] if self.kind == "gather" else x_refs[a].at[pid]
                for dst, out in ((me, sends), (pid, landings)):
                    out.append(pltpu.make_async_remote_copy(
                        src_ref=src, dst_ref=y_refs[a].at[dst],
                        send_sem=send_sems.at[7 * a + k - 1], recv_sem=recv_sems.at[7 * a + k - 1],
                        device_id=peer, device_id_type=pl.DeviceIdType.MESH))
        return local, sends, landings

    def run(self, first, last, x_refs, y_refs, sems, middle=None):
        if self.n == 0:
            return None
        local, sends, landings = self.copies(x_refs, y_refs, *sems)

        @pl.when(first)
        def _():
            for cp in local + sends:
                cp.start()

        return local, sends, landings, last

    @staticmethod
    def finish(state):
        if state is None:
            return
        local, sends, landings, last = state

        @pl.when(last)
        def _():
            for cp in landings:
                cp.wait_recv()
            for cp in sends:
                cp.wait_send()
            for cp in local:
                cp.wait()


class _RidingGather:
    def __init__(self, arrays):
        self.arrays, self.n = list(arrays), len(arrays)
        self.out_shape = [jax.ShapeDtypeStruct((N_DEV,) + a.shape, a.dtype) for a in self.arrays]
        self.specs = [pl.BlockSpec(memory_space=pl.ANY)] * self.n
        self.scratch = [pltpu.SemaphoreType.DMA((7 * self.n,)), pltpu.SemaphoreType.DMA((7 * self.n,)),
                        pltpu.SemaphoreType.DMA((self.n,))]

    def run(self, first, last, x_refs, y_refs, sems, middle):
        start, forward, finish = _two_level_gather(x_refs, y_refs, *sems)
        pl.when(first)(start)
        pl.when(middle)(forward)
        return finish, last

    @staticmethod
    def finish(state):
        finish, last = state
        pl.when(last)(finish)


class _RidingReduce:
    def __init__(self, arrays):
        self.arrays, self.n = list(arrays), len(arrays)
        self.out_shape = [jax.ShapeDtypeStruct((4,) + a.shape[1:], a.dtype) for a in self.arrays]
        self.specs = [pl.BlockSpec(memory_space=pl.ANY)] * self.n
        self.scratch = [pltpu.VMEM((4,) + a.shape[1:], a.dtype) for a in self.arrays for _ in range(3)]
        self.scratch += [pltpu.SemaphoreType.DMA((self.n,)) for _ in range(6)]

    def run(self, first, last, x_refs, y_refs, scratch, middle):
        n = self.n
        own, sib, tot = scratch[0:3 * n:3], scratch[1:3 * n:3], scratch[2:3 * n:3]
        d2d_send, d2d_recv, local_in, ici_send, ici_recv, local_out = scratch[3 * n:]
        x, y, c = lax.axis_index("x"), lax.axis_index("y"), lax.axis_index("c")
        my_chip = 2 * x + y
        sibling = (x, y, 1 - c)
        others = [(1 - x, y), (x, 1 - y), (1 - x, 1 - y)]

        def to_sibling(a, j=None):
            src = x_refs[a].at[pl.ds(0, 4)] if j is None else x_refs[a].at[2 * j + 1 - c]
            dst = sib[a] if j is None else sib[a].at[j]
            return pltpu.make_async_remote_copy(src_ref=src, dst_ref=dst, send_sem=d2d_send.at[a],
                                                recv_sem=d2d_recv.at[a], device_id=sibling,
                                                device_id_type=pl.DeviceIdType.MESH)

        def mine_in(a, j=None):
            src = x_refs[a].at[pl.ds(0, 4)] if j is None else x_refs[a].at[2 * j + c]
            return pltpu.make_async_copy(src, own[a] if j is None else own[a].at[j], local_in.at[a])

        def to_chip(a, chip=None):
            if chip is None:
                src, dst, peer = tot[a].at[pl.ds(0, 3)], y_refs[a].at[pl.ds(0, 3)], sibling
            else:
                src, dst, peer = tot[a].at[2 * chip[0] + chip[1]], y_refs[a].at[my_chip], (*chip, c)
            return pltpu.make_async_remote_copy(src_ref=src, dst_ref=dst, send_sem=ici_send.at[a],
                                                recv_sem=ici_recv.at[a], device_id=peer,
                                                device_id_type=pl.DeviceIdType.MESH)

        def mine_out(a):
            return pltpu.make_async_copy(tot[a].at[my_chip], y_refs[a].at[my_chip], local_out.at[a])

        @pl.when(first)
        def _():
            for a in range(n):
                for j in range(4):
                    to_sibling(a, j).start()
                    mine_in(a, j).start()

        @pl.when(middle)
        def _():
            for a in range(n):
                to_sibling(a).wait_recv()
                to_sibling(a).wait_send()
                mine_in(a).wait()
                tot[a][...] = (own[a][...].astype(F32) + sib[a][...].astype(F32)).astype(tot[a].dtype)
                for chip in others:
                    to_chip(a, chip).start()
                mine_out(a).start()

        def finish():
            @pl.when(last)
            def _():
                for a in range(n):
                    to_chip(a).wait_recv()
                    to_chip(a).wait_send()
                    mine_out(a).wait()

        return finish

    @staticmethod
    def finish(state):
        state()


_DIMS ={"nn": (((1,), (0,)), ((), ())), "nt": (((1,), (1,)), ((), ())), "tn": (((0,), (0,)), ((), ()))}
NT_DIMS = _DIMS["nt"]
TN_DIMS = _DIMS["tn"]


def _swap8(x):
    lane = lax.broadcasted_iota(jnp.int32, x.shape, 1)
    return jnp.where((lane & 15) < 8, pltpu.roll(x, LANES - 8, 1), pltpu.roll(x, 8, 1))


def _rope(x, cos, sgn, bwd):
    return x * cos + (_swap8(x * sgn) if bwd else _swap8(x) * sgn)


def _matmul(a, b, *, mode, name, out_dtype=F32, tm=512, tn=512, tk=512, m=None, k=None,
            epilogue=None, extra=(), slots=None):
    if mode == "nn":
        m = a.shape[0] if m is None else m
        k = a.shape[1]
        n = N_DEV * b.shape[2] if slots == "b_cols" else b.shape[1]
    elif mode == "nt":
        m = a.shape[0] if m is None else m
        k = a.shape[1]
        n = b.shape[0]
    else:
        k = a.shape[0] if k is None else k
        m, n = a.shape[1], b.shape[1]
    tm, tn, tk = min(tm, m), min(tn, n), min(tk, k)
    if slots == "b_cols":
        tn = b.shape[2]
    if slots == "out":
        tn = n // N_DEV
    assert m % tm == 0 and n % tn == 0 and k % tk == 0, (name, m, n, k, tm, tn, tk)
    nk = k // tk
    dims = _DIMS[mode]
    a_spec = (pl.BlockSpec((tk, tm), lambda i, j, kk: (kk, i)) if mode == "tn"
              else pl.BlockSpec((tm, tk), lambda i, j, kk: (i, kk)))
    if slots == "b_cols":
        b_spec = pl.BlockSpec((None, tk, tn), lambda i, j, kk: (j, kk, 0))
    elif mode == "nt":
        b_spec = pl.BlockSpec((tn, tk), lambda i, j, kk: (j, kk))
    else:
        b_spec = pl.BlockSpec((tk, tn), lambda i, j, kk: (kk, j))
    tile = pl.BlockSpec((tm, tn), lambda i, j, kk: (i, j))
    if slots == "out":
        o_spec = pl.BlockSpec((None, tm, tn), lambda i, j, kk: (j, i, 0))
        o_shape = (N_DEV, m, tn)
    else:
        o_spec, o_shape = tile, (m, n)
    in_specs, args = [a_spec, b_spec], [a, b]
    if epilogue == "drelu2":
        in_specs.append(tile)
    args += list(extra)
    if epilogue == "relu2":
        out_shape = (jax.ShapeDtypeStruct(o_shape, BF16), jax.ShapeDtypeStruct(o_shape, BF16))
        out_specs = (o_spec, o_spec)
    else:
        out_shape = jax.ShapeDtypeStruct(o_shape, out_dtype)
        out_specs = o_spec
    n_in = len(args)
    n_out = 2 if epilogue == "relu2" else 1

    def body(*refs):
        a_ref, b_ref = refs[0], refs[1]
        outs = refs[n_in:n_in + n_out]
        part = lax.dot_general(a_ref[...], b_ref[...], dims, preferred_element_type=F32)

        def finish(acc):
            if epilogue == "relu2":
                outs[0][...] = acc.astype(BF16)
                r = jnp.maximum(acc, 0.0)
                outs[1][...] = (r * r).astype(BF16)
            elif epilogue == "drelu2":
                u = refs[2][...].astype(F32)
                outs[0][...] = (acc * (2.0 * jnp.maximum(u, 0.0))).astype(out_dtype)
            else:
                outs[0][...] = acc.astype(out_dtype)

        if nk == 1:
            finish(part)
        else:
            acc_ref = refs[n_in + n_out]
            kk = pl.program_id(2)

            @pl.when(kk == 0)
            def _():
                acc_ref[...] = part

            @pl.when(kk > 0)
            def _():
                acc_ref[...] += part

            @pl.when(kk == nk - 1)
            def _():
                finish(acc_ref[...])

    return pl.pallas_call(
        body, name=name, grid=(m // tm, n // tn, nk),
        out_shape=out_shape, in_specs=in_specs, out_specs=out_specs,
        scratch_shapes=[pltpu.VMEM((tm, tn), F32)] if nk > 1 else [],
        compiler_params=_params(("parallel", "parallel", "arbitrary"), VMEM_BIG),
    )(*args)


def _rstd(x):
    return lax.rsqrt(jnp.mean(x * x, axis=1, keepdims=True) + EPS)


def _norm_bwd(dxn, xn, r):
    return r * (dxn - xn * jnp.mean(dxn * xn, axis=1, keepdims=True))


def _vec(col):
    return pl.BlockSpec((1, D_MODEL), lambda i: (0, col))


def _matmul_rows(a, b, epi, *, mode, name, tm, tk, rows=(), vecs=(), out_dtypes=(), sums=False, slots=None,
                 riding=None):
    m, k = a.shape
    n = D_MODEL
    tm, tk = min(tm, m), min(tk, k)
    riding = riding or _Riding("gather", [])
    group = 1
    if slots == "b_contract":
        group = max(1, tk // b.shape[2])
        tk = group * b.shape[2]
        b_spec = pl.BlockSpec((group, n, tk // group), lambda i, kk: (kk, 0, 0))
    elif mode == "nt":
        b_spec = pl.BlockSpec((n, tk), lambda i, kk: (0, kk))
    else:
        b_spec = pl.BlockSpec((tk, n), lambda i, kk: (kk, 0))
    assert m % tm == 0 and k % tk == 0, (name, m, k, tm, tk)
    ni, nk = m // tm, k // tk
    assert ni >= 2 or not isinstance(riding, _RidingReduce), "the two-level exchange needs a middle grid step"
    dims = _DIMS[mode]
    tile = pl.BlockSpec((tm, n), lambda i, kk: (i, 0))
    in_specs = [pl.BlockSpec((tm, tk), lambda i, kk: (i, kk)), b_spec] + [tile] * len(rows)
    in_specs += [pl.BlockSpec((1, n), lambda i, kk, col=col: (0, col)) for _, col in vecs]
    args = [a, b, *rows, *[v for v, _ in vecs]]
    out_shape = [jax.ShapeDtypeStruct((m, n), dt) for dt in out_dtypes]
    out_specs = [tile] * len(out_dtypes)
    if sums:
        out_shape.append(jax.ShapeDtypeStruct((8, n), F32))
        out_specs.append(pl.BlockSpec((8, n), lambda i, kk: (0, 0)))
    n_rows, n_vecs, n_outs, nr = len(rows), len(vecs), len(out_dtypes), riding.n
    n_in = 2 + n_rows + n_vecs

    def body(*refs):
        a_ref, b_ref = refs[0], refs[1]
        row_refs = refs[2:2 + n_rows]
        vec_refs = refs[2 + n_rows:n_in]
        x_refs = refs[n_in:n_in + nr]
        out_refs = refs[n_in + nr:n_in + nr + n_outs]
        pos = n_in + nr + n_outs
        sums_ref = refs[pos] if sums else None
        pos += 1 if sums else 0
        y_refs = refs[pos:pos + nr]
        pos += nr
        acc_ref = refs[pos] if nk > 1 else None
        sem_refs = refs[pos + (1 if nk > 1 else 0):]
        i, kk = pl.program_id(0), pl.program_id(1)
        state = riding.run((i == 0) & (kk == 0), (i == ni - 1) & (kk == nk - 1), x_refs, y_refs, sem_refs,
                           middle=(i == 1) & (kk == 0))
        if slots == "b_contract":
            c = tk // group
            part = lax.dot_general(a_ref[:, 0:c], b_ref[0], dims, preferred_element_type=F32)
            for u in range(1, group):
                part = part + lax.dot_general(a_ref[:, u * c:(u + 1) * c], b_ref[u], dims, preferred_element_type=F32)
        else:
            part = lax.dot_general(a_ref[...], b_ref[...], dims, preferred_element_type=F32)

        def finish(acc):
            nsub = tm // ROW_TILE
            for r in range(nsub):
                blk = pl.ds(r * ROW_TILE, ROW_TILE)
                epi(acc[r * ROW_TILE:(r + 1) * ROW_TILE], [ref.at[blk] for ref in row_refs], vec_refs,
                    [ref.at[blk] for ref in out_refs], sums_ref,
                    (i == 0) if r == 0 else None, (i == ni - 1) if r == nsub - 1 else None)

        if nk == 1:
            finish(part)
        else:
            @pl.when(kk == 0)
            def _():
                acc_ref[...] = part

            @pl.when(kk > 0)
            def _():
                acc_ref[...] += part

            @pl.when(kk == nk - 1)
            def _():
                finish(acc_ref)

        riding.finish(state)

    outs = pl.pallas_call(
        body, name=name, grid=(ni, nk),
        out_shape=(*out_shape, *riding.out_shape),
        in_specs=[*in_specs, *riding.specs], out_specs=(*out_specs, *riding.specs),
        scratch_shapes=([pltpu.VMEM((tm, n), F32)] if nk > 1 else []) + (riding.scratch if nr else []),
        compiler_params=_params(("arbitrary", "arbitrary"), VMEM_BIG),
    )(*args, *riding.arrays)
    n_own = len(out_shape)
    return list(outs[:n_own]), list(outs[n_own:])


def _zero_sums_at_start(sums_ref, first):
    if first is not None:
        @pl.when(first)
        def _():
            sums_ref[...] = jnp.zeros_like(sums_ref)


def _epi_resid_modulate(acc, rows, vecs, outs, sums_ref, first, last):
    (x_ref,), (g_ref, sh_ref, sc_ref) = rows, vecs
    x1 = x_ref[...] + g_ref[...] * acc
    outs[0][...] = acc
    outs[1][...] = x1
    outs[2][...] = (x1 * _rstd(x1) * (1.0 + sc_ref[...]) + sh_ref[...]).astype(BF16)


def _epi_final(acc, rows, vecs, outs, sums_ref, first, last):
    (x1_ref, t_ref), (g_ref, gf_ref) = rows, vecs
    d = acc.shape[1]
    x2 = x1_ref[...] + g_ref[...] * acc
    r = _rstd(x2)
    xn = x2 * r
    err = xn * gf_ref[...] - t_ref[...]
    dy = err * (1.0 / d)
    dx2 = _norm_bwd(dy * gf_ref[...], xn, r)
    outs[0][...] = dx2
    outs[1][...] = (dx2 * g_ref[...]).astype(BF16)
    _zero_sums_at_start(sums_ref, first)
    sums_ref[0:1, :] += jnp.sum(dy * xn, axis=0, keepdims=True)
    sums_ref[1:2, :] += jnp.sum(dx2 * acc, axis=0, keepdims=True)
    sums_ref[2:3, :] += jnp.sum(err * err, axis=0, keepdims=True)

    if last is not None:
        @pl.when(last)
        def _():
            tot = jnp.sum(sums_ref[2:3, :], axis=1, keepdims=True) * (0.5 / d)
            sums_ref[3:4, :] = jnp.broadcast_to(tot, (1, d))


def _epi_modulate2_bwd(acc, rows, vecs, outs, sums_ref, first, last):
    (x_ref, dres_ref, o_ref), (sc_ref, g_ref) = rows, vecs
    x = x_ref[...]
    r = _rstd(x)
    xn = x * r
    dx = dres_ref[...] + _norm_bwd(acc * (1.0 + sc_ref[...]), xn, r)
    outs[0][...] = dx
    outs[1][...] = (dx * g_ref[...]).astype(BF16)
    _zero_sums_at_start(sums_ref, first)
    sums_ref[0:1, :] += jnp.sum(acc * xn, axis=0, keepdims=True)
    sums_ref[1:2, :] += jnp.sum(acc, axis=0, keepdims=True)
    sums_ref[2:3, :] += jnp.sum(dx * o_ref[...], axis=0, keepdims=True)


def _epi_modulate1_bwd(acc, rows, vecs, outs, sums_ref, first, last):
    (add_ref, x_ref, dres_ref), (sc_ref,) = rows, vecs
    dh = acc + add_ref[...]
    x = x_ref[...]
    r = _rstd(x)
    xn = x * r
    outs[0][...] = dres_ref[...] + _norm_bwd(dh * (1.0 + sc_ref[...]), xn, r)
    _zero_sums_at_start(sums_ref, first)
    sums_ref[0:1, :] += jnp.sum(dh * xn, axis=0, keepdims=True)
    sums_ref[1:2, :] += jnp.sum(dh, axis=0, keepdims=True)


def _modulate_all(x, ctx, mod, mod_ctx, riding, name):
    s, d = x.shape
    t = s + ctx.shape[0]
    ns = s // ROW_TILE
    nc = ctx.shape[0] // ROW_TILE
    nr = riding.n

    def body(*refs):
        x_ref, c_ref, sh_ref, sc_ref, shc_ref, scc_ref = refs[:6]
        h_ref = refs[6 + nr]
        i = pl.program_id(0)
        state = riding.run(i == 0, i == ns + nc - 1, refs[6:6 + nr], refs[7 + nr:7 + 2 * nr], refs[7 + 2 * nr:],
                           middle=i == ns + nc - 3)

        @pl.when(i < ns)
        def _():
            v = x_ref[...]
            h_ref[...] = (v * _rstd(v) * (1.0 + sc_ref[...]) + sh_ref[...]).astype(BF16)

        @pl.when(i >= ns)
        def _():
            v = c_ref[...]
            h_ref[...] = (v * _rstd(v) * (1.0 + scc_ref[...]) + shc_ref[...]).astype(BF16)

        riding.finish(state)

    outs = pl.pallas_call(
        body, name=name, grid=(ns + nc,),
        out_shape=(jax.ShapeDtypeStruct((t, d), BF16), *riding.out_shape),
        in_specs=[pl.BlockSpec((ROW_TILE, d), lambda i: (jnp.minimum(i, ns - 1), 0)),
                  pl.BlockSpec((ROW_TILE, d), lambda i: (jnp.maximum(i - ns, 0), 0)),
                  _vec(0), _vec(1), _vec(0), _vec(1), *riding.specs],
        out_specs=(pl.BlockSpec((ROW_TILE, d), lambda i: (i, 0)), *riding.specs),
        scratch_shapes=riding.scratch,
        compiler_params=_params(("arbitrary",)),
    )(x, ctx, mod, mod, mod_ctx, mod_ctx, *riding.arrays)
    return outs[0], list(outs[1:])


def _modulate_sums(dh, row_off, xsrc):
    s, d = xsrc.shape

    def body(dh_ref, x_ref, sums_ref):
        i = pl.program_id(0)
        x = x_ref[...]
        dhv = dh_ref[...]

        @pl.when(i == 0)
        def _():
            sums_ref[...] = jnp.zeros_like(sums_ref)

        sums_ref[0:1, :] += jnp.sum(dhv * (x * _rstd(x)), axis=0, keepdims=True)
        sums_ref[1:2, :] += jnp.sum(dhv, axis=0, keepdims=True)

    return pl.pallas_call(
        body, name="modulate1_ctx_bwd", grid=(s // ROW_TILE,),
        out_shape=jax.ShapeDtypeStruct((8, d), F32),
        in_specs=[pl.BlockSpec((ROW_TILE, d), lambda i: (i + row_off, 0)), pl.BlockSpec((ROW_TILE, d), lambda i: (i, 0))],
        out_specs=pl.BlockSpec((8, d), lambda i: (0, 0)),
        compiler_params=_params(("arbitrary",)),
    )(dh, xsrc)


def _head_fwd(h_all, win_head, wq, wk, q_gain, kv_gain, cos, sgn, tm, name):
    t, d = h_all.shape
    nq, nkv = wq.shape[1], wk.shape[1]

    def body(h_ref, wi_ref, wq_ref, wk_ref, qg_ref, kg_ref, c_ref, s_ref, z_ref, cq_ref, kvin_ref, qf_ref, kv_ref):
        z = lax.dot_general(h_ref[...], wi_ref[...], NT_DIMS, preferred_element_type=F32)
        z_ref[...] = z
        cos, sgn = c_ref[...], s_ref[...]
        zq = z[:, 0:Q_RANK]
        cq = (zq * _rstd(zq) * qg_ref[...]).astype(BF16)
        cq_ref[...] = cq
        zk = z[:, Q_RANK:Q_RANK + KV_RANK]
        kv_in = jnp.concatenate([(zk * _rstd(zk) * kg_ref[...]).astype(BF16),
                                 _rope(z[:, Q_RANK + KV_RANK:HEAD_COLS], cos, sgn, False).astype(BF16)], axis=1)
        kvin_ref[...] = kv_in
        q = jnp.dot(cq, wq_ref[...], preferred_element_type=F32)
        for h in range(nq // LANES):
            sl = slice(h * LANES, (h + 1) * LANES)
            qf_ref[:, sl] = _rope(q[:, sl], cos, sgn, False).astype(BF16)
        kv_ref[...] = jnp.dot(kv_in, wk_ref[...], preferred_element_type=F32).astype(BF16)

    def row(w):
        return pl.BlockSpec((tm, w), lambda i: (i, 0))

    def whole(a):
        return pl.BlockSpec(a.shape, lambda i: (0, 0))

    return pl.pallas_call(
        body, name=name, grid=(t // tm,),
        out_shape=(jax.ShapeDtypeStruct((t, HEAD_COLS), F32), jax.ShapeDtypeStruct((t, Q_RANK), BF16),
                   jax.ShapeDtypeStruct((t, KV_RANK + LANES), BF16), jax.ShapeDtypeStruct((t, nq), BF16),
                   jax.ShapeDtypeStruct((t, nkv), BF16)),
        in_specs=[row(d), whole(win_head), whole(wq), whole(wk), whole(q_gain), whole(kv_gain), row(LANES), row(LANES)],
        out_specs=(row(HEAD_COLS), row(Q_RANK), row(KV_RANK + LANES), row(nq), row(nkv)),
        compiler_params=_params(("parallel",), VMEM_BIG),
    )(h_all, win_head, wq, wk, q_gain, kv_gain, cos, sgn)


def _head_bwd(dq, dk, dv, z, wq, wk_k, wk_v, win_head, q_gain, kv_gain, cos, sgn, s, name):
    t = z.shape[0]
    ns = s // ROW_TILE

    def body(dq_ref, dk_ref, dv_ref, z_ref, wq_ref, wkk_ref, wkv_ref, wi_ref, qg_ref, kg_ref, c_ref, s_ref,
             dz_ref, dh_ref, sums_ref):
        i = pl.program_id(0)

        @pl.when(i == 0)
        def _():
            sums_ref[...] = jnp.zeros_like(sums_ref)

        @pl.when(i < ns)
        def _():
            dc = lax.dot_general(dq_ref[...], wq_ref[...], NT_DIMS, preferred_element_type=F32)
            zq = z_ref[:, 0:Q_RANK]
            r = _rstd(zq)
            zn = zq * r
            sums_ref[0:1, :] += jnp.sum(dc * zn, axis=0, keepdims=True)
            dz_ref[:, 0:Q_RANK] = _norm_bwd(dc * qg_ref[...], zn, r).astype(BF16)

        @pl.when(i >= ns)
        def _():
            dz_ref[:, 0:Q_RANK] = jnp.zeros((ROW_TILE, Q_RANK), BF16)

        dkv = (lax.dot_general(dk_ref[...], wkk_ref[...], NT_DIMS, preferred_element_type=F32)
               + lax.dot_general(dv_ref[...], wkv_ref[...], NT_DIMS, preferred_element_type=F32))
        zk = z_ref[:, Q_RANK:Q_RANK + KV_RANK]
        r = _rstd(zk)
        zn = zk * r
        dc = dkv[:, 0:KV_RANK]
        sums_ref[1:2, 0:KV_RANK] += jnp.sum(dc * zn, axis=0, keepdims=True)
        dz_ref[:, Q_RANK:Q_RANK + KV_RANK] = _norm_bwd(dc * kg_ref[...], zn, r).astype(BF16)
        dz_ref[:, Q_RANK + KV_RANK:HEAD_COLS] = _rope(dkv[:, KV_RANK:KV_RANK + LANES], c_ref[...], s_ref[...],
                                                       True).astype(BF16)
        dh_ref[...] = jnp.dot(dz_ref[...], wi_ref[...], preferred_element_type=F32)

    def row(w):
        return pl.BlockSpec((ROW_TILE, w), lambda i: (i, 0))

    def whole(a):
        return pl.BlockSpec(a.shape, lambda i: (0, 0))

    return pl.pallas_call(
        body, name=name, grid=(t // ROW_TILE,),
        out_shape=(jax.ShapeDtypeStruct((t, HEAD_COLS), BF16), jax.ShapeDtypeStruct((t, D_MODEL), F32),
                   jax.ShapeDtypeStruct((8, Q_RANK), F32)),
        in_specs=[pl.BlockSpec((ROW_TILE, dq.shape[1]), lambda i: (jnp.minimum(i, ns - 1), 0)),
                  row(dk.shape[1]), row(dv.shape[1]), row(HEAD_COLS), whole(wq), whole(wk_k), whole(wk_v),
                  whole(win_head), whole(q_gain), whole(kv_gain), row(LANES), row(LANES)],
        out_specs=(row(HEAD_COLS), row(D_MODEL), pl.BlockSpec((8, Q_RANK), lambda i: (0, 0))),
        compiler_params=_params(("arbitrary",), VMEM_BIG),
    )(dq, dk, dv, z, wq, wk_k, wk_v, win_head, q_gain, kv_gain, cos, sgn)


def _shift_rows(u, s):
    rowi = lax.broadcasted_iota(jnp.int32, u.shape, 0)
    prev = jnp.where(rowi == 0, 0.0, pltpu.roll(u, 1, 0))
    nxt = jnp.where(rowi == s - 1, 0.0, pltpu.roll(u, s - 1, 0))
    return prev, nxt


def _conv_fwd(z_conv, cw, a_cat, name):
    s = z_conv.shape[0]

    def body(z_ref, w_ref, a_in_ref, o_ref):
        del a_in_ref
        gb, gc, xv = z_ref[:, 0:LANES], z_ref[:, LANES:2 * LANES], z_ref[:, 2 * LANES:3 * LANES]
        u = gc * xv
        prev, nxt = _shift_rows(u, s)
        y = w_ref[0:1, :] * prev + w_ref[1:2, :] * u + w_ref[2:3, :] * nxt
        o_ref[...] = (gb * y).astype(BF16)

    return pl.pallas_call(
        body, name=name, grid=(CONV_W // LANES,),
        out_shape=jax.ShapeDtypeStruct(a_cat.shape, a_cat.dtype),
        in_specs=[pl.BlockSpec((s, 3 * LANES), lambda j: (0, j)), pl.BlockSpec((3, LANES), lambda j: (0, j)),
                  pl.BlockSpec(memory_space=pl.ANY)],
        out_specs=pl.BlockSpec((s, LANES), lambda j: (0, 4 + j)),
        input_output_aliases={2: 0},
        compiler_params=_params(("parallel",), VMEM_BIG),
    )(z_conv, cw, a_cat)


def _conv_bwd(z_conv, cw, da, name):
    s = z_conv.shape[0]

    def body(z_ref, w_ref, da_ref, dz_ref, dw_ref):
        gb, gc, xv = z_ref[:, 0:LANES], z_ref[:, LANES:2 * LANES], z_ref[:, 2 * LANES:3 * LANES]
        u = gc * xv
        prev, nxt = _shift_rows(u, s)
        dcv = da_ref[...]
        dz_ref[:, 0:LANES] = (dcv * (w_ref[0:1, :] * prev + w_ref[1:2, :] * u + w_ref[2:3, :] * nxt)).astype(BF16)
        dy = dcv * gb
        dw_ref[0:1, :] = jnp.sum(dy * prev, axis=0, keepdims=True)
        dw_ref[1:2, :] = jnp.sum(dy * u, axis=0, keepdims=True)
        dw_ref[2:3, :] = jnp.sum(dy * nxt, axis=0, keepdims=True)
        dyp, dyn = _shift_rows(dy, s)
        du = w_ref[0:1, :] * dyn + w_ref[1:2, :] * dy + w_ref[2:3, :] * dyp
        dz_ref[:, LANES:2 * LANES] = (du * xv).astype(BF16)
        dz_ref[:, 2 * LANES:3 * LANES] = (du * gc).astype(BF16)

    blk = pl.BlockSpec((s, 3 * LANES), lambda j: (0, j))
    cws = pl.BlockSpec((3, LANES), lambda j: (0, j))
    return pl.pallas_call(
        body, name=name, grid=(CONV_W // LANES,),
        out_shape=(jax.ShapeDtypeStruct(z_conv.shape, BF16), jax.ShapeDtypeStruct((3, CONV_W), F32)),
        in_specs=[blk, cws, pl.BlockSpec((s, LANES), lambda j: (0, 4 + j))], out_specs=(blk, cws),
        compiler_params=_params(("parallel",), VMEM_BIG),
    )(z_conv, cw, da)


ATT_TQ = 512
ATT_Q_STEP = 1024
ATT_TQ_BWD = 512


def _head_mask(shape, hh):
    lane = lax.broadcasted_iota(jnp.int32, shape, 1)
    return (lane >= hh * V_DIM) & (lane < (hh + 1) * V_DIM)


def _attn_fwd(qf, kv, s, riding, name):
    t = kv.shape[0]
    step = min(ATT_Q_STEP, s)
    nq = s // step
    nr = riding.n

    def body(*refs):
        q_ref, k_ref, v_ref = refs[:3]
        o_ref, ob_ref, st_ref = refs[3 + nr:6 + nr]
        p, i = pl.program_id(0), pl.program_id(1)
        state = riding.run((p == 0) & (i == 0), (p == N_HEADS // 2 - 1) & (i == nq - 1),
                           refs[3:3 + nr], refs[6 + nr:6 + 2 * nr], refs[6 + 2 * nr:],
                           middle=(p == N_HEADS // 2 - 2) & (i == nq // 2))
        v = v_ref[...]
        vlane = lax.broadcasted_iota(jnp.int32, v.shape, 1)
        one_lane = [(1 - hh) * V_DIM for hh in range(2)]
        vm = [jnp.where(_head_mask(v.shape, hh), v, jnp.where(vlane == one_lane[hh], 1.0, 0.0).astype(BF16))
              for hh in range(2)]

        def block(r, carry):
            rows = pl.ds(pl.multiple_of(r * ATT_TQ, ATT_TQ), ATT_TQ)
            olane = lax.broadcasted_iota(jnp.int32, (ATT_TQ, LANES), 1)
            acc = jnp.zeros((ATT_TQ, LANES), F32)
            stat = jnp.zeros((ATT_TQ, LANES), F32)
            scores = [lax.dot_general(q_ref[rows, hh * LANES:(hh + 1) * LANES], k_ref[:, hh * LANES:(hh + 1) * LANES],
                                      NT_DIMS, preferred_element_type=F32) for hh in range(2)]
            maxes = [jnp.max(sc, axis=1, keepdims=True) for sc in scores]
            exps = [jnp.exp2((sc - mx) * EXP2_SCALE).astype(BF16) for sc, mx in zip(scores, maxes)]
            for hh in range(2):
                mx = maxes[hh]
                res = jnp.dot(exps[hh], vm[hh], preferred_element_type=F32)
                den = jnp.sum(jnp.where(olane == one_lane[hh], res, 0.0), axis=1, keepdims=True)
                acc = acc + jnp.where(_head_mask(res.shape, hh), res * (1.0 / den), 0.0)
                stat = stat + jnp.where(olane == hh, mx * EXP2_SCALE + jnp.log(den) * LOG2_E, 0.0)
            o_ref[rows, :] = acc
            ob_ref[rows, :] = acc.astype(BF16)
            st_ref[:, rows] = stat.T[0:8, :]
            return carry

        lax.fori_loop(0, step // ATT_TQ, block, 0)
        riding.finish(state)

    o_spec = pl.BlockSpec((step, LANES), lambda p, i: (i, p))
    outs = pl.pallas_call(
        body, name=name, grid=(N_HEADS // 2, nq),
        out_shape=(jax.ShapeDtypeStruct((s, N_HEADS * V_DIM), F32),
                   jax.ShapeDtypeStruct((s, D_MODEL), BF16),
                   jax.ShapeDtypeStruct((N_HEADS // 2 * 8, s), F32), *riding.out_shape),
        in_specs=[pl.BlockSpec((step, 2 * LANES), lambda p, i: (i, p)),
                  pl.BlockSpec((t, 2 * LANES), lambda p, i: (0, p)),
                  pl.BlockSpec((t, LANES), lambda p, i: (0, N_HEADS + p)), *riding.specs],
        out_specs=(o_spec, o_spec, pl.BlockSpec((8, step), lambda p, i: (p, i)), *riding.specs),
        scratch_shapes=riding.scratch,
        compiler_params=_params(("arbitrary", "arbitrary"), VMEM_BIG),
    )(qf, kv, kv, *riding.arrays)
    return outs[0], outs[1], outs[2], list(outs[3:])


def _attn_bwd(qf, kv, o, da, stats, cos, sgn, riding, name):
    s, t = o.shape[0], kv.shape[0]
    ATT_TQ = ATT_TQ_BWD
    nq = s // ATT_TQ
    nr = riding.n

    def body(*refs):
        q_ref, k_ref, v_ref, o_ref, do_ref, st_ref, c_ref, s_ref = refs[:8]
        dq_ref, dk_ref, dv_ref = refs[8 + nr:11 + nr]
        dk_acc, dv_acc = refs[11 + 2 * nr:13 + 2 * nr]
        p, i = pl.program_id(0), pl.program_id(1)
        state = riding.run((p == 0) & (i == 0), (p == N_HEADS // 2 - 1) & (i == nq - 1),
                           refs[8:8 + nr], refs[11 + nr:11 + 2 * nr], refs[13 + 2 * nr:])

        @pl.when(i == 0)
        def _():
            dk_acc[...] = jnp.zeros_like(dk_acc)
            dv_acc[...] = jnp.zeros_like(dv_acc)

        v = v_ref[...]
        do = do_ref[...]
        od = do * o_ref[...]
        ones = jnp.ones((8, LANES), F32)
        for hh in range(2):
            sl = slice(hh * LANES, (hh + 1) * LANES)
            q, k = q_ref[:, sl], k_ref[:, sl]
            mask = _head_mask(do.shape, hh)
            dom = jnp.where(mask, do, 0.0).astype(BF16)
            delta = lax.dot_general(ones, jnp.where(mask, od, 0.0), NT_DIMS, preferred_element_type=F32,
                                    precision=lax.Precision.HIGHEST)[0:1, :]
            st = lax.dot_general(k, q, NT_DIMS, preferred_element_type=F32)
            pt = jnp.exp2(st * EXP2_SCALE - st_ref[hh:hh + 1, :]).astype(BF16)
            dpt = lax.dot_general(v, dom, NT_DIMS, preferred_element_type=F32)
            dst = (pt.astype(F32) * (dpt - delta)).astype(BF16)
            dv_acc[...] += jnp.dot(pt, dom, preferred_element_type=F32)
            dk_acc[:, sl] += jnp.dot(dst, q, preferred_element_type=F32)
            dq = lax.dot_general(dst, k, TN_DIMS, preferred_element_type=F32) * ATTN_SCALE
            dq_ref[:, sl] = _rope(dq, c_ref[...], s_ref[...], True).astype(BF16)

        @pl.when(i == nq - 1)
        def _():
            dk_ref[...] = (dk_acc[...] * ATTN_SCALE).astype(BF16)
            dv_ref[...] = dv_acc[...].astype(BF16)

        riding.finish(state)

    o_spec = pl.BlockSpec((ATT_TQ, LANES), lambda p, i: (i, p))
    tab = pl.BlockSpec((ATT_TQ, LANES), lambda p, i: (i, 0))
    outs = pl.pallas_call(
        body, name=name, grid=(N_HEADS // 2, nq),
        out_shape=(jax.ShapeDtypeStruct((s, N_HEADS * LANES), BF16),
                   jax.ShapeDtypeStruct((t, N_HEADS * LANES), BF16),
                   jax.ShapeDtypeStruct((t, N_HEADS * V_DIM), BF16), *riding.out_shape),
        in_specs=[pl.BlockSpec((ATT_TQ, 2 * LANES), lambda p, i: (i, p)),
                  pl.BlockSpec((t, 2 * LANES), lambda p, i: (0, p)),
                  pl.BlockSpec((t, LANES), lambda p, i: (0, N_HEADS + p)),
                  o_spec, o_spec,
                  pl.BlockSpec((8, ATT_TQ), lambda p, i: (p, i)), tab, tab, *riding.specs],
        out_specs=(pl.BlockSpec((ATT_TQ, 2 * LANES), lambda p, i: (i, p)),
                   pl.BlockSpec((t, 2 * LANES), lambda p, i: (0, p)),
                   pl.BlockSpec((t, LANES), lambda p, i: (0, p)), *riding.specs),
        scratch_shapes=[pltpu.VMEM((t, 2 * LANES), F32), pltpu.VMEM((t, LANES), F32), *riding.scratch],
        compiler_params=_params(("arbitrary", "arbitrary"), VMEM_BIG),
    )(qf, kv, kv, o, da, stats, cos, sgn, *riding.arrays)
    return outs[0], outs[1], outs[2], list(outs[3:])


def _silu(x):
    return x * (1.0 / (1.0 + jnp.exp(-x)))


def _prologue(c_rows, c_ctx, w_mod, b_cols, extra_rows, name):
    d, cols = c_rows.shape[1], w_mod.shape[1]

    def body(c_ref, cctx_ref, wmod_ref, b_ref, x_ref, a_ref, modg_ref, c_all, blk, c_send, c_recv, m_send, m_recv):
        _direct_gather(c_ref, c_all, c_send, c_recv)()
        a_ref[...] = jnp.zeros_like(a_ref)
        for j in range(N_DEV):
            a_ref[j:j + 1, :] = c_all[j, 0:1, :]
        a_ref[N_DEV:N_DEV + 1, :] = cctx_ref[...]
        blk[0:16, :] = jnp.dot(_silu(a_ref[...]), wmod_ref[...], preferred_element_type=F32,
                               precision=lax.Precision.HIGHEST) + b_ref[...]
        blk[16:24, :] = x_ref[...]
        _direct_gather(blk, modg_ref, m_send, m_recv)()

    vmem = pl.BlockSpec(memory_space=pltpu.VMEM)
    return pl.pallas_call(
        body, name=name,
        out_shape=(jax.ShapeDtypeStruct((16, d), F32), jax.ShapeDtypeStruct((N_DEV, 24, cols), F32)),
        in_specs=[vmem] * 5, out_specs=(vmem, vmem),
        scratch_shapes=[pltpu.VMEM((N_DEV, 8, d), F32), pltpu.VMEM((24, cols), F32)]
        + [pltpu.SemaphoreType.DMA((7,)) for _ in range(4)],
        compiler_params=_params(None, VMEM_BIG),
    )(c_rows, c_ctx, w_mod, b_cols, extra_rows)


def _adaln_bwd_update(a_t, w, d_ex, d_ctx, d_all, d_cw, params, name):
    d = w.shape[0]
    n = len(params)

    def body(at_ref, w_ref, dex_ref, dctx_ref, dall_ref, dcw_ref, *rest):
        wmv = rest[:3 * n]
        gw_ref, dsum_ref = rest[3 * n:3 * n + 2]
        outs = rest[3 * n + 2:3 * n + 2 + 4 * n]
        dsil, dsil_all, send_sems, recv_sems = rest[3 * n + 2 + 4 * n:]
        dctx = dctx_ref[...]
        row = dctx[0:1, :]
        for j in range(1, N_DEV):
            row = row + dctx[j:j + 1, :]
        rowi = lax.broadcasted_iota(jnp.int32, dctx.shape, 0)
        ctx_rows = jnp.where(rowi == 0, jnp.broadcast_to(row, dctx.shape), 0.0)
        hi = lax.Precision.HIGHEST
        dsil[...] = lax.dot_general(ctx_rows, w_ref[...], NT_DIMS, preferred_element_type=F32, precision=hi)
        wait_dsil = _direct_gather(dsil, dsil_all, send_sems, recv_sems)
        d_rows = jnp.concatenate([dex_ref[...], ctx_rows], axis=0)
        gw_ref[...] = jnp.dot(_silu(at_ref[...]), d_rows, preferred_element_type=F32, precision=hi)
        tot = dall_ref[0]
        g_cw = dcw_ref[0]
        for j in range(1, N_DEV):
            tot = tot + dall_ref[j]
            g_cw = g_cw + dcw_ref[j]
        dsum_ref[...] = tot
        wait_dsil()
        dsil_tot = dsil_all[0]
        for j in range(1, N_DEV):
            dsil_tot = dsil_tot + dsil_all[j]
        cv = wmv[0][...]
        sg = 1.0 / (1.0 + jnp.exp(-cv))
        off = Q_RANK + KV_RANK
        grads = [dsil_tot[0:1, :] * (sg * (1.0 + cv * (1.0 - sg))),
                 tot[0:1, :] + tot[1:2, :],
                 tot[2:3, 0:Q_RANK], tot[2:3, Q_RANK:off], tot[2:3, off:off + d],
                 g_cw]
        for p, g in enumerate(grads):
            w_p, m_p, v_p = wmv[3 * p:3 * p + 3]
            at = 0 if len(w_p.shape) == 3 else Ellipsis
            res = (g,) + _adam_math(w_p[at], g, m_p[at], v_p[at])
            for q, val in enumerate(res):
                outs[4 * p + q][at] = val

    flat = [arr for wmv in params for arr in wmv]
    out_shape = (jax.ShapeDtypeStruct(w.shape, F32), jax.ShapeDtypeStruct(d_all.shape[1:], F32),
                 *[jax.ShapeDtypeStruct(wmv[0].shape, F32) for wmv in params for _ in range(4)])
    outs = pl.pallas_call(
        body, name=name, out_shape=out_shape,
        scratch_shapes=[pltpu.VMEM((8, d), F32), pltpu.VMEM((N_DEV, 8, d), F32),
                        pltpu.SemaphoreType.DMA((7,)), pltpu.SemaphoreType.DMA((7,))],
        compiler_params=_params(None, VMEM_BIG),
    )(a_t, w, d_ex, d_ctx, d_all, d_cw, *flat)
    return outs[0], outs[1], [outs[2 + 4 * p:6 + 4 * p] for p in range(n)]


def _pack_small(sums1, sums2, fsums, sums1c, psums, d_cw, name):
    d = D_MODEL

    def body(s1_ref, s2_ref, f_ref, s1c_ref, p_ref, cw_ref, o_ref):
        o_ref[...] = jnp.zeros_like(o_ref)
        for col, (ref, r) in enumerate([(s1_ref, 1), (s1_ref, 0), (s2_ref, 2), (s2_ref, 1), (s2_ref, 0), (f_ref, 1)]):
            o_ref[0:1, col * d:(col + 1) * d] = ref[r:r + 1, :]
        o_ref[1:2, 0:d] = s1c_ref[1:2, :]
        o_ref[1:2, d:2 * d] = s1c_ref[0:1, :]
        o_ref[2:3, 0:Q_RANK] = p_ref[0:1, :]
        o_ref[2:3, Q_RANK:Q_RANK + KV_RANK] = p_ref[1:2, 0:KV_RANK]
        o_ref[2:3, Q_RANK + KV_RANK:Q_RANK + KV_RANK + d] = f_ref[0:1, :]
        for r in range(3):
            o_ref[3 + r:4 + r, 0:CONV_W] = cw_ref[r:r + 1, :]
        o_ref[6:7, 0:d] = f_ref[3:4, :]

    return pl.pallas_call(body, name=name, out_shape=jax.ShapeDtypeStruct((8, 6 * d), F32))(
        sums1, sums2, fsums, sums1c, psums, d_cw)


def _adam_math(w, g, m, v):
    nm = ADAM_B1 * m + (1.0 - ADAM_B1) * g
    nv = ADAM_B2 * v + (1.0 - ADAM_B2) * (g * g)
    m_hat = nm / (1.0 - ADAM_B1 ** ADAM_STEP)
    v_hat = nv / (1.0 - ADAM_B2 ** ADAM_STEP)
    return -ADAM_LR * (m_hat / (jnp.sqrt(v_hat) + ADAM_EPS) + ADAM_WD * w), nm, nv


def _adamw(w, g, m, v, name, slots=False):
    _, rows, cols = w.shape
    tr = _pick(rows, (256, 128, 64, 32, 16, 8))

    def body(w_ref, g_ref, m_ref, v_ref, *outs):
        if slots:
            gv = g_ref[0].astype(F32)
            for j in range(1, g.shape[0]):
                gv = gv + g_ref[j].astype(F32)
            outs[0][...] = gv
        else:
            gv = g_ref[...]
        d_ref, nm_ref, nv_ref = outs[-3:]
        d_ref[...], nm_ref[...], nv_ref[...] = _adam_math(w_ref[...], gv, m_ref[...], v_ref[...])

    blk = pl.BlockSpec((None, tr, cols), lambda i: (0, i, 0))
    g_spec = (pl.BlockSpec((g.shape[0], tr, cols), lambda i: (0, i, 0)) if slots
              else pl.BlockSpec((tr, cols), lambda i: (i, 0)))
    sh = jax.ShapeDtypeStruct((1, rows, cols), F32)
    n_out = 4 if slots else 3
    return pl.pallas_call(
        body, name=name, grid=(rows // tr,), out_shape=(sh,) * n_out,
        in_specs=[blk, g_spec, blk, blk], out_specs=(blk,) * n_out,
        compiler_params=_params(("parallel",), VMEM_BIG),
    )(w, g, m, v)


def _rope_tables(s, l):
    tok = np.arange(s)
    row = (tok // GRID_W).astype(np.float32)
    col = (tok % GRID_W).astype(np.float32)
    half = QK_ROPE // 2
    freqs = np.float32(ROPE_THETA) ** (-np.arange(0, half, 2, dtype=np.float32) / np.float32(half))
    dd = np.arange(QK_ROPE)
    pos = np.where((dd // half)[None, :] == 0, row[:, None], col[:, None]).astype(np.float32)
    ang = (pos * freqs[dd % (half // 2)][None, :]).astype(np.float32)
    sin = np.sin(ang).astype(np.float32)
    cos_t = np.ones((s + l, LANES), np.float32)
    sgn_t = np.zeros((s + l, LANES), np.float32)
    cos_t[:s, QK_NOPE:QK_NOPE + QK_ROPE] = np.cos(ang)
    sgn_t[:s, QK_NOPE:QK_NOPE + QK_ROPE] = np.where(((dd % half) // (half // 2))[None, :] == 0, -sin, sin)
    return jnp.asarray(cos_t), jnp.asarray(sgn_t)


def _slots_to_cols(g):
    return g.transpose(1, 0, 2).reshape(g.shape[1], N_DEV * g.shape[2])


def _cols_to_slots(w):
    return w.reshape(w.shape[0], N_DEV, w.shape[1] // N_DEV).transpose(1, 0, 2)


def _unpack_small_weights(g_in_t, g_uq, g_ukv):
    w_t = g_in_t.reshape(N_DEV * g_in_t.shape[1], D_MODEL)
    zeros = jnp.zeros((QK_NOPE, D_MODEL), BF16)
    win_head_t = jnp.concatenate([w_t[:Q_RANK + KV_RANK], zeros, w_t[Q_RANK + KV_RANK:MLA_IN],
                                  zeros[:LANES - QK_NOPE - QK_ROPE]], axis=0)
    win_conv_t = w_t[MLA_IN:].reshape(3, CONV_W // LANES, LANES, D_MODEL).transpose(1, 0, 2, 3)
    win_conv_t = win_conv_t.reshape(3 * CONV_W, D_MODEL)
    w_uq = _slots_to_cols(g_uq).reshape(Q_RANK, N_HEADS, QK_NOPE + QK_ROPE)
    wq = jnp.pad(w_uq, ((0, 0), (0, 0), (0, LANES - QK_NOPE - QK_ROPE))).reshape(Q_RANK, N_HEADS * LANES)
    w_ukv = _slots_to_cols(g_ukv).reshape(KV_RANK, N_HEADS, QK_NOPE + V_DIM)
    k_top = jnp.pad(w_ukv[:, :, :QK_NOPE], ((0, 0), (0, 0), (0, LANES - QK_NOPE))).reshape(KV_RANK, N_HEADS * LANES)
    v_top = w_ukv[:, :, QK_NOPE:].reshape(KV_RANK, N_HEADS * V_DIM)
    eye = jnp.pad(jnp.eye(QK_ROPE, dtype=BF16), ((QK_NOPE, LANES - QK_NOPE - QK_ROPE),) * 2)
    wk = jnp.concatenate([
        jnp.concatenate([k_top, v_top], axis=1),
        jnp.concatenate([jnp.tile(eye, (1, N_HEADS)), jnp.zeros((LANES, N_HEADS * V_DIM), BF16)], axis=1)], axis=0)
    return win_head_t, win_conv_t, wq, wk


def _pack_small_grads(d_head_t, d_conv_t, d_wq, d_wkk, d_wkv):
    d_conv_t = d_conv_t.reshape(CONV_W // LANES, 3, LANES, D_MODEL).transpose(1, 0, 2, 3).reshape(3 * CONV_W, D_MODEL)
    rope0 = Q_RANK + KV_RANK + QK_NOPE
    g_in_t = jnp.concatenate([d_head_t[:Q_RANK + KV_RANK], d_head_t[rope0:rope0 + QK_ROPE], d_conv_t], axis=0)
    g_in_t = g_in_t.reshape(N_DEV, -1, D_MODEL).astype(BF16)
    g_uq = d_wq.reshape(Q_RANK, N_HEADS, LANES)[:, :, :QK_NOPE + QK_ROPE].reshape(Q_RANK, -1)
    g_kn = d_wkk[:KV_RANK].reshape(KV_RANK, N_HEADS, LANES)[:, :, :QK_NOPE]
    g_v = d_wkv[:KV_RANK].reshape(KV_RANK, N_HEADS, V_DIM)
    g_ukv = jnp.concatenate([g_kn, g_v], axis=2).reshape(KV_RANK, -1)
    return [g_in_t] + [_cols_to_slots(g).astype(BF16) for g in (g_uq, g_ukv)]


def kernel(x, c, ctx, c_ctx, w_mod, b_mod, w_in, q_norm_g, w_uq, kv_norm_g, w_ukv, conv_w, w_out, w_mlp1, w_mlp2, final_norm_g, loss_target, m_c_ctx, m_w_mod, m_b_mod, m_w_in, m_q_norm_g, m_w_uq, m_kv_norm_g, m_w_ukv, m_conv_w, m_w_out, m_w_mlp1, m_w_mlp2, m_final_norm_g, v_c_ctx, v_w_mod, v_b_mod, v_w_in, v_q_norm_g, v_w_uq, v_kv_norm_g, v_w_ukv, v_conv_w, v_w_out, v_w_mlp1, v_w_mlp2, v_final_norm_g):
    me = _my_index()
    x2d, ctx2d, tgt = x[0], ctx[0], loss_target[0]
    s, l = x2d.shape[0], ctx2d.shape[0]
    t = s + l
    d = D_MODEL
    mod_cols = w_mod.shape[2]
    cw_cols = conv_w.shape[2]

    b_cols = lax.dynamic_slice(b_mod, (0, me * mod_cols), (1, mod_cols))
    cw_blk = jnp.pad(conv_w[0], ((0, 5), (0, mod_cols - cw_cols)))
    a_rows, gathered = _prologue(jnp.pad(c, ((0, 7), (0, 0))), c_ctx[None, :], w_mod[0], b_cols, cw_blk,
                                 "prologue")
    mod_mine = lax.dynamic_index_in_dim(gathered, me, axis=1, keepdims=False).reshape(1, 6 * d)
    mod_ctx = gathered[:, 8, :].reshape(1, 6 * d)
    cw_full = gathered[:, 16:19, :cw_cols].transpose(1, 0, 2).reshape(3, CONV_W)

    early = [w.astype(BF16) for w in (w_in[0].T, w_uq[0], w_ukv[0])]
    late = [w.astype(BF16) for w in (w_out[0], w_mlp1[0], w_mlp2[0])]
    h_all, (g_in, g_uq, g_ukv) = _modulate_all(x2d, ctx2d, mod_mine, mod_ctx, _RidingGather(early),
                                               "modulate1")
    win_head, win_conv, wq, wk = _unpack_small_weights(g_in, g_uq, g_ukv)
    wk_k, wk_v = wk[:, :N_HEADS * LANES], wk[:, N_HEADS * LANES:]
    cos, sgn = _rope_tables(s, l)

    tm_t = _pick(t, (1088, 768, 256))
    tk_t = _pick(t, (2176, 768, 256))
    z_head, cq, kv_in, qf, kv = _head_fwd(h_all, win_head, wq, wk, q_norm_g, kv_norm_g, cos, sgn, tm_t, "head_fwd")
    z_conv = _matmul(h_all, win_conv, mode="nt", name="in_proj_conv", m=s, tm=1024, tn=1536, tk=1024)
    attn, a_cat, stats, (g_out, w1, g_w2) = _attn_fwd(qf, kv, s, _RidingGather(late), "attn_fwd")
    wo = g_out.reshape(d, d)
    w2 = g_w2.reshape(D_FF, d)
    a_cat = _conv_fwd(z_conv, cw_full, a_cat, "conv_fwd")
    (o, x1, h2), _ = _matmul_rows(a_cat, wo, _epi_resid_modulate, mode="nn", name="out_proj", tm=1024, tk=1024,
                                  rows=[x2d], vecs=[(mod_mine, 2), (mod_mine, 3), (mod_mine, 4)],
                                  out_dtypes=[F32, F32, BF16])
    u1, act = _matmul(h2, w1, mode="nn", name="mlp_up", tm=4096, tk=1024, epilogue="relu2", slots="b_cols")
    (dx2, dm, fsums), _ = _matmul_rows(act, w2, _epi_final, mode="nn", name="mlp_down", tm=512, tk=4096,
                                       rows=[x1, tgt], vecs=[(mod_mine, 5), (final_norm_g[None, :], 0)],
                                       out_dtypes=[F32, BF16], sums=True)

    d_w2 = _matmul(act, dm, mode="tn", name="d_w_mlp2", out_dtype=BF16, tm=1024, tn=1024, tk=4096)
    du1 = _matmul(dm, w2, mode="nt", name="d_act", out_dtype=BF16, tm=2048, tn=1024, tk=1024,
                  epilogue="drelu2", extra=(u1,))
    d_w1 = _matmul(h2, du1, mode="tn", name="d_w_mlp1", out_dtype=BF16, tm=1024, tk=4096, slots="out")
    (dx1, do, sums2), _ = _matmul_rows(du1, w1, _epi_modulate2_bwd, mode="nt", name="d_h2", tm=512, tk=4096,
                                       slots="b_contract", rows=[x1, dx2, o], vecs=[(mod_mine, 4), (mod_mine, 2)],
                                       out_dtypes=[F32, BF16], sums=True)
    d_wo = _matmul(a_cat, do, mode="tn", name="d_w_out", out_dtype=BF16, tm=1024, tn=1024, tk=2048)
    da = _matmul(do, wo, mode="nt", name="d_a", tm=1024, tn=1024, tk=1024)
    dz_conv, d_cw = _conv_bwd(z_conv, cw_full, da, "conv_bwd")
    ready = [d_wo.reshape(N_DEV, d // N_DEV, d), d_w1, d_w2.reshape(N_DEV, D_FF // N_DEV, d)]
    dq, dk, dv, rode = _attn_bwd(qf, kv, attn, da, stats, cos, sgn, _Riding("exchange", ready), "attn_bwd")
    d_wq = _matmul(cq, dq, mode="tn", name="d_w_uq", k=s, tm=256, tn=1024, tk=4096)
    d_wkk = _matmul(kv_in, dk, mode="tn", name="d_w_ukv_k", tm=256, tn=1024, tk=tk_t)
    d_wkv = _matmul(kv_in, dv, mode="tn", name="d_w_ukv_v", tm=256, tn=512, tk=tk_t)
    dz_head, dh_head, psums = _head_bwd(dq, dk, dv, z_head, wq, wk_k, wk_v, win_head, q_norm_g, kv_norm_g, cos, sgn, s,
                                        "head_bwd")
    d_head = _matmul(dz_head, h_all, mode="tn", name="d_w_in_head", tm=512, tn=1024, tk=tk_t)
    d_conv = _matmul(dz_conv, h_all, mode="tn", name="d_w_in_conv", k=s, tm=1536, tn=1024, tk=2048)
    send = _pack_small_grads(d_head, d_conv, d_wq, d_wkk, d_wkv)
    (grad_x, sums1), got = _matmul_rows(dz_conv, win_conv, _epi_modulate1_bwd, mode="nn", name="d_h1", tm=s // 4,
                                        tk=win_conv.shape[0], rows=[dh_head, x2d, dx1], vecs=[(mod_mine, 1)],
                                        out_dtypes=[F32], sums=True, riding=_RidingReduce(send))
    sums1c = _modulate_sums(dh_head, s // ROW_TILE, ctx2d)

    small = _pack_small(sums1, sums2, fsums, sums1c, psums, d_cw, "pack_small")
    (d_all,) = _all_gather([small], "gather_small_grads", True)
    d_cols = lax.dynamic_slice_in_dim(d_all, me * mod_cols, mod_cols, axis=2)
    d_cw_cols = lax.dynamic_slice(d_all, (0, 3, me * cw_cols), (N_DEV, 3, cw_cols))

    slots = dict(zip(["w_in", "w_uq", "w_ukv"], got))
    slots.update(zip(["w_out", "w_mlp1", "w_mlp2"], rode))

    grads = {}
    weights = {"c_ctx": c_ctx, "w_mod": w_mod, "b_mod": b_mod, "w_in": w_in, "q_norm_g": q_norm_g, "w_uq": w_uq,
               "kv_norm_g": kv_norm_g, "w_ukv": w_ukv, "conv_w": conv_w, "w_out": w_out, "w_mlp1": w_mlp1,
               "w_mlp2": w_mlp2, "final_norm_g": final_norm_g}
    m_in = {"c_ctx": m_c_ctx, "w_mod": m_w_mod, "b_mod": m_b_mod, "w_in": m_w_in, "q_norm_g": m_q_norm_g,
            "w_uq": m_w_uq, "kv_norm_g": m_kv_norm_g, "w_ukv": m_w_ukv, "conv_w": m_conv_w, "w_out": m_w_out,
            "w_mlp1": m_w_mlp1, "w_mlp2": m_w_mlp2, "final_norm_g": m_final_norm_g}
    v_in = {"c_ctx": v_c_ctx, "w_mod": v_w_mod, "b_mod": v_b_mod, "w_in": v_w_in, "q_norm_g": v_q_norm_g,
            "w_uq": v_w_uq, "kv_norm_g": v_kv_norm_g, "w_ukv": v_w_ukv, "conv_w": v_conv_w, "w_out": v_w_out,
            "w_mlp1": v_w_mlp1, "w_mlp2": v_w_mlp2, "final_norm_g": v_final_norm_g}
    names = list(weights)
    small_names = ["c_ctx", "b_mod", "q_norm_g", "kv_norm_g", "final_norm_g", "conv_w"]
    delta, new_m, new_v = {}, {}, {}

    def as_rows(a):
        return a[None, :] if a.ndim == 1 else a

    g_w_mod, dsum, small_out = _adaln_bwd_update(
        a_rows.T, w_mod[0], d_cols[:, 0, :], d_cols[:, 1, :], d_all, d_cw_cols,
        [[as_rows(src[n]) for src in (weights, m_in, v_in)] for n in small_names], "adaln_bwd_update")
    loss = dsum[6, 0]
    for n, outs in zip(small_names, small_out):
        grads[n], delta[n], new_m[n], new_v[n] = [a.reshape(weights[n].shape) for a in outs]
    for n in names:
        if n in small_names:
            continue
        if n == "w_in":
            wmv = [jnp.swapaxes(src[n], 1, 2) for src in (weights, m_in, v_in)]
            outs = _adamw(wmv[0], slots[n], wmv[1], wmv[2], "adamw_" + n, slots=True)
            grads[n], delta[n], new_m[n], new_v[n] = [jnp.swapaxes(a, 1, 2) for a in outs]
        elif n in slots:
            grads[n], delta[n], new_m[n], new_v[n] = _adamw(weights[n], slots[n], m_in[n], v_in[n], "adamw_" + n,
                                                            slots=True)
        else:
            delta[n], new_m[n], new_v[n] = _adamw(weights[n], g_w_mod, m_in[n], v_in[n], "adamw_" + n)
            grads[n] = g_w_mod[None]

    return (loss, grad_x[None], *[grads[n] for n in names], *[delta[n] for n in names],
            *[new_m[n] for n in names], *[new_v[n] for n in names])
```

```python
import math

import jax
import jax.numpy as jnp
import numpy as np
from jax import lax
from jax.experimental import pallas as pl
from jax.experimental.pallas import tpu as pltpu

F32 = jnp.float32
BF16 = jnp.bfloat16

D_MODEL = 1024
GRID_W = 64
N_HEADS = 8
QK_NOPE = 64
QK_ROPE = 32
V_DIM = 64
Q_RANK = 256
KV_RANK = 128
MLA_IN = Q_RANK + KV_RANK + QK_ROPE
CONV_W = 512
HEAD_COLS = 512
D_FF = 4096
ROPE_THETA = 10000.0
EPS = 1e-6
ATTN_SCALE = 1.0 / math.sqrt(QK_NOPE + QK_ROPE)
LOG2_E = 1.0 / math.log(2.0)
EXP2_SCALE = ATTN_SCALE * LOG2_E
N_DEV = 8
LANES = 128

ADAM_LR, ADAM_B1, ADAM_B2, ADAM_EPS, ADAM_WD, ADAM_STEP = 0.001, 0.9, 0.999, 1e-08, 0.01, 10

ROW_TILE = 256
VMEM_BIG = 60 * 1024 * 1024


def _params(sem=None, vmem=None):
    return pltpu.CompilerParams(dimension_semantics=sem, vmem_limit_bytes=vmem)


def _pick(n, prefs):
    for p in prefs:
        if n % p == 0:
            return p
    return n


def _my_index():
    return 4 * lax.axis_index("x") + 2 * lax.axis_index("y") + lax.axis_index("c")


def _two_level_gather(x_refs, out_refs, send_sems, recv_sems, local_sems):
    n = len(x_refs)
    x, y, c = lax.axis_index("x"), lax.axis_index("y"), lax.axis_index("c")
    me, sibling = (x, y, c), (x, y, 1 - c)
    chips = [(1 - x, y), (x, 1 - y), (1 - x, 1 - y)]

    def slot(a, px, py, pc):
        return out_refs[a].at[4 * px + 2 * py + pc]

    def copy(a, k, block, to, src=None):
        return pltpu.make_async_remote_copy(
            src_ref=slot(a, *block) if src is None else src, dst_ref=slot(a, *block),
            send_sem=send_sems.at[7 * a + k], recv_sem=recv_sems.at[7 * a + k],
            device_id=to, device_id_type=pl.DeviceIdType.MESH)

    mine = [pltpu.make_async_copy(x_refs[a], slot(a, *me), local_sems.at[a]) for a in range(n)]
    first = [cp for a in range(n) for cp in
             [copy(a, 0, me, sibling, src=x_refs[a])]
             + [copy(a, 1 + j, me, (*chip, c), src=x_refs[a]) for j, chip in enumerate(chips)]]
    passed = [[copy(a, 4 + j, (*chip, c), sibling) for j, chip in enumerate(chips)] for a in range(n)]

    def start():
        for cp in mine + first:
            cp.start()

    def forward():
        for a in range(n):
            for j, chip in enumerate(chips):
                copy(a, 1 + j, (*chip, c), me).wait_recv()
                passed[a][j].start()

    def finish():
        for a in range(n):
            copy(a, 0, sibling, me).wait_recv()
            for j, chip in enumerate(chips):
                copy(a, 4 + j, (*chip, 1 - c), me).wait_recv()
        for cp in first + [cp for per_array in passed for cp in per_array]:
            cp.wait_send()
        for cp in mine:
            cp.wait()

    return start, forward, finish


def _direct_gather(src_ref, dst_ref, send_sems, recv_sems):
    x, y, c = lax.axis_index("x"), lax.axis_index("y"), lax.axis_index("c")
    me = 4 * x + 2 * y + c
    dst_ref[me] = src_ref[...]
    sends, landings = [], []
    for k in range(1, N_DEV):
        peer = (1 - x if k & 4 else x, 1 - y if k & 2 else y, 1 - c if k & 1 else c)
        pid = 4 * peer[0] + 2 * peer[1] + peer[2]
        for dst, out in ((me, sends), (pid, landings)):
            out.append(pltpu.make_async_remote_copy(
                src_ref=src_ref, dst_ref=dst_ref.at[dst], send_sem=send_sems.at[k - 1], recv_sem=recv_sems.at[k - 1],
                device_id=peer, device_id_type=pl.DeviceIdType.MESH))
    for cp in sends:
        cp.start()

    def finish():
        for cp in landings:
            cp.wait_recv()
        for cp in sends:
            cp.wait_send()

    return finish


def _all_gather(arrays, name, in_vmem):
    space = pltpu.VMEM if in_vmem else pl.ANY
    n = len(arrays)

    def body(*refs):
        for phase in _two_level_gather(refs[:n], refs[n:2 * n], *refs[2 * n:]):
            phase()

    outs = pl.pallas_call(
        body, name=name,
        out_shape=tuple(jax.ShapeDtypeStruct((N_DEV,) + a.shape, a.dtype) for a in arrays),
        in_specs=[pl.BlockSpec(memory_space=space)] * n,
        out_specs=tuple(pl.BlockSpec(memory_space=space) for _ in arrays),
        scratch_shapes=[pltpu.SemaphoreType.DMA((7 * n,)), pltpu.SemaphoreType.DMA((7 * n,)),
                        pltpu.SemaphoreType.DMA((n,))],
    )(*arrays)
    return list(outs)


class _Riding:
    def __init__(self, arrays=()):
        self.arrays, self.n = list(arrays), len(arrays)
        self.out_shape = [jax.ShapeDtypeStruct(a.shape, a.dtype) for a in self.arrays]
        self.specs = [pl.BlockSpec(memory_space=pl.ANY)] * self.n
        self.scratch = [pltpu.SemaphoreType.DMA((7 * self.n,)), pltpu.SemaphoreType.DMA((7 * self.n,)),
                        pltpu.SemaphoreType.DMA((self.n,))]

    def copies(self, x_refs, y_refs, send_sems, recv_sems, local_sems):
        x, y, c = lax.axis_index("x"), lax.axis_index("y"), lax.axis_index("c")
        me = 4 * x + 2 * y + c
        local, sends, landings = [], [], []
        for a in range(self.n):
            local.append(pltpu.make_async_copy(x_refs[a].at[me], y_refs[a].at[me], local_sems.at[a]))
            for k in range(1, N_DEV):
                peer = (1 - x if k & 4 else x, 1 - y if k & 2 else y, 1 - c if k & 1 else c)
                pid = 4 * peer[0] + 2 * peer[1] + peer[2]
                for dst, out in ((me, sends), (pid, landings)):
                    out.append(pltpu.make_async_remote_copy(
                        src_ref=x_refs[a].at[pid], dst_ref=y_refs[a].at[dst],
                        send_sem=send_sems.at[7 * a + k - 1], recv_sem=recv_sems.at[7 * a + k - 1],
                        device_id=peer, device_id_type=pl.DeviceIdType.MESH))
        return local, sends, landings

    def run(self, first, last, x_refs, y_refs, sems, middle=None):
        if self.n == 0:
            return None
        local, sends, landings = self.copies(x_refs, y_refs, *sems)

        @pl.when(first)
        def _():
            for cp in local + sends:
                cp.start()

        return local, sends, landings, last

    @staticmethod
    def finish(state):
        if state is None:
            return
        local, sends, landings, last = state

        @pl.when(last)
        def _():
            for cp in landings:
                cp.wait_recv()
            for cp in sends:
                cp.wait_send()
            for cp in local:
                cp.wait()


class _RidingGather:
    def __init__(self, arrays):
        self.arrays, self.n = list(arrays), len(arrays)
        self.out_shape = [jax.ShapeDtypeStruct((N_DEV,) + a.shape, a.dtype) for a in self.arrays]
        self.specs = [pl.BlockSpec(memory_space=pl.ANY)] * self.n
        self.scratch = [pltpu.SemaphoreType.DMA((7 * self.n,)), pltpu.SemaphoreType.DMA((7 * self.n,)),
                        pltpu.SemaphoreType.DMA((self.n,))]

    def run(self, first, last, x_refs, y_refs, sems, middle):
        start, forward, finish = _two_level_gather(x_refs, y_refs, *sems)
        pl.when(first)(start)
        pl.when(middle)(forward)
        return finish, last

    @staticmethod
    def finish(state):
        finish, last = state
        pl.when(last)(finish)


class _RidingReduce:
    def __init__(self, arrays):
        self.arrays, self.n = list(arrays), len(arrays)
        self.out_shape = [jax.ShapeDtypeStruct((4,) + a.shape[1:], a.dtype) for a in self.arrays]
        self.specs = [pl.BlockSpec(memory_space=pl.ANY)] * self.n
        self.scratch = [pltpu.VMEM((4,) + a.shape[1:], a.dtype) for a in self.arrays for _ in range(3)]
        self.scratch += [pltpu.SemaphoreType.DMA((self.n,)) for _ in range(6)]

    def run(self, first, last, x_refs, y_refs, scratch, middle):
        n = self.n
        own, sib, tot = scratch[0:3 * n:3], scratch[1:3 * n:3], scratch[2:3 * n:3]
        d2d_send, d2d_recv, local_in, ici_send, ici_recv, local_out = scratch[3 * n:]
        x, y, c = lax.axis_index("x"), lax.axis_index("y"), lax.axis_index("c")
        my_chip = 2 * x + y
        sibling = (x, y, 1 - c)
        others = [(1 - x, y), (x, 1 - y), (1 - x, 1 - y)]

        def to_sibling(a, j=None):
            src = x_refs[a].at[pl.ds(0, 4)] if j is None else x_refs[a].at[2 * j + 1 - c]
            dst = sib[a] if j is None else sib[a].at[j]
            return pltpu.make_async_remote_copy(src_ref=src, dst_ref=dst, send_sem=d2d_send.at[a],
                                                recv_sem=d2d_recv.at[a], device_id=sibling,
                                                device_id_type=pl.DeviceIdType.MESH)

        def mine_in(a, j=None):
            src = x_refs[a].at[pl.ds(0, 4)] if j is None else x_refs[a].at[2 * j + c]
            return pltpu.make_async_copy(src, own[a] if j is None else own[a].at[j], local_in.at[a])

        def to_chip(a, chip=None):
            if chip is None:
                src, dst, peer = tot[a].at[pl.ds(0, 3)], y_refs[a].at[pl.ds(0, 3)], sibling
            else:
                src, dst, peer = tot[a].at[2 * chip[0] + chip[1]], y_refs[a].at[my_chip], (*chip, c)
            return pltpu.make_async_remote_copy(src_ref=src, dst_ref=dst, send_sem=ici_send.at[a],
                                                recv_sem=ici_recv.at[a], device_id=peer,
                                                device_id_type=pl.DeviceIdType.MESH)

        def mine_out(a):
            return pltpu.make_async_copy(tot[a].at[my_chip], y_refs[a].at[my_chip], local_out.at[a])

        @pl.when(first)
        def _():
            for a in range(n):
                for j in range(4):
                    to_sibling(a, j).start()
                    mine_in(a, j).start()

        @pl.when(middle)
        def _():
            for a in range(n):
                to_sibling(a).wait_recv()
                to_sibling(a).wait_send()
                mine_in(a).wait()
                tot[a][...] = (own[a][...].astype(F32) + sib[a][...].astype(F32)).astype(tot[a].dtype)
                for chip in others:
                    to_chip(a, chip).start()
                mine_out(a).start()

        def finish():
            @pl.when(last)
            def _():
                for a in range(n):
                    to_chip(a).wait_recv()
                    to_chip(a).wait_send()
                    mine_out(a).wait()

        return finish

    @staticmethod
    def finish(state):
        state()


_DIMS ={"nn": (((1,), (0,)), ((), ())), "nt": (((1,), (1,)), ((), ())), "tn": (((0,), (0,)), ((), ()))}
NT_DIMS = _DIMS["nt"]
TN_DIMS = _DIMS["tn"]


def _swap8(x):
    lane = lax.broadcasted_iota(jnp.int32, x.shape, 1)
    return jnp.where((lane & 15) < 8, pltpu.roll(x, LANES - 8, 1), pltpu.roll(x, 8, 1))


def _rope(x, cos, sgn, bwd):
    return x * cos + (_swap8(x * sgn) if bwd else _swap8(x) * sgn)


def _matmul(a, b, *, mode, name, out_dtype=F32, tm=512, tn=512, tk=512, m=None, k=None,
            epilogue=None, extra=(), slots=None):
    if mode == "nn":
        m = a.shape[0] if m is None else m
        k = a.shape[1]
        n = N_DEV * b.shape[2] if slots == "b_cols" else b.shape[1]
    elif mode == "nt":
        m = a.shape[0] if m is None else m
        k = a.shape[1]
        n = b.shape[0]
    else:
        k = a.shape[0] if k is None else k
        m, n = a.shape[1], b.shape[1]
    tm, tn, tk = min(tm, m), min(tn, n), min(tk, k)
    if slots == "b_cols":
        tn = b.shape[2]
    if slots == "out":
        tn = n // N_DEV
    assert m % tm == 0 and n % tn == 0 and k % tk == 0, (name, m, n, k, tm, tn, tk)
    nk = k // tk
    dims = _DIMS[mode]
    a_spec = (pl.BlockSpec((tk, tm), lambda i, j, kk: (kk, i)) if mode == "tn"
              else pl.BlockSpec((tm, tk), lambda i, j, kk: (i, kk)))
    if slots == "b_cols":
        b_spec = pl.BlockSpec((None, tk, tn), lambda i, j, kk: (j, kk, 0))
    elif mode == "nt":
        b_spec = pl.BlockSpec((tn, tk), lambda i, j, kk: (j, kk))
    else:
        b_spec = pl.BlockSpec((tk, tn), lambda i, j, kk: (kk, j))
    tile = pl.BlockSpec((tm, tn), lambda i, j, kk: (i, j))
    if slots == "out":
        o_spec = pl.BlockSpec((None, tm, tn), lambda i, j, kk: (j, i, 0))
        o_shape = (N_DEV, m, tn)
    else:
        o_spec, o_shape = tile, (m, n)
    in_specs, args = [a_spec, b_spec], [a, b]
    if epilogue == "drelu2":
        in_specs.append(tile)
    args += list(extra)
    if epilogue == "relu2":
        out_shape = (jax.ShapeDtypeStruct(o_shape, BF16), jax.ShapeDtypeStruct(o_shape, BF16))
        out_specs = (o_spec, o_spec)
    else:
        out_shape = jax.ShapeDtypeStruct(o_shape, out_dtype)
        out_specs = o_spec
    n_in = len(args)
    n_out = 2 if epilogue == "relu2" else 1

    def body(*refs):
        a_ref, b_ref = refs[0], refs[1]
        outs = refs[n_in:n_in + n_out]
        part = lax.dot_general(a_ref[...], b_ref[...], dims, preferred_element_type=F32)

        def finish(acc):
            if epilogue == "relu2":
                outs[0][...] = acc.astype(BF16)
                r = jnp.maximum(acc, 0.0)
                outs[1][...] = (r * r).astype(BF16)
            elif epilogue == "drelu2":
                u = refs[2][...].astype(F32)
                outs[0][...] = (acc * (2.0 * jnp.maximum(u, 0.0))).astype(out_dtype)
            else:
                outs[0][...] = acc.astype(out_dtype)

        if nk == 1:
            finish(part)
        else:
            acc_ref = refs[n_in + n_out]
            kk = pl.program_id(2)

            @pl.when(kk == 0)
            def _():
                acc_ref[...] = part

            @pl.when(kk > 0)
            def _():
                acc_ref[...] += part

            @pl.when(kk == nk - 1)
            def _():
                finish(acc_ref[...])

    return pl.pallas_call(
        body, name=name, grid=(m // tm, n // tn, nk),
        out_shape=out_shape, in_specs=in_specs, out_specs=out_specs,
        scratch_shapes=[pltpu.VMEM((tm, tn), F32)] if nk > 1 else [],
        compiler_params=_params(("parallel", "parallel", "arbitrary"), VMEM_BIG),
    )(*args)


def _rstd(x):
    return lax.rsqrt(jnp.mean(x * x, axis=1, keepdims=True) + EPS)


def _norm_bwd(dxn, xn, r):
    return r * (dxn - xn * jnp.mean(dxn * xn, axis=1, keepdims=True))


def _vec(col):
    return pl.BlockSpec((1, D_MODEL), lambda i: (0, col))


def _matmul_rows(a, b, epi, *, mode, name, tm, tk, rows=(), vecs=(), out_dtypes=(), sums=False, slots=None,
                 riding=None):
    m, k = a.shape
    n = D_MODEL
    tm, tk = min(tm, m), min(tk, k)
    riding = riding or _Riding()
    group = 1
    if slots == "b_contract":
        group = max(1, tk // b.shape[2])
        tk = group * b.shape[2]
        b_spec = pl.BlockSpec((group, n, tk // group), lambda i, kk: (kk, 0, 0))
    elif mode == "nt":
        b_spec = pl.BlockSpec((n, tk), lambda i, kk: (0, kk))
    else:
        b_spec = pl.BlockSpec((tk, n), lambda i, kk: (kk, 0))
    assert m % tm == 0 and k % tk == 0, (name, m, k, tm, tk)
    ni, nk = m // tm, k // tk
    assert ni >= 2 or not isinstance(riding, _RidingReduce), "the two-level exchange needs a middle grid step"
    dims = _DIMS[mode]
    tile = pl.BlockSpec((tm, n), lambda i, kk: (i, 0))
    in_specs = [pl.BlockSpec((tm, tk), lambda i, kk: (i, kk)), b_spec] + [tile] * len(rows)
    in_specs += [pl.BlockSpec((1, n), lambda i, kk, col=col: (0, col)) for _, col in vecs]
    args = [a, b, *rows, *[v for v, _ in vecs]]
    out_shape = [jax.ShapeDtypeStruct((m, n), dt) for dt in out_dtypes]
    out_specs = [tile] * len(out_dtypes)
    if sums:
        out_shape.append(jax.ShapeDtypeStruct((8, n), F32))
        out_specs.append(pl.BlockSpec((8, n), lambda i, kk: (0, 0)))
    n_rows, n_vecs, n_outs, nr = len(rows), len(vecs), len(out_dtypes), riding.n
    n_in = 2 + n_rows + n_vecs

    def body(*refs):
        a_ref, b_ref = refs[0], refs[1]
        row_refs = refs[2:2 + n_rows]
        vec_refs = refs[2 + n_rows:n_in]
        x_refs = refs[n_in:n_in + nr]
        out_refs = refs[n_in + nr:n_in + nr + n_outs]
        pos = n_in + nr + n_outs
        sums_ref = refs[pos] if sums else None
        pos += 1 if sums else 0
        y_refs = refs[pos:pos + nr]
        pos += nr
        acc_ref = refs[pos] if nk > 1 else None
        sem_refs = refs[pos + (1 if nk > 1 else 0):]
        i, kk = pl.program_id(0), pl.program_id(1)
        state = riding.run((i == 0) & (kk == 0), (i == ni - 1) & (kk == nk - 1), x_refs, y_refs, sem_refs,
                           middle=(i == 1) & (kk == 0))
        if slots == "b_contract":
            c = tk // group
            part = lax.dot_general(a_ref[:, 0:c], b_ref[0], dims, preferred_element_type=F32)
            for u in range(1, group):
                part = part + lax.dot_general(a_ref[:, u * c:(u + 1) * c], b_ref[u], dims, preferred_element_type=F32)
        else:
            part = lax.dot_general(a_ref[...], b_ref[...], dims, preferred_element_type=F32)

        def finish(acc):
            nsub = tm // ROW_TILE
            for r in range(nsub):
                blk = pl.ds(r * ROW_TILE, ROW_TILE)
                epi(acc[r * ROW_TILE:(r + 1) * ROW_TILE], [ref.at[blk] for ref in row_refs], vec_refs,
                    [ref.at[blk] for ref in out_refs], sums_ref,
                    (i == 0) if r == 0 else None, (i == ni - 1) if r == nsub - 1 else None)

        if nk == 1:
            finish(part)
        else:
            @pl.when(kk == 0)
            def _():
                acc_ref[...] = part

            @pl.when(kk > 0)
            def _():
                acc_ref[...] += part

            @pl.when(kk == nk - 1)
            def _():
                finish(acc_ref)

        riding.finish(state)

    outs = pl.pallas_call(
        body, name=name, grid=(ni, nk),
        out_shape=(*out_shape, *riding.out_shape),
        in_specs=[*in_specs, *riding.specs], out_specs=(*out_specs, *riding.specs),
        scratch_shapes=([pltpu.VMEM((tm, n), F32)] if nk > 1 else []) + (riding.scratch if nr else []),
        compiler_params=_params(("arbitrary", "arbitrary"), VMEM_BIG),
    )(*args, *riding.arrays)
    n_own = len(out_shape)
    return list(outs[:n_own]), list(outs[n_own:])


def _zero_sums_at_start(sums_ref, first):
    if first is not None:
        @pl.when(first)
        def _():
            sums_ref[...] = jnp.zeros_like(sums_ref)


def _epi_resid_modulate(acc, rows, vecs, outs, sums_ref, first, last):
    (x_ref,), (g_ref, sh_ref, sc_ref) = rows, vecs
    x1 = x_ref[...] + g_ref[...] * acc
    outs[0][...] = acc
    outs[1][...] = x1
    outs[2][...] = (x1 * _rstd(x1) * (1.0 + sc_ref[...]) + sh_ref[...]).astype(BF16)


def _epi_final(acc, rows, vecs, outs, sums_ref, first, last):
    (x1_ref, t_ref), (g_ref, gf_ref) = rows, vecs
    d = acc.shape[1]
    x2 = x1_ref[...] + g_ref[...] * acc
    r = _rstd(x2)
    xn = x2 * r
    err = xn * gf_ref[...] - t_ref[...]
    dy = err * (1.0 / d)
    dx2 = _norm_bwd(dy * gf_ref[...], xn, r)
    outs[0][...] = dx2
    outs[1][...] = (dx2 * g_ref[...]).astype(BF16)
    _zero_sums_at_start(sums_ref, first)
    sums_ref[0:1, :] += jnp.sum(dy * xn, axis=0, keepdims=True)
    sums_ref[1:2, :] += jnp.sum(dx2 * acc, axis=0, keepdims=True)
    sums_ref[2:3, :] += jnp.sum(err * err, axis=0, keepdims=True)

    if last is not None:
        @pl.when(last)
        def _():
            tot = jnp.sum(sums_ref[2:3, :], axis=1, keepdims=True) * (0.5 / d)
            sums_ref[3:4, :] = jnp.broadcast_to(tot, (1, d))


def _epi_modulate2_bwd(acc, rows, vecs, outs, sums_ref, first, last):
    (x_ref, dres_ref, o_ref), (sc_ref, g_ref) = rows, vecs
    x = x_ref[...]
    r = _rstd(x)
    xn = x * r
    dx = dres_ref[...] + _norm_bwd(acc * (1.0 + sc_ref[...]), xn, r)
    outs[0][...] = dx
    outs[1][...] = (dx * g_ref[...]).astype(BF16)
    _zero_sums_at_start(sums_ref, first)
    sums_ref[0:1, :] += jnp.sum(acc * xn, axis=0, keepdims=True)
    sums_ref[1:2, :] += jnp.sum(acc, axis=0, keepdims=True)
    sums_ref[2:3, :] += jnp.sum(dx * o_ref[...], axis=0, keepdims=True)


def _epi_modulate1_bwd(acc, rows, vecs, outs, sums_ref, first, last):
    (add_ref, x_ref, dres_ref), (sc_ref,) = rows, vecs
    dh = acc + add_ref[...]
    x = x_ref[...]
    r = _rstd(x)
    xn = x * r
    outs[0][...] = dres_ref[...] + _norm_bwd(dh * (1.0 + sc_ref[...]), xn, r)
    _zero_sums_at_start(sums_ref, first)
    sums_ref[0:1, :] += jnp.sum(dh * xn, axis=0, keepdims=True)
    sums_ref[1:2, :] += jnp.sum(dh, axis=0, keepdims=True)


def _modulate_all(x, ctx, mod, mod_ctx, riding, name):
    s, d = x.shape
    t = s + ctx.shape[0]
    ns = s // ROW_TILE
    nc = ctx.shape[0] // ROW_TILE
    nr = riding.n

    def body(*refs):
        x_ref, c_ref, sh_ref, sc_ref, shc_ref, scc_ref = refs[:6]
        h_ref = refs[6 + nr]
        i = pl.program_id(0)
        state = riding.run(i == 0, i == ns + nc - 1, refs[6:6 + nr], refs[7 + nr:7 + 2 * nr], refs[7 + 2 * nr:],
                           middle=i == ns + nc - 3)

        @pl.when(i < ns)
        def _():
            v = x_ref[...]
            h_ref[...] = (v * _rstd(v) * (1.0 + sc_ref[...]) + sh_ref[...]).astype(BF16)

        @pl.when(i >= ns)
        def _():
            v = c_ref[...]
            h_ref[...] = (v * _rstd(v) * (1.0 + scc_ref[...]) + shc_ref[...]).astype(BF16)

        riding.finish(state)

    outs = pl.pallas_call(
        body, name=name, grid=(ns + nc,),
        out_shape=(jax.ShapeDtypeStruct((t, d), BF16), *riding.out_shape),
        in_specs=[pl.BlockSpec((ROW_TILE, d), lambda i: (jnp.minimum(i, ns - 1), 0)),
                  pl.BlockSpec((ROW_TILE, d), lambda i: (jnp.maximum(i - ns, 0), 0)),
                  _vec(0), _vec(1), _vec(0), _vec(1), *riding.specs],
        out_specs=(pl.BlockSpec((ROW_TILE, d), lambda i: (i, 0)), *riding.specs),
        scratch_shapes=riding.scratch,
        compiler_params=_params(("arbitrary",)),
    )(x, ctx, mod, mod, mod_ctx, mod_ctx, *riding.arrays)
    return outs[0], list(outs[1:])


def _modulate_sums(dh, row_off, xsrc):
    s, d = xsrc.shape

    def body(dh_ref, x_ref, sums_ref):
        i = pl.program_id(0)
        x = x_ref[...]
        dhv = dh_ref[...]

        @pl.when(i == 0)
        def _():
            sums_ref[...] = jnp.zeros_like(sums_ref)

        sums_ref[0:1, :] += jnp.sum(dhv * (x * _rstd(x)), axis=0, keepdims=True)
        sums_ref[1:2, :] += jnp.sum(dhv, axis=0, keepdims=True)

    return pl.pallas_call(
        body, name="modulate1_ctx_bwd", grid=(s // ROW_TILE,),
        out_shape=jax.ShapeDtypeStruct((8, d), F32),
        in_specs=[pl.BlockSpec((ROW_TILE, d), lambda i: (i + row_off, 0)), pl.BlockSpec((ROW_TILE, d), lambda i: (i, 0))],
        out_specs=pl.BlockSpec((8, d), lambda i: (0, 0)),
        compiler_params=_params(("arbitrary",)),
    )(dh, xsrc)


def _head_fwd(h_all, win_head, wq, wk, q_gain, kv_gain, cos, sgn, tm, name):
    t, d = h_all.shape
    nq, nkv = wq.shape[1], wk.shape[1]

    def body(h_ref, wi_ref, wq_ref, wk_ref, qg_ref, kg_ref, c_ref, s_ref, z_ref, cq_ref, kvin_ref, qf_ref, kv_ref):
        z = lax.dot_general(h_ref[...], wi_ref[...], NT_DIMS, preferred_element_type=F32)
        z_ref[...] = z
        cos, sgn = c_ref[...], s_ref[...]
        zq = z[:, 0:Q_RANK]
        cq = (zq * _rstd(zq) * qg_ref[...]).astype(BF16)
        cq_ref[...] = cq
        zk = z[:, Q_RANK:Q_RANK + KV_RANK]
        kv_in = jnp.concatenate([(zk * _rstd(zk) * kg_ref[...]).astype(BF16),
                                 _rope(z[:, Q_RANK + KV_RANK:HEAD_COLS], cos, sgn, False).astype(BF16)], axis=1)
        kvin_ref[...] = kv_in
        q = jnp.dot(cq, wq_ref[...], preferred_element_type=F32)
        for h in range(nq // LANES):
            sl = slice(h * LANES, (h + 1) * LANES)
            qf_ref[:, sl] = _rope(q[:, sl], cos, sgn, False).astype(BF16)
        kv_ref[...] = jnp.dot(kv_in, wk_ref[...], preferred_element_type=F32).astype(BF16)

    def row(w):
        return pl.BlockSpec((tm, w), lambda i: (i, 0))

    def whole(a):
        return pl.BlockSpec(a.shape, lambda i: (0, 0))

    return pl.pallas_call(
        body, name=name, grid=(t // tm,),
        out_shape=(jax.ShapeDtypeStruct((t, HEAD_COLS), F32), jax.ShapeDtypeStruct((t, Q_RANK), BF16),
                   jax.ShapeDtypeStruct((t, KV_RANK + LANES), BF16), jax.ShapeDtypeStruct((t, nq), BF16),
                   jax.ShapeDtypeStruct((t, nkv), BF16)),
        in_specs=[row(d), whole(win_head), whole(wq), whole(wk), whole(q_gain), whole(kv_gain), row(LANES), row(LANES)],
        out_specs=(row(HEAD_COLS), row(Q_RANK), row(KV_RANK + LANES), row(nq), row(nkv)),
        compiler_params=_params(("parallel",), VMEM_BIG),
    )(h_all, win_head, wq, wk, q_gain, kv_gain, cos, sgn)


def _head_bwd(dq, dk, dv, z, wq, wk_k, wk_v, win_head, q_gain, kv_gain, cos, sgn, s, name):
    t = z.shape[0]
    ns = s // ROW_TILE

    def body(dq_ref, dk_ref, dv_ref, z_ref, wq_ref, wkk_ref, wkv_ref, wi_ref, qg_ref, kg_ref, c_ref, s_ref,
             dz_ref, dh_ref, sums_ref):
        i = pl.program_id(0)

        @pl.when(i == 0)
        def _():
            sums_ref[...] = jnp.zeros_like(sums_ref)

        @pl.when(i < ns)
        def _():
            dc = lax.dot_general(dq_ref[...], wq_ref[...], NT_DIMS, preferred_element_type=F32)
            zq = z_ref[:, 0:Q_RANK]
            r = _rstd(zq)
            zn = zq * r
            sums_ref[0:1, :] += jnp.sum(dc * zn, axis=0, keepdims=True)
            dz_ref[:, 0:Q_RANK] = _norm_bwd(dc * qg_ref[...], zn, r).astype(BF16)

        @pl.when(i >= ns)
        def _():
            dz_ref[:, 0:Q_RANK] = jnp.zeros((ROW_TILE, Q_RANK), BF16)

        dkv = (lax.dot_general(dk_ref[...], wkk_ref[...], NT_DIMS, preferred_element_type=F32)
               + lax.dot_general(dv_ref[...], wkv_ref[...], NT_DIMS, preferred_element_type=F32))
        zk = z_ref[:, Q_RANK:Q_RANK + KV_RANK]
        r = _rstd(zk)
        zn = zk * r
        dc = dkv[:, 0:KV_RANK]
        sums_ref[1:2, 0:KV_RANK] += jnp.sum(dc * zn, axis=0, keepdims=True)
        dz_ref[:, Q_RANK:Q_RANK + KV_RANK] = _norm_bwd(dc * kg_ref[...], zn, r).astype(BF16)
        dz_ref[:, Q_RANK + KV_RANK:HEAD_COLS] = _rope(dkv[:, KV_RANK:KV_RANK + LANES], c_ref[...], s_ref[...],
                                                       True).astype(BF16)
        dh_ref[...] = jnp.dot(dz_ref[...], wi_ref[...], preferred_element_type=F32)

    def row(w):
        return pl.BlockSpec((ROW_TILE, w), lambda i: (i, 0))

    def whole(a):
        return pl.BlockSpec(a.shape, lambda i: (0, 0))

    return pl.pallas_call(
        body, name=name, grid=(t // ROW_TILE,),
        out_shape=(jax.ShapeDtypeStruct((t, HEAD_COLS), BF16), jax.ShapeDtypeStruct((t, D_MODEL), F32),
                   jax.ShapeDtypeStruct((8, Q_RANK), F32)),
        in_specs=[pl.BlockSpec((ROW_TILE, dq.shape[1]), lambda i: (jnp.minimum(i, ns - 1), 0)),
                  row(dk.shape[1]), row(dv.shape[1]), row(HEAD_COLS), whole(wq), whole(wk_k), whole(wk_v),
                  whole(win_head), whole(q_gain), whole(kv_gain), row(LANES), row(LANES)],
        out_specs=(row(HEAD_COLS), row(D_MODEL), pl.BlockSpec((8, Q_RANK), lambda i: (0, 0))),
        compiler_params=_params(("arbitrary",), VMEM_BIG),
    )(dq, dk, dv, z, wq, wk_k, wk_v, win_head, q_gain, kv_gain, cos, sgn)


def _shift_rows(u, s):
    rowi = lax.broadcasted_iota(jnp.int32, u.shape, 0)
    prev = jnp.where(rowi == 0, 0.0, pltpu.roll(u, 1, 0))
    nxt = jnp.where(rowi == s - 1, 0.0, pltpu.roll(u, s - 1, 0))
    return prev, nxt


def _conv_fwd(z_conv, cw, a_cat, name):
    s = z_conv.shape[0]

    def body(z_ref, w_ref, a_in_ref, o_ref):
        del a_in_ref
        gb, gc, xv = z_ref[:, 0:LANES], z_ref[:, LANES:2 * LANES], z_ref[:, 2 * LANES:3 * LANES]
        u = gc * xv
        prev, nxt = _shift_rows(u, s)
        y = w_ref[0:1, :] * prev + w_ref[1:2, :] * u + w_ref[2:3, :] * nxt
        o_ref[...] = (gb * y).astype(BF16)

    return pl.pallas_call(
        body, name=name, grid=(CONV_W // LANES,),
        out_shape=jax.ShapeDtypeStruct(a_cat.shape, a_cat.dtype),
        in_specs=[pl.BlockSpec((s, 3 * LANES), lambda j: (0, j)), pl.BlockSpec((3, LANES), lambda j: (0, j)),
                  pl.BlockSpec(memory_space=pl.ANY)],
        out_specs=pl.BlockSpec((s, LANES), lambda j: (0, 4 + j)),
        input_output_aliases={2: 0},
        compiler_params=_params(("parallel",), VMEM_BIG),
    )(z_conv, cw, a_cat)


def _conv_bwd(z_conv, cw, da, name):
    s = z_conv.shape[0]

    def body(z_ref, w_ref, da_ref, dz_ref, dw_ref):
        gb, gc, xv = z_ref[:, 0:LANES], z_ref[:, LANES:2 * LANES], z_ref[:, 2 * LANES:3 * LANES]
        u = gc * xv
        prev, nxt = _shift_rows(u, s)
        dcv = da_ref[...]
        dz_ref[:, 0:LANES] = (dcv * (w_ref[0:1, :] * prev + w_ref[1:2, :] * u + w_ref[2:3, :] * nxt)).astype(BF16)
        dy = dcv * gb
        dw_ref[0:1, :] = jnp.sum(dy * prev, axis=0, keepdims=True)
        dw_ref[1:2, :] = jnp.sum(dy * u, axis=0, keepdims=True)
        dw_ref[2:3, :] = jnp.sum(dy * nxt, axis=0, keepdims=True)
        dyp, dyn = _shift_rows(dy, s)
        du = w_ref[0:1, :] * dyn + w_ref[1:2, :] * dy + w_ref[2:3, :] * dyp
        dz_ref[:, LANES:2 * LANES] = (du * xv).astype(BF16)
        dz_ref[:, 2 * LANES:3 * LANES] = (du * gc).astype(BF16)

    blk = pl.BlockSpec((s, 3 * LANES), lambda j: (0, j))
    cws = pl.BlockSpec((3, LANES), lambda j: (0, j))
    return pl.pallas_call(
        body, name=name, grid=(CONV_W // LANES,),
        out_shape=(jax.ShapeDtypeStruct(z_conv.shape, BF16), jax.ShapeDtypeStruct((3, CONV_W), F32)),
        in_specs=[blk, cws, pl.BlockSpec((s, LANES), lambda j: (0, 4 + j))], out_specs=(blk, cws),
        compiler_params=_params(("parallel",), VMEM_BIG),
    )(z_conv, cw, da)


ATT_TQ = 512
ATT_Q_STEP = 1024
ATT_TQ_BWD = 512


def _head_mask(shape, hh):
    lane = lax.broadcasted_iota(jnp.int32, shape, 1)
    return (lane >= hh * V_DIM) & (lane < (hh + 1) * V_DIM)


def _attn_fwd(qf, kv, s, riding, name):
    t = kv.shape[0]
    step = min(ATT_Q_STEP, s)
    nq = s // step
    nr = riding.n

    def body(*refs):
        q_ref, k_ref, v_ref = refs[:3]
        o_ref, ob_ref, st_ref = refs[3 + nr:6 + nr]
        p, i = pl.program_id(0), pl.program_id(1)
        state = riding.run((p == 0) & (i == 0), (p == N_HEADS // 2 - 1) & (i == nq - 1),
                           refs[3:3 + nr], refs[6 + nr:6 + 2 * nr], refs[6 + 2 * nr:],
                           middle=(p == N_HEADS // 2 - 2) & (i == nq // 2))
        v = v_ref[...]
        vlane = lax.broadcasted_iota(jnp.int32, v.shape, 1)
        one_lane = [(1 - hh) * V_DIM for hh in range(2)]
        vm = [jnp.where(_head_mask(v.shape, hh), v, jnp.where(vlane == one_lane[hh], 1.0, 0.0).astype(BF16))
              for hh in range(2)]

        def block(r, carry):
            rows = pl.ds(pl.multiple_of(r * ATT_TQ, ATT_TQ), ATT_TQ)
            olane = lax.broadcasted_iota(jnp.int32, (ATT_TQ, LANES), 1)
            acc = jnp.zeros((ATT_TQ, LANES), F32)
            stat = jnp.zeros((ATT_TQ, LANES), F32)
            scores = [lax.dot_general(q_ref[rows, hh * LANES:(hh + 1) * LANES], k_ref[:, hh * LANES:(hh + 1) * LANES],
                                      NT_DIMS, preferred_element_type=F32) for hh in range(2)]
            maxes = [jnp.max(sc, axis=1, keepdims=True) for sc in scores]
            exps = [jnp.exp2((sc - mx) * EXP2_SCALE).astype(BF16) for sc, mx in zip(scores, maxes)]
            for hh in range(2):
                mx = maxes[hh]
                res = jnp.dot(exps[hh], vm[hh], preferred_element_type=F32)
                den = jnp.sum(jnp.where(olane == one_lane[hh], res, 0.0), axis=1, keepdims=True)
                acc = acc + jnp.where(_head_mask(res.shape, hh), res * (1.0 / den), 0.0)
                stat = stat + jnp.where(olane == hh, mx * EXP2_SCALE + jnp.log(den) * LOG2_E, 0.0)
            o_ref[rows, :] = acc
            ob_ref[rows, :] = acc.astype(BF16)
            st_ref[:, rows] = stat.T[0:8, :]
            return carry

        lax.fori_loop(0, step // ATT_TQ, block, 0)
        riding.finish(state)

    o_spec = pl.BlockSpec((step, LANES), lambda p, i: (i, p))
    outs = pl.pallas_call(
        body, name=name, grid=(N_HEADS // 2, nq),
        out_shape=(jax.ShapeDtypeStruct((s, N_HEADS * V_DIM), F32),
                   jax.ShapeDtypeStruct((s, D_MODEL), BF16),
                   jax.ShapeDtypeStruct((N_HEADS // 2 * 8, s), F32), *riding.out_shape),
        in_specs=[pl.BlockSpec((step, 2 * LANES), lambda p, i: (i, p)),
                  pl.BlockSpec((t, 2 * LANES), lambda p, i: (0, p)),
                  pl.BlockSpec((t, LANES), lambda p, i: (0, N_HEADS + p)), *riding.specs],
        out_specs=(o_spec, o_spec, pl.BlockSpec((8, step), lambda p, i: (p, i)), *riding.specs),
        scratch_shapes=riding.scratch,
        compiler_params=_params(("arbitrary", "arbitrary"), VMEM_BIG),
    )(qf, kv, kv, *riding.arrays)
    return outs[0], outs[1], outs[2], list(outs[3:])


def _attn_bwd(qf, kv, o, da, stats, cos, sgn, riding, name):
    s, t = o.shape[0], kv.shape[0]
    ATT_TQ = ATT_TQ_BWD
    nq = s // ATT_TQ
    nr = riding.n

    def body(*refs):
        q_ref, k_ref, v_ref, o_ref, do_ref, st_ref, c_ref, s_ref = refs[:8]
        dq_ref, dk_ref, dv_ref = refs[8 + nr:11 + nr]
        dk_acc, dv_acc = refs[11 + 2 * nr:13 + 2 * nr]
        p, i = pl.program_id(0), pl.program_id(1)
        state = riding.run((p == 0) & (i == 0), (p == N_HEADS // 2 - 1) & (i == nq - 1),
                           refs[8:8 + nr], refs[11 + nr:11 + 2 * nr], refs[13 + 2 * nr:])

        @pl.when(i == 0)
        def _():
            dk_acc[...] = jnp.zeros_like(dk_acc)
            dv_acc[...] = jnp.zeros_like(dv_acc)

        v = v_ref[...]
        do = do_ref[...]
        od = do * o_ref[...]
        ones = jnp.ones((8, LANES), F32)
        for hh in range(2):
            sl = slice(hh * LANES, (hh + 1) * LANES)
            q, k = q_ref[:, sl], k_ref[:, sl]
            mask = _head_mask(do.shape, hh)
            dom = jnp.where(mask, do, 0.0).astype(BF16)
            delta = lax.dot_general(ones, jnp.where(mask, od, 0.0), NT_DIMS, preferred_element_type=F32,
                                    precision=lax.Precision.HIGHEST)[0:1, :]
            st = lax.dot_general(k, q, NT_DIMS, preferred_element_type=F32)
            pt = jnp.exp2(st * EXP2_SCALE - st_ref[hh:hh + 1, :]).astype(BF16)
            dpt = lax.dot_general(v, dom, NT_DIMS, preferred_element_type=F32)
            dst = (pt.astype(F32) * (dpt - delta)).astype(BF16)
            dv_acc[...] += jnp.dot(pt, dom, preferred_element_type=F32)
            dk_acc[:, sl] += jnp.dot(dst, q, preferred_element_type=F32)
            dq = lax.dot_general(dst, k, TN_DIMS, preferred_element_type=F32) * ATTN_SCALE
            dq_ref[:, sl] = _rope(dq, c_ref[...], s_ref[...], True).astype(BF16)

        @pl.when(i == nq - 1)
        def _():
            dk_ref[...] = (dk_acc[...] * ATTN_SCALE).astype(BF16)
            dv_ref[...] = dv_acc[...].astype(BF16)

        riding.finish(state)

    o_spec = pl.BlockSpec((ATT_TQ, LANES), lambda p, i: (i, p))
    tab = pl.BlockSpec((ATT_TQ, LANES), lambda p, i: (i, 0))
    outs = pl.pallas_call(
        body, name=name, grid=(N_HEADS // 2, nq),
        out_shape=(jax.ShapeDtypeStruct((s, N_HEADS * LANES), BF16),
                   jax.ShapeDtypeStruct((t, N_HEADS * LANES), BF16),
                   jax.ShapeDtypeStruct((t, N_HEADS * V_DIM), BF16), *riding.out_shape),
        in_specs=[pl.BlockSpec((ATT_TQ, 2 * LANES), lambda p, i: (i, p)),
                  pl.BlockSpec((t, 2 * LANES), lambda p, i: (0, p)),
                  pl.BlockSpec((t, LANES), lambda p, i: (0, N_HEADS + p)),
                  o_spec, o_spec,
                  pl.BlockSpec((8, ATT_TQ), lambda p, i: (p, i)), tab, tab, *riding.specs],
        out_specs=(pl.BlockSpec((ATT_TQ, 2 * LANES), lambda p, i: (i, p)),
                   pl.BlockSpec((t, 2 * LANES), lambda p, i: (0, p)),
                   pl.BlockSpec((t, LANES), lambda p, i: (0, p)), *riding.specs),
        scratch_shapes=[pltpu.VMEM((t, 2 * LANES), F32), pltpu.VMEM((t, LANES), F32), *riding.scratch],
        compiler_params=_params(("arbitrary", "arbitrary"), VMEM_BIG),
    )(qf, kv, kv, o, da, stats, cos, sgn, *riding.arrays)
    return outs[0], outs[1], outs[2], list(outs[3:])


def _silu(x):
    return x * (1.0 / (1.0 + jnp.exp(-x)))


def _prologue(c_rows, c_ctx, w_mod, b_cols, extra_rows, name):
    d, cols = c_rows.shape[1], w_mod.shape[1]

    def body(c_ref, cctx_ref, wmod_ref, b_ref, x_ref, a_ref, modg_ref, c_all, blk, c_send, c_recv, m_send, m_recv):
        _direct_gather(c_ref, c_all, c_send, c_recv)()
        a_ref[...] = jnp.zeros_like(a_ref)
        for j in range(N_DEV):
            a_ref[j:j + 1, :] = c_all[j, 0:1, :]
        a_ref[N_DEV:N_DEV + 1, :] = cctx_ref[...]
        blk[0:16, :] = jnp.dot(_silu(a_ref[...]), wmod_ref[...], preferred_element_type=F32,
                               precision=lax.Precision.HIGHEST) + b_ref[...]
        blk[16:24, :] = x_ref[...]
        _direct_gather(blk, modg_ref, m_send, m_recv)()

    vmem = pl.BlockSpec(memory_space=pltpu.VMEM)
    return pl.pallas_call(
        body, name=name,
        out_shape=(jax.ShapeDtypeStruct((16, d), F32), jax.ShapeDtypeStruct((N_DEV, 24, cols), F32)),
        in_specs=[vmem] * 5, out_specs=(vmem, vmem),
        scratch_shapes=[pltpu.VMEM((N_DEV, 8, d), F32), pltpu.VMEM((24, cols), F32)]
        + [pltpu.SemaphoreType.DMA((7,)) for _ in range(4)],
        compiler_params=_params(None, VMEM_BIG),
    )(c_rows, c_ctx, w_mod, b_cols, extra_rows)


def _adaln_bwd(a_t, w, d_ex, d_ctx, d_all, name):
    def body(at_ref, w_ref, dex_ref, dctx_ref, dall_ref, gw_ref, dsil_ref, dsum_ref):
        sil_t = _silu(at_ref[...])
        dctx = dctx_ref[...]
        row = dctx[0:1, :]
        for j in range(1, N_DEV):
            row = row + dctx[j:j + 1, :]
        rowi = lax.broadcasted_iota(jnp.int32, dctx.shape, 0)
        ctx_rows = jnp.where(rowi == 0, jnp.broadcast_to(row, dctx.shape), 0.0)
        hi = lax.Precision.HIGHEST
        d_rows = jnp.concatenate([dex_ref[...], ctx_rows], axis=0)
        gw_ref[...] = jnp.dot(sil_t, d_rows, preferred_element_type=F32, precision=hi)
        dsil_ref[...] = lax.dot_general(ctx_rows, w_ref[...], NT_DIMS, preferred_element_type=F32, precision=hi)
        tot = dall_ref[0]
        for j in range(1, N_DEV):
            tot = tot + dall_ref[j]
        dsum_ref[...] = tot

    return pl.pallas_call(
        body, name=name,
        out_shape=(jax.ShapeDtypeStruct(w.shape, F32), jax.ShapeDtypeStruct((8, w.shape[0]), F32),
                   jax.ShapeDtypeStruct(d_all.shape[1:], F32)),
        compiler_params=_params(None, VMEM_BIG),
    )(a_t, w, d_ex, d_ctx, d_all)


def _pack_small(sums1, sums2, fsums, sums1c, psums, d_cw, name):
    d = D_MODEL

    def body(s1_ref, s2_ref, f_ref, s1c_ref, p_ref, cw_ref, o_ref):
        o_ref[...] = jnp.zeros_like(o_ref)
        for col, (ref, r) in enumerate([(s1_ref, 1), (s1_ref, 0), (s2_ref, 2), (s2_ref, 1), (s2_ref, 0), (f_ref, 1)]):
            o_ref[0:1, col * d:(col + 1) * d] = ref[r:r + 1, :]
        o_ref[1:2, 0:d] = s1c_ref[1:2, :]
        o_ref[1:2, d:2 * d] = s1c_ref[0:1, :]
        o_ref[2:3, 0:Q_RANK] = p_ref[0:1, :]
        o_ref[2:3, Q_RANK:Q_RANK + KV_RANK] = p_ref[1:2, 0:KV_RANK]
        o_ref[2:3, Q_RANK + KV_RANK:Q_RANK + KV_RANK + d] = f_ref[0:1, :]
        for r in range(3):
            o_ref[3 + r:4 + r, 0:CONV_W] = cw_ref[r:r + 1, :]
        o_ref[6:7, 0:d] = f_ref[3:4, :]

    return pl.pallas_call(body, name=name, out_shape=jax.ShapeDtypeStruct((8, 6 * d), F32))(
        sums1, sums2, fsums, sums1c, psums, d_cw)


def _adam_math(w, g, m, v):
    nm = ADAM_B1 * m + (1.0 - ADAM_B1) * g
    nv = ADAM_B2 * v + (1.0 - ADAM_B2) * (g * g)
    m_hat = nm / (1.0 - ADAM_B1 ** ADAM_STEP)
    v_hat = nv / (1.0 - ADAM_B2 ** ADAM_STEP)
    return -ADAM_LR * (m_hat / (jnp.sqrt(v_hat) + ADAM_EPS) + ADAM_WD * w), nm, nv


def _small_update(dsum, dsil_all, g_cw, params, name):
    d = D_MODEL
    n = len(params)

    def body(*refs):
        dsum_ref, dsil_ref, gcw_ref = refs[:3]
        wmv = refs[3:3 + 3 * n]
        outs = refs[3 + 3 * n:]
        tot = dsil_ref[0]
        for j in range(1, N_DEV):
            tot = tot + dsil_ref[j]
        cv = wmv[0][...]
        sg = 1.0 / (1.0 + jnp.exp(-cv))
        off = Q_RANK + KV_RANK
        grads = [tot[0:1, :] * (sg * (1.0 + cv * (1.0 - sg))),
                 dsum_ref[0:1, :] + dsum_ref[1:2, :],
                 dsum_ref[2:3, 0:Q_RANK], dsum_ref[2:3, Q_RANK:off], dsum_ref[2:3, off:off + d],
                 gcw_ref[...]]
        for p, g in enumerate(grads):
            w_ref, m_ref, v_ref = wmv[3 * p:3 * p + 3]
            at = 0 if len(w_ref.shape) == 3 else Ellipsis
            res = (g,) + _adam_math(w_ref[at], g, m_ref[at], v_ref[at])
            for q, val in enumerate(res):
                outs[4 * p + q][at] = val

    flat = [a for wmv in params for a in wmv]
    out_shape = tuple(jax.ShapeDtypeStruct(wmv[0].shape, F32) for wmv in params for _ in range(4))
    outs = pl.pallas_call(body, name=name, out_shape=out_shape)(dsum, dsil_all, g_cw, *flat)
    return [outs[4 * p:4 * p + 4] for p in range(n)]


def _adamw(w, g, m, v, name, slots=False):
    _, rows, cols = w.shape
    tr = _pick(rows, (256, 128, 64, 32, 16, 8))

    def body(w_ref, g_ref, m_ref, v_ref, *outs):
        if slots:
            gv = g_ref[0].astype(F32)
            for j in range(1, g.shape[0]):
                gv = gv + g_ref[j].astype(F32)
            outs[0][...] = gv
        else:
            gv = g_ref[...]
        d_ref, nm_ref, nv_ref = outs[-3:]
        d_ref[...], nm_ref[...], nv_ref[...] = _adam_math(w_ref[...], gv, m_ref[...], v_ref[...])

    blk = pl.BlockSpec((None, tr, cols), lambda i: (0, i, 0))
    g_spec = (pl.BlockSpec((g.shape[0], tr, cols), lambda i: (0, i, 0)) if slots
              else pl.BlockSpec((tr, cols), lambda i: (i, 0)))
    sh = jax.ShapeDtypeStruct((1, rows, cols), F32)
    n_out = 4 if slots else 3
    return pl.pallas_call(
        body, name=name, grid=(rows // tr,), out_shape=(sh,) * n_out,
        in_specs=[blk, g_spec, blk, blk], out_specs=(blk,) * n_out,
        compiler_params=_params(("parallel",), VMEM_BIG),
    )(w, g, m, v)


def _rope_tables(s, l):
    tok = np.arange(s)
    row = (tok // GRID_W).astype(np.float32)
    col = (tok % GRID_W).astype(np.float32)
    half = QK_ROPE // 2
    freqs = np.float32(ROPE_THETA) ** (-np.arange(0, half, 2, dtype=np.float32) / np.float32(half))
    dd = np.arange(QK_ROPE)
    pos = np.where((dd // half)[None, :] == 0, row[:, None], col[:, None]).astype(np.float32)
    ang = (pos * freqs[dd % (half // 2)][None, :]).astype(np.float32)
    sin = np.sin(ang).astype(np.float32)
    cos_t = np.ones((s + l, LANES), np.float32)
    sgn_t = np.zeros((s + l, LANES), np.float32)
    cos_t[:s, QK_NOPE:QK_NOPE + QK_ROPE] = np.cos(ang)
    sgn_t[:s, QK_NOPE:QK_NOPE + QK_ROPE] = np.where(((dd % half) // (half // 2))[None, :] == 0, -sin, sin)
    return jnp.asarray(cos_t), jnp.asarray(sgn_t)


def _slots_to_cols(g):
    return g.transpose(1, 0, 2).reshape(g.shape[1], N_DEV * g.shape[2])


def _cols_to_slots(w):
    return w.reshape(w.shape[0], N_DEV, w.shape[1] // N_DEV).transpose(1, 0, 2)


def _unpack_small_weights(g_in_t, g_uq, g_ukv):
    w_t = g_in_t.reshape(N_DEV * g_in_t.shape[1], D_MODEL)
    zeros = jnp.zeros((QK_NOPE, D_MODEL), BF16)
    win_head_t = jnp.concatenate([w_t[:Q_RANK + KV_RANK], zeros, w_t[Q_RANK + KV_RANK:MLA_IN],
                                  zeros[:LANES - QK_NOPE - QK_ROPE]], axis=0)
    win_conv_t = w_t[MLA_IN:].reshape(3, CONV_W // LANES, LANES, D_MODEL).transpose(1, 0, 2, 3)
    win_conv_t = win_conv_t.reshape(3 * CONV_W, D_MODEL)
    w_uq = _slots_to_cols(g_uq).reshape(Q_RANK, N_HEADS, QK_NOPE + QK_ROPE)
    wq = jnp.pad(w_uq, ((0, 0), (0, 0), (0, LANES - QK_NOPE - QK_ROPE))).reshape(Q_RANK, N_HEADS * LANES)
    w_ukv = _slots_to_cols(g_ukv).reshape(KV_RANK, N_HEADS, QK_NOPE + V_DIM)
    k_top = jnp.pad(w_ukv[:, :, :QK_NOPE], ((0, 0), (0, 0), (0, LANES - QK_NOPE))).reshape(KV_RANK, N_HEADS * LANES)
    v_top = w_ukv[:, :, QK_NOPE:].reshape(KV_RANK, N_HEADS * V_DIM)
    eye = jnp.pad(jnp.eye(QK_ROPE, dtype=BF16), ((QK_NOPE, LANES - QK_NOPE - QK_ROPE),) * 2)
    wk = jnp.concatenate([
        jnp.concatenate([k_top, v_top], axis=1),
        jnp.concatenate([jnp.tile(eye, (1, N_HEADS)), jnp.zeros((LANES, N_HEADS * V_DIM), BF16)], axis=1)], axis=0)
    return win_head_t, win_conv_t, wq, wk


def _pack_small_grads(d_head_t, d_conv_t, d_wq, d_wkk, d_wkv):
    d_conv_t = d_conv_t.reshape(CONV_W // LANES, 3, LANES, D_MODEL).transpose(1, 0, 2, 3).reshape(3 * CONV_W, D_MODEL)
    rope0 = Q_RANK + KV_RANK + QK_NOPE
    g_in_t = jnp.concatenate([d_head_t[:Q_RANK + KV_RANK], d_head_t[rope0:rope0 + QK_ROPE], d_conv_t], axis=0)
    g_in_t = g_in_t.reshape(N_DEV, -1, D_MODEL).astype(BF16)
    g_uq = d_wq.reshape(Q_RANK, N_HEADS, LANES)[:, :, :QK_NOPE + QK_ROPE].reshape(Q_RANK, -1)
    g_kn = d_wkk[:KV_RANK].reshape(KV_RANK, N_HEADS, LANES)[:, :, :QK_NOPE]
    g_v = d_wkv[:KV_RANK].reshape(KV_RANK, N_HEADS, V_DIM)
    g_ukv = jnp.concatenate([g_kn, g_v], axis=2).reshape(KV_RANK, -1)
    return [g_in_t] + [_cols_to_slots(g).astype(BF16) for g in (g_uq, g_ukv)]


def kernel(x, c, ctx, c_ctx, w_mod, b_mod, w_in, q_norm_g, w_uq, kv_norm_g, w_ukv, conv_w, w_out, w_mlp1, w_mlp2, final_norm_g, loss_target, m_c_ctx, m_w_mod, m_b_mod, m_w_in, m_q_norm_g, m_w_uq, m_kv_norm_g, m_w_ukv, m_conv_w, m_w_out, m_w_mlp1, m_w_mlp2, m_final_norm_g, v_c_ctx, v_w_mod, v_b_mod, v_w_in, v_q_norm_g, v_w_uq, v_kv_norm_g, v_w_ukv, v_conv_w, v_w_out, v_w_mlp1, v_w_mlp2, v_final_norm_g):
    me = _my_index()
    x2d, ctx2d, tgt = x[0], ctx[0], loss_target[0]
    s, l = x2d.shape[0], ctx2d.shape[0]
    t = s + l
    d = D_MODEL
    mod_cols = w_mod.shape[2]
    cw_cols = conv_w.shape[2]

    b_cols = lax.dynamic_slice(b_mod, (0, me * mod_cols), (1, mod_cols))
    cw_blk = jnp.pad(conv_w[0], ((0, 5), (0, mod_cols - cw_cols)))
    a_rows, gathered = _prologue(jnp.pad(c, ((0, 7), (0, 0))), c_ctx[None, :], w_mod[0], b_cols, cw_blk,
                                 "prologue")
    mod_mine = lax.dynamic_index_in_dim(gathered, me, axis=1, keepdims=False).reshape(1, 6 * d)
    mod_ctx = gathered[:, 8, :].reshape(1, 6 * d)
    cw_full = gathered[:, 16:19, :cw_cols].transpose(1, 0, 2).reshape(3, CONV_W)

    early = [w.astype(BF16) for w in (w_in[0].T, w_uq[0], w_ukv[0])]
    late = [w.astype(BF16) for w in (w_out[0], w_mlp1[0], w_mlp2[0])]
    h_all, (g_in, g_uq, g_ukv) = _modulate_all(x2d, ctx2d, mod_mine, mod_ctx, _RidingGather(early),
                                               "modulate1")
    win_head, win_conv, wq, wk = _unpack_small_weights(g_in, g_uq, g_ukv)
    wk_k, wk_v = wk[:, :N_HEADS * LANES], wk[:, N_HEADS * LANES:]
    cos, sgn = _rope_tables(s, l)

    tm_t = _pick(t, (1088, 768, 256))
    tk_t = _pick(t, (2176, 768, 256))
    z_head, cq, kv_in, qf, kv = _head_fwd(h_all, win_head, wq, wk, q_norm_g, kv_norm_g, cos, sgn, tm_t, "head_fwd")
    z_conv = _matmul(h_all, win_conv, mode="nt", name="in_proj_conv", m=s, tm=1024, tn=1536, tk=1024)
    attn, a_cat, stats, (g_out, w1, g_w2) = _attn_fwd(qf, kv, s, _RidingGather(late), "attn_fwd")
    wo = g_out.reshape(d, d)
    w2 = g_w2.reshape(D_FF, d)
    a_cat = _conv_fwd(z_conv, cw_full, a_cat, "conv_fwd")
    (o, x1, h2), _ = _matmul_rows(a_cat, wo, _epi_resid_modulate, mode="nn", name="out_proj", tm=1024, tk=1024,
                                  rows=[x2d], vecs=[(mod_mine, 2), (mod_mine, 3), (mod_mine, 4)],
                                  out_dtypes=[F32, F32, BF16])
    u1, act = _matmul(h2, w1, mode="nn", name="mlp_up", tm=4096, tk=1024, epilogue="relu2", slots="b_cols")
    (dx2, dm, fsums), _ = _matmul_rows(act, w2, _epi_final, mode="nn", name="mlp_down", tm=512, tk=4096,
                                       rows=[x1, tgt], vecs=[(mod_mine, 5), (final_norm_g[None, :], 0)],
                                       out_dtypes=[F32, BF16], sums=True)

    d_w2 = _matmul(act, dm, mode="tn", name="d_w_mlp2", out_dtype=BF16, tm=1024, tn=1024, tk=4096)
    du1 = _matmul(dm, w2, mode="nt", name="d_act", out_dtype=BF16, tm=2048, tn=1024, tk=1024,
                  epilogue="drelu2", extra=(u1,))
    d_w1 = _matmul(h2, du1, mode="tn", name="d_w_mlp1", out_dtype=BF16, tm=1024, tk=4096, slots="out")
    (dx1, do, sums2), _ = _matmul_rows(du1, w1, _epi_modulate2_bwd, mode="nt", name="d_h2", tm=512, tk=4096,
                                       slots="b_contract", rows=[x1, dx2, o], vecs=[(mod_mine, 4), (mod_mine, 2)],
                                       out_dtypes=[F32, BF16], sums=True)
    d_wo = _matmul(a_cat, do, mode="tn", name="d_w_out", out_dtype=BF16, tm=1024, tn=1024, tk=2048)
    da = _matmul(do, wo, mode="nt", name="d_a", tm=1024, tn=1024, tk=1024)
    dz_conv, d_cw = _conv_bwd(z_conv, cw_full, da, "conv_bwd")
    ready = [d_wo.reshape(N_DEV, d // N_DEV, d), d_w1, d_w2.reshape(N_DEV, D_FF // N_DEV, d)]
    dq, dk, dv, rode = _attn_bwd(qf, kv, attn, da, stats, cos, sgn, _Riding(ready), "attn_bwd")
    d_wq = _matmul(cq, dq, mode="tn", name="d_w_uq", k=s, tm=256, tn=1024, tk=4096)
    d_wkk = _matmul(kv_in, dk, mode="tn", name="d_w_ukv_k", tm=256, tn=1024, tk=tk_t)
    d_wkv = _matmul(kv_in, dv, mode="tn", name="d_w_ukv_v", tm=256, tn=512, tk=tk_t)
    dz_head, dh_head, psums = _head_bwd(dq, dk, dv, z_head, wq, wk_k, wk_v, win_head, q_norm_g, kv_norm_g, cos, sgn, s,
                                        "head_bwd")
    d_head = _matmul(dz_head, h_all, mode="tn", name="d_w_in_head", tm=512, tn=1024, tk=tk_t)
    d_conv = _matmul(dz_conv, h_all, mode="tn", name="d_w_in_conv", k=s, tm=1536, tn=1024, tk=2048)
    send = _pack_small_grads(d_head, d_conv, d_wq, d_wkk, d_wkv)
    (grad_x, sums1), got = _matmul_rows(dz_conv, win_conv, _epi_modulate1_bwd, mode="nn", name="d_h1", tm=max(s // 8, ROW_TILE),
                                        tk=win_conv.shape[0], rows=[dh_head, x2d, dx1], vecs=[(mod_mine, 1)],
                                        out_dtypes=[F32], sums=True, riding=_RidingReduce(send))
    sums1c = _modulate_sums(dh_head, s // ROW_TILE, ctx2d)

    small = _pack_small(sums1, sums2, fsums, sums1c, psums, d_cw, "pack_small")
    (d_all,) = _all_gather([small], "gather_small_grads", True)
    d_cols = lax.dynamic_slice_in_dim(d_all, me * mod_cols, mod_cols, axis=2)
    g_w_mod, dsil, dsum = _adaln_bwd(a_rows.T, w_mod[0], d_cols[:, 0, :], d_cols[:, 1, :], d_all, "adaln_bwd")
    (dsil_all,) = _all_gather([dsil], "gather_d_cctx", True)
    loss = dsum[6, 0]
    g_cw = lax.dynamic_slice(dsum, (3, me * cw_cols), (3, cw_cols))

    slots = dict(zip(["w_in", "w_uq", "w_ukv"], got))
    slots.update(zip(["w_out", "w_mlp1", "w_mlp2"], rode))

    grads = {}
    weights = {"c_ctx": c_ctx, "w_mod": w_mod, "b_mod": b_mod, "w_in": w_in, "q_norm_g": q_norm_g, "w_uq": w_uq,
               "kv_norm_g": kv_norm_g, "w_ukv": w_ukv, "conv_w": conv_w, "w_out": w_out, "w_mlp1": w_mlp1,
               "w_mlp2": w_mlp2, "final_norm_g": final_norm_g}
    m_in = {"c_ctx": m_c_ctx, "w_mod": m_w_mod, "b_mod": m_b_mod, "w_in": m_w_in, "q_norm_g": m_q_norm_g,
            "w_uq": m_w_uq, "kv_norm_g": m_kv_norm_g, "w_ukv": m_w_ukv, "conv_w": m_conv_w, "w_out": m_w_out,
            "w_mlp1": m_w_mlp1, "w_mlp2": m_w_mlp2, "final_norm_g": m_final_norm_g}
    v_in = {"c_ctx": v_c_ctx, "w_mod": v_w_mod, "b_mod": v_b_mod, "w_in": v_w_in, "q_norm_g": v_q_norm_g,
            "w_uq": v_w_uq, "kv_norm_g": v_kv_norm_g, "w_ukv": v_w_ukv, "conv_w": v_conv_w, "w_out": v_w_out,
            "w_mlp1": v_w_mlp1, "w_mlp2": v_w_mlp2, "final_norm_g": v_final_norm_g}
    names = list(weights)
    small_names = ["c_ctx", "b_mod", "q_norm_g", "kv_norm_g", "final_norm_g", "conv_w"]
    delta, new_m, new_v = {}, {}, {}

    def as_rows(a):
        return a[None, :] if a.ndim == 1 else a

    small_out = _small_update(dsum, dsil_all, g_cw, [[as_rows(src[n]) for src in (weights, m_in, v_in)]
                                                      for n in small_names], "small_update")
    for n, outs in zip(small_names, small_out):
        grads[n], delta[n], new_m[n], new_v[n] = [a.reshape(weights[n].shape) for a in outs]
    for n in names:
        if n in small_names:
            continue
        if n == "w_in":
            wmv = [jnp.swapaxes(src[n], 1, 2) for src in (weights, m_in, v_in)]
            outs = _adamw(wmv[0], slots[n], wmv[1], wmv[2], "adamw_" + n, slots=True)
            grads[n], delta[n], new_m[n], new_v[n] = [jnp.swapaxes(a, 1, 2) for a in outs]
        elif n in slots:
            grads[n], delta[n], new_m[n], new_v[n] = _adamw(weights[n], slots[n], m_in[n], v_in[n], "adamw_" + n,
                                                            slots=True)
        else:
            delta[n], new_m[n], new_v[n] = _adamw(weights[n], g_w_mod, m_in[n], v_in[n], "adamw_" + n)
            grads[n] = g_w_mod[None]

    return (loss, grad_x[None], *[grads[n] for n in names], *[delta[n] for n in names],
            *[new_m[n] for n in names], *[new_v[n] for n in names])
```

```python
import math

import jax
import jax.numpy as jnp
import numpy as np
from jax import lax
from jax.experimental import pallas as pl
from jax.experimental.pallas import tpu as pltpu

F32 = jnp.float32
BF16 = jnp.bfloat16

D_MODEL = 1024
GRID_W = 64
N_HEADS = 8
QK_NOPE = 64
QK_ROPE = 32
V_DIM = 64
Q_RANK = 256
KV_RANK = 128
MLA_IN = Q_RANK + KV_RANK + QK_ROPE
CONV_W = 512
HEAD_COLS = 512
D_FF = 4096
ROPE_THETA = 10000.0
EPS = 1e-6
ATTN_SCALE = 1.0 / math.sqrt(QK_NOPE + QK_ROPE)
LOG2_E = 1.0 / math.log(2.0)
EXP2_SCALE = ATTN_SCALE * LOG2_E
N_DEV = 8
LANES = 128

ADAM_LR, ADAM_B1, ADAM_B2, ADAM_EPS, ADAM_WD, ADAM_STEP = 0.001, 0.9, 0.999, 1e-08, 0.01, 10

ROW_TILE = 256
VMEM_BIG = 60 * 1024 * 1024


def _params(sem=None, vmem=None):
    return pltpu.CompilerParams(dimension_semantics=sem, vmem_limit_bytes=vmem)


def _pick(n, prefs):
    for p in prefs:
        if n % p == 0:
            return p
    return n


def _my_index():
    return 4 * lax.axis_index("x") + 2 * lax.axis_index("y") + lax.axis_index("c")


def _two_level_gather(x_refs, out_refs, send_sems, recv_sems, local_sems):
    n = len(x_refs)
    x, y, c = lax.axis_index("x"), lax.axis_index("y"), lax.axis_index("c")
    me, sibling = (x, y, c), (x, y, 1 - c)
    chips = [(1 - x, y), (x, 1 - y), (1 - x, 1 - y)]

    def slot(a, px, py, pc):
        return out_refs[a].at[4 * px + 2 * py + pc]

    def copy(a, k, block, to, src=None):
        return pltpu.make_async_remote_copy(
            src_ref=slot(a, *block) if src is None else src, dst_ref=slot(a, *block),
            send_sem=send_sems.at[7 * a + k], recv_sem=recv_sems.at[7 * a + k],
            device_id=to, device_id_type=pl.DeviceIdType.MESH)

    mine = [pltpu.make_async_copy(x_refs[a], slot(a, *me), local_sems.at[a]) for a in range(n)]
    first = [cp for a in range(n) for cp in
             [copy(a, 0, me, sibling, src=x_refs[a])]
             + [copy(a, 1 + j, me, (*chip, c), src=x_refs[a]) for j, chip in enumerate(chips)]]
    passed = [[copy(a, 4 + j, (*chip, c), sibling) for j, chip in enumerate(chips)] for a in range(n)]

    def start():
        for cp in mine + first:
            cp.start()

    def forward():
        for a in range(n):
            for j, chip in enumerate(chips):
                copy(a, 1 + j, (*chip, c), me).wait_recv()
                passed[a][j].start()

    def finish():
        for a in range(n):
            copy(a, 0, sibling, me).wait_recv()
            for j, chip in enumerate(chips):
                copy(a, 4 + j, (*chip, 1 - c), me).wait_recv()
        for cp in first + [cp for per_array in passed for cp in per_array]:
            cp.wait_send()
        for cp in mine:
            cp.wait()

    return start, forward, finish


def _direct_gather(src_ref, dst_ref, send_sems, recv_sems, per_peer=False):
    x, y, c = lax.axis_index("x"), lax.axis_index("y"), lax.axis_index("c")
    me = 4 * x + 2 * y + c
    dst_ref[me] = src_ref[me] if per_peer else src_ref[...]
    sends, landings = [], []
    for k in range(1, N_DEV):
        peer = (1 - x if k & 4 else x, 1 - y if k & 2 else y, 1 - c if k & 1 else c)
        pid = 4 * peer[0] + 2 * peer[1] + peer[2]
        for dst, out in ((me, sends), (pid, landings)):
            out.append(pltpu.make_async_remote_copy(
                src_ref=src_ref.at[pid] if per_peer else src_ref, dst_ref=dst_ref.at[dst],
                send_sem=send_sems.at[k - 1], recv_sem=recv_sems.at[k - 1],
                device_id=peer, device_id_type=pl.DeviceIdType.MESH))
    for cp in sends:
        cp.start()

    def finish():
        for cp in landings:
            cp.wait_recv()
        for cp in sends:
            cp.wait_send()

    return finish


def _all_gather(arrays, name, in_vmem):
    space = pltpu.VMEM if in_vmem else pl.ANY
    n = len(arrays)

    def body(*refs):
        for phase in _two_level_gather(refs[:n], refs[n:2 * n], *refs[2 * n:]):
            phase()

    outs = pl.pallas_call(
        body, name=name,
        out_shape=tuple(jax.ShapeDtypeStruct((N_DEV,) + a.shape, a.dtype) for a in arrays),
        in_specs=[pl.BlockSpec(memory_space=space)] * n,
        out_specs=tuple(pl.BlockSpec(memory_space=space) for _ in arrays),
        scratch_shapes=[pltpu.SemaphoreType.DMA((7 * n,)), pltpu.SemaphoreType.DMA((7 * n,)),
                        pltpu.SemaphoreType.DMA((n,))],
    )(*arrays)
    return list(outs)


class _Riding:
    def __init__(self, arrays=()):
        self.arrays, self.n = list(arrays), len(arrays)
        self.out_shape = [jax.ShapeDtypeStruct(a.shape, a.dtype) for a in self.arrays]
        self.specs = [pl.BlockSpec(memory_space=pl.ANY)] * self.n
        self.scratch = [pltpu.SemaphoreType.DMA((7 * self.n,)), pltpu.SemaphoreType.DMA((7 * self.n,)),
                        pltpu.SemaphoreType.DMA((self.n,))]

    def copies(self, x_refs, y_refs, send_sems, recv_sems, local_sems):
        x, y, c = lax.axis_index("x"), lax.axis_index("y"), lax.axis_index("c")
        me = 4 * x + 2 * y + c
        local, sends, landings = [], [], []
        for a in range(self.n):
            local.append(pltpu.make_async_copy(x_refs[a].at[me], y_refs[a].at[me], local_sems.at[a]))
            for k in range(1, N_DEV):
                peer = (1 - x if k & 4 else x, 1 - y if k & 2 else y, 1 - c if k & 1 else c)
                pid = 4 * peer[0] + 2 * peer[1] + peer[2]
                for dst, out in ((me, sends), (pid, landings)):
                    out.append(pltpu.make_async_remote_copy(
                        src_ref=x_refs[a].at[pid], dst_ref=y_refs[a].at[dst],
                        send_sem=send_sems.at[7 * a + k - 1], recv_sem=recv_sems.at[7 * a + k - 1],
                        device_id=peer, device_id_type=pl.DeviceIdType.MESH))
        return local, sends, landings

    def run(self, first, last, x_refs, y_refs, sems, middle=None):
        if self.n == 0:
            return None
        local, sends, landings = self.copies(x_refs, y_refs, *sems)

        @pl.when(first)
        def _():
            for cp in local + sends:
                cp.start()

        return local, sends, landings, last

    @staticmethod
    def finish(state):
        if state is None:
            return
        local, sends, landings, last = state

        @pl.when(last)
        def _():
            for cp in landings:
                cp.wait_recv()
            for cp in sends:
                cp.wait_send()
            for cp in local:
                cp.wait()


class _RidingGather:
    def __init__(self, arrays):
        self.arrays, self.n = list(arrays), len(arrays)
        self.out_shape = [jax.ShapeDtypeStruct((N_DEV,) + a.shape, a.dtype) for a in self.arrays]
        self.specs = [pl.BlockSpec(memory_space=pl.ANY)] * self.n
        self.scratch = [pltpu.SemaphoreType.DMA((7 * self.n,)), pltpu.SemaphoreType.DMA((7 * self.n,)),
                        pltpu.SemaphoreType.DMA((self.n,))]

    def run(self, first, last, x_refs, y_refs, sems, middle):
        start, forward, finish = _two_level_gather(x_refs, y_refs, *sems)
        pl.when(first)(start)
        pl.when(middle)(forward)
        return finish, last

    @staticmethod
    def finish(state):
        finish, last = state
        pl.when(last)(finish)


class _RidingReduce:
    def __init__(self, arrays):
        self.arrays, self.n = list(arrays), len(arrays)
        self.out_shape = [jax.ShapeDtypeStruct((4,) + a.shape[1:], a.dtype) for a in self.arrays]
        self.specs = [pl.BlockSpec(memory_space=pl.ANY)] * self.n
        self.scratch = [pltpu.VMEM((4,) + a.shape[1:], a.dtype) for a in self.arrays for _ in range(3)]
        self.scratch += [pltpu.SemaphoreType.DMA((self.n,)) for _ in range(6)]

    def run(self, first, last, x_refs, y_refs, scratch, middle):
        n = self.n
        own, sib, tot = scratch[0:3 * n:3], scratch[1:3 * n:3], scratch[2:3 * n:3]
        d2d_send, d2d_recv, local_in, ici_send, ici_recv, local_out = scratch[3 * n:]
        x, y, c = lax.axis_index("x"), lax.axis_index("y"), lax.axis_index("c")
        my_chip = 2 * x + y
        sibling = (x, y, 1 - c)
        others = [(1 - x, y), (x, 1 - y), (1 - x, 1 - y)]

        def to_sibling(a, j=None):
            src = x_refs[a].at[pl.ds(0, 4)] if j is None else x_refs[a].at[2 * j + 1 - c]
            dst = sib[a] if j is None else sib[a].at[j]
            return pltpu.make_async_remote_copy(src_ref=src, dst_ref=dst, send_sem=d2d_send.at[a],
                                                recv_sem=d2d_recv.at[a], device_id=sibling,
                                                device_id_type=pl.DeviceIdType.MESH)

        def mine_in(a, j=None):
            src = x_refs[a].at[pl.ds(0, 4)] if j is None else x_refs[a].at[2 * j + c]
            return pltpu.make_async_copy(src, own[a] if j is None else own[a].at[j], local_in.at[a])

        def to_chip(a, chip=None):
            if chip is None:
                src, dst, peer = tot[a].at[pl.ds(0, 3)], y_refs[a].at[pl.ds(0, 3)], sibling
            else:
                src, dst, peer = tot[a].at[2 * chip[0] + chip[1]], y_refs[a].at[my_chip], (*chip, c)
            return pltpu.make_async_remote_copy(src_ref=src, dst_ref=dst, send_sem=ici_send.at[a],
                                                recv_sem=ici_recv.at[a], device_id=peer,
                                                device_id_type=pl.DeviceIdType.MESH)

        def mine_out(a):
            return pltpu.make_async_copy(tot[a].at[my_chip], y_refs[a].at[my_chip], local_out.at[a])

        @pl.when(first)
        def _():
            for a in range(n):
                for j in range(4):
                    to_sibling(a, j).start()
                    mine_in(a, j).start()

        @pl.when(middle)
        def _():
            for a in range(n):
                to_sibling(a).wait_recv()
                to_sibling(a).wait_send()
                mine_in(a).wait()
                tot[a][...] = (own[a][...].astype(F32) + sib[a][...].astype(F32)).astype(tot[a].dtype)
                for chip in others:
                    to_chip(a, chip).start()
                mine_out(a).start()

        def finish():
            @pl.when(last)
            def _():
                for a in range(n):
                    to_chip(a).wait_recv()
                    to_chip(a).wait_send()
                    mine_out(a).wait()

        return finish

    @staticmethod
    def finish(state):
        state()


_DIMS ={"nn": (((1,), (0,)), ((), ())), "nt": (((1,), (1,)), ((), ())), "tn": (((0,), (0,)), ((), ()))}
NT_DIMS = _DIMS["nt"]
TN_DIMS = _DIMS["tn"]


def _swap8(x):
    lane = lax.broadcasted_iota(jnp.int32, x.shape, 1)
    return jnp.where((lane & 15) < 8, pltpu.roll(x, LANES - 8, 1), pltpu.roll(x, 8, 1))


def _rope(x, cos, sgn, bwd):
    return x * cos + (_swap8(x * sgn) if bwd else _swap8(x) * sgn)


def _matmul(a, b, *, mode, name, out_dtype=F32, tm=512, tn=512, tk=512, m=None, k=None,
            epilogue=None, extra=(), slots=None):
    if mode == "nn":
        m = a.shape[0] if m is None else m
        k = a.shape[1]
        n = N_DEV * b.shape[2] if slots == "b_cols" else b.shape[1]
    elif mode == "nt":
        m = a.shape[0] if m is None else m
        k = a.shape[1]
        n = b.shape[0]
    else:
        k = a.shape[0] if k is None else k
        m, n = a.shape[1], b.shape[1]
    tm, tn, tk = min(tm, m), min(tn, n), min(tk, k)
    if slots == "b_cols":
        tn = b.shape[2]
    if slots == "out":
        tn = n // N_DEV
    assert m % tm == 0 and n % tn == 0 and k % tk == 0, (name, m, n, k, tm, tn, tk)
    nk = k // tk
    dims = _DIMS[mode]
    a_spec = (pl.BlockSpec((tk, tm), lambda i, j, kk: (kk, i)) if mode == "tn"
              else pl.BlockSpec((tm, tk), lambda i, j, kk: (i, kk)))
    if slots == "b_cols":
        b_spec = pl.BlockSpec((None, tk, tn), lambda i, j, kk: (j, kk, 0))
    elif mode == "nt":
        b_spec = pl.BlockSpec((tn, tk), lambda i, j, kk: (j, kk))
    else:
        b_spec = pl.BlockSpec((tk, tn), lambda i, j, kk: (kk, j))
    tile = pl.BlockSpec((tm, tn), lambda i, j, kk: (i, j))
    if slots == "out":
        o_spec = pl.BlockSpec((None, tm, tn), lambda i, j, kk: (j, i, 0))
        o_shape = (N_DEV, m, tn)
    else:
        o_spec, o_shape = tile, (m, n)
    in_specs, args = [a_spec, b_spec], [a, b]
    if epilogue == "drelu2":
        in_specs.append(tile)
    args += list(extra)
    if epilogue == "relu2":
        out_shape = (jax.ShapeDtypeStruct(o_shape, BF16), jax.ShapeDtypeStruct(o_shape, BF16))
        out_specs = (o_spec, o_spec)
    else:
        out_shape = jax.ShapeDtypeStruct(o_shape, out_dtype)
        out_specs = o_spec
    n_in = len(args)
    n_out = 2 if epilogue == "relu2" else 1

    def body(*refs):
        a_ref, b_ref = refs[0], refs[1]
        outs = refs[n_in:n_in + n_out]
        part = lax.dot_general(a_ref[...], b_ref[...], dims, preferred_element_type=F32)

        def finish(acc):
            if epilogue == "relu2":
                outs[0][...] = acc.astype(BF16)
                r = jnp.maximum(acc, 0.0)
                outs[1][...] = (r * r).astype(BF16)
            elif epilogue == "drelu2":
                u = refs[2][...].astype(F32)
                outs[0][...] = (acc * (2.0 * jnp.maximum(u, 0.0))).astype(out_dtype)
            else:
                outs[0][...] = acc.astype(out_dtype)

        if nk == 1:
            finish(part)
        else:
            acc_ref = refs[n_in + n_out]
            kk = pl.program_id(2)

            @pl.when(kk == 0)
            def _():
                acc_ref[...] = part

            @pl.when(kk > 0)
            def _():
                acc_ref[...] += part

            @pl.when(kk == nk - 1)
            def _():
                finish(acc_ref[...])

    return pl.pallas_call(
        body, name=name, grid=(m // tm, n // tn, nk),
        out_shape=out_shape, in_specs=in_specs, out_specs=out_specs,
        scratch_shapes=[pltpu.VMEM((tm, tn), F32)] if nk > 1 else [],
        compiler_params=_params(("parallel", "parallel", "arbitrary"), VMEM_BIG),
    )(*args)


def _rstd(x):
    return lax.rsqrt(jnp.mean(x * x, axis=1, keepdims=True) + EPS)


def _norm_bwd(dxn, xn, r):
    return r * (dxn - xn * jnp.mean(dxn * xn, axis=1, keepdims=True))


def _vec(col):
    return pl.BlockSpec((1, D_MODEL), lambda i: (0, col))


def _matmul_rows(a, b, epi, *, mode, name, tm, tk, rows=(), vecs=(), out_dtypes=(), sums=False, slots=None,
                 riding=None):
    m, k = a.shape
    n = D_MODEL
    tm, tk = min(tm, m), min(tk, k)
    riding = riding or _Riding()
    group = 1
    if slots == "b_contract":
        group = max(1, tk // b.shape[2])
        tk = group * b.shape[2]
        b_spec = pl.BlockSpec((group, n, tk // group), lambda i, kk: (kk, 0, 0))
    elif mode == "nt":
        b_spec = pl.BlockSpec((n, tk), lambda i, kk: (0, kk))
    else:
        b_spec = pl.BlockSpec((tk, n), lambda i, kk: (kk, 0))
    assert m % tm == 0 and k % tk == 0, (name, m, k, tm, tk)
    ni, nk = m // tm, k // tk
    assert ni >= 2 or not isinstance(riding, _RidingReduce), "the two-level exchange needs a middle grid step"
    dims = _DIMS[mode]
    tile = pl.BlockSpec((tm, n), lambda i, kk: (i, 0))
    in_specs = [pl.BlockSpec((tm, tk), lambda i, kk: (i, kk)), b_spec] + [tile] * len(rows)
    in_specs += [pl.BlockSpec((1, n), lambda i, kk, col=col: (0, col)) for _, col in vecs]
    args = [a, b, *rows, *[v for v, _ in vecs]]
    out_shape = [jax.ShapeDtypeStruct((m, n), dt) for dt in out_dtypes]
    out_specs = [tile] * len(out_dtypes)
    if sums:
        out_shape.append(jax.ShapeDtypeStruct((8, n), F32))
        out_specs.append(pl.BlockSpec((8, n), lambda i, kk: (0, 0)))
    n_rows, n_vecs, n_outs, nr = len(rows), len(vecs), len(out_dtypes), riding.n
    n_in = 2 + n_rows + n_vecs

    def body(*refs):
        a_ref, b_ref = refs[0], refs[1]
        row_refs = refs[2:2 + n_rows]
        vec_refs = refs[2 + n_rows:n_in]
        x_refs = refs[n_in:n_in + nr]
        out_refs = refs[n_in + nr:n_in + nr + n_outs]
        pos = n_in + nr + n_outs
        sums_ref = refs[pos] if sums else None
        pos += 1 if sums else 0
        y_refs = refs[pos:pos + nr]
        pos += nr
        acc_ref = refs[pos] if nk > 1 else None
        sem_refs = refs[pos + (1 if nk > 1 else 0):]
        i, kk = pl.program_id(0), pl.program_id(1)
        state = riding.run((i == 0) & (kk == 0), (i == ni - 1) & (kk == nk - 1), x_refs, y_refs, sem_refs,
                           middle=(i == 1) & (kk == 0))
        if slots == "b_contract":
            c = tk // group
            part = lax.dot_general(a_ref[:, 0:c], b_ref[0], dims, preferred_element_type=F32)
            for u in range(1, group):
                part = part + lax.dot_general(a_ref[:, u * c:(u + 1) * c], b_ref[u], dims, preferred_element_type=F32)
        else:
            part = lax.dot_general(a_ref[...], b_ref[...], dims, preferred_element_type=F32)

        def finish(acc):
            nsub = tm // ROW_TILE
            for r in range(nsub):
                blk = pl.ds(r * ROW_TILE, ROW_TILE)
                epi(acc[r * ROW_TILE:(r + 1) * ROW_TILE], [ref.at[blk] for ref in row_refs], vec_refs,
                    [ref.at[blk] for ref in out_refs], sums_ref,
                    (i == 0) if r == 0 else None, (i == ni - 1) if r == nsub - 1 else None)

        if nk == 1:
            finish(part)
        else:
            @pl.when(kk == 0)
            def _():
                acc_ref[...] = part

            @pl.when(kk > 0)
            def _():
                acc_ref[...] += part

            @pl.when(kk == nk - 1)
            def _():
                finish(acc_ref)

        riding.finish(state)

    outs = pl.pallas_call(
        body, name=name, grid=(ni, nk),
        out_shape=(*out_shape, *riding.out_shape),
        in_specs=[*in_specs, *riding.specs], out_specs=(*out_specs, *riding.specs),
        scratch_shapes=([pltpu.VMEM((tm, n), F32)] if nk > 1 else []) + (riding.scratch if nr else []),
        compiler_params=_params(("arbitrary", "arbitrary"), VMEM_BIG),
    )(*args, *riding.arrays)
    n_own = len(out_shape)
    return list(outs[:n_own]), list(outs[n_own:])


def _zero_sums_at_start(sums_ref, first):
    if first is not None:
        @pl.when(first)
        def _():
            sums_ref[...] = jnp.zeros_like(sums_ref)


def _epi_resid_modulate(acc, rows, vecs, outs, sums_ref, first, last):
    (x_ref,), (g_ref, sh_ref, sc_ref) = rows, vecs
    x1 = x_ref[...] + g_ref[...] * acc
    outs[0][...] = acc
    outs[1][...] = x1
    outs[2][...] = (x1 * _rstd(x1) * (1.0 + sc_ref[...]) + sh_ref[...]).astype(BF16)


def _epi_final(acc, rows, vecs, outs, sums_ref, first, last):
    (x1_ref, t_ref), (g_ref, gf_ref) = rows, vecs
    d = acc.shape[1]
    x2 = x1_ref[...] + g_ref[...] * acc
    r = _rstd(x2)
    xn = x2 * r
    err = xn * gf_ref[...] - t_ref[...]
    dy = err * (1.0 / d)
    dx2 = _norm_bwd(dy * gf_ref[...], xn, r)
    outs[0][...] = dx2
    outs[1][...] = (dx2 * g_ref[...]).astype(BF16)
    _zero_sums_at_start(sums_ref, first)
    sums_ref[0:1, :] += jnp.sum(dy * xn, axis=0, keepdims=True)
    sums_ref[1:2, :] += jnp.sum(dx2 * acc, axis=0, keepdims=True)
    sums_ref[2:3, :] += jnp.sum(err * err, axis=0, keepdims=True)

    if last is not None:
        @pl.when(last)
        def _():
            tot = jnp.sum(sums_ref[2:3, :], axis=1, keepdims=True) * (0.5 / d)
            sums_ref[3:4, :] = jnp.broadcast_to(tot, (1, d))


def _epi_modulate2_bwd(acc, rows, vecs, outs, sums_ref, first, last):
    (x_ref, dres_ref, o_ref), (sc_ref, g_ref) = rows, vecs
    x = x_ref[...]
    r = _rstd(x)
    xn = x * r
    dx = dres_ref[...] + _norm_bwd(acc * (1.0 + sc_ref[...]), xn, r)
    outs[0][...] = dx
    outs[1][...] = (dx * g_ref[...]).astype(BF16)
    _zero_sums_at_start(sums_ref, first)
    sums_ref[0:1, :] += jnp.sum(acc * xn, axis=0, keepdims=True)
    sums_ref[1:2, :] += jnp.sum(acc, axis=0, keepdims=True)
    sums_ref[2:3, :] += jnp.sum(dx * o_ref[...], axis=0, keepdims=True)


def _epi_modulate1_bwd(acc, rows, vecs, outs, sums_ref, first, last):
    (add_ref, x_ref, dres_ref), (sc_ref,) = rows, vecs
    dh = acc + add_ref[...]
    x = x_ref[...]
    r = _rstd(x)
    xn = x * r
    outs[0][...] = dres_ref[...] + _norm_bwd(dh * (1.0 + sc_ref[...]), xn, r)
    _zero_sums_at_start(sums_ref, first)
    sums_ref[0:1, :] += jnp.sum(dh * xn, axis=0, keepdims=True)
    sums_ref[1:2, :] += jnp.sum(dh, axis=0, keepdims=True)


def _modulate_all(x, ctx, mod, mod_ctx, riding, name):
    s, d = x.shape
    t = s + ctx.shape[0]
    ns = s // ROW_TILE
    nc = ctx.shape[0] // ROW_TILE
    nr = riding.n

    def body(*refs):
        x_ref, c_ref, sh_ref, sc_ref, shc_ref, scc_ref = refs[:6]
        h_ref = refs[6 + nr]
        i = pl.program_id(0)
        state = riding.run(i == 0, i == ns + nc - 1, refs[6:6 + nr], refs[7 + nr:7 + 2 * nr], refs[7 + 2 * nr:],
                           middle=i == ns + nc - 3)

        @pl.when(i < ns)
        def _():
            v = x_ref[...]
            h_ref[...] = (v * _rstd(v) * (1.0 + sc_ref[...]) + sh_ref[...]).astype(BF16)

        @pl.when(i >= ns)
        def _():
            v = c_ref[...]
            h_ref[...] = (v * _rstd(v) * (1.0 + scc_ref[...]) + shc_ref[...]).astype(BF16)

        riding.finish(state)

    outs = pl.pallas_call(
        body, name=name, grid=(ns + nc,),
        out_shape=(jax.ShapeDtypeStruct((t, d), BF16), *riding.out_shape),
        in_specs=[pl.BlockSpec((ROW_TILE, d), lambda i: (jnp.minimum(i, ns - 1), 0)),
                  pl.BlockSpec((ROW_TILE, d), lambda i: (jnp.maximum(i - ns, 0), 0)),
                  _vec(0), _vec(1), _vec(0), _vec(1), *riding.specs],
        out_specs=(pl.BlockSpec((ROW_TILE, d), lambda i: (i, 0)), *riding.specs),
        scratch_shapes=riding.scratch,
        compiler_params=_params(("arbitrary",)),
    )(x, ctx, mod, mod, mod_ctx, mod_ctx, *riding.arrays)
    return outs[0], list(outs[1:])


def _modulate_sums(dh, row_off, xsrc):
    s, d = xsrc.shape

    def body(dh_ref, x_ref, sums_ref):
        i = pl.program_id(0)
        x = x_ref[...]
        dhv = dh_ref[...]

        @pl.when(i == 0)
        def _():
            sums_ref[...] = jnp.zeros_like(sums_ref)

        sums_ref[0:1, :] += jnp.sum(dhv * (x * _rstd(x)), axis=0, keepdims=True)
        sums_ref[1:2, :] += jnp.sum(dhv, axis=0, keepdims=True)

    return pl.pallas_call(
        body, name="modulate1_ctx_bwd", grid=(s // ROW_TILE,),
        out_shape=jax.ShapeDtypeStruct((8, d), F32),
        in_specs=[pl.BlockSpec((ROW_TILE, d), lambda i: (i + row_off, 0)), pl.BlockSpec((ROW_TILE, d), lambda i: (i, 0))],
        out_specs=pl.BlockSpec((8, d), lambda i: (0, 0)),
        compiler_params=_params(("arbitrary",)),
    )(dh, xsrc)


def _head_fwd(h_all, win_head, wq, wk, q_gain, kv_gain, cos, sgn, tm, name):
    t, d = h_all.shape
    nq, nkv = wq.shape[1], wk.shape[1]

    def body(h_ref, wi_ref, wq_ref, wk_ref, qg_ref, kg_ref, c_ref, s_ref, z_ref, cq_ref, kvin_ref, qf_ref, kv_ref):
        z = lax.dot_general(h_ref[...], wi_ref[...], NT_DIMS, preferred_element_type=F32)
        z_ref[...] = z
        cos, sgn = c_ref[...], s_ref[...]
        zq = z[:, 0:Q_RANK]
        cq = (zq * _rstd(zq) * qg_ref[...]).astype(BF16)
        cq_ref[...] = cq
        zk = z[:, Q_RANK:Q_RANK + KV_RANK]
        kv_in = jnp.concatenate([(zk * _rstd(zk) * kg_ref[...]).astype(BF16),
                                 _rope(z[:, Q_RANK + KV_RANK:HEAD_COLS], cos, sgn, False).astype(BF16)], axis=1)
        kvin_ref[...] = kv_in
        q = jnp.dot(cq, wq_ref[...], preferred_element_type=F32)
        for h in range(nq // LANES):
            sl = slice(h * LANES, (h + 1) * LANES)
            qf_ref[:, sl] = _rope(q[:, sl], cos, sgn, False).astype(BF16)
        kv_ref[...] = jnp.dot(kv_in, wk_ref[...], preferred_element_type=F32).astype(BF16)

    def row(w):
        return pl.BlockSpec((tm, w), lambda i: (i, 0))

    def whole(a):
        return pl.BlockSpec(a.shape, lambda i: (0, 0))

    return pl.pallas_call(
        body, name=name, grid=(t // tm,),
        out_shape=(jax.ShapeDtypeStruct((t, HEAD_COLS), F32), jax.ShapeDtypeStruct((t, Q_RANK), BF16),
                   jax.ShapeDtypeStruct((t, KV_RANK + LANES), BF16), jax.ShapeDtypeStruct((t, nq), BF16),
                   jax.ShapeDtypeStruct((t, nkv), BF16)),
        in_specs=[row(d), whole(win_head), whole(wq), whole(wk), whole(q_gain), whole(kv_gain), row(LANES), row(LANES)],
        out_specs=(row(HEAD_COLS), row(Q_RANK), row(KV_RANK + LANES), row(nq), row(nkv)),
        compiler_params=_params(("parallel",), VMEM_BIG),
    )(h_all, win_head, wq, wk, q_gain, kv_gain, cos, sgn)


def _head_bwd(dq, dk, dv, z, wq, wk_k, wk_v, win_head, q_gain, kv_gain, cos, sgn, s, name):
    t = z.shape[0]
    ns = s // ROW_TILE

    def body(dq_ref, dk_ref, dv_ref, z_ref, wq_ref, wkk_ref, wkv_ref, wi_ref, qg_ref, kg_ref, c_ref, s_ref,
             dz_ref, dh_ref, sums_ref):
        i = pl.program_id(0)

        @pl.when(i == 0)
        def _():
            sums_ref[...] = jnp.zeros_like(sums_ref)

        @pl.when(i < ns)
        def _():
            dc = lax.dot_general(dq_ref[...], wq_ref[...], NT_DIMS, preferred_element_type=F32)
            zq = z_ref[:, 0:Q_RANK]
            r = _rstd(zq)
            zn = zq * r
            sums_ref[0:1, :] += jnp.sum(dc * zn, axis=0, keepdims=True)
            dz_ref[:, 0:Q_RANK] = _norm_bwd(dc * qg_ref[...], zn, r).astype(BF16)

        @pl.when(i >= ns)
        def _():
            dz_ref[:, 0:Q_RANK] = jnp.zeros((ROW_TILE, Q_RANK), BF16)

        dkv = (lax.dot_general(dk_ref[...], wkk_ref[...], NT_DIMS, preferred_element_type=F32)
               + lax.dot_general(dv_ref[...], wkv_ref[...], NT_DIMS, preferred_element_type=F32))
        zk = z_ref[:, Q_RANK:Q_RANK + KV_RANK]
        r = _rstd(zk)
        zn = zk * r
        dc = dkv[:, 0:KV_RANK]
        sums_ref[1:2, 0:KV_RANK] += jnp.sum(dc * zn, axis=0, keepdims=True)
        dz_ref[:, Q_RANK:Q_RANK + KV_RANK] = _norm_bwd(dc * kg_ref[...], zn, r).astype(BF16)
        dz_ref[:, Q_RANK + KV_RANK:HEAD_COLS] = _rope(dkv[:, KV_RANK:KV_RANK + LANES], c_ref[...], s_ref[...],
                                                       True).astype(BF16)
        dh_ref[...] = jnp.dot(dz_ref[...], wi_ref[...], preferred_element_type=F32)

    def row(w):
        return pl.BlockSpec((ROW_TILE, w), lambda i: (i, 0))

    def whole(a):
        return pl.BlockSpec(a.shape, lambda i: (0, 0))

    return pl.pallas_call(
        body, name=name, grid=(t // ROW_TILE,),
        out_shape=(jax.ShapeDtypeStruct((t, HEAD_COLS), BF16), jax.ShapeDtypeStruct((t, D_MODEL), F32),
                   jax.ShapeDtypeStruct((8, Q_RANK), F32)),
        in_specs=[pl.BlockSpec((ROW_TILE, dq.shape[1]), lambda i: (jnp.minimum(i, ns - 1), 0)),
                  row(dk.shape[1]), row(dv.shape[1]), row(HEAD_COLS), whole(wq), whole(wk_k), whole(wk_v),
                  whole(win_head), whole(q_gain), whole(kv_gain), row(LANES), row(LANES)],
        out_specs=(row(HEAD_COLS), row(D_MODEL), pl.BlockSpec((8, Q_RANK), lambda i: (0, 0))),
        compiler_params=_params(("arbitrary",), VMEM_BIG),
    )(dq, dk, dv, z, wq, wk_k, wk_v, win_head, q_gain, kv_gain, cos, sgn)


def _shift_rows(u, s):
    rowi = lax.broadcasted_iota(jnp.int32, u.shape, 0)
    prev = jnp.where(rowi == 0, 0.0, pltpu.roll(u, 1, 0))
    nxt = jnp.where(rowi == s - 1, 0.0, pltpu.roll(u, s - 1, 0))
    return prev, nxt


def _conv_fwd(z_conv, cw, a_cat, name):
    s = z_conv.shape[0]

    def body(z_ref, w_ref, a_in_ref, o_ref):
        del a_in_ref
        gb, gc, xv = z_ref[:, 0:LANES], z_ref[:, LANES:2 * LANES], z_ref[:, 2 * LANES:3 * LANES]
        u = gc * xv
        prev, nxt = _shift_rows(u, s)
        y = w_ref[0:1, :] * prev + w_ref[1:2, :] * u + w_ref[2:3, :] * nxt
        o_ref[...] = (gb * y).astype(BF16)

    return pl.pallas_call(
        body, name=name, grid=(CONV_W // LANES,),
        out_shape=jax.ShapeDtypeStruct(a_cat.shape, a_cat.dtype),
        in_specs=[pl.BlockSpec((s, 3 * LANES), lambda j: (0, j)), pl.BlockSpec((3, LANES), lambda j: (0, j)),
                  pl.BlockSpec(memory_space=pl.ANY)],
        out_specs=pl.BlockSpec((s, LANES), lambda j: (0, 4 + j)),
        input_output_aliases={2: 0},
        compiler_params=_params(("parallel",), VMEM_BIG),
    )(z_conv, cw, a_cat)


def _conv_bwd(z_conv, cw, da, name):
    s = z_conv.shape[0]

    def body(z_ref, w_ref, da_ref, dz_ref, dw_ref):
        gb, gc, xv = z_ref[:, 0:LANES], z_ref[:, LANES:2 * LANES], z_ref[:, 2 * LANES:3 * LANES]
        u = gc * xv
        prev, nxt = _shift_rows(u, s)
        dcv = da_ref[...]
        dz_ref[:, 0:LANES] = (dcv * (w_ref[0:1, :] * prev + w_ref[1:2, :] * u + w_ref[2:3, :] * nxt)).astype(BF16)
        dy = dcv * gb
        dw_ref[0:1, :] = jnp.sum(dy * prev, axis=0, keepdims=True)
        dw_ref[1:2, :] = jnp.sum(dy * u, axis=0, keepdims=True)
        dw_ref[2:3, :] = jnp.sum(dy * nxt, axis=0, keepdims=True)
        dyp, dyn = _shift_rows(dy, s)
        du = w_ref[0:1, :] * dyn + w_ref[1:2, :] * dy + w_ref[2:3, :] * dyp
        dz_ref[:, LANES:2 * LANES] = (du * xv).astype(BF16)
        dz_ref[:, 2 * LANES:3 * LANES] = (du * gc).astype(BF16)

    blk = pl.BlockSpec((s, 3 * LANES), lambda j: (0, j))
    cws = pl.BlockSpec((3, LANES), lambda j: (0, j))
    return pl.pallas_call(
        body, name=name, grid=(CONV_W // LANES,),
        out_shape=(jax.ShapeDtypeStruct(z_conv.shape, BF16), jax.ShapeDtypeStruct((3, CONV_W), F32)),
        in_specs=[blk, cws, pl.BlockSpec((s, LANES), lambda j: (0, 4 + j))], out_specs=(blk, cws),
        compiler_params=_params(("parallel",), VMEM_BIG),
    )(z_conv, cw, da)


ATT_TQ = 512
ATT_Q_STEP = 1024
ATT_TQ_BWD = 512


def _head_mask(shape, hh):
    lane = lax.broadcasted_iota(jnp.int32, shape, 1)
    return (lane >= hh * V_DIM) & (lane < (hh + 1) * V_DIM)


def _attn_fwd(qf, kv, s, riding, name):
    t = kv.shape[0]
    step = min(ATT_Q_STEP, s)
    nq = s // step
    nr = riding.n

    def body(*refs):
        q_ref, k_ref, v_ref = refs[:3]
        o_ref, ob_ref, st_ref = refs[3 + nr:6 + nr]
        p, i = pl.program_id(0), pl.program_id(1)
        state = riding.run((p == 0) & (i == 0), (p == N_HEADS // 2 - 1) & (i == nq - 1),
                           refs[3:3 + nr], refs[6 + nr:6 + 2 * nr], refs[6 + 2 * nr:],
                           middle=(p == N_HEADS // 2 - 2) & (i == nq // 2))
        v = v_ref[...]
        vlane = lax.broadcasted_iota(jnp.int32, v.shape, 1)
        one_lane = [(1 - hh) * V_DIM for hh in range(2)]
        vm = [jnp.where(_head_mask(v.shape, hh), v, jnp.where(vlane == one_lane[hh], 1.0, 0.0).astype(BF16))
              for hh in range(2)]

        def block(r, carry):
            rows = pl.ds(pl.multiple_of(r * ATT_TQ, ATT_TQ), ATT_TQ)
            olane = lax.broadcasted_iota(jnp.int32, (ATT_TQ, LANES), 1)
            acc = jnp.zeros((ATT_TQ, LANES), F32)
            stat = jnp.zeros((ATT_TQ, LANES), F32)
            scores = [lax.dot_general(q_ref[rows, hh * LANES:(hh + 1) * LANES], k_ref[:, hh * LANES:(hh + 1) * LANES],
                                      NT_DIMS, preferred_element_type=F32) for hh in range(2)]
            maxes = [jnp.max(sc, axis=1, keepdims=True) for sc in scores]
            exps = [jnp.exp2((sc - mx) * EXP2_SCALE).astype(BF16) for sc, mx in zip(scores, maxes)]
            for hh in range(2):
                mx = maxes[hh]
                res = jnp.dot(exps[hh], vm[hh], preferred_element_type=F32)
                den = jnp.sum(jnp.where(olane == one_lane[hh], res, 0.0), axis=1, keepdims=True)
                acc = acc + jnp.where(_head_mask(res.shape, hh), res * (1.0 / den), 0.0)
                stat = stat + jnp.where(olane == hh, mx * EXP2_SCALE + jnp.log(den) * LOG2_E, 0.0)
            o_ref[rows, :] = acc
            ob_ref[rows, :] = acc.astype(BF16)
            st_ref[:, rows] = stat.T[0:8, :]
            return carry

        lax.fori_loop(0, step // ATT_TQ, block, 0)
        riding.finish(state)

    o_spec = pl.BlockSpec((step, LANES), lambda p, i: (i, p))
    outs = pl.pallas_call(
        body, name=name, grid=(N_HEADS // 2, nq),
        out_shape=(jax.ShapeDtypeStruct((s, N_HEADS * V_DIM), F32),
                   jax.ShapeDtypeStruct((s, D_MODEL), BF16),
                   jax.ShapeDtypeStruct((N_HEADS // 2 * 8, s), F32), *riding.out_shape),
        in_specs=[pl.BlockSpec((step, 2 * LANES), lambda p, i: (i, p)),
                  pl.BlockSpec((t, 2 * LANES), lambda p, i: (0, p)),
                  pl.BlockSpec((t, LANES), lambda p, i: (0, N_HEADS + p)), *riding.specs],
        out_specs=(o_spec, o_spec, pl.BlockSpec((8, step), lambda p, i: (p, i)), *riding.specs),
        scratch_shapes=riding.scratch,
        compiler_params=_params(("arbitrary", "arbitrary"), VMEM_BIG),
    )(qf, kv, kv, *riding.arrays)
    return outs[0], outs[1], outs[2], list(outs[3:])


def _attn_bwd(qf, kv, o, da, stats, cos, sgn, riding, name):
    s, t = o.shape[0], kv.shape[0]
    ATT_TQ = ATT_TQ_BWD
    nq = s // ATT_TQ
    nr = riding.n

    def body(*refs):
        q_ref, k_ref, v_ref, o_ref, do_ref, st_ref, c_ref, s_ref = refs[:8]
        dq_ref, dk_ref, dv_ref = refs[8 + nr:11 + nr]
        dk_acc, dv_acc = refs[11 + 2 * nr:13 + 2 * nr]
        p, i = pl.program_id(0), pl.program_id(1)
        state = riding.run((p == 0) & (i == 0), (p == N_HEADS // 2 - 1) & (i == nq - 1),
                           refs[8:8 + nr], refs[11 + nr:11 + 2 * nr], refs[13 + 2 * nr:])

        @pl.when(i == 0)
        def _():
            dk_acc[...] = jnp.zeros_like(dk_acc)
            dv_acc[...] = jnp.zeros_like(dv_acc)

        v = v_ref[...]
        do = do_ref[...]
        od = do * o_ref[...]
        ones = jnp.ones((8, LANES), F32)
        for hh in range(2):
            sl = slice(hh * LANES, (hh + 1) * LANES)
            q, k = q_ref[:, sl], k_ref[:, sl]
            mask = _head_mask(do.shape, hh)
            dom = jnp.where(mask, do, 0.0).astype(BF16)
            delta = lax.dot_general(ones, jnp.where(mask, od, 0.0), NT_DIMS, preferred_element_type=F32,
                                    precision=lax.Precision.HIGHEST)[0:1, :]
            st = lax.dot_general(k, q, NT_DIMS, preferred_element_type=F32)
            pt = jnp.exp2(st * EXP2_SCALE - st_ref[hh:hh + 1, :]).astype(BF16)
            dpt = lax.dot_general(v, dom, NT_DIMS, preferred_element_type=F32)
            dst = (pt.astype(F32) * (dpt - delta)).astype(BF16)
            dv_acc[...] += jnp.dot(pt, dom, preferred_element_type=F32)
            dk_acc[:, sl] += jnp.dot(dst, q, preferred_element_type=F32)
            dq = lax.dot_general(dst, k, TN_DIMS, preferred_element_type=F32) * ATTN_SCALE
            dq_ref[:, sl] = _rope(dq, c_ref[...], s_ref[...], True).astype(BF16)

        @pl.when(i == nq - 1)
        def _():
            dk_ref[...] = (dk_acc[...] * ATTN_SCALE).astype(BF16)
            dv_ref[...] = dv_acc[...].astype(BF16)

        riding.finish(state)

    o_spec = pl.BlockSpec((ATT_TQ, LANES), lambda p, i: (i, p))
    tab = pl.BlockSpec((ATT_TQ, LANES), lambda p, i: (i, 0))
    outs = pl.pallas_call(
        body, name=name, grid=(N_HEADS // 2, nq),
        out_shape=(jax.ShapeDtypeStruct((s, N_HEADS * LANES), BF16),
                   jax.ShapeDtypeStruct((t, N_HEADS * LANES), BF16),
                   jax.ShapeDtypeStruct((t, N_HEADS * V_DIM), BF16), *riding.out_shape),
        in_specs=[pl.BlockSpec((ATT_TQ, 2 * LANES), lambda p, i: (i, p)),
                  pl.BlockSpec((t, 2 * LANES), lambda p, i: (0, p)),
                  pl.BlockSpec((t, LANES), lambda p, i: (0, N_HEADS + p)),
                  o_spec, o_spec,
                  pl.BlockSpec((8, ATT_TQ), lambda p, i: (p, i)), tab, tab, *riding.specs],
        out_specs=(pl.BlockSpec((ATT_TQ, 2 * LANES), lambda p, i: (i, p)),
                   pl.BlockSpec((t, 2 * LANES), lambda p, i: (0, p)),
                   pl.BlockSpec((t, LANES), lambda p, i: (0, p)), *riding.specs),
        scratch_shapes=[pltpu.VMEM((t, 2 * LANES), F32), pltpu.VMEM((t, LANES), F32), *riding.scratch],
        compiler_params=_params(("arbitrary", "arbitrary"), VMEM_BIG),
    )(qf, kv, kv, o, da, stats, cos, sgn, *riding.arrays)
    return outs[0], outs[1], outs[2], list(outs[3:])


def _silu(x):
    return x * (1.0 / (1.0 + jnp.exp(-x)))


def _prologue(c_rows, c_ctx, w_mod, b_cols, extra_rows, name):
    d, cols = c_rows.shape[1], w_mod.shape[1]

    def body(c_ref, cctx_ref, wmod_ref, b_ref, x_ref, a_ref, modg_ref, c_all, blk, c_send, c_recv, m_send, m_recv):
        _direct_gather(c_ref, c_all, c_send, c_recv)()
        a_ref[...] = jnp.zeros_like(a_ref)
        for j in range(N_DEV):
            a_ref[j:j + 1, :] = c_all[j, 0:1, :]
        a_ref[N_DEV:N_DEV + 1, :] = cctx_ref[...]
        mod = jnp.dot(_silu(a_ref[...]), wmod_ref[...], preferred_element_type=F32,
                      precision=lax.Precision.HIGHEST) + b_ref[...]
        blk[...] = jnp.zeros_like(blk)
        for p in range(N_DEV):
            blk[p, 0:1, :] = mod[p:p + 1, :]
            blk[p, 1:2, :] = mod[N_DEV:N_DEV + 1, :]
            blk[p, 2:5, :] = x_ref[...]
        _direct_gather(blk, modg_ref, m_send, m_recv, per_peer=True)()

    vmem = pl.BlockSpec(memory_space=pltpu.VMEM)
    return pl.pallas_call(
        body, name=name,
        out_shape=(jax.ShapeDtypeStruct((16, d), F32), jax.ShapeDtypeStruct((N_DEV, 8, cols), F32)),
        in_specs=[vmem] * 5, out_specs=(vmem, vmem),
        scratch_shapes=[pltpu.VMEM((N_DEV, 8, d), F32), pltpu.VMEM((N_DEV, 8, cols), F32)]
        + [pltpu.SemaphoreType.DMA((7,)) for _ in range(4)],
        compiler_params=_params(None, VMEM_BIG),
    )(c_rows, c_ctx, w_mod, b_cols, extra_rows)


def _adaln_bwd(a_t, w, d_ex, d_ctx, d_all, name):
    def body(at_ref, w_ref, dex_ref, dctx_ref, dall_ref, gw_ref, dsil_ref, dsum_ref):
        sil_t = _silu(at_ref[...])
        dctx = dctx_ref[...]
        row = dctx[0:1, :]
        for j in range(1, N_DEV):
            row = row + dctx[j:j + 1, :]
        rowi = lax.broadcasted_iota(jnp.int32, dctx.shape, 0)
        ctx_rows = jnp.where(rowi == 0, jnp.broadcast_to(row, dctx.shape), 0.0)
        hi = lax.Precision.HIGHEST
        d_rows = jnp.concatenate([dex_ref[...], ctx_rows], axis=0)
        gw_ref[...] = jnp.dot(sil_t, d_rows, preferred_element_type=F32, precision=hi)
        dsil_ref[...] = lax.dot_general(ctx_rows, w_ref[...], NT_DIMS, preferred_element_type=F32, precision=hi)
        tot = dall_ref[0]
        for j in range(1, N_DEV):
            tot = tot + dall_ref[j]
        dsum_ref[...] = tot

    return pl.pallas_call(
        body, name=name,
        out_shape=(jax.ShapeDtypeStruct(w.shape, F32), jax.ShapeDtypeStruct((8, w.shape[0]), F32),
                   jax.ShapeDtypeStruct(d_all.shape[1:], F32)),
        compiler_params=_params(None, VMEM_BIG),
    )(a_t, w, d_ex, d_ctx, d_all)


def _pack_small(sums1, sums2, fsums, sums1c, psums, d_cw, name):
    d = D_MODEL

    def body(s1_ref, s2_ref, f_ref, s1c_ref, p_ref, cw_ref, o_ref):
        o_ref[...] = jnp.zeros_like(o_ref)
        for col, (ref, r) in enumerate([(s1_ref, 1), (s1_ref, 0), (s2_ref, 2), (s2_ref, 1), (s2_ref, 0), (f_ref, 1)]):
            o_ref[0:1, col * d:(col + 1) * d] = ref[r:r + 1, :]
        o_ref[1:2, 0:d] = s1c_ref[1:2, :]
        o_ref[1:2, d:2 * d] = s1c_ref[0:1, :]
        o_ref[2:3, 0:Q_RANK] = p_ref[0:1, :]
        o_ref[2:3, Q_RANK:Q_RANK + KV_RANK] = p_ref[1:2, 0:KV_RANK]
        o_ref[2:3, Q_RANK + KV_RANK:Q_RANK + KV_RANK + d] = f_ref[0:1, :]
        for r in range(3):
            o_ref[3 + r:4 + r, 0:CONV_W] = cw_ref[r:r + 1, :]
        o_ref[6:7, 0:d] = f_ref[3:4, :]

    return pl.pallas_call(body, name=name, out_shape=jax.ShapeDtypeStruct((8, 6 * d), F32))(
        sums1, sums2, fsums, sums1c, psums, d_cw)


def _adam_math(w, g, m, v):
    nm = ADAM_B1 * m + (1.0 - ADAM_B1) * g
    nv = ADAM_B2 * v + (1.0 - ADAM_B2) * (g * g)
    m_hat = nm / (1.0 - ADAM_B1 ** ADAM_STEP)
    v_hat = nv / (1.0 - ADAM_B2 ** ADAM_STEP)
    return -ADAM_LR * (m_hat / (jnp.sqrt(v_hat) + ADAM_EPS) + ADAM_WD * w), nm, nv


def _small_update(dsum, dsil_all, g_cw, params, name):
    d = D_MODEL
    n = len(params)

    def body(*refs):
        dsum_ref, dsil_ref, gcw_ref = refs[:3]
        wmv = refs[3:3 + 3 * n]
        outs = refs[3 + 3 * n:]
        tot = dsil_ref[0]
        for j in range(1, N_DEV):
            tot = tot + dsil_ref[j]
        cv = wmv[0][...]
        sg = 1.0 / (1.0 + jnp.exp(-cv))
        off = Q_RANK + KV_RANK
        grads = [tot[0:1, :] * (sg * (1.0 + cv * (1.0 - sg))),
                 dsum_ref[0:1, :] + dsum_ref[1:2, :],
                 dsum_ref[2:3, 0:Q_RANK], dsum_ref[2:3, Q_RANK:off], dsum_ref[2:3, off:off + d],
                 gcw_ref[...]]
        for p, g in enumerate(grads):
            w_ref, m_ref, v_ref = wmv[3 * p:3 * p + 3]
            at = 0 if len(w_ref.shape) == 3 else Ellipsis
            res = (g,) + _adam_math(w_ref[at], g, m_ref[at], v_ref[at])
            for q, val in enumerate(res):
                outs[4 * p + q][at] = val

    flat = [a for wmv in params for a in wmv]
    out_shape = tuple(jax.ShapeDtypeStruct(wmv[0].shape, F32) for wmv in params for _ in range(4))
    outs = pl.pallas_call(body, name=name, out_shape=out_shape)(dsum, dsil_all, g_cw, *flat)
    return [outs[4 * p:4 * p + 4] for p in range(n)]


def _adamw(w, g, m, v, name, slots=False):
    _, rows, cols = w.shape
    tr = _pick(rows, (256, 128, 64, 32, 16, 8))

    def body(w_ref, g_ref, m_ref, v_ref, *outs):
        if slots:
            gv = g_ref[0].astype(F32)
            for j in range(1, g.shape[0]):
                gv = gv + g_ref[j].astype(F32)
            outs[0][...] = gv
        else:
            gv = g_ref[...]
        d_ref, nm_ref, nv_ref = outs[-3:]
        d_ref[...], nm_ref[...], nv_ref[...] = _adam_math(w_ref[...], gv, m_ref[...], v_ref[...])

    blk = pl.BlockSpec((None, tr, cols), lambda i: (0, i, 0))
    g_spec = (pl.BlockSpec((g.shape[0], tr, cols), lambda i: (0, i, 0)) if slots
              else pl.BlockSpec((tr, cols), lambda i: (i, 0)))
    sh = jax.ShapeDtypeStruct((1, rows, cols), F32)
    n_out = 4 if slots else 3
    return pl.pallas_call(
        body, name=name, grid=(rows // tr,), out_shape=(sh,) * n_out,
        in_specs=[blk, g_spec, blk, blk], out_specs=(blk,) * n_out,
        compiler_params=_params(("parallel",), VMEM_BIG),
    )(w, g, m, v)


def _rope_tables(s, l):
    tok = np.arange(s)
    row = (tok // GRID_W).astype(np.float32)
    col = (tok % GRID_W).astype(np.float32)
    half = QK_ROPE // 2
    freqs = np.float32(ROPE_THETA) ** (-np.arange(0, half, 2, dtype=np.float32) / np.float32(half))
    dd = np.arange(QK_ROPE)
    pos = np.where((dd // half)[None, :] == 0, row[:, None], col[:, None]).astype(np.float32)
    ang = (pos * freqs[dd % (half // 2)][None, :]).astype(np.float32)
    sin = np.sin(ang).astype(np.float32)
    cos_t = np.ones((s + l, LANES), np.float32)
    sgn_t = np.zeros((s + l, LANES), np.float32)
    cos_t[:s, QK_NOPE:QK_NOPE + QK_ROPE] = np.cos(ang)
    sgn_t[:s, QK_NOPE:QK_NOPE + QK_ROPE] = np.where(((dd % half) // (half // 2))[None, :] == 0, -sin, sin)
    return jnp.asarray(cos_t), jnp.asarray(sgn_t)


def _slots_to_cols(g):
    return g.transpose(1, 0, 2).reshape(g.shape[1], N_DEV * g.shape[2])


def _cols_to_slots(w):
    return w.reshape(w.shape[0], N_DEV, w.shape[1] // N_DEV).transpose(1, 0, 2)


def _unpack_small_weights(g_in_t, g_uq, g_ukv):
    w_t = g_in_t.reshape(N_DEV * g_in_t.shape[1], D_MODEL)
    zeros = jnp.zeros((QK_NOPE, D_MODEL), BF16)
    win_head_t = jnp.concatenate([w_t[:Q_RANK + KV_RANK], zeros, w_t[Q_RANK + KV_RANK:MLA_IN],
                                  zeros[:LANES - QK_NOPE - QK_ROPE]], axis=0)
    win_conv_t = w_t[MLA_IN:].reshape(3, CONV_W // LANES, LANES, D_MODEL).transpose(1, 0, 2, 3)
    win_conv_t = win_conv_t.reshape(3 * CONV_W, D_MODEL)
    w_uq = _slots_to_cols(g_uq).reshape(Q_RANK, N_HEADS, QK_NOPE + QK_ROPE)
    wq = jnp.pad(w_uq, ((0, 0), (0, 0), (0, LANES - QK_NOPE - QK_ROPE))).reshape(Q_RANK, N_HEADS * LANES)
    w_ukv = _slots_to_cols(g_ukv).reshape(KV_RANK, N_HEADS, QK_NOPE + V_DIM)
    k_top = jnp.pad(w_ukv[:, :, :QK_NOPE], ((0, 0), (0, 0), (0, LANES - QK_NOPE))).reshape(KV_RANK, N_HEADS * LANES)
    v_top = w_ukv[:, :, QK_NOPE:].reshape(KV_RANK, N_HEADS * V_DIM)
    eye = jnp.pad(jnp.eye(QK_ROPE, dtype=BF16), ((QK_NOPE, LANES - QK_NOPE - QK_ROPE),) * 2)
    wk = jnp.concatenate([
        jnp.concatenate([k_top, v_top], axis=1),
        jnp.concatenate([jnp.tile(eye, (1, N_HEADS)), jnp.zeros((LANES, N_HEADS * V_DIM), BF16)], axis=1)], axis=0)
    return win_head_t, win_conv_t, wq, wk


def _pack_small_grads(d_head_t, d_conv_t, d_wq, d_wkk, d_wkv):
    d_conv_t = d_conv_t.reshape(CONV_W // LANES, 3, LANES, D_MODEL).transpose(1, 0, 2, 3).reshape(3 * CONV_W, D_MODEL)
    rope0 = Q_RANK + KV_RANK + QK_NOPE
    g_in_t = jnp.concatenate([d_head_t[:Q_RANK + KV_RANK], d_head_t[rope0:rope0 + QK_ROPE], d_conv_t], axis=0)
    g_in_t = g_in_t.reshape(N_DEV, -1, D_MODEL).astype(BF16)
    g_uq = d_wq.reshape(Q_RANK, N_HEADS, LANES)[:, :, :QK_NOPE + QK_ROPE].reshape(Q_RANK, -1)
    g_kn = d_wkk[:KV_RANK].reshape(KV_RANK, N_HEADS, LANES)[:, :, :QK_NOPE]
    g_v = d_wkv[:KV_RANK].reshape(KV_RANK, N_HEADS, V_DIM)
    g_ukv = jnp.concatenate([g_kn, g_v], axis=2).reshape(KV_RANK, -1)
    return [g_in_t] + [_cols_to_slots(g).astype(BF16) for g in (g_uq, g_ukv)]


def kernel(x, c, ctx, c_ctx, w_mod, b_mod, w_in, q_norm_g, w_uq, kv_norm_g, w_ukv, conv_w, w_out, w_mlp1, w_mlp2, final_norm_g, loss_target, m_c_ctx, m_w_mod, m_b_mod, m_w_in, m_q_norm_g, m_w_uq, m_kv_norm_g, m_w_ukv, m_conv_w, m_w_out, m_w_mlp1, m_w_mlp2, m_final_norm_g, v_c_ctx, v_w_mod, v_b_mod, v_w_in, v_q_norm_g, v_w_uq, v_kv_norm_g, v_w_ukv, v_conv_w, v_w_out, v_w_mlp1, v_w_mlp2, v_final_norm_g):
    me = _my_index()
    x2d, ctx2d, tgt = x[0], ctx[0], loss_target[0]
    s, l = x2d.shape[0], ctx2d.shape[0]
    t = s + l
    d = D_MODEL
    mod_cols = w_mod.shape[2]
    cw_cols = conv_w.shape[2]

    b_cols = lax.dynamic_slice(b_mod, (0, me * mod_cols), (1, mod_cols))
    cw_blk = jnp.pad(conv_w[0], ((0, 0), (0, mod_cols - cw_cols)))
    a_rows, gathered = _prologue(jnp.pad(c, ((0, 7), (0, 0))), c_ctx[None, :], w_mod[0], b_cols, cw_blk,
                                 "prologue")
    mod_mine = gathered[:, 0, :].reshape(1, 6 * d)
    mod_ctx = gathered[:, 1, :].reshape(1, 6 * d)
    cw_full = gathered[:, 2:5, :cw_cols].transpose(1, 0, 2).reshape(3, CONV_W)

    early = [w.astype(BF16) for w in (w_in[0].T, w_uq[0], w_ukv[0])]
    late = [w.astype(BF16) for w in (w_out[0], w_mlp1[0], w_mlp2[0])]
    h_all, (g_in, g_uq, g_ukv) = _modulate_all(x2d, ctx2d, mod_mine, mod_ctx, _RidingGather(early),
                                               "modulate1")
    win_head, win_conv, wq, wk = _unpack_small_weights(g_in, g_uq, g_ukv)
    wk_k, wk_v = wk[:, :N_HEADS * LANES], wk[:, N_HEADS * LANES:]
    cos, sgn = _rope_tables(s, l)

    tm_t = _pick(t, (1088, 768, 256))
    tk_t = _pick(t, (2176, 768, 256))
    z_head, cq, kv_in, qf, kv = _head_fwd(h_all, win_head, wq, wk, q_norm_g, kv_norm_g, cos, sgn, tm_t, "head_fwd")
    z_conv = _matmul(h_all, win_conv, mode="nt", name="in_proj_conv", m=s, tm=1024, tn=1536, tk=1024)
    attn, a_cat, stats, (g_out, w1, g_w2) = _attn_fwd(qf, kv, s, _RidingGather(late), "attn_fwd")
    wo = g_out.reshape(d, d)
    w2 = g_w2.reshape(D_FF, d)
    a_cat = _conv_fwd(z_conv, cw_full, a_cat, "conv_fwd")
    (o, x1, h2), _ = _matmul_rows(a_cat, wo, _epi_resid_modulate, mode="nn", name="out_proj", tm=1024, tk=1024,
                                  rows=[x2d], vecs=[(mod_mine, 2), (mod_mine, 3), (mod_mine, 4)],
                                  out_dtypes=[F32, F32, BF16])
    u1, act = _matmul(h2, w1, mode="nn", name="mlp_up", tm=4096, tk=1024, epilogue="relu2", slots="b_cols")
    (dx2, dm, fsums), _ = _matmul_rows(act, w2, _epi_final, mode="nn", name="mlp_down", tm=512, tk=4096,
                                       rows=[x1, tgt], vecs=[(mod_mine, 5), (final_norm_g[None, :], 0)],
                                       out_dtypes=[F32, BF16], sums=True)

    d_w2 = _matmul(act, dm, mode="tn", name="d_w_mlp2", out_dtype=BF16, tm=1024, tn=1024, tk=4096)
    du1 = _matmul(dm, w2, mode="nt", name="d_act", out_dtype=BF16, tm=2048, tn=1024, tk=1024,
                  epilogue="drelu2", extra=(u1,))
    d_w1 = _matmul(h2, du1, mode="tn", name="d_w_mlp1", out_dtype=BF16, tm=1024, tk=4096, slots="out")
    (dx1, do, sums2), _ = _matmul_rows(du1, w1, _epi_modulate2_bwd, mode="nt", name="d_h2", tm=512, tk=4096,
                                       slots="b_contract", rows=[x1, dx2, o], vecs=[(mod_mine, 4), (mod_mine, 2)],
                                       out_dtypes=[F32, BF16], sums=True)
    d_wo = _matmul(a_cat, do, mode="tn", name="d_w_out", out_dtype=BF16, tm=1024, tn=1024, tk=2048)
    da = _matmul(do, wo, mode="nt", name="d_a", tm=1024, tn=1024, tk=1024)
    dz_conv, d_cw = _conv_bwd(z_conv, cw_full, da, "conv_bwd")
    ready = [d_wo.reshape(N_DEV, d // N_DEV, d), d_w1, d_w2.reshape(N_DEV, D_FF // N_DEV, d)]
    dq, dk, dv, rode = _attn_bwd(qf, kv, attn, da, stats, cos, sgn, _Riding(ready), "attn_bwd")
    d_wq = _matmul(cq, dq, mode="tn", name="d_w_uq", k=s, tm=256, tn=1024, tk=4096)
    d_wkk = _matmul(kv_in, dk, mode="tn", name="d_w_ukv_k", tm=256, tn=1024, tk=tk_t)
    d_wkv = _matmul(kv_in, dv, mode="tn", name="d_w_ukv_v", tm=256, tn=512, tk=tk_t)
    dz_head, dh_head, psums = _head_bwd(dq, dk, dv, z_head, wq, wk_k, wk_v, win_head, q_norm_g, kv_norm_g, cos, sgn, s,
                                        "head_bwd")
    d_head = _matmul(dz_head, h_all, mode="tn", name="d_w_in_head", tm=512, tn=1024, tk=tk_t)
    d_conv = _matmul(dz_conv, h_all, mode="tn", name="d_w_in_conv", k=s, tm=1536, tn=1024, tk=2048)
    send = _pack_small_grads(d_head, d_conv, d_wq, d_wkk, d_wkv)
    (grad_x, sums1), got = _matmul_rows(dz_conv, win_conv, _epi_modulate1_bwd, mode="nn", name="d_h1", tm=max(s // 8, ROW_TILE),
                                        tk=win_conv.shape[0], rows=[dh_head, x2d, dx1], vecs=[(mod_mine, 1)],
                                        out_dtypes=[F32], sums=True, riding=_RidingReduce(send))
    sums1c = _modulate_sums(dh_head, s // ROW_TILE, ctx2d)

    small = _pack_small(sums1, sums2, fsums, sums1c, psums, d_cw, "pack_small")
    (d_all,) = _all_gather([small], "gather_small_grads", True)
    d_cols = lax.dynamic_slice_in_dim(d_all, me * mod_cols, mod_cols, axis=2)
    g_w_mod, dsil, dsum = _adaln_bwd(a_rows.T, w_mod[0], d_cols[:, 0, :], d_cols[:, 1, :], d_all, "adaln_bwd")
    (dsil_all,) = _all_gather([dsil], "gather_d_cctx", True)
    loss = dsum[6, 0]
    g_cw = lax.dynamic_slice(dsum, (3, me * cw_cols), (3, cw_cols))

    slots = dict(zip(["w_in", "w_uq", "w_ukv"], got))
    slots.update(zip(["w_out", "w_mlp1", "w_mlp2"], rode))

    grads = {}
    weights = {"c_ctx": c_ctx, "w_mod": w_mod, "b_mod": b_mod, "w_in": w_in, "q_norm_g": q_norm_g, "w_uq": w_uq,
               "kv_norm_g": kv_norm_g, "w_ukv": w_ukv, "conv_w": conv_w, "w_out": w_out, "w_mlp1": w_mlp1,
               "w_mlp2": w_mlp2, "final_norm_g": final_norm_g}
    m_in = {"c_ctx": m_c_ctx, "w_mod": m_w_mod, "b_mod": m_b_mod, "w_in": m_w_in, "q_norm_g": m_q_norm_g,
            "w_uq": m_w_uq, "kv_norm_g": m_kv_norm_g, "w_ukv": m_w_ukv, "conv_w": m_conv_w, "w_out": m_w_out,
            "w_mlp1": m_w_mlp1, "w_mlp2": m_w_mlp2, "final_norm_g": m_final_norm_g}
    v_in = {"c_ctx": v_c_ctx, "w_mod": v_w_mod, "b_mod": v_b_mod, "w_in": v_w_in, "q_norm_g": v_q_norm_g,
            "w_uq": v_w_uq, "kv_norm_g": v_kv_norm_g, "w_ukv": v_w_ukv, "conv_w": v_conv_w, "w_out": v_w_out,
            "w_mlp1": v_w_mlp1, "w_mlp2": v_w_mlp2, "final_norm_g": v_final_norm_g}
    names = list(weights)
    small_names = ["c_ctx", "b_mod", "q_norm_g", "kv_norm_g", "final_norm_g", "conv_w"]
    delta, new_m, new_v = {}, {}, {}

    def as_rows(a):
        return a[None, :] if a.ndim == 1 else a

    small_out = _small_update(dsum, dsil_all, g_cw, [[as_rows(src[n]) for src in (weights, m_in, v_in)]
                                                      for n in small_names], "small_update")
    for n, outs in zip(small_names, small_out):
        grads[n], delta[n], new_m[n], new_v[n] = [a.reshape(weights[n].shape) for a in outs]
    for n in names:
        if n in small_names:
            continue
        if n == "w_in":
            wmv = [jnp.swapaxes(src[n], 1, 2) for src in (weights, m_in, v_in)]
            outs = _adamw(wmv[0], slots[n], wmv[1], wmv[2], "adamw_" + n, slots=True)
            grads[n], delta[n], new_m[n], new_v[n] = [jnp.swapaxes(a, 1, 2) for a in outs]
        elif n in slots:
            grads[n], delta[n], new_m[n], new_v[n] = _adamw(weights[n], slots[n], m_in[n], v_in[n], "adamw_" + n,
                                                            slots=True)
        else:
            delta[n], new_m[n], new_v[n] = _adamw(weights[n], g_w_mod, m_in[n], v_in[n], "adamw_" + n)
            grads[n] = g_w_mod[None]

    return (loss, grad_x[None], *[grads[n] for n in names], *[delta[n] for n in names],
            *[new_m[n] for n in names], *[new_v[n] for n in names])
```

```python
import math

import jax
import jax.numpy as jnp
import numpy as np
from jax import lax
from jax.experimental import pallas as pl
from jax.experimental.pallas import tpu as pltpu

F32 = jnp.float32
BF16 = jnp.bfloat16

D_MODEL = 1024
GRID_W = 64
N_HEADS = 8
QK_NOPE = 64
QK_ROPE = 32
V_DIM = 64
Q_RANK = 256
KV_RANK = 128
MLA_IN = Q_RANK + KV_RANK + QK_ROPE
CONV_W = 512
HEAD_COLS = 512
D_FF = 4096
ROPE_THETA = 10000.0
EPS = 1e-6
ATTN_SCALE = 1.0 / math.sqrt(QK_NOPE + QK_ROPE)
LOG2_E = 1.0 / math.log(2.0)
EXP2_SCALE = ATTN_SCALE * LOG2_E
N_DEV = 8
LANES = 128

ADAM_LR, ADAM_B1, ADAM_B2, ADAM_EPS, ADAM_WD, ADAM_STEP = 0.001, 0.9, 0.999, 1e-08, 0.01, 10

ROW_TILE = 256
VMEM_BIG = 60 * 1024 * 1024


def _params(sem=None, vmem=None):
    return pltpu.CompilerParams(dimension_semantics=sem, vmem_limit_bytes=vmem)


def _pick(n, prefs):
    for p in prefs:
        if n % p == 0:
            return p
    return n


def _my_index():
    return 4 * lax.axis_index("x") + 2 * lax.axis_index("y") + lax.axis_index("c")


def _two_level_gather(x_refs, out_refs, send_sems, recv_sems, local_sems):
    n = len(x_refs)
    x, y, c = lax.axis_index("x"), lax.axis_index("y"), lax.axis_index("c")
    me, sibling = (x, y, c), (x, y, 1 - c)
    chips = [(1 - x, y), (x, 1 - y), (1 - x, 1 - y)]

    def slot(a, px, py, pc):
        return out_refs[a].at[4 * px + 2 * py + pc]

    def copy(a, k, block, to, src=None):
        return pltpu.make_async_remote_copy(
            src_ref=slot(a, *block) if src is None else src, dst_ref=slot(a, *block),
            send_sem=send_sems.at[7 * a + k], recv_sem=recv_sems.at[7 * a + k],
            device_id=to, device_id_type=pl.DeviceIdType.MESH)

    mine = [pltpu.make_async_copy(x_refs[a], slot(a, *me), local_sems.at[a]) for a in range(n)]
    first = [cp for a in range(n) for cp in
             [copy(a, 0, me, sibling, src=x_refs[a])]
             + [copy(a, 1 + j, me, (*chip, c), src=x_refs[a]) for j, chip in enumerate(chips)]]
    passed = [[copy(a, 4 + j, (*chip, c), sibling) for j, chip in enumerate(chips)] for a in range(n)]

    def start():
        for cp in mine + first:
            cp.start()

    def forward():
        for a in range(n):
            for j, chip in enumerate(chips):
                copy(a, 1 + j, (*chip, c), me).wait_recv()
                passed[a][j].start()

    def finish():
        for a in range(n):
            copy(a, 0, sibling, me).wait_recv()
            for j, chip in enumerate(chips):
                copy(a, 4 + j, (*chip, 1 - c), me).wait_recv()
        for cp in first + [cp for per_array in passed for cp in per_array]:
            cp.wait_send()
        for cp in mine:
            cp.wait()

    return start, forward, finish


def _direct_gather(src_ref, dst_ref, send_sems, recv_sems, per_peer=False):
    x, y, c = lax.axis_index("x"), lax.axis_index("y"), lax.axis_index("c")
    me = 4 * x + 2 * y + c
    dst_ref[me] = src_ref[me] if per_peer else src_ref[...]
    sends, landings = [], []
    for k in range(1, N_DEV):
        peer = (1 - x if k & 4 else x, 1 - y if k & 2 else y, 1 - c if k & 1 else c)
        pid = 4 * peer[0] + 2 * peer[1] + peer[2]
        for dst, out in ((me, sends), (pid, landings)):
            out.append(pltpu.make_async_remote_copy(
                src_ref=src_ref.at[pid] if per_peer else src_ref, dst_ref=dst_ref.at[dst],
                send_sem=send_sems.at[k - 1], recv_sem=recv_sems.at[k - 1],
                device_id=peer, device_id_type=pl.DeviceIdType.MESH))
    for cp in sends:
        cp.start()

    def finish():
        for cp in landings:
            cp.wait_recv()
        for cp in sends:
            cp.wait_send()

    return finish


def _all_gather(arrays, name, in_vmem):
    space = pltpu.VMEM if in_vmem else pl.ANY
    n = len(arrays)

    def body(*refs):
        for phase in _two_level_gather(refs[:n], refs[n:2 * n], *refs[2 * n:]):
            phase()

    outs = pl.pallas_call(
        body, name=name,
        out_shape=tuple(jax.ShapeDtypeStruct((N_DEV,) + a.shape, a.dtype) for a in arrays),
        in_specs=[pl.BlockSpec(memory_space=space)] * n,
        out_specs=tuple(pl.BlockSpec(memory_space=space) for _ in arrays),
        scratch_shapes=[pltpu.SemaphoreType.DMA((7 * n,)), pltpu.SemaphoreType.DMA((7 * n,)),
                        pltpu.SemaphoreType.DMA((n,))],
    )(*arrays)
    return list(outs)


class _Riding:
    def __init__(self, arrays=()):
        self.arrays, self.n = list(arrays), len(arrays)
        self.out_shape = [jax.ShapeDtypeStruct(a.shape, a.dtype) for a in self.arrays]
        self.specs = [pl.BlockSpec(memory_space=pl.ANY)] * self.n
        self.scratch = [pltpu.SemaphoreType.DMA((7 * self.n,)), pltpu.SemaphoreType.DMA((7 * self.n,)),
                        pltpu.SemaphoreType.DMA((self.n,))]

    def copies(self, x_refs, y_refs, send_sems, recv_sems, local_sems):
        x, y, c = lax.axis_index("x"), lax.axis_index("y"), lax.axis_index("c")
        me = 4 * x + 2 * y + c
        local, sends, landings = [], [], []
        for a in range(self.n):
            local.append(pltpu.make_async_copy(x_refs[a].at[me], y_refs[a].at[me], local_sems.at[a]))
            for k in range(1, N_DEV):
                peer = (1 - x if k & 4 else x, 1 - y if k & 2 else y, 1 - c if k & 1 else c)
                pid = 4 * peer[0] + 2 * peer[1] + peer[2]
                for dst, out in ((me, sends), (pid, landings)):
                    out.append(pltpu.make_async_remote_copy(
                        src_ref=x_refs[a].at[pid], dst_ref=y_refs[a].at[dst],
                        send_sem=send_sems.at[7 * a + k - 1], recv_sem=recv_sems.at[7 * a + k - 1],
                        device_id=peer, device_id_type=pl.DeviceIdType.MESH))
        return local, sends, landings

    def run(self, first, last, x_refs, y_refs, sems, middle=None):
        if self.n == 0:
            return None
        local, sends, landings = self.copies(x_refs, y_refs, *sems)

        @pl.when(first)
        def _():
            for cp in local + sends:
                cp.start()

        return local, sends, landings, last

    @staticmethod
    def finish(state):
        if state is None:
            return
        local, sends, landings, last = state

        @pl.when(last)
        def _():
            for cp in landings:
                cp.wait_recv()
            for cp in sends:
                cp.wait_send()
            for cp in local:
                cp.wait()


class _RidingGather:
    def __init__(self, arrays):
        self.arrays, self.n = list(arrays), len(arrays)
        self.out_shape = [jax.ShapeDtypeStruct((N_DEV,) + a.shape, a.dtype) for a in self.arrays]
        self.specs = [pl.BlockSpec(memory_space=pl.ANY)] * self.n
        self.scratch = [pltpu.SemaphoreType.DMA((7 * self.n,)), pltpu.SemaphoreType.DMA((7 * self.n,)),
                        pltpu.SemaphoreType.DMA((self.n,))]

    def run(self, first, last, x_refs, y_refs, sems, middle):
        start, forward, finish = _two_level_gather(x_refs, y_refs, *sems)
        pl.when(first)(start)
        pl.when(middle)(forward)
        return finish, last

    @staticmethod
    def finish(state):
        finish, last = state
        pl.when(last)(finish)


class _RidingReduce:
    def __init__(self, arrays):
        self.arrays, self.n = list(arrays), len(arrays)
        self.out_shape = [jax.ShapeDtypeStruct((4,) + a.shape[1:], a.dtype) for a in self.arrays]
        self.specs = [pl.BlockSpec(memory_space=pl.ANY)] * self.n
        self.scratch = [pltpu.VMEM((4,) + a.shape[1:], a.dtype) for a in self.arrays for _ in range(3)]
        self.scratch += [pltpu.SemaphoreType.DMA((self.n,)) for _ in range(6)]

    def run(self, first, last, x_refs, y_refs, scratch, middle):
        n = self.n
        own, sib, tot = scratch[0:3 * n:3], scratch[1:3 * n:3], scratch[2:3 * n:3]
        d2d_send, d2d_recv, local_in, ici_send, ici_recv, local_out = scratch[3 * n:]
        x, y, c = lax.axis_index("x"), lax.axis_index("y"), lax.axis_index("c")
        my_chip = 2 * x + y
        sibling = (x, y, 1 - c)
        others = [(1 - x, y), (x, 1 - y), (1 - x, 1 - y)]

        def to_sibling(a, j=None):
            src = x_refs[a].at[pl.ds(0, 4)] if j is None else x_refs[a].at[2 * j + 1 - c]
            dst = sib[a] if j is None else sib[a].at[j]
            return pltpu.make_async_remote_copy(src_ref=src, dst_ref=dst, send_sem=d2d_send.at[a],
                                                recv_sem=d2d_recv.at[a], device_id=sibling,
                                                device_id_type=pl.DeviceIdType.MESH)

        def mine_in(a, j=None):
            src = x_refs[a].at[pl.ds(0, 4)] if j is None else x_refs[a].at[2 * j + c]
            return pltpu.make_async_copy(src, own[a] if j is None else own[a].at[j], local_in.at[a])

        def to_chip(a, chip=None):
            if chip is None:
                src, dst, peer = tot[a].at[pl.ds(0, 3)], y_refs[a].at[pl.ds(0, 3)], sibling
            else:
                src, dst, peer = tot[a].at[2 * chip[0] + chip[1]], y_refs[a].at[my_chip], (*chip, c)
            return pltpu.make_async_remote_copy(src_ref=src, dst_ref=dst, send_sem=ici_send.at[a],
                                                recv_sem=ici_recv.at[a], device_id=peer,
                                                device_id_type=pl.DeviceIdType.MESH)

        def mine_out(a):
            return pltpu.make_async_copy(tot[a].at[my_chip], y_refs[a].at[my_chip], local_out.at[a])

        @pl.when(first)
        def _():
            for a in range(n):
                for j in range(4):
                    to_sibling(a, j).start()
                    mine_in(a, j).start()

        @pl.when(middle)
        def _():
            for a in range(n):
                to_sibling(a).wait_recv()
                to_sibling(a).wait_send()
                mine_in(a).wait()
                tot[a][...] = (own[a][...].astype(F32) + sib[a][...].astype(F32)).astype(tot[a].dtype)
                for chip in others:
                    to_chip(a, chip).start()
                mine_out(a).start()

        def finish():
            @pl.when(last)
            def _():
                for a in range(n):
                    to_chip(a).wait_recv()
                    to_chip(a).wait_send()
                    mine_out(a).wait()

        return finish

    @staticmethod
    def finish(state):
        state()


_DIMS ={"nn": (((1,), (0,)), ((), ())), "nt": (((1,), (1,)), ((), ())), "tn": (((0,), (0,)), ((), ()))}
NT_DIMS = _DIMS["nt"]
TN_DIMS = _DIMS["tn"]


def _swap8(x):
    lane = lax.broadcasted_iota(jnp.int32, x.shape, 1)
    return jnp.where((lane & 15) < 8, pltpu.roll(x, LANES - 8, 1), pltpu.roll(x, 8, 1))


def _rope(x, cos, sgn, bwd):
    return x * cos + (_swap8(x * sgn) if bwd else _swap8(x) * sgn)


def _matmul(a, b, *, mode, name, out_dtype=F32, tm=512, tn=512, tk=512, m=None, k=None,
            epilogue=None, extra=(), slots=None):
    if mode == "nn":
        m = a.shape[0] if m is None else m
        k = a.shape[1]
        n = N_DEV * b.shape[2] if slots == "b_cols" else b.shape[1]
    elif mode == "nt":
        m = a.shape[0] if m is None else m
        k = a.shape[1]
        n = b.shape[0]
    else:
        k = a.shape[0] if k is None else k
        m, n = a.shape[1], b.shape[1]
    tm, tn, tk = min(tm, m), min(tn, n), min(tk, k)
    if slots == "b_cols":
        tn = b.shape[2]
    if slots == "out":
        tn = n // N_DEV
    assert m % tm == 0 and n % tn == 0 and k % tk == 0, (name, m, n, k, tm, tn, tk)
    nk = k // tk
    dims = _DIMS[mode]
    a_spec = (pl.BlockSpec((tk, tm), lambda i, j, kk: (kk, i)) if mode == "tn"
              else pl.BlockSpec((tm, tk), lambda i, j, kk: (i, kk)))
    if slots == "b_cols":
        b_spec = pl.BlockSpec((None, tk, tn), lambda i, j, kk: (j, kk, 0))
    elif mode == "nt":
        b_spec = pl.BlockSpec((tn, tk), lambda i, j, kk: (j, kk))
    else:
        b_spec = pl.BlockSpec((tk, tn), lambda i, j, kk: (kk, j))
    tile = pl.BlockSpec((tm, tn), lambda i, j, kk: (i, j))
    if slots == "out":
        o_spec = pl.BlockSpec((None, tm, tn), lambda i, j, kk: (j, i, 0))
        o_shape = (N_DEV, m, tn)
    else:
        o_spec, o_shape = tile, (m, n)
    in_specs, args = [a_spec, b_spec], [a, b]
    if epilogue == "drelu2":
        in_specs.append(tile)
    args += list(extra)
    if epilogue == "relu2":
        out_shape = (jax.ShapeDtypeStruct(o_shape, BF16), jax.ShapeDtypeStruct(o_shape, BF16))
        out_specs = (o_spec, o_spec)
    else:
        out_shape = jax.ShapeDtypeStruct(o_shape, out_dtype)
        out_specs = o_spec
    n_in = len(args)
    n_out = 2 if epilogue == "relu2" else 1

    def body(*refs):
        a_ref, b_ref = refs[0], refs[1]
        outs = refs[n_in:n_in + n_out]
        part = lax.dot_general(a_ref[...], b_ref[...], dims, preferred_element_type=F32)

        def finish(acc):
            if epilogue == "relu2":
                outs[0][...] = acc.astype(BF16)
                r = jnp.maximum(acc, 0.0)
                outs[1][...] = (r * r).astype(BF16)
            elif epilogue == "drelu2":
                u = refs[2][...].astype(F32)
                outs[0][...] = (acc * (2.0 * jnp.maximum(u, 0.0))).astype(out_dtype)
            else:
                outs[0][...] = acc.astype(out_dtype)

        if nk == 1:
            finish(part)
        else:
            acc_ref = refs[n_in + n_out]
            kk = pl.program_id(2)

            @pl.when(kk == 0)
            def _():
                acc_ref[...] = part

            @pl.when(kk > 0)
            def _():
                acc_ref[...] += part

            @pl.when(kk == nk - 1)
            def _():
                finish(acc_ref[...])

    return pl.pallas_call(
        body, name=name, grid=(m // tm, n // tn, nk),
        out_shape=out_shape, in_specs=in_specs, out_specs=out_specs,
        scratch_shapes=[pltpu.VMEM((tm, tn), F32)] if nk > 1 else [],
        compiler_params=_params(("parallel", "parallel", "arbitrary"), VMEM_BIG),
    )(*args)


def _rstd(x):
    return lax.rsqrt(jnp.mean(x * x, axis=1, keepdims=True) + EPS)


def _norm_bwd(dxn, xn, r):
    return r * (dxn - xn * jnp.mean(dxn * xn, axis=1, keepdims=True))


def _vec(col):
    return pl.BlockSpec((1, D_MODEL), lambda i: (0, col))


def _matmul_rows(a, b, epi, *, mode, name, tm, tk, rows=(), vecs=(), out_dtypes=(), sums=False, slots=None,
                 riding=None):
    m, k = a.shape
    n = D_MODEL
    tm, tk = min(tm, m), min(tk, k)
    riding = riding or _Riding()
    group = 1
    if slots == "b_contract":
        group = max(1, tk // b.shape[2])
        tk = group * b.shape[2]
        b_spec = pl.BlockSpec((group, n, tk // group), lambda i, kk: (kk, 0, 0))
    elif mode == "nt":
        b_spec = pl.BlockSpec((n, tk), lambda i, kk: (0, kk))
    else:
        b_spec = pl.BlockSpec((tk, n), lambda i, kk: (kk, 0))
    assert m % tm == 0 and k % tk == 0, (name, m, k, tm, tk)
    ni, nk = m // tm, k // tk
    assert ni >= 2 or not isinstance(riding, _RidingReduce), "the two-level exchange needs a middle grid step"
    dims = _DIMS[mode]
    tile = pl.BlockSpec((tm, n), lambda i, kk: (i, 0))
    in_specs = [pl.BlockSpec((tm, tk), lambda i, kk: (i, kk)), b_spec] + [tile] * len(rows)
    in_specs += [pl.BlockSpec((1, n), lambda i, kk, col=col: (0, col)) for _, col in vecs]
    args = [a, b, *rows, *[v for v, _ in vecs]]
    out_shape = [jax.ShapeDtypeStruct((m, n), dt) for dt in out_dtypes]
    out_specs = [tile] * len(out_dtypes)
    if sums:
        out_shape.append(jax.ShapeDtypeStruct((8, n), F32))
        out_specs.append(pl.BlockSpec((8, n), lambda i, kk: (0, 0)))
    n_rows, n_vecs, n_outs, nr = len(rows), len(vecs), len(out_dtypes), riding.n
    n_in = 2 + n_rows + n_vecs

    def body(*refs):
        a_ref, b_ref = refs[0], refs[1]
        row_refs = refs[2:2 + n_rows]
        vec_refs = refs[2 + n_rows:n_in]
        x_refs = refs[n_in:n_in + nr]
        out_refs = refs[n_in + nr:n_in + nr + n_outs]
        pos = n_in + nr + n_outs
        sums_ref = refs[pos] if sums else None
        pos += 1 if sums else 0
        y_refs = refs[pos:pos + nr]
        pos += nr
        acc_ref = refs[pos] if nk > 1 else None
        sem_refs = refs[pos + (1 if nk > 1 else 0):]
        i, kk = pl.program_id(0), pl.program_id(1)
        state = riding.run((i == 0) & (kk == 0), (i == ni - 1) & (kk == nk - 1), x_refs, y_refs, sem_refs,
                           middle=(i == 1) & (kk == 0))
        if slots == "b_contract":
            c = tk // group
            part = lax.dot_general(a_ref[:, 0:c], b_ref[0], dims, preferred_element_type=F32)
            for u in range(1, group):
                part = part + lax.dot_general(a_ref[:, u * c:(u + 1) * c], b_ref[u], dims, preferred_element_type=F32)
        else:
            part = lax.dot_general(a_ref[...], b_ref[...], dims, preferred_element_type=F32)

        def finish(acc):
            nsub = tm // ROW_TILE
            for r in range(nsub):
                blk = pl.ds(r * ROW_TILE, ROW_TILE)
                epi(acc[r * ROW_TILE:(r + 1) * ROW_TILE], [ref.at[blk] for ref in row_refs], vec_refs,
                    [ref.at[blk] for ref in out_refs], sums_ref,
                    (i == 0) if r == 0 else None, (i == ni - 1) if r == nsub - 1 else None)

        if nk == 1:
            finish(part)
        else:
            @pl.when(kk == 0)
            def _():
                acc_ref[...] = part

            @pl.when(kk > 0)
            def _():
                acc_ref[...] += part

            @pl.when(kk == nk - 1)
            def _():
                finish(acc_ref)

        riding.finish(state)

    outs = pl.pallas_call(
        body, name=name, grid=(ni, nk),
        out_shape=(*out_shape, *riding.out_shape),
        in_specs=[*in_specs, *riding.specs], out_specs=(*out_specs, *riding.specs),
        scratch_shapes=([pltpu.VMEM((tm, n), F32)] if nk > 1 else []) + (riding.scratch if nr else []),
        compiler_params=_params(("arbitrary", "arbitrary"), VMEM_BIG),
    )(*args, *riding.arrays)
    n_own = len(out_shape)
    return list(outs[:n_own]), list(outs[n_own:])


def _zero_sums_at_start(sums_ref, first):
    if first is not None:
        @pl.when(first)
        def _():
            sums_ref[...] = jnp.zeros_like(sums_ref)


def _epi_resid_modulate(acc, rows, vecs, outs, sums_ref, first, last):
    (x_ref,), (g_ref, sh_ref, sc_ref) = rows, vecs
    x1 = x_ref[...] + g_ref[...] * acc
    outs[0][...] = acc
    outs[1][...] = x1
    outs[2][...] = (x1 * _rstd(x1) * (1.0 + sc_ref[...]) + sh_ref[...]).astype(BF16)


def _epi_final(acc, rows, vecs, outs, sums_ref, first, last):
    (x1_ref, t_ref), (g_ref, gf_ref) = rows, vecs
    d = acc.shape[1]
    x2 = x1_ref[...] + g_ref[...] * acc
    r = _rstd(x2)
    xn = x2 * r
    err = xn * gf_ref[...] - t_ref[...]
    dy = err * (1.0 / d)
    dx2 = _norm_bwd(dy * gf_ref[...], xn, r)
    outs[0][...] = dx2
    outs[1][...] = (dx2 * g_ref[...]).astype(BF16)
    _zero_sums_at_start(sums_ref, first)
    sums_ref[0:1, :] += jnp.sum(dy * xn, axis=0, keepdims=True)
    sums_ref[1:2, :] += jnp.sum(dx2 * acc, axis=0, keepdims=True)
    sums_ref[2:3, :] += jnp.sum(err * err, axis=0, keepdims=True)

    if last is not None:
        @pl.when(last)
        def _():
            tot = jnp.sum(sums_ref[2:3, :], axis=1, keepdims=True) * (0.5 / d)
            sums_ref[3:4, :] = jnp.broadcast_to(tot, (1, d))


def _epi_modulate2_bwd(acc, rows, vecs, outs, sums_ref, first, last):
    (x_ref, dres_ref, o_ref), (sc_ref, g_ref) = rows, vecs
    x = x_ref[...]
    r = _rstd(x)
    xn = x * r
    dx = dres_ref[...] + _norm_bwd(acc * (1.0 + sc_ref[...]), xn, r)
    outs[0][...] = dx
    outs[1][...] = (dx * g_ref[...]).astype(BF16)
    _zero_sums_at_start(sums_ref, first)
    sums_ref[0:1, :] += jnp.sum(acc * xn, axis=0, keepdims=True)
    sums_ref[1:2, :] += jnp.sum(acc, axis=0, keepdims=True)
    sums_ref[2:3, :] += jnp.sum(dx * o_ref[...], axis=0, keepdims=True)


def _epi_modulate1_bwd(acc, rows, vecs, outs, sums_ref, first, last):
    (add_ref, x_ref, dres_ref), (sc_ref,) = rows, vecs
    dh = acc + add_ref[...]
    x = x_ref[...]
    r = _rstd(x)
    xn = x * r
    outs[0][...] = dres_ref[...] + _norm_bwd(dh * (1.0 + sc_ref[...]), xn, r)
    _zero_sums_at_start(sums_ref, first)
    sums_ref[0:1, :] += jnp.sum(dh * xn, axis=0, keepdims=True)
    sums_ref[1:2, :] += jnp.sum(dh, axis=0, keepdims=True)


def _modulate_all(x, ctx, mod, mod_ctx, riding, name):
    s, d = x.shape
    t = s + ctx.shape[0]
    ns = s // ROW_TILE
    nc = ctx.shape[0] // ROW_TILE
    nr = riding.n

    def body(*refs):
        x_ref, c_ref, sh_ref, sc_ref, shc_ref, scc_ref = refs[:6]
        h_ref = refs[6 + nr]
        i = pl.program_id(0)
        state = riding.run(i == 0, i == ns + nc - 1, refs[6:6 + nr], refs[7 + nr:7 + 2 * nr], refs[7 + 2 * nr:],
                           middle=i == ns + nc - 3)

        @pl.when(i < ns)
        def _():
            v = x_ref[...]
            h_ref[...] = (v * _rstd(v) * (1.0 + sc_ref[...]) + sh_ref[...]).astype(BF16)

        @pl.when(i >= ns)
        def _():
            v = c_ref[...]
            h_ref[...] = (v * _rstd(v) * (1.0 + scc_ref[...]) + shc_ref[...]).astype(BF16)

        riding.finish(state)

    outs = pl.pallas_call(
        body, name=name, grid=(ns + nc,),
        out_shape=(jax.ShapeDtypeStruct((t, d), BF16), *riding.out_shape),
        in_specs=[pl.BlockSpec((ROW_TILE, d), lambda i: (jnp.minimum(i, ns - 1), 0)),
                  pl.BlockSpec((ROW_TILE, d), lambda i: (jnp.maximum(i - ns, 0), 0)),
                  _vec(0), _vec(1), _vec(0), _vec(1), *riding.specs],
        out_specs=(pl.BlockSpec((ROW_TILE, d), lambda i: (i, 0)), *riding.specs),
        scratch_shapes=riding.scratch,
        compiler_params=_params(("arbitrary",)),
    )(x, ctx, mod, mod, mod_ctx, mod_ctx, *riding.arrays)
    return outs[0], list(outs[1:])


def _modulate_sums(dh, row_off, xsrc):
    s, d = xsrc.shape

    def body(dh_ref, x_ref, sums_ref):
        i = pl.program_id(0)
        x = x_ref[...]
        dhv = dh_ref[...]

        @pl.when(i == 0)
        def _():
            sums_ref[...] = jnp.zeros_like(sums_ref)

        sums_ref[0:1, :] += jnp.sum(dhv * (x * _rstd(x)), axis=0, keepdims=True)
        sums_ref[1:2, :] += jnp.sum(dhv, axis=0, keepdims=True)

    return pl.pallas_call(
        body, name="modulate1_ctx_bwd", grid=(s // ROW_TILE,),
        out_shape=jax.ShapeDtypeStruct((8, d), F32),
        in_specs=[pl.BlockSpec((ROW_TILE, d), lambda i: (i + row_off, 0)), pl.BlockSpec((ROW_TILE, d), lambda i: (i, 0))],
        out_specs=pl.BlockSpec((8, d), lambda i: (0, 0)),
        compiler_params=_params(("arbitrary",)),
    )(dh, xsrc)


def _head_fwd(h_all, win_head, wq, wk, q_gain, kv_gain, cos, sgn, tm, name):
    t, d = h_all.shape
    nq, nkv = wq.shape[1], wk.shape[1]

    def body(h_ref, wi_ref, wq_ref, wk_ref, qg_ref, kg_ref, c_ref, s_ref, z_ref, cq_ref, kvin_ref, qf_ref, kv_ref):
        z = lax.dot_general(h_ref[...], wi_ref[...], NT_DIMS, preferred_element_type=F32)
        z_ref[...] = z
        cos, sgn = c_ref[...], s_ref[...]
        zq = z[:, 0:Q_RANK]
        cq = (zq * _rstd(zq) * qg_ref[...]).astype(BF16)
        cq_ref[...] = cq
        zk = z[:, Q_RANK:Q_RANK + KV_RANK]
        kv_in = jnp.concatenate([(zk * _rstd(zk) * kg_ref[...]).astype(BF16),
                                 _rope(z[:, Q_RANK + KV_RANK:HEAD_COLS], cos, sgn, False).astype(BF16)], axis=1)
        kvin_ref[...] = kv_in
        q = jnp.dot(cq, wq_ref[...], preferred_element_type=F32)
        for h in range(nq // LANES):
            sl = slice(h * LANES, (h + 1) * LANES)
            qf_ref[:, sl] = _rope(q[:, sl], cos, sgn, False).astype(BF16)
        kv_ref[...] = jnp.dot(kv_in, wk_ref[...], preferred_element_type=F32).astype(BF16)

    def row(w):
        return pl.BlockSpec((tm, w), lambda i: (i, 0))

    def whole(a):
        return pl.BlockSpec(a.shape, lambda i: (0, 0))

    return pl.pallas_call(
        body, name=name, grid=(t // tm,),
        out_shape=(jax.ShapeDtypeStruct((t, HEAD_COLS), F32), jax.ShapeDtypeStruct((t, Q_RANK), BF16),
                   jax.ShapeDtypeStruct((t, KV_RANK + LANES), BF16), jax.ShapeDtypeStruct((t, nq), BF16),
                   jax.ShapeDtypeStruct((t, nkv), BF16)),
        in_specs=[row(d), whole(win_head), whole(wq), whole(wk), whole(q_gain), whole(kv_gain), row(LANES), row(LANES)],
        out_specs=(row(HEAD_COLS), row(Q_RANK), row(KV_RANK + LANES), row(nq), row(nkv)),
        compiler_params=_params(("parallel",), VMEM_BIG),
    )(h_all, win_head, wq, wk, q_gain, kv_gain, cos, sgn)


def _head_bwd(dq, dk, dv, z, wq, wk_k, wk_v, win_head, q_gain, kv_gain, cos, sgn, s, name):
    t = z.shape[0]
    ns = s // ROW_TILE

    def body(dq_ref, dk_ref, dv_ref, z_ref, wq_ref, wkk_ref, wkv_ref, wi_ref, qg_ref, kg_ref, c_ref, s_ref,
             dz_ref, dh_ref, sums_ref):
        i = pl.program_id(0)

        @pl.when(i == 0)
        def _():
            sums_ref[...] = jnp.zeros_like(sums_ref)

        @pl.when(i < ns)
        def _():
            dc = lax.dot_general(dq_ref[...], wq_ref[...], NT_DIMS, preferred_element_type=F32)
            zq = z_ref[:, 0:Q_RANK]
            r = _rstd(zq)
            zn = zq * r
            sums_ref[0:1, :] += jnp.sum(dc * zn, axis=0, keepdims=True)
            dz_ref[:, 0:Q_RANK] = _norm_bwd(dc * qg_ref[...], zn, r).astype(BF16)

        @pl.when(i >= ns)
        def _():
            dz_ref[:, 0:Q_RANK] = jnp.zeros((ROW_TILE, Q_RANK), BF16)

        dkv = (lax.dot_general(dk_ref[...], wkk_ref[...], NT_DIMS, preferred_element_type=F32)
               + lax.dot_general(dv_ref[...], wkv_ref[...], NT_DIMS, preferred_element_type=F32))
        zk = z_ref[:, Q_RANK:Q_RANK + KV_RANK]
        r = _rstd(zk)
        zn = zk * r
        dc = dkv[:, 0:KV_RANK]
        sums_ref[1:2, 0:KV_RANK] += jnp.sum(dc * zn, axis=0, keepdims=True)
        dz_ref[:, Q_RANK:Q_RANK + KV_RANK] = _norm_bwd(dc * kg_ref[...], zn, r).astype(BF16)
        dz_ref[:, Q_RANK + KV_RANK:HEAD_COLS] = _rope(dkv[:, KV_RANK:KV_RANK + LANES], c_ref[...], s_ref[...],
                                                       True).astype(BF16)
        dh_ref[...] = jnp.dot(dz_ref[...], wi_ref[...], preferred_element_type=F32)

    def row(w):
        return pl.BlockSpec((ROW_TILE, w), lambda i: (i, 0))

    def whole(a):
        return pl.BlockSpec(a.shape, lambda i: (0, 0))

    return pl.pallas_call(
        body, name=name, grid=(t // ROW_TILE,),
        out_shape=(jax.ShapeDtypeStruct((t, HEAD_COLS), BF16), jax.ShapeDtypeStruct((t, D_MODEL), F32),
                   jax.ShapeDtypeStruct((8, Q_RANK), F32)),
        in_specs=[pl.BlockSpec((ROW_TILE, dq.shape[1]), lambda i: (jnp.minimum(i, ns - 1), 0)),
                  row(dk.shape[1]), row(dv.shape[1]), row(HEAD_COLS), whole(wq), whole(wk_k), whole(wk_v),
                  whole(win_head), whole(q_gain), whole(kv_gain), row(LANES), row(LANES)],
        out_specs=(row(HEAD_COLS), row(D_MODEL), pl.BlockSpec((8, Q_RANK), lambda i: (0, 0))),
        compiler_params=_params(("arbitrary",), VMEM_BIG),
    )(dq, dk, dv, z, wq, wk_k, wk_v, win_head, q_gain, kv_gain, cos, sgn)


def _shift_rows(u, s):
    rowi = lax.broadcasted_iota(jnp.int32, u.shape, 0)
    prev = jnp.where(rowi == 0, 0.0, pltpu.roll(u, 1, 0))
    nxt = jnp.where(rowi == s - 1, 0.0, pltpu.roll(u, s - 1, 0))
    return prev, nxt


def _conv_fwd(z_conv, cw, a_cat, name):
    s = z_conv.shape[0]

    def body(z_ref, w_ref, a_in_ref, o_ref):
        del a_in_ref
        gb, gc, xv = z_ref[:, 0:LANES], z_ref[:, LANES:2 * LANES], z_ref[:, 2 * LANES:3 * LANES]
        u = gc * xv
        prev, nxt = _shift_rows(u, s)
        y = w_ref[0:1, :] * prev + w_ref[1:2, :] * u + w_ref[2:3, :] * nxt
        o_ref[...] = (gb * y).astype(BF16)

    return pl.pallas_call(
        body, name=name, grid=(CONV_W // LANES,),
        out_shape=jax.ShapeDtypeStruct(a_cat.shape, a_cat.dtype),
        in_specs=[pl.BlockSpec((s, 3 * LANES), lambda j: (0, j)), pl.BlockSpec((3, LANES), lambda j: (0, j)),
                  pl.BlockSpec(memory_space=pl.ANY)],
        out_specs=pl.BlockSpec((s, LANES), lambda j: (0, 4 + j)),
        input_output_aliases={2: 0},
        compiler_params=_params(("parallel",), VMEM_BIG),
    )(z_conv, cw, a_cat)


def _conv_bwd(z_conv, cw, da, name):
    s = z_conv.shape[0]

    def body(z_ref, w_ref, da_ref, dz_ref, dw_ref):
        gb, gc, xv = z_ref[:, 0:LANES], z_ref[:, LANES:2 * LANES], z_ref[:, 2 * LANES:3 * LANES]
        u = gc * xv
        prev, nxt = _shift_rows(u, s)
        dcv = da_ref[...]
        dz_ref[:, 0:LANES] = (dcv * (w_ref[0:1, :] * prev + w_ref[1:2, :] * u + w_ref[2:3, :] * nxt)).astype(BF16)
        dy = dcv * gb
        dw_ref[0:1, :] = jnp.sum(dy * prev, axis=0, keepdims=True)
        dw_ref[1:2, :] = jnp.sum(dy * u, axis=0, keepdims=True)
        dw_ref[2:3, :] = jnp.sum(dy * nxt, axis=0, keepdims=True)
        dyp, dyn = _shift_rows(dy, s)
        du = w_ref[0:1, :] * dyn + w_ref[1:2, :] * dy + w_ref[2:3, :] * dyp
        dz_ref[:, LANES:2 * LANES] = (du * xv).astype(BF16)
        dz_ref[:, 2 * LANES:3 * LANES] = (du * gc).astype(BF16)

    blk = pl.BlockSpec((s, 3 * LANES), lambda j: (0, j))
    cws = pl.BlockSpec((3, LANES), lambda j: (0, j))
    return pl.pallas_call(
        body, name=name, grid=(CONV_W // LANES,),
        out_shape=(jax.ShapeDtypeStruct(z_conv.shape, BF16), jax.ShapeDtypeStruct((3, CONV_W), F32)),
        in_specs=[blk, cws, pl.BlockSpec((s, LANES), lambda j: (0, 4 + j))], out_specs=(blk, cws),
        compiler_params=_params(("parallel",), VMEM_BIG),
    )(z_conv, cw, da)


ATT_TQ = 512
ATT_Q_STEP = 1024
ATT_TQ_BWD = 512


def _head_mask(shape, hh):
    lane = lax.broadcasted_iota(jnp.int32, shape, 1)
    return (lane >= hh * V_DIM) & (lane < (hh + 1) * V_DIM)


def _attn_fwd(qf, kv, s, riding, name):
    t = kv.shape[0]
    step = min(ATT_Q_STEP, s)
    nq = s // step
    nr = riding.n

    def body(*refs):
        q_ref, k_ref, v_ref = refs[:3]
        o_ref, ob_ref, st_ref = refs[3 + nr:6 + nr]
        p, i = pl.program_id(0), pl.program_id(1)
        state = riding.run((p == 0) & (i == 0), (p == N_HEADS // 2 - 1) & (i == nq - 1),
                           refs[3:3 + nr], refs[6 + nr:6 + 2 * nr], refs[6 + 2 * nr:],
                           middle=(p == N_HEADS // 2 - 2) & (i == nq // 2))
        v = v_ref[...]
        vlane = lax.broadcasted_iota(jnp.int32, v.shape, 1)
        one_lane = [(1 - hh) * V_DIM for hh in range(2)]
        vm = [jnp.where(_head_mask(v.shape, hh), v, jnp.where(vlane == one_lane[hh], 1.0, 0.0).astype(BF16))
              for hh in range(2)]

        def block(r, carry):
            rows = pl.ds(pl.multiple_of(r * ATT_TQ, ATT_TQ), ATT_TQ)
            olane = lax.broadcasted_iota(jnp.int32, (ATT_TQ, LANES), 1)
            acc = jnp.zeros((ATT_TQ, LANES), F32)
            stat = jnp.zeros((ATT_TQ, LANES), F32)
            scores = [lax.dot_general(q_ref[rows, hh * LANES:(hh + 1) * LANES], k_ref[:, hh * LANES:(hh + 1) * LANES],
                                      NT_DIMS, preferred_element_type=F32) for hh in range(2)]
            maxes = [jnp.max(sc, axis=1, keepdims=True) for sc in scores]
            exps = [jnp.exp2((sc - mx) * EXP2_SCALE).astype(BF16) for sc, mx in zip(scores, maxes)]
            for hh in range(2):
                mx = maxes[hh]
                res = jnp.dot(exps[hh], vm[hh], preferred_element_type=F32)
                den = jnp.sum(jnp.where(olane == one_lane[hh], res, 0.0), axis=1, keepdims=True)
                acc = acc + jnp.where(_head_mask(res.shape, hh), res * (1.0 / den), 0.0)
                stat = stat + jnp.where(olane == hh, mx * EXP2_SCALE + jnp.log(den) * LOG2_E, 0.0)
            o_ref[rows, :] = acc
            ob_ref[rows, :] = acc.astype(BF16)
            st_ref[:, rows] = stat.T[0:8, :]
            return carry

        lax.fori_loop(0, step // ATT_TQ, block, 0)
        riding.finish(state)

    o_spec = pl.BlockSpec((step, LANES), lambda p, i: (i, p))
    outs = pl.pallas_call(
        body, name=name, grid=(N_HEADS // 2, nq),
        out_shape=(jax.ShapeDtypeStruct((s, N_HEADS * V_DIM), F32),
                   jax.ShapeDtypeStruct((s, D_MODEL), BF16),
                   jax.ShapeDtypeStruct((N_HEADS // 2 * 8, s), F32), *riding.out_shape),
        in_specs=[pl.BlockSpec((step, 2 * LANES), lambda p, i: (i, p)),
                  pl.BlockSpec((t, 2 * LANES), lambda p, i: (0, p)),
                  pl.BlockSpec((t, LANES), lambda p, i: (0, N_HEADS + p)), *riding.specs],
        out_specs=(o_spec, o_spec, pl.BlockSpec((8, step), lambda p, i: (p, i)), *riding.specs),
        scratch_shapes=riding.scratch,
        compiler_params=_params(("arbitrary", "arbitrary"), VMEM_BIG),
    )(qf, kv, kv, *riding.arrays)
    return outs[0], outs[1], outs[2], list(outs[3:])


def _attn_bwd(qf, kv, o, da, stats, cos, sgn, riding, name):
    s, t = o.shape[0], kv.shape[0]
    ATT_TQ = ATT_TQ_BWD
    nq = s // ATT_TQ
    nr = riding.n

    def body(*refs):
        q_ref, k_ref, v_ref, o_ref, do_ref, st_ref, c_ref, s_ref = refs[:8]
        dq_ref, dk_ref, dv_ref = refs[8 + nr:11 + nr]
        dk_acc, dv_acc = refs[11 + 2 * nr:13 + 2 * nr]
        p, i = pl.program_id(0), pl.program_id(1)
        state = riding.run((p == 0) & (i == 0), (p == N_HEADS // 2 - 1) & (i == nq - 1),
                           refs[8:8 + nr], refs[11 + nr:11 + 2 * nr], refs[13 + 2 * nr:])

        @pl.when(i == 0)
        def _():
            dk_acc[...] = jnp.zeros_like(dk_acc)
            dv_acc[...] = jnp.zeros_like(dv_acc)

        v = v_ref[...]
        do = do_ref[...]
        od = do * o_ref[...]
        ones = jnp.ones((8, LANES), F32)
        for hh in range(2):
            sl = slice(hh * LANES, (hh + 1) * LANES)
            q, k = q_ref[:, sl], k_ref[:, sl]
            mask = _head_mask(do.shape, hh)
            dom = jnp.where(mask, do, 0.0).astype(BF16)
            delta = lax.dot_general(ones, jnp.where(mask, od, 0.0), NT_DIMS, preferred_element_type=F32,
                                    precision=lax.Precision.HIGHEST)[0:1, :]
            st = lax.dot_general(k, q, NT_DIMS, preferred_element_type=F32)
            pt = jnp.exp2(st * EXP2_SCALE - st_ref[hh:hh + 1, :]).astype(BF16)
            dpt = lax.dot_general(v, dom, NT_DIMS, preferred_element_type=F32)
            dst = (pt.astype(F32) * (dpt - delta)).astype(BF16)
            dv_acc[...] += jnp.dot(pt, dom, preferred_element_type=F32)
            dk_acc[:, sl] += jnp.dot(dst, q, preferred_element_type=F32)
            dq = lax.dot_general(dst, k, TN_DIMS, preferred_element_type=F32) * ATTN_SCALE
            dq_ref[:, sl] = _rope(dq, c_ref[...], s_ref[...], True).astype(BF16)

        @pl.when(i == nq - 1)
        def _():
            dk_ref[...] = (dk_acc[...] * ATTN_SCALE).astype(BF16)
            dv_ref[...] = dv_acc[...].astype(BF16)

        riding.finish(state)

    o_spec = pl.BlockSpec((ATT_TQ, LANES), lambda p, i: (i, p))
    tab = pl.BlockSpec((ATT_TQ, LANES), lambda p, i: (i, 0))
    outs = pl.pallas_call(
        body, name=name, grid=(N_HEADS // 2, nq),
        out_shape=(jax.ShapeDtypeStruct((s, N_HEADS * LANES), BF16),
                   jax.ShapeDtypeStruct((t, N_HEADS * LANES), BF16),
                   jax.ShapeDtypeStruct((t, N_HEADS * V_DIM), BF16), *riding.out_shape),
        in_specs=[pl.BlockSpec((ATT_TQ, 2 * LANES), lambda p, i: (i, p)),
                  pl.BlockSpec((t, 2 * LANES), lambda p, i: (0, p)),
                  pl.BlockSpec((t, LANES), lambda p, i: (0, N_HEADS + p)),
                  o_spec, o_spec,
                  pl.BlockSpec((8, ATT_TQ), lambda p, i: (p, i)), tab, tab, *riding.specs],
        out_specs=(pl.BlockSpec((ATT_TQ, 2 * LANES), lambda p, i: (i, p)),
                   pl.BlockSpec((t, 2 * LANES), lambda p, i: (0, p)),
                   pl.BlockSpec((t, LANES), lambda p, i: (0, p)), *riding.specs),
        scratch_shapes=[pltpu.VMEM((t, 2 * LANES), F32), pltpu.VMEM((t, LANES), F32), *riding.scratch],
        compiler_params=_params(("arbitrary", "arbitrary"), VMEM_BIG),
    )(qf, kv, kv, o, da, stats, cos, sgn, *riding.arrays)
    return outs[0], outs[1], outs[2], list(outs[3:])


def _silu(x):
    return x * (1.0 / (1.0 + jnp.exp(-x)))


def _prologue(c_rows, c_ctx, w_mod, b_cols, extra_rows, name):
    d, cols = c_rows.shape[1], w_mod.shape[1]

    def body(c_ref, cctx_ref, wmod_ref, b_ref, x_ref, a_ref, modg_ref, c_all, blk, c_send, c_recv, m_send, m_recv):
        _direct_gather(c_ref, c_all, c_send, c_recv)()
        a_ref[...] = jnp.zeros_like(a_ref)
        for j in range(N_DEV):
            a_ref[j:j + 1, :] = c_all[j, 0:1, :]
        a_ref[N_DEV:N_DEV + 1, :] = cctx_ref[...]
        mod = jnp.dot(_silu(a_ref[...]), wmod_ref[...], preferred_element_type=F32,
                      precision=lax.Precision.HIGHEST) + b_ref[...]
        blk[...] = jnp.zeros_like(blk)
        for p in range(N_DEV):
            blk[p, 0:1, :] = mod[p:p + 1, :]
            blk[p, 1:2, :] = mod[N_DEV:N_DEV + 1, :]
            blk[p, 2:5, :] = x_ref[...]
        _direct_gather(blk, modg_ref, m_send, m_recv, per_peer=True)()

    vmem = pl.BlockSpec(memory_space=pltpu.VMEM)
    return pl.pallas_call(
        body, name=name,
        out_shape=(jax.ShapeDtypeStruct((16, d), F32), jax.ShapeDtypeStruct((N_DEV, 8, cols), F32)),
        in_specs=[vmem] * 5, out_specs=(vmem, vmem),
        scratch_shapes=[pltpu.VMEM((N_DEV, 8, d), F32), pltpu.VMEM((N_DEV, 8, cols), F32)]
        + [pltpu.SemaphoreType.DMA((7,)) for _ in range(4)],
        compiler_params=_params(None, VMEM_BIG),
    )(c_rows, c_ctx, w_mod, b_cols, extra_rows)


def _adaln_bwd(a_t, w, d_ex, d_ctx, d_all, name):
    def body(at_ref, w_ref, dex_ref, dctx_ref, dall_ref, gw_ref, dsil_ref, dsum_ref):
        sil_t = _silu(at_ref[...])
        dctx = dctx_ref[...]
        row = dctx[0:1, :]
        for j in range(1, N_DEV):
            row = row + dctx[j:j + 1, :]
        rowi = lax.broadcasted_iota(jnp.int32, dctx.shape, 0)
        ctx_rows = jnp.where(rowi == 0, jnp.broadcast_to(row, dctx.shape), 0.0)
        hi = lax.Precision.HIGHEST
        d_rows = jnp.concatenate([dex_ref[...], ctx_rows], axis=0)
        gw_ref[...] = jnp.dot(sil_t, d_rows, preferred_element_type=F32, precision=hi)
        dsil_ref[...] = lax.dot_general(ctx_rows, w_ref[...], NT_DIMS, preferred_element_type=F32, precision=hi)
        tot = dall_ref[0]
        for j in range(1, N_DEV):
            tot = tot + dall_ref[j]
        dsum_ref[...] = tot

    return pl.pallas_call(
        body, name=name,
        out_shape=(jax.ShapeDtypeStruct(w.shape, F32), jax.ShapeDtypeStruct((8, w.shape[0]), F32),
                   jax.ShapeDtypeStruct(d_all.shape[1:], F32)),
        compiler_params=_params(None, VMEM_BIG),
    )(a_t, w, d_ex, d_ctx, d_all)


SMALL_ROWS = 24
SMALL_MISC, SMALL_CW, SMALL_LOSS = 16, 18, 21


def _pack_small(sums1, sums2, fsums, sums1c, psums, d_cw, cols, name):
    d = D_MODEL

    def body(s1_ref, s2_ref, f_ref, s1c_ref, p_ref, cw_ref, o_ref):
        o_ref[...] = jnp.zeros_like(o_ref)

        def blocks(row0, pieces):
            for j in range(N_DEV):
                lo, hi = j * cols, (j + 1) * cols
                for k, (ref, r) in enumerate(pieces):
                    a, b = max(lo, k * d), min(hi, (k + 1) * d)
                    if a < b:
                        o_ref[row0 + j:row0 + j + 1, a - lo:b - lo] = ref[r:r + 1, a - k * d:b - k * d]

        blocks(0, [(s1_ref, 1), (s1_ref, 0), (s2_ref, 2), (s2_ref, 1), (s2_ref, 0), (f_ref, 1)])
        blocks(N_DEV, [(s1c_ref, 1), (s1c_ref, 0)])
        head = Q_RANK + KV_RANK
        o_ref[SMALL_MISC:SMALL_MISC + 1, 0:Q_RANK] = p_ref[0:1, :]
        o_ref[SMALL_MISC:SMALL_MISC + 1, Q_RANK:head] = p_ref[1:2, 0:KV_RANK]
        o_ref[SMALL_MISC:SMALL_MISC + 1, head:cols] = f_ref[0:1, 0:cols - head]
        o_ref[SMALL_MISC + 1:SMALL_MISC + 2, 0:d - (cols - head)] = f_ref[0:1, cols - head:d]
        for r in range(3):
            o_ref[SMALL_CW + r:SMALL_CW + r + 1, 0:CONV_W] = cw_ref[r:r + 1, :]
        o_ref[SMALL_LOSS:SMALL_LOSS + 1, :] = f_ref[3:4, 0:cols]

    return pl.pallas_call(body, name=name, out_shape=jax.ShapeDtypeStruct((SMALL_ROWS, cols), F32))(
        sums1, sums2, fsums, sums1c, psums, d_cw)


def _adam_math(w, g, m, v):
    nm = ADAM_B1 * m + (1.0 - ADAM_B1) * g
    nv = ADAM_B2 * v + (1.0 - ADAM_B2) * (g * g)
    m_hat = nm / (1.0 - ADAM_B1 ** ADAM_STEP)
    v_hat = nv / (1.0 - ADAM_B2 ** ADAM_STEP)
    return -ADAM_LR * (m_hat / (jnp.sqrt(v_hat) + ADAM_EPS) + ADAM_WD * w), nm, nv


def _small_update(dsum, dsil_all, g_cw, params, name):
    d = D_MODEL
    n = len(params)
    cols = dsum.shape[1]

    def body(*refs):
        dsum_ref, dsil_ref, gcw_ref = refs[:3]
        wmv = refs[3:3 + 3 * n]
        outs = refs[3 + 3 * n:]
        tot = dsil_ref[0]
        for j in range(1, N_DEV):
            tot = tot + dsil_ref[j]
        cv = wmv[0][...]
        sg = 1.0 / (1.0 + jnp.exp(-cv))
        off = Q_RANK + KV_RANK
        misc = dsum_ref[SMALL_MISC:SMALL_MISC + 1, :]
        grads = [tot[0:1, :] * (sg * (1.0 + cv * (1.0 - sg))),
                 jnp.concatenate([dsum_ref[j:j + 1, :] + dsum_ref[N_DEV + j:N_DEV + j + 1, :] for j in range(N_DEV)],
                                 axis=1),
                 misc[:, 0:Q_RANK], misc[:, Q_RANK:off],
                 jnp.concatenate([misc[:, off:cols], dsum_ref[SMALL_MISC + 1:SMALL_MISC + 2, 0:d - (cols - off)]],
                                 axis=1),
                 gcw_ref[...]]
        for p, g in enumerate(grads):
            w_ref, m_ref, v_ref = wmv[3 * p:3 * p + 3]
            at = 0 if len(w_ref.shape) == 3 else Ellipsis
            res = (g,) + _adam_math(w_ref[at], g, m_ref[at], v_ref[at])
            for q, val in enumerate(res):
                outs[4 * p + q][at] = val

    flat = [a for wmv in params for a in wmv]
    out_shape = tuple(jax.ShapeDtypeStruct(wmv[0].shape, F32) for wmv in params for _ in range(4))
    outs = pl.pallas_call(body, name=name, out_shape=out_shape)(dsum, dsil_all, g_cw, *flat)
    return [outs[4 * p:4 * p + 4] for p in range(n)]


def _adamw(w, g, m, v, name, slots=False):
    _, rows, cols = w.shape
    tr = _pick(rows, (256, 128, 64, 32, 16, 8))

    def body(w_ref, g_ref, m_ref, v_ref, *outs):
        if slots:
            gv = g_ref[0].astype(F32)
            for j in range(1, g.shape[0]):
                gv = gv + g_ref[j].astype(F32)
            outs[0][...] = gv
        else:
            gv = g_ref[...]
        d_ref, nm_ref, nv_ref = outs[-3:]
        d_ref[...], nm_ref[...], nv_ref[...] = _adam_math(w_ref[...], gv, m_ref[...], v_ref[...])

    blk = pl.BlockSpec((None, tr, cols), lambda i: (0, i, 0))
    g_spec = (pl.BlockSpec((g.shape[0], tr, cols), lambda i: (0, i, 0)) if slots
              else pl.BlockSpec((tr, cols), lambda i: (i, 0)))
    sh = jax.ShapeDtypeStruct((1, rows, cols), F32)
    n_out = 4 if slots else 3
    return pl.pallas_call(
        body, name=name, grid=(rows // tr,), out_shape=(sh,) * n_out,
        in_specs=[blk, g_spec, blk, blk], out_specs=(blk,) * n_out,
        compiler_params=_params(("parallel",), VMEM_BIG),
    )(w, g, m, v)


def _rope_tables(s, l):
    tok = np.arange(s)
    row = (tok // GRID_W).astype(np.float32)
    col = (tok % GRID_W).astype(np.float32)
    half = QK_ROPE // 2
    freqs = np.float32(ROPE_THETA) ** (-np.arange(0, half, 2, dtype=np.float32) / np.float32(half))
    dd = np.arange(QK_ROPE)
    pos = np.where((dd // half)[None, :] == 0, row[:, None], col[:, None]).astype(np.float32)
    ang = (pos * freqs[dd % (half // 2)][None, :]).astype(np.float32)
    sin = np.sin(ang).astype(np.float32)
    cos_t = np.ones((s + l, LANES), np.float32)
    sgn_t = np.zeros((s + l, LANES), np.float32)
    cos_t[:s, QK_NOPE:QK_NOPE + QK_ROPE] = np.cos(ang)
    sgn_t[:s, QK_NOPE:QK_NOPE + QK_ROPE] = np.where(((dd % half) // (half // 2))[None, :] == 0, -sin, sin)
    return jnp.asarray(cos_t), jnp.asarray(sgn_t)


def _slots_to_cols(g):
    return g.transpose(1, 0, 2).reshape(g.shape[1], N_DEV * g.shape[2])


def _cols_to_slots(w):
    return w.reshape(w.shape[0], N_DEV, w.shape[1] // N_DEV).transpose(1, 0, 2)


def _unpack_small_weights(g_in_t, g_uq, g_ukv):
    w_t = g_in_t.reshape(N_DEV * g_in_t.shape[1], D_MODEL)
    zeros = jnp.zeros((QK_NOPE, D_MODEL), BF16)
    win_head_t = jnp.concatenate([w_t[:Q_RANK + KV_RANK], zeros, w_t[Q_RANK + KV_RANK:MLA_IN],
                                  zeros[:LANES - QK_NOPE - QK_ROPE]], axis=0)
    win_conv_t = w_t[MLA_IN:].reshape(3, CONV_W // LANES, LANES, D_MODEL).transpose(1, 0, 2, 3)
    win_conv_t = win_conv_t.reshape(3 * CONV_W, D_MODEL)
    w_uq = _slots_to_cols(g_uq).reshape(Q_RANK, N_HEADS, QK_NOPE + QK_ROPE)
    wq = jnp.pad(w_uq, ((0, 0), (0, 0), (0, LANES - QK_NOPE - QK_ROPE))).reshape(Q_RANK, N_HEADS * LANES)
    w_ukv = _slots_to_cols(g_ukv).reshape(KV_RANK, N_HEADS, QK_NOPE + V_DIM)
    k_top = jnp.pad(w_ukv[:, :, :QK_NOPE], ((0, 0), (0, 0), (0, LANES - QK_NOPE))).reshape(KV_RANK, N_HEADS * LANES)
    v_top = w_ukv[:, :, QK_NOPE:].reshape(KV_RANK, N_HEADS * V_DIM)
    eye = jnp.pad(jnp.eye(QK_ROPE, dtype=BF16), ((QK_NOPE, LANES - QK_NOPE - QK_ROPE),) * 2)
    wk = jnp.concatenate([
        jnp.concatenate([k_top, v_top], axis=1),
        jnp.concatenate([jnp.tile(eye, (1, N_HEADS)), jnp.zeros((LANES, N_HEADS * V_DIM), BF16)], axis=1)], axis=0)
    return win_head_t, win_conv_t, wq, wk


def _pack_small_grads(d_head_t, d_conv_t, d_wq, d_wkk, d_wkv):
    d_conv_t = d_conv_t.reshape(CONV_W // LANES, 3, LANES, D_MODEL).transpose(1, 0, 2, 3).reshape(3 * CONV_W, D_MODEL)
    rope0 = Q_RANK + KV_RANK + QK_NOPE
    g_in_t = jnp.concatenate([d_head_t[:Q_RANK + KV_RANK], d_head_t[rope0:rope0 + QK_ROPE], d_conv_t], axis=0)
    g_in_t = g_in_t.reshape(N_DEV, -1, D_MODEL).astype(BF16)
    g_uq = d_wq.reshape(Q_RANK, N_HEADS, LANES)[:, :, :QK_NOPE + QK_ROPE].reshape(Q_RANK, -1)
    g_kn = d_wkk[:KV_RANK].reshape(KV_RANK, N_HEADS, LANES)[:, :, :QK_NOPE]
    g_v = d_wkv[:KV_RANK].reshape(KV_RANK, N_HEADS, V_DIM)
    g_ukv = jnp.concatenate([g_kn, g_v], axis=2).reshape(KV_RANK, -1)
    return [g_in_t] + [_cols_to_slots(g).astype(BF16) for g in (g_uq, g_ukv)]


def kernel(x, c, ctx, c_ctx, w_mod, b_mod, w_in, q_norm_g, w_uq, kv_norm_g, w_ukv, conv_w, w_out, w_mlp1, w_mlp2, final_norm_g, loss_target, m_c_ctx, m_w_mod, m_b_mod, m_w_in, m_q_norm_g, m_w_uq, m_kv_norm_g, m_w_ukv, m_conv_w, m_w_out, m_w_mlp1, m_w_mlp2, m_final_norm_g, v_c_ctx, v_w_mod, v_b_mod, v_w_in, v_q_norm_g, v_w_uq, v_kv_norm_g, v_w_ukv, v_conv_w, v_w_out, v_w_mlp1, v_w_mlp2, v_final_norm_g):
    me = _my_index()
    x2d, ctx2d, tgt = x[0], ctx[0], loss_target[0]
    s, l = x2d.shape[0], ctx2d.shape[0]
    t = s + l
    d = D_MODEL
    mod_cols = w_mod.shape[2]
    cw_cols = conv_w.shape[2]

    b_cols = lax.dynamic_slice(b_mod, (0, me * mod_cols), (1, mod_cols))
    cw_blk = jnp.pad(conv_w[0], ((0, 0), (0, mod_cols - cw_cols)))
    a_rows, gathered = _prologue(jnp.pad(c, ((0, 7), (0, 0))), c_ctx[None, :], w_mod[0], b_cols, cw_blk,
                                 "prologue")
    mod_mine = gathered[:, 0, :].reshape(1, 6 * d)
    mod_ctx = gathered[:, 1, :].reshape(1, 6 * d)
    cw_full = gathered[:, 2:5, :cw_cols].transpose(1, 0, 2).reshape(3, CONV_W)

    early = [w.astype(BF16) for w in (w_in[0].T, w_uq[0], w_ukv[0])]
    late = [w.astype(BF16) for w in (w_out[0], w_mlp1[0], w_mlp2[0])]
    h_all, (g_in, g_uq, g_ukv) = _modulate_all(x2d, ctx2d, mod_mine, mod_ctx, _RidingGather(early),
                                               "modulate1")
    win_head, win_conv, wq, wk = _unpack_small_weights(g_in, g_uq, g_ukv)
    wk_k, wk_v = wk[:, :N_HEADS * LANES], wk[:, N_HEADS * LANES:]
    cos, sgn = _rope_tables(s, l)

    tm_t = _pick(t, (1088, 768, 256))
    tk_t = _pick(t, (2176, 768, 256))
    z_head, cq, kv_in, qf, kv = _head_fwd(h_all, win_head, wq, wk, q_norm_g, kv_norm_g, cos, sgn, tm_t, "head_fwd")
    z_conv = _matmul(h_all, win_conv, mode="nt", name="in_proj_conv", m=s, tm=1024, tn=1536, tk=1024)
    attn, a_cat, stats, (g_out, w1, g_w2) = _attn_fwd(qf, kv, s, _RidingGather(late), "attn_fwd")
    wo = g_out.reshape(d, d)
    w2 = g_w2.reshape(D_FF, d)
    a_cat = _conv_fwd(z_conv, cw_full, a_cat, "conv_fwd")
    (o, x1, h2), _ = _matmul_rows(a_cat, wo, _epi_resid_modulate, mode="nn", name="out_proj", tm=1024, tk=1024,
                                  rows=[x2d], vecs=[(mod_mine, 2), (mod_mine, 3), (mod_mine, 4)],
                                  out_dtypes=[F32, F32, BF16])
    u1, act = _matmul(h2, w1, mode="nn", name="mlp_up", tm=4096, tk=1024, epilogue="relu2", slots="b_cols")
    (dx2, dm, fsums), _ = _matmul_rows(act, w2, _epi_final, mode="nn", name="mlp_down", tm=512, tk=4096,
                                       rows=[x1, tgt], vecs=[(mod_mine, 5), (final_norm_g[None, :], 0)],
                                       out_dtypes=[F32, BF16], sums=True)

    d_w2 = _matmul(act, dm, mode="tn", name="d_w_mlp2", out_dtype=BF16, tm=1024, tn=1024, tk=4096)
    du1 = _matmul(dm, w2, mode="nt", name="d_act", out_dtype=BF16, tm=2048, tn=1024, tk=1024,
                  epilogue="drelu2", extra=(u1,))
    d_w1 = _matmul(h2, du1, mode="tn", name="d_w_mlp1", out_dtype=BF16, tm=1024, tk=4096, slots="out")
    (dx1, do, sums2), _ = _matmul_rows(du1, w1, _epi_modulate2_bwd, mode="nt", name="d_h2", tm=512, tk=4096,
                                       slots="b_contract", rows=[x1, dx2, o], vecs=[(mod_mine, 4), (mod_mine, 2)],
                                       out_dtypes=[F32, BF16], sums=True)
    d_wo = _matmul(a_cat, do, mode="tn", name="d_w_out", out_dtype=BF16, tm=1024, tn=1024, tk=2048)
    da = _matmul(do, wo, mode="nt", name="d_a", tm=1024, tn=1024, tk=1024)
    dz_conv, d_cw = _conv_bwd(z_conv, cw_full, da, "conv_bwd")
    ready = [d_wo.reshape(N_DEV, d // N_DEV, d), d_w1, d_w2.reshape(N_DEV, D_FF // N_DEV, d)]
    dq, dk, dv, rode = _attn_bwd(qf, kv, attn, da, stats, cos, sgn, _Riding(ready), "attn_bwd")
    d_wq = _matmul(cq, dq, mode="tn", name="d_w_uq", k=s, tm=256, tn=1024, tk=4096)
    d_wkk = _matmul(kv_in, dk, mode="tn", name="d_w_ukv_k", tm=256, tn=1024, tk=tk_t)
    d_wkv = _matmul(kv_in, dv, mode="tn", name="d_w_ukv_v", tm=256, tn=512, tk=tk_t)
    dz_head, dh_head, psums = _head_bwd(dq, dk, dv, z_head, wq, wk_k, wk_v, win_head, q_norm_g, kv_norm_g, cos, sgn, s,
                                        "head_bwd")
    d_head = _matmul(dz_head, h_all, mode="tn", name="d_w_in_head", tm=512, tn=1024, tk=tk_t)
    d_conv = _matmul(dz_conv, h_all, mode="tn", name="d_w_in_conv", k=s, tm=1536, tn=1024, tk=2048)
    send = _pack_small_grads(d_head, d_conv, d_wq, d_wkk, d_wkv)
    (grad_x, sums1), got = _matmul_rows(dz_conv, win_conv, _epi_modulate1_bwd, mode="nn", name="d_h1", tm=max(s // 8, ROW_TILE),
                                        tk=win_conv.shape[0], rows=[dh_head, x2d, dx1], vecs=[(mod_mine, 1)],
                                        out_dtypes=[F32], sums=True, riding=_RidingReduce(send))
    sums1c = _modulate_sums(dh_head, s // ROW_TILE, ctx2d)

    small = _pack_small(sums1, sums2, fsums, sums1c, psums, d_cw, mod_cols, "pack_small")
    (d_all,) = _all_gather([small], "gather_small_grads", True)
    d_ex = lax.dynamic_index_in_dim(d_all, me, axis=1, keepdims=False)
    d_ctx = lax.dynamic_index_in_dim(d_all, N_DEV + me, axis=1, keepdims=False)
    g_w_mod, dsil, dsum = _adaln_bwd(a_rows.T, w_mod[0], d_ex, d_ctx, d_all, "adaln_bwd")
    (dsil_all,) = _all_gather([dsil], "gather_d_cctx", True)
    loss = dsum[SMALL_LOSS, 0]
    g_cw = lax.dynamic_slice(dsum, (SMALL_CW, me * cw_cols), (3, cw_cols))

    slots = dict(zip(["w_in", "w_uq", "w_ukv"], got))
    slots.update(zip(["w_out", "w_mlp1", "w_mlp2"], rode))

    grads = {}
    weights = {"c_ctx": c_ctx, "w_mod": w_mod, "b_mod": b_mod, "w_in": w_in, "q_norm_g": q_norm_g, "w_uq": w_uq,
               "kv_norm_g": kv_norm_g, "w_ukv": w_ukv, "conv_w": conv_w, "w_out": w_out, "w_mlp1": w_mlp1,
               "w_mlp2": w_mlp2, "final_norm_g": final_norm_g}
    m_in = {"c_ctx": m_c_ctx, "w_mod": m_w_mod, "b_mod": m_b_mod, "w_in": m_w_in, "q_norm_g": m_q_norm_g,
            "w_uq": m_w_uq, "kv_norm_g": m_kv_norm_g, "w_ukv": m_w_ukv, "conv_w": m_conv_w, "w_out": m_w_out,
            "w_mlp1": m_w_mlp1, "w_mlp2": m_w_mlp2, "final_norm_g": m_final_norm_g}
    v_in = {"c_ctx": v_c_ctx, "w_mod": v_w_mod, "b_mod": v_b_mod, "w_in": v_w_in, "q_norm_g": v_q_norm_g,
            "w_uq": v_w_uq, "kv_norm_g": v_kv_norm_g, "w_ukv": v_w_ukv, "conv_w": v_conv_w, "w_out": v_w_out,
            "w_mlp1": v_w_mlp1, "w_mlp2": v_w_mlp2, "final_norm_g": v_final_norm_g}
    names = list(weights)
    small_names = ["c_ctx", "b_mod", "q_norm_g", "kv_norm_g", "final_norm_g", "conv_w"]
    delta, new_m, new_v = {}, {}, {}

    def as_rows(a):
        return a[None, :] if a.ndim == 1 else a

    small_out = _small_update(dsum, dsil_all, g_cw, [[as_rows(src[n]) for src in (weights, m_in, v_in)]
                                                      for n in small_names], "small_update")
    for n, outs in zip(small_names, small_out):
        grads[n], delta[n], new_m[n], new_v[n] = [a.reshape(weights[n].shape) for a in outs]
    for n in names:
        if n in small_names:
            continue
        if n == "w_in":
            wmv = [jnp.swapaxes(src[n], 1, 2) for src in (weights, m_in, v_in)]
            outs = _adamw(wmv[0], slots[n], wmv[1], wmv[2], "adamw_" + n, slots=True)
            grads[n], delta[n], new_m[n], new_v[n] = [jnp.swapaxes(a, 1, 2) for a in outs]
        elif n in slots:
            grads[n], delta[n], new_m[n], new_v[n] = _adamw(weights[n], slots[n], m_in[n], v_in[n], "adamw_" + n,
                                                            slots=True)
        else:
            delta[n], new_m[n], new_v[n] = _adamw(weights[n], g_w_mod, m_in[n], v_in[n], "adamw_" + n)
            grads[n] = g_w_mod[None]

    return (loss, grad_x[None], *[grads[n] for n in names], *[delta[n] for n in names],
            *[new_m[n] for n in names], *[new_v[n] for n in names])
```

```python
import math

import jax
import jax.numpy as jnp
import numpy as np
from jax import lax
from jax.experimental import pallas as pl
from jax.experimental.pallas import tpu as pltpu

F32 = jnp.float32
BF16 = jnp.bfloat16

D_MODEL = 1024
GRID_W = 64
N_HEADS = 8
QK_NOPE = 64
QK_ROPE = 32
V_DIM = 64
Q_RANK = 256
KV_RANK = 128
MLA_IN = Q_RANK + KV_RANK + QK_ROPE
CONV_W = 512
HEAD_COLS = 512
D_FF = 4096
ROPE_THETA = 10000.0
EPS = 1e-6
ATTN_SCALE = 1.0 / math.sqrt(QK_NOPE + QK_ROPE)
LOG2_E = 1.0 / math.log(2.0)
EXP2_SCALE = ATTN_SCALE * LOG2_E
N_DEV = 8
LANES = 128

ADAM_LR, ADAM_B1, ADAM_B2, ADAM_EPS, ADAM_WD, ADAM_STEP = 0.001, 0.9, 0.999, 1e-08, 0.01, 10

ROW_TILE = 256
VMEM_BIG = 60 * 1024 * 1024


def _params(sem=None, vmem=None):
    return pltpu.CompilerParams(dimension_semantics=sem, vmem_limit_bytes=vmem)


def _pick(n, prefs):
    for p in prefs:
        if n % p == 0:
            return p
    return n


def _my_index():
    return 4 * lax.axis_index("x") + 2 * lax.axis_index("y") + lax.axis_index("c")


def _two_level_gather(x_refs, out_refs, send_sems, recv_sems, local_sems):
    n = len(x_refs)
    x, y, c = lax.axis_index("x"), lax.axis_index("y"), lax.axis_index("c")
    me, sibling = (x, y, c), (x, y, 1 - c)
    chips = [(1 - x, y), (x, 1 - y), (1 - x, 1 - y)]

    def slot(a, px, py, pc):
        return out_refs[a].at[4 * px + 2 * py + pc]

    def copy(a, k, block, to, src=None):
        return pltpu.make_async_remote_copy(
            src_ref=slot(a, *block) if src is None else src, dst_ref=slot(a, *block),
            send_sem=send_sems.at[7 * a + k], recv_sem=recv_sems.at[7 * a + k],
            device_id=to, device_id_type=pl.DeviceIdType.MESH)

    mine = [pltpu.make_async_copy(x_refs[a], slot(a, *me), local_sems.at[a]) for a in range(n)]
    first = [cp for a in range(n) for cp in
             [copy(a, 0, me, sibling, src=x_refs[a])]
             + [copy(a, 1 + j, me, (*chip, c), src=x_refs[a]) for j, chip in enumerate(chips)]]
    passed = [[copy(a, 4 + j, (*chip, c), sibling) for j, chip in enumerate(chips)] for a in range(n)]

    def start():
        for cp in mine + first:
            cp.start()

    def forward():
        for a in range(n):
            for j, chip in enumerate(chips):
                copy(a, 1 + j, (*chip, c), me).wait_recv()
                passed[a][j].start()

    def finish():
        for a in range(n):
            copy(a, 0, sibling, me).wait_recv()
            for j, chip in enumerate(chips):
                copy(a, 4 + j, (*chip, 1 - c), me).wait_recv()
        for cp in first + [cp for per_array in passed for cp in per_array]:
            cp.wait_send()
        for cp in mine:
            cp.wait()

    return start, forward, finish


def _direct_gather(src_ref, dst_ref, send_sems, recv_sems, per_peer=False):
    x, y, c = lax.axis_index("x"), lax.axis_index("y"), lax.axis_index("c")
    me = 4 * x + 2 * y + c
    dst_ref[me] = src_ref[me] if per_peer else src_ref[...]
    sends, landings = [], []
    for k in range(1, N_DEV):
        peer = (1 - x if k & 4 else x, 1 - y if k & 2 else y, 1 - c if k & 1 else c)
        pid = 4 * peer[0] + 2 * peer[1] + peer[2]
        for dst, out in ((me, sends), (pid, landings)):
            out.append(pltpu.make_async_remote_copy(
                src_ref=src_ref.at[pid] if per_peer else src_ref, dst_ref=dst_ref.at[dst],
                send_sem=send_sems.at[k - 1], recv_sem=recv_sems.at[k - 1],
                device_id=peer, device_id_type=pl.DeviceIdType.MESH))
    for cp in sends:
        cp.start()

    def finish():
        for cp in landings:
            cp.wait_recv()
        for cp in sends:
            cp.wait_send()

    return finish


def _all_gather(arrays, name, in_vmem):
    space = pltpu.VMEM if in_vmem else pl.ANY
    n = len(arrays)

    def body(*refs):
        for phase in _two_level_gather(refs[:n], refs[n:2 * n], *refs[2 * n:]):
            phase()

    outs = pl.pallas_call(
        body, name=name,
        out_shape=tuple(jax.ShapeDtypeStruct((N_DEV,) + a.shape, a.dtype) for a in arrays),
        in_specs=[pl.BlockSpec(memory_space=space)] * n,
        out_specs=tuple(pl.BlockSpec(memory_space=space) for _ in arrays),
        scratch_shapes=[pltpu.SemaphoreType.DMA((7 * n,)), pltpu.SemaphoreType.DMA((7 * n,)),
                        pltpu.SemaphoreType.DMA((n,))],
    )(*arrays)
    return list(outs)


class _Riding:
    def __init__(self, arrays=()):
        self.arrays, self.n = list(arrays), len(arrays)
        self.out_shape = [jax.ShapeDtypeStruct(a.shape, a.dtype) for a in self.arrays]
        self.specs = [pl.BlockSpec(memory_space=pl.ANY)] * self.n
        self.scratch = [pltpu.SemaphoreType.DMA((7 * self.n,)), pltpu.SemaphoreType.DMA((7 * self.n,)),
                        pltpu.SemaphoreType.DMA((self.n,))]

    def copies(self, x_refs, y_refs, send_sems, recv_sems, local_sems):
        x, y, c = lax.axis_index("x"), lax.axis_index("y"), lax.axis_index("c")
        me = 4 * x + 2 * y + c
        local, sends, landings = [], [], []
        for a in range(self.n):
            local.append(pltpu.make_async_copy(x_refs[a].at[me], y_refs[a].at[me], local_sems.at[a]))
            for k in range(1, N_DEV):
                peer = (1 - x if k & 4 else x, 1 - y if k & 2 else y, 1 - c if k & 1 else c)
                pid = 4 * peer[0] + 2 * peer[1] + peer[2]
                for dst, out in ((me, sends), (pid, landings)):
                    out.append(pltpu.make_async_remote_copy(
                        src_ref=x_refs[a].at[pid], dst_ref=y_refs[a].at[dst],
                        send_sem=send_sems.at[7 * a + k - 1], recv_sem=recv_sems.at[7 * a + k - 1],
                        device_id=peer, device_id_type=pl.DeviceIdType.MESH))
        return local, sends, landings

    def run(self, first, last, x_refs, y_refs, sems, middle=None):
        if self.n == 0:
            return None
        local, sends, landings = self.copies(x_refs, y_refs, *sems)

        @pl.when(first)
        def _():
            for cp in local + sends:
                cp.start()

        return local, sends, landings, last

    @staticmethod
    def finish(state):
        if state is None:
            return
        local, sends, landings, last = state

        @pl.when(last)
        def _():
            for cp in landings:
                cp.wait_recv()
            for cp in sends:
                cp.wait_send()
            for cp in local:
                cp.wait()


class _RidingGather:
    def __init__(self, arrays):
        self.arrays, self.n = list(arrays), len(arrays)
        self.out_shape = [jax.ShapeDtypeStruct((N_DEV,) + a.shape, a.dtype) for a in self.arrays]
        self.specs = [pl.BlockSpec(memory_space=pl.ANY)] * self.n
        self.scratch = [pltpu.SemaphoreType.DMA((7 * self.n,)), pltpu.SemaphoreType.DMA((7 * self.n,)),
                        pltpu.SemaphoreType.DMA((self.n,))]

    def run(self, first, last, x_refs, y_refs, sems, middle):
        start, forward, finish = _two_level_gather(x_refs, y_refs, *sems)
        pl.when(first)(start)
        pl.when(middle)(forward)
        return finish, last

    @staticmethod
    def finish(state):
        finish, last = state
        pl.when(last)(finish)


class _RidingReduce:
    def __init__(self, arrays):
        self.arrays, self.n = list(arrays), len(arrays)
        self.out_shape = [jax.ShapeDtypeStruct((4,) + a.shape[1:], a.dtype) for a in self.arrays]
        self.specs = [pl.BlockSpec(memory_space=pl.ANY)] * self.n
        self.scratch = [pltpu.VMEM((4,) + a.shape[1:], a.dtype) for a in self.arrays for _ in range(3)]
        self.scratch += [pltpu.SemaphoreType.DMA((self.n,)) for _ in range(6)]

    def run(self, first, last, x_refs, y_refs, scratch, middle):
        n = self.n
        own, sib, tot = scratch[0:3 * n:3], scratch[1:3 * n:3], scratch[2:3 * n:3]
        d2d_send, d2d_recv, local_in, ici_send, ici_recv, local_out = scratch[3 * n:]
        x, y, c = lax.axis_index("x"), lax.axis_index("y"), lax.axis_index("c")
        my_chip = 2 * x + y
        sibling = (x, y, 1 - c)
        others = [(1 - x, y), (x, 1 - y), (1 - x, 1 - y)]

        def to_sibling(a, j=None):
            src = x_refs[a].at[pl.ds(0, 4)] if j is None else x_refs[a].at[2 * j + 1 - c]
            dst = sib[a] if j is None else sib[a].at[j]
            return pltpu.make_async_remote_copy(src_ref=src, dst_ref=dst, send_sem=d2d_send.at[a],
                                                recv_sem=d2d_recv.at[a], device_id=sibling,
                                                device_id_type=pl.DeviceIdType.MESH)

        def mine_in(a, j=None):
            src = x_refs[a].at[pl.ds(0, 4)] if j is None else x_refs[a].at[2 * j + c]
            return pltpu.make_async_copy(src, own[a] if j is None else own[a].at[j], local_in.at[a])

        def to_chip(a, chip=None):
            if chip is None:
                src, dst, peer = tot[a].at[pl.ds(0, 3)], y_refs[a].at[pl.ds(0, 3)], sibling
            else:
                src, dst, peer = tot[a].at[2 * chip[0] + chip[1]], y_refs[a].at[my_chip], (*chip, c)
            return pltpu.make_async_remote_copy(src_ref=src, dst_ref=dst, send_sem=ici_send.at[a],
                                                recv_sem=ici_recv.at[a], device_id=peer,
                                                device_id_type=pl.DeviceIdType.MESH)

        def mine_out(a):
            return pltpu.make_async_copy(tot[a].at[my_chip], y_refs[a].at[my_chip], local_out.at[a])

        @pl.when(first)
        def _():
            for a in range(n):
                for j in range(4):
                    to_sibling(a, j).start()
                    mine_in(a, j).start()

        @pl.when(middle)
        def _():
            for a in range(n):
                to_sibling(a).wait_recv()
                to_sibling(a).wait_send()
                mine_in(a).wait()
                tot[a][...] = (own[a][...].astype(F32) + sib[a][...].astype(F32)).astype(tot[a].dtype)
                for chip in others:
                    to_chip(a, chip).start()
                mine_out(a).start()

        def finish():
            @pl.when(last)
            def _():
                for a in range(n):
                    to_chip(a).wait_recv()
                    to_chip(a).wait_send()
                    mine_out(a).wait()

        return finish

    @staticmethod
    def finish(state):
        state()


_DIMS ={"nn": (((1,), (0,)), ((), ())), "nt": (((1,), (1,)), ((), ())), "tn": (((0,), (0,)), ((), ()))}
NT_DIMS = _DIMS["nt"]
TN_DIMS = _DIMS["tn"]


def _swap8(x):
    lane = lax.broadcasted_iota(jnp.int32, x.shape, 1)
    return jnp.where((lane & 15) < 8, pltpu.roll(x, LANES - 8, 1), pltpu.roll(x, 8, 1))


def _rope(x, cos, sgn, bwd):
    return x * cos + (_swap8(x * sgn) if bwd else _swap8(x) * sgn)


def _matmul(a, b, *, mode, name, out_dtype=F32, tm=512, tn=512, tk=512, m=None, k=None,
            epilogue=None, extra=(), slots=None):
    if mode == "nn":
        m = a.shape[0] if m is None else m
        k = a.shape[1]
        n = N_DEV * b.shape[2] if slots == "b_cols" else b.shape[1]
    elif mode == "nt":
        m = a.shape[0] if m is None else m
        k = a.shape[1]
        n = b.shape[0]
    else:
        k = a.shape[0] if k is None else k
        m, n = a.shape[1], b.shape[1]
    tm, tn, tk = min(tm, m), min(tn, n), min(tk, k)
    if slots == "b_cols":
        tn = b.shape[2]
    if slots == "out":
        tn = n // N_DEV
    assert m % tm == 0 and n % tn == 0 and k % tk == 0, (name, m, n, k, tm, tn, tk)
    nk = k // tk
    dims = _DIMS[mode]
    a_spec = (pl.BlockSpec((tk, tm), lambda i, j, kk: (kk, i)) if mode == "tn"
              else pl.BlockSpec((tm, tk), lambda i, j, kk: (i, kk)))
    if slots == "b_cols":
        b_spec = pl.BlockSpec((None, tk, tn), lambda i, j, kk: (j, kk, 0))
    elif mode == "nt":
        b_spec = pl.BlockSpec((tn, tk), lambda i, j, kk: (j, kk))
    else:
        b_spec = pl.BlockSpec((tk, tn), lambda i, j, kk: (kk, j))
    tile = pl.BlockSpec((tm, tn), lambda i, j, kk: (i, j))
    if slots == "out":
        o_spec = pl.BlockSpec((None, tm, tn), lambda i, j, kk: (j, i, 0))
        o_shape = (N_DEV, m, tn)
    else:
        o_spec, o_shape = tile, (m, n)
    in_specs, args = [a_spec, b_spec], [a, b]
    if epilogue == "drelu2":
        in_specs.append(tile)
    args += list(extra)
    if epilogue == "relu2":
        out_shape = (jax.ShapeDtypeStruct(o_shape, BF16), jax.ShapeDtypeStruct(o_shape, BF16))
        out_specs = (o_spec, o_spec)
    else:
        out_shape = jax.ShapeDtypeStruct(o_shape, out_dtype)
        out_specs = o_spec
    n_in = len(args)
    n_out = 2 if epilogue == "relu2" else 1

    def body(*refs):
        a_ref, b_ref = refs[0], refs[1]
        outs = refs[n_in:n_in + n_out]
        part = lax.dot_general(a_ref[...], b_ref[...], dims, preferred_element_type=F32)

        def finish(acc):
            if epilogue == "relu2":
                outs[0][...] = acc.astype(BF16)
                r = jnp.maximum(acc, 0.0)
                outs[1][...] = (r * r).astype(BF16)
            elif epilogue == "drelu2":
                u = refs[2][...].astype(F32)
                outs[0][...] = (acc * (2.0 * jnp.maximum(u, 0.0))).astype(out_dtype)
            else:
                outs[0][...] = acc.astype(out_dtype)

        if nk == 1:
            finish(part)
        else:
            acc_ref = refs[n_in + n_out]
            kk = pl.program_id(2)

            @pl.when(kk == 0)
            def _():
                acc_ref[...] = part

            @pl.when(kk > 0)
            def _():
                acc_ref[...] += part

            @pl.when(kk == nk - 1)
            def _():
                finish(acc_ref[...])

    return pl.pallas_call(
        body, name=name, grid=(m // tm, n // tn, nk),
        out_shape=out_shape, in_specs=in_specs, out_specs=out_specs,
        scratch_shapes=[pltpu.VMEM((tm, tn), F32)] if nk > 1 else [],
        compiler_params=_params(("parallel", "parallel", "arbitrary"), VMEM_BIG),
    )(*args)


def _rstd(x):
    return lax.rsqrt(jnp.mean(x * x, axis=1, keepdims=True) + EPS)


def _norm_bwd(dxn, xn, r):
    return r * (dxn - xn * jnp.mean(dxn * xn, axis=1, keepdims=True))


def _vec(col):
    return pl.BlockSpec((1, D_MODEL), lambda i: (0, col))


def _matmul_rows(a, b, epi, *, mode, name, tm, tk, rows=(), vecs=(), out_dtypes=(), sums=False, slots=None,
                 riding=None):
    m, k = a.shape
    n = D_MODEL
    tm, tk = min(tm, m), min(tk, k)
    riding = riding or _Riding()
    group = 1
    if slots == "b_contract":
        group = max(1, tk // b.shape[2])
        tk = group * b.shape[2]
        b_spec = pl.BlockSpec((group, n, tk // group), lambda i, kk: (kk, 0, 0))
    elif mode == "nt":
        b_spec = pl.BlockSpec((n, tk), lambda i, kk: (0, kk))
    else:
        b_spec = pl.BlockSpec((tk, n), lambda i, kk: (kk, 0))
    assert m % tm == 0 and k % tk == 0, (name, m, k, tm, tk)
    ni, nk = m // tm, k // tk
    assert ni >= 2 or not isinstance(riding, _RidingReduce), "the two-level exchange needs a middle grid step"
    dims = _DIMS[mode]
    tile = pl.BlockSpec((tm, n), lambda i, kk: (i, 0))
    in_specs = [pl.BlockSpec((tm, tk), lambda i, kk: (i, kk)), b_spec] + [tile] * len(rows)
    in_specs += [pl.BlockSpec((1, n), lambda i, kk, col=col: (0, col)) for _, col in vecs]
    args = [a, b, *rows, *[v for v, _ in vecs]]
    out_shape = [jax.ShapeDtypeStruct((m, n), dt) for dt in out_dtypes]
    out_specs = [tile] * len(out_dtypes)
    if sums:
        out_shape.append(jax.ShapeDtypeStruct((8, n), F32))
        out_specs.append(pl.BlockSpec((8, n), lambda i, kk: (0, 0)))
    n_rows, n_vecs, n_outs, nr = len(rows), len(vecs), len(out_dtypes), riding.n
    n_in = 2 + n_rows + n_vecs

    def body(*refs):
        a_ref, b_ref = refs[0], refs[1]
        row_refs = refs[2:2 + n_rows]
        vec_refs = refs[2 + n_rows:n_in]
        x_refs = refs[n_in:n_in + nr]
        out_refs = refs[n_in + nr:n_in + nr + n_outs]
        pos = n_in + nr + n_outs
        sums_ref = refs[pos] if sums else None
        pos += 1 if sums else 0
        y_refs = refs[pos:pos + nr]
        pos += nr
        acc_ref = refs[pos] if nk > 1 else None
        sem_refs = refs[pos + (1 if nk > 1 else 0):]
        i, kk = pl.program_id(0), pl.program_id(1)
        state = riding.run((i == 0) & (kk == 0), (i == ni - 1) & (kk == nk - 1), x_refs, y_refs, sem_refs,
                           middle=(i == 1) & (kk == 0))
        if slots == "b_contract":
            c = tk // group
            part = lax.dot_general(a_ref[:, 0:c], b_ref[0], dims, preferred_element_type=F32)
            for u in range(1, group):
                part = part + lax.dot_general(a_ref[:, u * c:(u + 1) * c], b_ref[u], dims, preferred_element_type=F32)
        else:
            part = lax.dot_general(a_ref[...], b_ref[...], dims, preferred_element_type=F32)

        def finish(acc):
            nsub = tm // ROW_TILE
            for r in range(nsub):
                blk = pl.ds(r * ROW_TILE, ROW_TILE)
                epi(acc[r * ROW_TILE:(r + 1) * ROW_TILE], [ref.at[blk] for ref in row_refs], vec_refs,
                    [ref.at[blk] for ref in out_refs], sums_ref,
                    (i == 0) if r == 0 else None, (i == ni - 1) if r == nsub - 1 else None)

        if nk == 1:
            finish(part)
        else:
            @pl.when(kk == 0)
            def _():
                acc_ref[...] = part

            @pl.when(kk > 0)
            def _():
                acc_ref[...] += part

            @pl.when(kk == nk - 1)
            def _():
                finish(acc_ref)

        riding.finish(state)

    outs = pl.pallas_call(
        body, name=name, grid=(ni, nk),
        out_shape=(*out_shape, *riding.out_shape),
        in_specs=[*in_specs, *riding.specs], out_specs=(*out_specs, *riding.specs),
        scratch_shapes=([pltpu.VMEM((tm, n), F32)] if nk > 1 else []) + (riding.scratch if nr else []),
        compiler_params=_params(("arbitrary", "arbitrary"), VMEM_BIG),
    )(*args, *riding.arrays)
    n_own = len(out_shape)
    return list(outs[:n_own]), list(outs[n_own:])


def _zero_sums_at_start(sums_ref, first):
    if first is not None:
        @pl.when(first)
        def _():
            sums_ref[...] = jnp.zeros_like(sums_ref)


def _epi_resid_modulate(acc, rows, vecs, outs, sums_ref, first, last):
    (x_ref,), (g_ref, sh_ref, sc_ref) = rows, vecs
    x1 = x_ref[...] + g_ref[...] * acc
    outs[0][...] = acc
    outs[1][...] = x1
    outs[2][...] = (x1 * _rstd(x1) * (1.0 + sc_ref[...]) + sh_ref[...]).astype(BF16)


def _epi_final(acc, rows, vecs, outs, sums_ref, first, last):
    (x1_ref, t_ref), (g_ref, gf_ref) = rows, vecs
    d = acc.shape[1]
    x2 = x1_ref[...] + g_ref[...] * acc
    r = _rstd(x2)
    xn = x2 * r
    err = xn * gf_ref[...] - t_ref[...]
    dy = err * (1.0 / d)
    dx2 = _norm_bwd(dy * gf_ref[...], xn, r)
    outs[0][...] = dx2
    outs[1][...] = (dx2 * g_ref[...]).astype(BF16)
    _zero_sums_at_start(sums_ref, first)
    sums_ref[0:1, :] += jnp.sum(dy * xn, axis=0, keepdims=True)
    sums_ref[1:2, :] += jnp.sum(dx2 * acc, axis=0, keepdims=True)
    sums_ref[2:3, :] += jnp.sum(err * err, axis=0, keepdims=True)

    if last is not None:
        @pl.when(last)
        def _():
            tot = jnp.sum(sums_ref[2:3, :], axis=1, keepdims=True) * (0.5 / d)
            sums_ref[3:4, :] = jnp.broadcast_to(tot, (1, d))


def _epi_modulate2_bwd(acc, rows, vecs, outs, sums_ref, first, last):
    (x_ref, dres_ref, o_ref), (sc_ref, g_ref) = rows, vecs
    x = x_ref[...]
    r = _rstd(x)
    xn = x * r
    dx = dres_ref[...] + _norm_bwd(acc * (1.0 + sc_ref[...]), xn, r)
    outs[0][...] = dx
    outs[1][...] = (dx * g_ref[...]).astype(BF16)
    _zero_sums_at_start(sums_ref, first)
    sums_ref[0:1, :] += jnp.sum(acc * xn, axis=0, keepdims=True)
    sums_ref[1:2, :] += jnp.sum(acc, axis=0, keepdims=True)
    sums_ref[2:3, :] += jnp.sum(dx * o_ref[...], axis=0, keepdims=True)


def _epi_modulate1_bwd(acc, rows, vecs, outs, sums_ref, first, last):
    (add_ref, x_ref, dres_ref), (sc_ref,) = rows, vecs
    dh = acc + add_ref[...]
    x = x_ref[...]
    r = _rstd(x)
    xn = x * r
    outs[0][...] = dres_ref[...] + _norm_bwd(dh * (1.0 + sc_ref[...]), xn, r)
    _zero_sums_at_start(sums_ref, first)
    sums_ref[0:1, :] += jnp.sum(dh * xn, axis=0, keepdims=True)
    sums_ref[1:2, :] += jnp.sum(dh, axis=0, keepdims=True)


def _modulate_all(x, ctx, mod, mod_ctx, riding, name):
    s, d = x.shape
    t = s + ctx.shape[0]
    ns = s // ROW_TILE
    nc = ctx.shape[0] // ROW_TILE
    nr = riding.n

    def body(*refs):
        x_ref, c_ref, sh_ref, sc_ref, shc_ref, scc_ref = refs[:6]
        h_ref = refs[6 + nr]
        i = pl.program_id(0)
        state = riding.run(i == 0, i == ns + nc - 1, refs[6:6 + nr], refs[7 + nr:7 + 2 * nr], refs[7 + 2 * nr:],
                           middle=i == ns + nc - 3)

        @pl.when(i < ns)
        def _():
            v = x_ref[...]
            h_ref[...] = (v * _rstd(v) * (1.0 + sc_ref[...]) + sh_ref[...]).astype(BF16)

        @pl.when(i >= ns)
        def _():
            v = c_ref[...]
            h_ref[...] = (v * _rstd(v) * (1.0 + scc_ref[...]) + shc_ref[...]).astype(BF16)

        riding.finish(state)

    outs = pl.pallas_call(
        body, name=name, grid=(ns + nc,),
        out_shape=(jax.ShapeDtypeStruct((t, d), BF16), *riding.out_shape),
        in_specs=[pl.BlockSpec((ROW_TILE, d), lambda i: (jnp.minimum(i, ns - 1), 0)),
                  pl.BlockSpec((ROW_TILE, d), lambda i: (jnp.maximum(i - ns, 0), 0)),
                  _vec(0), _vec(1), _vec(0), _vec(1), *riding.specs],
        out_specs=(pl.BlockSpec((ROW_TILE, d), lambda i: (i, 0)), *riding.specs),
        scratch_shapes=riding.scratch,
        compiler_params=_params(("arbitrary",)),
    )(x, ctx, mod, mod, mod_ctx, mod_ctx, *riding.arrays)
    return outs[0], list(outs[1:])


def _modulate_sums(dh, row_off, xsrc):
    s, d = xsrc.shape

    def body(dh_ref, x_ref, sums_ref):
        i = pl.program_id(0)
        x = x_ref[...]
        dhv = dh_ref[...]

        @pl.when(i == 0)
        def _():
            sums_ref[...] = jnp.zeros_like(sums_ref)

        sums_ref[0:1, :] += jnp.sum(dhv * (x * _rstd(x)), axis=0, keepdims=True)
        sums_ref[1:2, :] += jnp.sum(dhv, axis=0, keepdims=True)

    return pl.pallas_call(
        body, name="modulate1_ctx_bwd", grid=(s // ROW_TILE,),
        out_shape=jax.ShapeDtypeStruct((8, d), F32),
        in_specs=[pl.BlockSpec((ROW_TILE, d), lambda i: (i + row_off, 0)), pl.BlockSpec((ROW_TILE, d), lambda i: (i, 0))],
        out_specs=pl.BlockSpec((8, d), lambda i: (0, 0)),
        compiler_params=_params(("arbitrary",)),
    )(dh, xsrc)


def _head_fwd(h_all, win_head, wq, wk, q_gain, kv_gain, cos, sgn, tm, name):
    t, d = h_all.shape
    nq, nkv = wq.shape[1], wk.shape[1]

    def body(h_ref, wi_ref, wq_ref, wk_ref, qg_ref, kg_ref, c_ref, s_ref, z_ref, cq_ref, kvin_ref, qf_ref, kv_ref):
        z = lax.dot_general(h_ref[...], wi_ref[...], NT_DIMS, preferred_element_type=F32)
        z_ref[...] = z
        cos, sgn = c_ref[...], s_ref[...]
        zq = z[:, 0:Q_RANK]
        cq = (zq * _rstd(zq) * qg_ref[...]).astype(BF16)
        cq_ref[...] = cq
        zk = z[:, Q_RANK:Q_RANK + KV_RANK]
        kv_in = jnp.concatenate([(zk * _rstd(zk) * kg_ref[...]).astype(BF16),
                                 _rope(z[:, Q_RANK + KV_RANK:HEAD_COLS], cos, sgn, False).astype(BF16)], axis=1)
        kvin_ref[...] = kv_in
        q = jnp.dot(cq, wq_ref[...], preferred_element_type=F32)
        for h in range(nq // LANES):
            sl = slice(h * LANES, (h + 1) * LANES)
            qf_ref[:, sl] = _rope(q[:, sl], cos, sgn, False).astype(BF16)
        kv_ref[...] = jnp.dot(kv_in, wk_ref[...], preferred_element_type=F32).astype(BF16)

    def row(w):
        return pl.BlockSpec((tm, w), lambda i: (i, 0))

    def whole(a):
        return pl.BlockSpec(a.shape, lambda i: (0, 0))

    return pl.pallas_call(
        body, name=name, grid=(t // tm,),
        out_shape=(jax.ShapeDtypeStruct((t, HEAD_COLS), F32), jax.ShapeDtypeStruct((t, Q_RANK), BF16),
                   jax.ShapeDtypeStruct((t, KV_RANK + LANES), BF16), jax.ShapeDtypeStruct((t, nq), BF16),
                   jax.ShapeDtypeStruct((t, nkv), BF16)),
        in_specs=[row(d), whole(win_head), whole(wq), whole(wk), whole(q_gain), whole(kv_gain), row(LANES), row(LANES)],
        out_specs=(row(HEAD_COLS), row(Q_RANK), row(KV_RANK + LANES), row(nq), row(nkv)),
        compiler_params=_params(("parallel",), VMEM_BIG),
    )(h_all, win_head, wq, wk, q_gain, kv_gain, cos, sgn)


def _head_bwd(dq, dk, dv, z, wq, wk_k, wk_v, win_head, q_gain, kv_gain, cos, sgn, s, name):
    t = z.shape[0]
    ns = s // ROW_TILE

    def body(dq_ref, dk_ref, dv_ref, z_ref, wq_ref, wkk_ref, wkv_ref, wi_ref, qg_ref, kg_ref, c_ref, s_ref,
             dz_ref, dh_ref, sums_ref):
        i = pl.program_id(0)

        @pl.when(i == 0)
        def _():
            sums_ref[...] = jnp.zeros_like(sums_ref)

        @pl.when(i < ns)
        def _():
            dc = lax.dot_general(dq_ref[...], wq_ref[...], NT_DIMS, preferred_element_type=F32)
            zq = z_ref[:, 0:Q_RANK]
            r = _rstd(zq)
            zn = zq * r
            sums_ref[0:1, :] += jnp.sum(dc * zn, axis=0, keepdims=True)
            dz_ref[:, 0:Q_RANK] = _norm_bwd(dc * qg_ref[...], zn, r).astype(BF16)

        @pl.when(i >= ns)
        def _():
            dz_ref[:, 0:Q_RANK] = jnp.zeros((ROW_TILE, Q_RANK), BF16)

        dkv = (lax.dot_general(dk_ref[...], wkk_ref[...], NT_DIMS, preferred_element_type=F32)
               + lax.dot_general(dv_ref[...], wkv_ref[...], NT_DIMS, preferred_element_type=F32))
        zk = z_ref[:, Q_RANK:Q_RANK + KV_RANK]
        r = _rstd(zk)
        zn = zk * r
        dc = dkv[:, 0:KV_RANK]
        sums_ref[1:2, 0:KV_RANK] += jnp.sum(dc * zn, axis=0, keepdims=True)
        dz_ref[:, Q_RANK:Q_RANK + KV_RANK] = _norm_bwd(dc * kg_ref[...], zn, r).astype(BF16)
        dz_ref[:, Q_RANK + KV_RANK:HEAD_COLS] = _rope(dkv[:, KV_RANK:KV_RANK + LANES], c_ref[...], s_ref[...],
                                                       True).astype(BF16)
        dh_ref[...] = jnp.dot(dz_ref[...], wi_ref[...], preferred_element_type=F32)

    def row(w):
        return pl.BlockSpec((ROW_TILE, w), lambda i: (i, 0))

    def whole(a):
        return pl.BlockSpec(a.shape, lambda i: (0, 0))

    return pl.pallas_call(
        body, name=name, grid=(t // ROW_TILE,),
        out_shape=(jax.ShapeDtypeStruct((t, HEAD_COLS), BF16), jax.ShapeDtypeStruct((t, D_MODEL), F32),
                   jax.ShapeDtypeStruct((8, Q_RANK), F32)),
        in_specs=[pl.BlockSpec((ROW_TILE, dq.shape[1]), lambda i: (jnp.minimum(i, ns - 1), 0)),
                  row(dk.shape[1]), row(dv.shape[1]), row(HEAD_COLS), whole(wq), whole(wk_k), whole(wk_v),
                  whole(win_head), whole(q_gain), whole(kv_gain), row(LANES), row(LANES)],
        out_specs=(row(HEAD_COLS), row(D_MODEL), pl.BlockSpec((8, Q_RANK), lambda i: (0, 0))),
        compiler_params=_params(("arbitrary",), VMEM_BIG),
    )(dq, dk, dv, z, wq, wk_k, wk_v, win_head, q_gain, kv_gain, cos, sgn)


def _shift_rows(u, s):
    rowi = lax.broadcasted_iota(jnp.int32, u.shape, 0)
    prev = jnp.where(rowi == 0, 0.0, pltpu.roll(u, 1, 0))
    nxt = jnp.where(rowi == s - 1, 0.0, pltpu.roll(u, s - 1, 0))
    return prev, nxt


def _conv_fwd(z_conv, cw, a_cat, name):
    s = z_conv.shape[0]

    def body(z_ref, w_ref, a_in_ref, o_ref):
        del a_in_ref
        gb, gc, xv = z_ref[:, 0:LANES], z_ref[:, LANES:2 * LANES], z_ref[:, 2 * LANES:3 * LANES]
        u = gc * xv
        prev, nxt = _shift_rows(u, s)
        y = w_ref[0:1, :] * prev + w_ref[1:2, :] * u + w_ref[2:3, :] * nxt
        o_ref[...] = (gb * y).astype(BF16)

    return pl.pallas_call(
        body, name=name, grid=(CONV_W // LANES,),
        out_shape=jax.ShapeDtypeStruct(a_cat.shape, a_cat.dtype),
        in_specs=[pl.BlockSpec((s, 3 * LANES), lambda j: (0, j)), pl.BlockSpec((3, LANES), lambda j: (0, j)),
                  pl.BlockSpec(memory_space=pl.ANY)],
        out_specs=pl.BlockSpec((s, LANES), lambda j: (0, 4 + j)),
        input_output_aliases={2: 0},
        compiler_params=_params(("parallel",), VMEM_BIG),
    )(z_conv, cw, a_cat)


def _conv_bwd(z_conv, cw, da, name):
    s = z_conv.shape[0]

    def body(z_ref, w_ref, da_ref, dz_ref, dw_ref):
        gb, gc, xv = z_ref[:, 0:LANES], z_ref[:, LANES:2 * LANES], z_ref[:, 2 * LANES:3 * LANES]
        u = gc * xv
        prev, nxt = _shift_rows(u, s)
        dcv = da_ref[...]
        dz_ref[:, 0:LANES] = (dcv * (w_ref[0:1, :] * prev + w_ref[1:2, :] * u + w_ref[2:3, :] * nxt)).astype(BF16)
        dy = dcv * gb
        dw_ref[0:1, :] = jnp.sum(dy * prev, axis=0, keepdims=True)
        dw_ref[1:2, :] = jnp.sum(dy * u, axis=0, keepdims=True)
        dw_ref[2:3, :] = jnp.sum(dy * nxt, axis=0, keepdims=True)
        dyp, dyn = _shift_rows(dy, s)
        du = w_ref[0:1, :] * dyn + w_ref[1:2, :] * dy + w_ref[2:3, :] * dyp
        dz_ref[:, LANES:2 * LANES] = (du * xv).astype(BF16)
        dz_ref[:, 2 * LANES:3 * LANES] = (du * gc).astype(BF16)

    blk = pl.BlockSpec((s, 3 * LANES), lambda j: (0, j))
    cws = pl.BlockSpec((3, LANES), lambda j: (0, j))
    return pl.pallas_call(
        body, name=name, grid=(CONV_W // LANES,),
        out_shape=(jax.ShapeDtypeStruct(z_conv.shape, BF16), jax.ShapeDtypeStruct((3, CONV_W), F32)),
        in_specs=[blk, cws, pl.BlockSpec((s, LANES), lambda j: (0, 4 + j))], out_specs=(blk, cws),
        compiler_params=_params(("parallel",), VMEM_BIG),
    )(z_conv, cw, da)


ATT_TQ = 512
ATT_Q_STEP = 1024
ATT_TQ_BWD = 512


def _head_mask(shape, hh):
    lane = lax.broadcasted_iota(jnp.int32, shape, 1)
    return (lane >= hh * V_DIM) & (lane < (hh + 1) * V_DIM)


def _attn_fwd(qf, kv, s, riding, name):
    t = kv.shape[0]
    step = min(ATT_Q_STEP, s)
    nq = s // step
    nr = riding.n

    def body(*refs):
        q_ref, k_ref, v_ref = refs[:3]
        o_ref, ob_ref, st_ref = refs[3 + nr:6 + nr]
        p, i = pl.program_id(0), pl.program_id(1)
        state = riding.run((p == 0) & (i == 0), (p == N_HEADS // 2 - 1) & (i == nq - 1),
                           refs[3:3 + nr], refs[6 + nr:6 + 2 * nr], refs[6 + 2 * nr:],
                           middle=(p == N_HEADS // 2 - 2) & (i == nq // 2))
        v = v_ref[...]
        vlane = lax.broadcasted_iota(jnp.int32, v.shape, 1)
        one_lane = [(1 - hh) * V_DIM for hh in range(2)]
        vm = [jnp.where(_head_mask(v.shape, hh), v, jnp.where(vlane == one_lane[hh], 1.0, 0.0).astype(BF16))
              for hh in range(2)]

        def block(r, carry):
            rows = pl.ds(pl.multiple_of(r * ATT_TQ, ATT_TQ), ATT_TQ)
            olane = lax.broadcasted_iota(jnp.int32, (ATT_TQ, LANES), 1)
            acc = jnp.zeros((ATT_TQ, LANES), F32)
            stat = jnp.zeros((ATT_TQ, LANES), F32)
            scores = [lax.dot_general(q_ref[rows, hh * LANES:(hh + 1) * LANES], k_ref[:, hh * LANES:(hh + 1) * LANES],
                                      NT_DIMS, preferred_element_type=F32) for hh in range(2)]
            maxes = [jnp.max(sc, axis=1, keepdims=True) for sc in scores]
            exps = [jnp.exp2((sc - mx) * EXP2_SCALE).astype(BF16) for sc, mx in zip(scores, maxes)]
            for hh in range(2):
                mx = maxes[hh]
                res = jnp.dot(exps[hh], vm[hh], preferred_element_type=F32)
                den = jnp.sum(jnp.where(olane == one_lane[hh], res, 0.0), axis=1, keepdims=True)
                acc = acc + jnp.where(_head_mask(res.shape, hh), res * (1.0 / den), 0.0)
                stat = stat + jnp.where(olane == hh, mx * EXP2_SCALE + jnp.log(den) * LOG2_E, 0.0)
            o_ref[rows, :] = acc
            ob_ref[rows, :] = acc.astype(BF16)
            st_ref[:, rows] = stat.T[0:8, :]
            return carry

        lax.fori_loop(0, step // ATT_TQ, block, 0)
        riding.finish(state)

    o_spec = pl.BlockSpec((step, LANES), lambda p, i: (i, p))
    outs = pl.pallas_call(
        body, name=name, grid=(N_HEADS // 2, nq),
        out_shape=(jax.ShapeDtypeStruct((s, N_HEADS * V_DIM), F32),
                   jax.ShapeDtypeStruct((s, D_MODEL), BF16),
                   jax.ShapeDtypeStruct((N_HEADS // 2 * 8, s), F32), *riding.out_shape),
        in_specs=[pl.BlockSpec((step, 2 * LANES), lambda p, i: (i, p)),
                  pl.BlockSpec((t, 2 * LANES), lambda p, i: (0, p)),
                  pl.BlockSpec((t, LANES), lambda p, i: (0, N_HEADS + p)), *riding.specs],
        out_specs=(o_spec, o_spec, pl.BlockSpec((8, step), lambda p, i: (p, i)), *riding.specs),
        scratch_shapes=riding.scratch,
        compiler_params=_params(("arbitrary", "arbitrary"), VMEM_BIG),
    )(qf, kv, kv, *riding.arrays)
    return outs[0], outs[1], outs[2], list(outs[3:])


def _attn_bwd(qf, kv, o, da, stats, cos, sgn, riding, name):
    s, t = o.shape[0], kv.shape[0]
    ATT_TQ = ATT_TQ_BWD
    nq = s // ATT_TQ
    nr = riding.n

    def body(*refs):
        q_ref, k_ref, v_ref, o_ref, do_ref, st_ref, c_ref, s_ref = refs[:8]
        dq_ref, dk_ref, dv_ref = refs[8 + nr:11 + nr]
        dk_acc, dv_acc = refs[11 + 2 * nr:13 + 2 * nr]
        p, i = pl.program_id(0), pl.program_id(1)
        state = riding.run((p == 0) & (i == 0), (p == N_HEADS // 2 - 1) & (i == nq - 1),
                           refs[8:8 + nr], refs[11 + nr:11 + 2 * nr], refs[13 + 2 * nr:])

        @pl.when(i == 0)
        def _():
            dk_acc[...] = jnp.zeros_like(dk_acc)
            dv_acc[...] = jnp.zeros_like(dv_acc)

        v = v_ref[...]
        do = do_ref[...]
        od = do * o_ref[...]
        ones = jnp.ones((8, LANES), F32)
        for hh in range(2):
            sl = slice(hh * LANES, (hh + 1) * LANES)
            q, k = q_ref[:, sl], k_ref[:, sl]
            mask = _head_mask(do.shape, hh)
            dom = jnp.where(mask, do, 0.0).astype(BF16)
            delta = lax.dot_general(ones, jnp.where(mask, od, 0.0), NT_DIMS, preferred_element_type=F32,
                                    precision=lax.Precision.HIGHEST)[0:1, :]
            st = lax.dot_general(k, q, NT_DIMS, preferred_element_type=F32)
            pt = jnp.exp2(st * EXP2_SCALE - st_ref[hh:hh + 1, :]).astype(BF16)
            dpt = lax.dot_general(v, dom, NT_DIMS, preferred_element_type=F32)
            dst = (pt.astype(F32) * (dpt - delta)).astype(BF16)
            dv_acc[...] += jnp.dot(pt, dom, preferred_element_type=F32)
            dk_acc[:, sl] += jnp.dot(dst, q, preferred_element_type=F32)
            dq = lax.dot_general(dst, k, TN_DIMS, preferred_element_type=F32) * ATTN_SCALE
            dq_ref[:, sl] = _rope(dq, c_ref[...], s_ref[...], True).astype(BF16)

        @pl.when(i == nq - 1)
        def _():
            dk_ref[...] = (dk_acc[...] * ATTN_SCALE).astype(BF16)
            dv_ref[...] = dv_acc[...].astype(BF16)

        riding.finish(state)

    o_spec = pl.BlockSpec((ATT_TQ, LANES), lambda p, i: (i, p))
    tab = pl.BlockSpec((ATT_TQ, LANES), lambda p, i: (i, 0))
    outs = pl.pallas_call(
        body, name=name, grid=(N_HEADS // 2, nq),
        out_shape=(jax.ShapeDtypeStruct((s, N_HEADS * LANES), BF16),
                   jax.ShapeDtypeStruct((t, N_HEADS * LANES), BF16),
                   jax.ShapeDtypeStruct((t, N_HEADS * V_DIM), BF16), *riding.out_shape),
        in_specs=[pl.BlockSpec((ATT_TQ, 2 * LANES), lambda p, i: (i, p)),
                  pl.BlockSpec((t, 2 * LANES), lambda p, i: (0, p)),
                  pl.BlockSpec((t, LANES), lambda p, i: (0, N_HEADS + p)),
                  o_spec, o_spec,
                  pl.BlockSpec((8, ATT_TQ), lambda p, i: (p, i)), tab, tab, *riding.specs],
        out_specs=(pl.BlockSpec((ATT_TQ, 2 * LANES), lambda p, i: (i, p)),
                   pl.BlockSpec((t, 2 * LANES), lambda p, i: (0, p)),
                   pl.BlockSpec((t, LANES), lambda p, i: (0, p)), *riding.specs),
        scratch_shapes=[pltpu.VMEM((t, 2 * LANES), F32), pltpu.VMEM((t, LANES), F32), *riding.scratch],
        compiler_params=_params(("arbitrary", "arbitrary"), VMEM_BIG),
    )(qf, kv, kv, o, da, stats, cos, sgn, *riding.arrays)
    return outs[0], outs[1], outs[2], list(outs[3:])


def _silu(x):
    return x * (1.0 / (1.0 + jnp.exp(-x)))


def _prologue(c_rows, c_ctx, w_mod, b_cols, extra_rows, weights, name):
    d, cols = c_rows.shape[1], w_mod.shape[1]
    nw = len(weights)

    def body(c_ref, cctx_ref, wmod_ref, b_ref, x_ref, *rest):
        w_refs, (a_ref, modg_ref), wg_refs = rest[:nw], rest[nw:nw + 2], rest[nw + 2:2 * nw + 2]
        c_all, blk, c_send, c_recv, m_send, m_recv, w_send, w_recv, w_local = rest[2 * nw + 2:]
        wait_c = _direct_gather(c_ref, c_all, c_send, c_recv)
        start_weights, forward_weights, finish_weights = _two_level_gather(w_refs, wg_refs, w_send, w_recv, w_local)
        start_weights()
        wait_c()
        a_ref[...] = jnp.zeros_like(a_ref)
        for j in range(N_DEV):
            a_ref[j:j + 1, :] = c_all[j, 0:1, :]
        a_ref[N_DEV:N_DEV + 1, :] = cctx_ref[...]
        mod = jnp.dot(_silu(a_ref[...]), wmod_ref[...], preferred_element_type=F32,
                      precision=lax.Precision.HIGHEST) + b_ref[...]
        blk[...] = jnp.zeros_like(blk)
        for p in range(N_DEV):
            blk[p, 0:1, :] = mod[p:p + 1, :]
            blk[p, 1:2, :] = mod[N_DEV:N_DEV + 1, :]
            blk[p, 2:5, :] = x_ref[...]
        wait_mod = _direct_gather(blk, modg_ref, m_send, m_recv, per_peer=True)
        forward_weights()
        finish_weights()
        wait_mod()

    vmem, hbm = pl.BlockSpec(memory_space=pltpu.VMEM), pl.BlockSpec(memory_space=pl.ANY)
    outs = pl.pallas_call(
        body, name=name,
        out_shape=(jax.ShapeDtypeStruct((16, d), F32), jax.ShapeDtypeStruct((N_DEV, 8, cols), F32),
                   *[jax.ShapeDtypeStruct((N_DEV,) + w.shape, w.dtype) for w in weights]),
        in_specs=[vmem] * 5 + [hbm] * nw, out_specs=(vmem, vmem, *[hbm] * nw),
        scratch_shapes=[pltpu.VMEM((N_DEV, 8, d), F32), pltpu.VMEM((N_DEV, 8, cols), F32)]
        + [pltpu.SemaphoreType.DMA((7,)) for _ in range(4)]
        + [pltpu.SemaphoreType.DMA((7 * nw,)), pltpu.SemaphoreType.DMA((7 * nw,)), pltpu.SemaphoreType.DMA((nw,))],
        compiler_params=_params(None, VMEM_BIG),
    )(c_rows, c_ctx, w_mod, b_cols, extra_rows, *weights)
    return outs[0], outs[1], list(outs[2:])


def _adaln_bwd(a_t, w, d_ex, d_ctx, d_all, name):
    def body(at_ref, w_ref, dex_ref, dctx_ref, dall_ref, gw_ref, dsil_ref, dsum_ref):
        sil_t = _silu(at_ref[...])
        dctx = dctx_ref[...]
        row = dctx[0:1, :]
        for j in range(1, N_DEV):
            row = row + dctx[j:j + 1, :]
        rowi = lax.broadcasted_iota(jnp.int32, dctx.shape, 0)
        ctx_rows = jnp.where(rowi == 0, jnp.broadcast_to(row, dctx.shape), 0.0)
        hi = lax.Precision.HIGHEST
        d_rows = jnp.concatenate([dex_ref[...], ctx_rows], axis=0)
        gw_ref[...] = jnp.dot(sil_t, d_rows, preferred_element_type=F32, precision=hi)
        dsil_ref[...] = lax.dot_general(ctx_rows, w_ref[...], NT_DIMS, preferred_element_type=F32, precision=hi)
        tot = dall_ref[0]
        for j in range(1, N_DEV):
            tot = tot + dall_ref[j]
        dsum_ref[...] = tot

    return pl.pallas_call(
        body, name=name,
        out_shape=(jax.ShapeDtypeStruct(w.shape, F32), jax.ShapeDtypeStruct((8, w.shape[0]), F32),
                   jax.ShapeDtypeStruct(d_all.shape[1:], F32)),
        compiler_params=_params(None, VMEM_BIG),
    )(a_t, w, d_ex, d_ctx, d_all)


SMALL_ROWS = 24
SMALL_MISC, SMALL_CW, SMALL_LOSS = 16, 18, 21


def _pack_small(sums1, sums2, fsums, sums1c, psums, d_cw, cols, name):
    d = D_MODEL

    def body(s1_ref, s2_ref, f_ref, s1c_ref, p_ref, cw_ref, o_ref):
        o_ref[...] = jnp.zeros_like(o_ref)

        def blocks(row0, pieces):
            for j in range(N_DEV):
                lo, hi = j * cols, (j + 1) * cols
                for k, (ref, r) in enumerate(pieces):
                    a, b = max(lo, k * d), min(hi, (k + 1) * d)
                    if a < b:
                        o_ref[row0 + j:row0 + j + 1, a - lo:b - lo] = ref[r:r + 1, a - k * d:b - k * d]

        blocks(0, [(s1_ref, 1), (s1_ref, 0), (s2_ref, 2), (s2_ref, 1), (s2_ref, 0), (f_ref, 1)])
        blocks(N_DEV, [(s1c_ref, 1), (s1c_ref, 0)])
        head = Q_RANK + KV_RANK
        o_ref[SMALL_MISC:SMALL_MISC + 1, 0:Q_RANK] = p_ref[0:1, :]
        o_ref[SMALL_MISC:SMALL_MISC + 1, Q_RANK:head] = p_ref[1:2, 0:KV_RANK]
        o_ref[SMALL_MISC:SMALL_MISC + 1, head:cols] = f_ref[0:1, 0:cols - head]
        o_ref[SMALL_MISC + 1:SMALL_MISC + 2, 0:d - (cols - head)] = f_ref[0:1, cols - head:d]
        for r in range(3):
            o_ref[SMALL_CW + r:SMALL_CW + r + 1, 0:CONV_W] = cw_ref[r:r + 1, :]
        o_ref[SMALL_LOSS:SMALL_LOSS + 1, :] = f_ref[3:4, 0:cols]

    return pl.pallas_call(body, name=name, out_shape=jax.ShapeDtypeStruct((SMALL_ROWS, cols), F32))(
        sums1, sums2, fsums, sums1c, psums, d_cw)


def _adam_math(w, g, m, v):
    nm = ADAM_B1 * m + (1.0 - ADAM_B1) * g
    nv = ADAM_B2 * v + (1.0 - ADAM_B2) * (g * g)
    m_hat = nm / (1.0 - ADAM_B1 ** ADAM_STEP)
    v_hat = nv / (1.0 - ADAM_B2 ** ADAM_STEP)
    return -ADAM_LR * (m_hat / (jnp.sqrt(v_hat) + ADAM_EPS) + ADAM_WD * w), nm, nv


def _small_update(dsum, dsil_all, g_cw, params, name):
    d = D_MODEL
    n = len(params)
    cols = dsum.shape[1]

    def body(*refs):
        dsum_ref, dsil_ref, gcw_ref = refs[:3]
        wmv = refs[3:3 + 3 * n]
        outs = refs[3 + 3 * n:]
        tot = dsil_ref[0]
        for j in range(1, N_DEV):
            tot = tot + dsil_ref[j]
        cv = wmv[0][...]
        sg = 1.0 / (1.0 + jnp.exp(-cv))
        off = Q_RANK + KV_RANK
        misc = dsum_ref[SMALL_MISC:SMALL_MISC + 1, :]
        grads = [tot[0:1, :] * (sg * (1.0 + cv * (1.0 - sg))),
                 jnp.concatenate([dsum_ref[j:j + 1, :] + dsum_ref[N_DEV + j:N_DEV + j + 1, :] for j in range(N_DEV)],
                                 axis=1),
                 misc[:, 0:Q_RANK], misc[:, Q_RANK:off],
                 jnp.concatenate([misc[:, off:cols], dsum_ref[SMALL_MISC + 1:SMALL_MISC + 2, 0:d - (cols - off)]],
                                 axis=1),
                 gcw_ref[...]]
        for p, g in enumerate(grads):
            w_ref, m_ref, v_ref = wmv[3 * p:3 * p + 3]
            at = 0 if len(w_ref.shape) == 3 else Ellipsis
            res = (g,) + _adam_math(w_ref[at], g, m_ref[at], v_ref[at])
            for q, val in enumerate(res):
                outs[4 * p + q][at] = val

    flat = [a for wmv in params for a in wmv]
    out_shape = tuple(jax.ShapeDtypeStruct(wmv[0].shape, F32) for wmv in params for _ in range(4))
    outs = pl.pallas_call(body, name=name, out_shape=out_shape)(dsum, dsil_all, g_cw, *flat)
    return [outs[4 * p:4 * p + 4] for p in range(n)]


def _adamw(w, g, m, v, name, slots=False):
    _, rows, cols = w.shape
    tr = _pick(rows, (256, 128, 64, 32, 16, 8))

    def body(w_ref, g_ref, m_ref, v_ref, *outs):
        if slots:
            gv = g_ref[0].astype(F32)
            for j in range(1, g.shape[0]):
                gv = gv + g_ref[j].astype(F32)
            outs[0][...] = gv
        else:
            gv = g_ref[...]
        d_ref, nm_ref, nv_ref = outs[-3:]
        d_ref[...], nm_ref[...], nv_ref[...] = _adam_math(w_ref[...], gv, m_ref[...], v_ref[...])

    blk = pl.BlockSpec((None, tr, cols), lambda i: (0, i, 0))
    g_spec = (pl.BlockSpec((g.shape[0], tr, cols), lambda i: (0, i, 0)) if slots
              else pl.BlockSpec((tr, cols), lambda i: (i, 0)))
    sh = jax.ShapeDtypeStruct((1, rows, cols), F32)
    n_out = 4 if slots else 3
    return pl.pallas_call(
        body, name=name, grid=(rows // tr,), out_shape=(sh,) * n_out,
        in_specs=[blk, g_spec, blk, blk], out_specs=(blk,) * n_out,
        compiler_params=_params(("parallel",), VMEM_BIG),
    )(w, g, m, v)


def _rope_tables(s, l):
    tok = np.arange(s)
    row = (tok // GRID_W).astype(np.float32)
    col = (tok % GRID_W).astype(np.float32)
    half = QK_ROPE // 2
    freqs = np.float32(ROPE_THETA) ** (-np.arange(0, half, 2, dtype=np.float32) / np.float32(half))
    dd = np.arange(QK_ROPE)
    pos = np.where((dd // half)[None, :] == 0, row[:, None], col[:, None]).astype(np.float32)
    ang = (pos * freqs[dd % (half // 2)][None, :]).astype(np.float32)
    sin = np.sin(ang).astype(np.float32)
    cos_t = np.ones((s + l, LANES), np.float32)
    sgn_t = np.zeros((s + l, LANES), np.float32)
    cos_t[:s, QK_NOPE:QK_NOPE + QK_ROPE] = np.cos(ang)
    sgn_t[:s, QK_NOPE:QK_NOPE + QK_ROPE] = np.where(((dd % half) // (half // 2))[None, :] == 0, -sin, sin)
    return jnp.asarray(cos_t), jnp.asarray(sgn_t)


def _slots_to_cols(g):
    return g.transpose(1, 0, 2).reshape(g.shape[1], N_DEV * g.shape[2])


def _cols_to_slots(w):
    return w.reshape(w.shape[0], N_DEV, w.shape[1] // N_DEV).transpose(1, 0, 2)


def _unpack_small_weights(g_in_t, g_uq, g_ukv):
    w_t = g_in_t.reshape(N_DEV * g_in_t.shape[1], D_MODEL)
    zeros = jnp.zeros((QK_NOPE, D_MODEL), BF16)
    win_head_t = jnp.concatenate([w_t[:Q_RANK + KV_RANK], zeros, w_t[Q_RANK + KV_RANK:MLA_IN],
                                  zeros[:LANES - QK_NOPE - QK_ROPE]], axis=0)
    win_conv_t = w_t[MLA_IN:].reshape(3, CONV_W // LANES, LANES, D_MODEL).transpose(1, 0, 2, 3)
    win_conv_t = win_conv_t.reshape(3 * CONV_W, D_MODEL)
    w_uq = _slots_to_cols(g_uq).reshape(Q_RANK, N_HEADS, QK_NOPE + QK_ROPE)
    wq = jnp.pad(w_uq, ((0, 0), (0, 0), (0, LANES - QK_NOPE - QK_ROPE))).reshape(Q_RANK, N_HEADS * LANES)
    w_ukv = _slots_to_cols(g_ukv).reshape(KV_RANK, N_HEADS, QK_NOPE + V_DIM)
    k_top = jnp.pad(w_ukv[:, :, :QK_NOPE], ((0, 0), (0, 0), (0, LANES - QK_NOPE))).reshape(KV_RANK, N_HEADS * LANES)
    v_top = w_ukv[:, :, QK_NOPE:].reshape(KV_RANK, N_HEADS * V_DIM)
    eye = jnp.pad(jnp.eye(QK_ROPE, dtype=BF16), ((QK_NOPE, LANES - QK_NOPE - QK_ROPE),) * 2)
    wk = jnp.concatenate([
        jnp.concatenate([k_top, v_top], axis=1),
        jnp.concatenate([jnp.tile(eye, (1, N_HEADS)), jnp.zeros((LANES, N_HEADS * V_DIM), BF16)], axis=1)], axis=0)
    return win_head_t, win_conv_t, wq, wk


def _pack_small_grads(d_head_t, d_conv_t, d_wq, d_wkk, d_wkv):
    d_conv_t = d_conv_t.reshape(CONV_W // LANES, 3, LANES, D_MODEL).transpose(1, 0, 2, 3).reshape(3 * CONV_W, D_MODEL)
    rope0 = Q_RANK + KV_RANK + QK_NOPE
    g_in_t = jnp.concatenate([d_head_t[:Q_RANK + KV_RANK], d_head_t[rope0:rope0 + QK_ROPE], d_conv_t], axis=0)
    g_in_t = g_in_t.reshape(N_DEV, -1, D_MODEL).astype(BF16)
    g_uq = d_wq.reshape(Q_RANK, N_HEADS, LANES)[:, :, :QK_NOPE + QK_ROPE].reshape(Q_RANK, -1)
    g_kn = d_wkk[:KV_RANK].reshape(KV_RANK, N_HEADS, LANES)[:, :, :QK_NOPE]
    g_v = d_wkv[:KV_RANK].reshape(KV_RANK, N_HEADS, V_DIM)
    g_ukv = jnp.concatenate([g_kn, g_v], axis=2).reshape(KV_RANK, -1)
    return [g_in_t] + [_cols_to_slots(g).astype(BF16) for g in (g_uq, g_ukv)]


def kernel(x, c, ctx, c_ctx, w_mod, b_mod, w_in, q_norm_g, w_uq, kv_norm_g, w_ukv, conv_w, w_out, w_mlp1, w_mlp2, final_norm_g, loss_target, m_c_ctx, m_w_mod, m_b_mod, m_w_in, m_q_norm_g, m_w_uq, m_kv_norm_g, m_w_ukv, m_conv_w, m_w_out, m_w_mlp1, m_w_mlp2, m_final_norm_g, v_c_ctx, v_w_mod, v_b_mod, v_w_in, v_q_norm_g, v_w_uq, v_kv_norm_g, v_w_ukv, v_conv_w, v_w_out, v_w_mlp1, v_w_mlp2, v_final_norm_g):
    me = _my_index()
    x2d, ctx2d, tgt = x[0], ctx[0], loss_target[0]
    s, l = x2d.shape[0], ctx2d.shape[0]
    t = s + l
    d = D_MODEL
    mod_cols = w_mod.shape[2]
    cw_cols = conv_w.shape[2]

    b_cols = lax.dynamic_slice(b_mod, (0, me * mod_cols), (1, mod_cols))
    cw_blk = jnp.pad(conv_w[0], ((0, 0), (0, mod_cols - cw_cols)))
    w_in_t = w_in[0].T.astype(BF16)
    half = w_in_t.shape[0] // 2
    early = [w_in_t[:half], w_uq[0].astype(BF16), w_ukv[0].astype(BF16)]
    late = [w.astype(BF16) for w in (w_out[0], w_mlp1[0], w_mlp2[0])]
    a_rows, gathered, (g_in_a, g_uq, g_ukv) = _prologue(jnp.pad(c, ((0, 7), (0, 0))), c_ctx[None, :], w_mod[0], b_cols,
                                                        cw_blk, early, "prologue")
    mod_mine = gathered[:, 0, :].reshape(1, 6 * d)
    mod_ctx = gathered[:, 1, :].reshape(1, 6 * d)
    cw_full = gathered[:, 2:5, :cw_cols].transpose(1, 0, 2).reshape(3, CONV_W)
    h_all, (g_in_b,) = _modulate_all(x2d, ctx2d, mod_mine, mod_ctx, _RidingGather([w_in_t[half:]]),
                                     "modulate1")
    win_head, win_conv, wq, wk = _unpack_small_weights(jnp.concatenate([g_in_a, g_in_b], axis=1), g_uq, g_ukv)
    wk_k, wk_v = wk[:, :N_HEADS * LANES], wk[:, N_HEADS * LANES:]
    cos, sgn = _rope_tables(s, l)

    tm_t = _pick(t, (1088, 768, 256))
    tk_t = _pick(t, (2176, 768, 256))
    z_head, cq, kv_in, qf, kv = _head_fwd(h_all, win_head, wq, wk, q_norm_g, kv_norm_g, cos, sgn, tm_t, "head_fwd")
    z_conv = _matmul(h_all, win_conv, mode="nt", name="in_proj_conv", m=s, tm=1024, tn=1536, tk=1024)
    attn, a_cat, stats, (g_out, w1, g_w2) = _attn_fwd(qf, kv, s, _RidingGather(late), "attn_fwd")
    wo = g_out.reshape(d, d)
    w2 = g_w2.reshape(D_FF, d)
    a_cat = _conv_fwd(z_conv, cw_full, a_cat, "conv_fwd")
    (o, x1, h2), _ = _matmul_rows(a_cat, wo, _epi_resid_modulate, mode="nn", name="out_proj", tm=1024, tk=1024,
                                  rows=[x2d], vecs=[(mod_mine, 2), (mod_mine, 3), (mod_mine, 4)],
                                  out_dtypes=[F32, F32, BF16])
    u1, act = _matmul(h2, w1, mode="nn", name="mlp_up", tm=4096, tk=1024, epilogue="relu2", slots="b_cols")
    (dx2, dm, fsums), _ = _matmul_rows(act, w2, _epi_final, mode="nn", name="mlp_down", tm=512, tk=4096,
                                       rows=[x1, tgt], vecs=[(mod_mine, 5), (final_norm_g[None, :], 0)],
                                       out_dtypes=[F32, BF16], sums=True)

    d_w2 = _matmul(act, dm, mode="tn", name="d_w_mlp2", out_dtype=BF16, tm=1024, tn=1024, tk=4096)
    du1 = _matmul(dm, w2, mode="nt", name="d_act", out_dtype=BF16, tm=2048, tn=1024, tk=1024,
                  epilogue="drelu2", extra=(u1,))
    d_w1 = _matmul(h2, du1, mode="tn", name="d_w_mlp1", out_dtype=BF16, tm=1024, tk=4096, slots="out")
    (dx1, do, sums2), _ = _matmul_rows(du1, w1, _epi_modulate2_bwd, mode="nt", name="d_h2", tm=512, tk=4096,
                                       slots="b_contract", rows=[x1, dx2, o], vecs=[(mod_mine, 4), (mod_mine, 2)],
                                       out_dtypes=[F32, BF16], sums=True)
    d_wo = _matmul(a_cat, do, mode="tn", name="d_w_out", out_dtype=BF16, tm=1024, tn=1024, tk=2048)
    da = _matmul(do, wo, mode="nt", name="d_a", tm=1024, tn=1024, tk=1024)
    dz_conv, d_cw = _conv_bwd(z_conv, cw_full, da, "conv_bwd")
    ready = [d_wo.reshape(N_DEV, d // N_DEV, d), d_w1, d_w2.reshape(N_DEV, D_FF // N_DEV, d)]
    dq, dk, dv, rode = _attn_bwd(qf, kv, attn, da, stats, cos, sgn, _Riding(ready), "attn_bwd")
    d_wq = _matmul(cq, dq, mode="tn", name="d_w_uq", k=s, tm=256, tn=1024, tk=4096)
    d_wkk = _matmul(kv_in, dk, mode="tn", name="d_w_ukv_k", tm=256, tn=1024, tk=tk_t)
    d_wkv = _matmul(kv_in, dv, mode="tn", name="d_w_ukv_v", tm=256, tn=512, tk=tk_t)
    dz_head, dh_head, psums = _head_bwd(dq, dk, dv, z_head, wq, wk_k, wk_v, win_head, q_norm_g, kv_norm_g, cos, sgn, s,
                                        "head_bwd")
    d_head = _matmul(dz_head, h_all, mode="tn", name="d_w_in_head", tm=512, tn=1024, tk=tk_t)
    d_conv = _matmul(dz_conv, h_all, mode="tn", name="d_w_in_conv", k=s, tm=1536, tn=1024, tk=2048)
    send = _pack_small_grads(d_head, d_conv, d_wq, d_wkk, d_wkv)
    (grad_x, sums1), got = _matmul_rows(dz_conv, win_conv, _epi_modulate1_bwd, mode="nn", name="d_h1", tm=max(s // 8, ROW_TILE),
                                        tk=win_conv.shape[0], rows=[dh_head, x2d, dx1], vecs=[(mod_mine, 1)],
                                        out_dtypes=[F32], sums=True, riding=_RidingReduce(send))
    sums1c = _modulate_sums(dh_head, s // ROW_TILE, ctx2d)

    small = _pack_small(sums1, sums2, fsums, sums1c, psums, d_cw, mod_cols, "pack_small")
    (d_all,) = _all_gather([small], "gather_small_grads", True)
    d_ex = lax.dynamic_index_in_dim(d_all, me, axis=1, keepdims=False)
    d_ctx = lax.dynamic_index_in_dim(d_all, N_DEV + me, axis=1, keepdims=False)
    g_w_mod, dsil, dsum = _adaln_bwd(a_rows.T, w_mod[0], d_ex, d_ctx, d_all, "adaln_bwd")
    (dsil_all,) = _all_gather([dsil], "gather_d_cctx", True)
    loss = dsum[SMALL_LOSS, 0]
    g_cw = lax.dynamic_slice(dsum, (SMALL_CW, me * cw_cols), (3, cw_cols))

    slots = dict(zip(["w_in", "w_uq", "w_ukv"], got))
    slots.update(zip(["w_out", "w_mlp1", "w_mlp2"], rode))

    grads = {}
    weights = {"c_ctx": c_ctx, "w_mod": w_mod, "b_mod": b_mod, "w_in": w_in, "q_norm_g": q_norm_g, "w_uq": w_uq,
               "kv_norm_g": kv_norm_g, "w_ukv": w_ukv, "conv_w": conv_w, "w_out": w_out, "w_mlp1": w_mlp1,
               "w_mlp2": w_mlp2, "final_norm_g": final_norm_g}
    m_in = {"c_ctx": m_c_ctx, "w_mod": m_w_mod, "b_mod": m_b_mod, "w_in": m_w_in, "q_norm_g": m_q_norm_g,
            "w_uq": m_w_uq, "kv_norm_g": m_kv_norm_g, "w_ukv": m_w_ukv, "conv_w": m_conv_w, "w_out": m_w_out,
            "w_mlp1": m_w_mlp1, "w_mlp2": m_w_mlp2, "final_norm_g": m_final_norm_g}
    v_in = {"c_ctx": v_c_ctx, "w_mod": v_w_mod, "b_mod": v_b_mod, "w_in": v_w_in, "q_norm_g": v_q_norm_g,
            "w_uq": v_w_uq, "kv_norm_g": v_kv_norm_g, "w_ukv": v_w_ukv, "conv_w": v_conv_w, "w_out": v_w_out,
            "w_mlp1": v_w_mlp1, "w_mlp2": v_w_mlp2, "final_norm_g": v_final_norm_g}
    names = list(weights)
    small_names = ["c_ctx", "b_mod", "q_norm_g", "kv_norm_g", "final_norm_g", "conv_w"]
    delta, new_m, new_v = {}, {}, {}

    def as_rows(a):
        return a[None, :] if a.ndim == 1 else a

    small_out = _small_update(dsum, dsil_all, g_cw, [[as_rows(src[n]) for src in (weights, m_in, v_in)]
                                                      for n in small_names], "small_update")
    for n, outs in zip(small_names, small_out):
        grads[n], delta[n], new_m[n], new_v[n] = [a.reshape(weights[n].shape) for a in outs]
    for n in names:
        if n in small_names:
            continue
        if n == "w_in":
            wmv = [jnp.swapaxes(src[n], 1, 2) for src in (weights, m_in, v_in)]
            outs = _adamw(wmv[0], slots[n], wmv[1], wmv[2], "adamw_" + n, slots=True)
            grads[n], delta[n], new_m[n], new_v[n] = [jnp.swapaxes(a, 1, 2) for a in outs]
        elif n in slots:
            grads[n], delta[n], new_m[n], new_v[n] = _adamw(weights[n], slots[n], m_in[n], v_in[n], "adamw_" + n,
                                                            slots=True)
        else:
            delta[n], new_m[n], new_v[n] = _adamw(weights[n], g_w_mod, m_in[n], v_in[n], "adamw_" + n)
            grads[n] = g_w_mod[None]

    return (loss, grad_x[None], *[grads[n] for n in names], *[delta[n] for n in names],
            *[new_m[n] for n in names], *[new_v[n] for n in names])
```

```python
import math

import jax
import jax.numpy as jnp
import numpy as np
from jax import lax
from jax.experimental import pallas as pl
from jax.experimental.pallas import tpu as pltpu

F32 = jnp.float32
BF16 = jnp.bfloat16

D_MODEL = 1024
GRID_W = 64
N_HEADS = 8
QK_NOPE = 64
QK_ROPE = 32
V_DIM = 64
Q_RANK = 256
KV_RANK = 128
MLA_IN = Q_RANK + KV_RANK + QK_ROPE
CONV_W = 512
HEAD_COLS = 512
D_FF = 4096
ROPE_THETA = 10000.0
EPS = 1e-6
ATTN_SCALE = 1.0 / math.sqrt(QK_NOPE + QK_ROPE)
LOG2_E = 1.0 / math.log(2.0)
EXP2_SCALE = ATTN_SCALE * LOG2_E
N_DEV = 8
LANES = 128

ADAM_LR, ADAM_B1, ADAM_B2, ADAM_EPS, ADAM_WD, ADAM_STEP = 0.001, 0.9, 0.999, 1e-08, 0.01, 10

ROW_TILE = 256
HEAD_BWD_TILE = 512
VMEM_BIG = 60 * 1024 * 1024


def _params(sem=None, vmem=None):
    return pltpu.CompilerParams(dimension_semantics=sem, vmem_limit_bytes=vmem)


def _pick(n, prefs):
    for p in prefs:
        if n % p == 0:
            return p
    return n


def _my_index():
    return 4 * lax.axis_index("x") + 2 * lax.axis_index("y") + lax.axis_index("c")


def _two_level_gather(x_refs, out_refs, send_sems, recv_sems, local_sems):
    n = len(x_refs)
    x, y, c = lax.axis_index("x"), lax.axis_index("y"), lax.axis_index("c")
    me, sibling = (x, y, c), (x, y, 1 - c)
    chips = [(1 - x, y), (x, 1 - y), (1 - x, 1 - y)]

    def slot(a, px, py, pc):
        return out_refs[a].at[4 * px + 2 * py + pc]

    def copy(a, k, block, to, src=None):
        return pltpu.make_async_remote_copy(
            src_ref=slot(a, *block) if src is None else src, dst_ref=slot(a, *block),
            send_sem=send_sems.at[7 * a + k], recv_sem=recv_sems.at[7 * a + k],
            device_id=to, device_id_type=pl.DeviceIdType.MESH)

    mine = [pltpu.make_async_copy(x_refs[a], slot(a, *me), local_sems.at[a]) for a in range(n)]
    first = [cp for a in range(n) for cp in
             [copy(a, 0, me, sibling, src=x_refs[a])]
             + [copy(a, 1 + j, me, (*chip, c), src=x_refs[a]) for j, chip in enumerate(chips)]]
    passed = [[copy(a, 4 + j, (*chip, c), sibling) for j, chip in enumerate(chips)] for a in range(n)]

    def start():
        for cp in mine + first:
            cp.start()

    def forward():
        for a in range(n):
            for j, chip in enumerate(chips):
                copy(a, 1 + j, (*chip, c), me).wait_recv()
                passed[a][j].start()

    def finish():
        for a in range(n):
            copy(a, 0, sibling, me).wait_recv()
            for j, chip in enumerate(chips):
                copy(a, 4 + j, (*chip, 1 - c), me).wait_recv()
        for cp in first + [cp for per_array in passed for cp in per_array]:
            cp.wait_send()
        for cp in mine:
            cp.wait()

    return start, forward, finish


def _direct_gather(src_ref, dst_ref, send_sems, recv_sems, per_peer=False):
    x, y, c = lax.axis_index("x"), lax.axis_index("y"), lax.axis_index("c")
    me = 4 * x + 2 * y + c
    dst_ref[me] = src_ref[me] if per_peer else src_ref[...]
    sends, landings = [], []
    for k in range(1, N_DEV):
        peer = (1 - x if k & 4 else x, 1 - y if k & 2 else y, 1 - c if k & 1 else c)
        pid = 4 * peer[0] + 2 * peer[1] + peer[2]
        for dst, out in ((me, sends), (pid, landings)):
            out.append(pltpu.make_async_remote_copy(
                src_ref=src_ref.at[pid] if per_peer else src_ref, dst_ref=dst_ref.at[dst],
                send_sem=send_sems.at[k - 1], recv_sem=recv_sems.at[k - 1],
                device_id=peer, device_id_type=pl.DeviceIdType.MESH))
    for cp in sends:
        cp.start()

    def finish():
        for cp in landings:
            cp.wait_recv()
        for cp in sends:
            cp.wait_send()

    return finish


def _all_gather(arrays, name, in_vmem):
    space = pltpu.VMEM if in_vmem else pl.ANY
    n = len(arrays)

    def body(*refs):
        for phase in _two_level_gather(refs[:n], refs[n:2 * n], *refs[2 * n:]):
            phase()

    outs = pl.pallas_call(
        body, name=name,
        out_shape=tuple(jax.ShapeDtypeStruct((N_DEV,) + a.shape, a.dtype) for a in arrays),
        in_specs=[pl.BlockSpec(memory_space=space)] * n,
        out_specs=tuple(pl.BlockSpec(memory_space=space) for _ in arrays),
        scratch_shapes=[pltpu.SemaphoreType.DMA((7 * n,)), pltpu.SemaphoreType.DMA((7 * n,)),
                        pltpu.SemaphoreType.DMA((n,))],
    )(*arrays)
    return list(outs)


class _Riding:
    def __init__(self, arrays=()):
        self.arrays, self.n = list(arrays), len(arrays)
        self.out_shape = [jax.ShapeDtypeStruct(a.shape, a.dtype) for a in self.arrays]
        self.specs = [pl.BlockSpec(memory_space=pl.ANY)] * self.n
        self.scratch = [pltpu.SemaphoreType.DMA((7 * self.n,)), pltpu.SemaphoreType.DMA((7 * self.n,)),
                        pltpu.SemaphoreType.DMA((self.n,))]

    def copies(self, x_refs, y_refs, send_sems, recv_sems, local_sems):
        x, y, c = lax.axis_index("x"), lax.axis_index("y"), lax.axis_index("c")
        me = 4 * x + 2 * y + c
        local, sends, landings = [], [], []
        for a in range(self.n):
            local.append(pltpu.make_async_copy(x_refs[a].at[me], y_refs[a].at[me], local_sems.at[a]))
            for k in range(1, N_DEV):
                peer = (1 - x if k & 4 else x, 1 - y if k & 2 else y, 1 - c if k & 1 else c)
                pid = 4 * peer[0] + 2 * peer[1] + peer[2]
                for dst, out in ((me, sends), (pid, landings)):
                    out.append(pltpu.make_async_remote_copy(
                        src_ref=x_refs[a].at[pid], dst_ref=y_refs[a].at[dst],
                        send_sem=send_sems.at[7 * a + k - 1], recv_sem=recv_sems.at[7 * a + k - 1],
                        device_id=peer, device_id_type=pl.DeviceIdType.MESH))
        return local, sends, landings

    def run(self, first, last, x_refs, y_refs, sems, middle=None):
        if self.n == 0:
            return None
        local, sends, landings = self.copies(x_refs, y_refs, *sems)

        @pl.when(first)
        def _():
            for cp in local + sends:
                cp.start()

        return local, sends, landings, last

    @staticmethod
    def finish(state):
        if state is None:
            return
        local, sends, landings, last = state

        @pl.when(last)
        def _():
            for cp in landings:
                cp.wait_recv()
            for cp in sends:
                cp.wait_send()
            for cp in local:
                cp.wait()


class _RidingGather:
    def __init__(self, arrays):
        self.arrays, self.n = list(arrays), len(arrays)
        self.out_shape = [jax.ShapeDtypeStruct((N_DEV,) + a.shape, a.dtype) for a in self.arrays]
        self.specs = [pl.BlockSpec(memory_space=pl.ANY)] * self.n
        self.scratch = [pltpu.SemaphoreType.DMA((7 * self.n,)), pltpu.SemaphoreType.DMA((7 * self.n,)),
                        pltpu.SemaphoreType.DMA((self.n,))]

    def run(self, first, last, x_refs, y_refs, sems, middle):
        start, forward, finish = _two_level_gather(x_refs, y_refs, *sems)
        pl.when(first)(start)
        pl.when(middle)(forward)
        return finish, last

    @staticmethod
    def finish(state):
        finish, last = state
        pl.when(last)(finish)


class _RidingReduce:
    def __init__(self, arrays):
        self.arrays, self.n = list(arrays), len(arrays)
        self.out_shape = [jax.ShapeDtypeStruct((4,) + a.shape[1:], a.dtype) for a in self.arrays]
        self.specs = [pl.BlockSpec(memory_space=pl.ANY)] * self.n
        self.scratch = [pltpu.VMEM((4,) + a.shape[1:], a.dtype) for a in self.arrays for _ in range(3)]
        self.scratch += [pltpu.SemaphoreType.DMA((self.n,)) for _ in range(6)]

    def run(self, first, last, x_refs, y_refs, scratch, middle):
        n = self.n
        own, sib, tot = scratch[0:3 * n:3], scratch[1:3 * n:3], scratch[2:3 * n:3]
        d2d_send, d2d_recv, local_in, ici_send, ici_recv, local_out = scratch[3 * n:]
        x, y, c = lax.axis_index("x"), lax.axis_index("y"), lax.axis_index("c")
        my_chip = 2 * x + y
        sibling = (x, y, 1 - c)
        others = [(1 - x, y), (x, 1 - y), (1 - x, 1 - y)]

        def to_sibling(a, j=None):
            src = x_refs[a].at[pl.ds(0, 4)] if j is None else x_refs[a].at[2 * j + 1 - c]
            dst = sib[a] if j is None else sib[a].at[j]
            return pltpu.make_async_remote_copy(src_ref=src, dst_ref=dst, send_sem=d2d_send.at[a],
                                                recv_sem=d2d_recv.at[a], device_id=sibling,
                                                device_id_type=pl.DeviceIdType.MESH)

        def mine_in(a, j=None):
            src = x_refs[a].at[pl.ds(0, 4)] if j is None else x_refs[a].at[2 * j + c]
            return pltpu.make_async_copy(src, own[a] if j is None else own[a].at[j], local_in.at[a])

        def to_chip(a, chip=None):
            if chip is None:
                src, dst, peer = tot[a].at[pl.ds(0, 3)], y_refs[a].at[pl.ds(0, 3)], sibling
            else:
                src, dst, peer = tot[a].at[2 * chip[0] + chip[1]], y_refs[a].at[my_chip], (*chip, c)
            return pltpu.make_async_remote_copy(src_ref=src, dst_ref=dst, send_sem=ici_send.at[a],
                                                recv_sem=ici_recv.at[a], device_id=peer,
                                                device_id_type=pl.DeviceIdType.MESH)

        def mine_out(a):
            return pltpu.make_async_copy(tot[a].at[my_chip], y_refs[a].at[my_chip], local_out.at[a])

        @pl.when(first)
        def _():
            for a in range(n):
                for j in range(4):
                    to_sibling(a, j).start()
                    mine_in(a, j).start()

        @pl.when(middle)
        def _():
            for a in range(n):
                to_sibling(a).wait_recv()
                to_sibling(a).wait_send()
                mine_in(a).wait()
                tot[a][...] = (own[a][...].astype(F32) + sib[a][...].astype(F32)).astype(tot[a].dtype)
                for chip in others:
                    to_chip(a, chip).start()
                mine_out(a).start()

        def finish():
            @pl.when(last)
            def _():
                for a in range(n):
                    to_chip(a).wait_recv()
                    to_chip(a).wait_send()
                    mine_out(a).wait()

        return finish

    @staticmethod
    def finish(state):
        state()


_DIMS ={"nn": (((1,), (0,)), ((), ())), "nt": (((1,), (1,)), ((), ())), "tn": (((0,), (0,)), ((), ()))}
NT_DIMS = _DIMS["nt"]
TN_DIMS = _DIMS["tn"]


def _swap8(x):
    lane = lax.broadcasted_iota(jnp.int32, x.shape, 1)
    return jnp.where((lane & 15) < 8, pltpu.roll(x, LANES - 8, 1), pltpu.roll(x, 8, 1))


def _rope(x, cos, sgn, bwd):
    return x * cos + (_swap8(x * sgn) if bwd else _swap8(x) * sgn)


def _matmul(a, b, *, mode, name, out_dtype=F32, tm=512, tn=512, tk=512, m=None, k=None,
            epilogue=None, extra=(), slots=None):
    if mode == "nn":
        m = a.shape[0] if m is None else m
        k = a.shape[1]
        n = N_DEV * b.shape[2] if slots == "b_cols" else b.shape[1]
    elif mode == "nt":
        m = a.shape[0] if m is None else m
        k = a.shape[1]
        n = b.shape[0]
    else:
        k = a.shape[0] if k is None else k
        m, n = a.shape[1], b.shape[1]
    tm, tn, tk = min(tm, m), min(tn, n), min(tk, k)
    if slots == "b_cols":
        tn = b.shape[2]
    if slots == "out":
        tn = n // N_DEV
    assert m % tm == 0 and n % tn == 0 and k % tk == 0, (name, m, n, k, tm, tn, tk)
    nk = k // tk
    dims = _DIMS[mode]
    a_spec = (pl.BlockSpec((tk, tm), lambda i, j, kk: (kk, i)) if mode == "tn"
              else pl.BlockSpec((tm, tk), lambda i, j, kk: (i, kk)))
    if slots == "b_cols":
        b_spec = pl.BlockSpec((None, tk, tn), lambda i, j, kk: (j, kk, 0))
    elif mode == "nt":
        b_spec = pl.BlockSpec((tn, tk), lambda i, j, kk: (j, kk))
    else:
        b_spec = pl.BlockSpec((tk, tn), lambda i, j, kk: (kk, j))
    tile = pl.BlockSpec((tm, tn), lambda i, j, kk: (i, j))
    if slots == "out":
        o_spec = pl.BlockSpec((None, tm, tn), lambda i, j, kk: (j, i, 0))
        o_shape = (N_DEV, m, tn)
    else:
        o_spec, o_shape = tile, (m, n)
    in_specs, args = [a_spec, b_spec], [a, b]
    if epilogue == "drelu2":
        in_specs.append(tile)
    args += list(extra)
    if epilogue == "relu2":
        out_shape = (jax.ShapeDtypeStruct(o_shape, BF16), jax.ShapeDtypeStruct(o_shape, BF16))
        out_specs = (o_spec, o_spec)
    else:
        out_shape = jax.ShapeDtypeStruct(o_shape, out_dtype)
        out_specs = o_spec
    n_in = len(args)
    n_out = 2 if epilogue == "relu2" else 1

    def body(*refs):
        a_ref, b_ref = refs[0], refs[1]
        outs = refs[n_in:n_in + n_out]
        part = lax.dot_general(a_ref[...], b_ref[...], dims, preferred_element_type=F32)

        def finish(acc):
            if epilogue == "relu2":
                outs[0][...] = acc.astype(BF16)
                r = jnp.maximum(acc, 0.0)
                outs[1][...] = (r * r).astype(BF16)
            elif epilogue == "drelu2":
                u = refs[2][...].astype(F32)
                outs[0][...] = (acc * (2.0 * jnp.maximum(u, 0.0))).astype(out_dtype)
            else:
                outs[0][...] = acc.astype(out_dtype)

        if nk == 1:
            finish(part)
        else:
            acc_ref = refs[n_in + n_out]
            kk = pl.program_id(2)

            @pl.when(kk == 0)
            def _():
                acc_ref[...] = part

            @pl.when(kk > 0)
            def _():
                acc_ref[...] += part

            @pl.when(kk == nk - 1)
            def _():
                finish(acc_ref[...])

    return pl.pallas_call(
        body, name=name, grid=(m // tm, n // tn, nk),
        out_shape=out_shape, in_specs=in_specs, out_specs=out_specs,
        scratch_shapes=[pltpu.VMEM((tm, tn), F32)] if nk > 1 else [],
        compiler_params=_params(("parallel", "parallel", "arbitrary"), VMEM_BIG),
    )(*args)


def _rstd(x):
    return lax.rsqrt(jnp.mean(x * x, axis=1, keepdims=True) + EPS)


def _norm_bwd(dxn, xn, r):
    return r * (dxn - xn * jnp.mean(dxn * xn, axis=1, keepdims=True))


def _vec(col):
    return pl.BlockSpec((1, D_MODEL), lambda i: (0, col))


def _matmul_rows(a, b, epi, *, mode, name, tm, tk, rows=(), vecs=(), out_dtypes=(), sums=False, slots=None,
                 riding=None):
    m, k = a.shape
    n = D_MODEL
    tm, tk = min(tm, m), min(tk, k)
    riding = riding or _Riding()
    group = 1
    if slots == "b_contract":
        group = max(1, tk // b.shape[2])
        tk = group * b.shape[2]
        b_spec = pl.BlockSpec((group, n, tk // group), lambda i, kk: (kk, 0, 0))
    elif mode == "nt":
        b_spec = pl.BlockSpec((n, tk), lambda i, kk: (0, kk))
    else:
        b_spec = pl.BlockSpec((tk, n), lambda i, kk: (kk, 0))
    assert m % tm == 0 and k % tk == 0, (name, m, k, tm, tk)
    ni, nk = m // tm, k // tk
    assert ni >= 2 or not isinstance(riding, _RidingReduce), "the two-level exchange needs a middle grid step"
    dims = _DIMS[mode]
    tile = pl.BlockSpec((tm, n), lambda i, kk: (i, 0))
    in_specs = [pl.BlockSpec((tm, tk), lambda i, kk: (i, kk)), b_spec] + [tile] * len(rows)
    in_specs += [pl.BlockSpec((1, n), lambda i, kk, col=col: (0, col)) for _, col in vecs]
    args = [a, b, *rows, *[v for v, _ in vecs]]
    out_shape = [jax.ShapeDtypeStruct((m, n), dt) for dt in out_dtypes]
    out_specs = [tile] * len(out_dtypes)
    if sums:
        out_shape.append(jax.ShapeDtypeStruct((8, n), F32))
        out_specs.append(pl.BlockSpec((8, n), lambda i, kk: (0, 0)))
    n_rows, n_vecs, n_outs, nr = len(rows), len(vecs), len(out_dtypes), riding.n
    n_in = 2 + n_rows + n_vecs

    def body(*refs):
        a_ref, b_ref = refs[0], refs[1]
        row_refs = refs[2:2 + n_rows]
        vec_refs = refs[2 + n_rows:n_in]
        x_refs = refs[n_in:n_in + nr]
        out_refs = refs[n_in + nr:n_in + nr + n_outs]
        pos = n_in + nr + n_outs
        sums_ref = refs[pos] if sums else None
        pos += 1 if sums else 0
        y_refs = refs[pos:pos + nr]
        pos += nr
        acc_ref = refs[pos] if nk > 1 else None
        sem_refs = refs[pos + (1 if nk > 1 else 0):]
        i, kk = pl.program_id(0), pl.program_id(1)
        state = riding.run((i == 0) & (kk == 0), (i == ni - 1) & (kk == nk - 1), x_refs, y_refs, sem_refs,
                           middle=(i == 1) & (kk == 0))
        if slots == "b_contract":
            c = tk // group
            part = lax.dot_general(a_ref[:, 0:c], b_ref[0], dims, preferred_element_type=F32)
            for u in range(1, group):
                part = part + lax.dot_general(a_ref[:, u * c:(u + 1) * c], b_ref[u], dims, preferred_element_type=F32)
        else:
            part = lax.dot_general(a_ref[...], b_ref[...], dims, preferred_element_type=F32)

        def finish(acc):
            nsub = tm // ROW_TILE
            for r in range(nsub):
                blk = pl.ds(r * ROW_TILE, ROW_TILE)
                epi(acc[r * ROW_TILE:(r + 1) * ROW_TILE], [ref.at[blk] for ref in row_refs], vec_refs,
                    [ref.at[blk] for ref in out_refs], sums_ref,
                    (i == 0) if r == 0 else None, (i == ni - 1) if r == nsub - 1 else None)

        if nk == 1:
            finish(part)
        else:
            @pl.when(kk == 0)
            def _():
                acc_ref[...] = part

            @pl.when(kk > 0)
            def _():
                acc_ref[...] += part

            @pl.when(kk == nk - 1)
            def _():
                finish(acc_ref)

        riding.finish(state)

    outs = pl.pallas_call(
        body, name=name, grid=(ni, nk),
        out_shape=(*out_shape, *riding.out_shape),
        in_specs=[*in_specs, *riding.specs], out_specs=(*out_specs, *riding.specs),
        scratch_shapes=([pltpu.VMEM((tm, n), F32)] if nk > 1 else []) + (riding.scratch if nr else []),
        compiler_params=_params(("arbitrary", "arbitrary"), VMEM_BIG),
    )(*args, *riding.arrays)
    n_own = len(out_shape)
    return list(outs[:n_own]), list(outs[n_own:])


def _zero_sums_at_start(sums_ref, first):
    if first is not None:
        @pl.when(first)
        def _():
            sums_ref[...] = jnp.zeros_like(sums_ref)


def _epi_resid_modulate(acc, rows, vecs, outs, sums_ref, first, last):
    (x_ref,), (g_ref, sh_ref, sc_ref) = rows, vecs
    x1 = x_ref[...] + g_ref[...] * acc
    outs[0][...] = acc
    outs[1][...] = x1
    outs[2][...] = (x1 * _rstd(x1) * (1.0 + sc_ref[...]) + sh_ref[...]).astype(BF16)


def _epi_final(acc, rows, vecs, outs, sums_ref, first, last):
    (x1_ref, t_ref), (g_ref, gf_ref) = rows, vecs
    d = acc.shape[1]
    x2 = x1_ref[...] + g_ref[...] * acc
    r = _rstd(x2)
    xn = x2 * r
    err = xn * gf_ref[...] - t_ref[...]
    dy = err * (1.0 / d)
    dx2 = _norm_bwd(dy * gf_ref[...], xn, r)
    outs[0][...] = dx2
    outs[1][...] = (dx2 * g_ref[...]).astype(BF16)
    _zero_sums_at_start(sums_ref, first)
    sums_ref[0:1, :] += jnp.sum(dy * xn, axis=0, keepdims=True)
    sums_ref[1:2, :] += jnp.sum(dx2 * acc, axis=0, keepdims=True)
    sums_ref[2:3, :] += jnp.sum(err * err, axis=0, keepdims=True)

    if last is not None:
        @pl.when(last)
        def _():
            tot = jnp.sum(sums_ref[2:3, :], axis=1, keepdims=True) * (0.5 / d)
            sums_ref[3:4, :] = jnp.broadcast_to(tot, (1, d))


def _epi_modulate2_bwd(acc, rows, vecs, outs, sums_ref, first, last):
    (x_ref, dres_ref, o_ref), (sc_ref, g_ref) = rows, vecs
    x = x_ref[...]
    r = _rstd(x)
    xn = x * r
    dx = dres_ref[...] + _norm_bwd(acc * (1.0 + sc_ref[...]), xn, r)
    outs[0][...] = dx
    outs[1][...] = (dx * g_ref[...]).astype(BF16)
    _zero_sums_at_start(sums_ref, first)
    sums_ref[0:1, :] += jnp.sum(acc * xn, axis=0, keepdims=True)
    sums_ref[1:2, :] += jnp.sum(acc, axis=0, keepdims=True)
    sums_ref[2:3, :] += jnp.sum(dx * o_ref[...], axis=0, keepdims=True)


def _epi_modulate1_bwd(acc, rows, vecs, outs, sums_ref, first, last):
    (add_ref, x_ref, dres_ref), (sc_ref,) = rows, vecs
    dh = acc + add_ref[...]
    x = x_ref[...]
    r = _rstd(x)
    xn = x * r
    outs[0][...] = dres_ref[...] + _norm_bwd(dh * (1.0 + sc_ref[...]), xn, r)
    _zero_sums_at_start(sums_ref, first)
    sums_ref[0:1, :] += jnp.sum(dh * xn, axis=0, keepdims=True)
    sums_ref[1:2, :] += jnp.sum(dh, axis=0, keepdims=True)


def _modulate_all(x, ctx, mod, mod_ctx, riding, name):
    s, d = x.shape
    t = s + ctx.shape[0]
    ns = s // ROW_TILE
    nc = ctx.shape[0] // ROW_TILE
    nr = riding.n

    def body(*refs):
        x_ref, c_ref, sh_ref, sc_ref, shc_ref, scc_ref = refs[:6]
        h_ref = refs[6 + nr]
        i = pl.program_id(0)
        state = riding.run(i == 0, i == ns + nc - 1, refs[6:6 + nr], refs[7 + nr:7 + 2 * nr], refs[7 + 2 * nr:],
                           middle=i == ns + nc - 3)

        @pl.when(i < ns)
        def _():
            v = x_ref[...]
            h_ref[...] = (v * _rstd(v) * (1.0 + sc_ref[...]) + sh_ref[...]).astype(BF16)

        @pl.when(i >= ns)
        def _():
            v = c_ref[...]
            h_ref[...] = (v * _rstd(v) * (1.0 + scc_ref[...]) + shc_ref[...]).astype(BF16)

        riding.finish(state)

    outs = pl.pallas_call(
        body, name=name, grid=(ns + nc,),
        out_shape=(jax.ShapeDtypeStruct((t, d), BF16), *riding.out_shape),
        in_specs=[pl.BlockSpec((ROW_TILE, d), lambda i: (jnp.minimum(i, ns - 1), 0)),
                  pl.BlockSpec((ROW_TILE, d), lambda i: (jnp.maximum(i - ns, 0), 0)),
                  _vec(0), _vec(1), _vec(0), _vec(1), *riding.specs],
        out_specs=(pl.BlockSpec((ROW_TILE, d), lambda i: (i, 0)), *riding.specs),
        scratch_shapes=riding.scratch,
        compiler_params=_params(("arbitrary",)),
    )(x, ctx, mod, mod, mod_ctx, mod_ctx, *riding.arrays)
    return outs[0], list(outs[1:])


def _modulate_sums(dh, row_off, xsrc):
    s, d = xsrc.shape

    def body(dh_ref, x_ref, sums_ref):
        i = pl.program_id(0)
        x = x_ref[...]
        dhv = dh_ref[...]

        @pl.when(i == 0)
        def _():
            sums_ref[...] = jnp.zeros_like(sums_ref)

        sums_ref[0:1, :] += jnp.sum(dhv * (x * _rstd(x)), axis=0, keepdims=True)
        sums_ref[1:2, :] += jnp.sum(dhv, axis=0, keepdims=True)

    return pl.pallas_call(
        body, name="modulate1_ctx_bwd", grid=(s // ROW_TILE,),
        out_shape=jax.ShapeDtypeStruct((8, d), F32),
        in_specs=[pl.BlockSpec((ROW_TILE, d), lambda i: (i + row_off, 0)), pl.BlockSpec((ROW_TILE, d), lambda i: (i, 0))],
        out_specs=pl.BlockSpec((8, d), lambda i: (0, 0)),
        compiler_params=_params(("arbitrary",)),
    )(dh, xsrc)


def _head_fwd(h_all, win_head, wq, wk, q_gain, kv_gain, cos, sgn, tm, name):
    t, d = h_all.shape
    nq, nkv = wq.shape[1], wk.shape[1]

    def body(h_ref, wi_ref, wq_ref, wk_ref, qg_ref, kg_ref, c_ref, s_ref, z_ref, cq_ref, kvin_ref, qf_ref, kv_ref):
        z = lax.dot_general(h_ref[...], wi_ref[...], NT_DIMS, preferred_element_type=F32)
        z_ref[...] = z
        cos, sgn = c_ref[...], s_ref[...]
        zq = z[:, 0:Q_RANK]
        cq = (zq * _rstd(zq) * qg_ref[...]).astype(BF16)
        cq_ref[...] = cq
        zk = z[:, Q_RANK:Q_RANK + KV_RANK]
        kv_in = jnp.concatenate([(zk * _rstd(zk) * kg_ref[...]).astype(BF16),
                                 _rope(z[:, Q_RANK + KV_RANK:HEAD_COLS], cos, sgn, False).astype(BF16)], axis=1)
        kvin_ref[...] = kv_in
        q = jnp.dot(cq, wq_ref[...], preferred_element_type=F32)
        for h in range(nq // LANES):
            sl = slice(h * LANES, (h + 1) * LANES)
            qf_ref[:, sl] = _rope(q[:, sl], cos, sgn, False).astype(BF16)
        kv_ref[...] = jnp.dot(kv_in, wk_ref[...], preferred_element_type=F32).astype(BF16)

    def row(w):
        return pl.BlockSpec((tm, w), lambda i: (i, 0))

    def whole(a):
        return pl.BlockSpec(a.shape, lambda i: (0, 0))

    return pl.pallas_call(
        body, name=name, grid=(t // tm,),
        out_shape=(jax.ShapeDtypeStruct((t, HEAD_COLS), F32), jax.ShapeDtypeStruct((t, Q_RANK), BF16),
                   jax.ShapeDtypeStruct((t, KV_RANK + LANES), BF16), jax.ShapeDtypeStruct((t, nq), BF16),
                   jax.ShapeDtypeStruct((t, nkv), BF16)),
        in_specs=[row(d), whole(win_head), whole(wq), whole(wk), whole(q_gain), whole(kv_gain), row(LANES), row(LANES)],
        out_specs=(row(HEAD_COLS), row(Q_RANK), row(KV_RANK + LANES), row(nq), row(nkv)),
        compiler_params=_params(("parallel",), VMEM_BIG),
    )(h_all, win_head, wq, wk, q_gain, kv_gain, cos, sgn)


def _head_bwd(dq, dk, dv, z, wq, wk_k, wk_v, win_head, q_gain, kv_gain, cos, sgn, name, *, tile, first_block,
              n_blocks, carry=None):
    t = z.shape[0]
    with_q = dq is not None

    def body(*refs):
        it = iter(refs)
        dq_ref = next(it) if with_q else None
        dk_ref, dv_ref, z_ref, wq_ref, wkk_ref, wkv_ref, wi_ref, qg_ref, kg_ref, c_ref, s_ref = (next(it) for _ in range(11))
        if carry is not None:
            next(it), next(it)
        dz_ref, dh_ref, sums_ref = next(it), next(it), next(it)
        i = pl.program_id(0)

        @pl.when(i == 0)
        def _():
            sums_ref[...] = jnp.zeros_like(sums_ref)

        if with_q:
            dc = lax.dot_general(dq_ref[...], wq_ref[...], NT_DIMS, preferred_element_type=F32)
            zq = z_ref[:, 0:Q_RANK]
            r = _rstd(zq)
            zn = zq * r
            sums_ref[0:1, :] += jnp.sum(dc * zn, axis=0, keepdims=True)
            dz_ref[:, 0:Q_RANK] = _norm_bwd(dc * qg_ref[...], zn, r).astype(BF16)
        else:
            dz_ref[:, 0:Q_RANK] = jnp.zeros((tile, Q_RANK), BF16)
        dkv = (lax.dot_general(dk_ref[...], wkk_ref[...], NT_DIMS, preferred_element_type=F32)
               + lax.dot_general(dv_ref[...], wkv_ref[...], NT_DIMS, preferred_element_type=F32))
        zk = z_ref[:, Q_RANK:Q_RANK + KV_RANK]
        r = _rstd(zk)
        zn = zk * r
        dc = dkv[:, 0:KV_RANK]
        sums_ref[1:2, 0:KV_RANK] += jnp.sum(dc * zn, axis=0, keepdims=True)
        dz_ref[:, Q_RANK:Q_RANK + KV_RANK] = _norm_bwd(dc * kg_ref[...], zn, r).astype(BF16)
        dz_ref[:, Q_RANK + KV_RANK:HEAD_COLS] = _rope(dkv[:, KV_RANK:KV_RANK + LANES], c_ref[...], s_ref[...],
                                                       True).astype(BF16)
        dh_ref[...] = jnp.dot(dz_ref[...], wi_ref[...], preferred_element_type=F32)

    def row(w):
        return pl.BlockSpec((tile, w), lambda i: (i + first_block, 0))

    def whole(a):
        return pl.BlockSpec(a.shape, lambda i: (0, 0))

    args = ([dq] if with_q else []) + [dk, dv, z, wq, wk_k, wk_v, win_head, q_gain, kv_gain, cos, sgn]
    in_specs = ([row(dq.shape[1])] if with_q else []) + [
        row(dk.shape[1]), row(dv.shape[1]), row(HEAD_COLS), whole(wq), whole(wk_k), whole(wk_v),
        whole(win_head), whole(q_gain), whole(kv_gain), row(LANES), row(LANES)]
    aliases = {}
    if carry is not None:
        aliases = {len(args): 0, len(args) + 1: 1}
        args += list(carry)
        in_specs += [pl.BlockSpec(memory_space=pl.ANY)] * 2
    return pl.pallas_call(
        body, name=name, grid=(n_blocks,),
        out_shape=(jax.ShapeDtypeStruct((t, HEAD_COLS), BF16), jax.ShapeDtypeStruct((t, D_MODEL), F32),
                   jax.ShapeDtypeStruct((8, Q_RANK), F32)),
        in_specs=in_specs,
        out_specs=(row(HEAD_COLS), row(D_MODEL), pl.BlockSpec((8, Q_RANK), lambda i: (0, 0))),
        input_output_aliases=aliases,
        compiler_params=_params(("arbitrary",), VMEM_BIG),
    )(*args)


def _shift_rows(u, s):
    rowi = lax.broadcasted_iota(jnp.int32, u.shape, 0)
    prev = jnp.where(rowi == 0, 0.0, pltpu.roll(u, 1, 0))
    nxt = jnp.where(rowi == s - 1, 0.0, pltpu.roll(u, s - 1, 0))
    return prev, nxt


def _conv_fwd(z_conv, cw, a_cat, name):
    s = z_conv.shape[0]

    def body(z_ref, w_ref, a_in_ref, o_ref):
        del a_in_ref
        gb, gc, xv = z_ref[:, 0:LANES], z_ref[:, LANES:2 * LANES], z_ref[:, 2 * LANES:3 * LANES]
        u = gc * xv
        prev, nxt = _shift_rows(u, s)
        y = w_ref[0:1, :] * prev + w_ref[1:2, :] * u + w_ref[2:3, :] * nxt
        o_ref[...] = (gb * y).astype(BF16)

    return pl.pallas_call(
        body, name=name, grid=(CONV_W // LANES,),
        out_shape=jax.ShapeDtypeStruct(a_cat.shape, a_cat.dtype),
        in_specs=[pl.BlockSpec((s, 3 * LANES), lambda j: (0, j)), pl.BlockSpec((3, LANES), lambda j: (0, j)),
                  pl.BlockSpec(memory_space=pl.ANY)],
        out_specs=pl.BlockSpec((s, LANES), lambda j: (0, 4 + j)),
        input_output_aliases={2: 0},
        compiler_params=_params(("parallel",), VMEM_BIG),
    )(z_conv, cw, a_cat)


def _conv_bwd(z_conv, cw, da, name):
    s = z_conv.shape[0]

    def body(z_ref, w_ref, da_ref, dz_ref, dw_ref):
        gb, gc, xv = z_ref[:, 0:LANES], z_ref[:, LANES:2 * LANES], z_ref[:, 2 * LANES:3 * LANES]
        u = gc * xv
        prev, nxt = _shift_rows(u, s)
        dcv = da_ref[...]
        dz_ref[:, 0:LANES] = (dcv * (w_ref[0:1, :] * prev + w_ref[1:2, :] * u + w_ref[2:3, :] * nxt)).astype(BF16)
        dy = dcv * gb
        dw_ref[0:1, :] = jnp.sum(dy * prev, axis=0, keepdims=True)
        dw_ref[1:2, :] = jnp.sum(dy * u, axis=0, keepdims=True)
        dw_ref[2:3, :] = jnp.sum(dy * nxt, axis=0, keepdims=True)
        dyp, dyn = _shift_rows(dy, s)
        du = w_ref[0:1, :] * dyn + w_ref[1:2, :] * dy + w_ref[2:3, :] * dyp
        dz_ref[:, LANES:2 * LANES] = (du * xv).astype(BF16)
        dz_ref[:, 2 * LANES:3 * LANES] = (du * gc).astype(BF16)

    blk = pl.BlockSpec((s, 3 * LANES), lambda j: (0, j))
    cws = pl.BlockSpec((3, LANES), lambda j: (0, j))
    return pl.pallas_call(
        body, name=name, grid=(CONV_W // LANES,),
        out_shape=(jax.ShapeDtypeStruct(z_conv.shape, BF16), jax.ShapeDtypeStruct((3, CONV_W), F32)),
        in_specs=[blk, cws, pl.BlockSpec((s, LANES), lambda j: (0, 4 + j))], out_specs=(blk, cws),
        compiler_params=_params(("parallel",), VMEM_BIG),
    )(z_conv, cw, da)


ATT_TQ = 512
ATT_Q_STEP = 1024
ATT_TQ_BWD = 512


def _head_mask(shape, hh):
    lane = lax.broadcasted_iota(jnp.int32, shape, 1)
    return (lane >= hh * V_DIM) & (lane < (hh + 1) * V_DIM)


def _attn_fwd(qf, kv, s, riding, name):
    t = kv.shape[0]
    step = min(ATT_Q_STEP, s)
    nq = s // step
    nr = riding.n

    def body(*refs):
        q_ref, k_ref, v_ref = refs[:3]
        o_ref, ob_ref, st_ref = refs[3 + nr:6 + nr]
        p, i = pl.program_id(0), pl.program_id(1)
        state = riding.run((p == 0) & (i == 0), (p == N_HEADS // 2 - 1) & (i == nq - 1),
                           refs[3:3 + nr], refs[6 + nr:6 + 2 * nr], refs[6 + 2 * nr:],
                           middle=(p == N_HEADS // 2 - 2) & (i == nq // 2))
        v = v_ref[...]
        vlane = lax.broadcasted_iota(jnp.int32, v.shape, 1)
        one_lane = [(1 - hh) * V_DIM for hh in range(2)]
        vm = [jnp.where(_head_mask(v.shape, hh), v, jnp.where(vlane == one_lane[hh], 1.0, 0.0).astype(BF16))
              for hh in range(2)]

        def block(r, carry):
            rows = pl.ds(pl.multiple_of(r * ATT_TQ, ATT_TQ), ATT_TQ)
            olane = lax.broadcasted_iota(jnp.int32, (ATT_TQ, LANES), 1)
            acc = jnp.zeros((ATT_TQ, LANES), F32)
            stat = jnp.zeros((ATT_TQ, LANES), F32)
            scores = [lax.dot_general(q_ref[rows, hh * LANES:(hh + 1) * LANES], k_ref[:, hh * LANES:(hh + 1) * LANES],
                                      NT_DIMS, preferred_element_type=F32) for hh in range(2)]
            maxes = [jnp.max(sc, axis=1, keepdims=True) for sc in scores]
            exps = [jnp.exp2((sc - mx) * EXP2_SCALE).astype(BF16) for sc, mx in zip(scores, maxes)]
            for hh in range(2):
                mx = maxes[hh]
                res = jnp.dot(exps[hh], vm[hh], preferred_element_type=F32)
                den = jnp.sum(jnp.where(olane == one_lane[hh], res, 0.0), axis=1, keepdims=True)
                acc = acc + jnp.where(_head_mask(res.shape, hh), res * (1.0 / den), 0.0)
                stat = stat + jnp.where(olane == hh, mx * EXP2_SCALE + jnp.log(den) * LOG2_E, 0.0)
            o_ref[rows, :] = acc
            ob_ref[rows, :] = acc.astype(BF16)
            st_ref[:, rows] = stat.T[0:8, :]
            return carry

        lax.fori_loop(0, step // ATT_TQ, block, 0)
        riding.finish(state)

    o_spec = pl.BlockSpec((step, LANES), lambda p, i: (i, p))
    outs = pl.pallas_call(
        body, name=name, grid=(N_HEADS // 2, nq),
        out_shape=(jax.ShapeDtypeStruct((s, N_HEADS * V_DIM), F32),
                   jax.ShapeDtypeStruct((s, D_MODEL), BF16),
                   jax.ShapeDtypeStruct((N_HEADS // 2 * 8, s), F32), *riding.out_shape),
        in_specs=[pl.BlockSpec((step, 2 * LANES), lambda p, i: (i, p)),
                  pl.BlockSpec((t, 2 * LANES), lambda p, i: (0, p)),
                  pl.BlockSpec((t, LANES), lambda p, i: (0, N_HEADS + p)), *riding.specs],
        out_specs=(o_spec, o_spec, pl.BlockSpec((8, step), lambda p, i: (p, i)), *riding.specs),
        scratch_shapes=riding.scratch,
        compiler_params=_params(("arbitrary", "arbitrary"), VMEM_BIG),
    )(qf, kv, kv, *riding.arrays)
    return outs[0], outs[1], outs[2], list(outs[3:])


def _attn_bwd(qf, kv, o, da, stats, cos, sgn, riding, name):
    s, t = o.shape[0], kv.shape[0]
    ATT_TQ = ATT_TQ_BWD
    nq = s // ATT_TQ
    nr = riding.n

    def body(*refs):
        q_ref, k_ref, v_ref, o_ref, do_ref, st_ref, c_ref, s_ref = refs[:8]
        dq_ref, dk_ref, dv_ref = refs[8 + nr:11 + nr]
        dk_acc, dv_acc = refs[11 + 2 * nr:13 + 2 * nr]
        p, i = pl.program_id(0), pl.program_id(1)
        state = riding.run((p == 0) & (i == 0), (p == N_HEADS // 2 - 1) & (i == nq - 1),
                           refs[8:8 + nr], refs[11 + nr:11 + 2 * nr], refs[13 + 2 * nr:])

        @pl.when(i == 0)
        def _():
            dk_acc[...] = jnp.zeros_like(dk_acc)
            dv_acc[...] = jnp.zeros_like(dv_acc)

        v = v_ref[...]
        do = do_ref[...]
        od = do * o_ref[...]
        ones = jnp.ones((8, LANES), F32)
        for hh in range(2):
            sl = slice(hh * LANES, (hh + 1) * LANES)
            q, k = q_ref[:, sl], k_ref[:, sl]
            mask = _head_mask(do.shape, hh)
            dom = jnp.where(mask, do, 0.0).astype(BF16)
            delta = lax.dot_general(ones, jnp.where(mask, od, 0.0), NT_DIMS, preferred_element_type=F32,
                                    precision=lax.Precision.HIGHEST)[0:1, :]
            st = lax.dot_general(k, q, NT_DIMS, preferred_element_type=F32)
            pt = jnp.exp2(st * EXP2_SCALE - st_ref[hh:hh + 1, :]).astype(BF16)
            dpt = lax.dot_general(v, dom, NT_DIMS, preferred_element_type=F32)
            dst = (pt.astype(F32) * (dpt - delta)).astype(BF16)
            dv_acc[...] += jnp.dot(pt, dom, preferred_element_type=F32)
            dk_acc[:, sl] += jnp.dot(dst, q, preferred_element_type=F32)
            dq = lax.dot_general(dst, k, TN_DIMS, preferred_element_type=F32) * ATTN_SCALE
            dq_ref[:, sl] = _rope(dq, c_ref[...], s_ref[...], True).astype(BF16)

        @pl.when(i == nq - 1)
        def _():
            dk_ref[...] = (dk_acc[...] * ATTN_SCALE).astype(BF16)
            dv_ref[...] = dv_acc[...].astype(BF16)

        riding.finish(state)

    o_spec = pl.BlockSpec((ATT_TQ, LANES), lambda p, i: (i, p))
    tab = pl.BlockSpec((ATT_TQ, LANES), lambda p, i: (i, 0))
    outs = pl.pallas_call(
        body, name=name, grid=(N_HEADS // 2, nq),
        out_shape=(jax.ShapeDtypeStruct((s, N_HEADS * LANES), BF16),
                   jax.ShapeDtypeStruct((t, N_HEADS * LANES), BF16),
                   jax.ShapeDtypeStruct((t, N_HEADS * V_DIM), BF16), *riding.out_shape),
        in_specs=[pl.BlockSpec((ATT_TQ, 2 * LANES), lambda p, i: (i, p)),
                  pl.BlockSpec((t, 2 * LANES), lambda p, i: (0, p)),
                  pl.BlockSpec((t, LANES), lambda p, i: (0, N_HEADS + p)),
                  o_spec, o_spec,
                  pl.BlockSpec((8, ATT_TQ), lambda p, i: (p, i)), tab, tab, *riding.specs],
        out_specs=(pl.BlockSpec((ATT_TQ, 2 * LANES), lambda p, i: (i, p)),
                   pl.BlockSpec((t, 2 * LANES), lambda p, i: (0, p)),
                   pl.BlockSpec((t, LANES), lambda p, i: (0, p)), *riding.specs),
        scratch_shapes=[pltpu.VMEM((t, 2 * LANES), F32), pltpu.VMEM((t, LANES), F32), *riding.scratch],
        compiler_params=_params(("arbitrary", "arbitrary"), VMEM_BIG),
    )(qf, kv, kv, o, da, stats, cos, sgn, *riding.arrays)
    return outs[0], outs[1], outs[2], list(outs[3:])


def _silu(x):
    return x * (1.0 / (1.0 + jnp.exp(-x)))


def _prologue(c_rows, c_ctx, w_mod, b_cols, extra_rows, name):
    d, cols = c_rows.shape[1], w_mod.shape[1]

    def body(c_ref, cctx_ref, wmod_ref, b_ref, x_ref, a_ref, modg_ref, c_all, blk, c_send, c_recv, m_send, m_recv):
        _direct_gather(c_ref, c_all, c_send, c_recv)()
        a_ref[...] = jnp.zeros_like(a_ref)
        for j in range(N_DEV):
            a_ref[j:j + 1, :] = c_all[j, 0:1, :]
        a_ref[N_DEV:N_DEV + 1, :] = cctx_ref[...]
        mod = jnp.dot(_silu(a_ref[...]), wmod_ref[...], preferred_element_type=F32,
                      precision=lax.Precision.HIGHEST) + b_ref[...]
        blk[...] = jnp.zeros_like(blk)
        for p in range(N_DEV):
            blk[p, 0:1, :] = mod[p:p + 1, :]
            blk[p, 1:2, :] = mod[N_DEV:N_DEV + 1, :]
            blk[p, 2:5, :] = x_ref[...]
        _direct_gather(blk, modg_ref, m_send, m_recv, per_peer=True)()

    vmem = pl.BlockSpec(memory_space=pltpu.VMEM)
    return pl.pallas_call(
        body, name=name,
        out_shape=(jax.ShapeDtypeStruct((16, d), F32), jax.ShapeDtypeStruct((N_DEV, 8, cols), F32)),
        in_specs=[vmem] * 5, out_specs=(vmem, vmem),
        scratch_shapes=[pltpu.VMEM((N_DEV, 8, d), F32), pltpu.VMEM((N_DEV, 8, cols), F32)]
        + [pltpu.SemaphoreType.DMA((7,)) for _ in range(4)],
        compiler_params=_params(None, VMEM_BIG),
    )(c_rows, c_ctx, w_mod, b_cols, extra_rows)


def _adaln_bwd(a_t, w, d_ex, d_ctx, d_all, name):
    def body(at_ref, w_ref, dex_ref, dctx_ref, dall_ref, gw_ref, dsil_ref, dsum_ref):
        sil_t = _silu(at_ref[...])
        dctx = dctx_ref[...]
        row = dctx[0:1, :]
        for j in range(1, N_DEV):
            row = row + dctx[j:j + 1, :]
        rowi = lax.broadcasted_iota(jnp.int32, dctx.shape, 0)
        ctx_rows = jnp.where(rowi == 0, jnp.broadcast_to(row, dctx.shape), 0.0)
        hi = lax.Precision.HIGHEST
        d_rows = jnp.concatenate([dex_ref[...], ctx_rows], axis=0)
        gw_ref[...] = jnp.dot(sil_t, d_rows, preferred_element_type=F32, precision=hi)
        dsil_ref[...] = lax.dot_general(ctx_rows, w_ref[...], NT_DIMS, preferred_element_type=F32, precision=hi)
        tot = dall_ref[0]
        for j in range(1, N_DEV):
            tot = tot + dall_ref[j]
        dsum_ref[...] = tot

    return pl.pallas_call(
        body, name=name,
        out_shape=(jax.ShapeDtypeStruct(w.shape, F32), jax.ShapeDtypeStruct((8, w.shape[0]), F32),
                   jax.ShapeDtypeStruct(d_all.shape[1:], F32)),
        compiler_params=_params(None, VMEM_BIG),
    )(a_t, w, d_ex, d_ctx, d_all)


SMALL_ROWS = 24
SMALL_MISC, SMALL_CW, SMALL_LOSS = 16, 18, 21


def _pack_small(sums1, sums2, fsums, sums1c, psums, psums_c, d_cw, cols, name):
    d = D_MODEL

    def body(s1_ref, s2_ref, f_ref, s1c_ref, p_ref, pc_ref, cw_ref, o_ref):
        o_ref[...] = jnp.zeros_like(o_ref)

        def blocks(row0, pieces):
            for j in range(N_DEV):
                lo, hi = j * cols, (j + 1) * cols
                for k, (ref, r) in enumerate(pieces):
                    a, b = max(lo, k * d), min(hi, (k + 1) * d)
                    if a < b:
                        o_ref[row0 + j:row0 + j + 1, a - lo:b - lo] = ref[r:r + 1, a - k * d:b - k * d]

        blocks(0, [(s1_ref, 1), (s1_ref, 0), (s2_ref, 2), (s2_ref, 1), (s2_ref, 0), (f_ref, 1)])
        blocks(N_DEV, [(s1c_ref, 1), (s1c_ref, 0)])
        head = Q_RANK + KV_RANK
        o_ref[SMALL_MISC:SMALL_MISC + 1, 0:Q_RANK] = p_ref[0:1, :]
        o_ref[SMALL_MISC:SMALL_MISC + 1, Q_RANK:head] = p_ref[1:2, 0:KV_RANK] + pc_ref[1:2, 0:KV_RANK]
        o_ref[SMALL_MISC:SMALL_MISC + 1, head:cols] = f_ref[0:1, 0:cols - head]
        o_ref[SMALL_MISC + 1:SMALL_MISC + 2, 0:d - (cols - head)] = f_ref[0:1, cols - head:d]
        for r in range(3):
            o_ref[SMALL_CW + r:SMALL_CW + r + 1, 0:CONV_W] = cw_ref[r:r + 1, :]
        o_ref[SMALL_LOSS:SMALL_LOSS + 1, :] = f_ref[3:4, 0:cols]

    return pl.pallas_call(body, name=name, out_shape=jax.ShapeDtypeStruct((SMALL_ROWS, cols), F32))(
        sums1, sums2, fsums, sums1c, psums, psums_c, d_cw)


def _adam_math(w, g, m, v):
    nm = ADAM_B1 * m + (1.0 - ADAM_B1) * g
    nv = ADAM_B2 * v + (1.0 - ADAM_B2) * (g * g)
    m_hat = nm / (1.0 - ADAM_B1 ** ADAM_STEP)
    v_hat = nv / (1.0 - ADAM_B2 ** ADAM_STEP)
    return -ADAM_LR * (m_hat / (jnp.sqrt(v_hat) + ADAM_EPS) + ADAM_WD * w), nm, nv


def _small_update(dsum, dsil_all, g_cw, params, name):
    d = D_MODEL
    n = len(params)
    cols = dsum.shape[1]

    def body(*refs):
        dsum_ref, dsil_ref, gcw_ref = refs[:3]
        wmv = refs[3:3 + 3 * n]
        outs = refs[3 + 3 * n:]
        tot = dsil_ref[0]
        for j in range(1, N_DEV):
            tot = tot + dsil_ref[j]
        cv = wmv[0][...]
        sg = 1.0 / (1.0 + jnp.exp(-cv))
        off = Q_RANK + KV_RANK
        misc = dsum_ref[SMALL_MISC:SMALL_MISC + 1, :]
        grads = [tot[0:1, :] * (sg * (1.0 + cv * (1.0 - sg))),
                 jnp.concatenate([dsum_ref[j:j + 1, :] + dsum_ref[N_DEV + j:N_DEV + j + 1, :] for j in range(N_DEV)],
                                 axis=1),
                 misc[:, 0:Q_RANK], misc[:, Q_RANK:off],
                 jnp.concatenate([misc[:, off:cols], dsum_ref[SMALL_MISC + 1:SMALL_MISC + 2, 0:d - (cols - off)]],
                                 axis=1),
                 gcw_ref[...]]
        for p, g in enumerate(grads):
            w_ref, m_ref, v_ref = wmv[3 * p:3 * p + 3]
            at = 0 if len(w_ref.shape) == 3 else Ellipsis
            res = (g,) + _adam_math(w_ref[at], g, m_ref[at], v_ref[at])
            for q, val in enumerate(res):
                outs[4 * p + q][at] = val

    flat = [a for wmv in params for a in wmv]
    out_shape = tuple(jax.ShapeDtypeStruct(wmv[0].shape, F32) for wmv in params for _ in range(4))
    outs = pl.pallas_call(body, name=name, out_shape=out_shape)(dsum, dsil_all, g_cw, *flat)
    return [outs[4 * p:4 * p + 4] for p in range(n)]


def _adamw(w, g, m, v, name, slots=False):
    _, rows, cols = w.shape
    tr = _pick(rows, (256, 128, 64, 32, 16, 8))

    def body(w_ref, g_ref, m_ref, v_ref, *outs):
        if slots:
            gv = g_ref[0].astype(F32)
            for j in range(1, g.shape[0]):
                gv = gv + g_ref[j].astype(F32)
            outs[0][...] = gv
        else:
            gv = g_ref[...]
        d_ref, nm_ref, nv_ref = outs[-3:]
        d_ref[...], nm_ref[...], nv_ref[...] = _adam_math(w_ref[...], gv, m_ref[...], v_ref[...])

    blk = pl.BlockSpec((None, tr, cols), lambda i: (0, i, 0))
    g_spec = (pl.BlockSpec((g.shape[0], tr, cols), lambda i: (0, i, 0)) if slots
              else pl.BlockSpec((tr, cols), lambda i: (i, 0)))
    sh = jax.ShapeDtypeStruct((1, rows, cols), F32)
    n_out = 4 if slots else 3
    return pl.pallas_call(
        body, name=name, grid=(rows // tr,), out_shape=(sh,) * n_out,
        in_specs=[blk, g_spec, blk, blk], out_specs=(blk,) * n_out,
        compiler_params=_params(("parallel",), VMEM_BIG),
    )(w, g, m, v)


def _rope_tables(s, l):
    tok = np.arange(s)
    row = (tok // GRID_W).astype(np.float32)
    col = (tok % GRID_W).astype(np.float32)
    half = QK_ROPE // 2
    freqs = np.float32(ROPE_THETA) ** (-np.arange(0, half, 2, dtype=np.float32) / np.float32(half))
    dd = np.arange(QK_ROPE)
    pos = np.where((dd // half)[None, :] == 0, row[:, None], col[:, None]).astype(np.float32)
    ang = (pos * freqs[dd % (half // 2)][None, :]).astype(np.float32)
    sin = np.sin(ang).astype(np.float32)
    cos_t = np.ones((s + l, LANES), np.float32)
    sgn_t = np.zeros((s + l, LANES), np.float32)
    cos_t[:s, QK_NOPE:QK_NOPE + QK_ROPE] = np.cos(ang)
    sgn_t[:s, QK_NOPE:QK_NOPE + QK_ROPE] = np.where(((dd % half) // (half // 2))[None, :] == 0, -sin, sin)
    return jnp.asarray(cos_t), jnp.asarray(sgn_t)


def _slots_to_cols(g):
    return g.transpose(1, 0, 2).reshape(g.shape[1], N_DEV * g.shape[2])


def _cols_to_slots(w):
    return w.reshape(w.shape[0], N_DEV, w.shape[1] // N_DEV).transpose(1, 0, 2)


def _unpack_small_weights(g_in_t, g_uq, g_ukv):
    w_t = g_in_t.reshape(N_DEV * g_in_t.shape[1], D_MODEL)
    zeros = jnp.zeros((QK_NOPE, D_MODEL), BF16)
    win_head_t = jnp.concatenate([w_t[:Q_RANK + KV_RANK], zeros, w_t[Q_RANK + KV_RANK:MLA_IN],
                                  zeros[:LANES - QK_NOPE - QK_ROPE]], axis=0)
    win_conv_t = w_t[MLA_IN:].reshape(3, CONV_W // LANES, LANES, D_MODEL).transpose(1, 0, 2, 3)
    win_conv_t = win_conv_t.reshape(3 * CONV_W, D_MODEL)
    w_uq = _slots_to_cols(g_uq).reshape(Q_RANK, N_HEADS, QK_NOPE + QK_ROPE)
    wq = jnp.pad(w_uq, ((0, 0), (0, 0), (0, LANES - QK_NOPE - QK_ROPE))).reshape(Q_RANK, N_HEADS * LANES)
    w_ukv = _slots_to_cols(g_ukv).reshape(KV_RANK, N_HEADS, QK_NOPE + V_DIM)
    k_top = jnp.pad(w_ukv[:, :, :QK_NOPE], ((0, 0), (0, 0), (0, LANES - QK_NOPE))).reshape(KV_RANK, N_HEADS * LANES)
    v_top = w_ukv[:, :, QK_NOPE:].reshape(KV_RANK, N_HEADS * V_DIM)
    eye = jnp.pad(jnp.eye(QK_ROPE, dtype=BF16), ((QK_NOPE, LANES - QK_NOPE - QK_ROPE),) * 2)
    wk = jnp.concatenate([
        jnp.concatenate([k_top, v_top], axis=1),
        jnp.concatenate([jnp.tile(eye, (1, N_HEADS)), jnp.zeros((LANES, N_HEADS * V_DIM), BF16)], axis=1)], axis=0)
    return win_head_t, win_conv_t, wq, wk


def _pack_small_grads(d_head_t, d_conv_t, d_wq, d_wkk, d_wkv):
    d_conv_t = d_conv_t.reshape(CONV_W // LANES, 3, LANES, D_MODEL).transpose(1, 0, 2, 3).reshape(3 * CONV_W, D_MODEL)
    rope0 = Q_RANK + KV_RANK + QK_NOPE
    g_in_t = jnp.concatenate([d_head_t[:Q_RANK + KV_RANK], d_head_t[rope0:rope0 + QK_ROPE], d_conv_t], axis=0)
    g_in_t = g_in_t.reshape(N_DEV, -1, D_MODEL).astype(BF16)
    g_uq = d_wq.reshape(Q_RANK, N_HEADS, LANES)[:, :, :QK_NOPE + QK_ROPE].reshape(Q_RANK, -1)
    g_kn = d_wkk[:KV_RANK].reshape(KV_RANK, N_HEADS, LANES)[:, :, :QK_NOPE]
    g_v = d_wkv[:KV_RANK].reshape(KV_RANK, N_HEADS, V_DIM)
    g_ukv = jnp.concatenate([g_kn, g_v], axis=2).reshape(KV_RANK, -1)
    return [g_in_t] + [_cols_to_slots(g).astype(BF16) for g in (g_uq, g_ukv)]


def kernel(x, c, ctx, c_ctx, w_mod, b_mod, w_in, q_norm_g, w_uq, kv_norm_g, w_ukv, conv_w, w_out, w_mlp1, w_mlp2, final_norm_g, loss_target, m_c_ctx, m_w_mod, m_b_mod, m_w_in, m_q_norm_g, m_w_uq, m_kv_norm_g, m_w_ukv, m_conv_w, m_w_out, m_w_mlp1, m_w_mlp2, m_final_norm_g, v_c_ctx, v_w_mod, v_b_mod, v_w_in, v_q_norm_g, v_w_uq, v_kv_norm_g, v_w_ukv, v_conv_w, v_w_out, v_w_mlp1, v_w_mlp2, v_final_norm_g):
    me = _my_index()
    x2d, ctx2d, tgt = x[0], ctx[0], loss_target[0]
    s, l = x2d.shape[0], ctx2d.shape[0]
    t = s + l
    d = D_MODEL
    mod_cols = w_mod.shape[2]
    cw_cols = conv_w.shape[2]

    b_cols = lax.dynamic_slice(b_mod, (0, me * mod_cols), (1, mod_cols))
    cw_blk = jnp.pad(conv_w[0], ((0, 0), (0, mod_cols - cw_cols)))
    a_rows, gathered = _prologue(jnp.pad(c, ((0, 7), (0, 0))), c_ctx[None, :], w_mod[0], b_cols, cw_blk,
                                 "prologue")
    mod_mine = gathered[:, 0, :].reshape(1, 6 * d)
    mod_ctx = gathered[:, 1, :].reshape(1, 6 * d)
    cw_full = gathered[:, 2:5, :cw_cols].transpose(1, 0, 2).reshape(3, CONV_W)

    early = [w.astype(BF16) for w in (w_in[0].T, w_uq[0], w_ukv[0])]
    late = [w.astype(BF16) for w in (w_out[0], w_mlp1[0], w_mlp2[0])]
    h_all, (g_in, g_uq, g_ukv) = _modulate_all(x2d, ctx2d, mod_mine, mod_ctx, _RidingGather(early),
                                               "modulate1")
    win_head, win_conv, wq, wk = _unpack_small_weights(g_in, g_uq, g_ukv)
    wk_k, wk_v = wk[:, :N_HEADS * LANES], wk[:, N_HEADS * LANES:]
    cos, sgn = _rope_tables(s, l)

    tm_t = _pick(t, (1088, 768, 256))
    tk_t = _pick(t, (2176, 768, 256))
    z_head, cq, kv_in, qf, kv = _head_fwd(h_all, win_head, wq, wk, q_norm_g, kv_norm_g, cos, sgn, tm_t, "head_fwd")
    z_conv = _matmul(h_all, win_conv, mode="nt", name="in_proj_conv", m=s, tm=1024, tn=1536, tk=1024)
    attn, a_cat, stats, (g_out, w1, g_w2) = _attn_fwd(qf, kv, s, _RidingGather(late), "attn_fwd")
    wo = g_out.reshape(d, d)
    w2 = g_w2.reshape(D_FF, d)
    a_cat = _conv_fwd(z_conv, cw_full, a_cat, "conv_fwd")
    (o, x1, h2), _ = _matmul_rows(a_cat, wo, _epi_resid_modulate, mode="nn", name="out_proj", tm=1024, tk=1024,
                                  rows=[x2d], vecs=[(mod_mine, 2), (mod_mine, 3), (mod_mine, 4)],
                                  out_dtypes=[F32, F32, BF16])
    u1, act = _matmul(h2, w1, mode="nn", name="mlp_up", tm=4096, tk=1024, epilogue="relu2", slots="b_cols")
    (dx2, dm, fsums), _ = _matmul_rows(act, w2, _epi_final, mode="nn", name="mlp_down", tm=512, tk=4096,
                                       rows=[x1, tgt], vecs=[(mod_mine, 5), (final_norm_g[None, :], 0)],
                                       out_dtypes=[F32, BF16], sums=True)

    d_w2 = _matmul(act, dm, mode="tn", name="d_w_mlp2", out_dtype=BF16, tm=1024, tn=1024, tk=4096)
    du1 = _matmul(dm, w2, mode="nt", name="d_act", out_dtype=BF16, tm=2048, tn=1024, tk=1024,
                  epilogue="drelu2", extra=(u1,))
    d_w1 = _matmul(h2, du1, mode="tn", name="d_w_mlp1", out_dtype=BF16, tm=1024, tk=4096, slots="out")
    (dx1, do, sums2), _ = _matmul_rows(du1, w1, _epi_modulate2_bwd, mode="nt", name="d_h2", tm=512, tk=4096,
                                       slots="b_contract", rows=[x1, dx2, o], vecs=[(mod_mine, 4), (mod_mine, 2)],
                                       out_dtypes=[F32, BF16], sums=True)
    d_wo = _matmul(a_cat, do, mode="tn", name="d_w_out", out_dtype=BF16, tm=1024, tn=1024, tk=2048)
    da = _matmul(do, wo, mode="nt", name="d_a", tm=1024, tn=1024, tk=1024)
    dz_conv, d_cw = _conv_bwd(z_conv, cw_full, da, "conv_bwd")
    ready = [d_wo.reshape(N_DEV, d // N_DEV, d), d_w1, d_w2.reshape(N_DEV, D_FF // N_DEV, d)]
    dq, dk, dv, rode = _attn_bwd(qf, kv, attn, da, stats, cos, sgn, _Riding(ready), "attn_bwd")
    d_wq = _matmul(cq, dq, mode="tn", name="d_w_uq", k=s, tm=256, tn=1024, tk=4096)
    d_wkk = _matmul(kv_in, dk, mode="tn", name="d_w_ukv_k", tm=256, tn=1024, tk=tk_t)
    d_wkv = _matmul(kv_in, dv, mode="tn", name="d_w_ukv_v", tm=256, tn=512, tk=tk_t)
    head_args = (z_head, wq, wk_k, wk_v, win_head, q_norm_g, kv_norm_g, cos, sgn)
    dz_head, dh_head, psums = _head_bwd(dq, dk, dv, *head_args, "head_bwd", tile=HEAD_BWD_TILE, first_block=0,
                                        n_blocks=s // HEAD_BWD_TILE)
    dz_head, dh_head, psums_c = _head_bwd(None, dk, dv, *head_args, "head_bwd_ctx", tile=ROW_TILE,
                                          first_block=s // ROW_TILE, n_blocks=l // ROW_TILE,
                                          carry=(dz_head, dh_head))
    d_head = _matmul(dz_head, h_all, mode="tn", name="d_w_in_head", tm=512, tn=1024, tk=tk_t)
    d_conv = _matmul(dz_conv, h_all, mode="tn", name="d_w_in_conv", k=s, tm=1536, tn=1024, tk=2048)
    send = _pack_small_grads(d_head, d_conv, d_wq, d_wkk, d_wkv)
    (grad_x, sums1), got = _matmul_rows(dz_conv, win_conv, _epi_modulate1_bwd, mode="nn", name="d_h1", tm=max(s // 8, ROW_TILE),
                                        tk=win_conv.shape[0], rows=[dh_head, x2d, dx1], vecs=[(mod_mine, 1)],
                                        out_dtypes=[F32], sums=True, riding=_RidingReduce(send))
    sums1c = _modulate_sums(dh_head, s // ROW_TILE, ctx2d)

    small = _pack_small(sums1, sums2, fsums, sums1c, psums, psums_c, d_cw, mod_cols, "pack_small")
    (d_all,) = _all_gather([small], "gather_small_grads", True)
    d_ex = lax.dynamic_index_in_dim(d_all, me, axis=1, keepdims=False)
    d_ctx = lax.dynamic_index_in_dim(d_all, N_DEV + me, axis=1, keepdims=False)
    g_w_mod, dsil, dsum = _adaln_bwd(a_rows.T, w_mod[0], d_ex, d_ctx, d_all, "adaln_bwd")
    (dsil_all,) = _all_gather([dsil], "gather_d_cctx", True)
    loss = dsum[SMALL_LOSS, 0]
    g_cw = lax.dynamic_slice(dsum, (SMALL_CW, me * cw_cols), (3, cw_cols))

    slots = dict(zip(["w_in", "w_uq", "w_ukv"], got))
    slots.update(zip(["w_out", "w_mlp1", "w_mlp2"], rode))

    grads = {}
    weights = {"c_ctx": c_ctx, "w_mod": w_mod, "b_mod": b_mod, "w_in": w_in, "q_norm_g": q_norm_g, "w_uq": w_uq,
               "kv_norm_g": kv_norm_g, "w_ukv": w_ukv, "conv_w": conv_w, "w_out": w_out, "w_mlp1": w_mlp1,
               "w_mlp2": w_mlp2, "final_norm_g": final_norm_g}
    m_in = {"c_ctx": m_c_ctx, "w_mod": m_w_mod, "b_mod": m_b_mod, "w_in": m_w_in, "q_norm_g": m_q_norm_g,
            "w_uq": m_w_uq, "kv_norm_g": m_kv_norm_g, "w_ukv": m_w_ukv, "conv_w": m_conv_w, "w_out": m_w_out,
            "w_mlp1": m_w_mlp1, "w_mlp2": m_w_mlp2, "final_norm_g": m_final_norm_g}
    v_in = {"c_ctx": v_c_ctx, "w_mod": v_w_mod, "b_mod": v_b_mod, "w_in": v_w_in, "q_norm_g": v_q_norm_g,
            "w_uq": v_w_uq, "kv_norm_g": v_kv_norm_g, "w_ukv": v_w_ukv, "conv_w": v_conv_w, "w_out": v_w_out,
            "w_mlp1": v_w_mlp1, "w_mlp2": v_w_mlp2, "final_norm_g": v_final_norm_g}
    names = list(weights)
    small_names = ["c_ctx", "b_mod", "q_norm_g", "kv_norm_g", "final_norm_g", "conv_w"]
    delta, new_m, new_v = {}, {}, {}

    def as_rows(a):
        return a[None, :] if a.ndim == 1 else a

    small_out = _small_update(dsum, dsil_all, g_cw, [[as_rows(src[n]) for src in (weights, m_in, v_in)]
                                                      for n in small_names], "small_update")
    for n, outs in zip(small_names, small_out):
        grads[n], delta[n], new_m[n], new_v[n] = [a.reshape(weights[n].shape) for a in outs]
    for n in names:
        if n in small_names:
            continue
        if n == "w_in":
            wmv = [jnp.swapaxes(src[n], 1, 2) for src in (weights, m_in, v_in)]
            outs = _adamw(wmv[0], slots[n], wmv[1], wmv[2], "adamw_" + n, slots=True)
            grads[n], delta[n], new_m[n], new_v[n] = [jnp.swapaxes(a, 1, 2) for a in outs]
        elif n in slots:
            grads[n], delta[n], new_m[n], new_v[n] = _adamw(weights[n], slots[n], m_in[n], v_in[n], "adamw_" + n,
                                                            slots=True)
        else:
            delta[n], new_m[n], new_v[n] = _adamw(weights[n], g_w_mod, m_in[n], v_in[n], "adamw_" + n)
            grads[n] = g_w_mod[None]

    return (loss, grad_x[None], *[grads[n] for n in names], *[delta[n] for n in names],
            *[new_m[n] for n in names], *[new_v[n] for n in names])
```

```python
import math

import jax
import jax.numpy as jnp
import numpy as np
from jax import lax
from jax.experimental import pallas as pl
from jax.experimental.pallas import tpu as pltpu

F32 = jnp.float32
BF16 = jnp.bfloat16

D_MODEL = 1024
GRID_W = 64
N_HEADS = 8
QK_NOPE = 64
QK_ROPE = 32
V_DIM = 64
Q_RANK = 256
KV_RANK = 128
MLA_IN = Q_RANK + KV_RANK + QK_ROPE
CONV_W = 512
HEAD_COLS = 512
D_FF = 4096
ROPE_THETA = 10000.0
EPS = 1e-6
ATTN_SCALE = 1.0 / math.sqrt(QK_NOPE + QK_ROPE)
LOG2_E = 1.0 / math.log(2.0)
EXP2_SCALE = ATTN_SCALE * LOG2_E
N_DEV = 8
LANES = 128

ADAM_LR, ADAM_B1, ADAM_B2, ADAM_EPS, ADAM_WD, ADAM_STEP = 0.001, 0.9, 0.999, 1e-08, 0.01, 10

ROW_TILE = 256
HEAD_BWD_TILE = 1024
VMEM_BIG = 60 * 1024 * 1024


def _params(sem=None, vmem=None):
    return pltpu.CompilerParams(dimension_semantics=sem, vmem_limit_bytes=vmem)


def _pick(n, prefs):
    for p in prefs:
        if n % p == 0:
            return p
    return n


def _my_index():
    return 4 * lax.axis_index("x") + 2 * lax.axis_index("y") + lax.axis_index("c")


def _two_level_gather(x_refs, out_refs, send_sems, recv_sems, local_sems):
    n = len(x_refs)
    x, y, c = lax.axis_index("x"), lax.axis_index("y"), lax.axis_index("c")
    me, sibling = (x, y, c), (x, y, 1 - c)
    chips = [(1 - x, y), (x, 1 - y), (1 - x, 1 - y)]

    def slot(a, px, py, pc):
        return out_refs[a].at[4 * px + 2 * py + pc]

    def copy(a, k, block, to, src=None):
        return pltpu.make_async_remote_copy(
            src_ref=slot(a, *block) if src is None else src, dst_ref=slot(a, *block),
            send_sem=send_sems.at[7 * a + k], recv_sem=recv_sems.at[7 * a + k],
            device_id=to, device_id_type=pl.DeviceIdType.MESH)

    mine = [pltpu.make_async_copy(x_refs[a], slot(a, *me), local_sems.at[a]) for a in range(n)]
    first = [cp for a in range(n) for cp in
             [copy(a, 0, me, sibling, src=x_refs[a])]
             + [copy(a, 1 + j, me, (*chip, c), src=x_refs[a]) for j, chip in enumerate(chips)]]
    passed = [[copy(a, 4 + j, (*chip, c), sibling) for j, chip in enumerate(chips)] for a in range(n)]

    def start():
        for cp in mine + first:
            cp.start()

    def forward():
        for a in range(n):
            for j, chip in enumerate(chips):
                copy(a, 1 + j, (*chip, c), me).wait_recv()
                passed[a][j].start()

    def finish():
        for a in range(n):
            copy(a, 0, sibling, me).wait_recv()
            for j, chip in enumerate(chips):
                copy(a, 4 + j, (*chip, 1 - c), me).wait_recv()
        for cp in first + [cp for per_array in passed for cp in per_array]:
            cp.wait_send()
        for cp in mine:
            cp.wait()

    return start, forward, finish


def _direct_gather(src_ref, dst_ref, send_sems, recv_sems, per_peer=False):
    x, y, c = lax.axis_index("x"), lax.axis_index("y"), lax.axis_index("c")
    me = 4 * x + 2 * y + c
    dst_ref[me] = src_ref[me] if per_peer else src_ref[...]
    sends, landings = [], []
    for k in range(1, N_DEV):
        peer = (1 - x if k & 4 else x, 1 - y if k & 2 else y, 1 - c if k & 1 else c)
        pid = 4 * peer[0] + 2 * peer[1] + peer[2]
        for dst, out in ((me, sends), (pid, landings)):
            out.append(pltpu.make_async_remote_copy(
                src_ref=src_ref.at[pid] if per_peer else src_ref, dst_ref=dst_ref.at[dst],
                send_sem=send_sems.at[k - 1], recv_sem=recv_sems.at[k - 1],
                device_id=peer, device_id_type=pl.DeviceIdType.MESH))
    for cp in sends:
        cp.start()

    def finish():
        for cp in landings:
            cp.wait_recv()
        for cp in sends:
            cp.wait_send()

    return finish


def _all_gather(arrays, name, in_vmem):
    space = pltpu.VMEM if in_vmem else pl.ANY
    n = len(arrays)

    def body(*refs):
        for phase in _two_level_gather(refs[:n], refs[n:2 * n], *refs[2 * n:]):
            phase()

    outs = pl.pallas_call(
        body, name=name,
        out_shape=tuple(jax.ShapeDtypeStruct((N_DEV,) + a.shape, a.dtype) for a in arrays),
        in_specs=[pl.BlockSpec(memory_space=space)] * n,
        out_specs=tuple(pl.BlockSpec(memory_space=space) for _ in arrays),
        scratch_shapes=[pltpu.SemaphoreType.DMA((7 * n,)), pltpu.SemaphoreType.DMA((7 * n,)),
                        pltpu.SemaphoreType.DMA((n,))],
    )(*arrays)
    return list(outs)


class _Riding:
    def __init__(self, arrays=()):
        self.arrays, self.n = list(arrays), len(arrays)
        self.out_shape = [jax.ShapeDtypeStruct(a.shape, a.dtype) for a in self.arrays]
        self.specs = [pl.BlockSpec(memory_space=pl.ANY)] * self.n
        self.scratch = [pltpu.SemaphoreType.DMA((7 * self.n,)), pltpu.SemaphoreType.DMA((7 * self.n,)),
                        pltpu.SemaphoreType.DMA((self.n,))]

    def copies(self, x_refs, y_refs, send_sems, recv_sems, local_sems):
        x, y, c = lax.axis_index("x"), lax.axis_index("y"), lax.axis_index("c")
        me = 4 * x + 2 * y + c
        local, sends, landings = [], [], []
        for a in range(self.n):
            local.append(pltpu.make_async_copy(x_refs[a].at[me], y_refs[a].at[me], local_sems.at[a]))
            for k in range(1, N_DEV):
                peer = (1 - x if k & 4 else x, 1 - y if k & 2 else y, 1 - c if k & 1 else c)
                pid = 4 * peer[0] + 2 * peer[1] + peer[2]
                for dst, out in ((me, sends), (pid, landings)):
                    out.append(pltpu.make_async_remote_copy(
                        src_ref=x_refs[a].at[pid], dst_ref=y_refs[a].at[dst],
                        send_sem=send_sems.at[7 * a + k - 1], recv_sem=recv_sems.at[7 * a + k - 1],
                        device_id=peer, device_id_type=pl.DeviceIdType.MESH))
        return local, sends, landings

    def run(self, first, last, x_refs, y_refs, sems, middle=None):
        if self.n == 0:
            return None
        local, sends, landings = self.copies(x_refs, y_refs, *sems)

        @pl.when(first)
        def _():
            for cp in local + sends:
                cp.start()

        return local, sends, landings, last

    @staticmethod
    def finish(state):
        if state is None:
            return
        local, sends, landings, last = state

        @pl.when(last)
        def _():
            for cp in landings:
                cp.wait_recv()
            for cp in sends:
                cp.wait_send()
            for cp in local:
                cp.wait()


class _RidingGather:
    def __init__(self, arrays):
        self.arrays, self.n = list(arrays), len(arrays)
        self.out_shape = [jax.ShapeDtypeStruct((N_DEV,) + a.shape, a.dtype) for a in self.arrays]
        self.specs = [pl.BlockSpec(memory_space=pl.ANY)] * self.n
        self.scratch = [pltpu.SemaphoreType.DMA((7 * self.n,)), pltpu.SemaphoreType.DMA((7 * self.n,)),
                        pltpu.SemaphoreType.DMA((self.n,))]

    def run(self, first, last, x_refs, y_refs, sems, middle):
        start, forward, finish = _two_level_gather(x_refs, y_refs, *sems)
        pl.when(first)(start)
        pl.when(middle)(forward)
        return finish, last

    @staticmethod
    def finish(state):
        finish, last = state
        pl.when(last)(finish)


class _RidingReduce:
    def __init__(self, arrays):
        self.arrays, self.n = list(arrays), len(arrays)
        self.out_shape = [jax.ShapeDtypeStruct((4,) + a.shape[1:], a.dtype) for a in self.arrays]
        self.specs = [pl.BlockSpec(memory_space=pl.ANY)] * self.n
        self.scratch = [pltpu.VMEM((4,) + a.shape[1:], a.dtype) for a in self.arrays for _ in range(3)]
        self.scratch += [pltpu.SemaphoreType.DMA((self.n,)) for _ in range(6)]

    def run(self, first, last, x_refs, y_refs, scratch, middle):
        n = self.n
        own, sib, tot = scratch[0:3 * n:3], scratch[1:3 * n:3], scratch[2:3 * n:3]
        d2d_send, d2d_recv, local_in, ici_send, ici_recv, local_out = scratch[3 * n:]
        x, y, c = lax.axis_index("x"), lax.axis_index("y"), lax.axis_index("c")
        my_chip = 2 * x + y
        sibling = (x, y, 1 - c)
        others = [(1 - x, y), (x, 1 - y), (1 - x, 1 - y)]

        def to_sibling(a, j=None):
            src = x_refs[a].at[pl.ds(0, 4)] if j is None else x_refs[a].at[2 * j + 1 - c]
            dst = sib[a] if j is None else sib[a].at[j]
            return pltpu.make_async_remote_copy(src_ref=src, dst_ref=dst, send_sem=d2d_send.at[a],
                                                recv_sem=d2d_recv.at[a], device_id=sibling,
                                                device_id_type=pl.DeviceIdType.MESH)

        def mine_in(a, j=None):
            src = x_refs[a].at[pl.ds(0, 4)] if j is None else x_refs[a].at[2 * j + c]
            return pltpu.make_async_copy(src, own[a] if j is None else own[a].at[j], local_in.at[a])

        def to_chip(a, chip=None):
            if chip is None:
                src, dst, peer = tot[a].at[pl.ds(0, 3)], y_refs[a].at[pl.ds(0, 3)], sibling
            else:
                src, dst, peer = tot[a].at[2 * chip[0] + chip[1]], y_refs[a].at[my_chip], (*chip, c)
            return pltpu.make_async_remote_copy(src_ref=src, dst_ref=dst, send_sem=ici_send.at[a],
                                                recv_sem=ici_recv.at[a], device_id=peer,
                                                device_id_type=pl.DeviceIdType.MESH)

        def mine_out(a):
            return pltpu.make_async_copy(tot[a].at[my_chip], y_refs[a].at[my_chip], local_out.at[a])

        @pl.when(first)
        def _():
            for a in range(n):
                for j in range(4):
                    to_sibling(a, j).start()
                    mine_in(a, j).start()

        @pl.when(middle)
        def _():
            for a in range(n):
                to_sibling(a).wait_recv()
                to_sibling(a).wait_send()
                mine_in(a).wait()
                tot[a][...] = (own[a][...].astype(F32) + sib[a][...].astype(F32)).astype(tot[a].dtype)
                for chip in others:
                    to_chip(a, chip).start()
                mine_out(a).start()

        def finish():
            @pl.when(last)
            def _():
                for a in range(n):
                    to_chip(a).wait_recv()
                    to_chip(a).wait_send()
                    mine_out(a).wait()

        return finish

    @staticmethod
    def finish(state):
        state()


_DIMS ={"nn": (((1,), (0,)), ((), ())), "nt": (((1,), (1,)), ((), ())), "tn": (((0,), (0,)), ((), ()))}
NT_DIMS = _DIMS["nt"]
TN_DIMS = _DIMS["tn"]


def _swap8(x):
    lane = lax.broadcasted_iota(jnp.int32, x.shape, 1)
    return jnp.where((lane & 15) < 8, pltpu.roll(x, LANES - 8, 1), pltpu.roll(x, 8, 1))


def _rope(x, cos, sgn, bwd):
    return x * cos + (_swap8(x * sgn) if bwd else _swap8(x) * sgn)


def _matmul(a, b, *, mode, name, out_dtype=F32, tm=512, tn=512, tk=512, m=None, k=None,
            epilogue=None, extra=(), slots=None):
    if mode == "nn":
        m = a.shape[0] if m is None else m
        k = a.shape[1]
        n = N_DEV * b.shape[2] if slots == "b_cols" else b.shape[1]
    elif mode == "nt":
        m = a.shape[0] if m is None else m
        k = a.shape[1]
        n = b.shape[0]
    else:
        k = a.shape[0] if k is None else k
        m, n = a.shape[1], b.shape[1]
    tm, tn, tk = min(tm, m), min(tn, n), min(tk, k)
    if slots == "b_cols":
        tn = b.shape[2]
    if slots == "out":
        tn = n // N_DEV
    assert m % tm == 0 and n % tn == 0 and k % tk == 0, (name, m, n, k, tm, tn, tk)
    nk = k // tk
    dims = _DIMS[mode]
    a_spec = (pl.BlockSpec((tk, tm), lambda i, j, kk: (kk, i)) if mode == "tn"
              else pl.BlockSpec((tm, tk), lambda i, j, kk: (i, kk)))
    if slots == "b_cols":
        b_spec = pl.BlockSpec((None, tk, tn), lambda i, j, kk: (j, kk, 0))
    elif mode == "nt":
        b_spec = pl.BlockSpec((tn, tk), lambda i, j, kk: (j, kk))
    else:
        b_spec = pl.BlockSpec((tk, tn), lambda i, j, kk: (kk, j))
    tile = pl.BlockSpec((tm, tn), lambda i, j, kk: (i, j))
    if slots == "out":
        o_spec = pl.BlockSpec((None, tm, tn), lambda i, j, kk: (j, i, 0))
        o_shape = (N_DEV, m, tn)
    else:
        o_spec, o_shape = tile, (m, n)
    in_specs, args = [a_spec, b_spec], [a, b]
    if epilogue == "drelu2":
        in_specs.append(tile)
    args += list(extra)
    if epilogue == "relu2":
        out_shape = (jax.ShapeDtypeStruct(o_shape, BF16), jax.ShapeDtypeStruct(o_shape, BF16))
        out_specs = (o_spec, o_spec)
    else:
        out_shape = jax.ShapeDtypeStruct(o_shape, out_dtype)
        out_specs = o_spec
    n_in = len(args)
    n_out = 2 if epilogue == "relu2" else 1

    def body(*refs):
        a_ref, b_ref = refs[0], refs[1]
        outs = refs[n_in:n_in + n_out]
        part = lax.dot_general(a_ref[...], b_ref[...], dims, preferred_element_type=F32)

        def finish(acc):
            if epilogue == "relu2":
                outs[0][...] = acc.astype(BF16)
                r = jnp.maximum(acc, 0.0)
                outs[1][...] = (r * r).astype(BF16)
            elif epilogue == "drelu2":
                u = refs[2][...].astype(F32)
                outs[0][...] = (acc * (2.0 * jnp.maximum(u, 0.0))).astype(out_dtype)
            else:
                outs[0][...] = acc.astype(out_dtype)

        if nk == 1:
            finish(part)
        else:
            acc_ref = refs[n_in + n_out]
            kk = pl.program_id(2)

            @pl.when(kk == 0)
            def _():
                acc_ref[...] = part

            @pl.when(kk > 0)
            def _():
                acc_ref[...] += part

            @pl.when(kk == nk - 1)
            def _():
                finish(acc_ref[...])

    return pl.pallas_call(
        body, name=name, grid=(m // tm, n // tn, nk),
        out_shape=out_shape, in_specs=in_specs, out_specs=out_specs,
        scratch_shapes=[pltpu.VMEM((tm, tn), F32)] if nk > 1 else [],
        compiler_params=_params(("parallel", "parallel", "arbitrary"), VMEM_BIG),
    )(*args)


def _rstd(x):
    return lax.rsqrt(jnp.mean(x * x, axis=1, keepdims=True) + EPS)


def _norm_bwd(dxn, xn, r):
    return r * (dxn - xn * jnp.mean(dxn * xn, axis=1, keepdims=True))


def _vec(col):
    return pl.BlockSpec((1, D_MODEL), lambda i: (0, col))


def _matmul_rows(a, b, epi, *, mode, name, tm, tk, rows=(), vecs=(), out_dtypes=(), sums=False, slots=None,
                 riding=None):
    m, k = a.shape
    n = D_MODEL
    tm, tk = min(tm, m), min(tk, k)
    riding = riding or _Riding()
    group = 1
    if slots == "b_contract":
        group = max(1, tk // b.shape[2])
        tk = group * b.shape[2]
        b_spec = pl.BlockSpec((group, n, tk // group), lambda i, kk: (kk, 0, 0))
    elif mode == "nt":
        b_spec = pl.BlockSpec((n, tk), lambda i, kk: (0, kk))
    else:
        b_spec = pl.BlockSpec((tk, n), lambda i, kk: (kk, 0))
    assert m % tm == 0 and k % tk == 0, (name, m, k, tm, tk)
    ni, nk = m // tm, k // tk
    assert ni >= 2 or not isinstance(riding, _RidingReduce), "the two-level exchange needs a middle grid step"
    dims = _DIMS[mode]
    tile = pl.BlockSpec((tm, n), lambda i, kk: (i, 0))
    in_specs = [pl.BlockSpec((tm, tk), lambda i, kk: (i, kk)), b_spec] + [tile] * len(rows)
    in_specs += [pl.BlockSpec((1, n), lambda i, kk, col=col: (0, col)) for _, col in vecs]
    args = [a, b, *rows, *[v for v, _ in vecs]]
    out_shape = [jax.ShapeDtypeStruct((m, n), dt) for dt in out_dtypes]
    out_specs = [tile] * len(out_dtypes)
    if sums:
        out_shape.append(jax.ShapeDtypeStruct((8, n), F32))
        out_specs.append(pl.BlockSpec((8, n), lambda i, kk: (0, 0)))
    n_rows, n_vecs, n_outs, nr = len(rows), len(vecs), len(out_dtypes), riding.n
    n_in = 2 + n_rows + n_vecs

    def body(*refs):
        a_ref, b_ref = refs[0], refs[1]
        row_refs = refs[2:2 + n_rows]
        vec_refs = refs[2 + n_rows:n_in]
        x_refs = refs[n_in:n_in + nr]
        out_refs = refs[n_in + nr:n_in + nr + n_outs]
        pos = n_in + nr + n_outs
        sums_ref = refs[pos] if sums else None
        pos += 1 if sums else 0
        y_refs = refs[pos:pos + nr]
        pos += nr
        acc_ref = refs[pos] if nk > 1 else None
        sem_refs = refs[pos + (1 if nk > 1 else 0):]
        i, kk = pl.program_id(0), pl.program_id(1)
        state = riding.run((i == 0) & (kk == 0), (i == ni - 1) & (kk == nk - 1), x_refs, y_refs, sem_refs,
                           middle=(i == 1) & (kk == 0))
        if slots == "b_contract":
            c = tk // group
            part = lax.dot_general(a_ref[:, 0:c], b_ref[0], dims, preferred_element_type=F32)
            for u in range(1, group):
                part = part + lax.dot_general(a_ref[:, u * c:(u + 1) * c], b_ref[u], dims, preferred_element_type=F32)
        else:
            part = lax.dot_general(a_ref[...], b_ref[...], dims, preferred_element_type=F32)

        def finish(acc):
            nsub = tm // ROW_TILE
            for r in range(nsub):
                blk = pl.ds(r * ROW_TILE, ROW_TILE)
                epi(acc[r * ROW_TILE:(r + 1) * ROW_TILE], [ref.at[blk] for ref in row_refs], vec_refs,
                    [ref.at[blk] for ref in out_refs], sums_ref,
                    (i == 0) if r == 0 else None, (i == ni - 1) if r == nsub - 1 else None)

        if nk == 1:
            finish(part)
        else:
            @pl.when(kk == 0)
            def _():
                acc_ref[...] = part

            @pl.when(kk > 0)
            def _():
                acc_ref[...] += part

            @pl.when(kk == nk - 1)
            def _():
                finish(acc_ref)

        riding.finish(state)

    outs = pl.pallas_call(
        body, name=name, grid=(ni, nk),
        out_shape=(*out_shape, *riding.out_shape),
        in_specs=[*in_specs, *riding.specs], out_specs=(*out_specs, *riding.specs),
        scratch_shapes=([pltpu.VMEM((tm, n), F32)] if nk > 1 else []) + (riding.scratch if nr else []),
        compiler_params=_params(("arbitrary", "arbitrary"), VMEM_BIG),
    )(*args, *riding.arrays)
    n_own = len(out_shape)
    return list(outs[:n_own]), list(outs[n_own:])


def _zero_sums_at_start(sums_ref, first):
    if first is not None:
        @pl.when(first)
        def _():
            sums_ref[...] = jnp.zeros_like(sums_ref)


def _epi_resid_modulate(acc, rows, vecs, outs, sums_ref, first, last):
    (x_ref,), (g_ref, sh_ref, sc_ref) = rows, vecs
    x1 = x_ref[...] + g_ref[...] * acc
    outs[0][...] = acc
    outs[1][...] = x1
    outs[2][...] = (x1 * _rstd(x1) * (1.0 + sc_ref[...]) + sh_ref[...]).astype(BF16)


def _epi_final(acc, rows, vecs, outs, sums_ref, first, last):
    (x1_ref, t_ref), (g_ref, gf_ref) = rows, vecs
    d = acc.shape[1]
    x2 = x1_ref[...] + g_ref[...] * acc
    r = _rstd(x2)
    xn = x2 * r
    err = xn * gf_ref[...] - t_ref[...]
    dy = err * (1.0 / d)
    dx2 = _norm_bwd(dy * gf_ref[...], xn, r)
    outs[0][...] = dx2
    outs[1][...] = (dx2 * g_ref[...]).astype(BF16)
    _zero_sums_at_start(sums_ref, first)
    sums_ref[0:1, :] += jnp.sum(dy * xn, axis=0, keepdims=True)
    sums_ref[1:2, :] += jnp.sum(dx2 * acc, axis=0, keepdims=True)
    sums_ref[2:3, :] += jnp.sum(err * err, axis=0, keepdims=True)

    if last is not None:
        @pl.when(last)
        def _():
            tot = jnp.sum(sums_ref[2:3, :], axis=1, keepdims=True) * (0.5 / d)
            sums_ref[3:4, :] = jnp.broadcast_to(tot, (1, d))


def _epi_modulate2_bwd(acc, rows, vecs, outs, sums_ref, first, last):
    (x_ref, dres_ref, o_ref), (sc_ref, g_ref) = rows, vecs
    x = x_ref[...]
    r = _rstd(x)
    xn = x * r
    dx = dres_ref[...] + _norm_bwd(acc * (1.0 + sc_ref[...]), xn, r)
    outs[0][...] = dx
    outs[1][...] = (dx * g_ref[...]).astype(BF16)
    _zero_sums_at_start(sums_ref, first)
    sums_ref[0:1, :] += jnp.sum(acc * xn, axis=0, keepdims=True)
    sums_ref[1:2, :] += jnp.sum(acc, axis=0, keepdims=True)
    sums_ref[2:3, :] += jnp.sum(dx * o_ref[...], axis=0, keepdims=True)


def _epi_modulate1_bwd(acc, rows, vecs, outs, sums_ref, first, last):
    (add_ref, x_ref, dres_ref), (sc_ref,) = rows, vecs
    dh = acc + add_ref[...]
    x = x_ref[...]
    r = _rstd(x)
    xn = x * r
    outs[0][...] = dres_ref[...] + _norm_bwd(dh * (1.0 + sc_ref[...]), xn, r)
    _zero_sums_at_start(sums_ref, first)
    sums_ref[0:1, :] += jnp.sum(dh * xn, axis=0, keepdims=True)
    sums_ref[1:2, :] += jnp.sum(dh, axis=0, keepdims=True)


def _modulate_all(x, ctx, mod, mod_ctx, riding, name):
    s, d = x.shape
    t = s + ctx.shape[0]
    ns = s // ROW_TILE
    nc = ctx.shape[0] // ROW_TILE
    nr = riding.n

    def body(*refs):
        x_ref, c_ref, sh_ref, sc_ref, shc_ref, scc_ref = refs[:6]
        h_ref = refs[6 + nr]
        i = pl.program_id(0)
        state = riding.run(i == 0, i == ns + nc - 1, refs[6:6 + nr], refs[7 + nr:7 + 2 * nr], refs[7 + 2 * nr:],
                           middle=i == ns + nc - 3)

        @pl.when(i < ns)
        def _():
            v = x_ref[...]
            h_ref[...] = (v * _rstd(v) * (1.0 + sc_ref[...]) + sh_ref[...]).astype(BF16)

        @pl.when(i >= ns)
        def _():
            v = c_ref[...]
            h_ref[...] = (v * _rstd(v) * (1.0 + scc_ref[...]) + shc_ref[...]).astype(BF16)

        riding.finish(state)

    outs = pl.pallas_call(
        body, name=name, grid=(ns + nc,),
        out_shape=(jax.ShapeDtypeStruct((t, d), BF16), *riding.out_shape),
        in_specs=[pl.BlockSpec((ROW_TILE, d), lambda i: (jnp.minimum(i, ns - 1), 0)),
                  pl.BlockSpec((ROW_TILE, d), lambda i: (jnp.maximum(i - ns, 0), 0)),
                  _vec(0), _vec(1), _vec(0), _vec(1), *riding.specs],
        out_specs=(pl.BlockSpec((ROW_TILE, d), lambda i: (i, 0)), *riding.specs),
        scratch_shapes=riding.scratch,
        compiler_params=_params(("arbitrary",)),
    )(x, ctx, mod, mod, mod_ctx, mod_ctx, *riding.arrays)
    return outs[0], list(outs[1:])


def _modulate_sums(dh, row_off, xsrc):
    s, d = xsrc.shape

    def body(dh_ref, x_ref, sums_ref):
        i = pl.program_id(0)
        x = x_ref[...]
        dhv = dh_ref[...]

        @pl.when(i == 0)
        def _():
            sums_ref[...] = jnp.zeros_like(sums_ref)

        sums_ref[0:1, :] += jnp.sum(dhv * (x * _rstd(x)), axis=0, keepdims=True)
        sums_ref[1:2, :] += jnp.sum(dhv, axis=0, keepdims=True)

    return pl.pallas_call(
        body, name="modulate1_ctx_bwd", grid=(s // ROW_TILE,),
        out_shape=jax.ShapeDtypeStruct((8, d), F32),
        in_specs=[pl.BlockSpec((ROW_TILE, d), lambda i: (i + row_off, 0)), pl.BlockSpec((ROW_TILE, d), lambda i: (i, 0))],
        out_specs=pl.BlockSpec((8, d), lambda i: (0, 0)),
        compiler_params=_params(("arbitrary",)),
    )(dh, xsrc)


def _head_fwd(h_all, win_head, wq, wk, q_gain, kv_gain, cos, sgn, tm, name):
    t, d = h_all.shape
    nq, nkv = wq.shape[1], wk.shape[1]

    def body(h_ref, wi_ref, wq_ref, wk_ref, qg_ref, kg_ref, c_ref, s_ref, z_ref, cq_ref, kvin_ref, qf_ref, kv_ref):
        z = lax.dot_general(h_ref[...], wi_ref[...], NT_DIMS, preferred_element_type=F32)
        z_ref[...] = z
        cos, sgn = c_ref[...], s_ref[...]
        zq = z[:, 0:Q_RANK]
        cq = (zq * _rstd(zq) * qg_ref[...]).astype(BF16)
        cq_ref[...] = cq
        zk = z[:, Q_RANK:Q_RANK + KV_RANK]
        kv_in = jnp.concatenate([(zk * _rstd(zk) * kg_ref[...]).astype(BF16),
                                 _rope(z[:, Q_RANK + KV_RANK:HEAD_COLS], cos, sgn, False).astype(BF16)], axis=1)
        kvin_ref[...] = kv_in
        q = jnp.dot(cq, wq_ref[...], preferred_element_type=F32)
        for h in range(nq // LANES):
            sl = slice(h * LANES, (h + 1) * LANES)
            qf_ref[:, sl] = _rope(q[:, sl], cos, sgn, False).astype(BF16)
        kv_ref[...] = jnp.dot(kv_in, wk_ref[...], preferred_element_type=F32).astype(BF16)

    def row(w):
        return pl.BlockSpec((tm, w), lambda i: (i, 0))

    def whole(a):
        return pl.BlockSpec(a.shape, lambda i: (0, 0))

    return pl.pallas_call(
        body, name=name, grid=(t // tm,),
        out_shape=(jax.ShapeDtypeStruct((t, HEAD_COLS), F32), jax.ShapeDtypeStruct((t, Q_RANK), BF16),
                   jax.ShapeDtypeStruct((t, KV_RANK + LANES), BF16), jax.ShapeDtypeStruct((t, nq), BF16),
                   jax.ShapeDtypeStruct((t, nkv), BF16)),
        in_specs=[row(d), whole(win_head), whole(wq), whole(wk), whole(q_gain), whole(kv_gain), row(LANES), row(LANES)],
        out_specs=(row(HEAD_COLS), row(Q_RANK), row(KV_RANK + LANES), row(nq), row(nkv)),
        compiler_params=_params(("parallel",), VMEM_BIG),
    )(h_all, win_head, wq, wk, q_gain, kv_gain, cos, sgn)


def _head_bwd(dq, dk, dv, z, wq, wk_k, wk_v, win_head, q_gain, kv_gain, cos, sgn, name, *, tile, first_block,
              n_blocks, carry=None):
    t = z.shape[0]
    with_q = dq is not None

    def body(*refs):
        it = iter(refs)
        dq_ref = next(it) if with_q else None
        dk_ref, dv_ref, z_ref, wq_ref, wkk_ref, wkv_ref, wi_ref, qg_ref, kg_ref, c_ref, s_ref = (next(it) for _ in range(11))
        if carry is not None:
            next(it), next(it)
        dz_ref, dh_ref, sums_ref = next(it), next(it), next(it)
        i = pl.program_id(0)

        @pl.when(i == 0)
        def _():
            sums_ref[...] = jnp.zeros_like(sums_ref)

        if with_q:
            dc = lax.dot_general(dq_ref[...], wq_ref[...], NT_DIMS, preferred_element_type=F32)
            zq = z_ref[:, 0:Q_RANK]
            r = _rstd(zq)
            zn = zq * r
            sums_ref[0:1, :] += jnp.sum(dc * zn, axis=0, keepdims=True)
            dz_ref[:, 0:Q_RANK] = _norm_bwd(dc * qg_ref[...], zn, r).astype(BF16)
        else:
            dz_ref[:, 0:Q_RANK] = jnp.zeros((tile, Q_RANK), BF16)
        dkv = (lax.dot_general(dk_ref[...], wkk_ref[...], NT_DIMS, preferred_element_type=F32)
               + lax.dot_general(dv_ref[...], wkv_ref[...], NT_DIMS, preferred_element_type=F32))
        zk = z_ref[:, Q_RANK:Q_RANK + KV_RANK]
        r = _rstd(zk)
        zn = zk * r
        dc = dkv[:, 0:KV_RANK]
        sums_ref[1:2, 0:KV_RANK] += jnp.sum(dc * zn, axis=0, keepdims=True)
        dz_ref[:, Q_RANK:Q_RANK + KV_RANK] = _norm_bwd(dc * kg_ref[...], zn, r).astype(BF16)
        dz_ref[:, Q_RANK + KV_RANK:HEAD_COLS] = _rope(dkv[:, KV_RANK:KV_RANK + LANES], c_ref[...], s_ref[...],
                                                       True).astype(BF16)
        dh_ref[...] = jnp.dot(dz_ref[...], wi_ref[...], preferred_element_type=F32)

    def row(w):
        return pl.BlockSpec((tile, w), lambda i: (i + first_block, 0))

    def whole(a):
        return pl.BlockSpec(a.shape, lambda i: (0, 0))

    args = ([dq] if with_q else []) + [dk, dv, z, wq, wk_k, wk_v, win_head, q_gain, kv_gain, cos, sgn]
    in_specs = ([row(dq.shape[1])] if with_q else []) + [
        row(dk.shape[1]), row(dv.shape[1]), row(HEAD_COLS), whole(wq), whole(wk_k), whole(wk_v),
        whole(win_head), whole(q_gain), whole(kv_gain), row(LANES), row(LANES)]
    aliases = {}
    if carry is not None:
        aliases = {len(args): 0, len(args) + 1: 1}
        args += list(carry)
        in_specs += [pl.BlockSpec(memory_space=pl.ANY)] * 2
    return pl.pallas_call(
        body, name=name, grid=(n_blocks,),
        out_shape=(jax.ShapeDtypeStruct((t, HEAD_COLS), BF16), jax.ShapeDtypeStruct((t, D_MODEL), F32),
                   jax.ShapeDtypeStruct((8, Q_RANK), F32)),
        in_specs=in_specs,
        out_specs=(row(HEAD_COLS), row(D_MODEL), pl.BlockSpec((8, Q_RANK), lambda i: (0, 0))),
        input_output_aliases=aliases,
        compiler_params=_params(("arbitrary",), VMEM_BIG),
    )(*args)


def _shift_rows(u, s):
    rowi = lax.broadcasted_iota(jnp.int32, u.shape, 0)
    prev = jnp.where(rowi == 0, 0.0, pltpu.roll(u, 1, 0))
    nxt = jnp.where(rowi == s - 1, 0.0, pltpu.roll(u, s - 1, 0))
    return prev, nxt


def _conv_fwd(z_conv, cw, a_cat, name):
    s = z_conv.shape[0]

    def body(z_ref, w_ref, a_in_ref, o_ref):
        del a_in_ref
        gb, gc, xv = z_ref[:, 0:LANES], z_ref[:, LANES:2 * LANES], z_ref[:, 2 * LANES:3 * LANES]
        u = gc * xv
        prev, nxt = _shift_rows(u, s)
        y = w_ref[0:1, :] * prev + w_ref[1:2, :] * u + w_ref[2:3, :] * nxt
        o_ref[...] = (gb * y).astype(BF16)

    return pl.pallas_call(
        body, name=name, grid=(CONV_W // LANES,),
        out_shape=jax.ShapeDtypeStruct(a_cat.shape, a_cat.dtype),
        in_specs=[pl.BlockSpec((s, 3 * LANES), lambda j: (0, j)), pl.BlockSpec((3, LANES), lambda j: (0, j)),
                  pl.BlockSpec(memory_space=pl.ANY)],
        out_specs=pl.BlockSpec((s, LANES), lambda j: (0, 4 + j)),
        input_output_aliases={2: 0},
        compiler_params=_params(("parallel",), VMEM_BIG),
    )(z_conv, cw, a_cat)


def _conv_bwd(z_conv, cw, da, name):
    s = z_conv.shape[0]

    def body(z_ref, w_ref, da_ref, dz_ref, dw_ref):
        gb, gc, xv = z_ref[:, 0:LANES], z_ref[:, LANES:2 * LANES], z_ref[:, 2 * LANES:3 * LANES]
        u = gc * xv
        prev, nxt = _shift_rows(u, s)
        dcv = da_ref[...]
        dz_ref[:, 0:LANES] = (dcv * (w_ref[0:1, :] * prev + w_ref[1:2, :] * u + w_ref[2:3, :] * nxt)).astype(BF16)
        dy = dcv * gb
        dw_ref[0:1, :] = jnp.sum(dy * prev, axis=0, keepdims=True)
        dw_ref[1:2, :] = jnp.sum(dy * u, axis=0, keepdims=True)
        dw_ref[2:3, :] = jnp.sum(dy * nxt, axis=0, keepdims=True)
        dyp, dyn = _shift_rows(dy, s)
        du = w_ref[0:1, :] * dyn + w_ref[1:2, :] * dy + w_ref[2:3, :] * dyp
        dz_ref[:, LANES:2 * LANES] = (du * xv).astype(BF16)
        dz_ref[:, 2 * LANES:3 * LANES] = (du * gc).astype(BF16)

    blk = pl.BlockSpec((s, 3 * LANES), lambda j: (0, j))
    cws = pl.BlockSpec((3, LANES), lambda j: (0, j))
    return pl.pallas_call(
        body, name=name, grid=(CONV_W // LANES,),
        out_shape=(jax.ShapeDtypeStruct(z_conv.shape, BF16), jax.ShapeDtypeStruct((3, CONV_W), F32)),
        in_specs=[blk, cws, pl.BlockSpec((s, LANES), lambda j: (0, 4 + j))], out_specs=(blk, cws),
        compiler_params=_params(("parallel",), VMEM_BIG),
    )(z_conv, cw, da)


ATT_TQ = 512
ATT_Q_STEP = 1024
ATT_TQ_BWD = 512


def _head_mask(shape, hh):
    lane = lax.broadcasted_iota(jnp.int32, shape, 1)
    return (lane >= hh * V_DIM) & (lane < (hh + 1) * V_DIM)


def _attn_fwd(qf, kv, s, riding, name):
    t = kv.shape[0]
    step = min(ATT_Q_STEP, s)
    nq = s // step
    nr = riding.n

    def body(*refs):
        q_ref, k_ref, v_ref = refs[:3]
        o_ref, ob_ref, st_ref = refs[3 + nr:6 + nr]
        p, i = pl.program_id(0), pl.program_id(1)
        state = riding.run((p == 0) & (i == 0), (p == N_HEADS // 2 - 1) & (i == nq - 1),
                           refs[3:3 + nr], refs[6 + nr:6 + 2 * nr], refs[6 + 2 * nr:],
                           middle=(p == N_HEADS // 2 - 2) & (i == nq // 2))
        v = v_ref[...]
        vlane = lax.broadcasted_iota(jnp.int32, v.shape, 1)
        one_lane = [(1 - hh) * V_DIM for hh in range(2)]
        vm = [jnp.where(_head_mask(v.shape, hh), v, jnp.where(vlane == one_lane[hh], 1.0, 0.0).astype(BF16))
              for hh in range(2)]

        def block(r, carry):
            rows = pl.ds(pl.multiple_of(r * ATT_TQ, ATT_TQ), ATT_TQ)
            olane = lax.broadcasted_iota(jnp.int32, (ATT_TQ, LANES), 1)
            acc = jnp.zeros((ATT_TQ, LANES), F32)
            stat = jnp.zeros((ATT_TQ, LANES), F32)
            scores = [lax.dot_general(q_ref[rows, hh * LANES:(hh + 1) * LANES], k_ref[:, hh * LANES:(hh + 1) * LANES],
                                      NT_DIMS, preferred_element_type=F32) for hh in range(2)]
            maxes = [jnp.max(sc, axis=1, keepdims=True) for sc in scores]
            exps = [jnp.exp2((sc - mx) * EXP2_SCALE).astype(BF16) for sc, mx in zip(scores, maxes)]
            for hh in range(2):
                mx = maxes[hh]
                res = jnp.dot(exps[hh], vm[hh], preferred_element_type=F32)
                den = jnp.sum(jnp.where(olane == one_lane[hh], res, 0.0), axis=1, keepdims=True)
                acc = acc + jnp.where(_head_mask(res.shape, hh), res * (1.0 / den), 0.0)
                stat = stat + jnp.where(olane == hh, mx * EXP2_SCALE + jnp.log(den) * LOG2_E, 0.0)
            o_ref[rows, :] = acc
            ob_ref[rows, :] = acc.astype(BF16)
            st_ref[:, rows] = stat.T[0:8, :]
            return carry

        lax.fori_loop(0, step // ATT_TQ, block, 0)
        riding.finish(state)

    o_spec = pl.BlockSpec((step, LANES), lambda p, i: (i, p))
    outs = pl.pallas_call(
        body, name=name, grid=(N_HEADS // 2, nq),
        out_shape=(jax.ShapeDtypeStruct((s, N_HEADS * V_DIM), F32),
                   jax.ShapeDtypeStruct((s, D_MODEL), BF16),
                   jax.ShapeDtypeStruct((N_HEADS // 2 * 8, s), F32), *riding.out_shape),
        in_specs=[pl.BlockSpec((step, 2 * LANES), lambda p, i: (i, p)),
                  pl.BlockSpec((t, 2 * LANES), lambda p, i: (0, p)),
                  pl.BlockSpec((t, LANES), lambda p, i: (0, N_HEADS + p)), *riding.specs],
        out_specs=(o_spec, o_spec, pl.BlockSpec((8, step), lambda p, i: (p, i)), *riding.specs),
        scratch_shapes=riding.scratch,
        compiler_params=_params(("arbitrary", "arbitrary"), VMEM_BIG),
    )(qf, kv, kv, *riding.arrays)
    return outs[0], outs[1], outs[2], list(outs[3:])


def _attn_bwd(qf, kv, o, da, stats, cos, sgn, riding, name):
    s, t = o.shape[0], kv.shape[0]
    ATT_TQ = ATT_TQ_BWD
    nq = s // ATT_TQ
    nr = riding.n

    def body(*refs):
        q_ref, k_ref, v_ref, o_ref, do_ref, st_ref, c_ref, s_ref = refs[:8]
        dq_ref, dk_ref, dv_ref = refs[8 + nr:11 + nr]
        dk_acc, dv_acc = refs[11 + 2 * nr:13 + 2 * nr]
        p, i = pl.program_id(0), pl.program_id(1)
        state = riding.run((p == 0) & (i == 0), (p == N_HEADS // 2 - 1) & (i == nq - 1),
                           refs[8:8 + nr], refs[11 + nr:11 + 2 * nr], refs[13 + 2 * nr:])

        @pl.when(i == 0)
        def _():
            dk_acc[...] = jnp.zeros_like(dk_acc)
            dv_acc[...] = jnp.zeros_like(dv_acc)

        v = v_ref[...]
        do = do_ref[...]
        od = do * o_ref[...]
        ones = jnp.ones((8, LANES), F32)
        for hh in range(2):
            sl = slice(hh * LANES, (hh + 1) * LANES)
            q, k = q_ref[:, sl], k_ref[:, sl]
            mask = _head_mask(do.shape, hh)
            dom = jnp.where(mask, do, 0.0).astype(BF16)
            delta = lax.dot_general(ones, jnp.where(mask, od, 0.0), NT_DIMS, preferred_element_type=F32,
                                    precision=lax.Precision.HIGHEST)[0:1, :]
            st = lax.dot_general(k, q, NT_DIMS, preferred_element_type=F32)
            pt = jnp.exp2(st * EXP2_SCALE - st_ref[hh:hh + 1, :]).astype(BF16)
            dpt = lax.dot_general(v, dom, NT_DIMS, preferred_element_type=F32)
            dst = (pt.astype(F32) * (dpt - delta)).astype(BF16)
            dv_acc[...] += jnp.dot(pt, dom, preferred_element_type=F32)
            dk_acc[:, sl] += jnp.dot(dst, q, preferred_element_type=F32)
            dq = lax.dot_general(dst, k, TN_DIMS, preferred_element_type=F32) * ATTN_SCALE
            dq_ref[:, sl] = _rope(dq, c_ref[...], s_ref[...], True).astype(BF16)

        @pl.when(i == nq - 1)
        def _():
            dk_ref[...] = (dk_acc[...] * ATTN_SCALE).astype(BF16)
            dv_ref[...] = dv_acc[...].astype(BF16)

        riding.finish(state)

    o_spec = pl.BlockSpec((ATT_TQ, LANES), lambda p, i: (i, p))
    tab = pl.BlockSpec((ATT_TQ, LANES), lambda p, i: (i, 0))
    outs = pl.pallas_call(
        body, name=name, grid=(N_HEADS // 2, nq),
        out_shape=(jax.ShapeDtypeStruct((s, N_HEADS * LANES), BF16),
                   jax.ShapeDtypeStruct((t, N_HEADS * LANES), BF16),
                   jax.ShapeDtypeStruct((t, N_HEADS * V_DIM), BF16), *riding.out_shape),
        in_specs=[pl.BlockSpec((ATT_TQ, 2 * LANES), lambda p, i: (i, p)),
                  pl.BlockSpec((t, 2 * LANES), lambda p, i: (0, p)),
                  pl.BlockSpec((t, LANES), lambda p, i: (0, N_HEADS + p)),
                  o_spec, o_spec,
                  pl.BlockSpec((8, ATT_TQ), lambda p, i: (p, i)), tab, tab, *riding.specs],
        out_specs=(pl.BlockSpec((ATT_TQ, 2 * LANES), lambda p, i: (i, p)),
                   pl.BlockSpec((t, 2 * LANES), lambda p, i: (0, p)),
                   pl.BlockSpec((t, LANES), lambda p, i: (0, p)), *riding.specs),
        scratch_shapes=[pltpu.VMEM((t, 2 * LANES), F32), pltpu.VMEM((t, LANES), F32), *riding.scratch],
        compiler_params=_params(("arbitrary", "arbitrary"), VMEM_BIG),
    )(qf, kv, kv, o, da, stats, cos, sgn, *riding.arrays)
    return outs[0], outs[1], outs[2], list(outs[3:])


def _silu(x):
    return x * (1.0 / (1.0 + jnp.exp(-x)))


def _prologue(c_rows, c_ctx, w_mod, b_cols, extra_rows, name):
    d, cols = c_rows.shape[1], w_mod.shape[1]

    def body(c_ref, cctx_ref, wmod_ref, b_ref, x_ref, a_ref, modg_ref, c_all, blk, c_send, c_recv, m_send, m_recv):
        _direct_gather(c_ref, c_all, c_send, c_recv)()
        a_ref[...] = jnp.zeros_like(a_ref)
        for j in range(N_DEV):
            a_ref[j:j + 1, :] = c_all[j, 0:1, :]
        a_ref[N_DEV:N_DEV + 1, :] = cctx_ref[...]
        mod = jnp.dot(_silu(a_ref[...]), wmod_ref[...], preferred_element_type=F32,
                      precision=lax.Precision.HIGHEST) + b_ref[...]
        blk[...] = jnp.zeros_like(blk)
        for p in range(N_DEV):
            blk[p, 0:1, :] = mod[p:p + 1, :]
            blk[p, 1:2, :] = mod[N_DEV:N_DEV + 1, :]
            blk[p, 2:5, :] = x_ref[...]
        _direct_gather(blk, modg_ref, m_send, m_recv, per_peer=True)()

    vmem = pl.BlockSpec(memory_space=pltpu.VMEM)
    return pl.pallas_call(
        body, name=name,
        out_shape=(jax.ShapeDtypeStruct((16, d), F32), jax.ShapeDtypeStruct((N_DEV, 8, cols), F32)),
        in_specs=[vmem] * 5, out_specs=(vmem, vmem),
        scratch_shapes=[pltpu.VMEM((N_DEV, 8, d), F32), pltpu.VMEM((N_DEV, 8, cols), F32)]
        + [pltpu.SemaphoreType.DMA((7,)) for _ in range(4)],
        compiler_params=_params(None, VMEM_BIG),
    )(c_rows, c_ctx, w_mod, b_cols, extra_rows)


def _adaln_bwd(a_t, w, d_ex, d_ctx, d_all, name):
    def body(at_ref, w_ref, dex_ref, dctx_ref, dall_ref, gw_ref, dsil_ref, dsum_ref):
        sil_t = _silu(at_ref[...])
        dctx = dctx_ref[...]
        row = dctx[0:1, :]
        for j in range(1, N_DEV):
            row = row + dctx[j:j + 1, :]
        rowi = lax.broadcasted_iota(jnp.int32, dctx.shape, 0)
        ctx_rows = jnp.where(rowi == 0, jnp.broadcast_to(row, dctx.shape), 0.0)
        hi = lax.Precision.HIGHEST
        d_rows = jnp.concatenate([dex_ref[...], ctx_rows], axis=0)
        gw_ref[...] = jnp.dot(sil_t, d_rows, preferred_element_type=F32, precision=hi)
        dsil_ref[...] = lax.dot_general(ctx_rows, w_ref[...], NT_DIMS, preferred_element_type=F32, precision=hi)
        tot = dall_ref[0]
        for j in range(1, N_DEV):
            tot = tot + dall_ref[j]
        dsum_ref[...] = tot

    return pl.pallas_call(
        body, name=name,
        out_shape=(jax.ShapeDtypeStruct(w.shape, F32), jax.ShapeDtypeStruct((8, w.shape[0]), F32),
                   jax.ShapeDtypeStruct(d_all.shape[1:], F32)),
        compiler_params=_params(None, VMEM_BIG),
    )(a_t, w, d_ex, d_ctx, d_all)


SMALL_ROWS = 24
SMALL_MISC, SMALL_CW, SMALL_LOSS = 16, 18, 21


def _pack_small(sums1, sums2, fsums, sums1c, psums, psums_c, d_cw, cols, name):
    d = D_MODEL

    def body(s1_ref, s2_ref, f_ref, s1c_ref, p_ref, pc_ref, cw_ref, o_ref):
        o_ref[...] = jnp.zeros_like(o_ref)

        def blocks(row0, pieces):
            for j in range(N_DEV):
                lo, hi = j * cols, (j + 1) * cols
                for k, (ref, r) in enumerate(pieces):
                    a, b = max(lo, k * d), min(hi, (k + 1) * d)
                    if a < b:
                        o_ref[row0 + j:row0 + j + 1, a - lo:b - lo] = ref[r:r + 1, a - k * d:b - k * d]

        blocks(0, [(s1_ref, 1), (s1_ref, 0), (s2_ref, 2), (s2_ref, 1), (s2_ref, 0), (f_ref, 1)])
        blocks(N_DEV, [(s1c_ref, 1), (s1c_ref, 0)])
        head = Q_RANK + KV_RANK
        o_ref[SMALL_MISC:SMALL_MISC + 1, 0:Q_RANK] = p_ref[0:1, :]
        o_ref[SMALL_MISC:SMALL_MISC + 1, Q_RANK:head] = p_ref[1:2, 0:KV_RANK] + pc_ref[1:2, 0:KV_RANK]
        o_ref[SMALL_MISC:SMALL_MISC + 1, head:cols] = f_ref[0:1, 0:cols - head]
        o_ref[SMALL_MISC + 1:SMALL_MISC + 2, 0:d - (cols - head)] = f_ref[0:1, cols - head:d]
        for r in range(3):
            o_ref[SMALL_CW + r:SMALL_CW + r + 1, 0:CONV_W] = cw_ref[r:r + 1, :]
        o_ref[SMALL_LOSS:SMALL_LOSS + 1, :] = f_ref[3:4, 0:cols]

    return pl.pallas_call(body, name=name, out_shape=jax.ShapeDtypeStruct((SMALL_ROWS, cols), F32))(
        sums1, sums2, fsums, sums1c, psums, psums_c, d_cw)


def _adam_math(w, g, m, v):
    nm = ADAM_B1 * m + (1.0 - ADAM_B1) * g
    nv = ADAM_B2 * v + (1.0 - ADAM_B2) * (g * g)
    m_hat = nm / (1.0 - ADAM_B1 ** ADAM_STEP)
    v_hat = nv / (1.0 - ADAM_B2 ** ADAM_STEP)
    return -ADAM_LR * (m_hat / (jnp.sqrt(v_hat) + ADAM_EPS) + ADAM_WD * w), nm, nv


def _small_update(dsum, dsil_all, g_cw, params, name):
    d = D_MODEL
    n = len(params)
    cols = dsum.shape[1]

    def body(*refs):
        dsum_ref, dsil_ref, gcw_ref = refs[:3]
        wmv = refs[3:3 + 3 * n]
        outs = refs[3 + 3 * n:]
        tot = dsil_ref[0]
        for j in range(1, N_DEV):
            tot = tot + dsil_ref[j]
        cv = wmv[0][...]
        sg = 1.0 / (1.0 + jnp.exp(-cv))
        off = Q_RANK + KV_RANK
        misc = dsum_ref[SMALL_MISC:SMALL_MISC + 1, :]
        grads = [tot[0:1, :] * (sg * (1.0 + cv * (1.0 - sg))),
                 jnp.concatenate([dsum_ref[j:j + 1, :] + dsum_ref[N_DEV + j:N_DEV + j + 1, :] for j in range(N_DEV)],
                                 axis=1),
                 misc[:, 0:Q_RANK], misc[:, Q_RANK:off],
                 jnp.concatenate([misc[:, off:cols], dsum_ref[SMALL_MISC + 1:SMALL_MISC + 2, 0:d - (cols - off)]],
                                 axis=1),
                 gcw_ref[...]]
        for p, g in enumerate(grads):
            w_ref, m_ref, v_ref = wmv[3 * p:3 * p + 3]
            at = 0 if len(w_ref.shape) == 3 else Ellipsis
            res = (g,) + _adam_math(w_ref[at], g, m_ref[at], v_ref[at])
            for q, val in enumerate(res):
                outs[4 * p + q][at] = val

    flat = [a for wmv in params for a in wmv]
    out_shape = tuple(jax.ShapeDtypeStruct(wmv[0].shape, F32) for wmv in params for _ in range(4))
    outs = pl.pallas_call(body, name=name, out_shape=out_shape)(dsum, dsil_all, g_cw, *flat)
    return [outs[4 * p:4 * p + 4] for p in range(n)]


def _adamw(w, g, m, v, name, slots=False):
    _, rows, cols = w.shape
    tr = _pick(rows, (256, 128, 64, 32, 16, 8))

    def body(w_ref, g_ref, m_ref, v_ref, *outs):
        if slots:
            gv = g_ref[0].astype(F32)
            for j in range(1, g.shape[0]):
                gv = gv + g_ref[j].astype(F32)
            outs[0][...] = gv
        else:
            gv = g_ref[...]
        d_ref, nm_ref, nv_ref = outs[-3:]
        d_ref[...], nm_ref[...], nv_ref[...] = _adam_math(w_ref[...], gv, m_ref[...], v_ref[...])

    blk = pl.BlockSpec((None, tr, cols), lambda i: (0, i, 0))
    g_spec = (pl.BlockSpec((g.shape[0], tr, cols), lambda i: (0, i, 0)) if slots
              else pl.BlockSpec((tr, cols), lambda i: (i, 0)))
    sh = jax.ShapeDtypeStruct((1, rows, cols), F32)
    n_out = 4 if slots else 3
    return pl.pallas_call(
        body, name=name, grid=(rows // tr,), out_shape=(sh,) * n_out,
        in_specs=[blk, g_spec, blk, blk], out_specs=(blk,) * n_out,
        compiler_params=_params(("parallel",), VMEM_BIG),
    )(w, g, m, v)


def _rope_tables(s, l):
    tok = np.arange(s)
    row = (tok // GRID_W).astype(np.float32)
    col = (tok % GRID_W).astype(np.float32)
    half = QK_ROPE // 2
    freqs = np.float32(ROPE_THETA) ** (-np.arange(0, half, 2, dtype=np.float32) / np.float32(half))
    dd = np.arange(QK_ROPE)
    pos = np.where((dd // half)[None, :] == 0, row[:, None], col[:, None]).astype(np.float32)
    ang = (pos * freqs[dd % (half // 2)][None, :]).astype(np.float32)
    sin = np.sin(ang).astype(np.float32)
    cos_t = np.ones((s + l, LANES), np.float32)
    sgn_t = np.zeros((s + l, LANES), np.float32)
    cos_t[:s, QK_NOPE:QK_NOPE + QK_ROPE] = np.cos(ang)
    sgn_t[:s, QK_NOPE:QK_NOPE + QK_ROPE] = np.where(((dd % half) // (half // 2))[None, :] == 0, -sin, sin)
    return jnp.asarray(cos_t), jnp.asarray(sgn_t)


def _slots_to_cols(g):
    return g.transpose(1, 0, 2).reshape(g.shape[1], N_DEV * g.shape[2])


def _cols_to_slots(w):
    return w.reshape(w.shape[0], N_DEV, w.shape[1] // N_DEV).transpose(1, 0, 2)


def _unpack_small_weights(g_in_t, g_uq, g_ukv):
    w_t = g_in_t.reshape(N_DEV * g_in_t.shape[1], D_MODEL)
    zeros = jnp.zeros((QK_NOPE, D_MODEL), BF16)
    win_head_t = jnp.concatenate([w_t[:Q_RANK + KV_RANK], zeros, w_t[Q_RANK + KV_RANK:MLA_IN],
                                  zeros[:LANES - QK_NOPE - QK_ROPE]], axis=0)
    win_conv_t = w_t[MLA_IN:].reshape(3, CONV_W // LANES, LANES, D_MODEL).transpose(1, 0, 2, 3)
    win_conv_t = win_conv_t.reshape(3 * CONV_W, D_MODEL)
    w_uq = _slots_to_cols(g_uq).reshape(Q_RANK, N_HEADS, QK_NOPE + QK_ROPE)
    wq = jnp.pad(w_uq, ((0, 0), (0, 0), (0, LANES - QK_NOPE - QK_ROPE))).reshape(Q_RANK, N_HEADS * LANES)
    w_ukv = _slots_to_cols(g_ukv).reshape(KV_RANK, N_HEADS, QK_NOPE + V_DIM)
    k_top = jnp.pad(w_ukv[:, :, :QK_NOPE], ((0, 0), (0, 0), (0, LANES - QK_NOPE))).reshape(KV_RANK, N_HEADS * LANES)
    v_top = w_ukv[:, :, QK_NOPE:].reshape(KV_RANK, N_HEADS * V_DIM)
    eye = jnp.pad(jnp.eye(QK_ROPE, dtype=BF16), ((QK_NOPE, LANES - QK_NOPE - QK_ROPE),) * 2)
    wk = jnp.concatenate([
        jnp.concatenate([k_top, v_top], axis=1),
        jnp.concatenate([jnp.tile(eye, (1, N_HEADS)), jnp.zeros((LANES, N_HEADS * V_DIM), BF16)], axis=1)], axis=0)
    return win_head_t, win_conv_t, wq, wk


def _pack_small_grads(d_head_t, d_conv_t, d_wq, d_wkk, d_wkv):
    d_conv_t = d_conv_t.reshape(CONV_W // LANES, 3, LANES, D_MODEL).transpose(1, 0, 2, 3).reshape(3 * CONV_W, D_MODEL)
    rope0 = Q_RANK + KV_RANK + QK_NOPE
    g_in_t = jnp.concatenate([d_head_t[:Q_RANK + KV_RANK], d_head_t[rope0:rope0 + QK_ROPE], d_conv_t], axis=0)
    g_in_t = g_in_t.reshape(N_DEV, -1, D_MODEL).astype(BF16)
    g_uq = d_wq.reshape(Q_RANK, N_HEADS, LANES)[:, :, :QK_NOPE + QK_ROPE].reshape(Q_RANK, -1)
    g_kn = d_wkk[:KV_RANK].reshape(KV_RANK, N_HEADS, LANES)[:, :, :QK_NOPE]
    g_v = d_wkv[:KV_RANK].reshape(KV_RANK, N_HEADS, V_DIM)
    g_ukv = jnp.concatenate([g_kn, g_v], axis=2).reshape(KV_RANK, -1)
    return [g_in_t] + [_cols_to_slots(g).astype(BF16) for g in (g_uq, g_ukv)]


def kernel(x, c, ctx, c_ctx, w_mod, b_mod, w_in, q_norm_g, w_uq, kv_norm_g, w_ukv, conv_w, w_out, w_mlp1, w_mlp2, final_norm_g, loss_target, m_c_ctx, m_w_mod, m_b_mod, m_w_in, m_q_norm_g, m_w_uq, m_kv_norm_g, m_w_ukv, m_conv_w, m_w_out, m_w_mlp1, m_w_mlp2, m_final_norm_g, v_c_ctx, v_w_mod, v_b_mod, v_w_in, v_q_norm_g, v_w_uq, v_kv_norm_g, v_w_ukv, v_conv_w, v_w_out, v_w_mlp1, v_w_mlp2, v_final_norm_g):
    me = _my_index()
    x2d, ctx2d, tgt = x[0], ctx[0], loss_target[0]
    s, l = x2d.shape[0], ctx2d.shape[0]
    t = s + l
    d = D_MODEL
    mod_cols = w_mod.shape[2]
    cw_cols = conv_w.shape[2]

    b_cols = lax.dynamic_slice(b_mod, (0, me * mod_cols), (1, mod_cols))
    cw_blk = jnp.pad(conv_w[0], ((0, 0), (0, mod_cols - cw_cols)))
    a_rows, gathered = _prologue(jnp.pad(c, ((0, 7), (0, 0))), c_ctx[None, :], w_mod[0], b_cols, cw_blk,
                                 "prologue")
    mod_mine = gathered[:, 0, :].reshape(1, 6 * d)
    mod_ctx = gathered[:, 1, :].reshape(1, 6 * d)
    cw_full = gathered[:, 2:5, :cw_cols].transpose(1, 0, 2).reshape(3, CONV_W)

    early = [w.astype(BF16) for w in (w_in[0].T, w_uq[0], w_ukv[0])]
    late = [w.astype(BF16) for w in (w_out[0], w_mlp1[0], w_mlp2[0])]
    h_all, (g_in, g_uq, g_ukv) = _modulate_all(x2d, ctx2d, mod_mine, mod_ctx, _RidingGather(early),
                                               "modulate1")
    win_head, win_conv, wq, wk = _unpack_small_weights(g_in, g_uq, g_ukv)
    wk_k, wk_v = wk[:, :N_HEADS * LANES], wk[:, N_HEADS * LANES:]
    cos, sgn = _rope_tables(s, l)

    tm_t = _pick(t, (1088, 768, 256))
    tk_t = _pick(t, (2176, 768, 256))
    z_head, cq, kv_in, qf, kv = _head_fwd(h_all, win_head, wq, wk, q_norm_g, kv_norm_g, cos, sgn, tm_t, "head_fwd")
    z_conv = _matmul(h_all, win_conv, mode="nt", name="in_proj_conv", m=s, tm=1024, tn=1536, tk=1024)
    attn, a_cat, stats, (g_out, w1, g_w2) = _attn_fwd(qf, kv, s, _RidingGather(late), "attn_fwd")
    wo = g_out.reshape(d, d)
    w2 = g_w2.reshape(D_FF, d)
    a_cat = _conv_fwd(z_conv, cw_full, a_cat, "conv_fwd")
    (o, x1, h2), _ = _matmul_rows(a_cat, wo, _epi_resid_modulate, mode="nn", name="out_proj", tm=1024, tk=1024,
                                  rows=[x2d], vecs=[(mod_mine, 2), (mod_mine, 3), (mod_mine, 4)],
                                  out_dtypes=[F32, F32, BF16])
    u1, act = _matmul(h2, w1, mode="nn", name="mlp_up", tm=4096, tk=1024, epilogue="relu2", slots="b_cols")
    (dx2, dm, fsums), _ = _matmul_rows(act, w2, _epi_final, mode="nn", name="mlp_down", tm=512, tk=4096,
                                       rows=[x1, tgt], vecs=[(mod_mine, 5), (final_norm_g[None, :], 0)],
                                       out_dtypes=[F32, BF16], sums=True)

    d_w2 = _matmul(act, dm, mode="tn", name="d_w_mlp2", out_dtype=BF16, tm=1024, tn=1024, tk=4096)
    du1 = _matmul(dm, w2, mode="nt", name="d_act", out_dtype=BF16, tm=2048, tn=1024, tk=1024,
                  epilogue="drelu2", extra=(u1,))
    d_w1 = _matmul(h2, du1, mode="tn", name="d_w_mlp1", out_dtype=BF16, tm=1024, tk=4096, slots="out")
    (dx1, do, sums2), _ = _matmul_rows(du1, w1, _epi_modulate2_bwd, mode="nt", name="d_h2", tm=512, tk=4096,
                                       slots="b_contract", rows=[x1, dx2, o], vecs=[(mod_mine, 4), (mod_mine, 2)],
                                       out_dtypes=[F32, BF16], sums=True)
    d_wo = _matmul(a_cat, do, mode="tn", name="d_w_out", out_dtype=BF16, tm=1024, tn=1024, tk=2048)
    da = _matmul(do, wo, mode="nt", name="d_a", tm=1024, tn=1024, tk=1024)
    dz_conv, d_cw = _conv_bwd(z_conv, cw_full, da, "conv_bwd")
    ready = [d_wo.reshape(N_DEV, d // N_DEV, d), d_w1, d_w2.reshape(N_DEV, D_FF // N_DEV, d)]
    dq, dk, dv, rode = _attn_bwd(qf, kv, attn, da, stats, cos, sgn, _Riding(ready), "attn_bwd")
    d_wq = _matmul(cq, dq, mode="tn", name="d_w_uq", k=s, tm=256, tn=1024, tk=4096)
    d_wkk = _matmul(kv_in, dk, mode="tn", name="d_w_ukv_k", tm=256, tn=1024, tk=tk_t)
    d_wkv = _matmul(kv_in, dv, mode="tn", name="d_w_ukv_v", tm=256, tn=512, tk=tk_t)
    head_args = (z_head, wq, wk_k, wk_v, win_head, q_norm_g, kv_norm_g, cos, sgn)
    dz_head, dh_head, psums = _head_bwd(dq, dk, dv, *head_args, "head_bwd", tile=HEAD_BWD_TILE, first_block=0,
                                        n_blocks=s // HEAD_BWD_TILE)
    dz_head, dh_head, psums_c = _head_bwd(None, dk, dv, *head_args, "head_bwd_ctx", tile=ROW_TILE,
                                          first_block=s // ROW_TILE, n_blocks=l // ROW_TILE,
                                          carry=(dz_head, dh_head))
    d_head = _matmul(dz_head, h_all, mode="tn", name="d_w_in_head", tm=512, tn=1024, tk=tk_t)
    d_conv = _matmul(dz_conv, h_all, mode="tn", name="d_w_in_conv", k=s, tm=1536, tn=1024, tk=2048)
    send = _pack_small_grads(d_head, d_conv, d_wq, d_wkk, d_wkv)
    (grad_x, sums1), got = _matmul_rows(dz_conv, win_conv, _epi_modulate1_bwd, mode="nn", name="d_h1", tm=max(s // 8, ROW_TILE),
                                        tk=win_conv.shape[0], rows=[dh_head, x2d, dx1], vecs=[(mod_mine, 1)],
                                        out_dtypes=[F32], sums=True, riding=_RidingReduce(send))
    sums1c = _modulate_sums(dh_head, s // ROW_TILE, ctx2d)

    small = _pack_small(sums1, sums2, fsums, sums1c, psums, psums_c, d_cw, mod_cols, "pack_small")
    (d_all,) = _all_gather([small], "gather_small_grads", True)
    d_ex = lax.dynamic_index_in_dim(d_all, me, axis=1, keepdims=False)
    d_ctx = lax.dynamic_index_in_dim(d_all, N_DEV + me, axis=1, keepdims=False)
    g_w_mod, dsil, dsum = _adaln_bwd(a_rows.T, w_mod[0], d_ex, d_ctx, d_all, "adaln_bwd")
    (dsil_all,) = _all_gather([dsil], "gather_d_cctx", True)
    loss = dsum[SMALL_LOSS, 0]
    g_cw = lax.dynamic_slice(dsum, (SMALL_CW, me * cw_cols), (3, cw_cols))

    slots = dict(zip(["w_in", "w_uq", "w_ukv"], got))
    slots.update(zip(["w_out", "w_mlp1", "w_mlp2"], rode))

    grads = {}
    weights = {"c_ctx": c_ctx, "w_mod": w_mod, "b_mod": b_mod, "w_in": w_in, "q_norm_g": q_norm_g, "w_uq": w_uq,
               "kv_norm_g": kv_norm_g, "w_ukv": w_ukv, "conv_w": conv_w, "w_out": w_out, "w_mlp1": w_mlp1,
               "w_mlp2": w_mlp2, "final_norm_g": final_norm_g}
    m_in = {"c_ctx": m_c_ctx, "w_mod": m_w_mod, "b_mod": m_b_mod, "w_in": m_w_in, "q_norm_g": m_q_norm_g,
            "w_uq": m_w_uq, "kv_norm_g": m_kv_norm_g, "w_ukv": m_w_ukv, "conv_w": m_conv_w, "w_out": m_w_out,
            "w_mlp1": m_w_mlp1, "w_mlp2": m_w_mlp2, "final_norm_g": m_final_norm_g}
    v_in = {"c_ctx": v_c_ctx, "w_mod": v_w_mod, "b_mod": v_b_mod, "w_in": v_w_in, "q_norm_g": v_q_norm_g,
            "w_uq": v_w_uq, "kv_norm_g": v_kv_norm_g, "w_ukv": v_w_ukv, "conv_w": v_conv_w, "w_out": v_w_out,
            "w_mlp1": v_w_mlp1, "w_mlp2": v_w_mlp2, "final_norm_g": v_final_norm_g}
    names = list(weights)
    small_names = ["c_ctx", "b_mod", "q_norm_g", "kv_norm_g", "final_norm_g", "conv_w"]
    delta, new_m, new_v = {}, {}, {}

    def as_rows(a):
        return a[None, :] if a.ndim == 1 else a

    small_out = _small_update(dsum, dsil_all, g_cw, [[as_rows(src[n]) for src in (weights, m_in, v_in)]
                                                      for n in small_names], "small_update")
    for n, outs in zip(small_names, small_out):
        grads[n], delta[n], new_m[n], new_v[n] = [a.reshape(weights[n].shape) for a in outs]
    for n in names:
        if n in small_names:
            continue
        if n == "w_in":
            wmv = [jnp.swapaxes(src[n], 1, 2) for src in (weights, m_in, v_in)]
            outs = _adamw(wmv[0], slots[n], wmv[1], wmv[2], "adamw_" + n, slots=True)
            grads[n], delta[n], new_m[n], new_v[n] = [jnp.swapaxes(a, 1, 2) for a in outs]
        elif n in slots:
            grads[n], delta[n], new_m[n], new_v[n] = _adamw(weights[n], slots[n], m_in[n], v_in[n], "adamw_" + n,
                                                            slots=True)
        else:
            delta[n], new_m[n], new_v[n] = _adamw(weights[n], g_w_mod, m_in[n], v_in[n], "adamw_" + n)
            grads[n] = g_w_mod[None]

    return (loss, grad_x[None], *[grads[n] for n in names], *[delta[n] for n in names],
            *[new_m[n] for n in names], *[new_v[n] for n in names])
```

```python
import math

import jax
import jax.numpy as jnp
import numpy as np
from jax import lax
from jax.experimental import pallas as pl
from jax.experimental.pallas import tpu as pltpu

F32 = jnp.float32
BF16 = jnp.bfloat16

D_MODEL = 1024
GRID_W = 64
N_HEADS = 8
QK_NOPE = 64
QK_ROPE = 32
V_DIM = 64
Q_RANK = 256
KV_RANK = 128
MLA_IN = Q_RANK + KV_RANK + QK_ROPE
CONV_W = 512
HEAD_COLS = 512
D_FF = 4096
ROPE_THETA = 10000.0
EPS = 1e-6
ATTN_SCALE = 1.0 / math.sqrt(QK_NOPE + QK_ROPE)
LOG2_E = 1.0 / math.log(2.0)
EXP2_SCALE = ATTN_SCALE * LOG2_E
N_DEV = 8
LANES = 128

ADAM_LR, ADAM_B1, ADAM_B2, ADAM_EPS, ADAM_WD, ADAM_STEP = 0.001, 0.9, 0.999, 1e-08, 0.01, 10

ROW_TILE = 256
HEAD_BWD_TILE = 512
VMEM_BIG = 60 * 1024 * 1024


def _params(sem=None, vmem=None):
    return pltpu.CompilerParams(dimension_semantics=sem, vmem_limit_bytes=vmem)


def _pick(n, prefs):
    for p in prefs:
        if n % p == 0:
            return p
    return n


def _my_index():
    return 4 * lax.axis_index("x") + 2 * lax.axis_index("y") + lax.axis_index("c")


def _two_level_gather(x_refs, out_refs, send_sems, recv_sems, local_sems):
    n = len(x_refs)
    x, y, c = lax.axis_index("x"), lax.axis_index("y"), lax.axis_index("c")
    me, sibling = (x, y, c), (x, y, 1 - c)
    chips = [(1 - x, y), (x, 1 - y), (1 - x, 1 - y)]

    def slot(a, px, py, pc):
        return out_refs[a].at[4 * px + 2 * py + pc]

    def copy(a, k, block, to, src=None):
        return pltpu.make_async_remote_copy(
            src_ref=slot(a, *block) if src is None else src, dst_ref=slot(a, *block),
            send_sem=send_sems.at[7 * a + k], recv_sem=recv_sems.at[7 * a + k],
            device_id=to, device_id_type=pl.DeviceIdType.MESH)

    mine = [pltpu.make_async_copy(x_refs[a], slot(a, *me), local_sems.at[a]) for a in range(n)]
    first = [cp for a in range(n) for cp in
             [copy(a, 0, me, sibling, src=x_refs[a])]
             + [copy(a, 1 + j, me, (*chip, c), src=x_refs[a]) for j, chip in enumerate(chips)]]
    passed = [[copy(a, 4 + j, (*chip, c), sibling) for j, chip in enumerate(chips)] for a in range(n)]

    def start():
        for cp in mine + first:
            cp.start()

    def forward():
        for a in range(n):
            for j, chip in enumerate(chips):
                copy(a, 1 + j, (*chip, c), me).wait_recv()
                passed[a][j].start()

    def finish():
        for a in range(n):
            copy(a, 0, sibling, me).wait_recv()
            for j, chip in enumerate(chips):
                copy(a, 4 + j, (*chip, 1 - c), me).wait_recv()
        for cp in first + [cp for per_array in passed for cp in per_array]:
            cp.wait_send()
        for cp in mine:
            cp.wait()

    return start, forward, finish


def _direct_gather(src_ref, dst_ref, send_sems, recv_sems, per_peer=False):
    x, y, c = lax.axis_index("x"), lax.axis_index("y"), lax.axis_index("c")
    me = 4 * x + 2 * y + c
    dst_ref[me] = src_ref[me] if per_peer else src_ref[...]
    sends, landings = [], []
    for k in range(1, N_DEV):
        peer = (1 - x if k & 4 else x, 1 - y if k & 2 else y, 1 - c if k & 1 else c)
        pid = 4 * peer[0] + 2 * peer[1] + peer[2]
        for dst, out in ((me, sends), (pid, landings)):
            out.append(pltpu.make_async_remote_copy(
                src_ref=src_ref.at[pid] if per_peer else src_ref, dst_ref=dst_ref.at[dst],
                send_sem=send_sems.at[k - 1], recv_sem=recv_sems.at[k - 1],
                device_id=peer, device_id_type=pl.DeviceIdType.MESH))
    for cp in sends:
        cp.start()

    def finish():
        for cp in landings:
            cp.wait_recv()
        for cp in sends:
            cp.wait_send()

    return finish


def _all_gather(arrays, name, in_vmem):
    space = pltpu.VMEM if in_vmem else pl.ANY
    n = len(arrays)

    def body(*refs):
        for phase in _two_level_gather(refs[:n], refs[n:2 * n], *refs[2 * n:]):
            phase()

    outs = pl.pallas_call(
        body, name=name,
        out_shape=tuple(jax.ShapeDtypeStruct((N_DEV,) + a.shape, a.dtype) for a in arrays),
        in_specs=[pl.BlockSpec(memory_space=space)] * n,
        out_specs=tuple(pl.BlockSpec(memory_space=space) for _ in arrays),
        scratch_shapes=[pltpu.SemaphoreType.DMA((7 * n,)), pltpu.SemaphoreType.DMA((7 * n,)),
                        pltpu.SemaphoreType.DMA((n,))],
    )(*arrays)
    return list(outs)


class _Riding:
    def __init__(self, arrays=()):
        self.arrays, self.n = list(arrays), len(arrays)
        self.out_shape = [jax.ShapeDtypeStruct(a.shape, a.dtype) for a in self.arrays]
        self.specs = [pl.BlockSpec(memory_space=pl.ANY)] * self.n
        self.scratch = [pltpu.SemaphoreType.DMA((7 * self.n,)), pltpu.SemaphoreType.DMA((7 * self.n,)),
                        pltpu.SemaphoreType.DMA((self.n,))]

    def copies(self, x_refs, y_refs, send_sems, recv_sems, local_sems):
        x, y, c = lax.axis_index("x"), lax.axis_index("y"), lax.axis_index("c")
        me = 4 * x + 2 * y + c
        local, sends, landings = [], [], []
        for a in range(self.n):
            local.append(pltpu.make_async_copy(x_refs[a].at[me], y_refs[a].at[me], local_sems.at[a]))
            for k in range(1, N_DEV):
                peer = (1 - x if k & 4 else x, 1 - y if k & 2 else y, 1 - c if k & 1 else c)
                pid = 4 * peer[0] + 2 * peer[1] + peer[2]
                for dst, out in ((me, sends), (pid, landings)):
                    out.append(pltpu.make_async_remote_copy(
                        src_ref=x_refs[a].at[pid], dst_ref=y_refs[a].at[dst],
                        send_sem=send_sems.at[7 * a + k - 1], recv_sem=recv_sems.at[7 * a + k - 1],
                        device_id=peer, device_id_type=pl.DeviceIdType.MESH))
        return local, sends, landings

    def run(self, first, last, x_refs, y_refs, sems, middle=None):
        if self.n == 0:
            return None
        local, sends, landings = self.copies(x_refs, y_refs, *sems)

        @pl.when(first)
        def _():
            for cp in local + sends:
                cp.start()

        return local, sends, landings, last

    @staticmethod
    def finish(state):
        if state is None:
            return
        local, sends, landings, last = state

        @pl.when(last)
        def _():
            for cp in landings:
                cp.wait_recv()
            for cp in sends:
                cp.wait_send()
            for cp in local:
                cp.wait()


class _RidingGather:
    def __init__(self, arrays):
        self.arrays, self.n = list(arrays), len(arrays)
        self.out_shape = [jax.ShapeDtypeStruct((N_DEV,) + a.shape, a.dtype) for a in self.arrays]
        self.specs = [pl.BlockSpec(memory_space=pl.ANY)] * self.n
        self.scratch = [pltpu.SemaphoreType.DMA((7 * self.n,)), pltpu.SemaphoreType.DMA((7 * self.n,)),
                        pltpu.SemaphoreType.DMA((self.n,))]

    def run(self, first, last, x_refs, y_refs, sems, middle):
        start, forward, finish = _two_level_gather(x_refs, y_refs, *sems)
        pl.when(first)(start)
        pl.when(middle)(forward)
        return finish, last

    @staticmethod
    def finish(state):
        finish, last = state
        pl.when(last)(finish)


class _RidingReduce:
    def __init__(self, arrays):
        self.arrays, self.n = list(arrays), len(arrays)
        self.out_shape = [jax.ShapeDtypeStruct((4,) + a.shape[1:], a.dtype) for a in self.arrays]
        self.specs = [pl.BlockSpec(memory_space=pl.ANY)] * self.n
        self.scratch = [pltpu.VMEM((4,) + a.shape[1:], a.dtype) for a in self.arrays for _ in range(3)]
        self.scratch += [pltpu.SemaphoreType.DMA((self.n,)) for _ in range(6)]

    def run(self, first, last, x_refs, y_refs, scratch, middle):
        n = self.n
        own, sib, tot = scratch[0:3 * n:3], scratch[1:3 * n:3], scratch[2:3 * n:3]
        d2d_send, d2d_recv, local_in, ici_send, ici_recv, local_out = scratch[3 * n:]
        x, y, c = lax.axis_index("x"), lax.axis_index("y"), lax.axis_index("c")
        my_chip = 2 * x + y
        sibling = (x, y, 1 - c)
        others = [(1 - x, y), (x, 1 - y), (1 - x, 1 - y)]

        def to_sibling(a, j=None):
            src = x_refs[a].at[pl.ds(0, 4)] if j is None else x_refs[a].at[2 * j + 1 - c]
            dst = sib[a] if j is None else sib[a].at[j]
            return pltpu.make_async_remote_copy(src_ref=src, dst_ref=dst, send_sem=d2d_send.at[a],
                                                recv_sem=d2d_recv.at[a], device_id=sibling,
                                                device_id_type=pl.DeviceIdType.MESH)

        def mine_in(a, j=None):
            src = x_refs[a].at[pl.ds(0, 4)] if j is None else x_refs[a].at[2 * j + c]
            return pltpu.make_async_copy(src, own[a] if j is None else own[a].at[j], local_in.at[a])

        def to_chip(a, chip=None):
            if chip is None:
                src, dst, peer = tot[a].at[pl.ds(0, 3)], y_refs[a].at[pl.ds(0, 3)], sibling
            else:
                src, dst, peer = tot[a].at[2 * chip[0] + chip[1]], y_refs[a].at[my_chip], (*chip, c)
            return pltpu.make_async_remote_copy(src_ref=src, dst_ref=dst, send_sem=ici_send.at[a],
                                                recv_sem=ici_recv.at[a], device_id=peer,
                                                device_id_type=pl.DeviceIdType.MESH)

        def mine_out(a):
            return pltpu.make_async_copy(tot[a].at[my_chip], y_refs[a].at[my_chip], local_out.at[a])

        @pl.when(first)
        def _():
            for a in range(n):
                for j in range(4):
                    to_sibling(a, j).start()
                    mine_in(a, j).start()

        @pl.when(middle)
        def _():
            for a in range(n):
                to_sibling(a).wait_recv()
                to_sibling(a).wait_send()
                mine_in(a).wait()
                tot[a][...] = (own[a][...].astype(F32) + sib[a][...].astype(F32)).astype(tot[a].dtype)
                for chip in others:
                    to_chip(a, chip).start()
                mine_out(a).start()

        def finish():
            @pl.when(last)
            def _():
                for a in range(n):
                    to_chip(a).wait_recv()
                    to_chip(a).wait_send()
                    mine_out(a).wait()

        return finish

    @staticmethod
    def finish(state):
        state()


_DIMS ={"nn": (((1,), (0,)), ((), ())), "nt": (((1,), (1,)), ((), ())), "tn": (((0,), (0,)), ((), ()))}
NT_DIMS = _DIMS["nt"]
TN_DIMS = _DIMS["tn"]


def _swap8(x):
    lane = lax.broadcasted_iota(jnp.int32, x.shape, 1)
    return jnp.where((lane & 15) < 8, pltpu.roll(x, LANES - 8, 1), pltpu.roll(x, 8, 1))


def _rope(x, cos, sgn, bwd):
    return x * cos + (_swap8(x * sgn) if bwd else _swap8(x) * sgn)


def _matmul(a, b, *, mode, name, out_dtype=F32, tm=512, tn=512, tk=512, m=None, k=None,
            epilogue=None, extra=(), slots=None):
    if mode == "nn":
        m = a.shape[0] if m is None else m
        k = a.shape[1]
        n = N_DEV * b.shape[2] if slots == "b_cols" else b.shape[1]
    elif mode == "nt":
        m = a.shape[0] if m is None else m
        k = a.shape[1]
        n = b.shape[0]
    else:
        k = a.shape[0] if k is None else k
        m, n = a.shape[1], b.shape[1]
    tm, tn, tk = min(tm, m), min(tn, n), min(tk, k)
    if slots == "b_cols":
        tn = b.shape[2]
    if slots == "out":
        tn = n // N_DEV
    assert m % tm == 0 and n % tn == 0 and k % tk == 0, (name, m, n, k, tm, tn, tk)
    nk = k // tk
    dims = _DIMS[mode]
    a_spec = (pl.BlockSpec((tk, tm), lambda i, j, kk: (kk, i)) if mode == "tn"
              else pl.BlockSpec((tm, tk), lambda i, j, kk: (i, kk)))
    if slots == "b_cols":
        b_spec = pl.BlockSpec((None, tk, tn), lambda i, j, kk: (j, kk, 0))
    elif mode == "nt":
        b_spec = pl.BlockSpec((tn, tk), lambda i, j, kk: (j, kk))
    else:
        b_spec = pl.BlockSpec((tk, tn), lambda i, j, kk: (kk, j))
    tile = pl.BlockSpec((tm, tn), lambda i, j, kk: (i, j))
    if slots == "out":
        o_spec = pl.BlockSpec((None, tm, tn), lambda i, j, kk: (j, i, 0))
        o_shape = (N_DEV, m, tn)
    else:
        o_spec, o_shape = tile, (m, n)
    in_specs, args = [a_spec, b_spec], [a, b]
    if epilogue == "drelu2":
        in_specs.append(tile)
    args += list(extra)
    if epilogue == "relu2":
        out_shape = (jax.ShapeDtypeStruct(o_shape, BF16), jax.ShapeDtypeStruct(o_shape, BF16))
        out_specs = (o_spec, o_spec)
    else:
        out_shape = jax.ShapeDtypeStruct(o_shape, out_dtype)
        out_specs = o_spec
    n_in = len(args)
    n_out = 2 if epilogue == "relu2" else 1

    def body(*refs):
        a_ref, b_ref = refs[0], refs[1]
        outs = refs[n_in:n_in + n_out]
        part = lax.dot_general(a_ref[...], b_ref[...], dims, preferred_element_type=F32)

        def finish(acc):
            if epilogue == "relu2":
                outs[0][...] = acc.astype(BF16)
                r = jnp.maximum(acc, 0.0)
                outs[1][...] = (r * r).astype(BF16)
            elif epilogue == "drelu2":
                u = refs[2][...].astype(F32)
                outs[0][...] = (acc * (2.0 * jnp.maximum(u, 0.0))).astype(out_dtype)
            else:
                outs[0][...] = acc.astype(out_dtype)

        if nk == 1:
            finish(part)
        else:
            acc_ref = refs[n_in + n_out]
            kk = pl.program_id(2)

            @pl.when(kk == 0)
            def _():
                acc_ref[...] = part

            @pl.when(kk > 0)
            def _():
                acc_ref[...] += part

            @pl.when(kk == nk - 1)
            def _():
                finish(acc_ref[...])

    return pl.pallas_call(
        body, name=name, grid=(m // tm, n // tn, nk),
        out_shape=out_shape, in_specs=in_specs, out_specs=out_specs,
        scratch_shapes=[pltpu.VMEM((tm, tn), F32)] if nk > 1 else [],
        compiler_params=_params(("parallel", "parallel", "arbitrary"), VMEM_BIG),
    )(*args)


def _rstd(x):
    return lax.rsqrt(jnp.mean(x * x, axis=1, keepdims=True) + EPS)


def _norm_bwd(dxn, xn, r):
    return r * (dxn - xn * jnp.mean(dxn * xn, axis=1, keepdims=True))


def _vec(col):
    return pl.BlockSpec((1, D_MODEL), lambda i: (0, col))


def _matmul_rows(a, b, epi, *, mode, name, tm, tk, rows=(), vecs=(), out_dtypes=(), sums=False, slots=None,
                 riding=None):
    m, k = a.shape
    n = D_MODEL
    tm, tk = min(tm, m), min(tk, k)
    riding = riding or _Riding()
    group = 1
    if slots == "b_contract":
        group = max(1, tk // b.shape[2])
        tk = group * b.shape[2]
        b_spec = pl.BlockSpec((group, n, tk // group), lambda i, kk: (kk, 0, 0))
    elif mode == "nt":
        b_spec = pl.BlockSpec((n, tk), lambda i, kk: (0, kk))
    else:
        b_spec = pl.BlockSpec((tk, n), lambda i, kk: (kk, 0))
    assert m % tm == 0 and k % tk == 0, (name, m, k, tm, tk)
    ni, nk = m // tm, k // tk
    assert ni >= 2 or not isinstance(riding, _RidingReduce), "the two-level exchange needs a middle grid step"
    dims = _DIMS[mode]
    tile = pl.BlockSpec((tm, n), lambda i, kk: (i, 0))
    in_specs = [pl.BlockSpec((tm, tk), lambda i, kk: (i, kk)), b_spec] + [tile] * len(rows)
    in_specs += [pl.BlockSpec((1, n), lambda i, kk, col=col: (0, col)) for _, col in vecs]
    args = [a, b, *rows, *[v for v, _ in vecs]]
    out_shape = [jax.ShapeDtypeStruct((m, n), dt) for dt in out_dtypes]
    out_specs = [tile] * len(out_dtypes)
    if sums:
        out_shape.append(jax.ShapeDtypeStruct((8, n), F32))
        out_specs.append(pl.BlockSpec((8, n), lambda i, kk: (0, 0)))
    n_rows, n_vecs, n_outs, nr = len(rows), len(vecs), len(out_dtypes), riding.n
    n_in = 2 + n_rows + n_vecs

    def body(*refs):
        a_ref, b_ref = refs[0], refs[1]
        row_refs = refs[2:2 + n_rows]
        vec_refs = refs[2 + n_rows:n_in]
        x_refs = refs[n_in:n_in + nr]
        out_refs = refs[n_in + nr:n_in + nr + n_outs]
        pos = n_in + nr + n_outs
        sums_ref = refs[pos] if sums else None
        pos += 1 if sums else 0
        y_refs = refs[pos:pos + nr]
        pos += nr
        acc_ref = refs[pos] if nk > 1 else None
        sem_refs = refs[pos + (1 if nk > 1 else 0):]
        i, kk = pl.program_id(0), pl.program_id(1)
        state = riding.run((i == 0) & (kk == 0), (i == ni - 1) & (kk == nk - 1), x_refs, y_refs, sem_refs,
                           middle=(i == 1) & (kk == 0))
        if slots == "b_contract":
            c = tk // group
            part = lax.dot_general(a_ref[:, 0:c], b_ref[0], dims, preferred_element_type=F32)
            for u in range(1, group):
                part = part + lax.dot_general(a_ref[:, u * c:(u + 1) * c], b_ref[u], dims, preferred_element_type=F32)
        else:
            part = lax.dot_general(a_ref[...], b_ref[...], dims, preferred_element_type=F32)

        def finish(acc):
            nsub = tm // ROW_TILE
            for r in range(nsub):
                blk = pl.ds(r * ROW_TILE, ROW_TILE)
                epi(acc[r * ROW_TILE:(r + 1) * ROW_TILE], [ref.at[blk] for ref in row_refs], vec_refs,
                    [ref.at[blk] for ref in out_refs], sums_ref,
                    (i == 0) if r == 0 else None, (i == ni - 1) if r == nsub - 1 else None)

        if nk == 1:
            finish(part)
        else:
            @pl.when(kk == 0)
            def _():
                acc_ref[...] = part

            @pl.when(kk > 0)
            def _():
                acc_ref[...] += part

            @pl.when(kk == nk - 1)
            def _():
                finish(acc_ref)

        riding.finish(state)

    outs = pl.pallas_call(
        body, name=name, grid=(ni, nk),
        out_shape=(*out_shape, *riding.out_shape),
        in_specs=[*in_specs, *riding.specs], out_specs=(*out_specs, *riding.specs),
        scratch_shapes=([pltpu.VMEM((tm, n), F32)] if nk > 1 else []) + (riding.scratch if nr else []),
        compiler_params=_params(("arbitrary", "arbitrary"), VMEM_BIG),
    )(*args, *riding.arrays)
    n_own = len(out_shape)
    return list(outs[:n_own]), list(outs[n_own:])


def _zero_sums_at_start(sums_ref, first):
    if first is not None:
        @pl.when(first)
        def _():
            sums_ref[...] = jnp.zeros_like(sums_ref)


def _epi_resid_modulate(acc, rows, vecs, outs, sums_ref, first, last):
    (x_ref,), (g_ref, sh_ref, sc_ref) = rows, vecs
    x1 = x_ref[...] + g_ref[...] * acc
    outs[0][...] = acc
    outs[1][...] = x1
    outs[2][...] = (x1 * _rstd(x1) * (1.0 + sc_ref[...]) + sh_ref[...]).astype(BF16)


def _epi_final(acc, rows, vecs, outs, sums_ref, first, last):
    (x1_ref, t_ref), (g_ref, gf_ref) = rows, vecs
    d = acc.shape[1]
    x2 = x1_ref[...] + g_ref[...] * acc
    r = _rstd(x2)
    xn = x2 * r
    err = xn * gf_ref[...] - t_ref[...]
    dy = err * (1.0 / d)
    dx2 = _norm_bwd(dy * gf_ref[...], xn, r)
    outs[0][...] = dx2
    outs[1][...] = (dx2 * g_ref[...]).astype(BF16)
    _zero_sums_at_start(sums_ref, first)
    sums_ref[0:1, :] += jnp.sum(dy * xn, axis=0, keepdims=True)
    sums_ref[1:2, :] += jnp.sum(dx2 * acc, axis=0, keepdims=True)
    sums_ref[2:3, :] += jnp.sum(err * err, axis=0, keepdims=True)

    if last is not None:
        @pl.when(last)
        def _():
            tot = jnp.sum(sums_ref[2:3, :], axis=1, keepdims=True) * (0.5 / d)
            sums_ref[3:4, :] = jnp.broadcast_to(tot, (1, d))


def _epi_modulate2_bwd(acc, rows, vecs, outs, sums_ref, first, last):
    (x_ref, dres_ref, o_ref), (sc_ref, g_ref) = rows, vecs
    x = x_ref[...]
    r = _rstd(x)
    xn = x * r
    dx = dres_ref[...] + _norm_bwd(acc * (1.0 + sc_ref[...]), xn, r)
    outs[0][...] = dx
    outs[1][...] = (dx * g_ref[...]).astype(BF16)
    _zero_sums_at_start(sums_ref, first)
    sums_ref[0:1, :] += jnp.sum(acc * xn, axis=0, keepdims=True)
    sums_ref[1:2, :] += jnp.sum(acc, axis=0, keepdims=True)
    sums_ref[2:3, :] += jnp.sum(dx * o_ref[...], axis=0, keepdims=True)


def _epi_modulate1_bwd(acc, rows, vecs, outs, sums_ref, first, last):
    (add_ref, x_ref, dres_ref), (sc_ref,) = rows, vecs
    dh = acc + add_ref[...]
    x = x_ref[...]
    r = _rstd(x)
    xn = x * r
    outs[0][...] = dres_ref[...] + _norm_bwd(dh * (1.0 + sc_ref[...]), xn, r)
    _zero_sums_at_start(sums_ref, first)
    sums_ref[0:1, :] += jnp.sum(dh * xn, axis=0, keepdims=True)
    sums_ref[1:2, :] += jnp.sum(dh, axis=0, keepdims=True)


def _modulate_all(x, ctx, mod, mod_ctx, riding, name):
    s, d = x.shape
    t = s + ctx.shape[0]
    ns = s // ROW_TILE
    nc = ctx.shape[0] // ROW_TILE
    nr = riding.n

    def body(*refs):
        x_ref, c_ref, sh_ref, sc_ref, shc_ref, scc_ref = refs[:6]
        h_ref = refs[6 + nr]
        i = pl.program_id(0)
        state = riding.run(i == 0, i == ns + nc - 1, refs[6:6 + nr], refs[7 + nr:7 + 2 * nr], refs[7 + 2 * nr:],
                           middle=i == ns + nc - 3)

        @pl.when(i < ns)
        def _():
            v = x_ref[...]
            h_ref[...] = (v * _rstd(v) * (1.0 + sc_ref[...]) + sh_ref[...]).astype(BF16)

        @pl.when(i >= ns)
        def _():
            v = c_ref[...]
            h_ref[...] = (v * _rstd(v) * (1.0 + scc_ref[...]) + shc_ref[...]).astype(BF16)

        riding.finish(state)

    outs = pl.pallas_call(
        body, name=name, grid=(ns + nc,),
        out_shape=(jax.ShapeDtypeStruct((t, d), BF16), *riding.out_shape),
        in_specs=[pl.BlockSpec((ROW_TILE, d), lambda i: (jnp.minimum(i, ns - 1), 0)),
                  pl.BlockSpec((ROW_TILE, d), lambda i: (jnp.maximum(i - ns, 0), 0)),
                  _vec(0), _vec(1), _vec(0), _vec(1), *riding.specs],
        out_specs=(pl.BlockSpec((ROW_TILE, d), lambda i: (i, 0)), *riding.specs),
        scratch_shapes=riding.scratch,
        compiler_params=_params(("arbitrary",)),
    )(x, ctx, mod, mod, mod_ctx, mod_ctx, *riding.arrays)
    return outs[0], list(outs[1:])


def _modulate_sums(dh, row_off, xsrc, passed):
    s, d = xsrc.shape

    def body(dh_ref, x_ref, passed_ref, sums_ref, passed_out_ref):
        i = pl.program_id(0)
        x = x_ref[...]
        dhv = dh_ref[...]

        @pl.when(i == 0)
        def _():
            sums_ref[...] = jnp.zeros_like(sums_ref)

        sums_ref[0:1, :] += jnp.sum(dhv * (x * _rstd(x)), axis=0, keepdims=True)
        sums_ref[1:2, :] += jnp.sum(dhv, axis=0, keepdims=True)

    return pl.pallas_call(
        body, name="modulate1_ctx_bwd", grid=(s // ROW_TILE,),
        out_shape=(jax.ShapeDtypeStruct((8, d), F32), jax.ShapeDtypeStruct(passed.shape, passed.dtype)),
        in_specs=[pl.BlockSpec((ROW_TILE, d), lambda i: (i + row_off, 0)), pl.BlockSpec((ROW_TILE, d), lambda i: (i, 0)),
                  pl.BlockSpec(memory_space=pl.ANY)],
        out_specs=(pl.BlockSpec((8, d), lambda i: (0, 0)), pl.BlockSpec(memory_space=pl.ANY)),
        input_output_aliases={2: 1},
        compiler_params=_params(("arbitrary",)),
    )(dh, xsrc, passed)


def _head_fwd(h_all, win_head, wq, wk, q_gain, kv_gain, cos, sgn, tm, name):
    t, d = h_all.shape
    nq, nkv = wq.shape[1], wk.shape[1]

    def body(h_ref, wi_ref, wq_ref, wk_ref, qg_ref, kg_ref, c_ref, s_ref, z_ref, cq_ref, kvin_ref, qf_ref, kv_ref):
        z = lax.dot_general(h_ref[...], wi_ref[...], NT_DIMS, preferred_element_type=F32)
        z_ref[...] = z
        cos, sgn = c_ref[...], s_ref[...]
        zq = z[:, 0:Q_RANK]
        cq = (zq * _rstd(zq) * qg_ref[...]).astype(BF16)
        cq_ref[...] = cq
        zk = z[:, Q_RANK:Q_RANK + KV_RANK]
        kv_in = jnp.concatenate([(zk * _rstd(zk) * kg_ref[...]).astype(BF16),
                                 _rope(z[:, Q_RANK + KV_RANK:HEAD_COLS], cos, sgn, False).astype(BF16)], axis=1)
        kvin_ref[...] = kv_in
        q = jnp.dot(cq, wq_ref[...], preferred_element_type=F32)
        for h in range(nq // LANES):
            sl = slice(h * LANES, (h + 1) * LANES)
            qf_ref[:, sl] = _rope(q[:, sl], cos, sgn, False).astype(BF16)
        kv_ref[...] = jnp.dot(kv_in, wk_ref[...], preferred_element_type=F32).astype(BF16)

    def row(w):
        return pl.BlockSpec((tm, w), lambda i: (i, 0))

    def whole(a):
        return pl.BlockSpec(a.shape, lambda i: (0, 0))

    return pl.pallas_call(
        body, name=name, grid=(t // tm,),
        out_shape=(jax.ShapeDtypeStruct((t, HEAD_COLS), F32), jax.ShapeDtypeStruct((t, Q_RANK), BF16),
                   jax.ShapeDtypeStruct((t, KV_RANK + LANES), BF16), jax.ShapeDtypeStruct((t, nq), BF16),
                   jax.ShapeDtypeStruct((t, nkv), BF16)),
        in_specs=[row(d), whole(win_head), whole(wq), whole(wk), whole(q_gain), whole(kv_gain), row(LANES), row(LANES)],
        out_specs=(row(HEAD_COLS), row(Q_RANK), row(KV_RANK + LANES), row(nq), row(nkv)),
        compiler_params=_params(("parallel",), VMEM_BIG),
    )(h_all, win_head, wq, wk, q_gain, kv_gain, cos, sgn)


def _head_bwd(dq, dk, dv, z, wq, wk_k, wk_v, win_head, q_gain, kv_gain, cos, sgn, name, *, tile, first_block,
              n_blocks, carry=None):
    t = z.shape[0]
    with_q = dq is not None

    def body(*refs):
        it = iter(refs)
        dq_ref = next(it) if with_q else None
        dk_ref, dv_ref, z_ref, wq_ref, wkk_ref, wkv_ref, wi_ref, qg_ref, kg_ref, c_ref, s_ref = (next(it) for _ in range(11))
        if carry is not None:
            next(it), next(it)
        dz_ref, dh_ref, sums_ref = next(it), next(it), next(it)
        i = pl.program_id(0)

        @pl.when(i == 0)
        def _():
            sums_ref[...] = jnp.zeros_like(sums_ref)

        if with_q:
            dc = lax.dot_general(dq_ref[...], wq_ref[...], NT_DIMS, preferred_element_type=F32)
            zq = z_ref[:, 0:Q_RANK]
            r = _rstd(zq)
            zn = zq * r
            sums_ref[0:1, :] += jnp.sum(dc * zn, axis=0, keepdims=True)
            dz_ref[:, 0:Q_RANK] = _norm_bwd(dc * qg_ref[...], zn, r).astype(BF16)
        else:
            dz_ref[:, 0:Q_RANK] = jnp.zeros((tile, Q_RANK), BF16)
        dkv = (lax.dot_general(dk_ref[...], wkk_ref[...], NT_DIMS, preferred_element_type=F32)
               + lax.dot_general(dv_ref[...], wkv_ref[...], NT_DIMS, preferred_element_type=F32))
        zk = z_ref[:, Q_RANK:Q_RANK + KV_RANK]
        r = _rstd(zk)
        zn = zk * r
        dc = dkv[:, 0:KV_RANK]
        sums_ref[1:2, 0:KV_RANK] += jnp.sum(dc * zn, axis=0, keepdims=True)
        dz_ref[:, Q_RANK:Q_RANK + KV_RANK] = _norm_bwd(dc * kg_ref[...], zn, r).astype(BF16)
        dz_ref[:, Q_RANK + KV_RANK:HEAD_COLS] = _rope(dkv[:, KV_RANK:KV_RANK + LANES], c_ref[...], s_ref[...],
                                                       True).astype(BF16)
        dh_ref[...] = jnp.dot(dz_ref[...], wi_ref[...], preferred_element_type=F32)

    def row(w):
        return pl.BlockSpec((tile, w), lambda i: (i + first_block, 0))

    def whole(a):
        return pl.BlockSpec(a.shape, lambda i: (0, 0))

    args = ([dq] if with_q else []) + [dk, dv, z, wq, wk_k, wk_v, win_head, q_gain, kv_gain, cos, sgn]
    in_specs = ([row(dq.shape[1])] if with_q else []) + [
        row(dk.shape[1]), row(dv.shape[1]), row(HEAD_COLS), whole(wq), whole(wk_k), whole(wk_v),
        whole(win_head), whole(q_gain), whole(kv_gain), row(LANES), row(LANES)]
    aliases = {}
    if carry is not None:
        aliases = {len(args): 0, len(args) + 1: 1}
        args += list(carry)
        in_specs += [pl.BlockSpec(memory_space=pl.ANY)] * 2
    return pl.pallas_call(
        body, name=name, grid=(n_blocks,),
        out_shape=(jax.ShapeDtypeStruct((t, HEAD_COLS), BF16), jax.ShapeDtypeStruct((t, D_MODEL), F32),
                   jax.ShapeDtypeStruct((8, Q_RANK), F32)),
        in_specs=in_specs,
        out_specs=(row(HEAD_COLS), row(D_MODEL), pl.BlockSpec((8, Q_RANK), lambda i: (0, 0))),
        input_output_aliases=aliases,
        compiler_params=_params(("arbitrary",), VMEM_BIG),
    )(*args)


def _shift_rows(u, s):
    rowi = lax.broadcasted_iota(jnp.int32, u.shape, 0)
    prev = jnp.where(rowi == 0, 0.0, pltpu.roll(u, 1, 0))
    nxt = jnp.where(rowi == s - 1, 0.0, pltpu.roll(u, s - 1, 0))
    return prev, nxt


def _conv_fwd(z_conv, cw, a_cat, name):
    s = z_conv.shape[0]

    def body(z_ref, w_ref, a_in_ref, o_ref):
        del a_in_ref
        gb, gc, xv = z_ref[:, 0:LANES], z_ref[:, LANES:2 * LANES], z_ref[:, 2 * LANES:3 * LANES]
        u = gc * xv
        prev, nxt = _shift_rows(u, s)
        y = w_ref[0:1, :] * prev + w_ref[1:2, :] * u + w_ref[2:3, :] * nxt
        o_ref[...] = (gb * y).astype(BF16)

    return pl.pallas_call(
        body, name=name, grid=(CONV_W // LANES,),
        out_shape=jax.ShapeDtypeStruct(a_cat.shape, a_cat.dtype),
        in_specs=[pl.BlockSpec((s, 3 * LANES), lambda j: (0, j)), pl.BlockSpec((3, LANES), lambda j: (0, j)),
                  pl.BlockSpec(memory_space=pl.ANY)],
        out_specs=pl.BlockSpec((s, LANES), lambda j: (0, 4 + j)),
        input_output_aliases={2: 0},
        compiler_params=_params(("parallel",), VMEM_BIG),
    )(z_conv, cw, a_cat)


def _conv_bwd(z_conv, cw, da, name):
    s = z_conv.shape[0]

    def body(z_ref, w_ref, da_ref, dz_ref, dw_ref):
        gb, gc, xv = z_ref[:, 0:LANES], z_ref[:, LANES:2 * LANES], z_ref[:, 2 * LANES:3 * LANES]
        u = gc * xv
        prev, nxt = _shift_rows(u, s)
        dcv = da_ref[...]
        dz_ref[:, 0:LANES] = (dcv * (w_ref[0:1, :] * prev + w_ref[1:2, :] * u + w_ref[2:3, :] * nxt)).astype(BF16)
        dy = dcv * gb
        dw_ref[0:1, :] = jnp.sum(dy * prev, axis=0, keepdims=True)
        dw_ref[1:2, :] = jnp.sum(dy * u, axis=0, keepdims=True)
        dw_ref[2:3, :] = jnp.sum(dy * nxt, axis=0, keepdims=True)
        dyp, dyn = _shift_rows(dy, s)
        du = w_ref[0:1, :] * dyn + w_ref[1:2, :] * dy + w_ref[2:3, :] * dyp
        dz_ref[:, LANES:2 * LANES] = (du * xv).astype(BF16)
        dz_ref[:, 2 * LANES:3 * LANES] = (du * gc).astype(BF16)

    blk = pl.BlockSpec((s, 3 * LANES), lambda j: (0, j))
    cws = pl.BlockSpec((3, LANES), lambda j: (0, j))
    return pl.pallas_call(
        body, name=name, grid=(CONV_W // LANES,),
        out_shape=(jax.ShapeDtypeStruct(z_conv.shape, BF16), jax.ShapeDtypeStruct((3, CONV_W), F32)),
        in_specs=[blk, cws, pl.BlockSpec((s, LANES), lambda j: (0, 4 + j))], out_specs=(blk, cws),
        compiler_params=_params(("parallel",), VMEM_BIG),
    )(z_conv, cw, da)


ATT_TQ = 512
ATT_Q_STEP = 1024
ATT_TQ_BWD = 512


def _head_mask(shape, hh):
    lane = lax.broadcasted_iota(jnp.int32, shape, 1)
    return (lane >= hh * V_DIM) & (lane < (hh + 1) * V_DIM)


def _attn_fwd(qf, kv, s, riding, name):
    t = kv.shape[0]
    step = min(ATT_Q_STEP, s)
    nq = s // step
    nr = riding.n

    def body(*refs):
        q_ref, k_ref, v_ref = refs[:3]
        o_ref, ob_ref, st_ref = refs[3 + nr:6 + nr]
        p, i = pl.program_id(0), pl.program_id(1)
        state = riding.run((p == 0) & (i == 0), (p == N_HEADS // 2 - 1) & (i == nq - 1),
                           refs[3:3 + nr], refs[6 + nr:6 + 2 * nr], refs[6 + 2 * nr:],
                           middle=(p == N_HEADS // 2 - 2) & (i == nq // 2))
        v = v_ref[...]
        vlane = lax.broadcasted_iota(jnp.int32, v.shape, 1)
        one_lane = [(1 - hh) * V_DIM for hh in range(2)]
        vm = [jnp.where(_head_mask(v.shape, hh), v, jnp.where(vlane == one_lane[hh], 1.0, 0.0).astype(BF16))
              for hh in range(2)]

        def block(r, carry):
            rows = pl.ds(pl.multiple_of(r * ATT_TQ, ATT_TQ), ATT_TQ)
            olane = lax.broadcasted_iota(jnp.int32, (ATT_TQ, LANES), 1)
            acc = jnp.zeros((ATT_TQ, LANES), F32)
            stat = jnp.zeros((ATT_TQ, LANES), F32)
            scores = [lax.dot_general(q_ref[rows, hh * LANES:(hh + 1) * LANES], k_ref[:, hh * LANES:(hh + 1) * LANES],
                                      NT_DIMS, preferred_element_type=F32) for hh in range(2)]
            maxes = [jnp.max(sc, axis=1, keepdims=True) for sc in scores]
            exps = [jnp.exp2((sc - mx) * EXP2_SCALE).astype(BF16) for sc, mx in zip(scores, maxes)]
            for hh in range(2):
                mx = maxes[hh]
                res = jnp.dot(exps[hh], vm[hh], preferred_element_type=F32)
                den = jnp.sum(jnp.where(olane == one_lane[hh], res, 0.0), axis=1, keepdims=True)
                acc = acc + jnp.where(_head_mask(res.shape, hh), res * (1.0 / den), 0.0)
                stat = stat + jnp.where(olane == hh, mx * EXP2_SCALE + jnp.log(den) * LOG2_E, 0.0)
            o_ref[rows, :] = acc
            ob_ref[rows, :] = acc.astype(BF16)
            st_ref[:, rows] = stat.T[0:8, :]
            return carry

        lax.fori_loop(0, step // ATT_TQ, block, 0)
        riding.finish(state)

    o_spec = pl.BlockSpec((step, LANES), lambda p, i: (i, p))
    outs = pl.pallas_call(
        body, name=name, grid=(N_HEADS // 2, nq),
        out_shape=(jax.ShapeDtypeStruct((s, N_HEADS * V_DIM), F32),
                   jax.ShapeDtypeStruct((s, D_MODEL), BF16),
                   jax.ShapeDtypeStruct((N_HEADS // 2 * 8, s), F32), *riding.out_shape),
        in_specs=[pl.BlockSpec((step, 2 * LANES), lambda p, i: (i, p)),
                  pl.BlockSpec((t, 2 * LANES), lambda p, i: (0, p)),
                  pl.BlockSpec((t, LANES), lambda p, i: (0, N_HEADS + p)), *riding.specs],
        out_specs=(o_spec, o_spec, pl.BlockSpec((8, step), lambda p, i: (p, i)), *riding.specs),
        scratch_shapes=riding.scratch,
        compiler_params=_params(("arbitrary", "arbitrary"), VMEM_BIG),
    )(qf, kv, kv, *riding.arrays)
    return outs[0], outs[1], outs[2], list(outs[3:])


def _attn_bwd(qf, kv, o, da, stats, cos, sgn, riding, name):
    s, t = o.shape[0], kv.shape[0]
    ATT_TQ = ATT_TQ_BWD
    nq = s // ATT_TQ
    nr = riding.n

    def body(*refs):
        q_ref, k_ref, v_ref, o_ref, do_ref, st_ref, c_ref, s_ref = refs[:8]
        dq_ref, dk_ref, dv_ref = refs[8 + nr:11 + nr]
        dk_acc, dv_acc = refs[11 + 2 * nr:13 + 2 * nr]
        p, i = pl.program_id(0), pl.program_id(1)
        state = riding.run((p == 0) & (i == 0), (p == N_HEADS // 2 - 1) & (i == nq - 1),
                           refs[8:8 + nr], refs[11 + nr:11 + 2 * nr], refs[13 + 2 * nr:])

        @pl.when(i == 0)
        def _():
            dk_acc[...] = jnp.zeros_like(dk_acc)
            dv_acc[...] = jnp.zeros_like(dv_acc)

        v = v_ref[...]
        do = do_ref[...]
        od = do * o_ref[...]
        ones = jnp.ones((8, LANES), F32)
        for hh in range(2):
            sl = slice(hh * LANES, (hh + 1) * LANES)
            q, k = q_ref[:, sl], k_ref[:, sl]
            mask = _head_mask(do.shape, hh)
            dom = jnp.where(mask, do, 0.0).astype(BF16)
            delta = lax.dot_general(ones, jnp.where(mask, od, 0.0), NT_DIMS, preferred_element_type=F32,
                                    precision=lax.Precision.HIGHEST)[0:1, :]
            st = lax.dot_general(k, q, NT_DIMS, preferred_element_type=F32)
            pt = jnp.exp2(st * EXP2_SCALE - st_ref[hh:hh + 1, :]).astype(BF16)
            dpt = lax.dot_general(v, dom, NT_DIMS, preferred_element_type=F32)
            dst = (pt.astype(F32) * (dpt - delta)).astype(BF16)
            dv_acc[...] += jnp.dot(pt, dom, preferred_element_type=F32)
            dk_acc[:, sl] += jnp.dot(dst, q, preferred_element_type=F32)
            dq = lax.dot_general(dst, k, TN_DIMS, preferred_element_type=F32) * ATTN_SCALE
            dq_ref[:, sl] = _rope(dq, c_ref[...], s_ref[...], True).astype(BF16)

        @pl.when(i == nq - 1)
        def _():
            dk_ref[...] = (dk_acc[...] * ATTN_SCALE).astype(BF16)
            dv_ref[...] = dv_acc[...].astype(BF16)

        riding.finish(state)

    o_spec = pl.BlockSpec((ATT_TQ, LANES), lambda p, i: (i, p))
    tab = pl.BlockSpec((ATT_TQ, LANES), lambda p, i: (i, 0))
    outs = pl.pallas_call(
        body, name=name, grid=(N_HEADS // 2, nq),
        out_shape=(jax.ShapeDtypeStruct((s, N_HEADS * LANES), BF16),
                   jax.ShapeDtypeStruct((t, N_HEADS * LANES), BF16),
                   jax.ShapeDtypeStruct((t, N_HEADS * V_DIM), BF16), *riding.out_shape),
        in_specs=[pl.BlockSpec((ATT_TQ, 2 * LANES), lambda p, i: (i, p)),
                  pl.BlockSpec((t, 2 * LANES), lambda p, i: (0, p)),
                  pl.BlockSpec((t, LANES), lambda p, i: (0, N_HEADS + p)),
                  o_spec, o_spec,
                  pl.BlockSpec((8, ATT_TQ), lambda p, i: (p, i)), tab, tab, *riding.specs],
        out_specs=(pl.BlockSpec((ATT_TQ, 2 * LANES), lambda p, i: (i, p)),
                   pl.BlockSpec((t, 2 * LANES), lambda p, i: (0, p)),
                   pl.BlockSpec((t, LANES), lambda p, i: (0, p)), *riding.specs),
        scratch_shapes=[pltpu.VMEM((t, 2 * LANES), F32), pltpu.VMEM((t, LANES), F32), *riding.scratch],
        compiler_params=_params(("arbitrary", "arbitrary"), VMEM_BIG),
    )(qf, kv, kv, o, da, stats, cos, sgn, *riding.arrays)
    return outs[0], outs[1], outs[2], list(outs[3:])


def _silu(x):
    return x * (1.0 / (1.0 + jnp.exp(-x)))


def _prologue(c_rows, c_ctx, w_mod, b_cols, extra_rows, name):
    d, cols = c_rows.shape[1], w_mod.shape[1]

    def body(c_ref, cctx_ref, wmod_ref, b_ref, x_ref, a_ref, modg_ref, c_all, blk, c_send, c_recv, m_send, m_recv):
        _direct_gather(c_ref, c_all, c_send, c_recv)()
        a_ref[...] = jnp.zeros_like(a_ref)
        for j in range(N_DEV):
            a_ref[j:j + 1, :] = c_all[j, 0:1, :]
        a_ref[N_DEV:N_DEV + 1, :] = cctx_ref[...]
        mod = jnp.dot(_silu(a_ref[...]), wmod_ref[...], preferred_element_type=F32,
                      precision=lax.Precision.HIGHEST) + b_ref[...]
        blk[...] = jnp.zeros_like(blk)
        for p in range(N_DEV):
            blk[p, 0:1, :] = mod[p:p + 1, :]
            blk[p, 1:2, :] = mod[N_DEV:N_DEV + 1, :]
            blk[p, 2:5, :] = x_ref[...]
        _direct_gather(blk, modg_ref, m_send, m_recv, per_peer=True)()

    vmem = pl.BlockSpec(memory_space=pltpu.VMEM)
    return pl.pallas_call(
        body, name=name,
        out_shape=(jax.ShapeDtypeStruct((16, d), F32), jax.ShapeDtypeStruct((N_DEV, 8, cols), F32)),
        in_specs=[vmem] * 5, out_specs=(vmem, vmem),
        scratch_shapes=[pltpu.VMEM((N_DEV, 8, d), F32), pltpu.VMEM((N_DEV, 8, cols), F32)]
        + [pltpu.SemaphoreType.DMA((7,)) for _ in range(4)],
        compiler_params=_params(None, VMEM_BIG),
    )(c_rows, c_ctx, w_mod, b_cols, extra_rows)


def _adaln_bwd(a_t, w, d_ex, d_ctx, d_all, name):
    def body(at_ref, w_ref, dex_ref, dctx_ref, dall_ref, gw_ref, dsil_ref, dsum_ref):
        sil_t = _silu(at_ref[...])
        dctx = dctx_ref[...]
        row = dctx[0:1, :]
        for j in range(1, N_DEV):
            row = row + dctx[j:j + 1, :]
        rowi = lax.broadcasted_iota(jnp.int32, dctx.shape, 0)
        ctx_rows = jnp.where(rowi == 0, jnp.broadcast_to(row, dctx.shape), 0.0)
        hi = lax.Precision.HIGHEST
        d_rows = jnp.concatenate([dex_ref[...], ctx_rows], axis=0)
        gw_ref[...] = jnp.dot(sil_t, d_rows, preferred_element_type=F32, precision=hi)
        dsil_ref[...] = lax.dot_general(ctx_rows, w_ref[...], NT_DIMS, preferred_element_type=F32, precision=hi)
        tot = dall_ref[0]
        for j in range(1, N_DEV):
            tot = tot + dall_ref[j]
        dsum_ref[...] = tot

    return pl.pallas_call(
        body, name=name,
        out_shape=(jax.ShapeDtypeStruct(w.shape, F32), jax.ShapeDtypeStruct((8, w.shape[0]), F32),
                   jax.ShapeDtypeStruct(d_all.shape[1:], F32)),
        compiler_params=_params(None, VMEM_BIG),
    )(a_t, w, d_ex, d_ctx, d_all)


SMALL_ROWS = 24
SMALL_MISC, SMALL_CW, SMALL_LOSS = 16, 18, 21


def _pack_small(sums1, sums2, fsums, sums1c, psums, psums_c, d_cw, cols, name):
    d = D_MODEL

    def body(s1_ref, s2_ref, f_ref, s1c_ref, p_ref, pc_ref, cw_ref, o_ref):
        o_ref[...] = jnp.zeros_like(o_ref)

        def blocks(row0, pieces):
            for j in range(N_DEV):
                lo, hi = j * cols, (j + 1) * cols
                for k, (ref, r) in enumerate(pieces):
                    a, b = max(lo, k * d), min(hi, (k + 1) * d)
                    if a < b:
                        o_ref[row0 + j:row0 + j + 1, a - lo:b - lo] = ref[r:r + 1, a - k * d:b - k * d]

        blocks(0, [(s1_ref, 1), (s1_ref, 0), (s2_ref, 2), (s2_ref, 1), (s2_ref, 0), (f_ref, 1)])
        blocks(N_DEV, [(s1c_ref, 1), (s1c_ref, 0)])
        head = Q_RANK + KV_RANK
        o_ref[SMALL_MISC:SMALL_MISC + 1, 0:Q_RANK] = p_ref[0:1, :]
        o_ref[SMALL_MISC:SMALL_MISC + 1, Q_RANK:head] = p_ref[1:2, 0:KV_RANK] + pc_ref[1:2, 0:KV_RANK]
        o_ref[SMALL_MISC:SMALL_MISC + 1, head:cols] = f_ref[0:1, 0:cols - head]
        o_ref[SMALL_MISC + 1:SMALL_MISC + 2, 0:d - (cols - head)] = f_ref[0:1, cols - head:d]
        for r in range(3):
            o_ref[SMALL_CW + r:SMALL_CW + r + 1, 0:CONV_W] = cw_ref[r:r + 1, :]
        o_ref[SMALL_LOSS:SMALL_LOSS + 1, :] = f_ref[3:4, 0:cols]

    return pl.pallas_call(body, name=name, out_shape=jax.ShapeDtypeStruct((SMALL_ROWS, cols), F32))(
        sums1, sums2, fsums, sums1c, psums, psums_c, d_cw)


def _adam_math(w, g, m, v):
    nm = ADAM_B1 * m + (1.0 - ADAM_B1) * g
    nv = ADAM_B2 * v + (1.0 - ADAM_B2) * (g * g)
    m_hat = nm / (1.0 - ADAM_B1 ** ADAM_STEP)
    v_hat = nv / (1.0 - ADAM_B2 ** ADAM_STEP)
    return -ADAM_LR * (m_hat / (jnp.sqrt(v_hat) + ADAM_EPS) + ADAM_WD * w), nm, nv


def _small_update(dsum, dsil_all, g_cw, params, name):
    d = D_MODEL
    n = len(params)
    cols = dsum.shape[1]

    def body(*refs):
        dsum_ref, dsil_ref, gcw_ref = refs[:3]
        wmv = refs[3:3 + 3 * n]
        outs = refs[3 + 3 * n:]
        tot = dsil_ref[0]
        for j in range(1, N_DEV):
            tot = tot + dsil_ref[j]
        cv = wmv[0][...]
        sg = 1.0 / (1.0 + jnp.exp(-cv))
        off = Q_RANK + KV_RANK
        misc = dsum_ref[SMALL_MISC:SMALL_MISC + 1, :]
        grads = [tot[0:1, :] * (sg * (1.0 + cv * (1.0 - sg))),
                 jnp.concatenate([dsum_ref[j:j + 1, :] + dsum_ref[N_DEV + j:N_DEV + j + 1, :] for j in range(N_DEV)],
                                 axis=1),
                 misc[:, 0:Q_RANK], misc[:, Q_RANK:off],
                 jnp.concatenate([misc[:, off:cols], dsum_ref[SMALL_MISC + 1:SMALL_MISC + 2, 0:d - (cols - off)]],
                                 axis=1),
                 gcw_ref[...]]
        for p, g in enumerate(grads):
            w_ref, m_ref, v_ref = wmv[3 * p:3 * p + 3]
            at = 0 if len(w_ref.shape) == 3 else Ellipsis
            res = (g,) + _adam_math(w_ref[at], g, m_ref[at], v_ref[at])
            for q, val in enumerate(res):
                outs[4 * p + q][at] = val

    flat = [a for wmv in params for a in wmv]
    out_shape = tuple(jax.ShapeDtypeStruct(wmv[0].shape, F32) for wmv in params for _ in range(4))
    outs = pl.pallas_call(body, name=name, out_shape=out_shape)(dsum, dsil_all, g_cw, *flat)
    return [outs[4 * p:4 * p + 4] for p in range(n)]


def _adamw(w, g, m, v, name, slots=False):
    _, rows, cols = w.shape
    tr = _pick(rows, (256, 128, 64, 32, 16, 8))

    def body(w_ref, g_ref, m_ref, v_ref, *outs):
        if slots:
            gv = g_ref[0].astype(F32)
            for j in range(1, g.shape[0]):
                gv = gv + g_ref[j].astype(F32)
            outs[0][...] = gv
        else:
            gv = g_ref[...]
        d_ref, nm_ref, nv_ref = outs[-3:]
        d_ref[...], nm_ref[...], nv_ref[...] = _adam_math(w_ref[...], gv, m_ref[...], v_ref[...])

    blk = pl.BlockSpec((None, tr, cols), lambda i: (0, i, 0))
    g_spec = (pl.BlockSpec((g.shape[0], tr, cols), lambda i: (0, i, 0)) if slots
              else pl.BlockSpec((tr, cols), lambda i: (i, 0)))
    sh = jax.ShapeDtypeStruct((1, rows, cols), F32)
    n_out = 4 if slots else 3
    return pl.pallas_call(
        body, name=name, grid=(rows // tr,), out_shape=(sh,) * n_out,
        in_specs=[blk, g_spec, blk, blk], out_specs=(blk,) * n_out,
        compiler_params=_params(("parallel",), VMEM_BIG),
    )(w, g, m, v)


def _rope_tables(s, l):
    tok = np.arange(s)
    row = (tok // GRID_W).astype(np.float32)
    col = (tok % GRID_W).astype(np.float32)
    half = QK_ROPE // 2
    freqs = np.float32(ROPE_THETA) ** (-np.arange(0, half, 2, dtype=np.float32) / np.float32(half))
    dd = np.arange(QK_ROPE)
    pos = np.where((dd // half)[None, :] == 0, row[:, None], col[:, None]).astype(np.float32)
    ang = (pos * freqs[dd % (half // 2)][None, :]).astype(np.float32)
    sin = np.sin(ang).astype(np.float32)
    cos_t = np.ones((s + l, LANES), np.float32)
    sgn_t = np.zeros((s + l, LANES), np.float32)
    cos_t[:s, QK_NOPE:QK_NOPE + QK_ROPE] = np.cos(ang)
    sgn_t[:s, QK_NOPE:QK_NOPE + QK_ROPE] = np.where(((dd % half) // (half // 2))[None, :] == 0, -sin, sin)
    return jnp.asarray(cos_t), jnp.asarray(sgn_t)


def _slots_to_cols(g):
    return g.transpose(1, 0, 2).reshape(g.shape[1], N_DEV * g.shape[2])


def _cols_to_slots(w):
    return w.reshape(w.shape[0], N_DEV, w.shape[1] // N_DEV).transpose(1, 0, 2)


def _unpack_small_weights(g_in_t, g_uq, g_ukv):
    w_t = g_in_t.reshape(N_DEV * g_in_t.shape[1], D_MODEL)
    zeros = jnp.zeros((QK_NOPE, D_MODEL), BF16)
    win_head_t = jnp.concatenate([w_t[:Q_RANK + KV_RANK], zeros, w_t[Q_RANK + KV_RANK:MLA_IN],
                                  zeros[:LANES - QK_NOPE - QK_ROPE]], axis=0)
    win_conv_t = w_t[MLA_IN:].reshape(3, CONV_W // LANES, LANES, D_MODEL).transpose(1, 0, 2, 3)
    win_conv_t = win_conv_t.reshape(3 * CONV_W, D_MODEL)
    w_uq = _slots_to_cols(g_uq).reshape(Q_RANK, N_HEADS, QK_NOPE + QK_ROPE)
    wq = jnp.pad(w_uq, ((0, 0), (0, 0), (0, LANES - QK_NOPE - QK_ROPE))).reshape(Q_RANK, N_HEADS * LANES)
    w_ukv = _slots_to_cols(g_ukv).reshape(KV_RANK, N_HEADS, QK_NOPE + V_DIM)
    k_top = jnp.pad(w_ukv[:, :, :QK_NOPE], ((0, 0), (0, 0), (0, LANES - QK_NOPE))).reshape(KV_RANK, N_HEADS * LANES)
    v_top = w_ukv[:, :, QK_NOPE:].reshape(KV_RANK, N_HEADS * V_DIM)
    eye = jnp.pad(jnp.eye(QK_ROPE, dtype=BF16), ((QK_NOPE, LANES - QK_NOPE - QK_ROPE),) * 2)
    wk = jnp.concatenate([
        jnp.concatenate([k_top, v_top], axis=1),
        jnp.concatenate([jnp.tile(eye, (1, N_HEADS)), jnp.zeros((LANES, N_HEADS * V_DIM), BF16)], axis=1)], axis=0)
    return win_head_t, win_conv_t, wq, wk


def _pack_small_grads(d_head_t, d_conv_t, d_wq, d_wkk, d_wkv):
    d_conv_t = d_conv_t.reshape(CONV_W // LANES, 3, LANES, D_MODEL).transpose(1, 0, 2, 3).reshape(3 * CONV_W, D_MODEL)
    rope0 = Q_RANK + KV_RANK + QK_NOPE
    g_in_t = jnp.concatenate([d_head_t[:Q_RANK + KV_RANK], d_head_t[rope0:rope0 + QK_ROPE], d_conv_t], axis=0)
    g_in_t = g_in_t.reshape(N_DEV, -1, D_MODEL).astype(BF16)
    g_uq = d_wq.reshape(Q_RANK, N_HEADS, LANES)[:, :, :QK_NOPE + QK_ROPE].reshape(Q_RANK, -1)
    g_kn = d_wkk[:KV_RANK].reshape(KV_RANK, N_HEADS, LANES)[:, :, :QK_NOPE]
    g_v = d_wkv[:KV_RANK].reshape(KV_RANK, N_HEADS, V_DIM)
    g_ukv = jnp.concatenate([g_kn, g_v], axis=2).reshape(KV_RANK, -1)
    return [g_in_t] + [_cols_to_slots(g).astype(BF16) for g in (g_uq, g_ukv)]


def kernel(x, c, ctx, c_ctx, w_mod, b_mod, w_in, q_norm_g, w_uq, kv_norm_g, w_ukv, conv_w, w_out, w_mlp1, w_mlp2, final_norm_g, loss_target, m_c_ctx, m_w_mod, m_b_mod, m_w_in, m_q_norm_g, m_w_uq, m_kv_norm_g, m_w_ukv, m_conv_w, m_w_out, m_w_mlp1, m_w_mlp2, m_final_norm_g, v_c_ctx, v_w_mod, v_b_mod, v_w_in, v_q_norm_g, v_w_uq, v_kv_norm_g, v_w_ukv, v_conv_w, v_w_out, v_w_mlp1, v_w_mlp2, v_final_norm_g):
    me = _my_index()
    x2d, ctx2d, tgt = x[0], ctx[0], loss_target[0]
    s, l = x2d.shape[0], ctx2d.shape[0]
    t = s + l
    d = D_MODEL
    mod_cols = w_mod.shape[2]
    cw_cols = conv_w.shape[2]

    b_cols = lax.dynamic_slice(b_mod, (0, me * mod_cols), (1, mod_cols))
    cw_blk = jnp.pad(conv_w[0], ((0, 0), (0, mod_cols - cw_cols)))
    a_rows, gathered = _prologue(jnp.pad(c, ((0, 7), (0, 0))), c_ctx[None, :], w_mod[0], b_cols, cw_blk,
                                 "prologue")
    mod_mine = gathered[:, 0, :].reshape(1, 6 * d)
    mod_ctx = gathered[:, 1, :].reshape(1, 6 * d)
    cw_full = gathered[:, 2:5, :cw_cols].transpose(1, 0, 2).reshape(3, CONV_W)

    early = [w.astype(BF16) for w in (w_in[0].T, w_uq[0], w_ukv[0])]
    late = [w.astype(BF16) for w in (w_out[0], w_mlp1[0], w_mlp2[0])]
    h_all, (g_in, g_uq, g_ukv) = _modulate_all(x2d, ctx2d, mod_mine, mod_ctx, _RidingGather(early),
                                               "modulate1")
    win_head, win_conv, wq, wk = _unpack_small_weights(g_in, g_uq, g_ukv)
    wk_k, wk_v = wk[:, :N_HEADS * LANES], wk[:, N_HEADS * LANES:]
    cos, sgn = _rope_tables(s, l)

    tm_t = _pick(t, (1088, 768, 256))
    tk_t = _pick(t, (2176, 768, 256))
    z_head, cq, kv_in, qf, kv = _head_fwd(h_all, win_head, wq, wk, q_norm_g, kv_norm_g, cos, sgn, tm_t, "head_fwd")
    z_conv = _matmul(h_all, win_conv, mode="nt", name="in_proj_conv", m=s, tm=1024, tn=1536, tk=1024)
    attn, a_cat, stats, (g_out, w1, g_w2) = _attn_fwd(qf, kv, s, _RidingGather(late), "attn_fwd")
    wo = g_out.reshape(d, d)
    w2 = g_w2.reshape(D_FF, d)
    a_cat = _conv_fwd(z_conv, cw_full, a_cat, "conv_fwd")
    (o, x1, h2), _ = _matmul_rows(a_cat, wo, _epi_resid_modulate, mode="nn", name="out_proj", tm=1024, tk=1024,
                                  rows=[x2d], vecs=[(mod_mine, 2), (mod_mine, 3), (mod_mine, 4)],
                                  out_dtypes=[F32, F32, BF16])
    u1, act = _matmul(h2, w1, mode="nn", name="mlp_up", tm=4096, tk=1024, epilogue="relu2", slots="b_cols")
    (dx2, dm, fsums), _ = _matmul_rows(act, w2, _epi_final, mode="nn", name="mlp_down", tm=512, tk=4096,
                                       rows=[x1, tgt], vecs=[(mod_mine, 5), (final_norm_g[None, :], 0)],
                                       out_dtypes=[F32, BF16], sums=True)

    d_w2 = _matmul(act, dm, mode="tn", name="d_w_mlp2", out_dtype=BF16, tm=1024, tn=1024, tk=4096)
    du1 = _matmul(dm, w2, mode="nt", name="d_act", out_dtype=BF16, tm=2048, tn=1024, tk=1024,
                  epilogue="drelu2", extra=(u1,))
    d_w1 = _matmul(h2, du1, mode="tn", name="d_w_mlp1", out_dtype=BF16, tm=1024, tk=4096, slots="out")
    (dx1, do, sums2), _ = _matmul_rows(du1, w1, _epi_modulate2_bwd, mode="nt", name="d_h2", tm=512, tk=4096,
                                       slots="b_contract", rows=[x1, dx2, o], vecs=[(mod_mine, 4), (mod_mine, 2)],
                                       out_dtypes=[F32, BF16], sums=True)
    d_wo = _matmul(a_cat, do, mode="tn", name="d_w_out", out_dtype=BF16, tm=1024, tn=1024, tk=2048)
    da = _matmul(do, wo, mode="nt", name="d_a", tm=1024, tn=1024, tk=1024)
    dz_conv, d_cw = _conv_bwd(z_conv, cw_full, da, "conv_bwd")
    ready = [d_wo.reshape(N_DEV, d // N_DEV, d), d_w1, d_w2.reshape(N_DEV, D_FF // N_DEV, d)]
    dq, dk, dv, rode = _attn_bwd(qf, kv, attn, da, stats, cos, sgn, _Riding(ready), "attn_bwd")
    d_wq = _matmul(cq, dq, mode="tn", name="d_w_uq", k=s, tm=256, tn=1024, tk=4096)
    d_wkk = _matmul(kv_in, dk, mode="tn", name="d_w_ukv_k", tm=256, tn=1024, tk=tk_t)
    d_wkv = _matmul(kv_in, dv, mode="tn", name="d_w_ukv_v", tm=256, tn=512, tk=tk_t)
    head_args = (z_head, wq, wk_k, wk_v, win_head, q_norm_g, kv_norm_g, cos, sgn)
    dz_head, dh_head, psums = _head_bwd(dq, dk, dv, *head_args, "head_bwd", tile=HEAD_BWD_TILE, first_block=0,
                                        n_blocks=s // HEAD_BWD_TILE)
    dz_head, dh_head, psums_c = _head_bwd(None, dk, dv, *head_args, "head_bwd_ctx", tile=ROW_TILE,
                                          first_block=s // ROW_TILE, n_blocks=l // ROW_TILE,
                                          carry=(dz_head, dh_head))
    d_head = _matmul(dz_head, h_all, mode="tn", name="d_w_in_head", tm=512, tn=1024, tk=tk_t)
    d_conv = _matmul(dz_conv, h_all, mode="tn", name="d_w_in_conv", k=s, tm=1536, tn=1024, tk=2048)
    send = _pack_small_grads(d_head, d_conv, d_wq, d_wkk, d_wkv)
    (grad_x, sums1), got = _matmul_rows(dz_conv, win_conv, _epi_modulate1_bwd, mode="nn", name="d_h1", tm=max(s // 8, ROW_TILE),
                                        tk=win_conv.shape[0], rows=[dh_head, x2d, dx1], vecs=[(mod_mine, 1)],
                                        out_dtypes=[F32], sums=True, riding=_RidingReduce(send))
    sums1c, grad_x = _modulate_sums(dh_head, s // ROW_TILE, ctx2d, grad_x)

    small = _pack_small(sums1, sums2, fsums, sums1c, psums, psums_c, d_cw, mod_cols, "pack_small")
    (d_all,) = _all_gather([small], "gather_small_grads", True)
    d_ex = lax.dynamic_index_in_dim(d_all, me, axis=1, keepdims=False)
    d_ctx = lax.dynamic_index_in_dim(d_all, N_DEV + me, axis=1, keepdims=False)
    g_w_mod, dsil, dsum = _adaln_bwd(a_rows.T, w_mod[0], d_ex, d_ctx, d_all, "adaln_bwd")
    (dsil_all,) = _all_gather([dsil], "gather_d_cctx", True)
    loss = dsum[SMALL_LOSS, 0]
    g_cw = lax.dynamic_slice(dsum, (SMALL_CW, me * cw_cols), (3, cw_cols))

    slots = dict(zip(["w_in", "w_uq", "w_ukv"], got))
    slots.update(zip(["w_out", "w_mlp1", "w_mlp2"], rode))

    grads = {}
    weights = {"c_ctx": c_ctx, "w_mod": w_mod, "b_mod": b_mod, "w_in": w_in, "q_norm_g": q_norm_g, "w_uq": w_uq,
               "kv_norm_g": kv_norm_g, "w_ukv": w_ukv, "conv_w": conv_w, "w_out": w_out, "w_mlp1": w_mlp1,
               "w_mlp2": w_mlp2, "final_norm_g": final_norm_g}
    m_in = {"c_ctx": m_c_ctx, "w_mod": m_w_mod, "b_mod": m_b_mod, "w_in": m_w_in, "q_norm_g": m_q_norm_g,
            "w_uq": m_w_uq, "kv_norm_g": m_kv_norm_g, "w_ukv": m_w_ukv, "conv_w": m_conv_w, "w_out": m_w_out,
            "w_mlp1": m_w_mlp1, "w_mlp2": m_w_mlp2, "final_norm_g": m_final_norm_g}
    v_in = {"c_ctx": v_c_ctx, "w_mod": v_w_mod, "b_mod": v_b_mod, "w_in": v_w_in, "q_norm_g": v_q_norm_g,
            "w_uq": v_w_uq, "kv_norm_g": v_kv_norm_g, "w_ukv": v_w_ukv, "conv_w": v_conv_w, "w_out": v_w_out,
            "w_mlp1": v_w_mlp1, "w_mlp2": v_w_mlp2, "final_norm_g": v_final_norm_g}
    names = list(weights)
    small_names = ["c_ctx", "b_mod", "q_norm_g", "kv_norm_g", "final_norm_g", "conv_w"]
    delta, new_m, new_v = {}, {}, {}

    def as_rows(a):
        return a[None, :] if a.ndim == 1 else a

    small_out = _small_update(dsum, dsil_all, g_cw, [[as_rows(src[n]) for src in (weights, m_in, v_in)]
                                                      for n in small_names], "small_update")
    for n, outs in zip(small_names, small_out):
        grads[n], delta[n], new_m[n], new_v[n] = [a.reshape(weights[n].shape) for a in outs]
    for n in names:
        if n in small_names:
            continue
        if n == "w_in":
            wmv = [jnp.swapaxes(src[n], 1, 2) for src in (weights, m_in, v_in)]
            outs = _adamw(wmv[0], slots[n], wmv[1], wmv[2], "adamw_" + n, slots=True)
            grads[n], delta[n], new_m[n], new_v[n] = [jnp.swapaxes(a, 1, 2) for a in outs]
        elif n in slots:
            grads[n], delta[n], new_m[n], new_v[n] = _adamw(weights[n], slots[n], m_in[n], v_in[n], "adamw_" + n,
                                                            slots=True)
        else:
            delta[n], new_m[n], new_v[n] = _adamw(weights[n], g_w_mod, m_in[n], v_in[n], "adamw_" + n)
            grads[n] = g_w_mod[None]

    return (loss, grad_x[None], *[grads[n] for n in names], *[delta[n] for n in names],
            *[new_m[n] for n in names], *[new_v[n] for n in names])
```

```python
import math

import jax
import jax.numpy as jnp
import numpy as np
from jax import lax
from jax.experimental import pallas as pl
from jax.experimental.pallas import tpu as pltpu

F32 = jnp.float32
BF16 = jnp.bfloat16

D_MODEL = 1024
GRID_W = 64
N_HEADS = 8
QK_NOPE = 64
QK_ROPE = 32
V_DIM = 64
Q_RANK = 256
KV_RANK = 128
MLA_IN = Q_RANK + KV_RANK + QK_ROPE
CONV_W = 512
HEAD_COLS = 512
D_FF = 4096
ROPE_THETA = 10000.0
EPS = 1e-6
ATTN_SCALE = 1.0 / math.sqrt(QK_NOPE + QK_ROPE)
LOG2_E = 1.0 / math.log(2.0)
EXP2_SCALE = ATTN_SCALE * LOG2_E
N_DEV = 8
LANES = 128

ADAM_LR, ADAM_B1, ADAM_B2, ADAM_EPS, ADAM_WD, ADAM_STEP = 0.001, 0.9, 0.999, 1e-08, 0.01, 10

ROW_TILE = 256
HEAD_BWD_TILE = 512
VMEM_BIG = 60 * 1024 * 1024


def _params(sem=None, vmem=None):
    return pltpu.CompilerParams(dimension_semantics=sem, vmem_limit_bytes=vmem)


def _pick(n, prefs):
    for p in prefs:
        if n % p == 0:
            return p
    return n


def _my_index():
    return 4 * lax.axis_index("x") + 2 * lax.axis_index("y") + lax.axis_index("c")


def _two_level_gather(x_refs, out_refs, send_sems, recv_sems, local_sems):
    n = len(x_refs)
    x, y, c = lax.axis_index("x"), lax.axis_index("y"), lax.axis_index("c")
    me, sibling = (x, y, c), (x, y, 1 - c)
    chips = [(1 - x, y), (x, 1 - y), (1 - x, 1 - y)]

    def slot(a, px, py, pc):
        return out_refs[a].at[4 * px + 2 * py + pc]

    def copy(a, k, block, to, src=None):
        return pltpu.make_async_remote_copy(
            src_ref=slot(a, *block) if src is None else src, dst_ref=slot(a, *block),
            send_sem=send_sems.at[7 * a + k], recv_sem=recv_sems.at[7 * a + k],
            device_id=to, device_id_type=pl.DeviceIdType.MESH)

    mine = [pltpu.make_async_copy(x_refs[a], slot(a, *me), local_sems.at[a]) for a in range(n)]
    first = [cp for a in range(n) for cp in
             [copy(a, 0, me, sibling, src=x_refs[a])]
             + [copy(a, 1 + j, me, (*chip, c), src=x_refs[a]) for j, chip in enumerate(chips)]]
    passed = [[copy(a, 4 + j, (*chip, c), sibling) for j, chip in enumerate(chips)] for a in range(n)]

    def start():
        for cp in mine + first:
            cp.start()

    def forward():
        for a in range(n):
            for j, chip in enumerate(chips):
                copy(a, 1 + j, (*chip, c), me).wait_recv()
                passed[a][j].start()

    def finish():
        for a in range(n):
            copy(a, 0, sibling, me).wait_recv()
            for j, chip in enumerate(chips):
                copy(a, 4 + j, (*chip, 1 - c), me).wait_recv()
        for cp in first + [cp for per_array in passed for cp in per_array]:
            cp.wait_send()
        for cp in mine:
            cp.wait()

    return start, forward, finish


def _direct_gather(src_ref, dst_ref, send_sems, recv_sems, per_peer=False):
    x, y, c = lax.axis_index("x"), lax.axis_index("y"), lax.axis_index("c")
    me = 4 * x + 2 * y + c
    dst_ref[me] = src_ref[me] if per_peer else src_ref[...]
    sends, landings = [], []
    for k in range(1, N_DEV):
        peer = (1 - x if k & 4 else x, 1 - y if k & 2 else y, 1 - c if k & 1 else c)
        pid = 4 * peer[0] + 2 * peer[1] + peer[2]
        for dst, out in ((me, sends), (pid, landings)):
            out.append(pltpu.make_async_remote_copy(
                src_ref=src_ref.at[pid] if per_peer else src_ref, dst_ref=dst_ref.at[dst],
                send_sem=send_sems.at[k - 1], recv_sem=recv_sems.at[k - 1],
                device_id=peer, device_id_type=pl.DeviceIdType.MESH))
    for cp in sends:
        cp.start()

    def finish():
        for cp in landings:
            cp.wait_recv()
        for cp in sends:
            cp.wait_send()

    return finish


def _all_gather(arrays, name, in_vmem):
    space = pltpu.VMEM if in_vmem else pl.ANY
    n = len(arrays)

    def body(*refs):
        for phase in _two_level_gather(refs[:n], refs[n:2 * n], *refs[2 * n:]):
            phase()

    outs = pl.pallas_call(
        body, name=name,
        out_shape=tuple(jax.ShapeDtypeStruct((N_DEV,) + a.shape, a.dtype) for a in arrays),
        in_specs=[pl.BlockSpec(memory_space=space)] * n,
        out_specs=tuple(pl.BlockSpec(memory_space=space) for _ in arrays),
        scratch_shapes=[pltpu.SemaphoreType.DMA((7 * n,)), pltpu.SemaphoreType.DMA((7 * n,)),
                        pltpu.SemaphoreType.DMA((n,))],
    )(*arrays)
    return list(outs)


class _Riding:
    def __init__(self, arrays=()):
        self.arrays, self.n = list(arrays), len(arrays)
        self.out_shape = [jax.ShapeDtypeStruct(a.shape, a.dtype) for a in self.arrays]
        self.specs = [pl.BlockSpec(memory_space=pl.ANY)] * self.n
        self.scratch = [pltpu.SemaphoreType.DMA((7 * self.n,)), pltpu.SemaphoreType.DMA((7 * self.n,)),
                        pltpu.SemaphoreType.DMA((self.n,))]

    def copies(self, x_refs, y_refs, send_sems, recv_sems, local_sems):
        x, y, c = lax.axis_index("x"), lax.axis_index("y"), lax.axis_index("c")
        me = 4 * x + 2 * y + c
        local, sends, landings = [], [], []
        for a in range(self.n):
            local.append(pltpu.make_async_copy(x_refs[a].at[me], y_refs[a].at[me], local_sems.at[a]))
            for k in range(1, N_DEV):
                peer = (1 - x if k & 4 else x, 1 - y if k & 2 else y, 1 - c if k & 1 else c)
                pid = 4 * peer[0] + 2 * peer[1] + peer[2]
                for dst, out in ((me, sends), (pid, landings)):
                    out.append(pltpu.make_async_remote_copy(
                        src_ref=x_refs[a].at[pid], dst_ref=y_refs[a].at[dst],
                        send_sem=send_sems.at[7 * a + k - 1], recv_sem=recv_sems.at[7 * a + k - 1],
                        device_id=peer, device_id_type=pl.DeviceIdType.MESH))
        return local, sends, landings

    def run(self, first, last, x_refs, y_refs, sems, middle=None):
        if self.n == 0:
            return None
        local, sends, landings = self.copies(x_refs, y_refs, *sems)

        @pl.when(first)
        def _():
            for cp in local + sends:
                cp.start()

        return local, sends, landings, last

    @staticmethod
    def finish(state):
        if state is None:
            return
        local, sends, landings, last = state

        @pl.when(last)
        def _():
            for cp in landings:
                cp.wait_recv()
            for cp in sends:
                cp.wait_send()
            for cp in local:
                cp.wait()


class _RidingGather:
    def __init__(self, arrays):
        self.arrays, self.n = list(arrays), len(arrays)
        self.out_shape = [jax.ShapeDtypeStruct((N_DEV,) + a.shape, a.dtype) for a in self.arrays]
        self.specs = [pl.BlockSpec(memory_space=pl.ANY)] * self.n
        self.scratch = [pltpu.SemaphoreType.DMA((7 * self.n,)), pltpu.SemaphoreType.DMA((7 * self.n,)),
                        pltpu.SemaphoreType.DMA((self.n,))]

    def run(self, first, last, x_refs, y_refs, sems, middle):
        start, forward, finish = _two_level_gather(x_refs, y_refs, *sems)
        pl.when(first)(start)
        pl.when(middle)(forward)
        return finish, last

    @staticmethod
    def finish(state):
        finish, last = state
        pl.when(last)(finish)


class _RidingReduce:
    def __init__(self, arrays):
        self.arrays, self.n = list(arrays), len(arrays)
        self.out_shape = [jax.ShapeDtypeStruct((4,) + a.shape[1:], a.dtype) for a in self.arrays]
        self.specs = [pl.BlockSpec(memory_space=pl.ANY)] * self.n
        self.scratch = [pltpu.VMEM((4,) + a.shape[1:], a.dtype) for a in self.arrays for _ in range(3)]
        self.scratch += [pltpu.SemaphoreType.DMA((self.n,)) for _ in range(6)]

    def run(self, first, last, x_refs, y_refs, scratch, middle):
        n = self.n
        own, sib, tot = scratch[0:3 * n:3], scratch[1:3 * n:3], scratch[2:3 * n:3]
        d2d_send, d2d_recv, local_in, ici_send, ici_recv, local_out = scratch[3 * n:]
        x, y, c = lax.axis_index("x"), lax.axis_index("y"), lax.axis_index("c")
        my_chip = 2 * x + y
        sibling = (x, y, 1 - c)
        others = [(1 - x, y), (x, 1 - y), (1 - x, 1 - y)]

        def to_sibling(a, j=None):
            src = x_refs[a].at[pl.ds(0, 4)] if j is None else x_refs[a].at[2 * j + 1 - c]
            dst = sib[a] if j is None else sib[a].at[j]
            return pltpu.make_async_remote_copy(src_ref=src, dst_ref=dst, send_sem=d2d_send.at[a],
                                                recv_sem=d2d_recv.at[a], device_id=sibling,
                                                device_id_type=pl.DeviceIdType.MESH)

        def mine_in(a, j=None):
            src = x_refs[a].at[pl.ds(0, 4)] if j is None else x_refs[a].at[2 * j + c]
            return pltpu.make_async_copy(src, own[a] if j is None else own[a].at[j], local_in.at[a])

        def to_chip(a, chip=None):
            if chip is None:
                src, dst, peer = tot[a].at[pl.ds(0, 3)], y_refs[a].at[pl.ds(0, 3)], sibling
            else:
                src, dst, peer = tot[a].at[2 * chip[0] + chip[1]], y_refs[a].at[my_chip], (*chip, c)
            return pltpu.make_async_remote_copy(src_ref=src, dst_ref=dst, send_sem=ici_send.at[a],
                                                recv_sem=ici_recv.at[a], device_id=peer,
                                                device_id_type=pl.DeviceIdType.MESH)

        def mine_out(a):
            return pltpu.make_async_copy(tot[a].at[my_chip], y_refs[a].at[my_chip], local_out.at[a])

        @pl.when(first)
        def _():
            for a in range(n):
                for j in range(4):
                    to_sibling(a, j).start()
                    mine_in(a, j).start()

        @pl.when(middle)
        def _():
            for a in range(n):
                to_sibling(a).wait_recv()
                to_sibling(a).wait_send()
                mine_in(a).wait()
                tot[a][...] = (own[a][...].astype(F32) + sib[a][...].astype(F32)).astype(tot[a].dtype)
                for chip in others:
                    to_chip(a, chip).start()
                mine_out(a).start()

        def finish():
            @pl.when(last)
            def _():
                for a in range(n):
                    to_chip(a).wait_recv()
                    to_chip(a).wait_send()
                    mine_out(a).wait()

        return finish

    @staticmethod
    def finish(state):
        state()


_DIMS ={"nn": (((1,), (0,)), ((), ())), "nt": (((1,), (1,)), ((), ())), "tn": (((0,), (0,)), ((), ()))}
NT_DIMS = _DIMS["nt"]
TN_DIMS = _DIMS["tn"]


def _swap8(x):
    lane = lax.broadcasted_iota(jnp.int32, x.shape, 1)
    return jnp.where((lane & 15) < 8, pltpu.roll(x, LANES - 8, 1), pltpu.roll(x, 8, 1))


def _rope(x, cos, sgn, bwd):
    return x * cos + (_swap8(x * sgn) if bwd else _swap8(x) * sgn)


def _matmul(a, b, *, mode, name, out_dtype=F32, tm=512, tn=512, tk=512, m=None, k=None,
            epilogue=None, extra=(), slots=None):
    if mode == "nn":
        m = a.shape[0] if m is None else m
        k = a.shape[1]
        n = N_DEV * b.shape[2] if slots == "b_cols" else b.shape[1]
    elif mode == "nt":
        m = a.shape[0] if m is None else m
        k = a.shape[1]
        n = b.shape[0]
    else:
        k = a.shape[0] if k is None else k
        m, n = a.shape[1], b.shape[1]
    tm, tn, tk = min(tm, m), min(tn, n), min(tk, k)
    if slots == "b_cols":
        tn = b.shape[2]
    if slots == "out":
        tn = n // N_DEV
    assert m % tm == 0 and n % tn == 0 and k % tk == 0, (name, m, n, k, tm, tn, tk)
    nk = k // tk
    dims = _DIMS[mode]
    a_spec = (pl.BlockSpec((tk, tm), lambda i, j, kk: (kk, i)) if mode == "tn"
              else pl.BlockSpec((tm, tk), lambda i, j, kk: (i, kk)))
    if slots == "b_cols":
        b_spec = pl.BlockSpec((None, tk, tn), lambda i, j, kk: (j, kk, 0))
    elif mode == "nt":
        b_spec = pl.BlockSpec((tn, tk), lambda i, j, kk: (j, kk))
    else:
        b_spec = pl.BlockSpec((tk, tn), lambda i, j, kk: (kk, j))
    tile = pl.BlockSpec((tm, tn), lambda i, j, kk: (i, j))
    if slots == "out":
        o_spec = pl.BlockSpec((None, tm, tn), lambda i, j, kk: (j, i, 0))
        o_shape = (N_DEV, m, tn)
    else:
        o_spec, o_shape = tile, (m, n)
    in_specs, args = [a_spec, b_spec], [a, b]
    if epilogue == "drelu2":
        in_specs.append(tile)
    args += list(extra)
    if epilogue == "relu2":
        out_shape = (jax.ShapeDtypeStruct(o_shape, BF16), jax.ShapeDtypeStruct(o_shape, BF16))
        out_specs = (o_spec, o_spec)
    else:
        out_shape = jax.ShapeDtypeStruct(o_shape, out_dtype)
        out_specs = o_spec
    n_in = len(args)
    n_out = 2 if epilogue == "relu2" else 1

    def body(*refs):
        a_ref, b_ref = refs[0], refs[1]
        outs = refs[n_in:n_in + n_out]
        part = lax.dot_general(a_ref[...], b_ref[...], dims, preferred_element_type=F32)

        def finish(acc):
            if epilogue == "relu2":
                outs[0][...] = acc.astype(BF16)
                r = jnp.maximum(acc, 0.0)
                outs[1][...] = (r * r).astype(BF16)
            elif epilogue == "drelu2":
                u = refs[2][...].astype(F32)
                outs[0][...] = (acc * (2.0 * jnp.maximum(u, 0.0))).astype(out_dtype)
            else:
                outs[0][...] = acc.astype(out_dtype)

        if nk == 1:
            finish(part)
        else:
            acc_ref = refs[n_in + n_out]
            kk = pl.program_id(2)

            @pl.when(kk == 0)
            def _():
                acc_ref[...] = part

            @pl.when(kk > 0)
            def _():
                acc_ref[...] += part

            @pl.when(kk == nk - 1)
            def _():
                finish(acc_ref[...])

    return pl.pallas_call(
        body, name=name, grid=(m // tm, n // tn, nk),
        out_shape=out_shape, in_specs=in_specs, out_specs=out_specs,
        scratch_shapes=[pltpu.VMEM((tm, tn), F32)] if nk > 1 else [],
        compiler_params=_params(("parallel", "parallel", "arbitrary"), VMEM_BIG),
    )(*args)


def _rstd(x):
    return lax.rsqrt(jnp.mean(x * x, axis=1, keepdims=True) + EPS)


def _norm_bwd(dxn, xn, r):
    return r * (dxn - xn * jnp.mean(dxn * xn, axis=1, keepdims=True))


def _vec(col):
    return pl.BlockSpec((1, D_MODEL), lambda i: (0, col))


def _matmul_rows(a, b, epi, *, mode, name, tm, tk, rows=(), vecs=(), out_dtypes=(), sums=False, slots=None,
                 riding=None):
    m, k = a.shape
    n = D_MODEL
    tm, tk = min(tm, m), min(tk, k)
    riding = riding or _Riding()
    group = 1
    if slots == "b_contract":
        group = max(1, tk // b.shape[2])
        tk = group * b.shape[2]
        b_spec = pl.BlockSpec((group, n, tk // group), lambda i, kk: (kk, 0, 0))
    elif mode == "nt":
        b_spec = pl.BlockSpec((n, tk), lambda i, kk: (0, kk))
    else:
        b_spec = pl.BlockSpec((tk, n), lambda i, kk: (kk, 0))
    assert m % tm == 0 and k % tk == 0, (name, m, k, tm, tk)
    ni, nk = m // tm, k // tk
    assert ni >= 2 or not isinstance(riding, _RidingReduce), "the two-level exchange needs a middle grid step"
    dims = _DIMS[mode]
    tile = pl.BlockSpec((tm, n), lambda i, kk: (i, 0))
    in_specs = [pl.BlockSpec((tm, tk), lambda i, kk: (i, kk)), b_spec] + [tile] * len(rows)
    in_specs += [pl.BlockSpec((1, n), lambda i, kk, col=col: (0, col)) for _, col in vecs]
    args = [a, b, *rows, *[v for v, _ in vecs]]
    out_shape = [jax.ShapeDtypeStruct((m, n), dt) for dt in out_dtypes]
    out_specs = [tile] * len(out_dtypes)
    if sums:
        out_shape.append(jax.ShapeDtypeStruct((8, n), F32))
        out_specs.append(pl.BlockSpec((8, n), lambda i, kk: (0, 0)))
    n_rows, n_vecs, n_outs, nr = len(rows), len(vecs), len(out_dtypes), riding.n
    n_in = 2 + n_rows + n_vecs

    def body(*refs):
        a_ref, b_ref = refs[0], refs[1]
        row_refs = refs[2:2 + n_rows]
        vec_refs = refs[2 + n_rows:n_in]
        x_refs = refs[n_in:n_in + nr]
        out_refs = refs[n_in + nr:n_in + nr + n_outs]
        pos = n_in + nr + n_outs
        sums_ref = refs[pos] if sums else None
        pos += 1 if sums else 0
        y_refs = refs[pos:pos + nr]
        pos += nr
        acc_ref = refs[pos] if nk > 1 else None
        sem_refs = refs[pos + (1 if nk > 1 else 0):]
        i, kk = pl.program_id(0), pl.program_id(1)
        state = riding.run((i == 0) & (kk == 0), (i == ni - 1) & (kk == nk - 1), x_refs, y_refs, sem_refs,
                           middle=(i == 1) & (kk == 0))
        if slots == "b_contract":
            c = tk // group
            part = lax.dot_general(a_ref[:, 0:c], b_ref[0], dims, preferred_element_type=F32)
            for u in range(1, group):
                part = part + lax.dot_general(a_ref[:, u * c:(u + 1) * c], b_ref[u], dims, preferred_element_type=F32)
        else:
            part = lax.dot_general(a_ref[...], b_ref[...], dims, preferred_element_type=F32)

        def finish(acc):
            nsub = tm // ROW_TILE
            for r in range(nsub):
                blk = pl.ds(r * ROW_TILE, ROW_TILE)
                epi(acc[r * ROW_TILE:(r + 1) * ROW_TILE], [ref.at[blk] for ref in row_refs], vec_refs,
                    [ref.at[blk] for ref in out_refs], sums_ref,
                    (i == 0) if r == 0 else None, (i == ni - 1) if r == nsub - 1 else None)

        if nk == 1:
            finish(part)
        else:
            @pl.when(kk == 0)
            def _():
                acc_ref[...] = part

            @pl.when(kk > 0)
            def _():
                acc_ref[...] += part

            @pl.when(kk == nk - 1)
            def _():
                finish(acc_ref)

        riding.finish(state)

    outs = pl.pallas_call(
        body, name=name, grid=(ni, nk),
        out_shape=(*out_shape, *riding.out_shape),
        in_specs=[*in_specs, *riding.specs], out_specs=(*out_specs, *riding.specs),
        scratch_shapes=([pltpu.VMEM((tm, n), F32)] if nk > 1 else []) + (riding.scratch if nr else []),
        compiler_params=_params(("arbitrary", "arbitrary"), VMEM_BIG),
    )(*args, *riding.arrays)
    n_own = len(out_shape)
    return list(outs[:n_own]), list(outs[n_own:])


def _zero_sums_at_start(sums_ref, first):
    if first is not None:
        @pl.when(first)
        def _():
            sums_ref[...] = jnp.zeros_like(sums_ref)


def _epi_resid_modulate(acc, rows, vecs, outs, sums_ref, first, last):
    (x_ref,), (g_ref, sh_ref, sc_ref) = rows, vecs
    x1 = x_ref[...] + g_ref[...] * acc
    outs[0][...] = acc
    outs[1][...] = x1
    outs[2][...] = (x1 * _rstd(x1) * (1.0 + sc_ref[...]) + sh_ref[...]).astype(BF16)


def _epi_final(acc, rows, vecs, outs, sums_ref, first, last):
    (x1_ref, t_ref), (g_ref, gf_ref) = rows, vecs
    d = acc.shape[1]
    x2 = x1_ref[...] + g_ref[...] * acc
    r = _rstd(x2)
    xn = x2 * r
    err = xn * gf_ref[...] - t_ref[...]
    dy = err * (1.0 / d)
    dx2 = _norm_bwd(dy * gf_ref[...], xn, r)
    outs[0][...] = dx2
    outs[1][...] = (dx2 * g_ref[...]).astype(BF16)
    _zero_sums_at_start(sums_ref, first)
    sums_ref[0:1, :] += jnp.sum(dy * xn, axis=0, keepdims=True)
    sums_ref[1:2, :] += jnp.sum(dx2 * acc, axis=0, keepdims=True)
    sums_ref[2:3, :] += jnp.sum(err * err, axis=0, keepdims=True)

    if last is not None:
        @pl.when(last)
        def _():
            tot = jnp.sum(sums_ref[2:3, :], axis=1, keepdims=True) * (0.5 / d)
            sums_ref[3:4, :] = jnp.broadcast_to(tot, (1, d))


def _epi_modulate2_bwd(acc, rows, vecs, outs, sums_ref, first, last):
    (x_ref, dres_ref, o_ref), (sc_ref, g_ref) = rows, vecs
    x = x_ref[...]
    r = _rstd(x)
    xn = x * r
    dx = dres_ref[...] + _norm_bwd(acc * (1.0 + sc_ref[...]), xn, r)
    outs[0][...] = dx
    outs[1][...] = (dx * g_ref[...]).astype(BF16)
    _zero_sums_at_start(sums_ref, first)
    sums_ref[0:1, :] += jnp.sum(acc * xn, axis=0, keepdims=True)
    sums_ref[1:2, :] += jnp.sum(acc, axis=0, keepdims=True)
    sums_ref[2:3, :] += jnp.sum(dx * o_ref[...], axis=0, keepdims=True)


def _epi_modulate1_bwd(acc, rows, vecs, outs, sums_ref, first, last):
    (add_ref, x_ref, dres_ref), (sc_ref,) = rows, vecs
    dh = acc + add_ref[...]
    x = x_ref[...]
    r = _rstd(x)
    xn = x * r
    outs[0][...] = dres_ref[...] + _norm_bwd(dh * (1.0 + sc_ref[...]), xn, r)
    _zero_sums_at_start(sums_ref, first)
    sums_ref[0:1, :] += jnp.sum(dh * xn, axis=0, keepdims=True)
    sums_ref[1:2, :] += jnp.sum(dh, axis=0, keepdims=True)


def _modulate_all(x, ctx, mod, mod_ctx, riding, name):
    s, d = x.shape
    t = s + ctx.shape[0]
    ns = s // ROW_TILE
    nc = ctx.shape[0] // ROW_TILE
    nr = riding.n

    def body(*refs):
        x_ref, c_ref, sh_ref, sc_ref, shc_ref, scc_ref = refs[:6]
        h_ref = refs[6 + nr]
        i = pl.program_id(0)
        state = riding.run(i == 0, i == ns + nc - 1, refs[6:6 + nr], refs[7 + nr:7 + 2 * nr], refs[7 + 2 * nr:],
                           middle=i == ns + nc - 3)

        @pl.when(i < ns)
        def _():
            v = x_ref[...]
            h_ref[...] = (v * _rstd(v) * (1.0 + sc_ref[...]) + sh_ref[...]).astype(BF16)

        @pl.when(i >= ns)
        def _():
            v = c_ref[...]
            h_ref[...] = (v * _rstd(v) * (1.0 + scc_ref[...]) + shc_ref[...]).astype(BF16)

        riding.finish(state)

    outs = pl.pallas_call(
        body, name=name, grid=(ns + nc,),
        out_shape=(jax.ShapeDtypeStruct((t, d), BF16), *riding.out_shape),
        in_specs=[pl.BlockSpec((ROW_TILE, d), lambda i: (jnp.minimum(i, ns - 1), 0)),
                  pl.BlockSpec((ROW_TILE, d), lambda i: (jnp.maximum(i - ns, 0), 0)),
                  _vec(0), _vec(1), _vec(0), _vec(1), *riding.specs],
        out_specs=(pl.BlockSpec((ROW_TILE, d), lambda i: (i, 0)), *riding.specs),
        scratch_shapes=riding.scratch,
        compiler_params=_params(("arbitrary",)),
    )(x, ctx, mod, mod, mod_ctx, mod_ctx, *riding.arrays)
    return outs[0], list(outs[1:])


def _modulate_sums(dh, row_off, xsrc):
    s, d = xsrc.shape

    def body(dh_ref, x_ref, sums_ref):
        i = pl.program_id(0)
        x = x_ref[...]
        dhv = dh_ref[...]

        @pl.when(i == 0)
        def _():
            sums_ref[...] = jnp.zeros_like(sums_ref)

        sums_ref[0:1, :] += jnp.sum(dhv * (x * _rstd(x)), axis=0, keepdims=True)
        sums_ref[1:2, :] += jnp.sum(dhv, axis=0, keepdims=True)

    return pl.pallas_call(
        body, name="modulate1_ctx_bwd", grid=(s // ROW_TILE,),
        out_shape=jax.ShapeDtypeStruct((8, d), F32),
        in_specs=[pl.BlockSpec((ROW_TILE, d), lambda i: (i + row_off, 0)), pl.BlockSpec((ROW_TILE, d), lambda i: (i, 0))],
        out_specs=pl.BlockSpec((8, d), lambda i: (0, 0)),
        compiler_params=_params(("arbitrary",)),
    )(dh, xsrc)


def _head_fwd(h_all, win_head, wq, wk, q_gain, kv_gain, cos, sgn, tm, name):
    t, d = h_all.shape
    nq, nkv = wq.shape[1], wk.shape[1]

    def body(h_ref, wi_ref, wq_ref, wk_ref, qg_ref, kg_ref, c_ref, s_ref, z_ref, cq_ref, kvin_ref, qf_ref, kv_ref):
        z = lax.dot_general(h_ref[...], wi_ref[...], NT_DIMS, preferred_element_type=F32)
        z_ref[...] = z
        cos, sgn = c_ref[...], s_ref[...]
        zq = z[:, 0:Q_RANK]
        cq = (zq * _rstd(zq) * qg_ref[...]).astype(BF16)
        cq_ref[...] = cq
        zk = z[:, Q_RANK:Q_RANK + KV_RANK]
        kv_in = jnp.concatenate([(zk * _rstd(zk) * kg_ref[...]).astype(BF16),
                                 _rope(z[:, Q_RANK + KV_RANK:HEAD_COLS], cos, sgn, False).astype(BF16)], axis=1)
        kvin_ref[...] = kv_in
        q = jnp.dot(cq, wq_ref[...], preferred_element_type=F32)
        for h in range(nq // LANES):
            sl = slice(h * LANES, (h + 1) * LANES)
            qf_ref[:, sl] = _rope(q[:, sl], cos, sgn, False).astype(BF16)
        kv_ref[...] = jnp.dot(kv_in, wk_ref[...], preferred_element_type=F32).astype(BF16)

    def row(w):
        return pl.BlockSpec((tm, w), lambda i: (i, 0))

    def whole(a):
        return pl.BlockSpec(a.shape, lambda i: (0, 0))

    return pl.pallas_call(
        body, name=name, grid=(t // tm,),
        out_shape=(jax.ShapeDtypeStruct((t, HEAD_COLS), F32), jax.ShapeDtypeStruct((t, Q_RANK), BF16),
                   jax.ShapeDtypeStruct((t, KV_RANK + LANES), BF16), jax.ShapeDtypeStruct((t, nq), BF16),
                   jax.ShapeDtypeStruct((t, nkv), BF16)),
        in_specs=[row(d), whole(win_head), whole(wq), whole(wk), whole(q_gain), whole(kv_gain), row(LANES), row(LANES)],
        out_specs=(row(HEAD_COLS), row(Q_RANK), row(KV_RANK + LANES), row(nq), row(nkv)),
        compiler_params=_params(("parallel",), VMEM_BIG),
    )(h_all, win_head, wq, wk, q_gain, kv_gain, cos, sgn)


def _head_bwd(dq, dk, dv, z, wq, wk_k, wk_v, win_head, q_gain, kv_gain, cos, sgn, cq, kv_in, h_all, name, *, tile,
              first_block, n_blocks, carry=None):
    t = z.shape[0]
    with_q = dq is not None

    def body(*refs):
        it = iter(refs)
        dq_ref = next(it) if with_q else None
        dk_ref, dv_ref, z_ref, wq_ref, wkk_ref, wkv_ref, wi_ref, qg_ref, kg_ref, c_ref, s_ref = (next(it) for _ in range(11))
        cq_ref = next(it) if with_q else None
        kvin_ref, h_ref = next(it), next(it)
        before = None
        if carry is not None:
            next(it), next(it)
            before = (next(it), next(it), next(it))
        dz_ref, dh_ref, sums_ref = next(it), next(it), next(it)
        gq_ref = next(it) if with_q else None
        grads = (next(it), next(it), next(it))
        i = pl.program_id(0)

        @pl.when(i == 0)
        def _():
            sums_ref[...] = jnp.zeros_like(sums_ref)
            if with_q:
                gq_ref[...] = jnp.zeros_like(gq_ref)
            for k, g_ref in enumerate(grads):
                g_ref[...] = jnp.zeros_like(g_ref) if before is None else before[k][...]

        if with_q:
            gq_ref[...] += lax.dot_general(cq_ref[...], dq_ref[...], TN_DIMS, preferred_element_type=F32)
        grads[0][...] += lax.dot_general(kvin_ref[...], dk_ref[...], TN_DIMS, preferred_element_type=F32)
        grads[1][...] += lax.dot_general(kvin_ref[...], dv_ref[...], TN_DIMS, preferred_element_type=F32)
        if with_q:
            dc = lax.dot_general(dq_ref[...], wq_ref[...], NT_DIMS, preferred_element_type=F32)
            zq = z_ref[:, 0:Q_RANK]
            r = _rstd(zq)
            zn = zq * r
            sums_ref[0:1, :] += jnp.sum(dc * zn, axis=0, keepdims=True)
            dz_ref[:, 0:Q_RANK] = _norm_bwd(dc * qg_ref[...], zn, r).astype(BF16)
        else:
            dz_ref[:, 0:Q_RANK] = jnp.zeros((tile, Q_RANK), BF16)
        dkv = (lax.dot_general(dk_ref[...], wkk_ref[...], NT_DIMS, preferred_element_type=F32)
               + lax.dot_general(dv_ref[...], wkv_ref[...], NT_DIMS, preferred_element_type=F32))
        zk = z_ref[:, Q_RANK:Q_RANK + KV_RANK]
        r = _rstd(zk)
        zn = zk * r
        dc = dkv[:, 0:KV_RANK]
        sums_ref[1:2, 0:KV_RANK] += jnp.sum(dc * zn, axis=0, keepdims=True)
        dz_ref[:, Q_RANK:Q_RANK + KV_RANK] = _norm_bwd(dc * kg_ref[...], zn, r).astype(BF16)
        dz_ref[:, Q_RANK + KV_RANK:HEAD_COLS] = _rope(dkv[:, KV_RANK:KV_RANK + LANES], c_ref[...], s_ref[...],
                                                       True).astype(BF16)
        dh_ref[...] = jnp.dot(dz_ref[...], wi_ref[...], preferred_element_type=F32)
        grads[2][...] += lax.dot_general(dz_ref[...], h_ref[...], TN_DIMS, preferred_element_type=F32)

    def row(w):
        return pl.BlockSpec((tile, w), lambda i: (i + first_block, 0))

    def whole(a):
        return pl.BlockSpec(a.shape, lambda i: (0, 0))

    args = ([dq] if with_q else []) + [dk, dv, z, wq, wk_k, wk_v, win_head, q_gain, kv_gain, cos, sgn]
    args += ([cq] if with_q else []) + [kv_in, h_all]
    in_specs = ([row(dq.shape[1])] if with_q else []) + [
        row(dk.shape[1]), row(dv.shape[1]), row(HEAD_COLS), whole(wq), whole(wk_k), whole(wk_v),
        whole(win_head), whole(q_gain), whole(kv_gain), row(LANES), row(LANES)]
    in_specs += ([row(Q_RANK)] if with_q else []) + [row(kv_in.shape[1]), row(D_MODEL)]
    aliases = {}
    if carry is not None:
        aliases = {len(args): 0, len(args) + 1: 1}
        args += list(carry)
        in_specs += [pl.BlockSpec(memory_space=pl.ANY)] * 2 + [whole(a) for a in carry[2:]]
    grad_shapes = ([(Q_RANK, dq.shape[1])] if with_q else []) + [
        (kv_in.shape[1], dk.shape[1]), (kv_in.shape[1], dv.shape[1]), (HEAD_COLS, D_MODEL)]
    return pl.pallas_call(
        body, name=name, grid=(n_blocks,),
        out_shape=(jax.ShapeDtypeStruct((t, HEAD_COLS), BF16), jax.ShapeDtypeStruct((t, D_MODEL), F32),
                   jax.ShapeDtypeStruct((8, Q_RANK), F32), *[jax.ShapeDtypeStruct(g, F32) for g in grad_shapes]),
        in_specs=in_specs,
        out_specs=(row(HEAD_COLS), row(D_MODEL), pl.BlockSpec((8, Q_RANK), lambda i: (0, 0)),
                   *[pl.BlockSpec(g, lambda i: (0, 0)) for g in grad_shapes]),
        input_output_aliases=aliases,
        compiler_params=_params(("arbitrary",), VMEM_BIG),
    )(*args)


def _shift_rows(u, s):
    rowi = lax.broadcasted_iota(jnp.int32, u.shape, 0)
    prev = jnp.where(rowi == 0, 0.0, pltpu.roll(u, 1, 0))
    nxt = jnp.where(rowi == s - 1, 0.0, pltpu.roll(u, s - 1, 0))
    return prev, nxt


def _conv_fwd(z_conv, cw, a_cat, name):
    s = z_conv.shape[0]

    def body(z_ref, w_ref, a_in_ref, o_ref):
        del a_in_ref
        gb, gc, xv = z_ref[:, 0:LANES], z_ref[:, LANES:2 * LANES], z_ref[:, 2 * LANES:3 * LANES]
        u = gc * xv
        prev, nxt = _shift_rows(u, s)
        y = w_ref[0:1, :] * prev + w_ref[1:2, :] * u + w_ref[2:3, :] * nxt
        o_ref[...] = (gb * y).astype(BF16)

    return pl.pallas_call(
        body, name=name, grid=(CONV_W // LANES,),
        out_shape=jax.ShapeDtypeStruct(a_cat.shape, a_cat.dtype),
        in_specs=[pl.BlockSpec((s, 3 * LANES), lambda j: (0, j)), pl.BlockSpec((3, LANES), lambda j: (0, j)),
                  pl.BlockSpec(memory_space=pl.ANY)],
        out_specs=pl.BlockSpec((s, LANES), lambda j: (0, 4 + j)),
        input_output_aliases={2: 0},
        compiler_params=_params(("parallel",), VMEM_BIG),
    )(z_conv, cw, a_cat)


def _conv_bwd(z_conv, cw, da, name):
    s = z_conv.shape[0]

    def body(z_ref, w_ref, da_ref, dz_ref, dw_ref):
        gb, gc, xv = z_ref[:, 0:LANES], z_ref[:, LANES:2 * LANES], z_ref[:, 2 * LANES:3 * LANES]
        u = gc * xv
        prev, nxt = _shift_rows(u, s)
        dcv = da_ref[...]
        dz_ref[:, 0:LANES] = (dcv * (w_ref[0:1, :] * prev + w_ref[1:2, :] * u + w_ref[2:3, :] * nxt)).astype(BF16)
        dy = dcv * gb
        dw_ref[0:1, :] = jnp.sum(dy * prev, axis=0, keepdims=True)
        dw_ref[1:2, :] = jnp.sum(dy * u, axis=0, keepdims=True)
        dw_ref[2:3, :] = jnp.sum(dy * nxt, axis=0, keepdims=True)
        dyp, dyn = _shift_rows(dy, s)
        du = w_ref[0:1, :] * dyn + w_ref[1:2, :] * dy + w_ref[2:3, :] * dyp
        dz_ref[:, LANES:2 * LANES] = (du * xv).astype(BF16)
        dz_ref[:, 2 * LANES:3 * LANES] = (du * gc).astype(BF16)

    blk = pl.BlockSpec((s, 3 * LANES), lambda j: (0, j))
    cws = pl.BlockSpec((3, LANES), lambda j: (0, j))
    return pl.pallas_call(
        body, name=name, grid=(CONV_W // LANES,),
        out_shape=(jax.ShapeDtypeStruct(z_conv.shape, BF16), jax.ShapeDtypeStruct((3, CONV_W), F32)),
        in_specs=[blk, cws, pl.BlockSpec((s, LANES), lambda j: (0, 4 + j))], out_specs=(blk, cws),
        compiler_params=_params(("parallel",), VMEM_BIG),
    )(z_conv, cw, da)


ATT_TQ = 512
ATT_Q_STEP = 1024
ATT_TQ_BWD = 512


def _head_mask(shape, hh):
    lane = lax.broadcasted_iota(jnp.int32, shape, 1)
    return (lane >= hh * V_DIM) & (lane < (hh + 1) * V_DIM)


def _attn_fwd(qf, kv, s, riding, name):
    t = kv.shape[0]
    step = min(ATT_Q_STEP, s)
    nq = s // step
    nr = riding.n

    def body(*refs):
        q_ref, k_ref, v_ref = refs[:3]
        o_ref, ob_ref, st_ref = refs[3 + nr:6 + nr]
        p, i = pl.program_id(0), pl.program_id(1)
        state = riding.run((p == 0) & (i == 0), (p == N_HEADS // 2 - 1) & (i == nq - 1),
                           refs[3:3 + nr], refs[6 + nr:6 + 2 * nr], refs[6 + 2 * nr:],
                           middle=(p == N_HEADS // 2 - 2) & (i == nq // 2))
        v = v_ref[...]
        vlane = lax.broadcasted_iota(jnp.int32, v.shape, 1)
        one_lane = [(1 - hh) * V_DIM for hh in range(2)]
        vm = [jnp.where(_head_mask(v.shape, hh), v, jnp.where(vlane == one_lane[hh], 1.0, 0.0).astype(BF16))
              for hh in range(2)]

        def block(r, carry):
            rows = pl.ds(pl.multiple_of(r * ATT_TQ, ATT_TQ), ATT_TQ)
            olane = lax.broadcasted_iota(jnp.int32, (ATT_TQ, LANES), 1)
            acc = jnp.zeros((ATT_TQ, LANES), F32)
            stat = jnp.zeros((ATT_TQ, LANES), F32)
            scores = [lax.dot_general(q_ref[rows, hh * LANES:(hh + 1) * LANES], k_ref[:, hh * LANES:(hh + 1) * LANES],
                                      NT_DIMS, preferred_element_type=F32) for hh in range(2)]
            maxes = [jnp.max(sc, axis=1, keepdims=True) for sc in scores]
            exps = [jnp.exp2((sc - mx) * EXP2_SCALE).astype(BF16) for sc, mx in zip(scores, maxes)]
            for hh in range(2):
                mx = maxes[hh]
                res = jnp.dot(exps[hh], vm[hh], preferred_element_type=F32)
                den = jnp.sum(jnp.where(olane == one_lane[hh], res, 0.0), axis=1, keepdims=True)
                acc = acc + jnp.where(_head_mask(res.shape, hh), res * (1.0 / den), 0.0)
                stat = stat + jnp.where(olane == hh, mx * EXP2_SCALE + jnp.log(den) * LOG2_E, 0.0)
            o_ref[rows, :] = acc
            ob_ref[rows, :] = acc.astype(BF16)
            st_ref[:, rows] = stat.T[0:8, :]
            return carry

        lax.fori_loop(0, step // ATT_TQ, block, 0)
        riding.finish(state)

    o_spec = pl.BlockSpec((step, LANES), lambda p, i: (i, p))
    outs = pl.pallas_call(
        body, name=name, grid=(N_HEADS // 2, nq),
        out_shape=(jax.ShapeDtypeStruct((s, N_HEADS * V_DIM), F32),
                   jax.ShapeDtypeStruct((s, D_MODEL), BF16),
                   jax.ShapeDtypeStruct((N_HEADS // 2 * 8, s), F32), *riding.out_shape),
        in_specs=[pl.BlockSpec((step, 2 * LANES), lambda p, i: (i, p)),
                  pl.BlockSpec((t, 2 * LANES), lambda p, i: (0, p)),
                  pl.BlockSpec((t, LANES), lambda p, i: (0, N_HEADS + p)), *riding.specs],
        out_specs=(o_spec, o_spec, pl.BlockSpec((8, step), lambda p, i: (p, i)), *riding.specs),
        scratch_shapes=riding.scratch,
        compiler_params=_params(("arbitrary", "arbitrary"), VMEM_BIG),
    )(qf, kv, kv, *riding.arrays)
    return outs[0], outs[1], outs[2], list(outs[3:])


def _attn_bwd(qf, kv, o, da, stats, cos, sgn, riding, name):
    s, t = o.shape[0], kv.shape[0]
    ATT_TQ = ATT_TQ_BWD
    nq = s // ATT_TQ
    nr = riding.n

    def body(*refs):
        q_ref, k_ref, v_ref, o_ref, do_ref, st_ref, c_ref, s_ref = refs[:8]
        dq_ref, dk_ref, dv_ref = refs[8 + nr:11 + nr]
        dk_acc, dv_acc = refs[11 + 2 * nr:13 + 2 * nr]
        p, i = pl.program_id(0), pl.program_id(1)
        state = riding.run((p == 0) & (i == 0), (p == N_HEADS // 2 - 1) & (i == nq - 1),
                           refs[8:8 + nr], refs[11 + nr:11 + 2 * nr], refs[13 + 2 * nr:])

        @pl.when(i == 0)
        def _():
            dk_acc[...] = jnp.zeros_like(dk_acc)
            dv_acc[...] = jnp.zeros_like(dv_acc)

        v = v_ref[...]
        do = do_ref[...]
        od = do * o_ref[...]
        ones = jnp.ones((8, LANES), F32)
        for hh in range(2):
            sl = slice(hh * LANES, (hh + 1) * LANES)
            q, k = q_ref[:, sl], k_ref[:, sl]
            mask = _head_mask(do.shape, hh)
            dom = jnp.where(mask, do, 0.0).astype(BF16)
            delta = lax.dot_general(ones, jnp.where(mask, od, 0.0), NT_DIMS, preferred_element_type=F32,
                                    precision=lax.Precision.HIGHEST)[0:1, :]
            st = lax.dot_general(k, q, NT_DIMS, preferred_element_type=F32)
            pt = jnp.exp2(st * EXP2_SCALE - st_ref[hh:hh + 1, :]).astype(BF16)
            dpt = lax.dot_general(v, dom, NT_DIMS, preferred_element_type=F32)
            dst = (pt.astype(F32) * (dpt - delta)).astype(BF16)
            dv_acc[...] += jnp.dot(pt, dom, preferred_element_type=F32)
            dk_acc[:, sl] += jnp.dot(dst, q, preferred_element_type=F32)
            dq = lax.dot_general(dst, k, TN_DIMS, preferred_element_type=F32) * ATTN_SCALE
            dq_ref[:, sl] = _rope(dq, c_ref[...], s_ref[...], True).astype(BF16)

        @pl.when(i == nq - 1)
        def _():
            dk_ref[...] = (dk_acc[...] * ATTN_SCALE).astype(BF16)
            dv_ref[...] = dv_acc[...].astype(BF16)

        riding.finish(state)

    o_spec = pl.BlockSpec((ATT_TQ, LANES), lambda p, i: (i, p))
    tab = pl.BlockSpec((ATT_TQ, LANES), lambda p, i: (i, 0))
    outs = pl.pallas_call(
        body, name=name, grid=(N_HEADS // 2, nq),
        out_shape=(jax.ShapeDtypeStruct((s, N_HEADS * LANES), BF16),
                   jax.ShapeDtypeStruct((t, N_HEADS * LANES), BF16),
                   jax.ShapeDtypeStruct((t, N_HEADS * V_DIM), BF16), *riding.out_shape),
        in_specs=[pl.BlockSpec((ATT_TQ, 2 * LANES), lambda p, i: (i, p)),
                  pl.BlockSpec((t, 2 * LANES), lambda p, i: (0, p)),
                  pl.BlockSpec((t, LANES), lambda p, i: (0, N_HEADS + p)),
                  o_spec, o_spec,
                  pl.BlockSpec((8, ATT_TQ), lambda p, i: (p, i)), tab, tab, *riding.specs],
        out_specs=(pl.BlockSpec((ATT_TQ, 2 * LANES), lambda p, i: (i, p)),
                   pl.BlockSpec((t, 2 * LANES), lambda p, i: (0, p)),
                   pl.BlockSpec((t, LANES), lambda p, i: (0, p)), *riding.specs),
        scratch_shapes=[pltpu.VMEM((t, 2 * LANES), F32), pltpu.VMEM((t, LANES), F32), *riding.scratch],
        compiler_params=_params(("arbitrary", "arbitrary"), VMEM_BIG),
    )(qf, kv, kv, o, da, stats, cos, sgn, *riding.arrays)
    return outs[0], outs[1], outs[2], list(outs[3:])


def _silu(x):
    return x * (1.0 / (1.0 + jnp.exp(-x)))


def _prologue(c_rows, c_ctx, w_mod, b_cols, extra_rows, name):
    d, cols = c_rows.shape[1], w_mod.shape[1]

    def body(c_ref, cctx_ref, wmod_ref, b_ref, x_ref, a_ref, modg_ref, c_all, blk, c_send, c_recv, m_send, m_recv):
        _direct_gather(c_ref, c_all, c_send, c_recv)()
        a_ref[...] = jnp.zeros_like(a_ref)
        for j in range(N_DEV):
            a_ref[j:j + 1, :] = c_all[j, 0:1, :]
        a_ref[N_DEV:N_DEV + 1, :] = cctx_ref[...]
        mod = jnp.dot(_silu(a_ref[...]), wmod_ref[...], preferred_element_type=F32,
                      precision=lax.Precision.HIGHEST) + b_ref[...]
        blk[...] = jnp.zeros_like(blk)
        for p in range(N_DEV):
            blk[p, 0:1, :] = mod[p:p + 1, :]
            blk[p, 1:2, :] = mod[N_DEV:N_DEV + 1, :]
            blk[p, 2:5, :] = x_ref[...]
        _direct_gather(blk, modg_ref, m_send, m_recv, per_peer=True)()

    vmem = pl.BlockSpec(memory_space=pltpu.VMEM)
    return pl.pallas_call(
        body, name=name,
        out_shape=(jax.ShapeDtypeStruct((16, d), F32), jax.ShapeDtypeStruct((N_DEV, 8, cols), F32)),
        in_specs=[vmem] * 5, out_specs=(vmem, vmem),
        scratch_shapes=[pltpu.VMEM((N_DEV, 8, d), F32), pltpu.VMEM((N_DEV, 8, cols), F32)]
        + [pltpu.SemaphoreType.DMA((7,)) for _ in range(4)],
        compiler_params=_params(None, VMEM_BIG),
    )(c_rows, c_ctx, w_mod, b_cols, extra_rows)


def _adaln_bwd(a_t, w, d_ex, d_ctx, d_all, name):
    def body(at_ref, w_ref, dex_ref, dctx_ref, dall_ref, gw_ref, dsil_ref, dsum_ref):
        sil_t = _silu(at_ref[...])
        dctx = dctx_ref[...]
        row = dctx[0:1, :]
        for j in range(1, N_DEV):
            row = row + dctx[j:j + 1, :]
        rowi = lax.broadcasted_iota(jnp.int32, dctx.shape, 0)
        ctx_rows = jnp.where(rowi == 0, jnp.broadcast_to(row, dctx.shape), 0.0)
        hi = lax.Precision.HIGHEST
        d_rows = jnp.concatenate([dex_ref[...], ctx_rows], axis=0)
        gw_ref[...] = jnp.dot(sil_t, d_rows, preferred_element_type=F32, precision=hi)
        dsil_ref[...] = lax.dot_general(ctx_rows, w_ref[...], NT_DIMS, preferred_element_type=F32, precision=hi)
        tot = dall_ref[0]
        for j in range(1, N_DEV):
            tot = tot + dall_ref[j]
        dsum_ref[...] = tot

    return pl.pallas_call(
        body, name=name,
        out_shape=(jax.ShapeDtypeStruct(w.shape, F32), jax.ShapeDtypeStruct((8, w.shape[0]), F32),
                   jax.ShapeDtypeStruct(d_all.shape[1:], F32)),
        compiler_params=_params(None, VMEM_BIG),
    )(a_t, w, d_ex, d_ctx, d_all)


SMALL_ROWS = 24
SMALL_MISC, SMALL_CW, SMALL_LOSS = 16, 18, 21


def _pack_small(sums1, sums2, fsums, sums1c, psums, psums_c, d_cw, cols, name):
    d = D_MODEL

    def body(s1_ref, s2_ref, f_ref, s1c_ref, p_ref, pc_ref, cw_ref, o_ref):
        o_ref[...] = jnp.zeros_like(o_ref)

        def blocks(row0, pieces):
            for j in range(N_DEV):
                lo, hi = j * cols, (j + 1) * cols
                for k, (ref, r) in enumerate(pieces):
                    a, b = max(lo, k * d), min(hi, (k + 1) * d)
                    if a < b:
                        o_ref[row0 + j:row0 + j + 1, a - lo:b - lo] = ref[r:r + 1, a - k * d:b - k * d]

        blocks(0, [(s1_ref, 1), (s1_ref, 0), (s2_ref, 2), (s2_ref, 1), (s2_ref, 0), (f_ref, 1)])
        blocks(N_DEV, [(s1c_ref, 1), (s1c_ref, 0)])
        head = Q_RANK + KV_RANK
        o_ref[SMALL_MISC:SMALL_MISC + 1, 0:Q_RANK] = p_ref[0:1, :]
        o_ref[SMALL_MISC:SMALL_MISC + 1, Q_RANK:head] = p_ref[1:2, 0:KV_RANK] + pc_ref[1:2, 0:KV_RANK]
        o_ref[SMALL_MISC:SMALL_MISC + 1, head:cols] = f_ref[0:1, 0:cols - head]
        o_ref[SMALL_MISC + 1:SMALL_MISC + 2, 0:d - (cols - head)] = f_ref[0:1, cols - head:d]
        for r in range(3):
            o_ref[SMALL_CW + r:SMALL_CW + r + 1, 0:CONV_W] = cw_ref[r:r + 1, :]
        o_ref[SMALL_LOSS:SMALL_LOSS + 1, :] = f_ref[3:4, 0:cols]

    return pl.pallas_call(body, name=name, out_shape=jax.ShapeDtypeStruct((SMALL_ROWS, cols), F32))(
        sums1, sums2, fsums, sums1c, psums, psums_c, d_cw)


def _adam_math(w, g, m, v):
    nm = ADAM_B1 * m + (1.0 - ADAM_B1) * g
    nv = ADAM_B2 * v + (1.0 - ADAM_B2) * (g * g)
    m_hat = nm / (1.0 - ADAM_B1 ** ADAM_STEP)
    v_hat = nv / (1.0 - ADAM_B2 ** ADAM_STEP)
    return -ADAM_LR * (m_hat / (jnp.sqrt(v_hat) + ADAM_EPS) + ADAM_WD * w), nm, nv


def _small_update(dsum, dsil_all, g_cw, params, name):
    d = D_MODEL
    n = len(params)
    cols = dsum.shape[1]

    def body(*refs):
        dsum_ref, dsil_ref, gcw_ref = refs[:3]
        wmv = refs[3:3 + 3 * n]
        outs = refs[3 + 3 * n:]
        tot = dsil_ref[0]
        for j in range(1, N_DEV):
            tot = tot + dsil_ref[j]
        cv = wmv[0][...]
        sg = 1.0 / (1.0 + jnp.exp(-cv))
        off = Q_RANK + KV_RANK
        misc = dsum_ref[SMALL_MISC:SMALL_MISC + 1, :]
        grads = [tot[0:1, :] * (sg * (1.0 + cv * (1.0 - sg))),
                 jnp.concatenate([dsum_ref[j:j + 1, :] + dsum_ref[N_DEV + j:N_DEV + j + 1, :] for j in range(N_DEV)],
                                 axis=1),
                 misc[:, 0:Q_RANK], misc[:, Q_RANK:off],
                 jnp.concatenate([misc[:, off:cols], dsum_ref[SMALL_MISC + 1:SMALL_MISC + 2, 0:d - (cols - off)]],
                                 axis=1),
                 gcw_ref[...]]
        for p, g in enumerate(grads):
            w_ref, m_ref, v_ref = wmv[3 * p:3 * p + 3]
            at = 0 if len(w_ref.shape) == 3 else Ellipsis
            res = (g,) + _adam_math(w_ref[at], g, m_ref[at], v_ref[at])
            for q, val in enumerate(res):
                outs[4 * p + q][at] = val

    flat = [a for wmv in params for a in wmv]
    out_shape = tuple(jax.ShapeDtypeStruct(wmv[0].shape, F32) for wmv in params for _ in range(4))
    outs = pl.pallas_call(body, name=name, out_shape=out_shape)(dsum, dsil_all, g_cw, *flat)
    return [outs[4 * p:4 * p + 4] for p in range(n)]


def _adamw(w, g, m, v, name, slots=False):
    _, rows, cols = w.shape
    tr = _pick(rows, (256, 128, 64, 32, 16, 8))

    def body(w_ref, g_ref, m_ref, v_ref, *outs):
        if slots:
            gv = g_ref[0].astype(F32)
            for j in range(1, g.shape[0]):
                gv = gv + g_ref[j].astype(F32)
            outs[0][...] = gv
        else:
            gv = g_ref[...]
        d_ref, nm_ref, nv_ref = outs[-3:]
        d_ref[...], nm_ref[...], nv_ref[...] = _adam_math(w_ref[...], gv, m_ref[...], v_ref[...])

    blk = pl.BlockSpec((None, tr, cols), lambda i: (0, i, 0))
    g_spec = (pl.BlockSpec((g.shape[0], tr, cols), lambda i: (0, i, 0)) if slots
              else pl.BlockSpec((tr, cols), lambda i: (i, 0)))
    sh = jax.ShapeDtypeStruct((1, rows, cols), F32)
    n_out = 4 if slots else 3
    return pl.pallas_call(
        body, name=name, grid=(rows // tr,), out_shape=(sh,) * n_out,
        in_specs=[blk, g_spec, blk, blk], out_specs=(blk,) * n_out,
        compiler_params=_params(("parallel",), VMEM_BIG),
    )(w, g, m, v)


def _rope_tables(s, l):
    tok = np.arange(s)
    row = (tok // GRID_W).astype(np.float32)
    col = (tok % GRID_W).astype(np.float32)
    half = QK_ROPE // 2
    freqs = np.float32(ROPE_THETA) ** (-np.arange(0, half, 2, dtype=np.float32) / np.float32(half))
    dd = np.arange(QK_ROPE)
    pos = np.where((dd // half)[None, :] == 0, row[:, None], col[:, None]).astype(np.float32)
    ang = (pos * freqs[dd % (half // 2)][None, :]).astype(np.float32)
    sin = np.sin(ang).astype(np.float32)
    cos_t = np.ones((s + l, LANES), np.float32)
    sgn_t = np.zeros((s + l, LANES), np.float32)
    cos_t[:s, QK_NOPE:QK_NOPE + QK_ROPE] = np.cos(ang)
    sgn_t[:s, QK_NOPE:QK_NOPE + QK_ROPE] = np.where(((dd % half) // (half // 2))[None, :] == 0, -sin, sin)
    return jnp.asarray(cos_t), jnp.asarray(sgn_t)


def _slots_to_cols(g):
    return g.transpose(1, 0, 2).reshape(g.shape[1], N_DEV * g.shape[2])


def _cols_to_slots(w):
    return w.reshape(w.shape[0], N_DEV, w.shape[1] // N_DEV).transpose(1, 0, 2)


def _unpack_small_weights(g_in_t, g_uq, g_ukv):
    w_t = g_in_t.reshape(N_DEV * g_in_t.shape[1], D_MODEL)
    zeros = jnp.zeros((QK_NOPE, D_MODEL), BF16)
    win_head_t = jnp.concatenate([w_t[:Q_RANK + KV_RANK], zeros, w_t[Q_RANK + KV_RANK:MLA_IN],
                                  zeros[:LANES - QK_NOPE - QK_ROPE]], axis=0)
    win_conv_t = w_t[MLA_IN:].reshape(3, CONV_W // LANES, LANES, D_MODEL).transpose(1, 0, 2, 3)
    win_conv_t = win_conv_t.reshape(3 * CONV_W, D_MODEL)
    w_uq = _slots_to_cols(g_uq).reshape(Q_RANK, N_HEADS, QK_NOPE + QK_ROPE)
    wq = jnp.pad(w_uq, ((0, 0), (0, 0), (0, LANES - QK_NOPE - QK_ROPE))).reshape(Q_RANK, N_HEADS * LANES)
    w_ukv = _slots_to_cols(g_ukv).reshape(KV_RANK, N_HEADS, QK_NOPE + V_DIM)
    k_top = jnp.pad(w_ukv[:, :, :QK_NOPE], ((0, 0), (0, 0), (0, LANES - QK_NOPE))).reshape(KV_RANK, N_HEADS * LANES)
    v_top = w_ukv[:, :, QK_NOPE:].reshape(KV_RANK, N_HEADS * V_DIM)
    eye = jnp.pad(jnp.eye(QK_ROPE, dtype=BF16), ((QK_NOPE, LANES - QK_NOPE - QK_ROPE),) * 2)
    wk = jnp.concatenate([
        jnp.concatenate([k_top, v_top], axis=1),
        jnp.concatenate([jnp.tile(eye, (1, N_HEADS)), jnp.zeros((LANES, N_HEADS * V_DIM), BF16)], axis=1)], axis=0)
    return win_head_t, win_conv_t, wq, wk


def _pack_small_grads(d_head_t, d_conv_t, d_wq, d_wkk, d_wkv):
    d_conv_t = d_conv_t.reshape(CONV_W // LANES, 3, LANES, D_MODEL).transpose(1, 0, 2, 3).reshape(3 * CONV_W, D_MODEL)
    rope0 = Q_RANK + KV_RANK + QK_NOPE
    g_in_t = jnp.concatenate([d_head_t[:Q_RANK + KV_RANK], d_head_t[rope0:rope0 + QK_ROPE], d_conv_t], axis=0)
    g_in_t = g_in_t.reshape(N_DEV, -1, D_MODEL).astype(BF16)
    g_uq = d_wq.reshape(Q_RANK, N_HEADS, LANES)[:, :, :QK_NOPE + QK_ROPE].reshape(Q_RANK, -1)
    g_kn = d_wkk[:KV_RANK].reshape(KV_RANK, N_HEADS, LANES)[:, :, :QK_NOPE]
    g_v = d_wkv[:KV_RANK].reshape(KV_RANK, N_HEADS, V_DIM)
    g_ukv = jnp.concatenate([g_kn, g_v], axis=2).reshape(KV_RANK, -1)
    return [g_in_t] + [_cols_to_slots(g).astype(BF16) for g in (g_uq, g_ukv)]


def kernel(x, c, ctx, c_ctx, w_mod, b_mod, w_in, q_norm_g, w_uq, kv_norm_g, w_ukv, conv_w, w_out, w_mlp1, w_mlp2, final_norm_g, loss_target, m_c_ctx, m_w_mod, m_b_mod, m_w_in, m_q_norm_g, m_w_uq, m_kv_norm_g, m_w_ukv, m_conv_w, m_w_out, m_w_mlp1, m_w_mlp2, m_final_norm_g, v_c_ctx, v_w_mod, v_b_mod, v_w_in, v_q_norm_g, v_w_uq, v_kv_norm_g, v_w_ukv, v_conv_w, v_w_out, v_w_mlp1, v_w_mlp2, v_final_norm_g):
    me = _my_index()
    x2d, ctx2d, tgt = x[0], ctx[0], loss_target[0]
    s, l = x2d.shape[0], ctx2d.shape[0]
    t = s + l
    d = D_MODEL
    mod_cols = w_mod.shape[2]
    cw_cols = conv_w.shape[2]

    b_cols = lax.dynamic_slice(b_mod, (0, me * mod_cols), (1, mod_cols))
    cw_blk = jnp.pad(conv_w[0], ((0, 0), (0, mod_cols - cw_cols)))
    a_rows, gathered = _prologue(jnp.pad(c, ((0, 7), (0, 0))), c_ctx[None, :], w_mod[0], b_cols, cw_blk,
                                 "prologue")
    mod_mine = gathered[:, 0, :].reshape(1, 6 * d)
    mod_ctx = gathered[:, 1, :].reshape(1, 6 * d)
    cw_full = gathered[:, 2:5, :cw_cols].transpose(1, 0, 2).reshape(3, CONV_W)

    early = [w.astype(BF16) for w in (w_in[0].T, w_uq[0], w_ukv[0])]
    late = [w.astype(BF16) for w in (w_out[0], w_mlp1[0], w_mlp2[0])]
    h_all, (g_in, g_uq, g_ukv) = _modulate_all(x2d, ctx2d, mod_mine, mod_ctx, _RidingGather(early),
                                               "modulate1")
    win_head, win_conv, wq, wk = _unpack_small_weights(g_in, g_uq, g_ukv)
    wk_k, wk_v = wk[:, :N_HEADS * LANES], wk[:, N_HEADS * LANES:]
    cos, sgn = _rope_tables(s, l)

    tm_t = _pick(t, (1088, 768, 256))
    z_head, cq, kv_in, qf, kv = _head_fwd(h_all, win_head, wq, wk, q_norm_g, kv_norm_g, cos, sgn, tm_t, "head_fwd")
    z_conv = _matmul(h_all, win_conv, mode="nt", name="in_proj_conv", m=s, tm=1024, tn=1536, tk=1024)
    attn, a_cat, stats, (g_out, w1, g_w2) = _attn_fwd(qf, kv, s, _RidingGather(late), "attn_fwd")
    wo = g_out.reshape(d, d)
    w2 = g_w2.reshape(D_FF, d)
    a_cat = _conv_fwd(z_conv, cw_full, a_cat, "conv_fwd")
    (o, x1, h2), _ = _matmul_rows(a_cat, wo, _epi_resid_modulate, mode="nn", name="out_proj", tm=1024, tk=1024,
                                  rows=[x2d], vecs=[(mod_mine, 2), (mod_mine, 3), (mod_mine, 4)],
                                  out_dtypes=[F32, F32, BF16])
    u1, act = _matmul(h2, w1, mode="nn", name="mlp_up", tm=4096, tk=1024, epilogue="relu2", slots="b_cols")
    (dx2, dm, fsums), _ = _matmul_rows(act, w2, _epi_final, mode="nn", name="mlp_down", tm=512, tk=4096,
                                       rows=[x1, tgt], vecs=[(mod_mine, 5), (final_norm_g[None, :], 0)],
                                       out_dtypes=[F32, BF16], sums=True)

    d_w2 = _matmul(act, dm, mode="tn", name="d_w_mlp2", out_dtype=BF16, tm=1024, tn=1024, tk=4096)
    du1 = _matmul(dm, w2, mode="nt", name="d_act", out_dtype=BF16, tm=2048, tn=1024, tk=1024,
                  epilogue="drelu2", extra=(u1,))
    d_w1 = _matmul(h2, du1, mode="tn", name="d_w_mlp1", out_dtype=BF16, tm=1024, tk=4096, slots="out")
    (dx1, do, sums2), _ = _matmul_rows(du1, w1, _epi_modulate2_bwd, mode="nt", name="d_h2", tm=512, tk=4096,
                                       slots="b_contract", rows=[x1, dx2, o], vecs=[(mod_mine, 4), (mod_mine, 2)],
                                       out_dtypes=[F32, BF16], sums=True)
    d_wo = _matmul(a_cat, do, mode="tn", name="d_w_out", out_dtype=BF16, tm=1024, tn=1024, tk=2048)
    da = _matmul(do, wo, mode="nt", name="d_a", tm=1024, tn=1024, tk=1024)
    dz_conv, d_cw = _conv_bwd(z_conv, cw_full, da, "conv_bwd")
    ready = [d_wo.reshape(N_DEV, d // N_DEV, d), d_w1, d_w2.reshape(N_DEV, D_FF // N_DEV, d)]
    dq, dk, dv, rode = _attn_bwd(qf, kv, attn, da, stats, cos, sgn, _Riding(ready), "attn_bwd")
    head_args = (z_head, wq, wk_k, wk_v, win_head, q_norm_g, kv_norm_g, cos, sgn, cq, kv_in, h_all)
    dz_head, dh_head, psums, d_wq, *carried = _head_bwd(dq, dk, dv, *head_args, "head_bwd", tile=HEAD_BWD_TILE,
                                                        first_block=0, n_blocks=s // HEAD_BWD_TILE)
    _, dh_head, psums_c, d_wkk, d_wkv, d_head = _head_bwd(
        None, dk, dv, *head_args, "head_bwd_ctx", tile=ROW_TILE, first_block=s // ROW_TILE, n_blocks=l // ROW_TILE,
        carry=(dz_head, dh_head, *carried))
    d_conv = _matmul(dz_conv, h_all, mode="tn", name="d_w_in_conv", k=s, tm=1536, tn=1024, tk=2048)
    send = _pack_small_grads(d_head, d_conv, d_wq, d_wkk, d_wkv)
    (grad_x, sums1), got = _matmul_rows(dz_conv, win_conv, _epi_modulate1_bwd, mode="nn", name="d_h1", tm=max(s // 8, ROW_TILE),
                                        tk=win_conv.shape[0], rows=[dh_head, x2d, dx1], vecs=[(mod_mine, 1)],
                                        out_dtypes=[F32], sums=True, riding=_RidingReduce(send))
    sums1c = _modulate_sums(dh_head, s // ROW_TILE, ctx2d)

    small = _pack_small(sums1, sums2, fsums, sums1c, psums, psums_c, d_cw, mod_cols, "pack_small")
    (d_all,) = _all_gather([small], "gather_small_grads", True)
    d_ex = lax.dynamic_index_in_dim(d_all, me, axis=1, keepdims=False)
    d_ctx = lax.dynamic_index_in_dim(d_all, N_DEV + me, axis=1, keepdims=False)
    g_w_mod, dsil, dsum = _adaln_bwd(a_rows.T, w_mod[0], d_ex, d_ctx, d_all, "adaln_bwd")
    (dsil_all,) = _all_gather([dsil], "gather_d_cctx", True)
    loss = dsum[SMALL_LOSS, 0]
    g_cw = lax.dynamic_slice(dsum, (SMALL_CW, me * cw_cols), (3, cw_cols))

    slots = dict(zip(["w_in", "w_uq", "w_ukv"], got))
    slots.update(zip(["w_out", "w_mlp1", "w_mlp2"], rode))

    grads = {}
    weights = {"c_ctx": c_ctx, "w_mod": w_mod, "b_mod": b_mod, "w_in": w_in, "q_norm_g": q_norm_g, "w_uq": w_uq,
               "kv_norm_g": kv_norm_g, "w_ukv": w_ukv, "conv_w": conv_w, "w_out": w_out, "w_mlp1": w_mlp1,
               "w_mlp2": w_mlp2, "final_norm_g": final_norm_g}
    m_in = {"c_ctx": m_c_ctx, "w_mod": m_w_mod, "b_mod": m_b_mod, "w_in": m_w_in, "q_norm_g": m_q_norm_g,
            "w_uq": m_w_uq, "kv_norm_g": m_kv_norm_g, "w_ukv": m_w_ukv, "conv_w": m_conv_w, "w_out": m_w_out,
            "w_mlp1": m_w_mlp1, "w_mlp2": m_w_mlp2, "final_norm_g": m_final_norm_g}
    v_in = {"c_ctx": v_c_ctx, "w_mod": v_w_mod, "b_mod": v_b_mod, "w_in": v_w_in, "q_norm_g": v_q_norm_g,
            "w_uq": v_w_uq, "kv_norm_g": v_kv_norm_g, "w_ukv": v_w_ukv, "conv_w": v_conv_w, "w_out": v_w_out,
            "w_mlp1": v_w_mlp1, "w_mlp2": v_w_mlp2, "final_norm_g": v_final_norm_g}
    names = list(weights)
    small_names = ["c_ctx", "b_mod", "q_norm_g", "kv_norm_g", "final_norm_g", "conv_w"]
    delta, new_m, new_v = {}, {}, {}

    def as_rows(a):
        return a[None, :] if a.ndim == 1 else a

    small_out = _small_update(dsum, dsil_all, g_cw, [[as_rows(src[n]) for src in (weights, m_in, v_in)]
                                                      for n in small_names], "small_update")
    for n, outs in zip(small_names, small_out):
        grads[n], delta[n], new_m[n], new_v[n] = [a.reshape(weights[n].shape) for a in outs]
    for n in names:
        if n in small_names:
            continue
        if n == "w_in":
            wmv = [jnp.swapaxes(src[n], 1, 2) for src in (weights, m_in, v_in)]
            outs = _adamw(wmv[0], slots[n], wmv[1], wmv[2], "adamw_" + n, slots=True)
            grads[n], delta[n], new_m[n], new_v[n] = [jnp.swapaxes(a, 1, 2) for a in outs]
        elif n in slots:
            grads[n], delta[n], new_m[n], new_v[n] = _adamw(weights[n], slots[n], m_in[n], v_in[n], "adamw_" + n,
                                                            slots=True)
        else:
            delta[n], new_m[n], new_v[n] = _adamw(weights[n], g_w_mod, m_in[n], v_in[n], "adamw_" + n)
            grads[n] = g_w_mod[None]

    return (loss, grad_x[None], *[grads[n] for n in names], *[delta[n] for n in names],
            *[new_m[n] for n in names], *[new_v[n] for n in names])
```

```python
import math

import jax
import jax.numpy as jnp
import numpy as np
from jax import lax
from jax.experimental import pallas as pl
from jax.experimental.pallas import tpu as pltpu

F32 = jnp.float32
BF16 = jnp.bfloat16

D_MODEL = 1024
GRID_W = 64
N_HEADS = 8
QK_NOPE = 64
QK_ROPE = 32
V_DIM = 64
Q_RANK = 256
KV_RANK = 128
MLA_IN = Q_RANK + KV_RANK + QK_ROPE
CONV_W = 512
HEAD_COLS = 512
D_FF = 4096
ROPE_THETA = 10000.0
EPS = 1e-6
ATTN_SCALE = 1.0 / math.sqrt(QK_NOPE + QK_ROPE)
LOG2_E = 1.0 / math.log(2.0)
EXP2_SCALE = ATTN_SCALE * LOG2_E
N_DEV = 8
LANES = 128

ADAM_LR, ADAM_B1, ADAM_B2, ADAM_EPS, ADAM_WD, ADAM_STEP = 0.001, 0.9, 0.999, 1e-08, 0.01, 10

ROW_TILE = 256
HEAD_BWD_TILE = 512
VMEM_BIG = 60 * 1024 * 1024


def _params(sem=None, vmem=None):
    return pltpu.CompilerParams(dimension_semantics=sem, vmem_limit_bytes=vmem)


def _pick(n, prefs):
    for p in prefs:
        if n % p == 0:
            return p
    return n


def _my_index():
    return 4 * lax.axis_index("x") + 2 * lax.axis_index("y") + lax.axis_index("c")


def _two_level_gather(x_refs, out_refs, send_sems, recv_sems, local_sems):
    n = len(x_refs)
    x, y, c = lax.axis_index("x"), lax.axis_index("y"), lax.axis_index("c")
    me, sibling = (x, y, c), (x, y, 1 - c)
    chips = [(1 - x, y), (x, 1 - y), (1 - x, 1 - y)]

    def slot(a, px, py, pc):
        return out_refs[a].at[4 * px + 2 * py + pc]

    def copy(a, k, block, to, src=None):
        return pltpu.make_async_remote_copy(
            src_ref=slot(a, *block) if src is None else src, dst_ref=slot(a, *block),
            send_sem=send_sems.at[7 * a + k], recv_sem=recv_sems.at[7 * a + k],
            device_id=to, device_id_type=pl.DeviceIdType.MESH)

    mine = [pltpu.make_async_copy(x_refs[a], slot(a, *me), local_sems.at[a]) for a in range(n)]
    first = [cp for a in range(n) for cp in
             [copy(a, 0, me, sibling, src=x_refs[a])]
             + [copy(a, 1 + j, me, (*chip, c), src=x_refs[a]) for j, chip in enumerate(chips)]]
    passed = [[copy(a, 4 + j, (*chip, c), sibling) for j, chip in enumerate(chips)] for a in range(n)]

    def start():
        for cp in mine + first:
            cp.start()

    def forward():
        for a in range(n):
            for j, chip in enumerate(chips):
                copy(a, 1 + j, (*chip, c), me).wait_recv()
                passed[a][j].start()

    def finish():
        for a in range(n):
            copy(a, 0, sibling, me).wait_recv()
            for j, chip in enumerate(chips):
                copy(a, 4 + j, (*chip, 1 - c), me).wait_recv()
        for cp in first + [cp for per_array in passed for cp in per_array]:
            cp.wait_send()
        for cp in mine:
            cp.wait()

    return start, forward, finish


def _direct_gather(src_ref, dst_ref, send_sems, recv_sems, per_peer=False):
    x, y, c = lax.axis_index("x"), lax.axis_index("y"), lax.axis_index("c")
    me = 4 * x + 2 * y + c
    dst_ref[me] = src_ref[me] if per_peer else src_ref[...]
    sends, landings = [], []
    for k in range(1, N_DEV):
        peer = (1 - x if k & 4 else x, 1 - y if k & 2 else y, 1 - c if k & 1 else c)
        pid = 4 * peer[0] + 2 * peer[1] + peer[2]
        for dst, out in ((me, sends), (pid, landings)):
            out.append(pltpu.make_async_remote_copy(
                src_ref=src_ref.at[pid] if per_peer else src_ref, dst_ref=dst_ref.at[dst],
                send_sem=send_sems.at[k - 1], recv_sem=recv_sems.at[k - 1],
                device_id=peer, device_id_type=pl.DeviceIdType.MESH))
    for cp in sends:
        cp.start()

    def finish():
        for cp in landings:
            cp.wait_recv()
        for cp in sends:
            cp.wait_send()

    return finish


def _all_gather(arrays, name, in_vmem):
    space = pltpu.VMEM if in_vmem else pl.ANY
    n = len(arrays)

    def body(*refs):
        for phase in _two_level_gather(refs[:n], refs[n:2 * n], *refs[2 * n:]):
            phase()

    outs = pl.pallas_call(
        body, name=name,
        out_shape=tuple(jax.ShapeDtypeStruct((N_DEV,) + a.shape, a.dtype) for a in arrays),
        in_specs=[pl.BlockSpec(memory_space=space)] * n,
        out_specs=tuple(pl.BlockSpec(memory_space=space) for _ in arrays),
        scratch_shapes=[pltpu.SemaphoreType.DMA((7 * n,)), pltpu.SemaphoreType.DMA((7 * n,)),
                        pltpu.SemaphoreType.DMA((n,))],
    )(*arrays)
    return list(outs)


class _Riding:
    def __init__(self, arrays=()):
        self.arrays, self.n = list(arrays), len(arrays)
        self.out_shape = [jax.ShapeDtypeStruct(a.shape, a.dtype) for a in self.arrays]
        self.specs = [pl.BlockSpec(memory_space=pl.ANY)] * self.n
        self.scratch = [pltpu.SemaphoreType.DMA((7 * self.n,)), pltpu.SemaphoreType.DMA((7 * self.n,)),
                        pltpu.SemaphoreType.DMA((self.n,))]

    def copies(self, x_refs, y_refs, send_sems, recv_sems, local_sems):
        x, y, c = lax.axis_index("x"), lax.axis_index("y"), lax.axis_index("c")
        me = 4 * x + 2 * y + c
        local, sends, landings = [], [], []
        for a in range(self.n):
            local.append(pltpu.make_async_copy(x_refs[a].at[me], y_refs[a].at[me], local_sems.at[a]))
            for k in range(1, N_DEV):
                peer = (1 - x if k & 4 else x, 1 - y if k & 2 else y, 1 - c if k & 1 else c)
                pid = 4 * peer[0] + 2 * peer[1] + peer[2]
                for dst, out in ((me, sends), (pid, landings)):
                    out.append(pltpu.make_async_remote_copy(
                        src_ref=x_refs[a].at[pid], dst_ref=y_refs[a].at[dst],
                        send_sem=send_sems.at[7 * a + k - 1], recv_sem=recv_sems.at[7 * a + k - 1],
                        device_id=peer, device_id_type=pl.DeviceIdType.MESH))
        return local, sends, landings

    def run(self, first, last, x_refs, y_refs, sems, middle=None):
        if self.n == 0:
            return None
        local, sends, landings = self.copies(x_refs, y_refs, *sems)

        @pl.when(first)
        def _():
            for cp in local + sends:
                cp.start()

        return local, sends, landings, last

    @staticmethod
    def finish(state):
        if state is None:
            return
        local, sends, landings, last = state

        @pl.when(last)
        def _():
            for cp in landings:
                cp.wait_recv()
            for cp in sends:
                cp.wait_send()
            for cp in local:
                cp.wait()


class _RidingGather:
    def __init__(self, arrays):
        self.arrays, self.n = list(arrays), len(arrays)
        self.out_shape = [jax.ShapeDtypeStruct((N_DEV,) + a.shape, a.dtype) for a in self.arrays]
        self.specs = [pl.BlockSpec(memory_space=pl.ANY)] * self.n
        self.scratch = [pltpu.SemaphoreType.DMA((7 * self.n,)), pltpu.SemaphoreType.DMA((7 * self.n,)),
                        pltpu.SemaphoreType.DMA((self.n,))]

    def run(self, first, last, x_refs, y_refs, sems, middle):
        start, forward, finish = _two_level_gather(x_refs, y_refs, *sems)
        pl.when(first)(start)
        pl.when(middle)(forward)
        return finish, last

    @staticmethod
    def finish(state):
        finish, last = state
        pl.when(last)(finish)


class _RidingReduce:
    def __init__(self, arrays):
        self.arrays, self.n = list(arrays), len(arrays)
        self.out_shape = [jax.ShapeDtypeStruct((4,) + a.shape[1:], a.dtype) for a in self.arrays]
        self.specs = [pl.BlockSpec(memory_space=pl.ANY)] * self.n
        self.scratch = [pltpu.VMEM((4,) + a.shape[1:], a.dtype) for a in self.arrays for _ in range(3)]
        self.scratch += [pltpu.SemaphoreType.DMA((self.n,)) for _ in range(6)]

    def run(self, first, last, x_refs, y_refs, scratch, middle):
        n = self.n
        own, sib, tot = scratch[0:3 * n:3], scratch[1:3 * n:3], scratch[2:3 * n:3]
        d2d_send, d2d_recv, local_in, ici_send, ici_recv, local_out = scratch[3 * n:]
        x, y, c = lax.axis_index("x"), lax.axis_index("y"), lax.axis_index("c")
        my_chip = 2 * x + y
        sibling = (x, y, 1 - c)
        others = [(1 - x, y), (x, 1 - y), (1 - x, 1 - y)]

        def to_sibling(a, j=None):
            src = x_refs[a].at[pl.ds(0, 4)] if j is None else x_refs[a].at[2 * j + 1 - c]
            dst = sib[a] if j is None else sib[a].at[j]
            return pltpu.make_async_remote_copy(src_ref=src, dst_ref=dst, send_sem=d2d_send.at[a],
                                                recv_sem=d2d_recv.at[a], device_id=sibling,
                                                device_id_type=pl.DeviceIdType.MESH)

        def mine_in(a, j=None):
            src = x_refs[a].at[pl.ds(0, 4)] if j is None else x_refs[a].at[2 * j + c]
            return pltpu.make_async_copy(src, own[a] if j is None else own[a].at[j], local_in.at[a])

        def to_chip(a, chip=None):
            if chip is None:
                src, dst, peer = tot[a].at[pl.ds(0, 3)], y_refs[a].at[pl.ds(0, 3)], sibling
            else:
                src, dst, peer = tot[a].at[2 * chip[0] + chip[1]], y_refs[a].at[my_chip], (*chip, c)
            return pltpu.make_async_remote_copy(src_ref=src, dst_ref=dst, send_sem=ici_send.at[a],
                                                recv_sem=ici_recv.at[a], device_id=peer,
                                                device_id_type=pl.DeviceIdType.MESH)

        def mine_out(a):
            return pltpu.make_async_copy(tot[a].at[my_chip], y_refs[a].at[my_chip], local_out.at[a])

        @pl.when(first)
        def _():
            for a in range(n):
                for j in range(4):
                    to_sibling(a, j).start()
                    mine_in(a, j).start()

        @pl.when(middle)
        def _():
            for a in range(n):
                to_sibling(a).wait_recv()
                to_sibling(a).wait_send()
                mine_in(a).wait()
                tot[a][...] = (own[a][...].astype(F32) + sib[a][...].astype(F32)).astype(tot[a].dtype)
                for chip in others:
                    to_chip(a, chip).start()
                mine_out(a).start()

        def finish():
            @pl.when(last)
            def _():
                for a in range(n):
                    to_chip(a).wait_recv()
                    to_chip(a).wait_send()
                    mine_out(a).wait()

        return finish

    @staticmethod
    def finish(state):
        state()


_DIMS ={"nn": (((1,), (0,)), ((), ())), "nt": (((1,), (1,)), ((), ())), "tn": (((0,), (0,)), ((), ()))}
NT_DIMS = _DIMS["nt"]
TN_DIMS = _DIMS["tn"]


def _swap8(x):
    lane = lax.broadcasted_iota(jnp.int32, x.shape, 1)
    return jnp.where((lane & 15) < 8, pltpu.roll(x, LANES - 8, 1), pltpu.roll(x, 8, 1))


def _rope(x, cos, sgn, bwd):
    return x * cos + (_swap8(x * sgn) if bwd else _swap8(x) * sgn)


def _matmul(a, b, *, mode, name, out_dtype=F32, tm=512, tn=512, tk=512, m=None, k=None,
            epilogue=None, extra=(), slots=None):
    if mode == "nn":
        m = a.shape[0] if m is None else m
        k = a.shape[1]
        n = N_DEV * b.shape[2] if slots == "b_cols" else b.shape[1]
    elif mode == "nt":
        m = a.shape[0] if m is None else m
        k = a.shape[1]
        n = b.shape[0]
    else:
        k = a.shape[0] if k is None else k
        m, n = a.shape[1], b.shape[1]
    tm, tn, tk = min(tm, m), min(tn, n), min(tk, k)
    if slots == "b_cols":
        tn = b.shape[2]
    if slots == "out":
        tn = n // N_DEV
    assert m % tm == 0 and n % tn == 0 and k % tk == 0, (name, m, n, k, tm, tn, tk)
    nk = k // tk
    dims = _DIMS[mode]
    a_spec = (pl.BlockSpec((tk, tm), lambda i, j, kk: (kk, i)) if mode == "tn"
              else pl.BlockSpec((tm, tk), lambda i, j, kk: (i, kk)))
    if slots == "b_cols":
        b_spec = pl.BlockSpec((None, tk, tn), lambda i, j, kk: (j, kk, 0))
    elif mode == "nt":
        b_spec = pl.BlockSpec((tn, tk), lambda i, j, kk: (j, kk))
    else:
        b_spec = pl.BlockSpec((tk, tn), lambda i, j, kk: (kk, j))
    tile = pl.BlockSpec((tm, tn), lambda i, j, kk: (i, j))
    if slots == "out":
        o_spec = pl.BlockSpec((None, tm, tn), lambda i, j, kk: (j, i, 0))
        o_shape = (N_DEV, m, tn)
    else:
        o_spec, o_shape = tile, (m, n)
    in_specs, args = [a_spec, b_spec], [a, b]
    if epilogue == "drelu2":
        in_specs.append(tile)
    args += list(extra)
    if epilogue == "relu2":
        out_shape = (jax.ShapeDtypeStruct(o_shape, BF16), jax.ShapeDtypeStruct(o_shape, BF16))
        out_specs = (o_spec, o_spec)
    else:
        out_shape = jax.ShapeDtypeStruct(o_shape, out_dtype)
        out_specs = o_spec
    n_in = len(args)
    n_out = 2 if epilogue == "relu2" else 1

    def body(*refs):
        a_ref, b_ref = refs[0], refs[1]
        outs = refs[n_in:n_in + n_out]
        part = lax.dot_general(a_ref[...], b_ref[...], dims, preferred_element_type=F32)

        def finish(acc):
            if epilogue == "relu2":
                outs[0][...] = acc.astype(BF16)
                r = jnp.maximum(acc, 0.0)
                outs[1][...] = (r * r).astype(BF16)
            elif epilogue == "drelu2":
                u = refs[2][...].astype(F32)
                outs[0][...] = (acc * (2.0 * jnp.maximum(u, 0.0))).astype(out_dtype)
            else:
                outs[0][...] = acc.astype(out_dtype)

        if nk == 1:
            finish(part)
        else:
            acc_ref = refs[n_in + n_out]
            kk = pl.program_id(2)

            @pl.when(kk == 0)
            def _():
                acc_ref[...] = part

            @pl.when(kk > 0)
            def _():
                acc_ref[...] += part

            @pl.when(kk == nk - 1)
            def _():
                finish(acc_ref[...])

    return pl.pallas_call(
        body, name=name, grid=(m // tm, n // tn, nk),
        out_shape=out_shape, in_specs=in_specs, out_specs=out_specs,
        scratch_shapes=[pltpu.VMEM((tm, tn), F32)] if nk > 1 else [],
        compiler_params=_params(("parallel", "parallel", "arbitrary"), VMEM_BIG),
    )(*args)


def _rstd(x):
    return lax.rsqrt(jnp.mean(x * x, axis=1, keepdims=True) + EPS)


def _norm_bwd(dxn, xn, r):
    return r * (dxn - xn * jnp.mean(dxn * xn, axis=1, keepdims=True))


def _vec(col):
    return pl.BlockSpec((1, D_MODEL), lambda i: (0, col))


def _matmul_rows(a, b, epi, *, mode, name, tm, tk, rows=(), vecs=(), out_dtypes=(), sums=False, slots=None,
                 riding=None):
    m, k = a.shape
    n = D_MODEL
    tm, tk = min(tm, m), min(tk, k)
    riding = riding or _Riding()
    group = 1
    if slots == "b_contract":
        group = max(1, tk // b.shape[2])
        tk = group * b.shape[2]
        b_spec = pl.BlockSpec((group, n, tk // group), lambda i, kk: (kk, 0, 0))
    elif mode == "nt":
        b_spec = pl.BlockSpec((n, tk), lambda i, kk: (0, kk))
    else:
        b_spec = pl.BlockSpec((tk, n), lambda i, kk: (kk, 0))
    assert m % tm == 0 and k % tk == 0, (name, m, k, tm, tk)
    ni, nk = m // tm, k // tk
    assert ni >= 2 or not isinstance(riding, _RidingReduce), "the two-level exchange needs a middle grid step"
    dims = _DIMS[mode]
    tile = pl.BlockSpec((tm, n), lambda i, kk: (i, 0))
    in_specs = [pl.BlockSpec((tm, tk), lambda i, kk: (i, kk)), b_spec] + [tile] * len(rows)
    in_specs += [pl.BlockSpec((1, n), lambda i, kk, col=col: (0, col)) for _, col in vecs]
    args = [a, b, *rows, *[v for v, _ in vecs]]
    out_shape = [jax.ShapeDtypeStruct((m, n), dt) for dt in out_dtypes]
    out_specs = [tile] * len(out_dtypes)
    if sums:
        out_shape.append(jax.ShapeDtypeStruct((8, n), F32))
        out_specs.append(pl.BlockSpec((8, n), lambda i, kk: (0, 0)))
    n_rows, n_vecs, n_outs, nr = len(rows), len(vecs), len(out_dtypes), riding.n
    n_in = 2 + n_rows + n_vecs

    def body(*refs):
        a_ref, b_ref = refs[0], refs[1]
        row_refs = refs[2:2 + n_rows]
        vec_refs = refs[2 + n_rows:n_in]
        x_refs = refs[n_in:n_in + nr]
        out_refs = refs[n_in + nr:n_in + nr + n_outs]
        pos = n_in + nr + n_outs
        sums_ref = refs[pos] if sums else None
        pos += 1 if sums else 0
        y_refs = refs[pos:pos + nr]
        pos += nr
        acc_ref = refs[pos] if nk > 1 else None
        sem_refs = refs[pos + (1 if nk > 1 else 0):]
        i, kk = pl.program_id(0), pl.program_id(1)
        state = riding.run((i == 0) & (kk == 0), (i == ni - 1) & (kk == nk - 1), x_refs, y_refs, sem_refs,
                           middle=(i == 1) & (kk == 0))
        if slots == "b_contract":
            c = tk // group
            part = lax.dot_general(a_ref[:, 0:c], b_ref[0], dims, preferred_element_type=F32)
            for u in range(1, group):
                part = part + lax.dot_general(a_ref[:, u * c:(u + 1) * c], b_ref[u], dims, preferred_element_type=F32)
        else:
            part = lax.dot_general(a_ref[...], b_ref[...], dims, preferred_element_type=F32)

        def finish(acc):
            nsub = tm // ROW_TILE
            for r in range(nsub):
                blk = pl.ds(r * ROW_TILE, ROW_TILE)
                epi(acc[r * ROW_TILE:(r + 1) * ROW_TILE], [ref.at[blk] for ref in row_refs], vec_refs,
                    [ref.at[blk] for ref in out_refs], sums_ref,
                    (i == 0) if r == 0 else None, (i == ni - 1) if r == nsub - 1 else None)

        if nk == 1:
            finish(part)
        else:
            @pl.when(kk == 0)
            def _():
                acc_ref[...] = part

            @pl.when(kk > 0)
            def _():
                acc_ref[...] += part

            @pl.when(kk == nk - 1)
            def _():
                finish(acc_ref)

        riding.finish(state)

    outs = pl.pallas_call(
        body, name=name, grid=(ni, nk),
        out_shape=(*out_shape, *riding.out_shape),
        in_specs=[*in_specs, *riding.specs], out_specs=(*out_specs, *riding.specs),
        scratch_shapes=([pltpu.VMEM((tm, n), F32)] if nk > 1 else []) + (riding.scratch if nr else []),
        compiler_params=_params(("arbitrary", "arbitrary"), VMEM_BIG),
    )(*args, *riding.arrays)
    n_own = len(out_shape)
    return list(outs[:n_own]), list(outs[n_own:])


def _zero_sums_at_start(sums_ref, first):
    if first is not None:
        @pl.when(first)
        def _():
            sums_ref[...] = jnp.zeros_like(sums_ref)


def _epi_resid_modulate(acc, rows, vecs, outs, sums_ref, first, last):
    (x_ref,), (g_ref, sh_ref, sc_ref) = rows, vecs
    x1 = x_ref[...] + g_ref[...] * acc
    outs[0][...] = acc
    outs[1][...] = x1
    outs[2][...] = (x1 * _rstd(x1) * (1.0 + sc_ref[...]) + sh_ref[...]).astype(BF16)


def _epi_final(acc, rows, vecs, outs, sums_ref, first, last):
    (x1_ref, t_ref), (g_ref, gf_ref) = rows, vecs
    d = acc.shape[1]
    x2 = x1_ref[...] + g_ref[...] * acc
    r = _rstd(x2)
    xn = x2 * r
    err = xn * gf_ref[...] - t_ref[...]
    dy = err * (1.0 / d)
    dx2 = _norm_bwd(dy * gf_ref[...], xn, r)
    outs[0][...] = dx2
    outs[1][...] = (dx2 * g_ref[...]).astype(BF16)
    _zero_sums_at_start(sums_ref, first)
    sums_ref[0:1, :] += jnp.sum(dy * xn, axis=0, keepdims=True)
    sums_ref[1:2, :] += jnp.sum(dx2 * acc, axis=0, keepdims=True)
    sums_ref[2:3, :] += jnp.sum(err * err, axis=0, keepdims=True)

    if last is not None:
        @pl.when(last)
        def _():
            tot = jnp.sum(sums_ref[2:3, :], axis=1, keepdims=True) * (0.5 / d)
            sums_ref[3:4, :] = jnp.broadcast_to(tot, (1, d))


def _epi_modulate2_bwd(acc, rows, vecs, outs, sums_ref, first, last):
    (x_ref, dres_ref, o_ref), (sc_ref, g_ref) = rows, vecs
    x = x_ref[...]
    r = _rstd(x)
    xn = x * r
    dx = dres_ref[...] + _norm_bwd(acc * (1.0 + sc_ref[...]), xn, r)
    outs[0][...] = dx
    outs[1][...] = (dx * g_ref[...]).astype(BF16)
    _zero_sums_at_start(sums_ref, first)
    sums_ref[0:1, :] += jnp.sum(acc * xn, axis=0, keepdims=True)
    sums_ref[1:2, :] += jnp.sum(acc, axis=0, keepdims=True)
    sums_ref[2:3, :] += jnp.sum(dx * o_ref[...], axis=0, keepdims=True)


def _epi_modulate1_bwd(acc, rows, vecs, outs, sums_ref, first, last):
    (add_ref, x_ref, dres_ref), (sc_ref,) = rows, vecs
    dh = acc + add_ref[...]
    x = x_ref[...]
    r = _rstd(x)
    xn = x * r
    outs[0][...] = dres_ref[...] + _norm_bwd(dh * (1.0 + sc_ref[...]), xn, r)
    _zero_sums_at_start(sums_ref, first)
    sums_ref[0:1, :] += jnp.sum(dh * xn, axis=0, keepdims=True)
    sums_ref[1:2, :] += jnp.sum(dh, axis=0, keepdims=True)


def _modulate_all(x, ctx, mod, mod_ctx, riding, name):
    s, d = x.shape
    t = s + ctx.shape[0]
    ns = s // ROW_TILE
    nc = ctx.shape[0] // ROW_TILE
    nr = riding.n

    def body(*refs):
        x_ref, c_ref, sh_ref, sc_ref, shc_ref, scc_ref = refs[:6]
        h_ref = refs[6 + nr]
        i = pl.program_id(0)
        state = riding.run(i == 0, i == ns + nc - 1, refs[6:6 + nr], refs[7 + nr:7 + 2 * nr], refs[7 + 2 * nr:],
                           middle=i == ns + nc - 3)

        @pl.when(i < ns)
        def _():
            v = x_ref[...]
            h_ref[...] = (v * _rstd(v) * (1.0 + sc_ref[...]) + sh_ref[...]).astype(BF16)

        @pl.when(i >= ns)
        def _():
            v = c_ref[...]
            h_ref[...] = (v * _rstd(v) * (1.0 + scc_ref[...]) + shc_ref[...]).astype(BF16)

        riding.finish(state)

    outs = pl.pallas_call(
        body, name=name, grid=(ns + nc,),
        out_shape=(jax.ShapeDtypeStruct((t, d), BF16), *riding.out_shape),
        in_specs=[pl.BlockSpec((ROW_TILE, d), lambda i: (jnp.minimum(i, ns - 1), 0)),
                  pl.BlockSpec((ROW_TILE, d), lambda i: (jnp.maximum(i - ns, 0), 0)),
                  _vec(0), _vec(1), _vec(0), _vec(1), *riding.specs],
        out_specs=(pl.BlockSpec((ROW_TILE, d), lambda i: (i, 0)), *riding.specs),
        scratch_shapes=riding.scratch,
        compiler_params=_params(("arbitrary",)),
    )(x, ctx, mod, mod, mod_ctx, mod_ctx, *riding.arrays)
    return outs[0], list(outs[1:])


def _modulate_sums(dh, row_off, xsrc):
    s, d = xsrc.shape

    def body(dh_ref, x_ref, sums_ref):
        i = pl.program_id(0)
        x = x_ref[...]
        dhv = dh_ref[...]

        @pl.when(i == 0)
        def _():
            sums_ref[...] = jnp.zeros_like(sums_ref)

        sums_ref[0:1, :] += jnp.sum(dhv * (x * _rstd(x)), axis=0, keepdims=True)
        sums_ref[1:2, :] += jnp.sum(dhv, axis=0, keepdims=True)

    return pl.pallas_call(
        body, name="modulate1_ctx_bwd", grid=(s // ROW_TILE,),
        out_shape=jax.ShapeDtypeStruct((8, d), F32),
        in_specs=[pl.BlockSpec((ROW_TILE, d), lambda i: (i + row_off, 0)), pl.BlockSpec((ROW_TILE, d), lambda i: (i, 0))],
        out_specs=pl.BlockSpec((8, d), lambda i: (0, 0)),
        compiler_params=_params(("arbitrary",)),
    )(dh, xsrc)


def _head_fwd(h_all, win_head, wq, wk, q_gain, kv_gain, cos, sgn, tm, name):
    t, d = h_all.shape
    nq, nkv = wq.shape[1], wk.shape[1]

    def body(h_ref, wi_ref, wq_ref, wk_ref, qg_ref, kg_ref, c_ref, s_ref, z_ref, cq_ref, kvin_ref, qf_ref, kv_ref):
        z = lax.dot_general(h_ref[...], wi_ref[...], NT_DIMS, preferred_element_type=F32)
        z_ref[...] = z
        cos, sgn = c_ref[...], s_ref[...]
        zq = z[:, 0:Q_RANK]
        cq = (zq * _rstd(zq) * qg_ref[...]).astype(BF16)
        cq_ref[...] = cq
        zk = z[:, Q_RANK:Q_RANK + KV_RANK]
        kv_in = jnp.concatenate([(zk * _rstd(zk) * kg_ref[...]).astype(BF16),
                                 _rope(z[:, Q_RANK + KV_RANK:HEAD_COLS], cos, sgn, False).astype(BF16)], axis=1)
        kvin_ref[...] = kv_in
        q = jnp.dot(cq, wq_ref[...], preferred_element_type=F32)
        for h in range(nq // LANES):
            sl = slice(h * LANES, (h + 1) * LANES)
            qf_ref[:, sl] = _rope(q[:, sl], cos, sgn, False).astype(BF16)
        kv_ref[...] = jnp.dot(kv_in, wk_ref[...], preferred_element_type=F32).astype(BF16)

    def row(w):
        return pl.BlockSpec((tm, w), lambda i: (i, 0))

    def whole(a):
        return pl.BlockSpec(a.shape, lambda i: (0, 0))

    return pl.pallas_call(
        body, name=name, grid=(t // tm,),
        out_shape=(jax.ShapeDtypeStruct((t, HEAD_COLS), F32), jax.ShapeDtypeStruct((t, Q_RANK), BF16),
                   jax.ShapeDtypeStruct((t, KV_RANK + LANES), BF16), jax.ShapeDtypeStruct((t, nq), BF16),
                   jax.ShapeDtypeStruct((t, nkv), BF16)),
        in_specs=[row(d), whole(win_head), whole(wq), whole(wk), whole(q_gain), whole(kv_gain), row(LANES), row(LANES)],
        out_specs=(row(HEAD_COLS), row(Q_RANK), row(KV_RANK + LANES), row(nq), row(nkv)),
        compiler_params=_params(("parallel",), VMEM_BIG),
    )(h_all, win_head, wq, wk, q_gain, kv_gain, cos, sgn)


def _head_bwd(dq, dk, dv, z, wq, wk_k, wk_v, win_head, q_gain, kv_gain, cos, sgn, cq, kv_in, h_all, name, *, tile,
              first_block, n_blocks, carry=None):
    t = z.shape[0]
    with_q = dq is not None

    def body(*refs):
        it = iter(refs)
        dq_ref = next(it) if with_q else None
        dk_ref, dv_ref, z_ref, wq_ref, wkk_ref, wkv_ref, wi_ref, qg_ref, kg_ref, c_ref, s_ref = (next(it) for _ in range(11))
        cq_ref = next(it) if with_q else None
        kvin_ref, h_ref = next(it), next(it)
        before = None
        if carry is not None:
            next(it), next(it)
            before = (next(it), next(it), next(it))
        dz_ref, dh_ref, sums_ref = next(it), next(it), next(it)
        gq_ref = next(it) if with_q else None
        grads = (next(it), next(it), next(it))
        i = pl.program_id(0)

        @pl.when(i == 0)
        def _():
            sums_ref[...] = jnp.zeros_like(sums_ref)
            if with_q:
                gq_ref[...] = jnp.zeros_like(gq_ref)
            for k, g_ref in enumerate(grads):
                g_ref[...] = jnp.zeros_like(g_ref) if before is None else before[k][...]

        if with_q:
            gq_ref[...] += lax.dot_general(cq_ref[...], dq_ref[...], TN_DIMS, preferred_element_type=F32)
        grads[0][...] += lax.dot_general(kvin_ref[...], dk_ref[...], TN_DIMS, preferred_element_type=F32)
        grads[1][...] += lax.dot_general(kvin_ref[...], dv_ref[...], TN_DIMS, preferred_element_type=F32)
        if with_q:
            dc = lax.dot_general(dq_ref[...], wq_ref[...], NT_DIMS, preferred_element_type=F32)
            zq = z_ref[:, 0:Q_RANK]
            r = _rstd(zq)
            zn = zq * r
            sums_ref[0:1, :] += jnp.sum(dc * zn, axis=0, keepdims=True)
            dz_ref[:, 0:Q_RANK] = _norm_bwd(dc * qg_ref[...], zn, r).astype(BF16)
        else:
            dz_ref[:, 0:Q_RANK] = jnp.zeros((tile, Q_RANK), BF16)
        dkv = (lax.dot_general(dk_ref[...], wkk_ref[...], NT_DIMS, preferred_element_type=F32)
               + lax.dot_general(dv_ref[...], wkv_ref[...], NT_DIMS, preferred_element_type=F32))
        zk = z_ref[:, Q_RANK:Q_RANK + KV_RANK]
        r = _rstd(zk)
        zn = zk * r
        dc = dkv[:, 0:KV_RANK]
        sums_ref[1:2, 0:KV_RANK] += jnp.sum(dc * zn, axis=0, keepdims=True)
        dz_ref[:, Q_RANK:Q_RANK + KV_RANK] = _norm_bwd(dc * kg_ref[...], zn, r).astype(BF16)
        dz_ref[:, Q_RANK + KV_RANK:HEAD_COLS] = _rope(dkv[:, KV_RANK:KV_RANK + LANES], c_ref[...], s_ref[...],
                                                       True).astype(BF16)
        dh_ref[...] = jnp.dot(dz_ref[...], wi_ref[...], preferred_element_type=F32)
        grads[2][...] += lax.dot_general(dz_ref[...], h_ref[...], TN_DIMS, preferred_element_type=F32)

    def row(w):
        return pl.BlockSpec((tile, w), lambda i: (i + first_block, 0))

    def whole(a):
        return pl.BlockSpec(a.shape, lambda i: (0, 0))

    args = ([dq] if with_q else []) + [dk, dv, z, wq, wk_k, wk_v, win_head, q_gain, kv_gain, cos, sgn]
    args += ([cq] if with_q else []) + [kv_in, h_all]
    in_specs = ([row(dq.shape[1])] if with_q else []) + [
        row(dk.shape[1]), row(dv.shape[1]), row(HEAD_COLS), whole(wq), whole(wk_k), whole(wk_v),
        whole(win_head), whole(q_gain), whole(kv_gain), row(LANES), row(LANES)]
    in_specs += ([row(Q_RANK)] if with_q else []) + [row(kv_in.shape[1]), row(D_MODEL)]
    aliases = {}
    if carry is not None:
        aliases = {len(args): 0, len(args) + 1: 1}
        args += list(carry)
        in_specs += [pl.BlockSpec(memory_space=pl.ANY)] * 2 + [whole(a) for a in carry[2:]]
    grad_shapes = ([(Q_RANK, dq.shape[1])] if with_q else []) + [
        (kv_in.shape[1], dk.shape[1]), (kv_in.shape[1], dv.shape[1]), (HEAD_COLS, D_MODEL)]
    return pl.pallas_call(
        body, name=name, grid=(n_blocks,),
        out_shape=(jax.ShapeDtypeStruct((t, HEAD_COLS), BF16), jax.ShapeDtypeStruct((t, D_MODEL), F32),
                   jax.ShapeDtypeStruct((8, Q_RANK), F32), *[jax.ShapeDtypeStruct(g, F32) for g in grad_shapes]),
        in_specs=in_specs,
        out_specs=(row(HEAD_COLS), row(D_MODEL), pl.BlockSpec((8, Q_RANK), lambda i: (0, 0)),
                   *[pl.BlockSpec(g, lambda i: (0, 0)) for g in grad_shapes]),
        input_output_aliases=aliases,
        compiler_params=_params(("arbitrary",), VMEM_BIG),
    )(*args)


def _shift_rows(u, s):
    rowi = lax.broadcasted_iota(jnp.int32, u.shape, 0)
    prev = jnp.where(rowi == 0, 0.0, pltpu.roll(u, 1, 0))
    nxt = jnp.where(rowi == s - 1, 0.0, pltpu.roll(u, s - 1, 0))
    return prev, nxt


def _conv_fwd(z_conv, cw, a_cat, name):
    s = z_conv.shape[0]

    def body(z_ref, w_ref, a_in_ref, o_ref):
        del a_in_ref
        gb, gc, xv = z_ref[:, 0:LANES], z_ref[:, LANES:2 * LANES], z_ref[:, 2 * LANES:3 * LANES]
        u = gc * xv
        prev, nxt = _shift_rows(u, s)
        y = w_ref[0:1, :] * prev + w_ref[1:2, :] * u + w_ref[2:3, :] * nxt
        o_ref[...] = (gb * y).astype(BF16)

    return pl.pallas_call(
        body, name=name, grid=(CONV_W // LANES,),
        out_shape=jax.ShapeDtypeStruct(a_cat.shape, a_cat.dtype),
        in_specs=[pl.BlockSpec((s, 3 * LANES), lambda j: (0, j)), pl.BlockSpec((3, LANES), lambda j: (0, j)),
                  pl.BlockSpec(memory_space=pl.ANY)],
        out_specs=pl.BlockSpec((s, LANES), lambda j: (0, 4 + j)),
        input_output_aliases={2: 0},
        compiler_params=_params(("parallel",), VMEM_BIG),
    )(z_conv, cw, a_cat)


def _conv_bwd(z_conv, cw, da, h_all, name):
    s = z_conv.shape[0]

    def body(z_ref, w_ref, da_ref, h_ref, dz_ref, dw_ref, g_ref):
        gb, gc, xv = z_ref[:, 0:LANES], z_ref[:, LANES:2 * LANES], z_ref[:, 2 * LANES:3 * LANES]
        u = gc * xv
        prev, nxt = _shift_rows(u, s)
        dcv = da_ref[...]
        dz_ref[:, 0:LANES] = (dcv * (w_ref[0:1, :] * prev + w_ref[1:2, :] * u + w_ref[2:3, :] * nxt)).astype(BF16)
        dy = dcv * gb
        dw_ref[0:1, :] = jnp.sum(dy * prev, axis=0, keepdims=True)
        dw_ref[1:2, :] = jnp.sum(dy * u, axis=0, keepdims=True)
        dw_ref[2:3, :] = jnp.sum(dy * nxt, axis=0, keepdims=True)
        dyp, dyn = _shift_rows(dy, s)
        du = w_ref[0:1, :] * dyn + w_ref[1:2, :] * dy + w_ref[2:3, :] * dyp
        dz_ref[:, LANES:2 * LANES] = (du * xv).astype(BF16)
        dz_ref[:, 2 * LANES:3 * LANES] = (du * gc).astype(BF16)
        g_ref[...] = lax.dot_general(dz_ref[...], h_ref[...], TN_DIMS, preferred_element_type=F32)

    d = h_all.shape[1]
    blk = pl.BlockSpec((s, 3 * LANES), lambda j: (0, j))
    cws = pl.BlockSpec((3, LANES), lambda j: (0, j))
    return pl.pallas_call(
        body, name=name, grid=(CONV_W // LANES,),
        out_shape=(jax.ShapeDtypeStruct(z_conv.shape, BF16), jax.ShapeDtypeStruct((3, CONV_W), F32),
                   jax.ShapeDtypeStruct((z_conv.shape[1], d), F32)),
        in_specs=[blk, cws, pl.BlockSpec((s, LANES), lambda j: (0, 4 + j)), pl.BlockSpec((s, d), lambda j: (0, 0))],
        out_specs=(blk, cws, pl.BlockSpec((3 * LANES, d), lambda j: (j, 0))),
        compiler_params=_params(("parallel",), VMEM_BIG),
    )(z_conv, cw, da, h_all)


ATT_TQ = 512
ATT_Q_STEP = 1024
ATT_TQ_BWD = 512


def _head_mask(shape, hh):
    lane = lax.broadcasted_iota(jnp.int32, shape, 1)
    return (lane >= hh * V_DIM) & (lane < (hh + 1) * V_DIM)


def _attn_fwd(qf, kv, s, riding, name):
    t = kv.shape[0]
    step = min(ATT_Q_STEP, s)
    nq = s // step
    nr = riding.n

    def body(*refs):
        q_ref, k_ref, v_ref = refs[:3]
        o_ref, ob_ref, st_ref = refs[3 + nr:6 + nr]
        p, i = pl.program_id(0), pl.program_id(1)
        state = riding.run((p == 0) & (i == 0), (p == N_HEADS // 2 - 1) & (i == nq - 1),
                           refs[3:3 + nr], refs[6 + nr:6 + 2 * nr], refs[6 + 2 * nr:],
                           middle=(p == N_HEADS // 2 - 2) & (i == nq // 2))
        v = v_ref[...]
        vlane = lax.broadcasted_iota(jnp.int32, v.shape, 1)
        one_lane = [(1 - hh) * V_DIM for hh in range(2)]
        vm = [jnp.where(_head_mask(v.shape, hh), v, jnp.where(vlane == one_lane[hh], 1.0, 0.0).astype(BF16))
              for hh in range(2)]

        def block(r, carry):
            rows = pl.ds(pl.multiple_of(r * ATT_TQ, ATT_TQ), ATT_TQ)
            olane = lax.broadcasted_iota(jnp.int32, (ATT_TQ, LANES), 1)
            acc = jnp.zeros((ATT_TQ, LANES), F32)
            stat = jnp.zeros((ATT_TQ, LANES), F32)
            scores = [lax.dot_general(q_ref[rows, hh * LANES:(hh + 1) * LANES], k_ref[:, hh * LANES:(hh + 1) * LANES],
                                      NT_DIMS, preferred_element_type=F32) for hh in range(2)]
            maxes = [jnp.max(sc, axis=1, keepdims=True) for sc in scores]
            exps = [jnp.exp2((sc - mx) * EXP2_SCALE).astype(BF16) for sc, mx in zip(scores, maxes)]
            for hh in range(2):
                mx = maxes[hh]
                res = jnp.dot(exps[hh], vm[hh], preferred_element_type=F32)
                den = jnp.sum(jnp.where(olane == one_lane[hh], res, 0.0), axis=1, keepdims=True)
                acc = acc + jnp.where(_head_mask(res.shape, hh), res * (1.0 / den), 0.0)
                stat = stat + jnp.where(olane == hh, mx * EXP2_SCALE + jnp.log(den) * LOG2_E, 0.0)
            o_ref[rows, :] = acc
            ob_ref[rows, :] = acc.astype(BF16)
            st_ref[:, rows] = stat.T[0:8, :]
            return carry

        lax.fori_loop(0, step // ATT_TQ, block, 0)
        riding.finish(state)

    o_spec = pl.BlockSpec((step, LANES), lambda p, i: (i, p))
    outs = pl.pallas_call(
        body, name=name, grid=(N_HEADS // 2, nq),
        out_shape=(jax.ShapeDtypeStruct((s, N_HEADS * V_DIM), F32),
                   jax.ShapeDtypeStruct((s, D_MODEL), BF16),
                   jax.ShapeDtypeStruct((N_HEADS // 2 * 8, s), F32), *riding.out_shape),
        in_specs=[pl.BlockSpec((step, 2 * LANES), lambda p, i: (i, p)),
                  pl.BlockSpec((t, 2 * LANES), lambda p, i: (0, p)),
                  pl.BlockSpec((t, LANES), lambda p, i: (0, N_HEADS + p)), *riding.specs],
        out_specs=(o_spec, o_spec, pl.BlockSpec((8, step), lambda p, i: (p, i)), *riding.specs),
        scratch_shapes=riding.scratch,
        compiler_params=_params(("arbitrary", "arbitrary"), VMEM_BIG),
    )(qf, kv, kv, *riding.arrays)
    return outs[0], outs[1], outs[2], list(outs[3:])


def _attn_bwd(qf, kv, o, da, stats, cos, sgn, riding, name):
    s, t = o.shape[0], kv.shape[0]
    ATT_TQ = ATT_TQ_BWD
    nq = s // ATT_TQ
    nr = riding.n

    def body(*refs):
        q_ref, k_ref, v_ref, o_ref, do_ref, st_ref, c_ref, s_ref = refs[:8]
        dq_ref, dk_ref, dv_ref = refs[8 + nr:11 + nr]
        dk_acc, dv_acc = refs[11 + 2 * nr:13 + 2 * nr]
        p, i = pl.program_id(0), pl.program_id(1)
        state = riding.run((p == 0) & (i == 0), (p == N_HEADS // 2 - 1) & (i == nq - 1),
                           refs[8:8 + nr], refs[11 + nr:11 + 2 * nr], refs[13 + 2 * nr:])

        @pl.when(i == 0)
        def _():
            dk_acc[...] = jnp.zeros_like(dk_acc)
            dv_acc[...] = jnp.zeros_like(dv_acc)

        v = v_ref[...]
        do = do_ref[...]
        od = do * o_ref[...]
        ones = jnp.ones((8, LANES), F32)
        for hh in range(2):
            sl = slice(hh * LANES, (hh + 1) * LANES)
            q, k = q_ref[:, sl], k_ref[:, sl]
            mask = _head_mask(do.shape, hh)
            dom = jnp.where(mask, do, 0.0).astype(BF16)
            delta = lax.dot_general(ones, jnp.where(mask, od, 0.0), NT_DIMS, preferred_element_type=F32,
                                    precision=lax.Precision.HIGHEST)[0:1, :]
            st = lax.dot_general(k, q, NT_DIMS, preferred_element_type=F32)
            pt = jnp.exp2(st * EXP2_SCALE - st_ref[hh:hh + 1, :]).astype(BF16)
            dpt = lax.dot_general(v, dom, NT_DIMS, preferred_element_type=F32)
            dst = (pt.astype(F32) * (dpt - delta)).astype(BF16)
            dv_acc[...] += jnp.dot(pt, dom, preferred_element_type=F32)
            dk_acc[:, sl] += jnp.dot(dst, q, preferred_element_type=F32)
            dq = lax.dot_general(dst, k, TN_DIMS, preferred_element_type=F32) * ATTN_SCALE
            dq_ref[:, sl] = _rope(dq, c_ref[...], s_ref[...], True).astype(BF16)

        @pl.when(i == nq - 1)
        def _():
            dk_ref[...] = (dk_acc[...] * ATTN_SCALE).astype(BF16)
            dv_ref[...] = dv_acc[...].astype(BF16)

        riding.finish(state)

    o_spec = pl.BlockSpec((ATT_TQ, LANES), lambda p, i: (i, p))
    tab = pl.BlockSpec((ATT_TQ, LANES), lambda p, i: (i, 0))
    outs = pl.pallas_call(
        body, name=name, grid=(N_HEADS // 2, nq),
        out_shape=(jax.ShapeDtypeStruct((s, N_HEADS * LANES), BF16),
                   jax.ShapeDtypeStruct((t, N_HEADS * LANES), BF16),
                   jax.ShapeDtypeStruct((t, N_HEADS * V_DIM), BF16), *riding.out_shape),
        in_specs=[pl.BlockSpec((ATT_TQ, 2 * LANES), lambda p, i: (i, p)),
                  pl.BlockSpec((t, 2 * LANES), lambda p, i: (0, p)),
                  pl.BlockSpec((t, LANES), lambda p, i: (0, N_HEADS + p)),
                  o_spec, o_spec,
                  pl.BlockSpec((8, ATT_TQ), lambda p, i: (p, i)), tab, tab, *riding.specs],
        out_specs=(pl.BlockSpec((ATT_TQ, 2 * LANES), lambda p, i: (i, p)),
                   pl.BlockSpec((t, 2 * LANES), lambda p, i: (0, p)),
                   pl.BlockSpec((t, LANES), lambda p, i: (0, p)), *riding.specs),
        scratch_shapes=[pltpu.VMEM((t, 2 * LANES), F32), pltpu.VMEM((t, LANES), F32), *riding.scratch],
        compiler_params=_params(("arbitrary", "arbitrary"), VMEM_BIG),
    )(qf, kv, kv, o, da, stats, cos, sgn, *riding.arrays)
    return outs[0], outs[1], outs[2], list(outs[3:])


def _silu(x):
    return x * (1.0 / (1.0 + jnp.exp(-x)))


def _prologue(c_rows, c_ctx, w_mod, b_cols, extra_rows, name):
    d, cols = c_rows.shape[1], w_mod.shape[1]

    def body(c_ref, cctx_ref, wmod_ref, b_ref, x_ref, a_ref, modg_ref, c_all, blk, c_send, c_recv, m_send, m_recv):
        _direct_gather(c_ref, c_all, c_send, c_recv)()
        a_ref[...] = jnp.zeros_like(a_ref)
        for j in range(N_DEV):
            a_ref[j:j + 1, :] = c_all[j, 0:1, :]
        a_ref[N_DEV:N_DEV + 1, :] = cctx_ref[...]
        mod = jnp.dot(_silu(a_ref[...]), wmod_ref[...], preferred_element_type=F32,
                      precision=lax.Precision.HIGHEST) + b_ref[...]
        blk[...] = jnp.zeros_like(blk)
        for p in range(N_DEV):
            blk[p, 0:1, :] = mod[p:p + 1, :]
            blk[p, 1:2, :] = mod[N_DEV:N_DEV + 1, :]
            blk[p, 2:5, :] = x_ref[...]
        _direct_gather(blk, modg_ref, m_send, m_recv, per_peer=True)()

    vmem = pl.BlockSpec(memory_space=pltpu.VMEM)
    return pl.pallas_call(
        body, name=name,
        out_shape=(jax.ShapeDtypeStruct((16, d), F32), jax.ShapeDtypeStruct((N_DEV, 8, cols), F32)),
        in_specs=[vmem] * 5, out_specs=(vmem, vmem),
        scratch_shapes=[pltpu.VMEM((N_DEV, 8, d), F32), pltpu.VMEM((N_DEV, 8, cols), F32)]
        + [pltpu.SemaphoreType.DMA((7,)) for _ in range(4)],
        compiler_params=_params(None, VMEM_BIG),
    )(c_rows, c_ctx, w_mod, b_cols, extra_rows)


def _adaln_bwd(a_t, w, d_ex, d_ctx, d_all, name):
    def body(at_ref, w_ref, dex_ref, dctx_ref, dall_ref, gw_ref, dsil_ref, dsum_ref):
        sil_t = _silu(at_ref[...])
        dctx = dctx_ref[...]
        row = dctx[0:1, :]
        for j in range(1, N_DEV):
            row = row + dctx[j:j + 1, :]
        rowi = lax.broadcasted_iota(jnp.int32, dctx.shape, 0)
        ctx_rows = jnp.where(rowi == 0, jnp.broadcast_to(row, dctx.shape), 0.0)
        hi = lax.Precision.HIGHEST
        d_rows = jnp.concatenate([dex_ref[...], ctx_rows], axis=0)
        gw_ref[...] = jnp.dot(sil_t, d_rows, preferred_element_type=F32, precision=hi)
        dsil_ref[...] = lax.dot_general(ctx_rows, w_ref[...], NT_DIMS, preferred_element_type=F32, precision=hi)
        tot = dall_ref[0]
        for j in range(1, N_DEV):
            tot = tot + dall_ref[j]
        dsum_ref[...] = tot

    return pl.pallas_call(
        body, name=name,
        out_shape=(jax.ShapeDtypeStruct(w.shape, F32), jax.ShapeDtypeStruct((8, w.shape[0]), F32),
                   jax.ShapeDtypeStruct(d_all.shape[1:], F32)),
        compiler_params=_params(None, VMEM_BIG),
    )(a_t, w, d_ex, d_ctx, d_all)


SMALL_ROWS = 24
SMALL_MISC, SMALL_CW, SMALL_LOSS = 16, 18, 21


def _pack_small(sums1, sums2, fsums, sums1c, psums, psums_c, d_cw, cols, name):
    d = D_MODEL

    def body(s1_ref, s2_ref, f_ref, s1c_ref, p_ref, pc_ref, cw_ref, o_ref):
        o_ref[...] = jnp.zeros_like(o_ref)

        def blocks(row0, pieces):
            for j in range(N_DEV):
                lo, hi = j * cols, (j + 1) * cols
                for k, (ref, r) in enumerate(pieces):
                    a, b = max(lo, k * d), min(hi, (k + 1) * d)
                    if a < b:
                        o_ref[row0 + j:row0 + j + 1, a - lo:b - lo] = ref[r:r + 1, a - k * d:b - k * d]

        blocks(0, [(s1_ref, 1), (s1_ref, 0), (s2_ref, 2), (s2_ref, 1), (s2_ref, 0), (f_ref, 1)])
        blocks(N_DEV, [(s1c_ref, 1), (s1c_ref, 0)])
        head = Q_RANK + KV_RANK
        o_ref[SMALL_MISC:SMALL_MISC + 1, 0:Q_RANK] = p_ref[0:1, :]
        o_ref[SMALL_MISC:SMALL_MISC + 1, Q_RANK:head] = p_ref[1:2, 0:KV_RANK] + pc_ref[1:2, 0:KV_RANK]
        o_ref[SMALL_MISC:SMALL_MISC + 1, head:cols] = f_ref[0:1, 0:cols - head]
        o_ref[SMALL_MISC + 1:SMALL_MISC + 2, 0:d - (cols - head)] = f_ref[0:1, cols - head:d]
        for r in range(3):
            o_ref[SMALL_CW + r:SMALL_CW + r + 1, 0:CONV_W] = cw_ref[r:r + 1, :]
        o_ref[SMALL_LOSS:SMALL_LOSS + 1, :] = f_ref[3:4, 0:cols]

    return pl.pallas_call(body, name=name, out_shape=jax.ShapeDtypeStruct((SMALL_ROWS, cols), F32))(
        sums1, sums2, fsums, sums1c, psums, psums_c, d_cw)


def _adam_math(w, g, m, v):
    nm = ADAM_B1 * m + (1.0 - ADAM_B1) * g
    nv = ADAM_B2 * v + (1.0 - ADAM_B2) * (g * g)
    m_hat = nm / (1.0 - ADAM_B1 ** ADAM_STEP)
    v_hat = nv / (1.0 - ADAM_B2 ** ADAM_STEP)
    return -ADAM_LR * (m_hat / (jnp.sqrt(v_hat) + ADAM_EPS) + ADAM_WD * w), nm, nv


def _small_update(dsum, dsil_all, g_cw, params, name):
    d = D_MODEL
    n = len(params)
    cols = dsum.shape[1]

    def body(*refs):
        dsum_ref, dsil_ref, gcw_ref = refs[:3]
        wmv = refs[3:3 + 3 * n]
        outs = refs[3 + 3 * n:]
        tot = dsil_ref[0]
        for j in range(1, N_DEV):
            tot = tot + dsil_ref[j]
        cv = wmv[0][...]
        sg = 1.0 / (1.0 + jnp.exp(-cv))
        off = Q_RANK + KV_RANK
        misc = dsum_ref[SMALL_MISC:SMALL_MISC + 1, :]
        grads = [tot[0:1, :] * (sg * (1.0 + cv * (1.0 - sg))),
                 jnp.concatenate([dsum_ref[j:j + 1, :] + dsum_ref[N_DEV + j:N_DEV + j + 1, :] for j in range(N_DEV)],
                                 axis=1),
                 misc[:, 0:Q_RANK], misc[:, Q_RANK:off],
                 jnp.concatenate([misc[:, off:cols], dsum_ref[SMALL_MISC + 1:SMALL_MISC + 2, 0:d - (cols - off)]],
                                 axis=1),
                 gcw_ref[...]]
        for p, g in enumerate(grads):
            w_ref, m_ref, v_ref = wmv[3 * p:3 * p + 3]
            at = 0 if len(w_ref.shape) == 3 else Ellipsis
            res = (g,) + _adam_math(w_ref[at], g, m_ref[at], v_ref[at])
            for q, val in enumerate(res):
                outs[4 * p + q][at] = val

    flat = [a for wmv in params for a in wmv]
    out_shape = tuple(jax.ShapeDtypeStruct(wmv[0].shape, F32) for wmv in params for _ in range(4))
    outs = pl.pallas_call(body, name=name, out_shape=out_shape)(dsum, dsil_all, g_cw, *flat)
    return [outs[4 * p:4 * p + 4] for p in range(n)]


def _adamw(w, g, m, v, name, slots=False):
    _, rows, cols = w.shape
    tr = _pick(rows, (256, 128, 64, 32, 16, 8))

    def body(w_ref, g_ref, m_ref, v_ref, *outs):
        if slots:
            gv = g_ref[0].astype(F32)
            for j in range(1, g.shape[0]):
                gv = gv + g_ref[j].astype(F32)
            outs[0][...] = gv
        else:
            gv = g_ref[...]
        d_ref, nm_ref, nv_ref = outs[-3:]
        d_ref[...], nm_ref[...], nv_ref[...] = _adam_math(w_ref[...], gv, m_ref[...], v_ref[...])

    blk = pl.BlockSpec((None, tr, cols), lambda i: (0, i, 0))
    g_spec = (pl.BlockSpec((g.shape[0], tr, cols), lambda i: (0, i, 0)) if slots
              else pl.BlockSpec((tr, cols), lambda i: (i, 0)))
    sh = jax.ShapeDtypeStruct((1, rows, cols), F32)
    n_out = 4 if slots else 3
    return pl.pallas_call(
        body, name=name, grid=(rows // tr,), out_shape=(sh,) * n_out,
        in_specs=[blk, g_spec, blk, blk], out_specs=(blk,) * n_out,
        compiler_params=_params(("parallel",), VMEM_BIG),
    )(w, g, m, v)


def _rope_tables(s, l):
    tok = np.arange(s)
    row = (tok // GRID_W).astype(np.float32)
    col = (tok % GRID_W).astype(np.float32)
    half = QK_ROPE // 2
    freqs = np.float32(ROPE_THETA) ** (-np.arange(0, half, 2, dtype=np.float32) / np.float32(half))
    dd = np.arange(QK_ROPE)
    pos = np.where((dd // half)[None, :] == 0, row[:, None], col[:, None]).astype(np.float32)
    ang = (pos * freqs[dd % (half // 2)][None, :]).astype(np.float32)
    sin = np.sin(ang).astype(np.float32)
    cos_t = np.ones((s + l, LANES), np.float32)
    sgn_t = np.zeros((s + l, LANES), np.float32)
    cos_t[:s, QK_NOPE:QK_NOPE + QK_ROPE] = np.cos(ang)
    sgn_t[:s, QK_NOPE:QK_NOPE + QK_ROPE] = np.where(((dd % half) // (half // 2))[None, :] == 0, -sin, sin)
    return jnp.asarray(cos_t), jnp.asarray(sgn_t)


def _slots_to_cols(g):
    return g.transpose(1, 0, 2).reshape(g.shape[1], N_DEV * g.shape[2])


def _cols_to_slots(w):
    return w.reshape(w.shape[0], N_DEV, w.shape[1] // N_DEV).transpose(1, 0, 2)


def _unpack_small_weights(g_in_t, g_uq, g_ukv):
    w_t = g_in_t.reshape(N_DEV * g_in_t.shape[1], D_MODEL)
    zeros = jnp.zeros((QK_NOPE, D_MODEL), BF16)
    win_head_t = jnp.concatenate([w_t[:Q_RANK + KV_RANK], zeros, w_t[Q_RANK + KV_RANK:MLA_IN],
                                  zeros[:LANES - QK_NOPE - QK_ROPE]], axis=0)
    win_conv_t = w_t[MLA_IN:].reshape(3, CONV_W // LANES, LANES, D_MODEL).transpose(1, 0, 2, 3)
    win_conv_t = win_conv_t.reshape(3 * CONV_W, D_MODEL)
    w_uq = _slots_to_cols(g_uq).reshape(Q_RANK, N_HEADS, QK_NOPE + QK_ROPE)
    wq = jnp.pad(w_uq, ((0, 0), (0, 0), (0, LANES - QK_NOPE - QK_ROPE))).reshape(Q_RANK, N_HEADS * LANES)
    w_ukv = _slots_to_cols(g_ukv).reshape(KV_RANK, N_HEADS, QK_NOPE + V_DIM)
    k_top = jnp.pad(w_ukv[:, :, :QK_NOPE], ((0, 0), (0, 0), (0, LANES - QK_NOPE))).reshape(KV_RANK, N_HEADS * LANES)
    v_top = w_ukv[:, :, QK_NOPE:].reshape(KV_RANK, N_HEADS * V_DIM)
    eye = jnp.pad(jnp.eye(QK_ROPE, dtype=BF16), ((QK_NOPE, LANES - QK_NOPE - QK_ROPE),) * 2)
    wk = jnp.concatenate([
        jnp.concatenate([k_top, v_top], axis=1),
        jnp.concatenate([jnp.tile(eye, (1, N_HEADS)), jnp.zeros((LANES, N_HEADS * V_DIM), BF16)], axis=1)], axis=0)
    return win_head_t, win_conv_t, wq, wk


def _pack_small_grads(d_head_t, d_conv_t, d_wq, d_wkk, d_wkv):
    d_conv_t = d_conv_t.reshape(CONV_W // LANES, 3, LANES, D_MODEL).transpose(1, 0, 2, 3).reshape(3 * CONV_W, D_MODEL)
    rope0 = Q_RANK + KV_RANK + QK_NOPE
    g_in_t = jnp.concatenate([d_head_t[:Q_RANK + KV_RANK], d_head_t[rope0:rope0 + QK_ROPE], d_conv_t], axis=0)
    g_in_t = g_in_t.reshape(N_DEV, -1, D_MODEL).astype(BF16)
    g_uq = d_wq.reshape(Q_RANK, N_HEADS, LANES)[:, :, :QK_NOPE + QK_ROPE].reshape(Q_RANK, -1)
    g_kn = d_wkk[:KV_RANK].reshape(KV_RANK, N_HEADS, LANES)[:, :, :QK_NOPE]
    g_v = d_wkv[:KV_RANK].reshape(KV_RANK, N_HEADS, V_DIM)
    g_ukv = jnp.concatenate([g_kn, g_v], axis=2).reshape(KV_RANK, -1)
    return [g_in_t] + [_cols_to_slots(g).astype(BF16) for g in (g_uq, g_ukv)]


def kernel(x, c, ctx, c_ctx, w_mod, b_mod, w_in, q_norm_g, w_uq, kv_norm_g, w_ukv, conv_w, w_out, w_mlp1, w_mlp2, final_norm_g, loss_target, m_c_ctx, m_w_mod, m_b_mod, m_w_in, m_q_norm_g, m_w_uq, m_kv_norm_g, m_w_ukv, m_conv_w, m_w_out, m_w_mlp1, m_w_mlp2, m_final_norm_g, v_c_ctx, v_w_mod, v_b_mod, v_w_in, v_q_norm_g, v_w_uq, v_kv_norm_g, v_w_ukv, v_conv_w, v_w_out, v_w_mlp1, v_w_mlp2, v_final_norm_g):
    me = _my_index()
    x2d, ctx2d, tgt = x[0], ctx[0], loss_target[0]
    s, l = x2d.shape[0], ctx2d.shape[0]
    t = s + l
    d = D_MODEL
    mod_cols = w_mod.shape[2]
    cw_cols = conv_w.shape[2]

    b_cols = lax.dynamic_slice(b_mod, (0, me * mod_cols), (1, mod_cols))
    cw_blk = jnp.pad(conv_w[0], ((0, 0), (0, mod_cols - cw_cols)))
    a_rows, gathered = _prologue(jnp.pad(c, ((0, 7), (0, 0))), c_ctx[None, :], w_mod[0], b_cols, cw_blk,
                                 "prologue")
    mod_mine = gathered[:, 0, :].reshape(1, 6 * d)
    mod_ctx = gathered[:, 1, :].reshape(1, 6 * d)
    cw_full = gathered[:, 2:5, :cw_cols].transpose(1, 0, 2).reshape(3, CONV_W)

    early = [w.astype(BF16) for w in (w_in[0].T, w_uq[0], w_ukv[0])]
    late = [w.astype(BF16) for w in (w_out[0], w_mlp1[0], w_mlp2[0])]
    h_all, (g_in, g_uq, g_ukv) = _modulate_all(x2d, ctx2d, mod_mine, mod_ctx, _RidingGather(early),
                                               "modulate1")
    win_head, win_conv, wq, wk = _unpack_small_weights(g_in, g_uq, g_ukv)
    wk_k, wk_v = wk[:, :N_HEADS * LANES], wk[:, N_HEADS * LANES:]
    cos, sgn = _rope_tables(s, l)

    tm_t = _pick(t, (1088, 768, 256))
    z_head, cq, kv_in, qf, kv = _head_fwd(h_all, win_head, wq, wk, q_norm_g, kv_norm_g, cos, sgn, tm_t, "head_fwd")
    z_conv = _matmul(h_all, win_conv, mode="nt", name="in_proj_conv", m=s, tm=1024, tn=1536, tk=1024)
    attn, a_cat, stats, (g_out, w1, g_w2) = _attn_fwd(qf, kv, s, _RidingGather(late), "attn_fwd")
    wo = g_out.reshape(d, d)
    w2 = g_w2.reshape(D_FF, d)
    a_cat = _conv_fwd(z_conv, cw_full, a_cat, "conv_fwd")
    (o, x1, h2), _ = _matmul_rows(a_cat, wo, _epi_resid_modulate, mode="nn", name="out_proj", tm=1024, tk=1024,
                                  rows=[x2d], vecs=[(mod_mine, 2), (mod_mine, 3), (mod_mine, 4)],
                                  out_dtypes=[F32, F32, BF16])
    u1, act = _matmul(h2, w1, mode="nn", name="mlp_up", tm=4096, tk=1024, epilogue="relu2", slots="b_cols")
    (dx2, dm, fsums), _ = _matmul_rows(act, w2, _epi_final, mode="nn", name="mlp_down", tm=512, tk=4096,
                                       rows=[x1, tgt], vecs=[(mod_mine, 5), (final_norm_g[None, :], 0)],
                                       out_dtypes=[F32, BF16], sums=True)

    d_w2 = _matmul(act, dm, mode="tn", name="d_w_mlp2", out_dtype=BF16, tm=1024, tn=1024, tk=4096)
    du1 = _matmul(dm, w2, mode="nt", name="d_act", out_dtype=BF16, tm=2048, tn=1024, tk=1024,
                  epilogue="drelu2", extra=(u1,))
    d_w1 = _matmul(h2, du1, mode="tn", name="d_w_mlp1", out_dtype=BF16, tm=1024, tk=4096, slots="out")
    (dx1, do, sums2), _ = _matmul_rows(du1, w1, _epi_modulate2_bwd, mode="nt", name="d_h2", tm=512, tk=4096,
                                       slots="b_contract", rows=[x1, dx2, o], vecs=[(mod_mine, 4), (mod_mine, 2)],
                                       out_dtypes=[F32, BF16], sums=True)
    d_wo = _matmul(a_cat, do, mode="tn", name="d_w_out", out_dtype=BF16, tm=1024, tn=1024, tk=2048)
    da = _matmul(do, wo, mode="nt", name="d_a", tm=1024, tn=1024, tk=1024)
    dz_conv, d_cw, d_conv = _conv_bwd(z_conv, cw_full, da, h_all, "conv_bwd")
    ready = [d_wo.reshape(N_DEV, d // N_DEV, d), d_w1, d_w2.reshape(N_DEV, D_FF // N_DEV, d)]
    dq, dk, dv, rode = _attn_bwd(qf, kv, attn, da, stats, cos, sgn, _Riding(ready), "attn_bwd")
    head_args = (z_head, wq, wk_k, wk_v, win_head, q_norm_g, kv_norm_g, cos, sgn, cq, kv_in, h_all)
    dz_head, dh_head, psums, d_wq, *carried = _head_bwd(dq, dk, dv, *head_args, "head_bwd", tile=HEAD_BWD_TILE,
                                                        first_block=0, n_blocks=s // HEAD_BWD_TILE)
    _, dh_head, psums_c, d_wkk, d_wkv, d_head = _head_bwd(
        None, dk, dv, *head_args, "head_bwd_ctx", tile=ROW_TILE, first_block=s // ROW_TILE, n_blocks=l // ROW_TILE,
        carry=(dz_head, dh_head, *carried))
    send = _pack_small_grads(d_head, d_conv, d_wq, d_wkk, d_wkv)
    (grad_x, sums1), got = _matmul_rows(dz_conv, win_conv, _epi_modulate1_bwd, mode="nn", name="d_h1", tm=max(s // 8, ROW_TILE),
                                        tk=win_conv.shape[0], rows=[dh_head, x2d, dx1], vecs=[(mod_mine, 1)],
                                        out_dtypes=[F32], sums=True, riding=_RidingReduce(send))
    sums1c = _modulate_sums(dh_head, s // ROW_TILE, ctx2d)

    small = _pack_small(sums1, sums2, fsums, sums1c, psums, psums_c, d_cw, mod_cols, "pack_small")
    (d_all,) = _all_gather([small], "gather_small_grads", True)
    d_ex = lax.dynamic_index_in_dim(d_all, me, axis=1, keepdims=False)
    d_ctx = lax.dynamic_index_in_dim(d_all, N_DEV + me, axis=1, keepdims=False)
    g_w_mod, dsil, dsum = _adaln_bwd(a_rows.T, w_mod[0], d_ex, d_ctx, d_all, "adaln_bwd")
    (dsil_all,) = _all_gather([dsil], "gather_d_cctx", True)
    loss = dsum[SMALL_LOSS, 0]
    g_cw = lax.dynamic_slice(dsum, (SMALL_CW, me * cw_cols), (3, cw_cols))

    slots = dict(zip(["w_in", "w_uq", "w_ukv"], got))
    slots.update(zip(["w_out", "w_mlp1", "w_mlp2"], rode))

    grads = {}
    weights = {"c_ctx": c_ctx, "w_mod": w_mod, "b_mod": b_mod, "w_in": w_in, "q_norm_g": q_norm_g, "w_uq": w_uq,
               "kv_norm_g": kv_norm_g, "w_ukv": w_ukv, "conv_w": conv_w, "w_out": w_out, "w_mlp1": w_mlp1,
               "w_mlp2": w_mlp2, "final_norm_g": final_norm_g}
    m_in = {"c_ctx": m_c_ctx, "w_mod": m_w_mod, "b_mod": m_b_mod, "w_in": m_w_in, "q_norm_g": m_q_norm_g,
            "w_uq": m_w_uq, "kv_norm_g": m_kv_norm_g, "w_ukv": m_w_ukv, "conv_w": m_conv_w, "w_out": m_w_out,
            "w_mlp1": m_w_mlp1, "w_mlp2": m_w_mlp2, "final_norm_g": m_final_norm_g}
    v_in = {"c_ctx": v_c_ctx, "w_mod": v_w_mod, "b_mod": v_b_mod, "w_in": v_w_in, "q_norm_g": v_q_norm_g,
            "w_uq": v_w_uq, "kv_norm_g": v_kv_norm_g, "w_ukv": v_w_ukv, "conv_w": v_conv_w, "w_out": v_w_out,
            "w_mlp1": v_w_mlp1, "w_mlp2": v_w_mlp2, "final_norm_g": v_final_norm_g}
    names = list(weights)
    small_names = ["c_ctx", "b_mod", "q_norm_g", "kv_norm_g", "final_norm_g", "conv_w"]
    delta, new_m, new_v = {}, {}, {}

    def as_rows(a):
        return a[None, :] if a.ndim == 1 else a

    small_out = _small_update(dsum, dsil_all, g_cw, [[as_rows(src[n]) for src in (weights, m_in, v_in)]
                                                      for n in small_names], "small_update")
    for n, outs in zip(small_names, small_out):
        grads[n], delta[n], new_m[n], new_v[n] = [a.reshape(weights[n].shape) for a in outs]
    for n in names:
        if n in small_names:
            continue
        if n == "w_in":
            wmv = [jnp.swapaxes(src[n], 1, 2) for src in (weights, m_in, v_in)]
            outs = _adamw(wmv[0], slots[n], wmv[1], wmv[2], "adamw_" + n, slots=True)
            grads[n], delta[n], new_m[n], new_v[n] = [jnp.swapaxes(a, 1, 2) for a in outs]
        elif n in slots:
            grads[n], delta[n], new_m[n], new_v[n] = _adamw(weights[n], slots[n], m_in[n], v_in[n], "adamw_" + n,
                                                            slots=True)
        else:
            delta[n], new_m[n], new_v[n] = _adamw(weights[n], g_w_mod, m_in[n], v_in[n], "adamw_" + n)
            grads[n] = g_w_mod[None]

    return (loss, grad_x[None], *[grads[n] for n in names], *[delta[n] for n in names],
            *[new_m[n] for n in names], *[new_v[n] for n in names])
```

```python
import math

import jax
import jax.numpy as jnp
import numpy as np
from jax import lax
from jax.experimental import pallas as pl
from jax.experimental.pallas import tpu as pltpu

F32 = jnp.float32
BF16 = jnp.bfloat16

D_MODEL = 1024
GRID_W = 64
N_HEADS = 8
QK_NOPE = 64
QK_ROPE = 32
V_DIM = 64
Q_RANK = 256
KV_RANK = 128
MLA_IN = Q_RANK + KV_RANK + QK_ROPE
CONV_W = 512
HEAD_COLS = 512
D_FF = 4096
ROPE_THETA = 10000.0
EPS = 1e-6
ATTN_SCALE = 1.0 / math.sqrt(QK_NOPE + QK_ROPE)
LOG2_E = 1.0 / math.log(2.0)
EXP2_SCALE = ATTN_SCALE * LOG2_E
N_DEV = 8
LANES = 128

ADAM_LR, ADAM_B1, ADAM_B2, ADAM_EPS, ADAM_WD, ADAM_STEP = 0.001, 0.9, 0.999, 1e-08, 0.01, 10

ROW_TILE = 256
HEAD_BWD_TILE = 512
VMEM_BIG = 60 * 1024 * 1024


def _params(sem=None, vmem=None):
    return pltpu.CompilerParams(dimension_semantics=sem, vmem_limit_bytes=vmem)


def _pick(n, prefs):
    for p in prefs:
        if n % p == 0:
            return p
    return n


def _my_index():
    return 4 * lax.axis_index("x") + 2 * lax.axis_index("y") + lax.axis_index("c")


def _two_level_gather(x_refs, out_refs, send_sems, recv_sems, local_sems):
    n = len(x_refs)
    x, y, c = lax.axis_index("x"), lax.axis_index("y"), lax.axis_index("c")
    me, sibling = (x, y, c), (x, y, 1 - c)
    chips = [(1 - x, y), (x, 1 - y), (1 - x, 1 - y)]

    def slot(a, px, py, pc):
        return out_refs[a].at[4 * px + 2 * py + pc]

    def copy(a, k, block, to, src=None):
        return pltpu.make_async_remote_copy(
            src_ref=slot(a, *block) if src is None else src, dst_ref=slot(a, *block),
            send_sem=send_sems.at[7 * a + k], recv_sem=recv_sems.at[7 * a + k],
            device_id=to, device_id_type=pl.DeviceIdType.MESH)

    mine = [pltpu.make_async_copy(x_refs[a], slot(a, *me), local_sems.at[a]) for a in range(n)]
    first = [cp for a in range(n) for cp in
             [copy(a, 0, me, sibling, src=x_refs[a])]
             + [copy(a, 1 + j, me, (*chip, c), src=x_refs[a]) for j, chip in enumerate(chips)]]
    passed = [[copy(a, 4 + j, (*chip, c), sibling) for j, chip in enumerate(chips)] for a in range(n)]

    def start():
        for cp in mine + first:
            cp.start()

    def forward():
        for a in range(n):
            for j, chip in enumerate(chips):
                copy(a, 1 + j, (*chip, c), me).wait_recv()
                passed[a][j].start()

    def finish():
        for a in range(n):
            copy(a, 0, sibling, me).wait_recv()
            for j, chip in enumerate(chips):
                copy(a, 4 + j, (*chip, 1 - c), me).wait_recv()
        for cp in first + [cp for per_array in passed for cp in per_array]:
            cp.wait_send()
        for cp in mine:
            cp.wait()

    return start, forward, finish


def _direct_gather(src_ref, dst_ref, send_sems, recv_sems, per_peer=False):
    x, y, c = lax.axis_index("x"), lax.axis_index("y"), lax.axis_index("c")
    me = 4 * x + 2 * y + c
    dst_ref[me] = src_ref[me] if per_peer else src_ref[...]
    sends, landings = [], []
    for k in range(1, N_DEV):
        peer = (1 - x if k & 4 else x, 1 - y if k & 2 else y, 1 - c if k & 1 else c)
        pid = 4 * peer[0] + 2 * peer[1] + peer[2]
        for dst, out in ((me, sends), (pid, landings)):
            out.append(pltpu.make_async_remote_copy(
                src_ref=src_ref.at[pid] if per_peer else src_ref, dst_ref=dst_ref.at[dst],
                send_sem=send_sems.at[k - 1], recv_sem=recv_sems.at[k - 1],
                device_id=peer, device_id_type=pl.DeviceIdType.MESH))
    for cp in sends:
        cp.start()

    def finish():
        for cp in landings:
            cp.wait_recv()
        for cp in sends:
            cp.wait_send()

    return finish


def _all_gather(arrays, name, in_vmem):
    space = pltpu.VMEM if in_vmem else pl.ANY
    n = len(arrays)

    def body(*refs):
        for phase in _two_level_gather(refs[:n], refs[n:2 * n], *refs[2 * n:]):
            phase()

    outs = pl.pallas_call(
        body, name=name,
        out_shape=tuple(jax.ShapeDtypeStruct((N_DEV,) + a.shape, a.dtype) for a in arrays),
        in_specs=[pl.BlockSpec(memory_space=space)] * n,
        out_specs=tuple(pl.BlockSpec(memory_space=space) for _ in arrays),
        scratch_shapes=[pltpu.SemaphoreType.DMA((7 * n,)), pltpu.SemaphoreType.DMA((7 * n,)),
                        pltpu.SemaphoreType.DMA((n,))],
    )(*arrays)
    return list(outs)


class _Riding:
    def __init__(self, arrays=()):
        self.arrays, self.n = list(arrays), len(arrays)
        self.out_shape = [jax.ShapeDtypeStruct(a.shape, a.dtype) for a in self.arrays]
        self.specs = [pl.BlockSpec(memory_space=pl.ANY)] * self.n
        self.scratch = [pltpu.SemaphoreType.DMA((7 * self.n,)), pltpu.SemaphoreType.DMA((7 * self.n,)),
                        pltpu.SemaphoreType.DMA((self.n,))]

    def copies(self, x_refs, y_refs, send_sems, recv_sems, local_sems):
        x, y, c = lax.axis_index("x"), lax.axis_index("y"), lax.axis_index("c")
        me = 4 * x + 2 * y + c
        local, sends, landings = [], [], []
        for a in range(self.n):
            local.append(pltpu.make_async_copy(x_refs[a].at[me], y_refs[a].at[me], local_sems.at[a]))
            for k in range(1, N_DEV):
                peer = (1 - x if k & 4 else x, 1 - y if k & 2 else y, 1 - c if k & 1 else c)
                pid = 4 * peer[0] + 2 * peer[1] + peer[2]
                for dst, out in ((me, sends), (pid, landings)):
                    out.append(pltpu.make_async_remote_copy(
                        src_ref=x_refs[a].at[pid], dst_ref=y_refs[a].at[dst],
                        send_sem=send_sems.at[7 * a + k - 1], recv_sem=recv_sems.at[7 * a + k - 1],
                        device_id=peer, device_id_type=pl.DeviceIdType.MESH))
        return local, sends, landings

    def run(self, first, last, x_refs, y_refs, sems, middle=None):
        if self.n == 0:
            return None
        local, sends, landings = self.copies(x_refs, y_refs, *sems)

        @pl.when(first)
        def _():
            for cp in local + sends:
                cp.start()

        return local, sends, landings, last

    @staticmethod
    def finish(state):
        if state is None:
            return
        local, sends, landings, last = state

        @pl.when(last)
        def _():
            for cp in landings:
                cp.wait_recv()
            for cp in sends:
                cp.wait_send()
            for cp in local:
                cp.wait()


class _RidingGather:
    def __init__(self, arrays):
        self.arrays, self.n = list(arrays), len(arrays)
        self.out_shape = [jax.ShapeDtypeStruct((N_DEV,) + a.shape, a.dtype) for a in self.arrays]
        self.specs = [pl.BlockSpec(memory_space=pl.ANY)] * self.n
        self.scratch = [pltpu.SemaphoreType.DMA((7 * self.n,)), pltpu.SemaphoreType.DMA((7 * self.n,)),
                        pltpu.SemaphoreType.DMA((self.n,))]

    def run(self, first, last, x_refs, y_refs, sems, middle):
        start, forward, finish = _two_level_gather(x_refs, y_refs, *sems)
        pl.when(first)(start)
        pl.when(middle)(forward)
        return finish, last

    @staticmethod
    def finish(state):
        finish, last = state
        pl.when(last)(finish)


class _RidingReduce:
    def __init__(self, arrays):
        self.arrays, self.n = list(arrays), len(arrays)
        self.out_shape = [jax.ShapeDtypeStruct((4,) + a.shape[1:], a.dtype) for a in self.arrays]
        self.specs = [pl.BlockSpec(memory_space=pl.ANY)] * self.n
        self.scratch = [pltpu.VMEM((4,) + a.shape[1:], a.dtype) for a in self.arrays for _ in range(3)]
        self.scratch += [pltpu.SemaphoreType.DMA((self.n,)) for _ in range(6)]

    def run(self, first, last, x_refs, y_refs, scratch, middle):
        n = self.n
        own, sib, tot = scratch[0:3 * n:3], scratch[1:3 * n:3], scratch[2:3 * n:3]
        d2d_send, d2d_recv, local_in, ici_send, ici_recv, local_out = scratch[3 * n:]
        x, y, c = lax.axis_index("x"), lax.axis_index("y"), lax.axis_index("c")
        my_chip = 2 * x + y
        sibling = (x, y, 1 - c)
        others = [(1 - x, y), (x, 1 - y), (1 - x, 1 - y)]

        def to_sibling(a, j=None):
            src = x_refs[a].at[pl.ds(0, 4)] if j is None else x_refs[a].at[2 * j + 1 - c]
            dst = sib[a] if j is None else sib[a].at[j]
            return pltpu.make_async_remote_copy(src_ref=src, dst_ref=dst, send_sem=d2d_send.at[a],
                                                recv_sem=d2d_recv.at[a], device_id=sibling,
                                                device_id_type=pl.DeviceIdType.MESH)

        def mine_in(a, j=None):
            src = x_refs[a].at[pl.ds(0, 4)] if j is None else x_refs[a].at[2 * j + c]
            return pltpu.make_async_copy(src, own[a] if j is None else own[a].at[j], local_in.at[a])

        def to_chip(a, chip=None):
            if chip is None:
                src, dst, peer = tot[a].at[pl.ds(0, 3)], y_refs[a].at[pl.ds(0, 3)], sibling
            else:
                src, dst, peer = tot[a].at[2 * chip[0] + chip[1]], y_refs[a].at[my_chip], (*chip, c)
            return pltpu.make_async_remote_copy(src_ref=src, dst_ref=dst, send_sem=ici_send.at[a],
                                                recv_sem=ici_recv.at[a], device_id=peer,
                                                device_id_type=pl.DeviceIdType.MESH)

        def mine_out(a):
            return pltpu.make_async_copy(tot[a].at[my_chip], y_refs[a].at[my_chip], local_out.at[a])

        @pl.when(first)
        def _():
            for a in range(n):
                for j in range(4):
                    to_sibling(a, j).start()
                    mine_in(a, j).start()

        @pl.when(middle)
        def _():
            for a in range(n):
                to_sibling(a).wait_recv()
                to_sibling(a).wait_send()
                mine_in(a).wait()
                tot[a][...] = (own[a][...].astype(F32) + sib[a][...].astype(F32)).astype(tot[a].dtype)
                for chip in others:
                    to_chip(a, chip).start()
                mine_out(a).start()

        def finish():
            @pl.when(last)
            def _():
                for a in range(n):
                    to_chip(a).wait_recv()
                    to_chip(a).wait_send()
                    mine_out(a).wait()

        return finish

    @staticmethod
    def finish(state):
        state()


_DIMS ={"nn": (((1,), (0,)), ((), ())), "nt": (((1,), (1,)), ((), ())), "tn": (((0,), (0,)), ((), ()))}
NT_DIMS = _DIMS["nt"]
TN_DIMS = _DIMS["tn"]


def _swap8(x):
    lane = lax.broadcasted_iota(jnp.int32, x.shape, 1)
    return jnp.where((lane & 15) < 8, pltpu.roll(x, LANES - 8, 1), pltpu.roll(x, 8, 1))


def _rope(x, cos, sgn, bwd):
    return x * cos + (_swap8(x * sgn) if bwd else _swap8(x) * sgn)


def _matmul(a, b, *, mode, name, out_dtype=F32, tm=512, tn=512, tk=512, m=None, k=None,
            epilogue=None, extra=(), slots=None):
    if mode == "nn":
        m = a.shape[0] if m is None else m
        k = a.shape[1]
        n = N_DEV * b.shape[2] if slots == "b_cols" else b.shape[1]
    elif mode == "nt":
        m = a.shape[0] if m is None else m
        k = a.shape[1]
        n = b.shape[0]
    else:
        k = a.shape[0] if k is None else k
        m, n = a.shape[1], b.shape[1]
    tm, tn, tk = min(tm, m), min(tn, n), min(tk, k)
    if slots == "b_cols":
        tn = b.shape[2]
    if slots == "out":
        tn = n // N_DEV
    assert m % tm == 0 and n % tn == 0 and k % tk == 0, (name, m, n, k, tm, tn, tk)
    nk = k // tk
    dims = _DIMS[mode]
    a_spec = (pl.BlockSpec((tk, tm), lambda i, j, kk: (kk, i)) if mode == "tn"
              else pl.BlockSpec((tm, tk), lambda i, j, kk: (i, kk)))
    if slots == "b_cols":
        b_spec = pl.BlockSpec((None, tk, tn), lambda i, j, kk: (j, kk, 0))
    elif mode == "nt":
        b_spec = pl.BlockSpec((tn, tk), lambda i, j, kk: (j, kk))
    else:
        b_spec = pl.BlockSpec((tk, tn), lambda i, j, kk: (kk, j))
    tile = pl.BlockSpec((tm, tn), lambda i, j, kk: (i, j))
    if slots == "out":
        o_spec = pl.BlockSpec((None, tm, tn), lambda i, j, kk: (j, i, 0))
        o_shape = (N_DEV, m, tn)
    else:
        o_spec, o_shape = tile, (m, n)
    in_specs, args = [a_spec, b_spec], [a, b]
    if epilogue == "drelu2":
        in_specs.append(tile)
    args += list(extra)
    if epilogue == "relu2":
        out_shape = (jax.ShapeDtypeStruct(o_shape, BF16), jax.ShapeDtypeStruct(o_shape, BF16))
        out_specs = (o_spec, o_spec)
    else:
        out_shape = jax.ShapeDtypeStruct(o_shape, out_dtype)
        out_specs = o_spec
    n_in = len(args)
    n_out = 2 if epilogue == "relu2" else 1

    def body(*refs):
        a_ref, b_ref = refs[0], refs[1]
        outs = refs[n_in:n_in + n_out]
        part = lax.dot_general(a_ref[...], b_ref[...], dims, preferred_element_type=F32)

        def finish(acc):
            if epilogue == "relu2":
                outs[0][...] = acc.astype(BF16)
                r = jnp.maximum(acc, 0.0)
                outs[1][...] = (r * r).astype(BF16)
            elif epilogue == "drelu2":
                u = refs[2][...].astype(F32)
                outs[0][...] = (acc * (2.0 * jnp.maximum(u, 0.0))).astype(out_dtype)
            else:
                outs[0][...] = acc.astype(out_dtype)

        if nk == 1:
            finish(part)
        else:
            acc_ref = refs[n_in + n_out]
            kk = pl.program_id(2)

            @pl.when(kk == 0)
            def _():
                acc_ref[...] = part

            @pl.when(kk > 0)
            def _():
                acc_ref[...] += part

            @pl.when(kk == nk - 1)
            def _():
                finish(acc_ref[...])

    return pl.pallas_call(
        body, name=name, grid=(m // tm, n // tn, nk),
        out_shape=out_shape, in_specs=in_specs, out_specs=out_specs,
        scratch_shapes=[pltpu.VMEM((tm, tn), F32)] if nk > 1 else [],
        compiler_params=_params(("parallel", "parallel", "arbitrary"), VMEM_BIG),
    )(*args)


def _rstd(x):
    return lax.rsqrt(jnp.mean(x * x, axis=1, keepdims=True) + EPS)


def _norm_bwd(dxn, xn, r):
    return r * (dxn - xn * jnp.mean(dxn * xn, axis=1, keepdims=True))


def _vec(col):
    return pl.BlockSpec((1, D_MODEL), lambda i: (0, col))


def _matmul_rows(a, b, epi, *, mode, name, tm, tk, rows=(), vecs=(), out_dtypes=(), sums=False, slots=None,
                 riding=None):
    m, k = a.shape
    n = D_MODEL
    tm, tk = min(tm, m), min(tk, k)
    riding = riding or _Riding()
    group = 1
    if slots == "b_contract":
        group = max(1, tk // b.shape[2])
        tk = group * b.shape[2]
        b_spec = pl.BlockSpec((group, n, tk // group), lambda i, kk: (kk, 0, 0))
    elif mode == "nt":
        b_spec = pl.BlockSpec((n, tk), lambda i, kk: (0, kk))
    else:
        b_spec = pl.BlockSpec((tk, n), lambda i, kk: (kk, 0))
    assert m % tm == 0 and k % tk == 0, (name, m, k, tm, tk)
    ni, nk = m // tm, k // tk
    assert ni >= 2 or not isinstance(riding, _RidingReduce), "the two-level exchange needs a middle grid step"
    dims = _DIMS[mode]
    tile = pl.BlockSpec((tm, n), lambda i, kk: (i, 0))
    in_specs = [pl.BlockSpec((tm, tk), lambda i, kk: (i, kk)), b_spec] + [tile] * len(rows)
    in_specs += [pl.BlockSpec((1, n), lambda i, kk, col=col: (0, col)) for _, col in vecs]
    args = [a, b, *rows, *[v for v, _ in vecs]]
    out_shape = [jax.ShapeDtypeStruct((m, n), dt) for dt in out_dtypes]
    out_specs = [tile] * len(out_dtypes)
    if sums:
        out_shape.append(jax.ShapeDtypeStruct((8, n), F32))
        out_specs.append(pl.BlockSpec((8, n), lambda i, kk: (0, 0)))
    n_rows, n_vecs, n_outs, nr = len(rows), len(vecs), len(out_dtypes), riding.n
    n_in = 2 + n_rows + n_vecs

    def body(*refs):
        a_ref, b_ref = refs[0], refs[1]
        row_refs = refs[2:2 + n_rows]
        vec_refs = refs[2 + n_rows:n_in]
        x_refs = refs[n_in:n_in + nr]
        out_refs = refs[n_in + nr:n_in + nr + n_outs]
        pos = n_in + nr + n_outs
        sums_ref = refs[pos] if sums else None
        pos += 1 if sums else 0
        y_refs = refs[pos:pos + nr]
        pos += nr
        acc_ref = refs[pos] if nk > 1 else None
        sem_refs = refs[pos + (1 if nk > 1 else 0):]
        i, kk = pl.program_id(0), pl.program_id(1)
        state = riding.run((i == 0) & (kk == 0), (i == ni - 1) & (kk == nk - 1), x_refs, y_refs, sem_refs,
                           middle=(i == 1) & (kk == 0))
        if slots == "b_contract":
            c = tk // group
            part = lax.dot_general(a_ref[:, 0:c], b_ref[0], dims, preferred_element_type=F32)
            for u in range(1, group):
                part = part + lax.dot_general(a_ref[:, u * c:(u + 1) * c], b_ref[u], dims, preferred_element_type=F32)
        else:
            part = lax.dot_general(a_ref[...], b_ref[...], dims, preferred_element_type=F32)

        def finish(acc):
            nsub = tm // ROW_TILE
            for r in range(nsub):
                blk = pl.ds(r * ROW_TILE, ROW_TILE)
                epi(acc[r * ROW_TILE:(r + 1) * ROW_TILE], [ref.at[blk] for ref in row_refs], vec_refs,
                    [ref.at[blk] for ref in out_refs], sums_ref,
                    (i == 0) if r == 0 else None, (i == ni - 1) if r == nsub - 1 else None)

        if nk == 1:
            finish(part)
        else:
            @pl.when(kk == 0)
            def _():
                acc_ref[...] = part

            @pl.when(kk > 0)
            def _():
                acc_ref[...] += part

            @pl.when(kk == nk - 1)
            def _():
                finish(acc_ref)

        riding.finish(state)

    outs = pl.pallas_call(
        body, name=name, grid=(ni, nk),
        out_shape=(*out_shape, *riding.out_shape),
        in_specs=[*in_specs, *riding.specs], out_specs=(*out_specs, *riding.specs),
        scratch_shapes=([pltpu.VMEM((tm, n), F32)] if nk > 1 else []) + (riding.scratch if nr else []),
        compiler_params=_params(("arbitrary", "arbitrary"), VMEM_BIG),
    )(*args, *riding.arrays)
    n_own = len(out_shape)
    return list(outs[:n_own]), list(outs[n_own:])


def _zero_sums_at_start(sums_ref, first):
    if first is not None:
        @pl.when(first)
        def _():
            sums_ref[...] = jnp.zeros_like(sums_ref)


def _epi_resid_modulate(acc, rows, vecs, outs, sums_ref, first, last):
    (x_ref,), (g_ref, sh_ref, sc_ref) = rows, vecs
    x1 = x_ref[...] + g_ref[...] * acc
    outs[0][...] = acc
    outs[1][...] = x1
    outs[2][...] = (x1 * _rstd(x1) * (1.0 + sc_ref[...]) + sh_ref[...]).astype(BF16)


def _epi_final(acc, rows, vecs, outs, sums_ref, first, last):
    (x1_ref, t_ref), (g_ref, gf_ref) = rows, vecs
    d = acc.shape[1]
    x2 = x1_ref[...] + g_ref[...] * acc
    r = _rstd(x2)
    xn = x2 * r
    err = xn * gf_ref[...] - t_ref[...]
    dy = err * (1.0 / d)
    dx2 = _norm_bwd(dy * gf_ref[...], xn, r)
    outs[0][...] = dx2
    outs[1][...] = (dx2 * g_ref[...]).astype(BF16)
    _zero_sums_at_start(sums_ref, first)
    sums_ref[0:1, :] += jnp.sum(dy * xn, axis=0, keepdims=True)
    sums_ref[1:2, :] += jnp.sum(dx2 * acc, axis=0, keepdims=True)
    sums_ref[2:3, :] += jnp.sum(err * err, axis=0, keepdims=True)

    if last is not None:
        @pl.when(last)
        def _():
            tot = jnp.sum(sums_ref[2:3, :], axis=1, keepdims=True) * (0.5 / d)
            sums_ref[3:4, :] = jnp.broadcast_to(tot, (1, d))


def _epi_modulate2_bwd(acc, rows, vecs, outs, sums_ref, first, last):
    (x_ref, dres_ref, o_ref), (sc_ref, g_ref) = rows, vecs
    x = x_ref[...]
    r = _rstd(x)
    xn = x * r
    dx = dres_ref[...] + _norm_bwd(acc * (1.0 + sc_ref[...]), xn, r)
    outs[0][...] = dx
    outs[1][...] = (dx * g_ref[...]).astype(BF16)
    _zero_sums_at_start(sums_ref, first)
    sums_ref[0:1, :] += jnp.sum(acc * xn, axis=0, keepdims=True)
    sums_ref[1:2, :] += jnp.sum(acc, axis=0, keepdims=True)
    sums_ref[2:3, :] += jnp.sum(dx * o_ref[...], axis=0, keepdims=True)


def _epi_modulate1_bwd(acc, rows, vecs, outs, sums_ref, first, last):
    (add_ref, x_ref, dres_ref), (sc_ref,) = rows, vecs
    dh = acc + add_ref[...]
    x = x_ref[...]
    r = _rstd(x)
    xn = x * r
    outs[0][...] = dres_ref[...] + _norm_bwd(dh * (1.0 + sc_ref[...]), xn, r)
    _zero_sums_at_start(sums_ref, first)
    sums_ref[0:1, :] += jnp.sum(dh * xn, axis=0, keepdims=True)
    sums_ref[1:2, :] += jnp.sum(dh, axis=0, keepdims=True)


def _modulate_all(x, ctx, mod, mod_ctx, riding, name):
    s, d = x.shape
    t = s + ctx.shape[0]
    ns = s // ROW_TILE
    nc = ctx.shape[0] // ROW_TILE
    nr = riding.n

    def body(*refs):
        x_ref, c_ref, sh_ref, sc_ref, shc_ref, scc_ref = refs[:6]
        h_ref = refs[6 + nr]
        i = pl.program_id(0)
        state = riding.run(i == 0, i == ns + nc - 1, refs[6:6 + nr], refs[7 + nr:7 + 2 * nr], refs[7 + 2 * nr:],
                           middle=i == ns + nc - 3)

        @pl.when(i < ns)
        def _():
            v = x_ref[...]
            h_ref[...] = (v * _rstd(v) * (1.0 + sc_ref[...]) + sh_ref[...]).astype(BF16)

        @pl.when(i >= ns)
        def _():
            v = c_ref[...]
            h_ref[...] = (v * _rstd(v) * (1.0 + scc_ref[...]) + shc_ref[...]).astype(BF16)

        riding.finish(state)

    outs = pl.pallas_call(
        body, name=name, grid=(ns + nc,),
        out_shape=(jax.ShapeDtypeStruct((t, d), BF16), *riding.out_shape),
        in_specs=[pl.BlockSpec((ROW_TILE, d), lambda i: (jnp.minimum(i, ns - 1), 0)),
                  pl.BlockSpec((ROW_TILE, d), lambda i: (jnp.maximum(i - ns, 0), 0)),
                  _vec(0), _vec(1), _vec(0), _vec(1), *riding.specs],
        out_specs=(pl.BlockSpec((ROW_TILE, d), lambda i: (i, 0)), *riding.specs),
        scratch_shapes=riding.scratch,
        compiler_params=_params(("arbitrary",)),
    )(x, ctx, mod, mod, mod_ctx, mod_ctx, *riding.arrays)
    return outs[0], list(outs[1:])


def _modulate_sums(dh, row_off, xsrc):
    s, d = xsrc.shape

    def body(dh_ref, x_ref, sums_ref):
        i = pl.program_id(0)
        x = x_ref[...]
        dhv = dh_ref[...]

        @pl.when(i == 0)
        def _():
            sums_ref[...] = jnp.zeros_like(sums_ref)

        sums_ref[0:1, :] += jnp.sum(dhv * (x * _rstd(x)), axis=0, keepdims=True)
        sums_ref[1:2, :] += jnp.sum(dhv, axis=0, keepdims=True)

    return pl.pallas_call(
        body, name="modulate1_ctx_bwd", grid=(s // ROW_TILE,),
        out_shape=jax.ShapeDtypeStruct((8, d), F32),
        in_specs=[pl.BlockSpec((ROW_TILE, d), lambda i: (i + row_off, 0)), pl.BlockSpec((ROW_TILE, d), lambda i: (i, 0))],
        out_specs=pl.BlockSpec((8, d), lambda i: (0, 0)),
        compiler_params=_params(("arbitrary",)),
    )(dh, xsrc)


def _head_fwd(h_all, win_head, wq, wk, q_gain, kv_gain, cos, sgn, tm, name):
    t, d = h_all.shape
    nq, nkv = wq.shape[1], wk.shape[1]

    def body(h_ref, wi_ref, wq_ref, wk_ref, qg_ref, kg_ref, c_ref, s_ref, z_ref, cq_ref, kvin_ref, qf_ref, kv_ref):
        z = lax.dot_general(h_ref[...], wi_ref[...], NT_DIMS, preferred_element_type=F32)
        z_ref[...] = z
        cos, sgn = c_ref[...], s_ref[...]
        zq = z[:, 0:Q_RANK]
        cq = (zq * _rstd(zq) * qg_ref[...]).astype(BF16)
        cq_ref[...] = cq
        zk = z[:, Q_RANK:Q_RANK + KV_RANK]
        kv_in = jnp.concatenate([(zk * _rstd(zk) * kg_ref[...]).astype(BF16),
                                 _rope(z[:, Q_RANK + KV_RANK:HEAD_COLS], cos, sgn, False).astype(BF16)], axis=1)
        kvin_ref[...] = kv_in
        q = jnp.dot(cq, wq_ref[...], preferred_element_type=F32)
        for h in range(nq // LANES):
            sl = slice(h * LANES, (h + 1) * LANES)
            qf_ref[:, sl] = _rope(q[:, sl], cos, sgn, False).astype(BF16)
        kv_ref[...] = jnp.dot(kv_in, wk_ref[...], preferred_element_type=F32).astype(BF16)

    def row(w):
        return pl.BlockSpec((tm, w), lambda i: (i, 0))

    def whole(a):
        return pl.BlockSpec(a.shape, lambda i: (0, 0))

    return pl.pallas_call(
        body, name=name, grid=(t // tm,),
        out_shape=(jax.ShapeDtypeStruct((t, HEAD_COLS), F32), jax.ShapeDtypeStruct((t, Q_RANK), BF16),
                   jax.ShapeDtypeStruct((t, KV_RANK + LANES), BF16), jax.ShapeDtypeStruct((t, nq), BF16),
                   jax.ShapeDtypeStruct((t, nkv), BF16)),
        in_specs=[row(d), whole(win_head), whole(wq), whole(wk), whole(q_gain), whole(kv_gain), row(LANES), row(LANES)],
        out_specs=(row(HEAD_COLS), row(Q_RANK), row(KV_RANK + LANES), row(nq), row(nkv)),
        compiler_params=_params(("parallel",), VMEM_BIG),
    )(h_all, win_head, wq, wk, q_gain, kv_gain, cos, sgn)


def _head_bwd(dq, dk, dv, z, wq, wk_k, wk_v, win_head, q_gain, kv_gain, cos, sgn, cq, kv_in, h_all, name, *, tile,
              first_block, n_blocks, carry=None):
    t = z.shape[0]
    with_q = dq is not None

    def body(*refs):
        it = iter(refs)
        dq_ref = next(it) if with_q else None
        dk_ref, dv_ref, z_ref, wq_ref, wkk_ref, wkv_ref, wi_ref, qg_ref, kg_ref, c_ref, s_ref = (next(it) for _ in range(11))
        cq_ref = next(it) if with_q else None
        kvin_ref, h_ref = next(it), next(it)
        before = None
        if carry is not None:
            next(it), next(it)
            before = (next(it), next(it), next(it))
        dz_ref, dh_ref, sums_ref = next(it), next(it), next(it)
        gq_ref = next(it) if with_q else None
        grads = (next(it), next(it), next(it))
        i = pl.program_id(0)

        @pl.when(i == 0)
        def _():
            sums_ref[...] = jnp.zeros_like(sums_ref)
            if with_q:
                gq_ref[...] = jnp.zeros_like(gq_ref)
            for k, g_ref in enumerate(grads):
                g_ref[...] = jnp.zeros_like(g_ref) if before is None else before[k][...]

        if with_q:
            gq_ref[...] += lax.dot_general(cq_ref[...], dq_ref[...], TN_DIMS, preferred_element_type=F32)
        grads[0][...] += lax.dot_general(kvin_ref[...], dk_ref[...], TN_DIMS, preferred_element_type=F32)
        grads[1][...] += lax.dot_general(kvin_ref[...], dv_ref[...], TN_DIMS, preferred_element_type=F32)
        if with_q:
            dc = lax.dot_general(dq_ref[...], wq_ref[...], NT_DIMS, preferred_element_type=F32)
            zq = z_ref[:, 0:Q_RANK]
            r = _rstd(zq)
            zn = zq * r
            sums_ref[0:1, :] += jnp.sum(dc * zn, axis=0, keepdims=True)
            dz_ref[:, 0:Q_RANK] = _norm_bwd(dc * qg_ref[...], zn, r).astype(BF16)
        else:
            dz_ref[:, 0:Q_RANK] = jnp.zeros((tile, Q_RANK), BF16)
        dkv = (lax.dot_general(dk_ref[...], wkk_ref[...], NT_DIMS, preferred_element_type=F32)
               + lax.dot_general(dv_ref[...], wkv_ref[...], NT_DIMS, preferred_element_type=F32))
        zk = z_ref[:, Q_RANK:Q_RANK + KV_RANK]
        r = _rstd(zk)
        zn = zk * r
        dc = dkv[:, 0:KV_RANK]
        sums_ref[1:2, 0:KV_RANK] += jnp.sum(dc * zn, axis=0, keepdims=True)
        dz_ref[:, Q_RANK:Q_RANK + KV_RANK] = _norm_bwd(dc * kg_ref[...], zn, r).astype(BF16)
        dz_ref[:, Q_RANK + KV_RANK:HEAD_COLS] = _rope(dkv[:, KV_RANK:KV_RANK + LANES], c_ref[...], s_ref[...],
                                                       True).astype(BF16)
        dh_ref[...] = jnp.dot(dz_ref[...], wi_ref[...], preferred_element_type=F32)
        grads[2][...] += lax.dot_general(dz_ref[...], h_ref[...], TN_DIMS, preferred_element_type=F32)

    def row(w):
        return pl.BlockSpec((tile, w), lambda i: (i + first_block, 0))

    def whole(a):
        return pl.BlockSpec(a.shape, lambda i: (0, 0))

    args = ([dq] if with_q else []) + [dk, dv, z, wq, wk_k, wk_v, win_head, q_gain, kv_gain, cos, sgn]
    args += ([cq] if with_q else []) + [kv_in, h_all]
    in_specs = ([row(dq.shape[1])] if with_q else []) + [
        row(dk.shape[1]), row(dv.shape[1]), row(HEAD_COLS), whole(wq), whole(wk_k), whole(wk_v),
        whole(win_head), whole(q_gain), whole(kv_gain), row(LANES), row(LANES)]
    in_specs += ([row(Q_RANK)] if with_q else []) + [row(kv_in.shape[1]), row(D_MODEL)]
    aliases = {}
    if carry is not None:
        aliases = {len(args): 0, len(args) + 1: 1}
        args += list(carry)
        in_specs += [pl.BlockSpec(memory_space=pl.ANY)] * 2 + [whole(a) for a in carry[2:]]
    grad_shapes = ([(Q_RANK, dq.shape[1])] if with_q else []) + [
        (kv_in.shape[1], dk.shape[1]), (kv_in.shape[1], dv.shape[1]), (HEAD_COLS, D_MODEL)]
    return pl.pallas_call(
        body, name=name, grid=(n_blocks,),
        out_shape=(jax.ShapeDtypeStruct((t, HEAD_COLS), BF16), jax.ShapeDtypeStruct((t, D_MODEL), F32),
                   jax.ShapeDtypeStruct((8, Q_RANK), F32), *[jax.ShapeDtypeStruct(g, F32) for g in grad_shapes]),
        in_specs=in_specs,
        out_specs=(row(HEAD_COLS), row(D_MODEL), pl.BlockSpec((8, Q_RANK), lambda i: (0, 0)),
                   *[pl.BlockSpec(g, lambda i: (0, 0)) for g in grad_shapes]),
        input_output_aliases=aliases,
        compiler_params=_params(("arbitrary",), VMEM_BIG),
    )(*args)


def _out_proj_bwd(a, dy, w, name, tm=1024):
    s, k = a.shape
    n = dy.shape[1]

    def body(a_ref, dy_ref, w_ref, da_ref, dw_ref, acc_ref):
        i = pl.program_id(0)
        da_ref[...] = lax.dot_general(dy_ref[...], w_ref[...], NT_DIMS, preferred_element_type=F32)
        part = lax.dot_general(a_ref[...], dy_ref[...], TN_DIMS, preferred_element_type=F32)

        @pl.when(i == 0)
        def _():
            acc_ref[...] = part

        @pl.when(i > 0)
        def _():
            acc_ref[...] += part

        @pl.when(i == pl.num_programs(0) - 1)
        def _():
            dw_ref[...] = acc_ref[...].astype(BF16)

    return pl.pallas_call(
        body, name=name, grid=(s // tm,),
        out_shape=(jax.ShapeDtypeStruct((s, k), F32), jax.ShapeDtypeStruct((k, n), BF16)),
        in_specs=[pl.BlockSpec((tm, k), lambda i: (i, 0)), pl.BlockSpec((tm, n), lambda i: (i, 0)),
                  pl.BlockSpec(w.shape, lambda i: (0, 0))],
        out_specs=(pl.BlockSpec((tm, k), lambda i: (i, 0)), pl.BlockSpec((k, n), lambda i: (0, 0))),
        scratch_shapes=[pltpu.VMEM((k, n), F32)],
        compiler_params=_params(("arbitrary",), VMEM_BIG),
    )(a, dy, w)


def _shift_rows(u, s):
    rowi = lax.broadcasted_iota(jnp.int32, u.shape, 0)
    prev = jnp.where(rowi == 0, 0.0, pltpu.roll(u, 1, 0))
    nxt = jnp.where(rowi == s - 1, 0.0, pltpu.roll(u, s - 1, 0))
    return prev, nxt


def _conv_fwd(z_conv, cw, a_cat, name):
    s = z_conv.shape[0]

    def body(z_ref, w_ref, a_in_ref, o_ref):
        del a_in_ref
        gb, gc, xv = z_ref[:, 0:LANES], z_ref[:, LANES:2 * LANES], z_ref[:, 2 * LANES:3 * LANES]
        u = gc * xv
        prev, nxt = _shift_rows(u, s)
        y = w_ref[0:1, :] * prev + w_ref[1:2, :] * u + w_ref[2:3, :] * nxt
        o_ref[...] = (gb * y).astype(BF16)

    return pl.pallas_call(
        body, name=name, grid=(CONV_W // LANES,),
        out_shape=jax.ShapeDtypeStruct(a_cat.shape, a_cat.dtype),
        in_specs=[pl.BlockSpec((s, 3 * LANES), lambda j: (0, j)), pl.BlockSpec((3, LANES), lambda j: (0, j)),
                  pl.BlockSpec(memory_space=pl.ANY)],
        out_specs=pl.BlockSpec((s, LANES), lambda j: (0, 4 + j)),
        input_output_aliases={2: 0},
        compiler_params=_params(("parallel",), VMEM_BIG),
    )(z_conv, cw, a_cat)


def _conv_bwd(z_conv, cw, da, h_all, name):
    s = z_conv.shape[0]

    def body(z_ref, w_ref, da_ref, h_ref, dz_ref, dw_ref, g_ref):
        gb, gc, xv = z_ref[:, 0:LANES], z_ref[:, LANES:2 * LANES], z_ref[:, 2 * LANES:3 * LANES]
        u = gc * xv
        prev, nxt = _shift_rows(u, s)
        dcv = da_ref[...]
        dz_ref[:, 0:LANES] = (dcv * (w_ref[0:1, :] * prev + w_ref[1:2, :] * u + w_ref[2:3, :] * nxt)).astype(BF16)
        dy = dcv * gb
        dw_ref[0:1, :] = jnp.sum(dy * prev, axis=0, keepdims=True)
        dw_ref[1:2, :] = jnp.sum(dy * u, axis=0, keepdims=True)
        dw_ref[2:3, :] = jnp.sum(dy * nxt, axis=0, keepdims=True)
        dyp, dyn = _shift_rows(dy, s)
        du = w_ref[0:1, :] * dyn + w_ref[1:2, :] * dy + w_ref[2:3, :] * dyp
        dz_ref[:, LANES:2 * LANES] = (du * xv).astype(BF16)
        dz_ref[:, 2 * LANES:3 * LANES] = (du * gc).astype(BF16)
        g_ref[...] = lax.dot_general(dz_ref[...], h_ref[...], TN_DIMS, preferred_element_type=F32)

    d = h_all.shape[1]
    blk = pl.BlockSpec((s, 3 * LANES), lambda j: (0, j))
    cws = pl.BlockSpec((3, LANES), lambda j: (0, j))
    return pl.pallas_call(
        body, name=name, grid=(CONV_W // LANES,),
        out_shape=(jax.ShapeDtypeStruct(z_conv.shape, BF16), jax.ShapeDtypeStruct((3, CONV_W), F32),
                   jax.ShapeDtypeStruct((z_conv.shape[1], d), F32)),
        in_specs=[blk, cws, pl.BlockSpec((s, LANES), lambda j: (0, 4 + j)), pl.BlockSpec((s, d), lambda j: (0, 0))],
        out_specs=(blk, cws, pl.BlockSpec((3 * LANES, d), lambda j: (j, 0))),
        compiler_params=_params(("parallel",), VMEM_BIG),
    )(z_conv, cw, da, h_all)


ATT_TQ = 512
ATT_Q_STEP = 1024
ATT_TQ_BWD = 512


def _head_mask(shape, hh):
    lane = lax.broadcasted_iota(jnp.int32, shape, 1)
    return (lane >= hh * V_DIM) & (lane < (hh + 1) * V_DIM)


def _attn_fwd(qf, kv, s, riding, name):
    t = kv.shape[0]
    step = min(ATT_Q_STEP, s)
    nq = s // step
    nr = riding.n

    def body(*refs):
        q_ref, k_ref, v_ref = refs[:3]
        o_ref, ob_ref, st_ref = refs[3 + nr:6 + nr]
        p, i = pl.program_id(0), pl.program_id(1)
        state = riding.run((p == 0) & (i == 0), (p == N_HEADS // 2 - 1) & (i == nq - 1),
                           refs[3:3 + nr], refs[6 + nr:6 + 2 * nr], refs[6 + 2 * nr:],
                           middle=(p == N_HEADS // 2 - 2) & (i == nq // 2))
        v = v_ref[...]
        vlane = lax.broadcasted_iota(jnp.int32, v.shape, 1)
        one_lane = [(1 - hh) * V_DIM for hh in range(2)]
        vm = [jnp.where(_head_mask(v.shape, hh), v, jnp.where(vlane == one_lane[hh], 1.0, 0.0).astype(BF16))
              for hh in range(2)]

        def block(r, carry):
            rows = pl.ds(pl.multiple_of(r * ATT_TQ, ATT_TQ), ATT_TQ)
            olane = lax.broadcasted_iota(jnp.int32, (ATT_TQ, LANES), 1)
            acc = jnp.zeros((ATT_TQ, LANES), F32)
            stat = jnp.zeros((ATT_TQ, LANES), F32)
            scores = [lax.dot_general(q_ref[rows, hh * LANES:(hh + 1) * LANES], k_ref[:, hh * LANES:(hh + 1) * LANES],
                                      NT_DIMS, preferred_element_type=F32) for hh in range(2)]
            maxes = [jnp.max(sc, axis=1, keepdims=True) for sc in scores]
            exps = [jnp.exp2((sc - mx) * EXP2_SCALE).astype(BF16) for sc, mx in zip(scores, maxes)]
            for hh in range(2):
                mx = maxes[hh]
                res = jnp.dot(exps[hh], vm[hh], preferred_element_type=F32)
                den = jnp.sum(jnp.where(olane == one_lane[hh], res, 0.0), axis=1, keepdims=True)
                acc = acc + jnp.where(_head_mask(res.shape, hh), res * (1.0 / den), 0.0)
                stat = stat + jnp.where(olane == hh, mx * EXP2_SCALE + jnp.log(den) * LOG2_E, 0.0)
            o_ref[rows, :] = acc
            ob_ref[rows, :] = acc.astype(BF16)
            st_ref[:, rows] = stat.T[0:8, :]
            return carry

        lax.fori_loop(0, step // ATT_TQ, block, 0)
        riding.finish(state)

    o_spec = pl.BlockSpec((step, LANES), lambda p, i: (i, p))
    outs = pl.pallas_call(
        body, name=name, grid=(N_HEADS // 2, nq),
        out_shape=(jax.ShapeDtypeStruct((s, N_HEADS * V_DIM), F32),
                   jax.ShapeDtypeStruct((s, D_MODEL), BF16),
                   jax.ShapeDtypeStruct((N_HEADS // 2 * 8, s), F32), *riding.out_shape),
        in_specs=[pl.BlockSpec((step, 2 * LANES), lambda p, i: (i, p)),
                  pl.BlockSpec((t, 2 * LANES), lambda p, i: (0, p)),
                  pl.BlockSpec((t, LANES), lambda p, i: (0, N_HEADS + p)), *riding.specs],
        out_specs=(o_spec, o_spec, pl.BlockSpec((8, step), lambda p, i: (p, i)), *riding.specs),
        scratch_shapes=riding.scratch,
        compiler_params=_params(("arbitrary", "arbitrary"), VMEM_BIG),
    )(qf, kv, kv, *riding.arrays)
    return outs[0], outs[1], outs[2], list(outs[3:])


def _attn_bwd(qf, kv, o, da, stats, cos, sgn, riding, name):
    s, t = o.shape[0], kv.shape[0]
    ATT_TQ = ATT_TQ_BWD
    nq = s // ATT_TQ
    nr = riding.n

    def body(*refs):
        q_ref, k_ref, v_ref, o_ref, do_ref, st_ref, c_ref, s_ref = refs[:8]
        dq_ref, dk_ref, dv_ref = refs[8 + nr:11 + nr]
        dk_acc, dv_acc = refs[11 + 2 * nr:13 + 2 * nr]
        p, i = pl.program_id(0), pl.program_id(1)
        state = riding.run((p == 0) & (i == 0), (p == N_HEADS // 2 - 1) & (i == nq - 1),
                           refs[8:8 + nr], refs[11 + nr:11 + 2 * nr], refs[13 + 2 * nr:])

        @pl.when(i == 0)
        def _():
            dk_acc[...] = jnp.zeros_like(dk_acc)
            dv_acc[...] = jnp.zeros_like(dv_acc)

        v = v_ref[...]
        do = do_ref[...]
        od = do * o_ref[...]
        ones = jnp.ones((8, LANES), F32)
        for hh in range(2):
            sl = slice(hh * LANES, (hh + 1) * LANES)
            q, k = q_ref[:, sl], k_ref[:, sl]
            mask = _head_mask(do.shape, hh)
            dom = jnp.where(mask, do, 0.0).astype(BF16)
            delta = lax.dot_general(ones, jnp.where(mask, od, 0.0), NT_DIMS, preferred_element_type=F32,
                                    precision=lax.Precision.HIGHEST)[0:1, :]
            st = lax.dot_general(k, q, NT_DIMS, preferred_element_type=F32)
            pt = jnp.exp2(st * EXP2_SCALE - st_ref[hh:hh + 1, :]).astype(BF16)
            dpt = lax.dot_general(v, dom, NT_DIMS, preferred_element_type=F32)
            dst = (pt.astype(F32) * (dpt - delta)).astype(BF16)
            dv_acc[...] += jnp.dot(pt, dom, preferred_element_type=F32)
            dk_acc[:, sl] += jnp.dot(dst, q, preferred_element_type=F32)
            dq = lax.dot_general(dst, k, TN_DIMS, preferred_element_type=F32) * ATTN_SCALE
            dq_ref[:, sl] = _rope(dq, c_ref[...], s_ref[...], True).astype(BF16)

        @pl.when(i == nq - 1)
        def _():
            dk_ref[...] = (dk_acc[...] * ATTN_SCALE).astype(BF16)
            dv_ref[...] = dv_acc[...].astype(BF16)

        riding.finish(state)

    o_spec = pl.BlockSpec((ATT_TQ, LANES), lambda p, i: (i, p))
    tab = pl.BlockSpec((ATT_TQ, LANES), lambda p, i: (i, 0))
    outs = pl.pallas_call(
        body, name=name, grid=(N_HEADS // 2, nq),
        out_shape=(jax.ShapeDtypeStruct((s, N_HEADS * LANES), BF16),
                   jax.ShapeDtypeStruct((t, N_HEADS * LANES), BF16),
                   jax.ShapeDtypeStruct((t, N_HEADS * V_DIM), BF16), *riding.out_shape),
        in_specs=[pl.BlockSpec((ATT_TQ, 2 * LANES), lambda p, i: (i, p)),
                  pl.BlockSpec((t, 2 * LANES), lambda p, i: (0, p)),
                  pl.BlockSpec((t, LANES), lambda p, i: (0, N_HEADS + p)),
                  o_spec, o_spec,
                  pl.BlockSpec((8, ATT_TQ), lambda p, i: (p, i)), tab, tab, *riding.specs],
        out_specs=(pl.BlockSpec((ATT_TQ, 2 * LANES), lambda p, i: (i, p)),
                   pl.BlockSpec((t, 2 * LANES), lambda p, i: (0, p)),
                   pl.BlockSpec((t, LANES), lambda p, i: (0, p)), *riding.specs),
        scratch_shapes=[pltpu.VMEM((t, 2 * LANES), F32), pltpu.VMEM((t, LANES), F32), *riding.scratch],
        compiler_params=_params(("arbitrary", "arbitrary"), VMEM_BIG),
    )(qf, kv, kv, o, da, stats, cos, sgn, *riding.arrays)
    return outs[0], outs[1], outs[2], list(outs[3:])


def _silu(x):
    return x * (1.0 / (1.0 + jnp.exp(-x)))


def _prologue(c_rows, c_ctx, w_mod, b_cols, extra_rows, name):
    d, cols = c_rows.shape[1], w_mod.shape[1]

    def body(c_ref, cctx_ref, wmod_ref, b_ref, x_ref, a_ref, modg_ref, c_all, blk, c_send, c_recv, m_send, m_recv):
        _direct_gather(c_ref, c_all, c_send, c_recv)()
        a_ref[...] = jnp.zeros_like(a_ref)
        for j in range(N_DEV):
            a_ref[j:j + 1, :] = c_all[j, 0:1, :]
        a_ref[N_DEV:N_DEV + 1, :] = cctx_ref[...]
        mod = jnp.dot(_silu(a_ref[...]), wmod_ref[...], preferred_element_type=F32,
                      precision=lax.Precision.HIGHEST) + b_ref[...]
        blk[...] = jnp.zeros_like(blk)
        for p in range(N_DEV):
            blk[p, 0:1, :] = mod[p:p + 1, :]
            blk[p, 1:2, :] = mod[N_DEV:N_DEV + 1, :]
            blk[p, 2:5, :] = x_ref[...]
        _direct_gather(blk, modg_ref, m_send, m_recv, per_peer=True)()

    vmem = pl.BlockSpec(memory_space=pltpu.VMEM)
    return pl.pallas_call(
        body, name=name,
        out_shape=(jax.ShapeDtypeStruct((16, d), F32), jax.ShapeDtypeStruct((N_DEV, 8, cols), F32)),
        in_specs=[vmem] * 5, out_specs=(vmem, vmem),
        scratch_shapes=[pltpu.VMEM((N_DEV, 8, d), F32), pltpu.VMEM((N_DEV, 8, cols), F32)]
        + [pltpu.SemaphoreType.DMA((7,)) for _ in range(4)],
        compiler_params=_params(None, VMEM_BIG),
    )(c_rows, c_ctx, w_mod, b_cols, extra_rows)


def _adaln_bwd(a_t, w, d_ex, d_ctx, d_all, name):
    def body(at_ref, w_ref, dex_ref, dctx_ref, dall_ref, gw_ref, dsil_ref, dsum_ref):
        sil_t = _silu(at_ref[...])
        dctx = dctx_ref[...]
        row = dctx[0:1, :]
        for j in range(1, N_DEV):
            row = row + dctx[j:j + 1, :]
        rowi = lax.broadcasted_iota(jnp.int32, dctx.shape, 0)
        ctx_rows = jnp.where(rowi == 0, jnp.broadcast_to(row, dctx.shape), 0.0)
        hi = lax.Precision.HIGHEST
        d_rows = jnp.concatenate([dex_ref[...], ctx_rows], axis=0)
        gw_ref[...] = jnp.dot(sil_t, d_rows, preferred_element_type=F32, precision=hi)
        dsil_ref[...] = lax.dot_general(ctx_rows, w_ref[...], NT_DIMS, preferred_element_type=F32, precision=hi)
        tot = dall_ref[0]
        for j in range(1, N_DEV):
            tot = tot + dall_ref[j]
        dsum_ref[...] = tot

    return pl.pallas_call(
        body, name=name,
        out_shape=(jax.ShapeDtypeStruct(w.shape, F32), jax.ShapeDtypeStruct((8, w.shape[0]), F32),
                   jax.ShapeDtypeStruct(d_all.shape[1:], F32)),
        compiler_params=_params(None, VMEM_BIG),
    )(a_t, w, d_ex, d_ctx, d_all)


SMALL_ROWS = 24
SMALL_MISC, SMALL_CW, SMALL_LOSS = 16, 18, 21


def _pack_small(sums1, sums2, fsums, sums1c, psums, psums_c, d_cw, cols, name):
    d = D_MODEL

    def body(s1_ref, s2_ref, f_ref, s1c_ref, p_ref, pc_ref, cw_ref, o_ref):
        o_ref[...] = jnp.zeros_like(o_ref)

        def blocks(row0, pieces):
            for j in range(N_DEV):
                lo, hi = j * cols, (j + 1) * cols
                for k, (ref, r) in enumerate(pieces):
                    a, b = max(lo, k * d), min(hi, (k + 1) * d)
                    if a < b:
                        o_ref[row0 + j:row0 + j + 1, a - lo:b - lo] = ref[r:r + 1, a - k * d:b - k * d]

        blocks(0, [(s1_ref, 1), (s1_ref, 0), (s2_ref, 2), (s2_ref, 1), (s2_ref, 0), (f_ref, 1)])
        blocks(N_DEV, [(s1c_ref, 1), (s1c_ref, 0)])
        head = Q_RANK + KV_RANK
        o_ref[SMALL_MISC:SMALL_MISC + 1, 0:Q_RANK] = p_ref[0:1, :]
        o_ref[SMALL_MISC:SMALL_MISC + 1, Q_RANK:head] = p_ref[1:2, 0:KV_RANK] + pc_ref[1:2, 0:KV_RANK]
        o_ref[SMALL_MISC:SMALL_MISC + 1, head:cols] = f_ref[0:1, 0:cols - head]
        o_ref[SMALL_MISC + 1:SMALL_MISC + 2, 0:d - (cols - head)] = f_ref[0:1, cols - head:d]
        for r in range(3):
            o_ref[SMALL_CW + r:SMALL_CW + r + 1, 0:CONV_W] = cw_ref[r:r + 1, :]
        o_ref[SMALL_LOSS:SMALL_LOSS + 1, :] = f_ref[3:4, 0:cols]

    return pl.pallas_call(body, name=name, out_shape=jax.ShapeDtypeStruct((SMALL_ROWS, cols), F32))(
        sums1, sums2, fsums, sums1c, psums, psums_c, d_cw)


def _adam_math(w, g, m, v):
    nm = ADAM_B1 * m + (1.0 - ADAM_B1) * g
    nv = ADAM_B2 * v + (1.0 - ADAM_B2) * (g * g)
    m_hat = nm / (1.0 - ADAM_B1 ** ADAM_STEP)
    v_hat = nv / (1.0 - ADAM_B2 ** ADAM_STEP)
    return -ADAM_LR * (m_hat / (jnp.sqrt(v_hat) + ADAM_EPS) + ADAM_WD * w), nm, nv


def _small_update(dsum, dsil_all, g_cw, params, name):
    d = D_MODEL
    n = len(params)
    cols = dsum.shape[1]

    def body(*refs):
        dsum_ref, dsil_ref, gcw_ref = refs[:3]
        wmv = refs[3:3 + 3 * n]
        outs = refs[3 + 3 * n:]
        tot = dsil_ref[0]
        for j in range(1, N_DEV):
            tot = tot + dsil_ref[j]
        cv = wmv[0][...]
        sg = 1.0 / (1.0 + jnp.exp(-cv))
        off = Q_RANK + KV_RANK
        misc = dsum_ref[SMALL_MISC:SMALL_MISC + 1, :]
        grads = [tot[0:1, :] * (sg * (1.0 + cv * (1.0 - sg))),
                 jnp.concatenate([dsum_ref[j:j + 1, :] + dsum_ref[N_DEV + j:N_DEV + j + 1, :] for j in range(N_DEV)],
                                 axis=1),
                 misc[:, 0:Q_RANK], misc[:, Q_RANK:off],
                 jnp.concatenate([misc[:, off:cols], dsum_ref[SMALL_MISC + 1:SMALL_MISC + 2, 0:d - (cols - off)]],
                                 axis=1),
                 gcw_ref[...]]
        for p, g in enumerate(grads):
            w_ref, m_ref, v_ref = wmv[3 * p:3 * p + 3]
            at = 0 if len(w_ref.shape) == 3 else Ellipsis
            res = (g,) + _adam_math(w_ref[at], g, m_ref[at], v_ref[at])
            for q, val in enumerate(res):
                outs[4 * p + q][at] = val

    flat = [a for wmv in params for a in wmv]
    out_shape = tuple(jax.ShapeDtypeStruct(wmv[0].shape, F32) for wmv in params for _ in range(4))
    outs = pl.pallas_call(body, name=name, out_shape=out_shape)(dsum, dsil_all, g_cw, *flat)
    return [outs[4 * p:4 * p + 4] for p in range(n)]


def _adamw(w, g, m, v, name, slots=False):
    _, rows, cols = w.shape
    tr = _pick(rows, (256, 128, 64, 32, 16, 8))

    def body(w_ref, g_ref, m_ref, v_ref, *outs):
        if slots:
            gv = g_ref[0].astype(F32)
            for j in range(1, g.shape[0]):
                gv = gv + g_ref[j].astype(F32)
            outs[0][...] = gv
        else:
            gv = g_ref[...]
        d_ref, nm_ref, nv_ref = outs[-3:]
        d_ref[...], nm_ref[...], nv_ref[...] = _adam_math(w_ref[...], gv, m_ref[...], v_ref[...])

    blk = pl.BlockSpec((None, tr, cols), lambda i: (0, i, 0))
    g_spec = (pl.BlockSpec((g.shape[0], tr, cols), lambda i: (0, i, 0)) if slots
              else pl.BlockSpec((tr, cols), lambda i: (i, 0)))
    sh = jax.ShapeDtypeStruct((1, rows, cols), F32)
    n_out = 4 if slots else 3
    return pl.pallas_call(
        body, name=name, grid=(rows // tr,), out_shape=(sh,) * n_out,
        in_specs=[blk, g_spec, blk, blk], out_specs=(blk,) * n_out,
        compiler_params=_params(("parallel",), VMEM_BIG),
    )(w, g, m, v)


def _rope_tables(s, l):
    tok = np.arange(s)
    row = (tok // GRID_W).astype(np.float32)
    col = (tok % GRID_W).astype(np.float32)
    half = QK_ROPE // 2
    freqs = np.float32(ROPE_THETA) ** (-np.arange(0, half, 2, dtype=np.float32) / np.float32(half))
    dd = np.arange(QK_ROPE)
    pos = np.where((dd // half)[None, :] == 0, row[:, None], col[:, None]).astype(np.float32)
    ang = (pos * freqs[dd % (half // 2)][None, :]).astype(np.float32)
    sin = np.sin(ang).astype(np.float32)
    cos_t = np.ones((s + l, LANES), np.float32)
    sgn_t = np.zeros((s + l, LANES), np.float32)
    cos_t[:s, QK_NOPE:QK_NOPE + QK_ROPE] = np.cos(ang)
    sgn_t[:s, QK_NOPE:QK_NOPE + QK_ROPE] = np.where(((dd % half) // (half // 2))[None, :] == 0, -sin, sin)
    return jnp.asarray(cos_t), jnp.asarray(sgn_t)


def _slots_to_cols(g):
    return g.transpose(1, 0, 2).reshape(g.shape[1], N_DEV * g.shape[2])


def _cols_to_slots(w):
    return w.reshape(w.shape[0], N_DEV, w.shape[1] // N_DEV).transpose(1, 0, 2)


def _unpack_small_weights(g_in_t, g_uq, g_ukv):
    w_t = g_in_t.reshape(N_DEV * g_in_t.shape[1], D_MODEL)
    zeros = jnp.zeros((QK_NOPE, D_MODEL), BF16)
    win_head_t = jnp.concatenate([w_t[:Q_RANK + KV_RANK], zeros, w_t[Q_RANK + KV_RANK:MLA_IN],
                                  zeros[:LANES - QK_NOPE - QK_ROPE]], axis=0)
    win_conv_t = w_t[MLA_IN:].reshape(3, CONV_W // LANES, LANES, D_MODEL).transpose(1, 0, 2, 3)
    win_conv_t = win_conv_t.reshape(3 * CONV_W, D_MODEL)
    w_uq = _slots_to_cols(g_uq).reshape(Q_RANK, N_HEADS, QK_NOPE + QK_ROPE)
    wq = jnp.pad(w_uq, ((0, 0), (0, 0), (0, LANES - QK_NOPE - QK_ROPE))).reshape(Q_RANK, N_HEADS * LANES)
    w_ukv = _slots_to_cols(g_ukv).reshape(KV_RANK, N_HEADS, QK_NOPE + V_DIM)
    k_top = jnp.pad(w_ukv[:, :, :QK_NOPE], ((0, 0), (0, 0), (0, LANES - QK_NOPE))).reshape(KV_RANK, N_HEADS * LANES)
    v_top = w_ukv[:, :, QK_NOPE:].reshape(KV_RANK, N_HEADS * V_DIM)
    eye = jnp.pad(jnp.eye(QK_ROPE, dtype=BF16), ((QK_NOPE, LANES - QK_NOPE - QK_ROPE),) * 2)
    wk = jnp.concatenate([
        jnp.concatenate([k_top, v_top], axis=1),
        jnp.concatenate([jnp.tile(eye, (1, N_HEADS)), jnp.zeros((LANES, N_HEADS * V_DIM), BF16)], axis=1)], axis=0)
    return win_head_t, win_conv_t, wq, wk


def _pack_small_grads(d_head_t, d_conv_t, d_wq, d_wkk, d_wkv):
    d_conv_t = d_conv_t.reshape(CONV_W // LANES, 3, LANES, D_MODEL).transpose(1, 0, 2, 3).reshape(3 * CONV_W, D_MODEL)
    rope0 = Q_RANK + KV_RANK + QK_NOPE
    g_in_t = jnp.concatenate([d_head_t[:Q_RANK + KV_RANK], d_head_t[rope0:rope0 + QK_ROPE], d_conv_t], axis=0)
    g_in_t = g_in_t.reshape(N_DEV, -1, D_MODEL).astype(BF16)
    g_uq = d_wq.reshape(Q_RANK, N_HEADS, LANES)[:, :, :QK_NOPE + QK_ROPE].reshape(Q_RANK, -1)
    g_kn = d_wkk[:KV_RANK].reshape(KV_RANK, N_HEADS, LANES)[:, :, :QK_NOPE]
    g_v = d_wkv[:KV_RANK].reshape(KV_RANK, N_HEADS, V_DIM)
    g_ukv = jnp.concatenate([g_kn, g_v], axis=2).reshape(KV_RANK, -1)
    return [g_in_t] + [_cols_to_slots(g).astype(BF16) for g in (g_uq, g_ukv)]


def kernel(x, c, ctx, c_ctx, w_mod, b_mod, w_in, q_norm_g, w_uq, kv_norm_g, w_ukv, conv_w, w_out, w_mlp1, w_mlp2, final_norm_g, loss_target, m_c_ctx, m_w_mod, m_b_mod, m_w_in, m_q_norm_g, m_w_uq, m_kv_norm_g, m_w_ukv, m_conv_w, m_w_out, m_w_mlp1, m_w_mlp2, m_final_norm_g, v_c_ctx, v_w_mod, v_b_mod, v_w_in, v_q_norm_g, v_w_uq, v_kv_norm_g, v_w_ukv, v_conv_w, v_w_out, v_w_mlp1, v_w_mlp2, v_final_norm_g):
    me = _my_index()
    x2d, ctx2d, tgt = x[0], ctx[0], loss_target[0]
    s, l = x2d.shape[0], ctx2d.shape[0]
    t = s + l
    d = D_MODEL
    mod_cols = w_mod.shape[2]
    cw_cols = conv_w.shape[2]

    b_cols = lax.dynamic_slice(b_mod, (0, me * mod_cols), (1, mod_cols))
    cw_blk = jnp.pad(conv_w[0], ((0, 0), (0, mod_cols - cw_cols)))
    a_rows, gathered = _prologue(jnp.pad(c, ((0, 7), (0, 0))), c_ctx[None, :], w_mod[0], b_cols, cw_blk,
                                 "prologue")
    mod_mine = gathered[:, 0, :].reshape(1, 6 * d)
    mod_ctx = gathered[:, 1, :].reshape(1, 6 * d)
    cw_full = gathered[:, 2:5, :cw_cols].transpose(1, 0, 2).reshape(3, CONV_W)

    early = [w.astype(BF16) for w in (w_in[0].T, w_uq[0], w_ukv[0])]
    late = [w.astype(BF16) for w in (w_out[0], w_mlp1[0], w_mlp2[0])]
    h_all, (g_in, g_uq, g_ukv) = _modulate_all(x2d, ctx2d, mod_mine, mod_ctx, _RidingGather(early),
                                               "modulate1")
    win_head, win_conv, wq, wk = _unpack_small_weights(g_in, g_uq, g_ukv)
    wk_k, wk_v = wk[:, :N_HEADS * LANES], wk[:, N_HEADS * LANES:]
    cos, sgn = _rope_tables(s, l)

    tm_t = _pick(t, (1088, 768, 256))
    z_head, cq, kv_in, qf, kv = _head_fwd(h_all, win_head, wq, wk, q_norm_g, kv_norm_g, cos, sgn, tm_t, "head_fwd")
    z_conv = _matmul(h_all, win_conv, mode="nt", name="in_proj_conv", m=s, tm=1024, tn=1536, tk=1024)
    attn, a_cat, stats, (g_out, w1, g_w2) = _attn_fwd(qf, kv, s, _RidingGather(late), "attn_fwd")
    wo = g_out.reshape(d, d)
    w2 = g_w2.reshape(D_FF, d)
    a_cat = _conv_fwd(z_conv, cw_full, a_cat, "conv_fwd")
    (o, x1, h2), _ = _matmul_rows(a_cat, wo, _epi_resid_modulate, mode="nn", name="out_proj", tm=1024, tk=1024,
                                  rows=[x2d], vecs=[(mod_mine, 2), (mod_mine, 3), (mod_mine, 4)],
                                  out_dtypes=[F32, F32, BF16])
    u1, act = _matmul(h2, w1, mode="nn", name="mlp_up", tm=4096, tk=1024, epilogue="relu2", slots="b_cols")
    (dx2, dm, fsums), _ = _matmul_rows(act, w2, _epi_final, mode="nn", name="mlp_down", tm=512, tk=4096,
                                       rows=[x1, tgt], vecs=[(mod_mine, 5), (final_norm_g[None, :], 0)],
                                       out_dtypes=[F32, BF16], sums=True)

    d_w2 = _matmul(act, dm, mode="tn", name="d_w_mlp2", out_dtype=BF16, tm=1024, tn=1024, tk=4096)
    du1 = _matmul(dm, w2, mode="nt", name="d_act", out_dtype=BF16, tm=2048, tn=1024, tk=1024,
                  epilogue="drelu2", extra=(u1,))
    d_w1 = _matmul(h2, du1, mode="tn", name="d_w_mlp1", out_dtype=BF16, tm=1024, tk=4096, slots="out")
    (dx1, do, sums2), _ = _matmul_rows(du1, w1, _epi_modulate2_bwd, mode="nt", name="d_h2", tm=512, tk=4096,
                                       slots="b_contract", rows=[x1, dx2, o], vecs=[(mod_mine, 4), (mod_mine, 2)],
                                       out_dtypes=[F32, BF16], sums=True)
    da, d_wo = _out_proj_bwd(a_cat, do, wo, "out_proj_bwd")
    dz_conv, d_cw, d_conv = _conv_bwd(z_conv, cw_full, da, h_all, "conv_bwd")
    ready = [d_wo.reshape(N_DEV, d // N_DEV, d), d_w1, d_w2.reshape(N_DEV, D_FF // N_DEV, d)]
    dq, dk, dv, rode = _attn_bwd(qf, kv, attn, da, stats, cos, sgn, _Riding(ready), "attn_bwd")
    head_args = (z_head, wq, wk_k, wk_v, win_head, q_norm_g, kv_norm_g, cos, sgn, cq, kv_in, h_all)
    dz_head, dh_head, psums, d_wq, *carried = _head_bwd(dq, dk, dv, *head_args, "head_bwd", tile=HEAD_BWD_TILE,
                                                        first_block=0, n_blocks=s // HEAD_BWD_TILE)
    _, dh_head, psums_c, d_wkk, d_wkv, d_head = _head_bwd(
        None, dk, dv, *head_args, "head_bwd_ctx", tile=ROW_TILE, first_block=s // ROW_TILE, n_blocks=l // ROW_TILE,
        carry=(dz_head, dh_head, *carried))
    send = _pack_small_grads(d_head, d_conv, d_wq, d_wkk, d_wkv)
    (grad_x, sums1), got = _matmul_rows(dz_conv, win_conv, _epi_modulate1_bwd, mode="nn", name="d_h1", tm=max(s // 8, ROW_TILE),
                                        tk=win_conv.shape[0], rows=[dh_head, x2d, dx1], vecs=[(mod_mine, 1)],
                                        out_dtypes=[F32], sums=True, riding=_RidingReduce(send))
    sums1c = _modulate_sums(dh_head, s // ROW_TILE, ctx2d)

    small = _pack_small(sums1, sums2, fsums, sums1c, psums, psums_c, d_cw, mod_cols, "pack_small")
    (d_all,) = _all_gather([small], "gather_small_grads", True)
    d_ex = lax.dynamic_index_in_dim(d_all, me, axis=1, keepdims=False)
    d_ctx = lax.dynamic_index_in_dim(d_all, N_DEV + me, axis=1, keepdims=False)
    g_w_mod, dsil, dsum = _adaln_bwd(a_rows.T, w_mod[0], d_ex, d_ctx, d_all, "adaln_bwd")
    (dsil_all,) = _all_gather([dsil], "gather_d_cctx", True)
    loss = dsum[SMALL_LOSS, 0]
    g_cw = lax.dynamic_slice(dsum, (SMALL_CW, me * cw_cols), (3, cw_cols))

    slots = dict(zip(["w_in", "w_uq", "w_ukv"], got))
    slots.update(zip(["w_out", "w_mlp1", "w_mlp2"], rode))

    grads = {}
    weights = {"c_ctx": c_ctx, "w_mod": w_mod, "b_mod": b_mod, "w_in": w_in, "q_norm_g": q_norm_g, "w_uq": w_uq,
               "kv_norm_g": kv_norm_g, "w_ukv": w_ukv, "conv_w": conv_w, "w_out": w_out, "w_mlp1": w_mlp1,
               "w_mlp2": w_mlp2, "final_norm_g": final_norm_g}
    m_in = {"c_ctx": m_c_ctx, "w_mod": m_w_mod, "b_mod": m_b_mod, "w_in": m_w_in, "q_norm_g": m_q_norm_g,
            "w_uq": m_w_uq, "kv_norm_g": m_kv_norm_g, "w_ukv": m_w_ukv, "conv_w": m_conv_w, "w_out": m_w_out,
            "w_mlp1": m_w_mlp1, "w_mlp2": m_w_mlp2, "final_norm_g": m_final_norm_g}
    v_in = {"c_ctx": v_c_ctx, "w_mod": v_w_mod, "b_mod": v_b_mod, "w_in": v_w_in, "q_norm_g": v_q_norm_g,
            "w_uq": v_w_uq, "kv_norm_g": v_kv_norm_g, "w_ukv": v_w_ukv, "conv_w": v_conv_w, "w_out": v_w_out,
            "w_mlp1": v_w_mlp1, "w_mlp2": v_w_mlp2, "final_norm_g": v_final_norm_g}
    names = list(weights)
    small_names = ["c_ctx", "b_mod", "q_norm_g", "kv_norm_g", "final_norm_g", "conv_w"]
    delta, new_m, new_v = {}, {}, {}

    def as_rows(a):
        return a[None, :] if a.ndim == 1 else a

    small_out = _small_update(dsum, dsil_all, g_cw, [[as_rows(src[n]) for src in (weights, m_in, v_in)]
                                                      for n in small_names], "small_update")
    for n, outs in zip(small_names, small_out):
        grads[n], delta[n], new_m[n], new_v[n] = [a.reshape(weights[n].shape) for a in outs]
    for n in names:
        if n in small_names:
            continue
        if n == "w_in":
            wmv = [jnp.swapaxes(src[n], 1, 2) for src in (weights, m_in, v_in)]
            outs = _adamw(wmv[0], slots[n], wmv[1], wmv[2], "adamw_" + n, slots=True)
            grads[n], delta[n], new_m[n], new_v[n] = [jnp.swapaxes(a, 1, 2) for a in outs]
        elif n in slots:
            grads[n], delta[n], new_m[n], new_v[n] = _adamw(weights[n], slots[n], m_in[n], v_in[n], "adamw_" + n,
                                                            slots=True)
        else:
            delta[n], new_m[n], new_v[n] = _adamw(weights[n], g_w_mod, m_in[n], v_in[n], "adamw_" + n)
            grads[n] = g_w_mod[None]

    return (loss, grad_x[None], *[grads[n] for n in names], *[delta[n] for n in names],
            *[new_m[n] for n in names], *[new_v[n] for n in names])
```

```python
import math

import jax
import jax.numpy as jnp
import numpy as np
from jax import lax
from jax.experimental import pallas as pl
from jax.experimental.pallas import tpu as pltpu

F32 = jnp.float32
BF16 = jnp.bfloat16

D_MODEL = 1024
GRID_W = 64
N_HEADS = 8
QK_NOPE = 64
QK_ROPE = 32
V_DIM = 64
Q_RANK = 256
KV_RANK = 128
MLA_IN = Q_RANK + KV_RANK + QK_ROPE
CONV_W = 512
HEAD_COLS = 512
D_FF = 4096
ROPE_THETA = 10000.0
EPS = 1e-6
ATTN_SCALE = 1.0 / math.sqrt(QK_NOPE + QK_ROPE)
LOG2_E = 1.0 / math.log(2.0)
EXP2_SCALE = ATTN_SCALE * LOG2_E
N_DEV = 8
LANES = 128

ADAM_LR, ADAM_B1, ADAM_B2, ADAM_EPS, ADAM_WD, ADAM_STEP = 0.001, 0.9, 0.999, 1e-08, 0.01, 10

ROW_TILE = 256
HEAD_BWD_TILE = 1024
VMEM_BIG = 60 * 1024 * 1024


def _params(sem=None, vmem=None):
    return pltpu.CompilerParams(dimension_semantics=sem, vmem_limit_bytes=vmem)


def _pick(n, prefs):
    for p in prefs:
        if n % p == 0:
            return p
    return n


def _my_index():
    return 4 * lax.axis_index("x") + 2 * lax.axis_index("y") + lax.axis_index("c")


def _two_level_gather(x_refs, out_refs, send_sems, recv_sems, local_sems):
    n = len(x_refs)
    x, y, c = lax.axis_index("x"), lax.axis_index("y"), lax.axis_index("c")
    me, sibling = (x, y, c), (x, y, 1 - c)
    chips = [(1 - x, y), (x, 1 - y), (1 - x, 1 - y)]

    def slot(a, px, py, pc):
        return out_refs[a].at[4 * px + 2 * py + pc]

    def copy(a, k, block, to, src=None):
        return pltpu.make_async_remote_copy(
            src_ref=slot(a, *block) if src is None else src, dst_ref=slot(a, *block),
            send_sem=send_sems.at[7 * a + k], recv_sem=recv_sems.at[7 * a + k],
            device_id=to, device_id_type=pl.DeviceIdType.MESH)

    mine = [pltpu.make_async_copy(x_refs[a], slot(a, *me), local_sems.at[a]) for a in range(n)]
    first = [cp for a in range(n) for cp in
             [copy(a, 0, me, sibling, src=x_refs[a])]
             + [copy(a, 1 + j, me, (*chip, c), src=x_refs[a]) for j, chip in enumerate(chips)]]
    passed = [[copy(a, 4 + j, (*chip, c), sibling) for j, chip in enumerate(chips)] for a in range(n)]

    def start():
        for cp in mine + first:
            cp.start()

    def forward():
        for a in range(n):
            for j, chip in enumerate(chips):
                copy(a, 1 + j, (*chip, c), me).wait_recv()
                passed[a][j].start()

    def finish():
        for a in range(n):
            copy(a, 0, sibling, me).wait_recv()
            for j, chip in enumerate(chips):
                copy(a, 4 + j, (*chip, 1 - c), me).wait_recv()
        for cp in first + [cp for per_array in passed for cp in per_array]:
            cp.wait_send()
        for cp in mine:
            cp.wait()

    return start, forward, finish


def _direct_gather(src_ref, dst_ref, send_sems, recv_sems, per_peer=False):
    x, y, c = lax.axis_index("x"), lax.axis_index("y"), lax.axis_index("c")
    me = 4 * x + 2 * y + c
    dst_ref[me] = src_ref[me] if per_peer else src_ref[...]
    sends, landings = [], []
    for k in range(1, N_DEV):
        peer = (1 - x if k & 4 else x, 1 - y if k & 2 else y, 1 - c if k & 1 else c)
        pid = 4 * peer[0] + 2 * peer[1] + peer[2]
        for dst, out in ((me, sends), (pid, landings)):
            out.append(pltpu.make_async_remote_copy(
                src_ref=src_ref.at[pid] if per_peer else src_ref, dst_ref=dst_ref.at[dst],
                send_sem=send_sems.at[k - 1], recv_sem=recv_sems.at[k - 1],
                device_id=peer, device_id_type=pl.DeviceIdType.MESH))
    for cp in sends:
        cp.start()

    def finish():
        for cp in landings:
            cp.wait_recv()
        for cp in sends:
            cp.wait_send()

    return finish


def _all_gather(arrays, name, in_vmem):
    space = pltpu.VMEM if in_vmem else pl.ANY
    n = len(arrays)

    def body(*refs):
        for phase in _two_level_gather(refs[:n], refs[n:2 * n], *refs[2 * n:]):
            phase()

    outs = pl.pallas_call(
        body, name=name,
        out_shape=tuple(jax.ShapeDtypeStruct((N_DEV,) + a.shape, a.dtype) for a in arrays),
        in_specs=[pl.BlockSpec(memory_space=space)] * n,
        out_specs=tuple(pl.BlockSpec(memory_space=space) for _ in arrays),
        scratch_shapes=[pltpu.SemaphoreType.DMA((7 * n,)), pltpu.SemaphoreType.DMA((7 * n,)),
                        pltpu.SemaphoreType.DMA((n,))],
    )(*arrays)
    return list(outs)


class _Riding:
    def __init__(self, arrays=()):
        self.arrays, self.n = list(arrays), len(arrays)
        self.out_shape = [jax.ShapeDtypeStruct(a.shape, a.dtype) for a in self.arrays]
        self.specs = [pl.BlockSpec(memory_space=pl.ANY)] * self.n
        self.scratch = [pltpu.SemaphoreType.DMA((7 * self.n,)), pltpu.SemaphoreType.DMA((7 * self.n,)),
                        pltpu.SemaphoreType.DMA((self.n,))]

    def copies(self, x_refs, y_refs, send_sems, recv_sems, local_sems):
        x, y, c = lax.axis_index("x"), lax.axis_index("y"), lax.axis_index("c")
        me = 4 * x + 2 * y + c
        local, sends, landings = [], [], []
        for a in range(self.n):
            local.append(pltpu.make_async_copy(x_refs[a].at[me], y_refs[a].at[me], local_sems.at[a]))
            for k in range(1, N_DEV):
                peer = (1 - x if k & 4 else x, 1 - y if k & 2 else y, 1 - c if k & 1 else c)
                pid = 4 * peer[0] + 2 * peer[1] + peer[2]
                for dst, out in ((me, sends), (pid, landings)):
                    out.append(pltpu.make_async_remote_copy(
                        src_ref=x_refs[a].at[pid], dst_ref=y_refs[a].at[dst],
                        send_sem=send_sems.at[7 * a + k - 1], recv_sem=recv_sems.at[7 * a + k - 1],
                        device_id=peer, device_id_type=pl.DeviceIdType.MESH))
        return local, sends, landings

    def run(self, first, last, x_refs, y_refs, sems, middle=None):
        if self.n == 0:
            return None
        local, sends, landings = self.copies(x_refs, y_refs, *sems)

        @pl.when(first)
        def _():
            for cp in local + sends:
                cp.start()

        return local, sends, landings, last

    @staticmethod
    def finish(state):
        if state is None:
            return
        local, sends, landings, last = state

        @pl.when(last)
        def _():
            for cp in landings:
                cp.wait_recv()
            for cp in sends:
                cp.wait_send()
            for cp in local:
                cp.wait()


class _RidingGather:
    def __init__(self, arrays):
        self.arrays, self.n = list(arrays), len(arrays)
        self.out_shape = [jax.ShapeDtypeStruct((N_DEV,) + a.shape, a.dtype) for a in self.arrays]
        self.specs = [pl.BlockSpec(memory_space=pl.ANY)] * self.n
        self.scratch = [pltpu.SemaphoreType.DMA((7 * self.n,)), pltpu.SemaphoreType.DMA((7 * self.n,)),
                        pltpu.SemaphoreType.DMA((self.n,))]

    def run(self, first, last, x_refs, y_refs, sems, middle):
        start, forward, finish = _two_level_gather(x_refs, y_refs, *sems)
        pl.when(first)(start)
        pl.when(middle)(forward)
        return finish, last

    @staticmethod
    def finish(state):
        finish, last = state
        pl.when(last)(finish)


class _RidingReduce:
    def __init__(self, arrays):
        self.arrays, self.n = list(arrays), len(arrays)
        self.out_shape = [jax.ShapeDtypeStruct((4,) + a.shape[1:], a.dtype) for a in self.arrays]
        self.specs = [pl.BlockSpec(memory_space=pl.ANY)] * self.n
        self.scratch = [pltpu.VMEM((4,) + a.shape[1:], a.dtype) for a in self.arrays for _ in range(3)]
        self.scratch += [pltpu.SemaphoreType.DMA((self.n,)) for _ in range(6)]

    def run(self, first, last, x_refs, y_refs, scratch, middle):
        n = self.n
        own, sib, tot = scratch[0:3 * n:3], scratch[1:3 * n:3], scratch[2:3 * n:3]
        d2d_send, d2d_recv, local_in, ici_send, ici_recv, local_out = scratch[3 * n:]
        x, y, c = lax.axis_index("x"), lax.axis_index("y"), lax.axis_index("c")
        my_chip = 2 * x + y
        sibling = (x, y, 1 - c)
        others = [(1 - x, y), (x, 1 - y), (1 - x, 1 - y)]

        def to_sibling(a, j=None):
            src = x_refs[a].at[pl.ds(0, 4)] if j is None else x_refs[a].at[2 * j + 1 - c]
            dst = sib[a] if j is None else sib[a].at[j]
            return pltpu.make_async_remote_copy(src_ref=src, dst_ref=dst, send_sem=d2d_send.at[a],
                                                recv_sem=d2d_recv.at[a], device_id=sibling,
                                                device_id_type=pl.DeviceIdType.MESH)

        def mine_in(a, j=None):
            src = x_refs[a].at[pl.ds(0, 4)] if j is None else x_refs[a].at[2 * j + c]
            return pltpu.make_async_copy(src, own[a] if j is None else own[a].at[j], local_in.at[a])

        def to_chip(a, chip=None):
            if chip is None:
                src, dst, peer = tot[a].at[pl.ds(0, 3)], y_refs[a].at[pl.ds(0, 3)], sibling
            else:
                src, dst, peer = tot[a].at[2 * chip[0] + chip[1]], y_refs[a].at[my_chip], (*chip, c)
            return pltpu.make_async_remote_copy(src_ref=src, dst_ref=dst, send_sem=ici_send.at[a],
                                                recv_sem=ici_recv.at[a], device_id=peer,
                                                device_id_type=pl.DeviceIdType.MESH)

        def mine_out(a):
            return pltpu.make_async_copy(tot[a].at[my_chip], y_refs[a].at[my_chip], local_out.at[a])

        @pl.when(first)
        def _():
            for a in range(n):
                for j in range(4):
                    to_sibling(a, j).start()
                    mine_in(a, j).start()

        @pl.when(middle)
        def _():
            for a in range(n):
                to_sibling(a).wait_recv()
                to_sibling(a).wait_send()
                mine_in(a).wait()
                tot[a][...] = (own[a][...].astype(F32) + sib[a][...].astype(F32)).astype(tot[a].dtype)
                for chip in others:
                    to_chip(a, chip).start()
                mine_out(a).start()

        def finish():
            @pl.when(last)
            def _():
                for a in range(n):
                    to_chip(a).wait_recv()
                    to_chip(a).wait_send()
                    mine_out(a).wait()

        return finish

    @staticmethod
    def finish(state):
        state()


_DIMS ={"nn": (((1,), (0,)), ((), ())), "nt": (((1,), (1,)), ((), ())), "tn": (((0,), (0,)), ((), ()))}
NT_DIMS = _DIMS["nt"]
TN_DIMS = _DIMS["tn"]


def _swap8(x):
    lane = lax.broadcasted_iota(jnp.int32, x.shape, 1)
    return jnp.where((lane & 15) < 8, pltpu.roll(x, LANES - 8, 1), pltpu.roll(x, 8, 1))


def _rope(x, cos, sgn, bwd):
    return x * cos + (_swap8(x * sgn) if bwd else _swap8(x) * sgn)


def _matmul(a, b, *, mode, name, out_dtype=F32, tm=512, tn=512, tk=512, m=None, k=None,
            epilogue=None, extra=(), slots=None):
    if mode == "nn":
        m = a.shape[0] if m is None else m
        k = a.shape[1]
        n = N_DEV * b.shape[2] if slots == "b_cols" else b.shape[1]
    elif mode == "nt":
        m = a.shape[0] if m is None else m
        k = a.shape[1]
        n = b.shape[0]
    else:
        k = a.shape[0] if k is None else k
        m, n = a.shape[1], b.shape[1]
    tm, tn, tk = min(tm, m), min(tn, n), min(tk, k)
    if slots == "b_cols":
        tn = b.shape[2]
    if slots == "out":
        tn = n // N_DEV
    assert m % tm == 0 and n % tn == 0 and k % tk == 0, (name, m, n, k, tm, tn, tk)
    nk = k // tk
    dims = _DIMS[mode]
    a_spec = (pl.BlockSpec((tk, tm), lambda i, j, kk: (kk, i)) if mode == "tn"
              else pl.BlockSpec((tm, tk), lambda i, j, kk: (i, kk)))
    if slots == "b_cols":
        b_spec = pl.BlockSpec((None, tk, tn), lambda i, j, kk: (j, kk, 0))
    elif mode == "nt":
        b_spec = pl.BlockSpec((tn, tk), lambda i, j, kk: (j, kk))
    else:
        b_spec = pl.BlockSpec((tk, tn), lambda i, j, kk: (kk, j))
    tile = pl.BlockSpec((tm, tn), lambda i, j, kk: (i, j))
    if slots == "out":
        o_spec = pl.BlockSpec((None, tm, tn), lambda i, j, kk: (j, i, 0))
        o_shape = (N_DEV, m, tn)
    else:
        o_spec, o_shape = tile, (m, n)
    in_specs, args = [a_spec, b_spec], [a, b]
    if epilogue == "drelu2":
        in_specs.append(tile)
    args += list(extra)
    if epilogue == "relu2":
        out_shape = (jax.ShapeDtypeStruct(o_shape, BF16), jax.ShapeDtypeStruct(o_shape, BF16))
        out_specs = (o_spec, o_spec)
    else:
        out_shape = jax.ShapeDtypeStruct(o_shape, out_dtype)
        out_specs = o_spec
    n_in = len(args)
    n_out = 2 if epilogue == "relu2" else 1

    def body(*refs):
        a_ref, b_ref = refs[0], refs[1]
        outs = refs[n_in:n_in + n_out]
        part = lax.dot_general(a_ref[...], b_ref[...], dims, preferred_element_type=F32)

        def finish(acc):
            if epilogue == "relu2":
                outs[0][...] = acc.astype(BF16)
                r = jnp.maximum(acc, 0.0)
                outs[1][...] = (r * r).astype(BF16)
            elif epilogue == "drelu2":
                u = refs[2][...].astype(F32)
                outs[0][...] = (acc * (2.0 * jnp.maximum(u, 0.0))).astype(out_dtype)
            else:
                outs[0][...] = acc.astype(out_dtype)

        if nk == 1:
            finish(part)
        else:
            acc_ref = refs[n_in + n_out]
            kk = pl.program_id(2)

            @pl.when(kk == 0)
            def _():
                acc_ref[...] = part

            @pl.when(kk > 0)
            def _():
                acc_ref[...] += part

            @pl.when(kk == nk - 1)
            def _():
                finish(acc_ref[...])

    return pl.pallas_call(
        body, name=name, grid=(m // tm, n // tn, nk),
        out_shape=out_shape, in_specs=in_specs, out_specs=out_specs,
        scratch_shapes=[pltpu.VMEM((tm, tn), F32)] if nk > 1 else [],
        compiler_params=_params(("parallel", "parallel", "arbitrary"), VMEM_BIG),
    )(*args)


def _rstd(x):
    return lax.rsqrt(jnp.mean(x * x, axis=1, keepdims=True) + EPS)


def _norm_bwd(dxn, xn, r):
    return r * (dxn - xn * jnp.mean(dxn * xn, axis=1, keepdims=True))


def _vec(col):
    return pl.BlockSpec((1, D_MODEL), lambda i: (0, col))


def _matmul_rows(a, b, epi, *, mode, name, tm, tk, rows=(), vecs=(), out_dtypes=(), sums=False, slots=None,
                 riding=None):
    m, k = a.shape
    n = D_MODEL
    tm, tk = min(tm, m), min(tk, k)
    riding = riding or _Riding()
    group = 1
    if slots == "b_contract":
        group = max(1, tk // b.shape[2])
        tk = group * b.shape[2]
        b_spec = pl.BlockSpec((group, n, tk // group), lambda i, kk: (kk, 0, 0))
    elif mode == "nt":
        b_spec = pl.BlockSpec((n, tk), lambda i, kk: (0, kk))
    else:
        b_spec = pl.BlockSpec((tk, n), lambda i, kk: (kk, 0))
    assert m % tm == 0 and k % tk == 0, (name, m, k, tm, tk)
    ni, nk = m // tm, k // tk
    assert ni >= 2 or not isinstance(riding, _RidingReduce), "the two-level exchange needs a middle grid step"
    dims = _DIMS[mode]
    tile = pl.BlockSpec((tm, n), lambda i, kk: (i, 0))
    in_specs = [pl.BlockSpec((tm, tk), lambda i, kk: (i, kk)), b_spec] + [tile] * len(rows)
    in_specs += [pl.BlockSpec((1, n), lambda i, kk, col=col: (0, col)) for _, col in vecs]
    args = [a, b, *rows, *[v for v, _ in vecs]]
    out_shape = [jax.ShapeDtypeStruct((m, n), dt) for dt in out_dtypes]
    out_specs = [tile] * len(out_dtypes)
    if sums:
        out_shape.append(jax.ShapeDtypeStruct((8, n), F32))
        out_specs.append(pl.BlockSpec((8, n), lambda i, kk: (0, 0)))
    n_rows, n_vecs, n_outs, nr = len(rows), len(vecs), len(out_dtypes), riding.n
    n_in = 2 + n_rows + n_vecs

    def body(*refs):
        a_ref, b_ref = refs[0], refs[1]
        row_refs = refs[2:2 + n_rows]
        vec_refs = refs[2 + n_rows:n_in]
        x_refs = refs[n_in:n_in + nr]
        out_refs = refs[n_in + nr:n_in + nr + n_outs]
        pos = n_in + nr + n_outs
        sums_ref = refs[pos] if sums else None
        pos += 1 if sums else 0
        y_refs = refs[pos:pos + nr]
        pos += nr
        acc_ref = refs[pos] if nk > 1 else None
        sem_refs = refs[pos + (1 if nk > 1 else 0):]
        i, kk = pl.program_id(0), pl.program_id(1)
        state = riding.run((i == 0) & (kk == 0), (i == ni - 1) & (kk == nk - 1), x_refs, y_refs, sem_refs,
                           middle=(i == 1) & (kk == 0))
        if slots == "b_contract":
            c = tk // group
            part = lax.dot_general(a_ref[:, 0:c], b_ref[0], dims, preferred_element_type=F32)
            for u in range(1, group):
                part = part + lax.dot_general(a_ref[:, u * c:(u + 1) * c], b_ref[u], dims, preferred_element_type=F32)
        else:
            part = lax.dot_general(a_ref[...], b_ref[...], dims, preferred_element_type=F32)

        def finish(acc):
            nsub = tm // ROW_TILE
            for r in range(nsub):
                blk = pl.ds(r * ROW_TILE, ROW_TILE)
                epi(acc[r * ROW_TILE:(r + 1) * ROW_TILE], [ref.at[blk] for ref in row_refs], vec_refs,
                    [ref.at[blk] for ref in out_refs], sums_ref,
                    (i == 0) if r == 0 else None, (i == ni - 1) if r == nsub - 1 else None)

        if nk == 1:
            finish(part)
        else:
            @pl.when(kk == 0)
            def _():
                acc_ref[...] = part

            @pl.when(kk > 0)
            def _():
                acc_ref[...] += part

            @pl.when(kk == nk - 1)
            def _():
                finish(acc_ref)

        riding.finish(state)

    outs = pl.pallas_call(
        body, name=name, grid=(ni, nk),
        out_shape=(*out_shape, *riding.out_shape),
        in_specs=[*in_specs, *riding.specs], out_specs=(*out_specs, *riding.specs),
        scratch_shapes=([pltpu.VMEM((tm, n), F32)] if nk > 1 else []) + (riding.scratch if nr else []),
        compiler_params=_params(("arbitrary", "arbitrary"), VMEM_BIG),
    )(*args, *riding.arrays)
    n_own = len(out_shape)
    return list(outs[:n_own]), list(outs[n_own:])


def _zero_sums_at_start(sums_ref, first):
    if first is not None:
        @pl.when(first)
        def _():
            sums_ref[...] = jnp.zeros_like(sums_ref)


def _epi_resid_modulate(acc, rows, vecs, outs, sums_ref, first, last):
    (x_ref,), (g_ref, sh_ref, sc_ref) = rows, vecs
    x1 = x_ref[...] + g_ref[...] * acc
    outs[0][...] = acc
    outs[1][...] = x1
    outs[2][...] = (x1 * _rstd(x1) * (1.0 + sc_ref[...]) + sh_ref[...]).astype(BF16)


def _epi_final(acc, rows, vecs, outs, sums_ref, first, last):
    (x1_ref, t_ref), (g_ref, gf_ref) = rows, vecs
    d = acc.shape[1]
    x2 = x1_ref[...] + g_ref[...] * acc
    r = _rstd(x2)
    xn = x2 * r
    err = xn * gf_ref[...] - t_ref[...]
    dy = err * (1.0 / d)
    dx2 = _norm_bwd(dy * gf_ref[...], xn, r)
    outs[0][...] = dx2
    outs[1][...] = (dx2 * g_ref[...]).astype(BF16)
    _zero_sums_at_start(sums_ref, first)
    sums_ref[0:1, :] += jnp.sum(dy * xn, axis=0, keepdims=True)
    sums_ref[1:2, :] += jnp.sum(dx2 * acc, axis=0, keepdims=True)
    sums_ref[2:3, :] += jnp.sum(err * err, axis=0, keepdims=True)

    if last is not None:
        @pl.when(last)
        def _():
            tot = jnp.sum(sums_ref[2:3, :], axis=1, keepdims=True) * (0.5 / d)
            sums_ref[3:4, :] = jnp.broadcast_to(tot, (1, d))


def _epi_modulate2_bwd(acc, rows, vecs, outs, sums_ref, first, last):
    (x_ref, dres_ref, o_ref), (sc_ref, g_ref) = rows, vecs
    x = x_ref[...]
    r = _rstd(x)
    xn = x * r
    dx = dres_ref[...] + _norm_bwd(acc * (1.0 + sc_ref[...]), xn, r)
    outs[0][...] = dx
    outs[1][...] = (dx * g_ref[...]).astype(BF16)
    _zero_sums_at_start(sums_ref, first)
    sums_ref[0:1, :] += jnp.sum(acc * xn, axis=0, keepdims=True)
    sums_ref[1:2, :] += jnp.sum(acc, axis=0, keepdims=True)
    sums_ref[2:3, :] += jnp.sum(dx * o_ref[...], axis=0, keepdims=True)


def _epi_modulate1_bwd(acc, rows, vecs, outs, sums_ref, first, last):
    (add_ref, x_ref, dres_ref), (sc_ref,) = rows, vecs
    dh = acc + add_ref[...]
    x = x_ref[...]
    r = _rstd(x)
    xn = x * r
    outs[0][...] = dres_ref[...] + _norm_bwd(dh * (1.0 + sc_ref[...]), xn, r)
    _zero_sums_at_start(sums_ref, first)
    sums_ref[0:1, :] += jnp.sum(dh * xn, axis=0, keepdims=True)
    sums_ref[1:2, :] += jnp.sum(dh, axis=0, keepdims=True)


def _modulate_all(x, ctx, mod, mod_ctx, riding, name):
    s, d = x.shape
    t = s + ctx.shape[0]
    ns = s // ROW_TILE
    nc = ctx.shape[0] // ROW_TILE
    nr = riding.n

    def body(*refs):
        x_ref, c_ref, sh_ref, sc_ref, shc_ref, scc_ref = refs[:6]
        h_ref = refs[6 + nr]
        i = pl.program_id(0)
        state = riding.run(i == 0, i == ns + nc - 1, refs[6:6 + nr], refs[7 + nr:7 + 2 * nr], refs[7 + 2 * nr:],
                           middle=i == ns + nc - 3)

        @pl.when(i < ns)
        def _():
            v = x_ref[...]
            h_ref[...] = (v * _rstd(v) * (1.0 + sc_ref[...]) + sh_ref[...]).astype(BF16)

        @pl.when(i >= ns)
        def _():
            v = c_ref[...]
            h_ref[...] = (v * _rstd(v) * (1.0 + scc_ref[...]) + shc_ref[...]).astype(BF16)

        riding.finish(state)

    outs = pl.pallas_call(
        body, name=name, grid=(ns + nc,),
        out_shape=(jax.ShapeDtypeStruct((t, d), BF16), *riding.out_shape),
        in_specs=[pl.BlockSpec((ROW_TILE, d), lambda i: (jnp.minimum(i, ns - 1), 0)),
                  pl.BlockSpec((ROW_TILE, d), lambda i: (jnp.maximum(i - ns, 0), 0)),
                  _vec(0), _vec(1), _vec(0), _vec(1), *riding.specs],
        out_specs=(pl.BlockSpec((ROW_TILE, d), lambda i: (i, 0)), *riding.specs),
        scratch_shapes=riding.scratch,
        compiler_params=_params(("arbitrary",)),
    )(x, ctx, mod, mod, mod_ctx, mod_ctx, *riding.arrays)
    return outs[0], list(outs[1:])


def _modulate_sums(dh, row_off, xsrc):
    s, d = xsrc.shape

    def body(dh_ref, x_ref, sums_ref):
        i = pl.program_id(0)
        x = x_ref[...]
        dhv = dh_ref[...]

        @pl.when(i == 0)
        def _():
            sums_ref[...] = jnp.zeros_like(sums_ref)

        sums_ref[0:1, :] += jnp.sum(dhv * (x * _rstd(x)), axis=0, keepdims=True)
        sums_ref[1:2, :] += jnp.sum(dhv, axis=0, keepdims=True)

    return pl.pallas_call(
        body, name="modulate1_ctx_bwd", grid=(s // ROW_TILE,),
        out_shape=jax.ShapeDtypeStruct((8, d), F32),
        in_specs=[pl.BlockSpec((ROW_TILE, d), lambda i: (i + row_off, 0)), pl.BlockSpec((ROW_TILE, d), lambda i: (i, 0))],
        out_specs=pl.BlockSpec((8, d), lambda i: (0, 0)),
        compiler_params=_params(("arbitrary",)),
    )(dh, xsrc)


def _head_fwd(h_all, win_head, wq, wk, q_gain, kv_gain, cos, sgn, tm, name):
    t, d = h_all.shape
    nq, nkv = wq.shape[1], wk.shape[1]

    def body(h_ref, wi_ref, wq_ref, wk_ref, qg_ref, kg_ref, c_ref, s_ref, z_ref, cq_ref, kvin_ref, qf_ref, kv_ref):
        z = lax.dot_general(h_ref[...], wi_ref[...], NT_DIMS, preferred_element_type=F32)
        z_ref[...] = z
        cos, sgn = c_ref[...], s_ref[...]
        zq = z[:, 0:Q_RANK]
        cq = (zq * _rstd(zq) * qg_ref[...]).astype(BF16)
        cq_ref[...] = cq
        zk = z[:, Q_RANK:Q_RANK + KV_RANK]
        kv_in = jnp.concatenate([(zk * _rstd(zk) * kg_ref[...]).astype(BF16),
                                 _rope(z[:, Q_RANK + KV_RANK:HEAD_COLS], cos, sgn, False).astype(BF16)], axis=1)
        kvin_ref[...] = kv_in
        q = jnp.dot(cq, wq_ref[...], preferred_element_type=F32)
        for h in range(nq // LANES):
            sl = slice(h * LANES, (h + 1) * LANES)
            qf_ref[:, sl] = _rope(q[:, sl], cos, sgn, False).astype(BF16)
        kv_ref[...] = jnp.dot(kv_in, wk_ref[...], preferred_element_type=F32).astype(BF16)

    def row(w):
        return pl.BlockSpec((tm, w), lambda i: (i, 0))

    def whole(a):
        return pl.BlockSpec(a.shape, lambda i: (0, 0))

    return pl.pallas_call(
        body, name=name, grid=(t // tm,),
        out_shape=(jax.ShapeDtypeStruct((t, HEAD_COLS), F32), jax.ShapeDtypeStruct((t, Q_RANK), BF16),
                   jax.ShapeDtypeStruct((t, KV_RANK + LANES), BF16), jax.ShapeDtypeStruct((t, nq), BF16),
                   jax.ShapeDtypeStruct((t, nkv), BF16)),
        in_specs=[row(d), whole(win_head), whole(wq), whole(wk), whole(q_gain), whole(kv_gain), row(LANES), row(LANES)],
        out_specs=(row(HEAD_COLS), row(Q_RANK), row(KV_RANK + LANES), row(nq), row(nkv)),
        compiler_params=_params(("parallel",), VMEM_BIG),
    )(h_all, win_head, wq, wk, q_gain, kv_gain, cos, sgn)


def _head_bwd(dq, dk, dv, z, wq, wk_k, wk_v, win_head, q_gain, kv_gain, cos, sgn, cq, kv_in, h_all, name, *, tile,
              first_block, n_blocks, carry=None):
    t = z.shape[0]
    with_q = dq is not None

    def body(*refs):
        it = iter(refs)
        dq_ref = next(it) if with_q else None
        dk_ref, dv_ref, z_ref, wq_ref, wkk_ref, wkv_ref, wi_ref, qg_ref, kg_ref, c_ref, s_ref = (next(it) for _ in range(11))
        cq_ref = next(it) if with_q else None
        kvin_ref, h_ref = next(it), next(it)
        before = None
        if carry is not None:
            next(it), next(it)
            before = (next(it), next(it), next(it))
        dz_ref, dh_ref, sums_ref = next(it), next(it), next(it)
        gq_ref = next(it) if with_q else None
        grads = (next(it), next(it), next(it))
        i = pl.program_id(0)

        @pl.when(i == 0)
        def _():
            sums_ref[...] = jnp.zeros_like(sums_ref)
            if with_q:
                gq_ref[...] = jnp.zeros_like(gq_ref)
            for k, g_ref in enumerate(grads):
                g_ref[...] = jnp.zeros_like(g_ref) if before is None else before[k][...]

        if with_q:
            gq_ref[...] += lax.dot_general(cq_ref[...], dq_ref[...], TN_DIMS, preferred_element_type=F32)
        grads[0][...] += lax.dot_general(kvin_ref[...], dk_ref[...], TN_DIMS, preferred_element_type=F32)
        grads[1][...] += lax.dot_general(kvin_ref[...], dv_ref[...], TN_DIMS, preferred_element_type=F32)
        if with_q:
            dc = lax.dot_general(dq_ref[...], wq_ref[...], NT_DIMS, preferred_element_type=F32)
            zq = z_ref[:, 0:Q_RANK]
            r = _rstd(zq)
            zn = zq * r
            sums_ref[0:1, :] += jnp.sum(dc * zn, axis=0, keepdims=True)
            dz_ref[:, 0:Q_RANK] = _norm_bwd(dc * qg_ref[...], zn, r).astype(BF16)
        else:
            dz_ref[:, 0:Q_RANK] = jnp.zeros((tile, Q_RANK), BF16)
        dkv = (lax.dot_general(dk_ref[...], wkk_ref[...], NT_DIMS, preferred_element_type=F32)
               + lax.dot_general(dv_ref[...], wkv_ref[...], NT_DIMS, preferred_element_type=F32))
        zk = z_ref[:, Q_RANK:Q_RANK + KV_RANK]
        r = _rstd(zk)
        zn = zk * r
        dc = dkv[:, 0:KV_RANK]
        sums_ref[1:2, 0:KV_RANK] += jnp.sum(dc * zn, axis=0, keepdims=True)
        dz_ref[:, Q_RANK:Q_RANK + KV_RANK] = _norm_bwd(dc * kg_ref[...], zn, r).astype(BF16)
        dz_ref[:, Q_RANK + KV_RANK:HEAD_COLS] = _rope(dkv[:, KV_RANK:KV_RANK + LANES], c_ref[...], s_ref[...],
                                                       True).astype(BF16)
        dh_ref[...] = jnp.dot(dz_ref[...], wi_ref[...], preferred_element_type=F32)
        grads[2][...] += lax.dot_general(dz_ref[...], h_ref[...], TN_DIMS, preferred_element_type=F32)

    def row(w):
        return pl.BlockSpec((tile, w), lambda i: (i + first_block, 0))

    def whole(a):
        return pl.BlockSpec(a.shape, lambda i: (0, 0))

    args = ([dq] if with_q else []) + [dk, dv, z, wq, wk_k, wk_v, win_head, q_gain, kv_gain, cos, sgn]
    args += ([cq] if with_q else []) + [kv_in, h_all]
    in_specs = ([row(dq.shape[1])] if with_q else []) + [
        row(dk.shape[1]), row(dv.shape[1]), row(HEAD_COLS), whole(wq), whole(wk_k), whole(wk_v),
        whole(win_head), whole(q_gain), whole(kv_gain), row(LANES), row(LANES)]
    in_specs += ([row(Q_RANK)] if with_q else []) + [row(kv_in.shape[1]), row(D_MODEL)]
    aliases = {}
    if carry is not None:
        aliases = {len(args): 0, len(args) + 1: 1}
        args += list(carry)
        in_specs += [pl.BlockSpec(memory_space=pl.ANY)] * 2 + [whole(a) for a in carry[2:]]
    grad_shapes = ([(Q_RANK, dq.shape[1])] if with_q else []) + [
        (kv_in.shape[1], dk.shape[1]), (kv_in.shape[1], dv.shape[1]), (HEAD_COLS, D_MODEL)]
    return pl.pallas_call(
        body, name=name, grid=(n_blocks,),
        out_shape=(jax.ShapeDtypeStruct((t, HEAD_COLS), BF16), jax.ShapeDtypeStruct((t, D_MODEL), F32),
                   jax.ShapeDtypeStruct((8, Q_RANK), F32), *[jax.ShapeDtypeStruct(g, F32) for g in grad_shapes]),
        in_specs=in_specs,
        out_specs=(row(HEAD_COLS), row(D_MODEL), pl.BlockSpec((8, Q_RANK), lambda i: (0, 0)),
                   *[pl.BlockSpec(g, lambda i: (0, 0)) for g in grad_shapes]),
        input_output_aliases=aliases,
        compiler_params=_params(("arbitrary",), VMEM_BIG),
    )(*args)


def _out_proj_bwd(a, dy, w, name, tm=1024):
    s, k = a.shape
    n = dy.shape[1]

    def body(a_ref, dy_ref, w_ref, da_ref, dw_ref, acc_ref):
        i = pl.program_id(0)
        da_ref[...] = lax.dot_general(dy_ref[...], w_ref[...], NT_DIMS, preferred_element_type=F32)
        part = lax.dot_general(a_ref[...], dy_ref[...], TN_DIMS, preferred_element_type=F32)

        @pl.when(i == 0)
        def _():
            acc_ref[...] = part

        @pl.when(i > 0)
        def _():
            acc_ref[...] += part

        @pl.when(i == pl.num_programs(0) - 1)
        def _():
            dw_ref[...] = acc_ref[...].astype(BF16)

    return pl.pallas_call(
        body, name=name, grid=(s // tm,),
        out_shape=(jax.ShapeDtypeStruct((s, k), F32), jax.ShapeDtypeStruct((k, n), BF16)),
        in_specs=[pl.BlockSpec((tm, k), lambda i: (i, 0)), pl.BlockSpec((tm, n), lambda i: (i, 0)),
                  pl.BlockSpec(w.shape, lambda i: (0, 0))],
        out_specs=(pl.BlockSpec((tm, k), lambda i: (i, 0)), pl.BlockSpec((k, n), lambda i: (0, 0))),
        scratch_shapes=[pltpu.VMEM((k, n), F32)],
        compiler_params=_params(("arbitrary",), VMEM_BIG),
    )(a, dy, w)


def _shift_rows(u, s):
    rowi = lax.broadcasted_iota(jnp.int32, u.shape, 0)
    prev = jnp.where(rowi == 0, 0.0, pltpu.roll(u, 1, 0))
    nxt = jnp.where(rowi == s - 1, 0.0, pltpu.roll(u, s - 1, 0))
    return prev, nxt


def _conv_fwd(z_conv, cw, a_cat, name):
    s = z_conv.shape[0]

    def body(z_ref, w_ref, a_in_ref, o_ref):
        del a_in_ref
        gb, gc, xv = z_ref[:, 0:LANES], z_ref[:, LANES:2 * LANES], z_ref[:, 2 * LANES:3 * LANES]
        u = gc * xv
        prev, nxt = _shift_rows(u, s)
        y = w_ref[0:1, :] * prev + w_ref[1:2, :] * u + w_ref[2:3, :] * nxt
        o_ref[...] = (gb * y).astype(BF16)

    return pl.pallas_call(
        body, name=name, grid=(CONV_W // LANES,),
        out_shape=jax.ShapeDtypeStruct(a_cat.shape, a_cat.dtype),
        in_specs=[pl.BlockSpec((s, 3 * LANES), lambda j: (0, j)), pl.BlockSpec((3, LANES), lambda j: (0, j)),
                  pl.BlockSpec(memory_space=pl.ANY)],
        out_specs=pl.BlockSpec((s, LANES), lambda j: (0, 4 + j)),
        input_output_aliases={2: 0},
        compiler_params=_params(("parallel",), VMEM_BIG),
    )(z_conv, cw, a_cat)


def _conv_bwd(z_conv, cw, da, h_all, name):
    s = z_conv.shape[0]

    def body(z_ref, w_ref, da_ref, h_ref, dz_ref, dw_ref, g_ref):
        gb, gc, xv = z_ref[:, 0:LANES], z_ref[:, LANES:2 * LANES], z_ref[:, 2 * LANES:3 * LANES]
        u = gc * xv
        prev, nxt = _shift_rows(u, s)
        dcv = da_ref[...]
        dz_ref[:, 0:LANES] = (dcv * (w_ref[0:1, :] * prev + w_ref[1:2, :] * u + w_ref[2:3, :] * nxt)).astype(BF16)
        dy = dcv * gb
        dw_ref[0:1, :] = jnp.sum(dy * prev, axis=0, keepdims=True)
        dw_ref[1:2, :] = jnp.sum(dy * u, axis=0, keepdims=True)
        dw_ref[2:3, :] = jnp.sum(dy * nxt, axis=0, keepdims=True)
        dyp, dyn = _shift_rows(dy, s)
        du = w_ref[0:1, :] * dyn + w_ref[1:2, :] * dy + w_ref[2:3, :] * dyp
        dz_ref[:, LANES:2 * LANES] = (du * xv).astype(BF16)
        dz_ref[:, 2 * LANES:3 * LANES] = (du * gc).astype(BF16)
        g_ref[...] = lax.dot_general(dz_ref[...], h_ref[...], TN_DIMS, preferred_element_type=F32)

    d = h_all.shape[1]
    blk = pl.BlockSpec((s, 3 * LANES), lambda j: (0, j))
    cws = pl.BlockSpec((3, LANES), lambda j: (0, j))
    return pl.pallas_call(
        body, name=name, grid=(CONV_W // LANES,),
        out_shape=(jax.ShapeDtypeStruct(z_conv.shape, BF16), jax.ShapeDtypeStruct((3, CONV_W), F32),
                   jax.ShapeDtypeStruct((z_conv.shape[1], d), F32)),
        in_specs=[blk, cws, pl.BlockSpec((s, LANES), lambda j: (0, 4 + j)), pl.BlockSpec((s, d), lambda j: (0, 0))],
        out_specs=(blk, cws, pl.BlockSpec((3 * LANES, d), lambda j: (j, 0))),
        compiler_params=_params(("parallel",), VMEM_BIG),
    )(z_conv, cw, da, h_all)


ATT_TQ = 512
ATT_Q_STEP = 1024
ATT_TQ_BWD = 512


def _head_mask(shape, hh):
    lane = lax.broadcasted_iota(jnp.int32, shape, 1)
    return (lane >= hh * V_DIM) & (lane < (hh + 1) * V_DIM)


def _attn_fwd(qf, kv, s, riding, name):
    t = kv.shape[0]
    step = min(ATT_Q_STEP, s)
    nq = s // step
    nr = riding.n

    def body(*refs):
        q_ref, k_ref, v_ref = refs[:3]
        o_ref, ob_ref, st_ref = refs[3 + nr:6 + nr]
        p, i = pl.program_id(0), pl.program_id(1)
        state = riding.run((p == 0) & (i == 0), (p == N_HEADS // 2 - 1) & (i == nq - 1),
                           refs[3:3 + nr], refs[6 + nr:6 + 2 * nr], refs[6 + 2 * nr:],
                           middle=(p == N_HEADS // 2 - 2) & (i == nq // 2))
        v = v_ref[...]
        vlane = lax.broadcasted_iota(jnp.int32, v.shape, 1)
        one_lane = [(1 - hh) * V_DIM for hh in range(2)]
        vm = [jnp.where(_head_mask(v.shape, hh), v, jnp.where(vlane == one_lane[hh], 1.0, 0.0).astype(BF16))
              for hh in range(2)]

        def block(r, carry):
            rows = pl.ds(pl.multiple_of(r * ATT_TQ, ATT_TQ), ATT_TQ)
            olane = lax.broadcasted_iota(jnp.int32, (ATT_TQ, LANES), 1)
            acc = jnp.zeros((ATT_TQ, LANES), F32)
            stat = jnp.zeros((ATT_TQ, LANES), F32)
            scores = [lax.dot_general(q_ref[rows, hh * LANES:(hh + 1) * LANES], k_ref[:, hh * LANES:(hh + 1) * LANES],
                                      NT_DIMS, preferred_element_type=F32) for hh in range(2)]
            maxes = [jnp.max(sc, axis=1, keepdims=True) for sc in scores]
            exps = [jnp.exp2((sc - mx) * EXP2_SCALE).astype(BF16) for sc, mx in zip(scores, maxes)]
            for hh in range(2):
                mx = maxes[hh]
                res = jnp.dot(exps[hh], vm[hh], preferred_element_type=F32)
                den = jnp.sum(jnp.where(olane == one_lane[hh], res, 0.0), axis=1, keepdims=True)
                acc = acc + jnp.where(_head_mask(res.shape, hh), res * (1.0 / den), 0.0)
                stat = stat + jnp.where(olane == hh, mx * EXP2_SCALE + jnp.log(den) * LOG2_E, 0.0)
            o_ref[rows, :] = acc
            ob_ref[rows, :] = acc.astype(BF16)
            st_ref[:, rows] = stat.T[0:8, :]
            return carry

        lax.fori_loop(0, step // ATT_TQ, block, 0)
        riding.finish(state)

    o_spec = pl.BlockSpec((step, LANES), lambda p, i: (i, p))
    outs = pl.pallas_call(
        body, name=name, grid=(N_HEADS // 2, nq),
        out_shape=(jax.ShapeDtypeStruct((s, N_HEADS * V_DIM), F32),
                   jax.ShapeDtypeStruct((s, D_MODEL), BF16),
                   jax.ShapeDtypeStruct((N_HEADS // 2 * 8, s), F32), *riding.out_shape),
        in_specs=[pl.BlockSpec((step, 2 * LANES), lambda p, i: (i, p)),
                  pl.BlockSpec((t, 2 * LANES), lambda p, i: (0, p)),
                  pl.BlockSpec((t, LANES), lambda p, i: (0, N_HEADS + p)), *riding.specs],
        out_specs=(o_spec, o_spec, pl.BlockSpec((8, step), lambda p, i: (p, i)), *riding.specs),
        scratch_shapes=riding.scratch,
        compiler_params=_params(("arbitrary", "arbitrary"), VMEM_BIG),
    )(qf, kv, kv, *riding.arrays)
    return outs[0], outs[1], outs[2], list(outs[3:])


def _attn_bwd(qf, kv, o, da, stats, cos, sgn, riding, name):
    s, t = o.shape[0], kv.shape[0]
    ATT_TQ = ATT_TQ_BWD
    nq = s // ATT_TQ
    nr = riding.n

    def body(*refs):
        q_ref, k_ref, v_ref, o_ref, do_ref, st_ref, c_ref, s_ref = refs[:8]
        dq_ref, dk_ref, dv_ref = refs[8 + nr:11 + nr]
        dk_acc, dv_acc = refs[11 + 2 * nr:13 + 2 * nr]
        p, i = pl.program_id(0), pl.program_id(1)
        state = riding.run((p == 0) & (i == 0), (p == N_HEADS // 2 - 1) & (i == nq - 1),
                           refs[8:8 + nr], refs[11 + nr:11 + 2 * nr], refs[13 + 2 * nr:])

        @pl.when(i == 0)
        def _():
            dk_acc[...] = jnp.zeros_like(dk_acc)
            dv_acc[...] = jnp.zeros_like(dv_acc)

        v = v_ref[...]
        do = do_ref[...]
        od = do * o_ref[...]
        ones = jnp.ones((8, LANES), F32)
        for hh in range(2):
            sl = slice(hh * LANES, (hh + 1) * LANES)
            q, k = q_ref[:, sl], k_ref[:, sl]
            mask = _head_mask(do.shape, hh)
            dom = jnp.where(mask, do, 0.0).astype(BF16)
            delta = lax.dot_general(ones, jnp.where(mask, od, 0.0), NT_DIMS, preferred_element_type=F32,
                                    precision=lax.Precision.HIGHEST)[0:1, :]
            st = lax.dot_general(k, q, NT_DIMS, preferred_element_type=F32)
            pt = jnp.exp2(st * EXP2_SCALE - st_ref[hh:hh + 1, :]).astype(BF16)
            dpt = lax.dot_general(v, dom, NT_DIMS, preferred_element_type=F32)
            dst = (pt.astype(F32) * (dpt - delta)).astype(BF16)
            dv_acc[...] += jnp.dot(pt, dom, preferred_element_type=F32)
            dk_acc[:, sl] += jnp.dot(dst, q, preferred_element_type=F32)
            dq = lax.dot_general(dst, k, TN_DIMS, preferred_element_type=F32) * ATTN_SCALE
            dq_ref[:, sl] = _rope(dq, c_ref[...], s_ref[...], True).astype(BF16)

        @pl.when(i == nq - 1)
        def _():
            dk_ref[...] = (dk_acc[...] * ATTN_SCALE).astype(BF16)
            dv_ref[...] = dv_acc[...].astype(BF16)

        riding.finish(state)

    o_spec = pl.BlockSpec((ATT_TQ, LANES), lambda p, i: (i, p))
    tab = pl.BlockSpec((ATT_TQ, LANES), lambda p, i: (i, 0))
    outs = pl.pallas_call(
        body, name=name, grid=(N_HEADS // 2, nq),
        out_shape=(jax.ShapeDtypeStruct((s, N_HEADS * LANES), BF16),
                   jax.ShapeDtypeStruct((t, N_HEADS * LANES), BF16),
                   jax.ShapeDtypeStruct((t, N_HEADS * V_DIM), BF16), *riding.out_shape),
        in_specs=[pl.BlockSpec((ATT_TQ, 2 * LANES), lambda p, i: (i, p)),
                  pl.BlockSpec((t, 2 * LANES), lambda p, i: (0, p)),
                  pl.BlockSpec((t, LANES), lambda p, i: (0, N_HEADS + p)),
                  o_spec, o_spec,
                  pl.BlockSpec((8, ATT_TQ), lambda p, i: (p, i)), tab, tab, *riding.specs],
        out_specs=(pl.BlockSpec((ATT_TQ, 2 * LANES), lambda p, i: (i, p)),
                   pl.BlockSpec((t, 2 * LANES), lambda p, i: (0, p)),
                   pl.BlockSpec((t, LANES), lambda p, i: (0, p)), *riding.specs),
        scratch_shapes=[pltpu.VMEM((t, 2 * LANES), F32), pltpu.VMEM((t, LANES), F32), *riding.scratch],
        compiler_params=_params(("arbitrary", "arbitrary"), VMEM_BIG),
    )(qf, kv, kv, o, da, stats, cos, sgn, *riding.arrays)
    return outs[0], outs[1], outs[2], list(outs[3:])


def _silu(x):
    return x * (1.0 / (1.0 + jnp.exp(-x)))


def _prologue(c_rows, c_ctx, w_mod, b_cols, extra_rows, name):
    d, cols = c_rows.shape[1], w_mod.shape[1]

    def body(c_ref, cctx_ref, wmod_ref, b_ref, x_ref, a_ref, modg_ref, c_all, blk, c_send, c_recv, m_send, m_recv):
        _direct_gather(c_ref, c_all, c_send, c_recv)()
        a_ref[...] = jnp.zeros_like(a_ref)
        for j in range(N_DEV):
            a_ref[j:j + 1, :] = c_all[j, 0:1, :]
        a_ref[N_DEV:N_DEV + 1, :] = cctx_ref[...]
        mod = jnp.dot(_silu(a_ref[...]), wmod_ref[...], preferred_element_type=F32,
                      precision=lax.Precision.HIGHEST) + b_ref[...]
        blk[...] = jnp.zeros_like(blk)
        for p in range(N_DEV):
            blk[p, 0:1, :] = mod[p:p + 1, :]
            blk[p, 1:2, :] = mod[N_DEV:N_DEV + 1, :]
            blk[p, 2:5, :] = x_ref[...]
        _direct_gather(blk, modg_ref, m_send, m_recv, per_peer=True)()

    vmem = pl.BlockSpec(memory_space=pltpu.VMEM)
    return pl.pallas_call(
        body, name=name,
        out_shape=(jax.ShapeDtypeStruct((16, d), F32), jax.ShapeDtypeStruct((N_DEV, 8, cols), F32)),
        in_specs=[vmem] * 5, out_specs=(vmem, vmem),
        scratch_shapes=[pltpu.VMEM((N_DEV, 8, d), F32), pltpu.VMEM((N_DEV, 8, cols), F32)]
        + [pltpu.SemaphoreType.DMA((7,)) for _ in range(4)],
        compiler_params=_params(None, VMEM_BIG),
    )(c_rows, c_ctx, w_mod, b_cols, extra_rows)


def _adaln_bwd(a_t, w, d_ex, d_ctx, d_all, name):
    def body(at_ref, w_ref, dex_ref, dctx_ref, dall_ref, gw_ref, dsil_ref, dsum_ref):
        sil_t = _silu(at_ref[...])
        dctx = dctx_ref[...]
        row = dctx[0:1, :]
        for j in range(1, N_DEV):
            row = row + dctx[j:j + 1, :]
        rowi = lax.broadcasted_iota(jnp.int32, dctx.shape, 0)
        ctx_rows = jnp.where(rowi == 0, jnp.broadcast_to(row, dctx.shape), 0.0)
        hi = lax.Precision.HIGHEST
        d_rows = jnp.concatenate([dex_ref[...], ctx_rows], axis=0)
        gw_ref[...] = jnp.dot(sil_t, d_rows, preferred_element_type=F32, precision=hi)
        dsil_ref[...] = lax.dot_general(ctx_rows, w_ref[...], NT_DIMS, preferred_element_type=F32, precision=hi)
        tot = dall_ref[0]
        for j in range(1, N_DEV):
            tot = tot + dall_ref[j]
        dsum_ref[...] = tot

    return pl.pallas_call(
        body, name=name,
        out_shape=(jax.ShapeDtypeStruct(w.shape, F32), jax.ShapeDtypeStruct((8, w.shape[0]), F32),
                   jax.ShapeDtypeStruct(d_all.shape[1:], F32)),
        compiler_params=_params(None, VMEM_BIG),
    )(a_t, w, d_ex, d_ctx, d_all)


SMALL_ROWS = 24
SMALL_MISC, SMALL_CW, SMALL_LOSS = 16, 18, 21


def _pack_small(sums1, sums2, fsums, sums1c, psums, psums_c, d_cw, cols, name):
    d = D_MODEL

    def body(s1_ref, s2_ref, f_ref, s1c_ref, p_ref, pc_ref, cw_ref, o_ref):
        o_ref[...] = jnp.zeros_like(o_ref)

        def blocks(row0, pieces):
            for j in range(N_DEV):
                lo, hi = j * cols, (j + 1) * cols
                for k, (ref, r) in enumerate(pieces):
                    a, b = max(lo, k * d), min(hi, (k + 1) * d)
                    if a < b:
                        o_ref[row0 + j:row0 + j + 1, a - lo:b - lo] = ref[r:r + 1, a - k * d:b - k * d]

        blocks(0, [(s1_ref, 1), (s1_ref, 0), (s2_ref, 2), (s2_ref, 1), (s2_ref, 0), (f_ref, 1)])
        blocks(N_DEV, [(s1c_ref, 1), (s1c_ref, 0)])
        head = Q_RANK + KV_RANK
        o_ref[SMALL_MISC:SMALL_MISC + 1, 0:Q_RANK] = p_ref[0:1, :]
        o_ref[SMALL_MISC:SMALL_MISC + 1, Q_RANK:head] = p_ref[1:2, 0:KV_RANK] + pc_ref[1:2, 0:KV_RANK]
        o_ref[SMALL_MISC:SMALL_MISC + 1, head:cols] = f_ref[0:1, 0:cols - head]
        o_ref[SMALL_MISC + 1:SMALL_MISC + 2, 0:d - (cols - head)] = f_ref[0:1, cols - head:d]
        for r in range(3):
            o_ref[SMALL_CW + r:SMALL_CW + r + 1, 0:CONV_W] = cw_ref[r:r + 1, :]
        o_ref[SMALL_LOSS:SMALL_LOSS + 1, :] = f_ref[3:4, 0:cols]

    return pl.pallas_call(body, name=name, out_shape=jax.ShapeDtypeStruct((SMALL_ROWS, cols), F32))(
        sums1, sums2, fsums, sums1c, psums, psums_c, d_cw)


def _adam_math(w, g, m, v):
    nm = ADAM_B1 * m + (1.0 - ADAM_B1) * g
    nv = ADAM_B2 * v + (1.0 - ADAM_B2) * (g * g)
    m_hat = nm / (1.0 - ADAM_B1 ** ADAM_STEP)
    v_hat = nv / (1.0 - ADAM_B2 ** ADAM_STEP)
    return -ADAM_LR * (m_hat / (jnp.sqrt(v_hat) + ADAM_EPS) + ADAM_WD * w), nm, nv


def _small_update(dsum, dsil_all, g_cw, params, name):
    d = D_MODEL
    n = len(params)
    cols = dsum.shape[1]

    def body(*refs):
        dsum_ref, dsil_ref, gcw_ref = refs[:3]
        wmv = refs[3:3 + 3 * n]
        outs = refs[3 + 3 * n:]
        tot = dsil_ref[0]
        for j in range(1, N_DEV):
            tot = tot + dsil_ref[j]
        cv = wmv[0][...]
        sg = 1.0 / (1.0 + jnp.exp(-cv))
        off = Q_RANK + KV_RANK
        misc = dsum_ref[SMALL_MISC:SMALL_MISC + 1, :]
        grads = [tot[0:1, :] * (sg * (1.0 + cv * (1.0 - sg))),
                 jnp.concatenate([dsum_ref[j:j + 1, :] + dsum_ref[N_DEV + j:N_DEV + j + 1, :] for j in range(N_DEV)],
                                 axis=1),
                 misc[:, 0:Q_RANK], misc[:, Q_RANK:off],
                 jnp.concatenate([misc[:, off:cols], dsum_ref[SMALL_MISC + 1:SMALL_MISC + 2, 0:d - (cols - off)]],
                                 axis=1),
                 gcw_ref[...]]
        for p, g in enumerate(grads):
            w_ref, m_ref, v_ref = wmv[3 * p:3 * p + 3]
            at = 0 if len(w_ref.shape) == 3 else Ellipsis
            res = (g,) + _adam_math(w_ref[at], g, m_ref[at], v_ref[at])
            for q, val in enumerate(res):
                outs[4 * p + q][at] = val

    flat = [a for wmv in params for a in wmv]
    out_shape = tuple(jax.ShapeDtypeStruct(wmv[0].shape, F32) for wmv in params for _ in range(4))
    outs = pl.pallas_call(body, name=name, out_shape=out_shape)(dsum, dsil_all, g_cw, *flat)
    return [outs[4 * p:4 * p + 4] for p in range(n)]


def _adamw(w, g, m, v, name, slots=False):
    _, rows, cols = w.shape
    tr = _pick(rows, (256, 128, 64, 32, 16, 8))

    def body(w_ref, g_ref, m_ref, v_ref, *outs):
        if slots:
            gv = g_ref[0].astype(F32)
            for j in range(1, g.shape[0]):
                gv = gv + g_ref[j].astype(F32)
            outs[0][...] = gv
        else:
            gv = g_ref[...]
        d_ref, nm_ref, nv_ref = outs[-3:]
        d_ref[...], nm_ref[...], nv_ref[...] = _adam_math(w_ref[...], gv, m_ref[...], v_ref[...])

    blk = pl.BlockSpec((None, tr, cols), lambda i: (0, i, 0))
    g_spec = (pl.BlockSpec((g.shape[0], tr, cols), lambda i: (0, i, 0)) if slots
              else pl.BlockSpec((tr, cols), lambda i: (i, 0)))
    sh = jax.ShapeDtypeStruct((1, rows, cols), F32)
    n_out = 4 if slots else 3
    return pl.pallas_call(
        body, name=name, grid=(rows // tr,), out_shape=(sh,) * n_out,
        in_specs=[blk, g_spec, blk, blk], out_specs=(blk,) * n_out,
        compiler_params=_params(("parallel",), VMEM_BIG),
    )(w, g, m, v)


def _rope_tables(s, l):
    tok = np.arange(s)
    row = (tok // GRID_W).astype(np.float32)
    col = (tok % GRID_W).astype(np.float32)
    half = QK_ROPE // 2
    freqs = np.float32(ROPE_THETA) ** (-np.arange(0, half, 2, dtype=np.float32) / np.float32(half))
    dd = np.arange(QK_ROPE)
    pos = np.where((dd // half)[None, :] == 0, row[:, None], col[:, None]).astype(np.float32)
    ang = (pos * freqs[dd % (half // 2)][None, :]).astype(np.float32)
    sin = np.sin(ang).astype(np.float32)
    cos_t = np.ones((s + l, LANES), np.float32)
    sgn_t = np.zeros((s + l, LANES), np.float32)
    cos_t[:s, QK_NOPE:QK_NOPE + QK_ROPE] = np.cos(ang)
    sgn_t[:s, QK_NOPE:QK_NOPE + QK_ROPE] = np.where(((dd % half) // (half // 2))[None, :] == 0, -sin, sin)
    return jnp.asarray(cos_t), jnp.asarray(sgn_t)


def _slots_to_cols(g):
    return g.transpose(1, 0, 2).reshape(g.shape[1], N_DEV * g.shape[2])


def _cols_to_slots(w):
    return w.reshape(w.shape[0], N_DEV, w.shape[1] // N_DEV).transpose(1, 0, 2)


def _unpack_small_weights(g_in_t, g_uq, g_ukv):
    w_t = g_in_t.reshape(N_DEV * g_in_t.shape[1], D_MODEL)
    zeros = jnp.zeros((QK_NOPE, D_MODEL), BF16)
    win_head_t = jnp.concatenate([w_t[:Q_RANK + KV_RANK], zeros, w_t[Q_RANK + KV_RANK:MLA_IN],
                                  zeros[:LANES - QK_NOPE - QK_ROPE]], axis=0)
    win_conv_t = w_t[MLA_IN:].reshape(3, CONV_W // LANES, LANES, D_MODEL).transpose(1, 0, 2, 3)
    win_conv_t = win_conv_t.reshape(3 * CONV_W, D_MODEL)
    w_uq = _slots_to_cols(g_uq).reshape(Q_RANK, N_HEADS, QK_NOPE + QK_ROPE)
    wq = jnp.pad(w_uq, ((0, 0), (0, 0), (0, LANES - QK_NOPE - QK_ROPE))).reshape(Q_RANK, N_HEADS * LANES)
    w_ukv = _slots_to_cols(g_ukv).reshape(KV_RANK, N_HEADS, QK_NOPE + V_DIM)
    k_top = jnp.pad(w_ukv[:, :, :QK_NOPE], ((0, 0), (0, 0), (0, LANES - QK_NOPE))).reshape(KV_RANK, N_HEADS * LANES)
    v_top = w_ukv[:, :, QK_NOPE:].reshape(KV_RANK, N_HEADS * V_DIM)
    eye = jnp.pad(jnp.eye(QK_ROPE, dtype=BF16), ((QK_NOPE, LANES - QK_NOPE - QK_ROPE),) * 2)
    wk = jnp.concatenate([
        jnp.concatenate([k_top, v_top], axis=1),
        jnp.concatenate([jnp.tile(eye, (1, N_HEADS)), jnp.zeros((LANES, N_HEADS * V_DIM), BF16)], axis=1)], axis=0)
    return win_head_t, win_conv_t, wq, wk


def _pack_small_grads(d_head_t, d_conv_t, d_wq, d_wkk, d_wkv):
    d_conv_t = d_conv_t.reshape(CONV_W // LANES, 3, LANES, D_MODEL).transpose(1, 0, 2, 3).reshape(3 * CONV_W, D_MODEL)
    rope0 = Q_RANK + KV_RANK + QK_NOPE
    g_in_t = jnp.concatenate([d_head_t[:Q_RANK + KV_RANK], d_head_t[rope0:rope0 + QK_ROPE], d_conv_t], axis=0)
    g_in_t = g_in_t.reshape(N_DEV, -1, D_MODEL).astype(BF16)
    g_uq = d_wq.reshape(Q_RANK, N_HEADS, LANES)[:, :, :QK_NOPE + QK_ROPE].reshape(Q_RANK, -1)
    g_kn = d_wkk[:KV_RANK].reshape(KV_RANK, N_HEADS, LANES)[:, :, :QK_NOPE]
    g_v = d_wkv[:KV_RANK].reshape(KV_RANK, N_HEADS, V_DIM)
    g_ukv = jnp.concatenate([g_kn, g_v], axis=2).reshape(KV_RANK, -1)
    return [g_in_t] + [_cols_to_slots(g).astype(BF16) for g in (g_uq, g_ukv)]


def kernel(x, c, ctx, c_ctx, w_mod, b_mod, w_in, q_norm_g, w_uq, kv_norm_g, w_ukv, conv_w, w_out, w_mlp1, w_mlp2, final_norm_g, loss_target, m_c_ctx, m_w_mod, m_b_mod, m_w_in, m_q_norm_g, m_w_uq, m_kv_norm_g, m_w_ukv, m_conv_w, m_w_out, m_w_mlp1, m_w_mlp2, m_final_norm_g, v_c_ctx, v_w_mod, v_b_mod, v_w_in, v_q_norm_g, v_w_uq, v_kv_norm_g, v_w_ukv, v_conv_w, v_w_out, v_w_mlp1, v_w_mlp2, v_final_norm_g):
    me = _my_index()
    x2d, ctx2d, tgt = x[0], ctx[0], loss_target[0]
    s, l = x2d.shape[0], ctx2d.shape[0]
    t = s + l
    d = D_MODEL
    mod_cols = w_mod.shape[2]
    cw_cols = conv_w.shape[2]

    b_cols = lax.dynamic_slice(b_mod, (0, me * mod_cols), (1, mod_cols))
    cw_blk = jnp.pad(conv_w[0], ((0, 0), (0, mod_cols - cw_cols)))
    a_rows, gathered = _prologue(jnp.pad(c, ((0, 7), (0, 0))), c_ctx[None, :], w_mod[0], b_cols, cw_blk,
                                 "prologue")
    mod_mine = gathered[:, 0, :].reshape(1, 6 * d)
    mod_ctx = gathered[:, 1, :].reshape(1, 6 * d)
    cw_full = gathered[:, 2:5, :cw_cols].transpose(1, 0, 2).reshape(3, CONV_W)

    early = [w.astype(BF16) for w in (w_in[0].T, w_uq[0], w_ukv[0])]
    late = [w.astype(BF16) for w in (w_out[0], w_mlp1[0], w_mlp2[0])]
    h_all, (g_in, g_uq, g_ukv) = _modulate_all(x2d, ctx2d, mod_mine, mod_ctx, _RidingGather(early),
                                               "modulate1")
    win_head, win_conv, wq, wk = _unpack_small_weights(g_in, g_uq, g_ukv)
    wk_k, wk_v = wk[:, :N_HEADS * LANES], wk[:, N_HEADS * LANES:]
    cos, sgn = _rope_tables(s, l)

    tm_t = _pick(t, (1088, 768, 256))
    z_head, cq, kv_in, qf, kv = _head_fwd(h_all, win_head, wq, wk, q_norm_g, kv_norm_g, cos, sgn, tm_t, "head_fwd")
    z_conv = _matmul(h_all, win_conv, mode="nt", name="in_proj_conv", m=s, tm=1024, tn=1536, tk=1024)
    attn, a_cat, stats, (g_out, w1, g_w2) = _attn_fwd(qf, kv, s, _RidingGather(late), "attn_fwd")
    wo = g_out.reshape(d, d)
    w2 = g_w2.reshape(D_FF, d)
    a_cat = _conv_fwd(z_conv, cw_full, a_cat, "conv_fwd")
    (o, x1, h2), _ = _matmul_rows(a_cat, wo, _epi_resid_modulate, mode="nn", name="out_proj", tm=1024, tk=1024,
                                  rows=[x2d], vecs=[(mod_mine, 2), (mod_mine, 3), (mod_mine, 4)],
                                  out_dtypes=[F32, F32, BF16])
    u1, act = _matmul(h2, w1, mode="nn", name="mlp_up", tm=4096, tk=1024, epilogue="relu2", slots="b_cols")
    (dx2, dm, fsums), _ = _matmul_rows(act, w2, _epi_final, mode="nn", name="mlp_down", tm=512, tk=4096,
                                       rows=[x1, tgt], vecs=[(mod_mine, 5), (final_norm_g[None, :], 0)],
                                       out_dtypes=[F32, BF16], sums=True)

    d_w2 = _matmul(act, dm, mode="tn", name="d_w_mlp2", out_dtype=BF16, tm=1024, tn=1024, tk=4096)
    du1 = _matmul(dm, w2, mode="nt", name="d_act", out_dtype=BF16, tm=2048, tn=1024, tk=1024,
                  epilogue="drelu2", extra=(u1,))
    d_w1 = _matmul(h2, du1, mode="tn", name="d_w_mlp1", out_dtype=BF16, tm=1024, tk=4096, slots="out")
    (dx1, do, sums2), _ = _matmul_rows(du1, w1, _epi_modulate2_bwd, mode="nt", name="d_h2", tm=512, tk=4096,
                                       slots="b_contract", rows=[x1, dx2, o], vecs=[(mod_mine, 4), (mod_mine, 2)],
                                       out_dtypes=[F32, BF16], sums=True)
    da, d_wo = _out_proj_bwd(a_cat, do, wo, "out_proj_bwd")
    dz_conv, d_cw, d_conv = _conv_bwd(z_conv, cw_full, da, h_all, "conv_bwd")
    ready = [d_wo.reshape(N_DEV, d // N_DEV, d), d_w1, d_w2.reshape(N_DEV, D_FF // N_DEV, d)]
    dq, dk, dv, rode = _attn_bwd(qf, kv, attn, da, stats, cos, sgn, _Riding(ready), "attn_bwd")
    head_args = (z_head, wq, wk_k, wk_v, win_head, q_norm_g, kv_norm_g, cos, sgn, cq, kv_in, h_all)
    dz_head, dh_head, psums, d_wq, *carried = _head_bwd(dq, dk, dv, *head_args, "head_bwd", tile=HEAD_BWD_TILE,
                                                        first_block=0, n_blocks=s // HEAD_BWD_TILE)
    _, dh_head, psums_c, d_wkk, d_wkv, d_head = _head_bwd(
        None, dk, dv, *head_args, "head_bwd_ctx", tile=ROW_TILE, first_block=s // ROW_TILE, n_blocks=l // ROW_TILE,
        carry=(dz_head, dh_head, *carried))
    send = _pack_small_grads(d_head, d_conv, d_wq, d_wkk, d_wkv)
    (grad_x, sums1), got = _matmul_rows(dz_conv, win_conv, _epi_modulate1_bwd, mode="nn", name="d_h1", tm=max(s // 8, ROW_TILE),
                                        tk=win_conv.shape[0], rows=[dh_head, x2d, dx1], vecs=[(mod_mine, 1)],
                                        out_dtypes=[F32], sums=True, riding=_RidingReduce(send))
    sums1c = _modulate_sums(dh_head, s // ROW_TILE, ctx2d)

    small = _pack_small(sums1, sums2, fsums, sums1c, psums, psums_c, d_cw, mod_cols, "pack_small")
    (d_all,) = _all_gather([small], "gather_small_grads", True)
    d_ex = lax.dynamic_index_in_dim(d_all, me, axis=1, keepdims=False)
    d_ctx = lax.dynamic_index_in_dim(d_all, N_DEV + me, axis=1, keepdims=False)
    g_w_mod, dsil, dsum = _adaln_bwd(a_rows.T, w_mod[0], d_ex, d_ctx, d_all, "adaln_bwd")
    (dsil_all,) = _all_gather([dsil], "gather_d_cctx", True)
    loss = dsum[SMALL_LOSS, 0]
    g_cw = lax.dynamic_slice(dsum, (SMALL_CW, me * cw_cols), (3, cw_cols))

    slots = dict(zip(["w_in", "w_uq", "w_ukv"], got))
    slots.update(zip(["w_out", "w_mlp1", "w_mlp2"], rode))

    grads = {}
    weights = {"c_ctx": c_ctx, "w_mod": w_mod, "b_mod": b_mod, "w_in": w_in, "q_norm_g": q_norm_g, "w_uq": w_uq,
               "kv_norm_g": kv_norm_g, "w_ukv": w_ukv, "conv_w": conv_w, "w_out": w_out, "w_mlp1": w_mlp1,
               "w_mlp2": w_mlp2, "final_norm_g": final_norm_g}
    m_in = {"c_ctx": m_c_ctx, "w_mod": m_w_mod, "b_mod": m_b_mod, "w_in": m_w_in, "q_norm_g": m_q_norm_g,
            "w_uq": m_w_uq, "kv_norm_g": m_kv_norm_g, "w_ukv": m_w_ukv, "conv_w": m_conv_w, "w_out": m_w_out,
            "w_mlp1": m_w_mlp1, "w_mlp2": m_w_mlp2, "final_norm_g": m_final_norm_g}
    v_in = {"c_ctx": v_c_ctx, "w_mod": v_w_mod, "b_mod": v_b_mod, "w_in": v_w_in, "q_norm_g": v_q_norm_g,
            "w_uq": v_w_uq, "kv_norm_g": v_kv_norm_g, "w_ukv": v_w_ukv, "conv_w": v_conv_w, "w_out": v_w_out,
            "w_mlp1": v_w_mlp1, "w_mlp2": v_w_mlp2, "final_norm_g": v_final_norm_g}
    names = list(weights)
    small_names = ["c_ctx", "b_mod", "q_norm_g", "kv_norm_g", "final_norm_g", "conv_w"]
    delta, new_m, new_v = {}, {}, {}

    def as_rows(a):
        return a[None, :] if a.ndim == 1 else a

    small_out = _small_update(dsum, dsil_all, g_cw, [[as_rows(src[n]) for src in (weights, m_in, v_in)]
                                                      for n in small_names], "small_update")
    for n, outs in zip(small_names, small_out):
        grads[n], delta[n], new_m[n], new_v[n] = [a.reshape(weights[n].shape) for a in outs]
    for n in names:
        if n in small_names:
            continue
        if n == "w_in":
            wmv = [jnp.swapaxes(src[n], 1, 2) for src in (weights, m_in, v_in)]
            outs = _adamw(wmv[0], slots[n], wmv[1], wmv[2], "adamw_" + n, slots=True)
            grads[n], delta[n], new_m[n], new_v[n] = [jnp.swapaxes(a, 1, 2) for a in outs]
        elif n in slots:
            grads[n], delta[n], new_m[n], new_v[n] = _adamw(weights[n], slots[n], m_in[n], v_in[n], "adamw_" + n,
                                                            slots=True)
        else:
            delta[n], new_m[n], new_v[n] = _adamw(weights[n], g_w_mod, m_in[n], v_in[n], "adamw_" + n)
            grads[n] = g_w_mod[None]

    return (loss, grad_x[None], *[grads[n] for n in names], *[delta[n] for n in names],
            *[new_m[n] for n in names], *[new_v[n] for n in names])
```

```python
import math

import jax
import jax.numpy as jnp
import numpy as np
from jax import lax
from jax.experimental import pallas as pl
from jax.experimental.pallas import tpu as pltpu

F32 = jnp.float32
BF16 = jnp.bfloat16

D_MODEL = 1024
GRID_W = 64
N_HEADS = 8
QK_NOPE = 64
QK_ROPE = 32
V_DIM = 64
Q_RANK = 256
KV_RANK = 128
MLA_IN = Q_RANK + KV_RANK + QK_ROPE
CONV_W = 512
HEAD_COLS = 512
D_FF = 4096
ROPE_THETA = 10000.0
EPS = 1e-6
ATTN_SCALE = 1.0 / math.sqrt(QK_NOPE + QK_ROPE)
LOG2_E = 1.0 / math.log(2.0)
EXP2_SCALE = ATTN_SCALE * LOG2_E
N_DEV = 8
LANES = 128

ADAM_LR, ADAM_B1, ADAM_B2, ADAM_EPS, ADAM_WD, ADAM_STEP = 0.001, 0.9, 0.999, 1e-08, 0.01, 10

ROW_TILE = 256
MATMUL_ROW_CHUNK = 512
HEAD_BWD_TILE = 512
VMEM_BIG = 60 * 1024 * 1024


def _params(sem=None, vmem=None):
    return pltpu.CompilerParams(dimension_semantics=sem, vmem_limit_bytes=vmem)


def _pick(n, prefs):
    for p in prefs:
        if n % p == 0:
            return p
    return n


def _my_index():
    return 4 * lax.axis_index("x") + 2 * lax.axis_index("y") + lax.axis_index("c")


def _two_level_gather(x_refs, out_refs, send_sems, recv_sems, local_sems):
    n = len(x_refs)
    x, y, c = lax.axis_index("x"), lax.axis_index("y"), lax.axis_index("c")
    me, sibling = (x, y, c), (x, y, 1 - c)
    chips = [(1 - x, y), (x, 1 - y), (1 - x, 1 - y)]

    def slot(a, px, py, pc):
        return out_refs[a].at[4 * px + 2 * py + pc]

    def copy(a, k, block, to, src=None):
        return pltpu.make_async_remote_copy(
            src_ref=slot(a, *block) if src is None else src, dst_ref=slot(a, *block),
            send_sem=send_sems.at[7 * a + k], recv_sem=recv_sems.at[7 * a + k],
            device_id=to, device_id_type=pl.DeviceIdType.MESH)

    mine = [pltpu.make_async_copy(x_refs[a], slot(a, *me), local_sems.at[a]) for a in range(n)]
    first = [cp for a in range(n) for cp in
             [copy(a, 0, me, sibling, src=x_refs[a])]
             + [copy(a, 1 + j, me, (*chip, c), src=x_refs[a]) for j, chip in enumerate(chips)]]
    passed = [[copy(a, 4 + j, (*chip, c), sibling) for j, chip in enumerate(chips)] for a in range(n)]

    def start():
        for cp in mine + first:
            cp.start()

    def forward():
        for a in range(n):
            for j, chip in enumerate(chips):
                copy(a, 1 + j, (*chip, c), me).wait_recv()
                passed[a][j].start()

    def finish():
        for a in range(n):
            copy(a, 0, sibling, me).wait_recv()
            for j, chip in enumerate(chips):
                copy(a, 4 + j, (*chip, 1 - c), me).wait_recv()
        for cp in first + [cp for per_array in passed for cp in per_array]:
            cp.wait_send()
        for cp in mine:
            cp.wait()

    return start, forward, finish


def _direct_gather(src_ref, dst_ref, send_sems, recv_sems, per_peer=False):
    x, y, c = lax.axis_index("x"), lax.axis_index("y"), lax.axis_index("c")
    me = 4 * x + 2 * y + c
    dst_ref[me] = src_ref[me] if per_peer else src_ref[...]
    sends, landings = [], []
    for k in range(1, N_DEV):
        peer = (1 - x if k & 4 else x, 1 - y if k & 2 else y, 1 - c if k & 1 else c)
        pid = 4 * peer[0] + 2 * peer[1] + peer[2]
        for dst, out in ((me, sends), (pid, landings)):
            out.append(pltpu.make_async_remote_copy(
                src_ref=src_ref.at[pid] if per_peer else src_ref, dst_ref=dst_ref.at[dst],
                send_sem=send_sems.at[k - 1], recv_sem=recv_sems.at[k - 1],
                device_id=peer, device_id_type=pl.DeviceIdType.MESH))
    for cp in sends:
        cp.start()

    def finish():
        for cp in landings:
            cp.wait_recv()
        for cp in sends:
            cp.wait_send()

    return finish


def _all_gather(arrays, name, in_vmem):
    space = pltpu.VMEM if in_vmem else pl.ANY
    n = len(arrays)

    def body(*refs):
        for phase in _two_level_gather(refs[:n], refs[n:2 * n], *refs[2 * n:]):
            phase()

    outs = pl.pallas_call(
        body, name=name,
        out_shape=tuple(jax.ShapeDtypeStruct((N_DEV,) + a.shape, a.dtype) for a in arrays),
        in_specs=[pl.BlockSpec(memory_space=space)] * n,
        out_specs=tuple(pl.BlockSpec(memory_space=space) for _ in arrays),
        scratch_shapes=[pltpu.SemaphoreType.DMA((7 * n,)), pltpu.SemaphoreType.DMA((7 * n,)),
                        pltpu.SemaphoreType.DMA((n,))],
    )(*arrays)
    return list(outs)


class _Riding:
    def __init__(self, arrays=()):
        self.arrays, self.n = list(arrays), len(arrays)
        self.out_shape = [jax.ShapeDtypeStruct(a.shape, a.dtype) for a in self.arrays]
        self.specs = [pl.BlockSpec(memory_space=pl.ANY)] * self.n
        self.scratch = [pltpu.SemaphoreType.DMA((7 * self.n,)), pltpu.SemaphoreType.DMA((7 * self.n,)),
                        pltpu.SemaphoreType.DMA((self.n,))]

    def copies(self, x_refs, y_refs, send_sems, recv_sems, local_sems):
        x, y, c = lax.axis_index("x"), lax.axis_index("y"), lax.axis_index("c")
        me = 4 * x + 2 * y + c
        local, sends, landings = [], [], []
        for a in range(self.n):
            local.append(pltpu.make_async_copy(x_refs[a].at[me], y_refs[a].at[me], local_sems.at[a]))
            for k in range(1, N_DEV):
                peer = (1 - x if k & 4 else x, 1 - y if k & 2 else y, 1 - c if k & 1 else c)
                pid = 4 * peer[0] + 2 * peer[1] + peer[2]
                for dst, out in ((me, sends), (pid, landings)):
                    out.append(pltpu.make_async_remote_copy(
                        src_ref=x_refs[a].at[pid], dst_ref=y_refs[a].at[dst],
                        send_sem=send_sems.at[7 * a + k - 1], recv_sem=recv_sems.at[7 * a + k - 1],
                        device_id=peer, device_id_type=pl.DeviceIdType.MESH))
        return local, sends, landings

    def run(self, first, last, x_refs, y_refs, sems, middle=None):
        if self.n == 0:
            return None
        local, sends, landings = self.copies(x_refs, y_refs, *sems)

        @pl.when(first)
        def _():
            for cp in local + sends:
                cp.start()

        return local, sends, landings, last

    @staticmethod
    def finish(state):
        if state is None:
            return
        local, sends, landings, last = state

        @pl.when(last)
        def _():
            for cp in landings:
                cp.wait_recv()
            for cp in sends:
                cp.wait_send()
            for cp in local:
                cp.wait()


class _RidingGather:
    def __init__(self, arrays):
        self.arrays, self.n = list(arrays), len(arrays)
        self.out_shape = [jax.ShapeDtypeStruct((N_DEV,) + a.shape, a.dtype) for a in self.arrays]
        self.specs = [pl.BlockSpec(memory_space=pl.ANY)] * self.n
        self.scratch = [pltpu.SemaphoreType.DMA((7 * self.n,)), pltpu.SemaphoreType.DMA((7 * self.n,)),
                        pltpu.SemaphoreType.DMA((self.n,))]

    def run(self, first, last, x_refs, y_refs, sems, middle):
        start, forward, finish = _two_level_gather(x_refs, y_refs, *sems)
        pl.when(first)(start)
        pl.when(middle)(forward)
        return finish, last

    @staticmethod
    def finish(state):
        finish, last = state
        pl.when(last)(finish)


class _RidingReduce:
    def __init__(self, arrays):
        self.arrays, self.n = list(arrays), len(arrays)
        self.out_shape = [jax.ShapeDtypeStruct((4,) + a.shape[1:], a.dtype) for a in self.arrays]
        self.specs = [pl.BlockSpec(memory_space=pl.ANY)] * self.n
        self.scratch = [pltpu.VMEM((4,) + a.shape[1:], a.dtype) for a in self.arrays for _ in range(3)]
        self.scratch += [pltpu.SemaphoreType.DMA((self.n,)) for _ in range(6)]

    def run(self, first, last, x_refs, y_refs, scratch, middle):
        n = self.n
        own, sib, tot = scratch[0:3 * n:3], scratch[1:3 * n:3], scratch[2:3 * n:3]
        d2d_send, d2d_recv, local_in, ici_send, ici_recv, local_out = scratch[3 * n:]
        x, y, c = lax.axis_index("x"), lax.axis_index("y"), lax.axis_index("c")
        my_chip = 2 * x + y
        sibling = (x, y, 1 - c)
        others = [(1 - x, y), (x, 1 - y), (1 - x, 1 - y)]

        def to_sibling(a, j=None):
            src = x_refs[a].at[pl.ds(0, 4)] if j is None else x_refs[a].at[2 * j + 1 - c]
            dst = sib[a] if j is None else sib[a].at[j]
            return pltpu.make_async_remote_copy(src_ref=src, dst_ref=dst, send_sem=d2d_send.at[a],
                                                recv_sem=d2d_recv.at[a], device_id=sibling,
                                                device_id_type=pl.DeviceIdType.MESH)

        def mine_in(a, j=None):
            src = x_refs[a].at[pl.ds(0, 4)] if j is None else x_refs[a].at[2 * j + c]
            return pltpu.make_async_copy(src, own[a] if j is None else own[a].at[j], local_in.at[a])

        def to_chip(a, chip=None):
            if chip is None:
                src, dst, peer = tot[a].at[pl.ds(0, 3)], y_refs[a].at[pl.ds(0, 3)], sibling
            else:
                src, dst, peer = tot[a].at[2 * chip[0] + chip[1]], y_refs[a].at[my_chip], (*chip, c)
            return pltpu.make_async_remote_copy(src_ref=src, dst_ref=dst, send_sem=ici_send.at[a],
                                                recv_sem=ici_recv.at[a], device_id=peer,
                                                device_id_type=pl.DeviceIdType.MESH)

        def mine_out(a):
            return pltpu.make_async_copy(tot[a].at[my_chip], y_refs[a].at[my_chip], local_out.at[a])

        @pl.when(first)
        def _():
            for a in range(n):
                for j in range(4):
                    to_sibling(a, j).start()
                    mine_in(a, j).start()

        @pl.when(middle)
        def _():
            for a in range(n):
                to_sibling(a).wait_recv()
                to_sibling(a).wait_send()
                mine_in(a).wait()
                tot[a][...] = (own[a][...].astype(F32) + sib[a][...].astype(F32)).astype(tot[a].dtype)
                for chip in others:
                    to_chip(a, chip).start()
                mine_out(a).start()

        def finish():
            @pl.when(last)
            def _():
                for a in range(n):
                    to_chip(a).wait_recv()
                    to_chip(a).wait_send()
                    mine_out(a).wait()

        return finish

    @staticmethod
    def finish(state):
        state()


_DIMS ={"nn": (((1,), (0,)), ((), ())), "nt": (((1,), (1,)), ((), ())), "tn": (((0,), (0,)), ((), ()))}
NT_DIMS = _DIMS["nt"]
TN_DIMS = _DIMS["tn"]


def _swap8(x):
    lane = lax.broadcasted_iota(jnp.int32, x.shape, 1)
    return jnp.where((lane & 15) < 8, pltpu.roll(x, LANES - 8, 1), pltpu.roll(x, 8, 1))


def _rope(x, cos, sgn, bwd):
    return x * cos + (_swap8(x * sgn) if bwd else _swap8(x) * sgn)


def _matmul(a, b, *, mode, name, out_dtype=F32, tm=512, tn=512, tk=512, m=None, k=None,
            epilogue=None, extra=(), slots=None):
    if mode == "nn":
        m = a.shape[0] if m is None else m
        k = a.shape[1]
        n = N_DEV * b.shape[2] if slots == "b_cols" else b.shape[1]
    elif mode == "nt":
        m = a.shape[0] if m is None else m
        k = a.shape[1]
        n = b.shape[0]
    else:
        k = a.shape[0] if k is None else k
        m, n = a.shape[1], b.shape[1]
    tm, tn, tk = min(tm, m), min(tn, n), min(tk, k)
    if slots == "b_cols":
        tn = b.shape[2]
    if slots == "out":
        tn = n // N_DEV
    assert m % tm == 0 and n % tn == 0 and k % tk == 0, (name, m, n, k, tm, tn, tk)
    nk = k // tk
    dims = _DIMS[mode]
    a_spec = (pl.BlockSpec((tk, tm), lambda i, j, kk: (kk, i)) if mode == "tn"
              else pl.BlockSpec((tm, tk), lambda i, j, kk: (i, kk)))
    if slots == "b_cols":
        b_spec = pl.BlockSpec((None, tk, tn), lambda i, j, kk: (j, kk, 0))
    elif mode == "nt":
        b_spec = pl.BlockSpec((tn, tk), lambda i, j, kk: (j, kk))
    else:
        b_spec = pl.BlockSpec((tk, tn), lambda i, j, kk: (kk, j))
    tile = pl.BlockSpec((tm, tn), lambda i, j, kk: (i, j))
    if slots == "out":
        o_spec = pl.BlockSpec((None, tm, tn), lambda i, j, kk: (j, i, 0))
        o_shape = (N_DEV, m, tn)
    else:
        o_spec, o_shape = tile, (m, n)
    in_specs, args = [a_spec, b_spec], [a, b]
    if epilogue == "drelu2":
        in_specs.append(tile)
    args += list(extra)
    if epilogue == "relu2":
        out_shape = (jax.ShapeDtypeStruct(o_shape, BF16), jax.ShapeDtypeStruct(o_shape, BF16))
        out_specs = (o_spec, o_spec)
    else:
        out_shape = jax.ShapeDtypeStruct(o_shape, out_dtype)
        out_specs = o_spec
    n_in = len(args)
    n_out = 2 if epilogue == "relu2" else 1

    def body(*refs):
        a_ref, b_ref = refs[0], refs[1]
        outs = refs[n_in:n_in + n_out]
        def finish(acc, rows=slice(None)):
            if epilogue == "relu2":
                outs[0][rows, :] = acc.astype(BF16)
                r = jnp.maximum(acc, 0.0)
                outs[1][rows, :] = (r * r).astype(BF16)
            elif epilogue == "drelu2":
                u = refs[2][rows, :].astype(F32)
                outs[0][rows, :] = (acc * (2.0 * jnp.maximum(u, 0.0))).astype(out_dtype)
            else:
                outs[0][rows, :] = acc.astype(out_dtype)

        if nk == 1 and mode != "tn" and tm > MATMUL_ROW_CHUNK:
            for r0 in range(0, tm, MATMUL_ROW_CHUNK):
                rows = slice(r0, r0 + MATMUL_ROW_CHUNK)
                finish(lax.dot_general(a_ref[rows, :], b_ref[...], dims, preferred_element_type=F32), rows)
            return
        part = lax.dot_general(a_ref[...], b_ref[...], dims, preferred_element_type=F32)
        if nk == 1:
            finish(part)
        else:
            acc_ref = refs[n_in + n_out]
            kk = pl.program_id(2)

            @pl.when(kk == 0)
            def _():
                acc_ref[...] = part

            @pl.when(kk > 0)
            def _():
                acc_ref[...] += part

            @pl.when(kk == nk - 1)
            def _():
                finish(acc_ref[...])

    return pl.pallas_call(
        body, name=name, grid=(m // tm, n // tn, nk),
        out_shape=out_shape, in_specs=in_specs, out_specs=out_specs,
        scratch_shapes=[pltpu.VMEM((tm, tn), F32)] if nk > 1 else [],
        compiler_params=_params(("parallel", "parallel", "arbitrary"), VMEM_BIG),
    )(*args)


def _rstd(x):
    return lax.rsqrt(jnp.mean(x * x, axis=1, keepdims=True) + EPS)


def _norm_bwd(dxn, xn, r):
    return r * (dxn - xn * jnp.mean(dxn * xn, axis=1, keepdims=True))


def _vec(col):
    return pl.BlockSpec((1, D_MODEL), lambda i: (0, col))


def _matmul_rows(a, b, epi, *, mode, name, tm, tk, rows=(), vecs=(), out_dtypes=(), sums=False, slots=None,
                 riding=None):
    m, k = a.shape
    n = D_MODEL
    tm, tk = min(tm, m), min(tk, k)
    riding = riding or _Riding()
    group = 1
    if slots == "b_contract":
        group = max(1, tk // b.shape[2])
        tk = group * b.shape[2]
        b_spec = pl.BlockSpec((group, n, tk // group), lambda i, kk: (kk, 0, 0))
    elif mode == "nt":
        b_spec = pl.BlockSpec((n, tk), lambda i, kk: (0, kk))
    else:
        b_spec = pl.BlockSpec((tk, n), lambda i, kk: (kk, 0))
    assert m % tm == 0 and k % tk == 0, (name, m, k, tm, tk)
    ni, nk = m // tm, k // tk
    assert ni >= 2 or not isinstance(riding, _RidingReduce), "the two-level exchange needs a middle grid step"
    dims = _DIMS[mode]
    tile = pl.BlockSpec((tm, n), lambda i, kk: (i, 0))
    in_specs = [pl.BlockSpec((tm, tk), lambda i, kk: (i, kk)), b_spec] + [tile] * len(rows)
    in_specs += [pl.BlockSpec((1, n), lambda i, kk, col=col: (0, col)) for _, col in vecs]
    args = [a, b, *rows, *[v for v, _ in vecs]]
    out_shape = [jax.ShapeDtypeStruct((m, n), dt) for dt in out_dtypes]
    out_specs = [tile] * len(out_dtypes)
    if sums:
        out_shape.append(jax.ShapeDtypeStruct((8, n), F32))
        out_specs.append(pl.BlockSpec((8, n), lambda i, kk: (0, 0)))
    n_rows, n_vecs, n_outs, nr = len(rows), len(vecs), len(out_dtypes), riding.n
    n_in = 2 + n_rows + n_vecs

    def body(*refs):
        a_ref, b_ref = refs[0], refs[1]
        row_refs = refs[2:2 + n_rows]
        vec_refs = refs[2 + n_rows:n_in]
        x_refs = refs[n_in:n_in + nr]
        out_refs = refs[n_in + nr:n_in + nr + n_outs]
        pos = n_in + nr + n_outs
        sums_ref = refs[pos] if sums else None
        pos += 1 if sums else 0
        y_refs = refs[pos:pos + nr]
        pos += nr
        acc_ref = refs[pos] if nk > 1 else None
        sem_refs = refs[pos + (1 if nk > 1 else 0):]
        i, kk = pl.program_id(0), pl.program_id(1)
        state = riding.run((i == 0) & (kk == 0), (i == ni - 1) & (kk == nk - 1), x_refs, y_refs, sem_refs,
                           middle=(i == 1) & (kk == 0))
        if slots == "b_contract":
            c = tk // group
            part = lax.dot_general(a_ref[:, 0:c], b_ref[0], dims, preferred_element_type=F32)
            for u in range(1, group):
                part = part + lax.dot_general(a_ref[:, u * c:(u + 1) * c], b_ref[u], dims, preferred_element_type=F32)
        else:
            part = lax.dot_general(a_ref[...], b_ref[...], dims, preferred_element_type=F32)

        def finish(acc):
            nsub = tm // ROW_TILE
            for r in range(nsub):
                blk = pl.ds(r * ROW_TILE, ROW_TILE)
                epi(acc[r * ROW_TILE:(r + 1) * ROW_TILE], [ref.at[blk] for ref in row_refs], vec_refs,
                    [ref.at[blk] for ref in out_refs], sums_ref,
                    (i == 0) if r == 0 else None, (i == ni - 1) if r == nsub - 1 else None)

        if nk == 1:
            finish(part)
        else:
            @pl.when(kk == 0)
            def _():
                acc_ref[...] = part

            @pl.when(kk > 0)
            def _():
                acc_ref[...] += part

            @pl.when(kk == nk - 1)
            def _():
                finish(acc_ref)

        riding.finish(state)

    outs = pl.pallas_call(
        body, name=name, grid=(ni, nk),
        out_shape=(*out_shape, *riding.out_shape),
        in_specs=[*in_specs, *riding.specs], out_specs=(*out_specs, *riding.specs),
        scratch_shapes=([pltpu.VMEM((tm, n), F32)] if nk > 1 else []) + (riding.scratch if nr else []),
        compiler_params=_params(("arbitrary", "arbitrary"), VMEM_BIG),
    )(*args, *riding.arrays)
    n_own = len(out_shape)
    return list(outs[:n_own]), list(outs[n_own:])


def _zero_sums_at_start(sums_ref, first):
    if first is not None:
        @pl.when(first)
        def _():
            sums_ref[...] = jnp.zeros_like(sums_ref)


def _epi_resid_modulate(acc, rows, vecs, outs, sums_ref, first, last):
    (x_ref,), (g_ref, sh_ref, sc_ref) = rows, vecs
    x1 = x_ref[...] + g_ref[...] * acc
    outs[0][...] = acc
    outs[1][...] = x1
    outs[2][...] = (x1 * _rstd(x1) * (1.0 + sc_ref[...]) + sh_ref[...]).astype(BF16)


def _epi_final(acc, rows, vecs, outs, sums_ref, first, last):
    (x1_ref, t_ref), (g_ref, gf_ref) = rows, vecs
    d = acc.shape[1]
    x2 = x1_ref[...] + g_ref[...] * acc
    r = _rstd(x2)
    xn = x2 * r
    err = xn * gf_ref[...] - t_ref[...]
    dy = err * (1.0 / d)
    dx2 = _norm_bwd(dy * gf_ref[...], xn, r)
    outs[0][...] = dx2
    outs[1][...] = (dx2 * g_ref[...]).astype(BF16)
    _zero_sums_at_start(sums_ref, first)
    sums_ref[0:1, :] += jnp.sum(dy * xn, axis=0, keepdims=True)
    sums_ref[1:2, :] += jnp.sum(dx2 * acc, axis=0, keepdims=True)
    sums_ref[2:3, :] += jnp.sum(err * err, axis=0, keepdims=True)

    if last is not None:
        @pl.when(last)
        def _():
            tot = jnp.sum(sums_ref[2:3, :], axis=1, keepdims=True) * (0.5 / d)
            sums_ref[3:4, :] = jnp.broadcast_to(tot, (1, d))


def _epi_modulate2_bwd(acc, rows, vecs, outs, sums_ref, first, last):
    (x_ref, dres_ref, o_ref), (sc_ref, g_ref) = rows, vecs
    x = x_ref[...]
    r = _rstd(x)
    xn = x * r
    dx = dres_ref[...] + _norm_bwd(acc * (1.0 + sc_ref[...]), xn, r)
    outs[0][...] = dx
    outs[1][...] = (dx * g_ref[...]).astype(BF16)
    _zero_sums_at_start(sums_ref, first)
    sums_ref[0:1, :] += jnp.sum(acc * xn, axis=0, keepdims=True)
    sums_ref[1:2, :] += jnp.sum(acc, axis=0, keepdims=True)
    sums_ref[2:3, :] += jnp.sum(dx * o_ref[...], axis=0, keepdims=True)


def _epi_modulate1_bwd(acc, rows, vecs, outs, sums_ref, first, last):
    (add_ref, x_ref, dres_ref), (sc_ref,) = rows, vecs
    dh = acc + add_ref[...]
    x = x_ref[...]
    r = _rstd(x)
    xn = x * r
    outs[0][...] = dres_ref[...] + _norm_bwd(dh * (1.0 + sc_ref[...]), xn, r)
    _zero_sums_at_start(sums_ref, first)
    sums_ref[0:1, :] += jnp.sum(dh * xn, axis=0, keepdims=True)
    sums_ref[1:2, :] += jnp.sum(dh, axis=0, keepdims=True)


def _modulate_all(x, ctx, mod, mod_ctx, riding, name):
    s, d = x.shape
    t = s + ctx.shape[0]
    ns = s // ROW_TILE
    nc = ctx.shape[0] // ROW_TILE
    nr = riding.n

    def body(*refs):
        x_ref, c_ref, sh_ref, sc_ref, shc_ref, scc_ref = refs[:6]
        h_ref = refs[6 + nr]
        i = pl.program_id(0)
        state = riding.run(i == 0, i == ns + nc - 1, refs[6:6 + nr], refs[7 + nr:7 + 2 * nr], refs[7 + 2 * nr:],
                           middle=i == ns + nc - 3)

        @pl.when(i < ns)
        def _():
            v = x_ref[...]
            h_ref[...] = (v * _rstd(v) * (1.0 + sc_ref[...]) + sh_ref[...]).astype(BF16)

        @pl.when(i >= ns)
        def _():
            v = c_ref[...]
            h_ref[...] = (v * _rstd(v) * (1.0 + scc_ref[...]) + shc_ref[...]).astype(BF16)

        riding.finish(state)

    outs = pl.pallas_call(
        body, name=name, grid=(ns + nc,),
        out_shape=(jax.ShapeDtypeStruct((t, d), BF16), *riding.out_shape),
        in_specs=[pl.BlockSpec((ROW_TILE, d), lambda i: (jnp.minimum(i, ns - 1), 0)),
                  pl.BlockSpec((ROW_TILE, d), lambda i: (jnp.maximum(i - ns, 0), 0)),
                  _vec(0), _vec(1), _vec(0), _vec(1), *riding.specs],
        out_specs=(pl.BlockSpec((ROW_TILE, d), lambda i: (i, 0)), *riding.specs),
        scratch_shapes=riding.scratch,
        compiler_params=_params(("arbitrary",)),
    )(x, ctx, mod, mod, mod_ctx, mod_ctx, *riding.arrays)
    return outs[0], list(outs[1:])


def _modulate_sums(dh, row_off, xsrc):
    s, d = xsrc.shape

    def body(dh_ref, x_ref, sums_ref):
        i = pl.program_id(0)
        x = x_ref[...]
        dhv = dh_ref[...]

        @pl.when(i == 0)
        def _():
            sums_ref[...] = jnp.zeros_like(sums_ref)

        sums_ref[0:1, :] += jnp.sum(dhv * (x * _rstd(x)), axis=0, keepdims=True)
        sums_ref[1:2, :] += jnp.sum(dhv, axis=0, keepdims=True)

    return pl.pallas_call(
        body, name="modulate1_ctx_bwd", grid=(s // ROW_TILE,),
        out_shape=jax.ShapeDtypeStruct((8, d), F32),
        in_specs=[pl.BlockSpec((ROW_TILE, d), lambda i: (i + row_off, 0)), pl.BlockSpec((ROW_TILE, d), lambda i: (i, 0))],
        out_specs=pl.BlockSpec((8, d), lambda i: (0, 0)),
        compiler_params=_params(("arbitrary",)),
    )(dh, xsrc)


def _head_fwd(h_all, win_head, wq, wk, q_gain, kv_gain, cos, sgn, tm, name):
    t, d = h_all.shape
    nq, nkv = wq.shape[1], wk.shape[1]

    def body(h_ref, wi_ref, wq_ref, wk_ref, qg_ref, kg_ref, c_ref, s_ref, z_ref, cq_ref, kvin_ref, qf_ref, kv_ref):
        z = lax.dot_general(h_ref[...], wi_ref[...], NT_DIMS, preferred_element_type=F32)
        z_ref[...] = z
        cos, sgn = c_ref[...], s_ref[...]
        zq = z[:, 0:Q_RANK]
        cq = (zq * _rstd(zq) * qg_ref[...]).astype(BF16)
        cq_ref[...] = cq
        zk = z[:, Q_RANK:Q_RANK + KV_RANK]
        kv_in = jnp.concatenate([(zk * _rstd(zk) * kg_ref[...]).astype(BF16),
                                 _rope(z[:, Q_RANK + KV_RANK:HEAD_COLS], cos, sgn, False).astype(BF16)], axis=1)
        kvin_ref[...] = kv_in
        q = jnp.dot(cq, wq_ref[...], preferred_element_type=F32)
        for h in range(nq // LANES):
            sl = slice(h * LANES, (h + 1) * LANES)
            qf_ref[:, sl] = _rope(q[:, sl], cos, sgn, False).astype(BF16)
        kv_ref[...] = jnp.dot(kv_in, wk_ref[...], preferred_element_type=F32).astype(BF16)

    def row(w):
        return pl.BlockSpec((tm, w), lambda i: (i, 0))

    def whole(a):
        return pl.BlockSpec(a.shape, lambda i: (0, 0))

    return pl.pallas_call(
        body, name=name, grid=(t // tm,),
        out_shape=(jax.ShapeDtypeStruct((t, HEAD_COLS), F32), jax.ShapeDtypeStruct((t, Q_RANK), BF16),
                   jax.ShapeDtypeStruct((t, KV_RANK + LANES), BF16), jax.ShapeDtypeStruct((t, nq), BF16),
                   jax.ShapeDtypeStruct((t, nkv), BF16)),
        in_specs=[row(d), whole(win_head), whole(wq), whole(wk), whole(q_gain), whole(kv_gain), row(LANES), row(LANES)],
        out_specs=(row(HEAD_COLS), row(Q_RANK), row(KV_RANK + LANES), row(nq), row(nkv)),
        compiler_params=_params(("parallel",), VMEM_BIG),
    )(h_all, win_head, wq, wk, q_gain, kv_gain, cos, sgn)


def _head_bwd(dq, dk, dv, z, wq, wk_k, wk_v, win_head, q_gain, kv_gain, cos, sgn, cq, kv_in, h_all, name, *, tile,
              first_block, n_blocks, carry=None):
    t = z.shape[0]
    with_q = dq is not None

    def body(*refs):
        it = iter(refs)
        dq_ref = next(it) if with_q else None
        dk_ref, dv_ref, z_ref, wq_ref, wkk_ref, wkv_ref, wi_ref, qg_ref, kg_ref, c_ref, s_ref = (next(it) for _ in range(11))
        cq_ref = next(it) if with_q else None
        kvin_ref, h_ref = next(it), next(it)
        before = None
        if carry is not None:
            next(it), next(it)
            before = (next(it), next(it), next(it))
        dz_ref, dh_ref, sums_ref = next(it), next(it), next(it)
        gq_ref = next(it) if with_q else None
        grads = (next(it), next(it), next(it))
        i = pl.program_id(0)

        @pl.when(i == 0)
        def _():
            sums_ref[...] = jnp.zeros_like(sums_ref)
            if with_q:
                gq_ref[...] = jnp.zeros_like(gq_ref)
            for k, g_ref in enumerate(grads):
                g_ref[...] = jnp.zeros_like(g_ref) if before is None else before[k][...]

        if with_q:
            gq_ref[...] += lax.dot_general(cq_ref[...], dq_ref[...], TN_DIMS, preferred_element_type=F32)
        grads[0][...] += lax.dot_general(kvin_ref[...], dk_ref[...], TN_DIMS, preferred_element_type=F32)
        grads[1][...] += lax.dot_general(kvin_ref[...], dv_ref[...], TN_DIMS, preferred_element_type=F32)
        if with_q:
            dc = lax.dot_general(dq_ref[...], wq_ref[...], NT_DIMS, preferred_element_type=F32)
            zq = z_ref[:, 0:Q_RANK]
            r = _rstd(zq)
            zn = zq * r
            sums_ref[0:1, :] += jnp.sum(dc * zn, axis=0, keepdims=True)
            dz_ref[:, 0:Q_RANK] = _norm_bwd(dc * qg_ref[...], zn, r).astype(BF16)
        else:
            dz_ref[:, 0:Q_RANK] = jnp.zeros((tile, Q_RANK), BF16)
        dkv = (lax.dot_general(dk_ref[...], wkk_ref[...], NT_DIMS, preferred_element_type=F32)
               + lax.dot_general(dv_ref[...], wkv_ref[...], NT_DIMS, preferred_element_type=F32))
        zk = z_ref[:, Q_RANK:Q_RANK + KV_RANK]
        r = _rstd(zk)
        zn = zk * r
        dc = dkv[:, 0:KV_RANK]
        sums_ref[1:2, 0:KV_RANK] += jnp.sum(dc * zn, axis=0, keepdims=True)
        dz_ref[:, Q_RANK:Q_RANK + KV_RANK] = _norm_bwd(dc * kg_ref[...], zn, r).astype(BF16)
        dz_ref[:, Q_RANK + KV_RANK:HEAD_COLS] = _rope(dkv[:, KV_RANK:KV_RANK + LANES], c_ref[...], s_ref[...],
                                                       True).astype(BF16)
        dh_ref[...] = jnp.dot(dz_ref[...], wi_ref[...], preferred_element_type=F32)
        grads[2][...] += lax.dot_general(dz_ref[...], h_ref[...], TN_DIMS, preferred_element_type=F32)

    def row(w):
        return pl.BlockSpec((tile, w), lambda i: (i + first_block, 0))

    def whole(a):
        return pl.BlockSpec(a.shape, lambda i: (0, 0))

    args = ([dq] if with_q else []) + [dk, dv, z, wq, wk_k, wk_v, win_head, q_gain, kv_gain, cos, sgn]
    args += ([cq] if with_q else []) + [kv_in, h_all]
    in_specs = ([row(dq.shape[1])] if with_q else []) + [
        row(dk.shape[1]), row(dv.shape[1]), row(HEAD_COLS), whole(wq), whole(wk_k), whole(wk_v),
        whole(win_head), whole(q_gain), whole(kv_gain), row(LANES), row(LANES)]
    in_specs += ([row(Q_RANK)] if with_q else []) + [row(kv_in.shape[1]), row(D_MODEL)]
    aliases = {}
    if carry is not None:
        aliases = {len(args): 0, len(args) + 1: 1}
        args += list(carry)
        in_specs += [pl.BlockSpec(memory_space=pl.ANY)] * 2 + [whole(a) for a in carry[2:]]
    grad_shapes = ([(Q_RANK, dq.shape[1])] if with_q else []) + [
        (kv_in.shape[1], dk.shape[1]), (kv_in.shape[1], dv.shape[1]), (HEAD_COLS, D_MODEL)]
    return pl.pallas_call(
        body, name=name, grid=(n_blocks,),
        out_shape=(jax.ShapeDtypeStruct((t, HEAD_COLS), BF16), jax.ShapeDtypeStruct((t, D_MODEL), F32),
                   jax.ShapeDtypeStruct((8, Q_RANK), F32), *[jax.ShapeDtypeStruct(g, F32) for g in grad_shapes]),
        in_specs=in_specs,
        out_specs=(row(HEAD_COLS), row(D_MODEL), pl.BlockSpec((8, Q_RANK), lambda i: (0, 0)),
                   *[pl.BlockSpec(g, lambda i: (0, 0)) for g in grad_shapes]),
        input_output_aliases=aliases,
        compiler_params=_params(("arbitrary",), VMEM_BIG),
    )(*args)


def _out_proj_bwd(a, dy, w, name, tm=1024):
    s, k = a.shape
    n = dy.shape[1]

    def body(a_ref, dy_ref, w_ref, da_ref, dw_ref, acc_ref):
        i = pl.program_id(0)
        da_ref[...] = lax.dot_general(dy_ref[...], w_ref[...], NT_DIMS, preferred_element_type=F32)
        part = lax.dot_general(a_ref[...], dy_ref[...], TN_DIMS, preferred_element_type=F32)

        @pl.when(i == 0)
        def _():
            acc_ref[...] = part

        @pl.when(i > 0)
        def _():
            acc_ref[...] += part

        @pl.when(i == pl.num_programs(0) - 1)
        def _():
            dw_ref[...] = acc_ref[...].astype(BF16)

    return pl.pallas_call(
        body, name=name, grid=(s // tm,),
        out_shape=(jax.ShapeDtypeStruct((s, k), F32), jax.ShapeDtypeStruct((k, n), BF16)),
        in_specs=[pl.BlockSpec((tm, k), lambda i: (i, 0)), pl.BlockSpec((tm, n), lambda i: (i, 0)),
                  pl.BlockSpec(w.shape, lambda i: (0, 0))],
        out_specs=(pl.BlockSpec((tm, k), lambda i: (i, 0)), pl.BlockSpec((k, n), lambda i: (0, 0))),
        scratch_shapes=[pltpu.VMEM((k, n), F32)],
        compiler_params=_params(("arbitrary",), VMEM_BIG),
    )(a, dy, w)


def _shift_rows(u, s):
    rowi = lax.broadcasted_iota(jnp.int32, u.shape, 0)
    prev = jnp.where(rowi == 0, 0.0, pltpu.roll(u, 1, 0))
    nxt = jnp.where(rowi == s - 1, 0.0, pltpu.roll(u, s - 1, 0))
    return prev, nxt


def _conv_fwd(z_conv, cw, a_cat, name):
    s = z_conv.shape[0]

    def body(z_ref, w_ref, a_in_ref, o_ref):
        del a_in_ref
        gb, gc, xv = z_ref[:, 0:LANES], z_ref[:, LANES:2 * LANES], z_ref[:, 2 * LANES:3 * LANES]
        u = gc * xv
        prev, nxt = _shift_rows(u, s)
        y = w_ref[0:1, :] * prev + w_ref[1:2, :] * u + w_ref[2:3, :] * nxt
        o_ref[...] = (gb * y).astype(BF16)

    return pl.pallas_call(
        body, name=name, grid=(CONV_W // LANES,),
        out_shape=jax.ShapeDtypeStruct(a_cat.shape, a_cat.dtype),
        in_specs=[pl.BlockSpec((s, 3 * LANES), lambda j: (0, j)), pl.BlockSpec((3, LANES), lambda j: (0, j)),
                  pl.BlockSpec(memory_space=pl.ANY)],
        out_specs=pl.BlockSpec((s, LANES), lambda j: (0, 4 + j)),
        input_output_aliases={2: 0},
        compiler_params=_params(("parallel",), VMEM_BIG),
    )(z_conv, cw, a_cat)


def _conv_bwd(z_conv, cw, da, h_all, name):
    s = z_conv.shape[0]

    def body(z_ref, w_ref, da_ref, h_ref, dz_ref, dw_ref, g_ref):
        gb, gc, xv = z_ref[:, 0:LANES], z_ref[:, LANES:2 * LANES], z_ref[:, 2 * LANES:3 * LANES]
        u = gc * xv
        prev, nxt = _shift_rows(u, s)
        dcv = da_ref[...]
        dz_ref[:, 0:LANES] = (dcv * (w_ref[0:1, :] * prev + w_ref[1:2, :] * u + w_ref[2:3, :] * nxt)).astype(BF16)
        dy = dcv * gb
        dw_ref[0:1, :] = jnp.sum(dy * prev, axis=0, keepdims=True)
        dw_ref[1:2, :] = jnp.sum(dy * u, axis=0, keepdims=True)
        dw_ref[2:3, :] = jnp.sum(dy * nxt, axis=0, keepdims=True)
        dyp, dyn = _shift_rows(dy, s)
        du = w_ref[0:1, :] * dyn + w_ref[1:2, :] * dy + w_ref[2:3, :] * dyp
        dz_ref[:, LANES:2 * LANES] = (du * xv).astype(BF16)
        dz_ref[:, 2 * LANES:3 * LANES] = (du * gc).astype(BF16)
        g_ref[...] = lax.dot_general(dz_ref[...], h_ref[...], TN_DIMS, preferred_element_type=F32)

    d = h_all.shape[1]
    blk = pl.BlockSpec((s, 3 * LANES), lambda j: (0, j))
    cws = pl.BlockSpec((3, LANES), lambda j: (0, j))
    return pl.pallas_call(
        body, name=name, grid=(CONV_W // LANES,),
        out_shape=(jax.ShapeDtypeStruct(z_conv.shape, BF16), jax.ShapeDtypeStruct((3, CONV_W), F32),
                   jax.ShapeDtypeStruct((z_conv.shape[1], d), F32)),
        in_specs=[blk, cws, pl.BlockSpec((s, LANES), lambda j: (0, 4 + j)), pl.BlockSpec((s, d), lambda j: (0, 0))],
        out_specs=(blk, cws, pl.BlockSpec((3 * LANES, d), lambda j: (j, 0))),
        compiler_params=_params(("parallel",), VMEM_BIG),
    )(z_conv, cw, da, h_all)


ATT_TQ = 512
ATT_Q_STEP = 1024
ATT_TQ_BWD = 512


def _head_mask(shape, hh):
    lane = lax.broadcasted_iota(jnp.int32, shape, 1)
    return (lane >= hh * V_DIM) & (lane < (hh + 1) * V_DIM)


def _attn_fwd(qf, kv, s, riding, name):
    t = kv.shape[0]
    step = min(ATT_Q_STEP, s)
    nq = s // step
    nr = riding.n

    def body(*refs):
        q_ref, k_ref, v_ref = refs[:3]
        o_ref, ob_ref, st_ref = refs[3 + nr:6 + nr]
        p, i = pl.program_id(0), pl.program_id(1)
        state = riding.run((p == 0) & (i == 0), (p == N_HEADS // 2 - 1) & (i == nq - 1),
                           refs[3:3 + nr], refs[6 + nr:6 + 2 * nr], refs[6 + 2 * nr:],
                           middle=(p == N_HEADS // 2 - 2) & (i == nq // 2))
        v = v_ref[...]
        vlane = lax.broadcasted_iota(jnp.int32, v.shape, 1)
        one_lane = [(1 - hh) * V_DIM for hh in range(2)]
        vm = [jnp.where(_head_mask(v.shape, hh), v, jnp.where(vlane == one_lane[hh], 1.0, 0.0).astype(BF16))
              for hh in range(2)]

        def block(r, carry):
            rows = pl.ds(pl.multiple_of(r * ATT_TQ, ATT_TQ), ATT_TQ)
            olane = lax.broadcasted_iota(jnp.int32, (ATT_TQ, LANES), 1)
            acc = jnp.zeros((ATT_TQ, LANES), F32)
            stat = jnp.zeros((ATT_TQ, LANES), F32)
            scores = [lax.dot_general(q_ref[rows, hh * LANES:(hh + 1) * LANES], k_ref[:, hh * LANES:(hh + 1) * LANES],
                                      NT_DIMS, preferred_element_type=F32) for hh in range(2)]
            maxes = [jnp.max(sc, axis=1, keepdims=True) for sc in scores]
            exps = [jnp.exp2((sc - mx) * EXP2_SCALE).astype(BF16) for sc, mx in zip(scores, maxes)]
            for hh in range(2):
                mx = maxes[hh]
                res = jnp.dot(exps[hh], vm[hh], preferred_element_type=F32)
                den = jnp.sum(jnp.where(olane == one_lane[hh], res, 0.0), axis=1, keepdims=True)
                acc = acc + jnp.where(_head_mask(res.shape, hh), res * (1.0 / den), 0.0)
                stat = stat + jnp.where(olane == hh, mx * EXP2_SCALE + jnp.log(den) * LOG2_E, 0.0)
            o_ref[rows, :] = acc
            ob_ref[rows, :] = acc.astype(BF16)
            st_ref[:, rows] = stat.T[0:8, :]
            return carry

        lax.fori_loop(0, step // ATT_TQ, block, 0)
        riding.finish(state)

    o_spec = pl.BlockSpec((step, LANES), lambda p, i: (i, p))
    outs = pl.pallas_call(
        body, name=name, grid=(N_HEADS // 2, nq),
        out_shape=(jax.ShapeDtypeStruct((s, N_HEADS * V_DIM), F32),
                   jax.ShapeDtypeStruct((s, D_MODEL), BF16),
                   jax.ShapeDtypeStruct((N_HEADS // 2 * 8, s), F32), *riding.out_shape),
        in_specs=[pl.BlockSpec((step, 2 * LANES), lambda p, i: (i, p)),
                  pl.BlockSpec((t, 2 * LANES), lambda p, i: (0, p)),
                  pl.BlockSpec((t, LANES), lambda p, i: (0, N_HEADS + p)), *riding.specs],
        out_specs=(o_spec, o_spec, pl.BlockSpec((8, step), lambda p, i: (p, i)), *riding.specs),
        scratch_shapes=riding.scratch,
        compiler_params=_params(("arbitrary", "arbitrary"), VMEM_BIG),
    )(qf, kv, kv, *riding.arrays)
    return outs[0], outs[1], outs[2], list(outs[3:])


def _attn_bwd(qf, kv, o, da, stats, cos, sgn, riding, name):
    s, t = o.shape[0], kv.shape[0]
    ATT_TQ = ATT_TQ_BWD
    nq = s // ATT_TQ
    nr = riding.n

    def body(*refs):
        q_ref, k_ref, v_ref, o_ref, do_ref, st_ref, c_ref, s_ref = refs[:8]
        dq_ref, dk_ref, dv_ref = refs[8 + nr:11 + nr]
        dk_acc, dv_acc = refs[11 + 2 * nr:13 + 2 * nr]
        p, i = pl.program_id(0), pl.program_id(1)
        state = riding.run((p == 0) & (i == 0), (p == N_HEADS // 2 - 1) & (i == nq - 1),
                           refs[8:8 + nr], refs[11 + nr:11 + 2 * nr], refs[13 + 2 * nr:])

        @pl.when(i == 0)
        def _():
            dk_acc[...] = jnp.zeros_like(dk_acc)
            dv_acc[...] = jnp.zeros_like(dv_acc)

        v = v_ref[...]
        do = do_ref[...]
        od = do * o_ref[...]
        ones = jnp.ones((8, LANES), F32)
        for hh in range(2):
            sl = slice(hh * LANES, (hh + 1) * LANES)
            q, k = q_ref[:, sl], k_ref[:, sl]
            mask = _head_mask(do.shape, hh)
            dom = jnp.where(mask, do, 0.0).astype(BF16)
            delta = lax.dot_general(ones, jnp.where(mask, od, 0.0), NT_DIMS, preferred_element_type=F32,
                                    precision=lax.Precision.HIGHEST)[0:1, :]
            st = lax.dot_general(k, q, NT_DIMS, preferred_element_type=F32)
            pt = jnp.exp2(st * EXP2_SCALE - st_ref[hh:hh + 1, :]).astype(BF16)
            dpt = lax.dot_general(v, dom, NT_DIMS, preferred_element_type=F32)
            dst = (pt.astype(F32) * (dpt - delta)).astype(BF16)
            dv_acc[...] += jnp.dot(pt, dom, preferred_element_type=F32)
            dk_acc[:, sl] += jnp.dot(dst, q, preferred_element_type=F32)
            dq = lax.dot_general(dst, k, TN_DIMS, preferred_element_type=F32) * ATTN_SCALE
            dq_ref[:, sl] = _rope(dq, c_ref[...], s_ref[...], True).astype(BF16)

        @pl.when(i == nq - 1)
        def _():
            dk_ref[...] = (dk_acc[...] * ATTN_SCALE).astype(BF16)
            dv_ref[...] = dv_acc[...].astype(BF16)

        riding.finish(state)

    o_spec = pl.BlockSpec((ATT_TQ, LANES), lambda p, i: (i, p))
    tab = pl.BlockSpec((ATT_TQ, LANES), lambda p, i: (i, 0))
    outs = pl.pallas_call(
        body, name=name, grid=(N_HEADS // 2, nq),
        out_shape=(jax.ShapeDtypeStruct((s, N_HEADS * LANES), BF16),
                   jax.ShapeDtypeStruct((t, N_HEADS * LANES), BF16),
                   jax.ShapeDtypeStruct((t, N_HEADS * V_DIM), BF16), *riding.out_shape),
        in_specs=[pl.BlockSpec((ATT_TQ, 2 * LANES), lambda p, i: (i, p)),
                  pl.BlockSpec((t, 2 * LANES), lambda p, i: (0, p)),
                  pl.BlockSpec((t, LANES), lambda p, i: (0, N_HEADS + p)),
                  o_spec, o_spec,
                  pl.BlockSpec((8, ATT_TQ), lambda p, i: (p, i)), tab, tab, *riding.specs],
        out_specs=(pl.BlockSpec((ATT_TQ, 2 * LANES), lambda p, i: (i, p)),
                   pl.BlockSpec((t, 2 * LANES), lambda p, i: (0, p)),
                   pl.BlockSpec((t, LANES), lambda p, i: (0, p)), *riding.specs),
        scratch_shapes=[pltpu.VMEM((t, 2 * LANES), F32), pltpu.VMEM((t, LANES), F32), *riding.scratch],
        compiler_params=_params(("arbitrary", "arbitrary"), VMEM_BIG),
    )(qf, kv, kv, o, da, stats, cos, sgn, *riding.arrays)
    return outs[0], outs[1], outs[2], list(outs[3:])


def _silu(x):
    return x * (1.0 / (1.0 + jnp.exp(-x)))


def _prologue(c_rows, c_ctx, w_mod, b_cols, extra_rows, name):
    d, cols = c_rows.shape[1], w_mod.shape[1]

    def body(c_ref, cctx_ref, wmod_ref, b_ref, x_ref, a_ref, modg_ref, c_all, blk, c_send, c_recv, m_send, m_recv):
        _direct_gather(c_ref, c_all, c_send, c_recv)()
        a_ref[...] = jnp.zeros_like(a_ref)
        for j in range(N_DEV):
            a_ref[j:j + 1, :] = c_all[j, 0:1, :]
        a_ref[N_DEV:N_DEV + 1, :] = cctx_ref[...]
        mod = jnp.dot(_silu(a_ref[...]), wmod_ref[...], preferred_element_type=F32,
                      precision=lax.Precision.HIGHEST) + b_ref[...]
        blk[...] = jnp.zeros_like(blk)
        for p in range(N_DEV):
            blk[p, 0:1, :] = mod[p:p + 1, :]
            blk[p, 1:2, :] = mod[N_DEV:N_DEV + 1, :]
            blk[p, 2:5, :] = x_ref[...]
        _direct_gather(blk, modg_ref, m_send, m_recv, per_peer=True)()

    vmem = pl.BlockSpec(memory_space=pltpu.VMEM)
    return pl.pallas_call(
        body, name=name,
        out_shape=(jax.ShapeDtypeStruct((16, d), F32), jax.ShapeDtypeStruct((N_DEV, 8, cols), F32)),
        in_specs=[vmem] * 5, out_specs=(vmem, vmem),
        scratch_shapes=[pltpu.VMEM((N_DEV, 8, d), F32), pltpu.VMEM((N_DEV, 8, cols), F32)]
        + [pltpu.SemaphoreType.DMA((7,)) for _ in range(4)],
        compiler_params=_params(None, VMEM_BIG),
    )(c_rows, c_ctx, w_mod, b_cols, extra_rows)


def _adaln_bwd(a_t, w, d_ex, d_ctx, d_all, name):
    def body(at_ref, w_ref, dex_ref, dctx_ref, dall_ref, gw_ref, dsil_ref, dsum_ref):
        sil_t = _silu(at_ref[...])
        dctx = dctx_ref[...]
        row = dctx[0:1, :]
        for j in range(1, N_DEV):
            row = row + dctx[j:j + 1, :]
        rowi = lax.broadcasted_iota(jnp.int32, dctx.shape, 0)
        ctx_rows = jnp.where(rowi == 0, jnp.broadcast_to(row, dctx.shape), 0.0)
        hi = lax.Precision.HIGHEST
        d_rows = jnp.concatenate([dex_ref[...], ctx_rows], axis=0)
        gw_ref[...] = jnp.dot(sil_t, d_rows, preferred_element_type=F32, precision=hi)
        dsil_ref[...] = lax.dot_general(ctx_rows, w_ref[...], NT_DIMS, preferred_element_type=F32, precision=hi)
        tot = dall_ref[0]
        for j in range(1, N_DEV):
            tot = tot + dall_ref[j]
        dsum_ref[...] = tot

    return pl.pallas_call(
        body, name=name,
        out_shape=(jax.ShapeDtypeStruct(w.shape, F32), jax.ShapeDtypeStruct((8, w.shape[0]), F32),
                   jax.ShapeDtypeStruct(d_all.shape[1:], F32)),
        compiler_params=_params(None, VMEM_BIG),
    )(a_t, w, d_ex, d_ctx, d_all)


SMALL_ROWS = 24
SMALL_MISC, SMALL_CW, SMALL_LOSS = 16, 18, 21


def _pack_small(sums1, sums2, fsums, sums1c, psums, psums_c, d_cw, cols, name):
    d = D_MODEL

    def body(s1_ref, s2_ref, f_ref, s1c_ref, p_ref, pc_ref, cw_ref, o_ref):
        o_ref[...] = jnp.zeros_like(o_ref)

        def blocks(row0, pieces):
            for j in range(N_DEV):
                lo, hi = j * cols, (j + 1) * cols
                for k, (ref, r) in enumerate(pieces):
                    a, b = max(lo, k * d), min(hi, (k + 1) * d)
                    if a < b:
                        o_ref[row0 + j:row0 + j + 1, a - lo:b - lo] = ref[r:r + 1, a - k * d:b - k * d]

        blocks(0, [(s1_ref, 1), (s1_ref, 0), (s2_ref, 2), (s2_ref, 1), (s2_ref, 0), (f_ref, 1)])
        blocks(N_DEV, [(s1c_ref, 1), (s1c_ref, 0)])
        head = Q_RANK + KV_RANK
        o_ref[SMALL_MISC:SMALL_MISC + 1, 0:Q_RANK] = p_ref[0:1, :]
        o_ref[SMALL_MISC:SMALL_MISC + 1, Q_RANK:head] = p_ref[1:2, 0:KV_RANK] + pc_ref[1:2, 0:KV_RANK]
        o_ref[SMALL_MISC:SMALL_MISC + 1, head:cols] = f_ref[0:1, 0:cols - head]
        o_ref[SMALL_MISC + 1:SMALL_MISC + 2, 0:d - (cols - head)] = f_ref[0:1, cols - head:d]
        for r in range(3):
            o_ref[SMALL_CW + r:SMALL_CW + r + 1, 0:CONV_W] = cw_ref[r:r + 1, :]
        o_ref[SMALL_LOSS:SMALL_LOSS + 1, :] = f_ref[3:4, 0:cols]

    return pl.pallas_call(body, name=name, out_shape=jax.ShapeDtypeStruct((SMALL_ROWS, cols), F32))(
        sums1, sums2, fsums, sums1c, psums, psums_c, d_cw)


def _adam_math(w, g, m, v):
    nm = ADAM_B1 * m + (1.0 - ADAM_B1) * g
    nv = ADAM_B2 * v + (1.0 - ADAM_B2) * (g * g)
    m_hat = nm / (1.0 - ADAM_B1 ** ADAM_STEP)
    v_hat = nv / (1.0 - ADAM_B2 ** ADAM_STEP)
    return -ADAM_LR * (m_hat / (jnp.sqrt(v_hat) + ADAM_EPS) + ADAM_WD * w), nm, nv


def _small_update(dsum, dsil_all, g_cw, params, name):
    d = D_MODEL
    n = len(params)
    cols = dsum.shape[1]

    def body(*refs):
        dsum_ref, dsil_ref, gcw_ref = refs[:3]
        wmv = refs[3:3 + 3 * n]
        outs = refs[3 + 3 * n:]
        tot = dsil_ref[0]
        for j in range(1, N_DEV):
            tot = tot + dsil_ref[j]
        cv = wmv[0][...]
        sg = 1.0 / (1.0 + jnp.exp(-cv))
        off = Q_RANK + KV_RANK
        misc = dsum_ref[SMALL_MISC:SMALL_MISC + 1, :]
        grads = [tot[0:1, :] * (sg * (1.0 + cv * (1.0 - sg))),
                 jnp.concatenate([dsum_ref[j:j + 1, :] + dsum_ref[N_DEV + j:N_DEV + j + 1, :] for j in range(N_DEV)],
                                 axis=1),
                 misc[:, 0:Q_RANK], misc[:, Q_RANK:off],
                 jnp.concatenate([misc[:, off:cols], dsum_ref[SMALL_MISC + 1:SMALL_MISC + 2, 0:d - (cols - off)]],
                                 axis=1),
                 gcw_ref[...]]
        for p, g in enumerate(grads):
            w_ref, m_ref, v_ref = wmv[3 * p:3 * p + 3]
            at = 0 if len(w_ref.shape) == 3 else Ellipsis
            res = (g,) + _adam_math(w_ref[at], g, m_ref[at], v_ref[at])
            for q, val in enumerate(res):
                outs[4 * p + q][at] = val

    flat = [a for wmv in params for a in wmv]
    out_shape = tuple(jax.ShapeDtypeStruct(wmv[0].shape, F32) for wmv in params for _ in range(4))
    outs = pl.pallas_call(body, name=name, out_shape=out_shape)(dsum, dsil_all, g_cw, *flat)
    return [outs[4 * p:4 * p + 4] for p in range(n)]


def _adamw(w, g, m, v, name, slots=False):
    _, rows, cols = w.shape
    tr = _pick(rows, (256, 128, 64, 32, 16, 8))

    def body(w_ref, g_ref, m_ref, v_ref, *outs):
        if slots:
            gv = g_ref[0].astype(F32)
            for j in range(1, g.shape[0]):
                gv = gv + g_ref[j].astype(F32)
            outs[0][...] = gv
        else:
            gv = g_ref[...]
        d_ref, nm_ref, nv_ref = outs[-3:]
        d_ref[...], nm_ref[...], nv_ref[...] = _adam_math(w_ref[...], gv, m_ref[...], v_ref[...])

    blk = pl.BlockSpec((None, tr, cols), lambda i: (0, i, 0))
    g_spec = (pl.BlockSpec((g.shape[0], tr, cols), lambda i: (0, i, 0)) if slots
              else pl.BlockSpec((tr, cols), lambda i: (i, 0)))
    sh = jax.ShapeDtypeStruct((1, rows, cols), F32)
    n_out = 4 if slots else 3
    return pl.pallas_call(
        body, name=name, grid=(rows // tr,), out_shape=(sh,) * n_out,
        in_specs=[blk, g_spec, blk, blk], out_specs=(blk,) * n_out,
        compiler_params=_params(("parallel",), VMEM_BIG),
    )(w, g, m, v)


def _rope_tables(s, l):
    tok = np.arange(s)
    row = (tok // GRID_W).astype(np.float32)
    col = (tok % GRID_W).astype(np.float32)
    half = QK_ROPE // 2
    freqs = np.float32(ROPE_THETA) ** (-np.arange(0, half, 2, dtype=np.float32) / np.float32(half))
    dd = np.arange(QK_ROPE)
    pos = np.where((dd // half)[None, :] == 0, row[:, None], col[:, None]).astype(np.float32)
    ang = (pos * freqs[dd % (half // 2)][None, :]).astype(np.float32)
    sin = np.sin(ang).astype(np.float32)
    cos_t = np.ones((s + l, LANES), np.float32)
    sgn_t = np.zeros((s + l, LANES), np.float32)
    cos_t[:s, QK_NOPE:QK_NOPE + QK_ROPE] = np.cos(ang)
    sgn_t[:s, QK_NOPE:QK_NOPE + QK_ROPE] = np.where(((dd % half) // (half // 2))[None, :] == 0, -sin, sin)
    return jnp.asarray(cos_t), jnp.asarray(sgn_t)


def _slots_to_cols(g):
    return g.transpose(1, 0, 2).reshape(g.shape[1], N_DEV * g.shape[2])


def _cols_to_slots(w):
    return w.reshape(w.shape[0], N_DEV, w.shape[1] // N_DEV).transpose(1, 0, 2)


def _unpack_small_weights(g_in_t, g_uq, g_ukv):
    w_t = g_in_t.reshape(N_DEV * g_in_t.shape[1], D_MODEL)
    zeros = jnp.zeros((QK_NOPE, D_MODEL), BF16)
    win_head_t = jnp.concatenate([w_t[:Q_RANK + KV_RANK], zeros, w_t[Q_RANK + KV_RANK:MLA_IN],
                                  zeros[:LANES - QK_NOPE - QK_ROPE]], axis=0)
    win_conv_t = w_t[MLA_IN:].reshape(3, CONV_W // LANES, LANES, D_MODEL).transpose(1, 0, 2, 3)
    win_conv_t = win_conv_t.reshape(3 * CONV_W, D_MODEL)
    w_uq = _slots_to_cols(g_uq).reshape(Q_RANK, N_HEADS, QK_NOPE + QK_ROPE)
    wq = jnp.pad(w_uq, ((0, 0), (0, 0), (0, LANES - QK_NOPE - QK_ROPE))).reshape(Q_RANK, N_HEADS * LANES)
    w_ukv = _slots_to_cols(g_ukv).reshape(KV_RANK, N_HEADS, QK_NOPE + V_DIM)
    k_top = jnp.pad(w_ukv[:, :, :QK_NOPE], ((0, 0), (0, 0), (0, LANES - QK_NOPE))).reshape(KV_RANK, N_HEADS * LANES)
    v_top = w_ukv[:, :, QK_NOPE:].reshape(KV_RANK, N_HEADS * V_DIM)
    eye = jnp.pad(jnp.eye(QK_ROPE, dtype=BF16), ((QK_NOPE, LANES - QK_NOPE - QK_ROPE),) * 2)
    wk = jnp.concatenate([
        jnp.concatenate([k_top, v_top], axis=1),
        jnp.concatenate([jnp.tile(eye, (1, N_HEADS)), jnp.zeros((LANES, N_HEADS * V_DIM), BF16)], axis=1)], axis=0)
    return win_head_t, win_conv_t, wq, wk


def _pack_small_grads(d_head_t, d_conv_t, d_wq, d_wkk, d_wkv):
    d_conv_t = d_conv_t.reshape(CONV_W // LANES, 3, LANES, D_MODEL).transpose(1, 0, 2, 3).reshape(3 * CONV_W, D_MODEL)
    rope0 = Q_RANK + KV_RANK + QK_NOPE
    g_in_t = jnp.concatenate([d_head_t[:Q_RANK + KV_RANK], d_head_t[rope0:rope0 + QK_ROPE], d_conv_t], axis=0)
    g_in_t = g_in_t.reshape(N_DEV, -1, D_MODEL).astype(BF16)
    g_uq = d_wq.reshape(Q_RANK, N_HEADS, LANES)[:, :, :QK_NOPE + QK_ROPE].reshape(Q_RANK, -1)
    g_kn = d_wkk[:KV_RANK].reshape(KV_RANK, N_HEADS, LANES)[:, :, :QK_NOPE]
    g_v = d_wkv[:KV_RANK].reshape(KV_RANK, N_HEADS, V_DIM)
    g_ukv = jnp.concatenate([g_kn, g_v], axis=2).reshape(KV_RANK, -1)
    return [g_in_t] + [_cols_to_slots(g).astype(BF16) for g in (g_uq, g_ukv)]


def kernel(x, c, ctx, c_ctx, w_mod, b_mod, w_in, q_norm_g, w_uq, kv_norm_g, w_ukv, conv_w, w_out, w_mlp1, w_mlp2, final_norm_g, loss_target, m_c_ctx, m_w_mod, m_b_mod, m_w_in, m_q_norm_g, m_w_uq, m_kv_norm_g, m_w_ukv, m_conv_w, m_w_out, m_w_mlp1, m_w_mlp2, m_final_norm_g, v_c_ctx, v_w_mod, v_b_mod, v_w_in, v_q_norm_g, v_w_uq, v_kv_norm_g, v_w_ukv, v_conv_w, v_w_out, v_w_mlp1, v_w_mlp2, v_final_norm_g):
    me = _my_index()
    x2d, ctx2d, tgt = x[0], ctx[0], loss_target[0]
    s, l = x2d.shape[0], ctx2d.shape[0]
    t = s + l
    d = D_MODEL
    mod_cols = w_mod.shape[2]
    cw_cols = conv_w.shape[2]

    b_cols = lax.dynamic_slice(b_mod, (0, me * mod_cols), (1, mod_cols))
    cw_blk = jnp.pad(conv_w[0], ((0, 0), (0, mod_cols - cw_cols)))
    a_rows, gathered = _prologue(jnp.pad(c, ((0, 7), (0, 0))), c_ctx[None, :], w_mod[0], b_cols, cw_blk,
                                 "prologue")
    mod_mine = gathered[:, 0, :].reshape(1, 6 * d)
    mod_ctx = gathered[:, 1, :].reshape(1, 6 * d)
    cw_full = gathered[:, 2:5, :cw_cols].transpose(1, 0, 2).reshape(3, CONV_W)

    early = [w.astype(BF16) for w in (w_in[0].T, w_uq[0], w_ukv[0])]
    late = [w.astype(BF16) for w in (w_out[0], w_mlp1[0], w_mlp2[0])]
    h_all, (g_in, g_uq, g_ukv) = _modulate_all(x2d, ctx2d, mod_mine, mod_ctx, _RidingGather(early),
                                               "modulate1")
    win_head, win_conv, wq, wk = _unpack_small_weights(g_in, g_uq, g_ukv)
    wk_k, wk_v = wk[:, :N_HEADS * LANES], wk[:, N_HEADS * LANES:]
    cos, sgn = _rope_tables(s, l)

    tm_t = _pick(t, (1088, 768, 256))
    z_head, cq, kv_in, qf, kv = _head_fwd(h_all, win_head, wq, wk, q_norm_g, kv_norm_g, cos, sgn, tm_t, "head_fwd")
    z_conv = _matmul(h_all, win_conv, mode="nt", name="in_proj_conv", m=s, tm=1024, tn=1536, tk=1024)
    attn, a_cat, stats, (g_out, w1, g_w2) = _attn_fwd(qf, kv, s, _RidingGather(late), "attn_fwd")
    wo = g_out.reshape(d, d)
    w2 = g_w2.reshape(D_FF, d)
    a_cat = _conv_fwd(z_conv, cw_full, a_cat, "conv_fwd")
    (o, x1, h2), _ = _matmul_rows(a_cat, wo, _epi_resid_modulate, mode="nn", name="out_proj", tm=1024, tk=1024,
                                  rows=[x2d], vecs=[(mod_mine, 2), (mod_mine, 3), (mod_mine, 4)],
                                  out_dtypes=[F32, F32, BF16])
    u1, act = _matmul(h2, w1, mode="nn", name="mlp_up", tm=4096, tk=1024, epilogue="relu2", slots="b_cols")
    (dx2, dm, fsums), _ = _matmul_rows(act, w2, _epi_final, mode="nn", name="mlp_down", tm=512, tk=4096,
                                       rows=[x1, tgt], vecs=[(mod_mine, 5), (final_norm_g[None, :], 0)],
                                       out_dtypes=[F32, BF16], sums=True)

    d_w2 = _matmul(act, dm, mode="tn", name="d_w_mlp2", out_dtype=BF16, tm=1024, tn=1024, tk=4096)
    du1 = _matmul(dm, w2, mode="nt", name="d_act", out_dtype=BF16, tm=2048, tn=1024, tk=1024,
                  epilogue="drelu2", extra=(u1,))
    d_w1 = _matmul(h2, du1, mode="tn", name="d_w_mlp1", out_dtype=BF16, tm=1024, tk=4096, slots="out")
    (dx1, do, sums2), _ = _matmul_rows(du1, w1, _epi_modulate2_bwd, mode="nt", name="d_h2", tm=512, tk=4096,
                                       slots="b_contract", rows=[x1, dx2, o], vecs=[(mod_mine, 4), (mod_mine, 2)],
                                       out_dtypes=[F32, BF16], sums=True)
    da, d_wo = _out_proj_bwd(a_cat, do, wo, "out_proj_bwd")
    dz_conv, d_cw, d_conv = _conv_bwd(z_conv, cw_full, da, h_all, "conv_bwd")
    ready = [d_wo.reshape(N_DEV, d // N_DEV, d), d_w1, d_w2.reshape(N_DEV, D_FF // N_DEV, d)]
    dq, dk, dv, rode = _attn_bwd(qf, kv, attn, da, stats, cos, sgn, _Riding(ready), "attn_bwd")
    head_args = (z_head, wq, wk_k, wk_v, win_head, q_norm_g, kv_norm_g, cos, sgn, cq, kv_in, h_all)
    dz_head, dh_head, psums, d_wq, *carried = _head_bwd(dq, dk, dv, *head_args, "head_bwd", tile=HEAD_BWD_TILE,
                                                        first_block=0, n_blocks=s // HEAD_BWD_TILE)
    _, dh_head, psums_c, d_wkk, d_wkv, d_head = _head_bwd(
        None, dk, dv, *head_args, "head_bwd_ctx", tile=ROW_TILE, first_block=s // ROW_TILE, n_blocks=l // ROW_TILE,
        carry=(dz_head, dh_head, *carried))
    send = _pack_small_grads(d_head, d_conv, d_wq, d_wkk, d_wkv)
    (grad_x, sums1), got = _matmul_rows(dz_conv, win_conv, _epi_modulate1_bwd, mode="nn", name="d_h1", tm=max(s // 8, ROW_TILE),
                                        tk=win_conv.shape[0], rows=[dh_head, x2d, dx1], vecs=[(mod_mine, 1)],
                                        out_dtypes=[F32], sums=True, riding=_RidingReduce(send))
    sums1c = _modulate_sums(dh_head, s // ROW_TILE, ctx2d)

    small = _pack_small(sums1, sums2, fsums, sums1c, psums, psums_c, d_cw, mod_cols, "pack_small")
    (d_all,) = _all_gather([small], "gather_small_grads", True)
    d_ex = lax.dynamic_index_in_dim(d_all, me, axis=1, keepdims=False)
    d_ctx = lax.dynamic_index_in_dim(d_all, N_DEV + me, axis=1, keepdims=False)
    g_w_mod, dsil, dsum = _adaln_bwd(a_rows.T, w_mod[0], d_ex, d_ctx, d_all, "adaln_bwd")
    (dsil_all,) = _all_gather([dsil], "gather_d_cctx", True)
    loss = dsum[SMALL_LOSS, 0]
    g_cw = lax.dynamic_slice(dsum, (SMALL_CW, me * cw_cols), (3, cw_cols))

    slots = dict(zip(["w_in", "w_uq", "w_ukv"], got))
    slots.update(zip(["w_out", "w_mlp1", "w_mlp2"], rode))

    grads = {}
    weights = {"c_ctx": c_ctx, "w_mod": w_mod, "b_mod": b_mod, "w_in": w_in, "q_norm_g": q_norm_g, "w_uq": w_uq,
               "kv_norm_g": kv_norm_g, "w_ukv": w_ukv, "conv_w": conv_w, "w_out": w_out, "w_mlp1": w_mlp1,
               "w_mlp2": w_mlp2, "final_norm_g": final_norm_g}
    m_in = {"c_ctx": m_c_ctx, "w_mod": m_w_mod, "b_mod": m_b_mod, "w_in": m_w_in, "q_norm_g": m_q_norm_g,
            "w_uq": m_w_uq, "kv_norm_g": m_kv_norm_g, "w_ukv": m_w_ukv, "conv_w": m_conv_w, "w_out": m_w_out,
            "w_mlp1": m_w_mlp1, "w_mlp2": m_w_mlp2, "final_norm_g": m_final_norm_g}
    v_in = {"c_ctx": v_c_ctx, "w_mod": v_w_mod, "b_mod": v_b_mod, "w_in": v_w_in, "q_norm_g": v_q_norm_g,
            "w_uq": v_w_uq, "kv_norm_g": v_kv_norm_g, "w_ukv": v_w_ukv, "conv_w": v_conv_w, "w_out": v_w_out,
            "w_mlp1": v_w_mlp1, "w_mlp2": v_w_mlp2, "final_norm_g": v_final_norm_g}
    names = list(weights)
    small_names = ["c_ctx", "b_mod", "q_norm_g", "kv_norm_g", "final_norm_g", "conv_w"]
    delta, new_m, new_v = {}, {}, {}

    def as_rows(a):
        return a[None, :] if a.ndim == 1 else a

    small_out = _small_update(dsum, dsil_all, g_cw, [[as_rows(src[n]) for src in (weights, m_in, v_in)]
                                                      for n in small_names], "small_update")
    for n, outs in zip(small_names, small_out):
        grads[n], delta[n], new_m[n], new_v[n] = [a.reshape(weights[n].shape) for a in outs]
    for n in names:
        if n in small_names:
            continue
        if n == "w_in":
            wmv = [jnp.swapaxes(src[n], 1, 2) for src in (weights, m_in, v_in)]
            outs = _adamw(wmv[0], slots[n], wmv[1], wmv[2], "adamw_" + n, slots=True)
            grads[n], delta[n], new_m[n], new_v[n] = [jnp.swapaxes(a, 1, 2) for a in outs]
        elif n in slots:
            grads[n], delta[n], new_m[n], new_v[n] = _adamw(weights[n], slots[n], m_in[n], v_in[n], "adamw_" + n,
                                                            slots=True)
        else:
            delta[n], new_m[n], new_v[n] = _adamw(weights[n], g_w_mod, m_in[n], v_in[n], "adamw_" + n)
            grads[n] = g_w_mod[None]

    return (loss, grad_x[None], *[grads[n] for n in names], *[delta[n] for n in names],
            *[new_m[n] for n in names], *[new_v[n] for n in names])
```

```python
import math

import jax
import jax.numpy as jnp
import numpy as np
from jax import lax
from jax.experimental import pallas as pl
from jax.experimental.pallas import tpu as pltpu

F32 = jnp.float32
BF16 = jnp.bfloat16

D_MODEL = 1024
GRID_W = 64
N_HEADS = 8
QK_NOPE = 64
QK_ROPE = 32
V_DIM = 64
Q_RANK = 256
KV_RANK = 128
MLA_IN = Q_RANK + KV_RANK + QK_ROPE
CONV_W = 512
HEAD_COLS = 512
D_FF = 4096
ROPE_THETA = 10000.0
EPS = 1e-6
ATTN_SCALE = 1.0 / math.sqrt(QK_NOPE + QK_ROPE)
LOG2_E = 1.0 / math.log(2.0)
EXP2_SCALE = ATTN_SCALE * LOG2_E
N_DEV = 8
LANES = 128

ADAM_LR, ADAM_B1, ADAM_B2, ADAM_EPS, ADAM_WD, ADAM_STEP = 0.001, 0.9, 0.999, 1e-08, 0.01, 10

ROW_TILE = 256
MATMUL_ROW_CHUNK = 512
HEAD_BWD_TILE = 512
VMEM_BIG = 60 * 1024 * 1024


def _params(sem=None, vmem=None):
    return pltpu.CompilerParams(dimension_semantics=sem, vmem_limit_bytes=vmem)


def _pick(n, prefs):
    for p in prefs:
        if n % p == 0:
            return p
    return n


def _my_index():
    return 4 * lax.axis_index("x") + 2 * lax.axis_index("y") + lax.axis_index("c")


def _two_level_gather(x_refs, out_refs, send_sems, recv_sems, local_sems):
    n = len(x_refs)
    x, y, c = lax.axis_index("x"), lax.axis_index("y"), lax.axis_index("c")
    me, sibling = (x, y, c), (x, y, 1 - c)
    chips = [(1 - x, y), (x, 1 - y), (1 - x, 1 - y)]

    def slot(a, px, py, pc):
        return out_refs[a].at[4 * px + 2 * py + pc]

    def copy(a, k, block, to, src=None):
        return pltpu.make_async_remote_copy(
            src_ref=slot(a, *block) if src is None else src, dst_ref=slot(a, *block),
            send_sem=send_sems.at[7 * a + k], recv_sem=recv_sems.at[7 * a + k],
            device_id=to, device_id_type=pl.DeviceIdType.MESH)

    mine = [pltpu.make_async_copy(x_refs[a], slot(a, *me), local_sems.at[a]) for a in range(n)]
    first = [cp for a in range(n) for cp in
             [copy(a, 0, me, sibling, src=x_refs[a])]
             + [copy(a, 1 + j, me, (*chip, c), src=x_refs[a]) for j, chip in enumerate(chips)]]
    passed = [[copy(a, 4 + j, (*chip, c), sibling) for j, chip in enumerate(chips)] for a in range(n)]

    def start():
        for cp in mine + first:
            cp.start()

    def forward():
        for a in range(n):
            for j, chip in enumerate(chips):
                copy(a, 1 + j, (*chip, c), me).wait_recv()
                passed[a][j].start()

    def finish():
        for a in range(n):
            copy(a, 0, sibling, me).wait_recv()
            for j, chip in enumerate(chips):
                copy(a, 4 + j, (*chip, 1 - c), me).wait_recv()
        for cp in first + [cp for per_array in passed for cp in per_array]:
            cp.wait_send()
        for cp in mine:
            cp.wait()

    return start, forward, finish


def _direct_gather(src_ref, dst_ref, send_sems, recv_sems, per_peer=False):
    x, y, c = lax.axis_index("x"), lax.axis_index("y"), lax.axis_index("c")
    me = 4 * x + 2 * y + c
    dst_ref[me] = src_ref[me] if per_peer else src_ref[...]
    sends, landings = [], []
    for k in range(1, N_DEV):
        peer = (1 - x if k & 4 else x, 1 - y if k & 2 else y, 1 - c if k & 1 else c)
        pid = 4 * peer[0] + 2 * peer[1] + peer[2]
        for dst, out in ((me, sends), (pid, landings)):
            out.append(pltpu.make_async_remote_copy(
                src_ref=src_ref.at[pid] if per_peer else src_ref, dst_ref=dst_ref.at[dst],
                send_sem=send_sems.at[k - 1], recv_sem=recv_sems.at[k - 1],
                device_id=peer, device_id_type=pl.DeviceIdType.MESH))
    for cp in sends:
        cp.start()

    def finish():
        for cp in landings:
            cp.wait_recv()
        for cp in sends:
            cp.wait_send()

    return finish


def _all_gather(arrays, name, in_vmem):
    space = pltpu.VMEM if in_vmem else pl.ANY
    n = len(arrays)

    def body(*refs):
        for phase in _two_level_gather(refs[:n], refs[n:2 * n], *refs[2 * n:]):
            phase()

    outs = pl.pallas_call(
        body, name=name,
        out_shape=tuple(jax.ShapeDtypeStruct((N_DEV,) + a.shape, a.dtype) for a in arrays),
        in_specs=[pl.BlockSpec(memory_space=space)] * n,
        out_specs=tuple(pl.BlockSpec(memory_space=space) for _ in arrays),
        scratch_shapes=[pltpu.SemaphoreType.DMA((7 * n,)), pltpu.SemaphoreType.DMA((7 * n,)),
                        pltpu.SemaphoreType.DMA((n,))],
    )(*arrays)
    return list(outs)


class _Riding:
    def __init__(self, arrays=()):
        self.arrays, self.n = list(arrays), len(arrays)
        self.out_shape = [jax.ShapeDtypeStruct(a.shape, a.dtype) for a in self.arrays]
        self.specs = [pl.BlockSpec(memory_space=pl.ANY)] * self.n
        self.scratch = [pltpu.SemaphoreType.DMA((7 * self.n,)), pltpu.SemaphoreType.DMA((7 * self.n,)),
                        pltpu.SemaphoreType.DMA((self.n,))]

    def copies(self, x_refs, y_refs, send_sems, recv_sems, local_sems):
        x, y, c = lax.axis_index("x"), lax.axis_index("y"), lax.axis_index("c")
        me = 4 * x + 2 * y + c
        local, sends, landings = [], [], []
        for a in range(self.n):
            local.append(pltpu.make_async_copy(x_refs[a].at[me], y_refs[a].at[me], local_sems.at[a]))
            for k in range(1, N_DEV):
                peer = (1 - x if k & 4 else x, 1 - y if k & 2 else y, 1 - c if k & 1 else c)
                pid = 4 * peer[0] + 2 * peer[1] + peer[2]
                for dst, out in ((me, sends), (pid, landings)):
                    out.append(pltpu.make_async_remote_copy(
                        src_ref=x_refs[a].at[pid], dst_ref=y_refs[a].at[dst],
                        send_sem=send_sems.at[7 * a + k - 1], recv_sem=recv_sems.at[7 * a + k - 1],
                        device_id=peer, device_id_type=pl.DeviceIdType.MESH))
        return local, sends, landings

    def run(self, first, last, x_refs, y_refs, sems, middle=None):
        if self.n == 0:
            return None
        local, sends, landings = self.copies(x_refs, y_refs, *sems)

        @pl.when(first)
        def _():
            for cp in local + sends:
                cp.start()

        return local, sends, landings, last

    @staticmethod
    def finish(state):
        if state is None:
            return
        local, sends, landings, last = state

        @pl.when(last)
        def _():
            for cp in landings:
                cp.wait_recv()
            for cp in sends:
                cp.wait_send()
            for cp in local:
                cp.wait()


class _RidingGather:
    def __init__(self, arrays):
        self.arrays, self.n = list(arrays), len(arrays)
        self.out_shape = [jax.ShapeDtypeStruct((N_DEV,) + a.shape, a.dtype) for a in self.arrays]
        self.specs = [pl.BlockSpec(memory_space=pl.ANY)] * self.n
        self.scratch = [pltpu.SemaphoreType.DMA((7 * self.n,)), pltpu.SemaphoreType.DMA((7 * self.n,)),
                        pltpu.SemaphoreType.DMA((self.n,))]

    def run(self, first, last, x_refs, y_refs, sems, middle):
        start, forward, finish = _two_level_gather(x_refs, y_refs, *sems)
        pl.when(first)(start)
        pl.when(middle)(forward)
        return finish, last

    @staticmethod
    def finish(state):
        finish, last = state
        pl.when(last)(finish)


class _RidingReduce:
    def __init__(self, arrays):
        self.arrays, self.n = list(arrays), len(arrays)
        self.out_shape = [jax.ShapeDtypeStruct((4,) + a.shape[1:], a.dtype) for a in self.arrays]
        self.specs = [pl.BlockSpec(memory_space=pl.ANY)] * self.n
        self.scratch = [pltpu.VMEM((4,) + a.shape[1:], a.dtype) for a in self.arrays for _ in range(3)]
        self.scratch += [pltpu.SemaphoreType.DMA((self.n,)) for _ in range(6)]

    def run(self, first, last, x_refs, y_refs, scratch, middle):
        n = self.n
        own, sib, tot = scratch[0:3 * n:3], scratch[1:3 * n:3], scratch[2:3 * n:3]
        d2d_send, d2d_recv, local_in, ici_send, ici_recv, local_out = scratch[3 * n:]
        x, y, c = lax.axis_index("x"), lax.axis_index("y"), lax.axis_index("c")
        my_chip = 2 * x + y
        sibling = (x, y, 1 - c)
        others = [(1 - x, y), (x, 1 - y), (1 - x, 1 - y)]

        def to_sibling(a, j=None):
            src = x_refs[a].at[pl.ds(0, 4)] if j is None else x_refs[a].at[2 * j + 1 - c]
            dst = sib[a] if j is None else sib[a].at[j]
            return pltpu.make_async_remote_copy(src_ref=src, dst_ref=dst, send_sem=d2d_send.at[a],
                                                recv_sem=d2d_recv.at[a], device_id=sibling,
                                                device_id_type=pl.DeviceIdType.MESH)

        def mine_in(a, j=None):
            src = x_refs[a].at[pl.ds(0, 4)] if j is None else x_refs[a].at[2 * j + c]
            return pltpu.make_async_copy(src, own[a] if j is None else own[a].at[j], local_in.at[a])

        def to_chip(a, chip=None):
            if chip is None:
                src, dst, peer = tot[a].at[pl.ds(0, 3)], y_refs[a].at[pl.ds(0, 3)], sibling
            else:
                src, dst, peer = tot[a].at[2 * chip[0] + chip[1]], y_refs[a].at[my_chip], (*chip, c)
            return pltpu.make_async_remote_copy(src_ref=src, dst_ref=dst, send_sem=ici_send.at[a],
                                                recv_sem=ici_recv.at[a], device_id=peer,
                                                device_id_type=pl.DeviceIdType.MESH)

        def mine_out(a):
            return pltpu.make_async_copy(tot[a].at[my_chip], y_refs[a].at[my_chip], local_out.at[a])

        @pl.when(first)
        def _():
            for a in range(n):
                for j in range(4):
                    to_sibling(a, j).start()
                    mine_in(a, j).start()

        @pl.when(middle)
        def _():
            for a in range(n):
                to_sibling(a).wait_recv()
                to_sibling(a).wait_send()
                mine_in(a).wait()
                tot[a][...] = (own[a][...].astype(F32) + sib[a][...].astype(F32)).astype(tot[a].dtype)
                for chip in others:
                    to_chip(a, chip).start()
                mine_out(a).start()

        def finish():
            @pl.when(last)
            def _():
                for a in range(n):
                    to_chip(a).wait_recv()
                    to_chip(a).wait_send()
                    mine_out(a).wait()

        return finish

    @staticmethod
    def finish(state):
        state()


_DIMS ={"nn": (((1,), (0,)), ((), ())), "nt": (((1,), (1,)), ((), ())), "tn": (((0,), (0,)), ((), ()))}
NT_DIMS = _DIMS["nt"]
TN_DIMS = _DIMS["tn"]


def _swap8(x):
    lane = lax.broadcasted_iota(jnp.int32, x.shape, 1)
    return jnp.where((lane & 15) < 8, pltpu.roll(x, LANES - 8, 1), pltpu.roll(x, 8, 1))


def _rope(x, cos, sgn, bwd):
    return x * cos + (_swap8(x * sgn) if bwd else _swap8(x) * sgn)


def _matmul(a, b, *, mode, name, out_dtype=F32, tm=512, tn=512, tk=512, m=None, k=None,
            epilogue=None, extra=(), slots=None):
    if mode == "nn":
        m = a.shape[0] if m is None else m
        k = a.shape[1]
        n = N_DEV * b.shape[2] if slots == "b_cols" else b.shape[1]
    elif mode == "nt":
        m = a.shape[0] if m is None else m
        k = a.shape[1]
        n = b.shape[0]
    else:
        k = a.shape[0] if k is None else k
        m, n = a.shape[1], b.shape[1]
    tm, tn, tk = min(tm, m), min(tn, n), min(tk, k)
    if slots == "b_cols":
        tn = b.shape[2]
    if slots == "out":
        tn = n // N_DEV
    assert m % tm == 0 and n % tn == 0 and k % tk == 0, (name, m, n, k, tm, tn, tk)
    nk = k // tk
    dims = _DIMS[mode]
    a_spec = (pl.BlockSpec((tk, tm), lambda i, j, kk: (kk, i)) if mode == "tn"
              else pl.BlockSpec((tm, tk), lambda i, j, kk: (i, kk)))
    if slots == "b_cols":
        b_spec = pl.BlockSpec((None, tk, tn), lambda i, j, kk: (j, kk, 0))
    elif mode == "nt":
        b_spec = pl.BlockSpec((tn, tk), lambda i, j, kk: (j, kk))
    else:
        b_spec = pl.BlockSpec((tk, tn), lambda i, j, kk: (kk, j))
    tile = pl.BlockSpec((tm, tn), lambda i, j, kk: (i, j))
    if slots == "out":
        o_spec = pl.BlockSpec((None, tm, tn), lambda i, j, kk: (j, i, 0))
        o_shape = (N_DEV, m, tn)
    else:
        o_spec, o_shape = tile, (m, n)
    in_specs, args = [a_spec, b_spec], [a, b]
    if epilogue == "drelu2":
        in_specs.append(tile)
    args += list(extra)
    if epilogue == "relu2":
        out_shape = (jax.ShapeDtypeStruct(o_shape, BF16), jax.ShapeDtypeStruct(o_shape, BF16))
        out_specs = (o_spec, o_spec)
    else:
        out_shape = jax.ShapeDtypeStruct(o_shape, out_dtype)
        out_specs = o_spec
    n_in = len(args)
    n_out = 2 if epilogue == "relu2" else 1

    def body(*refs):
        a_ref, b_ref = refs[0], refs[1]
        outs = refs[n_in:n_in + n_out]
        def finish(acc, rows=slice(None)):
            if epilogue == "relu2":
                outs[0][rows, :] = acc.astype(BF16)
                r = jnp.maximum(acc, 0.0)
                outs[1][rows, :] = (r * r).astype(BF16)
            elif epilogue == "drelu2":
                u = refs[2][rows, :].astype(F32)
                outs[0][rows, :] = (acc * (2.0 * jnp.maximum(u, 0.0))).astype(out_dtype)
            else:
                outs[0][rows, :] = acc.astype(out_dtype)

        if nk == 1 and tm > MATMUL_ROW_CHUNK:
            for r0 in range(0, tm, MATMUL_ROW_CHUNK):
                rows = slice(r0, r0 + MATMUL_ROW_CHUNK)
                lhs = a_ref[:, rows] if mode == "tn" else a_ref[rows, :]
                finish(lax.dot_general(lhs, b_ref[...], dims, preferred_element_type=F32), rows)
            return
        part = lax.dot_general(a_ref[...], b_ref[...], dims, preferred_element_type=F32)
        if nk == 1:
            finish(part)
        else:
            acc_ref = refs[n_in + n_out]
            kk = pl.program_id(2)

            @pl.when(kk == 0)
            def _():
                acc_ref[...] = part

            @pl.when(kk > 0)
            def _():
                acc_ref[...] += part

            @pl.when(kk == nk - 1)
            def _():
                finish(acc_ref[...])

    return pl.pallas_call(
        body, name=name, grid=(m // tm, n // tn, nk),
        out_shape=out_shape, in_specs=in_specs, out_specs=out_specs,
        scratch_shapes=[pltpu.VMEM((tm, tn), F32)] if nk > 1 else [],
        compiler_params=_params(("parallel", "parallel", "arbitrary"), VMEM_BIG),
    )(*args)


def _rstd(x):
    return lax.rsqrt(jnp.mean(x * x, axis=1, keepdims=True) + EPS)


def _norm_bwd(dxn, xn, r):
    return r * (dxn - xn * jnp.mean(dxn * xn, axis=1, keepdims=True))


def _vec(col):
    return pl.BlockSpec((1, D_MODEL), lambda i: (0, col))


def _matmul_rows(a, b, epi, *, mode, name, tm, tk, rows=(), vecs=(), out_dtypes=(), sums=False, slots=None,
                 riding=None):
    m, k = a.shape
    n = D_MODEL
    tm, tk = min(tm, m), min(tk, k)
    riding = riding or _Riding()
    group = 1
    if slots == "b_contract":
        group = max(1, tk // b.shape[2])
        tk = group * b.shape[2]
        b_spec = pl.BlockSpec((group, n, tk // group), lambda i, kk: (kk, 0, 0))
    elif mode == "nt":
        b_spec = pl.BlockSpec((n, tk), lambda i, kk: (0, kk))
    else:
        b_spec = pl.BlockSpec((tk, n), lambda i, kk: (kk, 0))
    assert m % tm == 0 and k % tk == 0, (name, m, k, tm, tk)
    ni, nk = m // tm, k // tk
    assert ni >= 2 or not isinstance(riding, _RidingReduce), "the two-level exchange needs a middle grid step"
    dims = _DIMS[mode]
    tile = pl.BlockSpec((tm, n), lambda i, kk: (i, 0))
    in_specs = [pl.BlockSpec((tm, tk), lambda i, kk: (i, kk)), b_spec] + [tile] * len(rows)
    in_specs += [pl.BlockSpec((1, n), lambda i, kk, col=col: (0, col)) for _, col in vecs]
    args = [a, b, *rows, *[v for v, _ in vecs]]
    out_shape = [jax.ShapeDtypeStruct((m, n), dt) for dt in out_dtypes]
    out_specs = [tile] * len(out_dtypes)
    if sums:
        out_shape.append(jax.ShapeDtypeStruct((8, n), F32))
        out_specs.append(pl.BlockSpec((8, n), lambda i, kk: (0, 0)))
    n_rows, n_vecs, n_outs, nr = len(rows), len(vecs), len(out_dtypes), riding.n
    n_in = 2 + n_rows + n_vecs

    def body(*refs):
        a_ref, b_ref = refs[0], refs[1]
        row_refs = refs[2:2 + n_rows]
        vec_refs = refs[2 + n_rows:n_in]
        x_refs = refs[n_in:n_in + nr]
        out_refs = refs[n_in + nr:n_in + nr + n_outs]
        pos = n_in + nr + n_outs
        sums_ref = refs[pos] if sums else None
        pos += 1 if sums else 0
        y_refs = refs[pos:pos + nr]
        pos += nr
        acc_ref = refs[pos] if nk > 1 else None
        sem_refs = refs[pos + (1 if nk > 1 else 0):]
        i, kk = pl.program_id(0), pl.program_id(1)
        state = riding.run((i == 0) & (kk == 0), (i == ni - 1) & (kk == nk - 1), x_refs, y_refs, sem_refs,
                           middle=(i == 1) & (kk == 0))
        if slots == "b_contract":
            c = tk // group
            part = lax.dot_general(a_ref[:, 0:c], b_ref[0], dims, preferred_element_type=F32)
            for u in range(1, group):
                part = part + lax.dot_general(a_ref[:, u * c:(u + 1) * c], b_ref[u], dims, preferred_element_type=F32)
        else:
            part = lax.dot_general(a_ref[...], b_ref[...], dims, preferred_element_type=F32)

        def finish(acc):
            nsub = tm // ROW_TILE
            for r in range(nsub):
                blk = pl.ds(r * ROW_TILE, ROW_TILE)
                epi(acc[r * ROW_TILE:(r + 1) * ROW_TILE], [ref.at[blk] for ref in row_refs], vec_refs,
                    [ref.at[blk] for ref in out_refs], sums_ref,
                    (i == 0) if r == 0 else None, (i == ni - 1) if r == nsub - 1 else None)

        if nk == 1:
            finish(part)
        else:
            @pl.when(kk == 0)
            def _():
                acc_ref[...] = part

            @pl.when(kk > 0)
            def _():
                acc_ref[...] += part

            @pl.when(kk == nk - 1)
            def _():
                finish(acc_ref)

        riding.finish(state)

    outs = pl.pallas_call(
        body, name=name, grid=(ni, nk),
        out_shape=(*out_shape, *riding.out_shape),
        in_specs=[*in_specs, *riding.specs], out_specs=(*out_specs, *riding.specs),
        scratch_shapes=([pltpu.VMEM((tm, n), F32)] if nk > 1 else []) + (riding.scratch if nr else []),
        compiler_params=_params(("arbitrary", "arbitrary"), VMEM_BIG),
    )(*args, *riding.arrays)
    n_own = len(out_shape)
    return list(outs[:n_own]), list(outs[n_own:])


def _zero_sums_at_start(sums_ref, first):
    if first is not None:
        @pl.when(first)
        def _():
            sums_ref[...] = jnp.zeros_like(sums_ref)


def _epi_resid_modulate(acc, rows, vecs, outs, sums_ref, first, last):
    (x_ref,), (g_ref, sh_ref, sc_ref) = rows, vecs
    x1 = x_ref[...] + g_ref[...] * acc
    outs[0][...] = acc
    outs[1][...] = x1
    outs[2][...] = (x1 * _rstd(x1) * (1.0 + sc_ref[...]) + sh_ref[...]).astype(BF16)


def _epi_final(acc, rows, vecs, outs, sums_ref, first, last):
    (x1_ref, t_ref), (g_ref, gf_ref) = rows, vecs
    d = acc.shape[1]
    x2 = x1_ref[...] + g_ref[...] * acc
    r = _rstd(x2)
    xn = x2 * r
    err = xn * gf_ref[...] - t_ref[...]
    dy = err * (1.0 / d)
    dx2 = _norm_bwd(dy * gf_ref[...], xn, r)
    outs[0][...] = dx2
    outs[1][...] = (dx2 * g_ref[...]).astype(BF16)
    _zero_sums_at_start(sums_ref, first)
    sums_ref[0:1, :] += jnp.sum(dy * xn, axis=0, keepdims=True)
    sums_ref[1:2, :] += jnp.sum(dx2 * acc, axis=0, keepdims=True)
    sums_ref[2:3, :] += jnp.sum(err * err, axis=0, keepdims=True)

    if last is not None:
        @pl.when(last)
        def _():
            tot = jnp.sum(sums_ref[2:3, :], axis=1, keepdims=True) * (0.5 / d)
            sums_ref[3:4, :] = jnp.broadcast_to(tot, (1, d))


def _epi_modulate2_bwd(acc, rows, vecs, outs, sums_ref, first, last):
    (x_ref, dres_ref, o_ref), (sc_ref, g_ref) = rows, vecs
    x = x_ref[...]
    r = _rstd(x)
    xn = x * r
    dx = dres_ref[...] + _norm_bwd(acc * (1.0 + sc_ref[...]), xn, r)
    outs[0][...] = dx
    outs[1][...] = (dx * g_ref[...]).astype(BF16)
    _zero_sums_at_start(sums_ref, first)
    sums_ref[0:1, :] += jnp.sum(acc * xn, axis=0, keepdims=True)
    sums_ref[1:2, :] += jnp.sum(acc, axis=0, keepdims=True)
    sums_ref[2:3, :] += jnp.sum(dx * o_ref[...], axis=0, keepdims=True)


def _epi_modulate1_bwd(acc, rows, vecs, outs, sums_ref, first, last):
    (add_ref, x_ref, dres_ref), (sc_ref,) = rows, vecs
    dh = acc + add_ref[...]
    x = x_ref[...]
    r = _rstd(x)
    xn = x * r
    outs[0][...] = dres_ref[...] + _norm_bwd(dh * (1.0 + sc_ref[...]), xn, r)
    _zero_sums_at_start(sums_ref, first)
    sums_ref[0:1, :] += jnp.sum(dh * xn, axis=0, keepdims=True)
    sums_ref[1:2, :] += jnp.sum(dh, axis=0, keepdims=True)


def _modulate_all(x, ctx, mod, mod_ctx, riding, name):
    s, d = x.shape
    t = s + ctx.shape[0]
    ns = s // ROW_TILE
    nc = ctx.shape[0] // ROW_TILE
    nr = riding.n

    def body(*refs):
        x_ref, c_ref, sh_ref, sc_ref, shc_ref, scc_ref = refs[:6]
        h_ref = refs[6 + nr]
        i = pl.program_id(0)
        state = riding.run(i == 0, i == ns + nc - 1, refs[6:6 + nr], refs[7 + nr:7 + 2 * nr], refs[7 + 2 * nr:],
                           middle=i == ns + nc - 3)

        @pl.when(i < ns)
        def _():
            v = x_ref[...]
            h_ref[...] = (v * _rstd(v) * (1.0 + sc_ref[...]) + sh_ref[...]).astype(BF16)

        @pl.when(i >= ns)
        def _():
            v = c_ref[...]
            h_ref[...] = (v * _rstd(v) * (1.0 + scc_ref[...]) + shc_ref[...]).astype(BF16)

        riding.finish(state)

    outs = pl.pallas_call(
        body, name=name, grid=(ns + nc,),
        out_shape=(jax.ShapeDtypeStruct((t, d), BF16), *riding.out_shape),
        in_specs=[pl.BlockSpec((ROW_TILE, d), lambda i: (jnp.minimum(i, ns - 1), 0)),
                  pl.BlockSpec((ROW_TILE, d), lambda i: (jnp.maximum(i - ns, 0), 0)),
                  _vec(0), _vec(1), _vec(0), _vec(1), *riding.specs],
        out_specs=(pl.BlockSpec((ROW_TILE, d), lambda i: (i, 0)), *riding.specs),
        scratch_shapes=riding.scratch,
        compiler_params=_params(("arbitrary",)),
    )(x, ctx, mod, mod, mod_ctx, mod_ctx, *riding.arrays)
    return outs[0], list(outs[1:])


def _modulate_sums(dh, row_off, xsrc):
    s, d = xsrc.shape

    def body(dh_ref, x_ref, sums_ref):
        i = pl.program_id(0)
        x = x_ref[...]
        dhv = dh_ref[...]

        @pl.when(i == 0)
        def _():
            sums_ref[...] = jnp.zeros_like(sums_ref)

        sums_ref[0:1, :] += jnp.sum(dhv * (x * _rstd(x)), axis=0, keepdims=True)
        sums_ref[1:2, :] += jnp.sum(dhv, axis=0, keepdims=True)

    return pl.pallas_call(
        body, name="modulate1_ctx_bwd", grid=(s // ROW_TILE,),
        out_shape=jax.ShapeDtypeStruct((8, d), F32),
        in_specs=[pl.BlockSpec((ROW_TILE, d), lambda i: (i + row_off, 0)), pl.BlockSpec((ROW_TILE, d), lambda i: (i, 0))],
        out_specs=pl.BlockSpec((8, d), lambda i: (0, 0)),
        compiler_params=_params(("arbitrary",)),
    )(dh, xsrc)


def _head_fwd(h_all, win_head, wq, wk, q_gain, kv_gain, cos, sgn, tm, name):
    t, d = h_all.shape
    nq, nkv = wq.shape[1], wk.shape[1]

    def body(h_ref, wi_ref, wq_ref, wk_ref, qg_ref, kg_ref, c_ref, s_ref, z_ref, cq_ref, kvin_ref, qf_ref, kv_ref):
        z = lax.dot_general(h_ref[...], wi_ref[...], NT_DIMS, preferred_element_type=F32)
        z_ref[...] = z
        cos, sgn = c_ref[...], s_ref[...]
        zq = z[:, 0:Q_RANK]
        cq = (zq * _rstd(zq) * qg_ref[...]).astype(BF16)
        cq_ref[...] = cq
        zk = z[:, Q_RANK:Q_RANK + KV_RANK]
        kv_in = jnp.concatenate([(zk * _rstd(zk) * kg_ref[...]).astype(BF16),
                                 _rope(z[:, Q_RANK + KV_RANK:HEAD_COLS], cos, sgn, False).astype(BF16)], axis=1)
        kvin_ref[...] = kv_in
        q = jnp.dot(cq, wq_ref[...], preferred_element_type=F32)
        for h in range(nq // LANES):
            sl = slice(h * LANES, (h + 1) * LANES)
            qf_ref[:, sl] = _rope(q[:, sl], cos, sgn, False).astype(BF16)
        kv_ref[...] = jnp.dot(kv_in, wk_ref[...], preferred_element_type=F32).astype(BF16)

    def row(w):
        return pl.BlockSpec((tm, w), lambda i: (i, 0))

    def whole(a):
        return pl.BlockSpec(a.shape, lambda i: (0, 0))

    return pl.pallas_call(
        body, name=name, grid=(t // tm,),
        out_shape=(jax.ShapeDtypeStruct((t, HEAD_COLS), F32), jax.ShapeDtypeStruct((t, Q_RANK), BF16),
                   jax.ShapeDtypeStruct((t, KV_RANK + LANES), BF16), jax.ShapeDtypeStruct((t, nq), BF16),
                   jax.ShapeDtypeStruct((t, nkv), BF16)),
        in_specs=[row(d), whole(win_head), whole(wq), whole(wk), whole(q_gain), whole(kv_gain), row(LANES), row(LANES)],
        out_specs=(row(HEAD_COLS), row(Q_RANK), row(KV_RANK + LANES), row(nq), row(nkv)),
        compiler_params=_params(("parallel",), VMEM_BIG),
    )(h_all, win_head, wq, wk, q_gain, kv_gain, cos, sgn)


def _head_bwd(dq, dk, dv, z, wq, wk_k, wk_v, win_head, q_gain, kv_gain, cos, sgn, cq, kv_in, h_all, name, *, tile,
              first_block, n_blocks, carry=None):
    t = z.shape[0]
    with_q = dq is not None

    def body(*refs):
        it = iter(refs)
        dq_ref = next(it) if with_q else None
        dk_ref, dv_ref, z_ref, wq_ref, wkk_ref, wkv_ref, wi_ref, qg_ref, kg_ref, c_ref, s_ref = (next(it) for _ in range(11))
        cq_ref = next(it) if with_q else None
        kvin_ref, h_ref = next(it), next(it)
        before = None
        if carry is not None:
            next(it), next(it)
            before = (next(it), next(it), next(it))
        dz_ref, dh_ref, sums_ref = next(it), next(it), next(it)
        gq_ref = next(it) if with_q else None
        grads = (next(it), next(it), next(it))
        i = pl.program_id(0)

        @pl.when(i == 0)
        def _():
            sums_ref[...] = jnp.zeros_like(sums_ref)
            if with_q:
                gq_ref[...] = jnp.zeros_like(gq_ref)
            for k, g_ref in enumerate(grads):
                g_ref[...] = jnp.zeros_like(g_ref) if before is None else before[k][...]

        if with_q:
            gq_ref[...] += lax.dot_general(cq_ref[...], dq_ref[...], TN_DIMS, preferred_element_type=F32)
        grads[0][...] += lax.dot_general(kvin_ref[...], dk_ref[...], TN_DIMS, preferred_element_type=F32)
        grads[1][...] += lax.dot_general(kvin_ref[...], dv_ref[...], TN_DIMS, preferred_element_type=F32)
        if with_q:
            dc = lax.dot_general(dq_ref[...], wq_ref[...], NT_DIMS, preferred_element_type=F32)
            zq = z_ref[:, 0:Q_RANK]
            r = _rstd(zq)
            zn = zq * r
            sums_ref[0:1, :] += jnp.sum(dc * zn, axis=0, keepdims=True)
            dz_ref[:, 0:Q_RANK] = _norm_bwd(dc * qg_ref[...], zn, r).astype(BF16)
        else:
            dz_ref[:, 0:Q_RANK] = jnp.zeros((tile, Q_RANK), BF16)
        dkv = (lax.dot_general(dk_ref[...], wkk_ref[...], NT_DIMS, preferred_element_type=F32)
               + lax.dot_general(dv_ref[...], wkv_ref[...], NT_DIMS, preferred_element_type=F32))
        zk = z_ref[:, Q_RANK:Q_RANK + KV_RANK]
        r = _rstd(zk)
        zn = zk * r
        dc = dkv[:, 0:KV_RANK]
        sums_ref[1:2, 0:KV_RANK] += jnp.sum(dc * zn, axis=0, keepdims=True)
        dz_ref[:, Q_RANK:Q_RANK + KV_RANK] = _norm_bwd(dc * kg_ref[...], zn, r).astype(BF16)
        dz_ref[:, Q_RANK + KV_RANK:HEAD_COLS] = _rope(dkv[:, KV_RANK:KV_RANK + LANES], c_ref[...], s_ref[...],
                                                       True).astype(BF16)
        dh_ref[...] = jnp.dot(dz_ref[...], wi_ref[...], preferred_element_type=F32)
        grads[2][...] += lax.dot_general(dz_ref[...], h_ref[...], TN_DIMS, preferred_element_type=F32)

    def row(w):
        return pl.BlockSpec((tile, w), lambda i: (i + first_block, 0))

    def whole(a):
        return pl.BlockSpec(a.shape, lambda i: (0, 0))

    args = ([dq] if with_q else []) + [dk, dv, z, wq, wk_k, wk_v, win_head, q_gain, kv_gain, cos, sgn]
    args += ([cq] if with_q else []) + [kv_in, h_all]
    in_specs = ([row(dq.shape[1])] if with_q else []) + [
        row(dk.shape[1]), row(dv.shape[1]), row(HEAD_COLS), whole(wq), whole(wk_k), whole(wk_v),
        whole(win_head), whole(q_gain), whole(kv_gain), row(LANES), row(LANES)]
    in_specs += ([row(Q_RANK)] if with_q else []) + [row(kv_in.shape[1]), row(D_MODEL)]
    aliases = {}
    if carry is not None:
        aliases = {len(args): 0, len(args) + 1: 1}
        args += list(carry)
        in_specs += [pl.BlockSpec(memory_space=pl.ANY)] * 2 + [whole(a) for a in carry[2:]]
    grad_shapes = ([(Q_RANK, dq.shape[1])] if with_q else []) + [
        (kv_in.shape[1], dk.shape[1]), (kv_in.shape[1], dv.shape[1]), (HEAD_COLS, D_MODEL)]
    return pl.pallas_call(
        body, name=name, grid=(n_blocks,),
        out_shape=(jax.ShapeDtypeStruct((t, HEAD_COLS), BF16), jax.ShapeDtypeStruct((t, D_MODEL), F32),
                   jax.ShapeDtypeStruct((8, Q_RANK), F32), *[jax.ShapeDtypeStruct(g, F32) for g in grad_shapes]),
        in_specs=in_specs,
        out_specs=(row(HEAD_COLS), row(D_MODEL), pl.BlockSpec((8, Q_RANK), lambda i: (0, 0)),
                   *[pl.BlockSpec(g, lambda i: (0, 0)) for g in grad_shapes]),
        input_output_aliases=aliases,
        compiler_params=_params(("arbitrary",), VMEM_BIG),
    )(*args)


def _out_proj_bwd(a, dy, w, name, tm=1024):
    s, k = a.shape
    n = dy.shape[1]

    def body(a_ref, dy_ref, w_ref, da_ref, dw_ref, acc_ref):
        i = pl.program_id(0)
        da_ref[...] = lax.dot_general(dy_ref[...], w_ref[...], NT_DIMS, preferred_element_type=F32)
        part = lax.dot_general(a_ref[...], dy_ref[...], TN_DIMS, preferred_element_type=F32)

        @pl.when(i == 0)
        def _():
            acc_ref[...] = part

        @pl.when(i > 0)
        def _():
            acc_ref[...] += part

        @pl.when(i == pl.num_programs(0) - 1)
        def _():
            dw_ref[...] = acc_ref[...].astype(BF16)

    return pl.pallas_call(
        body, name=name, grid=(s // tm,),
        out_shape=(jax.ShapeDtypeStruct((s, k), F32), jax.ShapeDtypeStruct((k, n), BF16)),
        in_specs=[pl.BlockSpec((tm, k), lambda i: (i, 0)), pl.BlockSpec((tm, n), lambda i: (i, 0)),
                  pl.BlockSpec(w.shape, lambda i: (0, 0))],
        out_specs=(pl.BlockSpec((tm, k), lambda i: (i, 0)), pl.BlockSpec((k, n), lambda i: (0, 0))),
        scratch_shapes=[pltpu.VMEM((k, n), F32)],
        compiler_params=_params(("arbitrary",), VMEM_BIG),
    )(a, dy, w)


def _shift_rows(u, s):
    rowi = lax.broadcasted_iota(jnp.int32, u.shape, 0)
    prev = jnp.where(rowi == 0, 0.0, pltpu.roll(u, 1, 0))
    nxt = jnp.where(rowi == s - 1, 0.0, pltpu.roll(u, s - 1, 0))
    return prev, nxt


def _conv_fwd(z_conv, cw, a_cat, name):
    s = z_conv.shape[0]

    def body(z_ref, w_ref, a_in_ref, o_ref):
        del a_in_ref
        gb, gc, xv = z_ref[:, 0:LANES], z_ref[:, LANES:2 * LANES], z_ref[:, 2 * LANES:3 * LANES]
        u = gc * xv
        prev, nxt = _shift_rows(u, s)
        y = w_ref[0:1, :] * prev + w_ref[1:2, :] * u + w_ref[2:3, :] * nxt
        o_ref[...] = (gb * y).astype(BF16)

    return pl.pallas_call(
        body, name=name, grid=(CONV_W // LANES,),
        out_shape=jax.ShapeDtypeStruct(a_cat.shape, a_cat.dtype),
        in_specs=[pl.BlockSpec((s, 3 * LANES), lambda j: (0, j)), pl.BlockSpec((3, LANES), lambda j: (0, j)),
                  pl.BlockSpec(memory_space=pl.ANY)],
        out_specs=pl.BlockSpec((s, LANES), lambda j: (0, 4 + j)),
        input_output_aliases={2: 0},
        compiler_params=_params(("parallel",), VMEM_BIG),
    )(z_conv, cw, a_cat)


def _conv_bwd(z_conv, cw, da, h_all, name):
    s = z_conv.shape[0]

    def body(z_ref, w_ref, da_ref, h_ref, dz_ref, dw_ref, g_ref):
        gb, gc, xv = z_ref[:, 0:LANES], z_ref[:, LANES:2 * LANES], z_ref[:, 2 * LANES:3 * LANES]
        u = gc * xv
        prev, nxt = _shift_rows(u, s)
        dcv = da_ref[...]
        dz_ref[:, 0:LANES] = (dcv * (w_ref[0:1, :] * prev + w_ref[1:2, :] * u + w_ref[2:3, :] * nxt)).astype(BF16)
        dy = dcv * gb
        dw_ref[0:1, :] = jnp.sum(dy * prev, axis=0, keepdims=True)
        dw_ref[1:2, :] = jnp.sum(dy * u, axis=0, keepdims=True)
        dw_ref[2:3, :] = jnp.sum(dy * nxt, axis=0, keepdims=True)
        dyp, dyn = _shift_rows(dy, s)
        du = w_ref[0:1, :] * dyn + w_ref[1:2, :] * dy + w_ref[2:3, :] * dyp
        dz_ref[:, LANES:2 * LANES] = (du * xv).astype(BF16)
        dz_ref[:, 2 * LANES:3 * LANES] = (du * gc).astype(BF16)
        g_ref[...] = lax.dot_general(dz_ref[...], h_ref[...], TN_DIMS, preferred_element_type=F32)

    d = h_all.shape[1]
    blk = pl.BlockSpec((s, 3 * LANES), lambda j: (0, j))
    cws = pl.BlockSpec((3, LANES), lambda j: (0, j))
    return pl.pallas_call(
        body, name=name, grid=(CONV_W // LANES,),
        out_shape=(jax.ShapeDtypeStruct(z_conv.shape, BF16), jax.ShapeDtypeStruct((3, CONV_W), F32),
                   jax.ShapeDtypeStruct((z_conv.shape[1], d), F32)),
        in_specs=[blk, cws, pl.BlockSpec((s, LANES), lambda j: (0, 4 + j)), pl.BlockSpec((s, d), lambda j: (0, 0))],
        out_specs=(blk, cws, pl.BlockSpec((3 * LANES, d), lambda j: (j, 0))),
        compiler_params=_params(("parallel",), VMEM_BIG),
    )(z_conv, cw, da, h_all)


ATT_TQ = 512
ATT_Q_STEP = 1024
ATT_TQ_BWD = 512


def _head_mask(shape, hh):
    lane = lax.broadcasted_iota(jnp.int32, shape, 1)
    return (lane >= hh * V_DIM) & (lane < (hh + 1) * V_DIM)


def _attn_fwd(qf, kv, s, riding, name):
    t = kv.shape[0]
    step = min(ATT_Q_STEP, s)
    nq = s // step
    nr = riding.n

    def body(*refs):
        q_ref, k_ref, v_ref = refs[:3]
        o_ref, ob_ref, st_ref = refs[3 + nr:6 + nr]
        p, i = pl.program_id(0), pl.program_id(1)
        state = riding.run((p == 0) & (i == 0), (p == N_HEADS // 2 - 1) & (i == nq - 1),
                           refs[3:3 + nr], refs[6 + nr:6 + 2 * nr], refs[6 + 2 * nr:],
                           middle=(p == N_HEADS // 2 - 2) & (i == nq // 2))
        v = v_ref[...]
        vlane = lax.broadcasted_iota(jnp.int32, v.shape, 1)
        one_lane = [(1 - hh) * V_DIM for hh in range(2)]
        vm = [jnp.where(_head_mask(v.shape, hh), v, jnp.where(vlane == one_lane[hh], 1.0, 0.0).astype(BF16))
              for hh in range(2)]

        def block(r, carry):
            rows = pl.ds(pl.multiple_of(r * ATT_TQ, ATT_TQ), ATT_TQ)
            olane = lax.broadcasted_iota(jnp.int32, (ATT_TQ, LANES), 1)
            acc = jnp.zeros((ATT_TQ, LANES), F32)
            stat = jnp.zeros((ATT_TQ, LANES), F32)
            scores = [lax.dot_general(q_ref[rows, hh * LANES:(hh + 1) * LANES], k_ref[:, hh * LANES:(hh + 1) * LANES],
                                      NT_DIMS, preferred_element_type=F32) for hh in range(2)]
            maxes = [jnp.max(sc, axis=1, keepdims=True) for sc in scores]
            exps = [jnp.exp2((sc - mx) * EXP2_SCALE).astype(BF16) for sc, mx in zip(scores, maxes)]
            for hh in range(2):
                mx = maxes[hh]
                res = jnp.dot(exps[hh], vm[hh], preferred_element_type=F32)
                den = jnp.sum(jnp.where(olane == one_lane[hh], res, 0.0), axis=1, keepdims=True)
                acc = acc + jnp.where(_head_mask(res.shape, hh), res * (1.0 / den), 0.0)
                stat = stat + jnp.where(olane == hh, mx * EXP2_SCALE + jnp.log(den) * LOG2_E, 0.0)
            o_ref[rows, :] = acc
            ob_ref[rows, :] = acc.astype(BF16)
            st_ref[:, rows] = stat.T[0:8, :]
            return carry

        lax.fori_loop(0, step // ATT_TQ, block, 0)
        riding.finish(state)

    o_spec = pl.BlockSpec((step, LANES), lambda p, i: (i, p))
    outs = pl.pallas_call(
        body, name=name, grid=(N_HEADS // 2, nq),
        out_shape=(jax.ShapeDtypeStruct((s, N_HEADS * V_DIM), F32),
                   jax.ShapeDtypeStruct((s, D_MODEL), BF16),
                   jax.ShapeDtypeStruct((N_HEADS // 2 * 8, s), F32), *riding.out_shape),
        in_specs=[pl.BlockSpec((step, 2 * LANES), lambda p, i: (i, p)),
                  pl.BlockSpec((t, 2 * LANES), lambda p, i: (0, p)),
                  pl.BlockSpec((t, LANES), lambda p, i: (0, N_HEADS + p)), *riding.specs],
        out_specs=(o_spec, o_spec, pl.BlockSpec((8, step), lambda p, i: (p, i)), *riding.specs),
        scratch_shapes=riding.scratch,
        compiler_params=_params(("arbitrary", "arbitrary"), VMEM_BIG),
    )(qf, kv, kv, *riding.arrays)
    return outs[0], outs[1], outs[2], list(outs[3:])


def _attn_bwd(qf, kv, o, da, stats, cos, sgn, riding, name):
    s, t = o.shape[0], kv.shape[0]
    ATT_TQ = ATT_TQ_BWD
    nq = s // ATT_TQ
    nr = riding.n

    def body(*refs):
        q_ref, k_ref, v_ref, o_ref, do_ref, st_ref, c_ref, s_ref = refs[:8]
        dq_ref, dk_ref, dv_ref = refs[8 + nr:11 + nr]
        dk_acc, dv_acc = refs[11 + 2 * nr:13 + 2 * nr]
        p, i = pl.program_id(0), pl.program_id(1)
        state = riding.run((p == 0) & (i == 0), (p == N_HEADS // 2 - 1) & (i == nq - 1),
                           refs[8:8 + nr], refs[11 + nr:11 + 2 * nr], refs[13 + 2 * nr:])

        @pl.when(i == 0)
        def _():
            dk_acc[...] = jnp.zeros_like(dk_acc)
            dv_acc[...] = jnp.zeros_like(dv_acc)

        v = v_ref[...]
        do = do_ref[...]
        od = do * o_ref[...]
        ones = jnp.ones((8, LANES), F32)
        for hh in range(2):
            sl = slice(hh * LANES, (hh + 1) * LANES)
            q, k = q_ref[:, sl], k_ref[:, sl]
            mask = _head_mask(do.shape, hh)
            dom = jnp.where(mask, do, 0.0).astype(BF16)
            delta = lax.dot_general(ones, jnp.where(mask, od, 0.0), NT_DIMS, preferred_element_type=F32,
                                    precision=lax.Precision.HIGHEST)[0:1, :]
            st = lax.dot_general(k, q, NT_DIMS, preferred_element_type=F32)
            pt = jnp.exp2(st * EXP2_SCALE - st_ref[hh:hh + 1, :]).astype(BF16)
            dpt = lax.dot_general(v, dom, NT_DIMS, preferred_element_type=F32)
            dst = (pt.astype(F32) * (dpt - delta)).astype(BF16)
            dv_acc[...] += jnp.dot(pt, dom, preferred_element_type=F32)
            dk_acc[:, sl] += jnp.dot(dst, q, preferred_element_type=F32)
            dq = lax.dot_general(dst, k, TN_DIMS, preferred_element_type=F32) * ATTN_SCALE
            dq_ref[:, sl] = _rope(dq, c_ref[...], s_ref[...], True).astype(BF16)

        @pl.when(i == nq - 1)
        def _():
            dk_ref[...] = (dk_acc[...] * ATTN_SCALE).astype(BF16)
            dv_ref[...] = dv_acc[...].astype(BF16)

        riding.finish(state)

    o_spec = pl.BlockSpec((ATT_TQ, LANES), lambda p, i: (i, p))
    tab = pl.BlockSpec((ATT_TQ, LANES), lambda p, i: (i, 0))
    outs = pl.pallas_call(
        body, name=name, grid=(N_HEADS // 2, nq),
        out_shape=(jax.ShapeDtypeStruct((s, N_HEADS * LANES), BF16),
                   jax.ShapeDtypeStruct((t, N_HEADS * LANES), BF16),
                   jax.ShapeDtypeStruct((t, N_HEADS * V_DIM), BF16), *riding.out_shape),
        in_specs=[pl.BlockSpec((ATT_TQ, 2 * LANES), lambda p, i: (i, p)),
                  pl.BlockSpec((t, 2 * LANES), lambda p, i: (0, p)),
                  pl.BlockSpec((t, LANES), lambda p, i: (0, N_HEADS + p)),
                  o_spec, o_spec,
                  pl.BlockSpec((8, ATT_TQ), lambda p, i: (p, i)), tab, tab, *riding.specs],
        out_specs=(pl.BlockSpec((ATT_TQ, 2 * LANES), lambda p, i: (i, p)),
                   pl.BlockSpec((t, 2 * LANES), lambda p, i: (0, p)),
                   pl.BlockSpec((t, LANES), lambda p, i: (0, p)), *riding.specs),
        scratch_shapes=[pltpu.VMEM((t, 2 * LANES), F32), pltpu.VMEM((t, LANES), F32), *riding.scratch],
        compiler_params=_params(("arbitrary", "arbitrary"), VMEM_BIG),
    )(qf, kv, kv, o, da, stats, cos, sgn, *riding.arrays)
    return outs[0], outs[1], outs[2], list(outs[3:])


def _silu(x):
    return x * (1.0 / (1.0 + jnp.exp(-x)))


def _prologue(c_rows, c_ctx, w_mod, b_cols, extra_rows, name):
    d, cols = c_rows.shape[1], w_mod.shape[1]

    def body(c_ref, cctx_ref, wmod_ref, b_ref, x_ref, a_ref, modg_ref, c_all, blk, c_send, c_recv, m_send, m_recv):
        _direct_gather(c_ref, c_all, c_send, c_recv)()
        a_ref[...] = jnp.zeros_like(a_ref)
        for j in range(N_DEV):
            a_ref[j:j + 1, :] = c_all[j, 0:1, :]
        a_ref[N_DEV:N_DEV + 1, :] = cctx_ref[...]
        mod = jnp.dot(_silu(a_ref[...]), wmod_ref[...], preferred_element_type=F32,
                      precision=lax.Precision.HIGHEST) + b_ref[...]
        blk[...] = jnp.zeros_like(blk)
        for p in range(N_DEV):
            blk[p, 0:1, :] = mod[p:p + 1, :]
            blk[p, 1:2, :] = mod[N_DEV:N_DEV + 1, :]
            blk[p, 2:5, :] = x_ref[...]
        _direct_gather(blk, modg_ref, m_send, m_recv, per_peer=True)()

    vmem = pl.BlockSpec(memory_space=pltpu.VMEM)
    return pl.pallas_call(
        body, name=name,
        out_shape=(jax.ShapeDtypeStruct((16, d), F32), jax.ShapeDtypeStruct((N_DEV, 8, cols), F32)),
        in_specs=[vmem] * 5, out_specs=(vmem, vmem),
        scratch_shapes=[pltpu.VMEM((N_DEV, 8, d), F32), pltpu.VMEM((N_DEV, 8, cols), F32)]
        + [pltpu.SemaphoreType.DMA((7,)) for _ in range(4)],
        compiler_params=_params(None, VMEM_BIG),
    )(c_rows, c_ctx, w_mod, b_cols, extra_rows)


def _adaln_bwd(a_t, w, d_ex, d_ctx, d_all, name):
    def body(at_ref, w_ref, dex_ref, dctx_ref, dall_ref, gw_ref, dsil_ref, dsum_ref):
        sil_t = _silu(at_ref[...])
        dctx = dctx_ref[...]
        row = dctx[0:1, :]
        for j in range(1, N_DEV):
            row = row + dctx[j:j + 1, :]
        rowi = lax.broadcasted_iota(jnp.int32, dctx.shape, 0)
        ctx_rows = jnp.where(rowi == 0, jnp.broadcast_to(row, dctx.shape), 0.0)
        hi = lax.Precision.HIGHEST
        d_rows = jnp.concatenate([dex_ref[...], ctx_rows], axis=0)
        gw_ref[...] = jnp.dot(sil_t, d_rows, preferred_element_type=F32, precision=hi)
        dsil_ref[...] = lax.dot_general(ctx_rows, w_ref[...], NT_DIMS, preferred_element_type=F32, precision=hi)
        tot = dall_ref[0]
        for j in range(1, N_DEV):
            tot = tot + dall_ref[j]
        dsum_ref[...] = tot

    return pl.pallas_call(
        body, name=name,
        out_shape=(jax.ShapeDtypeStruct(w.shape, F32), jax.ShapeDtypeStruct((8, w.shape[0]), F32),
                   jax.ShapeDtypeStruct(d_all.shape[1:], F32)),
        compiler_params=_params(None, VMEM_BIG),
    )(a_t, w, d_ex, d_ctx, d_all)


SMALL_ROWS = 24
SMALL_MISC, SMALL_CW, SMALL_LOSS = 16, 18, 21


def _pack_small(sums1, sums2, fsums, sums1c, psums, psums_c, d_cw, cols, name):
    d = D_MODEL

    def body(s1_ref, s2_ref, f_ref, s1c_ref, p_ref, pc_ref, cw_ref, o_ref):
        o_ref[...] = jnp.zeros_like(o_ref)

        def blocks(row0, pieces):
            for j in range(N_DEV):
                lo, hi = j * cols, (j + 1) * cols
                for k, (ref, r) in enumerate(pieces):
                    a, b = max(lo, k * d), min(hi, (k + 1) * d)
                    if a < b:
                        o_ref[row0 + j:row0 + j + 1, a - lo:b - lo] = ref[r:r + 1, a - k * d:b - k * d]

        blocks(0, [(s1_ref, 1), (s1_ref, 0), (s2_ref, 2), (s2_ref, 1), (s2_ref, 0), (f_ref, 1)])
        blocks(N_DEV, [(s1c_ref, 1), (s1c_ref, 0)])
        head = Q_RANK + KV_RANK
        o_ref[SMALL_MISC:SMALL_MISC + 1, 0:Q_RANK] = p_ref[0:1, :]
        o_ref[SMALL_MISC:SMALL_MISC + 1, Q_RANK:head] = p_ref[1:2, 0:KV_RANK] + pc_ref[1:2, 0:KV_RANK]
        o_ref[SMALL_MISC:SMALL_MISC + 1, head:cols] = f_ref[0:1, 0:cols - head]
        o_ref[SMALL_MISC + 1:SMALL_MISC + 2, 0:d - (cols - head)] = f_ref[0:1, cols - head:d]
        for r in range(3):
            o_ref[SMALL_CW + r:SMALL_CW + r + 1, 0:CONV_W] = cw_ref[r:r + 1, :]
        o_ref[SMALL_LOSS:SMALL_LOSS + 1, :] = f_ref[3:4, 0:cols]

    return pl.pallas_call(body, name=name, out_shape=jax.ShapeDtypeStruct((SMALL_ROWS, cols), F32))(
        sums1, sums2, fsums, sums1c, psums, psums_c, d_cw)


def _adam_math(w, g, m, v):
    nm = ADAM_B1 * m + (1.0 - ADAM_B1) * g
    nv = ADAM_B2 * v + (1.0 - ADAM_B2) * (g * g)
    m_hat = nm / (1.0 - ADAM_B1 ** ADAM_STEP)
    v_hat = nv / (1.0 - ADAM_B2 ** ADAM_STEP)
    return -ADAM_LR * (m_hat / (jnp.sqrt(v_hat) + ADAM_EPS) + ADAM_WD * w), nm, nv


def _small_update(dsum, dsil_all, g_cw, params, name):
    d = D_MODEL
    n = len(params)
    cols = dsum.shape[1]

    def body(*refs):
        dsum_ref, dsil_ref, gcw_ref = refs[:3]
        wmv = refs[3:3 + 3 * n]
        outs = refs[3 + 3 * n:]
        tot = dsil_ref[0]
        for j in range(1, N_DEV):
            tot = tot + dsil_ref[j]
        cv = wmv[0][...]
        sg = 1.0 / (1.0 + jnp.exp(-cv))
        off = Q_RANK + KV_RANK
        misc = dsum_ref[SMALL_MISC:SMALL_MISC + 1, :]
        grads = [tot[0:1, :] * (sg * (1.0 + cv * (1.0 - sg))),
                 jnp.concatenate([dsum_ref[j:j + 1, :] + dsum_ref[N_DEV + j:N_DEV + j + 1, :] for j in range(N_DEV)],
                                 axis=1),
                 misc[:, 0:Q_RANK], misc[:, Q_RANK:off],
                 jnp.concatenate([misc[:, off:cols], dsum_ref[SMALL_MISC + 1:SMALL_MISC + 2, 0:d - (cols - off)]],
                                 axis=1),
                 gcw_ref[...]]
        for p, g in enumerate(grads):
            w_ref, m_ref, v_ref = wmv[3 * p:3 * p + 3]
            at = 0 if len(w_ref.shape) == 3 else Ellipsis
            res = (g,) + _adam_math(w_ref[at], g, m_ref[at], v_ref[at])
            for q, val in enumerate(res):
                outs[4 * p + q][at] = val

    flat = [a for wmv in params for a in wmv]
    out_shape = tuple(jax.ShapeDtypeStruct(wmv[0].shape, F32) for wmv in params for _ in range(4))
    outs = pl.pallas_call(body, name=name, out_shape=out_shape)(dsum, dsil_all, g_cw, *flat)
    return [outs[4 * p:4 * p + 4] for p in range(n)]


def _adamw(w, g, m, v, name, slots=False):
    _, rows, cols = w.shape
    tr = _pick(rows, (256, 128, 64, 32, 16, 8))

    def body(w_ref, g_ref, m_ref, v_ref, *outs):
        if slots:
            gv = g_ref[0].astype(F32)
            for j in range(1, g.shape[0]):
                gv = gv + g_ref[j].astype(F32)
            outs[0][...] = gv
        else:
            gv = g_ref[...]
        d_ref, nm_ref, nv_ref = outs[-3:]
        d_ref[...], nm_ref[...], nv_ref[...] = _adam_math(w_ref[...], gv, m_ref[...], v_ref[...])

    blk = pl.BlockSpec((None, tr, cols), lambda i: (0, i, 0))
    g_spec = (pl.BlockSpec((g.shape[0], tr, cols), lambda i: (0, i, 0)) if slots
              else pl.BlockSpec((tr, cols), lambda i: (i, 0)))
    sh = jax.ShapeDtypeStruct((1, rows, cols), F32)
    n_out = 4 if slots else 3
    return pl.pallas_call(
        body, name=name, grid=(rows // tr,), out_shape=(sh,) * n_out,
        in_specs=[blk, g_spec, blk, blk], out_specs=(blk,) * n_out,
        compiler_params=_params(("parallel",), VMEM_BIG),
    )(w, g, m, v)


def _rope_tables(s, l):
    tok = np.arange(s)
    row = (tok // GRID_W).astype(np.float32)
    col = (tok % GRID_W).astype(np.float32)
    half = QK_ROPE // 2
    freqs = np.float32(ROPE_THETA) ** (-np.arange(0, half, 2, dtype=np.float32) / np.float32(half))
    dd = np.arange(QK_ROPE)
    pos = np.where((dd // half)[None, :] == 0, row[:, None], col[:, None]).astype(np.float32)
    ang = (pos * freqs[dd % (half // 2)][None, :]).astype(np.float32)
    sin = np.sin(ang).astype(np.float32)
    cos_t = np.ones((s + l, LANES), np.float32)
    sgn_t = np.zeros((s + l, LANES), np.float32)
    cos_t[:s, QK_NOPE:QK_NOPE + QK_ROPE] = np.cos(ang)
    sgn_t[:s, QK_NOPE:QK_NOPE + QK_ROPE] = np.where(((dd % half) // (half // 2))[None, :] == 0, -sin, sin)
    return jnp.asarray(cos_t), jnp.asarray(sgn_t)


def _slots_to_cols(g):
    return g.transpose(1, 0, 2).reshape(g.shape[1], N_DEV * g.shape[2])


def _cols_to_slots(w):
    return w.reshape(w.shape[0], N_DEV, w.shape[1] // N_DEV).transpose(1, 0, 2)


def _unpack_small_weights(g_in_t, g_uq, g_ukv):
    w_t = g_in_t.reshape(N_DEV * g_in_t.shape[1], D_MODEL)
    zeros = jnp.zeros((QK_NOPE, D_MODEL), BF16)
    win_head_t = jnp.concatenate([w_t[:Q_RANK + KV_RANK], zeros, w_t[Q_RANK + KV_RANK:MLA_IN],
                                  zeros[:LANES - QK_NOPE - QK_ROPE]], axis=0)
    win_conv_t = w_t[MLA_IN:].reshape(3, CONV_W // LANES, LANES, D_MODEL).transpose(1, 0, 2, 3)
    win_conv_t = win_conv_t.reshape(3 * CONV_W, D_MODEL)
    w_uq = _slots_to_cols(g_uq).reshape(Q_RANK, N_HEADS, QK_NOPE + QK_ROPE)
    wq = jnp.pad(w_uq, ((0, 0), (0, 0), (0, LANES - QK_NOPE - QK_ROPE))).reshape(Q_RANK, N_HEADS * LANES)
    w_ukv = _slots_to_cols(g_ukv).reshape(KV_RANK, N_HEADS, QK_NOPE + V_DIM)
    k_top = jnp.pad(w_ukv[:, :, :QK_NOPE], ((0, 0), (0, 0), (0, LANES - QK_NOPE))).reshape(KV_RANK, N_HEADS * LANES)
    v_top = w_ukv[:, :, QK_NOPE:].reshape(KV_RANK, N_HEADS * V_DIM)
    eye = jnp.pad(jnp.eye(QK_ROPE, dtype=BF16), ((QK_NOPE, LANES - QK_NOPE - QK_ROPE),) * 2)
    wk = jnp.concatenate([
        jnp.concatenate([k_top, v_top], axis=1),
        jnp.concatenate([jnp.tile(eye, (1, N_HEADS)), jnp.zeros((LANES, N_HEADS * V_DIM), BF16)], axis=1)], axis=0)
    return win_head_t, win_conv_t, wq, wk


def _pack_small_grads(d_head_t, d_conv_t, d_wq, d_wkk, d_wkv):
    d_conv_t = d_conv_t.reshape(CONV_W // LANES, 3, LANES, D_MODEL).transpose(1, 0, 2, 3).reshape(3 * CONV_W, D_MODEL)
    rope0 = Q_RANK + KV_RANK + QK_NOPE
    g_in_t = jnp.concatenate([d_head_t[:Q_RANK + KV_RANK], d_head_t[rope0:rope0 + QK_ROPE], d_conv_t], axis=0)
    g_in_t = g_in_t.reshape(N_DEV, -1, D_MODEL).astype(BF16)
    g_uq = d_wq.reshape(Q_RANK, N_HEADS, LANES)[:, :, :QK_NOPE + QK_ROPE].reshape(Q_RANK, -1)
    g_kn = d_wkk[:KV_RANK].reshape(KV_RANK, N_HEADS, LANES)[:, :, :QK_NOPE]
    g_v = d_wkv[:KV_RANK].reshape(KV_RANK, N_HEADS, V_DIM)
    g_ukv = jnp.concatenate([g_kn, g_v], axis=2).reshape(KV_RANK, -1)
    return [g_in_t] + [_cols_to_slots(g).astype(BF16) for g in (g_uq, g_ukv)]


def kernel(x, c, ctx, c_ctx, w_mod, b_mod, w_in, q_norm_g, w_uq, kv_norm_g, w_ukv, conv_w, w_out, w_mlp1, w_mlp2, final_norm_g, loss_target, m_c_ctx, m_w_mod, m_b_mod, m_w_in, m_q_norm_g, m_w_uq, m_kv_norm_g, m_w_ukv, m_conv_w, m_w_out, m_w_mlp1, m_w_mlp2, m_final_norm_g, v_c_ctx, v_w_mod, v_b_mod, v_w_in, v_q_norm_g, v_w_uq, v_kv_norm_g, v_w_ukv, v_conv_w, v_w_out, v_w_mlp1, v_w_mlp2, v_final_norm_g):
    me = _my_index()
    x2d, ctx2d, tgt = x[0], ctx[0], loss_target[0]
    s, l = x2d.shape[0], ctx2d.shape[0]
    t = s + l
    d = D_MODEL
    mod_cols = w_mod.shape[2]
    cw_cols = conv_w.shape[2]

    b_cols = lax.dynamic_slice(b_mod, (0, me * mod_cols), (1, mod_cols))
    cw_blk = jnp.pad(conv_w[0], ((0, 0), (0, mod_cols - cw_cols)))
    a_rows, gathered = _prologue(jnp.pad(c, ((0, 7), (0, 0))), c_ctx[None, :], w_mod[0], b_cols, cw_blk,
                                 "prologue")
    mod_mine = gathered[:, 0, :].reshape(1, 6 * d)
    mod_ctx = gathered[:, 1, :].reshape(1, 6 * d)
    cw_full = gathered[:, 2:5, :cw_cols].transpose(1, 0, 2).reshape(3, CONV_W)

    early = [w.astype(BF16) for w in (w_in[0].T, w_uq[0], w_ukv[0])]
    late = [w.astype(BF16) for w in (w_out[0], w_mlp1[0], w_mlp2[0])]
    h_all, (g_in, g_uq, g_ukv) = _modulate_all(x2d, ctx2d, mod_mine, mod_ctx, _RidingGather(early),
                                               "modulate1")
    win_head, win_conv, wq, wk = _unpack_small_weights(g_in, g_uq, g_ukv)
    wk_k, wk_v = wk[:, :N_HEADS * LANES], wk[:, N_HEADS * LANES:]
    cos, sgn = _rope_tables(s, l)

    tm_t = _pick(t, (1088, 768, 256))
    z_head, cq, kv_in, qf, kv = _head_fwd(h_all, win_head, wq, wk, q_norm_g, kv_norm_g, cos, sgn, tm_t, "head_fwd")
    z_conv = _matmul(h_all, win_conv, mode="nt", name="in_proj_conv", m=s, tm=1024, tn=1536, tk=1024)
    attn, a_cat, stats, (g_out, w1, g_w2) = _attn_fwd(qf, kv, s, _RidingGather(late), "attn_fwd")
    wo = g_out.reshape(d, d)
    w2 = g_w2.reshape(D_FF, d)
    a_cat = _conv_fwd(z_conv, cw_full, a_cat, "conv_fwd")
    (o, x1, h2), _ = _matmul_rows(a_cat, wo, _epi_resid_modulate, mode="nn", name="out_proj", tm=1024, tk=1024,
                                  rows=[x2d], vecs=[(mod_mine, 2), (mod_mine, 3), (mod_mine, 4)],
                                  out_dtypes=[F32, F32, BF16])
    u1, act = _matmul(h2, w1, mode="nn", name="mlp_up", tm=4096, tk=1024, epilogue="relu2", slots="b_cols")
    (dx2, dm, fsums), _ = _matmul_rows(act, w2, _epi_final, mode="nn", name="mlp_down", tm=512, tk=4096,
                                       rows=[x1, tgt], vecs=[(mod_mine, 5), (final_norm_g[None, :], 0)],
                                       out_dtypes=[F32, BF16], sums=True)

    d_w2 = _matmul(act, dm, mode="tn", name="d_w_mlp2", out_dtype=BF16, tm=1024, tn=1024, tk=4096)
    du1 = _matmul(dm, w2, mode="nt", name="d_act", out_dtype=BF16, tm=2048, tn=1024, tk=1024,
                  epilogue="drelu2", extra=(u1,))
    d_w1 = _matmul(h2, du1, mode="tn", name="d_w_mlp1", out_dtype=BF16, tm=1024, tk=4096, slots="out")
    (dx1, do, sums2), _ = _matmul_rows(du1, w1, _epi_modulate2_bwd, mode="nt", name="d_h2", tm=512, tk=4096,
                                       slots="b_contract", rows=[x1, dx2, o], vecs=[(mod_mine, 4), (mod_mine, 2)],
                                       out_dtypes=[F32, BF16], sums=True)
    da, d_wo = _out_proj_bwd(a_cat, do, wo, "out_proj_bwd")
    dz_conv, d_cw, d_conv = _conv_bwd(z_conv, cw_full, da, h_all, "conv_bwd")
    ready = [d_wo.reshape(N_DEV, d // N_DEV, d), d_w1, d_w2.reshape(N_DEV, D_FF // N_DEV, d)]
    dq, dk, dv, rode = _attn_bwd(qf, kv, attn, da, stats, cos, sgn, _Riding(ready), "attn_bwd")
    head_args = (z_head, wq, wk_k, wk_v, win_head, q_norm_g, kv_norm_g, cos, sgn, cq, kv_in, h_all)
    dz_head, dh_head, psums, d_wq, *carried = _head_bwd(dq, dk, dv, *head_args, "head_bwd", tile=HEAD_BWD_TILE,
                                                        first_block=0, n_blocks=s // HEAD_BWD_TILE)
    _, dh_head, psums_c, d_wkk, d_wkv, d_head = _head_bwd(
        None, dk, dv, *head_args, "head_bwd_ctx", tile=ROW_TILE, first_block=s // ROW_TILE, n_blocks=l // ROW_TILE,
        carry=(dz_head, dh_head, *carried))
    send = _pack_small_grads(d_head, d_conv, d_wq, d_wkk, d_wkv)
    (grad_x, sums1), got = _matmul_rows(dz_conv, win_conv, _epi_modulate1_bwd, mode="nn", name="d_h1", tm=max(s // 8, ROW_TILE),
                                        tk=win_conv.shape[0], rows=[dh_head, x2d, dx1], vecs=[(mod_mine, 1)],
                                        out_dtypes=[F32], sums=True, riding=_RidingReduce(send))
    sums1c = _modulate_sums(dh_head, s // ROW_TILE, ctx2d)

    small = _pack_small(sums1, sums2, fsums, sums1c, psums, psums_c, d_cw, mod_cols, "pack_small")
    (d_all,) = _all_gather([small], "gather_small_grads", True)
    d_ex = lax.dynamic_index_in_dim(d_all, me, axis=1, keepdims=False)
    d_ctx = lax.dynamic_index_in_dim(d_all, N_DEV + me, axis=1, keepdims=False)
    g_w_mod, dsil, dsum = _adaln_bwd(a_rows.T, w_mod[0], d_ex, d_ctx, d_all, "adaln_bwd")
    (dsil_all,) = _all_gather([dsil], "gather_d_cctx", True)
    loss = dsum[SMALL_LOSS, 0]
    g_cw = lax.dynamic_slice(dsum, (SMALL_CW, me * cw_cols), (3, cw_cols))

    slots = dict(zip(["w_in", "w_uq", "w_ukv"], got))
    slots.update(zip(["w_out", "w_mlp1", "w_mlp2"], rode))

    grads = {}
    weights = {"c_ctx": c_ctx, "w_mod": w_mod, "b_mod": b_mod, "w_in": w_in, "q_norm_g": q_norm_g, "w_uq": w_uq,
               "kv_norm_g": kv_norm_g, "w_ukv": w_ukv, "conv_w": conv_w, "w_out": w_out, "w_mlp1": w_mlp1,
               "w_mlp2": w_mlp2, "final_norm_g": final_norm_g}
    m_in = {"c_ctx": m_c_ctx, "w_mod": m_w_mod, "b_mod": m_b_mod, "w_in": m_w_in, "q_norm_g": m_q_norm_g,
            "w_uq": m_w_uq, "kv_norm_g": m_kv_norm_g, "w_ukv": m_w_ukv, "conv_w": m_conv_w, "w_out": m_w_out,
            "w_mlp1": m_w_mlp1, "w_mlp2": m_w_mlp2, "final_norm_g": m_final_norm_g}
    v_in = {"c_ctx": v_c_ctx, "w_mod": v_w_mod, "b_mod": v_b_mod, "w_in": v_w_in, "q_norm_g": v_q_norm_g,
            "w_uq": v_w_uq, "kv_norm_g": v_kv_norm_g, "w_ukv": v_w_ukv, "conv_w": v_conv_w, "w_out": v_w_out,
            "w_mlp1": v_w_mlp1, "w_mlp2": v_w_mlp2, "final_norm_g": v_final_norm_g}
    names = list(weights)
    small_names = ["c_ctx", "b_mod", "q_norm_g", "kv_norm_g", "final_norm_g", "conv_w"]
    delta, new_m, new_v = {}, {}, {}

    def as_rows(a):
        return a[None, :] if a.ndim == 1 else a

    small_out = _small_update(dsum, dsil_all, g_cw, [[as_rows(src[n]) for src in (weights, m_in, v_in)]
                                                      for n in small_names], "small_update")
    for n, outs in zip(small_names, small_out):
        grads[n], delta[n], new_m[n], new_v[n] = [a.reshape(weights[n].shape) for a in outs]
    for n in names:
        if n in small_names:
            continue
        if n == "w_in":
            wmv = [jnp.swapaxes(src[n], 1, 2) for src in (weights, m_in, v_in)]
            outs = _adamw(wmv[0], slots[n], wmv[1], wmv[2], "adamw_" + n, slots=True)
            grads[n], delta[n], new_m[n], new_v[n] = [jnp.swapaxes(a, 1, 2) for a in outs]
        elif n in slots:
            grads[n], delta[n], new_m[n], new_v[n] = _adamw(weights[n], slots[n], m_in[n], v_in[n], "adamw_" + n,
                                                            slots=True)
        else:
            delta[n], new_m[n], new_v[n] = _adamw(weights[n], g_w_mod, m_in[n], v_in[n], "adamw_" + n)
            grads[n] = g_w_mod[None]

    return (loss, grad_x[None], *[grads[n] for n in names], *[delta[n] for n in names],
            *[new_m[n] for n in names], *[new_v[n] for n in names])
```
